```python
import math
import jax
import jax.numpy as jnp
from jax import lax
import numpy as np

D_MODEL = 1024
BATCH = 16
SEQ = 2048
DEPTH = 1

PLE_DIM = 256
D_FF = 2816
SSM_GROUP_CH = 16
SSM_GROUPS = 32
D_SSM = SSM_GROUPS * SSM_GROUP_CH
SSM_STATE = 64
GMLP_HEADS = 8
GMLP_HEAD_DIM = 64
D_GMLP = GMLP_HEADS * GMLP_HEAD_DIM
CHUNK = 128
D_IN = D_SSM + 2 * D_GMLP + 2 * D_MODEL
LN_EPS = 1e-5
DEEPNORM_ALPHA = (2.0 * DEPTH) ** 0.25
DEEPNORM_BETA = (8.0 * DEPTH) ** -0.25

kernel_name = "hybrid_s5_gmlp_macaron_deepnorm"


def _layer_norm(x, g, b):
    xf = x.astype(jnp.float32)
    mu = jnp.mean(xf, axis=-1, keepdims=True)
    var = jnp.mean(jnp.square(xf - mu), axis=-1, keepdims=True)
    y = (xf - mu) * lax.rsqrt(var + LN_EPS) * g.astype(jnp.float32) + b.astype(jnp.float32)
    return y.astype(x.dtype)


def _swiglu(x, w_in, w_out):
    h = x @ w_in
    gate, up = jnp.split(h, 2, axis=-1)
    return (jax.nn.silu(gate) * up) @ w_out


def _complex_affine_combine(e1, e2):
    a1r, a1i, b1r, b1i = e1
    a2r, a2i, b2r, b2i = e2
    ar = a2r * a1r - a2i * a1i
    ai = a2r * a1i + a2i * a1r
    br = a2r * b1r - a2i * b1i + b2r
    bi = a2r * b1i + a2i * b1r + b2i
    return (ar, ai, br, bi)


def _s5_branch(u, lam_re, lam_im, log_dt, b_re, b_im, c_re, c_im, d_skip, glu_w, glu_b):
    bsz, seq, _ = u.shape
    f32 = jnp.float32
    ug = u.reshape(bsz, seq, SSM_GROUPS, SSM_GROUP_CH).astype(f32)
    dt = jnp.exp(log_dt.astype(f32))[:, None]
    lre = lam_re.astype(f32)
    lim = lam_im.astype(f32)
    mag = jnp.exp(lre * dt)
    ab_re = mag * jnp.cos(lim * dt)
    ab_im = mag * jnp.sin(lim * dt)
    nr = ab_re - 1.0
    ni = ab_im
    den = lre * lre + lim * lim
    coef_re = ((nr * lre + ni * lim) / den)[..., None]
    coef_im = ((ni * lre - nr * lim) / den)[..., None]
    bre = b_re.astype(f32)
    bim = b_im.astype(f32)
    bb_re = coef_re * bre - coef_im * bim
    bb_im = coef_re * bim + coef_im * bre
    bu_re = jnp.einsum('blgi,gpi->blgp', ug, bb_re)
    bu_im = jnp.einsum('blgi,gpi->blgp', ug, bb_im)
    a_re = jnp.broadcast_to(ab_re, bu_re.shape)
    a_im = jnp.broadcast_to(ab_im, bu_im.shape)
    _, _, h_re, h_im = lax.associative_scan(
        _complex_affine_combine, (a_re, a_im, bu_re, bu_im), axis=1)
    y = (jnp.einsum('gip,blgp->blgi', c_re.astype(f32), h_re)
         - jnp.einsum('gip,blgp->blgi', c_im.astype(f32), h_im))
    y = y.reshape(bsz, seq, D_SSM).astype(u.dtype) + d_skip * u
    y = jax.nn.gelu(y)
    return y * jax.nn.sigmoid(y @ glu_w + glu_b)


def _gmlp_branch(z_u, z_v, ln_g, ln_b, w_s, b_s):
    bsz, seq, _ = z_v.shape
    n_chunks = seq // CHUNK
    u = jax.nn.gelu(z_u)
    v = _layer_norm(jax.nn.gelu(z_v), ln_g, ln_b)
    vh = v.reshape(bsz, n_chunks, CHUNK, GMLP_HEADS, GMLP_HEAD_DIM)
    causal = jnp.tril(jnp.ones((CHUNK, CHUNK), dtype=bool))
    ws = jnp.where(causal[None], w_s, 0.0)
    s = jnp.einsum('hts,bcshd->bcthd', ws, vh) + b_s.T[:, :, None]
    return u * s.reshape(bsz, seq, D_GMLP)


def _fwd_setup_inputs(seed: int = 0) -> dict:
    key = jax.random.key(seed)
    ks = jax.random.split(key, 40)
    f32 = jnp.float32

    def nrm(k, shape, scale):
        return jax.random.normal(k, shape, f32) * scale

    def gain(k, shape):
        return 1.0 + 0.02 * jax.random.normal(k, shape, f32)

    def bias(k, shape):
        return 0.02 * jax.random.normal(k, shape, f32)

    L = DEPTH
    x = jax.random.normal(ks[0], (BATCH, SEQ, D_MODEL), f32)
    p = jax.random.normal(ks[1], (DEPTH, BATCH, SEQ, PLE_DIM), f32)

    ffn1_w_in = nrm(ks[2], (L, D_MODEL, 2 * D_FF), D_MODEL ** -0.5)
    ffn1_w_out = nrm(ks[3], (L, D_FF, D_MODEL), D_FF ** -0.5 * DEEPNORM_BETA)
    ln1_g = gain(ks[4], (L, D_MODEL))
    ln1_b = bias(ks[5], (L, D_MODEL))

    mix_w_in = nrm(ks[6], (L, D_MODEL, D_IN), D_MODEL ** -0.5)
    n_idx = jnp.arange(SSM_STATE, dtype=f32)
    ssm_lambda_re = -0.5 + 0.01 * jax.random.normal(ks[7], (L, SSM_GROUPS, SSM_STATE), f32)
    ssm_lambda_im = (math.pi * n_idx)[None, None, :] + 0.01 * jax.random.normal(
        ks[8], (L, SSM_GROUPS, SSM_STATE), f32)
    ssm_log_dt = jax.random.uniform(ks[9], (L, SSM_GROUPS), f32,
                                    minval=math.log(1e-3), maxval=math.log(1e-1))
    b_scale = (2.0 * SSM_GROUP_CH) ** -0.5
    c_scale = (2.0 * SSM_STATE) ** -0.5
    ssm_b_re = nrm(ks[10], (L, SSM_GROUPS, SSM_STATE, SSM_GROUP_CH), b_scale)
    ssm_b_im = nrm(ks[11], (L, SSM_GROUPS, SSM_STATE, SSM_GROUP_CH), b_scale)
    ssm_c_re = nrm(ks[12], (L, SSM_GROUPS, SSM_GROUP_CH, SSM_STATE), c_scale)
    ssm_c_im = nrm(ks[13], (L, SSM_GROUPS, SSM_GROUP_CH, SSM_STATE), c_scale)
    ssm_d = nrm(ks[14], (L, D_SSM), 1.0)
    ssm_glu_w = nrm(ks[15], (L, D_SSM, D_SSM), D_SSM ** -0.5)
    ssm_glu_b = bias(ks[16], (L, D_SSM))

    gmlp_ln_g = gain(ks[17], (L, D_GMLP))
    gmlp_ln_b = bias(ks[18], (L, D_GMLP))
    gmlp_w_s = nrm(ks[19], (L, GMLP_HEADS, CHUNK, CHUNK), CHUNK ** -0.5)
    gmlp_b_s = 1.0 + 0.02 * jax.random.normal(ks[20], (L, GMLP_HEADS, CHUNK), f32)

    up_a = nrm(ks[21], (L, D_SSM, D_MODEL), D_SSM ** -0.5)
    up_b = nrm(ks[22], (L, D_GMLP, D_MODEL), D_GMLP ** -0.5)
    mix_w_out = nrm(ks[23], (L, D_MODEL, D_MODEL), D_MODEL ** -0.5 * DEEPNORM_BETA)
    ln2_g = gain(ks[24], (L, D_MODEL))
    ln2_b = bias(ks[25], (L, D_MODEL))

    ffn2_w_in = nrm(ks[26], (L, D_MODEL, 2 * D_FF), D_MODEL ** -0.5)
    ffn2_w_out = nrm(ks[27], (L, D_FF, D_MODEL), D_FF ** -0.5 * DEEPNORM_BETA)
    ln3_g = gain(ks[28], (L, D_MODEL))
    ln3_b = bias(ks[29], (L, D_MODEL))

    ple_w_proj = nrm(ks[30], (L, PLE_DIM, D_MODEL), PLE_DIM ** -0.5 * DEEPNORM_BETA)
    ple_w_gate = nrm(ks[31], (L, D_MODEL, D_MODEL), D_MODEL ** -0.5)

    return {
        "x": x, "p": p,
        "ffn1_w_in": ffn1_w_in, "ffn1_w_out": ffn1_w_out, "ln1_g": ln1_g, "ln1_b": ln1_b,
        "mix_w_in": mix_w_in,
        "ssm_lambda_re": ssm_lambda_re, "ssm_lambda_im": ssm_lambda_im, "ssm_log_dt": ssm_log_dt,
        "ssm_b_re": ssm_b_re, "ssm_b_im": ssm_b_im, "ssm_c_re": ssm_c_re, "ssm_c_im": ssm_c_im,
        "ssm_d": ssm_d, "ssm_glu_w": ssm_glu_w, "ssm_glu_b": ssm_glu_b,
        "gmlp_ln_g": gmlp_ln_g, "gmlp_ln_b": gmlp_ln_b, "gmlp_w_s": gmlp_w_s, "gmlp_b_s": gmlp_b_s,
        "up_a": up_a, "up_b": up_b, "mix_w_out": mix_w_out, "ln2_g": ln2_g, "ln2_b": ln2_b,
        "ffn2_w_in": ffn2_w_in, "ffn2_w_out": ffn2_w_out, "ln3_g": ln3_g, "ln3_b": ln3_b,
        "ple_w_proj": ple_w_proj, "ple_w_gate": ple_w_gate,
    }


def _fwd_reference(x, p, ffn1_w_in, ffn1_w_out, ln1_g, ln1_b, mix_w_in,
              ssm_lambda_re, ssm_lambda_im, ssm_log_dt, ssm_b_re, ssm_b_im, ssm_c_re, ssm_c_im,
              ssm_d, ssm_glu_w, ssm_glu_b, gmlp_ln_g, gmlp_ln_b, gmlp_w_s, gmlp_b_s,
              up_a, up_b, mix_w_out, ln2_g, ln2_b, ffn2_w_in, ffn2_w_out, ln3_g, ln3_b,
              ple_w_proj, ple_w_gate):
    splits = [D_SSM, D_SSM + D_GMLP, D_SSM + 2 * D_GMLP, D_SSM + 2 * D_GMLP + D_MODEL]
    for i in range(DEPTH):
        x = _layer_norm(DEEPNORM_ALPHA * x + 0.5 * _swiglu(x, ffn1_w_in[i], ffn1_w_out[i]),
                        ln1_g[i], ln1_b[i])
        proj = x @ mix_w_in[i]
        z_a, z_u, z_v, g_a, g_b = jnp.split(proj, splits, axis=-1)
        y_a = _s5_branch(z_a, ssm_lambda_re[i], ssm_lambda_im[i], ssm_log_dt[i],
                         ssm_b_re[i], ssm_b_im[i], ssm_c_re[i], ssm_c_im[i],
                         ssm_d[i], ssm_glu_w[i], ssm_glu_b[i]) @ up_a[i]
        y_b = _gmlp_branch(z_u, z_v, gmlp_ln_g[i], gmlp_ln_b[i],
                           gmlp_w_s[i], gmlp_b_s[i]) @ up_b[i]
        mixed = (jax.nn.sigmoid(g_a) * y_a + jax.nn.sigmoid(g_b) * y_b) @ mix_w_out[i]
        x = _layer_norm(DEEPNORM_ALPHA * x + mixed, ln2_g[i], ln2_b[i])
        x = _layer_norm(DEEPNORM_ALPHA * x + 0.5 * _swiglu(x, ffn2_w_in[i], ffn2_w_out[i]),
                        ln3_g[i], ln3_b[i])
        x = x + jax.nn.sigmoid(x @ ple_w_gate[i]) * (p[i] @ ple_w_proj[i])
    return x


import jax as _jax
import jax.numpy as _jnp

TWIN_FORMAT = 'train_step'
FWD_PARAMS = ['x', 'p', 'ffn1_w_in', 'ffn1_w_out', 'ln1_g', 'ln1_b', 'mix_w_in', 'ssm_lambda_re', 'ssm_lambda_im', 'ssm_log_dt', 'ssm_b_re', 'ssm_b_im', 'ssm_c_re', 'ssm_c_im', 'ssm_d', 'ssm_glu_w', 'ssm_glu_b', 'gmlp_ln_g', 'gmlp_ln_b', 'gmlp_w_s', 'gmlp_b_s', 'up_a', 'up_b', 'mix_w_out', 'ln2_g', 'ln2_b', 'ffn2_w_in', 'ffn2_w_out', 'ln3_g', 'ln3_b', 'ple_w_proj', 'ple_w_gate']
TWIN_WEIGHTS = ['ffn1_w_in', 'ffn1_w_out', 'ln1_g', 'ln1_b', 'mix_w_in', 'ssm_lambda_re', 'ssm_lambda_im', 'ssm_log_dt', 'ssm_b_re', 'ssm_b_im', 'ssm_c_re', 'ssm_c_im', 'ssm_d', 'ssm_glu_w', 'ssm_glu_b', 'gmlp_ln_g', 'gmlp_ln_b', 'gmlp_w_s', 'gmlp_b_s', 'up_a', 'up_b', 'mix_w_out', 'ln2_g', 'ln2_b', 'ffn2_w_in', 'ffn2_w_out', 'ln3_g', 'ln3_b', 'ple_w_proj', 'ple_w_gate']
TWIN_DIFF_INPUT = 'x'
TWIN_INPUTS = ['x', 'p', 'ffn1_w_in', 'ffn1_w_out', 'ln1_g', 'ln1_b', 'mix_w_in', 'ssm_lambda_re', 'ssm_lambda_im', 'ssm_log_dt', 'ssm_b_re', 'ssm_b_im', 'ssm_c_re', 'ssm_c_im', 'ssm_d', 'ssm_glu_w', 'ssm_glu_b', 'gmlp_ln_g', 'gmlp_ln_b', 'gmlp_w_s', 'gmlp_b_s', 'up_a', 'up_b', 'mix_w_out', 'ln2_g', 'ln2_b', 'ffn2_w_in', 'ffn2_w_out', 'ln3_g', 'ln3_b', 'ple_w_proj', 'ple_w_gate', 'loss_target', 'm_ffn1_w_in', 'm_ffn1_w_out', 'm_ln1_g', 'm_ln1_b', 'm_mix_w_in', 'm_ssm_lambda_re', 'm_ssm_lambda_im', 'm_ssm_log_dt', 'm_ssm_b_re', 'm_ssm_b_im', 'm_ssm_c_re', 'm_ssm_c_im', 'm_ssm_d', 'm_ssm_glu_w', 'm_ssm_glu_b', 'm_gmlp_ln_g', 'm_gmlp_ln_b', 'm_gmlp_w_s', 'm_gmlp_b_s', 'm_up_a', 'm_up_b', 'm_mix_w_out', 'm_ln2_g', 'm_ln2_b', 'm_ffn2_w_in', 'm_ffn2_w_out', 'm_ln3_g', 'm_ln3_b', 'm_ple_w_proj', 'm_ple_w_gate', 'v_ffn1_w_in', 'v_ffn1_w_out', 'v_ln1_g', 'v_ln1_b', 'v_mix_w_in', 'v_ssm_lambda_re', 'v_ssm_lambda_im', 'v_ssm_log_dt', 'v_ssm_b_re', 'v_ssm_b_im', 'v_ssm_c_re', 'v_ssm_c_im', 'v_ssm_d', 'v_ssm_glu_w', 'v_ssm_glu_b', 'v_gmlp_ln_g', 'v_gmlp_ln_b', 'v_gmlp_w_s', 'v_gmlp_b_s', 'v_up_a', 'v_up_b', 'v_mix_w_out', 'v_ln2_g', 'v_ln2_b', 'v_ffn2_w_in', 'v_ffn2_w_out', 'v_ln3_g', 'v_ln3_b', 'v_ple_w_proj', 'v_ple_w_gate']
TWIN_OUTPUTS = ['loss', 'grad_x', 'grad_ffn1_w_in', 'grad_ffn1_w_out', 'grad_ln1_g', 'grad_ln1_b', 'grad_mix_w_in', 'grad_ssm_lambda_re', 'grad_ssm_lambda_im', 'grad_ssm_log_dt', 'grad_ssm_b_re', 'grad_ssm_b_im', 'grad_ssm_c_re', 'grad_ssm_c_im', 'grad_ssm_d', 'grad_ssm_glu_w', 'grad_ssm_glu_b', 'grad_gmlp_ln_g', 'grad_gmlp_ln_b', 'grad_gmlp_w_s', 'grad_gmlp_b_s', 'grad_up_a', 'grad_up_b', 'grad_mix_w_out', 'grad_ln2_g', 'grad_ln2_b', 'grad_ffn2_w_in', 'grad_ffn2_w_out', 'grad_ln3_g', 'grad_ln3_b', 'grad_ple_w_proj', 'grad_ple_w_gate', 'delta_ffn1_w_in', 'delta_ffn1_w_out', 'delta_ln1_g', 'delta_ln1_b', 'delta_mix_w_in', 'delta_ssm_lambda_re', 'delta_ssm_lambda_im', 'delta_ssm_log_dt', 'delta_ssm_b_re', 'delta_ssm_b_im', 'delta_ssm_c_re', 'delta_ssm_c_im', 'delta_ssm_d', 'delta_ssm_glu_w', 'delta_ssm_glu_b', 'delta_gmlp_ln_g', 'delta_gmlp_ln_b', 'delta_gmlp_w_s', 'delta_gmlp_b_s', 'delta_up_a', 'delta_up_b', 'delta_mix_w_out', 'delta_ln2_g', 'delta_ln2_b', 'delta_ffn2_w_in', 'delta_ffn2_w_out', 'delta_ln3_g', 'delta_ln3_b', 'delta_ple_w_proj', 'delta_ple_w_gate', 'new_m_ffn1_w_in', 'new_m_ffn1_w_out', 'new_m_ln1_g', 'new_m_ln1_b', 'new_m_mix_w_in', 'new_m_ssm_lambda_re', 'new_m_ssm_lambda_im', 'new_m_ssm_log_dt', 'new_m_ssm_b_re', 'new_m_ssm_b_im', 'new_m_ssm_c_re', 'new_m_ssm_c_im', 'new_m_ssm_d', 'new_m_ssm_glu_w', 'new_m_ssm_glu_b', 'new_m_gmlp_ln_g', 'new_m_gmlp_ln_b', 'new_m_gmlp_w_s', 'new_m_gmlp_b_s', 'new_m_up_a', 'new_m_up_b', 'new_m_mix_w_out', 'new_m_ln2_g', 'new_m_ln2_b', 'new_m_ffn2_w_in', 'new_m_ffn2_w_out', 'new_m_ln3_g', 'new_m_ln3_b', 'new_m_ple_w_proj', 'new_m_ple_w_gate', 'new_v_ffn1_w_in', 'new_v_ffn1_w_out', 'new_v_ln1_g', 'new_v_ln1_b', 'new_v_mix_w_in', 'new_v_ssm_lambda_re', 'new_v_ssm_lambda_im', 'new_v_ssm_log_dt', 'new_v_ssm_b_re', 'new_v_ssm_b_im', 'new_v_ssm_c_re', 'new_v_ssm_c_im', 'new_v_ssm_d', 'new_v_ssm_glu_w', 'new_v_ssm_glu_b', 'new_v_gmlp_ln_g', 'new_v_gmlp_ln_b', 'new_v_gmlp_w_s', 'new_v_gmlp_b_s', 'new_v_up_a', 'new_v_up_b', 'new_v_mix_w_out', 'new_v_ln2_g', 'new_v_ln2_b', 'new_v_ffn2_w_in', 'new_v_ffn2_w_out', 'new_v_ln3_g', 'new_v_ln3_b', 'new_v_ple_w_proj', 'new_v_ple_w_gate']
TWIN_LEAF_KINDS = {'loss': 'loss', 'grad_x': 'grad_x', 'grad_ffn1_w_in': 'grad_w', 'grad_ffn1_w_out': 'grad_w', 'grad_ln1_g': 'grad_w', 'grad_ln1_b': 'grad_w', 'grad_mix_w_in': 'grad_w', 'grad_ssm_lambda_re': 'grad_w', 'grad_ssm_lambda_im': 'grad_w', 'grad_ssm_log_dt': 'grad_w', 'grad_ssm_b_re': 'grad_w', 'grad_ssm_b_im': 'grad_w', 'grad_ssm_c_re': 'grad_w', 'grad_ssm_c_im': 'grad_w', 'grad_ssm_d': 'grad_w', 'grad_ssm_glu_w': 'grad_w', 'grad_ssm_glu_b': 'grad_w', 'grad_gmlp_ln_g': 'grad_w', 'grad_gmlp_ln_b': 'grad_w', 'grad_gmlp_w_s': 'grad_w', 'grad_gmlp_b_s': 'grad_w', 'grad_up_a': 'grad_w', 'grad_up_b': 'grad_w', 'grad_mix_w_out': 'grad_w', 'grad_ln2_g': 'grad_w', 'grad_ln2_b': 'grad_w', 'grad_ffn2_w_in': 'grad_w', 'grad_ffn2_w_out': 'grad_w', 'grad_ln3_g': 'grad_w', 'grad_ln3_b': 'grad_w', 'grad_ple_w_proj': 'grad_w', 'grad_ple_w_gate': 'grad_w', 'delta_ffn1_w_in': 'delta_w', 'delta_ffn1_w_out': 'delta_w', 'delta_ln1_g': 'delta_w', 'delta_ln1_b': 'delta_w', 'delta_mix_w_in': 'delta_w', 'delta_ssm_lambda_re': 'delta_w', 'delta_ssm_lambda_im': 'delta_w', 'delta_ssm_log_dt': 'delta_w', 'delta_ssm_b_re': 'delta_w', 'delta_ssm_b_im': 'delta_w', 'delta_ssm_c_re': 'delta_w', 'delta_ssm_c_im': 'delta_w', 'delta_ssm_d': 'delta_w', 'delta_ssm_glu_w': 'delta_w', 'delta_ssm_glu_b': 'delta_w', 'delta_gmlp_ln_g': 'delta_w', 'delta_gmlp_ln_b': 'delta_w', 'delta_gmlp_w_s': 'delta_w', 'delta_gmlp_b_s': 'delta_w', 'delta_up_a': 'delta_w', 'delta_up_b': 'delta_w', 'delta_mix_w_out': 'delta_w', 'delta_ln2_g': 'delta_w', 'delta_ln2_b': 'delta_w', 'delta_ffn2_w_in': 'delta_w', 'delta_ffn2_w_out': 'delta_w', 'delta_ln3_g': 'delta_w', 'delta_ln3_b': 'delta_w', 'delta_ple_w_proj': 'delta_w', 'delta_ple_w_gate': 'delta_w', 'new_m_ffn1_w_in': 'new_m', 'new_m_ffn1_w_out': 'new_m', 'new_m_ln1_g': 'new_m', 'new_m_ln1_b': 'new_m', 'new_m_mix_w_in': 'new_m', 'new_m_ssm_lambda_re': 'new_m', 'new_m_ssm_lambda_im': 'new_m', 'new_m_ssm_log_dt': 'new_m', 'new_m_ssm_b_re': 'new_m', 'new_m_ssm_b_im': 'new_m', 'new_m_ssm_c_re': 'new_m', 'new_m_ssm_c_im': 'new_m', 'new_m_ssm_d': 'new_m', 'new_m_ssm_glu_w': 'new_m', 'new_m_ssm_glu_b': 'new_m', 'new_m_gmlp_ln_g': 'new_m', 'new_m_gmlp_ln_b': 'new_m', 'new_m_gmlp_w_s': 'new_m', 'new_m_gmlp_b_s': 'new_m', 'new_m_up_a': 'new_m', 'new_m_up_b': 'new_m', 'new_m_mix_w_out': 'new_m', 'new_m_ln2_g': 'new_m', 'new_m_ln2_b': 'new_m', 'new_m_ffn2_w_in': 'new_m', 'new_m_ffn2_w_out': 'new_m', 'new_m_ln3_g': 'new_m', 'new_m_ln3_b': 'new_m', 'new_m_ple_w_proj': 'new_m', 'new_m_ple_w_gate': 'new_m', 'new_v_ffn1_w_in': 'new_v', 'new_v_ffn1_w_out': 'new_v', 'new_v_ln1_g': 'new_v', 'new_v_ln1_b': 'new_v', 'new_v_mix_w_in': 'new_v', 'new_v_ssm_lambda_re': 'new_v', 'new_v_ssm_lambda_im': 'new_v', 'new_v_ssm_log_dt': 'new_v', 'new_v_ssm_b_re': 'new_v', 'new_v_ssm_b_im': 'new_v', 'new_v_ssm_c_re': 'new_v', 'new_v_ssm_c_im': 'new_v', 'new_v_ssm_d': 'new_v', 'new_v_ssm_glu_w': 'new_v', 'new_v_ssm_glu_b': 'new_v', 'new_v_gmlp_ln_g': 'new_v', 'new_v_gmlp_ln_b': 'new_v', 'new_v_gmlp_w_s': 'new_v', 'new_v_gmlp_b_s': 'new_v', 'new_v_up_a': 'new_v', 'new_v_up_b': 'new_v', 'new_v_mix_w_out': 'new_v', 'new_v_ln2_g': 'new_v', 'new_v_ln2_b': 'new_v', 'new_v_ffn2_w_in': 'new_v', 'new_v_ffn2_w_out': 'new_v', 'new_v_ln3_g': 'new_v', 'new_v_ln3_b': 'new_v', 'new_v_ple_w_proj': 'new_v', 'new_v_ple_w_gate': 'new_v'}


def _forward(args):
    return _fwd_reference(*[args[k] for k in FWD_PARAMS])


def _output_shape():
    out = _jax.eval_shape(lambda: _forward(_fwd_setup_inputs(0)))
    return out.shape, out.dtype

N_MICROBATCH = 1
ADAM_LR = 0.001
ADAM_B1 = 0.9
ADAM_B2 = 0.999
ADAM_EPS = 1e-08
ADAM_WD = 0.01
ADAM_STEP = 10
PER_EXAMPLE_BATCH_AXIS = {'x': 0, 'p': 1, 'loss_target': 0}
SHARED_INPUTS = []
_WEIGHT_DTYPES = {'ffn1_w_in': _jnp.float32, 'ffn1_w_out': _jnp.float32, 'ln1_g': _jnp.float32, 'ln1_b': _jnp.float32, 'mix_w_in': _jnp.float32, 'ssm_lambda_re': _jnp.float32, 'ssm_lambda_im': _jnp.float32, 'ssm_log_dt': _jnp.float32, 'ssm_b_re': _jnp.float32, 'ssm_b_im': _jnp.float32, 'ssm_c_re': _jnp.float32, 'ssm_c_im': _jnp.float32, 'ssm_d': _jnp.float32, 'ssm_glu_w': _jnp.float32, 'ssm_glu_b': _jnp.float32, 'gmlp_ln_g': _jnp.float32, 'gmlp_ln_b': _jnp.float32, 'gmlp_w_s': _jnp.float32, 'gmlp_b_s': _jnp.float32, 'up_a': _jnp.float32, 'up_b': _jnp.float32, 'mix_w_out': _jnp.float32, 'ln2_g': _jnp.float32, 'ln2_b': _jnp.float32, 'ffn2_w_in': _jnp.float32, 'ffn2_w_out': _jnp.float32, 'ln3_g': _jnp.float32, 'ln3_b': _jnp.float32, 'ple_w_proj': _jnp.float32, 'ple_w_gate': _jnp.float32}
MOMENT_SCALE = {'ffn1_w_in': 1.757893e-02, 'ffn1_w_out': 4.823885e-02, 'ln1_g': 1.037117e+00, 'ln1_b': 1.053230e+00, 'mix_w_in': 2.985671e-02, 'ssm_lambda_re': 1.373371e-03, 'ssm_lambda_im': 1.410626e-03, 'ssm_log_dt': 5.278422e-01, 'ssm_b_re': 8.776804e-04, 'ssm_b_im': 8.998813e-04, 'ssm_c_re': 1.797099e-03, 'ssm_c_im': 1.834072e-03, 'ssm_d': 6.288007e-02, 'ssm_glu_w': 8.991039e-03, 'ssm_glu_b': 2.270919e-02, 'gmlp_ln_g': 3.210229e-02, 'gmlp_ln_b': 3.423098e-02, 'gmlp_w_s': 2.331353e-02, 'gmlp_b_s': 3.363833e-02, 'up_a': 3.665402e-02, 'up_b': 6.665149e-02, 'mix_w_out': 1.284754e-01, 'ln2_g': 1.110295e+00, 'ln2_b': 1.094017e+00, 'ffn2_w_in': 1.715646e-02, 'ffn2_w_out': 4.748341e-02, 'ln3_g': 3.232021e+01, 'ln3_b': 2.680780e+00, 'ple_w_proj': 2.513527e-01, 'ple_w_gate': 7.018074e-02}


def _to_microbatches(a, axis):
    t = _jnp.moveaxis(a, axis, 0)
    t = t.reshape((N_MICROBATCH, t.shape[0] // N_MICROBATCH) + t.shape[1:])
    return _jnp.moveaxis(t, 1, axis + 1)


def setup_inputs(seed: int = 0) -> dict:
    inp = _fwd_setup_inputs(seed)
    key = _jax.random.fold_in(_jax.random.key(seed), 7919)
    shape, _ = _output_shape()
    out = dict(inp)
    out["loss_target"] = _jax.random.normal(_jax.random.fold_in(key, 0), shape, _jnp.float32)
    for i, name in enumerate(TWIN_WEIGHTS):
        w = inp[name].astype(_jnp.float32)
        if MOMENT_SCALE is None:
            s = _jnp.sqrt(_jnp.mean(_jnp.square(w)) + 1e-30)
        else:
            s = MOMENT_SCALE[name]
        km, kv = _jax.random.split(_jax.random.fold_in(key, i + 1))
        out[name] = w
        out["m_" + name] = s * _jax.random.normal(km, w.shape, _jnp.float32)
        out["v_" + name] = (s * s) * _jax.random.uniform(kv, w.shape, _jnp.float32, 0.5, 1.5)
    if N_MICROBATCH > 1:
        for name, axis in PER_EXAMPLE_BATCH_AXIS.items():
            out[name] = _to_microbatches(out[name], axis)
    return {'x': out['x'], 'p': out['p'], 'ffn1_w_in': out['ffn1_w_in'], 'ffn1_w_out': out['ffn1_w_out'], 'ln1_g': out['ln1_g'], 'ln1_b': out['ln1_b'], 'mix_w_in': out['mix_w_in'], 'ssm_lambda_re': out['ssm_lambda_re'], 'ssm_lambda_im': out['ssm_lambda_im'], 'ssm_log_dt': out['ssm_log_dt'], 'ssm_b_re': out['ssm_b_re'], 'ssm_b_im': out['ssm_b_im'], 'ssm_c_re': out['ssm_c_re'], 'ssm_c_im': out['ssm_c_im'], 'ssm_d': out['ssm_d'], 'ssm_glu_w': out['ssm_glu_w'], 'ssm_glu_b': out['ssm_glu_b'], 'gmlp_ln_g': out['gmlp_ln_g'], 'gmlp_ln_b': out['gmlp_ln_b'], 'gmlp_w_s': out['gmlp_w_s'], 'gmlp_b_s': out['gmlp_b_s'], 'up_a': out['up_a'], 'up_b': out['up_b'], 'mix_w_out': out['mix_w_out'], 'ln2_g': out['ln2_g'], 'ln2_b': out['ln2_b'], 'ffn2_w_in': out['ffn2_w_in'], 'ffn2_w_out': out['ffn2_w_out'], 'ln3_g': out['ln3_g'], 'ln3_b': out['ln3_b'], 'ple_w_proj': out['ple_w_proj'], 'ple_w_gate': out['ple_w_gate'], 'loss_target': out['loss_target'], 'm_ffn1_w_in': out['m_ffn1_w_in'], 'm_ffn1_w_out': out['m_ffn1_w_out'], 'm_ln1_g': out['m_ln1_g'], 'm_ln1_b': out['m_ln1_b'], 'm_mix_w_in': out['m_mix_w_in'], 'm_ssm_lambda_re': out['m_ssm_lambda_re'], 'm_ssm_lambda_im': out['m_ssm_lambda_im'], 'm_ssm_log_dt': out['m_ssm_log_dt'], 'm_ssm_b_re': out['m_ssm_b_re'], 'm_ssm_b_im': out['m_ssm_b_im'], 'm_ssm_c_re': out['m_ssm_c_re'], 'm_ssm_c_im': out['m_ssm_c_im'], 'm_ssm_d': out['m_ssm_d'], 'm_ssm_glu_w': out['m_ssm_glu_w'], 'm_ssm_glu_b': out['m_ssm_glu_b'], 'm_gmlp_ln_g': out['m_gmlp_ln_g'], 'm_gmlp_ln_b': out['m_gmlp_ln_b'], 'm_gmlp_w_s': out['m_gmlp_w_s'], 'm_gmlp_b_s': out['m_gmlp_b_s'], 'm_up_a': out['m_up_a'], 'm_up_b': out['m_up_b'], 'm_mix_w_out': out['m_mix_w_out'], 'm_ln2_g': out['m_ln2_g'], 'm_ln2_b': out['m_ln2_b'], 'm_ffn2_w_in': out['m_ffn2_w_in'], 'm_ffn2_w_out': out['m_ffn2_w_out'], 'm_ln3_g': out['m_ln3_g'], 'm_ln3_b': out['m_ln3_b'], 'm_ple_w_proj': out['m_ple_w_proj'], 'm_ple_w_gate': out['m_ple_w_gate'], 'v_ffn1_w_in': out['v_ffn1_w_in'], 'v_ffn1_w_out': out['v_ffn1_w_out'], 'v_ln1_g': out['v_ln1_g'], 'v_ln1_b': out['v_ln1_b'], 'v_mix_w_in': out['v_mix_w_in'], 'v_ssm_lambda_re': out['v_ssm_lambda_re'], 'v_ssm_lambda_im': out['v_ssm_lambda_im'], 'v_ssm_log_dt': out['v_ssm_log_dt'], 'v_ssm_b_re': out['v_ssm_b_re'], 'v_ssm_b_im': out['v_ssm_b_im'], 'v_ssm_c_re': out['v_ssm_c_re'], 'v_ssm_c_im': out['v_ssm_c_im'], 'v_ssm_d': out['v_ssm_d'], 'v_ssm_glu_w': out['v_ssm_glu_w'], 'v_ssm_glu_b': out['v_ssm_glu_b'], 'v_gmlp_ln_g': out['v_gmlp_ln_g'], 'v_gmlp_ln_b': out['v_gmlp_ln_b'], 'v_gmlp_w_s': out['v_gmlp_w_s'], 'v_gmlp_b_s': out['v_gmlp_b_s'], 'v_up_a': out['v_up_a'], 'v_up_b': out['v_up_b'], 'v_mix_w_out': out['v_mix_w_out'], 'v_ln2_g': out['v_ln2_g'], 'v_ln2_b': out['v_ln2_b'], 'v_ffn2_w_in': out['v_ffn2_w_in'], 'v_ffn2_w_out': out['v_ffn2_w_out'], 'v_ln3_g': out['v_ln3_g'], 'v_ln3_b': out['v_ln3_b'], 'v_ple_w_proj': out['v_ple_w_proj'], 'v_ple_w_gate': out['v_ple_w_gate']}


def _loss(weights, diff, rest, loss_target):
    with _jax.named_scope("forward"):
        args = {**rest, TWIN_DIFF_INPUT: diff, **{k: w.astype(_WEIGHT_DTYPES[k]) for k, w in weights.items()}}
        y = _forward(args)
    with _jax.named_scope("loss_head"):
        err = _jnp.square(y.astype(_jnp.float32) - loss_target)
        return 0.5 * _jnp.sum(_jnp.mean(err, axis=-1)) if err.ndim else 0.5 * err


def _adamw(w, g, m, v):
    m = ADAM_B1 * m + (1.0 - ADAM_B1) * g
    v = ADAM_B2 * v + (1.0 - ADAM_B2) * _jnp.square(g)
    m_hat = m / (1.0 - ADAM_B1 ** ADAM_STEP)
    v_hat = v / (1.0 - ADAM_B2 ** ADAM_STEP)
    delta = -ADAM_LR * (m_hat / (_jnp.sqrt(v_hat) + ADAM_EPS) + ADAM_WD * w)
    return delta, m, v


def reference(x, p, ffn1_w_in, ffn1_w_out, ln1_g, ln1_b, mix_w_in, ssm_lambda_re, ssm_lambda_im, ssm_log_dt, ssm_b_re, ssm_b_im, ssm_c_re, ssm_c_im, ssm_d, ssm_glu_w, ssm_glu_b, gmlp_ln_g, gmlp_ln_b, gmlp_w_s, gmlp_b_s, up_a, up_b, mix_w_out, ln2_g, ln2_b, ffn2_w_in, ffn2_w_out, ln3_g, ln3_b, ple_w_proj, ple_w_gate, loss_target, m_ffn1_w_in, m_ffn1_w_out, m_ln1_g, m_ln1_b, m_mix_w_in, m_ssm_lambda_re, m_ssm_lambda_im, m_ssm_log_dt, m_ssm_b_re, m_ssm_b_im, m_ssm_c_re, m_ssm_c_im, m_ssm_d, m_ssm_glu_w, m_ssm_glu_b, m_gmlp_ln_g, m_gmlp_ln_b, m_gmlp_w_s, m_gmlp_b_s, m_up_a, m_up_b, m_mix_w_out, m_ln2_g, m_ln2_b, m_ffn2_w_in, m_ffn2_w_out, m_ln3_g, m_ln3_b, m_ple_w_proj, m_ple_w_gate, v_ffn1_w_in, v_ffn1_w_out, v_ln1_g, v_ln1_b, v_mix_w_in, v_ssm_lambda_re, v_ssm_lambda_im, v_ssm_log_dt, v_ssm_b_re, v_ssm_b_im, v_ssm_c_re, v_ssm_c_im, v_ssm_d, v_ssm_glu_w, v_ssm_glu_b, v_gmlp_ln_g, v_gmlp_ln_b, v_gmlp_w_s, v_gmlp_b_s, v_up_a, v_up_b, v_mix_w_out, v_ln2_g, v_ln2_b, v_ffn2_w_in, v_ffn2_w_out, v_ln3_g, v_ln3_b, v_ple_w_proj, v_ple_w_gate):
    given = dict(x=x, p=p, ffn1_w_in=ffn1_w_in, ffn1_w_out=ffn1_w_out, ln1_g=ln1_g, ln1_b=ln1_b, mix_w_in=mix_w_in, ssm_lambda_re=ssm_lambda_re, ssm_lambda_im=ssm_lambda_im, ssm_log_dt=ssm_log_dt, ssm_b_re=ssm_b_re, ssm_b_im=ssm_b_im, ssm_c_re=ssm_c_re, ssm_c_im=ssm_c_im, ssm_d=ssm_d, ssm_glu_w=ssm_glu_w, ssm_glu_b=ssm_glu_b, gmlp_ln_g=gmlp_ln_g, gmlp_ln_b=gmlp_ln_b, gmlp_w_s=gmlp_w_s, gmlp_b_s=gmlp_b_s, up_a=up_a, up_b=up_b, mix_w_out=mix_w_out, ln2_g=ln2_g, ln2_b=ln2_b, ffn2_w_in=ffn2_w_in, ffn2_w_out=ffn2_w_out, ln3_g=ln3_g, ln3_b=ln3_b, ple_w_proj=ple_w_proj, ple_w_gate=ple_w_gate, loss_target=loss_target, m_ffn1_w_in=m_ffn1_w_in, m_ffn1_w_out=m_ffn1_w_out, m_ln1_g=m_ln1_g, m_ln1_b=m_ln1_b, m_mix_w_in=m_mix_w_in, m_ssm_lambda_re=m_ssm_lambda_re, m_ssm_lambda_im=m_ssm_lambda_im, m_ssm_log_dt=m_ssm_log_dt, m_ssm_b_re=m_ssm_b_re, m_ssm_b_im=m_ssm_b_im, m_ssm_c_re=m_ssm_c_re, m_ssm_c_im=m_ssm_c_im, m_ssm_d=m_ssm_d, m_ssm_glu_w=m_ssm_glu_w, m_ssm_glu_b=m_ssm_glu_b, m_gmlp_ln_g=m_gmlp_ln_g, m_gmlp_ln_b=m_gmlp_ln_b, m_gmlp_w_s=m_gmlp_w_s, m_gmlp_b_s=m_gmlp_b_s, m_up_a=m_up_a, m_up_b=m_up_b, m_mix_w_out=m_mix_w_out, m_ln2_g=m_ln2_g, m_ln2_b=m_ln2_b, m_ffn2_w_in=m_ffn2_w_in, m_ffn2_w_out=m_ffn2_w_out, m_ln3_g=m_ln3_g, m_ln3_b=m_ln3_b, m_ple_w_proj=m_ple_w_proj, m_ple_w_gate=m_ple_w_gate, v_ffn1_w_in=v_ffn1_w_in, v_ffn1_w_out=v_ffn1_w_out, v_ln1_g=v_ln1_g, v_ln1_b=v_ln1_b, v_mix_w_in=v_mix_w_in, v_ssm_lambda_re=v_ssm_lambda_re, v_ssm_lambda_im=v_ssm_lambda_im, v_ssm_log_dt=v_ssm_log_dt, v_ssm_b_re=v_ssm_b_re, v_ssm_b_im=v_ssm_b_im, v_ssm_c_re=v_ssm_c_re, v_ssm_c_im=v_ssm_c_im, v_ssm_d=v_ssm_d, v_ssm_glu_w=v_ssm_glu_w, v_ssm_glu_b=v_ssm_glu_b, v_gmlp_ln_g=v_gmlp_ln_g, v_gmlp_ln_b=v_gmlp_ln_b, v_gmlp_w_s=v_gmlp_w_s, v_gmlp_b_s=v_gmlp_b_s, v_up_a=v_up_a, v_up_b=v_up_b, v_mix_w_out=v_mix_w_out, v_ln2_g=v_ln2_g, v_ln2_b=v_ln2_b, v_ffn2_w_in=v_ffn2_w_in, v_ffn2_w_out=v_ffn2_w_out, v_ln3_g=v_ln3_g, v_ln3_b=v_ln3_b, v_ple_w_proj=v_ple_w_proj, v_ple_w_gate=v_ple_w_gate)
    weights = {n: given[n] for n in TWIN_WEIGHTS}
    shared = {n: given[n] for n in SHARED_INPUTS}
    per_example = {n: given[n] for n in ['x', 'p']}
    grad_fn = _jax.value_and_grad(_loss, argnums=(0, 1))

    def one_microbatch(ex, loss_target):
        ex = dict(ex)
        diff = ex.pop(TWIN_DIFF_INPUT)
        return grad_fn(weights, diff, {**shared, **ex}, loss_target)

    if N_MICROBATCH == 1:
        loss, (grad_w, grad_x) = one_microbatch(per_example, given["loss_target"])
    else:
        def body(carry, xs):
            loss_sum, grad_sum = carry
            l_k, (gw_k, gx_k) = one_microbatch(xs[0], xs[1])
            with _jax.named_scope("update"):
                return (loss_sum + l_k, _jax.tree.map(_jnp.add, grad_sum, gw_k)), gx_k

        init = (_jnp.zeros((), _jnp.float32), _jax.tree.map(_jnp.zeros_like, weights))
        (loss, grad_w), grad_x = _jax.lax.scan(body, init, (per_example, given["loss_target"]))
    with _jax.named_scope("update"):
        delta_w, new_m, new_v = {}, {}, {}
        for n in TWIN_WEIGHTS:
            delta_w[n], new_m[n], new_v[n] = _adamw(weights[n], grad_w[n], given["m_" + n], given["v_" + n])
    return (loss, grad_x, *[grad_w[n] for n in TWIN_WEIGHTS], *[delta_w[n] for n in TWIN_WEIGHTS],
            *[new_m[n] for n in TWIN_WEIGHTS], *[new_v[n] for n in TWIN_WEIGHTS])
```

```python
import functools
import math

import jax
import jax.numpy as jnp
from jax import lax
from jax.experimental import pallas as pl
from jax.experimental.pallas import tpu as pltpu

F32 = jnp.float32
BF16 = jnp.bfloat16
MESH = pl.DeviceIdType.MESH
SDS = jax.ShapeDtypeStruct

D_MODEL = 1024
D_FF = 2816
D_SSM = 512
D_GMLP = 512
SSM_GROUPS = 32
SSM_GROUP_CH = 16
SSM_STATE = 64
SSM_LANES = SSM_GROUPS * SSM_STATE
GMLP_HEADS = 8
GMLP_HEAD_DIM = 64
CHUNK = 128
PLE_DIM = 256
LN_EPS = 1e-5
ALPHA = 2.0 ** 0.25

ADAM_LR = 0.001
ADAM_B1 = 0.9
ADAM_B2 = 0.999
ADAM_EPS = 1e-08
ADAM_WD = 0.01
ADAM_STEP = 10
ADAM_C1 = 1.0 - ADAM_B1 ** ADAM_STEP
ADAM_C2 = 1.0 - ADAM_B2 ** ADAM_STEP

N_CHIPS = 4
N_DEV = 8
VMEM_LIMIT_BYTES = 56 * 1024 * 1024
FFN_COLS = 1408
S5_BLOCKS = 4
S5_BLOCK_IN = D_SSM // S5_BLOCKS
S5_BLOCK_ST = SSM_LANES // S5_BLOCKS
SCAN_LANES = 512
_G0 = math.sqrt(2.0 / math.pi)
_G1 = 0.044715


def _dot(a, b):
    return jnp.dot(a, b, preferred_element_type=F32)


def _dot_nt(a, b):
    return lax.dot_general(a, b, (((1,), (1,)), ((), ())), preferred_element_type=F32)


def _dot_tn(a, b):
    return lax.dot_general(a, b, (((0,), (0,)), ((), ())), preferred_element_type=F32)


def _sigmoid(x):
    return 1.0 / (1.0 + jnp.exp(-x))


def _gelu(x):
    t = jnp.tanh(_G0 * (x + _G1 * x * x * x))
    return 0.5 * x * (1.0 + t)


def _gelu_grad(x):
    t = jnp.tanh(_G0 * (x + _G1 * x * x * x))
    return 0.5 * (1.0 + t) + 0.5 * x * (1.0 - t * t) * _G0 * (1.0 + 3.0 * _G1 * x * x)


def _ln_fwd(r, g, b):
    mu = jnp.mean(r, axis=-1, keepdims=True)
    d = r - mu
    var = jnp.mean(d * d, axis=-1, keepdims=True)
    rstd = lax.rsqrt(var + LN_EPS)
    xh = d * rstd
    return xh * g + b, xh, rstd


def _ln_bwd(dy, xh, rstd, g):
    dxh = dy * g
    m1 = jnp.mean(dxh, axis=-1, keepdims=True)
    m2 = jnp.mean(dxh * xh, axis=-1, keepdims=True)
    return rstd * (dxh - m1 - xh * m2)


def _resident(shape):
    nd = len(shape)
    return pl.BlockSpec(shape, lambda *_: (0,) * nd, pipeline_mode=pl.Buffered(1))


def _fixed(shape):
    nd = len(shape)
    return pl.BlockSpec(shape, lambda *_: (0,) * nd)


def _rows(tm, cols):
    return pl.BlockSpec((tm, cols), lambda i: (i, 0))


def _params(sem):
    return pltpu.CompilerParams(dimension_semantics=sem, vmem_limit_bytes=VMEM_LIMIT_BYTES)


def _ffn_fwd(x, w_in, w_out, g, b, tm, name):
    t = x.shape[0]
    nch = D_FF // FFN_COLS

    def body(x_ref, win_ref, wout_ref, g_ref, b_ref, xn_ref, xh_ref, rstd_ref, xb_ref, h_ref):
        xv = x_ref[...]
        xb = xv.astype(BF16)
        xb_ref[...] = xb
        f = jnp.zeros((tm, D_MODEL), F32)
        for k in range(nch):
            cg = slice(k * FFN_COLS, (k + 1) * FFN_COLS)
            cu = slice(D_FF + k * FFN_COLS, D_FF + (k + 1) * FFN_COLS)
            hg = _dot(xb, win_ref[:, cg])
            hu = _dot(xb, win_ref[:, cu])
            h_ref[:, cg] = hg.astype(BF16)
            h_ref[:, cu] = hu.astype(BF16)
            a = hg * _sigmoid(hg) * hu
            f = f + _dot(a.astype(BF16), wout_ref[cg, :])
        y, xh, rstd = _ln_fwd(ALPHA * xv + 0.5 * f, g_ref[...], b_ref[...])
        xn_ref[...] = y
        xh_ref[...] = xh
        rstd_ref[...] = rstd

    return pl.pallas_call(
        body, name=name, grid=(t // tm,),
        out_shape=(SDS((t, D_MODEL), F32), SDS((t, D_MODEL), F32), SDS((t, 1), F32),
                   SDS((t, D_MODEL), BF16), SDS((t, 2 * D_FF), BF16)),
        in_specs=[_rows(tm, D_MODEL), _resident((D_MODEL, 2 * D_FF)), _resident((D_FF, D_MODEL)),
                  _fixed((1, D_MODEL)), _fixed((1, D_MODEL))],
        out_specs=(_rows(tm, D_MODEL), _rows(tm, D_MODEL), _rows(tm, 1), _rows(tm, D_MODEL),
                   _rows(tm, 2 * D_FF)),
        compiler_params=_params(("parallel",)),
    )(x, w_in, w_out, g, b)


def _ffn_bwd(dxn, xh, rstd, h, w_in, w_out, g, tm, name):
    t = dxn.shape[0]
    nch = D_FF // FFN_COLS

    def body(dxn_ref, xh_ref, rstd_ref, h_ref, win_ref, wout_ref, g_ref,
             dx_ref, dh_ref, a_ref, df_ref, dg_ref, db_ref):
        @pl.when(pl.program_id(0) == 0)
        def _():
            dg_ref[...] = jnp.zeros_like(dg_ref)
            db_ref[...] = jnp.zeros_like(db_ref)

        dy = dxn_ref[...]
        xhv = xh_ref[...]
        dr = _ln_bwd(dy, xhv, rstd_ref[...], g_ref[...])
        dg_ref[...] += jnp.sum(dy * xhv, axis=0, keepdims=True)
        db_ref[...] += jnp.sum(dy, axis=0, keepdims=True)
        df = (0.5 * dr).astype(BF16)
        df_ref[...] = df
        dx = ALPHA * dr
        for k in range(nch):
            cg = slice(k * FFN_COLS, (k + 1) * FFN_COLS)
            cu = slice(D_FF + k * FFN_COLS, D_FF + (k + 1) * FFN_COLS)
            hg = h_ref[:, cg].astype(F32)
            hu = h_ref[:, cu].astype(F32)
            sg = _sigmoid(hg)
            silu = hg * sg
            a_ref[:, cg] = (silu * hu).astype(BF16)
            da = _dot_nt(df, wout_ref[cg, :])
            dhu = (da * silu).astype(BF16)
            dhg = (da * hu * (sg * (1.0 + hg * (1.0 - sg)))).astype(BF16)
            dh_ref[:, cg] = dhg
            dh_ref[:, cu] = dhu
            dx = dx + _dot_nt(dhg, win_ref[:, cg]) + _dot_nt(dhu, win_ref[:, cu])
        dx_ref[...] = dx

    return pl.pallas_call(
        body, name=name, grid=(t // tm,),
        out_shape=(SDS((t, D_MODEL), F32), SDS((t, 2 * D_FF), BF16), SDS((t, D_FF), BF16),
                   SDS((t, D_MODEL), BF16), SDS((1, D_MODEL), F32), SDS((1, D_MODEL), F32)),
        in_specs=[_rows(tm, D_MODEL), _rows(tm, D_MODEL), _rows(tm, 1), _rows(tm, 2 * D_FF),
                  _resident((D_MODEL, 2 * D_FF)), _resident((D_FF, D_MODEL)), _fixed((1, D_MODEL))],
        out_specs=(_rows(tm, D_MODEL), _rows(tm, 2 * D_FF), _rows(tm, D_FF), _rows(tm, D_MODEL),
                   _fixed((1, D_MODEL)), _fixed((1, D_MODEL))),
        compiler_params=_params(("arbitrary",)),
    )(dxn, xh, rstd, h, w_in, w_out, g)


def _tn_matmul(a, b, name, bm, bn, col_block=0, total_cols=None, prev=None):
    t, m = a.shape
    n = b.shape[1]
    total_cols = n if total_cols is None else total_cols
    bk = min(512, t)
    nk = t // bk

    def body(*refs):
        a_ref, b_ref = refs[0], refs[1]
        o_ref, ob_ref = refs[-2], refs[-1]
        k = pl.program_id(2)

        @pl.when(k == 0)
        def _():
            o_ref[...] = jnp.zeros_like(o_ref)

        o_ref[...] += _dot_tn(a_ref[...], b_ref[...])

        @pl.when(k == nk - 1)
        def _():
            ob_ref[...] = o_ref[...].astype(BF16)

    in_specs = [pl.BlockSpec((bk, bm), lambda i, j, k: (k, i)),
                pl.BlockSpec((bk, bn), lambda i, j, k: (k, j))]
    args = [a, b]
    aliases = {}
    if prev is not None:
        in_specs += [pl.BlockSpec(memory_space=pl.ANY), pl.BlockSpec(memory_space=pl.ANY)]
        args += list(prev)
        aliases = {2: 0, 3: 1}
    out_spec = pl.BlockSpec((bm, bn), lambda i, j, k: (i, j + col_block))
    return pl.pallas_call(
        body, name=name, grid=(m // bm, n // bn, nk),
        out_shape=(SDS((m, total_cols), F32), SDS((m, total_cols), BF16)),
        in_specs=in_specs, out_specs=(out_spec, out_spec), input_output_aliases=aliases,
        compiler_params=_params(("parallel", "parallel", "arbitrary")),
    )(*args)


def _mixin_fwd(x1, w, tm):
    t = x1.shape[0]

    def body(x_ref, w_ref, xb_ref, za_ref, zuv_ref, gab_ref):
        xb = x_ref[...].astype(BF16)
        xb_ref[...] = xb
        za_ref[...] = _dot(xb, w_ref[:, 0:512]).astype(BF16)
        zuv_ref[...] = _dot(xb, w_ref[:, 512:1536]).astype(BF16)
        gab_ref[...] = _dot(xb, w_ref[:, 1536:3584]).astype(BF16)

    return pl.pallas_call(
        body, name="mixin_fwd", grid=(t // tm,),
        out_shape=(SDS((t, D_MODEL), BF16), SDS((t, 512), BF16), SDS((t, 1024), BF16),
                   SDS((t, 2048), BF16)),
        in_specs=[_rows(tm, D_MODEL), _resident((D_MODEL, 3584))],
        out_specs=(_rows(tm, D_MODEL), _rows(tm, 512), _rows(tm, 1024), _rows(tm, 2048)),
        compiler_params=_params(("parallel",)),
    )(x1, w)


def _mixin_bwd(dx1a, dza, dzuv, dgab, w, tm):
    t = dx1a.shape[0]

    def body(d_ref, dza_ref, dzuv_ref, dgab_ref, w_ref, dx_ref):
        dx_ref[...] = (d_ref[...] + _dot_nt(dza_ref[...], w_ref[:, 0:512])
                       + _dot_nt(dzuv_ref[...], w_ref[:, 512:1536])
                       + _dot_nt(dgab_ref[...], w_ref[:, 1536:3584]))

    return pl.pallas_call(
        body, name="mixin_bwd", grid=(t // tm,),
        out_shape=SDS((t, D_MODEL), F32),
        in_specs=[_rows(tm, D_MODEL), _rows(tm, 512), _rows(tm, 1024), _rows(tm, 2048),
                  _resident((D_MODEL, 3584))],
        out_specs=_rows(tm, D_MODEL),
        compiler_params=_params(("parallel",)),
    )(dx1a, dza, dzuv, dgab, w)


def _scan_fwd(hr_ref, hi_ref, a_ref, ap_ref, carry_ref, seg, cin_ref):
    for lc in range(SSM_LANES // SCAN_LANES):
        ls = slice(lc * SCAN_LANES, (lc + 1) * SCAN_LANES)
        a_r = jnp.broadcast_to(a_ref[0:1, ls], (8, SCAN_LANES))
        a_i = jnp.broadcast_to(a_ref[1:2, ls], (8, SCAN_LANES))

        def step(j, hc, ls=ls, a_r=a_r, a_i=a_i):
            h_r, h_i = hc
            rows = pl.ds(pl.multiple_of(j * 8, 8), 8)
            n_r = a_r * h_r - a_i * h_i + hr_ref[rows, ls]
            n_i = a_r * h_i + a_i * h_r + hi_ref[rows, ls]
            hr_ref[rows, ls] = n_r
            hi_ref[rows, ls] = n_i
            return n_r, n_i

        zero = jnp.zeros((8, SCAN_LANES), F32)
        f_r, f_i = lax.fori_loop(0, seg, step, (zero, zero))
        c_r = carry_ref[0:1, ls]
        c_i = carry_ref[1:2, ls]
        p_r = ap_ref[0:1, ls]
        p_i = ap_ref[1:2, ls]
        rows_r, rows_i = [], []
        for s in range(8):
            rows_r.append(c_r)
            rows_i.append(c_i)
            c_r, c_i = (f_r[s:s + 1] + p_r * c_r - p_i * c_i,
                        f_i[s:s + 1] + p_r * c_i + p_i * c_r)
        carry_ref[0:1, ls] = c_r
        carry_ref[1:2, ls] = c_i
        cin_r = jnp.concatenate(rows_r, axis=0)
        cin_i = jnp.concatenate(rows_i, axis=0)
        if cin_ref is not None:
            cin_ref[0, :, ls] = cin_r
            cin_ref[1, :, ls] = cin_i

        def fix(j, cc, ls=ls, a_r=a_r, a_i=a_i):
            c_r, c_i = cc
            c_r, c_i = a_r * c_r - a_i * c_i, a_r * c_i + a_i * c_r
            rows = pl.ds(pl.multiple_of(j * 8, 8), 8)
            hr_ref[rows, ls] = hr_ref[rows, ls] + c_r
            hi_ref[rows, ls] = hi_ref[rows, ls] + c_i
            return c_r, c_i

        lax.fori_loop(0, seg, fix, (cin_r, cin_i))


def _scan_bwd(gr_ref, gi_ref, hr_ref, hi_ref, cin_ref, a_ref, ap_ref, rcarry_ref, da_ref, seg):
    for lc in range(SSM_LANES // SCAN_LANES):
        ls = slice(lc * SCAN_LANES, (lc + 1) * SCAN_LANES)
        a_r = jnp.broadcast_to(a_ref[0:1, ls], (8, SCAN_LANES))
        a_i = jnp.broadcast_to(a_ref[1:2, ls], (8, SCAN_LANES))

        def step(t, gc, ls=ls, a_r=a_r, a_i=a_i):
            g_r, g_i = gc
            rows = pl.ds(pl.multiple_of((seg - 1 - t) * 8, 8), 8)
            n_r = gr_ref[rows, ls] + a_r * g_r + a_i * g_i
            n_i = gi_ref[rows, ls] + a_r * g_i - a_i * g_r
            gr_ref[rows, ls] = n_r
            gi_ref[rows, ls] = n_i
            return n_r, n_i

        zero = jnp.zeros((8, SCAN_LANES), F32)
        f_r, f_i = lax.fori_loop(0, seg, step, (zero, zero))
        c_r = rcarry_ref[0:1, ls]
        c_i = rcarry_ref[1:2, ls]
        p_r = ap_ref[0:1, ls]
        p_i = ap_ref[1:2, ls]
        rows_r, rows_i = [None] * 8, [None] * 8
        for s in range(7, -1, -1):
            rows_r[s] = c_r
            rows_i[s] = c_i
            c_r, c_i = (f_r[s:s + 1] + p_r * c_r + p_i * c_i,
                        f_i[s:s + 1] + p_r * c_i - p_i * c_r)
        rcarry_ref[0:1, ls] = c_r
        rcarry_ref[1:2, ls] = c_i
        cin_r = jnp.concatenate(rows_r, axis=0)
        cin_i = jnp.concatenate(rows_i, axis=0)

        def fix_row(j_rows, hp_r, hp_i, cc, ls=ls, a_r=a_r, a_i=a_i):
            c_r, c_i, acc_r, acc_i = cc
            c_r, c_i = a_r * c_r + a_i * c_i, a_r * c_i - a_i * c_r
            g_r = gr_ref[j_rows, ls] + c_r
            g_i = gi_ref[j_rows, ls] + c_i
            gr_ref[j_rows, ls] = g_r
            gi_ref[j_rows, ls] = g_i
            acc_r = acc_r + g_r * hp_r + g_i * hp_i
            acc_i = acc_i + g_i * hp_r - g_r * hp_i
            return c_r, c_i, acc_r, acc_i

        def fix(t, cc, ls=ls, fix_row=fix_row):
            j = seg - 1 - t
            rows = pl.ds(pl.multiple_of(j * 8, 8), 8)
            prev = pl.ds(pl.multiple_of((j - 1) * 8, 8), 8)
            return fix_row(rows, hr_ref[prev, ls], hi_ref[prev, ls], cc)

        cc = lax.fori_loop(0, seg - 1, fix, (cin_r, cin_i, zero, zero))
        _, _, acc_r, acc_i = fix_row(pl.ds(0, 8), cin_ref[0, :, ls], cin_ref[1, :, ls], cc)
        da_ref[0, :, ls] += acc_r
        da_ref[1, :, ls] += acc_i


def _s5_fwd(za, sp, bsz, seq, tb):
    nb = seq // tb
    seg = tb // 8
    t = bsz * seq

    def body(za_ref, perm_ref, permt_ref, mre_ref, mim_ref, nre_ref, nim_ref, a_ref, ap_ref,
             dsk_ref, gw_ref, gb_ref, out_ref, y2_ref, car_ref, hr_ref, hi_ref, carry_ref):
        @pl.when(pl.program_id(1) == 0)
        def _():
            carry_ref[...] = jnp.zeros_like(carry_ref)

        car_ref[0] = carry_ref[...]
        up = _dot(perm_ref[...], za_ref[...])
        upb = up.astype(BF16)
        for bb in range(S5_BLOCKS):
            ub = upb[:, bb * S5_BLOCK_IN:(bb + 1) * S5_BLOCK_IN]
            st = slice(bb * S5_BLOCK_ST, (bb + 1) * S5_BLOCK_ST)
            hr_ref[:, st] = _dot(ub, mre_ref[bb])
            hi_ref[:, st] = _dot(ub, mim_ref[bb])
        _scan_fwd(hr_ref, hi_ref, a_ref, ap_ref, carry_ref, seg, None)
        ys = []
        for bb in range(S5_BLOCKS):
            st = slice(bb * S5_BLOCK_ST, (bb + 1) * S5_BLOCK_ST)
            ys.append(_dot(hr_ref[:, st].astype(BF16), nre_ref[bb])
                      - _dot(hi_ref[:, st].astype(BF16), nim_ref[bb]))
        y2 = jnp.concatenate(ys, axis=1) + dsk_ref[...] * up
        y2_ref[...] = y2
        y3 = _gelu(y2)
        gl = _dot(y3.astype(BF16), gw_ref[...]) + gb_ref[...]
        oa = y3 * _sigmoid(gl)
        out_ref[...] = _dot(permt_ref[...], oa.astype(BF16)).astype(BF16)

    blk = pl.BlockSpec((tb, D_SSM), lambda b, j: (b * nb + j, 0))
    return pl.pallas_call(
        body, name="s5_fwd", grid=(bsz, nb),
        out_shape=(SDS((t, D_SSM), BF16), SDS((t, D_SSM), F32), SDS((bsz * nb, 2, SSM_LANES), F32)),
        in_specs=[blk, _fixed((tb, tb)), _fixed((tb, tb)),
                  _fixed((S5_BLOCKS, S5_BLOCK_IN, S5_BLOCK_ST)), _fixed((S5_BLOCKS, S5_BLOCK_IN, S5_BLOCK_ST)),
                  _fixed((S5_BLOCKS, S5_BLOCK_ST, S5_BLOCK_IN)), _fixed((S5_BLOCKS, S5_BLOCK_ST, S5_BLOCK_IN)),
                  _fixed((2, SSM_LANES)), _fixed((2, SSM_LANES)), _fixed((1, D_SSM)),
                  _fixed((D_SSM, D_SSM)), _fixed((1, D_SSM))],
        out_specs=(blk, blk, pl.BlockSpec((1, 2, SSM_LANES), lambda b, j: (b * nb + j, 0, 0))),
        scratch_shapes=[pltpu.VMEM((tb, SSM_LANES), F32), pltpu.VMEM((tb, SSM_LANES), F32),
                        pltpu.VMEM((2, SSM_LANES), F32)],
        compiler_params=_params(("arbitrary", "arbitrary")),
    )(za, sp["perm"], sp["permt"], sp["mre"], sp["mim"], sp["nre"], sp["nim"], sp["a"], sp["ap"],
      sp["dskip"], sp["glu_w"], sp["glu_b"])


def _s5_bwd(za, y2p, doa, carries, sp, bsz, seq, tb):
    nb = seq // tb
    seg = tb // 8
    t = bsz * seq

    def body(za_ref, y2_ref, doa_ref, car_ref, perm_ref, permt_ref, mre_ref, mim_ref, mtre_ref, mtim_ref,
             nre_ref, nim_ref, ntre_ref, ntim_ref, a_ref, ap_ref, dsk_ref, gw_ref, gwt_ref, gb_ref,
             dza_ref, dmr_ref, dmi_ref, dnr_ref, dni_ref, da_ref, ddsk_ref, dgw_ref, dgb_ref,
             hr_ref, hi_ref, gr_ref, gi_ref, cin_ref, carry_ref, rcarry_ref):
        first = jnp.logical_and(pl.program_id(0) == 0, pl.program_id(1) == 0)

        @pl.when(first)
        def _():
            for r in (dmr_ref, dmi_ref, dnr_ref, dni_ref, da_ref, ddsk_ref, dgw_ref, dgb_ref):
                r[...] = jnp.zeros_like(r)

        @pl.when(pl.program_id(1) == 0)
        def _():
            rcarry_ref[...] = jnp.zeros_like(rcarry_ref)

        carry_ref[...] = car_ref[0]
        perm = perm_ref[...]
        up = _dot(perm, za_ref[...])
        upb = up.astype(BF16)
        for bb in range(S5_BLOCKS):
            ub = upb[:, bb * S5_BLOCK_IN:(bb + 1) * S5_BLOCK_IN]
            st = slice(bb * S5_BLOCK_ST, (bb + 1) * S5_BLOCK_ST)
            hr_ref[:, st] = _dot(ub, mre_ref[bb])
            hi_ref[:, st] = _dot(ub, mim_ref[bb])
        _scan_fwd(hr_ref, hi_ref, a_ref, ap_ref, carry_ref, seg, cin_ref)

        y2 = y2_ref[...]
        y3 = _gelu(y2)
        y3b = y3.astype(BF16)
        sg = _sigmoid(_dot(y3b, gw_ref[...]) + gb_ref[...])
        d0 = doa_ref[...]
        d_hi = d0.astype(BF16)
        d1 = d0 - d_hi.astype(F32)
        d_mid = d1.astype(BF16)
        d_lo = (d1 - d_mid.astype(F32)).astype(BF16)
        doap = _dot(perm, d_hi) + _dot(perm, d_mid) + _dot(perm, d_lo)
        dgl = doap * y3 * sg * (1.0 - sg)
        dglb = dgl.astype(BF16)
        dy3 = doap * sg + _dot(dglb, gwt_ref[...])
        dgw_ref[...] += _dot_tn(y3b, dglb)
        dgb_ref[...] += jnp.sum(dgl, axis=0, keepdims=True)
        dy2 = dy3 * _gelu_grad(y2)
        ddsk_ref[...] += jnp.sum(dy2 * up, axis=0, keepdims=True)
        dyb = dy2.astype(BF16)
        for bb in range(S5_BLOCKS):
            dyc = dyb[:, bb * S5_BLOCK_IN:(bb + 1) * S5_BLOCK_IN]
            st = slice(bb * S5_BLOCK_ST, (bb + 1) * S5_BLOCK_ST)
            gr_ref[:, st] = _dot(dyc, ntre_ref[bb])
            gi_ref[:, st] = -_dot(dyc, ntim_ref[bb])
            dnr_ref[bb] += _dot_tn(hr_ref[:, st].astype(BF16), dyc)
            dni_ref[bb] += -_dot_tn(hi_ref[:, st].astype(BF16), dyc)
        _scan_bwd(gr_ref, gi_ref, hr_ref, hi_ref, cin_ref, a_ref, ap_ref, rcarry_ref, da_ref, seg)
        dus = []
        for bb in range(S5_BLOCKS):
            st = slice(bb * S5_BLOCK_ST, (bb + 1) * S5_BLOCK_ST)
            grb = gr_ref[:, st].astype(BF16)
            gib = gi_ref[:, st].astype(BF16)
            dus.append(_dot(grb, mtre_ref[bb]) + _dot(gib, mtim_ref[bb]))
            ub = upb[:, bb * S5_BLOCK_IN:(bb + 1) * S5_BLOCK_IN]
            dmr_ref[bb] += _dot_tn(ub, grb)
            dmi_ref[bb] += _dot_tn(ub, gib)
        du = jnp.concatenate(dus, axis=1) + dy2 * dsk_ref[...]
        dza_ref[...] = _dot(permt_ref[...], du.astype(BF16)).astype(BF16)

    def rev(b, j):
        return (b * nb + (nb - 1 - j), 0)

    blk = pl.BlockSpec((tb, D_SSM), rev)
    m_shape = (S5_BLOCKS, S5_BLOCK_IN, S5_BLOCK_ST)
    n_shape = (S5_BLOCKS, S5_BLOCK_ST, S5_BLOCK_IN)
    return pl.pallas_call(
        body, name="s5_bwd", grid=(bsz, nb),
        out_shape=(SDS((t, D_SSM), BF16), SDS(m_shape, F32), SDS(m_shape, F32), SDS(n_shape, F32),
                   SDS(n_shape, F32), SDS((2, 8, SSM_LANES), F32), SDS((1, D_SSM), F32),
                   SDS((D_SSM, D_SSM), F32), SDS((1, D_SSM), F32)),
        in_specs=[blk, blk, blk,
                  pl.BlockSpec((1, 2, SSM_LANES), lambda b, j: (b * nb + (nb - 1 - j), 0, 0)),
                  _fixed((tb, tb)), _fixed((tb, tb)),
                  _fixed(m_shape), _fixed(m_shape), _fixed(n_shape), _fixed(n_shape),
                  _fixed(n_shape), _fixed(n_shape), _fixed(m_shape), _fixed(m_shape),
                  _fixed((2, SSM_LANES)), _fixed((2, SSM_LANES)), _fixed((1, D_SSM)),
                  _fixed((D_SSM, D_SSM)), _fixed((D_SSM, D_SSM)), _fixed((1, D_SSM))],
        out_specs=(blk, _fixed(m_shape), _fixed(m_shape), _fixed(n_shape), _fixed(n_shape),
                   _fixed((2, 8, SSM_LANES)), _fixed((1, D_SSM)), _fixed((D_SSM, D_SSM)),
                   _fixed((1, D_SSM))),
        scratch_shapes=[pltpu.VMEM((tb, SSM_LANES), F32), pltpu.VMEM((tb, SSM_LANES), F32),
                        pltpu.VMEM((tb, SSM_LANES), F32), pltpu.VMEM((tb, SSM_LANES), F32),
                        pltpu.VMEM((2, 8, SSM_LANES), F32), pltpu.VMEM((2, SSM_LANES), F32),
                        pltpu.VMEM((2, SSM_LANES), F32)],
        compiler_params=_params(("arbitrary", "arbitrary")),
    )(za, y2p, doa, carries, sp["perm"], sp["permt"], sp["mre"], sp["mim"], sp["mtre"], sp["mtim"],
      sp["nre"], sp["nim"], sp["ntre"], sp["ntim"], sp["a"], sp["ap"], sp["dskip"], sp["glu_w"],
      sp["glu_wt"], sp["glu_b"])


def _gmlp_spatial(ws_ref, vb):
    lane = lax.broadcasted_iota(jnp.int32, (CHUNK, 128), 1)
    parts = []
    for j in range(GMLP_HEADS // 2):
        vp = vb[:, 128 * j:128 * (j + 1)]
        parts.append(jnp.where(lane < GMLP_HEAD_DIM, _dot(ws_ref[2 * j], vp), _dot(ws_ref[2 * j + 1], vp)))
    return jnp.concatenate(parts, axis=1)


def _gmlp_fwd(zuv, ln_g, ln_b, wsm, bias):
    t = zuv.shape[0]

    def body(z_ref, g_ref, b_ref, ws_ref, bias_ref, out_ref):
        u = _gelu(z_ref[:, 0:D_GMLP].astype(F32))
        v0 = _gelu(z_ref[:, D_GMLP:2 * D_GMLP].astype(F32))
        v, _, _ = _ln_fwd(v0, g_ref[...], b_ref[...])
        s = _gmlp_spatial(ws_ref, v.astype(BF16)) + bias_ref[...]
        out_ref[...] = (u * s).astype(BF16)

    return pl.pallas_call(
        body, name="gmlp_fwd", grid=(t // CHUNK,),
        out_shape=SDS((t, D_GMLP), BF16),
        in_specs=[_rows(CHUNK, 2 * D_GMLP), _fixed((1, D_GMLP)), _fixed((1, D_GMLP)),
                  _fixed((GMLP_HEADS, CHUNK, CHUNK)), _fixed((CHUNK, D_GMLP))],
        out_specs=_rows(CHUNK, D_GMLP),
        compiler_params=_params(("parallel",)),
    )(zuv, ln_g, ln_b, wsm, bias)


def _gmlp_bwd(zuv, dgm, ln_g, ln_b, wsm, wsmt, bias):
    t = zuv.shape[0]

    def body(z_ref, d_ref, g_ref, b_ref, ws_ref, wst_ref, bias_ref,
             dz_ref, dws_ref, dbias_ref, dg_ref, db_ref):
        @pl.when(pl.program_id(0) == 0)
        def _():
            for r in (dws_ref, dbias_ref, dg_ref, db_ref):
                r[...] = jnp.zeros_like(r)

        zu = z_ref[:, 0:D_GMLP].astype(F32)
        zv = z_ref[:, D_GMLP:2 * D_GMLP].astype(F32)
        u = _gelu(zu)
        v0 = _gelu(zv)
        gam = g_ref[...]
        v, vhat, rstd = _ln_fwd(v0, gam, b_ref[...])
        vb = v.astype(BF16)
        s = _gmlp_spatial(ws_ref, vb) + bias_ref[...]
        d = d_ref[...]
        dz_ref[:, 0:D_GMLP] = (d * s * _gelu_grad(zu)).astype(BF16)
        ds = d * u
        dbias_ref[...] += ds
        dsb = ds.astype(BF16)
        lane = lax.broadcasted_iota(jnp.int32, (CHUNK, 128), 1)
        tril = (lax.broadcasted_iota(jnp.int32, (CHUNK, CHUNK), 0)
                >= lax.broadcasted_iota(jnp.int32, (CHUNK, CHUNK), 1))
        zero_b = jnp.zeros((CHUNK, 128), BF16)
        parts = []
        for j in range(GMLP_HEADS // 2):
            dsp = dsb[:, 128 * j:128 * (j + 1)]
            vp = vb[:, 128 * j:128 * (j + 1)]
            parts.append(jnp.where(lane < GMLP_HEAD_DIM, _dot(wst_ref[2 * j], dsp),
                                   _dot(wst_ref[2 * j + 1], dsp)))
            lo = jnp.where(lane < GMLP_HEAD_DIM, dsp, zero_b)
            hi = jnp.where(lane < GMLP_HEAD_DIM, zero_b, dsp)
            dws_ref[2 * j] += jnp.where(tril, _dot_nt(lo, vp), 0.0)
            dws_ref[2 * j + 1] += jnp.where(tril, _dot_nt(hi, vp), 0.0)
        dv = jnp.concatenate(parts, axis=1)
        dg_ref[...] += jnp.sum(dv * vhat, axis=0, keepdims=True)
        db_ref[...] += jnp.sum(dv, axis=0, keepdims=True)
        dz_ref[:, D_GMLP:2 * D_GMLP] = (_ln_bwd(dv, vhat, rstd, gam) * _gelu_grad(zv)).astype(BF16)

    return pl.pallas_call(
        body, name="gmlp_bwd", grid=(t // CHUNK,),
        out_shape=(SDS((t, 2 * D_GMLP), BF16), SDS((GMLP_HEADS, CHUNK, CHUNK), F32),
                   SDS((CHUNK, D_GMLP), F32), SDS((1, D_GMLP), F32), SDS((1, D_GMLP), F32)),
        in_specs=[_rows(CHUNK, 2 * D_GMLP), _rows(CHUNK, D_GMLP), _fixed((1, D_GMLP)), _fixed((1, D_GMLP)),
                  _fixed((GMLP_HEADS, CHUNK, CHUNK)), _fixed((GMLP_HEADS, CHUNK, CHUNK)),
                  _fixed((CHUNK, D_GMLP))],
        out_specs=(_rows(CHUNK, 2 * D_GMLP), _fixed((GMLP_HEADS, CHUNK, CHUNK)), _fixed((CHUNK, D_GMLP)),
                   _fixed((1, D_GMLP)), _fixed((1, D_GMLP))),
        compiler_params=_params(("arbitrary",)),
    )(zuv, dgm, ln_g, ln_b, wsm, wsmt, bias)


def _mixout_fwd(x1, s5o, gm, gab, ua, ub, wmo, g, b, tm):
    t = x1.shape[0]

    def body(x_ref, s_ref, m_ref, gab_ref, ua_ref, ub_ref, wmo_ref, g_ref, b_ref,
             xn_ref, xh_ref, rstd_ref, xb_ref):
        ya = _dot(s_ref[...], ua_ref[...])
        yb = _dot(m_ref[...], ub_ref[...])
        mix = (_sigmoid(gab_ref[:, 0:D_MODEL].astype(F32)) * ya
               + _sigmoid(gab_ref[:, D_MODEL:2 * D_MODEL].astype(F32)) * yb)
        r = ALPHA * x_ref[...] + _dot(mix.astype(BF16), wmo_ref[...])
        y, xh, rstd = _ln_fwd(r, g_ref[...], b_ref[...])
        xn_ref[...] = y
        xh_ref[...] = xh
        rstd_ref[...] = rstd
        xb_ref[...] = y.astype(BF16)

    return pl.pallas_call(
        body, name="mixout_fwd", grid=(t // tm,),
        out_shape=(SDS((t, D_MODEL), F32), SDS((t, D_MODEL), F32), SDS((t, 1), F32), SDS((t, D_MODEL), BF16)),
        in_specs=[_rows(tm, D_MODEL), _rows(tm, D_SSM), _rows(tm, D_GMLP), _rows(tm, 2 * D_MODEL),
                  _resident((D_SSM, D_MODEL)), _resident((D_GMLP, D_MODEL)), _resident((D_MODEL, D_MODEL)),
                  _fixed((1, D_MODEL)), _fixed((1, D_MODEL))],
        out_specs=(_rows(tm, D_MODEL), _rows(tm, D_MODEL), _rows(tm, 1), _rows(tm, D_MODEL)),
        compiler_params=_params(("parallel",)),
    )(x1, s5o, gm, gab, ua, ub, wmo, g, b)


def _mixout_bwd(dx2, xh, rstd, s5o, gm, gab, ua, ub, wmo, g, tm):
    t = dx2.shape[0]

    def body(d_ref, xh_ref, rstd_ref, s_ref, m_ref, gab_ref, ua_ref, ub_ref, wmo_ref, g_ref,
             dx1_ref, dmx_ref, mb_ref, dya_ref, dyb_ref, ds5_ref, dgm_ref, dgab_ref, dg_ref, db_ref):
        @pl.when(pl.program_id(0) == 0)
        def _():
            dg_ref[...] = jnp.zeros_like(dg_ref)
            db_ref[...] = jnp.zeros_like(db_ref)

        dy = d_ref[...]
        xhv = xh_ref[...]
        dr = _ln_bwd(dy, xhv, rstd_ref[...], g_ref[...])
        dg_ref[...] += jnp.sum(dy * xhv, axis=0, keepdims=True)
        db_ref[...] += jnp.sum(dy, axis=0, keepdims=True)
        dx1_ref[...] = ALPHA * dr
        drb = dr.astype(BF16)
        dmx_ref[...] = drb
        dm = _dot_nt(drb, wmo_ref[...])
        ya = _dot(s_ref[...], ua_ref[...])
        yb = _dot(m_ref[...], ub_ref[...])
        sa = _sigmoid(gab_ref[:, 0:D_MODEL].astype(F32))
        sb = _sigmoid(gab_ref[:, D_MODEL:2 * D_MODEL].astype(F32))
        mb_ref[...] = (sa * ya + sb * yb).astype(BF16)
        dya = (dm * sa).astype(BF16)
        dyb = (dm * sb).astype(BF16)
        dya_ref[...] = dya
        dyb_ref[...] = dyb
        dgab_ref[:, 0:D_MODEL] = (dm * ya * sa * (1.0 - sa)).astype(BF16)
        dgab_ref[:, D_MODEL:2 * D_MODEL] = (dm * yb * sb * (1.0 - sb)).astype(BF16)
        ds5_ref[...] = _dot_nt(dya, ua_ref[...])
        dgm_ref[...] = _dot_nt(dyb, ub_ref[...])

    return pl.pallas_call(
        body, name="mixout_bwd", grid=(t // tm,),
        out_shape=(SDS((t, D_MODEL), F32), SDS((t, D_MODEL), BF16), SDS((t, D_MODEL), BF16),
                   SDS((t, D_MODEL), BF16), SDS((t, D_MODEL), BF16), SDS((t, D_SSM), F32),
                   SDS((t, D_GMLP), F32), SDS((t, 2 * D_MODEL), BF16),
                   SDS((1, D_MODEL), F32), SDS((1, D_MODEL), F32)),
        in_specs=[_rows(tm, D_MODEL), _rows(tm, D_MODEL), _rows(tm, 1), _rows(tm, D_SSM), _rows(tm, D_GMLP),
                  _rows(tm, 2 * D_MODEL), _resident((D_SSM, D_MODEL)), _resident((D_GMLP, D_MODEL)),
                  _resident((D_MODEL, D_MODEL)), _fixed((1, D_MODEL))],
        out_specs=(_rows(tm, D_MODEL), _rows(tm, D_MODEL), _rows(tm, D_MODEL), _rows(tm, D_MODEL),
                   _rows(tm, D_MODEL), _rows(tm, D_SSM), _rows(tm, D_GMLP), _rows(tm, 2 * D_MODEL),
                   _fixed((1, D_MODEL)), _fixed((1, D_MODEL))),
        compiler_params=_params(("arbitrary",)),
    )(dx2, xh, rstd, s5o, gm, gab, ua, ub, wmo, g)


def _ple_loss(x3, p, tgt, wpg, wpp, tm):
    t = x3.shape[0]

    def body(x_ref, p_ref, t_ref, wpg_ref, wpp_ref, dx_ref, xb_ref, pb_ref, dq_ref, de_ref, loss_ref):
        @pl.when(pl.program_id(0) == 0)
        def _():
            loss_ref[...] = jnp.zeros_like(loss_ref)

        x3v = x_ref[...]
        xb = x3v.astype(BF16)
        pb = p_ref[...].astype(BF16)
        xb_ref[...] = xb
        pb_ref[...] = pb
        s = _sigmoid(_dot(xb, wpg_ref[...]))
        e = _dot(pb, wpp_ref[...])
        diff = x3v + s * e - t_ref[...]
        loss_ref[...] += jnp.sum(diff * diff, axis=0, keepdims=True)
        dout = diff * (1.0 / D_MODEL)
        de_ref[...] = (dout * s).astype(BF16)
        dq = (dout * e * s * (1.0 - s)).astype(BF16)
        dq_ref[...] = dq
        dx_ref[...] = dout + _dot_nt(dq, wpg_ref[...])

    return pl.pallas_call(
        body, name="ple_loss", grid=(t // tm,),
        out_shape=(SDS((t, D_MODEL), F32), SDS((t, D_MODEL), BF16), SDS((t, PLE_DIM), BF16),
                   SDS((t, D_MODEL), BF16), SDS((t, D_MODEL), BF16), SDS((1, D_MODEL), F32)),
        in_specs=[_rows(tm, D_MODEL), _rows(tm, PLE_DIM), _rows(tm, D_MODEL),
                  _resident((D_MODEL, D_MODEL)), _resident((PLE_DIM, D_MODEL))],
        out_specs=(_rows(tm, D_MODEL), _rows(tm, D_MODEL), _rows(tm, PLE_DIM), _rows(tm, D_MODEL),
                   _rows(tm, D_MODEL), _fixed((1, D_MODEL))),
        compiler_params=_params(("arbitrary",)),
    )(x3, p, tgt, wpg, wpp)


def _s5_discretise(lre, lim, log_dt, bre, bim):
    dt = jnp.exp(log_dt)[:, None]
    mag = jnp.exp(lre * dt)
    abr = mag * jnp.cos(lim * dt)
    abi = mag * jnp.sin(lim * dt)
    nr = abr - 1.0
    ni = abi
    den = lre * lre + lim * lim
    cr = ((nr * lre + ni * lim) / den)[..., None]
    ci = ((ni * lre - nr * lim) / den)[..., None]
    return abr, abi, cr * bre - ci * bim, cr * bim + ci * bre


def _block_diag_in(bb):
    v = bb.reshape(S5_BLOCKS, 8, SSM_STATE, SSM_GROUP_CH).transpose(0, 1, 3, 2)
    return jnp.einsum("bgip,gh->bgihp", v, jnp.eye(8, dtype=bb.dtype)).reshape(
        S5_BLOCKS, S5_BLOCK_IN, S5_BLOCK_ST)


def _block_diag_in_t(dm):
    v = dm.reshape(S5_BLOCKS, 8, SSM_GROUP_CH, 8, SSM_STATE)
    d = jnp.einsum("bgihp,gh->bgip", v, jnp.eye(8, dtype=dm.dtype))
    return d.transpose(0, 1, 3, 2).reshape(SSM_GROUPS, SSM_STATE, SSM_GROUP_CH)


def _block_diag_out(cc):
    v = cc.reshape(S5_BLOCKS, 8, SSM_GROUP_CH, SSM_STATE)
    return jnp.einsum("bgip,gh->bgphi", v, jnp.eye(8, dtype=cc.dtype)).reshape(
        S5_BLOCKS, S5_BLOCK_ST, S5_BLOCK_IN)


def _block_diag_out_t(dn):
    v = dn.reshape(S5_BLOCKS, 8, SSM_STATE, 8, SSM_GROUP_CH)
    d = jnp.einsum("bgphi,gh->bgip", v, jnp.eye(8, dtype=dn.dtype))
    return d.reshape(SSM_GROUPS, SSM_GROUP_CH, SSM_STATE)


def _s5_setup(lre, lim, log_dt, bre, bim, cre, cim, d_skip, glu_w, glu_b, tb):
    seg = tb // 8
    abr, abi, bbr, bbi = _s5_discretise(lre, lim, log_dt, bre, bim)
    pr, pi = abr, abi
    for _ in range(int(math.log2(seg))):
        pr, pi = pr * pr - pi * pi, 2.0 * pr * pi
    rows = jnp.arange(tb)
    src = (rows % 8) * seg + rows // 8
    perm = (src[:, None] == jnp.arange(tb)[None, :]).astype(BF16)
    mre = _block_diag_in(bbr)
    mim = _block_diag_in(bbi)
    nre = _block_diag_out(cre)
    nim = _block_diag_out(cim)
    return {
        "perm": perm, "permt": perm.T,
        "mre": mre.astype(BF16), "mim": mim.astype(BF16),
        "mtre": mre.transpose(0, 2, 1).astype(BF16), "mtim": mim.transpose(0, 2, 1).astype(BF16),
        "nre": nre.astype(BF16), "nim": nim.astype(BF16),
        "ntre": nre.transpose(0, 2, 1).astype(BF16), "ntim": nim.transpose(0, 2, 1).astype(BF16),
        "a": jnp.stack([abr.reshape(-1), abi.reshape(-1)]),
        "ap": jnp.stack([pr.reshape(-1), pi.reshape(-1)]),
        "dskip": d_skip.reshape(1, D_SSM), "glu_w": glu_w, "glu_wt": glu_w.T,
        "glu_b": glu_b.reshape(1, D_SSM),
    }


BIG = ("ffn1_w_in", "ffn1_w_out", "mix_w_in", "ssm_glu_w", "up_a", "up_b", "mix_w_out",
       "ffn2_w_in", "ffn2_w_out", "ple_w_proj", "ple_w_gate")
BIG_AXIS = {"ffn1_w_in": 1, "ffn1_w_out": 0, "mix_w_in": 1, "ssm_glu_w": 0, "up_a": 1, "up_b": 1,
            "mix_w_out": 0, "ffn2_w_in": 1, "ffn2_w_out": 0, "ple_w_proj": 1, "ple_w_gate": 0}
SMALL = ("ln1_g", "ln1_b", "ssm_lambda_re", "ssm_lambda_im", "ssm_log_dt", "ssm_b_re", "ssm_b_im",
         "ssm_c_re", "ssm_c_im", "ssm_d", "ssm_glu_b", "gmlp_ln_g", "gmlp_ln_b", "gmlp_w_s",
         "gmlp_b_s", "ln2_g", "ln2_b", "ln3_g", "ln3_b")
SMALL_2D = {"ln1_g": (1, 1024), "ln1_b": (1, 1024), "ssm_lambda_re": (32, 64), "ssm_lambda_im": (32, 64),
            "ssm_log_dt": (1, 32), "ssm_b_re": (32, 1024), "ssm_b_im": (32, 1024), "ssm_c_re": (32, 1024),
            "ssm_c_im": (32, 1024), "ssm_d": (1, 512), "ssm_glu_b": (1, 512), "gmlp_ln_g": (1, 512),
            "gmlp_ln_b": (1, 512), "gmlp_w_s": (1024, 128), "gmlp_b_s": (8, 128), "ln2_g": (1, 1024),
            "ln2_b": (1, 1024), "ln3_g": (1, 1024), "ln3_b": (1, 1024)}


def _local_step(x, p, tgt, wb, ws):
    bsz, seq, _ = x.shape
    t = bsz * seq
    tm = min(256, t)
    tb = min(256, seq)
    x0 = x.reshape(t, D_MODEL)
    p0 = p.reshape(t, PLE_DIM)
    tg = tgt.reshape(t, D_MODEL)
    row = lambda v: v.reshape(1, -1)

    sp = _s5_setup(ws["ssm_lambda_re"], ws["ssm_lambda_im"], ws["ssm_log_dt"], ws["ssm_b_re"],
                   ws["ssm_b_im"], ws["ssm_c_re"], ws["ssm_c_im"], ws["ssm_d"], wb["ssm_glu_w"],
                   ws["ssm_glu_b"], tb)
    tril = jnp.tril(jnp.ones((CHUNK, CHUNK), dtype=bool))
    wsm = jnp.where(tril[None], ws["gmlp_w_s"], 0.0)
    wsm_b = wsm.astype(BF16)
    wsmt_b = wsm.transpose(0, 2, 1).astype(BF16)
    bias = jnp.repeat(ws["gmlp_b_s"].T, GMLP_HEAD_DIM, axis=1)

    x1, xh1, rstd1, x0b, h1 = _ffn_fwd(x0, wb["ffn1_w_in"], wb["ffn1_w_out"], row(ws["ln1_g"]),
                                        row(ws["ln1_b"]), tm, "ffn1_fwd")
    x1b, za, zuv, gab = _mixin_fwd(x1, wb["mix_w_in"], tm)
    s5o, y2p, carries = _s5_fwd(za, sp, bsz, seq, tb)
    gm = _gmlp_fwd(zuv, row(ws["gmlp_ln_g"]), row(ws["gmlp_ln_b"]), wsm_b, bias)
    x2, xh2, rstd2, x2b = _mixout_fwd(x1, s5o, gm, gab, wb["up_a"], wb["up_b"], wb["mix_w_out"],
                                      row(ws["ln2_g"]), row(ws["ln2_b"]), tm)
    x3, xh3, rstd3, _, h2 = _ffn_fwd(x2, wb["ffn2_w_in"], wb["ffn2_w_out"], row(ws["ln3_g"]),
                                     row(ws["ln3_b"]), tm, "ffn2_fwd")
    dx3, x3b, pb, dq, de, loss_rows = _ple_loss(x3, p0, tg, wb["ple_w_gate"], wb["ple_w_proj"], tm)
    gb = {}
    gs = {}
    gb["ple_w_gate"] = _tn_matmul(x3b, dq, "dw_ple_gate", 1024, 1024)
    gb["ple_w_proj"] = _tn_matmul(pb, de, "dw_ple_proj", 256, 1024)
    dx2, dh2, a2, df2, gs["ln3_g"], gs["ln3_b"] = _ffn_bwd(dx3, xh3, rstd3, h2, wb["ffn2_w_in"],
                                                          wb["ffn2_w_out"], row(ws["ln3_g"]), tm, "ffn2_bwd")
    gb["ffn2_w_in"] = _tn_matmul(x2b, dh2, "dw_ffn2_in", 1024, 1408)
    gb["ffn2_w_out"] = _tn_matmul(a2, df2, "dw_ffn2_out", 1408, 1024)
    (dx1a, dmx, mb, dya, dyb, ds5, dgm, dgab, gs["ln2_g"], gs["ln2_b"]) = _mixout_bwd(
        dx2, xh2, rstd2, s5o, gm, gab, wb["up_a"], wb["up_b"], wb["mix_w_out"], row(ws["ln2_g"]), tm)
    gb["mix_w_out"] = _tn_matmul(mb, dmx, "dw_mix_out", 1024, 1024)
    gb["up_a"] = _tn_matmul(s5o, dya, "dw_up_a", 512, 1024)
    gb["up_b"] = _tn_matmul(gm, dyb, "dw_up_b", 512, 1024)
    (dza, dmr, dmi, dnr, dni, da, ddsk, dgw, dgb) = _s5_bwd(za, y2p, ds5, carries, sp, bsz, seq, tb)
    dzuv, dws, dbias, gs["gmlp_ln_g"], gs["gmlp_ln_b"] = _gmlp_bwd(
        zuv, dgm, row(ws["gmlp_ln_g"]), row(ws["gmlp_ln_b"]), wsm_b, wsmt_b, bias)
    dx1 = _mixin_bwd(dx1a, dza, dzuv, dgab, wb["mix_w_in"], tm)
    g_mi = _tn_matmul(x1b, dza, "dw_mix_in_a", 1024, 512, 0, 3584)
    g_mi = _tn_matmul(x1b, dzuv, "dw_mix_in_uv", 1024, 512, 1, 3584, g_mi)
    gb["mix_w_in"] = _tn_matmul(x1b, dgab, "dw_mix_in_g", 1024, 512, 3, 3584, g_mi)
    dx0, dh1, a1, df1, gs["ln1_g"], gs["ln1_b"] = _ffn_bwd(dx1, xh1, rstd1, h1, wb["ffn1_w_in"],
                                                          wb["ffn1_w_out"], row(ws["ln1_g"]), tm, "ffn1_bwd")
    gb["ffn1_w_in"] = _tn_matmul(x0b, dh1, "dw_ffn1_in", 1024, 1408)
    gb["ffn1_w_out"] = _tn_matmul(a1, df1, "dw_ffn1_out", 1408, 1024)
    gb["ssm_glu_w"] = (dgw, dgw.astype(BF16))

    d_abr = da[0].sum(axis=0).reshape(SSM_GROUPS, SSM_STATE)
    d_abi = da[1].sum(axis=0).reshape(SSM_GROUPS, SSM_STATE)
    _, vjp = jax.vjp(_s5_discretise, ws["ssm_lambda_re"], ws["ssm_lambda_im"], ws["ssm_log_dt"],
                     ws["ssm_b_re"], ws["ssm_b_im"])
    (gs["ssm_lambda_re"], gs["ssm_lambda_im"], gs["ssm_log_dt"], gs["ssm_b_re"], gs["ssm_b_im"]) = vjp(
        (d_abr, d_abi, _block_diag_in_t(dmr), _block_diag_in_t(dmi)))
    gs["ssm_c_re"] = _block_diag_out_t(dnr)
    gs["ssm_c_im"] = _block_diag_out_t(dni)
    gs["ssm_d"] = ddsk
    gs["ssm_glu_b"] = dgb
    gs["gmlp_w_s"] = dws
    gs["gmlp_b_s"] = dbias.reshape(CHUNK, GMLP_HEADS, GMLP_HEAD_DIM).sum(axis=-1).T
    gs = {k: gs[k].reshape(SMALL_2D[k]) for k in SMALL}
    return loss_rows, dx0.reshape(bsz, seq, D_MODEL), gb, gs


def _place():
    return lax.axis_index("x"), lax.axis_index("y"), lax.axis_index("c")


def _window(ref, shard_shape, axis, chip, half):
    r, c = shard_shape
    hr = r // 2
    if axis == 0:
        if half is None:
            return ref.at[pl.ds(chip * r, r), :]
        return ref.at[pl.ds(chip * r + half * hr, hr), :]
    if half is None:
        return ref.at[:, pl.ds(chip * c, c)]
    return ref.at[pl.ds(half * hr, hr), pl.ds(chip * c, c)]


def _gather_weights(shards, axes):
    n = len(shards)
    shapes = [s.shape for s in shards]
    full = [(4 * r, c) if ax == 0 else (r, 4 * c) for (r, c), ax in zip(shapes, axes)]

    def body(*refs):
        ins, outs = refs[:n], refs[n:2 * n]
        send_sems, recv_sems, local_sems = refs[2 * n:]
        x, y, c = _place()
        me = 2 * x + y
        sibling = (x, y, 1 - c)
        chips = [(1 - x, y), (x, 1 - y), (1 - x, 1 - y)]

        def remote(i, sem, src, dst, to):
            return pltpu.make_async_remote_copy(src_ref=src, dst_ref=dst, send_sem=send_sems.at[6 * i + sem],
                                                recv_sem=recv_sems.at[6 * i + sem], device_id=to,
                                                device_id_type=MESH)

        started = []
        local = []
        for i in range(n):
            cp = pltpu.make_async_copy(ins[i], _window(outs[i], shapes[i], axes[i], me, None), local_sems.at[i])
            cp.start()
            local.append(cp)
            hr = shapes[i][0] // 2
            mine = ins[i].at[pl.ds(c * hr, hr), :]
            for j, (cx, cy) in enumerate(chips):
                cp = remote(i, j, mine, _window(outs[i], shapes[i], axes[i], me, c), (cx, cy, c))
                cp.start()
                started.append(cp)
        for j, (cx, cy) in enumerate(chips):
            for i in range(n):
                w = _window(outs[i], shapes[i], axes[i], 2 * cx + cy, c)
                remote(i, j, w, w, (cx, cy, c)).wait_recv()
                cp = remote(i, 3 + j, w, w, sibling)
                cp.start()
                started.append(cp)
        for j, (cx, cy) in enumerate(chips):
            for i in range(n):
                w = _window(outs[i], shapes[i], axes[i], 2 * cx + cy, 1 - c)
                remote(i, 3 + j, w, w, sibling).wait_recv()
        for cp in started:
            cp.wait_send()
        for cp in local:
            cp.wait()

    any_spec = pl.BlockSpec(memory_space=pl.ANY)
    return pl.pallas_call(
        body, name="gather_weights",
        out_shape=tuple(SDS(f, BF16) for f in full),
        in_specs=[any_spec] * n, out_specs=tuple([any_spec] * n),
        scratch_shapes=[pltpu.SemaphoreType.DMA((6 * n,)), pltpu.SemaphoreType.DMA((6 * n,)),
                        pltpu.SemaphoreType.DMA((n,))],
    )(*shards)


def _scatter_grads(parts, shapes, axes):
    n = len(parts)

    def body(*refs):
        ins, outs = refs[:n], refs[n:2 * n]
        send_sems, recv_sems = refs[2 * n:]
        x, y, c = _place()
        chips = [(1 - x, y), (x, 1 - y), (1 - x, 1 - y)]
        copies = []
        for i in range(n):
            for j, (cx, cy) in enumerate(chips):
                cp = pltpu.make_async_remote_copy(
                    src_ref=_window(ins[i], shapes[i], axes[i], 2 * cx + cy, None), dst_ref=outs[i].at[j],
                    send_sem=send_sems.at[3 * i + j], recv_sem=recv_sems.at[3 * i + j],
                    device_id=(cx, cy, c), device_id_type=MESH)
                cp.start()
                copies.append(cp)
        for cp in copies:
            cp.wait()

    any_spec = pl.BlockSpec(memory_space=pl.ANY)
    return pl.pallas_call(
        body, name="scatter_grads",
        out_shape=tuple(SDS((3,) + tuple(s), BF16) for s in shapes),
        in_specs=[any_spec] * n, out_specs=tuple([any_spec] * n),
        scratch_shapes=[pltpu.SemaphoreType.DMA((3 * n,)), pltpu.SemaphoreType.DMA((3 * n,))],
    )(*parts)


def _swap_with_sibling(arrs):
    n = len(arrs)

    def body(*refs):
        ins, outs = refs[:n], refs[n:2 * n]
        send_sems, recv_sems = refs[2 * n:]
        x, y, c = _place()
        copies = []
        for i in range(n):
            cp = pltpu.make_async_remote_copy(src_ref=ins[i], dst_ref=outs[i], send_sem=send_sems.at[i],
                                              recv_sem=recv_sems.at[i], device_id=(x, y, 1 - c),
                                              device_id_type=MESH)
            cp.start()
            copies.append(cp)
        for cp in copies:
            cp.wait()

    any_spec = pl.BlockSpec(memory_space=pl.ANY)
    return pl.pallas_call(
        body, name="swap_with_sibling",
        out_shape=tuple(SDS(a.shape, a.dtype) for a in arrs),
        in_specs=[any_spec] * n, out_specs=tuple([any_spec] * n),
        scratch_shapes=[pltpu.SemaphoreType.DMA((n,)), pltpu.SemaphoreType.DMA((n,))],
    )(*arrs)


def _gather_small(arrs):
    n = len(arrs)

    def body(*refs):
        ins, outs = refs[:n], refs[n:2 * n]
        send_sems, recv_sems = refs[2 * n:]
        x, y, c = _place()
        sibling = (x, y, 1 - c)
        chips = [(1 - x, y), (x, 1 - y), (1 - x, 1 - y)]

        def copy(i, k, block, to, src=None):
            px, py, pc = block
            dst = outs[i].at[4 * px + 2 * py + pc]
            return pltpu.make_async_remote_copy(
                src_ref=dst if src is None else src, dst_ref=dst, send_sem=send_sems.at[7 * i + k],
                recv_sem=recv_sems.at[7 * i + k], device_id=to, device_id_type=MESH)

        started = []
        for i in range(n):
            outs[i][4 * x + 2 * y + c] = ins[i][...]
            cp = copy(i, 0, (x, y, c), sibling, src=ins[i])
            cp.start()
            started.append(cp)
            for j, (cx, cy) in enumerate(chips):
                cp = copy(i, 1 + j, (x, y, c), (cx, cy, c), src=ins[i])
                cp.start()
                started.append(cp)
        for j, (cx, cy) in enumerate(chips):
            for i in range(n):
                copy(i, 1 + j, (cx, cy, c), (x, y, c)).wait_recv()
                cp = copy(i, 4 + j, (cx, cy, c), sibling)
                cp.start()
                started.append(cp)
        for i in range(n):
            copy(i, 0, (x, y, 1 - c), (x, y, c)).wait_recv()
            for j, (cx, cy) in enumerate(chips):
                copy(i, 4 + j, (cx, cy, 1 - c), (x, y, c)).wait_recv()
        for cp in started:
            cp.wait_send()

    vmem = pl.BlockSpec(memory_space=pltpu.VMEM)
    return pl.pallas_call(
        body, name="gather_small",
        out_shape=tuple(SDS((N_DEV,) + a.shape, F32) for a in arrs),
        in_specs=[vmem] * n, out_specs=tuple([vmem] * n),
        scratch_shapes=[pltpu.SemaphoreType.DMA((7 * n,)), pltpu.SemaphoreType.DMA((7 * n,))],
    )(*arrs)


def _adamw(w, g, m, v):
    m = ADAM_B1 * m + (1.0 - ADAM_B1) * g
    v = ADAM_B2 * v + (1.0 - ADAM_B2) * (g * g)
    m_hat = m / ADAM_C1
    v_hat = v / ADAM_C2
    delta = -ADAM_LR * (m_hat / (jnp.sqrt(v_hat) + ADAM_EPS) + ADAM_WD * w)
    return delta, m, v


def _sum_blocks(part, recv, shape, axis, chip, name):
    r, c = shape
    rb = r // 8

    def body(chip_ref, p_ref, r_ref, o_ref):
        o_ref[...] = (p_ref[...] + r_ref[0].astype(F32) + r_ref[1].astype(F32) + r_ref[2].astype(F32))

    if axis == 0:
        own = pl.BlockSpec((rb, c), lambda i, k: (k[0] * 8 + i, 0))
    else:
        own = pl.BlockSpec((rb, c), lambda i, k: (i, k[0]))
    grid_spec = pltpu.PrefetchScalarGridSpec(
        num_scalar_prefetch=1, grid=(8,),
        in_specs=[own, pl.BlockSpec((3, rb, c), lambda i, k: (0, i, 0))],
        out_specs=pl.BlockSpec((rb, c), lambda i, k: (i, 0)))
    return pl.pallas_call(body, name=name, out_shape=SDS((r, c), F32), grid_spec=grid_spec,
                          compiler_params=_params(("parallel",)))(chip, part, recv)


def _adam_big(w, ga, gb, m, v, name):
    r, c = w.shape
    rb = r // 8

    def body(w_ref, ga_ref, gb_ref, m_ref, v_ref, g_ref, d_ref, nm_ref, nv_ref):
        g = ga_ref[...] + gb_ref[...]
        g_ref[...] = g
        d_ref[...], nm_ref[...], nv_ref[...] = _adamw(w_ref[...], g, m_ref[...], v_ref[...])

    spec = pl.BlockSpec((rb, c), lambda i: (i, 0))
    return pl.pallas_call(
        body, name=name, grid=(8,), out_shape=tuple(SDS((r, c), F32) for _ in range(4)),
        in_specs=[spec] * 5, out_specs=(spec,) * 4, compiler_params=_params(("parallel",)),
    )(w, ga, gb, m, v)


def _adam_small(ws, gathered, ms, vs):
    n = len(ws)

    def body(*refs):
        w_refs, g_refs, m_refs, v_refs = refs[:n], refs[n:2 * n], refs[2 * n:3 * n], refs[3 * n:4 * n]
        outs = refs[4 * n:]
        for i in range(n):
            g = g_refs[i][0]
            for d in range(1, N_DEV):
                g = g + g_refs[i][d]
            delta, nm, nv = _adamw(w_refs[i][...], g, m_refs[i][...], v_refs[i][...])
            outs[i][...] = g
            outs[n + i][...] = delta
            outs[2 * n + i][...] = nm
            outs[3 * n + i][...] = nv

    vmem = pl.BlockSpec(memory_space=pltpu.VMEM)
    shapes = [w.shape for w in ws]
    return pl.pallas_call(
        body, name="adam_small", out_shape=tuple(SDS(s, F32) for s in shapes * 4),
        in_specs=[vmem] * (4 * n), out_specs=tuple([vmem] * (4 * n)),
        compiler_params=pltpu.CompilerParams(vmem_limit_bytes=VMEM_LIMIT_BYTES),
    )(*ws, *gathered, *ms, *vs)


def _sum_loss(gathered):
    def body(g_ref, o_ref):
        tot = g_ref[0]
        for d in range(1, N_DEV):
            tot = tot + g_ref[d]
        o_ref[...] = (0.5 / D_MODEL) * jnp.sum(tot, axis=1, keepdims=True)

    vmem = pl.BlockSpec(memory_space=pltpu.VMEM)
    return pl.pallas_call(body, name="sum_loss", out_shape=SDS((1, 1), F32), in_specs=[vmem],
                          out_specs=vmem)(gathered)


def kernel(x, p, ffn1_w_in, ffn1_w_out, ln1_g, ln1_b, mix_w_in, ssm_lambda_re, ssm_lambda_im, ssm_log_dt, ssm_b_re, ssm_b_im, ssm_c_re, ssm_c_im, ssm_d, ssm_glu_w, ssm_glu_b, gmlp_ln_g, gmlp_ln_b, gmlp_w_s, gmlp_b_s, up_a, up_b, mix_w_out, ln2_g, ln2_b, ffn2_w_in, ffn2_w_out, ln3_g, ln3_b, ple_w_proj, ple_w_gate, loss_target, m_ffn1_w_in, m_ffn1_w_out, m_ln1_g, m_ln1_b, m_mix_w_in, m_ssm_lambda_re, m_ssm_lambda_im, m_ssm_log_dt, m_ssm_b_re, m_ssm_b_im, m_ssm_c_re, m_ssm_c_im, m_ssm_d, m_ssm_glu_w, m_ssm_glu_b, m_gmlp_ln_g, m_gmlp_ln_b, m_gmlp_w_s, m_gmlp_b_s, m_up_a, m_up_b, m_mix_w_out, m_ln2_g, m_ln2_b, m_ffn2_w_in, m_ffn2_w_out, m_ln3_g, m_ln3_b, m_ple_w_proj, m_ple_w_gate, v_ffn1_w_in, v_ffn1_w_out, v_ln1_g, v_ln1_b, v_mix_w_in, v_ssm_lambda_re, v_ssm_lambda_im, v_ssm_log_dt, v_ssm_b_re, v_ssm_b_im, v_ssm_c_re, v_ssm_c_im, v_ssm_d, v_ssm_glu_w, v_ssm_glu_b, v_gmlp_ln_g, v_gmlp_ln_b, v_gmlp_w_s, v_gmlp_b_s, v_up_a, v_up_b, v_mix_w_out, v_ln2_g, v_ln2_b, v_ffn2_w_in, v_ffn2_w_out, v_ln3_g, v_ln3_b, v_ple_w_proj, v_ple_w_gate):
    given = dict(locals())
    names = BIG + SMALL
    order = ("ffn1_w_in", "ffn1_w_out", "ln1_g", "ln1_b", "mix_w_in", "ssm_lambda_re", "ssm_lambda_im",
             "ssm_log_dt", "ssm_b_re", "ssm_b_im", "ssm_c_re", "ssm_c_im", "ssm_d", "ssm_glu_w", "ssm_glu_b",
             "gmlp_ln_g", "gmlp_ln_b", "gmlp_w_s", "gmlp_b_s", "up_a", "up_b", "mix_w_out", "ln2_g", "ln2_b",
             "ffn2_w_in", "ffn2_w_out", "ln3_g", "ln3_b", "ple_w_proj", "ple_w_gate")
    assert set(order) == set(names)

    shard = {k: given[k][0] for k in BIG}
    axes = [BIG_AXIS[k] for k in BIG]
    shapes = [shard[k].shape for k in BIG]
    whole = _gather_weights([shard[k].astype(BF16) for k in BIG], axes)
    wb = dict(zip(BIG, whole))
    ws = {k: given[k][0] for k in SMALL}

    loss_rows, grad_x, gb, gs = _local_step(x, given["p"][0], loss_target, wb, ws)

    xi, yi, ci = _place()
    chip = (2 * xi + yi).astype(jnp.int32).reshape(1)
    recv = _scatter_grads([gb[k][1] for k in BIG], shapes, axes)
    sums = [_sum_blocks(gb[k][0], recv[i], shapes[i], axes[i], chip, "sum_" + k) for i, k in enumerate(BIG)]
    other = _swap_with_sibling(sums)
    out = {}
    for i, k in enumerate(BIG):
        out[k] = _adam_big(shard[k], sums[i], other[i], given["m_" + k][0], given["v_" + k][0], "adam_" + k)

    small_in = [gs[k] for k in SMALL] + [loss_rows]
    gathered = _gather_small(small_in)
    res = _adam_small([ws[k].reshape(SMALL_2D[k]) for k in SMALL], gathered[:-1],
                      [given["m_" + k][0].reshape(SMALL_2D[k]) for k in SMALL],
                      [given["v_" + k][0].reshape(SMALL_2D[k]) for k in SMALL])
    ns = len(SMALL)
    for i, k in enumerate(SMALL):
        out[k] = tuple(res[j * ns + i].reshape(given[k][0].shape) for j in range(4))
    loss = _sum_loss(gathered[-1]).reshape(())

    lead = lambda a: a[None]
    return (loss, grad_x, *[lead(out[k][0]) for k in order], *[lead(out[k][1]) for k in order],
            *[lead(out[k][2]) for k in order], *[lead(out[k][3]) for k in order])
```

```python
import math

import jax
import jax.numpy as jnp
from jax import lax
from jax.experimental import pallas as pl
from jax.experimental.pallas import tpu as pltpu

F32 = jnp.float32
BF16 = jnp.bfloat16
MESH = pl.DeviceIdType.MESH
SDS = jax.ShapeDtypeStruct

D_MODEL = 1024
D_FF = 2816
D_SSM = 512
D_GMLP = 512
SSM_GROUPS = 32
SSM_GROUP_CH = 16
SSM_STATE = 64
SSM_LANES = SSM_GROUPS * SSM_STATE
GMLP_HEADS = 8
GMLP_HEAD_DIM = 64
CHUNK = 128
PLE_DIM = 256
LN_EPS = 1e-5
ALPHA = 2.0 ** 0.25

ADAM_LR = 0.001
ADAM_B1 = 0.9
ADAM_B2 = 0.999
ADAM_EPS = 1e-08
ADAM_WD = 0.01
ADAM_STEP = 10
ADAM_C1 = 1.0 - ADAM_B1 ** ADAM_STEP
ADAM_C2 = 1.0 - ADAM_B2 ** ADAM_STEP

N_DEV = 8
VMEM_LIMIT_BYTES = 56 * 1024 * 1024
FFN_COLS = 1408
S5_BLOCKS = 4
S5_BLOCK_IN = D_SSM // S5_BLOCKS
S5_BLOCK_ST = SSM_LANES // S5_BLOCKS
SCAN_LANES = 512
_G0 = math.sqrt(2.0 / math.pi)
_G1 = 0.044715


def _dot(a, b):
    return jnp.dot(a, b, preferred_element_type=F32)


def _dot_nt(a, b):
    return lax.dot_general(a, b, (((1,), (1,)), ((), ())), preferred_element_type=F32)


def _dot_tn(a, b):
    return lax.dot_general(a, b, (((0,), (0,)), ((), ())), preferred_element_type=F32)


def _sigmoid(x):
    return 1.0 / (1.0 + jnp.exp(-x))


def _gelu(x):
    t = jnp.tanh(_G0 * (x + _G1 * x * x * x))
    return 0.5 * x * (1.0 + t)


def _gelu_grad(x):
    t = jnp.tanh(_G0 * (x + _G1 * x * x * x))
    return 0.5 * (1.0 + t) + 0.5 * x * (1.0 - t * t) * _G0 * (1.0 + 3.0 * _G1 * x * x)


def _ln_fwd(r, g, b):
    mu = jnp.mean(r, axis=-1, keepdims=True)
    d = r - mu
    var = jnp.mean(d * d, axis=-1, keepdims=True)
    rstd = lax.rsqrt(var + LN_EPS)
    xh = d * rstd
    return xh * g + b, xh, rstd


def _ln_bwd(dy, xh, rstd, g):
    dxh = dy * g
    m1 = jnp.mean(dxh, axis=-1, keepdims=True)
    m2 = jnp.mean(dxh * xh, axis=-1, keepdims=True)
    return rstd * (dxh - m1 - xh * m2)


def _resident(shape):
    nd = len(shape)
    return pl.BlockSpec(shape, lambda *_: (0,) * nd, pipeline_mode=pl.Buffered(1))


def _fixed(shape):
    nd = len(shape)
    return pl.BlockSpec(shape, lambda *_: (0,) * nd)


def _rows(tm, cols):
    return pl.BlockSpec((tm, cols), lambda i: (i, 0))


def _params(sem):
    return pltpu.CompilerParams(dimension_semantics=sem, vmem_limit_bytes=VMEM_LIMIT_BYTES)


class _Exchange:
    def __init__(self, args, out_shape, sems, start, finish):
        self.args, self.out_shape, self.sems = list(args), list(out_shape), list(sems)
        self.start, self.finish = start, finish
        self.cuts = [(0, len(self.out_shape))]


def _call(body, name, grid, in_specs, out_specs, out_shape, args, scratch=(), sem=None, bg=None, aliases=None):
    aliases = {} if aliases is None else aliases
    if bg is None:
        res = pl.pallas_call(body, name=name, grid=grid, out_shape=tuple(out_shape), in_specs=list(in_specs),
                             out_specs=tuple(out_specs), scratch_shapes=list(scratch),
                             input_output_aliases=aliases, compiler_params=_params(sem))(*args)
        return tuple(res), ()
    n_in, n_out, n_bi, n_bo, n_sc = len(args), len(out_shape), len(bg.args), len(bg.out_shape), len(scratch)

    def wrapped(*refs):
        ins = refs[:n_in]
        b_ins = refs[n_in:n_in + n_bi]
        outs = refs[n_in + n_bi:n_in + n_bi + n_out]
        b_outs = refs[n_in + n_bi + n_out:n_in + n_bi + n_out + n_bo]
        rest = refs[n_in + n_bi + n_out + n_bo:]
        scr, b_sems = rest[:n_sc], rest[n_sc:]
        first = pl.program_id(0) == 0
        last = pl.program_id(0) == grid[0] - 1
        for ax in range(1, len(grid)):
            first = jnp.logical_and(first, pl.program_id(ax) == 0)
            last = jnp.logical_and(last, pl.program_id(ax) == grid[ax] - 1)

        @pl.when(first)
        def _():
            bg.start(b_ins, b_outs, b_sems)

        body(*ins, *outs, *scr)

        @pl.when(last)
        def _():
            bg.finish(b_ins, b_outs, b_sems)

    any_spec = pl.BlockSpec(memory_space=pl.ANY)
    res = pl.pallas_call(
        wrapped, name=name, grid=grid, out_shape=tuple(out_shape) + tuple(bg.out_shape),
        in_specs=list(in_specs) + [any_spec] * n_bi, out_specs=tuple(out_specs) + (any_spec,) * n_bo,
        scratch_shapes=list(scratch) + list(bg.sems), input_output_aliases=aliases,
        compiler_params=_params(tuple("arbitrary" for _ in grid)))(*args, *bg.args)
    return tuple(res[:n_out]), tuple(res[n_out:])


def _run_exchange(ex, name):
    n_i, n_o = len(ex.args), len(ex.out_shape)

    def body(*refs):
        ins, outs, sems = refs[:n_i], refs[n_i:n_i + n_o], refs[n_i + n_o:]
        ex.start(ins, outs, sems)
        ex.finish(ins, outs, sems)

    any_spec = pl.BlockSpec(memory_space=pl.ANY)
    return tuple(pl.pallas_call(body, name=name, out_shape=tuple(ex.out_shape), in_specs=[any_spec] * n_i,
                                out_specs=(any_spec,) * n_o, scratch_shapes=list(ex.sems))(*ex.args))


def _join(exchanges):
    cuts = []
    a = o = q = 0
    for e in exchanges:
        cuts.append((a, a + len(e.args), o, o + len(e.out_shape), q, q + len(e.sems)))
        a, o, q = cuts[-1][1], cuts[-1][3], cuts[-1][5]

    def start(ins, outs, sems):
        for e, (a0, a1, o0, o1, q0, q1) in zip(exchanges, cuts):
            e.start(ins[a0:a1], outs[o0:o1], sems[q0:q1])

    def finish(ins, outs, sems):
        for e, (a0, a1, o0, o1, q0, q1) in zip(exchanges, cuts):
            e.finish(ins[a0:a1], outs[o0:o1], sems[q0:q1])

    joined = _Exchange(sum((e.args for e in exchanges), []), sum((e.out_shape for e in exchanges), []),
                       sum((e.sems for e in exchanges), []), start, finish)
    joined.cuts = [(c[2], c[3]) for c in cuts]
    return joined


def _ffn_fwd(x, w_in, w_out, g, b, tm, name, bg=None):
    t = x.shape[0]
    nch = D_FF // FFN_COLS

    def body(x_ref, win_ref, wout_ref, g_ref, b_ref, xn_ref, xh_ref, rstd_ref, xb_ref, h_ref):
        xv = x_ref[...]
        xb = xv.astype(BF16)
        xb_ref[...] = xb
        f = jnp.zeros((tm, D_MODEL), F32)
        for k in range(nch):
            cg = slice(k * FFN_COLS, (k + 1) * FFN_COLS)
            cu = slice(D_FF + k * FFN_COLS, D_FF + (k + 1) * FFN_COLS)
            hg = _dot(xb, win_ref[:, cg])
            hu = _dot(xb, win_ref[:, cu])
            h_ref[:, cg] = hg.astype(BF16)
            h_ref[:, cu] = hu.astype(BF16)
            a = hg * _sigmoid(hg) * hu
            f = f + _dot(a.astype(BF16), wout_ref[cg, :])
        y, xh, rstd = _ln_fwd(ALPHA * xv + 0.5 * f, g_ref[...], b_ref[...])
        xn_ref[...] = y
        xh_ref[...] = xh
        rstd_ref[...] = rstd

    return _call(
        body, name, (t // tm,),
        [_rows(tm, D_MODEL), _resident((D_MODEL, 2 * D_FF)), _resident((D_FF, D_MODEL)),
         _fixed((1, D_MODEL)), _fixed((1, D_MODEL))],
        (_rows(tm, D_MODEL), _rows(tm, D_MODEL), _rows(tm, 1), _rows(tm, D_MODEL), _rows(tm, 2 * D_FF)),
        (SDS((t, D_MODEL), F32), SDS((t, D_MODEL), F32), SDS((t, 1), F32), SDS((t, D_MODEL), BF16),
         SDS((t, 2 * D_FF), BF16)),
        (x, w_in, w_out, g, b), sem=("parallel",), bg=bg)


def _ffn_bwd(dxn, xh, rstd, h, w_in, w_out, g, tm, name, bg=None):
    t = dxn.shape[0]
    nch = D_FF // FFN_COLS

    def body(dxn_ref, xh_ref, rstd_ref, h_ref, win_ref, wout_ref, g_ref,
             dx_ref, dh_ref, a_ref, df_ref, dg_ref, db_ref):
        @pl.when(pl.program_id(0) == 0)
        def _():
            dg_ref[...] = jnp.zeros_like(dg_ref)
            db_ref[...] = jnp.zeros_like(db_ref)

        dy = dxn_ref[...]
        xhv = xh_ref[...]
        dr = _ln_bwd(dy, xhv, rstd_ref[...], g_ref[...])
        dg_ref[...] += jnp.sum(dy * xhv, axis=0, keepdims=True)
        db_ref[...] += jnp.sum(dy, axis=0, keepdims=True)
        df = (0.5 * dr).astype(BF16)
        df_ref[...] = df
        dx = ALPHA * dr
        for k in range(nch):
            cg = slice(k * FFN_COLS, (k + 1) * FFN_COLS)
            cu = slice(D_FF + k * FFN_COLS, D_FF + (k + 1) * FFN_COLS)
            hg = h_ref[:, cg].astype(F32)
            hu = h_ref[:, cu].astype(F32)
            sg = _sigmoid(hg)
            silu = hg * sg
            a_ref[:, cg] = (silu * hu).astype(BF16)
            da = _dot_nt(df, wout_ref[cg, :])
            dhu = (da * silu).astype(BF16)
            dhg = (da * hu * (sg * (1.0 + hg * (1.0 - sg)))).astype(BF16)
            dh_ref[:, cg] = dhg
            dh_ref[:, cu] = dhu
            dx = dx + _dot_nt(dhg, win_ref[:, cg]) + _dot_nt(dhu, win_ref[:, cu])
        dx_ref[...] = dx

    return _call(
        body, name, (t // tm,),
        [_rows(tm, D_MODEL), _rows(tm, D_MODEL), _rows(tm, 1), _rows(tm, 2 * D_FF),
         _resident((D_MODEL, 2 * D_FF)), _resident((D_FF, D_MODEL)), _fixed((1, D_MODEL))],
        (_rows(tm, D_MODEL), _rows(tm, 2 * D_FF), _rows(tm, D_FF), _rows(tm, D_MODEL),
         _fixed((1, D_MODEL)), _fixed((1, D_MODEL))),
        (SDS((t, D_MODEL), F32), SDS((t, 2 * D_FF), BF16), SDS((t, D_FF), BF16), SDS((t, D_MODEL), BF16),
         SDS((1, D_MODEL), F32), SDS((1, D_MODEL), F32)),
        (dxn, xh, rstd, h, w_in, w_out, g), sem=("arbitrary",), bg=bg)


def _tn_matmul(a, b, name, bm, bn, col_block=0, total_cols=None, prev=None, bg=None):
    t, m = a.shape
    n = b.shape[1]
    total_cols = n if total_cols is None else total_cols
    bk = min(512, t)
    nk = t // bk
    n_in = 2 if prev is None else 4

    def body(*refs):
        a_ref, b_ref = refs[0], refs[1]
        o_ref, ob_ref = refs[n_in], refs[n_in + 1]
        k = pl.program_id(2)

        @pl.when(k == 0)
        def _():
            o_ref[...] = jnp.zeros_like(o_ref)

        o_ref[...] += _dot_tn(a_ref[...], b_ref[...])

        @pl.when(k == nk - 1)
        def _():
            ob_ref[...] = o_ref[...].astype(BF16)

    in_specs = [pl.BlockSpec((bk, bm), lambda i, j, k: (k, i)),
                pl.BlockSpec((bk, bn), lambda i, j, k: (k, j))]
    args = [a, b]
    aliases = {}
    if prev is not None:
        in_specs += [pl.BlockSpec(memory_space=pl.ANY), pl.BlockSpec(memory_space=pl.ANY)]
        args += list(prev)
        aliases = {2: 0, 3: 1}
    out_spec = pl.BlockSpec((bm, bn), lambda i, j, k: (i, j + col_block))
    return _call(body, name, (m // bm, n // bn, nk), in_specs, (out_spec, out_spec),
                 (SDS((m, total_cols), F32), SDS((m, total_cols), BF16)), args,
                 sem=("parallel", "parallel", "arbitrary"), bg=bg, aliases=aliases)


def _mixin_fwd(x1, w, tm, bg=None):
    t = x1.shape[0]

    def body(x_ref, w_ref, xb_ref, za_ref, zuv_ref, gab_ref):
        xb = x_ref[...].astype(BF16)
        xb_ref[...] = xb
        za_ref[...] = _dot(xb, w_ref[:, 0:512]).astype(BF16)
        zuv_ref[...] = _dot(xb, w_ref[:, 512:1536]).astype(BF16)
        gab_ref[...] = _dot(xb, w_ref[:, 1536:3584]).astype(BF16)

    return _call(
        body, "mixin_fwd", (t // tm,),
        [_rows(tm, D_MODEL), _resident((D_MODEL, 3584))],
        (_rows(tm, D_MODEL), _rows(tm, 512), _rows(tm, 1024), _rows(tm, 2048)),
        (SDS((t, D_MODEL), BF16), SDS((t, 512), BF16), SDS((t, 1024), BF16), SDS((t, 2048), BF16)),
        (x1, w), sem=("parallel",), bg=bg)


def _mixin_bwd(dx1a, dza, dzuv, dgab, w, tm, bg=None):
    t = dx1a.shape[0]

    def body(d_ref, dza_ref, dzuv_ref, dgab_ref, w_ref, dx_ref):
        dx_ref[...] = (d_ref[...] + _dot_nt(dza_ref[...], w_ref[:, 0:512])
                       + _dot_nt(dzuv_ref[...], w_ref[:, 512:1536])
                       + _dot_nt(dgab_ref[...], w_ref[:, 1536:3584]))

    return _call(
        body, "mixin_bwd", (t // tm,),
        [_rows(tm, D_MODEL), _rows(tm, 512), _rows(tm, 1024), _rows(tm, 2048), _resident((D_MODEL, 3584))],
        (_rows(tm, D_MODEL),), (SDS((t, D_MODEL), F32),),
        (dx1a, dza, dzuv, dgab, w), sem=("parallel",), bg=bg)


def _scan_fwd(hr_ref, hi_ref, a_ref, ap_ref, carry_ref, seg, cin_ref):
    for lc in range(SSM_LANES // SCAN_LANES):
        ls = slice(lc * SCAN_LANES, (lc + 1) * SCAN_LANES)
        a_r = jnp.broadcast_to(a_ref[0:1, ls], (8, SCAN_LANES))
        a_i = jnp.broadcast_to(a_ref[1:2, ls], (8, SCAN_LANES))

        def step(j, hc, ls=ls, a_r=a_r, a_i=a_i):
            h_r, h_i = hc
            rows = pl.ds(pl.multiple_of(j * 8, 8), 8)
            n_r = a_r * h_r - a_i * h_i + hr_ref[rows, ls]
            n_i = a_r * h_i + a_i * h_r + hi_ref[rows, ls]
            hr_ref[rows, ls] = n_r
            hi_ref[rows, ls] = n_i
            return n_r, n_i

        zero = jnp.zeros((8, SCAN_LANES), F32)
        f_r, f_i = lax.fori_loop(0, seg, step, (zero, zero))
        c_r = carry_ref[0:1, ls]
        c_i = carry_ref[1:2, ls]
        p_r = ap_ref[0:1, ls]
        p_i = ap_ref[1:2, ls]
        rows_r, rows_i = [], []
        for s in range(8):
            rows_r.append(c_r)
            rows_i.append(c_i)
            c_r, c_i = (f_r[s:s + 1] + p_r * c_r - p_i * c_i,
                        f_i[s:s + 1] + p_r * c_i + p_i * c_r)
        carry_ref[0:1, ls] = c_r
        carry_ref[1:2, ls] = c_i
        cin_r = jnp.concatenate(rows_r, axis=0)
        cin_i = jnp.concatenate(rows_i, axis=0)
        if cin_ref is not None:
            cin_ref[0, :, ls] = cin_r
            cin_ref[1, :, ls] = cin_i

        def fix(j, cc, ls=ls, a_r=a_r, a_i=a_i):
            c_r, c_i = cc
            c_r, c_i = a_r * c_r - a_i * c_i, a_r * c_i + a_i * c_r
            rows = pl.ds(pl.multiple_of(j * 8, 8), 8)
            hr_ref[rows, ls] = hr_ref[rows, ls] + c_r
            hi_ref[rows, ls] = hi_ref[rows, ls] + c_i
            return c_r, c_i

        lax.fori_loop(0, seg, fix, (cin_r, cin_i))


def _scan_bwd(gr_ref, gi_ref, hr_ref, hi_ref, cin_ref, a_ref, ap_ref, rcarry_ref, da_ref, seg):
    for lc in range(SSM_LANES // SCAN_LANES):
        ls = slice(lc * SCAN_LANES, (lc + 1) * SCAN_LANES)
        a_r = jnp.broadcast_to(a_ref[0:1, ls], (8, SCAN_LANES))
        a_i = jnp.broadcast_to(a_ref[1:2, ls], (8, SCAN_LANES))

        def step(t, gc, ls=ls, a_r=a_r, a_i=a_i):
            g_r, g_i = gc
            rows = pl.ds(pl.multiple_of((seg - 1 - t) * 8, 8), 8)
            n_r = gr_ref[rows, ls] + a_r * g_r + a_i * g_i
            n_i = gi_ref[rows, ls] + a_r * g_i - a_i * g_r
            gr_ref[rows, ls] = n_r
            gi_ref[rows, ls] = n_i
            return n_r, n_i

        zero = jnp.zeros((8, SCAN_LANES), F32)
        f_r, f_i = lax.fori_loop(0, seg, step, (zero, zero))
        c_r = rcarry_ref[0:1, ls]
        c_i = rcarry_ref[1:2, ls]
        p_r = ap_ref[0:1, ls]
        p_i = ap_ref[1:2, ls]
        rows_r, rows_i = [None] * 8, [None] * 8
        for s in range(7, -1, -1):
            rows_r[s] = c_r
            rows_i[s] = c_i
            c_r, c_i = (f_r[s:s + 1] + p_r * c_r + p_i * c_i,
                        f_i[s:s + 1] + p_r * c_i - p_i * c_r)
        rcarry_ref[0:1, ls] = c_r
        rcarry_ref[1:2, ls] = c_i
        cin_r = jnp.concatenate(rows_r, axis=0)
        cin_i = jnp.concatenate(rows_i, axis=0)

        def fix_row(j_rows, hp_r, hp_i, cc, ls=ls, a_r=a_r, a_i=a_i):
            c_r, c_i, acc_r, acc_i = cc
            c_r, c_i = a_r * c_r + a_i * c_i, a_r * c_i - a_i * c_r
            g_r = gr_ref[j_rows, ls] + c_r
            g_i = gi_ref[j_rows, ls] + c_i
            gr_ref[j_rows, ls] = g_r
            gi_ref[j_rows, ls] = g_i
            acc_r = acc_r + g_r * hp_r + g_i * hp_i
            acc_i = acc_i + g_i * hp_r - g_r * hp_i
            return c_r, c_i, acc_r, acc_i

        def fix(t, cc, ls=ls, fix_row=fix_row):
            j = seg - 1 - t
            rows = pl.ds(pl.multiple_of(j * 8, 8), 8)
            prev = pl.ds(pl.multiple_of((j - 1) * 8, 8), 8)
            return fix_row(rows, hr_ref[prev, ls], hi_ref[prev, ls], cc)

        cc = lax.fori_loop(0, seg - 1, fix, (cin_r, cin_i, zero, zero))
        _, _, acc_r, acc_i = fix_row(pl.ds(0, 8), cin_ref[0, :, ls], cin_ref[1, :, ls], cc)
        da_ref[0, :, ls] += acc_r
        da_ref[1, :, ls] += acc_i


def _s5_fwd(za, sp, bsz, seq, tb, bg=None):
    nb = seq // tb
    seg = tb // 8
    t = bsz * seq

    def body(za_ref, perm_ref, permt_ref, mre_ref, mim_ref, nre_ref, nim_ref, a_ref, ap_ref,
             dsk_ref, gw_ref, gb_ref, out_ref, y2_ref, car_ref, hr_ref, hi_ref, carry_ref):
        @pl.when(pl.program_id(1) == 0)
        def _():
            carry_ref[...] = jnp.zeros_like(carry_ref)

        car_ref[0] = carry_ref[...]
        up = _dot(perm_ref[...], za_ref[...])
        upb = up.astype(BF16)
        for bb in range(S5_BLOCKS):
            ub = upb[:, bb * S5_BLOCK_IN:(bb + 1) * S5_BLOCK_IN]
            st = slice(bb * S5_BLOCK_ST, (bb + 1) * S5_BLOCK_ST)
            hr_ref[:, st] = _dot(ub, mre_ref[bb])
            hi_ref[:, st] = _dot(ub, mim_ref[bb])
        _scan_fwd(hr_ref, hi_ref, a_ref, ap_ref, carry_ref, seg, None)
        ys = []
        for bb in range(S5_BLOCKS):
            st = slice(bb * S5_BLOCK_ST, (bb + 1) * S5_BLOCK_ST)
            ys.append(_dot(hr_ref[:, st].astype(BF16), nre_ref[bb])
                      - _dot(hi_ref[:, st].astype(BF16), nim_ref[bb]))
        y2 = jnp.concatenate(ys, axis=1) + dsk_ref[...] * up
        y2_ref[...] = y2
        y3 = _gelu(y2)
        gl = _dot(y3.astype(BF16), gw_ref[...]) + gb_ref[...]
        oa = y3 * _sigmoid(gl)
        out_ref[...] = _dot(permt_ref[...], oa.astype(BF16)).astype(BF16)

    blk = pl.BlockSpec((tb, D_SSM), lambda b, j: (b * nb + j, 0))
    m_shape = (S5_BLOCKS, S5_BLOCK_IN, S5_BLOCK_ST)
    n_shape = (S5_BLOCKS, S5_BLOCK_ST, S5_BLOCK_IN)
    return _call(
        body, "s5_fwd", (bsz, nb),
        [blk, _fixed((tb, tb)), _fixed((tb, tb)), _fixed(m_shape), _fixed(m_shape), _fixed(n_shape),
         _fixed(n_shape), _fixed((2, SSM_LANES)), _fixed((2, SSM_LANES)), _fixed((1, D_SSM)),
         _fixed((D_SSM, D_SSM)), _fixed((1, D_SSM))],
        (blk, blk, pl.BlockSpec((1, 2, SSM_LANES), lambda b, j: (b * nb + j, 0, 0))),
        (SDS((t, D_SSM), BF16), SDS((t, D_SSM), F32), SDS((bsz * nb, 2, SSM_LANES), F32)),
        (za, sp["perm"], sp["permt"], sp["mre"], sp["mim"], sp["nre"], sp["nim"], sp["a"], sp["ap"],
         sp["dskip"], sp["glu_w"], sp["glu_b"]),
        scratch=[pltpu.VMEM((tb, SSM_LANES), F32), pltpu.VMEM((tb, SSM_LANES), F32),
                 pltpu.VMEM((2, SSM_LANES), F32)],
        sem=("arbitrary", "arbitrary"), bg=bg)


def _s5_bwd(za, y2p, doa, carries, sp, bsz, seq, tb, bg=None):
    nb = seq // tb
    seg = tb // 8
    t = bsz * seq

    def body(za_ref, y2_ref, doa_ref, car_ref, perm_ref, permt_ref, mre_ref, mim_ref, mtre_ref, mtim_ref,
             nre_ref, nim_ref, ntre_ref, ntim_ref, a_ref, ap_ref, dsk_ref, gw_ref, gwt_ref, gb_ref,
             dza_ref, dmr_ref, dmi_ref, dnr_ref, dni_ref, da_ref, ddsk_ref, dgw_ref, dgb_ref,
             hr_ref, hi_ref, gr_ref, gi_ref, cin_ref, carry_ref, rcarry_ref):
        first = jnp.logical_and(pl.program_id(0) == 0, pl.program_id(1) == 0)

        @pl.when(first)
        def _():
            for r in (dmr_ref, dmi_ref, dnr_ref, dni_ref, da_ref, ddsk_ref, dgw_ref, dgb_ref):
                r[...] = jnp.zeros_like(r)

        @pl.when(pl.program_id(1) == 0)
        def _():
            rcarry_ref[...] = jnp.zeros_like(rcarry_ref)

        carry_ref[...] = car_ref[0]
        perm = perm_ref[...]
        up = _dot(perm, za_ref[...])
        upb = up.astype(BF16)
        for bb in range(S5_BLOCKS):
            ub = upb[:, bb * S5_BLOCK_IN:(bb + 1) * S5_BLOCK_IN]
            st = slice(bb * S5_BLOCK_ST, (bb + 1) * S5_BLOCK_ST)
            hr_ref[:, st] = _dot(ub, mre_ref[bb])
            hi_ref[:, st] = _dot(ub, mim_ref[bb])
        _scan_fwd(hr_ref, hi_ref, a_ref, ap_ref, carry_ref, seg, cin_ref)

        y2 = y2_ref[...]
        y3 = _gelu(y2)
        y3b = y3.astype(BF16)
        sg = _sigmoid(_dot(y3b, gw_ref[...]) + gb_ref[...])
        d0 = doa_ref[...]
        d_hi = d0.astype(BF16)
        d1 = d0 - d_hi.astype(F32)
        d_mid = d1.astype(BF16)
        d_lo = (d1 - d_mid.astype(F32)).astype(BF16)
        doap = _dot(perm, d_hi) + _dot(perm, d_mid) + _dot(perm, d_lo)
        dgl = doap * y3 * sg * (1.0 - sg)
        dglb = dgl.astype(BF16)
        dy3 = doap * sg + _dot(dglb, gwt_ref[...])
        dgw_ref[...] += _dot_tn(y3b, dglb)
        dgb_ref[...] += jnp.sum(dgl, axis=0, keepdims=True)
        dy2 = dy3 * _gelu_grad(y2)
        ddsk_ref[...] += jnp.sum(dy2 * up, axis=0, keepdims=True)
        dyb = dy2.astype(BF16)
        for bb in range(S5_BLOCKS):
            dyc = dyb[:, bb * S5_BLOCK_IN:(bb + 1) * S5_BLOCK_IN]
            st = slice(bb * S5_BLOCK_ST, (bb + 1) * S5_BLOCK_ST)
            gr_ref[:, st] = _dot(dyc, ntre_ref[bb])
            gi_ref[:, st] = -_dot(dyc, ntim_ref[bb])
            dnr_ref[bb] += _dot_tn(hr_ref[:, st].astype(BF16), dyc)
            dni_ref[bb] += -_dot_tn(hi_ref[:, st].astype(BF16), dyc)
        _scan_bwd(gr_ref, gi_ref, hr_ref, hi_ref, cin_ref, a_ref, ap_ref, rcarry_ref, da_ref, seg)
        dus = []
        for bb in range(S5_BLOCKS):
            st = slice(bb * S5_BLOCK_ST, (bb + 1) * S5_BLOCK_ST)
            grb = gr_ref[:, st].astype(BF16)
            gib = gi_ref[:, st].astype(BF16)
            dus.append(_dot(grb, mtre_ref[bb]) + _dot(gib, mtim_ref[bb]))
            ub = upb[:, bb * S5_BLOCK_IN:(bb + 1) * S5_BLOCK_IN]
            dmr_ref[bb] += _dot_tn(ub, grb)
            dmi_ref[bb] += _dot_tn(ub, gib)
        du = jnp.concatenate(dus, axis=1) + dy2 * dsk_ref[...]
        dza_ref[...] = _dot(permt_ref[...], du.astype(BF16)).astype(BF16)

    def rev(b, j):
        return (b * nb + (nb - 1 - j), 0)

    blk = pl.BlockSpec((tb, D_SSM), rev)
    m_shape = (S5_BLOCKS, S5_BLOCK_IN, S5_BLOCK_ST)
    n_shape = (S5_BLOCKS, S5_BLOCK_ST, S5_BLOCK_IN)
    return _call(
        body, "s5_bwd", (bsz, nb),
        [blk, blk, blk, pl.BlockSpec((1, 2, SSM_LANES), lambda b, j: (b * nb + (nb - 1 - j), 0, 0)),
         _fixed((tb, tb)), _fixed((tb, tb)), _fixed(m_shape), _fixed(m_shape), _fixed(n_shape), _fixed(n_shape),
         _fixed(n_shape), _fixed(n_shape), _fixed(m_shape), _fixed(m_shape),
         _fixed((2, SSM_LANES)), _fixed((2, SSM_LANES)), _fixed((1, D_SSM)),
         _fixed((D_SSM, D_SSM)), _fixed((D_SSM, D_SSM)), _fixed((1, D_SSM))],
        (blk, _fixed(m_shape), _fixed(m_shape), _fixed(n_shape), _fixed(n_shape),
         _fixed((2, 8, SSM_LANES)), _fixed((1, D_SSM)), _fixed((D_SSM, D_SSM)), _fixed((1, D_SSM))),
        (SDS((t, D_SSM), BF16), SDS(m_shape, F32), SDS(m_shape, F32), SDS(n_shape, F32), SDS(n_shape, F32),
         SDS((2, 8, SSM_LANES), F32), SDS((1, D_SSM), F32), SDS((D_SSM, D_SSM), F32), SDS((1, D_SSM), F32)),
        (za, y2p, doa, carries, sp["perm"], sp["permt"], sp["mre"], sp["mim"], sp["mtre"], sp["mtim"],
         sp["nre"], sp["nim"], sp["ntre"], sp["ntim"], sp["a"], sp["ap"], sp["dskip"], sp["glu_w"],
         sp["glu_wt"], sp["glu_b"]),
        scratch=[pltpu.VMEM((tb, SSM_LANES), F32), pltpu.VMEM((tb, SSM_LANES), F32),
                 pltpu.VMEM((tb, SSM_LANES), F32), pltpu.VMEM((tb, SSM_LANES), F32),
                 pltpu.VMEM((2, 8, SSM_LANES), F32), pltpu.VMEM((2, SSM_LANES), F32),
                 pltpu.VMEM((2, SSM_LANES), F32)],
        sem=("arbitrary", "arbitrary"), bg=bg)


def _gmlp_spatial(ws_ref, vb):
    lane = lax.broadcasted_iota(jnp.int32, (CHUNK, 128), 1)
    parts = []
    for j in range(GMLP_HEADS // 2):
        vp = vb[:, 128 * j:128 * (j + 1)]
        parts.append(jnp.where(lane < GMLP_HEAD_DIM, _dot(ws_ref[2 * j], vp), _dot(ws_ref[2 * j + 1], vp)))
    return jnp.concatenate(parts, axis=1)


def _gmlp_fwd(zuv, ln_g, ln_b, wsm, bias, bg=None):
    t = zuv.shape[0]

    def body(z_ref, g_ref, b_ref, ws_ref, bias_ref, out_ref):
        u = _gelu(z_ref[:, 0:D_GMLP].astype(F32))
        v0 = _gelu(z_ref[:, D_GMLP:2 * D_GMLP].astype(F32))
        v, _, _ = _ln_fwd(v0, g_ref[...], b_ref[...])
        s = _gmlp_spatial(ws_ref, v.astype(BF16)) + bias_ref[...]
        out_ref[...] = (u * s).astype(BF16)

    return _call(
        body, "gmlp_fwd", (t // CHUNK,),
        [_rows(CHUNK, 2 * D_GMLP), _fixed((1, D_GMLP)), _fixed((1, D_GMLP)),
         _fixed((GMLP_HEADS, CHUNK, CHUNK)), _fixed((CHUNK, D_GMLP))],
        (_rows(CHUNK, D_GMLP),), (SDS((t, D_GMLP), BF16),),
        (zuv, ln_g, ln_b, wsm, bias), sem=("parallel",), bg=bg)


def _gmlp_bwd(zuv, dgm, ln_g, ln_b, wsm, wsmt, bias, bg=None):
    t = zuv.shape[0]

    def body(z_ref, d_ref, g_ref, b_ref, ws_ref, wst_ref, bias_ref,
             dz_ref, dws_ref, dbias_ref, dg_ref, db_ref):
        @pl.when(pl.program_id(0) == 0)
        def _():
            for r in (dws_ref, dbias_ref, dg_ref, db_ref):
                r[...] = jnp.zeros_like(r)

        zu = z_ref[:, 0:D_GMLP].astype(F32)
        zv = z_ref[:, D_GMLP:2 * D_GMLP].astype(F32)
        u = _gelu(zu)
        v0 = _gelu(zv)
        gam = g_ref[...]
        v, vhat, rstd = _ln_fwd(v0, gam, b_ref[...])
        vb = v.astype(BF16)
        s = _gmlp_spatial(ws_ref, vb) + bias_ref[...]
        d = d_ref[...]
        dz_ref[:, 0:D_GMLP] = (d * s * _gelu_grad(zu)).astype(BF16)
        ds = d * u
        dbias_ref[...] += ds
        dsb = ds.astype(BF16)
        lane = lax.broadcasted_iota(jnp.int32, (CHUNK, 128), 1)
        tril = (lax.broadcasted_iota(jnp.int32, (CHUNK, CHUNK), 0)
                >= lax.broadcasted_iota(jnp.int32, (CHUNK, CHUNK), 1))
        zero_b = jnp.zeros((CHUNK, 128), BF16)
        parts = []
        for j in range(GMLP_HEADS // 2):
            dsp = dsb[:, 128 * j:128 * (j + 1)]
            vp = vb[:, 128 * j:128 * (j + 1)]
            parts.append(jnp.where(lane < GMLP_HEAD_DIM, _dot(wst_ref[2 * j], dsp),
                                   _dot(wst_ref[2 * j + 1], dsp)))
            lo = jnp.where(lane < GMLP_HEAD_DIM, dsp, zero_b)
            hi = jnp.where(lane < GMLP_HEAD_DIM, zero_b, dsp)
            dws_ref[2 * j] += jnp.where(tril, _dot_nt(lo, vp), 0.0)
            dws_ref[2 * j + 1] += jnp.where(tril, _dot_nt(hi, vp), 0.0)
        dv = jnp.concatenate(parts, axis=1)
        dg_ref[...] += jnp.sum(dv * vhat, axis=0, keepdims=True)
        db_ref[...] += jnp.sum(dv, axis=0, keepdims=True)
        dz_ref[:, D_GMLP:2 * D_GMLP] = (_ln_bwd(dv, vhat, rstd, gam) * _gelu_grad(zv)).astype(BF16)

    return _call(
        body, "gmlp_bwd", (t // CHUNK,),
        [_rows(CHUNK, 2 * D_GMLP), _rows(CHUNK, D_GMLP), _fixed((1, D_GMLP)), _fixed((1, D_GMLP)),
         _fixed((GMLP_HEADS, CHUNK, CHUNK)), _fixed((GMLP_HEADS, CHUNK, CHUNK)), _fixed((CHUNK, D_GMLP))],
        (_rows(CHUNK, 2 * D_GMLP), _fixed((GMLP_HEADS, CHUNK, CHUNK)), _fixed((CHUNK, D_GMLP)),
         _fixed((1, D_GMLP)), _fixed((1, D_GMLP))),
        (SDS((t, 2 * D_GMLP), BF16), SDS((GMLP_HEADS, CHUNK, CHUNK), F32), SDS((CHUNK, D_GMLP), F32),
         SDS((1, D_GMLP), F32), SDS((1, D_GMLP), F32)),
        (zuv, dgm, ln_g, ln_b, wsm, wsmt, bias), sem=("arbitrary",), bg=bg)


def _mixout_fwd(x1, s5o, gm, gab, ua, ub, wmo, g, b, tm, bg=None):
    t = x1.shape[0]

    def body(x_ref, s_ref, m_ref, gab_ref, ua_ref, ub_ref, wmo_ref, g_ref, b_ref,
             xn_ref, xh_ref, rstd_ref, xb_ref):
        ya = _dot(s_ref[...], ua_ref[...])
        yb = _dot(m_ref[...], ub_ref[...])
        mix = (_sigmoid(gab_ref[:, 0:D_MODEL].astype(F32)) * ya
               + _sigmoid(gab_ref[:, D_MODEL:2 * D_MODEL].astype(F32)) * yb)
        r = ALPHA * x_ref[...] + _dot(mix.astype(BF16), wmo_ref[...])
        y, xh, rstd = _ln_fwd(r, g_ref[...], b_ref[...])
        xn_ref[...] = y
        xh_ref[...] = xh
        rstd_ref[...] = rstd
        xb_ref[...] = y.astype(BF16)

    return _call(
        body, "mixout_fwd", (t // tm,),
        [_rows(tm, D_MODEL), _rows(tm, D_SSM), _rows(tm, D_GMLP), _rows(tm, 2 * D_MODEL),
         _resident((D_SSM, D_MODEL)), _resident((D_GMLP, D_MODEL)), _resident((D_MODEL, D_MODEL)),
         _fixed((1, D_MODEL)), _fixed((1, D_MODEL))],
        (_rows(tm, D_MODEL), _rows(tm, D_MODEL), _rows(tm, 1), _rows(tm, D_MODEL)),
        (SDS((t, D_MODEL), F32), SDS((t, D_MODEL), F32), SDS((t, 1), F32), SDS((t, D_MODEL), BF16)),
        (x1, s5o, gm, gab, ua, ub, wmo, g, b), sem=("parallel",), bg=bg)


def _mixout_bwd(dx2, xh, rstd, s5o, gm, gab, ua, ub, wmo, g, tm, bg=None):
    t = dx2.shape[0]

    def body(d_ref, xh_ref, rstd_ref, s_ref, m_ref, gab_ref, ua_ref, ub_ref, wmo_ref, g_ref,
             dx1_ref, dmx_ref, mb_ref, dya_ref, dyb_ref, ds5_ref, dgm_ref, dgab_ref, dg_ref, db_ref):
        @pl.when(pl.program_id(0) == 0)
        def _():
            dg_ref[...] = jnp.zeros_like(dg_ref)
            db_ref[...] = jnp.zeros_like(db_ref)

        dy = d_ref[...]
        xhv = xh_ref[...]
        dr = _ln_bwd(dy, xhv, rstd_ref[...], g_ref[...])
        dg_ref[...] += jnp.sum(dy * xhv, axis=0, keepdims=True)
        db_ref[...] += jnp.sum(dy, axis=0, keepdims=True)
        dx1_ref[...] = ALPHA * dr
        drb = dr.astype(BF16)
        dmx_ref[...] = drb
        dm = _dot_nt(drb, wmo_ref[...])
        ya = _dot(s_ref[...], ua_ref[...])
        yb = _dot(m_ref[...], ub_ref[...])
        sa = _sigmoid(gab_ref[:, 0:D_MODEL].astype(F32))
        sb = _sigmoid(gab_ref[:, D_MODEL:2 * D_MODEL].astype(F32))
        mb_ref[...] = (sa * ya + sb * yb).astype(BF16)
        dya = (dm * sa).astype(BF16)
        dyb = (dm * sb).astype(BF16)
        dya_ref[...] = dya
        dyb_ref[...] = dyb
        dgab_ref[:, 0:D_MODEL] = (dm * ya * sa * (1.0 - sa)).astype(BF16)
        dgab_ref[:, D_MODEL:2 * D_MODEL] = (dm * yb * sb * (1.0 - sb)).astype(BF16)
        ds5_ref[...] = _dot_nt(dya, ua_ref[...])
        dgm_ref[...] = _dot_nt(dyb, ub_ref[...])

    return _call(
        body, "mixout_bwd", (t // tm,),
        [_rows(tm, D_MODEL), _rows(tm, D_MODEL), _rows(tm, 1), _rows(tm, D_SSM), _rows(tm, D_GMLP),
         _rows(tm, 2 * D_MODEL), _resident((D_SSM, D_MODEL)), _resident((D_GMLP, D_MODEL)),
         _resident((D_MODEL, D_MODEL)), _fixed((1, D_MODEL))],
        (_rows(tm, D_MODEL), _rows(tm, D_MODEL), _rows(tm, D_MODEL), _rows(tm, D_MODEL),
         _rows(tm, D_MODEL), _rows(tm, D_SSM), _rows(tm, D_GMLP), _rows(tm, 2 * D_MODEL),
         _fixed((1, D_MODEL)), _fixed((1, D_MODEL))),
        (SDS((t, D_MODEL), F32), SDS((t, D_MODEL), BF16), SDS((t, D_MODEL), BF16),
         SDS((t, D_MODEL), BF16), SDS((t, D_MODEL), BF16), SDS((t, D_SSM), F32),
         SDS((t, D_GMLP), F32), SDS((t, 2 * D_MODEL), BF16),
         SDS((1, D_MODEL), F32), SDS((1, D_MODEL), F32)),
        (dx2, xh, rstd, s5o, gm, gab, ua, ub, wmo, g), sem=("arbitrary",), bg=bg)


def _ple_loss(x3, p, tgt, wpg, wpp, tm, bg=None):
    t = x3.shape[0]

    def body(x_ref, p_ref, t_ref, wpg_ref, wpp_ref, dx_ref, xb_ref, pb_ref, dq_ref, de_ref, loss_ref):
        @pl.when(pl.program_id(0) == 0)
        def _():
            loss_ref[...] = jnp.zeros_like(loss_ref)

        x3v = x_ref[...]
        xb = x3v.astype(BF16)
        pb = p_ref[...].astype(BF16)
        xb_ref[...] = xb
        pb_ref[...] = pb
        s = _sigmoid(_dot(xb, wpg_ref[...]))
        e = _dot(pb, wpp_ref[...])
        diff = x3v + s * e - t_ref[...]
        loss_ref[...] += jnp.sum(diff * diff, axis=0, keepdims=True)
        dout = diff * (1.0 / D_MODEL)
        de_ref[...] = (dout * s).astype(BF16)
        dq = (dout * e * s * (1.0 - s)).astype(BF16)
        dq_ref[...] = dq
        dx_ref[...] = dout + _dot_nt(dq, wpg_ref[...])

    return _call(
        body, "ple_loss", (t // tm,),
        [_rows(tm, D_MODEL), _rows(tm, PLE_DIM), _rows(tm, D_MODEL),
         _resident((D_MODEL, D_MODEL)), _resident((PLE_DIM, D_MODEL))],
        (_rows(tm, D_MODEL), _rows(tm, D_MODEL), _rows(tm, PLE_DIM), _rows(tm, D_MODEL),
         _rows(tm, D_MODEL), _fixed((1, D_MODEL))),
        (SDS((t, D_MODEL), F32), SDS((t, D_MODEL), BF16), SDS((t, PLE_DIM), BF16),
         SDS((t, D_MODEL), BF16), SDS((t, D_MODEL), BF16), SDS((1, D_MODEL), F32)),
        (x3, p, tgt, wpg, wpp), sem=("arbitrary",), bg=bg)


def _s5_discretise(lre, lim, log_dt, bre, bim):
    dt = jnp.exp(log_dt)[:, None]
    mag = jnp.exp(lre * dt)
    abr = mag * jnp.cos(lim * dt)
    abi = mag * jnp.sin(lim * dt)
    nr = abr - 1.0
    ni = abi
    den = lre * lre + lim * lim
    cr = ((nr * lre + ni * lim) / den)[..., None]
    ci = ((ni * lre - nr * lim) / den)[..., None]
    return abr, abi, cr * bre - ci * bim, cr * bim + ci * bre


def _block_diag_in(bb):
    v = bb.reshape(S5_BLOCKS, 8, SSM_STATE, SSM_GROUP_CH).transpose(0, 1, 3, 2)
    return jnp.einsum("bgip,gh->bgihp", v, jnp.eye(8, dtype=bb.dtype)).reshape(
        S5_BLOCKS, S5_BLOCK_IN, S5_BLOCK_ST)


def _block_diag_in_t(dm):
    v = dm.reshape(S5_BLOCKS, 8, SSM_GROUP_CH, 8, SSM_STATE)
    d = jnp.einsum("bgihp,gh->bgip", v, jnp.eye(8, dtype=dm.dtype))
    return d.transpose(0, 1, 3, 2).reshape(SSM_GROUPS, SSM_STATE, SSM_GROUP_CH)


def _block_diag_out(cc):
    v = cc.reshape(S5_BLOCKS, 8, SSM_GROUP_CH, SSM_STATE)
    return jnp.einsum("bgip,gh->bgphi", v, jnp.eye(8, dtype=cc.dtype)).reshape(
        S5_BLOCKS, S5_BLOCK_ST, S5_BLOCK_IN)


def _block_diag_out_t(dn):
    v = dn.reshape(S5_BLOCKS, 8, SSM_STATE, 8, SSM_GROUP_CH)
    d = jnp.einsum("bgphi,gh->bgip", v, jnp.eye(8, dtype=dn.dtype))
    return d.reshape(SSM_GROUPS, SSM_GROUP_CH, SSM_STATE)


def _s5_setup(lre, lim, log_dt, bre, bim, cre, cim, d_skip, glu_w, glu_b, tb):
    seg = tb // 8
    abr, abi, bbr, bbi = _s5_discretise(lre, lim, log_dt, bre, bim)
    pr, pi = abr, abi
    for _ in range(int(math.log2(seg))):
        pr, pi = pr * pr - pi * pi, 2.0 * pr * pi
    rows = jnp.arange(tb)
    src = (rows % 8) * seg + rows // 8
    perm = (src[:, None] == jnp.arange(tb)[None, :]).astype(BF16)
    mre = _block_diag_in(bbr)
    mim = _block_diag_in(bbi)
    nre = _block_diag_out(cre)
    nim = _block_diag_out(cim)
    return {
        "perm": perm, "permt": perm.T,
        "mre": mre.astype(BF16), "mim": mim.astype(BF16),
        "mtre": mre.transpose(0, 2, 1).astype(BF16), "mtim": mim.transpose(0, 2, 1).astype(BF16),
        "nre": nre.astype(BF16), "nim": nim.astype(BF16),
        "ntre": nre.transpose(0, 2, 1).astype(BF16), "ntim": nim.transpose(0, 2, 1).astype(BF16),
        "a": jnp.stack([abr.reshape(-1), abi.reshape(-1)]),
        "ap": jnp.stack([pr.reshape(-1), pi.reshape(-1)]),
        "dskip": d_skip.reshape(1, D_SSM), "glu_w": glu_w, "glu_wt": glu_w.T,
        "glu_b": glu_b.reshape(1, D_SSM),
    }


BIG = ("ffn1_w_in", "ffn1_w_out", "mix_w_in", "ssm_glu_w", "up_a", "up_b", "mix_w_out",
       "ffn2_w_in", "ffn2_w_out", "ple_w_proj", "ple_w_gate")
BIG_AXIS = {"ffn1_w_in": 1, "ffn1_w_out": 0, "mix_w_in": 1, "ssm_glu_w": 0, "up_a": 1, "up_b": 1,
            "mix_w_out": 0, "ffn2_w_in": 1, "ffn2_w_out": 0, "ple_w_proj": 1, "ple_w_gate": 0}
SMALL = ("ln1_g", "ln1_b", "ssm_lambda_re", "ssm_lambda_im", "ssm_log_dt", "ssm_b_re", "ssm_b_im",
         "ssm_c_re", "ssm_c_im", "ssm_d", "ssm_glu_b", "gmlp_ln_g", "gmlp_ln_b", "gmlp_w_s",
         "gmlp_b_s", "ln2_g", "ln2_b", "ln3_g", "ln3_b")
SMALL_2D = {"ln1_g": (1, 1024), "ln1_b": (1, 1024), "ssm_lambda_re": (32, 64), "ssm_lambda_im": (32, 64),
            "ssm_log_dt": (1, 32), "ssm_b_re": (32, 1024), "ssm_b_im": (32, 1024), "ssm_c_re": (32, 1024),
            "ssm_c_im": (32, 1024), "ssm_d": (1, 512), "ssm_glu_b": (1, 512), "gmlp_ln_g": (1, 512),
            "gmlp_ln_b": (1, 512), "gmlp_w_s": (1024, 128), "gmlp_b_s": (8, 128), "ln2_g": (1, 1024),
            "ln2_b": (1, 1024), "ln3_g": (1, 1024), "ln3_b": (1, 1024)}


def _place():
    return lax.axis_index("x"), lax.axis_index("y"), lax.axis_index("c")


def _other_chips(x, y):
    return [(1 - x, y), (x, 1 - y), (1 - x, 1 - y)]


def _window(ref, shard_shape, axis, chip, half):
    r, c = shard_shape
    hr = r // 2
    if axis == 0:
        if half is None:
            return ref.at[pl.ds(chip * r, r), :]
        return ref.at[pl.ds(chip * r + half * hr, hr), :]
    if half is None:
        return ref.at[:, pl.ds(chip * c, c)]
    return ref.at[pl.ds(half * hr, hr), pl.ds(chip * c, c)]


def _gather_weights(shards, axes):
    n = len(shards)
    shapes = [s.shape for s in shards]
    full = [(4 * r, c) if ax == 0 else (r, 4 * c) for (r, c), ax in zip(shapes, axes)]

    def remote(sems, i, k, src, dst, to):
        return pltpu.make_async_remote_copy(src_ref=src, dst_ref=dst, send_sem=sems[0].at[6 * i + k],
                                            recv_sem=sems[1].at[6 * i + k], device_id=to, device_id_type=MESH)

    def own_copies(ins, outs, sems):
        x, y, c = _place()
        me = 2 * x + y
        cps = []
        for i in range(n):
            hr = shapes[i][0] // 2
            mine = ins[i].at[pl.ds(c * hr, hr), :]
            for j, (cx, cy) in enumerate(_other_chips(x, y)):
                cps.append(remote(sems, i, j, mine, _window(outs[i], shapes[i], axes[i], me, c), (cx, cy, c)))
        local = [pltpu.make_async_copy(ins[i], _window(outs[i], shapes[i], axes[i], me, None), sems[2].at[i])
                 for i in range(n)]
        return cps, local

    def start(ins, outs, sems):
        cps, local = own_copies(ins, outs, sems)
        for cp in local + cps:
            cp.start()

    def finish(ins, outs, sems):
        x, y, c = _place()
        sibling = (x, y, 1 - c)
        passed = []
        for j, (cx, cy) in enumerate(_other_chips(x, y)):
            for i in range(n):
                w = _window(outs[i], shapes[i], axes[i], 2 * cx + cy, c)
                remote(sems, i, j, w, w, (cx, cy, c)).wait_recv()
                cp = remote(sems, i, 3 + j, w, w, sibling)
                cp.start()
                passed.append(cp)
        for j, (cx, cy) in enumerate(_other_chips(x, y)):
            for i in range(n):
                w = _window(outs[i], shapes[i], axes[i], 2 * cx + cy, 1 - c)
                remote(sems, i, 3 + j, w, w, sibling).wait_recv()
        cps, local = own_copies(ins, outs, sems)
        for cp in cps + passed:
            cp.wait_send()
        for cp in local:
            cp.wait()

    return _Exchange(shards, [SDS(f, BF16) for f in full],
                     [pltpu.SemaphoreType.DMA((6 * n,)), pltpu.SemaphoreType.DMA((6 * n,)),
                      pltpu.SemaphoreType.DMA((n,))], start, finish)


def _scatter_grads(parts, shapes, axes):
    n = len(parts)

    def copies(ins, outs, sems):
        x, y, c = _place()
        return [pltpu.make_async_remote_copy(
            src_ref=_window(ins[i], shapes[i], axes[i], 2 * cx + cy, None), dst_ref=outs[i].at[j],
            send_sem=sems[0].at[3 * i + j], recv_sem=sems[1].at[3 * i + j],
            device_id=(cx, cy, c), device_id_type=MESH)
            for i in range(n) for j, (cx, cy) in enumerate(_other_chips(x, y))]

    def start(ins, outs, sems):
        for cp in copies(ins, outs, sems):
            cp.start()

    def finish(ins, outs, sems):
        for cp in copies(ins, outs, sems):
            cp.wait()

    return _Exchange(parts, [SDS((3,) + tuple(s), BF16) for s in shapes],
                     [pltpu.SemaphoreType.DMA((3 * n,)), pltpu.SemaphoreType.DMA((3 * n,))], start, finish)


def _swap_with_sibling(arrs):
    n = len(arrs)

    def copies(ins, outs, sems):
        x, y, c = _place()
        return [pltpu.make_async_remote_copy(src_ref=ins[i], dst_ref=outs[i], send_sem=sems[0].at[i],
                                             recv_sem=sems[1].at[i], device_id=(x, y, 1 - c),
                                             device_id_type=MESH) for i in range(n)]

    def start(ins, outs, sems):
        for cp in copies(ins, outs, sems):
            cp.start()

    def finish(ins, outs, sems):
        for cp in copies(ins, outs, sems):
            cp.wait()

    return _Exchange(arrs, [SDS(a.shape, a.dtype) for a in arrs],
                     [pltpu.SemaphoreType.DMA((n,)), pltpu.SemaphoreType.DMA((n,))], start, finish)


def _gather_small(arrs):
    n = len(arrs)

    def copy(sems, outs, i, k, block, to, src=None):
        px, py, pc = block
        dst = outs[i].at[4 * px + 2 * py + pc]
        return pltpu.make_async_remote_copy(
            src_ref=dst if src is None else src, dst_ref=dst, send_sem=sems[0].at[7 * i + k],
            recv_sem=sems[1].at[7 * i + k], device_id=to, device_id_type=MESH)

    def own_copies(ins, outs, sems):
        x, y, c = _place()
        cps = []
        for i in range(n):
            cps.append(copy(sems, outs, i, 0, (x, y, c), (x, y, 1 - c), src=ins[i]))
            for j, (cx, cy) in enumerate(_other_chips(x, y)):
                cps.append(copy(sems, outs, i, 1 + j, (x, y, c), (cx, cy, c), src=ins[i]))
        local = [pltpu.make_async_copy(ins[i], outs[i].at[4 * x + 2 * y + c], sems[2].at[i]) for i in range(n)]
        return cps, local

    def start(ins, outs, sems):
        cps, local = own_copies(ins, outs, sems)
        for cp in local + cps:
            cp.start()

    def finish(ins, outs, sems):
        x, y, c = _place()
        passed = []
        for j, (cx, cy) in enumerate(_other_chips(x, y)):
            for i in range(n):
                copy(sems, outs, i, 1 + j, (cx, cy, c), (x, y, c)).wait_recv()
                cp = copy(sems, outs, i, 4 + j, (cx, cy, c), (x, y, 1 - c))
                cp.start()
                passed.append(cp)
        for i in range(n):
            copy(sems, outs, i, 0, (x, y, 1 - c), (x, y, c)).wait_recv()
            for j, (cx, cy) in enumerate(_other_chips(x, y)):
                copy(sems, outs, i, 4 + j, (cx, cy, 1 - c), (x, y, c)).wait_recv()
        cps, local = own_copies(ins, outs, sems)
        for cp in cps + passed:
            cp.wait_send()
        for cp in local:
            cp.wait()

    return _Exchange(arrs, [SDS((N_DEV,) + a.shape, F32) for a in arrs],
                     [pltpu.SemaphoreType.DMA((7 * n,)), pltpu.SemaphoreType.DMA((7 * n,)),
                      pltpu.SemaphoreType.DMA((n,))], start, finish)


def _local_step(x, p, tgt, wb, ws, shards=None):
    bsz, seq, _ = x.shape
    t = bsz * seq
    tm = min(256, t)
    tb = min(256, seq)
    x0 = x.reshape(t, D_MODEL)
    p0 = p.reshape(t, PLE_DIM)
    tg = tgt.reshape(t, D_MODEL)
    row = lambda v: v.reshape(1, -1)
    dist = shards is not None
    wb = dict(wb)
    recv = {}
    gb = {}
    gs = {}

    def gather(names):
        return _gather_weights([shards[k] for k in names], [BIG_AXIS[k] for k in names]) if dist else None

    def scatter(names):
        if not dist:
            return None
        return _scatter_grads([gb[k][1] for k in names], [shards[k].shape for k in names],
                              [BIG_AXIS[k] for k in names])

    tril = jnp.tril(jnp.ones((CHUNK, CHUNK), dtype=bool))
    wsm = jnp.where(tril[None], ws["gmlp_w_s"], 0.0)
    wsm_b = wsm.astype(BF16)
    wsmt_b = wsm.transpose(0, 2, 1).astype(BF16)
    bias = jnp.repeat(ws["gmlp_b_s"].T, GMLP_HEAD_DIM, axis=1)

    if dist:
        names = ("ffn1_w_in", "ffn1_w_out")
        wb.update(zip(names, _run_exchange(gather(names), "gather_ffn1")))
    names = ("mix_w_in", "ssm_glu_w", "up_a", "up_b", "mix_w_out")
    (x1, xh1, rstd1, x0b, h1), got = _ffn_fwd(x0, wb["ffn1_w_in"], wb["ffn1_w_out"], row(ws["ln1_g"]),
                                              row(ws["ln1_b"]), tm, "ffn1_fwd", gather(names))
    wb.update(zip(names, got))
    sp = _s5_setup(ws["ssm_lambda_re"], ws["ssm_lambda_im"], ws["ssm_log_dt"], ws["ssm_b_re"],
                   ws["ssm_b_im"], ws["ssm_c_re"], ws["ssm_c_im"], ws["ssm_d"], wb["ssm_glu_w"],
                   ws["ssm_glu_b"], tb)
    names = ("ffn2_w_out",)
    (x1b, za, zuv, gab), got = _mixin_fwd(x1, wb["mix_w_in"], tm, gather(names))
    wb.update(zip(names, got))
    names = ("ffn2_w_in",)
    (s5o, y2p, carries), got = _s5_fwd(za, sp, bsz, seq, tb, gather(names))
    wb.update(zip(names, got))
    names = ("ple_w_gate", "ple_w_proj")
    (gm,), got = _gmlp_fwd(zuv, row(ws["gmlp_ln_g"]), row(ws["gmlp_ln_b"]), wsm_b, bias, gather(names))
    wb.update(zip(names, got))
    (x2, xh2, rstd2, x2b), _ = _mixout_fwd(x1, s5o, gm, gab, wb["up_a"], wb["up_b"], wb["mix_w_out"],
                                           row(ws["ln2_g"]), row(ws["ln2_b"]), tm)
    (x3, xh3, rstd3, _, h2), _ = _ffn_fwd(x2, wb["ffn2_w_in"], wb["ffn2_w_out"], row(ws["ln3_g"]),
                                          row(ws["ln3_b"]), tm, "ffn2_fwd")
    (dx3, x3b, pb, dq, de, loss_rows), _ = _ple_loss(x3, p0, tg, wb["ple_w_gate"], wb["ple_w_proj"], tm)
    gb["ple_w_gate"], _ = _tn_matmul(x3b, dq, "dw_ple_gate", 1024, 1024)
    gb["ple_w_proj"], _ = _tn_matmul(pb, de, "dw_ple_proj", 256, 1024)
    names = ("ple_w_gate", "ple_w_proj")
    (dx2, dh2, a2, df2, gs["ln3_g"], gs["ln3_b"]), got = _ffn_bwd(
        dx3, xh3, rstd3, h2, wb["ffn2_w_in"], wb["ffn2_w_out"], row(ws["ln3_g"]), tm, "ffn2_bwd", scatter(names))
    recv.update(zip(names, got))
    gb["ffn2_w_out"], _ = _tn_matmul(a2, df2, "dw_ffn2_out", 1408, 1024)
    names = ("ffn2_w_out",)
    gb["ffn2_w_in"], got = _tn_matmul(x2b, dh2, "dw_ffn2_in", 1024, 1408, bg=scatter(names))
    recv.update(zip(names, got))
    (dx1a, dmx, mb, dya, dyb, ds5, dgm, dgab, gs["ln2_g"], gs["ln2_b"]), _ = _mixout_bwd(
        dx2, xh2, rstd2, s5o, gm, gab, wb["up_a"], wb["up_b"], wb["mix_w_out"], row(ws["ln2_g"]), tm)
    gb["mix_w_out"], _ = _tn_matmul(mb, dmx, "dw_mix_out", 1024, 1024)
    gb["up_a"], _ = _tn_matmul(s5o, dya, "dw_up_a", 512, 1024)
    gb["up_b"], _ = _tn_matmul(gm, dyb, "dw_up_b", 512, 1024)
    names = ("ffn2_w_in",)
    (dza, dmr, dmi, dnr, dni, da, ddsk, dgw, dgb), got = _s5_bwd(za, y2p, ds5, carries, sp, bsz, seq, tb,
                                                                 scatter(names))
    recv.update(zip(names, got))
    gb["ssm_glu_w"] = (dgw, dgw.astype(BF16))
    names = ("mix_w_out", "up_a")
    (dzuv, dws, dbias, gs["gmlp_ln_g"], gs["gmlp_ln_b"]), got = _gmlp_bwd(
        zuv, dgm, row(ws["gmlp_ln_g"]), row(ws["gmlp_ln_b"]), wsm_b, wsmt_b, bias, scatter(names))
    recv.update(zip(names, got))
    names = ("up_b", "ssm_glu_w")
    (dx1,), got = _mixin_bwd(dx1a, dza, dzuv, dgab, wb["mix_w_in"], tm, scatter(names))
    recv.update(zip(names, got))
    g_mi, _ = _tn_matmul(x1b, dza, "dw_mix_in_a", 1024, 512, 0, 3584)
    g_mi, _ = _tn_matmul(x1b, dzuv, "dw_mix_in_uv", 1024, 512, 1, 3584, g_mi)
    gb["mix_w_in"], _ = _tn_matmul(x1b, dgab, "dw_mix_in_g", 1024, 512, 3, 3584, g_mi)
    names = ("mix_w_in",)
    (dx0, dh1, a1, df1, gs["ln1_g"], gs["ln1_b"]), got = _ffn_bwd(
        dx1, xh1, rstd1, h1, wb["ffn1_w_in"], wb["ffn1_w_out"], row(ws["ln1_g"]), tm, "ffn1_bwd", scatter(names))
    recv.update(zip(names, got))
    gb["ffn1_w_out"], _ = _tn_matmul(a1, df1, "dw_ffn1_out", 1408, 1024)
    names = ("ffn1_w_out",)
    gb["ffn1_w_in"], got = _tn_matmul(x0b, dh1, "dw_ffn1_in", 1024, 1408, bg=scatter(names))
    recv.update(zip(names, got))

    d_abr = da[0].sum(axis=0).reshape(SSM_GROUPS, SSM_STATE)
    d_abi = da[1].sum(axis=0).reshape(SSM_GROUPS, SSM_STATE)
    _, vjp = jax.vjp(_s5_discretise, ws["ssm_lambda_re"], ws["ssm_lambda_im"], ws["ssm_log_dt"],
                     ws["ssm_b_re"], ws["ssm_b_im"])
    (gs["ssm_lambda_re"], gs["ssm_lambda_im"], gs["ssm_log_dt"], gs["ssm_b_re"], gs["ssm_b_im"]) = vjp(
        (d_abr, d_abi, _block_diag_in_t(dmr), _block_diag_in_t(dmi)))
    gs["ssm_c_re"] = _block_diag_out_t(dnr)
    gs["ssm_c_im"] = _block_diag_out_t(dni)
    gs["ssm_d"] = ddsk
    gs["ssm_glu_b"] = dgb
    gs["gmlp_w_s"] = dws
    gs["gmlp_b_s"] = dbias.reshape(CHUNK, GMLP_HEADS, GMLP_HEAD_DIM).sum(axis=-1).T
    gs = {k: gs[k].reshape(SMALL_2D[k]) for k in SMALL}
    return loss_rows, dx0.reshape(bsz, seq, D_MODEL), gb, gs, recv


def _adamw(w, g, m, v):
    m = ADAM_B1 * m + (1.0 - ADAM_B1) * g
    v = ADAM_B2 * v + (1.0 - ADAM_B2) * (g * g)
    m_hat = m / ADAM_C1
    v_hat = v / ADAM_C2
    delta = -ADAM_LR * (m_hat / (jnp.sqrt(v_hat) + ADAM_EPS) + ADAM_WD * w)
    return delta, m, v


def _sum_blocks(part, recv, shape, axis, chip, name):
    r, c = shape
    rb = r // 8

    def body(chip_ref, p_ref, r_ref, o_ref):
        o_ref[...] = (p_ref[...] + r_ref[0].astype(F32) + r_ref[1].astype(F32) + r_ref[2].astype(F32))

    if axis == 0:
        own = pl.BlockSpec((rb, c), lambda i, k: (k[0] * 8 + i, 0))
    else:
        own = pl.BlockSpec((rb, c), lambda i, k: (i, k[0]))
    grid_spec = pltpu.PrefetchScalarGridSpec(
        num_scalar_prefetch=1, grid=(8,),
        in_specs=[own, pl.BlockSpec((3, rb, c), lambda i, k: (0, i, 0))],
        out_specs=pl.BlockSpec((rb, c), lambda i, k: (i, 0)))
    return pl.pallas_call(body, name=name, out_shape=SDS((r, c), F32), grid_spec=grid_spec,
                          compiler_params=_params(("parallel",)))(chip, part, recv)


def _adam_big(w, ga, gb, m, v, name):
    r, c = w.shape
    rb = r // 8

    def body(w_ref, ga_ref, gb_ref, m_ref, v_ref, g_ref, d_ref, nm_ref, nv_ref):
        g = ga_ref[...] + gb_ref[...]
        g_ref[...] = g
        d_ref[...], nm_ref[...], nv_ref[...] = _adamw(w_ref[...], g, m_ref[...], v_ref[...])

    spec = pl.BlockSpec((rb, c), lambda i: (i, 0))
    return pl.pallas_call(
        body, name=name, grid=(8,), out_shape=tuple(SDS((r, c), F32) for _ in range(4)),
        in_specs=[spec] * 5, out_specs=(spec,) * 4, compiler_params=_params(("parallel",)),
    )(w, ga, gb, m, v)


def _adam_small(ws, gathered, ms, vs):
    n = len(ws)

    def body(*refs):
        w_refs, g_refs, m_refs, v_refs = refs[:n], refs[n:2 * n], refs[2 * n:3 * n], refs[3 * n:4 * n]
        outs = refs[4 * n:]
        for i in range(n):
            g = g_refs[i][0]
            for d in range(1, N_DEV):
                g = g + g_refs[i][d]
            delta, nm, nv = _adamw(w_refs[i][...], g, m_refs[i][...], v_refs[i][...])
            outs[i][...] = g
            outs[n + i][...] = delta
            outs[2 * n + i][...] = nm
            outs[3 * n + i][...] = nv

    vmem = pl.BlockSpec(memory_space=pltpu.VMEM)
    shapes = [w.shape for w in ws]
    return pl.pallas_call(
        body, name="adam_small", out_shape=tuple(SDS(s, F32) for s in shapes * 4),
        in_specs=[vmem] * (4 * n), out_specs=tuple([vmem] * (4 * n)),
        compiler_params=pltpu.CompilerParams(vmem_limit_bytes=VMEM_LIMIT_BYTES),
    )(*ws, *gathered, *ms, *vs)


def _sum_loss(gathered):
    def body(g_ref, o_ref):
        tot = g_ref[0]
        for d in range(1, N_DEV):
            tot = tot + g_ref[d]
        o_ref[...] = (0.5 / D_MODEL) * jnp.sum(tot, axis=1, keepdims=True)

    vmem = pl.BlockSpec(memory_space=pltpu.VMEM)
    return pl.pallas_call(body, name="sum_loss", out_shape=SDS((1, 1), F32), in_specs=[vmem],
                          out_specs=vmem)(gathered)


def kernel(x, p, ffn1_w_in, ffn1_w_out, ln1_g, ln1_b, mix_w_in, ssm_lambda_re, ssm_lambda_im, ssm_log_dt, ssm_b_re, ssm_b_im, ssm_c_re, ssm_c_im, ssm_d, ssm_glu_w, ssm_glu_b, gmlp_ln_g, gmlp_ln_b, gmlp_w_s, gmlp_b_s, up_a, up_b, mix_w_out, ln2_g, ln2_b, ffn2_w_in, ffn2_w_out, ln3_g, ln3_b, ple_w_proj, ple_w_gate, loss_target, m_ffn1_w_in, m_ffn1_w_out, m_ln1_g, m_ln1_b, m_mix_w_in, m_ssm_lambda_re, m_ssm_lambda_im, m_ssm_log_dt, m_ssm_b_re, m_ssm_b_im, m_ssm_c_re, m_ssm_c_im, m_ssm_d, m_ssm_glu_w, m_ssm_glu_b, m_gmlp_ln_g, m_gmlp_ln_b, m_gmlp_w_s, m_gmlp_b_s, m_up_a, m_up_b, m_mix_w_out, m_ln2_g, m_ln2_b, m_ffn2_w_in, m_ffn2_w_out, m_ln3_g, m_ln3_b, m_ple_w_proj, m_ple_w_gate, v_ffn1_w_in, v_ffn1_w_out, v_ln1_g, v_ln1_b, v_mix_w_in, v_ssm_lambda_re, v_ssm_lambda_im, v_ssm_log_dt, v_ssm_b_re, v_ssm_b_im, v_ssm_c_re, v_ssm_c_im, v_ssm_d, v_ssm_glu_w, v_ssm_glu_b, v_gmlp_ln_g, v_gmlp_ln_b, v_gmlp_w_s, v_gmlp_b_s, v_up_a, v_up_b, v_mix_w_out, v_ln2_g, v_ln2_b, v_ffn2_w_in, v_ffn2_w_out, v_ln3_g, v_ln3_b, v_ple_w_proj, v_ple_w_gate):
    given = dict(locals())
    order = ("ffn1_w_in", "ffn1_w_out", "ln1_g", "ln1_b", "mix_w_in", "ssm_lambda_re", "ssm_lambda_im",
             "ssm_log_dt", "ssm_b_re", "ssm_b_im", "ssm_c_re", "ssm_c_im", "ssm_d", "ssm_glu_w", "ssm_glu_b",
             "gmlp_ln_g", "gmlp_ln_b", "gmlp_w_s", "gmlp_b_s", "up_a", "up_b", "mix_w_out", "ln2_g", "ln2_b",
             "ffn2_w_in", "ffn2_w_out", "ln3_g", "ln3_b", "ple_w_proj", "ple_w_gate")
    assert set(order) == set(BIG + SMALL)

    shard = {k: given[k][0] for k in BIG}
    shard_b = {k: shard[k].astype(BF16) for k in BIG}
    ws = {k: given[k][0] for k in SMALL}
    loss_rows, grad_x, gb, gs, recv = _local_step(x, given["p"][0], loss_target, {}, ws, shard_b)

    small_in = [gs[k] for k in SMALL] + [loss_rows]
    tail = _join([_scatter_grads([gb["ffn1_w_in"][1]], [shard["ffn1_w_in"].shape], [BIG_AXIS["ffn1_w_in"]]),
                  _gather_small(small_in)])
    got = _run_exchange(tail, "scatter_last_gather_small")
    recv["ffn1_w_in"] = got[0]
    gathered = got[1:]

    xi, yi, _ = _place()
    chip = (2 * xi + yi).astype(jnp.int32).reshape(1)
    sums = [_sum_blocks(gb[k][0], recv[k], shard[k].shape, BIG_AXIS[k], chip, "sum_" + k) for k in BIG]
    other = _run_exchange(_swap_with_sibling(sums), "swap_with_sibling")
    out = {}
    for i, k in enumerate(BIG):
        out[k] = _adam_big(shard[k], sums[i], other[i], given["m_" + k][0], given["v_" + k][0], "adam_" + k)

    res = _adam_small([given[k].reshape(SMALL_2D[k]) for k in SMALL], gathered[:-1],
                      [given["m_" + k].reshape(SMALL_2D[k]) for k in SMALL],
                      [given["v_" + k].reshape(SMALL_2D[k]) for k in SMALL])
    ns = len(SMALL)
    for i, k in enumerate(SMALL):
        out[k] = tuple(res[j * ns + i].reshape(given[k].shape) for j in range(4))
    loss = _sum_loss(gathered[-1]).reshape(())

    lead = lambda k, j: out[k][j][None] if k in BIG else out[k][j]
    return (loss, grad_x, *[lead(k, 0) for k in order], *[lead(k, 1) for k in order],
            *[lead(k, 2) for k in order], *[lead(k, 3) for k in order])
```

```python
import math

import jax
import jax.numpy as jnp
from jax import lax
from jax.experimental import pallas as pl
from jax.experimental.pallas import tpu as pltpu

F32 = jnp.float32
BF16 = jnp.bfloat16
MESH = pl.DeviceIdType.MESH
SDS = jax.ShapeDtypeStruct

D_MODEL = 1024
D_FF = 2816
D_SSM = 512
D_GMLP = 512
SSM_GROUPS = 32
SSM_GROUP_CH = 16
SSM_STATE = 64
SSM_LANES = SSM_GROUPS * SSM_STATE
GMLP_HEADS = 8
GMLP_HEAD_DIM = 64
CHUNK = 128
PLE_DIM = 256
LN_EPS = 1e-5
ALPHA = 2.0 ** 0.25

ADAM_LR = 0.001
ADAM_B1 = 0.9
ADAM_B2 = 0.999
ADAM_EPS = 1e-08
ADAM_WD = 0.01
ADAM_STEP = 10
ADAM_C1 = 1.0 - ADAM_B1 ** ADAM_STEP
ADAM_C2 = 1.0 - ADAM_B2 ** ADAM_STEP

N_DEV = 8
VMEM_LIMIT_BYTES = 56 * 1024 * 1024
FFN_COLS = 1408
S5_BLOCKS = 4
S5_BLOCK_IN = D_SSM // S5_BLOCKS
S5_BLOCK_ST = SSM_LANES // S5_BLOCKS
SCAN_LANES = 512
LAST_PIECES = 4
LAST_PIECE = "ffn1_w_in_q%d"
_G0 = math.sqrt(2.0 / math.pi)
_G1 = 0.044715


def _dot(a, b):
    return jnp.dot(a, b, preferred_element_type=F32)


def _dot_nt(a, b):
    return lax.dot_general(a, b, (((1,), (1,)), ((), ())), preferred_element_type=F32)


def _dot_tn(a, b):
    return lax.dot_general(a, b, (((0,), (0,)), ((), ())), preferred_element_type=F32)


def _sigmoid(x):
    return 1.0 / (1.0 + jnp.exp(-x))


def _gelu(x):
    t = jnp.tanh(_G0 * (x + _G1 * x * x * x))
    return 0.5 * x * (1.0 + t)


def _gelu_grad(x):
    t = jnp.tanh(_G0 * (x + _G1 * x * x * x))
    return 0.5 * (1.0 + t) + 0.5 * x * (1.0 - t * t) * _G0 * (1.0 + 3.0 * _G1 * x * x)


def _ln_fwd(r, g, b):
    mu = jnp.mean(r, axis=-1, keepdims=True)
    d = r - mu
    var = jnp.mean(d * d, axis=-1, keepdims=True)
    rstd = lax.rsqrt(var + LN_EPS)
    xh = d * rstd
    return xh * g + b, xh, rstd


def _ln_bwd(dy, xh, rstd, g):
    dxh = dy * g
    m1 = jnp.mean(dxh, axis=-1, keepdims=True)
    m2 = jnp.mean(dxh * xh, axis=-1, keepdims=True)
    return rstd * (dxh - m1 - xh * m2)


def _resident(shape):
    nd = len(shape)
    return pl.BlockSpec(shape, lambda *_: (0,) * nd, pipeline_mode=pl.Buffered(1))


def _fixed(shape):
    nd = len(shape)
    return pl.BlockSpec(shape, lambda *_: (0,) * nd)


def _rows(tm, cols):
    return pl.BlockSpec((tm, cols), lambda i: (i, 0))


def _params(sem):
    return pltpu.CompilerParams(dimension_semantics=sem, vmem_limit_bytes=VMEM_LIMIT_BYTES)


class _Exchange:
    def __init__(self, args, out_shape, sems, start, finish):
        self.args, self.out_shape, self.sems = list(args), list(out_shape), list(sems)
        self.start, self.finish = start, finish
        self.cuts = [(0, len(self.out_shape))]


def _call(body, name, grid, in_specs, out_specs, out_shape, args, scratch=(), sem=None, bg=None, aliases=None):
    aliases = {} if aliases is None else aliases
    if bg is None:
        res = pl.pallas_call(body, name=name, grid=grid, out_shape=tuple(out_shape), in_specs=list(in_specs),
                             out_specs=tuple(out_specs), scratch_shapes=list(scratch),
                             input_output_aliases=aliases, compiler_params=_params(sem))(*args)
        return tuple(res), ()
    n_in, n_out, n_bi, n_bo, n_sc = len(args), len(out_shape), len(bg.args), len(bg.out_shape), len(scratch)

    def wrapped(*refs):
        ins = refs[:n_in]
        b_ins = refs[n_in:n_in + n_bi]
        outs = refs[n_in + n_bi:n_in + n_bi + n_out]
        b_outs = refs[n_in + n_bi + n_out:n_in + n_bi + n_out + n_bo]
        rest = refs[n_in + n_bi + n_out + n_bo:]
        scr, b_sems = rest[:n_sc], rest[n_sc:]
        first = pl.program_id(0) == 0
        last = pl.program_id(0) == grid[0] - 1
        for ax in range(1, len(grid)):
            first = jnp.logical_and(first, pl.program_id(ax) == 0)
            last = jnp.logical_and(last, pl.program_id(ax) == grid[ax] - 1)

        @pl.when(first)
        def _():
            bg.start(b_ins, b_outs, b_sems)

        body(*ins, *outs, *scr)

        @pl.when(last)
        def _():
            bg.finish(b_ins, b_outs, b_sems)

    any_spec = pl.BlockSpec(memory_space=pl.ANY)
    res = pl.pallas_call(
        wrapped, name=name, grid=grid, out_shape=tuple(out_shape) + tuple(bg.out_shape),
        in_specs=list(in_specs) + [any_spec] * n_bi, out_specs=tuple(out_specs) + (any_spec,) * n_bo,
        scratch_shapes=list(scratch) + list(bg.sems), input_output_aliases=aliases,
        compiler_params=_params(tuple("arbitrary" for _ in grid)))(*args, *bg.args)
    return tuple(res[:n_out]), tuple(res[n_out:])


def _run_exchange(ex, name):
    n_i, n_o = len(ex.args), len(ex.out_shape)

    def body(*refs):
        ins, outs, sems = refs[:n_i], refs[n_i:n_i + n_o], refs[n_i + n_o:]
        ex.start(ins, outs, sems)
        ex.finish(ins, outs, sems)

    any_spec = pl.BlockSpec(memory_space=pl.ANY)
    return tuple(pl.pallas_call(body, name=name, out_shape=tuple(ex.out_shape), in_specs=[any_spec] * n_i,
                                out_specs=(any_spec,) * n_o, scratch_shapes=list(ex.sems))(*ex.args))


def _join(exchanges):
    cuts = []
    a = o = q = 0
    for e in exchanges:
        cuts.append((a, a + len(e.args), o, o + len(e.out_shape), q, q + len(e.sems)))
        a, o, q = cuts[-1][1], cuts[-1][3], cuts[-1][5]

    def start(ins, outs, sems):
        for e, (a0, a1, o0, o1, q0, q1) in zip(exchanges, cuts):
            e.start(ins[a0:a1], outs[o0:o1], sems[q0:q1])

    def finish(ins, outs, sems):
        for e, (a0, a1, o0, o1, q0, q1) in zip(exchanges, cuts):
            e.finish(ins[a0:a1], outs[o0:o1], sems[q0:q1])

    joined = _Exchange(sum((e.args for e in exchanges), []), sum((e.out_shape for e in exchanges), []),
                       sum((e.sems for e in exchanges), []), start, finish)
    joined.cuts = [(c[2], c[3]) for c in cuts]
    return joined


def _ffn_fwd(x, w_in, w_out, g, b, tm, name, bg=None):
    t = x.shape[0]
    nch = D_FF // FFN_COLS

    def body(x_ref, win_ref, wout_ref, g_ref, b_ref, xn_ref, xh_ref, rstd_ref, xb_ref, h_ref):
        xv = x_ref[...]
        xb = xv.astype(BF16)
        xb_ref[...] = xb
        f = jnp.zeros((tm, D_MODEL), F32)
        for k in range(nch):
            cg = slice(k * FFN_COLS, (k + 1) * FFN_COLS)
            cu = slice(D_FF + k * FFN_COLS, D_FF + (k + 1) * FFN_COLS)
            hg = _dot(xb, win_ref[:, cg])
            hu = _dot(xb, win_ref[:, cu])
            h_ref[:, cg] = hg.astype(BF16)
            h_ref[:, cu] = hu.astype(BF16)
            a = hg * _sigmoid(hg) * hu
            f = f + _dot(a.astype(BF16), wout_ref[cg, :])
        y, xh, rstd = _ln_fwd(ALPHA * xv + 0.5 * f, g_ref[...], b_ref[...])
        xn_ref[...] = y
        xh_ref[...] = xh
        rstd_ref[...] = rstd

    return _call(
        body, name, (t // tm,),
        [_rows(tm, D_MODEL), _resident((D_MODEL, 2 * D_FF)), _resident((D_FF, D_MODEL)),
         _fixed((1, D_MODEL)), _fixed((1, D_MODEL))],
        (_rows(tm, D_MODEL), _rows(tm, D_MODEL), _rows(tm, 1), _rows(tm, D_MODEL), _rows(tm, 2 * D_FF)),
        (SDS((t, D_MODEL), F32), SDS((t, D_MODEL), F32), SDS((t, 1), F32), SDS((t, D_MODEL), BF16),
         SDS((t, 2 * D_FF), BF16)),
        (x, w_in, w_out, g, b), sem=("parallel",), bg=bg)


def _ffn_bwd(dxn, xh, rstd, h, w_in, w_out, g, tm, name, bg=None):
    t = dxn.shape[0]
    nch = D_FF // FFN_COLS

    def body(dxn_ref, xh_ref, rstd_ref, h_ref, win_ref, wout_ref, g_ref,
             dx_ref, dh_ref, a_ref, df_ref, dg_ref, db_ref):
        @pl.when(pl.program_id(0) == 0)
        def _():
            dg_ref[...] = jnp.zeros_like(dg_ref)
            db_ref[...] = jnp.zeros_like(db_ref)

        dy = dxn_ref[...]
        xhv = xh_ref[...]
        dr = _ln_bwd(dy, xhv, rstd_ref[...], g_ref[...])
        dg_ref[...] += jnp.sum(dy * xhv, axis=0, keepdims=True)
        db_ref[...] += jnp.sum(dy, axis=0, keepdims=True)
        df = (0.5 * dr).astype(BF16)
        df_ref[...] = df
        dx = ALPHA * dr
        for k in range(nch):
            cg = slice(k * FFN_COLS, (k + 1) * FFN_COLS)
            cu = slice(D_FF + k * FFN_COLS, D_FF + (k + 1) * FFN_COLS)
            hg = h_ref[:, cg].astype(F32)
            hu = h_ref[:, cu].astype(F32)
            sg = _sigmoid(hg)
            silu = hg * sg
            a_ref[:, cg] = (silu * hu).astype(BF16)
            da = _dot_nt(df, wout_ref[cg, :])
            dhu = (da * silu).astype(BF16)
            dhg = (da * hu * (sg * (1.0 + hg * (1.0 - sg)))).astype(BF16)
            dh_ref[:, cg] = dhg
            dh_ref[:, cu] = dhu
            dx = dx + _dot_nt(dhg, win_ref[:, cg]) + _dot_nt(dhu, win_ref[:, cu])
        dx_ref[...] = dx

    return _call(
        body, name, (t // tm,),
        [_rows(tm, D_MODEL), _rows(tm, D_MODEL), _rows(tm, 1), _rows(tm, 2 * D_FF),
         _resident((D_MODEL, 2 * D_FF)), _resident((D_FF, D_MODEL)), _fixed((1, D_MODEL))],
        (_rows(tm, D_MODEL), _rows(tm, 2 * D_FF), _rows(tm, D_FF), _rows(tm, D_MODEL),
         _fixed((1, D_MODEL)), _fixed((1, D_MODEL))),
        (SDS((t, D_MODEL), F32), SDS((t, 2 * D_FF), BF16), SDS((t, D_FF), BF16), SDS((t, D_MODEL), BF16),
         SDS((1, D_MODEL), F32), SDS((1, D_MODEL), F32)),
        (dxn, xh, rstd, h, w_in, w_out, g), sem=("arbitrary",), bg=bg)


def _tn_matmul(a, b, name, bm, bn, col_block=0, total_cols=None, prev=None, bg=None, a_cols=None):
    t, m = a.shape
    a_first = 0
    if a_cols is not None:
        a_first, m = a_cols[0], a_cols[1] * bm
    n = b.shape[1]
    total_cols = n if total_cols is None else total_cols
    bk = min(512, t)
    nk = t // bk
    n_in = 2 if prev is None else 4

    def body(*refs):
        a_ref, b_ref = refs[0], refs[1]
        o_ref, ob_ref = refs[n_in], refs[n_in + 1]
        k = pl.program_id(2)

        @pl.when(k == 0)
        def _():
            o_ref[...] = jnp.zeros_like(o_ref)

        o_ref[...] += _dot_tn(a_ref[...], b_ref[...])

        @pl.when(k == nk - 1)
        def _():
            ob_ref[...] = o_ref[...].astype(BF16)

    in_specs = [pl.BlockSpec((bk, bm), lambda i, j, k: (k, i + a_first)),
                pl.BlockSpec((bk, bn), lambda i, j, k: (k, j))]
    args = [a, b]
    aliases = {}
    if prev is not None:
        in_specs += [pl.BlockSpec(memory_space=pl.ANY), pl.BlockSpec(memory_space=pl.ANY)]
        args += list(prev)
        aliases = {2: 0, 3: 1}
    out_spec = pl.BlockSpec((bm, bn), lambda i, j, k: (i, j + col_block))
    return _call(body, name, (m // bm, n // bn, nk), in_specs, (out_spec, out_spec),
                 (SDS((m, total_cols), F32), SDS((m, total_cols), BF16)), args,
                 sem=("parallel", "parallel", "arbitrary"), bg=bg, aliases=aliases)


def _mixin_fwd(x1, w, tm, bg=None):
    t = x1.shape[0]

    def body(x_ref, w_ref, xb_ref, za_ref, zuv_ref, gab_ref):
        xb = x_ref[...].astype(BF16)
        xb_ref[...] = xb
        za_ref[...] = _dot(xb, w_ref[:, 0:512]).astype(BF16)
        zuv_ref[...] = _dot(xb, w_ref[:, 512:1536]).astype(BF16)
        gab_ref[...] = _dot(xb, w_ref[:, 1536:3584]).astype(BF16)

    return _call(
        body, "mixin_fwd", (t // tm,),
        [_rows(tm, D_MODEL), _resident((D_MODEL, 3584))],
        (_rows(tm, D_MODEL), _rows(tm, 512), _rows(tm, 1024), _rows(tm, 2048)),
        (SDS((t, D_MODEL), BF16), SDS((t, 512), BF16), SDS((t, 1024), BF16), SDS((t, 2048), BF16)),
        (x1, w), sem=("parallel",), bg=bg)


def _mixin_bwd(dx1a, dza, dzuv, dgab, w, tm, bg=None):
    t = dx1a.shape[0]

    def body(d_ref, dza_ref, dzuv_ref, dgab_ref, w_ref, dx_ref):
        dx_ref[...] = (d_ref[...] + _dot_nt(dza_ref[...], w_ref[:, 0:512])
                       + _dot_nt(dzuv_ref[...], w_ref[:, 512:1536])
                       + _dot_nt(dgab_ref[...], w_ref[:, 1536:3584]))

    return _call(
        body, "mixin_bwd", (t // tm,),
        [_rows(tm, D_MODEL), _rows(tm, 512), _rows(tm, 1024), _rows(tm, 2048), _resident((D_MODEL, 3584))],
        (_rows(tm, D_MODEL),), (SDS((t, D_MODEL), F32),),
        (dx1a, dza, dzuv, dgab, w), sem=("parallel",), bg=bg)


def _scan_fwd(hr_ref, hi_ref, a_ref, ap_ref, carry_ref, seg, cin_ref):
    for lc in range(SSM_LANES // SCAN_LANES):
        ls = slice(lc * SCAN_LANES, (lc + 1) * SCAN_LANES)
        a_r = jnp.broadcast_to(a_ref[0:1, ls], (8, SCAN_LANES))
        a_i = jnp.broadcast_to(a_ref[1:2, ls], (8, SCAN_LANES))

        def step(j, hc, ls=ls, a_r=a_r, a_i=a_i):
            h_r, h_i = hc
            rows = pl.ds(pl.multiple_of(j * 8, 8), 8)
            n_r = a_r * h_r - a_i * h_i + hr_ref[rows, ls]
            n_i = a_r * h_i + a_i * h_r + hi_ref[rows, ls]
            hr_ref[rows, ls] = n_r
            hi_ref[rows, ls] = n_i
            return n_r, n_i

        zero = jnp.zeros((8, SCAN_LANES), F32)
        f_r, f_i = lax.fori_loop(0, seg, step, (zero, zero))
        c_r = carry_ref[0:1, ls]
        c_i = carry_ref[1:2, ls]
        p_r = ap_ref[0:1, ls]
        p_i = ap_ref[1:2, ls]
        rows_r, rows_i = [], []
        for s in range(8):
            rows_r.append(c_r)
            rows_i.append(c_i)
            c_r, c_i = (f_r[s:s + 1] + p_r * c_r - p_i * c_i,
                        f_i[s:s + 1] + p_r * c_i + p_i * c_r)
        carry_ref[0:1, ls] = c_r
        carry_ref[1:2, ls] = c_i
        cin_r = jnp.concatenate(rows_r, axis=0)
        cin_i = jnp.concatenate(rows_i, axis=0)
        if cin_ref is not None:
            cin_ref[0, :, ls] = cin_r
            cin_ref[1, :, ls] = cin_i

        def fix(j, cc, ls=ls, a_r=a_r, a_i=a_i):
            c_r, c_i = cc
            c_r, c_i = a_r * c_r - a_i * c_i, a_r * c_i + a_i * c_r
            rows = pl.ds(pl.multiple_of(j * 8, 8), 8)
            hr_ref[rows, ls] = hr_ref[rows, ls] + c_r
            hi_ref[rows, ls] = hi_ref[rows, ls] + c_i
            return c_r, c_i

        lax.fori_loop(0, seg, fix, (cin_r, cin_i))


def _scan_bwd(gr_ref, gi_ref, hr_ref, hi_ref, cin_ref, a_ref, ap_ref, rcarry_ref, da_ref, seg):
    for lc in range(SSM_LANES // SCAN_LANES):
        ls = slice(lc * SCAN_LANES, (lc + 1) * SCAN_LANES)
        a_r = jnp.broadcast_to(a_ref[0:1, ls], (8, SCAN_LANES))
        a_i = jnp.broadcast_to(a_ref[1:2, ls], (8, SCAN_LANES))

        def step(t, gc, ls=ls, a_r=a_r, a_i=a_i):
            g_r, g_i = gc
            rows = pl.ds(pl.multiple_of((seg - 1 - t) * 8, 8), 8)
            n_r = gr_ref[rows, ls] + a_r * g_r + a_i * g_i
            n_i = gi_ref[rows, ls] + a_r * g_i - a_i * g_r
            gr_ref[rows, ls] = n_r
            gi_ref[rows, ls] = n_i
            return n_r, n_i

        zero = jnp.zeros((8, SCAN_LANES), F32)
        f_r, f_i = lax.fori_loop(0, seg, step, (zero, zero))
        c_r = rcarry_ref[0:1, ls]
        c_i = rcarry_ref[1:2, ls]
        p_r = ap_ref[0:1, ls]
        p_i = ap_ref[1:2, ls]
        rows_r, rows_i = [None] * 8, [None] * 8
        for s in range(7, -1, -1):
            rows_r[s] = c_r
            rows_i[s] = c_i
            c_r, c_i = (f_r[s:s + 1] + p_r * c_r + p_i * c_i,
                        f_i[s:s + 1] + p_r * c_i - p_i * c_r)
        rcarry_ref[0:1, ls] = c_r
        rcarry_ref[1:2, ls] = c_i
        cin_r = jnp.concatenate(rows_r, axis=0)
        cin_i = jnp.concatenate(rows_i, axis=0)

        def fix_row(j_rows, hp_r, hp_i, cc, ls=ls, a_r=a_r, a_i=a_i):
            c_r, c_i, acc_r, acc_i = cc
            c_r, c_i = a_r * c_r + a_i * c_i, a_r * c_i - a_i * c_r
            g_r = gr_ref[j_rows, ls] + c_r
            g_i = gi_ref[j_rows, ls] + c_i
            gr_ref[j_rows, ls] = g_r
            gi_ref[j_rows, ls] = g_i
            acc_r = acc_r + g_r * hp_r + g_i * hp_i
            acc_i = acc_i + g_i * hp_r - g_r * hp_i
            return c_r, c_i, acc_r, acc_i

        def fix(t, cc, ls=ls, fix_row=fix_row):
            j = seg - 1 - t
            rows = pl.ds(pl.multiple_of(j * 8, 8), 8)
            prev = pl.ds(pl.multiple_of((j - 1) * 8, 8), 8)
            return fix_row(rows, hr_ref[prev, ls], hi_ref[prev, ls], cc)

        cc = lax.fori_loop(0, seg - 1, fix, (cin_r, cin_i, zero, zero))
        _, _, acc_r, acc_i = fix_row(pl.ds(0, 8), cin_ref[0, :, ls], cin_ref[1, :, ls], cc)
        da_ref[0, :, ls] += acc_r
        da_ref[1, :, ls] += acc_i


def _s5_fwd(za, sp, bsz, seq, tb, bg=None):
    nb = seq // tb
    seg = tb // 8
    t = bsz * seq

    def body(za_ref, perm_ref, permt_ref, mre_ref, mim_ref, nre_ref, nim_ref, a_ref, ap_ref,
             dsk_ref, gw_ref, gb_ref, out_ref, y2_ref, car_ref, hr_ref, hi_ref, carry_ref):
        @pl.when(pl.program_id(1) == 0)
        def _():
            carry_ref[...] = jnp.zeros_like(carry_ref)

        car_ref[0] = carry_ref[...]
        up = _dot(perm_ref[...], za_ref[...])
        upb = up.astype(BF16)
        for bb in range(S5_BLOCKS):
            ub = upb[:, bb * S5_BLOCK_IN:(bb + 1) * S5_BLOCK_IN]
            st = slice(bb * S5_BLOCK_ST, (bb + 1) * S5_BLOCK_ST)
            hr_ref[:, st] = _dot(ub, mre_ref[bb])
            hi_ref[:, st] = _dot(ub, mim_ref[bb])
        _scan_fwd(hr_ref, hi_ref, a_ref, ap_ref, carry_ref, seg, None)
        ys = []
        for bb in range(S5_BLOCKS):
            st = slice(bb * S5_BLOCK_ST, (bb + 1) * S5_BLOCK_ST)
            ys.append(_dot(hr_ref[:, st].astype(BF16), nre_ref[bb])
                      - _dot(hi_ref[:, st].astype(BF16), nim_ref[bb]))
        y2 = jnp.concatenate(ys, axis=1) + dsk_ref[...] * up
        y2_ref[...] = y2
        y3 = _gelu(y2)
        gl = _dot(y3.astype(BF16), gw_ref[...]) + gb_ref[...]
        oa = y3 * _sigmoid(gl)
        out_ref[...] = _dot(permt_ref[...], oa.astype(BF16)).astype(BF16)

    blk = pl.BlockSpec((tb, D_SSM), lambda b, j: (b * nb + j, 0))
    m_shape = (S5_BLOCKS, S5_BLOCK_IN, S5_BLOCK_ST)
    n_shape = (S5_BLOCKS, S5_BLOCK_ST, S5_BLOCK_IN)
    return _call(
        body, "s5_fwd", (bsz, nb),
        [blk, _fixed((tb, tb)), _fixed((tb, tb)), _fixed(m_shape), _fixed(m_shape), _fixed(n_shape),
         _fixed(n_shape), _fixed((2, SSM_LANES)), _fixed((2, SSM_LANES)), _fixed((1, D_SSM)),
         _fixed((D_SSM, D_SSM)), _fixed((1, D_SSM))],
        (blk, blk, pl.BlockSpec((1, 2, SSM_LANES), lambda b, j: (b * nb + j, 0, 0))),
        (SDS((t, D_SSM), BF16), SDS((t, D_SSM), F32), SDS((bsz * nb, 2, SSM_LANES), F32)),
        (za, sp["perm"], sp["permt"], sp["mre"], sp["mim"], sp["nre"], sp["nim"], sp["a"], sp["ap"],
         sp["dskip"], sp["glu_w"], sp["glu_b"]),
        scratch=[pltpu.VMEM((tb, SSM_LANES), F32), pltpu.VMEM((tb, SSM_LANES), F32),
                 pltpu.VMEM((2, SSM_LANES), F32)],
        sem=("arbitrary", "arbitrary"), bg=bg)


def _s5_bwd(za, y2p, doa, carries, sp, bsz, seq, tb, bg=None):
    nb = seq // tb
    seg = tb // 8
    t = bsz * seq

    def body(za_ref, y2_ref, doa_ref, car_ref, perm_ref, permt_ref, mre_ref, mim_ref, mtre_ref, mtim_ref,
             nre_ref, nim_ref, ntre_ref, ntim_ref, a_ref, ap_ref, dsk_ref, gw_ref, gwt_ref, gb_ref,
             dza_ref, dmr_ref, dmi_ref, dnr_ref, dni_ref, da_ref, ddsk_ref, dgw_ref, dgb_ref,
             hr_ref, hi_ref, gr_ref, gi_ref, cin_ref, carry_ref, rcarry_ref):
        first = jnp.logical_and(pl.program_id(0) == 0, pl.program_id(1) == 0)

        @pl.when(first)
        def _():
            for r in (dmr_ref, dmi_ref, dnr_ref, dni_ref, da_ref, ddsk_ref, dgw_ref, dgb_ref):
                r[...] = jnp.zeros_like(r)

        @pl.when(pl.program_id(1) == 0)
        def _():
            rcarry_ref[...] = jnp.zeros_like(rcarry_ref)

        carry_ref[...] = car_ref[0]
        perm = perm_ref[...]
        up = _dot(perm, za_ref[...])
        upb = up.astype(BF16)
        for bb in range(S5_BLOCKS):
            ub = upb[:, bb * S5_BLOCK_IN:(bb + 1) * S5_BLOCK_IN]
            st = slice(bb * S5_BLOCK_ST, (bb + 1) * S5_BLOCK_ST)
            hr_ref[:, st] = _dot(ub, mre_ref[bb])
            hi_ref[:, st] = _dot(ub, mim_ref[bb])
        _scan_fwd(hr_ref, hi_ref, a_ref, ap_ref, carry_ref, seg, cin_ref)

        y2 = y2_ref[...]
        y3 = _gelu(y2)
        y3b = y3.astype(BF16)
        sg = _sigmoid(_dot(y3b, gw_ref[...]) + gb_ref[...])
        d0 = doa_ref[...]
        d_hi = d0.astype(BF16)
        d1 = d0 - d_hi.astype(F32)
        d_mid = d1.astype(BF16)
        d_lo = (d1 - d_mid.astype(F32)).astype(BF16)
        doap = _dot(perm, d_hi) + _dot(perm, d_mid) + _dot(perm, d_lo)
        dgl = doap * y3 * sg * (1.0 - sg)
        dglb = dgl.astype(BF16)
        dy3 = doap * sg + _dot(dglb, gwt_ref[...])
        dgw_ref[...] += _dot_tn(y3b, dglb)
        dgb_ref[...] += jnp.sum(dgl, axis=0, keepdims=True)
        dy2 = dy3 * _gelu_grad(y2)
        ddsk_ref[...] += jnp.sum(dy2 * up, axis=0, keepdims=True)
        dyb = dy2.astype(BF16)
        for bb in range(S5_BLOCKS):
            dyc = dyb[:, bb * S5_BLOCK_IN:(bb + 1) * S5_BLOCK_IN]
            st = slice(bb * S5_BLOCK_ST, (bb + 1) * S5_BLOCK_ST)
            gr_ref[:, st] = _dot(dyc, ntre_ref[bb])
            gi_ref[:, st] = -_dot(dyc, ntim_ref[bb])
            dnr_ref[bb] += _dot_tn(hr_ref[:, st].astype(BF16), dyc)
            dni_ref[bb] += -_dot_tn(hi_ref[:, st].astype(BF16), dyc)
        _scan_bwd(gr_ref, gi_ref, hr_ref, hi_ref, cin_ref, a_ref, ap_ref, rcarry_ref, da_ref, seg)
        dus = []
        for bb in range(S5_BLOCKS):
            st = slice(bb * S5_BLOCK_ST, (bb + 1) * S5_BLOCK_ST)
            grb = gr_ref[:, st].astype(BF16)
            gib = gi_ref[:, st].astype(BF16)
            dus.append(_dot(grb, mtre_ref[bb]) + _dot(gib, mtim_ref[bb]))
            ub = upb[:, bb * S5_BLOCK_IN:(bb + 1) * S5_BLOCK_IN]
            dmr_ref[bb] += _dot_tn(ub, grb)
            dmi_ref[bb] += _dot_tn(ub, gib)
        du = jnp.concatenate(dus, axis=1) + dy2 * dsk_ref[...]
        dza_ref[...] = _dot(permt_ref[...], du.astype(BF16)).astype(BF16)

    def rev(b, j):
        return (b * nb + (nb - 1 - j), 0)

    blk = pl.BlockSpec((tb, D_SSM), rev)
    m_shape = (S5_BLOCKS, S5_BLOCK_IN, S5_BLOCK_ST)
    n_shape = (S5_BLOCKS, S5_BLOCK_ST, S5_BLOCK_IN)
    return _call(
        body, "s5_bwd", (bsz, nb),
        [blk, blk, blk, pl.BlockSpec((1, 2, SSM_LANES), lambda b, j: (b * nb + (nb - 1 - j), 0, 0)),
         _fixed((tb, tb)), _fixed((tb, tb)), _fixed(m_shape), _fixed(m_shape), _fixed(n_shape), _fixed(n_shape),
         _fixed(n_shape), _fixed(n_shape), _fixed(m_shape), _fixed(m_shape),
         _fixed((2, SSM_LANES)), _fixed((2, SSM_LANES)), _fixed((1, D_SSM)),
         _fixed((D_SSM, D_SSM)), _fixed((D_SSM, D_SSM)), _fixed((1, D_SSM))],
        (blk, _fixed(m_shape), _fixed(m_shape), _fixed(n_shape), _fixed(n_shape),
         _fixed((2, 8, SSM_LANES)), _fixed((1, D_SSM)), _fixed((D_SSM, D_SSM)), _fixed((1, D_SSM))),
        (SDS((t, D_SSM), BF16), SDS(m_shape, F32), SDS(m_shape, F32), SDS(n_shape, F32), SDS(n_shape, F32),
         SDS((2, 8, SSM_LANES), F32), SDS((1, D_SSM), F32), SDS((D_SSM, D_SSM), F32), SDS((1, D_SSM), F32)),
        (za, y2p, doa, carries, sp["perm"], sp["permt"], sp["mre"], sp["mim"], sp["mtre"], sp["mtim"],
         sp["nre"], sp["nim"], sp["ntre"], sp["ntim"], sp["a"], sp["ap"], sp["dskip"], sp["glu_w"],
         sp["glu_wt"], sp["glu_b"]),
        scratch=[pltpu.VMEM((tb, SSM_LANES), F32), pltpu.VMEM((tb, SSM_LANES), F32),
                 pltpu.VMEM((tb, SSM_LANES), F32), pltpu.VMEM((tb, SSM_LANES), F32),
                 pltpu.VMEM((2, 8, SSM_LANES), F32), pltpu.VMEM((2, SSM_LANES), F32),
                 pltpu.VMEM((2, SSM_LANES), F32)],
        sem=("arbitrary", "arbitrary"), bg=bg)


def _gmlp_spatial(ws_ref, vb):
    lane = lax.broadcasted_iota(jnp.int32, (CHUNK, 128), 1)
    parts = []
    for j in range(GMLP_HEADS // 2):
        vp = vb[:, 128 * j:128 * (j + 1)]
        parts.append(jnp.where(lane < GMLP_HEAD_DIM, _dot(ws_ref[2 * j], vp), _dot(ws_ref[2 * j + 1], vp)))
    return jnp.concatenate(parts, axis=1)


def _gmlp_fwd(zuv, ln_g, ln_b, wsm, bias, bg=None):
    t = zuv.shape[0]

    def body(z_ref, g_ref, b_ref, ws_ref, bias_ref, out_ref):
        u = _gelu(z_ref[:, 0:D_GMLP].astype(F32))
        v0 = _gelu(z_ref[:, D_GMLP:2 * D_GMLP].astype(F32))
        v, _, _ = _ln_fwd(v0, g_ref[...], b_ref[...])
        s = _gmlp_spatial(ws_ref, v.astype(BF16)) + bias_ref[...]
        out_ref[...] = (u * s).astype(BF16)

    return _call(
        body, "gmlp_fwd", (t // CHUNK,),
        [_rows(CHUNK, 2 * D_GMLP), _fixed((1, D_GMLP)), _fixed((1, D_GMLP)),
         _fixed((GMLP_HEADS, CHUNK, CHUNK)), _fixed((CHUNK, D_GMLP))],
        (_rows(CHUNK, D_GMLP),), (SDS((t, D_GMLP), BF16),),
        (zuv, ln_g, ln_b, wsm, bias), sem=("parallel",), bg=bg)


def _gmlp_bwd(zuv, dgm, ln_g, ln_b, wsm, wsmt, bias, bg=None):
    t = zuv.shape[0]

    def body(z_ref, d_ref, g_ref, b_ref, ws_ref, wst_ref, bias_ref,
             dz_ref, dws_ref, dbias_ref, dg_ref, db_ref):
        @pl.when(pl.program_id(0) == 0)
        def _():
            for r in (dws_ref, dbias_ref, dg_ref, db_ref):
                r[...] = jnp.zeros_like(r)

        zu = z_ref[:, 0:D_GMLP].astype(F32)
        zv = z_ref[:, D_GMLP:2 * D_GMLP].astype(F32)
        u = _gelu(zu)
        v0 = _gelu(zv)
        gam = g_ref[...]
        v, vhat, rstd = _ln_fwd(v0, gam, b_ref[...])
        vb = v.astype(BF16)
        s = _gmlp_spatial(ws_ref, vb) + bias_ref[...]
        d = d_ref[...]
        dz_ref[:, 0:D_GMLP] = (d * s * _gelu_grad(zu)).astype(BF16)
        ds = d * u
        dbias_ref[...] += ds
        dsb = ds.astype(BF16)
        lane = lax.broadcasted_iota(jnp.int32, (CHUNK, 128), 1)
        tril = (lax.broadcasted_iota(jnp.int32, (CHUNK, CHUNK), 0)
                >= lax.broadcasted_iota(jnp.int32, (CHUNK, CHUNK), 1))
        zero_b = jnp.zeros((CHUNK, 128), BF16)
        parts = []
        for j in range(GMLP_HEADS // 2):
            dsp = dsb[:, 128 * j:128 * (j + 1)]
            vp = vb[:, 128 * j:128 * (j + 1)]
            parts.append(jnp.where(lane < GMLP_HEAD_DIM, _dot(wst_ref[2 * j], dsp),
                                   _dot(wst_ref[2 * j + 1], dsp)))
            lo = jnp.where(lane < GMLP_HEAD_DIM, dsp, zero_b)
            hi = jnp.where(lane < GMLP_HEAD_DIM, zero_b, dsp)
            dws_ref[2 * j] += jnp.where(tril, _dot_nt(lo, vp), 0.0)
            dws_ref[2 * j + 1] += jnp.where(tril, _dot_nt(hi, vp), 0.0)
        dv = jnp.concatenate(parts, axis=1)
        dg_ref[...] += jnp.sum(dv * vhat, axis=0, keepdims=True)
        db_ref[...] += jnp.sum(dv, axis=0, keepdims=True)
        dz_ref[:, D_GMLP:2 * D_GMLP] = (_ln_bwd(dv, vhat, rstd, gam) * _gelu_grad(zv)).astype(BF16)

    return _call(
        body, "gmlp_bwd", (t // CHUNK,),
        [_rows(CHUNK, 2 * D_GMLP), _rows(CHUNK, D_GMLP), _fixed((1, D_GMLP)), _fixed((1, D_GMLP)),
         _fixed((GMLP_HEADS, CHUNK, CHUNK)), _fixed((GMLP_HEADS, CHUNK, CHUNK)), _fixed((CHUNK, D_GMLP))],
        (_rows(CHUNK, 2 * D_GMLP), _fixed((GMLP_HEADS, CHUNK, CHUNK)), _fixed((CHUNK, D_GMLP)),
         _fixed((1, D_GMLP)), _fixed((1, D_GMLP))),
        (SDS((t, 2 * D_GMLP), BF16), SDS((GMLP_HEADS, CHUNK, CHUNK), F32), SDS((CHUNK, D_GMLP), F32),
         SDS((1, D_GMLP), F32), SDS((1, D_GMLP), F32)),
        (zuv, dgm, ln_g, ln_b, wsm, wsmt, bias), sem=("arbitrary",), bg=bg)


def _mixout_fwd(x1, s5o, gm, gab, ua, ub, wmo, g, b, tm, bg=None):
    t = x1.shape[0]

    def body(x_ref, s_ref, m_ref, gab_ref, ua_ref, ub_ref, wmo_ref, g_ref, b_ref,
             xn_ref, xh_ref, rstd_ref, xb_ref):
        ya = _dot(s_ref[...], ua_ref[...])
        yb = _dot(m_ref[...], ub_ref[...])
        mix = (_sigmoid(gab_ref[:, 0:D_MODEL].astype(F32)) * ya
               + _sigmoid(gab_ref[:, D_MODEL:2 * D_MODEL].astype(F32)) * yb)
        r = ALPHA * x_ref[...] + _dot(mix.astype(BF16), wmo_ref[...])
        y, xh, rstd = _ln_fwd(r, g_ref[...], b_ref[...])
        xn_ref[...] = y
        xh_ref[...] = xh
        rstd_ref[...] = rstd
        xb_ref[...] = y.astype(BF16)

    return _call(
        body, "mixout_fwd", (t // tm,),
        [_rows(tm, D_MODEL), _rows(tm, D_SSM), _rows(tm, D_GMLP), _rows(tm, 2 * D_MODEL),
         _resident((D_SSM, D_MODEL)), _resident((D_GMLP, D_MODEL)), _resident((D_MODEL, D_MODEL)),
         _fixed((1, D_MODEL)), _fixed((1, D_MODEL))],
        (_rows(tm, D_MODEL), _rows(tm, D_MODEL), _rows(tm, 1), _rows(tm, D_MODEL)),
        (SDS((t, D_MODEL), F32), SDS((t, D_MODEL), F32), SDS((t, 1), F32), SDS((t, D_MODEL), BF16)),
        (x1, s5o, gm, gab, ua, ub, wmo, g, b), sem=("parallel",), bg=bg)


def _mixout_bwd(dx2, xh, rstd, s5o, gm, gab, ua, ub, wmo, g, tm, bg=None):
    t = dx2.shape[0]

    def body(d_ref, xh_ref, rstd_ref, s_ref, m_ref, gab_ref, ua_ref, ub_ref, wmo_ref, g_ref,
             dx1_ref, dmx_ref, mb_ref, dya_ref, dyb_ref, ds5_ref, dgm_ref, dgab_ref, dg_ref, db_ref):
        @pl.when(pl.program_id(0) == 0)
        def _():
            dg_ref[...] = jnp.zeros_like(dg_ref)
            db_ref[...] = jnp.zeros_like(db_ref)

        dy = d_ref[...]
        xhv = xh_ref[...]
        dr = _ln_bwd(dy, xhv, rstd_ref[...], g_ref[...])
        dg_ref[...] += jnp.sum(dy * xhv, axis=0, keepdims=True)
        db_ref[...] += jnp.sum(dy, axis=0, keepdims=True)
        dx1_ref[...] = ALPHA * dr
        drb = dr.astype(BF16)
        dmx_ref[...] = drb
        dm = _dot_nt(drb, wmo_ref[...])
        ya = _dot(s_ref[...], ua_ref[...])
        yb = _dot(m_ref[...], ub_ref[...])
        sa = _sigmoid(gab_ref[:, 0:D_MODEL].astype(F32))
        sb = _sigmoid(gab_ref[:, D_MODEL:2 * D_MODEL].astype(F32))
        mb_ref[...] = (sa * ya + sb * yb).astype(BF16)
        dya = (dm * sa).astype(BF16)
        dyb = (dm * sb).astype(BF16)
        dya_ref[...] = dya
        dyb_ref[...] = dyb
        dgab_ref[:, 0:D_MODEL] = (dm * ya * sa * (1.0 - sa)).astype(BF16)
        dgab_ref[:, D_MODEL:2 * D_MODEL] = (dm * yb * sb * (1.0 - sb)).astype(BF16)
        ds5_ref[...] = _dot_nt(dya, ua_ref[...])
        dgm_ref[...] = _dot_nt(dyb, ub_ref[...])

    return _call(
        body, "mixout_bwd", (t // tm,),
        [_rows(tm, D_MODEL), _rows(tm, D_MODEL), _rows(tm, 1), _rows(tm, D_SSM), _rows(tm, D_GMLP),
         _rows(tm, 2 * D_MODEL), _resident((D_SSM, D_MODEL)), _resident((D_GMLP, D_MODEL)),
         _resident((D_MODEL, D_MODEL)), _fixed((1, D_MODEL))],
        (_rows(tm, D_MODEL), _rows(tm, D_MODEL), _rows(tm, D_MODEL), _rows(tm, D_MODEL),
         _rows(tm, D_MODEL), _rows(tm, D_SSM), _rows(tm, D_GMLP), _rows(tm, 2 * D_MODEL),
         _fixed((1, D_MODEL)), _fixed((1, D_MODEL))),
        (SDS((t, D_MODEL), F32), SDS((t, D_MODEL), BF16), SDS((t, D_MODEL), BF16),
         SDS((t, D_MODEL), BF16), SDS((t, D_MODEL), BF16), SDS((t, D_SSM), F32),
         SDS((t, D_GMLP), F32), SDS((t, 2 * D_MODEL), BF16),
         SDS((1, D_MODEL), F32), SDS((1, D_MODEL), F32)),
        (dx2, xh, rstd, s5o, gm, gab, ua, ub, wmo, g), sem=("arbitrary",), bg=bg)


def _ple_loss(x3, p, tgt, wpg, wpp, tm, bg=None):
    t = x3.shape[0]

    def body(x_ref, p_ref, t_ref, wpg_ref, wpp_ref, dx_ref, xb_ref, pb_ref, dq_ref, de_ref, loss_ref):
        @pl.when(pl.program_id(0) == 0)
        def _():
            loss_ref[...] = jnp.zeros_like(loss_ref)

        x3v = x_ref[...]
        xb = x3v.astype(BF16)
        pb = p_ref[...].astype(BF16)
        xb_ref[...] = xb
        pb_ref[...] = pb
        s = _sigmoid(_dot(xb, wpg_ref[...]))
        e = _dot(pb, wpp_ref[...])
        diff = x3v + s * e - t_ref[...]
        loss_ref[...] += jnp.sum(diff * diff, axis=0, keepdims=True)
        dout = diff * (1.0 / D_MODEL)
        de_ref[...] = (dout * s).astype(BF16)
        dq = (dout * e * s * (1.0 - s)).astype(BF16)
        dq_ref[...] = dq
        dx_ref[...] = dout + _dot_nt(dq, wpg_ref[...])

    return _call(
        body, "ple_loss", (t // tm,),
        [_rows(tm, D_MODEL), _rows(tm, PLE_DIM), _rows(tm, D_MODEL),
         _resident((D_MODEL, D_MODEL)), _resident((PLE_DIM, D_MODEL))],
        (_rows(tm, D_MODEL), _rows(tm, D_MODEL), _rows(tm, PLE_DIM), _rows(tm, D_MODEL),
         _rows(tm, D_MODEL), _fixed((1, D_MODEL))),
        (SDS((t, D_MODEL), F32), SDS((t, D_MODEL), BF16), SDS((t, PLE_DIM), BF16),
         SDS((t, D_MODEL), BF16), SDS((t, D_MODEL), BF16), SDS((1, D_MODEL), F32)),
        (x3, p, tgt, wpg, wpp), sem=("arbitrary",), bg=bg)


def _s5_discretise(lre, lim, log_dt, bre, bim):
    dt = jnp.exp(log_dt)[:, None]
    mag = jnp.exp(lre * dt)
    abr = mag * jnp.cos(lim * dt)
    abi = mag * jnp.sin(lim * dt)
    nr = abr - 1.0
    ni = abi
    den = lre * lre + lim * lim
    cr = ((nr * lre + ni * lim) / den)[..., None]
    ci = ((ni * lre - nr * lim) / den)[..., None]
    return abr, abi, cr * bre - ci * bim, cr * bim + ci * bre


def _block_diag_in(bb):
    v = bb.reshape(S5_BLOCKS, 8, SSM_STATE, SSM_GROUP_CH).transpose(0, 1, 3, 2)
    return jnp.einsum("bgip,gh->bgihp", v, jnp.eye(8, dtype=bb.dtype)).reshape(
        S5_BLOCKS, S5_BLOCK_IN, S5_BLOCK_ST)


def _block_diag_in_t(dm):
    v = dm.reshape(S5_BLOCKS, 8, SSM_GROUP_CH, 8, SSM_STATE)
    d = jnp.einsum("bgihp,gh->bgip", v, jnp.eye(8, dtype=dm.dtype))
    return d.transpose(0, 1, 3, 2).reshape(SSM_GROUPS, SSM_STATE, SSM_GROUP_CH)


def _block_diag_out(cc):
    v = cc.reshape(S5_BLOCKS, 8, SSM_GROUP_CH, SSM_STATE)
    return jnp.einsum("bgip,gh->bgphi", v, jnp.eye(8, dtype=cc.dtype)).reshape(
        S5_BLOCKS, S5_BLOCK_ST, S5_BLOCK_IN)


def _block_diag_out_t(dn):
    v = dn.reshape(S5_BLOCKS, 8, SSM_STATE, 8, SSM_GROUP_CH)
    d = jnp.einsum("bgphi,gh->bgip", v, jnp.eye(8, dtype=dn.dtype))
    return d.reshape(SSM_GROUPS, SSM_GROUP_CH, SSM_STATE)


def _s5_setup(lre, lim, log_dt, bre, bim, cre, cim, d_skip, glu_w, glu_b, tb):
    seg = tb // 8
    abr, abi, bbr, bbi = _s5_discretise(lre, lim, log_dt, bre, bim)
    pr, pi = abr, abi
    for _ in range(int(math.log2(seg))):
        pr, pi = pr * pr - pi * pi, 2.0 * pr * pi
    rows = jnp.arange(tb)
    src = (rows % 8) * seg + rows // 8
    perm = (src[:, None] == jnp.arange(tb)[None, :]).astype(BF16)
    mre = _block_diag_in(bbr)
    mim = _block_diag_in(bbi)
    nre = _block_diag_out(cre)
    nim = _block_diag_out(cim)
    return {
        "perm": perm, "permt": perm.T,
        "mre": mre.astype(BF16), "mim": mim.astype(BF16),
        "mtre": mre.transpose(0, 2, 1).astype(BF16), "mtim": mim.transpose(0, 2, 1).astype(BF16),
        "nre": nre.astype(BF16), "nim": nim.astype(BF16),
        "ntre": nre.transpose(0, 2, 1).astype(BF16), "ntim": nim.transpose(0, 2, 1).astype(BF16),
        "a": jnp.stack([abr.reshape(-1), abi.reshape(-1)]),
        "ap": jnp.stack([pr.reshape(-1), pi.reshape(-1)]),
        "dskip": d_skip.reshape(1, D_SSM), "glu_w": glu_w, "glu_wt": glu_w.T,
        "glu_b": glu_b.reshape(1, D_SSM),
    }


BIG = ("ffn1_w_in", "ffn1_w_out", "mix_w_in", "ssm_glu_w", "up_a", "up_b", "mix_w_out",
       "ffn2_w_in", "ffn2_w_out", "ple_w_proj", "ple_w_gate")
BIG_AXIS = {"ffn1_w_in": 1, "ffn1_w_out": 0, "mix_w_in": 1, "ssm_glu_w": 0, "up_a": 1, "up_b": 1,
            "mix_w_out": 0, "ffn2_w_in": 1, "ffn2_w_out": 0, "ple_w_proj": 1, "ple_w_gate": 0}
SMALL = ("ln1_g", "ln1_b", "ssm_lambda_re", "ssm_lambda_im", "ssm_log_dt", "ssm_b_re", "ssm_b_im",
         "ssm_c_re", "ssm_c_im", "ssm_d", "ssm_glu_b", "gmlp_ln_g", "gmlp_ln_b", "gmlp_w_s",
         "gmlp_b_s", "ln2_g", "ln2_b", "ln3_g", "ln3_b")
SMALL_2D = {"ln1_g": (1, 1024), "ln1_b": (1, 1024), "ssm_lambda_re": (32, 64), "ssm_lambda_im": (32, 64),
            "ssm_log_dt": (1, 32), "ssm_b_re": (32, 1024), "ssm_b_im": (32, 1024), "ssm_c_re": (32, 1024),
            "ssm_c_im": (32, 1024), "ssm_d": (1, 512), "ssm_glu_b": (1, 512), "gmlp_ln_g": (1, 512),
            "gmlp_ln_b": (1, 512), "gmlp_w_s": (1024, 128), "gmlp_b_s": (8, 128), "ln2_g": (1, 1024),
            "ln2_b": (1, 1024), "ln3_g": (1, 1024), "ln3_b": (1, 1024)}


def _place():
    return lax.axis_index("x"), lax.axis_index("y"), lax.axis_index("c")


def _other_chips(x, y):
    return [(1 - x, y), (x, 1 - y), (1 - x, 1 - y)]


def _window(ref, shard_shape, axis, chip, half):
    r, c = shard_shape
    hr = r // 2
    if axis == 0:
        if half is None:
            return ref.at[pl.ds(chip * r, r), :]
        return ref.at[pl.ds(chip * r + half * hr, hr), :]
    if half is None:
        return ref.at[:, pl.ds(chip * c, c)]
    return ref.at[pl.ds(half * hr, hr), pl.ds(chip * c, c)]


def _gather_weights(shards, axes):
    n = len(shards)
    shapes = [s.shape for s in shards]
    full = [(4 * r, c) if ax == 0 else (r, 4 * c) for (r, c), ax in zip(shapes, axes)]

    def remote(sems, i, k, src, dst, to):
        return pltpu.make_async_remote_copy(src_ref=src, dst_ref=dst, send_sem=sems[0].at[6 * i + k],
                                            recv_sem=sems[1].at[6 * i + k], device_id=to, device_id_type=MESH)

    def own_copies(ins, outs, sems):
        x, y, c = _place()
        me = 2 * x + y
        cps = []
        for i in range(n):
            hr = shapes[i][0] // 2
            mine = ins[i].at[pl.ds(c * hr, hr), :]
            for j, (cx, cy) in enumerate(_other_chips(x, y)):
                cps.append(remote(sems, i, j, mine, _window(outs[i], shapes[i], axes[i], me, c), (cx, cy, c)))
        local = [pltpu.make_async_copy(ins[i], _window(outs[i], shapes[i], axes[i], me, None), sems[2].at[i])
                 for i in range(n)]
        return cps, local

    def start(ins, outs, sems):
        cps, local = own_copies(ins, outs, sems)
        for cp in local + cps:
            cp.start()

    def finish(ins, outs, sems):
        x, y, c = _place()
        sibling = (x, y, 1 - c)
        passed = []
        for j, (cx, cy) in enumerate(_other_chips(x, y)):
            for i in range(n):
                w = _window(outs[i], shapes[i], axes[i], 2 * cx + cy, c)
                remote(sems, i, j, w, w, (cx, cy, c)).wait_recv()
                cp = remote(sems, i, 3 + j, w, w, sibling)
                cp.start()
                passed.append(cp)
        for j, (cx, cy) in enumerate(_other_chips(x, y)):
            for i in range(n):
                w = _window(outs[i], shapes[i], axes[i], 2 * cx + cy, 1 - c)
                remote(sems, i, 3 + j, w, w, sibling).wait_recv()
        cps, local = own_copies(ins, outs, sems)
        for cp in cps + passed:
            cp.wait_send()
        for cp in local:
            cp.wait()

    return _Exchange(shards, [SDS(f, BF16) for f in full],
                     [pltpu.SemaphoreType.DMA((6 * n,)), pltpu.SemaphoreType.DMA((6 * n,)),
                      pltpu.SemaphoreType.DMA((n,))], start, finish)


def _scatter_grads(parts, shapes, axes):
    n = len(parts)

    def copies(ins, outs, sems):
        x, y, c = _place()
        return [pltpu.make_async_remote_copy(
            src_ref=_window(ins[i], shapes[i], axes[i], 2 * cx + cy, None), dst_ref=outs[i].at[j],
            send_sem=sems[0].at[3 * i + j], recv_sem=sems[1].at[3 * i + j],
            device_id=(cx, cy, c), device_id_type=MESH)
            for i in range(n) for j, (cx, cy) in enumerate(_other_chips(x, y))]

    def start(ins, outs, sems):
        for cp in copies(ins, outs, sems):
            cp.start()

    def finish(ins, outs, sems):
        for cp in copies(ins, outs, sems):
            cp.wait()

    return _Exchange(parts, [SDS((3,) + tuple(s), BF16) for s in shapes],
                     [pltpu.SemaphoreType.DMA((3 * n,)), pltpu.SemaphoreType.DMA((3 * n,))], start, finish)


def _swap_with_sibling(arrs):
    n = len(arrs)

    def copies(ins, outs, sems):
        x, y, c = _place()
        return [pltpu.make_async_remote_copy(src_ref=ins[i], dst_ref=outs[i], send_sem=sems[0].at[i],
                                             recv_sem=sems[1].at[i], device_id=(x, y, 1 - c),
                                             device_id_type=MESH) for i in range(n)]

    def start(ins, outs, sems):
        for cp in copies(ins, outs, sems):
            cp.start()

    def finish(ins, outs, sems):
        for cp in copies(ins, outs, sems):
            cp.wait()

    return _Exchange(arrs, [SDS(a.shape, a.dtype) for a in arrs],
                     [pltpu.SemaphoreType.DMA((n,)), pltpu.SemaphoreType.DMA((n,))], start, finish)


def _gather_small(arrs):
    n = len(arrs)

    def copy(sems, outs, i, k, block, to, src=None):
        px, py, pc = block
        dst = outs[i].at[4 * px + 2 * py + pc]
        return pltpu.make_async_remote_copy(
            src_ref=dst if src is None else src, dst_ref=dst, send_sem=sems[0].at[7 * i + k],
            recv_sem=sems[1].at[7 * i + k], device_id=to, device_id_type=MESH)

    def own_copies(ins, outs, sems):
        x, y, c = _place()
        cps = []
        for i in range(n):
            cps.append(copy(sems, outs, i, 0, (x, y, c), (x, y, 1 - c), src=ins[i]))
            for j, (cx, cy) in enumerate(_other_chips(x, y)):
                cps.append(copy(sems, outs, i, 1 + j, (x, y, c), (cx, cy, c), src=ins[i]))
        local = [pltpu.make_async_copy(ins[i], outs[i].at[4 * x + 2 * y + c], sems[2].at[i]) for i in range(n)]
        return cps, local

    def start(ins, outs, sems):
        cps, local = own_copies(ins, outs, sems)
        for cp in local + cps:
            cp.start()

    def finish(ins, outs, sems):
        x, y, c = _place()
        passed = []
        for j, (cx, cy) in enumerate(_other_chips(x, y)):
            for i in range(n):
                copy(sems, outs, i, 1 + j, (cx, cy, c), (x, y, c)).wait_recv()
                cp = copy(sems, outs, i, 4 + j, (cx, cy, c), (x, y, 1 - c))
                cp.start()
                passed.append(cp)
        for i in range(n):
            copy(sems, outs, i, 0, (x, y, 1 - c), (x, y, c)).wait_recv()
            for j, (cx, cy) in enumerate(_other_chips(x, y)):
                copy(sems, outs, i, 4 + j, (cx, cy, 1 - c), (x, y, c)).wait_recv()
        cps, local = own_copies(ins, outs, sems)
        for cp in cps + passed:
            cp.wait_send()
        for cp in local:
            cp.wait()

    return _Exchange(arrs, [SDS((N_DEV,) + a.shape, F32) for a in arrs],
                     [pltpu.SemaphoreType.DMA((7 * n,)), pltpu.SemaphoreType.DMA((7 * n,)),
                      pltpu.SemaphoreType.DMA((n,))], start, finish)


def _local_step(x, p, tgt, wb, ws, shards=None):
    bsz, seq, _ = x.shape
    t = bsz * seq
    tm = min(256, t)
    tb = min(256, seq)
    x0 = x.reshape(t, D_MODEL)
    p0 = p.reshape(t, PLE_DIM)
    tg = tgt.reshape(t, D_MODEL)
    row = lambda v: v.reshape(1, -1)
    dist = shards is not None
    wb = dict(wb)
    recv, sums, other, gathered = {}, {}, {}, {}
    gb = {}
    gs = {}
    shape_of, axis_of = {}, {}
    chip = None
    if dist:
        shape_of = {k: tuple(shards[k].shape) for k in BIG}
        axis_of = dict(BIG_AXIS)
        for q in range(LAST_PIECES):
            shape_of[LAST_PIECE % q] = (D_MODEL // LAST_PIECES, shape_of["ffn1_w_in"][1])
            axis_of[LAST_PIECE % q] = 1
        xi, yi, _ = _place()
        chip = (2 * xi + yi).astype(jnp.int32).reshape(1)

    def gather(names):
        return _gather_weights([shards[k] for k in names], [BIG_AXIS[k] for k in names]) if dist else None

    def exchange(scat=(), swap=(), extra=None):
        if not dist:
            return None, []
        parts, tags = [], []
        if scat:
            parts.append(_scatter_grads([gb[k][1] for k in scat], [shape_of[k] for k in scat],
                                        [axis_of[k] for k in scat]))
            tags.append((recv, scat))
        if swap:
            for k in swap:
                sums[k] = _sum_blocks(gb[k][0], recv[k], shape_of[k], axis_of[k], chip, "sum_" + k)
            parts.append(_swap_with_sibling([sums[k] for k in swap]))
            tags.append((other, swap))
        if extra is not None:
            parts.append(extra[0])
            tags.append((extra[1], extra[2]))
        return _join(parts), tags

    def take(ex_tags, got):
        ex, tags = ex_tags
        if ex is not None:
            for (dst, names), (o0, o1) in zip(tags, ex.cuts):
                dst.update(zip(names, got[o0:o1]))

    tril = jnp.tril(jnp.ones((CHUNK, CHUNK), dtype=bool))
    wsm = jnp.where(tril[None], ws["gmlp_w_s"], 0.0)
    wsm_b = wsm.astype(BF16)
    wsmt_b = wsm.transpose(0, 2, 1).astype(BF16)
    bias = jnp.repeat(ws["gmlp_b_s"].T, GMLP_HEAD_DIM, axis=1)

    if dist:
        names = ("ffn1_w_in", "ffn1_w_out")
        wb.update(zip(names, _run_exchange(gather(names), "gather_ffn1")))
    names = ("mix_w_in", "ssm_glu_w", "up_a", "up_b", "mix_w_out")
    (x1, xh1, rstd1, x0b, h1), got = _ffn_fwd(x0, wb["ffn1_w_in"], wb["ffn1_w_out"], row(ws["ln1_g"]),
                                              row(ws["ln1_b"]), tm, "ffn1_fwd", gather(names))
    wb.update(zip(names, got))
    sp = _s5_setup(ws["ssm_lambda_re"], ws["ssm_lambda_im"], ws["ssm_log_dt"], ws["ssm_b_re"],
                   ws["ssm_b_im"], ws["ssm_c_re"], ws["ssm_c_im"], ws["ssm_d"], wb["ssm_glu_w"],
                   ws["ssm_glu_b"], tb)
    names = ("ffn2_w_out",)
    (x1b, za, zuv, gab), got = _mixin_fwd(x1, wb["mix_w_in"], tm, gather(names))
    wb.update(zip(names, got))
    names = ("ffn2_w_in",)
    (s5o, y2p, carries), got = _s5_fwd(za, sp, bsz, seq, tb, gather(names))
    wb.update(zip(names, got))
    names = ("ple_w_gate", "ple_w_proj")
    (gm,), got = _gmlp_fwd(zuv, row(ws["gmlp_ln_g"]), row(ws["gmlp_ln_b"]), wsm_b, bias, gather(names))
    wb.update(zip(names, got))
    (x2, xh2, rstd2, x2b), _ = _mixout_fwd(x1, s5o, gm, gab, wb["up_a"], wb["up_b"], wb["mix_w_out"],
                                           row(ws["ln2_g"]), row(ws["ln2_b"]), tm)
    (x3, xh3, rstd3, _, h2), _ = _ffn_fwd(x2, wb["ffn2_w_in"], wb["ffn2_w_out"], row(ws["ln3_g"]),
                                          row(ws["ln3_b"]), tm, "ffn2_fwd")
    (dx3, x3b, pb, dq, de, loss_rows), _ = _ple_loss(x3, p0, tg, wb["ple_w_gate"], wb["ple_w_proj"], tm)
    gb["ple_w_gate"], _ = _tn_matmul(x3b, dq, "dw_ple_gate", 1024, 1024)
    gb["ple_w_proj"], _ = _tn_matmul(pb, de, "dw_ple_proj", 256, 1024)
    et = exchange(scat=("ple_w_gate", "ple_w_proj"))
    (dx2, dh2, a2, df2, gs["ln3_g"], gs["ln3_b"]), got = _ffn_bwd(
        dx3, xh3, rstd3, h2, wb["ffn2_w_in"], wb["ffn2_w_out"], row(ws["ln3_g"]), tm, "ffn2_bwd", et[0])
    take(et, got)
    gb["ffn2_w_out"], _ = _tn_matmul(a2, df2, "dw_ffn2_out", 1408, 1024)
    et = exchange(scat=("ffn2_w_out",))
    gb["ffn2_w_in"], got = _tn_matmul(x2b, dh2, "dw_ffn2_in", 1024, 1408, bg=et[0])
    take(et, got)
    et = exchange(swap=("ple_w_gate", "ple_w_proj", "ffn2_w_out"))
    (dx1a, dmx, mb, dya, dyb, ds5, dgm, dgab, gs["ln2_g"], gs["ln2_b"]), got = _mixout_bwd(
        dx2, xh2, rstd2, s5o, gm, gab, wb["up_a"], wb["up_b"], wb["mix_w_out"], row(ws["ln2_g"]), tm, et[0])
    take(et, got)
    gb["mix_w_out"], _ = _tn_matmul(mb, dmx, "dw_mix_out", 1024, 1024)
    gb["up_a"], _ = _tn_matmul(s5o, dya, "dw_up_a", 512, 1024)
    gb["up_b"], _ = _tn_matmul(gm, dyb, "dw_up_b", 512, 1024)
    et = exchange(scat=("ffn2_w_in",))
    (dza, dmr, dmi, dnr, dni, da, ddsk, dgw, dgb), got = _s5_bwd(za, y2p, ds5, carries, sp, bsz, seq, tb, et[0])
    take(et, got)
    gb["ssm_glu_w"] = (dgw, dgw.astype(BF16))
    et = exchange(scat=("mix_w_out", "up_a"), swap=("ffn2_w_in",))
    (dzuv, dws, dbias, gs["gmlp_ln_g"], gs["gmlp_ln_b"]), got = _gmlp_bwd(
        zuv, dgm, row(ws["gmlp_ln_g"]), row(ws["gmlp_ln_b"]), wsm_b, wsmt_b, bias, et[0])
    take(et, got)
    et = exchange(scat=("up_b", "ssm_glu_w"))
    (dx1,), got = _mixin_bwd(dx1a, dza, dzuv, dgab, wb["mix_w_in"], tm, et[0])
    take(et, got)
    g_mi, _ = _tn_matmul(x1b, dza, "dw_mix_in_a", 1024, 512, 0, 3584)
    g_mi, _ = _tn_matmul(x1b, dzuv, "dw_mix_in_uv", 1024, 512, 1, 3584, g_mi)
    et = exchange(swap=("mix_w_out", "up_a", "up_b", "ssm_glu_w"))
    gb["mix_w_in"], got = _tn_matmul(x1b, dgab, "dw_mix_in_g", 1024, 512, 3, 3584, g_mi, bg=et[0])
    take(et, got)
    et = exchange(scat=("mix_w_in",))
    (dx0, dh1, a1, df1, gs["ln1_g"], gs["ln1_b"]), got = _ffn_bwd(
        dx1, xh1, rstd1, h1, wb["ffn1_w_in"], wb["ffn1_w_out"], row(ws["ln1_g"]), tm, "ffn1_bwd", et[0])
    take(et, got)

    d_abr = da[0].sum(axis=0).reshape(SSM_GROUPS, SSM_STATE)
    d_abi = da[1].sum(axis=0).reshape(SSM_GROUPS, SSM_STATE)
    _, vjp = jax.vjp(_s5_discretise, ws["ssm_lambda_re"], ws["ssm_lambda_im"], ws["ssm_log_dt"],
                     ws["ssm_b_re"], ws["ssm_b_im"])
    (gs["ssm_lambda_re"], gs["ssm_lambda_im"], gs["ssm_log_dt"], gs["ssm_b_re"], gs["ssm_b_im"]) = vjp(
        (d_abr, d_abi, _block_diag_in_t(dmr), _block_diag_in_t(dmi)))
    gs["ssm_c_re"] = _block_diag_out_t(dnr)
    gs["ssm_c_im"] = _block_diag_out_t(dni)
    gs["ssm_d"] = ddsk
    gs["ssm_glu_b"] = dgb
    gs["gmlp_w_s"] = dws
    gs["gmlp_b_s"] = dbias.reshape(CHUNK, GMLP_HEADS, GMLP_HEAD_DIM).sum(axis=-1).T
    gs = {k: gs[k].reshape(SMALL_2D[k]) for k in SMALL}
    grad_x = dx0.reshape(bsz, seq, D_MODEL)

    if not dist:
        gb["ffn1_w_out"], _ = _tn_matmul(a1, df1, "dw_ffn1_out", 1408, 1024)
        gb["ffn1_w_in"], _ = _tn_matmul(x0b, dh1, "dw_ffn1_in", 1024, 1408)
        return loss_rows, grad_x, gb, gs, sums, other, gathered
    small = SMALL + ("loss_rows",)
    et = exchange(swap=("mix_w_in",), extra=(_gather_small([gs[k] for k in SMALL] + [loss_rows]), gathered, small))
    gb["ffn1_w_out"], got = _tn_matmul(a1, df1, "dw_ffn1_out", 1408, 1024, bg=et[0])
    take(et, got)
    piece = [LAST_PIECE % q for q in range(LAST_PIECES)]
    behind = [dict(scat=("ffn1_w_out",)), dict(scat=(piece[0],)),
              dict(scat=(piece[1],), swap=("ffn1_w_out",)), dict(scat=(piece[2],), swap=(piece[0], piece[1]))]
    for q in range(LAST_PIECES):
        et = exchange(**behind[q])
        gb[piece[q]], got = _tn_matmul(x0b, dh1, "dw_" + piece[q], D_MODEL // LAST_PIECES, 1408, bg=et[0],
                                       a_cols=(q, 1))
        take(et, got)
    et = exchange(scat=(piece[3],), swap=(piece[2],))
    take(et, _run_exchange(et[0], "scatter_last"))
    et = exchange(swap=(piece[3],))
    take(et, _run_exchange(et[0], "swap_last"))
    return loss_rows, grad_x, gb, gs, sums, other, gathered


def _adamw(w, g, m, v):
    m = ADAM_B1 * m + (1.0 - ADAM_B1) * g
    v = ADAM_B2 * v + (1.0 - ADAM_B2) * (g * g)
    m_hat = m / ADAM_C1
    v_hat = v / ADAM_C2
    delta = -ADAM_LR * (m_hat / (jnp.sqrt(v_hat) + ADAM_EPS) + ADAM_WD * w)
    return delta, m, v


def _sum_blocks(part, recv, shape, axis, chip, name):
    r, c = shape
    rb = r // 8

    def body(chip_ref, p_ref, r_ref, o_ref):
        o_ref[...] = (p_ref[...] + r_ref[0].astype(F32) + r_ref[1].astype(F32) + r_ref[2].astype(F32))

    if axis == 0:
        own = pl.BlockSpec((rb, c), lambda i, k: (k[0] * 8 + i, 0))
    else:
        own = pl.BlockSpec((rb, c), lambda i, k: (i, k[0]))
    grid_spec = pltpu.PrefetchScalarGridSpec(
        num_scalar_prefetch=1, grid=(8,),
        in_specs=[own, pl.BlockSpec((3, rb, c), lambda i, k: (0, i, 0))],
        out_specs=pl.BlockSpec((rb, c), lambda i, k: (i, 0)))
    return pl.pallas_call(body, name=name, out_shape=SDS((r, c), F32), grid_spec=grid_spec,
                          compiler_params=_params(("parallel",)))(chip, part, recv)


def _adam_big(w, ga, gb, m, v, name, piece=0, prev=None):
    r, c = w.shape
    pr = ga.shape[0]
    steps = 8 if pr == r else 2
    rb = pr // steps
    off = piece * steps

    def body(w_ref, ga_ref, gb_ref, m_ref, v_ref, *rest):
        g_ref, d_ref, nm_ref, nv_ref = rest[-4:]
        g = ga_ref[...] + gb_ref[...]
        g_ref[...] = g
        d_ref[...], nm_ref[...], nv_ref[...] = _adamw(w_ref[...], g, m_ref[...], v_ref[...])

    whole = pl.BlockSpec((rb, c), lambda i: (i + off, 0))
    part = pl.BlockSpec((rb, c), lambda i: (i, 0))
    in_specs = [whole, part, part, whole, whole]
    args = [w, ga, gb, m, v]
    aliases = {}
    if prev is not None:
        in_specs += [pl.BlockSpec(memory_space=pl.ANY)] * 4
        args += list(prev)
        aliases = {5: 0, 6: 1, 7: 2, 8: 3}
    return pl.pallas_call(
        body, name=name, grid=(steps,), out_shape=tuple(SDS((r, c), F32) for _ in range(4)),
        in_specs=in_specs, out_specs=(whole,) * 4, input_output_aliases=aliases,
        compiler_params=_params(("parallel",)),
    )(*args)


def _adam_small(ws, gathered, ms, vs):
    n = len(ws)

    def body(*refs):
        w_refs, g_refs, m_refs, v_refs = refs[:n], refs[n:2 * n], refs[2 * n:3 * n], refs[3 * n:4 * n]
        outs = refs[4 * n:]
        for i in range(n):
            g = g_refs[i][0]
            for d in range(1, N_DEV):
                g = g + g_refs[i][d]
            delta, nm, nv = _adamw(w_refs[i][...], g, m_refs[i][...], v_refs[i][...])
            outs[i][...] = g
            outs[n + i][...] = delta
            outs[2 * n + i][...] = nm
            outs[3 * n + i][...] = nv

    vmem = pl.BlockSpec(memory_space=pltpu.VMEM)
    shapes = [w.shape for w in ws]
    return pl.pallas_call(
        body, name="adam_small", out_shape=tuple(SDS(s, F32) for s in shapes * 4),
        in_specs=[vmem] * (4 * n), out_specs=tuple([vmem] * (4 * n)),
        compiler_params=pltpu.CompilerParams(vmem_limit_bytes=VMEM_LIMIT_BYTES),
    )(*ws, *gathered, *ms, *vs)


def _sum_loss(gathered):
    def body(g_ref, o_ref):
        tot = g_ref[0]
        for d in range(1, N_DEV):
            tot = tot + g_ref[d]
        o_ref[...] = (0.5 / D_MODEL) * jnp.sum(tot, axis=1, keepdims=True)

    vmem = pl.BlockSpec(memory_space=pltpu.VMEM)
    return pl.pallas_call(body, name="sum_loss", out_shape=SDS((1, 1), F32), in_specs=[vmem],
                          out_specs=vmem)(gathered)


def kernel(x, p, ffn1_w_in, ffn1_w_out, ln1_g, ln1_b, mix_w_in, ssm_lambda_re, ssm_lambda_im, ssm_log_dt, ssm_b_re, ssm_b_im, ssm_c_re, ssm_c_im, ssm_d, ssm_glu_w, ssm_glu_b, gmlp_ln_g, gmlp_ln_b, gmlp_w_s, gmlp_b_s, up_a, up_b, mix_w_out, ln2_g, ln2_b, ffn2_w_in, ffn2_w_out, ln3_g, ln3_b, ple_w_proj, ple_w_gate, loss_target, m_ffn1_w_in, m_ffn1_w_out, m_ln1_g, m_ln1_b, m_mix_w_in, m_ssm_lambda_re, m_ssm_lambda_im, m_ssm_log_dt, m_ssm_b_re, m_ssm_b_im, m_ssm_c_re, m_ssm_c_im, m_ssm_d, m_ssm_glu_w, m_ssm_glu_b, m_gmlp_ln_g, m_gmlp_ln_b, m_gmlp_w_s, m_gmlp_b_s, m_up_a, m_up_b, m_mix_w_out, m_ln2_g, m_ln2_b, m_ffn2_w_in, m_ffn2_w_out, m_ln3_g, m_ln3_b, m_ple_w_proj, m_ple_w_gate, v_ffn1_w_in, v_ffn1_w_out, v_ln1_g, v_ln1_b, v_mix_w_in, v_ssm_lambda_re, v_ssm_lambda_im, v_ssm_log_dt, v_ssm_b_re, v_ssm_b_im, v_ssm_c_re, v_ssm_c_im, v_ssm_d, v_ssm_glu_w, v_ssm_glu_b, v_gmlp_ln_g, v_gmlp_ln_b, v_gmlp_w_s, v_gmlp_b_s, v_up_a, v_up_b, v_mix_w_out, v_ln2_g, v_ln2_b, v_ffn2_w_in, v_ffn2_w_out, v_ln3_g, v_ln3_b, v_ple_w_proj, v_ple_w_gate):
    given = dict(locals())
    order = ("ffn1_w_in", "ffn1_w_out", "ln1_g", "ln1_b", "mix_w_in", "ssm_lambda_re", "ssm_lambda_im",
             "ssm_log_dt", "ssm_b_re", "ssm_b_im", "ssm_c_re", "ssm_c_im", "ssm_d", "ssm_glu_w", "ssm_glu_b",
             "gmlp_ln_g", "gmlp_ln_b", "gmlp_w_s", "gmlp_b_s", "up_a", "up_b", "mix_w_out", "ln2_g", "ln2_b",
             "ffn2_w_in", "ffn2_w_out", "ln3_g", "ln3_b", "ple_w_proj", "ple_w_gate")
    assert set(order) == set(BIG + SMALL)

    shard = {k: given[k][0] for k in BIG}
    shard_b = {k: shard[k].astype(BF16) for k in BIG}
    ws = {k: given[k][0] for k in SMALL}
    loss_rows, grad_x, gb, gs, sums, other, gathered = _local_step(x, given["p"][0], loss_target, {}, ws, shard_b)

    out = {}
    for k in BIG:
        moments = (given["m_" + k][0], given["v_" + k][0])
        if k != "ffn1_w_in":
            out[k] = _adam_big(shard[k], sums[k], other[k], *moments, "adam_" + k)
            continue
        for q in range(LAST_PIECES):
            kq = LAST_PIECE % q
            out[k] = _adam_big(shard[k], sums[kq], other[kq], *moments, "adam_" + kq, q, out.get(k))

    res = _adam_small([given[k].reshape(SMALL_2D[k]) for k in SMALL], [gathered[k] for k in SMALL],
                      [given["m_" + k].reshape(SMALL_2D[k]) for k in SMALL],
                      [given["v_" + k].reshape(SMALL_2D[k]) for k in SMALL])
    ns = len(SMALL)
    for i, k in enumerate(SMALL):
        out[k] = tuple(res[j * ns + i].reshape(given[k].shape) for j in range(4))
    loss = _sum_loss(gathered["loss_rows"]).reshape(())

    lead = lambda k, j: out[k][j][None] if k in BIG else out[k][j]
    return (loss, grad_x, *[lead(k, 0) for k in order], *[lead(k, 1) for k in order],
            *[lead(k, 2) for k in order], *[lead(k, 3) for k in order])
```

```python
import math

import jax
import jax.numpy as jnp
from jax import lax
from jax.experimental import pallas as pl
from jax.experimental.pallas import tpu as pltpu

F32 = jnp.float32
BF16 = jnp.bfloat16
MESH = pl.DeviceIdType.MESH
SDS = jax.ShapeDtypeStruct

D_MODEL = 1024
D_FF = 2816
D_SSM = 512
D_GMLP = 512
SSM_GROUPS = 32
SSM_GROUP_CH = 16
SSM_STATE = 64
SSM_LANES = SSM_GROUPS * SSM_STATE
GMLP_HEADS = 8
GMLP_HEAD_DIM = 64
CHUNK = 128
PLE_DIM = 256
LN_EPS = 1e-5
ALPHA = 2.0 ** 0.25

ADAM_LR = 0.001
ADAM_B1 = 0.9
ADAM_B2 = 0.999
ADAM_EPS = 1e-08
ADAM_WD = 0.01
ADAM_STEP = 10
ADAM_C1 = 1.0 - ADAM_B1 ** ADAM_STEP
ADAM_C2 = 1.0 - ADAM_B2 ** ADAM_STEP

N_DEV = 8
VMEM_LIMIT_BYTES = 56 * 1024 * 1024
FFN_COLS = 1408
S5_BLOCKS = 4
S5_BLOCK_IN = D_SSM // S5_BLOCKS
S5_BLOCK_ST = SSM_LANES // S5_BLOCKS
SCAN_LANES = 512
LAST_PIECES = 4
LAST_PIECE = "ffn1_w_in_q%d"
_G0 = math.sqrt(2.0 / math.pi)
_G1 = 0.044715


def _dot(a, b):
    return jnp.dot(a, b, preferred_element_type=F32)


def _dot_nt(a, b):
    return lax.dot_general(a, b, (((1,), (1,)), ((), ())), preferred_element_type=F32)


def _dot_tn(a, b):
    return lax.dot_general(a, b, (((0,), (0,)), ((), ())), preferred_element_type=F32)


def _sigmoid(x):
    return 1.0 / (1.0 + jnp.exp(-x))


def _gelu(x):
    t = jnp.tanh(_G0 * (x + _G1 * x * x * x))
    return 0.5 * x * (1.0 + t)


def _gelu_grad(x):
    t = jnp.tanh(_G0 * (x + _G1 * x * x * x))
    return 0.5 * (1.0 + t) + 0.5 * x * (1.0 - t * t) * _G0 * (1.0 + 3.0 * _G1 * x * x)


def _ln_fwd(r, g, b):
    mu = jnp.mean(r, axis=-1, keepdims=True)
    d = r - mu
    var = jnp.mean(d * d, axis=-1, keepdims=True)
    rstd = lax.rsqrt(var + LN_EPS)
    xh = d * rstd
    return xh * g + b, xh, rstd


def _ln_bwd(dy, xh, rstd, g):
    dxh = dy * g
    m1 = jnp.mean(dxh, axis=-1, keepdims=True)
    m2 = jnp.mean(dxh * xh, axis=-1, keepdims=True)
    return rstd * (dxh - m1 - xh * m2)


def _resident(shape):
    nd = len(shape)
    return pl.BlockSpec(shape, lambda *_: (0,) * nd, pipeline_mode=pl.Buffered(1))


def _fixed(shape):
    nd = len(shape)
    return pl.BlockSpec(shape, lambda *_: (0,) * nd)


def _rows(tm, cols):
    return pl.BlockSpec((tm, cols), lambda i: (i, 0))


def _params(sem):
    return pltpu.CompilerParams(dimension_semantics=sem, vmem_limit_bytes=VMEM_LIMIT_BYTES)


class _Exchange:
    def __init__(self, args, out_shape, sems, start, finish):
        self.args, self.out_shape, self.sems = list(args), list(out_shape), list(sems)
        self.start, self.finish = start, finish
        self.cuts = [(0, len(self.out_shape))]


def _call(body, name, grid, in_specs, out_specs, out_shape, args, scratch=(), sem=None, bg=None, aliases=None):
    aliases = {} if aliases is None else aliases
    if bg is None:
        res = pl.pallas_call(body, name=name, grid=grid, out_shape=tuple(out_shape), in_specs=list(in_specs),
                             out_specs=tuple(out_specs), scratch_shapes=list(scratch),
                             input_output_aliases=aliases, compiler_params=_params(sem))(*args)
        return tuple(res), ()
    n_in, n_out, n_bi, n_bo, n_sc = len(args), len(out_shape), len(bg.args), len(bg.out_shape), len(scratch)

    def wrapped(*refs):
        ins = refs[:n_in]
        b_ins = refs[n_in:n_in + n_bi]
        outs = refs[n_in + n_bi:n_in + n_bi + n_out]
        b_outs = refs[n_in + n_bi + n_out:n_in + n_bi + n_out + n_bo]
        rest = refs[n_in + n_bi + n_out + n_bo:]
        scr, b_sems = rest[:n_sc], rest[n_sc:]
        first = pl.program_id(0) == 0
        last = pl.program_id(0) == grid[0] - 1
        for ax in range(1, len(grid)):
            first = jnp.logical_and(first, pl.program_id(ax) == 0)
            last = jnp.logical_and(last, pl.program_id(ax) == grid[ax] - 1)

        @pl.when(first)
        def _():
            bg.start(b_ins, b_outs, b_sems)

        body(*ins, *outs, *scr)

        @pl.when(last)
        def _():
            bg.finish(b_ins, b_outs, b_sems)

    any_spec = pl.BlockSpec(memory_space=pl.ANY)
    res = pl.pallas_call(
        wrapped, name=name, grid=grid, out_shape=tuple(out_shape) + tuple(bg.out_shape),
        in_specs=list(in_specs) + [any_spec] * n_bi, out_specs=tuple(out_specs) + (any_spec,) * n_bo,
        scratch_shapes=list(scratch) + list(bg.sems), input_output_aliases=aliases,
        compiler_params=_params(tuple("arbitrary" for _ in grid)))(*args, *bg.args)
    return tuple(res[:n_out]), tuple(res[n_out:])


def _run_exchange(ex, name):
    n_i, n_o = len(ex.args), len(ex.out_shape)

    def body(*refs):
        ins, outs, sems = refs[:n_i], refs[n_i:n_i + n_o], refs[n_i + n_o:]
        ex.start(ins, outs, sems)
        ex.finish(ins, outs, sems)

    any_spec = pl.BlockSpec(memory_space=pl.ANY)
    return tuple(pl.pallas_call(body, name=name, out_shape=tuple(ex.out_shape), in_specs=[any_spec] * n_i,
                                out_specs=(any_spec,) * n_o, scratch_shapes=list(ex.sems))(*ex.args))


def _join(exchanges):
    cuts = []
    a = o = q = 0
    for e in exchanges:
        cuts.append((a, a + len(e.args), o, o + len(e.out_shape), q, q + len(e.sems)))
        a, o, q = cuts[-1][1], cuts[-1][3], cuts[-1][5]

    def start(ins, outs, sems):
        for e, (a0, a1, o0, o1, q0, q1) in zip(exchanges, cuts):
            e.start(ins[a0:a1], outs[o0:o1], sems[q0:q1])

    def finish(ins, outs, sems):
        for e, (a0, a1, o0, o1, q0, q1) in zip(exchanges, cuts):
            e.finish(ins[a0:a1], outs[o0:o1], sems[q0:q1])

    joined = _Exchange(sum((e.args for e in exchanges), []), sum((e.out_shape for e in exchanges), []),
                       sum((e.sems for e in exchanges), []), start, finish)
    joined.cuts = [(c[2], c[3]) for c in cuts]
    return joined


def _ffn_fwd(x, w_in, w_out, g, b, tm, name, bg=None):
    t = x.shape[0]
    nch = D_FF // FFN_COLS

    def body(x_ref, win_ref, wout_ref, g_ref, b_ref, xn_ref, xh_ref, rstd_ref, xb_ref, h_ref):
        xv = x_ref[...]
        xb = xv.astype(BF16)
        xb_ref[...] = xb
        f = jnp.zeros((tm, D_MODEL), F32)
        for k in range(nch):
            cg = slice(k * FFN_COLS, (k + 1) * FFN_COLS)
            cu = slice(D_FF + k * FFN_COLS, D_FF + (k + 1) * FFN_COLS)
            hg = _dot(xb, win_ref[:, cg])
            hu = _dot(xb, win_ref[:, cu])
            h_ref[:, cg] = hg.astype(BF16)
            h_ref[:, cu] = hu.astype(BF16)
            a = hg * _sigmoid(hg) * hu
            f = f + _dot(a.astype(BF16), wout_ref[cg, :])
        y, xh, rstd = _ln_fwd(ALPHA * xv + 0.5 * f, g_ref[...], b_ref[...])
        xn_ref[...] = y
        xh_ref[...] = xh
        rstd_ref[...] = rstd

    return _call(
        body, name, (t // tm,),
        [_rows(tm, D_MODEL), _resident((D_MODEL, 2 * D_FF)), _resident((D_FF, D_MODEL)),
         _fixed((1, D_MODEL)), _fixed((1, D_MODEL))],
        (_rows(tm, D_MODEL), _rows(tm, D_MODEL), _rows(tm, 1), _rows(tm, D_MODEL), _rows(tm, 2 * D_FF)),
        (SDS((t, D_MODEL), F32), SDS((t, D_MODEL), F32), SDS((t, 1), F32), SDS((t, D_MODEL), BF16),
         SDS((t, 2 * D_FF), BF16)),
        (x, w_in, w_out, g, b), sem=("parallel",), bg=bg)


def _ffn_bwd(dxn, xh, rstd, h, w_in, w_out, g, tm, name, bg=None):
    t = dxn.shape[0]
    nch = D_FF // FFN_COLS

    def body(dxn_ref, xh_ref, rstd_ref, h_ref, win_ref, wout_ref, g_ref,
             dx_ref, dh_ref, a_ref, df_ref, dg_ref, db_ref):
        @pl.when(pl.program_id(0) == 0)
        def _():
            dg_ref[...] = jnp.zeros_like(dg_ref)
            db_ref[...] = jnp.zeros_like(db_ref)

        dy = dxn_ref[...]
        xhv = xh_ref[...]
        dr = _ln_bwd(dy, xhv, rstd_ref[...], g_ref[...])
        dg_ref[...] += jnp.sum(dy * xhv, axis=0, keepdims=True)
        db_ref[...] += jnp.sum(dy, axis=0, keepdims=True)
        df = (0.5 * dr).astype(BF16)
        df_ref[...] = df
        dx = ALPHA * dr
        for k in range(nch):
            cg = slice(k * FFN_COLS, (k + 1) * FFN_COLS)
            cu = slice(D_FF + k * FFN_COLS, D_FF + (k + 1) * FFN_COLS)
            hg = h_ref[:, cg].astype(F32)
            hu = h_ref[:, cu].astype(F32)
            sg = _sigmoid(hg)
            silu = hg * sg
            a_ref[:, cg] = (silu * hu).astype(BF16)
            da = _dot_nt(df, wout_ref[cg, :])
            dhu = (da * silu).astype(BF16)
            dhg = (da * hu * (sg * (1.0 + hg * (1.0 - sg)))).astype(BF16)
            dh_ref[:, cg] = dhg
            dh_ref[:, cu] = dhu
            dx = dx + _dot_nt(dhg, win_ref[:, cg]) + _dot_nt(dhu, win_ref[:, cu])
        dx_ref[...] = dx

    return _call(
        body, name, (t // tm,),
        [_rows(tm, D_MODEL), _rows(tm, D_MODEL), _rows(tm, 1), _rows(tm, 2 * D_FF),
         _resident((D_MODEL, 2 * D_FF)), _resident((D_FF, D_MODEL)), _fixed((1, D_MODEL))],
        (_rows(tm, D_MODEL), _rows(tm, 2 * D_FF), _rows(tm, D_FF), _rows(tm, D_MODEL),
         _fixed((1, D_MODEL)), _fixed((1, D_MODEL))),
        (SDS((t, D_MODEL), F32), SDS((t, 2 * D_FF), BF16), SDS((t, D_FF), BF16), SDS((t, D_MODEL), BF16),
         SDS((1, D_MODEL), F32), SDS((1, D_MODEL), F32)),
        (dxn, xh, rstd, h, w_in, w_out, g), sem=("arbitrary",), bg=bg)


def _tn_matmul(a, b, name, bm, bn, col_block=0, total_cols=None, prev=None, bg=None, a_cols=None):
    t, m = a.shape
    a_first = 0
    if a_cols is not None:
        a_first, m = a_cols[0], a_cols[1] * bm
    n = b.shape[1]
    total_cols = n if total_cols is None else total_cols
    bk = min(512, t)
    nk = t // bk
    n_in = 2 if prev is None else 4

    def body(*refs):
        a_ref, b_ref = refs[0], refs[1]
        o_ref, ob_ref = refs[n_in], refs[n_in + 1]
        k = pl.program_id(2)

        @pl.when(k == 0)
        def _():
            o_ref[...] = jnp.zeros_like(o_ref)

        o_ref[...] += _dot_tn(a_ref[...], b_ref[...])

        @pl.when(k == nk - 1)
        def _():
            ob_ref[...] = o_ref[...].astype(BF16)

    in_specs = [pl.BlockSpec((bk, bm), lambda i, j, k: (k, i + a_first)),
                pl.BlockSpec((bk, bn), lambda i, j, k: (k, j))]
    args = [a, b]
    aliases = {}
    if prev is not None:
        in_specs += [pl.BlockSpec(memory_space=pl.ANY), pl.BlockSpec(memory_space=pl.ANY)]
        args += list(prev)
        aliases = {2: 0, 3: 1}
    out_spec = pl.BlockSpec((bm, bn), lambda i, j, k: (i, j + col_block))
    return _call(body, name, (m // bm, n // bn, nk), in_specs, (out_spec, out_spec),
                 (SDS((m, total_cols), F32), SDS((m, total_cols), BF16)), args,
                 sem=("parallel", "parallel", "arbitrary"), bg=bg, aliases=aliases)


def _mixin_fwd(x1, w, tm, bg=None):
    t = x1.shape[0]

    def body(x_ref, w_ref, xb_ref, za_ref, zuv_ref, gab_ref):
        xb = x_ref[...].astype(BF16)
        xb_ref[...] = xb
        za_ref[...] = _dot(xb, w_ref[:, 0:512]).astype(BF16)
        zuv_ref[...] = _dot(xb, w_ref[:, 512:1536]).astype(BF16)
        gab_ref[...] = _dot(xb, w_ref[:, 1536:3584]).astype(BF16)

    return _call(
        body, "mixin_fwd", (t // tm,),
        [_rows(tm, D_MODEL), _resident((D_MODEL, 3584))],
        (_rows(tm, D_MODEL), _rows(tm, 512), _rows(tm, 1024), _rows(tm, 2048)),
        (SDS((t, D_MODEL), BF16), SDS((t, 512), BF16), SDS((t, 1024), BF16), SDS((t, 2048), BF16)),
        (x1, w), sem=("parallel",), bg=bg)


def _mixin_bwd(dx1a, dza, dzuv, dgab, w, tm, bg=None):
    t = dx1a.shape[0]

    def body(d_ref, dza_ref, dzuv_ref, dgab_ref, w_ref, dx_ref):
        dx_ref[...] = (d_ref[...] + _dot_nt(dza_ref[...], w_ref[:, 0:512])
                       + _dot_nt(dzuv_ref[...], w_ref[:, 512:1536])
                       + _dot_nt(dgab_ref[...], w_ref[:, 1536:3584]))

    return _call(
        body, "mixin_bwd", (t // tm,),
        [_rows(tm, D_MODEL), _rows(tm, 512), _rows(tm, 1024), _rows(tm, 2048), _resident((D_MODEL, 3584))],
        (_rows(tm, D_MODEL),), (SDS((t, D_MODEL), F32),),
        (dx1a, dza, dzuv, dgab, w), sem=("parallel",), bg=bg)


def _scan_fwd(hr_ref, hi_ref, a_ref, ap_ref, carry_ref, seg, cin_ref):
    for lc in range(SSM_LANES // SCAN_LANES):
        ls = slice(lc * SCAN_LANES, (lc + 1) * SCAN_LANES)
        a_r = jnp.broadcast_to(a_ref[0:1, ls], (8, SCAN_LANES))
        a_i = jnp.broadcast_to(a_ref[1:2, ls], (8, SCAN_LANES))

        def step(j, hc, ls=ls, a_r=a_r, a_i=a_i):
            h_r, h_i = hc
            rows = pl.ds(pl.multiple_of(j * 8, 8), 8)
            n_r = a_r * h_r - a_i * h_i + hr_ref[rows, ls]
            n_i = a_r * h_i + a_i * h_r + hi_ref[rows, ls]
            hr_ref[rows, ls] = n_r
            hi_ref[rows, ls] = n_i
            return n_r, n_i

        zero = jnp.zeros((8, SCAN_LANES), F32)
        f_r, f_i = lax.fori_loop(0, seg, step, (zero, zero))
        c_r = carry_ref[0:1, ls]
        c_i = carry_ref[1:2, ls]
        p_r = ap_ref[0:1, ls]
        p_i = ap_ref[1:2, ls]
        rows_r, rows_i = [], []
        for s in range(8):
            rows_r.append(c_r)
            rows_i.append(c_i)
            c_r, c_i = (f_r[s:s + 1] + p_r * c_r - p_i * c_i,
                        f_i[s:s + 1] + p_r * c_i + p_i * c_r)
        carry_ref[0:1, ls] = c_r
        carry_ref[1:2, ls] = c_i
        cin_r = jnp.concatenate(rows_r, axis=0)
        cin_i = jnp.concatenate(rows_i, axis=0)
        if cin_ref is not None:
            cin_ref[0, :, ls] = cin_r
            cin_ref[1, :, ls] = cin_i

        def fix(j, cc, ls=ls, a_r=a_r, a_i=a_i):
            c_r, c_i = cc
            c_r, c_i = a_r * c_r - a_i * c_i, a_r * c_i + a_i * c_r
            rows = pl.ds(pl.multiple_of(j * 8, 8), 8)
            hr_ref[rows, ls] = hr_ref[rows, ls] + c_r
            hi_ref[rows, ls] = hi_ref[rows, ls] + c_i
            return c_r, c_i

        lax.fori_loop(0, seg, fix, (cin_r, cin_i))


def _scan_bwd(gr_ref, gi_ref, hr_ref, hi_ref, cin_ref, a_ref, ap_ref, rcarry_ref, da_ref, seg):
    for lc in range(SSM_LANES // SCAN_LANES):
        ls = slice(lc * SCAN_LANES, (lc + 1) * SCAN_LANES)
        a_r = jnp.broadcast_to(a_ref[0:1, ls], (8, SCAN_LANES))
        a_i = jnp.broadcast_to(a_ref[1:2, ls], (8, SCAN_LANES))

        def step(t, gc, ls=ls, a_r=a_r, a_i=a_i):
            g_r, g_i = gc
            rows = pl.ds(pl.multiple_of((seg - 1 - t) * 8, 8), 8)
            n_r = gr_ref[rows, ls] + a_r * g_r + a_i * g_i
            n_i = gi_ref[rows, ls] + a_r * g_i - a_i * g_r
            gr_ref[rows, ls] = n_r
            gi_ref[rows, ls] = n_i
            return n_r, n_i

        zero = jnp.zeros((8, SCAN_LANES), F32)
        f_r, f_i = lax.fori_loop(0, seg, step, (zero, zero))
        c_r = rcarry_ref[0:1, ls]
        c_i = rcarry_ref[1:2, ls]
        p_r = ap_ref[0:1, ls]
        p_i = ap_ref[1:2, ls]
        rows_r, rows_i = [None] * 8, [None] * 8
        for s in range(7, -1, -1):
            rows_r[s] = c_r
            rows_i[s] = c_i
            c_r, c_i = (f_r[s:s + 1] + p_r * c_r + p_i * c_i,
                        f_i[s:s + 1] + p_r * c_i - p_i * c_r)
        rcarry_ref[0:1, ls] = c_r
        rcarry_ref[1:2, ls] = c_i
        cin_r = jnp.concatenate(rows_r, axis=0)
        cin_i = jnp.concatenate(rows_i, axis=0)

        def fix_row(j_rows, hp_r, hp_i, cc, ls=ls, a_r=a_r, a_i=a_i):
            c_r, c_i, acc_r, acc_i = cc
            c_r, c_i = a_r * c_r + a_i * c_i, a_r * c_i - a_i * c_r
            g_r = gr_ref[j_rows, ls] + c_r
            g_i = gi_ref[j_rows, ls] + c_i
            gr_ref[j_rows, ls] = g_r
            gi_ref[j_rows, ls] = g_i
            acc_r = acc_r + g_r * hp_r + g_i * hp_i
            acc_i = acc_i + g_i * hp_r - g_r * hp_i
            return c_r, c_i, acc_r, acc_i

        def fix(t, cc, ls=ls, fix_row=fix_row):
            j = seg - 1 - t
            rows = pl.ds(pl.multiple_of(j * 8, 8), 8)
            prev = pl.ds(pl.multiple_of((j - 1) * 8, 8), 8)
            return fix_row(rows, hr_ref[prev, ls], hi_ref[prev, ls], cc)

        cc = lax.fori_loop(0, seg - 1, fix, (cin_r, cin_i, zero, zero))
        _, _, acc_r, acc_i = fix_row(pl.ds(0, 8), cin_ref[0, :, ls], cin_ref[1, :, ls], cc)
        da_ref[0, :, ls] += acc_r
        da_ref[1, :, ls] += acc_i


def _s5_fwd(za, sp, bsz, seq, tb, bg=None):
    nb = seq // tb
    seg = tb // 8
    t = bsz * seq

    def body(za_ref, perm_ref, permt_ref, mre_ref, mim_ref, nre_ref, nim_ref, a_ref, ap_ref,
             dsk_ref, gw_ref, gb_ref, out_ref, y2_ref, car_ref, hr_ref, hi_ref, carry_ref):
        @pl.when(pl.program_id(1) == 0)
        def _():
            carry_ref[...] = jnp.zeros_like(carry_ref)

        car_ref[0] = carry_ref[...]
        up = _dot(perm_ref[...], za_ref[...])
        upb = up.astype(BF16)
        for bb in range(S5_BLOCKS):
            ub = upb[:, bb * S5_BLOCK_IN:(bb + 1) * S5_BLOCK_IN]
            st = slice(bb * S5_BLOCK_ST, (bb + 1) * S5_BLOCK_ST)
            hr_ref[:, st] = _dot(ub, mre_ref[bb])
            hi_ref[:, st] = _dot(ub, mim_ref[bb])
        _scan_fwd(hr_ref, hi_ref, a_ref, ap_ref, carry_ref, seg, None)
        ys = []
        for bb in range(S5_BLOCKS):
            st = slice(bb * S5_BLOCK_ST, (bb + 1) * S5_BLOCK_ST)
            ys.append(_dot(hr_ref[:, st].astype(BF16), nre_ref[bb])
                      - _dot(hi_ref[:, st].astype(BF16), nim_ref[bb]))
        y2 = jnp.concatenate(ys, axis=1) + dsk_ref[...] * up
        y2_ref[...] = y2
        y3 = _gelu(y2)
        gl = _dot(y3.astype(BF16), gw_ref[...]) + gb_ref[...]
        oa = y3 * _sigmoid(gl)
        out_ref[...] = _dot(permt_ref[...], oa.astype(BF16)).astype(BF16)

    blk = pl.BlockSpec((tb, D_SSM), lambda b, j: (b * nb + j, 0))
    m_shape = (S5_BLOCKS, S5_BLOCK_IN, S5_BLOCK_ST)
    n_shape = (S5_BLOCKS, S5_BLOCK_ST, S5_BLOCK_IN)
    return _call(
        body, "s5_fwd", (bsz, nb),
        [blk, _fixed((tb, tb)), _fixed((tb, tb)), _fixed(m_shape), _fixed(m_shape), _fixed(n_shape),
         _fixed(n_shape), _fixed((2, SSM_LANES)), _fixed((2, SSM_LANES)), _fixed((1, D_SSM)),
         _fixed((D_SSM, D_SSM)), _fixed((1, D_SSM))],
        (blk, blk, pl.BlockSpec((1, 2, SSM_LANES), lambda b, j: (b * nb + j, 0, 0))),
        (SDS((t, D_SSM), BF16), SDS((t, D_SSM), F32), SDS((bsz * nb, 2, SSM_LANES), F32)),
        (za, sp["perm"], sp["permt"], sp["mre"], sp["mim"], sp["nre"], sp["nim"], sp["a"], sp["ap"],
         sp["dskip"], sp["glu_w"], sp["glu_b"]),
        scratch=[pltpu.VMEM((tb, SSM_LANES), F32), pltpu.VMEM((tb, SSM_LANES), F32),
                 pltpu.VMEM((2, SSM_LANES), F32)],
        sem=("arbitrary", "arbitrary"), bg=bg)


def _s5_bwd(za, y2p, doa, carries, sp, bsz, seq, tb, bg=None):
    nb = seq // tb
    seg = tb // 8
    t = bsz * seq

    def body(za_ref, y2_ref, doa_ref, car_ref, perm_ref, permt_ref, mre_ref, mim_ref, mtre_ref, mtim_ref,
             nre_ref, nim_ref, ntre_ref, ntim_ref, a_ref, ap_ref, dsk_ref, gw_ref, gwt_ref, gb_ref,
             dza_ref, dmr_ref, dmi_ref, dnr_ref, dni_ref, da_ref, ddsk_ref, dgw_ref, dgb_ref,
             hr_ref, hi_ref, gr_ref, gi_ref, cin_ref, carry_ref, rcarry_ref):
        first = jnp.logical_and(pl.program_id(0) == 0, pl.program_id(1) == 0)

        @pl.when(first)
        def _():
            for r in (dmr_ref, dmi_ref, dnr_ref, dni_ref, da_ref, ddsk_ref, dgw_ref, dgb_ref):
                r[...] = jnp.zeros_like(r)

        @pl.when(pl.program_id(1) == 0)
        def _():
            rcarry_ref[...] = jnp.zeros_like(rcarry_ref)

        carry_ref[...] = car_ref[0]
        perm = perm_ref[...]
        up = _dot(perm, za_ref[...])
        upb = up.astype(BF16)
        for bb in range(S5_BLOCKS):
            ub = upb[:, bb * S5_BLOCK_IN:(bb + 1) * S5_BLOCK_IN]
            st = slice(bb * S5_BLOCK_ST, (bb + 1) * S5_BLOCK_ST)
            hr_ref[:, st] = _dot(ub, mre_ref[bb])
            hi_ref[:, st] = _dot(ub, mim_ref[bb])
        _scan_fwd(hr_ref, hi_ref, a_ref, ap_ref, carry_ref, seg, cin_ref)

        y2 = y2_ref[...]
        y3 = _gelu(y2)
        y3b = y3.astype(BF16)
        sg = _sigmoid(_dot(y3b, gw_ref[...]) + gb_ref[...])
        d0 = doa_ref[...]
        d_hi = d0.astype(BF16)
        d1 = d0 - d_hi.astype(F32)
        d_mid = d1.astype(BF16)
        d_lo = (d1 - d_mid.astype(F32)).astype(BF16)
        doap = _dot(perm, d_hi) + _dot(perm, d_mid) + _dot(perm, d_lo)
        dgl = doap * y3 * sg * (1.0 - sg)
        dglb = dgl.astype(BF16)
        dy3 = doap * sg + _dot(dglb, gwt_ref[...])
        dgw_ref[...] += _dot_tn(y3b, dglb)
        dgb_ref[...] += jnp.sum(dgl, axis=0, keepdims=True)
        dy2 = dy3 * _gelu_grad(y2)
        ddsk_ref[...] += jnp.sum(dy2 * up, axis=0, keepdims=True)
        dyb = dy2.astype(BF16)
        for bb in range(S5_BLOCKS):
            dyc = dyb[:, bb * S5_BLOCK_IN:(bb + 1) * S5_BLOCK_IN]
            st = slice(bb * S5_BLOCK_ST, (bb + 1) * S5_BLOCK_ST)
            gr_ref[:, st] = _dot(dyc, ntre_ref[bb])
            gi_ref[:, st] = -_dot(dyc, ntim_ref[bb])
            dnr_ref[bb] += _dot_tn(hr_ref[:, st].astype(BF16), dyc)
            dni_ref[bb] += -_dot_tn(hi_ref[:, st].astype(BF16), dyc)
        _scan_bwd(gr_ref, gi_ref, hr_ref, hi_ref, cin_ref, a_ref, ap_ref, rcarry_ref, da_ref, seg)
        dus = []
        for bb in range(S5_BLOCKS):
            st = slice(bb * S5_BLOCK_ST, (bb + 1) * S5_BLOCK_ST)
            grb = gr_ref[:, st].astype(BF16)
            gib = gi_ref[:, st].astype(BF16)
            dus.append(_dot(grb, mtre_ref[bb]) + _dot(gib, mtim_ref[bb]))
            ub = upb[:, bb * S5_BLOCK_IN:(bb + 1) * S5_BLOCK_IN]
            dmr_ref[bb] += _dot_tn(ub, grb)
            dmi_ref[bb] += _dot_tn(ub, gib)
        du = jnp.concatenate(dus, axis=1) + dy2 * dsk_ref[...]
        dza_ref[...] = _dot(permt_ref[...], du.astype(BF16)).astype(BF16)

    def rev(b, j):
        return (b * nb + (nb - 1 - j), 0)

    blk = pl.BlockSpec((tb, D_SSM), rev)
    m_shape = (S5_BLOCKS, S5_BLOCK_IN, S5_BLOCK_ST)
    n_shape = (S5_BLOCKS, S5_BLOCK_ST, S5_BLOCK_IN)
    return _call(
        body, "s5_bwd", (bsz, nb),
        [blk, blk, blk, pl.BlockSpec((1, 2, SSM_LANES), lambda b, j: (b * nb + (nb - 1 - j), 0, 0)),
         _fixed((tb, tb)), _fixed((tb, tb)), _fixed(m_shape), _fixed(m_shape), _fixed(n_shape), _fixed(n_shape),
         _fixed(n_shape), _fixed(n_shape), _fixed(m_shape), _fixed(m_shape),
         _fixed((2, SSM_LANES)), _fixed((2, SSM_LANES)), _fixed((1, D_SSM)),
         _fixed((D_SSM, D_SSM)), _fixed((D_SSM, D_SSM)), _fixed((1, D_SSM))],
        (blk, _fixed(m_shape), _fixed(m_shape), _fixed(n_shape), _fixed(n_shape),
         _fixed((2, 8, SSM_LANES)), _fixed((1, D_SSM)), _fixed((D_SSM, D_SSM)), _fixed((1, D_SSM))),
        (SDS((t, D_SSM), BF16), SDS(m_shape, F32), SDS(m_shape, F32), SDS(n_shape, F32), SDS(n_shape, F32),
         SDS((2, 8, SSM_LANES), F32), SDS((1, D_SSM), F32), SDS((D_SSM, D_SSM), F32), SDS((1, D_SSM), F32)),
        (za, y2p, doa, carries, sp["perm"], sp["permt"], sp["mre"], sp["mim"], sp["mtre"], sp["mtim"],
         sp["nre"], sp["nim"], sp["ntre"], sp["ntim"], sp["a"], sp["ap"], sp["dskip"], sp["glu_w"],
         sp["glu_wt"], sp["glu_b"]),
        scratch=[pltpu.VMEM((tb, SSM_LANES), F32), pltpu.VMEM((tb, SSM_LANES), F32),
                 pltpu.VMEM((tb, SSM_LANES), F32), pltpu.VMEM((tb, SSM_LANES), F32),
                 pltpu.VMEM((2, 8, SSM_LANES), F32), pltpu.VMEM((2, SSM_LANES), F32),
                 pltpu.VMEM((2, SSM_LANES), F32)],
        sem=("arbitrary", "arbitrary"), bg=bg)


def _gmlp_spatial(ws_ref, vb):
    lane = lax.broadcasted_iota(jnp.int32, (CHUNK, 128), 1)
    parts = []
    for j in range(GMLP_HEADS // 2):
        vp = vb[:, 128 * j:128 * (j + 1)]
        parts.append(jnp.where(lane < GMLP_HEAD_DIM, _dot(ws_ref[2 * j], vp), _dot(ws_ref[2 * j + 1], vp)))
    return jnp.concatenate(parts, axis=1)


def _gmlp_fwd(zuv, ln_g, ln_b, wsm, bias, bg=None):
    t = zuv.shape[0]

    def body(z_ref, g_ref, b_ref, ws_ref, bias_ref, out_ref):
        u = _gelu(z_ref[:, 0:D_GMLP].astype(F32))
        v0 = _gelu(z_ref[:, D_GMLP:2 * D_GMLP].astype(F32))
        v, _, _ = _ln_fwd(v0, g_ref[...], b_ref[...])
        s = _gmlp_spatial(ws_ref, v.astype(BF16)) + bias_ref[...]
        out_ref[...] = (u * s).astype(BF16)

    return _call(
        body, "gmlp_fwd", (t // CHUNK,),
        [_rows(CHUNK, 2 * D_GMLP), _fixed((1, D_GMLP)), _fixed((1, D_GMLP)),
         _fixed((GMLP_HEADS, CHUNK, CHUNK)), _fixed((CHUNK, D_GMLP))],
        (_rows(CHUNK, D_GMLP),), (SDS((t, D_GMLP), BF16),),
        (zuv, ln_g, ln_b, wsm, bias), sem=("parallel",), bg=bg)


def _gmlp_bwd(zuv, dgm, ln_g, ln_b, wsm, wsmt, bias, bg=None):
    t = zuv.shape[0]

    def body(z_ref, d_ref, g_ref, b_ref, ws_ref, wst_ref, bias_ref,
             dz_ref, dws_ref, dbias_ref, dg_ref, db_ref):
        @pl.when(pl.program_id(0) == 0)
        def _():
            for r in (dws_ref, dbias_ref, dg_ref, db_ref):
                r[...] = jnp.zeros_like(r)

        zu = z_ref[:, 0:D_GMLP].astype(F32)
        zv = z_ref[:, D_GMLP:2 * D_GMLP].astype(F32)
        u = _gelu(zu)
        v0 = _gelu(zv)
        gam = g_ref[...]
        v, vhat, rstd = _ln_fwd(v0, gam, b_ref[...])
        vb = v.astype(BF16)
        s = _gmlp_spatial(ws_ref, vb) + bias_ref[...]
        d = d_ref[...]
        dz_ref[:, 0:D_GMLP] = (d * s * _gelu_grad(zu)).astype(BF16)
        ds = d * u
        dbias_ref[...] += ds
        dsb = ds.astype(BF16)
        lane = lax.broadcasted_iota(jnp.int32, (CHUNK, 128), 1)
        tril = (lax.broadcasted_iota(jnp.int32, (CHUNK, CHUNK), 0)
                >= lax.broadcasted_iota(jnp.int32, (CHUNK, CHUNK), 1))
        zero_b = jnp.zeros((CHUNK, 128), BF16)
        parts = []
        for j in range(GMLP_HEADS // 2):
            dsp = dsb[:, 128 * j:128 * (j + 1)]
            vp = vb[:, 128 * j:128 * (j + 1)]
            parts.append(jnp.where(lane < GMLP_HEAD_DIM, _dot(wst_ref[2 * j], dsp),
                                   _dot(wst_ref[2 * j + 1], dsp)))
            lo = jnp.where(lane < GMLP_HEAD_DIM, dsp, zero_b)
            hi = jnp.where(lane < GMLP_HEAD_DIM, zero_b, dsp)
            dws_ref[2 * j] += jnp.where(tril, _dot_nt(lo, vp), 0.0)
            dws_ref[2 * j + 1] += jnp.where(tril, _dot_nt(hi, vp), 0.0)
        dv = jnp.concatenate(parts, axis=1)
        dg_ref[...] += jnp.sum(dv * vhat, axis=0, keepdims=True)
        db_ref[...] += jnp.sum(dv, axis=0, keepdims=True)
        dz_ref[:, D_GMLP:2 * D_GMLP] = (_ln_bwd(dv, vhat, rstd, gam) * _gelu_grad(zv)).astype(BF16)

    return _call(
        body, "gmlp_bwd", (t // CHUNK,),
        [_rows(CHUNK, 2 * D_GMLP), _rows(CHUNK, D_GMLP), _fixed((1, D_GMLP)), _fixed((1, D_GMLP)),
         _fixed((GMLP_HEADS, CHUNK, CHUNK)), _fixed((GMLP_HEADS, CHUNK, CHUNK)), _fixed((CHUNK, D_GMLP))],
        (_rows(CHUNK, 2 * D_GMLP), _fixed((GMLP_HEADS, CHUNK, CHUNK)), _fixed((CHUNK, D_GMLP)),
         _fixed((1, D_GMLP)), _fixed((1, D_GMLP))),
        (SDS((t, 2 * D_GMLP), BF16), SDS((GMLP_HEADS, CHUNK, CHUNK), F32), SDS((CHUNK, D_GMLP), F32),
         SDS((1, D_GMLP), F32), SDS((1, D_GMLP), F32)),
        (zuv, dgm, ln_g, ln_b, wsm, wsmt, bias), sem=("arbitrary",), bg=bg)


def _mixout_fwd(x1, s5o, gm, gab, ua, ub, wmo, g, b, tm, bg=None):
    t = x1.shape[0]

    def body(x_ref, s_ref, m_ref, gab_ref, ua_ref, ub_ref, wmo_ref, g_ref, b_ref,
             xn_ref, xh_ref, rstd_ref, xb_ref):
        ya = _dot(s_ref[...], ua_ref[...])
        yb = _dot(m_ref[...], ub_ref[...])
        mix = (_sigmoid(gab_ref[:, 0:D_MODEL].astype(F32)) * ya
               + _sigmoid(gab_ref[:, D_MODEL:2 * D_MODEL].astype(F32)) * yb)
        r = ALPHA * x_ref[...] + _dot(mix.astype(BF16), wmo_ref[...])
        y, xh, rstd = _ln_fwd(r, g_ref[...], b_ref[...])
        xn_ref[...] = y
        xh_ref[...] = xh
        rstd_ref[...] = rstd
        xb_ref[...] = y.astype(BF16)

    return _call(
        body, "mixout_fwd", (t // tm,),
        [_rows(tm, D_MODEL), _rows(tm, D_SSM), _rows(tm, D_GMLP), _rows(tm, 2 * D_MODEL),
         _resident((D_SSM, D_MODEL)), _resident((D_GMLP, D_MODEL)), _resident((D_MODEL, D_MODEL)),
         _fixed((1, D_MODEL)), _fixed((1, D_MODEL))],
        (_rows(tm, D_MODEL), _rows(tm, D_MODEL), _rows(tm, 1), _rows(tm, D_MODEL)),
        (SDS((t, D_MODEL), F32), SDS((t, D_MODEL), F32), SDS((t, 1), F32), SDS((t, D_MODEL), BF16)),
        (x1, s5o, gm, gab, ua, ub, wmo, g, b), sem=("parallel",), bg=bg)


def _mixout_bwd(dx2, xh, rstd, s5o, gm, gab, ua, ub, wmo, g, tm, bg=None):
    t = dx2.shape[0]

    def body(d_ref, xh_ref, rstd_ref, s_ref, m_ref, gab_ref, ua_ref, ub_ref, wmo_ref, g_ref,
             dx1_ref, dmx_ref, mb_ref, dya_ref, dyb_ref, ds5_ref, dgm_ref, dgab_ref, dg_ref, db_ref):
        @pl.when(pl.program_id(0) == 0)
        def _():
            dg_ref[...] = jnp.zeros_like(dg_ref)
            db_ref[...] = jnp.zeros_like(db_ref)

        dy = d_ref[...]
        xhv = xh_ref[...]
        dr = _ln_bwd(dy, xhv, rstd_ref[...], g_ref[...])
        dg_ref[...] += jnp.sum(dy * xhv, axis=0, keepdims=True)
        db_ref[...] += jnp.sum(dy, axis=0, keepdims=True)
        dx1_ref[...] = ALPHA * dr
        drb = dr.astype(BF16)
        dmx_ref[...] = drb
        dm = _dot_nt(drb, wmo_ref[...])
        ya = _dot(s_ref[...], ua_ref[...])
        yb = _dot(m_ref[...], ub_ref[...])
        sa = _sigmoid(gab_ref[:, 0:D_MODEL].astype(F32))
        sb = _sigmoid(gab_ref[:, D_MODEL:2 * D_MODEL].astype(F32))
        mb_ref[...] = (sa * ya + sb * yb).astype(BF16)
        dya = (dm * sa).astype(BF16)
        dyb = (dm * sb).astype(BF16)
        dya_ref[...] = dya
        dyb_ref[...] = dyb
        dgab_ref[:, 0:D_MODEL] = (dm * ya * sa * (1.0 - sa)).astype(BF16)
        dgab_ref[:, D_MODEL:2 * D_MODEL] = (dm * yb * sb * (1.0 - sb)).astype(BF16)
        ds5_ref[...] = _dot_nt(dya, ua_ref[...])
        dgm_ref[...] = _dot_nt(dyb, ub_ref[...])

    return _call(
        body, "mixout_bwd", (t // tm,),
        [_rows(tm, D_MODEL), _rows(tm, D_MODEL), _rows(tm, 1), _rows(tm, D_SSM), _rows(tm, D_GMLP),
         _rows(tm, 2 * D_MODEL), _resident((D_SSM, D_MODEL)), _resident((D_GMLP, D_MODEL)),
         _resident((D_MODEL, D_MODEL)), _fixed((1, D_MODEL))],
        (_rows(tm, D_MODEL), _rows(tm, D_MODEL), _rows(tm, D_MODEL), _rows(tm, D_MODEL),
         _rows(tm, D_MODEL), _rows(tm, D_SSM), _rows(tm, D_GMLP), _rows(tm, 2 * D_MODEL),
         _fixed((1, D_MODEL)), _fixed((1, D_MODEL))),
        (SDS((t, D_MODEL), F32), SDS((t, D_MODEL), BF16), SDS((t, D_MODEL), BF16),
         SDS((t, D_MODEL), BF16), SDS((t, D_MODEL), BF16), SDS((t, D_SSM), F32),
         SDS((t, D_GMLP), F32), SDS((t, 2 * D_MODEL), BF16),
         SDS((1, D_MODEL), F32), SDS((1, D_MODEL), F32)),
        (dx2, xh, rstd, s5o, gm, gab, ua, ub, wmo, g), sem=("arbitrary",), bg=bg)


def _ple_loss(x3, p, tgt, wpg, wpp, tm, bg=None):
    t = x3.shape[0]

    def body(x_ref, p_ref, t_ref, wpg_ref, wpp_ref, dx_ref, xb_ref, pb_ref, dq_ref, de_ref, loss_ref):
        @pl.when(pl.program_id(0) == 0)
        def _():
            loss_ref[...] = jnp.zeros_like(loss_ref)

        x3v = x_ref[...]
        xb = x3v.astype(BF16)
        pb = p_ref[...].astype(BF16)
        xb_ref[...] = xb
        pb_ref[...] = pb
        s = _sigmoid(_dot(xb, wpg_ref[...]))
        e = _dot(pb, wpp_ref[...])
        diff = x3v + s * e - t_ref[...]
        loss_ref[...] += jnp.sum(diff * diff, axis=0, keepdims=True)
        dout = diff * (1.0 / D_MODEL)
        de_ref[...] = (dout * s).astype(BF16)
        dq = (dout * e * s * (1.0 - s)).astype(BF16)
        dq_ref[...] = dq
        dx_ref[...] = dout + _dot_nt(dq, wpg_ref[...])

    return _call(
        body, "ple_loss", (t // tm,),
        [_rows(tm, D_MODEL), _rows(tm, PLE_DIM), _rows(tm, D_MODEL),
         _resident((D_MODEL, D_MODEL)), _resident((PLE_DIM, D_MODEL))],
        (_rows(tm, D_MODEL), _rows(tm, D_MODEL), _rows(tm, PLE_DIM), _rows(tm, D_MODEL),
         _rows(tm, D_MODEL), _fixed((1, D_MODEL))),
        (SDS((t, D_MODEL), F32), SDS((t, D_MODEL), BF16), SDS((t, PLE_DIM), BF16),
         SDS((t, D_MODEL), BF16), SDS((t, D_MODEL), BF16), SDS((1, D_MODEL), F32)),
        (x3, p, tgt, wpg, wpp), sem=("arbitrary",), bg=bg)


def _s5_discretise(lre, lim, log_dt, bre, bim):
    dt = jnp.exp(log_dt)[:, None]
    mag = jnp.exp(lre * dt)
    abr = mag * jnp.cos(lim * dt)
    abi = mag * jnp.sin(lim * dt)
    nr = abr - 1.0
    ni = abi
    den = lre * lre + lim * lim
    cr = ((nr * lre + ni * lim) / den)[..., None]
    ci = ((ni * lre - nr * lim) / den)[..., None]
    return abr, abi, cr * bre - ci * bim, cr * bim + ci * bre


def _block_diag_in(bb):
    v = bb.reshape(S5_BLOCKS, 8, SSM_STATE, SSM_GROUP_CH).transpose(0, 1, 3, 2)
    return jnp.einsum("bgip,gh->bgihp", v, jnp.eye(8, dtype=bb.dtype)).reshape(
        S5_BLOCKS, S5_BLOCK_IN, S5_BLOCK_ST)


def _block_diag_in_t(dm):
    v = dm.reshape(S5_BLOCKS, 8, SSM_GROUP_CH, 8, SSM_STATE)
    d = jnp.einsum("bgihp,gh->bgip", v, jnp.eye(8, dtype=dm.dtype))
    return d.transpose(0, 1, 3, 2).reshape(SSM_GROUPS, SSM_STATE, SSM_GROUP_CH)


def _block_diag_out(cc):
    v = cc.reshape(S5_BLOCKS, 8, SSM_GROUP_CH, SSM_STATE)
    return jnp.einsum("bgip,gh->bgphi", v, jnp.eye(8, dtype=cc.dtype)).reshape(
        S5_BLOCKS, S5_BLOCK_ST, S5_BLOCK_IN)


def _block_diag_out_t(dn):
    v = dn.reshape(S5_BLOCKS, 8, SSM_STATE, 8, SSM_GROUP_CH)
    d = jnp.einsum("bgphi,gh->bgip", v, jnp.eye(8, dtype=dn.dtype))
    return d.reshape(SSM_GROUPS, SSM_GROUP_CH, SSM_STATE)


def _s5_setup(lre, lim, log_dt, bre, bim, cre, cim, d_skip, glu_w, glu_b, tb):
    seg = tb // 8
    abr, abi, bbr, bbi = _s5_discretise(lre, lim, log_dt, bre, bim)
    pr, pi = abr, abi
    for _ in range(int(math.log2(seg))):
        pr, pi = pr * pr - pi * pi, 2.0 * pr * pi
    rows = jnp.arange(tb)
    src = (rows % 8) * seg + rows // 8
    perm = (src[:, None] == jnp.arange(tb)[None, :]).astype(BF16)
    mre = _block_diag_in(bbr)
    mim = _block_diag_in(bbi)
    nre = _block_diag_out(cre)
    nim = _block_diag_out(cim)
    return {
        "perm": perm, "permt": perm.T,
        "mre": mre.astype(BF16), "mim": mim.astype(BF16),
        "mtre": mre.transpose(0, 2, 1).astype(BF16), "mtim": mim.transpose(0, 2, 1).astype(BF16),
        "nre": nre.astype(BF16), "nim": nim.astype(BF16),
        "ntre": nre.transpose(0, 2, 1).astype(BF16), "ntim": nim.transpose(0, 2, 1).astype(BF16),
        "a": jnp.stack([abr.reshape(-1), abi.reshape(-1)]),
        "ap": jnp.stack([pr.reshape(-1), pi.reshape(-1)]),
        "dskip": d_skip.reshape(1, D_SSM), "glu_w": glu_w, "glu_wt": glu_w.T,
        "glu_b": glu_b.reshape(1, D_SSM),
    }


BIG = ("ffn1_w_in", "ffn1_w_out", "mix_w_in", "ssm_glu_w", "up_a", "up_b", "mix_w_out",
       "ffn2_w_in", "ffn2_w_out", "ple_w_proj", "ple_w_gate")
BIG_AXIS = {"ffn1_w_in": 1, "ffn1_w_out": 0, "mix_w_in": 1, "ssm_glu_w": 0, "up_a": 1, "up_b": 1,
            "mix_w_out": 0, "ffn2_w_in": 1, "ffn2_w_out": 0, "ple_w_proj": 1, "ple_w_gate": 0}
SMALL = ("ln1_g", "ln1_b", "ssm_lambda_re", "ssm_lambda_im", "ssm_log_dt", "ssm_b_re", "ssm_b_im",
         "ssm_c_re", "ssm_c_im", "ssm_d", "ssm_glu_b", "gmlp_ln_g", "gmlp_ln_b", "gmlp_w_s",
         "gmlp_b_s", "ln2_g", "ln2_b", "ln3_g", "ln3_b")
SMALL_2D = {"ln1_g": (1, 1024), "ln1_b": (1, 1024), "ssm_lambda_re": (32, 64), "ssm_lambda_im": (32, 64),
            "ssm_log_dt": (1, 32), "ssm_b_re": (32, 1024), "ssm_b_im": (32, 1024), "ssm_c_re": (32, 1024),
            "ssm_c_im": (32, 1024), "ssm_d": (1, 512), "ssm_glu_b": (1, 512), "gmlp_ln_g": (1, 512),
            "gmlp_ln_b": (1, 512), "gmlp_w_s": (1024, 128), "gmlp_b_s": (8, 128), "ln2_g": (1, 1024),
            "ln2_b": (1, 1024), "ln3_g": (1, 1024), "ln3_b": (1, 1024)}


def _place():
    return lax.axis_index("x"), lax.axis_index("y"), lax.axis_index("c")


def _other_chips(x, y):
    return [(1 - x, y), (x, 1 - y), (1 - x, 1 - y)]


def _window(ref, shard_shape, axis, chip, half):
    r, c = shard_shape
    hr = r // 2
    if axis == 0:
        if half is None:
            return ref.at[pl.ds(chip * r, r), :]
        return ref.at[pl.ds(chip * r + half * hr, hr), :]
    if half is None:
        return ref.at[:, pl.ds(chip * c, c)]
    return ref.at[pl.ds(half * hr, hr), pl.ds(chip * c, c)]


def _gather_weights(shards, axes):
    n = len(shards)
    shapes = [s.shape for s in shards]
    full = [(4 * r, c) if ax == 0 else (r, 4 * c) for (r, c), ax in zip(shapes, axes)]

    def remote(sems, i, k, src, dst, to):
        return pltpu.make_async_remote_copy(src_ref=src, dst_ref=dst, send_sem=sems[0].at[6 * i + k],
                                            recv_sem=sems[1].at[6 * i + k], device_id=to, device_id_type=MESH)

    def own_copies(ins, outs, sems):
        x, y, c = _place()
        me = 2 * x + y
        cps = []
        for i in range(n):
            hr = shapes[i][0] // 2
            mine = ins[i].at[pl.ds(c * hr, hr), :]
            for j, (cx, cy) in enumerate(_other_chips(x, y)):
                cps.append(remote(sems, i, j, mine, _window(outs[i], shapes[i], axes[i], me, c), (cx, cy, c)))
        local = [pltpu.make_async_copy(ins[i], _window(outs[i], shapes[i], axes[i], me, None), sems[2].at[i])
                 for i in range(n)]
        return cps, local

    def start(ins, outs, sems):
        cps, local = own_copies(ins, outs, sems)
        for cp in local + cps:
            cp.start()

    def finish(ins, outs, sems):
        x, y, c = _place()
        sibling = (x, y, 1 - c)
        passed = []
        for j, (cx, cy) in enumerate(_other_chips(x, y)):
            for i in range(n):
                w = _window(outs[i], shapes[i], axes[i], 2 * cx + cy, c)
                remote(sems, i, j, w, w, (cx, cy, c)).wait_recv()
                cp = remote(sems, i, 3 + j, w, w, sibling)
                cp.start()
                passed.append(cp)
        for j, (cx, cy) in enumerate(_other_chips(x, y)):
            for i in range(n):
                w = _window(outs[i], shapes[i], axes[i], 2 * cx + cy, 1 - c)
                remote(sems, i, 3 + j, w, w, sibling).wait_recv()
        cps, local = own_copies(ins, outs, sems)
        for cp in cps + passed:
            cp.wait_send()
        for cp in local:
            cp.wait()

    return _Exchange(shards, [SDS(f, BF16) for f in full],
                     [pltpu.SemaphoreType.DMA((6 * n,)), pltpu.SemaphoreType.DMA((6 * n,)),
                      pltpu.SemaphoreType.DMA((n,))], start, finish)


def _scatter_grads(parts, shapes, axes):
    n = len(parts)

    def copies(ins, outs, sems):
        x, y, c = _place()
        return [pltpu.make_async_remote_copy(
            src_ref=_window(ins[i], shapes[i], axes[i], 2 * cx + cy, None), dst_ref=outs[i].at[j],
            send_sem=sems[0].at[3 * i + j], recv_sem=sems[1].at[3 * i + j],
            device_id=(cx, cy, c), device_id_type=MESH)
            for i in range(n) for j, (cx, cy) in enumerate(_other_chips(x, y))]

    def start(ins, outs, sems):
        for cp in copies(ins, outs, sems):
            cp.start()

    def finish(ins, outs, sems):
        for cp in copies(ins, outs, sems):
            cp.wait()

    return _Exchange(parts, [SDS((3,) + tuple(s), BF16) for s in shapes],
                     [pltpu.SemaphoreType.DMA((3 * n,)), pltpu.SemaphoreType.DMA((3 * n,))], start, finish)


def _swap_halves(parts, shapes, axes):
    n = len(parts)

    def copies(ins, outs, sems):
        x, y, c = _place()
        cps = []
        for i in range(n):
            r, _ = shapes[i]
            hr = r // 2
            if axes[i] == 0:
                cps += [pltpu.make_async_remote_copy(
                    src_ref=ins[i].at[pl.ds(k * r + (1 - c) * hr, hr), :], dst_ref=outs[i].at[k],
                    send_sem=sems[0].at[i], recv_sem=sems[1].at[i], device_id=(x, y, 1 - c),
                    device_id_type=MESH) for k in range(4)]
            else:
                cps.append(pltpu.make_async_remote_copy(
                    src_ref=ins[i].at[pl.ds((1 - c) * hr, hr), :], dst_ref=outs[i],
                    send_sem=sems[0].at[i], recv_sem=sems[1].at[i], device_id=(x, y, 1 - c),
                    device_id_type=MESH))
        return cps

    def start(ins, outs, sems):
        for cp in copies(ins, outs, sems):
            cp.start()

    def finish(ins, outs, sems):
        x, y, c = _place()
        for i in range(n):
            pltpu.make_async_remote_copy(src_ref=outs[i], dst_ref=outs[i], send_sem=sems[0].at[i],
                                         recv_sem=sems[1].at[i], device_id=(x, y, 1 - c),
                                         device_id_type=MESH).wait()

    out = [SDS((4, r // 2, c), BF16) if ax == 0 else SDS((r // 2, 4 * c), BF16)
           for (r, c), ax in zip(shapes, axes)]
    return _Exchange(parts, out, [pltpu.SemaphoreType.DMA((n,)), pltpu.SemaphoreType.DMA((n,))], start, finish)


def _scatter_halves(pres, shapes):
    n = len(pres)

    def copies(ins, outs, sems):
        x, y, c = _place()
        return [pltpu.make_async_remote_copy(
            src_ref=ins[i].at[1 + j], dst_ref=outs[i].at[j], send_sem=sems[0].at[3 * i + j],
            recv_sem=sems[1].at[3 * i + j], device_id=(cx, cy, c), device_id_type=MESH)
            for i in range(n) for j, (cx, cy) in enumerate(_other_chips(x, y))]

    def start(ins, outs, sems):
        for cp in copies(ins, outs, sems):
            cp.start()

    def finish(ins, outs, sems):
        for cp in copies(ins, outs, sems):
            cp.wait()

    return _Exchange(pres, [SDS((3, r // 2, c), BF16) for r, c in shapes],
                     [pltpu.SemaphoreType.DMA((3 * n,)), pltpu.SemaphoreType.DMA((3 * n,))], start, finish)


def _swap_with_sibling(arrs):
    n = len(arrs)

    def copies(ins, outs, sems):
        x, y, c = _place()
        return [pltpu.make_async_remote_copy(src_ref=ins[i], dst_ref=outs[i], send_sem=sems[0].at[i],
                                             recv_sem=sems[1].at[i], device_id=(x, y, 1 - c),
                                             device_id_type=MESH) for i in range(n)]

    def start(ins, outs, sems):
        for cp in copies(ins, outs, sems):
            cp.start()

    def finish(ins, outs, sems):
        for cp in copies(ins, outs, sems):
            cp.wait()

    return _Exchange(arrs, [SDS(a.shape, a.dtype) for a in arrs],
                     [pltpu.SemaphoreType.DMA((n,)), pltpu.SemaphoreType.DMA((n,))], start, finish)


def _gather_small(arrs):
    n = len(arrs)

    def copy(sems, outs, i, k, block, to, src=None):
        px, py, pc = block
        dst = outs[i].at[4 * px + 2 * py + pc]
        return pltpu.make_async_remote_copy(
            src_ref=dst if src is None else src, dst_ref=dst, send_sem=sems[0].at[7 * i + k],
            recv_sem=sems[1].at[7 * i + k], device_id=to, device_id_type=MESH)

    def own_copies(ins, outs, sems):
        x, y, c = _place()
        cps = []
        for i in range(n):
            cps.append(copy(sems, outs, i, 0, (x, y, c), (x, y, 1 - c), src=ins[i]))
            for j, (cx, cy) in enumerate(_other_chips(x, y)):
                cps.append(copy(sems, outs, i, 1 + j, (x, y, c), (cx, cy, c), src=ins[i]))
        local = [pltpu.make_async_copy(ins[i], outs[i].at[4 * x + 2 * y + c], sems[2].at[i]) for i in range(n)]
        return cps, local

    def start(ins, outs, sems):
        cps, local = own_copies(ins, outs, sems)
        for cp in local + cps:
            cp.start()

    def finish(ins, outs, sems):
        x, y, c = _place()
        passed = []
        for j, (cx, cy) in enumerate(_other_chips(x, y)):
            for i in range(n):
                copy(sems, outs, i, 1 + j, (cx, cy, c), (x, y, c)).wait_recv()
                cp = copy(sems, outs, i, 4 + j, (cx, cy, c), (x, y, 1 - c))
                cp.start()
                passed.append(cp)
        for i in range(n):
            copy(sems, outs, i, 0, (x, y, 1 - c), (x, y, c)).wait_recv()
            for j, (cx, cy) in enumerate(_other_chips(x, y)):
                copy(sems, outs, i, 4 + j, (cx, cy, 1 - c), (x, y, c)).wait_recv()
        cps, local = own_copies(ins, outs, sems)
        for cp in cps + passed:
            cp.wait_send()
        for cp in local:
            cp.wait()

    return _Exchange(arrs, [SDS((N_DEV,) + a.shape, F32) for a in arrs],
                     [pltpu.SemaphoreType.DMA((7 * n,)), pltpu.SemaphoreType.DMA((7 * n,)),
                      pltpu.SemaphoreType.DMA((n,))], start, finish)


def _local_step(x, p, tgt, wb, ws, shards=None):
    bsz, seq, _ = x.shape
    t = bsz * seq
    tm = min(256, t)
    tb = min(256, seq)
    x0 = x.reshape(t, D_MODEL)
    p0 = p.reshape(t, PLE_DIM)
    tg = tgt.reshape(t, D_MODEL)
    row = lambda v: v.reshape(1, -1)
    dist = shards is not None
    wb = dict(wb)
    recv, sums, other, gathered = {}, {}, {}, {}
    gb = {}
    gs = {}
    shape_of, axis_of = {}, {}
    chip = None
    if dist:
        shape_of = {k: tuple(shards[k].shape) for k in BIG}
        axis_of = dict(BIG_AXIS)
        for q in range(LAST_PIECES):
            shape_of[LAST_PIECE % q] = (D_MODEL // LAST_PIECES, shape_of["ffn1_w_in"][1])
            axis_of[LAST_PIECE % q] = 1
        xi, yi, ci = _place()
        chip = (2 * xi + yi).astype(jnp.int32).reshape(1)
        ids = jnp.stack([2 * xi + yi] + [2 * cx + cy for cx, cy in _other_chips(xi, yi)] + [ci]).astype(jnp.int32)
    halfbuf, pre = {}, {}

    def gather(names):
        return _gather_weights([shards[k] for k in names], [BIG_AXIS[k] for k in names]) if dist else None

    def exchange(scat=(), swap=(), halves=(), scat2=(), swap2=(), extra=None):
        if not dist:
            return None, []
        parts, tags = [], []
        if scat:
            parts.append(_scatter_grads([gb[k][1] for k in scat], [shape_of[k] for k in scat],
                                        [axis_of[k] for k in scat]))
            tags.append((recv, scat))
        if swap:
            for k in swap:
                sums[k] = _sum_blocks(gb[k][0], recv[k], shape_of[k], axis_of[k], chip, "sum_" + k)
            parts.append(_swap_with_sibling([sums[k] for k in swap]))
            tags.append((other, swap))
        if halves:
            parts.append(_swap_halves([gb[k][1] for k in halves], [shape_of[k] for k in halves],
                                      [axis_of[k] for k in halves]))
            tags.append((halfbuf, halves))
        if scat2:
            for k in scat2:
                pre[k] = _presum(gb[k][0], halfbuf[k], shape_of[k], axis_of[k], ids, "presum_" + k)
            parts.append(_scatter_halves([pre[k][1] for k in scat2], [shape_of[k] for k in scat2]))
            tags.append((recv, scat2))
        if swap2:
            for k in swap2:
                sums[k] = _sum_half(pre[k][0], recv[k], "sum_" + k)
            parts.append(_swap_with_sibling([sums[k] for k in swap2]))
            tags.append((other, swap2))
        if extra is not None:
            parts.append(extra[0])
            tags.append((extra[1], extra[2]))
        return _join(parts), tags

    def take(ex_tags, got):
        ex, tags = ex_tags
        if ex is not None:
            for (dst, names), (o0, o1) in zip(tags, ex.cuts):
                dst.update(zip(names, got[o0:o1]))

    tril = jnp.tril(jnp.ones((CHUNK, CHUNK), dtype=bool))
    wsm = jnp.where(tril[None], ws["gmlp_w_s"], 0.0)
    wsm_b = wsm.astype(BF16)
    wsmt_b = wsm.transpose(0, 2, 1).astype(BF16)
    bias = jnp.repeat(ws["gmlp_b_s"].T, GMLP_HEAD_DIM, axis=1)

    if dist:
        names = ("ffn1_w_in", "ffn1_w_out")
        wb.update(zip(names, _run_exchange(gather(names), "gather_ffn1")))
    names = ("mix_w_in", "ssm_glu_w", "up_a", "up_b", "mix_w_out")
    (x1, xh1, rstd1, x0b, h1), got = _ffn_fwd(x0, wb["ffn1_w_in"], wb["ffn1_w_out"], row(ws["ln1_g"]),
                                              row(ws["ln1_b"]), tm, "ffn1_fwd", gather(names))
    wb.update(zip(names, got))
    sp = _s5_setup(ws["ssm_lambda_re"], ws["ssm_lambda_im"], ws["ssm_log_dt"], ws["ssm_b_re"],
                   ws["ssm_b_im"], ws["ssm_c_re"], ws["ssm_c_im"], ws["ssm_d"], wb["ssm_glu_w"],
                   ws["ssm_glu_b"], tb)
    names = ("ffn2_w_out",)
    (x1b, za, zuv, gab), got = _mixin_fwd(x1, wb["mix_w_in"], tm, gather(names))
    wb.update(zip(names, got))
    names = ("ffn2_w_in",)
    (s5o, y2p, carries), got = _s5_fwd(za, sp, bsz, seq, tb, gather(names))
    wb.update(zip(names, got))
    names = ("ple_w_gate", "ple_w_proj")
    (gm,), got = _gmlp_fwd(zuv, row(ws["gmlp_ln_g"]), row(ws["gmlp_ln_b"]), wsm_b, bias, gather(names))
    wb.update(zip(names, got))
    (x2, xh2, rstd2, x2b), _ = _mixout_fwd(x1, s5o, gm, gab, wb["up_a"], wb["up_b"], wb["mix_w_out"],
                                           row(ws["ln2_g"]), row(ws["ln2_b"]), tm)
    (x3, xh3, rstd3, _, h2), _ = _ffn_fwd(x2, wb["ffn2_w_in"], wb["ffn2_w_out"], row(ws["ln3_g"]),
                                          row(ws["ln3_b"]), tm, "ffn2_fwd")
    (dx3, x3b, pb, dq, de, loss_rows), _ = _ple_loss(x3, p0, tg, wb["ple_w_gate"], wb["ple_w_proj"], tm)
    gb["ple_w_gate"], _ = _tn_matmul(x3b, dq, "dw_ple_gate", 1024, 1024)
    gb["ple_w_proj"], _ = _tn_matmul(pb, de, "dw_ple_proj", 256, 1024)
    et = exchange(scat=("ple_w_gate", "ple_w_proj"))
    (dx2, dh2, a2, df2, gs["ln3_g"], gs["ln3_b"]), got = _ffn_bwd(
        dx3, xh3, rstd3, h2, wb["ffn2_w_in"], wb["ffn2_w_out"], row(ws["ln3_g"]), tm, "ffn2_bwd", et[0])
    take(et, got)
    gb["ffn2_w_out"], _ = _tn_matmul(a2, df2, "dw_ffn2_out", 1408, 1024)
    et = exchange(scat=("ffn2_w_out",))
    gb["ffn2_w_in"], got = _tn_matmul(x2b, dh2, "dw_ffn2_in", 1024, 1408, bg=et[0])
    take(et, got)
    et = exchange(swap=("ple_w_gate", "ple_w_proj", "ffn2_w_out"))
    (dx1a, dmx, mb, dya, dyb, ds5, dgm, dgab, gs["ln2_g"], gs["ln2_b"]), got = _mixout_bwd(
        dx2, xh2, rstd2, s5o, gm, gab, wb["up_a"], wb["up_b"], wb["mix_w_out"], row(ws["ln2_g"]), tm, et[0])
    take(et, got)
    gb["mix_w_out"], _ = _tn_matmul(mb, dmx, "dw_mix_out", 1024, 1024)
    gb["up_a"], _ = _tn_matmul(s5o, dya, "dw_up_a", 512, 1024)
    gb["up_b"], _ = _tn_matmul(gm, dyb, "dw_up_b", 512, 1024)
    et = exchange(scat=("ffn2_w_in",))
    (dza, dmr, dmi, dnr, dni, da, ddsk, dgw, dgb), got = _s5_bwd(za, y2p, ds5, carries, sp, bsz, seq, tb, et[0])
    take(et, got)
    gb["ssm_glu_w"] = (dgw, dgw.astype(BF16))
    et = exchange(scat=("mix_w_out", "up_a"), swap=("ffn2_w_in",))
    (dzuv, dws, dbias, gs["gmlp_ln_g"], gs["gmlp_ln_b"]), got = _gmlp_bwd(
        zuv, dgm, row(ws["gmlp_ln_g"]), row(ws["gmlp_ln_b"]), wsm_b, wsmt_b, bias, et[0])
    take(et, got)
    et = exchange(scat=("up_b", "ssm_glu_w"))
    (dx1,), got = _mixin_bwd(dx1a, dza, dzuv, dgab, wb["mix_w_in"], tm, et[0])
    take(et, got)
    g_mi, _ = _tn_matmul(x1b, dza, "dw_mix_in_a", 1024, 512, 0, 3584)
    g_mi, _ = _tn_matmul(x1b, dzuv, "dw_mix_in_uv", 1024, 512, 1, 3584, g_mi)
    et = exchange(swap=("mix_w_out", "up_a", "up_b", "ssm_glu_w"))
    gb["mix_w_in"], got = _tn_matmul(x1b, dgab, "dw_mix_in_g", 1024, 512, 3, 3584, g_mi, bg=et[0])
    take(et, got)
    et = exchange(scat=("mix_w_in",))
    (dx0, dh1, a1, df1, gs["ln1_g"], gs["ln1_b"]), got = _ffn_bwd(
        dx1, xh1, rstd1, h1, wb["ffn1_w_in"], wb["ffn1_w_out"], row(ws["ln1_g"]), tm, "ffn1_bwd", et[0])
    take(et, got)

    d_abr = da[0].sum(axis=0).reshape(SSM_GROUPS, SSM_STATE)
    d_abi = da[1].sum(axis=0).reshape(SSM_GROUPS, SSM_STATE)
    _, vjp = jax.vjp(_s5_discretise, ws["ssm_lambda_re"], ws["ssm_lambda_im"], ws["ssm_log_dt"],
                     ws["ssm_b_re"], ws["ssm_b_im"])
    (gs["ssm_lambda_re"], gs["ssm_lambda_im"], gs["ssm_log_dt"], gs["ssm_b_re"], gs["ssm_b_im"]) = vjp(
        (d_abr, d_abi, _block_diag_in_t(dmr), _block_diag_in_t(dmi)))
    gs["ssm_c_re"] = _block_diag_out_t(dnr)
    gs["ssm_c_im"] = _block_diag_out_t(dni)
    gs["ssm_d"] = ddsk
    gs["ssm_glu_b"] = dgb
    gs["gmlp_w_s"] = dws
    gs["gmlp_b_s"] = dbias.reshape(CHUNK, GMLP_HEADS, GMLP_HEAD_DIM).sum(axis=-1).T
    gs = {k: gs[k].reshape(SMALL_2D[k]) for k in SMALL}
    grad_x = dx0.reshape(bsz, seq, D_MODEL)

    if not dist:
        gb["ffn1_w_out"], _ = _tn_matmul(a1, df1, "dw_ffn1_out", 1408, 1024)
        gb["ffn1_w_in"], _ = _tn_matmul(x0b, dh1, "dw_ffn1_in", 1024, 1408)
        return loss_rows, grad_x, gb, gs, sums, other, gathered, None
    small = SMALL + ("loss_rows",)
    et = exchange(swap=("mix_w_in",), extra=(_gather_small([gs[k] for k in SMALL] + [loss_rows]), gathered, small))
    gb["ffn1_w_out"], got = _tn_matmul(a1, df1, "dw_ffn1_out", 1408, 1024, bg=et[0])
    take(et, got)
    last = ["ffn1_w_out"] + [LAST_PIECE % q for q in range(LAST_PIECES)]
    for i in range(1, len(last) + 3):
        stage = lambda d: tuple(last[i - d:i - d + 1]) if 0 <= i - d < len(last) else ()
        et = exchange(halves=stage(1), scat2=stage(2), swap2=stage(3))
        if i < len(last):
            gb[last[i]], got = _tn_matmul(x0b, dh1, "dw_" + last[i], D_MODEL // LAST_PIECES, 1408, bg=et[0],
                                          a_cols=(i - 1, 1))
        else:
            got = _run_exchange(et[0], "reduce_last_%d" % (i - len(last)))
        take(et, got)
    return loss_rows, grad_x, gb, gs, sums, other, gathered, ids


def _adamw(w, g, m, v):
    m = ADAM_B1 * m + (1.0 - ADAM_B1) * g
    v = ADAM_B2 * v + (1.0 - ADAM_B2) * (g * g)
    m_hat = m / ADAM_C1
    v_hat = v / ADAM_C2
    delta = -ADAM_LR * (m_hat / (jnp.sqrt(v_hat) + ADAM_EPS) + ADAM_WD * w)
    return delta, m, v


def _sum_blocks(part, recv, shape, axis, chip, name):
    r, c = shape
    rb = r // 8

    def body(chip_ref, p_ref, r_ref, o_ref):
        o_ref[...] = (p_ref[...] + r_ref[0].astype(F32) + r_ref[1].astype(F32) + r_ref[2].astype(F32))

    if axis == 0:
        own = pl.BlockSpec((rb, c), lambda i, k: (k[0] * 8 + i, 0))
    else:
        own = pl.BlockSpec((rb, c), lambda i, k: (i, k[0]))
    grid_spec = pltpu.PrefetchScalarGridSpec(
        num_scalar_prefetch=1, grid=(8,),
        in_specs=[own, pl.BlockSpec((3, rb, c), lambda i, k: (0, i, 0))],
        out_specs=pl.BlockSpec((rb, c), lambda i, k: (i, 0)))
    return pl.pallas_call(body, name=name, out_shape=SDS((r, c), F32), grid_spec=grid_spec,
                          compiler_params=_params(("parallel",)))(chip, part, recv)


def _presum(part, half, shape, axis, ids, name):
    r, c = shape
    rb = r // 4

    def body(ids_ref, p_ref, h_ref, of_ref, ob_ref):
        s = p_ref[...] + h_ref[...].astype(F32)
        ob_ref[...] = s.astype(BF16)

        @pl.when(pl.program_id(1) == 0)
        def _():
            of_ref[...] = s

    if axis == 0:
        p_spec = pl.BlockSpec((rb, c), lambda i, t, ids: (ids[t] * 4 + ids[4] * 2 + i, 0))
        h_spec = pl.BlockSpec((None, rb, c), lambda i, t, ids: (ids[t], i, 0))
    else:
        p_spec = pl.BlockSpec((rb, c), lambda i, t, ids: (ids[4] * 2 + i, ids[t]))
        h_spec = pl.BlockSpec((rb, c), lambda i, t, ids: (i, ids[t]))
    grid_spec = pltpu.PrefetchScalarGridSpec(
        num_scalar_prefetch=1, grid=(2, 4), in_specs=[p_spec, h_spec],
        out_specs=(pl.BlockSpec((rb, c), lambda i, t, ids: (i, 0)),
                   pl.BlockSpec((None, rb, c), lambda i, t, ids: (t, i, 0))))
    return pl.pallas_call(body, name=name, out_shape=(SDS((r // 2, c), F32), SDS((4, r // 2, c), BF16)),
                          grid_spec=grid_spec, compiler_params=_params(("parallel", "arbitrary")))(ids, part, half)


def _sum_half(pre, recv, name):
    hr, c = pre.shape
    rb = hr // 2

    def body(p_ref, r_ref, o_ref):
        o_ref[...] = (p_ref[...] + r_ref[0].astype(F32) + r_ref[1].astype(F32) + r_ref[2].astype(F32))

    spec = pl.BlockSpec((rb, c), lambda i: (i, 0))
    return pl.pallas_call(body, name=name, grid=(2,), out_shape=SDS((hr, c), F32),
                          in_specs=[spec, pl.BlockSpec((3, rb, c), lambda i: (0, i, 0))], out_specs=spec,
                          compiler_params=_params(("parallel",)))(pre, recv)


def _adam_halves(w, mine, oth, m, v, ids, name, piece=0, prev=None):
    r, c = w.shape
    rb = mine.shape[0] // 2

    def body(ids_ref, w_ref, a_ref, b_ref, m_ref, v_ref, *rest):
        g_ref, d_ref, nm_ref, nv_ref = rest[-4:]
        g = jnp.where(pl.program_id(0) // 2 == ids_ref[4], a_ref[...], b_ref[...])
        g_ref[...] = g
        d_ref[...], nm_ref[...], nv_ref[...] = _adamw(w_ref[...], g, m_ref[...], v_ref[...])

    whole = pl.BlockSpec((rb, c), lambda i, ids: (i + 4 * piece, 0))
    part = pl.BlockSpec((rb, c), lambda i, ids: (i % 2, 0))
    in_specs = [whole, part, part, whole, whole]
    args = [w, mine, oth, m, v]
    aliases = {}
    if prev is not None:
        in_specs += [pl.BlockSpec(memory_space=pl.ANY)] * 4
        args += list(prev)
        aliases = {6: 0, 7: 1, 8: 2, 9: 3}
    grid_spec = pltpu.PrefetchScalarGridSpec(num_scalar_prefetch=1, grid=(4,), in_specs=in_specs,
                                             out_specs=(whole,) * 4)
    return pl.pallas_call(body, name=name, out_shape=tuple(SDS((r, c), F32) for _ in range(4)),
                          grid_spec=grid_spec, input_output_aliases=aliases,
                          compiler_params=_params(("parallel",)))(ids, *args)


def _adam_big(w, ga, gb, m, v, name, piece=0, prev=None):
    r, c = w.shape
    pr = ga.shape[0]
    steps = 8 if pr == r else 2
    rb = pr // steps
    off = piece * steps

    def body(w_ref, ga_ref, gb_ref, m_ref, v_ref, *rest):
        g_ref, d_ref, nm_ref, nv_ref = rest[-4:]
        g = ga_ref[...] + gb_ref[...]
        g_ref[...] = g
        d_ref[...], nm_ref[...], nv_ref[...] = _adamw(w_ref[...], g, m_ref[...], v_ref[...])

    whole = pl.BlockSpec((rb, c), lambda i: (i + off, 0))
    part = pl.BlockSpec((rb, c), lambda i: (i, 0))
    in_specs = [whole, part, part, whole, whole]
    args = [w, ga, gb, m, v]
    aliases = {}
    if prev is not None:
        in_specs += [pl.BlockSpec(memory_space=pl.ANY)] * 4
        args += list(prev)
        aliases = {5: 0, 6: 1, 7: 2, 8: 3}
    return pl.pallas_call(
        body, name=name, grid=(steps,), out_shape=tuple(SDS((r, c), F32) for _ in range(4)),
        in_specs=in_specs, out_specs=(whole,) * 4, input_output_aliases=aliases,
        compiler_params=_params(("parallel",)),
    )(*args)


def _adam_small(ws, gathered, ms, vs):
    n = len(ws)

    def body(*refs):
        w_refs, g_refs, m_refs, v_refs = refs[:n], refs[n:2 * n], refs[2 * n:3 * n], refs[3 * n:4 * n]
        outs = refs[4 * n:]
        for i in range(n):
            g = g_refs[i][0]
            for d in range(1, N_DEV):
                g = g + g_refs[i][d]
            delta, nm, nv = _adamw(w_refs[i][...], g, m_refs[i][...], v_refs[i][...])
            outs[i][...] = g
            outs[n + i][...] = delta
            outs[2 * n + i][...] = nm
            outs[3 * n + i][...] = nv

    vmem = pl.BlockSpec(memory_space=pltpu.VMEM)
    shapes = [w.shape for w in ws]
    return pl.pallas_call(
        body, name="adam_small", out_shape=tuple(SDS(s, F32) for s in shapes * 4),
        in_specs=[vmem] * (4 * n), out_specs=tuple([vmem] * (4 * n)),
        compiler_params=pltpu.CompilerParams(vmem_limit_bytes=VMEM_LIMIT_BYTES),
    )(*ws, *gathered, *ms, *vs)


def _sum_loss(gathered):
    def body(g_ref, o_ref):
        tot = g_ref[0]
        for d in range(1, N_DEV):
            tot = tot + g_ref[d]
        o_ref[...] = (0.5 / D_MODEL) * jnp.sum(tot, axis=1, keepdims=True)

    vmem = pl.BlockSpec(memory_space=pltpu.VMEM)
    return pl.pallas_call(body, name="sum_loss", out_shape=SDS((1, 1), F32), in_specs=[vmem],
                          out_specs=vmem)(gathered)


def kernel(x, p, ffn1_w_in, ffn1_w_out, ln1_g, ln1_b, mix_w_in, ssm_lambda_re, ssm_lambda_im, ssm_log_dt, ssm_b_re, ssm_b_im, ssm_c_re, ssm_c_im, ssm_d, ssm_glu_w, ssm_glu_b, gmlp_ln_g, gmlp_ln_b, gmlp_w_s, gmlp_b_s, up_a, up_b, mix_w_out, ln2_g, ln2_b, ffn2_w_in, ffn2_w_out, ln3_g, ln3_b, ple_w_proj, ple_w_gate, loss_target, m_ffn1_w_in, m_ffn1_w_out, m_ln1_g, m_ln1_b, m_mix_w_in, m_ssm_lambda_re, m_ssm_lambda_im, m_ssm_log_dt, m_ssm_b_re, m_ssm_b_im, m_ssm_c_re, m_ssm_c_im, m_ssm_d, m_ssm_glu_w, m_ssm_glu_b, m_gmlp_ln_g, m_gmlp_ln_b, m_gmlp_w_s, m_gmlp_b_s, m_up_a, m_up_b, m_mix_w_out, m_ln2_g, m_ln2_b, m_ffn2_w_in, m_ffn2_w_out, m_ln3_g, m_ln3_b, m_ple_w_proj, m_ple_w_gate, v_ffn1_w_in, v_ffn1_w_out, v_ln1_g, v_ln1_b, v_mix_w_in, v_ssm_lambda_re, v_ssm_lambda_im, v_ssm_log_dt, v_ssm_b_re, v_ssm_b_im, v_ssm_c_re, v_ssm_c_im, v_ssm_d, v_ssm_glu_w, v_ssm_glu_b, v_gmlp_ln_g, v_gmlp_ln_b, v_gmlp_w_s, v_gmlp_b_s, v_up_a, v_up_b, v_mix_w_out, v_ln2_g, v_ln2_b, v_ffn2_w_in, v_ffn2_w_out, v_ln3_g, v_ln3_b, v_ple_w_proj, v_ple_w_gate):
    given = dict(locals())
    order = ("ffn1_w_in", "ffn1_w_out", "ln1_g", "ln1_b", "mix_w_in", "ssm_lambda_re", "ssm_lambda_im",
             "ssm_log_dt", "ssm_b_re", "ssm_b_im", "ssm_c_re", "ssm_c_im", "ssm_d", "ssm_glu_w", "ssm_glu_b",
             "gmlp_ln_g", "gmlp_ln_b", "gmlp_w_s", "gmlp_b_s", "up_a", "up_b", "mix_w_out", "ln2_g", "ln2_b",
             "ffn2_w_in", "ffn2_w_out", "ln3_g", "ln3_b", "ple_w_proj", "ple_w_gate")
    assert set(order) == set(BIG + SMALL)

    shard = {k: given[k][0] for k in BIG}
    shard_b = {k: shard[k].astype(BF16) for k in BIG}
    ws = {k: given[k][0] for k in SMALL}
    loss_rows, grad_x, gb, gs, sums, other, gathered, ids = _local_step(
        x, given["p"][0], loss_target, {}, ws, shard_b)

    out = {}
    for k in BIG:
        moments = (given["m_" + k][0], given["v_" + k][0])
        if k == "ffn1_w_out":
            out[k] = _adam_halves(shard[k], sums[k], other[k], *moments, ids, "adam_" + k)
        elif k == "ffn1_w_in":
            for q in range(LAST_PIECES):
                kq = LAST_PIECE % q
                out[k] = _adam_halves(shard[k], sums[kq], other[kq], *moments, ids, "adam_" + kq, q, out.get(k))
        else:
            out[k] = _adam_big(shard[k], sums[k], other[k], *moments, "adam_" + k)

    res = _adam_small([given[k].reshape(SMALL_2D[k]) for k in SMALL], [gathered[k] for k in SMALL],
                      [given["m_" + k].reshape(SMALL_2D[k]) for k in SMALL],
                      [given["v_" + k].reshape(SMALL_2D[k]) for k in SMALL])
    ns = len(SMALL)
    for i, k in enumerate(SMALL):
        out[k] = tuple(res[j * ns + i].reshape(given[k].shape) for j in range(4))
    loss = _sum_loss(gathered["loss_rows"]).reshape(())

    lead = lambda k, j: out[k][j][None] if k in BIG else out[k][j]
    return (loss, grad_x, *[lead(k, 0) for k in order], *[lead(k, 1) for k in order],
            *[lead(k, 2) for k in order], *[lead(k, 3) for k in order])
```

```python
import math

import jax
import jax.numpy as jnp
from jax import lax
from jax.experimental import pallas as pl
from jax.experimental.pallas import tpu as pltpu

F32 = jnp.float32
BF16 = jnp.bfloat16
MESH = pl.DeviceIdType.MESH
SDS = jax.ShapeDtypeStruct

D_MODEL = 1024
D_FF = 2816
D_SSM = 512
D_GMLP = 512
SSM_GROUPS = 32
SSM_GROUP_CH = 16
SSM_STATE = 64
SSM_LANES = SSM_GROUPS * SSM_STATE
GMLP_HEADS = 8
GMLP_HEAD_DIM = 64
CHUNK = 128
PLE_DIM = 256
LN_EPS = 1e-5
ALPHA = 2.0 ** 0.25

ADAM_LR = 0.001
ADAM_B1 = 0.9
ADAM_B2 = 0.999
ADAM_EPS = 1e-08
ADAM_WD = 0.01
ADAM_STEP = 10
ADAM_C1 = 1.0 - ADAM_B1 ** ADAM_STEP
ADAM_C2 = 1.0 - ADAM_B2 ** ADAM_STEP

N_DEV = 8
VMEM_LIMIT_BYTES = 56 * 1024 * 1024
FFN_COLS = 1408
S5_BLOCKS = 4
S5_BLOCK_IN = D_SSM // S5_BLOCKS
S5_BLOCK_ST = SSM_LANES // S5_BLOCKS
SCAN_LANES = 512
LAST_PIECES = 2
LAST_PIECE = "ffn1_w_in_q%d"
_G0 = math.sqrt(2.0 / math.pi)
_G1 = 0.044715


def _dot(a, b):
    return jnp.dot(a, b, preferred_element_type=F32)


def _dot_nt(a, b):
    return lax.dot_general(a, b, (((1,), (1,)), ((), ())), preferred_element_type=F32)


def _dot_tn(a, b):
    return lax.dot_general(a, b, (((0,), (0,)), ((), ())), preferred_element_type=F32)


def _sigmoid(x):
    return 1.0 / (1.0 + jnp.exp(-x))


def _gelu(x):
    t = jnp.tanh(_G0 * (x + _G1 * x * x * x))
    return 0.5 * x * (1.0 + t)


def _gelu_grad(x):
    t = jnp.tanh(_G0 * (x + _G1 * x * x * x))
    return 0.5 * (1.0 + t) + 0.5 * x * (1.0 - t * t) * _G0 * (1.0 + 3.0 * _G1 * x * x)


def _ln_fwd(r, g, b):
    mu = jnp.mean(r, axis=-1, keepdims=True)
    d = r - mu
    var = jnp.mean(d * d, axis=-1, keepdims=True)
    rstd = lax.rsqrt(var + LN_EPS)
    xh = d * rstd
    return xh * g + b, xh, rstd


def _ln_bwd(dy, xh, rstd, g):
    dxh = dy * g
    m1 = jnp.mean(dxh, axis=-1, keepdims=True)
    m2 = jnp.mean(dxh * xh, axis=-1, keepdims=True)
    return rstd * (dxh - m1 - xh * m2)


def _resident(shape):
    nd = len(shape)
    return pl.BlockSpec(shape, lambda *_: (0,) * nd, pipeline_mode=pl.Buffered(1))


def _fixed(shape):
    nd = len(shape)
    return pl.BlockSpec(shape, lambda *_: (0,) * nd)


def _rows(tm, cols):
    return pl.BlockSpec((tm, cols), lambda i: (i, 0))


def _params(sem):
    return pltpu.CompilerParams(dimension_semantics=sem, vmem_limit_bytes=VMEM_LIMIT_BYTES)


class _Exchange:
    def __init__(self, args, out_shape, sems, start, finish):
        self.args, self.out_shape, self.sems = list(args), list(out_shape), list(sems)
        self.start, self.finish = start, finish
        self.cuts = [(0, len(self.out_shape))]


def _call(body, name, grid, in_specs, out_specs, out_shape, args, scratch=(), sem=None, bg=None, aliases=None):
    aliases = {} if aliases is None else aliases
    if bg is None:
        res = pl.pallas_call(body, name=name, grid=grid, out_shape=tuple(out_shape), in_specs=list(in_specs),
                             out_specs=tuple(out_specs), scratch_shapes=list(scratch),
                             input_output_aliases=aliases, compiler_params=_params(sem))(*args)
        return tuple(res), ()
    n_in, n_out, n_bi, n_bo, n_sc = len(args), len(out_shape), len(bg.args), len(bg.out_shape), len(scratch)

    def wrapped(*refs):
        ins = refs[:n_in]
        b_ins = refs[n_in:n_in + n_bi]
        outs = refs[n_in + n_bi:n_in + n_bi + n_out]
        b_outs = refs[n_in + n_bi + n_out:n_in + n_bi + n_out + n_bo]
        rest = refs[n_in + n_bi + n_out + n_bo:]
        scr, b_sems = rest[:n_sc], rest[n_sc:]
        first = pl.program_id(0) == 0
        last = pl.program_id(0) == grid[0] - 1
        for ax in range(1, len(grid)):
            first = jnp.logical_and(first, pl.program_id(ax) == 0)
            last = jnp.logical_and(last, pl.program_id(ax) == grid[ax] - 1)

        @pl.when(first)
        def _():
            bg.start(b_ins, b_outs, b_sems)

        body(*ins, *outs, *scr)

        @pl.when(last)
        def _():
            bg.finish(b_ins, b_outs, b_sems)

    any_spec = pl.BlockSpec(memory_space=pl.ANY)
    res = pl.pallas_call(
        wrapped, name=name, grid=grid, out_shape=tuple(out_shape) + tuple(bg.out_shape),
        in_specs=list(in_specs) + [any_spec] * n_bi, out_specs=tuple(out_specs) + (any_spec,) * n_bo,
        scratch_shapes=list(scratch) + list(bg.sems), input_output_aliases=aliases,
        compiler_params=_params(tuple("arbitrary" for _ in grid)))(*args, *bg.args)
    return tuple(res[:n_out]), tuple(res[n_out:])


def _run_exchange(ex, name):
    n_i, n_o = len(ex.args), len(ex.out_shape)

    def body(*refs):
        ins, outs, sems = refs[:n_i], refs[n_i:n_i + n_o], refs[n_i + n_o:]
        ex.start(ins, outs, sems)
        ex.finish(ins, outs, sems)

    any_spec = pl.BlockSpec(memory_space=pl.ANY)
    return tuple(pl.pallas_call(body, name=name, out_shape=tuple(ex.out_shape), in_specs=[any_spec] * n_i,
                                out_specs=(any_spec,) * n_o, scratch_shapes=list(ex.sems))(*ex.args))


def _join(exchanges):
    cuts = []
    a = o = q = 0
    for e in exchanges:
        cuts.append((a, a + len(e.args), o, o + len(e.out_shape), q, q + len(e.sems)))
        a, o, q = cuts[-1][1], cuts[-1][3], cuts[-1][5]

    def start(ins, outs, sems):
        for e, (a0, a1, o0, o1, q0, q1) in zip(exchanges, cuts):
            e.start(ins[a0:a1], outs[o0:o1], sems[q0:q1])

    def finish(ins, outs, sems):
        for e, (a0, a1, o0, o1, q0, q1) in zip(exchanges, cuts):
            e.finish(ins[a0:a1], outs[o0:o1], sems[q0:q1])

    joined = _Exchange(sum((e.args for e in exchanges), []), sum((e.out_shape for e in exchanges), []),
                       sum((e.sems for e in exchanges), []), start, finish)
    joined.cuts = [(c[2], c[3]) for c in cuts]
    return joined


def _ffn_fwd(x, w_in, w_out, g, b, tm, name, bg=None):
    t = x.shape[0]
    nch = D_FF // FFN_COLS

    def body(x_ref, win_ref, wout_ref, g_ref, b_ref, xn_ref, xh_ref, rstd_ref, xb_ref, h_ref):
        xv = x_ref[...]
        xb = xv.astype(BF16)
        xb_ref[...] = xb
        f = jnp.zeros((tm, D_MODEL), F32)
        for k in range(nch):
            cg = slice(k * FFN_COLS, (k + 1) * FFN_COLS)
            cu = slice(D_FF + k * FFN_COLS, D_FF + (k + 1) * FFN_COLS)
            hg = _dot(xb, win_ref[:, cg])
            hu = _dot(xb, win_ref[:, cu])
            h_ref[:, cg] = hg.astype(BF16)
            h_ref[:, cu] = hu.astype(BF16)
            a = hg * _sigmoid(hg) * hu
            f = f + _dot(a.astype(BF16), wout_ref[cg, :])
        y, xh, rstd = _ln_fwd(ALPHA * xv + 0.5 * f, g_ref[...], b_ref[...])
        xn_ref[...] = y
        xh_ref[...] = xh
        rstd_ref[...] = rstd

    return _call(
        body, name, (t // tm,),
        [_rows(tm, D_MODEL), _resident((D_MODEL, 2 * D_FF)), _resident((D_FF, D_MODEL)),
         _fixed((1, D_MODEL)), _fixed((1, D_MODEL))],
        (_rows(tm, D_MODEL), _rows(tm, D_MODEL), _rows(tm, 1), _rows(tm, D_MODEL), _rows(tm, 2 * D_FF)),
        (SDS((t, D_MODEL), F32), SDS((t, D_MODEL), F32), SDS((t, 1), F32), SDS((t, D_MODEL), BF16),
         SDS((t, 2 * D_FF), BF16)),
        (x, w_in, w_out, g, b), sem=("parallel",), bg=bg)


def _ffn_bwd(dxn, xh, rstd, h, w_in, w_out, g, tm, name, bg=None):
    t = dxn.shape[0]
    nch = D_FF // FFN_COLS

    def body(dxn_ref, xh_ref, rstd_ref, h_ref, win_ref, wout_ref, g_ref,
             dx_ref, dh_ref, a_ref, df_ref, dg_ref, db_ref):
        @pl.when(pl.program_id(0) == 0)
        def _():
            dg_ref[...] = jnp.zeros_like(dg_ref)
            db_ref[...] = jnp.zeros_like(db_ref)

        dy = dxn_ref[...]
        xhv = xh_ref[...]
        dr = _ln_bwd(dy, xhv, rstd_ref[...], g_ref[...])
        dg_ref[...] += jnp.sum(dy * xhv, axis=0, keepdims=True)
        db_ref[...] += jnp.sum(dy, axis=0, keepdims=True)
        df = (0.5 * dr).astype(BF16)
        df_ref[...] = df
        dx = ALPHA * dr
        for k in range(nch):
            cg = slice(k * FFN_COLS, (k + 1) * FFN_COLS)
            cu = slice(D_FF + k * FFN_COLS, D_FF + (k + 1) * FFN_COLS)
            hg = h_ref[:, cg].astype(F32)
            hu = h_ref[:, cu].astype(F32)
            sg = _sigmoid(hg)
            silu = hg * sg
            a_ref[:, cg] = (silu * hu).astype(BF16)
            da = _dot_nt(df, wout_ref[cg, :])
            dhu = (da * silu).astype(BF16)
            dhg = (da * hu * (sg * (1.0 + hg * (1.0 - sg)))).astype(BF16)
            dh_ref[:, cg] = dhg
            dh_ref[:, cu] = dhu
            dx = dx + _dot_nt(dhg, win_ref[:, cg]) + _dot_nt(dhu, win_ref[:, cu])
        dx_ref[...] = dx

    return _call(
        body, name, (t // tm,),
        [_rows(tm, D_MODEL), _rows(tm, D_MODEL), _rows(tm, 1), _rows(tm, 2 * D_FF),
         _resident((D_MODEL, 2 * D_FF)), _resident((D_FF, D_MODEL)), _fixed((1, D_MODEL))],
        (_rows(tm, D_MODEL), _rows(tm, 2 * D_FF), _rows(tm, D_FF), _rows(tm, D_MODEL),
         _fixed((1, D_MODEL)), _fixed((1, D_MODEL))),
        (SDS((t, D_MODEL), F32), SDS((t, 2 * D_FF), BF16), SDS((t, D_FF), BF16), SDS((t, D_MODEL), BF16),
         SDS((1, D_MODEL), F32), SDS((1, D_MODEL), F32)),
        (dxn, xh, rstd, h, w_in, w_out, g), sem=("arbitrary",), bg=bg)


def _tn_matmul(a, b, name, bm, bn, col_block=0, total_cols=None, prev=None, bg=None, a_cols=None):
    t, m = a.shape
    a_first = 0
    if a_cols is not None:
        a_first, m = a_cols[0], a_cols[1] * bm
    n = b.shape[1]
    total_cols = n if total_cols is None else total_cols
    bk = min(512, t)
    nk = t // bk
    n_in = 2 if prev is None else 4

    def body(*refs):
        a_ref, b_ref = refs[0], refs[1]
        o_ref, ob_ref = refs[n_in], refs[n_in + 1]
        k = pl.program_id(2)

        @pl.when(k == 0)
        def _():
            o_ref[...] = jnp.zeros_like(o_ref)

        o_ref[...] += _dot_tn(a_ref[...], b_ref[...])

        @pl.when(k == nk - 1)
        def _():
            ob_ref[...] = o_ref[...].astype(BF16)

    in_specs = [pl.BlockSpec((bk, bm), lambda i, j, k: (k, i + a_first)),
                pl.BlockSpec((bk, bn), lambda i, j, k: (k, j))]
    args = [a, b]
    aliases = {}
    if prev is not None:
        in_specs += [pl.BlockSpec(memory_space=pl.ANY), pl.BlockSpec(memory_space=pl.ANY)]
        args += list(prev)
        aliases = {2: 0, 3: 1}
    out_spec = pl.BlockSpec((bm, bn), lambda i, j, k: (i, j + col_block))
    return _call(body, name, (m // bm, n // bn, nk), in_specs, (out_spec, out_spec),
                 (SDS((m, total_cols), F32), SDS((m, total_cols), BF16)), args,
                 sem=("parallel", "parallel", "arbitrary"), bg=bg, aliases=aliases)


def _mixin_fwd(x1, w, tm, bg=None):
    t = x1.shape[0]

    def body(x_ref, w_ref, xb_ref, za_ref, zuv_ref, gab_ref):
        xb = x_ref[...].astype(BF16)
        xb_ref[...] = xb
        za_ref[...] = _dot(xb, w_ref[:, 0:512]).astype(BF16)
        zuv_ref[...] = _dot(xb, w_ref[:, 512:1536]).astype(BF16)
        gab_ref[...] = _dot(xb, w_ref[:, 1536:3584]).astype(BF16)

    return _call(
        body, "mixin_fwd", (t // tm,),
        [_rows(tm, D_MODEL), _resident((D_MODEL, 3584))],
        (_rows(tm, D_MODEL), _rows(tm, 512), _rows(tm, 1024), _rows(tm, 2048)),
        (SDS((t, D_MODEL), BF16), SDS((t, 512), BF16), SDS((t, 1024), BF16), SDS((t, 2048), BF16)),
        (x1, w), sem=("parallel",), bg=bg)


def _mixin_bwd(dx1a, dza, dzuv, dgab, w, tm, bg=None):
    t = dx1a.shape[0]

    def body(d_ref, dza_ref, dzuv_ref, dgab_ref, w_ref, dx_ref):
        dx_ref[...] = (d_ref[...] + _dot_nt(dza_ref[...], w_ref[:, 0:512])
                       + _dot_nt(dzuv_ref[...], w_ref[:, 512:1536])
                       + _dot_nt(dgab_ref[...], w_ref[:, 1536:3584]))

    return _call(
        body, "mixin_bwd", (t // tm,),
        [_rows(tm, D_MODEL), _rows(tm, 512), _rows(tm, 1024), _rows(tm, 2048), _resident((D_MODEL, 3584))],
        (_rows(tm, D_MODEL),), (SDS((t, D_MODEL), F32),),
        (dx1a, dza, dzuv, dgab, w), sem=("parallel",), bg=bg)


def _scan_fwd(hr_ref, hi_ref, a_ref, ap_ref, carry_ref, seg, cin_ref):
    for lc in range(SSM_LANES // SCAN_LANES):
        ls = slice(lc * SCAN_LANES, (lc + 1) * SCAN_LANES)
        a_r = jnp.broadcast_to(a_ref[0:1, ls], (8, SCAN_LANES))
        a_i = jnp.broadcast_to(a_ref[1:2, ls], (8, SCAN_LANES))

        def step(j, hc, ls=ls, a_r=a_r, a_i=a_i):
            h_r, h_i = hc
            rows = pl.ds(pl.multiple_of(j * 8, 8), 8)
            n_r = a_r * h_r - a_i * h_i + hr_ref[rows, ls]
            n_i = a_r * h_i + a_i * h_r + hi_ref[rows, ls]
            hr_ref[rows, ls] = n_r
            hi_ref[rows, ls] = n_i
            return n_r, n_i

        zero = jnp.zeros((8, SCAN_LANES), F32)
        f_r, f_i = lax.fori_loop(0, seg, step, (zero, zero))
        c_r = carry_ref[0:1, ls]
        c_i = carry_ref[1:2, ls]
        p_r = ap_ref[0:1, ls]
        p_i = ap_ref[1:2, ls]
        rows_r, rows_i = [], []
        for s in range(8):
            rows_r.append(c_r)
            rows_i.append(c_i)
            c_r, c_i = (f_r[s:s + 1] + p_r * c_r - p_i * c_i,
                        f_i[s:s + 1] + p_r * c_i + p_i * c_r)
        carry_ref[0:1, ls] = c_r
        carry_ref[1:2, ls] = c_i
        cin_r = jnp.concatenate(rows_r, axis=0)
        cin_i = jnp.concatenate(rows_i, axis=0)
        if cin_ref is not None:
            cin_ref[0, :, ls] = cin_r
            cin_ref[1, :, ls] = cin_i

        def fix(j, cc, ls=ls, a_r=a_r, a_i=a_i):
            c_r, c_i = cc
            c_r, c_i = a_r * c_r - a_i * c_i, a_r * c_i + a_i * c_r
            rows = pl.ds(pl.multiple_of(j * 8, 8), 8)
            hr_ref[rows, ls] = hr_ref[rows, ls] + c_r
            hi_ref[rows, ls] = hi_ref[rows, ls] + c_i
            return c_r, c_i

        lax.fori_loop(0, seg, fix, (cin_r, cin_i))


def _scan_bwd(gr_ref, gi_ref, hr_ref, hi_ref, cin_ref, a_ref, ap_ref, rcarry_ref, da_ref, seg):
    for lc in range(SSM_LANES // SCAN_LANES):
        ls = slice(lc * SCAN_LANES, (lc + 1) * SCAN_LANES)
        a_r = jnp.broadcast_to(a_ref[0:1, ls], (8, SCAN_LANES))
        a_i = jnp.broadcast_to(a_ref[1:2, ls], (8, SCAN_LANES))

        def step(t, gc, ls=ls, a_r=a_r, a_i=a_i):
            g_r, g_i = gc
            rows = pl.ds(pl.multiple_of((seg - 1 - t) * 8, 8), 8)
            n_r = gr_ref[rows, ls] + a_r * g_r + a_i * g_i
            n_i = gi_ref[rows, ls] + a_r * g_i - a_i * g_r
            gr_ref[rows, ls] = n_r
            gi_ref[rows, ls] = n_i
            return n_r, n_i

        zero = jnp.zeros((8, SCAN_LANES), F32)
        f_r, f_i = lax.fori_loop(0, seg, step, (zero, zero))
        c_r = rcarry_ref[0:1, ls]
        c_i = rcarry_ref[1:2, ls]
        p_r = ap_ref[0:1, ls]
        p_i = ap_ref[1:2, ls]
        rows_r, rows_i = [None] * 8, [None] * 8
        for s in range(7, -1, -1):
            rows_r[s] = c_r
            rows_i[s] = c_i
            c_r, c_i = (f_r[s:s + 1] + p_r * c_r + p_i * c_i,
                        f_i[s:s + 1] + p_r * c_i - p_i * c_r)
        rcarry_ref[0:1, ls] = c_r
        rcarry_ref[1:2, ls] = c_i
        cin_r = jnp.concatenate(rows_r, axis=0)
        cin_i = jnp.concatenate(rows_i, axis=0)

        def fix_row(j_rows, hp_r, hp_i, cc, ls=ls, a_r=a_r, a_i=a_i):
            c_r, c_i, acc_r, acc_i = cc
            c_r, c_i = a_r * c_r + a_i * c_i, a_r * c_i - a_i * c_r
            g_r = gr_ref[j_rows, ls] + c_r
            g_i = gi_ref[j_rows, ls] + c_i
            gr_ref[j_rows, ls] = g_r
            gi_ref[j_rows, ls] = g_i
            acc_r = acc_r + g_r * hp_r + g_i * hp_i
            acc_i = acc_i + g_i * hp_r - g_r * hp_i
            return c_r, c_i, acc_r, acc_i

        def fix(t, cc, ls=ls, fix_row=fix_row):
            j = seg - 1 - t
            rows = pl.ds(pl.multiple_of(j * 8, 8), 8)
            prev = pl.ds(pl.multiple_of((j - 1) * 8, 8), 8)
            return fix_row(rows, hr_ref[prev, ls], hi_ref[prev, ls], cc)

        cc = lax.fori_loop(0, seg - 1, fix, (cin_r, cin_i, zero, zero))
        _, _, acc_r, acc_i = fix_row(pl.ds(0, 8), cin_ref[0, :, ls], cin_ref[1, :, ls], cc)
        da_ref[0, :, ls] += acc_r
        da_ref[1, :, ls] += acc_i


def _s5_fwd(za, sp, bsz, seq, tb, bg=None):
    nb = seq // tb
    seg = tb // 8
    t = bsz * seq

    def body(za_ref, perm_ref, permt_ref, mre_ref, mim_ref, nre_ref, nim_ref, a_ref, ap_ref,
             dsk_ref, gw_ref, gb_ref, out_ref, y2_ref, car_ref, hr_ref, hi_ref, carry_ref):
        @pl.when(pl.program_id(1) == 0)
        def _():
            carry_ref[...] = jnp.zeros_like(carry_ref)

        car_ref[0] = carry_ref[...]
        up = _dot(perm_ref[...], za_ref[...])
        upb = up.astype(BF16)
        for bb in range(S5_BLOCKS):
            ub = upb[:, bb * S5_BLOCK_IN:(bb + 1) * S5_BLOCK_IN]
            st = slice(bb * S5_BLOCK_ST, (bb + 1) * S5_BLOCK_ST)
            hr_ref[:, st] = _dot(ub, mre_ref[bb])
            hi_ref[:, st] = _dot(ub, mim_ref[bb])
        _scan_fwd(hr_ref, hi_ref, a_ref, ap_ref, carry_ref, seg, None)
        ys = []
        for bb in range(S5_BLOCKS):
            st = slice(bb * S5_BLOCK_ST, (bb + 1) * S5_BLOCK_ST)
            ys.append(_dot(hr_ref[:, st].astype(BF16), nre_ref[bb])
                      - _dot(hi_ref[:, st].astype(BF16), nim_ref[bb]))
        y2 = jnp.concatenate(ys, axis=1) + dsk_ref[...] * up
        y2_ref[...] = y2
        y3 = _gelu(y2)
        gl = _dot(y3.astype(BF16), gw_ref[...]) + gb_ref[...]
        oa = y3 * _sigmoid(gl)
        out_ref[...] = _dot(permt_ref[...], oa.astype(BF16)).astype(BF16)

    blk = pl.BlockSpec((tb, D_SSM), lambda b, j: (b * nb + j, 0))
    m_shape = (S5_BLOCKS, S5_BLOCK_IN, S5_BLOCK_ST)
    n_shape = (S5_BLOCKS, S5_BLOCK_ST, S5_BLOCK_IN)
    return _call(
        body, "s5_fwd", (bsz, nb),
        [blk, _fixed((tb, tb)), _fixed((tb, tb)), _fixed(m_shape), _fixed(m_shape), _fixed(n_shape),
         _fixed(n_shape), _fixed((2, SSM_LANES)), _fixed((2, SSM_LANES)), _fixed((1, D_SSM)),
         _fixed((D_SSM, D_SSM)), _fixed((1, D_SSM))],
        (blk, blk, pl.BlockSpec((1, 2, SSM_LANES), lambda b, j: (b * nb + j, 0, 0))),
        (SDS((t, D_SSM), BF16), SDS((t, D_SSM), F32), SDS((bsz * nb, 2, SSM_LANES), F32)),
        (za, sp["perm"], sp["permt"], sp["mre"], sp["mim"], sp["nre"], sp["nim"], sp["a"], sp["ap"],
         sp["dskip"], sp["glu_w"], sp["glu_b"]),
        scratch=[pltpu.VMEM((tb, SSM_LANES), F32), pltpu.VMEM((tb, SSM_LANES), F32),
                 pltpu.VMEM((2, SSM_LANES), F32)],
        sem=("arbitrary", "arbitrary"), bg=bg)


def _s5_bwd(za, y2p, doa, carries, sp, bsz, seq, tb, bg=None):
    nb = seq // tb
    seg = tb // 8
    t = bsz * seq

    def body(za_ref, y2_ref, doa_ref, car_ref, perm_ref, permt_ref, mre_ref, mim_ref, mtre_ref, mtim_ref,
             nre_ref, nim_ref, ntre_ref, ntim_ref, a_ref, ap_ref, dsk_ref, gw_ref, gwt_ref, gb_ref,
             dza_ref, dmr_ref, dmi_ref, dnr_ref, dni_ref, da_ref, ddsk_ref, dgw_ref, dgb_ref,
             hr_ref, hi_ref, gr_ref, gi_ref, cin_ref, carry_ref, rcarry_ref):
        first = jnp.logical_and(pl.program_id(0) == 0, pl.program_id(1) == 0)

        @pl.when(first)
        def _():
            for r in (dmr_ref, dmi_ref, dnr_ref, dni_ref, da_ref, ddsk_ref, dgw_ref, dgb_ref):
                r[...] = jnp.zeros_like(r)

        @pl.when(pl.program_id(1) == 0)
        def _():
            rcarry_ref[...] = jnp.zeros_like(rcarry_ref)

        carry_ref[...] = car_ref[0]
        perm = perm_ref[...]
        up = _dot(perm, za_ref[...])
        upb = up.astype(BF16)
        for bb in range(S5_BLOCKS):
            ub = upb[:, bb * S5_BLOCK_IN:(bb + 1) * S5_BLOCK_IN]
            st = slice(bb * S5_BLOCK_ST, (bb + 1) * S5_BLOCK_ST)
            hr_ref[:, st] = _dot(ub, mre_ref[bb])
            hi_ref[:, st] = _dot(ub, mim_ref[bb])
        _scan_fwd(hr_ref, hi_ref, a_ref, ap_ref, carry_ref, seg, cin_ref)

        y2 = y2_ref[...]
        y3 = _gelu(y2)
        y3b = y3.astype(BF16)
        sg = _sigmoid(_dot(y3b, gw_ref[...]) + gb_ref[...])
        d0 = doa_ref[...]
        d_hi = d0.astype(BF16)
        d1 = d0 - d_hi.astype(F32)
        d_mid = d1.astype(BF16)
        d_lo = (d1 - d_mid.astype(F32)).astype(BF16)
        doap = _dot(perm, d_hi) + _dot(perm, d_mid) + _dot(perm, d_lo)
        dgl = doap * y3 * sg * (1.0 - sg)
        dglb = dgl.astype(BF16)
        dy3 = doap * sg + _dot(dglb, gwt_ref[...])
        dgw_ref[...] += _dot_tn(y3b, dglb)
        dgb_ref[...] += jnp.sum(dgl, axis=0, keepdims=True)
        dy2 = dy3 * _gelu_grad(y2)
        ddsk_ref[...] += jnp.sum(dy2 * up, axis=0, keepdims=True)
        dyb = dy2.astype(BF16)
        for bb in range(S5_BLOCKS):
            dyc = dyb[:, bb * S5_BLOCK_IN:(bb + 1) * S5_BLOCK_IN]
            st = slice(bb * S5_BLOCK_ST, (bb + 1) * S5_BLOCK_ST)
            gr_ref[:, st] = _dot(dyc, ntre_ref[bb])
            gi_ref[:, st] = -_dot(dyc, ntim_ref[bb])
            dnr_ref[bb] += _dot_tn(hr_ref[:, st].astype(BF16), dyc)
            dni_ref[bb] += -_dot_tn(hi_ref[:, st].astype(BF16), dyc)
        _scan_bwd(gr_ref, gi_ref, hr_ref, hi_ref, cin_ref, a_ref, ap_ref, rcarry_ref, da_ref, seg)
        dus = []
        for bb in range(S5_BLOCKS):
            st = slice(bb * S5_BLOCK_ST, (bb + 1) * S5_BLOCK_ST)
            grb = gr_ref[:, st].astype(BF16)
            gib = gi_ref[:, st].astype(BF16)
            dus.append(_dot(grb, mtre_ref[bb]) + _dot(gib, mtim_ref[bb]))
            ub = upb[:, bb * S5_BLOCK_IN:(bb + 1) * S5_BLOCK_IN]
            dmr_ref[bb] += _dot_tn(ub, grb)
            dmi_ref[bb] += _dot_tn(ub, gib)
        du = jnp.concatenate(dus, axis=1) + dy2 * dsk_ref[...]
        dza_ref[...] = _dot(permt_ref[...], du.astype(BF16)).astype(BF16)

    def rev(b, j):
        return (b * nb + (nb - 1 - j), 0)

    blk = pl.BlockSpec((tb, D_SSM), rev)
    m_shape = (S5_BLOCKS, S5_BLOCK_IN, S5_BLOCK_ST)
    n_shape = (S5_BLOCKS, S5_BLOCK_ST, S5_BLOCK_IN)
    return _call(
        body, "s5_bwd", (bsz, nb),
        [blk, blk, blk, pl.BlockSpec((1, 2, SSM_LANES), lambda b, j: (b * nb + (nb - 1 - j), 0, 0)),
         _fixed((tb, tb)), _fixed((tb, tb)), _fixed(m_shape), _fixed(m_shape), _fixed(n_shape), _fixed(n_shape),
         _fixed(n_shape), _fixed(n_shape), _fixed(m_shape), _fixed(m_shape),
         _fixed((2, SSM_LANES)), _fixed((2, SSM_LANES)), _fixed((1, D_SSM)),
         _fixed((D_SSM, D_SSM)), _fixed((D_SSM, D_SSM)), _fixed((1, D_SSM))],
        (blk, _fixed(m_shape), _fixed(m_shape), _fixed(n_shape), _fixed(n_shape),
         _fixed((2, 8, SSM_LANES)), _fixed((1, D_SSM)), _fixed((D_SSM, D_SSM)), _fixed((1, D_SSM))),
        (SDS((t, D_SSM), BF16), SDS(m_shape, F32), SDS(m_shape, F32), SDS(n_shape, F32), SDS(n_shape, F32),
         SDS((2, 8, SSM_LANES), F32), SDS((1, D_SSM), F32), SDS((D_SSM, D_SSM), F32), SDS((1, D_SSM), F32)),
        (za, y2p, doa, carries, sp["perm"], sp["permt"], sp["mre"], sp["mim"], sp["mtre"], sp["mtim"],
         sp["nre"], sp["nim"], sp["ntre"], sp["ntim"], sp["a"], sp["ap"], sp["dskip"], sp["glu_w"],
         sp["glu_wt"], sp["glu_b"]),
        scratch=[pltpu.VMEM((tb, SSM_LANES), F32), pltpu.VMEM((tb, SSM_LANES), F32),
                 pltpu.VMEM((tb, SSM_LANES), F32), pltpu.VMEM((tb, SSM_LANES), F32),
                 pltpu.VMEM((2, 8, SSM_LANES), F32), pltpu.VMEM((2, SSM_LANES), F32),
                 pltpu.VMEM((2, SSM_LANES), F32)],
        sem=("arbitrary", "arbitrary"), bg=bg)


def _gmlp_spatial(ws_ref, vb):
    lane = lax.broadcasted_iota(jnp.int32, (CHUNK, 128), 1)
    parts = []
    for j in range(GMLP_HEADS // 2):
        vp = vb[:, 128 * j:128 * (j + 1)]
        parts.append(jnp.where(lane < GMLP_HEAD_DIM, _dot(ws_ref[2 * j], vp), _dot(ws_ref[2 * j + 1], vp)))
    return jnp.concatenate(parts, axis=1)


def _gmlp_fwd(zuv, ln_g, ln_b, wsm, bias, bg=None):
    t = zuv.shape[0]

    def body(z_ref, g_ref, b_ref, ws_ref, bias_ref, out_ref):
        u = _gelu(z_ref[:, 0:D_GMLP].astype(F32))
        v0 = _gelu(z_ref[:, D_GMLP:2 * D_GMLP].astype(F32))
        v, _, _ = _ln_fwd(v0, g_ref[...], b_ref[...])
        s = _gmlp_spatial(ws_ref, v.astype(BF16)) + bias_ref[...]
        out_ref[...] = (u * s).astype(BF16)

    return _call(
        body, "gmlp_fwd", (t // CHUNK,),
        [_rows(CHUNK, 2 * D_GMLP), _fixed((1, D_GMLP)), _fixed((1, D_GMLP)),
         _fixed((GMLP_HEADS, CHUNK, CHUNK)), _fixed((CHUNK, D_GMLP))],
        (_rows(CHUNK, D_GMLP),), (SDS((t, D_GMLP), BF16),),
        (zuv, ln_g, ln_b, wsm, bias), sem=("parallel",), bg=bg)


def _gmlp_bwd(zuv, dgm, ln_g, ln_b, wsm, wsmt, bias, bg=None):
    t = zuv.shape[0]

    def body(z_ref, d_ref, g_ref, b_ref, ws_ref, wst_ref, bias_ref,
             dz_ref, dws_ref, dbias_ref, dg_ref, db_ref):
        @pl.when(pl.program_id(0) == 0)
        def _():
            for r in (dws_ref, dbias_ref, dg_ref, db_ref):
                r[...] = jnp.zeros_like(r)

        zu = z_ref[:, 0:D_GMLP].astype(F32)
        zv = z_ref[:, D_GMLP:2 * D_GMLP].astype(F32)
        u = _gelu(zu)
        v0 = _gelu(zv)
        gam = g_ref[...]
        v, vhat, rstd = _ln_fwd(v0, gam, b_ref[...])
        vb = v.astype(BF16)
        s = _gmlp_spatial(ws_ref, vb) + bias_ref[...]
        d = d_ref[...]
        dz_ref[:, 0:D_GMLP] = (d * s * _gelu_grad(zu)).astype(BF16)
        ds = d * u
        dbias_ref[...] += ds
        dsb = ds.astype(BF16)
        lane = lax.broadcasted_iota(jnp.int32, (CHUNK, 128), 1)
        tril = (lax.broadcasted_iota(jnp.int32, (CHUNK, CHUNK), 0)
                >= lax.broadcasted_iota(jnp.int32, (CHUNK, CHUNK), 1))
        zero_b = jnp.zeros((CHUNK, 128), BF16)
        parts = []
        for j in range(GMLP_HEADS // 2):
            dsp = dsb[:, 128 * j:128 * (j + 1)]
            vp = vb[:, 128 * j:128 * (j + 1)]
            parts.append(jnp.where(lane < GMLP_HEAD_DIM, _dot(wst_ref[2 * j], dsp),
                                   _dot(wst_ref[2 * j + 1], dsp)))
            lo = jnp.where(lane < GMLP_HEAD_DIM, dsp, zero_b)
            hi = jnp.where(lane < GMLP_HEAD_DIM, zero_b, dsp)
            dws_ref[2 * j] += jnp.where(tril, _dot_nt(lo, vp), 0.0)
            dws_ref[2 * j + 1] += jnp.where(tril, _dot_nt(hi, vp), 0.0)
        dv = jnp.concatenate(parts, axis=1)
        dg_ref[...] += jnp.sum(dv * vhat, axis=0, keepdims=True)
        db_ref[...] += jnp.sum(dv, axis=0, keepdims=True)
        dz_ref[:, D_GMLP:2 * D_GMLP] = (_ln_bwd(dv, vhat, rstd, gam) * _gelu_grad(zv)).astype(BF16)

    return _call(
        body, "gmlp_bwd", (t // CHUNK,),
        [_rows(CHUNK, 2 * D_GMLP), _rows(CHUNK, D_GMLP), _fixed((1, D_GMLP)), _fixed((1, D_GMLP)),
         _fixed((GMLP_HEADS, CHUNK, CHUNK)), _fixed((GMLP_HEADS, CHUNK, CHUNK)), _fixed((CHUNK, D_GMLP))],
        (_rows(CHUNK, 2 * D_GMLP), _fixed((GMLP_HEADS, CHUNK, CHUNK)), _fixed((CHUNK, D_GMLP)),
         _fixed((1, D_GMLP)), _fixed((1, D_GMLP))),
        (SDS((t, 2 * D_GMLP), BF16), SDS((GMLP_HEADS, CHUNK, CHUNK), F32), SDS((CHUNK, D_GMLP), F32),
         SDS((1, D_GMLP), F32), SDS((1, D_GMLP), F32)),
        (zuv, dgm, ln_g, ln_b, wsm, wsmt, bias), sem=("arbitrary",), bg=bg)


def _mixout_fwd(x1, s5o, gm, gab, ua, ub, wmo, g, b, tm, bg=None):
    t = x1.shape[0]

    def body(x_ref, s_ref, m_ref, gab_ref, ua_ref, ub_ref, wmo_ref, g_ref, b_ref,
             xn_ref, xh_ref, rstd_ref, xb_ref):
        ya = _dot(s_ref[...], ua_ref[...])
        yb = _dot(m_ref[...], ub_ref[...])
        mix = (_sigmoid(gab_ref[:, 0:D_MODEL].astype(F32)) * ya
               + _sigmoid(gab_ref[:, D_MODEL:2 * D_MODEL].astype(F32)) * yb)
        r = ALPHA * x_ref[...] + _dot(mix.astype(BF16), wmo_ref[...])
        y, xh, rstd = _ln_fwd(r, g_ref[...], b_ref[...])
        xn_ref[...] = y
        xh_ref[...] = xh
        rstd_ref[...] = rstd
        xb_ref[...] = y.astype(BF16)

    return _call(
        body, "mixout_fwd", (t // tm,),
        [_rows(tm, D_MODEL), _rows(tm, D_SSM), _rows(tm, D_GMLP), _rows(tm, 2 * D_MODEL),
         _resident((D_SSM, D_MODEL)), _resident((D_GMLP, D_MODEL)), _resident((D_MODEL, D_MODEL)),
         _fixed((1, D_MODEL)), _fixed((1, D_MODEL))],
        (_rows(tm, D_MODEL), _rows(tm, D_MODEL), _rows(tm, 1), _rows(tm, D_MODEL)),
        (SDS((t, D_MODEL), F32), SDS((t, D_MODEL), F32), SDS((t, 1), F32), SDS((t, D_MODEL), BF16)),
        (x1, s5o, gm, gab, ua, ub, wmo, g, b), sem=("parallel",), bg=bg)


def _mixout_bwd(dx2, xh, rstd, s5o, gm, gab, ua, ub, wmo, g, tm, bg=None):
    t = dx2.shape[0]

    def body(d_ref, xh_ref, rstd_ref, s_ref, m_ref, gab_ref, ua_ref, ub_ref, wmo_ref, g_ref,
             dx1_ref, dmx_ref, mb_ref, dya_ref, dyb_ref, ds5_ref, dgm_ref, dgab_ref, dg_ref, db_ref):
        @pl.when(pl.program_id(0) == 0)
        def _():
            dg_ref[...] = jnp.zeros_like(dg_ref)
            db_ref[...] = jnp.zeros_like(db_ref)

        dy = d_ref[...]
        xhv = xh_ref[...]
        dr = _ln_bwd(dy, xhv, rstd_ref[...], g_ref[...])
        dg_ref[...] += jnp.sum(dy * xhv, axis=0, keepdims=True)
        db_ref[...] += jnp.sum(dy, axis=0, keepdims=True)
        dx1_ref[...] = ALPHA * dr
        drb = dr.astype(BF16)
        dmx_ref[...] = drb
        dm = _dot_nt(drb, wmo_ref[...])
        ya = _dot(s_ref[...], ua_ref[...])
        yb = _dot(m_ref[...], ub_ref[...])
        sa = _sigmoid(gab_ref[:, 0:D_MODEL].astype(F32))
        sb = _sigmoid(gab_ref[:, D_MODEL:2 * D_MODEL].astype(F32))
        mb_ref[...] = (sa * ya + sb * yb).astype(BF16)
        dya = (dm * sa).astype(BF16)
        dyb = (dm * sb).astype(BF16)
        dya_ref[...] = dya
        dyb_ref[...] = dyb
        dgab_ref[:, 0:D_MODEL] = (dm * ya * sa * (1.0 - sa)).astype(BF16)
        dgab_ref[:, D_MODEL:2 * D_MODEL] = (dm * yb * sb * (1.0 - sb)).astype(BF16)
        ds5_ref[...] = _dot_nt(dya, ua_ref[...])
        dgm_ref[...] = _dot_nt(dyb, ub_ref[...])

    return _call(
        body, "mixout_bwd", (t // tm,),
        [_rows(tm, D_MODEL), _rows(tm, D_MODEL), _rows(tm, 1), _rows(tm, D_SSM), _rows(tm, D_GMLP),
         _rows(tm, 2 * D_MODEL), _resident((D_SSM, D_MODEL)), _resident((D_GMLP, D_MODEL)),
         _resident((D_MODEL, D_MODEL)), _fixed((1, D_MODEL))],
        (_rows(tm, D_MODEL), _rows(tm, D_MODEL), _rows(tm, D_MODEL), _rows(tm, D_MODEL),
         _rows(tm, D_MODEL), _rows(tm, D_SSM), _rows(tm, D_GMLP), _rows(tm, 2 * D_MODEL),
         _fixed((1, D_MODEL)), _fixed((1, D_MODEL))),
        (SDS((t, D_MODEL), F32), SDS((t, D_MODEL), BF16), SDS((t, D_MODEL), BF16),
         SDS((t, D_MODEL), BF16), SDS((t, D_MODEL), BF16), SDS((t, D_SSM), F32),
         SDS((t, D_GMLP), F32), SDS((t, 2 * D_MODEL), BF16),
         SDS((1, D_MODEL), F32), SDS((1, D_MODEL), F32)),
        (dx2, xh, rstd, s5o, gm, gab, ua, ub, wmo, g), sem=("arbitrary",), bg=bg)


def _ple_loss(x3, p, tgt, wpg, wpp, tm, bg=None):
    t = x3.shape[0]

    def body(x_ref, p_ref, t_ref, wpg_ref, wpp_ref, dx_ref, xb_ref, pb_ref, dq_ref, de_ref, loss_ref):
        @pl.when(pl.program_id(0) == 0)
        def _():
            loss_ref[...] = jnp.zeros_like(loss_ref)

        x3v = x_ref[...]
        xb = x3v.astype(BF16)
        pb = p_ref[...].astype(BF16)
        xb_ref[...] = xb
        pb_ref[...] = pb
        s = _sigmoid(_dot(xb, wpg_ref[...]))
        e = _dot(pb, wpp_ref[...])
        diff = x3v + s * e - t_ref[...]
        loss_ref[...] += jnp.sum(diff * diff, axis=0, keepdims=True)
        dout = diff * (1.0 / D_MODEL)
        de_ref[...] = (dout * s).astype(BF16)
        dq = (dout * e * s * (1.0 - s)).astype(BF16)
        dq_ref[...] = dq
        dx_ref[...] = dout + _dot_nt(dq, wpg_ref[...])

    return _call(
        body, "ple_loss", (t // tm,),
        [_rows(tm, D_MODEL), _rows(tm, PLE_DIM), _rows(tm, D_MODEL),
         _resident((D_MODEL, D_MODEL)), _resident((PLE_DIM, D_MODEL))],
        (_rows(tm, D_MODEL), _rows(tm, D_MODEL), _rows(tm, PLE_DIM), _rows(tm, D_MODEL),
         _rows(tm, D_MODEL), _fixed((1, D_MODEL))),
        (SDS((t, D_MODEL), F32), SDS((t, D_MODEL), BF16), SDS((t, PLE_DIM), BF16),
         SDS((t, D_MODEL), BF16), SDS((t, D_MODEL), BF16), SDS((1, D_MODEL), F32)),
        (x3, p, tgt, wpg, wpp), sem=("arbitrary",), bg=bg)


def _s5_discretise(lre, lim, log_dt, bre, bim):
    dt = jnp.exp(log_dt)[:, None]
    mag = jnp.exp(lre * dt)
    abr = mag * jnp.cos(lim * dt)
    abi = mag * jnp.sin(lim * dt)
    nr = abr - 1.0
    ni = abi
    den = lre * lre + lim * lim
    cr = ((nr * lre + ni * lim) / den)[..., None]
    ci = ((ni * lre - nr * lim) / den)[..., None]
    return abr, abi, cr * bre - ci * bim, cr * bim + ci * bre


def _block_diag_in(bb):
    v = bb.reshape(S5_BLOCKS, 8, SSM_STATE, SSM_GROUP_CH).transpose(0, 1, 3, 2)
    return jnp.einsum("bgip,gh->bgihp", v, jnp.eye(8, dtype=bb.dtype)).reshape(
        S5_BLOCKS, S5_BLOCK_IN, S5_BLOCK_ST)


def _block_diag_in_t(dm):
    v = dm.reshape(S5_BLOCKS, 8, SSM_GROUP_CH, 8, SSM_STATE)
    d = jnp.einsum("bgihp,gh->bgip", v, jnp.eye(8, dtype=dm.dtype))
    return d.transpose(0, 1, 3, 2).reshape(SSM_GROUPS, SSM_STATE, SSM_GROUP_CH)


def _block_diag_out(cc):
    v = cc.reshape(S5_BLOCKS, 8, SSM_GROUP_CH, SSM_STATE)
    return jnp.einsum("bgip,gh->bgphi", v, jnp.eye(8, dtype=cc.dtype)).reshape(
        S5_BLOCKS, S5_BLOCK_ST, S5_BLOCK_IN)


def _block_diag_out_t(dn):
    v = dn.reshape(S5_BLOCKS, 8, SSM_STATE, 8, SSM_GROUP_CH)
    d = jnp.einsum("bgphi,gh->bgip", v, jnp.eye(8, dtype=dn.dtype))
    return d.reshape(SSM_GROUPS, SSM_GROUP_CH, SSM_STATE)


def _s5_setup(lre, lim, log_dt, bre, bim, cre, cim, d_skip, glu_w, glu_b, tb):
    seg = tb // 8
    abr, abi, bbr, bbi = _s5_discretise(lre, lim, log_dt, bre, bim)
    pr, pi = abr, abi
    for _ in range(int(math.log2(seg))):
        pr, pi = pr * pr - pi * pi, 2.0 * pr * pi
    rows = jnp.arange(tb)
    src = (rows % 8) * seg + rows // 8
    perm = (src[:, None] == jnp.arange(tb)[None, :]).astype(BF16)
    mre = _block_diag_in(bbr)
    mim = _block_diag_in(bbi)
    nre = _block_diag_out(cre)
    nim = _block_diag_out(cim)
    return {
        "perm": perm, "permt": perm.T,
        "mre": mre.astype(BF16), "mim": mim.astype(BF16),
        "mtre": mre.transpose(0, 2, 1).astype(BF16), "mtim": mim.transpose(0, 2, 1).astype(BF16),
        "nre": nre.astype(BF16), "nim": nim.astype(BF16),
        "ntre": nre.transpose(0, 2, 1).astype(BF16), "ntim": nim.transpose(0, 2, 1).astype(BF16),
        "a": jnp.stack([abr.reshape(-1), abi.reshape(-1)]),
        "ap": jnp.stack([pr.reshape(-1), pi.reshape(-1)]),
        "dskip": d_skip.reshape(1, D_SSM), "glu_w": glu_w, "glu_wt": glu_w.T,
        "glu_b": glu_b.reshape(1, D_SSM),
    }


BIG = ("ffn1_w_in", "ffn1_w_out", "mix_w_in", "ssm_glu_w", "up_a", "up_b", "mix_w_out",
       "ffn2_w_in", "ffn2_w_out", "ple_w_proj", "ple_w_gate")
BIG_AXIS = {"ffn1_w_in": 1, "ffn1_w_out": 0, "mix_w_in": 1, "ssm_glu_w": 0, "up_a": 1, "up_b": 1,
            "mix_w_out": 0, "ffn2_w_in": 1, "ffn2_w_out": 0, "ple_w_proj": 1, "ple_w_gate": 0}
SMALL = ("ln1_g", "ln1_b", "ssm_lambda_re", "ssm_lambda_im", "ssm_log_dt", "ssm_b_re", "ssm_b_im",
         "ssm_c_re", "ssm_c_im", "ssm_d", "ssm_glu_b", "gmlp_ln_g", "gmlp_ln_b", "gmlp_w_s",
         "gmlp_b_s", "ln2_g", "ln2_b", "ln3_g", "ln3_b")
SMALL_2D = {"ln1_g": (1, 1024), "ln1_b": (1, 1024), "ssm_lambda_re": (32, 64), "ssm_lambda_im": (32, 64),
            "ssm_log_dt": (1, 32), "ssm_b_re": (32, 1024), "ssm_b_im": (32, 1024), "ssm_c_re": (32, 1024),
            "ssm_c_im": (32, 1024), "ssm_d": (1, 512), "ssm_glu_b": (1, 512), "gmlp_ln_g": (1, 512),
            "gmlp_ln_b": (1, 512), "gmlp_w_s": (1024, 128), "gmlp_b_s": (8, 128), "ln2_g": (1, 1024),
            "ln2_b": (1, 1024), "ln3_g": (1, 1024), "ln3_b": (1, 1024)}


def _place():
    return lax.axis_index("x"), lax.axis_index("y"), lax.axis_index("c")


def _other_chips(x, y):
    return [(1 - x, y), (x, 1 - y), (1 - x, 1 - y)]


def _window(ref, shard_shape, axis, chip, half):
    r, c = shard_shape
    hr = r // 2
    if axis == 0:
        if half is None:
            return ref.at[pl.ds(chip * r, r), :]
        return ref.at[pl.ds(chip * r + half * hr, hr), :]
    if half is None:
        return ref.at[:, pl.ds(chip * c, c)]
    return ref.at[pl.ds(half * hr, hr), pl.ds(chip * c, c)]


def _gather_weights(shards, axes):
    n = len(shards)
    shapes = [s.shape for s in shards]
    full = [(4 * r, c) if ax == 0 else (r, 4 * c) for (r, c), ax in zip(shapes, axes)]

    def remote(sems, i, k, src, dst, to):
        return pltpu.make_async_remote_copy(src_ref=src, dst_ref=dst, send_sem=sems[0].at[6 * i + k],
                                            recv_sem=sems[1].at[6 * i + k], device_id=to, device_id_type=MESH)

    def own_copies(ins, outs, sems):
        x, y, c = _place()
        me = 2 * x + y
        cps = []
        for i in range(n):
            hr = shapes[i][0] // 2
            mine = ins[i].at[pl.ds(c * hr, hr), :]
            for j, (cx, cy) in enumerate(_other_chips(x, y)):
                cps.append(remote(sems, i, j, mine, _window(outs[i], shapes[i], axes[i], me, c), (cx, cy, c)))
        local = [pltpu.make_async_copy(ins[i], _window(outs[i], shapes[i], axes[i], me, None), sems[2].at[i])
                 for i in range(n)]
        return cps, local

    def start(ins, outs, sems):
        cps, local = own_copies(ins, outs, sems)
        for cp in local + cps:
            cp.start()

    def finish(ins, outs, sems):
        x, y, c = _place()
        sibling = (x, y, 1 - c)
        passed = []
        for j, (cx, cy) in enumerate(_other_chips(x, y)):
            for i in range(n):
                w = _window(outs[i], shapes[i], axes[i], 2 * cx + cy, c)
                remote(sems, i, j, w, w, (cx, cy, c)).wait_recv()
                cp = remote(sems, i, 3 + j, w, w, sibling)
                cp.start()
                passed.append(cp)
        for j, (cx, cy) in enumerate(_other_chips(x, y)):
            for i in range(n):
                w = _window(outs[i], shapes[i], axes[i], 2 * cx + cy, 1 - c)
                remote(sems, i, 3 + j, w, w, sibling).wait_recv()
        cps, local = own_copies(ins, outs, sems)
        for cp in cps + passed:
            cp.wait_send()
        for cp in local:
            cp.wait()

    return _Exchange(shards, [SDS(f, BF16) for f in full],
                     [pltpu.SemaphoreType.DMA((6 * n,)), pltpu.SemaphoreType.DMA((6 * n,)),
                      pltpu.SemaphoreType.DMA((n,))], start, finish)


def _scatter_grads(parts, shapes, axes):
    n = len(parts)

    def copies(ins, outs, sems):
        x, y, c = _place()
        return [pltpu.make_async_remote_copy(
            src_ref=_window(ins[i], shapes[i], axes[i], 2 * cx + cy, None), dst_ref=outs[i].at[j],
            send_sem=sems[0].at[3 * i + j], recv_sem=sems[1].at[3 * i + j],
            device_id=(cx, cy, c), device_id_type=MESH)
            for i in range(n) for j, (cx, cy) in enumerate(_other_chips(x, y))]

    def start(ins, outs, sems):
        for cp in copies(ins, outs, sems):
            cp.start()

    def finish(ins, outs, sems):
        for cp in copies(ins, outs, sems):
            cp.wait()

    return _Exchange(parts, [SDS((3,) + tuple(s), BF16) for s in shapes],
                     [pltpu.SemaphoreType.DMA((3 * n,)), pltpu.SemaphoreType.DMA((3 * n,))], start, finish)


def _swap_halves(parts, shapes, axes):
    n = len(parts)

    def copies(ins, outs, sems):
        x, y, c = _place()
        cps = []
        for i in range(n):
            r, _ = shapes[i]
            hr = r // 2
            if axes[i] == 0:
                cps += [pltpu.make_async_remote_copy(
                    src_ref=ins[i].at[pl.ds(k * r + (1 - c) * hr, hr), :], dst_ref=outs[i].at[k],
                    send_sem=sems[0].at[i], recv_sem=sems[1].at[i], device_id=(x, y, 1 - c),
                    device_id_type=MESH) for k in range(4)]
            else:
                cps.append(pltpu.make_async_remote_copy(
                    src_ref=ins[i].at[pl.ds((1 - c) * hr, hr), :], dst_ref=outs[i],
                    send_sem=sems[0].at[i], recv_sem=sems[1].at[i], device_id=(x, y, 1 - c),
                    device_id_type=MESH))
        return cps

    def start(ins, outs, sems):
        for cp in copies(ins, outs, sems):
            cp.start()

    def finish(ins, outs, sems):
        x, y, c = _place()
        for i in range(n):
            pltpu.make_async_remote_copy(src_ref=outs[i], dst_ref=outs[i], send_sem=sems[0].at[i],
                                         recv_sem=sems[1].at[i], device_id=(x, y, 1 - c),
                                         device_id_type=MESH).wait()

    out = [SDS((4, r // 2, c), BF16) if ax == 0 else SDS((r // 2, 4 * c), BF16)
           for (r, c), ax in zip(shapes, axes)]
    return _Exchange(parts, out, [pltpu.SemaphoreType.DMA((n,)), pltpu.SemaphoreType.DMA((n,))], start, finish)


def _scatter_halves(pres, shapes):
    n = len(pres)

    def copies(ins, outs, sems):
        x, y, c = _place()
        return [pltpu.make_async_remote_copy(
            src_ref=ins[i].at[1 + j], dst_ref=outs[i].at[j], send_sem=sems[0].at[3 * i + j],
            recv_sem=sems[1].at[3 * i + j], device_id=(cx, cy, c), device_id_type=MESH)
            for i in range(n) for j, (cx, cy) in enumerate(_other_chips(x, y))]

    def start(ins, outs, sems):
        for cp in copies(ins, outs, sems):
            cp.start()

    def finish(ins, outs, sems):
        for cp in copies(ins, outs, sems):
            cp.wait()

    return _Exchange(pres, [SDS((3, r // 2, c), BF16) for r, c in shapes],
                     [pltpu.SemaphoreType.DMA((3 * n,)), pltpu.SemaphoreType.DMA((3 * n,))], start, finish)


def _swap_with_sibling(arrs):
    n = len(arrs)

    def copies(ins, outs, sems):
        x, y, c = _place()
        return [pltpu.make_async_remote_copy(src_ref=ins[i], dst_ref=outs[i], send_sem=sems[0].at[i],
                                             recv_sem=sems[1].at[i], device_id=(x, y, 1 - c),
                                             device_id_type=MESH) for i in range(n)]

    def start(ins, outs, sems):
        for cp in copies(ins, outs, sems):
            cp.start()

    def finish(ins, outs, sems):
        for cp in copies(ins, outs, sems):
            cp.wait()

    return _Exchange(arrs, [SDS(a.shape, a.dtype) for a in arrs],
                     [pltpu.SemaphoreType.DMA((n,)), pltpu.SemaphoreType.DMA((n,))], start, finish)


def _gather_small(arrs):
    n = len(arrs)

    def copy(sems, outs, i, k, block, to, src=None):
        px, py, pc = block
        dst = outs[i].at[4 * px + 2 * py + pc]
        return pltpu.make_async_remote_copy(
            src_ref=dst if src is None else src, dst_ref=dst, send_sem=sems[0].at[7 * i + k],
            recv_sem=sems[1].at[7 * i + k], device_id=to, device_id_type=MESH)

    def own_copies(ins, outs, sems):
        x, y, c = _place()
        cps = []
        for i in range(n):
            cps.append(copy(sems, outs, i, 0, (x, y, c), (x, y, 1 - c), src=ins[i]))
            for j, (cx, cy) in enumerate(_other_chips(x, y)):
                cps.append(copy(sems, outs, i, 1 + j, (x, y, c), (cx, cy, c), src=ins[i]))
        local = [pltpu.make_async_copy(ins[i], outs[i].at[4 * x + 2 * y + c], sems[2].at[i]) for i in range(n)]
        return cps, local

    def start(ins, outs, sems):
        cps, local = own_copies(ins, outs, sems)
        for cp in local + cps:
            cp.start()

    def finish(ins, outs, sems):
        x, y, c = _place()
        passed = []
        for j, (cx, cy) in enumerate(_other_chips(x, y)):
            for i in range(n):
                copy(sems, outs, i, 1 + j, (cx, cy, c), (x, y, c)).wait_recv()
                cp = copy(sems, outs, i, 4 + j, (cx, cy, c), (x, y, 1 - c))
                cp.start()
                passed.append(cp)
        for i in range(n):
            copy(sems, outs, i, 0, (x, y, 1 - c), (x, y, c)).wait_recv()
            for j, (cx, cy) in enumerate(_other_chips(x, y)):
                copy(sems, outs, i, 4 + j, (cx, cy, 1 - c), (x, y, c)).wait_recv()
        cps, local = own_copies(ins, outs, sems)
        for cp in cps + passed:
            cp.wait_send()
        for cp in local:
            cp.wait()

    return _Exchange(arrs, [SDS((N_DEV,) + a.shape, F32) for a in arrs],
                     [pltpu.SemaphoreType.DMA((7 * n,)), pltpu.SemaphoreType.DMA((7 * n,)),
                      pltpu.SemaphoreType.DMA((n,))], start, finish)


def _local_step(x, p, tgt, wb, ws, shards=None):
    bsz, seq, _ = x.shape
    t = bsz * seq
    tm = min(256, t)
    tb = min(256, seq)
    x0 = x.reshape(t, D_MODEL)
    p0 = p.reshape(t, PLE_DIM)
    tg = tgt.reshape(t, D_MODEL)
    row = lambda v: v.reshape(1, -1)
    dist = shards is not None
    wb = dict(wb)
    recv, sums, other, gathered = {}, {}, {}, {}
    gb = {}
    gs = {}
    shape_of, axis_of = {}, {}
    chip = None
    if dist:
        shape_of = {k: tuple(shards[k].shape) for k in BIG}
        axis_of = dict(BIG_AXIS)
        for q in range(LAST_PIECES):
            shape_of[LAST_PIECE % q] = (D_MODEL // LAST_PIECES, shape_of["ffn1_w_in"][1])
            axis_of[LAST_PIECE % q] = 1
        xi, yi, ci = _place()
        chip = (2 * xi + yi).astype(jnp.int32).reshape(1)
        ids = jnp.stack([2 * xi + yi] + [2 * cx + cy for cx, cy in _other_chips(xi, yi)] + [ci]).astype(jnp.int32)
    halfbuf, pre = {}, {}

    def gather(names):
        return _gather_weights([shards[k] for k in names], [BIG_AXIS[k] for k in names]) if dist else None

    def exchange(scat=(), swap=(), halves=(), scat2=(), swap2=(), extra=None):
        if not dist:
            return None, []
        parts, tags = [], []
        if scat:
            parts.append(_scatter_grads([gb[k][1] for k in scat], [shape_of[k] for k in scat],
                                        [axis_of[k] for k in scat]))
            tags.append((recv, scat))
        if swap:
            for k in swap:
                sums[k] = _sum_blocks(gb[k][0], recv[k], shape_of[k], axis_of[k], chip, "sum_" + k)
            parts.append(_swap_with_sibling([sums[k] for k in swap]))
            tags.append((other, swap))
        if halves:
            parts.append(_swap_halves([gb[k][1] for k in halves], [shape_of[k] for k in halves],
                                      [axis_of[k] for k in halves]))
            tags.append((halfbuf, halves))
        if scat2:
            for k in scat2:
                pre[k] = _presum(gb[k][0], halfbuf[k], shape_of[k], axis_of[k], ids, "presum_" + k)
            parts.append(_scatter_halves([pre[k][1] for k in scat2], [shape_of[k] for k in scat2]))
            tags.append((recv, scat2))
        if swap2:
            for k in swap2:
                sums[k] = _sum_half(pre[k][0], recv[k], "sum_" + k)
            parts.append(_swap_with_sibling([sums[k] for k in swap2]))
            tags.append((other, swap2))
        if extra is not None:
            parts.append(extra[0])
            tags.append((extra[1], extra[2]))
        return _join(parts), tags

    def take(ex_tags, got):
        ex, tags = ex_tags
        if ex is not None:
            for (dst, names), (o0, o1) in zip(tags, ex.cuts):
                dst.update(zip(names, got[o0:o1]))

    tril = jnp.tril(jnp.ones((CHUNK, CHUNK), dtype=bool))
    wsm = jnp.where(tril[None], ws["gmlp_w_s"], 0.0)
    wsm_b = wsm.astype(BF16)
    wsmt_b = wsm.transpose(0, 2, 1).astype(BF16)
    bias = jnp.repeat(ws["gmlp_b_s"].T, GMLP_HEAD_DIM, axis=1)

    if dist:
        names = ("ffn1_w_in", "ffn1_w_out")
        wb.update(zip(names, _run_exchange(gather(names), "gather_ffn1")))
    names = ("mix_w_in", "ssm_glu_w", "up_a", "up_b", "mix_w_out")
    (x1, xh1, rstd1, x0b, h1), got = _ffn_fwd(x0, wb["ffn1_w_in"], wb["ffn1_w_out"], row(ws["ln1_g"]),
                                              row(ws["ln1_b"]), tm, "ffn1_fwd", gather(names))
    wb.update(zip(names, got))
    sp = _s5_setup(ws["ssm_lambda_re"], ws["ssm_lambda_im"], ws["ssm_log_dt"], ws["ssm_b_re"],
                   ws["ssm_b_im"], ws["ssm_c_re"], ws["ssm_c_im"], ws["ssm_d"], wb["ssm_glu_w"],
                   ws["ssm_glu_b"], tb)
    names = ("ffn2_w_out",)
    (x1b, za, zuv, gab), got = _mixin_fwd(x1, wb["mix_w_in"], tm, gather(names))
    wb.update(zip(names, got))
    names = ("ffn2_w_in",)
    (s5o, y2p, carries), got = _s5_fwd(za, sp, bsz, seq, tb, gather(names))
    wb.update(zip(names, got))
    names = ("ple_w_gate", "ple_w_proj")
    (gm,), got = _gmlp_fwd(zuv, row(ws["gmlp_ln_g"]), row(ws["gmlp_ln_b"]), wsm_b, bias, gather(names))
    wb.update(zip(names, got))
    (x2, xh2, rstd2, x2b), _ = _mixout_fwd(x1, s5o, gm, gab, wb["up_a"], wb["up_b"], wb["mix_w_out"],
                                           row(ws["ln2_g"]), row(ws["ln2_b"]), tm)
    (x3, xh3, rstd3, _, h2), _ = _ffn_fwd(x2, wb["ffn2_w_in"], wb["ffn2_w_out"], row(ws["ln3_g"]),
                                          row(ws["ln3_b"]), tm, "ffn2_fwd")
    (dx3, x3b, pb, dq, de, loss_rows), _ = _ple_loss(x3, p0, tg, wb["ple_w_gate"], wb["ple_w_proj"], tm)
    gb["ple_w_gate"], _ = _tn_matmul(x3b, dq, "dw_ple_gate", 1024, 1024)
    gb["ple_w_proj"], _ = _tn_matmul(pb, de, "dw_ple_proj", 256, 1024)
    et = exchange(scat=("ple_w_gate", "ple_w_proj"))
    (dx2, dh2, a2, df2, gs["ln3_g"], gs["ln3_b"]), got = _ffn_bwd(
        dx3, xh3, rstd3, h2, wb["ffn2_w_in"], wb["ffn2_w_out"], row(ws["ln3_g"]), tm, "ffn2_bwd", et[0])
    take(et, got)
    gb["ffn2_w_out"], _ = _tn_matmul(a2, df2, "dw_ffn2_out", 1408, 1024)
    et = exchange(scat=("ffn2_w_out",))
    gb["ffn2_w_in"], got = _tn_matmul(x2b, dh2, "dw_ffn2_in", 1024, 1408, bg=et[0])
    take(et, got)
    et = exchange(swap=("ple_w_gate", "ple_w_proj", "ffn2_w_out"))
    (dx1a, dmx, mb, dya, dyb, ds5, dgm, dgab, gs["ln2_g"], gs["ln2_b"]), got = _mixout_bwd(
        dx2, xh2, rstd2, s5o, gm, gab, wb["up_a"], wb["up_b"], wb["mix_w_out"], row(ws["ln2_g"]), tm, et[0])
    take(et, got)
    gb["mix_w_out"], _ = _tn_matmul(mb, dmx, "dw_mix_out", 1024, 1024)
    gb["up_a"], _ = _tn_matmul(s5o, dya, "dw_up_a", 512, 1024)
    gb["up_b"], _ = _tn_matmul(gm, dyb, "dw_up_b", 512, 1024)
    et = exchange(scat=("ffn2_w_in",))
    (dza, dmr, dmi, dnr, dni, da, ddsk, dgw, dgb), got = _s5_bwd(za, y2p, ds5, carries, sp, bsz, seq, tb, et[0])
    take(et, got)
    gb["ssm_glu_w"] = (dgw, dgw.astype(BF16))
    et = exchange(scat=("mix_w_out", "up_a"), swap=("ffn2_w_in",))
    (dzuv, dws, dbias, gs["gmlp_ln_g"], gs["gmlp_ln_b"]), got = _gmlp_bwd(
        zuv, dgm, row(ws["gmlp_ln_g"]), row(ws["gmlp_ln_b"]), wsm_b, wsmt_b, bias, et[0])
    take(et, got)
    et = exchange(scat=("up_b", "ssm_glu_w"))
    (dx1,), got = _mixin_bwd(dx1a, dza, dzuv, dgab, wb["mix_w_in"], tm, et[0])
    take(et, got)
    g_mi, _ = _tn_matmul(x1b, dza, "dw_mix_in_a", 1024, 512, 0, 3584)
    g_mi, _ = _tn_matmul(x1b, dzuv, "dw_mix_in_uv", 1024, 512, 1, 3584, g_mi)
    et = exchange(swap=("mix_w_out", "up_a", "up_b", "ssm_glu_w"))
    gb["mix_w_in"], got = _tn_matmul(x1b, dgab, "dw_mix_in_g", 1024, 512, 3, 3584, g_mi, bg=et[0])
    take(et, got)
    et = exchange(scat=("mix_w_in",))
    (dx0, dh1, a1, df1, gs["ln1_g"], gs["ln1_b"]), got = _ffn_bwd(
        dx1, xh1, rstd1, h1, wb["ffn1_w_in"], wb["ffn1_w_out"], row(ws["ln1_g"]), tm, "ffn1_bwd", et[0])
    take(et, got)

    d_abr = da[0].sum(axis=0).reshape(SSM_GROUPS, SSM_STATE)
    d_abi = da[1].sum(axis=0).reshape(SSM_GROUPS, SSM_STATE)
    _, vjp = jax.vjp(_s5_discretise, ws["ssm_lambda_re"], ws["ssm_lambda_im"], ws["ssm_log_dt"],
                     ws["ssm_b_re"], ws["ssm_b_im"])
    (gs["ssm_lambda_re"], gs["ssm_lambda_im"], gs["ssm_log_dt"], gs["ssm_b_re"], gs["ssm_b_im"]) = vjp(
        (d_abr, d_abi, _block_diag_in_t(dmr), _block_diag_in_t(dmi)))
    gs["ssm_c_re"] = _block_diag_out_t(dnr)
    gs["ssm_c_im"] = _block_diag_out_t(dni)
    gs["ssm_d"] = ddsk
    gs["ssm_glu_b"] = dgb
    gs["gmlp_w_s"] = dws
    gs["gmlp_b_s"] = dbias.reshape(CHUNK, GMLP_HEADS, GMLP_HEAD_DIM).sum(axis=-1).T
    gs = {k: gs[k].reshape(SMALL_2D[k]) for k in SMALL}
    grad_x = dx0.reshape(bsz, seq, D_MODEL)

    if not dist:
        gb["ffn1_w_out"], _ = _tn_matmul(a1, df1, "dw_ffn1_out", 1408, 1024)
        gb["ffn1_w_in"], _ = _tn_matmul(x0b, dh1, "dw_ffn1_in", 1024, 1408)
        return loss_rows, grad_x, gb, gs, sums, other, gathered, None
    small = SMALL + ("loss_rows",)
    et = exchange(swap=("mix_w_in",), extra=(_gather_small([gs[k] for k in SMALL] + [loss_rows]), gathered, small))
    gb["ffn1_w_out"], got = _tn_matmul(a1, df1, "dw_ffn1_out", 1408, 1024, bg=et[0])
    take(et, got)
    last = ["ffn1_w_out"] + [LAST_PIECE % q for q in range(LAST_PIECES)]
    for i in range(1, len(last) + 3):
        stage = lambda d: tuple(last[i - d:i - d + 1]) if 0 <= i - d < len(last) else ()
        et = exchange(halves=stage(1), scat2=stage(2), swap2=stage(3))
        if i < len(last):
            gb[last[i]], got = _tn_matmul(x0b, dh1, "dw_" + last[i], D_MODEL // LAST_PIECES, 1408, bg=et[0],
                                          a_cols=(i - 1, 1))
        else:
            got = _run_exchange(et[0], "reduce_last_%d" % (i - len(last)))
        take(et, got)
    return loss_rows, grad_x, gb, gs, sums, other, gathered, ids


def _adamw(w, g, m, v):
    m = ADAM_B1 * m + (1.0 - ADAM_B1) * g
    v = ADAM_B2 * v + (1.0 - ADAM_B2) * (g * g)
    m_hat = m / ADAM_C1
    v_hat = v / ADAM_C2
    delta = -ADAM_LR * (m_hat / (jnp.sqrt(v_hat) + ADAM_EPS) + ADAM_WD * w)
    return delta, m, v


def _sum_blocks(part, recv, shape, axis, chip, name):
    r, c = shape
    rb = r // 8

    def body(chip_ref, p_ref, r_ref, o_ref):
        o_ref[...] = (p_ref[...] + r_ref[0].astype(F32) + r_ref[1].astype(F32) + r_ref[2].astype(F32))

    if axis == 0:
        own = pl.BlockSpec((rb, c), lambda i, k: (k[0] * 8 + i, 0))
    else:
        own = pl.BlockSpec((rb, c), lambda i, k: (i, k[0]))
    grid_spec = pltpu.PrefetchScalarGridSpec(
        num_scalar_prefetch=1, grid=(8,),
        in_specs=[own, pl.BlockSpec((3, rb, c), lambda i, k: (0, i, 0))],
        out_specs=pl.BlockSpec((rb, c), lambda i, k: (i, 0)))
    return pl.pallas_call(body, name=name, out_shape=SDS((r, c), F32), grid_spec=grid_spec,
                          compiler_params=_params(("parallel",)))(chip, part, recv)


def _presum(part, half, shape, axis, ids, name):
    r, c = shape
    rb = r // 4

    def body(ids_ref, p_ref, h_ref, of_ref, ob_ref):
        s = p_ref[...] + h_ref[...].astype(F32)
        ob_ref[...] = s.astype(BF16)

        @pl.when(pl.program_id(1) == 0)
        def _():
            of_ref[...] = s

    if axis == 0:
        p_spec = pl.BlockSpec((rb, c), lambda i, t, ids: (ids[t] * 4 + ids[4] * 2 + i, 0))
        h_spec = pl.BlockSpec((None, rb, c), lambda i, t, ids: (ids[t], i, 0))
    else:
        p_spec = pl.BlockSpec((rb, c), lambda i, t, ids: (ids[4] * 2 + i, ids[t]))
        h_spec = pl.BlockSpec((rb, c), lambda i, t, ids: (i, ids[t]))
    grid_spec = pltpu.PrefetchScalarGridSpec(
        num_scalar_prefetch=1, grid=(2, 4), in_specs=[p_spec, h_spec],
        out_specs=(pl.BlockSpec((rb, c), lambda i, t, ids: (i, 0)),
                   pl.BlockSpec((None, rb, c), lambda i, t, ids: (t, i, 0))))
    return pl.pallas_call(body, name=name, out_shape=(SDS((r // 2, c), F32), SDS((4, r // 2, c), BF16)),
                          grid_spec=grid_spec, compiler_params=_params(("parallel", "arbitrary")))(ids, part, half)


def _sum_half(pre, recv, name):
    hr, c = pre.shape
    rb = hr // 2

    def body(p_ref, r_ref, o_ref):
        o_ref[...] = (p_ref[...] + r_ref[0].astype(F32) + r_ref[1].astype(F32) + r_ref[2].astype(F32))

    spec = pl.BlockSpec((rb, c), lambda i: (i, 0))
    return pl.pallas_call(body, name=name, grid=(2,), out_shape=SDS((hr, c), F32),
                          in_specs=[spec, pl.BlockSpec((3, rb, c), lambda i: (0, i, 0))], out_specs=spec,
                          compiler_params=_params(("parallel",)))(pre, recv)


def _adam_halves(w, mine, oth, m, v, ids, name, piece=0, prev=None):
    r, c = w.shape
    rb = mine.shape[0] // 2

    def body(ids_ref, w_ref, a_ref, b_ref, m_ref, v_ref, *rest):
        g_ref, d_ref, nm_ref, nv_ref = rest[-4:]
        g = jnp.where(pl.program_id(0) // 2 == ids_ref[4], a_ref[...], b_ref[...])
        g_ref[...] = g
        d_ref[...], nm_ref[...], nv_ref[...] = _adamw(w_ref[...], g, m_ref[...], v_ref[...])

    whole = pl.BlockSpec((rb, c), lambda i, ids: (i + 4 * piece, 0))
    part = pl.BlockSpec((rb, c), lambda i, ids: (i % 2, 0))
    in_specs = [whole, part, part, whole, whole]
    args = [w, mine, oth, m, v]
    aliases = {}
    if prev is not None:
        in_specs += [pl.BlockSpec(memory_space=pl.ANY)] * 4
        args += list(prev)
        aliases = {6: 0, 7: 1, 8: 2, 9: 3}
    grid_spec = pltpu.PrefetchScalarGridSpec(num_scalar_prefetch=1, grid=(4,), in_specs=in_specs,
                                             out_specs=(whole,) * 4)
    return pl.pallas_call(body, name=name, out_shape=tuple(SDS((r, c), F32) for _ in range(4)),
                          grid_spec=grid_spec, input_output_aliases=aliases,
                          compiler_params=_params(("parallel",)))(ids, *args)


def _adam_big(w, ga, gb, m, v, name, piece=0, prev=None):
    r, c = w.shape
    pr = ga.shape[0]
    steps = 8 if pr == r else 2
    rb = pr // steps
    off = piece * steps

    def body(w_ref, ga_ref, gb_ref, m_ref, v_ref, *rest):
        g_ref, d_ref, nm_ref, nv_ref = rest[-4:]
        g = ga_ref[...] + gb_ref[...]
        g_ref[...] = g
        d_ref[...], nm_ref[...], nv_ref[...] = _adamw(w_ref[...], g, m_ref[...], v_ref[...])

    whole = pl.BlockSpec((rb, c), lambda i: (i + off, 0))
    part = pl.BlockSpec((rb, c), lambda i: (i, 0))
    in_specs = [whole, part, part, whole, whole]
    args = [w, ga, gb, m, v]
    aliases = {}
    if prev is not None:
        in_specs += [pl.BlockSpec(memory_space=pl.ANY)] * 4
        args += list(prev)
        aliases = {5: 0, 6: 1, 7: 2, 8: 3}
    return pl.pallas_call(
        body, name=name, grid=(steps,), out_shape=tuple(SDS((r, c), F32) for _ in range(4)),
        in_specs=in_specs, out_specs=(whole,) * 4, input_output_aliases=aliases,
        compiler_params=_params(("parallel",)),
    )(*args)


def _adam_small(ws, gathered, ms, vs):
    n = len(ws)

    def body(*refs):
        w_refs, g_refs, m_refs, v_refs = refs[:n], refs[n:2 * n], refs[2 * n:3 * n], refs[3 * n:4 * n]
        outs = refs[4 * n:]
        for i in range(n):
            g = g_refs[i][0]
            for d in range(1, N_DEV):
                g = g + g_refs[i][d]
            delta, nm, nv = _adamw(w_refs[i][...], g, m_refs[i][...], v_refs[i][...])
            outs[i][...] = g
            outs[n + i][...] = delta
            outs[2 * n + i][...] = nm
            outs[3 * n + i][...] = nv

    vmem = pl.BlockSpec(memory_space=pltpu.VMEM)
    shapes = [w.shape for w in ws]
    return pl.pallas_call(
        body, name="adam_small", out_shape=tuple(SDS(s, F32) for s in shapes * 4),
        in_specs=[vmem] * (4 * n), out_specs=tuple([vmem] * (4 * n)),
        compiler_params=pltpu.CompilerParams(vmem_limit_bytes=VMEM_LIMIT_BYTES),
    )(*ws, *gathered, *ms, *vs)


def _sum_loss(gathered):
    def body(g_ref, o_ref):
        tot = g_ref[0]
        for d in range(1, N_DEV):
            tot = tot + g_ref[d]
        o_ref[...] = (0.5 / D_MODEL) * jnp.sum(tot, axis=1, keepdims=True)

    vmem = pl.BlockSpec(memory_space=pltpu.VMEM)
    return pl.pallas_call(body, name="sum_loss", out_shape=SDS((1, 1), F32), in_specs=[vmem],
                          out_specs=vmem)(gathered)


def kernel(x, p, ffn1_w_in, ffn1_w_out, ln1_g, ln1_b, mix_w_in, ssm_lambda_re, ssm_lambda_im, ssm_log_dt, ssm_b_re, ssm_b_im, ssm_c_re, ssm_c_im, ssm_d, ssm_glu_w, ssm_glu_b, gmlp_ln_g, gmlp_ln_b, gmlp_w_s, gmlp_b_s, up_a, up_b, mix_w_out, ln2_g, ln2_b, ffn2_w_in, ffn2_w_out, ln3_g, ln3_b, ple_w_proj, ple_w_gate, loss_target, m_ffn1_w_in, m_ffn1_w_out, m_ln1_g, m_ln1_b, m_mix_w_in, m_ssm_lambda_re, m_ssm_lambda_im, m_ssm_log_dt, m_ssm_b_re, m_ssm_b_im, m_ssm_c_re, m_ssm_c_im, m_ssm_d, m_ssm_glu_w, m_ssm_glu_b, m_gmlp_ln_g, m_gmlp_ln_b, m_gmlp_w_s, m_gmlp_b_s, m_up_a, m_up_b, m_mix_w_out, m_ln2_g, m_ln2_b, m_ffn2_w_in, m_ffn2_w_out, m_ln3_g, m_ln3_b, m_ple_w_proj, m_ple_w_gate, v_ffn1_w_in, v_ffn1_w_out, v_ln1_g, v_ln1_b, v_mix_w_in, v_ssm_lambda_re, v_ssm_lambda_im, v_ssm_log_dt, v_ssm_b_re, v_ssm_b_im, v_ssm_c_re, v_ssm_c_im, v_ssm_d, v_ssm_glu_w, v_ssm_glu_b, v_gmlp_ln_g, v_gmlp_ln_b, v_gmlp_w_s, v_gmlp_b_s, v_up_a, v_up_b, v_mix_w_out, v_ln2_g, v_ln2_b, v_ffn2_w_in, v_ffn2_w_out, v_ln3_g, v_ln3_b, v_ple_w_proj, v_ple_w_gate):
    given = dict(locals())
    order = ("ffn1_w_in", "ffn1_w_out", "ln1_g", "ln1_b", "mix_w_in", "ssm_lambda_re", "ssm_lambda_im",
             "ssm_log_dt", "ssm_b_re", "ssm_b_im", "ssm_c_re", "ssm_c_im", "ssm_d", "ssm_glu_w", "ssm_glu_b",
             "gmlp_ln_g", "gmlp_ln_b", "gmlp_w_s", "gmlp_b_s", "up_a", "up_b", "mix_w_out", "ln2_g", "ln2_b",
             "ffn2_w_in", "ffn2_w_out", "ln3_g", "ln3_b", "ple_w_proj", "ple_w_gate")
    assert set(order) == set(BIG + SMALL)

    shard = {k: given[k][0] for k in BIG}
    shard_b = {k: shard[k].astype(BF16) for k in BIG}
    ws = {k: given[k][0] for k in SMALL}
    loss_rows, grad_x, gb, gs, sums, other, gathered, ids = _local_step(
        x, given["p"][0], loss_target, {}, ws, shard_b)

    out = {}
    for k in BIG:
        moments = (given["m_" + k][0], given["v_" + k][0])
        if k == "ffn1_w_out":
            out[k] = _adam_halves(shard[k], sums[k], other[k], *moments, ids, "adam_" + k)
        elif k == "ffn1_w_in":
            for q in range(LAST_PIECES):
                kq = LAST_PIECE % q
                out[k] = _adam_halves(shard[k], sums[kq], other[kq], *moments, ids, "adam_" + kq, q, out.get(k))
        else:
            out[k] = _adam_big(shard[k], sums[k], other[k], *moments, "adam_" + k)

    res = _adam_small([given[k].reshape(SMALL_2D[k]) for k in SMALL], [gathered[k] for k in SMALL],
                      [given["m_" + k].reshape(SMALL_2D[k]) for k in SMALL],
                      [given["v_" + k].reshape(SMALL_2D[k]) for k in SMALL])
    ns = len(SMALL)
    for i, k in enumerate(SMALL):
        out[k] = tuple(res[j * ns + i].reshape(given[k].shape) for j in range(4))
    loss = _sum_loss(gathered["loss_rows"]).reshape(())

    lead = lambda k, j: out[k][j][None] if k in BIG else out[k][j]
    return (loss, grad_x, *[lead(k, 0) for k in order], *[lead(k, 1) for k in order],
            *[lead(k, 2) for k in order], *[lead(k, 3) for k in order])
```

```python
import math

import jax
import jax.numpy as jnp
from jax import lax
from jax.experimental import pallas as pl
from jax.experimental.pallas import tpu as pltpu

F32 = jnp.float32
BF16 = jnp.bfloat16
MESH = pl.DeviceIdType.MESH
SDS = jax.ShapeDtypeStruct

D_MODEL = 1024
D_FF = 2816
D_SSM = 512
D_GMLP = 512
SSM_GROUPS = 32
SSM_GROUP_CH = 16
SSM_STATE = 64
SSM_LANES = SSM_GROUPS * SSM_STATE
GMLP_HEADS = 8
GMLP_HEAD_DIM = 64
CHUNK = 128
PLE_DIM = 256
LN_EPS = 1e-5
ALPHA = 2.0 ** 0.25

ADAM_LR = 0.001
ADAM_B1 = 0.9
ADAM_B2 = 0.999
ADAM_EPS = 1e-08
ADAM_WD = 0.01
ADAM_STEP = 10
ADAM_C1 = 1.0 - ADAM_B1 ** ADAM_STEP
ADAM_C2 = 1.0 - ADAM_B2 ** ADAM_STEP

N_DEV = 8
VMEM_LIMIT_BYTES = 56 * 1024 * 1024
FFN_COLS = 1408
S5_BLOCKS = 4
S5_BLOCK_IN = D_SSM // S5_BLOCKS
S5_BLOCK_ST = SSM_LANES // S5_BLOCKS
SCAN_LANES = 512
TN_K_BLOCK = 2048
LAST_PIECES = 2
LAST_PIECE = "ffn1_w_in_q%d"
_G0 = math.sqrt(2.0 / math.pi)
_G1 = 0.044715


def _dot(a, b):
    return jnp.dot(a, b, preferred_element_type=F32)


def _dot_nt(a, b):
    return lax.dot_general(a, b, (((1,), (1,)), ((), ())), preferred_element_type=F32)


def _dot_tn(a, b):
    return lax.dot_general(a, b, (((0,), (0,)), ((), ())), preferred_element_type=F32)


def _sigmoid(x):
    return 1.0 / (1.0 + jnp.exp(-x))


def _gelu(x):
    t = jnp.tanh(_G0 * (x + _G1 * x * x * x))
    return 0.5 * x * (1.0 + t)


def _gelu_grad(x):
    t = jnp.tanh(_G0 * (x + _G1 * x * x * x))
    return 0.5 * (1.0 + t) + 0.5 * x * (1.0 - t * t) * _G0 * (1.0 + 3.0 * _G1 * x * x)


def _ln_fwd(r, g, b):
    mu = jnp.mean(r, axis=-1, keepdims=True)
    d = r - mu
    var = jnp.mean(d * d, axis=-1, keepdims=True)
    rstd = lax.rsqrt(var + LN_EPS)
    xh = d * rstd
    return xh * g + b, xh, rstd


def _ln_bwd(dy, xh, rstd, g):
    dxh = dy * g
    m1 = jnp.mean(dxh, axis=-1, keepdims=True)
    m2 = jnp.mean(dxh * xh, axis=-1, keepdims=True)
    return rstd * (dxh - m1 - xh * m2)


def _resident(shape):
    nd = len(shape)
    return pl.BlockSpec(shape, lambda *_: (0,) * nd, pipeline_mode=pl.Buffered(1))


def _fixed(shape):
    nd = len(shape)
    return pl.BlockSpec(shape, lambda *_: (0,) * nd)


def _rows(tm, cols):
    return pl.BlockSpec((tm, cols), lambda i: (i, 0))


def _params(sem):
    return pltpu.CompilerParams(dimension_semantics=sem, vmem_limit_bytes=VMEM_LIMIT_BYTES)


class _Exchange:
    def __init__(self, args, out_shape, sems, start, finish):
        self.args, self.out_shape, self.sems = list(args), list(out_shape), list(sems)
        self.start, self.finish = start, finish
        self.cuts = [(0, len(self.out_shape))]


def _call(body, name, grid, in_specs, out_specs, out_shape, args, scratch=(), sem=None, bg=None, aliases=None):
    aliases = {} if aliases is None else aliases
    if bg is None:
        res = pl.pallas_call(body, name=name, grid=grid, out_shape=tuple(out_shape), in_specs=list(in_specs),
                             out_specs=tuple(out_specs), scratch_shapes=list(scratch),
                             input_output_aliases=aliases, compiler_params=_params(sem))(*args)
        return tuple(res), ()
    n_in, n_out, n_bi, n_bo, n_sc = len(args), len(out_shape), len(bg.args), len(bg.out_shape), len(scratch)

    def wrapped(*refs):
        ins = refs[:n_in]
        b_ins = refs[n_in:n_in + n_bi]
        outs = refs[n_in + n_bi:n_in + n_bi + n_out]
        b_outs = refs[n_in + n_bi + n_out:n_in + n_bi + n_out + n_bo]
        rest = refs[n_in + n_bi + n_out + n_bo:]
        scr, b_sems = rest[:n_sc], rest[n_sc:]
        first = pl.program_id(0) == 0
        last = pl.program_id(0) == grid[0] - 1
        for ax in range(1, len(grid)):
            first = jnp.logical_and(first, pl.program_id(ax) == 0)
            last = jnp.logical_and(last, pl.program_id(ax) == grid[ax] - 1)

        @pl.when(first)
        def _():
            bg.start(b_ins, b_outs, b_sems)

        body(*ins, *outs, *scr)

        @pl.when(last)
        def _():
            bg.finish(b_ins, b_outs, b_sems)

    any_spec = pl.BlockSpec(memory_space=pl.ANY)
    res = pl.pallas_call(
        wrapped, name=name, grid=grid, out_shape=tuple(out_shape) + tuple(bg.out_shape),
        in_specs=list(in_specs) + [any_spec] * n_bi, out_specs=tuple(out_specs) + (any_spec,) * n_bo,
        scratch_shapes=list(scratch) + list(bg.sems), input_output_aliases=aliases,
        compiler_params=_params(tuple("arbitrary" for _ in grid)))(*args, *bg.args)
    return tuple(res[:n_out]), tuple(res[n_out:])


def _run_exchange(ex, name):
    n_i, n_o = len(ex.args), len(ex.out_shape)

    def body(*refs):
        ins, outs, sems = refs[:n_i], refs[n_i:n_i + n_o], refs[n_i + n_o:]
        ex.start(ins, outs, sems)
        ex.finish(ins, outs, sems)

    any_spec = pl.BlockSpec(memory_space=pl.ANY)
    return tuple(pl.pallas_call(body, name=name, out_shape=tuple(ex.out_shape), in_specs=[any_spec] * n_i,
                                out_specs=(any_spec,) * n_o, scratch_shapes=list(ex.sems))(*ex.args))


def _join(exchanges):
    cuts = []
    a = o = q = 0
    for e in exchanges:
        cuts.append((a, a + len(e.args), o, o + len(e.out_shape), q, q + len(e.sems)))
        a, o, q = cuts[-1][1], cuts[-1][3], cuts[-1][5]

    def start(ins, outs, sems):
        for e, (a0, a1, o0, o1, q0, q1) in zip(exchanges, cuts):
            e.start(ins[a0:a1], outs[o0:o1], sems[q0:q1])

    def finish(ins, outs, sems):
        for e, (a0, a1, o0, o1, q0, q1) in zip(exchanges, cuts):
            e.finish(ins[a0:a1], outs[o0:o1], sems[q0:q1])

    joined = _Exchange(sum((e.args for e in exchanges), []), sum((e.out_shape for e in exchanges), []),
                       sum((e.sems for e in exchanges), []), start, finish)
    joined.cuts = [(c[2], c[3]) for c in cuts]
    return joined


def _ffn_fwd(x, w_in, w_out, g, b, tm, name, bg=None):
    t = x.shape[0]
    nch = D_FF // FFN_COLS

    def body(x_ref, win_ref, wout_ref, g_ref, b_ref, xn_ref, xh_ref, rstd_ref, xb_ref, h_ref):
        xv = x_ref[...]
        xb = xv.astype(BF16)
        xb_ref[...] = xb
        f = jnp.zeros((tm, D_MODEL), F32)
        for k in range(nch):
            cg = slice(k * FFN_COLS, (k + 1) * FFN_COLS)
            cu = slice(D_FF + k * FFN_COLS, D_FF + (k + 1) * FFN_COLS)
            hg = _dot(xb, win_ref[:, cg])
            hu = _dot(xb, win_ref[:, cu])
            h_ref[:, cg] = hg.astype(BF16)
            h_ref[:, cu] = hu.astype(BF16)
            a = hg * _sigmoid(hg) * hu
            f = f + _dot(a.astype(BF16), wout_ref[cg, :])
        y, xh, rstd = _ln_fwd(ALPHA * xv + 0.5 * f, g_ref[...], b_ref[...])
        xn_ref[...] = y
        xh_ref[...] = xh
        rstd_ref[...] = rstd

    return _call(
        body, name, (t // tm,),
        [_rows(tm, D_MODEL), _resident((D_MODEL, 2 * D_FF)), _resident((D_FF, D_MODEL)),
         _fixed((1, D_MODEL)), _fixed((1, D_MODEL))],
        (_rows(tm, D_MODEL), _rows(tm, D_MODEL), _rows(tm, 1), _rows(tm, D_MODEL), _rows(tm, 2 * D_FF)),
        (SDS((t, D_MODEL), F32), SDS((t, D_MODEL), F32), SDS((t, 1), F32), SDS((t, D_MODEL), BF16),
         SDS((t, 2 * D_FF), BF16)),
        (x, w_in, w_out, g, b), sem=("parallel",), bg=bg)


def _ffn_bwd(dxn, xh, rstd, h, w_in, w_out, g, tm, name, bg=None):
    t = dxn.shape[0]
    nch = D_FF // FFN_COLS

    def body(dxn_ref, xh_ref, rstd_ref, h_ref, win_ref, wout_ref, g_ref,
             dx_ref, dh_ref, a_ref, df_ref, dg_ref, db_ref):
        @pl.when(pl.program_id(0) == 0)
        def _():
            dg_ref[...] = jnp.zeros_like(dg_ref)
            db_ref[...] = jnp.zeros_like(db_ref)

        dy = dxn_ref[...]
        xhv = xh_ref[...]
        dr = _ln_bwd(dy, xhv, rstd_ref[...], g_ref[...])
        dg_ref[...] += jnp.sum(dy * xhv, axis=0, keepdims=True)
        db_ref[...] += jnp.sum(dy, axis=0, keepdims=True)
        df = (0.5 * dr).astype(BF16)
        df_ref[...] = df
        dx = ALPHA * dr
        for k in range(nch):
            cg = slice(k * FFN_COLS, (k + 1) * FFN_COLS)
            cu = slice(D_FF + k * FFN_COLS, D_FF + (k + 1) * FFN_COLS)
            hg = h_ref[:, cg].astype(F32)
            hu = h_ref[:, cu].astype(F32)
            sg = _sigmoid(hg)
            silu = hg * sg
            a_ref[:, cg] = (silu * hu).astype(BF16)
            da = _dot_nt(df, wout_ref[cg, :])
            dhu = (da * silu).astype(BF16)
            dhg = (da * hu * (sg * (1.0 + hg * (1.0 - sg)))).astype(BF16)
            dh_ref[:, cg] = dhg
            dh_ref[:, cu] = dhu
            dx = dx + _dot_nt(dhg, win_ref[:, cg]) + _dot_nt(dhu, win_ref[:, cu])
        dx_ref[...] = dx

    return _call(
        body, name, (t // tm,),
        [_rows(tm, D_MODEL), _rows(tm, D_MODEL), _rows(tm, 1), _rows(tm, 2 * D_FF),
         _resident((D_MODEL, 2 * D_FF)), _resident((D_FF, D_MODEL)), _fixed((1, D_MODEL))],
        (_rows(tm, D_MODEL), _rows(tm, 2 * D_FF), _rows(tm, D_FF), _rows(tm, D_MODEL),
         _fixed((1, D_MODEL)), _fixed((1, D_MODEL))),
        (SDS((t, D_MODEL), F32), SDS((t, 2 * D_FF), BF16), SDS((t, D_FF), BF16), SDS((t, D_MODEL), BF16),
         SDS((1, D_MODEL), F32), SDS((1, D_MODEL), F32)),
        (dxn, xh, rstd, h, w_in, w_out, g), sem=("arbitrary",), bg=bg)


def _tn_matmul(a, b, name, bm, bn, col_block=0, total_cols=None, prev=None, bg=None, a_cols=None):
    t, m = a.shape
    a_first = 0
    if a_cols is not None:
        a_first, m = a_cols[0], a_cols[1] * bm
    n = b.shape[1]
    total_cols = n if total_cols is None else total_cols
    bk = min(TN_K_BLOCK, t)
    nk = t // bk
    n_in = 2 if prev is None else 4

    def body(*refs):
        a_ref, b_ref = refs[0], refs[1]
        o_ref, ob_ref = refs[n_in], refs[n_in + 1]
        k = pl.program_id(2)

        @pl.when(k == 0)
        def _():
            o_ref[...] = jnp.zeros_like(o_ref)

        o_ref[...] += _dot_tn(a_ref[...], b_ref[...])

        @pl.when(k == nk - 1)
        def _():
            ob_ref[...] = o_ref[...].astype(BF16)

    in_specs = [pl.BlockSpec((bk, bm), lambda i, j, k: (k, i + a_first)),
                pl.BlockSpec((bk, bn), lambda i, j, k: (k, j))]
    args = [a, b]
    aliases = {}
    if prev is not None:
        in_specs += [pl.BlockSpec(memory_space=pl.ANY), pl.BlockSpec(memory_space=pl.ANY)]
        args += list(prev)
        aliases = {2: 0, 3: 1}
    out_spec = pl.BlockSpec((bm, bn), lambda i, j, k: (i, j + col_block))
    return _call(body, name, (m // bm, n // bn, nk), in_specs, (out_spec, out_spec),
                 (SDS((m, total_cols), F32), SDS((m, total_cols), BF16)), args,
                 sem=("parallel", "parallel", "arbitrary"), bg=bg, aliases=aliases)


def _mixin_fwd(x1, w, tm, bg=None):
    t = x1.shape[0]

    def body(x_ref, w_ref, xb_ref, za_ref, zuv_ref, gab_ref):
        xb = x_ref[...].astype(BF16)
        xb_ref[...] = xb
        za_ref[...] = _dot(xb, w_ref[:, 0:512]).astype(BF16)
        zuv_ref[...] = _dot(xb, w_ref[:, 512:1536]).astype(BF16)
        gab_ref[...] = _dot(xb, w_ref[:, 1536:3584]).astype(BF16)

    return _call(
        body, "mixin_fwd", (t // tm,),
        [_rows(tm, D_MODEL), _resident((D_MODEL, 3584))],
        (_rows(tm, D_MODEL), _rows(tm, 512), _rows(tm, 1024), _rows(tm, 2048)),
        (SDS((t, D_MODEL), BF16), SDS((t, 512), BF16), SDS((t, 1024), BF16), SDS((t, 2048), BF16)),
        (x1, w), sem=("parallel",), bg=bg)


def _mixin_bwd(dx1a, dza, dzuv, dgab, w, tm, bg=None):
    t = dx1a.shape[0]

    def body(d_ref, dza_ref, dzuv_ref, dgab_ref, w_ref, dx_ref):
        dx_ref[...] = (d_ref[...] + _dot_nt(dza_ref[...], w_ref[:, 0:512])
                       + _dot_nt(dzuv_ref[...], w_ref[:, 512:1536])
                       + _dot_nt(dgab_ref[...], w_ref[:, 1536:3584]))

    return _call(
        body, "mixin_bwd", (t // tm,),
        [_rows(tm, D_MODEL), _rows(tm, 512), _rows(tm, 1024), _rows(tm, 2048), _resident((D_MODEL, 3584))],
        (_rows(tm, D_MODEL),), (SDS((t, D_MODEL), F32),),
        (dx1a, dza, dzuv, dgab, w), sem=("parallel",), bg=bg)


def _unrolled(lo, hi, body, carry):
    for j in range(lo, hi):
        carry = body(j, carry)
    return carry


def _scan_fwd(hr_ref, hi_ref, a_ref, ap_ref, carry_ref, seg, cin_ref):
    for lc in range(SSM_LANES // SCAN_LANES):
        ls = slice(lc * SCAN_LANES, (lc + 1) * SCAN_LANES)
        a_r = jnp.broadcast_to(a_ref[0:1, ls], (8, SCAN_LANES))
        a_i = jnp.broadcast_to(a_ref[1:2, ls], (8, SCAN_LANES))

        def step(j, hc, ls=ls, a_r=a_r, a_i=a_i):
            h_r, h_i = hc
            rows = pl.ds(j * 8, 8)
            n_r = a_r * h_r - a_i * h_i + hr_ref[rows, ls]
            n_i = a_r * h_i + a_i * h_r + hi_ref[rows, ls]
            hr_ref[rows, ls] = n_r
            hi_ref[rows, ls] = n_i
            return n_r, n_i

        zero = jnp.zeros((8, SCAN_LANES), F32)
        f_r, f_i = _unrolled(0, seg, step, (zero, zero))
        c_r = carry_ref[0:1, ls]
        c_i = carry_ref[1:2, ls]
        p_r = ap_ref[0:1, ls]
        p_i = ap_ref[1:2, ls]
        rows_r, rows_i = [], []
        for s in range(8):
            rows_r.append(c_r)
            rows_i.append(c_i)
            c_r, c_i = (f_r[s:s + 1] + p_r * c_r - p_i * c_i,
                        f_i[s:s + 1] + p_r * c_i + p_i * c_r)
        carry_ref[0:1, ls] = c_r
        carry_ref[1:2, ls] = c_i
        cin_r = jnp.concatenate(rows_r, axis=0)
        cin_i = jnp.concatenate(rows_i, axis=0)
        if cin_ref is not None:
            cin_ref[0, :, ls] = cin_r
            cin_ref[1, :, ls] = cin_i

        def fix(j, cc, ls=ls, a_r=a_r, a_i=a_i):
            c_r, c_i = cc
            c_r, c_i = a_r * c_r - a_i * c_i, a_r * c_i + a_i * c_r
            rows = pl.ds(j * 8, 8)
            hr_ref[rows, ls] = hr_ref[rows, ls] + c_r
            hi_ref[rows, ls] = hi_ref[rows, ls] + c_i
            return c_r, c_i

        _unrolled(0, seg, fix, (cin_r, cin_i))


def _scan_bwd(gr_ref, gi_ref, hr_ref, hi_ref, cin_ref, a_ref, ap_ref, rcarry_ref, da_ref, seg):
    for lc in range(SSM_LANES // SCAN_LANES):
        ls = slice(lc * SCAN_LANES, (lc + 1) * SCAN_LANES)
        a_r = jnp.broadcast_to(a_ref[0:1, ls], (8, SCAN_LANES))
        a_i = jnp.broadcast_to(a_ref[1:2, ls], (8, SCAN_LANES))

        def step(t, gc, ls=ls, a_r=a_r, a_i=a_i):
            g_r, g_i = gc
            rows = pl.ds((seg - 1 - t) * 8, 8)
            n_r = gr_ref[rows, ls] + a_r * g_r + a_i * g_i
            n_i = gi_ref[rows, ls] + a_r * g_i - a_i * g_r
            gr_ref[rows, ls] = n_r
            gi_ref[rows, ls] = n_i
            return n_r, n_i

        zero = jnp.zeros((8, SCAN_LANES), F32)
        f_r, f_i = _unrolled(0, seg, step, (zero, zero))
        c_r = rcarry_ref[0:1, ls]
        c_i = rcarry_ref[1:2, ls]
        p_r = ap_ref[0:1, ls]
        p_i = ap_ref[1:2, ls]
        rows_r, rows_i = [None] * 8, [None] * 8
        for s in range(7, -1, -1):
            rows_r[s] = c_r
            rows_i[s] = c_i
            c_r, c_i = (f_r[s:s + 1] + p_r * c_r + p_i * c_i,
                        f_i[s:s + 1] + p_r * c_i - p_i * c_r)
        rcarry_ref[0:1, ls] = c_r
        rcarry_ref[1:2, ls] = c_i
        cin_r = jnp.concatenate(rows_r, axis=0)
        cin_i = jnp.concatenate(rows_i, axis=0)

        def fix_row(j_rows, hp_r, hp_i, cc, ls=ls, a_r=a_r, a_i=a_i):
            c_r, c_i, acc_r, acc_i = cc
            c_r, c_i = a_r * c_r + a_i * c_i, a_r * c_i - a_i * c_r
            g_r = gr_ref[j_rows, ls] + c_r
            g_i = gi_ref[j_rows, ls] + c_i
            gr_ref[j_rows, ls] = g_r
            gi_ref[j_rows, ls] = g_i
            acc_r = acc_r + g_r * hp_r + g_i * hp_i
            acc_i = acc_i + g_i * hp_r - g_r * hp_i
            return c_r, c_i, acc_r, acc_i

        def fix(t, cc, ls=ls, fix_row=fix_row):
            j = seg - 1 - t
            rows = pl.ds(j * 8, 8)
            prev = pl.ds((j - 1) * 8, 8)
            return fix_row(rows, hr_ref[prev, ls], hi_ref[prev, ls], cc)

        cc = _unrolled(0, seg - 1, fix, (cin_r, cin_i, zero, zero))
        _, _, acc_r, acc_i = fix_row(pl.ds(0, 8), cin_ref[0, :, ls], cin_ref[1, :, ls], cc)
        da_ref[0, :, ls] += acc_r
        da_ref[1, :, ls] += acc_i


def _s5_fwd(za, sp, bsz, seq, tb, bg=None):
    nb = seq // tb
    seg = tb // 8
    t = bsz * seq

    def body(za_ref, perm_ref, permt_ref, mre_ref, mim_ref, nre_ref, nim_ref, a_ref, ap_ref,
             dsk_ref, gw_ref, gb_ref, out_ref, y2_ref, car_ref, hr_ref, hi_ref, carry_ref):
        @pl.when(pl.program_id(1) == 0)
        def _():
            carry_ref[...] = jnp.zeros_like(carry_ref)

        car_ref[0] = carry_ref[...]
        up = _dot(perm_ref[...], za_ref[...])
        upb = up.astype(BF16)
        for bb in range(S5_BLOCKS):
            ub = upb[:, bb * S5_BLOCK_IN:(bb + 1) * S5_BLOCK_IN]
            st = slice(bb * S5_BLOCK_ST, (bb + 1) * S5_BLOCK_ST)
            hr_ref[:, st] = _dot(ub, mre_ref[bb])
            hi_ref[:, st] = _dot(ub, mim_ref[bb])
        _scan_fwd(hr_ref, hi_ref, a_ref, ap_ref, carry_ref, seg, None)
        ys = []
        for bb in range(S5_BLOCKS):
            st = slice(bb * S5_BLOCK_ST, (bb + 1) * S5_BLOCK_ST)
            ys.append(_dot(hr_ref[:, st].astype(BF16), nre_ref[bb])
                      - _dot(hi_ref[:, st].astype(BF16), nim_ref[bb]))
        y2 = jnp.concatenate(ys, axis=1) + dsk_ref[...] * up
        y2_ref[...] = y2
        y3 = _gelu(y2)
        gl = _dot(y3.astype(BF16), gw_ref[...]) + gb_ref[...]
        oa = y3 * _sigmoid(gl)
        out_ref[...] = _dot(permt_ref[...], oa.astype(BF16)).astype(BF16)

    blk = pl.BlockSpec((tb, D_SSM), lambda b, j: (b * nb + j, 0))
    m_shape = (S5_BLOCKS, S5_BLOCK_IN, S5_BLOCK_ST)
    n_shape = (S5_BLOCKS, S5_BLOCK_ST, S5_BLOCK_IN)
    return _call(
        body, "s5_fwd", (bsz, nb),
        [blk, _fixed((tb, tb)), _fixed((tb, tb)), _fixed(m_shape), _fixed(m_shape), _fixed(n_shape),
         _fixed(n_shape), _fixed((2, SSM_LANES)), _fixed((2, SSM_LANES)), _fixed((1, D_SSM)),
         _fixed((D_SSM, D_SSM)), _fixed((1, D_SSM))],
        (blk, blk, pl.BlockSpec((1, 2, SSM_LANES), lambda b, j: (b * nb + j, 0, 0))),
        (SDS((t, D_SSM), BF16), SDS((t, D_SSM), F32), SDS((bsz * nb, 2, SSM_LANES), F32)),
        (za, sp["perm"], sp["permt"], sp["mre"], sp["mim"], sp["nre"], sp["nim"], sp["a"], sp["ap"],
         sp["dskip"], sp["glu_w"], sp["glu_b"]),
        scratch=[pltpu.VMEM((tb, SSM_LANES), F32), pltpu.VMEM((tb, SSM_LANES), F32),
                 pltpu.VMEM((2, SSM_LANES), F32)],
        sem=("arbitrary", "arbitrary"), bg=bg)


def _s5_bwd(za, y2p, doa, carries, sp, bsz, seq, tb, bg=None):
    nb = seq // tb
    seg = tb // 8
    t = bsz * seq

    def body(za_ref, y2_ref, doa_ref, car_ref, perm_ref, permt_ref, mre_ref, mim_ref, mtre_ref, mtim_ref,
             nre_ref, nim_ref, ntre_ref, ntim_ref, a_ref, ap_ref, dsk_ref, gw_ref, gwt_ref, gb_ref,
             dza_ref, dmr_ref, dmi_ref, dnr_ref, dni_ref, da_ref, ddsk_ref, dgw_ref, dgb_ref,
             hr_ref, hi_ref, gr_ref, gi_ref, cin_ref, carry_ref, rcarry_ref):
        first = jnp.logical_and(pl.program_id(0) == 0, pl.program_id(1) == 0)

        @pl.when(first)
        def _():
            for r in (dmr_ref, dmi_ref, dnr_ref, dni_ref, da_ref, ddsk_ref, dgw_ref, dgb_ref):
                r[...] = jnp.zeros_like(r)

        @pl.when(pl.program_id(1) == 0)
        def _():
            rcarry_ref[...] = jnp.zeros_like(rcarry_ref)

        carry_ref[...] = car_ref[0]
        perm = perm_ref[...]
        up = _dot(perm, za_ref[...])
        upb = up.astype(BF16)
        for bb in range(S5_BLOCKS):
            ub = upb[:, bb * S5_BLOCK_IN:(bb + 1) * S5_BLOCK_IN]
            st = slice(bb * S5_BLOCK_ST, (bb + 1) * S5_BLOCK_ST)
            hr_ref[:, st] = _dot(ub, mre_ref[bb])
            hi_ref[:, st] = _dot(ub, mim_ref[bb])
        _scan_fwd(hr_ref, hi_ref, a_ref, ap_ref, carry_ref, seg, cin_ref)

        y2 = y2_ref[...]
        y3 = _gelu(y2)
        y3b = y3.astype(BF16)
        sg = _sigmoid(_dot(y3b, gw_ref[...]) + gb_ref[...])
        d0 = doa_ref[...]
        d_hi = d0.astype(BF16)
        d1 = d0 - d_hi.astype(F32)
        d_mid = d1.astype(BF16)
        d_lo = (d1 - d_mid.astype(F32)).astype(BF16)
        doap = _dot(perm, d_hi) + _dot(perm, d_mid) + _dot(perm, d_lo)
        dgl = doap * y3 * sg * (1.0 - sg)
        dglb = dgl.astype(BF16)
        dy3 = doap * sg + _dot(dglb, gwt_ref[...])
        dgw_ref[...] += _dot_tn(y3b, dglb)
        dgb_ref[...] += jnp.sum(dgl, axis=0, keepdims=True)
        dy2 = dy3 * _gelu_grad(y2)
        ddsk_ref[...] += jnp.sum(dy2 * up, axis=0, keepdims=True)
        dyb = dy2.astype(BF16)
        for bb in range(S5_BLOCKS):
            dyc = dyb[:, bb * S5_BLOCK_IN:(bb + 1) * S5_BLOCK_IN]
            st = slice(bb * S5_BLOCK_ST, (bb + 1) * S5_BLOCK_ST)
            gr_ref[:, st] = _dot(dyc, ntre_ref[bb])
            gi_ref[:, st] = -_dot(dyc, ntim_ref[bb])
            dnr_ref[bb] += _dot_tn(hr_ref[:, st].astype(BF16), dyc)
            dni_ref[bb] += -_dot_tn(hi_ref[:, st].astype(BF16), dyc)
        _scan_bwd(gr_ref, gi_ref, hr_ref, hi_ref, cin_ref, a_ref, ap_ref, rcarry_ref, da_ref, seg)
        dus = []
        for bb in range(S5_BLOCKS):
            st = slice(bb * S5_BLOCK_ST, (bb + 1) * S5_BLOCK_ST)
            grb = gr_ref[:, st].astype(BF16)
            gib = gi_ref[:, st].astype(BF16)
            dus.append(_dot(grb, mtre_ref[bb]) + _dot(gib, mtim_ref[bb]))
            ub = upb[:, bb * S5_BLOCK_IN:(bb + 1) * S5_BLOCK_IN]
            dmr_ref[bb] += _dot_tn(ub, grb)
            dmi_ref[bb] += _dot_tn(ub, gib)
        du = jnp.concatenate(dus, axis=1) + dy2 * dsk_ref[...]
        dza_ref[...] = _dot(permt_ref[...], du.astype(BF16)).astype(BF16)

    def rev(b, j):
        return (b * nb + (nb - 1 - j), 0)

    blk = pl.BlockSpec((tb, D_SSM), rev)
    m_shape = (S5_BLOCKS, S5_BLOCK_IN, S5_BLOCK_ST)
    n_shape = (S5_BLOCKS, S5_BLOCK_ST, S5_BLOCK_IN)
    return _call(
        body, "s5_bwd", (bsz, nb),
        [blk, blk, blk, pl.BlockSpec((1, 2, SSM_LANES), lambda b, j: (b * nb + (nb - 1 - j), 0, 0)),
         _fixed((tb, tb)), _fixed((tb, tb)), _fixed(m_shape), _fixed(m_shape), _fixed(n_shape), _fixed(n_shape),
         _fixed(n_shape), _fixed(n_shape), _fixed(m_shape), _fixed(m_shape),
         _fixed((2, SSM_LANES)), _fixed((2, SSM_LANES)), _fixed((1, D_SSM)),
         _fixed((D_SSM, D_SSM)), _fixed((D_SSM, D_SSM)), _fixed((1, D_SSM))],
        (blk, _fixed(m_shape), _fixed(m_shape), _fixed(n_shape), _fixed(n_shape),
         _fixed((2, 8, SSM_LANES)), _fixed((1, D_SSM)), _fixed((D_SSM, D_SSM)), _fixed((1, D_SSM))),
        (SDS((t, D_SSM), BF16), SDS(m_shape, F32), SDS(m_shape, F32), SDS(n_shape, F32), SDS(n_shape, F32),
         SDS((2, 8, SSM_LANES), F32), SDS((1, D_SSM), F32), SDS((D_SSM, D_SSM), F32), SDS((1, D_SSM), F32)),
        (za, y2p, doa, carries, sp["perm"], sp["permt"], sp["mre"], sp["mim"], sp["mtre"], sp["mtim"],
         sp["nre"], sp["nim"], sp["ntre"], sp["ntim"], sp["a"], sp["ap"], sp["dskip"], sp["glu_w"],
         sp["glu_wt"], sp["glu_b"]),
        scratch=[pltpu.VMEM((tb, SSM_LANES), F32), pltpu.VMEM((tb, SSM_LANES), F32),
                 pltpu.VMEM((tb, SSM_LANES), F32), pltpu.VMEM((tb, SSM_LANES), F32),
                 pltpu.VMEM((2, 8, SSM_LANES), F32), pltpu.VMEM((2, SSM_LANES), F32),
                 pltpu.VMEM((2, SSM_LANES), F32)],
        sem=("arbitrary", "arbitrary"), bg=bg)


def _gmlp_spatial(ws_ref, vb):
    lane = lax.broadcasted_iota(jnp.int32, (CHUNK, 128), 1)
    parts = []
    for j in range(GMLP_HEADS // 2):
        vp = vb[:, 128 * j:128 * (j + 1)]
        parts.append(jnp.where(lane < GMLP_HEAD_DIM, _dot(ws_ref[2 * j], vp), _dot(ws_ref[2 * j + 1], vp)))
    return jnp.concatenate(parts, axis=1)


def _gmlp_fwd(zuv, ln_g, ln_b, wsm, bias, bg=None):
    t = zuv.shape[0]

    def body(z_ref, g_ref, b_ref, ws_ref, bias_ref, out_ref):
        u = _gelu(z_ref[:, 0:D_GMLP].astype(F32))
        v0 = _gelu(z_ref[:, D_GMLP:2 * D_GMLP].astype(F32))
        v, _, _ = _ln_fwd(v0, g_ref[...], b_ref[...])
        s = _gmlp_spatial(ws_ref, v.astype(BF16)) + bias_ref[...]
        out_ref[...] = (u * s).astype(BF16)

    return _call(
        body, "gmlp_fwd", (t // CHUNK,),
        [_rows(CHUNK, 2 * D_GMLP), _fixed((1, D_GMLP)), _fixed((1, D_GMLP)),
         _fixed((GMLP_HEADS, CHUNK, CHUNK)), _fixed((CHUNK, D_GMLP))],
        (_rows(CHUNK, D_GMLP),), (SDS((t, D_GMLP), BF16),),
        (zuv, ln_g, ln_b, wsm, bias), sem=("parallel",), bg=bg)


def _gmlp_bwd(zuv, dgm, ln_g, ln_b, wsm, wsmt, bias, bg=None):
    t = zuv.shape[0]

    def body(z_ref, d_ref, g_ref, b_ref, ws_ref, wst_ref, bias_ref,
             dz_ref, dws_ref, dbias_ref, dg_ref, db_ref):
        @pl.when(pl.program_id(0) == 0)
        def _():
            for r in (dws_ref, dbias_ref, dg_ref, db_ref):
                r[...] = jnp.zeros_like(r)

        zu = z_ref[:, 0:D_GMLP].astype(F32)
        zv = z_ref[:, D_GMLP:2 * D_GMLP].astype(F32)
        u = _gelu(zu)
        v0 = _gelu(zv)
        gam = g_ref[...]
        v, vhat, rstd = _ln_fwd(v0, gam, b_ref[...])
        vb = v.astype(BF16)
        s = _gmlp_spatial(ws_ref, vb) + bias_ref[...]
        d = d_ref[...]
        dz_ref[:, 0:D_GMLP] = (d * s * _gelu_grad(zu)).astype(BF16)
        ds = d * u
        dbias_ref[...] += ds
        dsb = ds.astype(BF16)
        lane = lax.broadcasted_iota(jnp.int32, (CHUNK, 128), 1)
        tril = (lax.broadcasted_iota(jnp.int32, (CHUNK, CHUNK), 0)
                >= lax.broadcasted_iota(jnp.int32, (CHUNK, CHUNK), 1))
        zero_b = jnp.zeros((CHUNK, 128), BF16)
        parts = []
        for j in range(GMLP_HEADS // 2):
            dsp = dsb[:, 128 * j:128 * (j + 1)]
            vp = vb[:, 128 * j:128 * (j + 1)]
            parts.append(jnp.where(lane < GMLP_HEAD_DIM, _dot(wst_ref[2 * j], dsp),
                                   _dot(wst_ref[2 * j + 1], dsp)))
            lo = jnp.where(lane < GMLP_HEAD_DIM, dsp, zero_b)
            hi = jnp.where(lane < GMLP_HEAD_DIM, zero_b, dsp)
            dws_ref[2 * j] += jnp.where(tril, _dot_nt(lo, vp), 0.0)
            dws_ref[2 * j + 1] += jnp.where(tril, _dot_nt(hi, vp), 0.0)
        dv = jnp.concatenate(parts, axis=1)
        dg_ref[...] += jnp.sum(dv * vhat, axis=0, keepdims=True)
        db_ref[...] += jnp.sum(dv, axis=0, keepdims=True)
        dz_ref[:, D_GMLP:2 * D_GMLP] = (_ln_bwd(dv, vhat, rstd, gam) * _gelu_grad(zv)).astype(BF16)

    return _call(
        body, "gmlp_bwd", (t // CHUNK,),
        [_rows(CHUNK, 2 * D_GMLP), _rows(CHUNK, D_GMLP), _fixed((1, D_GMLP)), _fixed((1, D_GMLP)),
         _fixed((GMLP_HEADS, CHUNK, CHUNK)), _fixed((GMLP_HEADS, CHUNK, CHUNK)), _fixed((CHUNK, D_GMLP))],
        (_rows(CHUNK, 2 * D_GMLP), _fixed((GMLP_HEADS, CHUNK, CHUNK)), _fixed((CHUNK, D_GMLP)),
         _fixed((1, D_GMLP)), _fixed((1, D_GMLP))),
        (SDS((t, 2 * D_GMLP), BF16), SDS((GMLP_HEADS, CHUNK, CHUNK), F32), SDS((CHUNK, D_GMLP), F32),
         SDS((1, D_GMLP), F32), SDS((1, D_GMLP), F32)),
        (zuv, dgm, ln_g, ln_b, wsm, wsmt, bias), sem=("arbitrary",), bg=bg)


def _mixout_fwd(x1, s5o, gm, gab, ua, ub, wmo, g, b, tm, bg=None):
    t = x1.shape[0]

    def body(x_ref, s_ref, m_ref, gab_ref, ua_ref, ub_ref, wmo_ref, g_ref, b_ref,
             xn_ref, xh_ref, rstd_ref, xb_ref):
        ya = _dot(s_ref[...], ua_ref[...])
        yb = _dot(m_ref[...], ub_ref[...])
        mix = (_sigmoid(gab_ref[:, 0:D_MODEL].astype(F32)) * ya
               + _sigmoid(gab_ref[:, D_MODEL:2 * D_MODEL].astype(F32)) * yb)
        r = ALPHA * x_ref[...] + _dot(mix.astype(BF16), wmo_ref[...])
        y, xh, rstd = _ln_fwd(r, g_ref[...], b_ref[...])
        xn_ref[...] = y
        xh_ref[...] = xh
        rstd_ref[...] = rstd
        xb_ref[...] = y.astype(BF16)

    return _call(
        body, "mixout_fwd", (t // tm,),
        [_rows(tm, D_MODEL), _rows(tm, D_SSM), _rows(tm, D_GMLP), _rows(tm, 2 * D_MODEL),
         _resident((D_SSM, D_MODEL)), _resident((D_GMLP, D_MODEL)), _resident((D_MODEL, D_MODEL)),
         _fixed((1, D_MODEL)), _fixed((1, D_MODEL))],
        (_rows(tm, D_MODEL), _rows(tm, D_MODEL), _rows(tm, 1), _rows(tm, D_MODEL)),
        (SDS((t, D_MODEL), F32), SDS((t, D_MODEL), F32), SDS((t, 1), F32), SDS((t, D_MODEL), BF16)),
        (x1, s5o, gm, gab, ua, ub, wmo, g, b), sem=("parallel",), bg=bg)


def _mixout_bwd(dx2, xh, rstd, s5o, gm, gab, ua, ub, wmo, g, tm, bg=None):
    t = dx2.shape[0]

    def body(d_ref, xh_ref, rstd_ref, s_ref, m_ref, gab_ref, ua_ref, ub_ref, wmo_ref, g_ref,
             dx1_ref, dmx_ref, mb_ref, dya_ref, dyb_ref, ds5_ref, dgm_ref, dgab_ref, dg_ref, db_ref):
        @pl.when(pl.program_id(0) == 0)
        def _():
            dg_ref[...] = jnp.zeros_like(dg_ref)
            db_ref[...] = jnp.zeros_like(db_ref)

        dy = d_ref[...]
        xhv = xh_ref[...]
        dr = _ln_bwd(dy, xhv, rstd_ref[...], g_ref[...])
        dg_ref[...] += jnp.sum(dy * xhv, axis=0, keepdims=True)
        db_ref[...] += jnp.sum(dy, axis=0, keepdims=True)
        dx1_ref[...] = ALPHA * dr
        drb = dr.astype(BF16)
        dmx_ref[...] = drb
        dm = _dot_nt(drb, wmo_ref[...])
        ya = _dot(s_ref[...], ua_ref[...])
        yb = _dot(m_ref[...], ub_ref[...])
        sa = _sigmoid(gab_ref[:, 0:D_MODEL].astype(F32))
        sb = _sigmoid(gab_ref[:, D_MODEL:2 * D_MODEL].astype(F32))
        mb_ref[...] = (sa * ya + sb * yb).astype(BF16)
        dya = (dm * sa).astype(BF16)
        dyb = (dm * sb).astype(BF16)
        dya_ref[...] = dya
        dyb_ref[...] = dyb
        dgab_ref[:, 0:D_MODEL] = (dm * ya * sa * (1.0 - sa)).astype(BF16)
        dgab_ref[:, D_MODEL:2 * D_MODEL] = (dm * yb * sb * (1.0 - sb)).astype(BF16)
        ds5_ref[...] = _dot_nt(dya, ua_ref[...])
        dgm_ref[...] = _dot_nt(dyb, ub_ref[...])

    return _call(
        body, "mixout_bwd", (t // tm,),
        [_rows(tm, D_MODEL), _rows(tm, D_MODEL), _rows(tm, 1), _rows(tm, D_SSM), _rows(tm, D_GMLP),
         _rows(tm, 2 * D_MODEL), _resident((D_SSM, D_MODEL)), _resident((D_GMLP, D_MODEL)),
         _resident((D_MODEL, D_MODEL)), _fixed((1, D_MODEL))],
        (_rows(tm, D_MODEL), _rows(tm, D_MODEL), _rows(tm, D_MODEL), _rows(tm, D_MODEL),
         _rows(tm, D_MODEL), _rows(tm, D_SSM), _rows(tm, D_GMLP), _rows(tm, 2 * D_MODEL),
         _fixed((1, D_MODEL)), _fixed((1, D_MODEL))),
        (SDS((t, D_MODEL), F32), SDS((t, D_MODEL), BF16), SDS((t, D_MODEL), BF16),
         SDS((t, D_MODEL), BF16), SDS((t, D_MODEL), BF16), SDS((t, D_SSM), F32),
         SDS((t, D_GMLP), F32), SDS((t, 2 * D_MODEL), BF16),
         SDS((1, D_MODEL), F32), SDS((1, D_MODEL), F32)),
        (dx2, xh, rstd, s5o, gm, gab, ua, ub, wmo, g), sem=("arbitrary",), bg=bg)


def _ple_loss(x3, p, tgt, wpg, wpp, tm, bg=None):
    t = x3.shape[0]

    def body(x_ref, p_ref, t_ref, wpg_ref, wpp_ref, dx_ref, xb_ref, pb_ref, dq_ref, de_ref, loss_ref):
        @pl.when(pl.program_id(0) == 0)
        def _():
            loss_ref[...] = jnp.zeros_like(loss_ref)

        x3v = x_ref[...]
        xb = x3v.astype(BF16)
        pb = p_ref[...].astype(BF16)
        xb_ref[...] = xb
        pb_ref[...] = pb
        s = _sigmoid(_dot(xb, wpg_ref[...]))
        e = _dot(pb, wpp_ref[...])
        diff = x3v + s * e - t_ref[...]
        loss_ref[...] += jnp.sum(diff * diff, axis=0, keepdims=True)
        dout = diff * (1.0 / D_MODEL)
        de_ref[...] = (dout * s).astype(BF16)
        dq = (dout * e * s * (1.0 - s)).astype(BF16)
        dq_ref[...] = dq
        dx_ref[...] = dout + _dot_nt(dq, wpg_ref[...])

    return _call(
        body, "ple_loss", (t // tm,),
        [_rows(tm, D_MODEL), _rows(tm, PLE_DIM), _rows(tm, D_MODEL),
         _resident((D_MODEL, D_MODEL)), _resident((PLE_DIM, D_MODEL))],
        (_rows(tm, D_MODEL), _rows(tm, D_MODEL), _rows(tm, PLE_DIM), _rows(tm, D_MODEL),
         _rows(tm, D_MODEL), _fixed((1, D_MODEL))),
        (SDS((t, D_MODEL), F32), SDS((t, D_MODEL), BF16), SDS((t, PLE_DIM), BF16),
         SDS((t, D_MODEL), BF16), SDS((t, D_MODEL), BF16), SDS((1, D_MODEL), F32)),
        (x3, p, tgt, wpg, wpp), sem=("arbitrary",), bg=bg)


def _s5_discretise(lre, lim, log_dt, bre, bim):
    dt = jnp.exp(log_dt)[:, None]
    mag = jnp.exp(lre * dt)
    abr = mag * jnp.cos(lim * dt)
    abi = mag * jnp.sin(lim * dt)
    nr = abr - 1.0
    ni = abi
    den = lre * lre + lim * lim
    cr = ((nr * lre + ni * lim) / den)[..., None]
    ci = ((ni * lre - nr * lim) / den)[..., None]
    return abr, abi, cr * bre - ci * bim, cr * bim + ci * bre


def _block_diag_in(bb):
    v = bb.reshape(S5_BLOCKS, 8, SSM_STATE, SSM_GROUP_CH).transpose(0, 1, 3, 2)
    return jnp.einsum("bgip,gh->bgihp", v, jnp.eye(8, dtype=bb.dtype)).reshape(
        S5_BLOCKS, S5_BLOCK_IN, S5_BLOCK_ST)


def _block_diag_in_t(dm):
    v = dm.reshape(S5_BLOCKS, 8, SSM_GROUP_CH, 8, SSM_STATE)
    d = jnp.einsum("bgihp,gh->bgip", v, jnp.eye(8, dtype=dm.dtype))
    return d.transpose(0, 1, 3, 2).reshape(SSM_GROUPS, SSM_STATE, SSM_GROUP_CH)


def _block_diag_out(cc):
    v = cc.reshape(S5_BLOCKS, 8, SSM_GROUP_CH, SSM_STATE)
    return jnp.einsum("bgip,gh->bgphi", v, jnp.eye(8, dtype=cc.dtype)).reshape(
        S5_BLOCKS, S5_BLOCK_ST, S5_BLOCK_IN)


def _block_diag_out_t(dn):
    v = dn.reshape(S5_BLOCKS, 8, SSM_STATE, 8, SSM_GROUP_CH)
    d = jnp.einsum("bgphi,gh->bgip", v, jnp.eye(8, dtype=dn.dtype))
    return d.reshape(SSM_GROUPS, SSM_GROUP_CH, SSM_STATE)


def _s5_setup(lre, lim, log_dt, bre, bim, cre, cim, d_skip, glu_w, glu_b, tb):
    seg = tb // 8
    abr, abi, bbr, bbi = _s5_discretise(lre, lim, log_dt, bre, bim)
    pr, pi = abr, abi
    for _ in range(int(math.log2(seg))):
        pr, pi = pr * pr - pi * pi, 2.0 * pr * pi
    rows = jnp.arange(tb)
    src = (rows % 8) * seg + rows // 8
    perm = (src[:, None] == jnp.arange(tb)[None, :]).astype(BF16)
    mre = _block_diag_in(bbr)
    mim = _block_diag_in(bbi)
    nre = _block_diag_out(cre)
    nim = _block_diag_out(cim)
    return {
        "perm": perm, "permt": perm.T,
        "mre": mre.astype(BF16), "mim": mim.astype(BF16),
        "mtre": mre.transpose(0, 2, 1).astype(BF16), "mtim": mim.transpose(0, 2, 1).astype(BF16),
        "nre": nre.astype(BF16), "nim": nim.astype(BF16),
        "ntre": nre.transpose(0, 2, 1).astype(BF16), "ntim": nim.transpose(0, 2, 1).astype(BF16),
        "a": jnp.stack([abr.reshape(-1), abi.reshape(-1)]),
        "ap": jnp.stack([pr.reshape(-1), pi.reshape(-1)]),
        "dskip": d_skip.reshape(1, D_SSM), "glu_w": glu_w, "glu_wt": glu_w.T,
        "glu_b": glu_b.reshape(1, D_SSM),
    }


BIG = ("ffn1_w_in", "ffn1_w_out", "mix_w_in", "ssm_glu_w", "up_a", "up_b", "mix_w_out",
       "ffn2_w_in", "ffn2_w_out", "ple_w_proj", "ple_w_gate")
BIG_AXIS = {"ffn1_w_in": 1, "ffn1_w_out": 0, "mix_w_in": 1, "ssm_glu_w": 0, "up_a": 1, "up_b": 1,
            "mix_w_out": 0, "ffn2_w_in": 1, "ffn2_w_out": 0, "ple_w_proj": 1, "ple_w_gate": 0}
SMALL = ("ln1_g", "ln1_b", "ssm_lambda_re", "ssm_lambda_im", "ssm_log_dt", "ssm_b_re", "ssm_b_im",
         "ssm_c_re", "ssm_c_im", "ssm_d", "ssm_glu_b", "gmlp_ln_g", "gmlp_ln_b", "gmlp_w_s",
         "gmlp_b_s", "ln2_g", "ln2_b", "ln3_g", "ln3_b")
SMALL_2D = {"ln1_g": (1, 1024), "ln1_b": (1, 1024), "ssm_lambda_re": (32, 64), "ssm_lambda_im": (32, 64),
            "ssm_log_dt": (1, 32), "ssm_b_re": (32, 1024), "ssm_b_im": (32, 1024), "ssm_c_re": (32, 1024),
            "ssm_c_im": (32, 1024), "ssm_d": (1, 512), "ssm_glu_b": (1, 512), "gmlp_ln_g": (1, 512),
            "gmlp_ln_b": (1, 512), "gmlp_w_s": (1024, 128), "gmlp_b_s": (8, 128), "ln2_g": (1, 1024),
            "ln2_b": (1, 1024), "ln3_g": (1, 1024), "ln3_b": (1, 1024)}


def _place():
    return lax.axis_index("x"), lax.axis_index("y"), lax.axis_index("c")


def _other_chips(x, y):
    return [(1 - x, y), (x, 1 - y), (1 - x, 1 - y)]


def _window(ref, shard_shape, axis, chip, half):
    r, c = shard_shape
    hr = r // 2
    if axis == 0:
        if half is None:
            return ref.at[pl.ds(chip * r, r), :]
        return ref.at[pl.ds(chip * r + half * hr, hr), :]
    if half is None:
        return ref.at[:, pl.ds(chip * c, c)]
    return ref.at[pl.ds(half * hr, hr), pl.ds(chip * c, c)]


def _gather_weights(shards, axes):
    n = len(shards)
    shapes = [s.shape for s in shards]
    full = [(4 * r, c) if ax == 0 else (r, 4 * c) for (r, c), ax in zip(shapes, axes)]

    def remote(sems, i, k, src, dst, to):
        return pltpu.make_async_remote_copy(src_ref=src, dst_ref=dst, send_sem=sems[0].at[6 * i + k],
                                            recv_sem=sems[1].at[6 * i + k], device_id=to, device_id_type=MESH)

    def own_copies(ins, outs, sems):
        x, y, c = _place()
        me = 2 * x + y
        cps = []
        for i in range(n):
            hr = shapes[i][0] // 2
            mine = ins[i].at[pl.ds(c * hr, hr), :]
            for j, (cx, cy) in enumerate(_other_chips(x, y)):
                cps.append(remote(sems, i, j, mine, _window(outs[i], shapes[i], axes[i], me, c), (cx, cy, c)))
        local = [pltpu.make_async_copy(ins[i], _window(outs[i], shapes[i], axes[i], me, None), sems[2].at[i])
                 for i in range(n)]
        return cps, local

    def start(ins, outs, sems):
        cps, local = own_copies(ins, outs, sems)
        for cp in local + cps:
            cp.start()

    def finish(ins, outs, sems):
        x, y, c = _place()
        sibling = (x, y, 1 - c)
        passed = []
        for j, (cx, cy) in enumerate(_other_chips(x, y)):
            for i in range(n):
                w = _window(outs[i], shapes[i], axes[i], 2 * cx + cy, c)
                remote(sems, i, j, w, w, (cx, cy, c)).wait_recv()
                cp = remote(sems, i, 3 + j, w, w, sibling)
                cp.start()
                passed.append(cp)
        for j, (cx, cy) in enumerate(_other_chips(x, y)):
            for i in range(n):
                w = _window(outs[i], shapes[i], axes[i], 2 * cx + cy, 1 - c)
                remote(sems, i, 3 + j, w, w, sibling).wait_recv()
        cps, local = own_copies(ins, outs, sems)
        for cp in cps + passed:
            cp.wait_send()
        for cp in local:
            cp.wait()

    return _Exchange(shards, [SDS(f, BF16) for f in full],
                     [pltpu.SemaphoreType.DMA((6 * n,)), pltpu.SemaphoreType.DMA((6 * n,)),
                      pltpu.SemaphoreType.DMA((n,))], start, finish)


def _scatter_grads(parts, shapes, axes):
    n = len(parts)

    def copies(ins, outs, sems):
        x, y, c = _place()
        return [pltpu.make_async_remote_copy(
            src_ref=_window(ins[i], shapes[i], axes[i], 2 * cx + cy, None), dst_ref=outs[i].at[j],
            send_sem=sems[0].at[3 * i + j], recv_sem=sems[1].at[3 * i + j],
            device_id=(cx, cy, c), device_id_type=MESH)
            for i in range(n) for j, (cx, cy) in enumerate(_other_chips(x, y))]

    def start(ins, outs, sems):
        for cp in copies(ins, outs, sems):
            cp.start()

    def finish(ins, outs, sems):
        for cp in copies(ins, outs, sems):
            cp.wait()

    return _Exchange(parts, [SDS((3,) + tuple(s), BF16) for s in shapes],
                     [pltpu.SemaphoreType.DMA((3 * n,)), pltpu.SemaphoreType.DMA((3 * n,))], start, finish)


def _swap_halves(parts, shapes, axes):
    n = len(parts)

    def copies(ins, outs, sems):
        x, y, c = _place()
        cps = []
        for i in range(n):
            r, _ = shapes[i]
            hr = r // 2
            if axes[i] == 0:
                cps += [pltpu.make_async_remote_copy(
                    src_ref=ins[i].at[pl.ds(k * r + (1 - c) * hr, hr), :], dst_ref=outs[i].at[k],
                    send_sem=sems[0].at[i], recv_sem=sems[1].at[i], device_id=(x, y, 1 - c),
                    device_id_type=MESH) for k in range(4)]
            else:
                cps.append(pltpu.make_async_remote_copy(
                    src_ref=ins[i].at[pl.ds((1 - c) * hr, hr), :], dst_ref=outs[i],
                    send_sem=sems[0].at[i], recv_sem=sems[1].at[i], device_id=(x, y, 1 - c),
                    device_id_type=MESH))
        return cps

    def start(ins, outs, sems):
        for cp in copies(ins, outs, sems):
            cp.start()

    def finish(ins, outs, sems):
        x, y, c = _place()
        for i in range(n):
            pltpu.make_async_remote_copy(src_ref=outs[i], dst_ref=outs[i], send_sem=sems[0].at[i],
                                         recv_sem=sems[1].at[i], device_id=(x, y, 1 - c),
                                         device_id_type=MESH).wait()

    out = [SDS((4, r // 2, c), BF16) if ax == 0 else SDS((r // 2, 4 * c), BF16)
           for (r, c), ax in zip(shapes, axes)]
    return _Exchange(parts, out, [pltpu.SemaphoreType.DMA((n,)), pltpu.SemaphoreType.DMA((n,))], start, finish)


def _scatter_halves(pres, shapes):
    n = len(pres)

    def copies(ins, outs, sems):
        x, y, c = _place()
        return [pltpu.make_async_remote_copy(
            src_ref=ins[i].at[1 + j], dst_ref=outs[i].at[j], send_sem=sems[0].at[3 * i + j],
            recv_sem=sems[1].at[3 * i + j], device_id=(cx, cy, c), device_id_type=MESH)
            for i in range(n) for j, (cx, cy) in enumerate(_other_chips(x, y))]

    def start(ins, outs, sems):
        for cp in copies(ins, outs, sems):
            cp.start()

    def finish(ins, outs, sems):
        for cp in copies(ins, outs, sems):
            cp.wait()

    return _Exchange(pres, [SDS((3, r // 2, c), BF16) for r, c in shapes],
                     [pltpu.SemaphoreType.DMA((3 * n,)), pltpu.SemaphoreType.DMA((3 * n,))], start, finish)


def _swap_with_sibling(arrs):
    n = len(arrs)

    def copies(ins, outs, sems):
        x, y, c = _place()
        return [pltpu.make_async_remote_copy(src_ref=ins[i], dst_ref=outs[i], send_sem=sems[0].at[i],
                                             recv_sem=sems[1].at[i], device_id=(x, y, 1 - c),
                                             device_id_type=MESH) for i in range(n)]

    def start(ins, outs, sems):
        for cp in copies(ins, outs, sems):
            cp.start()

    def finish(ins, outs, sems):
        for cp in copies(ins, outs, sems):
            cp.wait()

    return _Exchange(arrs, [SDS(a.shape, a.dtype) for a in arrs],
                     [pltpu.SemaphoreType.DMA((n,)), pltpu.SemaphoreType.DMA((n,))], start, finish)


def _gather_small(arrs):
    n = len(arrs)

    def copy(sems, outs, i, k, block, to, src=None):
        px, py, pc = block
        dst = outs[i].at[4 * px + 2 * py + pc]
        return pltpu.make_async_remote_copy(
            src_ref=dst if src is None else src, dst_ref=dst, send_sem=sems[0].at[7 * i + k],
            recv_sem=sems[1].at[7 * i + k], device_id=to, device_id_type=MESH)

    def own_copies(ins, outs, sems):
        x, y, c = _place()
        cps = []
        for i in range(n):
            cps.append(copy(sems, outs, i, 0, (x, y, c), (x, y, 1 - c), src=ins[i]))
            for j, (cx, cy) in enumerate(_other_chips(x, y)):
                cps.append(copy(sems, outs, i, 1 + j, (x, y, c), (cx, cy, c), src=ins[i]))
        local = [pltpu.make_async_copy(ins[i], outs[i].at[4 * x + 2 * y + c], sems[2].at[i]) for i in range(n)]
        return cps, local

    def start(ins, outs, sems):
        cps, local = own_copies(ins, outs, sems)
        for cp in local + cps:
            cp.start()

    def finish(ins, outs, sems):
        x, y, c = _place()
        passed = []
        for j, (cx, cy) in enumerate(_other_chips(x, y)):
            for i in range(n):
                copy(sems, outs, i, 1 + j, (cx, cy, c), (x, y, c)).wait_recv()
                cp = copy(sems, outs, i, 4 + j, (cx, cy, c), (x, y, 1 - c))
                cp.start()
                passed.append(cp)
        for i in range(n):
            copy(sems, outs, i, 0, (x, y, 1 - c), (x, y, c)).wait_recv()
            for j, (cx, cy) in enumerate(_other_chips(x, y)):
                copy(sems, outs, i, 4 + j, (cx, cy, 1 - c), (x, y, c)).wait_recv()
        cps, local = own_copies(ins, outs, sems)
        for cp in cps + passed:
            cp.wait_send()
        for cp in local:
            cp.wait()

    return _Exchange(arrs, [SDS((N_DEV,) + a.shape, F32) for a in arrs],
                     [pltpu.SemaphoreType.DMA((7 * n,)), pltpu.SemaphoreType.DMA((7 * n,)),
                      pltpu.SemaphoreType.DMA((n,))], start, finish)


def _local_step(x, p, tgt, wb, ws, shards=None):
    bsz, seq, _ = x.shape
    t = bsz * seq
    tm = min(256, t)
    tb = min(256, seq)
    x0 = x.reshape(t, D_MODEL)
    p0 = p.reshape(t, PLE_DIM)
    tg = tgt.reshape(t, D_MODEL)
    row = lambda v: v.reshape(1, -1)
    dist = shards is not None
    wb = dict(wb)
    recv, sums, other, gathered = {}, {}, {}, {}
    gb = {}
    gs = {}
    shape_of, axis_of = {}, {}
    chip = None
    if dist:
        shape_of = {k: tuple(shards[k].shape) for k in BIG}
        axis_of = dict(BIG_AXIS)
        for q in range(LAST_PIECES):
            shape_of[LAST_PIECE % q] = (D_MODEL // LAST_PIECES, shape_of["ffn1_w_in"][1])
            axis_of[LAST_PIECE % q] = 1
        xi, yi, ci = _place()
        chip = (2 * xi + yi).astype(jnp.int32).reshape(1)
        ids = jnp.stack([2 * xi + yi] + [2 * cx + cy for cx, cy in _other_chips(xi, yi)] + [ci]).astype(jnp.int32)
    halfbuf, pre = {}, {}

    def gather(names):
        return _gather_weights([shards[k] for k in names], [BIG_AXIS[k] for k in names]) if dist else None

    def exchange(scat=(), swap=(), halves=(), scat2=(), swap2=(), extra=None):
        if not dist:
            return None, []
        parts, tags = [], []
        if scat:
            parts.append(_scatter_grads([gb[k][1] for k in scat], [shape_of[k] for k in scat],
                                        [axis_of[k] for k in scat]))
            tags.append((recv, scat))
        if swap:
            for k in swap:
                sums[k] = _sum_blocks(gb[k][0], recv[k], shape_of[k], axis_of[k], chip, "sum_" + k)
            parts.append(_swap_with_sibling([sums[k] for k in swap]))
            tags.append((other, swap))
        if halves:
            parts.append(_swap_halves([gb[k][1] for k in halves], [shape_of[k] for k in halves],
                                      [axis_of[k] for k in halves]))
            tags.append((halfbuf, halves))
        if scat2:
            for k in scat2:
                pre[k] = _presum(gb[k][0], halfbuf[k], shape_of[k], axis_of[k], ids, "presum_" + k)
            parts.append(_scatter_halves([pre[k][1] for k in scat2], [shape_of[k] for k in scat2]))
            tags.append((recv, scat2))
        if swap2:
            for k in swap2:
                sums[k] = _sum_half(pre[k][0], recv[k], "sum_" + k)
            parts.append(_swap_with_sibling([sums[k] for k in swap2]))
            tags.append((other, swap2))
        if extra is not None:
            parts.append(extra[0])
            tags.append((extra[1], extra[2]))
        return _join(parts), tags

    def take(ex_tags, got):
        ex, tags = ex_tags
        if ex is not None:
            for (dst, names), (o0, o1) in zip(tags, ex.cuts):
                dst.update(zip(names, got[o0:o1]))

    tril = jnp.tril(jnp.ones((CHUNK, CHUNK), dtype=bool))
    wsm = jnp.where(tril[None], ws["gmlp_w_s"], 0.0)
    wsm_b = wsm.astype(BF16)
    wsmt_b = wsm.transpose(0, 2, 1).astype(BF16)
    bias = jnp.repeat(ws["gmlp_b_s"].T, GMLP_HEAD_DIM, axis=1)

    if dist:
        names = ("ffn1_w_in", "ffn1_w_out")
        wb.update(zip(names, _run_exchange(gather(names), "gather_ffn1")))
    names = ("mix_w_in", "ssm_glu_w", "up_a", "up_b", "mix_w_out")
    (x1, xh1, rstd1, x0b, h1), got = _ffn_fwd(x0, wb["ffn1_w_in"], wb["ffn1_w_out"], row(ws["ln1_g"]),
                                              row(ws["ln1_b"]), tm, "ffn1_fwd", gather(names))
    wb.update(zip(names, got))
    sp = _s5_setup(ws["ssm_lambda_re"], ws["ssm_lambda_im"], ws["ssm_log_dt"], ws["ssm_b_re"],
                   ws["ssm_b_im"], ws["ssm_c_re"], ws["ssm_c_im"], ws["ssm_d"], wb["ssm_glu_w"],
                   ws["ssm_glu_b"], tb)
    names = ("ffn2_w_out",)
    (x1b, za, zuv, gab), got = _mixin_fwd(x1, wb["mix_w_in"], tm, gather(names))
    wb.update(zip(names, got))
    names = ("ffn2_w_in",)
    (s5o, y2p, carries), got = _s5_fwd(za, sp, bsz, seq, tb, gather(names))
    wb.update(zip(names, got))
    names = ("ple_w_gate", "ple_w_proj")
    (gm,), got = _gmlp_fwd(zuv, row(ws["gmlp_ln_g"]), row(ws["gmlp_ln_b"]), wsm_b, bias, gather(names))
    wb.update(zip(names, got))
    (x2, xh2, rstd2, x2b), _ = _mixout_fwd(x1, s5o, gm, gab, wb["up_a"], wb["up_b"], wb["mix_w_out"],
                                           row(ws["ln2_g"]), row(ws["ln2_b"]), tm)
    (x3, xh3, rstd3, _, h2), _ = _ffn_fwd(x2, wb["ffn2_w_in"], wb["ffn2_w_out"], row(ws["ln3_g"]),
                                          row(ws["ln3_b"]), tm, "ffn2_fwd")
    (dx3, x3b, pb, dq, de, loss_rows), _ = _ple_loss(x3, p0, tg, wb["ple_w_gate"], wb["ple_w_proj"], tm)
    gb["ple_w_gate"], _ = _tn_matmul(x3b, dq, "dw_ple_gate", 1024, 1024)
    gb["ple_w_proj"], _ = _tn_matmul(pb, de, "dw_ple_proj", 256, 1024)
    et = exchange(scat=("ple_w_gate", "ple_w_proj"))
    (dx2, dh2, a2, df2, gs["ln3_g"], gs["ln3_b"]), got = _ffn_bwd(
        dx3, xh3, rstd3, h2, wb["ffn2_w_in"], wb["ffn2_w_out"], row(ws["ln3_g"]), tm, "ffn2_bwd", et[0])
    take(et, got)
    gb["ffn2_w_out"], _ = _tn_matmul(a2, df2, "dw_ffn2_out", 1408, 1024)
    et = exchange(scat=("ffn2_w_out",))
    gb["ffn2_w_in"], got = _tn_matmul(x2b, dh2, "dw_ffn2_in", 1024, 1408, bg=et[0])
    take(et, got)
    et = exchange(swap=("ple_w_gate", "ple_w_proj", "ffn2_w_out"))
    (dx1a, dmx, mb, dya, dyb, ds5, dgm, dgab, gs["ln2_g"], gs["ln2_b"]), got = _mixout_bwd(
        dx2, xh2, rstd2, s5o, gm, gab, wb["up_a"], wb["up_b"], wb["mix_w_out"], row(ws["ln2_g"]), tm, et[0])
    take(et, got)
    gb["mix_w_out"], _ = _tn_matmul(mb, dmx, "dw_mix_out", 1024, 1024)
    gb["up_a"], _ = _tn_matmul(s5o, dya, "dw_up_a", 512, 1024)
    gb["up_b"], _ = _tn_matmul(gm, dyb, "dw_up_b", 512, 1024)
    et = exchange(scat=("ffn2_w_in",))
    (dza, dmr, dmi, dnr, dni, da, ddsk, dgw, dgb), got = _s5_bwd(za, y2p, ds5, carries, sp, bsz, seq, tb, et[0])
    take(et, got)
    gb["ssm_glu_w"] = (dgw, dgw.astype(BF16))
    et = exchange(scat=("mix_w_out", "up_a"), swap=("ffn2_w_in",))
    (dzuv, dws, dbias, gs["gmlp_ln_g"], gs["gmlp_ln_b"]), got = _gmlp_bwd(
        zuv, dgm, row(ws["gmlp_ln_g"]), row(ws["gmlp_ln_b"]), wsm_b, wsmt_b, bias, et[0])
    take(et, got)
    et = exchange(scat=("up_b", "ssm_glu_w"))
    (dx1,), got = _mixin_bwd(dx1a, dza, dzuv, dgab, wb["mix_w_in"], tm, et[0])
    take(et, got)
    g_mi, _ = _tn_matmul(x1b, dza, "dw_mix_in_a", 1024, 512, 0, 3584)
    g_mi, _ = _tn_matmul(x1b, dzuv, "dw_mix_in_uv", 1024, 512, 1, 3584, g_mi)
    et = exchange(swap=("mix_w_out", "up_a", "up_b", "ssm_glu_w"))
    gb["mix_w_in"], got = _tn_matmul(x1b, dgab, "dw_mix_in_g", 1024, 512, 3, 3584, g_mi, bg=et[0])
    take(et, got)
    et = exchange(scat=("mix_w_in",))
    (dx0, dh1, a1, df1, gs["ln1_g"], gs["ln1_b"]), got = _ffn_bwd(
        dx1, xh1, rstd1, h1, wb["ffn1_w_in"], wb["ffn1_w_out"], row(ws["ln1_g"]), tm, "ffn1_bwd", et[0])
    take(et, got)

    d_abr = da[0].sum(axis=0).reshape(SSM_GROUPS, SSM_STATE)
    d_abi = da[1].sum(axis=0).reshape(SSM_GROUPS, SSM_STATE)
    _, vjp = jax.vjp(_s5_discretise, ws["ssm_lambda_re"], ws["ssm_lambda_im"], ws["ssm_log_dt"],
                     ws["ssm_b_re"], ws["ssm_b_im"])
    (gs["ssm_lambda_re"], gs["ssm_lambda_im"], gs["ssm_log_dt"], gs["ssm_b_re"], gs["ssm_b_im"]) = vjp(
        (d_abr, d_abi, _block_diag_in_t(dmr), _block_diag_in_t(dmi)))
    gs["ssm_c_re"] = _block_diag_out_t(dnr)
    gs["ssm_c_im"] = _block_diag_out_t(dni)
    gs["ssm_d"] = ddsk
    gs["ssm_glu_b"] = dgb
    gs["gmlp_w_s"] = dws
    gs["gmlp_b_s"] = dbias.reshape(CHUNK, GMLP_HEADS, GMLP_HEAD_DIM).sum(axis=-1).T
    gs = {k: gs[k].reshape(SMALL_2D[k]) for k in SMALL}
    grad_x = dx0.reshape(bsz, seq, D_MODEL)

    if not dist:
        gb["ffn1_w_out"], _ = _tn_matmul(a1, df1, "dw_ffn1_out", 1408, 1024)
        gb["ffn1_w_in"], _ = _tn_matmul(x0b, dh1, "dw_ffn1_in", 1024, 1408)
        return loss_rows, grad_x, gb, gs, sums, other, gathered, None
    small = SMALL + ("loss_rows",)
    et = exchange(swap=("mix_w_in",), extra=(_gather_small([gs[k] for k in SMALL] + [loss_rows]), gathered, small))
    gb["ffn1_w_out"], got = _tn_matmul(a1, df1, "dw_ffn1_out", 1408, 1024, bg=et[0])
    take(et, got)
    last = ["ffn1_w_out"] + [LAST_PIECE % q for q in range(LAST_PIECES)]
    for i in range(1, len(last) + 3):
        stage = lambda d: tuple(last[i - d:i - d + 1]) if 0 <= i - d < len(last) else ()
        et = exchange(halves=stage(1), scat2=stage(2), swap2=stage(3))
        if i < len(last):
            gb[last[i]], got = _tn_matmul(x0b, dh1, "dw_" + last[i], D_MODEL // LAST_PIECES, 1408, bg=et[0],
                                          a_cols=(i - 1, 1))
        else:
            got = _run_exchange(et[0], "reduce_last_%d" % (i - len(last)))
        take(et, got)
    return loss_rows, grad_x, gb, gs, sums, other, gathered, ids


def _adamw(w, g, m, v):
    m = ADAM_B1 * m + (1.0 - ADAM_B1) * g
    v = ADAM_B2 * v + (1.0 - ADAM_B2) * (g * g)
    m_hat = m / ADAM_C1
    v_hat = v / ADAM_C2
    delta = -ADAM_LR * (m_hat / (jnp.sqrt(v_hat) + ADAM_EPS) + ADAM_WD * w)
    return delta, m, v


def _sum_blocks(part, recv, shape, axis, chip, name):
    r, c = shape
    rb = r // 8

    def body(chip_ref, p_ref, r_ref, o_ref):
        o_ref[...] = (p_ref[...] + r_ref[0].astype(F32) + r_ref[1].astype(F32) + r_ref[2].astype(F32))

    if axis == 0:
        own = pl.BlockSpec((rb, c), lambda i, k: (k[0] * 8 + i, 0))
    else:
        own = pl.BlockSpec((rb, c), lambda i, k: (i, k[0]))
    grid_spec = pltpu.PrefetchScalarGridSpec(
        num_scalar_prefetch=1, grid=(8,),
        in_specs=[own, pl.BlockSpec((3, rb, c), lambda i, k: (0, i, 0))],
        out_specs=pl.BlockSpec((rb, c), lambda i, k: (i, 0)))
    return pl.pallas_call(body, name=name, out_shape=SDS((r, c), F32), grid_spec=grid_spec,
                          compiler_params=_params(("parallel",)))(chip, part, recv)


def _presum(part, half, shape, axis, ids, name):
    r, c = shape
    rb = r // 4

    def body(ids_ref, p_ref, h_ref, of_ref, ob_ref):
        s = p_ref[...] + h_ref[...].astype(F32)
        ob_ref[...] = s.astype(BF16)

        @pl.when(pl.program_id(1) == 0)
        def _():
            of_ref[...] = s

    if axis == 0:
        p_spec = pl.BlockSpec((rb, c), lambda i, t, ids: (ids[t] * 4 + ids[4] * 2 + i, 0))
        h_spec = pl.BlockSpec((None, rb, c), lambda i, t, ids: (ids[t], i, 0))
    else:
        p_spec = pl.BlockSpec((rb, c), lambda i, t, ids: (ids[4] * 2 + i, ids[t]))
        h_spec = pl.BlockSpec((rb, c), lambda i, t, ids: (i, ids[t]))
    grid_spec = pltpu.PrefetchScalarGridSpec(
        num_scalar_prefetch=1, grid=(2, 4), in_specs=[p_spec, h_spec],
        out_specs=(pl.BlockSpec((rb, c), lambda i, t, ids: (i, 0)),
                   pl.BlockSpec((None, rb, c), lambda i, t, ids: (t, i, 0))))
    return pl.pallas_call(body, name=name, out_shape=(SDS((r // 2, c), F32), SDS((4, r // 2, c), BF16)),
                          grid_spec=grid_spec, compiler_params=_params(("parallel", "arbitrary")))(ids, part, half)


def _sum_half(pre, recv, name):
    hr, c = pre.shape
    rb = hr // 2

    def body(p_ref, r_ref, o_ref):
        o_ref[...] = (p_ref[...] + r_ref[0].astype(F32) + r_ref[1].astype(F32) + r_ref[2].astype(F32))

    spec = pl.BlockSpec((rb, c), lambda i: (i, 0))
    return pl.pallas_call(body, name=name, grid=(2,), out_shape=SDS((hr, c), F32),
                          in_specs=[spec, pl.BlockSpec((3, rb, c), lambda i: (0, i, 0))], out_specs=spec,
                          compiler_params=_params(("parallel",)))(pre, recv)


def _adam_halves(w, mine, oth, m, v, ids, name, piece=0, prev=None):
    r, c = w.shape
    rb = mine.shape[0] // 2

    def body(ids_ref, w_ref, a_ref, b_ref, m_ref, v_ref, *rest):
        g_ref, d_ref, nm_ref, nv_ref = rest[-4:]
        g = jnp.where(pl.program_id(0) // 2 == ids_ref[4], a_ref[...], b_ref[...])
        g_ref[...] = g
        d_ref[...], nm_ref[...], nv_ref[...] = _adamw(w_ref[...], g, m_ref[...], v_ref[...])

    whole = pl.BlockSpec((rb, c), lambda i, ids: (i + 4 * piece, 0))
    part = pl.BlockSpec((rb, c), lambda i, ids: (i % 2, 0))
    in_specs = [whole, part, part, whole, whole]
    args = [w, mine, oth, m, v]
    aliases = {}
    if prev is not None:
        in_specs += [pl.BlockSpec(memory_space=pl.ANY)] * 4
        args += list(prev)
        aliases = {6: 0, 7: 1, 8: 2, 9: 3}
    grid_spec = pltpu.PrefetchScalarGridSpec(num_scalar_prefetch=1, grid=(4,), in_specs=in_specs,
                                             out_specs=(whole,) * 4)
    return pl.pallas_call(body, name=name, out_shape=tuple(SDS((r, c), F32) for _ in range(4)),
                          grid_spec=grid_spec, input_output_aliases=aliases,
                          compiler_params=_params(("parallel",)))(ids, *args)


def _adam_big(w, ga, gb, m, v, name, piece=0, prev=None):
    r, c = w.shape
    pr = ga.shape[0]
    steps = 8 if pr == r else 2
    rb = pr // steps
    off = piece * steps

    def body(w_ref, ga_ref, gb_ref, m_ref, v_ref, *rest):
        g_ref, d_ref, nm_ref, nv_ref = rest[-4:]
        g = ga_ref[...] + gb_ref[...]
        g_ref[...] = g
        d_ref[...], nm_ref[...], nv_ref[...] = _adamw(w_ref[...], g, m_ref[...], v_ref[...])

    whole = pl.BlockSpec((rb, c), lambda i: (i + off, 0))
    part = pl.BlockSpec((rb, c), lambda i: (i, 0))
    in_specs = [whole, part, part, whole, whole]
    args = [w, ga, gb, m, v]
    aliases = {}
    if prev is not None:
        in_specs += [pl.BlockSpec(memory_space=pl.ANY)] * 4
        args += list(prev)
        aliases = {5: 0, 6: 1, 7: 2, 8: 3}
    return pl.pallas_call(
        body, name=name, grid=(steps,), out_shape=tuple(SDS((r, c), F32) for _ in range(4)),
        in_specs=in_specs, out_specs=(whole,) * 4, input_output_aliases=aliases,
        compiler_params=_params(("parallel",)),
    )(*args)


def _adam_small(ws, gathered, ms, vs):
    n = len(ws)

    def body(*refs):
        w_refs, g_refs, m_refs, v_refs = refs[:n], refs[n:2 * n], refs[2 * n:3 * n], refs[3 * n:4 * n]
        outs = refs[4 * n:]
        for i in range(n):
            g = g_refs[i][0]
            for d in range(1, N_DEV):
                g = g + g_refs[i][d]
            delta, nm, nv = _adamw(w_refs[i][...], g, m_refs[i][...], v_refs[i][...])
            outs[i][...] = g
            outs[n + i][...] = delta
            outs[2 * n + i][...] = nm
            outs[3 * n + i][...] = nv

    vmem = pl.BlockSpec(memory_space=pltpu.VMEM)
    shapes = [w.shape for w in ws]
    return pl.pallas_call(
        body, name="adam_small", out_shape=tuple(SDS(s, F32) for s in shapes * 4),
        in_specs=[vmem] * (4 * n), out_specs=tuple([vmem] * (4 * n)),
        compiler_params=pltpu.CompilerParams(vmem_limit_bytes=VMEM_LIMIT_BYTES),
    )(*ws, *gathered, *ms, *vs)


def _sum_loss(gathered):
    def body(g_ref, o_ref):
        tot = g_ref[0]
        for d in range(1, N_DEV):
            tot = tot + g_ref[d]
        o_ref[...] = (0.5 / D_MODEL) * jnp.sum(tot, axis=1, keepdims=True)

    vmem = pl.BlockSpec(memory_space=pltpu.VMEM)
    return pl.pallas_call(body, name="sum_loss", out_shape=SDS((1, 1), F32), in_specs=[vmem],
                          out_specs=vmem)(gathered)


def kernel(x, p, ffn1_w_in, ffn1_w_out, ln1_g, ln1_b, mix_w_in, ssm_lambda_re, ssm_lambda_im, ssm_log_dt, ssm_b_re, ssm_b_im, ssm_c_re, ssm_c_im, ssm_d, ssm_glu_w, ssm_glu_b, gmlp_ln_g, gmlp_ln_b, gmlp_w_s, gmlp_b_s, up_a, up_b, mix_w_out, ln2_g, ln2_b, ffn2_w_in, ffn2_w_out, ln3_g, ln3_b, ple_w_proj, ple_w_gate, loss_target, m_ffn1_w_in, m_ffn1_w_out, m_ln1_g, m_ln1_b, m_mix_w_in, m_ssm_lambda_re, m_ssm_lambda_im, m_ssm_log_dt, m_ssm_b_re, m_ssm_b_im, m_ssm_c_re, m_ssm_c_im, m_ssm_d, m_ssm_glu_w, m_ssm_glu_b, m_gmlp_ln_g, m_gmlp_ln_b, m_gmlp_w_s, m_gmlp_b_s, m_up_a, m_up_b, m_mix_w_out, m_ln2_g, m_ln2_b, m_ffn2_w_in, m_ffn2_w_out, m_ln3_g, m_ln3_b, m_ple_w_proj, m_ple_w_gate, v_ffn1_w_in, v_ffn1_w_out, v_ln1_g, v_ln1_b, v_mix_w_in, v_ssm_lambda_re, v_ssm_lambda_im, v_ssm_log_dt, v_ssm_b_re, v_ssm_b_im, v_ssm_c_re, v_ssm_c_im, v_ssm_d, v_ssm_glu_w, v_ssm_glu_b, v_gmlp_ln_g, v_gmlp_ln_b, v_gmlp_w_s, v_gmlp_b_s, v_up_a, v_up_b, v_mix_w_out, v_ln2_g, v_ln2_b, v_ffn2_w_in, v_ffn2_w_out, v_ln3_g, v_ln3_b, v_ple_w_proj, v_ple_w_gate):
    given = dict(locals())
    order = ("ffn1_w_in", "ffn1_w_out", "ln1_g", "ln1_b", "mix_w_in", "ssm_lambda_re", "ssm_lambda_im",
             "ssm_log_dt", "ssm_b_re", "ssm_b_im", "ssm_c_re", "ssm_c_im", "ssm_d", "ssm_glu_w", "ssm_glu_b",
             "gmlp_ln_g", "gmlp_ln_b", "gmlp_w_s", "gmlp_b_s", "up_a", "up_b", "mix_w_out", "ln2_g", "ln2_b",
             "ffn2_w_in", "ffn2_w_out", "ln3_g", "ln3_b", "ple_w_proj", "ple_w_gate")
    assert set(order) == set(BIG + SMALL)

    shard = {k: given[k][0] for k in BIG}
    shard_b = {k: shard[k].astype(BF16) for k in BIG}
    ws = {k: given[k][0] for k in SMALL}
    loss_rows, grad_x, gb, gs, sums, other, gathered, ids = _local_step(
        x, given["p"][0], loss_target, {}, ws, shard_b)

    out = {}
    for k in BIG:
        moments = (given["m_" + k][0], given["v_" + k][0])
        if k == "ffn1_w_out":
            out[k] = _adam_halves(shard[k], sums[k], other[k], *moments, ids, "adam_" + k)
        elif k == "ffn1_w_in":
            for q in range(LAST_PIECES):
                kq = LAST_PIECE % q
                out[k] = _adam_halves(shard[k], sums[kq], other[kq], *moments, ids, "adam_" + kq, q, out.get(k))
        else:
            out[k] = _adam_big(shard[k], sums[k], other[k], *moments, "adam_" + k)

    res = _adam_small([given[k].reshape(SMALL_2D[k]) for k in SMALL], [gathered[k] for k in SMALL],
                      [given["m_" + k].reshape(SMALL_2D[k]) for k in SMALL],
                      [given["v_" + k].reshape(SMALL_2D[k]) for k in SMALL])
    ns = len(SMALL)
    for i, k in enumerate(SMALL):
        out[k] = tuple(res[j * ns + i].reshape(given[k].shape) for j in range(4))
    loss = _sum_loss(gathered["loss_rows"]).reshape(())

    lead = lambda k, j: out[k][j][None] if k in BIG else out[k][j]
    return (loss, grad_x, *[lead(k, 0) for k in order], *[lead(k, 1) for k in order],
            *[lead(k, 2) for k in order], *[lead(k, 3) for k in order])
```

```python
import math

import jax
import jax.numpy as jnp
from jax import lax
from jax.experimental import pallas as pl
from jax.experimental.pallas import tpu as pltpu

F32 = jnp.float32
BF16 = jnp.bfloat16
MESH = pl.DeviceIdType.MESH
SDS = jax.ShapeDtypeStruct

D_MODEL = 1024
D_FF = 2816
D_SSM = 512
D_GMLP = 512
SSM_GROUPS = 32
SSM_GROUP_CH = 16
SSM_STATE = 64
SSM_LANES = SSM_GROUPS * SSM_STATE
GMLP_HEADS = 8
GMLP_HEAD_DIM = 64
CHUNK = 128
PLE_DIM = 256
LN_EPS = 1e-5
ALPHA = 2.0 ** 0.25

ADAM_LR = 0.001
ADAM_B1 = 0.9
ADAM_B2 = 0.999
ADAM_EPS = 1e-08
ADAM_WD = 0.01
ADAM_STEP = 10
ADAM_C1 = 1.0 - ADAM_B1 ** ADAM_STEP
ADAM_C2 = 1.0 - ADAM_B2 ** ADAM_STEP

N_DEV = 8
VMEM_LIMIT_BYTES = 56 * 1024 * 1024
FFN_COLS = 1408
S5_BLOCKS = 4
S5_BLOCK_IN = D_SSM // S5_BLOCKS
S5_BLOCK_ST = SSM_LANES // S5_BLOCKS
SCAN_LANES = 512
TN_K_BLOCK = 2048
LAST_PIECES = 2
LAST_PIECE = "ffn1_w_in_q%d"
_G0 = math.sqrt(2.0 / math.pi)
_G1 = 0.044715


def _dot(a, b):
    return jnp.dot(a, b, preferred_element_type=F32)


def _dot_nt(a, b):
    return lax.dot_general(a, b, (((1,), (1,)), ((), ())), preferred_element_type=F32)


def _dot_tn(a, b):
    return lax.dot_general(a, b, (((0,), (0,)), ((), ())), preferred_element_type=F32)


def _sigmoid(x):
    return 1.0 / (1.0 + jnp.exp(-x))


def _gelu(x):
    t = jnp.tanh(_G0 * (x + _G1 * x * x * x))
    return 0.5 * x * (1.0 + t)


def _gelu_grad(x):
    t = jnp.tanh(_G0 * (x + _G1 * x * x * x))
    return 0.5 * (1.0 + t) + 0.5 * x * (1.0 - t * t) * _G0 * (1.0 + 3.0 * _G1 * x * x)


def _ln_fwd(r, g, b):
    mu = jnp.mean(r, axis=-1, keepdims=True)
    d = r - mu
    var = jnp.mean(d * d, axis=-1, keepdims=True)
    rstd = lax.rsqrt(var + LN_EPS)
    xh = d * rstd
    return xh * g + b, xh, rstd


def _ln_bwd(dy, xh, rstd, g):
    dxh = dy * g
    m1 = jnp.mean(dxh, axis=-1, keepdims=True)
    m2 = jnp.mean(dxh * xh, axis=-1, keepdims=True)
    return rstd * (dxh - m1 - xh * m2)


def _resident(shape):
    nd = len(shape)
    return pl.BlockSpec(shape, lambda *_: (0,) * nd, pipeline_mode=pl.Buffered(1))


def _fixed(shape):
    nd = len(shape)
    return pl.BlockSpec(shape, lambda *_: (0,) * nd)


def _rows(tm, cols):
    return pl.BlockSpec((tm, cols), lambda i: (i, 0))


def _params(sem):
    return pltpu.CompilerParams(dimension_semantics=sem, vmem_limit_bytes=VMEM_LIMIT_BYTES)


class _Exchange:
    def __init__(self, args, out_shape, sems, start, finish):
        self.args, self.out_shape, self.sems = list(args), list(out_shape), list(sems)
        self.start, self.finish = start, finish
        self.cuts = [(0, len(self.out_shape))]


def _call(body, name, grid, in_specs, out_specs, out_shape, args, scratch=(), sem=None, bg=None, aliases=None):
    aliases = {} if aliases is None else aliases
    if bg is None:
        res = pl.pallas_call(body, name=name, grid=grid, out_shape=tuple(out_shape), in_specs=list(in_specs),
                             out_specs=tuple(out_specs), scratch_shapes=list(scratch),
                             input_output_aliases=aliases, compiler_params=_params(sem))(*args)
        return tuple(res), ()
    n_in, n_out, n_bi, n_bo, n_sc = len(args), len(out_shape), len(bg.args), len(bg.out_shape), len(scratch)

    def wrapped(*refs):
        ins = refs[:n_in]
        b_ins = refs[n_in:n_in + n_bi]
        outs = refs[n_in + n_bi:n_in + n_bi + n_out]
        b_outs = refs[n_in + n_bi + n_out:n_in + n_bi + n_out + n_bo]
        rest = refs[n_in + n_bi + n_out + n_bo:]
        scr, b_sems = rest[:n_sc], rest[n_sc:]
        first = pl.program_id(0) == 0
        last = pl.program_id(0) == grid[0] - 1
        for ax in range(1, len(grid)):
            first = jnp.logical_and(first, pl.program_id(ax) == 0)
            last = jnp.logical_and(last, pl.program_id(ax) == grid[ax] - 1)

        @pl.when(first)
        def _():
            bg.start(b_ins, b_outs, b_sems)

        body(*ins, *outs, *scr)

        @pl.when(last)
        def _():
            bg.finish(b_ins, b_outs, b_sems)

    any_spec = pl.BlockSpec(memory_space=pl.ANY)
    res = pl.pallas_call(
        wrapped, name=name, grid=grid, out_shape=tuple(out_shape) + tuple(bg.out_shape),
        in_specs=list(in_specs) + [any_spec] * n_bi, out_specs=tuple(out_specs) + (any_spec,) * n_bo,
        scratch_shapes=list(scratch) + list(bg.sems), input_output_aliases=aliases,
        compiler_params=_params(tuple("arbitrary" for _ in grid)))(*args, *bg.args)
    return tuple(res[:n_out]), tuple(res[n_out:])


def _run_exchange(ex, name):
    n_i, n_o = len(ex.args), len(ex.out_shape)

    def body(*refs):
        ins, outs, sems = refs[:n_i], refs[n_i:n_i + n_o], refs[n_i + n_o:]
        ex.start(ins, outs, sems)
        ex.finish(ins, outs, sems)

    any_spec = pl.BlockSpec(memory_space=pl.ANY)
    return tuple(pl.pallas_call(body, name=name, out_shape=tuple(ex.out_shape), in_specs=[any_spec] * n_i,
                                out_specs=(any_spec,) * n_o, scratch_shapes=list(ex.sems))(*ex.args))


def _join(exchanges):
    cuts = []
    a = o = q = 0
    for e in exchanges:
        cuts.append((a, a + len(e.args), o, o + len(e.out_shape), q, q + len(e.sems)))
        a, o, q = cuts[-1][1], cuts[-1][3], cuts[-1][5]

    def start(ins, outs, sems):
        for e, (a0, a1, o0, o1, q0, q1) in zip(exchanges, cuts):
            e.start(ins[a0:a1], outs[o0:o1], sems[q0:q1])

    def finish(ins, outs, sems):
        for e, (a0, a1, o0, o1, q0, q1) in zip(exchanges, cuts):
            e.finish(ins[a0:a1], outs[o0:o1], sems[q0:q1])

    joined = _Exchange(sum((e.args for e in exchanges), []), sum((e.out_shape for e in exchanges), []),
                       sum((e.sems for e in exchanges), []), start, finish)
    joined.cuts = [(c[2], c[3]) for c in cuts]
    return joined


def _ffn_proj(x, w_in, tm, name, bg=None):
    t = x.shape[0]
    nch = D_FF // FFN_COLS

    def body(x_ref, win_ref, xb_ref, h_ref, a_ref):
        xb = x_ref[...].astype(BF16)
        xb_ref[...] = xb
        for k in range(nch):
            cg = slice(k * FFN_COLS, (k + 1) * FFN_COLS)
            cu = slice(D_FF + k * FFN_COLS, D_FF + (k + 1) * FFN_COLS)
            hg = _dot(xb, win_ref[:, cg])
            hu = _dot(xb, win_ref[:, cu])
            h_ref[:, cg] = hg.astype(BF16)
            h_ref[:, cu] = hu.astype(BF16)
            a_ref[:, cg] = (hg * _sigmoid(hg) * hu).astype(BF16)

    return _call(
        body, name, (t // tm,),
        [_rows(tm, D_MODEL), _resident((D_MODEL, 2 * D_FF))],
        (_rows(tm, D_MODEL), _rows(tm, 2 * D_FF), _rows(tm, D_FF)),
        (SDS((t, D_MODEL), BF16), SDS((t, 2 * D_FF), BF16), SDS((t, D_FF), BF16)),
        (x, w_in), sem=("parallel",), bg=bg)


def _ffn_out(x, a, w_out, g, b, tm, name, bg=None):
    t = x.shape[0]

    def body(x_ref, a_ref, wout_ref, g_ref, b_ref, xn_ref, xh_ref, rstd_ref):
        f = _dot(a_ref[...], wout_ref[...])
        y, xh, rstd = _ln_fwd(ALPHA * x_ref[...] + 0.5 * f, g_ref[...], b_ref[...])
        xn_ref[...] = y
        xh_ref[...] = xh
        rstd_ref[...] = rstd

    return _call(
        body, name, (t // tm,),
        [_rows(tm, D_MODEL), _rows(tm, D_FF), _resident((D_FF, D_MODEL)), _fixed((1, D_MODEL)), _fixed((1, D_MODEL))],
        (_rows(tm, D_MODEL), _rows(tm, D_MODEL), _rows(tm, 1)),
        (SDS((t, D_MODEL), F32), SDS((t, D_MODEL), F32), SDS((t, 1), F32)),
        (x, a, w_out, g, b), sem=("parallel",), bg=bg)


def _ffn_bwd(dxn, xh, rstd, h, w_in, w_out, g, tm, name, bg=None):
    t = dxn.shape[0]
    nch = D_FF // FFN_COLS

    def body(dxn_ref, xh_ref, rstd_ref, h_ref, win_ref, wout_ref, g_ref,
             dx_ref, dh_ref, df_ref, dg_ref, db_ref):
        @pl.when(pl.program_id(0) == 0)
        def _():
            dg_ref[...] = jnp.zeros_like(dg_ref)
            db_ref[...] = jnp.zeros_like(db_ref)

        dy = dxn_ref[...]
        xhv = xh_ref[...]
        dr = _ln_bwd(dy, xhv, rstd_ref[...], g_ref[...])
        dg_ref[...] += jnp.sum(dy * xhv, axis=0, keepdims=True)
        db_ref[...] += jnp.sum(dy, axis=0, keepdims=True)
        df = (0.5 * dr).astype(BF16)
        df_ref[...] = df
        dx = ALPHA * dr
        for k in range(nch):
            cg = slice(k * FFN_COLS, (k + 1) * FFN_COLS)
            cu = slice(D_FF + k * FFN_COLS, D_FF + (k + 1) * FFN_COLS)
            hg = h_ref[:, cg].astype(F32)
            hu = h_ref[:, cu].astype(F32)
            sg = _sigmoid(hg)
            silu = hg * sg
            da = _dot_nt(df, wout_ref[cg, :])
            dhu = (da * silu).astype(BF16)
            dhg = (da * hu * (sg * (1.0 + hg * (1.0 - sg)))).astype(BF16)
            dh_ref[:, cg] = dhg
            dh_ref[:, cu] = dhu
            dx = dx + _dot_nt(dhg, win_ref[:, cg]) + _dot_nt(dhu, win_ref[:, cu])
        dx_ref[...] = dx

    return _call(
        body, name, (t // tm,),
        [_rows(tm, D_MODEL), _rows(tm, D_MODEL), _rows(tm, 1), _rows(tm, 2 * D_FF),
         _resident((D_MODEL, 2 * D_FF)), _resident((D_FF, D_MODEL)), _fixed((1, D_MODEL))],
        (_rows(tm, D_MODEL), _rows(tm, 2 * D_FF), _rows(tm, D_MODEL),
         _fixed((1, D_MODEL)), _fixed((1, D_MODEL))),
        (SDS((t, D_MODEL), F32), SDS((t, 2 * D_FF), BF16), SDS((t, D_MODEL), BF16),
         SDS((1, D_MODEL), F32), SDS((1, D_MODEL), F32)),
        (dxn, xh, rstd, h, w_in, w_out, g), sem=("arbitrary",), bg=bg)


def _tn_matmul(a, b, name, bm, bn, col_block=0, total_cols=None, prev=None, bg=None, a_cols=None):
    t, m = a.shape
    a_first = 0
    if a_cols is not None:
        a_first, m = a_cols[0], a_cols[1] * bm
    n = b.shape[1]
    total_cols = n if total_cols is None else total_cols
    bk = min(TN_K_BLOCK, t)
    nk = t // bk
    n_in = 2 if prev is None else 4

    def body(*refs):
        a_ref, b_ref = refs[0], refs[1]
        o_ref, ob_ref = refs[n_in], refs[n_in + 1]
        k = pl.program_id(2)

        @pl.when(k == 0)
        def _():
            o_ref[...] = jnp.zeros_like(o_ref)

        o_ref[...] += _dot_tn(a_ref[...], b_ref[...])

        @pl.when(k == nk - 1)
        def _():
            ob_ref[...] = o_ref[...].astype(BF16)

    in_specs = [pl.BlockSpec((bk, bm), lambda i, j, k: (k, i + a_first)),
                pl.BlockSpec((bk, bn), lambda i, j, k: (k, j))]
    args = [a, b]
    aliases = {}
    if prev is not None:
        in_specs += [pl.BlockSpec(memory_space=pl.ANY), pl.BlockSpec(memory_space=pl.ANY)]
        args += list(prev)
        aliases = {2: 0, 3: 1}
    out_spec = pl.BlockSpec((bm, bn), lambda i, j, k: (i, j + col_block))
    return _call(body, name, (m // bm, n // bn, nk), in_specs, (out_spec, out_spec),
                 (SDS((m, total_cols), F32), SDS((m, total_cols), BF16)), args,
                 sem=("parallel", "parallel", "arbitrary"), bg=bg, aliases=aliases)


def _mixin_fwd(x1, w, tm, bg=None):
    t = x1.shape[0]

    def body(x_ref, w_ref, xb_ref, za_ref, zuv_ref, gab_ref):
        xb = x_ref[...].astype(BF16)
        xb_ref[...] = xb
        za_ref[...] = _dot(xb, w_ref[:, 0:512]).astype(BF16)
        zuv_ref[...] = _dot(xb, w_ref[:, 512:1536]).astype(BF16)
        gab_ref[...] = _dot(xb, w_ref[:, 1536:3584]).astype(BF16)

    return _call(
        body, "mixin_fwd", (t // tm,),
        [_rows(tm, D_MODEL), _resident((D_MODEL, 3584))],
        (_rows(tm, D_MODEL), _rows(tm, 512), _rows(tm, 1024), _rows(tm, 2048)),
        (SDS((t, D_MODEL), BF16), SDS((t, 512), BF16), SDS((t, 1024), BF16), SDS((t, 2048), BF16)),
        (x1, w), sem=("parallel",), bg=bg)


def _mixin_bwd(dx1a, dza, dzuv, dgab, w, tm, bg=None):
    t = dx1a.shape[0]

    def body(d_ref, dza_ref, dzuv_ref, dgab_ref, w_ref, dx_ref):
        dx_ref[...] = (d_ref[...] + _dot_nt(dza_ref[...], w_ref[:, 0:512])
                       + _dot_nt(dzuv_ref[...], w_ref[:, 512:1536])
                       + _dot_nt(dgab_ref[...], w_ref[:, 1536:3584]))

    return _call(
        body, "mixin_bwd", (t // tm,),
        [_rows(tm, D_MODEL), _rows(tm, 512), _rows(tm, 1024), _rows(tm, 2048), _resident((D_MODEL, 3584))],
        (_rows(tm, D_MODEL),), (SDS((t, D_MODEL), F32),),
        (dx1a, dza, dzuv, dgab, w), sem=("parallel",), bg=bg)


def _unrolled(lo, hi, body, carry):
    for j in range(lo, hi):
        carry = body(j, carry)
    return carry


def _scan_fwd(hr_ref, hi_ref, a_ref, ap_ref, carry_ref, seg, cin_ref):
    for lc in range(SSM_LANES // SCAN_LANES):
        ls = slice(lc * SCAN_LANES, (lc + 1) * SCAN_LANES)
        a_r = jnp.broadcast_to(a_ref[0:1, ls], (8, SCAN_LANES))
        a_i = jnp.broadcast_to(a_ref[1:2, ls], (8, SCAN_LANES))

        def step(j, hc, ls=ls, a_r=a_r, a_i=a_i):
            h_r, h_i = hc
            rows = pl.ds(j * 8, 8)
            n_r = a_r * h_r - a_i * h_i + hr_ref[rows, ls]
            n_i = a_r * h_i + a_i * h_r + hi_ref[rows, ls]
            hr_ref[rows, ls] = n_r
            hi_ref[rows, ls] = n_i
            return n_r, n_i

        zero = jnp.zeros((8, SCAN_LANES), F32)
        f_r, f_i = _unrolled(0, seg, step, (zero, zero))
        c_r = carry_ref[0:1, ls]
        c_i = carry_ref[1:2, ls]
        p_r = ap_ref[0:1, ls]
        p_i = ap_ref[1:2, ls]
        rows_r, rows_i = [], []
        for s in range(8):
            rows_r.append(c_r)
            rows_i.append(c_i)
            c_r, c_i = (f_r[s:s + 1] + p_r * c_r - p_i * c_i,
                        f_i[s:s + 1] + p_r * c_i + p_i * c_r)
        carry_ref[0:1, ls] = c_r
        carry_ref[1:2, ls] = c_i
        cin_r = jnp.concatenate(rows_r, axis=0)
        cin_i = jnp.concatenate(rows_i, axis=0)
        if cin_ref is not None:
            cin_ref[0, :, ls] = cin_r
            cin_ref[1, :, ls] = cin_i

        def fix(j, cc, ls=ls, a_r=a_r, a_i=a_i):
            c_r, c_i = cc
            c_r, c_i = a_r * c_r - a_i * c_i, a_r * c_i + a_i * c_r
            rows = pl.ds(j * 8, 8)
            hr_ref[rows, ls] = hr_ref[rows, ls] + c_r
            hi_ref[rows, ls] = hi_ref[rows, ls] + c_i
            return c_r, c_i

        _unrolled(0, seg, fix, (cin_r, cin_i))


def _scan_bwd(gr_ref, gi_ref, hr_ref, hi_ref, cin_ref, a_ref, ap_ref, rcarry_ref, da_ref, seg):
    for lc in range(SSM_LANES // SCAN_LANES):
        ls = slice(lc * SCAN_LANES, (lc + 1) * SCAN_LANES)
        a_r = jnp.broadcast_to(a_ref[0:1, ls], (8, SCAN_LANES))
        a_i = jnp.broadcast_to(a_ref[1:2, ls], (8, SCAN_LANES))

        def step(t, gc, ls=ls, a_r=a_r, a_i=a_i):
            g_r, g_i = gc
            rows = pl.ds((seg - 1 - t) * 8, 8)
            n_r = gr_ref[rows, ls] + a_r * g_r + a_i * g_i
            n_i = gi_ref[rows, ls] + a_r * g_i - a_i * g_r
            gr_ref[rows, ls] = n_r
            gi_ref[rows, ls] = n_i
            return n_r, n_i

        zero = jnp.zeros((8, SCAN_LANES), F32)
        f_r, f_i = _unrolled(0, seg, step, (zero, zero))
        c_r = rcarry_ref[0:1, ls]
        c_i = rcarry_ref[1:2, ls]
        p_r = ap_ref[0:1, ls]
        p_i = ap_ref[1:2, ls]
        rows_r, rows_i = [None] * 8, [None] * 8
        for s in range(7, -1, -1):
            rows_r[s] = c_r
            rows_i[s] = c_i
            c_r, c_i = (f_r[s:s + 1] + p_r * c_r + p_i * c_i,
                        f_i[s:s + 1] + p_r * c_i - p_i * c_r)
        rcarry_ref[0:1, ls] = c_r
        rcarry_ref[1:2, ls] = c_i
        cin_r = jnp.concatenate(rows_r, axis=0)
        cin_i = jnp.concatenate(rows_i, axis=0)

        def fix_row(j_rows, hp_r, hp_i, cc, ls=ls, a_r=a_r, a_i=a_i):
            c_r, c_i, acc_r, acc_i = cc
            c_r, c_i = a_r * c_r + a_i * c_i, a_r * c_i - a_i * c_r
            g_r = gr_ref[j_rows, ls] + c_r
            g_i = gi_ref[j_rows, ls] + c_i
            gr_ref[j_rows, ls] = g_r
            gi_ref[j_rows, ls] = g_i
            acc_r = acc_r + g_r * hp_r + g_i * hp_i
            acc_i = acc_i + g_i * hp_r - g_r * hp_i
            return c_r, c_i, acc_r, acc_i

        def fix(t, cc, ls=ls, fix_row=fix_row):
            j = seg - 1 - t
            rows = pl.ds(j * 8, 8)
            prev = pl.ds((j - 1) * 8, 8)
            return fix_row(rows, hr_ref[prev, ls], hi_ref[prev, ls], cc)

        cc = _unrolled(0, seg - 1, fix, (cin_r, cin_i, zero, zero))
        _, _, acc_r, acc_i = fix_row(pl.ds(0, 8), cin_ref[0, :, ls], cin_ref[1, :, ls], cc)
        da_ref[0, :, ls] += acc_r
        da_ref[1, :, ls] += acc_i


def _s5_fwd(za, sp, bsz, seq, tb, bg=None):
    nb = seq // tb
    seg = tb // 8
    t = bsz * seq

    def body(za_ref, perm_ref, permt_ref, mre_ref, mim_ref, nre_ref, nim_ref, a_ref, ap_ref,
             dsk_ref, gw_ref, gb_ref, out_ref, y2_ref, car_ref, hr_ref, hi_ref, carry_ref):
        @pl.when(pl.program_id(1) == 0)
        def _():
            carry_ref[...] = jnp.zeros_like(carry_ref)

        car_ref[0] = carry_ref[...]
        up = _dot(perm_ref[...], za_ref[...])
        upb = up.astype(BF16)
        for bb in range(S5_BLOCKS):
            ub = upb[:, bb * S5_BLOCK_IN:(bb + 1) * S5_BLOCK_IN]
            st = slice(bb * S5_BLOCK_ST, (bb + 1) * S5_BLOCK_ST)
            hr_ref[:, st] = _dot(ub, mre_ref[bb])
            hi_ref[:, st] = _dot(ub, mim_ref[bb])
        _scan_fwd(hr_ref, hi_ref, a_ref, ap_ref, carry_ref, seg, None)
        ys = []
        for bb in range(S5_BLOCKS):
            st = slice(bb * S5_BLOCK_ST, (bb + 1) * S5_BLOCK_ST)
            ys.append(_dot(hr_ref[:, st].astype(BF16), nre_ref[bb])
                      - _dot(hi_ref[:, st].astype(BF16), nim_ref[bb]))
        y2 = jnp.concatenate(ys, axis=1) + dsk_ref[...] * up
        y2_ref[...] = y2
        y3 = _gelu(y2)
        gl = _dot(y3.astype(BF16), gw_ref[...]) + gb_ref[...]
        oa = y3 * _sigmoid(gl)
        out_ref[...] = _dot(permt_ref[...], oa.astype(BF16)).astype(BF16)

    blk = pl.BlockSpec((tb, D_SSM), lambda b, j: (b * nb + j, 0))
    m_shape = (S5_BLOCKS, S5_BLOCK_IN, S5_BLOCK_ST)
    n_shape = (S5_BLOCKS, S5_BLOCK_ST, S5_BLOCK_IN)
    return _call(
        body, "s5_fwd", (bsz, nb),
        [blk, _fixed((tb, tb)), _fixed((tb, tb)), _fixed(m_shape), _fixed(m_shape), _fixed(n_shape),
         _fixed(n_shape), _fixed((2, SSM_LANES)), _fixed((2, SSM_LANES)), _fixed((1, D_SSM)),
         _fixed((D_SSM, D_SSM)), _fixed((1, D_SSM))],
        (blk, blk, pl.BlockSpec((1, 2, SSM_LANES), lambda b, j: (b * nb + j, 0, 0))),
        (SDS((t, D_SSM), BF16), SDS((t, D_SSM), F32), SDS((bsz * nb, 2, SSM_LANES), F32)),
        (za, sp["perm"], sp["permt"], sp["mre"], sp["mim"], sp["nre"], sp["nim"], sp["a"], sp["ap"],
         sp["dskip"], sp["glu_w"], sp["glu_b"]),
        scratch=[pltpu.VMEM((tb, SSM_LANES), F32), pltpu.VMEM((tb, SSM_LANES), F32),
                 pltpu.VMEM((2, SSM_LANES), F32)],
        sem=("arbitrary", "arbitrary"), bg=bg)


def _s5_bwd(za, y2p, doa, carries, sp, bsz, seq, tb, bg=None):
    nb = seq // tb
    seg = tb // 8
    t = bsz * seq

    def body(za_ref, y2_ref, doa_ref, car_ref, perm_ref, permt_ref, mre_ref, mim_ref, mtre_ref, mtim_ref,
             nre_ref, nim_ref, ntre_ref, ntim_ref, a_ref, ap_ref, dsk_ref, gw_ref, gwt_ref, gb_ref,
             dza_ref, dmr_ref, dmi_ref, dnr_ref, dni_ref, da_ref, ddsk_ref, dgw_ref, dgb_ref,
             hr_ref, hi_ref, gr_ref, gi_ref, cin_ref, carry_ref, rcarry_ref):
        first = jnp.logical_and(pl.program_id(0) == 0, pl.program_id(1) == 0)

        @pl.when(first)
        def _():
            for r in (dmr_ref, dmi_ref, dnr_ref, dni_ref, da_ref, ddsk_ref, dgw_ref, dgb_ref):
                r[...] = jnp.zeros_like(r)

        @pl.when(pl.program_id(1) == 0)
        def _():
            rcarry_ref[...] = jnp.zeros_like(rcarry_ref)

        carry_ref[...] = car_ref[0]
        perm = perm_ref[...]
        up = _dot(perm, za_ref[...])
        upb = up.astype(BF16)
        for bb in range(S5_BLOCKS):
            ub = upb[:, bb * S5_BLOCK_IN:(bb + 1) * S5_BLOCK_IN]
            st = slice(bb * S5_BLOCK_ST, (bb + 1) * S5_BLOCK_ST)
            hr_ref[:, st] = _dot(ub, mre_ref[bb])
            hi_ref[:, st] = _dot(ub, mim_ref[bb])
        _scan_fwd(hr_ref, hi_ref, a_ref, ap_ref, carry_ref, seg, cin_ref)

        y2 = y2_ref[...]
        y3 = _gelu(y2)
        y3b = y3.astype(BF16)
        sg = _sigmoid(_dot(y3b, gw_ref[...]) + gb_ref[...])
        d0 = doa_ref[...]
        d_hi = d0.astype(BF16)
        d1 = d0 - d_hi.astype(F32)
        d_mid = d1.astype(BF16)
        d_lo = (d1 - d_mid.astype(F32)).astype(BF16)
        doap = _dot(perm, d_hi) + _dot(perm, d_mid) + _dot(perm, d_lo)
        dgl = doap * y3 * sg * (1.0 - sg)
        dglb = dgl.astype(BF16)
        dy3 = doap * sg + _dot(dglb, gwt_ref[...])
        dgw_ref[...] += _dot_tn(y3b, dglb)
        dgb_ref[...] += jnp.sum(dgl, axis=0, keepdims=True)
        dy2 = dy3 * _gelu_grad(y2)
        ddsk_ref[...] += jnp.sum(dy2 * up, axis=0, keepdims=True)
        dyb = dy2.astype(BF16)
        for bb in range(S5_BLOCKS):
            dyc = dyb[:, bb * S5_BLOCK_IN:(bb + 1) * S5_BLOCK_IN]
            st = slice(bb * S5_BLOCK_ST, (bb + 1) * S5_BLOCK_ST)
            gr_ref[:, st] = _dot(dyc, ntre_ref[bb])
            gi_ref[:, st] = -_dot(dyc, ntim_ref[bb])
            dnr_ref[bb] += _dot_tn(hr_ref[:, st].astype(BF16), dyc)
            dni_ref[bb] += -_dot_tn(hi_ref[:, st].astype(BF16), dyc)
        _scan_bwd(gr_ref, gi_ref, hr_ref, hi_ref, cin_ref, a_ref, ap_ref, rcarry_ref, da_ref, seg)
        dus = []
        for bb in range(S5_BLOCKS):
            st = slice(bb * S5_BLOCK_ST, (bb + 1) * S5_BLOCK_ST)
            grb = gr_ref[:, st].astype(BF16)
            gib = gi_ref[:, st].astype(BF16)
            dus.append(_dot(grb, mtre_ref[bb]) + _dot(gib, mtim_ref[bb]))
            ub = upb[:, bb * S5_BLOCK_IN:(bb + 1) * S5_BLOCK_IN]
            dmr_ref[bb] += _dot_tn(ub, grb)
            dmi_ref[bb] += _dot_tn(ub, gib)
        du = jnp.concatenate(dus, axis=1) + dy2 * dsk_ref[...]
        dza_ref[...] = _dot(permt_ref[...], du.astype(BF16)).astype(BF16)

    def rev(b, j):
        return (b * nb + (nb - 1 - j), 0)

    blk = pl.BlockSpec((tb, D_SSM), rev)
    m_shape = (S5_BLOCKS, S5_BLOCK_IN, S5_BLOCK_ST)
    n_shape = (S5_BLOCKS, S5_BLOCK_ST, S5_BLOCK_IN)
    return _call(
        body, "s5_bwd", (bsz, nb),
        [blk, blk, blk, pl.BlockSpec((1, 2, SSM_LANES), lambda b, j: (b * nb + (nb - 1 - j), 0, 0)),
         _fixed((tb, tb)), _fixed((tb, tb)), _fixed(m_shape), _fixed(m_shape), _fixed(n_shape), _fixed(n_shape),
         _fixed(n_shape), _fixed(n_shape), _fixed(m_shape), _fixed(m_shape),
         _fixed((2, SSM_LANES)), _fixed((2, SSM_LANES)), _fixed((1, D_SSM)),
         _fixed((D_SSM, D_SSM)), _fixed((D_SSM, D_SSM)), _fixed((1, D_SSM))],
        (blk, _fixed(m_shape), _fixed(m_shape), _fixed(n_shape), _fixed(n_shape),
         _fixed((2, 8, SSM_LANES)), _fixed((1, D_SSM)), _fixed((D_SSM, D_SSM)), _fixed((1, D_SSM))),
        (SDS((t, D_SSM), BF16), SDS(m_shape, F32), SDS(m_shape, F32), SDS(n_shape, F32), SDS(n_shape, F32),
         SDS((2, 8, SSM_LANES), F32), SDS((1, D_SSM), F32), SDS((D_SSM, D_SSM), F32), SDS((1, D_SSM), F32)),
        (za, y2p, doa, carries, sp["perm"], sp["permt"], sp["mre"], sp["mim"], sp["mtre"], sp["mtim"],
         sp["nre"], sp["nim"], sp["ntre"], sp["ntim"], sp["a"], sp["ap"], sp["dskip"], sp["glu_w"],
         sp["glu_wt"], sp["glu_b"]),
        scratch=[pltpu.VMEM((tb, SSM_LANES), F32), pltpu.VMEM((tb, SSM_LANES), F32),
                 pltpu.VMEM((tb, SSM_LANES), F32), pltpu.VMEM((tb, SSM_LANES), F32),
                 pltpu.VMEM((2, 8, SSM_LANES), F32), pltpu.VMEM((2, SSM_LANES), F32),
                 pltpu.VMEM((2, SSM_LANES), F32)],
        sem=("arbitrary", "arbitrary"), bg=bg)


def _gmlp_spatial(ws_ref, vb):
    lane = lax.broadcasted_iota(jnp.int32, (CHUNK, 128), 1)
    parts = []
    for j in range(GMLP_HEADS // 2):
        vp = vb[:, 128 * j:128 * (j + 1)]
        parts.append(jnp.where(lane < GMLP_HEAD_DIM, _dot(ws_ref[2 * j], vp), _dot(ws_ref[2 * j + 1], vp)))
    return jnp.concatenate(parts, axis=1)


def _gmlp_fwd(zuv, ln_g, ln_b, wsm, bias, bg=None):
    t = zuv.shape[0]

    def body(z_ref, g_ref, b_ref, ws_ref, bias_ref, out_ref):
        u = _gelu(z_ref[:, 0:D_GMLP].astype(F32))
        v0 = _gelu(z_ref[:, D_GMLP:2 * D_GMLP].astype(F32))
        v, _, _ = _ln_fwd(v0, g_ref[...], b_ref[...])
        s = _gmlp_spatial(ws_ref, v.astype(BF16)) + bias_ref[...]
        out_ref[...] = (u * s).astype(BF16)

    return _call(
        body, "gmlp_fwd", (t // CHUNK,),
        [_rows(CHUNK, 2 * D_GMLP), _fixed((1, D_GMLP)), _fixed((1, D_GMLP)),
         _fixed((GMLP_HEADS, CHUNK, CHUNK)), _fixed((CHUNK, D_GMLP))],
        (_rows(CHUNK, D_GMLP),), (SDS((t, D_GMLP), BF16),),
        (zuv, ln_g, ln_b, wsm, bias), sem=("parallel",), bg=bg)


def _gmlp_bwd(zuv, dgm, ln_g, ln_b, wsm, wsmt, bias, bg=None):
    t = zuv.shape[0]

    def body(z_ref, d_ref, g_ref, b_ref, ws_ref, wst_ref, bias_ref,
             dz_ref, dws_ref, dbias_ref, dg_ref, db_ref):
        @pl.when(pl.program_id(0) == 0)
        def _():
            for r in (dws_ref, dbias_ref, dg_ref, db_ref):
                r[...] = jnp.zeros_like(r)

        zu = z_ref[:, 0:D_GMLP].astype(F32)
        zv = z_ref[:, D_GMLP:2 * D_GMLP].astype(F32)
        u = _gelu(zu)
        v0 = _gelu(zv)
        gam = g_ref[...]
        v, vhat, rstd = _ln_fwd(v0, gam, b_ref[...])
        vb = v.astype(BF16)
        s = _gmlp_spatial(ws_ref, vb) + bias_ref[...]
        d = d_ref[...]
        dz_ref[:, 0:D_GMLP] = (d * s * _gelu_grad(zu)).astype(BF16)
        ds = d * u
        dbias_ref[...] += ds
        dsb = ds.astype(BF16)
        lane = lax.broadcasted_iota(jnp.int32, (CHUNK, 128), 1)
        tril = (lax.broadcasted_iota(jnp.int32, (CHUNK, CHUNK), 0)
                >= lax.broadcasted_iota(jnp.int32, (CHUNK, CHUNK), 1))
        zero_b = jnp.zeros((CHUNK, 128), BF16)
        parts = []
        for j in range(GMLP_HEADS // 2):
            dsp = dsb[:, 128 * j:128 * (j + 1)]
            vp = vb[:, 128 * j:128 * (j + 1)]
            parts.append(jnp.where(lane < GMLP_HEAD_DIM, _dot(wst_ref[2 * j], dsp),
                                   _dot(wst_ref[2 * j + 1], dsp)))
            lo = jnp.where(lane < GMLP_HEAD_DIM, dsp, zero_b)
            hi = jnp.where(lane < GMLP_HEAD_DIM, zero_b, dsp)
            dws_ref[2 * j] += jnp.where(tril, _dot_nt(lo, vp), 0.0)
            dws_ref[2 * j + 1] += jnp.where(tril, _dot_nt(hi, vp), 0.0)
        dv = jnp.concatenate(parts, axis=1)
        dg_ref[...] += jnp.sum(dv * vhat, axis=0, keepdims=True)
        db_ref[...] += jnp.sum(dv, axis=0, keepdims=True)
        dz_ref[:, D_GMLP:2 * D_GMLP] = (_ln_bwd(dv, vhat, rstd, gam) * _gelu_grad(zv)).astype(BF16)

    return _call(
        body, "gmlp_bwd", (t // CHUNK,),
        [_rows(CHUNK, 2 * D_GMLP), _rows(CHUNK, D_GMLP), _fixed((1, D_GMLP)), _fixed((1, D_GMLP)),
         _fixed((GMLP_HEADS, CHUNK, CHUNK)), _fixed((GMLP_HEADS, CHUNK, CHUNK)), _fixed((CHUNK, D_GMLP))],
        (_rows(CHUNK, 2 * D_GMLP), _fixed((GMLP_HEADS, CHUNK, CHUNK)), _fixed((CHUNK, D_GMLP)),
         _fixed((1, D_GMLP)), _fixed((1, D_GMLP))),
        (SDS((t, 2 * D_GMLP), BF16), SDS((GMLP_HEADS, CHUNK, CHUNK), F32), SDS((CHUNK, D_GMLP), F32),
         SDS((1, D_GMLP), F32), SDS((1, D_GMLP), F32)),
        (zuv, dgm, ln_g, ln_b, wsm, wsmt, bias), sem=("arbitrary",), bg=bg)


def _mixout_fwd(x1, s5o, gm, gab, ua, ub, wmo, g, b, tm, bg=None):
    t = x1.shape[0]

    def body(x_ref, s_ref, m_ref, gab_ref, ua_ref, ub_ref, wmo_ref, g_ref, b_ref,
             xn_ref, xh_ref, rstd_ref):
        ya = _dot(s_ref[...], ua_ref[...])
        yb = _dot(m_ref[...], ub_ref[...])
        mix = (_sigmoid(gab_ref[:, 0:D_MODEL].astype(F32)) * ya
               + _sigmoid(gab_ref[:, D_MODEL:2 * D_MODEL].astype(F32)) * yb)
        r = ALPHA * x_ref[...] + _dot(mix.astype(BF16), wmo_ref[...])
        y, xh, rstd = _ln_fwd(r, g_ref[...], b_ref[...])
        xn_ref[...] = y
        xh_ref[...] = xh
        rstd_ref[...] = rstd

    return _call(
        body, "mixout_fwd", (t // tm,),
        [_rows(tm, D_MODEL), _rows(tm, D_SSM), _rows(tm, D_GMLP), _rows(tm, 2 * D_MODEL),
         _resident((D_SSM, D_MODEL)), _resident((D_GMLP, D_MODEL)), _resident((D_MODEL, D_MODEL)),
         _fixed((1, D_MODEL)), _fixed((1, D_MODEL))],
        (_rows(tm, D_MODEL), _rows(tm, D_MODEL), _rows(tm, 1)),
        (SDS((t, D_MODEL), F32), SDS((t, D_MODEL), F32), SDS((t, 1), F32)),
        (x1, s5o, gm, gab, ua, ub, wmo, g, b), sem=("parallel",), bg=bg)


def _mixout_bwd(dx2, xh, rstd, s5o, gm, gab, ua, ub, wmo, g, tm, bg=None):
    t = dx2.shape[0]

    def body(d_ref, xh_ref, rstd_ref, s_ref, m_ref, gab_ref, ua_ref, ub_ref, wmo_ref, g_ref,
             dx1_ref, dmx_ref, mb_ref, dya_ref, dyb_ref, ds5_ref, dgm_ref, dgab_ref, dg_ref, db_ref):
        @pl.when(pl.program_id(0) == 0)
        def _():
            dg_ref[...] = jnp.zeros_like(dg_ref)
            db_ref[...] = jnp.zeros_like(db_ref)

        dy = d_ref[...]
        xhv = xh_ref[...]
        dr = _ln_bwd(dy, xhv, rstd_ref[...], g_ref[...])
        dg_ref[...] += jnp.sum(dy * xhv, axis=0, keepdims=True)
        db_ref[...] += jnp.sum(dy, axis=0, keepdims=True)
        dx1_ref[...] = ALPHA * dr
        drb = dr.astype(BF16)
        dmx_ref[...] = drb
        dm = _dot_nt(drb, wmo_ref[...])
        ya = _dot(s_ref[...], ua_ref[...])
        yb = _dot(m_ref[...], ub_ref[...])
        sa = _sigmoid(gab_ref[:, 0:D_MODEL].astype(F32))
        sb = _sigmoid(gab_ref[:, D_MODEL:2 * D_MODEL].astype(F32))
        mb_ref[...] = (sa * ya + sb * yb).astype(BF16)
        dya = (dm * sa).astype(BF16)
        dyb = (dm * sb).astype(BF16)
        dya_ref[...] = dya
        dyb_ref[...] = dyb
        dgab_ref[:, 0:D_MODEL] = (dm * ya * sa * (1.0 - sa)).astype(BF16)
        dgab_ref[:, D_MODEL:2 * D_MODEL] = (dm * yb * sb * (1.0 - sb)).astype(BF16)
        ds5_ref[...] = _dot_nt(dya, ua_ref[...])
        dgm_ref[...] = _dot_nt(dyb, ub_ref[...])

    return _call(
        body, "mixout_bwd", (t // tm,),
        [_rows(tm, D_MODEL), _rows(tm, D_MODEL), _rows(tm, 1), _rows(tm, D_SSM), _rows(tm, D_GMLP),
         _rows(tm, 2 * D_MODEL), _resident((D_SSM, D_MODEL)), _resident((D_GMLP, D_MODEL)),
         _resident((D_MODEL, D_MODEL)), _fixed((1, D_MODEL))],
        (_rows(tm, D_MODEL), _rows(tm, D_MODEL), _rows(tm, D_MODEL), _rows(tm, D_MODEL),
         _rows(tm, D_MODEL), _rows(tm, D_SSM), _rows(tm, D_GMLP), _rows(tm, 2 * D_MODEL),
         _fixed((1, D_MODEL)), _fixed((1, D_MODEL))),
        (SDS((t, D_MODEL), F32), SDS((t, D_MODEL), BF16), SDS((t, D_MODEL), BF16),
         SDS((t, D_MODEL), BF16), SDS((t, D_MODEL), BF16), SDS((t, D_SSM), F32),
         SDS((t, D_GMLP), F32), SDS((t, 2 * D_MODEL), BF16),
         SDS((1, D_MODEL), F32), SDS((1, D_MODEL), F32)),
        (dx2, xh, rstd, s5o, gm, gab, ua, ub, wmo, g), sem=("arbitrary",), bg=bg)


def _ple_loss(x3, p, tgt, wpg, wpp, tm, bg=None):
    t = x3.shape[0]

    def body(x_ref, p_ref, t_ref, wpg_ref, wpp_ref, dx_ref, xb_ref, pb_ref, dq_ref, de_ref, loss_ref):
        @pl.when(pl.program_id(0) == 0)
        def _():
            loss_ref[...] = jnp.zeros_like(loss_ref)

        x3v = x_ref[...]
        xb = x3v.astype(BF16)
        pb = p_ref[...].astype(BF16)
        xb_ref[...] = xb
        pb_ref[...] = pb
        s = _sigmoid(_dot(xb, wpg_ref[...]))
        e = _dot(pb, wpp_ref[...])
        diff = x3v + s * e - t_ref[...]
        loss_ref[...] += jnp.sum(diff * diff, axis=0, keepdims=True)
        dout = diff * (1.0 / D_MODEL)
        de_ref[...] = (dout * s).astype(BF16)
        dq = (dout * e * s * (1.0 - s)).astype(BF16)
        dq_ref[...] = dq
        dx_ref[...] = dout + _dot_nt(dq, wpg_ref[...])

    return _call(
        body, "ple_loss", (t // tm,),
        [_rows(tm, D_MODEL), _rows(tm, PLE_DIM), _rows(tm, D_MODEL),
         _resident((D_MODEL, D_MODEL)), _resident((PLE_DIM, D_MODEL))],
        (_rows(tm, D_MODEL), _rows(tm, D_MODEL), _rows(tm, PLE_DIM), _rows(tm, D_MODEL),
         _rows(tm, D_MODEL), _fixed((1, D_MODEL))),
        (SDS((t, D_MODEL), F32), SDS((t, D_MODEL), BF16), SDS((t, PLE_DIM), BF16),
         SDS((t, D_MODEL), BF16), SDS((t, D_MODEL), BF16), SDS((1, D_MODEL), F32)),
        (x3, p, tgt, wpg, wpp), sem=("arbitrary",), bg=bg)


def _s5_discretise(lre, lim, log_dt, bre, bim):
    dt = jnp.exp(log_dt)[:, None]
    mag = jnp.exp(lre * dt)
    abr = mag * jnp.cos(lim * dt)
    abi = mag * jnp.sin(lim * dt)
    nr = abr - 1.0
    ni = abi
    den = lre * lre + lim * lim
    cr = ((nr * lre + ni * lim) / den)[..., None]
    ci = ((ni * lre - nr * lim) / den)[..., None]
    return abr, abi, cr * bre - ci * bim, cr * bim + ci * bre


def _block_diag_in(bb):
    v = bb.reshape(S5_BLOCKS, 8, SSM_STATE, SSM_GROUP_CH).transpose(0, 1, 3, 2)
    return jnp.einsum("bgip,gh->bgihp", v, jnp.eye(8, dtype=bb.dtype)).reshape(
        S5_BLOCKS, S5_BLOCK_IN, S5_BLOCK_ST)


def _block_diag_in_t(dm):
    v = dm.reshape(S5_BLOCKS, 8, SSM_GROUP_CH, 8, SSM_STATE)
    d = jnp.einsum("bgihp,gh->bgip", v, jnp.eye(8, dtype=dm.dtype))
    return d.transpose(0, 1, 3, 2).reshape(SSM_GROUPS, SSM_STATE, SSM_GROUP_CH)


def _block_diag_out(cc):
    v = cc.reshape(S5_BLOCKS, 8, SSM_GROUP_CH, SSM_STATE)
    return jnp.einsum("bgip,gh->bgphi", v, jnp.eye(8, dtype=cc.dtype)).reshape(
        S5_BLOCKS, S5_BLOCK_ST, S5_BLOCK_IN)


def _block_diag_out_t(dn):
    v = dn.reshape(S5_BLOCKS, 8, SSM_STATE, 8, SSM_GROUP_CH)
    d = jnp.einsum("bgphi,gh->bgip", v, jnp.eye(8, dtype=dn.dtype))
    return d.reshape(SSM_GROUPS, SSM_GROUP_CH, SSM_STATE)


def _s5_setup(lre, lim, log_dt, bre, bim, cre, cim, d_skip, glu_w, glu_b, tb):
    seg = tb // 8
    abr, abi, bbr, bbi = _s5_discretise(lre, lim, log_dt, bre, bim)
    pr, pi = abr, abi
    for _ in range(int(math.log2(seg))):
        pr, pi = pr * pr - pi * pi, 2.0 * pr * pi
    rows = jnp.arange(tb)
    src = (rows % 8) * seg + rows // 8
    perm = (src[:, None] == jnp.arange(tb)[None, :]).astype(BF16)
    mre = _block_diag_in(bbr)
    mim = _block_diag_in(bbi)
    nre = _block_diag_out(cre)
    nim = _block_diag_out(cim)
    return {
        "perm": perm, "permt": perm.T,
        "mre": mre.astype(BF16), "mim": mim.astype(BF16),
        "mtre": mre.transpose(0, 2, 1).astype(BF16), "mtim": mim.transpose(0, 2, 1).astype(BF16),
        "nre": nre.astype(BF16), "nim": nim.astype(BF16),
        "ntre": nre.transpose(0, 2, 1).astype(BF16), "ntim": nim.transpose(0, 2, 1).astype(BF16),
        "a": jnp.stack([abr.reshape(-1), abi.reshape(-1)]),
        "ap": jnp.stack([pr.reshape(-1), pi.reshape(-1)]),
        "dskip": d_skip.reshape(1, D_SSM), "glu_w": glu_w, "glu_wt": glu_w.T,
        "glu_b": glu_b.reshape(1, D_SSM),
    }


BIG = ("ffn1_w_in", "ffn1_w_out", "mix_w_in", "ssm_glu_w", "up_a", "up_b", "mix_w_out",
       "ffn2_w_in", "ffn2_w_out", "ple_w_proj", "ple_w_gate")
BIG_AXIS = {"ffn1_w_in": 1, "ffn1_w_out": 0, "mix_w_in": 1, "ssm_glu_w": 0, "up_a": 1, "up_b": 1,
            "mix_w_out": 0, "ffn2_w_in": 1, "ffn2_w_out": 0, "ple_w_proj": 1, "ple_w_gate": 0}
SMALL = ("ln1_g", "ln1_b", "ssm_lambda_re", "ssm_lambda_im", "ssm_log_dt", "ssm_b_re", "ssm_b_im",
         "ssm_c_re", "ssm_c_im", "ssm_d", "ssm_glu_b", "gmlp_ln_g", "gmlp_ln_b", "gmlp_w_s",
         "gmlp_b_s", "ln2_g", "ln2_b", "ln3_g", "ln3_b")
SMALL_2D = {"ln1_g": (1, 1024), "ln1_b": (1, 1024), "ssm_lambda_re": (32, 64), "ssm_lambda_im": (32, 64),
            "ssm_log_dt": (1, 32), "ssm_b_re": (32, 1024), "ssm_b_im": (32, 1024), "ssm_c_re": (32, 1024),
            "ssm_c_im": (32, 1024), "ssm_d": (1, 512), "ssm_glu_b": (1, 512), "gmlp_ln_g": (1, 512),
            "gmlp_ln_b": (1, 512), "gmlp_w_s": (1024, 128), "gmlp_b_s": (8, 128), "ln2_g": (1, 1024),
            "ln2_b": (1, 1024), "ln3_g": (1, 1024), "ln3_b": (1, 1024)}


def _place():
    return lax.axis_index("x"), lax.axis_index("y"), lax.axis_index("c")


def _other_chips(x, y):
    return [(1 - x, y), (x, 1 - y), (1 - x, 1 - y)]


def _window(ref, shard_shape, axis, chip, half):
    r, c = shard_shape
    hr = r // 2
    if axis == 0:
        if half is None:
            return ref.at[pl.ds(chip * r, r), :]
        return ref.at[pl.ds(chip * r + half * hr, hr), :]
    if half is None:
        return ref.at[:, pl.ds(chip * c, c)]
    return ref.at[pl.ds(half * hr, hr), pl.ds(chip * c, c)]


def _gather_weights(shards, axes):
    n = len(shards)
    shapes = [s.shape for s in shards]
    full = [(4 * r, c) if ax == 0 else (r, 4 * c) for (r, c), ax in zip(shapes, axes)]

    def remote(sems, i, k, src, dst, to):
        return pltpu.make_async_remote_copy(src_ref=src, dst_ref=dst, send_sem=sems[0].at[6 * i + k],
                                            recv_sem=sems[1].at[6 * i + k], device_id=to, device_id_type=MESH)

    def own_copies(ins, outs, sems):
        x, y, c = _place()
        me = 2 * x + y
        cps = []
        for i in range(n):
            hr = shapes[i][0] // 2
            mine = ins[i].at[pl.ds(c * hr, hr), :]
            for j, (cx, cy) in enumerate(_other_chips(x, y)):
                cps.append(remote(sems, i, j, mine, _window(outs[i], shapes[i], axes[i], me, c), (cx, cy, c)))
        local = [pltpu.make_async_copy(ins[i], _window(outs[i], shapes[i], axes[i], me, None), sems[2].at[i])
                 for i in range(n)]
        return cps, local

    def start(ins, outs, sems):
        cps, local = own_copies(ins, outs, sems)
        for cp in local + cps:
            cp.start()

    def finish(ins, outs, sems):
        x, y, c = _place()
        sibling = (x, y, 1 - c)
        passed = []
        for j, (cx, cy) in enumerate(_other_chips(x, y)):
            for i in range(n):
                w = _window(outs[i], shapes[i], axes[i], 2 * cx + cy, c)
                remote(sems, i, j, w, w, (cx, cy, c)).wait_recv()
                cp = remote(sems, i, 3 + j, w, w, sibling)
                cp.start()
                passed.append(cp)
        for j, (cx, cy) in enumerate(_other_chips(x, y)):
            for i in range(n):
                w = _window(outs[i], shapes[i], axes[i], 2 * cx + cy, 1 - c)
                remote(sems, i, 3 + j, w, w, sibling).wait_recv()
        cps, local = own_copies(ins, outs, sems)
        for cp in cps + passed:
            cp.wait_send()
        for cp in local:
            cp.wait()

    return _Exchange(shards, [SDS(f, BF16) for f in full],
                     [pltpu.SemaphoreType.DMA((6 * n,)), pltpu.SemaphoreType.DMA((6 * n,)),
                      pltpu.SemaphoreType.DMA((n,))], start, finish)


def _scatter_grads(parts, shapes, axes):
    n = len(parts)

    def copies(ins, outs, sems):
        x, y, c = _place()
        return [pltpu.make_async_remote_copy(
            src_ref=_window(ins[i], shapes[i], axes[i], 2 * cx + cy, None), dst_ref=outs[i].at[j],
            send_sem=sems[0].at[3 * i + j], recv_sem=sems[1].at[3 * i + j],
            device_id=(cx, cy, c), device_id_type=MESH)
            for i in range(n) for j, (cx, cy) in enumerate(_other_chips(x, y))]

    def start(ins, outs, sems):
        for cp in copies(ins, outs, sems):
            cp.start()

    def finish(ins, outs, sems):
        for cp in copies(ins, outs, sems):
            cp.wait()

    return _Exchange(parts, [SDS((3,) + tuple(s), BF16) for s in shapes],
                     [pltpu.SemaphoreType.DMA((3 * n,)), pltpu.SemaphoreType.DMA((3 * n,))], start, finish)


def _swap_halves(parts, shapes, axes):
    n = len(parts)

    def copies(ins, outs, sems):
        x, y, c = _place()
        cps = []
        for i in range(n):
            r, _ = shapes[i]
            hr = r // 2
            if axes[i] == 0:
                cps += [pltpu.make_async_remote_copy(
                    src_ref=ins[i].at[pl.ds(k * r + (1 - c) * hr, hr), :], dst_ref=outs[i].at[k],
                    send_sem=sems[0].at[i], recv_sem=sems[1].at[i], device_id=(x, y, 1 - c),
                    device_id_type=MESH) for k in range(4)]
            else:
                cps.append(pltpu.make_async_remote_copy(
                    src_ref=ins[i].at[pl.ds((1 - c) * hr, hr), :], dst_ref=outs[i],
                    send_sem=sems[0].at[i], recv_sem=sems[1].at[i], device_id=(x, y, 1 - c),
                    device_id_type=MESH))
        return cps

    def start(ins, outs, sems):
        for cp in copies(ins, outs, sems):
            cp.start()

    def finish(ins, outs, sems):
        x, y, c = _place()
        for i in range(n):
            pltpu.make_async_remote_copy(src_ref=outs[i], dst_ref=outs[i], send_sem=sems[0].at[i],
                                         recv_sem=sems[1].at[i], device_id=(x, y, 1 - c),
                                         device_id_type=MESH).wait()

    out = [SDS((4, r // 2, c), BF16) if ax == 0 else SDS((r // 2, 4 * c), BF16)
           for (r, c), ax in zip(shapes, axes)]
    return _Exchange(parts, out, [pltpu.SemaphoreType.DMA((n,)), pltpu.SemaphoreType.DMA((n,))], start, finish)


def _scatter_halves(pres, shapes):
    n = len(pres)

    def copies(ins, outs, sems):
        x, y, c = _place()
        return [pltpu.make_async_remote_copy(
            src_ref=ins[i].at[1 + j], dst_ref=outs[i].at[j], send_sem=sems[0].at[3 * i + j],
            recv_sem=sems[1].at[3 * i + j], device_id=(cx, cy, c), device_id_type=MESH)
            for i in range(n) for j, (cx, cy) in enumerate(_other_chips(x, y))]

    def start(ins, outs, sems):
        for cp in copies(ins, outs, sems):
            cp.start()

    def finish(ins, outs, sems):
        for cp in copies(ins, outs, sems):
            cp.wait()

    return _Exchange(pres, [SDS((3, r // 2, c), BF16) for r, c in shapes],
                     [pltpu.SemaphoreType.DMA((3 * n,)), pltpu.SemaphoreType.DMA((3 * n,))], start, finish)


def _swap_with_sibling(arrs):
    n = len(arrs)

    def copies(ins, outs, sems):
        x, y, c = _place()
        return [pltpu.make_async_remote_copy(src_ref=ins[i], dst_ref=outs[i], send_sem=sems[0].at[i],
                                             recv_sem=sems[1].at[i], device_id=(x, y, 1 - c),
                                             device_id_type=MESH) for i in range(n)]

    def start(ins, outs, sems):
        for cp in copies(ins, outs, sems):
            cp.start()

    def finish(ins, outs, sems):
        for cp in copies(ins, outs, sems):
            cp.wait()

    return _Exchange(arrs, [SDS(a.shape, a.dtype) for a in arrs],
                     [pltpu.SemaphoreType.DMA((n,)), pltpu.SemaphoreType.DMA((n,))], start, finish)


def _gather_small(arrs):
    n = len(arrs)

    def copy(sems, outs, i, k, block, to, src=None):
        px, py, pc = block
        dst = outs[i].at[4 * px + 2 * py + pc]
        return pltpu.make_async_remote_copy(
            src_ref=dst if src is None else src, dst_ref=dst, send_sem=sems[0].at[7 * i + k],
            recv_sem=sems[1].at[7 * i + k], device_id=to, device_id_type=MESH)

    def own_copies(ins, outs, sems):
        x, y, c = _place()
        cps = []
        for i in range(n):
            cps.append(copy(sems, outs, i, 0, (x, y, c), (x, y, 1 - c), src=ins[i]))
            for j, (cx, cy) in enumerate(_other_chips(x, y)):
                cps.append(copy(sems, outs, i, 1 + j, (x, y, c), (cx, cy, c), src=ins[i]))
        local = [pltpu.make_async_copy(ins[i], outs[i].at[4 * x + 2 * y + c], sems[2].at[i]) for i in range(n)]
        return cps, local

    def start(ins, outs, sems):
        cps, local = own_copies(ins, outs, sems)
        for cp in local + cps:
            cp.start()

    def finish(ins, outs, sems):
        x, y, c = _place()
        passed = []
        for j, (cx, cy) in enumerate(_other_chips(x, y)):
            for i in range(n):
                copy(sems, outs, i, 1 + j, (cx, cy, c), (x, y, c)).wait_recv()
                cp = copy(sems, outs, i, 4 + j, (cx, cy, c), (x, y, 1 - c))
                cp.start()
                passed.append(cp)
        for i in range(n):
            copy(sems, outs, i, 0, (x, y, 1 - c), (x, y, c)).wait_recv()
            for j, (cx, cy) in enumerate(_other_chips(x, y)):
                copy(sems, outs, i, 4 + j, (cx, cy, 1 - c), (x, y, c)).wait_recv()
        cps, local = own_copies(ins, outs, sems)
        for cp in cps + passed:
            cp.wait_send()
        for cp in local:
            cp.wait()

    return _Exchange(arrs, [SDS((N_DEV,) + a.shape, F32) for a in arrs],
                     [pltpu.SemaphoreType.DMA((7 * n,)), pltpu.SemaphoreType.DMA((7 * n,)),
                      pltpu.SemaphoreType.DMA((n,))], start, finish)


def _local_step(x, p, tgt, wb, ws, shards=None):
    bsz, seq, _ = x.shape
    t = bsz * seq
    tm = min(256, t)
    tb = min(256, seq)
    x0 = x.reshape(t, D_MODEL)
    p0 = p.reshape(t, PLE_DIM)
    tg = tgt.reshape(t, D_MODEL)
    row = lambda v: v.reshape(1, -1)
    dist = shards is not None
    wb = dict(wb)
    recv, sums, other, gathered = {}, {}, {}, {}
    gb = {}
    gs = {}
    shape_of, axis_of = {}, {}
    chip = None
    if dist:
        shape_of = {k: tuple(shards[k].shape) for k in BIG}
        axis_of = dict(BIG_AXIS)
        for q in range(LAST_PIECES):
            shape_of[LAST_PIECE % q] = (D_MODEL // LAST_PIECES, shape_of["ffn1_w_in"][1])
            axis_of[LAST_PIECE % q] = 1
        xi, yi, ci = _place()
        chip = (2 * xi + yi).astype(jnp.int32).reshape(1)
        ids = jnp.stack([2 * xi + yi] + [2 * cx + cy for cx, cy in _other_chips(xi, yi)] + [ci]).astype(jnp.int32)
    halfbuf, pre = {}, {}

    def gather(names):
        return _gather_weights([shards[k] for k in names], [BIG_AXIS[k] for k in names]) if dist else None

    def exchange(scat=(), swap=(), halves=(), scat2=(), swap2=(), extra=None):
        if not dist:
            return None, []
        parts, tags = [], []
        if scat:
            parts.append(_scatter_grads([gb[k][1] for k in scat], [shape_of[k] for k in scat],
                                        [axis_of[k] for k in scat]))
            tags.append((recv, scat))
        if swap:
            for k in swap:
                sums[k] = _sum_blocks(gb[k][0], recv[k], shape_of[k], axis_of[k], chip, "sum_" + k)
            parts.append(_swap_with_sibling([sums[k] for k in swap]))
            tags.append((other, swap))
        if halves:
            parts.append(_swap_halves([gb[k][1] for k in halves], [shape_of[k] for k in halves],
                                      [axis_of[k] for k in halves]))
            tags.append((halfbuf, halves))
        if scat2:
            for k in scat2:
                pre[k] = _presum(gb[k][0], halfbuf[k], shape_of[k], axis_of[k], ids, "presum_" + k)
            parts.append(_scatter_halves([pre[k][1] for k in scat2], [shape_of[k] for k in scat2]))
            tags.append((recv, scat2))
        if swap2:
            for k in swap2:
                sums[k] = _sum_half(pre[k][0], recv[k], "sum_" + k)
            parts.append(_swap_with_sibling([sums[k] for k in swap2]))
            tags.append((other, swap2))
        if extra is not None:
            parts.append(extra[0])
            tags.append((extra[1], extra[2]))
        return _join(parts), tags

    def take(ex_tags, got):
        ex, tags = ex_tags
        if ex is not None:
            for (dst, names), (o0, o1) in zip(tags, ex.cuts):
                dst.update(zip(names, got[o0:o1]))

    tril = jnp.tril(jnp.ones((CHUNK, CHUNK), dtype=bool))
    wsm = jnp.where(tril[None], ws["gmlp_w_s"], 0.0)
    wsm_b = wsm.astype(BF16)
    wsmt_b = wsm.transpose(0, 2, 1).astype(BF16)
    bias = jnp.repeat(ws["gmlp_b_s"].T, GMLP_HEAD_DIM, axis=1)

    tf = min(512, t)
    if dist:
        names = ("ffn1_w_in",)
        wb.update(zip(names, _run_exchange(gather(names), "gather_ffn1_in")))
    names = ("ffn1_w_out", "mix_w_in")
    (x0b, h1, a1), got = _ffn_proj(x0, wb["ffn1_w_in"], tf, "ffn1_proj", gather(names))
    wb.update(zip(names, got))
    names = ("ssm_glu_w", "up_a", "up_b", "mix_w_out")
    (x1, xh1, rstd1), got = _ffn_out(x0, a1, wb["ffn1_w_out"], row(ws["ln1_g"]), row(ws["ln1_b"]), tf,
                                     "ffn1_out", gather(names))
    wb.update(zip(names, got))
    sp = _s5_setup(ws["ssm_lambda_re"], ws["ssm_lambda_im"], ws["ssm_log_dt"], ws["ssm_b_re"],
                   ws["ssm_b_im"], ws["ssm_c_re"], ws["ssm_c_im"], ws["ssm_d"], wb["ssm_glu_w"],
                   ws["ssm_glu_b"], tb)
    names = ("ffn2_w_out",)
    (x1b, za, zuv, gab), got = _mixin_fwd(x1, wb["mix_w_in"], tm, gather(names))
    wb.update(zip(names, got))
    names = ("ffn2_w_in",)
    (s5o, y2p, carries), got = _s5_fwd(za, sp, bsz, seq, tb, gather(names))
    wb.update(zip(names, got))
    names = ("ple_w_gate", "ple_w_proj")
    (gm,), got = _gmlp_fwd(zuv, row(ws["gmlp_ln_g"]), row(ws["gmlp_ln_b"]), wsm_b, bias, gather(names))
    wb.update(zip(names, got))
    (x2, xh2, rstd2), _ = _mixout_fwd(x1, s5o, gm, gab, wb["up_a"], wb["up_b"], wb["mix_w_out"],
                                           row(ws["ln2_g"]), row(ws["ln2_b"]), tm)
    (x2b, h2, a2), _ = _ffn_proj(x2, wb["ffn2_w_in"], tf, "ffn2_proj")
    (x3, xh3, rstd3), _ = _ffn_out(x2, a2, wb["ffn2_w_out"], row(ws["ln3_g"]), row(ws["ln3_b"]), tf, "ffn2_out")
    (dx3, x3b, pb, dq, de, loss_rows), _ = _ple_loss(x3, p0, tg, wb["ple_w_gate"], wb["ple_w_proj"], tm)
    gb["ple_w_gate"], _ = _tn_matmul(x3b, dq, "dw_ple_gate", 1024, 1024)
    gb["ple_w_proj"], _ = _tn_matmul(pb, de, "dw_ple_proj", 256, 1024)
    et = exchange(scat=("ple_w_gate", "ple_w_proj"))
    (dx2, dh2, df2, gs["ln3_g"], gs["ln3_b"]), got = _ffn_bwd(
        dx3, xh3, rstd3, h2, wb["ffn2_w_in"], wb["ffn2_w_out"], row(ws["ln3_g"]), tm, "ffn2_bwd", et[0])
    take(et, got)
    gb["ffn2_w_out"], _ = _tn_matmul(a2, df2, "dw_ffn2_out", 1408, 1024)
    et = exchange(scat=("ffn2_w_out",))
    gb["ffn2_w_in"], got = _tn_matmul(x2b, dh2, "dw_ffn2_in", 1024, 1408, bg=et[0])
    take(et, got)
    et = exchange(swap=("ple_w_gate", "ple_w_proj", "ffn2_w_out"))
    (dx1a, dmx, mb, dya, dyb, ds5, dgm, dgab, gs["ln2_g"], gs["ln2_b"]), got = _mixout_bwd(
        dx2, xh2, rstd2, s5o, gm, gab, wb["up_a"], wb["up_b"], wb["mix_w_out"], row(ws["ln2_g"]), tm, et[0])
    take(et, got)
    gb["mix_w_out"], _ = _tn_matmul(mb, dmx, "dw_mix_out", 1024, 1024)
    gb["up_a"], _ = _tn_matmul(s5o, dya, "dw_up_a", 512, 1024)
    gb["up_b"], _ = _tn_matmul(gm, dyb, "dw_up_b", 512, 1024)
    et = exchange(scat=("ffn2_w_in",))
    (dza, dmr, dmi, dnr, dni, da, ddsk, dgw, dgb), got = _s5_bwd(za, y2p, ds5, carries, sp, bsz, seq, tb, et[0])
    take(et, got)
    gb["ssm_glu_w"] = (dgw, dgw.astype(BF16))
    et = exchange(scat=("mix_w_out", "up_a"), swap=("ffn2_w_in",))
    (dzuv, dws, dbias, gs["gmlp_ln_g"], gs["gmlp_ln_b"]), got = _gmlp_bwd(
        zuv, dgm, row(ws["gmlp_ln_g"]), row(ws["gmlp_ln_b"]), wsm_b, wsmt_b, bias, et[0])
    take(et, got)
    et = exchange(scat=("up_b", "ssm_glu_w"))
    (dx1,), got = _mixin_bwd(dx1a, dza, dzuv, dgab, wb["mix_w_in"], tm, et[0])
    take(et, got)
    g_mi, _ = _tn_matmul(x1b, dza, "dw_mix_in_a", 1024, 512, 0, 3584)
    g_mi, _ = _tn_matmul(x1b, dzuv, "dw_mix_in_uv", 1024, 512, 1, 3584, g_mi)
    et = exchange(swap=("mix_w_out", "up_a", "up_b", "ssm_glu_w"))
    gb["mix_w_in"], got = _tn_matmul(x1b, dgab, "dw_mix_in_g", 1024, 512, 3, 3584, g_mi, bg=et[0])
    take(et, got)

    d_abr = da[0].sum(axis=0).reshape(SSM_GROUPS, SSM_STATE)
    d_abi = da[1].sum(axis=0).reshape(SSM_GROUPS, SSM_STATE)
    _, vjp = jax.vjp(_s5_discretise, ws["ssm_lambda_re"], ws["ssm_lambda_im"], ws["ssm_log_dt"],
                     ws["ssm_b_re"], ws["ssm_b_im"])
    (gs["ssm_lambda_re"], gs["ssm_lambda_im"], gs["ssm_log_dt"], gs["ssm_b_re"], gs["ssm_b_im"]) = vjp(
        (d_abr, d_abi, _block_diag_in_t(dmr), _block_diag_in_t(dmi)))
    gs["ssm_c_re"] = _block_diag_out_t(dnr)
    gs["ssm_c_im"] = _block_diag_out_t(dni)
    gs["ssm_d"] = ddsk
    gs["ssm_glu_b"] = dgb
    gs["gmlp_w_s"] = dws
    gs["gmlp_b_s"] = dbias.reshape(CHUNK, GMLP_HEADS, GMLP_HEAD_DIM).sum(axis=-1).T
    gs["loss_rows"] = loss_rows
    shape2d = dict(SMALL_2D, loss_rows=(1, D_MODEL))

    def small_gather(names):
        return (_gather_small([gs[k].reshape(shape2d[k]) for k in names]), gathered, names) if dist else None

    late = ("ln1_g", "ln1_b")
    et = exchange(scat=("mix_w_in",), extra=small_gather(tuple(k for k in SMALL + ("loss_rows",) if k not in late)))
    (dx0, dh1, df1, gs["ln1_g"], gs["ln1_b"]), got = _ffn_bwd(
        dx1, xh1, rstd1, h1, wb["ffn1_w_in"], wb["ffn1_w_out"], row(ws["ln1_g"]), tm, "ffn1_bwd", et[0])
    take(et, got)
    grad_x = dx0.reshape(bsz, seq, D_MODEL)
    if not dist:
        gb["ffn1_w_out"], _ = _tn_matmul(a1, df1, "dw_ffn1_out", 1408, 1024)
        gb["ffn1_w_in"], _ = _tn_matmul(x0b, dh1, "dw_ffn1_in", 1024, 1408)
        return loss_rows, grad_x, gb, {k: gs[k].reshape(SMALL_2D[k]) for k in SMALL}, sums, other, gathered, None
    et = exchange(swap=("mix_w_in",), extra=small_gather(late))
    gb["ffn1_w_out"], got = _tn_matmul(a1, df1, "dw_ffn1_out", 1408, 1024, bg=et[0])
    take(et, got)
    last = ["ffn1_w_out"] + [LAST_PIECE % q for q in range(LAST_PIECES)]
    for i in range(1, len(last) + 3):
        stage = lambda d: tuple(last[i - d:i - d + 1]) if 0 <= i - d < len(last) else ()
        et = exchange(halves=stage(1), scat2=stage(2), swap2=stage(3))
        if i < len(last):
            gb[last[i]], got = _tn_matmul(x0b, dh1, "dw_" + last[i], D_MODEL // LAST_PIECES, 1408, bg=et[0],
                                          a_cols=(i - 1, 1))
        else:
            got = _run_exchange(et[0], "reduce_last_%d" % (i - len(last)))
        take(et, got)
    return loss_rows, grad_x, gb, gs, sums, other, gathered, ids


def _adamw(w, g, m, v):
    m = ADAM_B1 * m + (1.0 - ADAM_B1) * g
    v = ADAM_B2 * v + (1.0 - ADAM_B2) * (g * g)
    m_hat = m / ADAM_C1
    v_hat = v / ADAM_C2
    delta = -ADAM_LR * (m_hat / (jnp.sqrt(v_hat) + ADAM_EPS) + ADAM_WD * w)
    return delta, m, v


def _sum_blocks(part, recv, shape, axis, chip, name):
    r, c = shape
    rb = r // 8

    def body(chip_ref, p_ref, r_ref, o_ref):
        o_ref[...] = (p_ref[...] + r_ref[0].astype(F32) + r_ref[1].astype(F32) + r_ref[2].astype(F32))

    if axis == 0:
        own = pl.BlockSpec((rb, c), lambda i, k: (k[0] * 8 + i, 0))
    else:
        own = pl.BlockSpec((rb, c), lambda i, k: (i, k[0]))
    grid_spec = pltpu.PrefetchScalarGridSpec(
        num_scalar_prefetch=1, grid=(8,),
        in_specs=[own, pl.BlockSpec((3, rb, c), lambda i, k: (0, i, 0))],
        out_specs=pl.BlockSpec((rb, c), lambda i, k: (i, 0)))
    return pl.pallas_call(body, name=name, out_shape=SDS((r, c), F32), grid_spec=grid_spec,
                          compiler_params=_params(("parallel",)))(chip, part, recv)


def _presum(part, half, shape, axis, ids, name):
    r, c = shape
    rb = r // 4

    def body(ids_ref, p_ref, h_ref, of_ref, ob_ref):
        s = p_ref[...] + h_ref[...].astype(F32)
        ob_ref[...] = s.astype(BF16)

        @pl.when(pl.program_id(1) == 0)
        def _():
            of_ref[...] = s

    if axis == 0:
        p_spec = pl.BlockSpec((rb, c), lambda i, t, ids: (ids[t] * 4 + ids[4] * 2 + i, 0))
        h_spec = pl.BlockSpec((None, rb, c), lambda i, t, ids: (ids[t], i, 0))
    else:
        p_spec = pl.BlockSpec((rb, c), lambda i, t, ids: (ids[4] * 2 + i, ids[t]))
        h_spec = pl.BlockSpec((rb, c), lambda i, t, ids: (i, ids[t]))
    grid_spec = pltpu.PrefetchScalarGridSpec(
        num_scalar_prefetch=1, grid=(2, 4), in_specs=[p_spec, h_spec],
        out_specs=(pl.BlockSpec((rb, c), lambda i, t, ids: (i, 0)),
                   pl.BlockSpec((None, rb, c), lambda i, t, ids: (t, i, 0))))
    return pl.pallas_call(body, name=name, out_shape=(SDS((r // 2, c), F32), SDS((4, r // 2, c), BF16)),
                          grid_spec=grid_spec, compiler_params=_params(("parallel", "arbitrary")))(ids, part, half)


def _sum_half(pre, recv, name):
    hr, c = pre.shape
    rb = hr // 2

    def body(p_ref, r_ref, o_ref):
        o_ref[...] = (p_ref[...] + r_ref[0].astype(F32) + r_ref[1].astype(F32) + r_ref[2].astype(F32))

    spec = pl.BlockSpec((rb, c), lambda i: (i, 0))
    return pl.pallas_call(body, name=name, grid=(2,), out_shape=SDS((hr, c), F32),
                          in_specs=[spec, pl.BlockSpec((3, rb, c), lambda i: (0, i, 0))], out_specs=spec,
                          compiler_params=_params(("parallel",)))(pre, recv)


def _adam_halves(w, mine, oth, m, v, ids, name, piece=0, prev=None):
    r, c = w.shape
    rb = mine.shape[0] // 2

    def body(ids_ref, w_ref, a_ref, b_ref, m_ref, v_ref, *rest):
        g_ref, d_ref, nm_ref, nv_ref = rest[-4:]
        g = jnp.where(pl.program_id(0) // 2 == ids_ref[4], a_ref[...], b_ref[...])
        g_ref[...] = g
        d_ref[...], nm_ref[...], nv_ref[...] = _adamw(w_ref[...], g, m_ref[...], v_ref[...])

    whole = pl.BlockSpec((rb, c), lambda i, ids: (i + 4 * piece, 0))
    part = pl.BlockSpec((rb, c), lambda i, ids: (i % 2, 0))
    in_specs = [whole, part, part, whole, whole]
    args = [w, mine, oth, m, v]
    aliases = {}
    if prev is not None:
        in_specs += [pl.BlockSpec(memory_space=pl.ANY)] * 4
        args += list(prev)
        aliases = {6: 0, 7: 1, 8: 2, 9: 3}
    grid_spec = pltpu.PrefetchScalarGridSpec(num_scalar_prefetch=1, grid=(4,), in_specs=in_specs,
                                             out_specs=(whole,) * 4)
    return pl.pallas_call(body, name=name, out_shape=tuple(SDS((r, c), F32) for _ in range(4)),
                          grid_spec=grid_spec, input_output_aliases=aliases,
                          compiler_params=_params(("parallel",)))(ids, *args)


def _adam_big(w, ga, gb, m, v, name, piece=0, prev=None):
    r, c = w.shape
    pr = ga.shape[0]
    steps = 8 if pr == r else 2
    rb = pr // steps
    off = piece * steps

    def body(w_ref, ga_ref, gb_ref, m_ref, v_ref, *rest):
        g_ref, d_ref, nm_ref, nv_ref = rest[-4:]
        g = ga_ref[...] + gb_ref[...]
        g_ref[...] = g
        d_ref[...], nm_ref[...], nv_ref[...] = _adamw(w_ref[...], g, m_ref[...], v_ref[...])

    whole = pl.BlockSpec((rb, c), lambda i: (i + off, 0))
    part = pl.BlockSpec((rb, c), lambda i: (i, 0))
    in_specs = [whole, part, part, whole, whole]
    args = [w, ga, gb, m, v]
    aliases = {}
    if prev is not None:
        in_specs += [pl.BlockSpec(memory_space=pl.ANY)] * 4
        args += list(prev)
        aliases = {5: 0, 6: 1, 7: 2, 8: 3}
    return pl.pallas_call(
        body, name=name, grid=(steps,), out_shape=tuple(SDS((r, c), F32) for _ in range(4)),
        in_specs=in_specs, out_specs=(whole,) * 4, input_output_aliases=aliases,
        compiler_params=_params(("parallel",)),
    )(*args)


def _adam_small(ws, gathered, ms, vs):
    n = len(ws)

    def body(*refs):
        w_refs, g_refs, m_refs, v_refs = refs[:n], refs[n:2 * n], refs[2 * n:3 * n], refs[3 * n:4 * n]
        outs = refs[4 * n:]
        for i in range(n):
            g = g_refs[i][0]
            for d in range(1, N_DEV):
                g = g + g_refs[i][d]
            delta, nm, nv = _adamw(w_refs[i][...], g, m_refs[i][...], v_refs[i][...])
            outs[i][...] = g
            outs[n + i][...] = delta
            outs[2 * n + i][...] = nm
            outs[3 * n + i][...] = nv

    vmem = pl.BlockSpec(memory_space=pltpu.VMEM)
    shapes = [w.shape for w in ws]
    return pl.pallas_call(
        body, name="adam_small", out_shape=tuple(SDS(s, F32) for s in shapes * 4),
        in_specs=[vmem] * (4 * n), out_specs=tuple([vmem] * (4 * n)),
        compiler_params=pltpu.CompilerParams(vmem_limit_bytes=VMEM_LIMIT_BYTES),
    )(*ws, *gathered, *ms, *vs)


def _sum_loss(gathered):
    def body(g_ref, o_ref):
        tot = g_ref[0]
        for d in range(1, N_DEV):
            tot = tot + g_ref[d]
        o_ref[...] = (0.5 / D_MODEL) * jnp.sum(tot, axis=1, keepdims=True)

    vmem = pl.BlockSpec(memory_space=pltpu.VMEM)
    return pl.pallas_call(body, name="sum_loss", out_shape=SDS((1, 1), F32), in_specs=[vmem],
                          out_specs=vmem)(gathered)


def kernel(x, p, ffn1_w_in, ffn1_w_out, ln1_g, ln1_b, mix_w_in, ssm_lambda_re, ssm_lambda_im, ssm_log_dt, ssm_b_re, ssm_b_im, ssm_c_re, ssm_c_im, ssm_d, ssm_glu_w, ssm_glu_b, gmlp_ln_g, gmlp_ln_b, gmlp_w_s, gmlp_b_s, up_a, up_b, mix_w_out, ln2_g, ln2_b, ffn2_w_in, ffn2_w_out, ln3_g, ln3_b, ple_w_proj, ple_w_gate, loss_target, m_ffn1_w_in, m_ffn1_w_out, m_ln1_g, m_ln1_b, m_mix_w_in, m_ssm_lambda_re, m_ssm_lambda_im, m_ssm_log_dt, m_ssm_b_re, m_ssm_b_im, m_ssm_c_re, m_ssm_c_im, m_ssm_d, m_ssm_glu_w, m_ssm_glu_b, m_gmlp_ln_g, m_gmlp_ln_b, m_gmlp_w_s, m_gmlp_b_s, m_up_a, m_up_b, m_mix_w_out, m_ln2_g, m_ln2_b, m_ffn2_w_in, m_ffn2_w_out, m_ln3_g, m_ln3_b, m_ple_w_proj, m_ple_w_gate, v_ffn1_w_in, v_ffn1_w_out, v_ln1_g, v_ln1_b, v_mix_w_in, v_ssm_lambda_re, v_ssm_lambda_im, v_ssm_log_dt, v_ssm_b_re, v_ssm_b_im, v_ssm_c_re, v_ssm_c_im, v_ssm_d, v_ssm_glu_w, v_ssm_glu_b, v_gmlp_ln_g, v_gmlp_ln_b, v_gmlp_w_s, v_gmlp_b_s, v_up_a, v_up_b, v_mix_w_out, v_ln2_g, v_ln2_b, v_ffn2_w_in, v_ffn2_w_out, v_ln3_g, v_ln3_b, v_ple_w_proj, v_ple_w_gate):
    given = dict(locals())
    order = ("ffn1_w_in", "ffn1_w_out", "ln1_g", "ln1_b", "mix_w_in", "ssm_lambda_re", "ssm_lambda_im",
             "ssm_log_dt", "ssm_b_re", "ssm_b_im", "ssm_c_re", "ssm_c_im", "ssm_d", "ssm_glu_w", "ssm_glu_b",
             "gmlp_ln_g", "gmlp_ln_b", "gmlp_w_s", "gmlp_b_s", "up_a", "up_b", "mix_w_out", "ln2_g", "ln2_b",
             "ffn2_w_in", "ffn2_w_out", "ln3_g", "ln3_b", "ple_w_proj", "ple_w_gate")
    assert set(order) == set(BIG + SMALL)

    shard = {k: given[k][0] for k in BIG}
    shard_b = {k: shard[k].astype(BF16) for k in BIG}
    ws = {k: given[k][0] for k in SMALL}
    loss_rows, grad_x, gb, gs, sums, other, gathered, ids = _local_step(
        x, given["p"][0], loss_target, {}, ws, shard_b)

    out = {}
    for k in BIG:
        moments = (given["m_" + k][0], given["v_" + k][0])
        if k == "ffn1_w_out":
            out[k] = _adam_halves(shard[k], sums[k], other[k], *moments, ids, "adam_" + k)
        elif k == "ffn1_w_in":
            for q in range(LAST_PIECES):
                kq = LAST_PIECE % q
                out[k] = _adam_halves(shard[k], sums[kq], other[kq], *moments, ids, "adam_" + kq, q, out.get(k))
        else:
            out[k] = _adam_big(shard[k], sums[k], other[k], *moments, "adam_" + k)

    res = _adam_small([given[k].reshape(SMALL_2D[k]) for k in SMALL], [gathered[k] for k in SMALL],
                      [given["m_" + k].reshape(SMALL_2D[k]) for k in SMALL],
                      [given["v_" + k].reshape(SMALL_2D[k]) for k in SMALL])
    ns = len(SMALL)
    for i, k in enumerate(SMALL):
        out[k] = tuple(res[j * ns + i].reshape(given[k].shape) for j in range(4))
    loss = _sum_loss(gathered["loss_rows"]).reshape(())

    lead = lambda k, j: out[k][j][None] if k in BIG else out[k][j]
    return (loss, grad_x, *[lead(k, 0) for k in order], *[lead(k, 1) for k in order],
            *[lead(k, 2) for k in order], *[lead(k, 3) for k in order])
```

```python
import math

import jax
import jax.numpy as jnp
from jax import lax
from jax.experimental import pallas as pl
from jax.experimental.pallas import tpu as pltpu

F32 = jnp.float32
BF16 = jnp.bfloat16
MESH = pl.DeviceIdType.MESH
SDS = jax.ShapeDtypeStruct

D_MODEL = 1024
D_FF = 2816
D_SSM = 512
D_GMLP = 512
SSM_GROUPS = 32
SSM_GROUP_CH = 16
SSM_STATE = 64
SSM_LANES = SSM_GROUPS * SSM_STATE
GMLP_HEADS = 8
GMLP_HEAD_DIM = 64
CHUNK = 128
PLE_DIM = 256
LN_EPS = 1e-5
ALPHA = 2.0 ** 0.25

ADAM_LR = 0.001
ADAM_B1 = 0.9
ADAM_B2 = 0.999
ADAM_EPS = 1e-08
ADAM_WD = 0.01
ADAM_STEP = 10
ADAM_C1 = 1.0 - ADAM_B1 ** ADAM_STEP
ADAM_C2 = 1.0 - ADAM_B2 ** ADAM_STEP

N_DEV = 8
VMEM_LIMIT_BYTES = 56 * 1024 * 1024
FFN_COLS = 1408
S5_BLOCKS = 4
S5_BLOCK_IN = D_SSM // S5_BLOCKS
S5_BLOCK_ST = SSM_LANES // S5_BLOCKS
SCAN_LANES = 512
TN_K_BLOCK = 2048
DIRECT_GATHER_BYTES = 16 * 1024
LAST_PIECES = 2
LAST_PIECE = "ffn1_w_in_q%d"
_G0 = math.sqrt(2.0 / math.pi)
_G1 = 0.044715


def _dot(a, b):
    return jnp.dot(a, b, preferred_element_type=F32)


def _dot_nt(a, b):
    return lax.dot_general(a, b, (((1,), (1,)), ((), ())), preferred_element_type=F32)


def _dot_tn(a, b):
    return lax.dot_general(a, b, (((0,), (0,)), ((), ())), preferred_element_type=F32)


def _sigmoid(x):
    return 1.0 / (1.0 + jnp.exp(-x))


def _gelu(x):
    t = jnp.tanh(_G0 * (x + _G1 * x * x * x))
    return 0.5 * x * (1.0 + t)


def _gelu_grad(x):
    t = jnp.tanh(_G0 * (x + _G1 * x * x * x))
    return 0.5 * (1.0 + t) + 0.5 * x * (1.0 - t * t) * _G0 * (1.0 + 3.0 * _G1 * x * x)


def _ln_fwd(r, g, b):
    mu = jnp.mean(r, axis=-1, keepdims=True)
    d = r - mu
    var = jnp.mean(d * d, axis=-1, keepdims=True)
    rstd = lax.rsqrt(var + LN_EPS)
    xh = d * rstd
    return xh * g + b, xh, rstd


def _ln_bwd(dy, xh, rstd, g):
    dxh = dy * g
    m1 = jnp.mean(dxh, axis=-1, keepdims=True)
    m2 = jnp.mean(dxh * xh, axis=-1, keepdims=True)
    return rstd * (dxh - m1 - xh * m2)


def _resident(shape):
    nd = len(shape)
    return pl.BlockSpec(shape, lambda *_: (0,) * nd, pipeline_mode=pl.Buffered(1))


def _fixed(shape):
    nd = len(shape)
    return pl.BlockSpec(shape, lambda *_: (0,) * nd)


def _rows(tm, cols):
    return pl.BlockSpec((tm, cols), lambda i: (i, 0))


def _params(sem):
    return pltpu.CompilerParams(dimension_semantics=sem, vmem_limit_bytes=VMEM_LIMIT_BYTES)


class _Exchange:
    def __init__(self, args, out_shape, sems, start, finish):
        self.args, self.out_shape, self.sems = list(args), list(out_shape), list(sems)
        self.start, self.finish = start, finish
        self.cuts = [(0, len(self.out_shape))]


def _call(body, name, grid, in_specs, out_specs, out_shape, args, scratch=(), sem=None, bg=None, aliases=None):
    aliases = {} if aliases is None else aliases
    if bg is None:
        res = pl.pallas_call(body, name=name, grid=grid, out_shape=tuple(out_shape), in_specs=list(in_specs),
                             out_specs=tuple(out_specs), scratch_shapes=list(scratch),
                             input_output_aliases=aliases, compiler_params=_params(sem))(*args)
        return tuple(res), ()
    n_in, n_out, n_bi, n_bo, n_sc = len(args), len(out_shape), len(bg.args), len(bg.out_shape), len(scratch)

    def wrapped(*refs):
        ins = refs[:n_in]
        b_ins = refs[n_in:n_in + n_bi]
        outs = refs[n_in + n_bi:n_in + n_bi + n_out]
        b_outs = refs[n_in + n_bi + n_out:n_in + n_bi + n_out + n_bo]
        rest = refs[n_in + n_bi + n_out + n_bo:]
        scr, b_sems = rest[:n_sc], rest[n_sc:]
        first = pl.program_id(0) == 0
        last = pl.program_id(0) == grid[0] - 1
        for ax in range(1, len(grid)):
            first = jnp.logical_and(first, pl.program_id(ax) == 0)
            last = jnp.logical_and(last, pl.program_id(ax) == grid[ax] - 1)

        @pl.when(first)
        def _():
            bg.start(b_ins, b_outs, b_sems)

        body(*ins, *outs, *scr)

        @pl.when(last)
        def _():
            bg.finish(b_ins, b_outs, b_sems)

    any_spec = pl.BlockSpec(memory_space=pl.ANY)
    res = pl.pallas_call(
        wrapped, name=name, grid=grid, out_shape=tuple(out_shape) + tuple(bg.out_shape),
        in_specs=list(in_specs) + [any_spec] * n_bi, out_specs=tuple(out_specs) + (any_spec,) * n_bo,
        scratch_shapes=list(scratch) + list(bg.sems), input_output_aliases=aliases,
        compiler_params=_params(tuple("arbitrary" for _ in grid)))(*args, *bg.args)
    return tuple(res[:n_out]), tuple(res[n_out:])


def _run_exchange(ex, name):
    n_i, n_o = len(ex.args), len(ex.out_shape)

    def body(*refs):
        ins, outs, sems = refs[:n_i], refs[n_i:n_i + n_o], refs[n_i + n_o:]
        ex.start(ins, outs, sems)
        ex.finish(ins, outs, sems)

    any_spec = pl.BlockSpec(memory_space=pl.ANY)
    return tuple(pl.pallas_call(body, name=name, out_shape=tuple(ex.out_shape), in_specs=[any_spec] * n_i,
                                out_specs=(any_spec,) * n_o, scratch_shapes=list(ex.sems))(*ex.args))


def _join(exchanges):
    cuts = []
    a = o = q = 0
    for e in exchanges:
        cuts.append((a, a + len(e.args), o, o + len(e.out_shape), q, q + len(e.sems)))
        a, o, q = cuts[-1][1], cuts[-1][3], cuts[-1][5]

    def start(ins, outs, sems):
        for e, (a0, a1, o0, o1, q0, q1) in zip(exchanges, cuts):
            e.start(ins[a0:a1], outs[o0:o1], sems[q0:q1])

    def finish(ins, outs, sems):
        for e, (a0, a1, o0, o1, q0, q1) in zip(exchanges, cuts):
            e.finish(ins[a0:a1], outs[o0:o1], sems[q0:q1])

    joined = _Exchange(sum((e.args for e in exchanges), []), sum((e.out_shape for e in exchanges), []),
                       sum((e.sems for e in exchanges), []), start, finish)
    joined.cuts = [(c[2], c[3]) for c in cuts]
    return joined


def _ffn_proj(x, w_in, tm, name, bg=None):
    t = x.shape[0]
    nch = D_FF // FFN_COLS

    def body(x_ref, win_ref, xb_ref, h_ref, a_ref):
        xb = x_ref[...].astype(BF16)
        xb_ref[...] = xb
        for k in range(nch):
            cg = slice(k * FFN_COLS, (k + 1) * FFN_COLS)
            cu = slice(D_FF + k * FFN_COLS, D_FF + (k + 1) * FFN_COLS)
            hg = _dot(xb, win_ref[:, cg])
            hu = _dot(xb, win_ref[:, cu])
            h_ref[:, cg] = hg.astype(BF16)
            h_ref[:, cu] = hu.astype(BF16)
            a_ref[:, cg] = (hg * _sigmoid(hg) * hu).astype(BF16)

    return _call(
        body, name, (t // tm,),
        [_rows(tm, D_MODEL), _resident((D_MODEL, 2 * D_FF))],
        (_rows(tm, D_MODEL), _rows(tm, 2 * D_FF), _rows(tm, D_FF)),
        (SDS((t, D_MODEL), BF16), SDS((t, 2 * D_FF), BF16), SDS((t, D_FF), BF16)),
        (x, w_in), sem=("parallel",), bg=bg)


def _ffn_out(x, a, w_out, g, b, tm, name, bg=None):
    t = x.shape[0]

    def body(x_ref, a_ref, wout_ref, g_ref, b_ref, xn_ref, xh_ref, rstd_ref):
        f = _dot(a_ref[...], wout_ref[...])
        y, xh, rstd = _ln_fwd(ALPHA * x_ref[...] + 0.5 * f, g_ref[...], b_ref[...])
        xn_ref[...] = y
        xh_ref[...] = xh
        rstd_ref[...] = rstd

    return _call(
        body, name, (t // tm,),
        [_rows(tm, D_MODEL), _rows(tm, D_FF), _resident((D_FF, D_MODEL)), _fixed((1, D_MODEL)), _fixed((1, D_MODEL))],
        (_rows(tm, D_MODEL), _rows(tm, D_MODEL), _rows(tm, 1)),
        (SDS((t, D_MODEL), F32), SDS((t, D_MODEL), F32), SDS((t, 1), F32)),
        (x, a, w_out, g, b), sem=("parallel",), bg=bg)


def _ffn_bwd(dxn, xh, rstd, h, w_in, w_out, g, tm, name, bg=None):
    t = dxn.shape[0]
    nch = D_FF // FFN_COLS

    def body(dxn_ref, xh_ref, rstd_ref, h_ref, win_ref, wout_ref, g_ref,
             dx_ref, dh_ref, df_ref, dg_ref, db_ref):
        @pl.when(pl.program_id(0) == 0)
        def _():
            dg_ref[...] = jnp.zeros_like(dg_ref)
            db_ref[...] = jnp.zeros_like(db_ref)

        dy = dxn_ref[...]
        xhv = xh_ref[...]
        dr = _ln_bwd(dy, xhv, rstd_ref[...], g_ref[...])
        dg_ref[...] += jnp.sum(dy * xhv, axis=0, keepdims=True)
        db_ref[...] += jnp.sum(dy, axis=0, keepdims=True)
        df = (0.5 * dr).astype(BF16)
        df_ref[...] = df
        dx = ALPHA * dr
        for k in range(nch):
            cg = slice(k * FFN_COLS, (k + 1) * FFN_COLS)
            cu = slice(D_FF + k * FFN_COLS, D_FF + (k + 1) * FFN_COLS)
            hg = h_ref[:, cg].astype(F32)
            hu = h_ref[:, cu].astype(F32)
            sg = _sigmoid(hg)
            silu = hg * sg
            da = _dot_nt(df, wout_ref[cg, :])
            dhu = (da * silu).astype(BF16)
            dhg = (da * hu * (sg * (1.0 + hg * (1.0 - sg)))).astype(BF16)
            dh_ref[:, cg] = dhg
            dh_ref[:, cu] = dhu
            dx = dx + _dot_nt(dhg, win_ref[:, cg]) + _dot_nt(dhu, win_ref[:, cu])
        dx_ref[...] = dx

    return _call(
        body, name, (t // tm,),
        [_rows(tm, D_MODEL), _rows(tm, D_MODEL), _rows(tm, 1), _rows(tm, 2 * D_FF),
         _resident((D_MODEL, 2 * D_FF)), _resident((D_FF, D_MODEL)), _fixed((1, D_MODEL))],
        (_rows(tm, D_MODEL), _rows(tm, 2 * D_FF), _rows(tm, D_MODEL),
         _fixed((1, D_MODEL)), _fixed((1, D_MODEL))),
        (SDS((t, D_MODEL), F32), SDS((t, 2 * D_FF), BF16), SDS((t, D_MODEL), BF16),
         SDS((1, D_MODEL), F32), SDS((1, D_MODEL), F32)),
        (dxn, xh, rstd, h, w_in, w_out, g), sem=("arbitrary",), bg=bg)


def _tn_matmul(a, b, name, bm, bn, col_block=0, total_cols=None, prev=None, bg=None, a_cols=None):
    t, m = a.shape
    a_first = 0
    if a_cols is not None:
        a_first, m = a_cols[0], a_cols[1] * bm
    n = b.shape[1]
    total_cols = n if total_cols is None else total_cols
    bk = min(TN_K_BLOCK, t)
    nk = t // bk
    n_in = 2 if prev is None else 4

    def body(*refs):
        a_ref, b_ref = refs[0], refs[1]
        o_ref, ob_ref = refs[n_in], refs[n_in + 1]
        k = pl.program_id(2)

        @pl.when(k == 0)
        def _():
            o_ref[...] = jnp.zeros_like(o_ref)

        o_ref[...] += _dot_tn(a_ref[...], b_ref[...])

        @pl.when(k == nk - 1)
        def _():
            ob_ref[...] = o_ref[...].astype(BF16)

    in_specs = [pl.BlockSpec((bk, bm), lambda i, j, k: (k, i + a_first)),
                pl.BlockSpec((bk, bn), lambda i, j, k: (k, j))]
    args = [a, b]
    aliases = {}
    if prev is not None:
        in_specs += [pl.BlockSpec(memory_space=pl.ANY), pl.BlockSpec(memory_space=pl.ANY)]
        args += list(prev)
        aliases = {2: 0, 3: 1}
    out_spec = pl.BlockSpec((bm, bn), lambda i, j, k: (i, j + col_block))
    return _call(body, name, (m // bm, n // bn, nk), in_specs, (out_spec, out_spec),
                 (SDS((m, total_cols), F32), SDS((m, total_cols), BF16)), args,
                 sem=("parallel", "parallel", "arbitrary"), bg=bg, aliases=aliases)


def _mixin_fwd(x1, w, tm, bg=None):
    t = x1.shape[0]

    def body(x_ref, w_ref, xb_ref, za_ref, zuv_ref, gab_ref):
        xb = x_ref[...].astype(BF16)
        xb_ref[...] = xb
        za_ref[...] = _dot(xb, w_ref[:, 0:512]).astype(BF16)
        zuv_ref[...] = _dot(xb, w_ref[:, 512:1536]).astype(BF16)
        gab_ref[...] = _dot(xb, w_ref[:, 1536:3584]).astype(BF16)

    return _call(
        body, "mixin_fwd", (t // tm,),
        [_rows(tm, D_MODEL), _resident((D_MODEL, 3584))],
        (_rows(tm, D_MODEL), _rows(tm, 512), _rows(tm, 1024), _rows(tm, 2048)),
        (SDS((t, D_MODEL), BF16), SDS((t, 512), BF16), SDS((t, 1024), BF16), SDS((t, 2048), BF16)),
        (x1, w), sem=("parallel",), bg=bg)


def _mixin_bwd(dx1a, dza, dzuv, dgab, w, tm, bg=None):
    t = dx1a.shape[0]

    def body(d_ref, dza_ref, dzuv_ref, dgab_ref, w_ref, dx_ref):
        dx_ref[...] = (d_ref[...] + _dot_nt(dza_ref[...], w_ref[:, 0:512])
                       + _dot_nt(dzuv_ref[...], w_ref[:, 512:1536])
                       + _dot_nt(dgab_ref[...], w_ref[:, 1536:3584]))

    return _call(
        body, "mixin_bwd", (t // tm,),
        [_rows(tm, D_MODEL), _rows(tm, 512), _rows(tm, 1024), _rows(tm, 2048), _resident((D_MODEL, 3584))],
        (_rows(tm, D_MODEL),), (SDS((t, D_MODEL), F32),),
        (dx1a, dza, dzuv, dgab, w), sem=("parallel",), bg=bg)


def _unrolled(lo, hi, body, carry):
    for j in range(lo, hi):
        carry = body(j, carry)
    return carry


def _scan_fwd(hr_ref, hi_ref, a_ref, ap_ref, carry_ref, seg, cin_ref):
    for lc in range(SSM_LANES // SCAN_LANES):
        ls = slice(lc * SCAN_LANES, (lc + 1) * SCAN_LANES)
        a_r = jnp.broadcast_to(a_ref[0:1, ls], (8, SCAN_LANES))
        a_i = jnp.broadcast_to(a_ref[1:2, ls], (8, SCAN_LANES))

        def step(j, hc, ls=ls, a_r=a_r, a_i=a_i):
            h_r, h_i = hc
            rows = pl.ds(j * 8, 8)
            n_r = a_r * h_r - a_i * h_i + hr_ref[rows, ls]
            n_i = a_r * h_i + a_i * h_r + hi_ref[rows, ls]
            hr_ref[rows, ls] = n_r
            hi_ref[rows, ls] = n_i
            return n_r, n_i

        zero = jnp.zeros((8, SCAN_LANES), F32)
        f_r, f_i = _unrolled(0, seg, step, (zero, zero))
        c_r = carry_ref[0:1, ls]
        c_i = carry_ref[1:2, ls]
        p_r = ap_ref[0:1, ls]
        p_i = ap_ref[1:2, ls]
        rows_r, rows_i = [], []
        for s in range(8):
            rows_r.append(c_r)
            rows_i.append(c_i)
            c_r, c_i = (f_r[s:s + 1] + p_r * c_r - p_i * c_i,
                        f_i[s:s + 1] + p_r * c_i + p_i * c_r)
        carry_ref[0:1, ls] = c_r
        carry_ref[1:2, ls] = c_i
        cin_r = jnp.concatenate(rows_r, axis=0)
        cin_i = jnp.concatenate(rows_i, axis=0)
        if cin_ref is not None:
            cin_ref[0, :, ls] = cin_r
            cin_ref[1, :, ls] = cin_i

        def fix(j, cc, ls=ls, a_r=a_r, a_i=a_i):
            c_r, c_i = cc
            c_r, c_i = a_r * c_r - a_i * c_i, a_r * c_i + a_i * c_r
            rows = pl.ds(j * 8, 8)
            hr_ref[rows, ls] = hr_ref[rows, ls] + c_r
            hi_ref[rows, ls] = hi_ref[rows, ls] + c_i
            return c_r, c_i

        _unrolled(0, seg, fix, (cin_r, cin_i))


def _scan_bwd(gr_ref, gi_ref, hr_ref, hi_ref, cin_ref, a_ref, ap_ref, rcarry_ref, da_ref, seg):
    for lc in range(SSM_LANES // SCAN_LANES):
        ls = slice(lc * SCAN_LANES, (lc + 1) * SCAN_LANES)
        a_r = jnp.broadcast_to(a_ref[0:1, ls], (8, SCAN_LANES))
        a_i = jnp.broadcast_to(a_ref[1:2, ls], (8, SCAN_LANES))

        def step(t, gc, ls=ls, a_r=a_r, a_i=a_i):
            g_r, g_i = gc
            rows = pl.ds((seg - 1 - t) * 8, 8)
            n_r = gr_ref[rows, ls] + a_r * g_r + a_i * g_i
            n_i = gi_ref[rows, ls] + a_r * g_i - a_i * g_r
            gr_ref[rows, ls] = n_r
            gi_ref[rows, ls] = n_i
            return n_r, n_i

        zero = jnp.zeros((8, SCAN_LANES), F32)
        f_r, f_i = _unrolled(0, seg, step, (zero, zero))
        c_r = rcarry_ref[0:1, ls]
        c_i = rcarry_ref[1:2, ls]
        p_r = ap_ref[0:1, ls]
        p_i = ap_ref[1:2, ls]
        rows_r, rows_i = [None] * 8, [None] * 8
        for s in range(7, -1, -1):
            rows_r[s] = c_r
            rows_i[s] = c_i
            c_r, c_i = (f_r[s:s + 1] + p_r * c_r + p_i * c_i,
                        f_i[s:s + 1] + p_r * c_i - p_i * c_r)
        rcarry_ref[0:1, ls] = c_r
        rcarry_ref[1:2, ls] = c_i
        cin_r = jnp.concatenate(rows_r, axis=0)
        cin_i = jnp.concatenate(rows_i, axis=0)

        def fix_row(j_rows, hp_r, hp_i, cc, ls=ls, a_r=a_r, a_i=a_i):
            c_r, c_i, acc_r, acc_i = cc
            c_r, c_i = a_r * c_r + a_i * c_i, a_r * c_i - a_i * c_r
            g_r = gr_ref[j_rows, ls] + c_r
            g_i = gi_ref[j_rows, ls] + c_i
            gr_ref[j_rows, ls] = g_r
            gi_ref[j_rows, ls] = g_i
            acc_r = acc_r + g_r * hp_r + g_i * hp_i
            acc_i = acc_i + g_i * hp_r - g_r * hp_i
            return c_r, c_i, acc_r, acc_i

        def fix(t, cc, ls=ls, fix_row=fix_row):
            j = seg - 1 - t
            rows = pl.ds(j * 8, 8)
            prev = pl.ds((j - 1) * 8, 8)
            return fix_row(rows, hr_ref[prev, ls], hi_ref[prev, ls], cc)

        cc = _unrolled(0, seg - 1, fix, (cin_r, cin_i, zero, zero))
        _, _, acc_r, acc_i = fix_row(pl.ds(0, 8), cin_ref[0, :, ls], cin_ref[1, :, ls], cc)
        da_ref[0, :, ls] += acc_r
        da_ref[1, :, ls] += acc_i


def _s5_fwd(za, sp, bsz, seq, tb, bg=None):
    nb = seq // tb
    seg = tb // 8
    t = bsz * seq

    def body(za_ref, perm_ref, permt_ref, mre_ref, mim_ref, nre_ref, nim_ref, a_ref, ap_ref,
             dsk_ref, gw_ref, gb_ref, out_ref, y2_ref, car_ref, hr_ref, hi_ref, carry_ref):
        @pl.when(pl.program_id(1) == 0)
        def _():
            carry_ref[...] = jnp.zeros_like(carry_ref)

        car_ref[0] = carry_ref[...]
        up = _dot(perm_ref[...], za_ref[...])
        upb = up.astype(BF16)
        for bb in range(S5_BLOCKS):
            ub = upb[:, bb * S5_BLOCK_IN:(bb + 1) * S5_BLOCK_IN]
            st = slice(bb * S5_BLOCK_ST, (bb + 1) * S5_BLOCK_ST)
            hr_ref[:, st] = _dot(ub, mre_ref[bb])
            hi_ref[:, st] = _dot(ub, mim_ref[bb])
        _scan_fwd(hr_ref, hi_ref, a_ref, ap_ref, carry_ref, seg, None)
        ys = []
        for bb in range(S5_BLOCKS):
            st = slice(bb * S5_BLOCK_ST, (bb + 1) * S5_BLOCK_ST)
            ys.append(_dot(hr_ref[:, st].astype(BF16), nre_ref[bb])
                      - _dot(hi_ref[:, st].astype(BF16), nim_ref[bb]))
        y2 = jnp.concatenate(ys, axis=1) + dsk_ref[...] * up
        y2_ref[...] = y2
        y3 = _gelu(y2)
        gl = _dot(y3.astype(BF16), gw_ref[...]) + gb_ref[...]
        oa = y3 * _sigmoid(gl)
        out_ref[...] = _dot(permt_ref[...], oa.astype(BF16)).astype(BF16)

    blk = pl.BlockSpec((tb, D_SSM), lambda b, j: (b * nb + j, 0))
    m_shape = (S5_BLOCKS, S5_BLOCK_IN, S5_BLOCK_ST)
    n_shape = (S5_BLOCKS, S5_BLOCK_ST, S5_BLOCK_IN)
    return _call(
        body, "s5_fwd", (bsz, nb),
        [blk, _fixed((tb, tb)), _fixed((tb, tb)), _fixed(m_shape), _fixed(m_shape), _fixed(n_shape),
         _fixed(n_shape), _fixed((2, SSM_LANES)), _fixed((2, SSM_LANES)), _fixed((1, D_SSM)),
         _fixed((D_SSM, D_SSM)), _fixed((1, D_SSM))],
        (blk, blk, pl.BlockSpec((1, 2, SSM_LANES), lambda b, j: (b * nb + j, 0, 0))),
        (SDS((t, D_SSM), BF16), SDS((t, D_SSM), F32), SDS((bsz * nb, 2, SSM_LANES), F32)),
        (za, sp["perm"], sp["permt"], sp["mre"], sp["mim"], sp["nre"], sp["nim"], sp["a"], sp["ap"],
         sp["dskip"], sp["glu_w"], sp["glu_b"]),
        scratch=[pltpu.VMEM((tb, SSM_LANES), F32), pltpu.VMEM((tb, SSM_LANES), F32),
                 pltpu.VMEM((2, SSM_LANES), F32)],
        sem=("arbitrary", "arbitrary"), bg=bg)


def _s5_bwd(za, y2p, doa, carries, sp, bsz, seq, tb, bg=None):
    nb = seq // tb
    seg = tb // 8
    t = bsz * seq

    def body(za_ref, y2_ref, doa_ref, car_ref, perm_ref, permt_ref, mre_ref, mim_ref, mtre_ref, mtim_ref,
             nre_ref, nim_ref, ntre_ref, ntim_ref, a_ref, ap_ref, dsk_ref, gw_ref, gwt_ref, gb_ref,
             dza_ref, dmr_ref, dmi_ref, dnr_ref, dni_ref, da_ref, ddsk_ref, dgw_ref, dgb_ref,
             hr_ref, hi_ref, gr_ref, gi_ref, cin_ref, carry_ref, rcarry_ref):
        first = jnp.logical_and(pl.program_id(0) == 0, pl.program_id(1) == 0)

        @pl.when(first)
        def _():
            for r in (dmr_ref, dmi_ref, dnr_ref, dni_ref, da_ref, ddsk_ref, dgw_ref, dgb_ref):
                r[...] = jnp.zeros_like(r)

        @pl.when(pl.program_id(1) == 0)
        def _():
            rcarry_ref[...] = jnp.zeros_like(rcarry_ref)

        carry_ref[...] = car_ref[0]
        perm = perm_ref[...]
        up = _dot(perm, za_ref[...])
        upb = up.astype(BF16)
        for bb in range(S5_BLOCKS):
            ub = upb[:, bb * S5_BLOCK_IN:(bb + 1) * S5_BLOCK_IN]
            st = slice(bb * S5_BLOCK_ST, (bb + 1) * S5_BLOCK_ST)
            hr_ref[:, st] = _dot(ub, mre_ref[bb])
            hi_ref[:, st] = _dot(ub, mim_ref[bb])
        _scan_fwd(hr_ref, hi_ref, a_ref, ap_ref, carry_ref, seg, cin_ref)

        y2 = y2_ref[...]
        y3 = _gelu(y2)
        y3b = y3.astype(BF16)
        sg = _sigmoid(_dot(y3b, gw_ref[...]) + gb_ref[...])
        d0 = doa_ref[...]
        d_hi = d0.astype(BF16)
        d1 = d0 - d_hi.astype(F32)
        d_mid = d1.astype(BF16)
        d_lo = (d1 - d_mid.astype(F32)).astype(BF16)
        doap = _dot(perm, d_hi) + _dot(perm, d_mid) + _dot(perm, d_lo)
        dgl = doap * y3 * sg * (1.0 - sg)
        dglb = dgl.astype(BF16)
        dy3 = doap * sg + _dot(dglb, gwt_ref[...])
        dgw_ref[...] += _dot_tn(y3b, dglb)
        dgb_ref[...] += jnp.sum(dgl, axis=0, keepdims=True)
        dy2 = dy3 * _gelu_grad(y2)
        ddsk_ref[...] += jnp.sum(dy2 * up, axis=0, keepdims=True)
        dyb = dy2.astype(BF16)
        for bb in range(S5_BLOCKS):
            dyc = dyb[:, bb * S5_BLOCK_IN:(bb + 1) * S5_BLOCK_IN]
            st = slice(bb * S5_BLOCK_ST, (bb + 1) * S5_BLOCK_ST)
            gr_ref[:, st] = _dot(dyc, ntre_ref[bb])
            gi_ref[:, st] = -_dot(dyc, ntim_ref[bb])
            dnr_ref[bb] += _dot_tn(hr_ref[:, st].astype(BF16), dyc)
            dni_ref[bb] += -_dot_tn(hi_ref[:, st].astype(BF16), dyc)
        _scan_bwd(gr_ref, gi_ref, hr_ref, hi_ref, cin_ref, a_ref, ap_ref, rcarry_ref, da_ref, seg)
        dus = []
        for bb in range(S5_BLOCKS):
            st = slice(bb * S5_BLOCK_ST, (bb + 1) * S5_BLOCK_ST)
            grb = gr_ref[:, st].astype(BF16)
            gib = gi_ref[:, st].astype(BF16)
            dus.append(_dot(grb, mtre_ref[bb]) + _dot(gib, mtim_ref[bb]))
            ub = upb[:, bb * S5_BLOCK_IN:(bb + 1) * S5_BLOCK_IN]
            dmr_ref[bb] += _dot_tn(ub, grb)
            dmi_ref[bb] += _dot_tn(ub, gib)
        du = jnp.concatenate(dus, axis=1) + dy2 * dsk_ref[...]
        dza_ref[...] = _dot(permt_ref[...], du.astype(BF16)).astype(BF16)

    def rev(b, j):
        return (b * nb + (nb - 1 - j), 0)

    blk = pl.BlockSpec((tb, D_SSM), rev)
    m_shape = (S5_BLOCKS, S5_BLOCK_IN, S5_BLOCK_ST)
    n_shape = (S5_BLOCKS, S5_BLOCK_ST, S5_BLOCK_IN)
    return _call(
        body, "s5_bwd", (bsz, nb),
        [blk, blk, blk, pl.BlockSpec((1, 2, SSM_LANES), lambda b, j: (b * nb + (nb - 1 - j), 0, 0)),
         _fixed((tb, tb)), _fixed((tb, tb)), _fixed(m_shape), _fixed(m_shape), _fixed(n_shape), _fixed(n_shape),
         _fixed(n_shape), _fixed(n_shape), _fixed(m_shape), _fixed(m_shape),
         _fixed((2, SSM_LANES)), _fixed((2, SSM_LANES)), _fixed((1, D_SSM)),
         _fixed((D_SSM, D_SSM)), _fixed((D_SSM, D_SSM)), _fixed((1, D_SSM))],
        (blk, _fixed(m_shape), _fixed(m_shape), _fixed(n_shape), _fixed(n_shape),
         _fixed((2, 8, SSM_LANES)), _fixed((1, D_SSM)), _fixed((D_SSM, D_SSM)), _fixed((1, D_SSM))),
        (SDS((t, D_SSM), BF16), SDS(m_shape, F32), SDS(m_shape, F32), SDS(n_shape, F32), SDS(n_shape, F32),
         SDS((2, 8, SSM_LANES), F32), SDS((1, D_SSM), F32), SDS((D_SSM, D_SSM), F32), SDS((1, D_SSM), F32)),
        (za, y2p, doa, carries, sp["perm"], sp["permt"], sp["mre"], sp["mim"], sp["mtre"], sp["mtim"],
         sp["nre"], sp["nim"], sp["ntre"], sp["ntim"], sp["a"], sp["ap"], sp["dskip"], sp["glu_w"],
         sp["glu_wt"], sp["glu_b"]),
        scratch=[pltpu.VMEM((tb, SSM_LANES), F32), pltpu.VMEM((tb, SSM_LANES), F32),
                 pltpu.VMEM((tb, SSM_LANES), F32), pltpu.VMEM((tb, SSM_LANES), F32),
                 pltpu.VMEM((2, 8, SSM_LANES), F32), pltpu.VMEM((2, SSM_LANES), F32),
                 pltpu.VMEM((2, SSM_LANES), F32)],
        sem=("arbitrary", "arbitrary"), bg=bg)


def _gmlp_spatial(ws_ref, vb):
    lane = lax.broadcasted_iota(jnp.int32, (CHUNK, 128), 1)
    parts = []
    for j in range(GMLP_HEADS // 2):
        vp = vb[:, 128 * j:128 * (j + 1)]
        parts.append(jnp.where(lane < GMLP_HEAD_DIM, _dot(ws_ref[2 * j], vp), _dot(ws_ref[2 * j + 1], vp)))
    return jnp.concatenate(parts, axis=1)


def _gmlp_fwd(zuv, ln_g, ln_b, wsm, bias, bg=None):
    t = zuv.shape[0]

    def body(z_ref, g_ref, b_ref, ws_ref, bias_ref, out_ref):
        u = _gelu(z_ref[:, 0:D_GMLP].astype(F32))
        v0 = _gelu(z_ref[:, D_GMLP:2 * D_GMLP].astype(F32))
        v, _, _ = _ln_fwd(v0, g_ref[...], b_ref[...])
        s = _gmlp_spatial(ws_ref, v.astype(BF16)) + bias_ref[...]
        out_ref[...] = (u * s).astype(BF16)

    return _call(
        body, "gmlp_fwd", (t // CHUNK,),
        [_rows(CHUNK, 2 * D_GMLP), _fixed((1, D_GMLP)), _fixed((1, D_GMLP)),
         _fixed((GMLP_HEADS, CHUNK, CHUNK)), _fixed((CHUNK, D_GMLP))],
        (_rows(CHUNK, D_GMLP),), (SDS((t, D_GMLP), BF16),),
        (zuv, ln_g, ln_b, wsm, bias), sem=("parallel",), bg=bg)


def _gmlp_bwd(zuv, dgm, ln_g, ln_b, wsm, wsmt, bias, bg=None):
    t = zuv.shape[0]

    def body(z_ref, d_ref, g_ref, b_ref, ws_ref, wst_ref, bias_ref,
             dz_ref, dws_ref, dbias_ref, dg_ref, db_ref):
        @pl.when(pl.program_id(0) == 0)
        def _():
            for r in (dws_ref, dbias_ref, dg_ref, db_ref):
                r[...] = jnp.zeros_like(r)

        zu = z_ref[:, 0:D_GMLP].astype(F32)
        zv = z_ref[:, D_GMLP:2 * D_GMLP].astype(F32)
        u = _gelu(zu)
        v0 = _gelu(zv)
        gam = g_ref[...]
        v, vhat, rstd = _ln_fwd(v0, gam, b_ref[...])
        vb = v.astype(BF16)
        s = _gmlp_spatial(ws_ref, vb) + bias_ref[...]
        d = d_ref[...]
        dz_ref[:, 0:D_GMLP] = (d * s * _gelu_grad(zu)).astype(BF16)
        ds = d * u
        dbias_ref[...] += ds
        dsb = ds.astype(BF16)
        lane = lax.broadcasted_iota(jnp.int32, (CHUNK, 128), 1)
        tril = (lax.broadcasted_iota(jnp.int32, (CHUNK, CHUNK), 0)
                >= lax.broadcasted_iota(jnp.int32, (CHUNK, CHUNK), 1))
        zero_b = jnp.zeros((CHUNK, 128), BF16)
        parts = []
        for j in range(GMLP_HEADS // 2):
            dsp = dsb[:, 128 * j:128 * (j + 1)]
            vp = vb[:, 128 * j:128 * (j + 1)]
            parts.append(jnp.where(lane < GMLP_HEAD_DIM, _dot(wst_ref[2 * j], dsp),
                                   _dot(wst_ref[2 * j + 1], dsp)))
            lo = jnp.where(lane < GMLP_HEAD_DIM, dsp, zero_b)
            hi = jnp.where(lane < GMLP_HEAD_DIM, zero_b, dsp)
            dws_ref[2 * j] += jnp.where(tril, _dot_nt(lo, vp), 0.0)
            dws_ref[2 * j + 1] += jnp.where(tril, _dot_nt(hi, vp), 0.0)
        dv = jnp.concatenate(parts, axis=1)
        dg_ref[...] += jnp.sum(dv * vhat, axis=0, keepdims=True)
        db_ref[...] += jnp.sum(dv, axis=0, keepdims=True)
        dz_ref[:, D_GMLP:2 * D_GMLP] = (_ln_bwd(dv, vhat, rstd, gam) * _gelu_grad(zv)).astype(BF16)

    return _call(
        body, "gmlp_bwd", (t // CHUNK,),
        [_rows(CHUNK, 2 * D_GMLP), _rows(CHUNK, D_GMLP), _fixed((1, D_GMLP)), _fixed((1, D_GMLP)),
         _fixed((GMLP_HEADS, CHUNK, CHUNK)), _fixed((GMLP_HEADS, CHUNK, CHUNK)), _fixed((CHUNK, D_GMLP))],
        (_rows(CHUNK, 2 * D_GMLP), _fixed((GMLP_HEADS, CHUNK, CHUNK)), _fixed((CHUNK, D_GMLP)),
         _fixed((1, D_GMLP)), _fixed((1, D_GMLP))),
        (SDS((t, 2 * D_GMLP), BF16), SDS((GMLP_HEADS, CHUNK, CHUNK), F32), SDS((CHUNK, D_GMLP), F32),
         SDS((1, D_GMLP), F32), SDS((1, D_GMLP), F32)),
        (zuv, dgm, ln_g, ln_b, wsm, wsmt, bias), sem=("arbitrary",), bg=bg)


def _mixout_fwd(x1, s5o, gm, gab, ua, ub, wmo, g, b, tm, bg=None):
    t = x1.shape[0]

    def body(x_ref, s_ref, m_ref, gab_ref, ua_ref, ub_ref, wmo_ref, g_ref, b_ref,
             xn_ref, xh_ref, rstd_ref):
        ya = _dot(s_ref[...], ua_ref[...])
        yb = _dot(m_ref[...], ub_ref[...])
        mix = (_sigmoid(gab_ref[:, 0:D_MODEL].astype(F32)) * ya
               + _sigmoid(gab_ref[:, D_MODEL:2 * D_MODEL].astype(F32)) * yb)
        r = ALPHA * x_ref[...] + _dot(mix.astype(BF16), wmo_ref[...])
        y, xh, rstd = _ln_fwd(r, g_ref[...], b_ref[...])
        xn_ref[...] = y
        xh_ref[...] = xh
        rstd_ref[...] = rstd

    return _call(
        body, "mixout_fwd", (t // tm,),
        [_rows(tm, D_MODEL), _rows(tm, D_SSM), _rows(tm, D_GMLP), _rows(tm, 2 * D_MODEL),
         _resident((D_SSM, D_MODEL)), _resident((D_GMLP, D_MODEL)), _resident((D_MODEL, D_MODEL)),
         _fixed((1, D_MODEL)), _fixed((1, D_MODEL))],
        (_rows(tm, D_MODEL), _rows(tm, D_MODEL), _rows(tm, 1)),
        (SDS((t, D_MODEL), F32), SDS((t, D_MODEL), F32), SDS((t, 1), F32)),
        (x1, s5o, gm, gab, ua, ub, wmo, g, b), sem=("parallel",), bg=bg)


def _mixout_bwd(dx2, xh, rstd, s5o, gm, gab, ua, ub, wmo, g, tm, bg=None):
    t = dx2.shape[0]

    def body(d_ref, xh_ref, rstd_ref, s_ref, m_ref, gab_ref, ua_ref, ub_ref, wmo_ref, g_ref,
             dx1_ref, dmx_ref, mb_ref, dya_ref, dyb_ref, ds5_ref, dgm_ref, dgab_ref, dg_ref, db_ref):
        @pl.when(pl.program_id(0) == 0)
        def _():
            dg_ref[...] = jnp.zeros_like(dg_ref)
            db_ref[...] = jnp.zeros_like(db_ref)

        dy = d_ref[...]
        xhv = xh_ref[...]
        dr = _ln_bwd(dy, xhv, rstd_ref[...], g_ref[...])
        dg_ref[...] += jnp.sum(dy * xhv, axis=0, keepdims=True)
        db_ref[...] += jnp.sum(dy, axis=0, keepdims=True)
        dx1_ref[...] = ALPHA * dr
        drb = dr.astype(BF16)
        dmx_ref[...] = drb
        dm = _dot_nt(drb, wmo_ref[...])
        ya = _dot(s_ref[...], ua_ref[...])
        yb = _dot(m_ref[...], ub_ref[...])
        sa = _sigmoid(gab_ref[:, 0:D_MODEL].astype(F32))
        sb = _sigmoid(gab_ref[:, D_MODEL:2 * D_MODEL].astype(F32))
        mb_ref[...] = (sa * ya + sb * yb).astype(BF16)
        dya = (dm * sa).astype(BF16)
        dyb = (dm * sb).astype(BF16)
        dya_ref[...] = dya
        dyb_ref[...] = dyb
        dgab_ref[:, 0:D_MODEL] = (dm * ya * sa * (1.0 - sa)).astype(BF16)
        dgab_ref[:, D_MODEL:2 * D_MODEL] = (dm * yb * sb * (1.0 - sb)).astype(BF16)
        ds5_ref[...] = _dot_nt(dya, ua_ref[...])
        dgm_ref[...] = _dot_nt(dyb, ub_ref[...])

    return _call(
        body, "mixout_bwd", (t // tm,),
        [_rows(tm, D_MODEL), _rows(tm, D_MODEL), _rows(tm, 1), _rows(tm, D_SSM), _rows(tm, D_GMLP),
         _rows(tm, 2 * D_MODEL), _resident((D_SSM, D_MODEL)), _resident((D_GMLP, D_MODEL)),
         _resident((D_MODEL, D_MODEL)), _fixed((1, D_MODEL))],
        (_rows(tm, D_MODEL), _rows(tm, D_MODEL), _rows(tm, D_MODEL), _rows(tm, D_MODEL),
         _rows(tm, D_MODEL), _rows(tm, D_SSM), _rows(tm, D_GMLP), _rows(tm, 2 * D_MODEL),
         _fixed((1, D_MODEL)), _fixed((1, D_MODEL))),
        (SDS((t, D_MODEL), F32), SDS((t, D_MODEL), BF16), SDS((t, D_MODEL), BF16),
         SDS((t, D_MODEL), BF16), SDS((t, D_MODEL), BF16), SDS((t, D_SSM), F32),
         SDS((t, D_GMLP), F32), SDS((t, 2 * D_MODEL), BF16),
         SDS((1, D_MODEL), F32), SDS((1, D_MODEL), F32)),
        (dx2, xh, rstd, s5o, gm, gab, ua, ub, wmo, g), sem=("arbitrary",), bg=bg)


def _ple_loss(x3, p, tgt, wpg, wpp, tm, bg=None):
    t = x3.shape[0]

    def body(x_ref, p_ref, t_ref, wpg_ref, wpp_ref, dx_ref, xb_ref, pb_ref, dq_ref, de_ref, loss_ref):
        @pl.when(pl.program_id(0) == 0)
        def _():
            loss_ref[...] = jnp.zeros_like(loss_ref)

        x3v = x_ref[...]
        xb = x3v.astype(BF16)
        pb = p_ref[...].astype(BF16)
        xb_ref[...] = xb
        pb_ref[...] = pb
        s = _sigmoid(_dot(xb, wpg_ref[...]))
        e = _dot(pb, wpp_ref[...])
        diff = x3v + s * e - t_ref[...]
        loss_ref[...] += jnp.sum(diff * diff, axis=0, keepdims=True)
        dout = diff * (1.0 / D_MODEL)
        de_ref[...] = (dout * s).astype(BF16)
        dq = (dout * e * s * (1.0 - s)).astype(BF16)
        dq_ref[...] = dq
        dx_ref[...] = dout + _dot_nt(dq, wpg_ref[...])

    return _call(
        body, "ple_loss", (t // tm,),
        [_rows(tm, D_MODEL), _rows(tm, PLE_DIM), _rows(tm, D_MODEL),
         _resident((D_MODEL, D_MODEL)), _resident((PLE_DIM, D_MODEL))],
        (_rows(tm, D_MODEL), _rows(tm, D_MODEL), _rows(tm, PLE_DIM), _rows(tm, D_MODEL),
         _rows(tm, D_MODEL), _fixed((1, D_MODEL))),
        (SDS((t, D_MODEL), F32), SDS((t, D_MODEL), BF16), SDS((t, PLE_DIM), BF16),
         SDS((t, D_MODEL), BF16), SDS((t, D_MODEL), BF16), SDS((1, D_MODEL), F32)),
        (x3, p, tgt, wpg, wpp), sem=("arbitrary",), bg=bg)


def _s5_discretise(lre, lim, log_dt, bre, bim):
    dt = jnp.exp(log_dt)[:, None]
    mag = jnp.exp(lre * dt)
    abr = mag * jnp.cos(lim * dt)
    abi = mag * jnp.sin(lim * dt)
    nr = abr - 1.0
    ni = abi
    den = lre * lre + lim * lim
    cr = ((nr * lre + ni * lim) / den)[..., None]
    ci = ((ni * lre - nr * lim) / den)[..., None]
    return abr, abi, cr * bre - ci * bim, cr * bim + ci * bre


def _block_diag_in(bb):
    v = bb.reshape(S5_BLOCKS, 8, SSM_STATE, SSM_GROUP_CH).transpose(0, 1, 3, 2)
    return jnp.einsum("bgip,gh->bgihp", v, jnp.eye(8, dtype=bb.dtype)).reshape(
        S5_BLOCKS, S5_BLOCK_IN, S5_BLOCK_ST)


def _block_diag_in_t(dm):
    v = dm.reshape(S5_BLOCKS, 8, SSM_GROUP_CH, 8, SSM_STATE)
    d = jnp.einsum("bgihp,gh->bgip", v, jnp.eye(8, dtype=dm.dtype))
    return d.transpose(0, 1, 3, 2).reshape(SSM_GROUPS, SSM_STATE, SSM_GROUP_CH)


def _block_diag_out(cc):
    v = cc.reshape(S5_BLOCKS, 8, SSM_GROUP_CH, SSM_STATE)
    return jnp.einsum("bgip,gh->bgphi", v, jnp.eye(8, dtype=cc.dtype)).reshape(
        S5_BLOCKS, S5_BLOCK_ST, S5_BLOCK_IN)


def _block_diag_out_t(dn):
    v = dn.reshape(S5_BLOCKS, 8, SSM_STATE, 8, SSM_GROUP_CH)
    d = jnp.einsum("bgphi,gh->bgip", v, jnp.eye(8, dtype=dn.dtype))
    return d.reshape(SSM_GROUPS, SSM_GROUP_CH, SSM_STATE)


def _s5_setup(lre, lim, log_dt, bre, bim, cre, cim, d_skip, glu_w, glu_b, tb):
    seg = tb // 8
    abr, abi, bbr, bbi = _s5_discretise(lre, lim, log_dt, bre, bim)
    pr, pi = abr, abi
    for _ in range(int(math.log2(seg))):
        pr, pi = pr * pr - pi * pi, 2.0 * pr * pi
    rows = jnp.arange(tb)
    src = (rows % 8) * seg + rows // 8
    perm = (src[:, None] == jnp.arange(tb)[None, :]).astype(BF16)
    mre = _block_diag_in(bbr)
    mim = _block_diag_in(bbi)
    nre = _block_diag_out(cre)
    nim = _block_diag_out(cim)
    return {
        "perm": perm, "permt": perm.T,
        "mre": mre.astype(BF16), "mim": mim.astype(BF16),
        "mtre": mre.transpose(0, 2, 1).astype(BF16), "mtim": mim.transpose(0, 2, 1).astype(BF16),
        "nre": nre.astype(BF16), "nim": nim.astype(BF16),
        "ntre": nre.transpose(0, 2, 1).astype(BF16), "ntim": nim.transpose(0, 2, 1).astype(BF16),
        "a": jnp.stack([abr.reshape(-1), abi.reshape(-1)]),
        "ap": jnp.stack([pr.reshape(-1), pi.reshape(-1)]),
        "dskip": d_skip.reshape(1, D_SSM), "glu_w": glu_w, "glu_wt": glu_w.T,
        "glu_b": glu_b.reshape(1, D_SSM),
    }


BIG = ("ffn1_w_in", "ffn1_w_out", "mix_w_in", "ssm_glu_w", "up_a", "up_b", "mix_w_out",
       "ffn2_w_in", "ffn2_w_out", "ple_w_proj", "ple_w_gate")
BIG_AXIS = {"ffn1_w_in": 1, "ffn1_w_out": 0, "mix_w_in": 1, "ssm_glu_w": 0, "up_a": 1, "up_b": 1,
            "mix_w_out": 0, "ffn2_w_in": 1, "ffn2_w_out": 0, "ple_w_proj": 1, "ple_w_gate": 0}
SMALL = ("ln1_g", "ln1_b", "ssm_lambda_re", "ssm_lambda_im", "ssm_log_dt", "ssm_b_re", "ssm_b_im",
         "ssm_c_re", "ssm_c_im", "ssm_d", "ssm_glu_b", "gmlp_ln_g", "gmlp_ln_b", "gmlp_w_s",
         "gmlp_b_s", "ln2_g", "ln2_b", "ln3_g", "ln3_b")
SMALL_VIEW = {"ssm_b_re": (SSM_GROUPS, SSM_STATE * SSM_GROUP_CH), "ssm_b_im": (SSM_GROUPS, SSM_STATE * SSM_GROUP_CH)}


def _small_view(k, a):
    return a.reshape(SMALL_VIEW[k]) if k in SMALL_VIEW else a


def _place():
    return lax.axis_index("x"), lax.axis_index("y"), lax.axis_index("c")


def _other_chips(x, y):
    return [(1 - x, y), (x, 1 - y), (1 - x, 1 - y)]


def _window(ref, shard_shape, axis, chip, half):
    r, c = shard_shape
    hr = r // 2
    if axis == 0:
        if half is None:
            return ref.at[pl.ds(chip * r, r), :]
        return ref.at[pl.ds(chip * r + half * hr, hr), :]
    if half is None:
        return ref.at[:, pl.ds(chip * c, c)]
    return ref.at[pl.ds(half * hr, hr), pl.ds(chip * c, c)]


def _gather_weights(shards, axes):
    n = len(shards)
    shapes = [s.shape for s in shards]
    full = [(4 * r, c) if ax == 0 else (r, 4 * c) for (r, c), ax in zip(shapes, axes)]

    def remote(sems, i, k, src, dst, to):
        return pltpu.make_async_remote_copy(src_ref=src, dst_ref=dst, send_sem=sems[0].at[6 * i + k],
                                            recv_sem=sems[1].at[6 * i + k], device_id=to, device_id_type=MESH)

    def own_copies(ins, outs, sems):
        x, y, c = _place()
        me = 2 * x + y
        cps = []
        for i in range(n):
            hr = shapes[i][0] // 2
            mine = ins[i].at[pl.ds(c * hr, hr), :]
            for j, (cx, cy) in enumerate(_other_chips(x, y)):
                cps.append(remote(sems, i, j, mine, _window(outs[i], shapes[i], axes[i], me, c), (cx, cy, c)))
        local = [pltpu.make_async_copy(ins[i], _window(outs[i], shapes[i], axes[i], me, None), sems[2].at[i])
                 for i in range(n)]
        return cps, local

    def start(ins, outs, sems):
        cps, local = own_copies(ins, outs, sems)
        for cp in local + cps:
            cp.start()

    def finish(ins, outs, sems):
        x, y, c = _place()
        sibling = (x, y, 1 - c)
        passed = []
        for j, (cx, cy) in enumerate(_other_chips(x, y)):
            for i in range(n):
                w = _window(outs[i], shapes[i], axes[i], 2 * cx + cy, c)
                remote(sems, i, j, w, w, (cx, cy, c)).wait_recv()
                cp = remote(sems, i, 3 + j, w, w, sibling)
                cp.start()
                passed.append(cp)
        for j, (cx, cy) in enumerate(_other_chips(x, y)):
            for i in range(n):
                w = _window(outs[i], shapes[i], axes[i], 2 * cx + cy, 1 - c)
                remote(sems, i, 3 + j, w, w, sibling).wait_recv()
        cps, local = own_copies(ins, outs, sems)
        for cp in cps + passed:
            cp.wait_send()
        for cp in local:
            cp.wait()

    return _Exchange(shards, [SDS(f, BF16) for f in full],
                     [pltpu.SemaphoreType.DMA((6 * n,)), pltpu.SemaphoreType.DMA((6 * n,)),
                      pltpu.SemaphoreType.DMA((n,))], start, finish)


def _scatter_grads(parts, shapes, axes):
    n = len(parts)

    def copies(ins, outs, sems):
        x, y, c = _place()
        return [pltpu.make_async_remote_copy(
            src_ref=_window(ins[i], shapes[i], axes[i], 2 * cx + cy, None), dst_ref=outs[i].at[j],
            send_sem=sems[0].at[3 * i + j], recv_sem=sems[1].at[3 * i + j],
            device_id=(cx, cy, c), device_id_type=MESH)
            for i in range(n) for j, (cx, cy) in enumerate(_other_chips(x, y))]

    def start(ins, outs, sems):
        for cp in copies(ins, outs, sems):
            cp.start()

    def finish(ins, outs, sems):
        for cp in copies(ins, outs, sems):
            cp.wait()

    return _Exchange(parts, [SDS((3,) + tuple(s), BF16) for s in shapes],
                     [pltpu.SemaphoreType.DMA((3 * n,)), pltpu.SemaphoreType.DMA((3 * n,))], start, finish)


def _swap_halves(parts, shapes, axes):
    n = len(parts)

    def copies(ins, outs, sems):
        x, y, c = _place()
        cps = []
        for i in range(n):
            r, _ = shapes[i]
            hr = r // 2
            if axes[i] == 0:
                cps += [pltpu.make_async_remote_copy(
                    src_ref=ins[i].at[pl.ds(k * r + (1 - c) * hr, hr), :], dst_ref=outs[i].at[k],
                    send_sem=sems[0].at[i], recv_sem=sems[1].at[i], device_id=(x, y, 1 - c),
                    device_id_type=MESH) for k in range(4)]
            else:
                cps.append(pltpu.make_async_remote_copy(
                    src_ref=ins[i].at[pl.ds((1 - c) * hr, hr), :], dst_ref=outs[i],
                    send_sem=sems[0].at[i], recv_sem=sems[1].at[i], device_id=(x, y, 1 - c),
                    device_id_type=MESH))
        return cps

    def start(ins, outs, sems):
        for cp in copies(ins, outs, sems):
            cp.start()

    def finish(ins, outs, sems):
        x, y, c = _place()
        for i in range(n):
            pltpu.make_async_remote_copy(src_ref=outs[i], dst_ref=outs[i], send_sem=sems[0].at[i],
                                         recv_sem=sems[1].at[i], device_id=(x, y, 1 - c),
                                         device_id_type=MESH).wait()

    out = [SDS((4, r // 2, c), BF16) if ax == 0 else SDS((r // 2, 4 * c), BF16)
           for (r, c), ax in zip(shapes, axes)]
    return _Exchange(parts, out, [pltpu.SemaphoreType.DMA((n,)), pltpu.SemaphoreType.DMA((n,))], start, finish)


def _scatter_halves(pres, shapes):
    n = len(pres)

    def copies(ins, outs, sems):
        x, y, c = _place()
        return [pltpu.make_async_remote_copy(
            src_ref=ins[i].at[1 + j], dst_ref=outs[i].at[j], send_sem=sems[0].at[3 * i + j],
            recv_sem=sems[1].at[3 * i + j], device_id=(cx, cy, c), device_id_type=MESH)
            for i in range(n) for j, (cx, cy) in enumerate(_other_chips(x, y))]

    def start(ins, outs, sems):
        for cp in copies(ins, outs, sems):
            cp.start()

    def finish(ins, outs, sems):
        for cp in copies(ins, outs, sems):
            cp.wait()

    return _Exchange(pres, [SDS((3, r // 2, c), BF16) for r, c in shapes],
                     [pltpu.SemaphoreType.DMA((3 * n,)), pltpu.SemaphoreType.DMA((3 * n,))], start, finish)


def _swap_with_sibling(arrs):
    n = len(arrs)

    def copies(ins, outs, sems):
        x, y, c = _place()
        return [pltpu.make_async_remote_copy(src_ref=ins[i], dst_ref=outs[i], send_sem=sems[0].at[i],
                                             recv_sem=sems[1].at[i], device_id=(x, y, 1 - c),
                                             device_id_type=MESH) for i in range(n)]

    def start(ins, outs, sems):
        for cp in copies(ins, outs, sems):
            cp.start()

    def finish(ins, outs, sems):
        for cp in copies(ins, outs, sems):
            cp.wait()

    return _Exchange(arrs, [SDS(a.shape, a.dtype) for a in arrs],
                     [pltpu.SemaphoreType.DMA((n,)), pltpu.SemaphoreType.DMA((n,))], start, finish)


def _gather_small(arrs):
    n = len(arrs)

    def copy(sems, outs, i, k, block, to, src=None):
        px, py, pc = block
        dst = outs[i].at[4 * px + 2 * py + pc]
        return pltpu.make_async_remote_copy(
            src_ref=dst if src is None else src, dst_ref=dst, send_sem=sems[0].at[7 * i + k],
            recv_sem=sems[1].at[7 * i + k], device_id=to, device_id_type=MESH)

    direct = [math.prod(a.shape) * 4 <= DIRECT_GATHER_BYTES for a in arrs]

    def own_copies(ins, outs, sems):
        x, y, c = _place()
        cps = []
        for i in range(n):
            cps.append(copy(sems, outs, i, 0, (x, y, c), (x, y, 1 - c), src=ins[i]))
            for j, (cx, cy) in enumerate(_other_chips(x, y)):
                cps.append(copy(sems, outs, i, 1 + j, (x, y, c), (cx, cy, c), src=ins[i]))
                if direct[i]:
                    cps.append(copy(sems, outs, i, 4 + j, (x, y, c), (cx, cy, 1 - c), src=ins[i]))
        local = [pltpu.make_async_copy(ins[i], outs[i].at[4 * x + 2 * y + c], sems[2].at[i]) for i in range(n)]
        return cps, local

    def start(ins, outs, sems):
        cps, local = own_copies(ins, outs, sems)
        for cp in local + cps:
            cp.start()

    def finish(ins, outs, sems):
        x, y, c = _place()
        passed = []
        for j, (cx, cy) in enumerate(_other_chips(x, y)):
            for i in range(n):
                copy(sems, outs, i, 1 + j, (cx, cy, c), (x, y, c)).wait_recv()
                if not direct[i]:
                    cp = copy(sems, outs, i, 4 + j, (cx, cy, c), (x, y, 1 - c))
                    cp.start()
                    passed.append(cp)
        for i in range(n):
            copy(sems, outs, i, 0, (x, y, 1 - c), (x, y, c)).wait_recv()
            for j, (cx, cy) in enumerate(_other_chips(x, y)):
                copy(sems, outs, i, 4 + j, (cx, cy, 1 - c), (x, y, c)).wait_recv()
        cps, local = own_copies(ins, outs, sems)
        for cp in cps + passed:
            cp.wait_send()
        for cp in local:
            cp.wait()

    return _Exchange(arrs, [SDS((N_DEV,) + a.shape, F32) for a in arrs],
                     [pltpu.SemaphoreType.DMA((7 * n,)), pltpu.SemaphoreType.DMA((7 * n,)),
                      pltpu.SemaphoreType.DMA((n,))], start, finish)


def _local_step(x, p, tgt, wb, ws, shards=None):
    bsz, seq, _ = x.shape
    t = bsz * seq
    tm = min(256, t)
    tb = min(256, seq)
    x0 = x.reshape(t, D_MODEL)
    p0 = p.reshape(t, PLE_DIM)
    tg = tgt.reshape(t, D_MODEL)
    row = lambda v: v.reshape(1, -1)
    dist = shards is not None
    wb = dict(wb)
    recv, sums, other, gathered = {}, {}, {}, {}
    gb = {}
    gs = {}
    shape_of, axis_of = {}, {}
    chip = None
    if dist:
        shape_of = {k: tuple(shards[k].shape) for k in BIG}
        axis_of = dict(BIG_AXIS)
        for q in range(LAST_PIECES):
            shape_of[LAST_PIECE % q] = (D_MODEL // LAST_PIECES, shape_of["ffn1_w_in"][1])
            axis_of[LAST_PIECE % q] = 1
        xi, yi, ci = _place()
        chip = (2 * xi + yi).astype(jnp.int32).reshape(1)
        ids = jnp.stack([2 * xi + yi] + [2 * cx + cy for cx, cy in _other_chips(xi, yi)] + [ci]).astype(jnp.int32)
    halfbuf, pre = {}, {}

    def gather(names):
        return _gather_weights([shards[k] for k in names], [BIG_AXIS[k] for k in names]) if dist else None

    def exchange(scat=(), swap=(), halves=(), scat2=(), swap2=(), extra=None):
        if not dist:
            return None, []
        parts, tags = [], []
        if scat:
            parts.append(_scatter_grads([gb[k][1] for k in scat], [shape_of[k] for k in scat],
                                        [axis_of[k] for k in scat]))
            tags.append((recv, scat))
        if swap:
            for k in swap:
                sums[k] = _sum_blocks(gb[k][0], recv[k], shape_of[k], axis_of[k], chip, "sum_" + k)
            parts.append(_swap_with_sibling([sums[k] for k in swap]))
            tags.append((other, swap))
        if halves:
            parts.append(_swap_halves([gb[k][1] for k in halves], [shape_of[k] for k in halves],
                                      [axis_of[k] for k in halves]))
            tags.append((halfbuf, halves))
        if scat2:
            for k in scat2:
                pre[k] = _presum(gb[k][0], halfbuf[k], shape_of[k], axis_of[k], ids, "presum_" + k)
            parts.append(_scatter_halves([pre[k][1] for k in scat2], [shape_of[k] for k in scat2]))
            tags.append((recv, scat2))
        if swap2:
            for k in swap2:
                sums[k] = _sum_half(pre[k][0], recv[k], "sum_" + k)
            parts.append(_swap_with_sibling([sums[k] for k in swap2]))
            tags.append((other, swap2))
        if extra is not None:
            parts.append(extra[0])
            tags.append((extra[1], extra[2]))
        return _join(parts), tags

    def take(ex_tags, got):
        ex, tags = ex_tags
        if ex is not None:
            for (dst, names), (o0, o1) in zip(tags, ex.cuts):
                dst.update(zip(names, got[o0:o1]))

    small_shape = {k: _small_view(k, v).shape for k, v in ws.items()}
    small_shape["loss_rows"] = (1, D_MODEL)
    ws = {k: v if (v.ndim == 2 and k != "ssm_log_dt") else v[0] for k, v in ws.items()}
    tril = jnp.tril(jnp.ones((CHUNK, CHUNK), dtype=bool))
    wsm = jnp.where(tril[None], ws["gmlp_w_s"], 0.0)
    wsm_b = wsm.astype(BF16)
    wsmt_b = wsm.transpose(0, 2, 1).astype(BF16)
    bias = jnp.repeat(ws["gmlp_b_s"].T, GMLP_HEAD_DIM, axis=1)

    tf = min(512, t)
    if dist:
        names = ("ffn1_w_in",)
        wb.update(zip(names, _run_exchange(gather(names), "gather_ffn1_in")))
    names = ("ffn1_w_out", "mix_w_in")
    (x0b, h1, a1), got = _ffn_proj(x0, wb["ffn1_w_in"], tf, "ffn1_proj", gather(names))
    wb.update(zip(names, got))
    names = ("ssm_glu_w", "up_a", "up_b", "mix_w_out")
    (x1, xh1, rstd1), got = _ffn_out(x0, a1, wb["ffn1_w_out"], row(ws["ln1_g"]), row(ws["ln1_b"]), tf,
                                     "ffn1_out", gather(names))
    wb.update(zip(names, got))
    sp = _s5_setup(ws["ssm_lambda_re"], ws["ssm_lambda_im"], ws["ssm_log_dt"], ws["ssm_b_re"],
                   ws["ssm_b_im"], ws["ssm_c_re"], ws["ssm_c_im"], ws["ssm_d"], wb["ssm_glu_w"],
                   ws["ssm_glu_b"], tb)
    names = ("ffn2_w_out",)
    (x1b, za, zuv, gab), got = _mixin_fwd(x1, wb["mix_w_in"], tm, gather(names))
    wb.update(zip(names, got))
    names = ("ffn2_w_in",)
    (s5o, y2p, carries), got = _s5_fwd(za, sp, bsz, seq, tb, gather(names))
    wb.update(zip(names, got))
    names = ("ple_w_gate", "ple_w_proj")
    (gm,), got = _gmlp_fwd(zuv, row(ws["gmlp_ln_g"]), row(ws["gmlp_ln_b"]), wsm_b, bias, gather(names))
    wb.update(zip(names, got))
    (x2, xh2, rstd2), _ = _mixout_fwd(x1, s5o, gm, gab, wb["up_a"], wb["up_b"], wb["mix_w_out"],
                                           row(ws["ln2_g"]), row(ws["ln2_b"]), tm)
    (x2b, h2, a2), _ = _ffn_proj(x2, wb["ffn2_w_in"], tf, "ffn2_proj")
    (x3, xh3, rstd3), _ = _ffn_out(x2, a2, wb["ffn2_w_out"], row(ws["ln3_g"]), row(ws["ln3_b"]), tf, "ffn2_out")
    (dx3, x3b, pb, dq, de, loss_rows), _ = _ple_loss(x3, p0, tg, wb["ple_w_gate"], wb["ple_w_proj"], tm)
    gb["ple_w_gate"], _ = _tn_matmul(x3b, dq, "dw_ple_gate", 1024, 1024)
    gb["ple_w_proj"], _ = _tn_matmul(pb, de, "dw_ple_proj", 256, 1024)
    et = exchange(scat=("ple_w_gate", "ple_w_proj"))
    (dx2, dh2, df2, gs["ln3_g"], gs["ln3_b"]), got = _ffn_bwd(
        dx3, xh3, rstd3, h2, wb["ffn2_w_in"], wb["ffn2_w_out"], row(ws["ln3_g"]), tm, "ffn2_bwd", et[0])
    take(et, got)
    gb["ffn2_w_out"], _ = _tn_matmul(a2, df2, "dw_ffn2_out", 1408, 1024)
    et = exchange(scat=("ffn2_w_out",))
    gb["ffn2_w_in"], got = _tn_matmul(x2b, dh2, "dw_ffn2_in", 1024, 1408, bg=et[0])
    take(et, got)
    et = exchange(swap=("ple_w_gate", "ple_w_proj", "ffn2_w_out"))
    (dx1a, dmx, mb, dya, dyb, ds5, dgm, dgab, gs["ln2_g"], gs["ln2_b"]), got = _mixout_bwd(
        dx2, xh2, rstd2, s5o, gm, gab, wb["up_a"], wb["up_b"], wb["mix_w_out"], row(ws["ln2_g"]), tm, et[0])
    take(et, got)
    gb["mix_w_out"], _ = _tn_matmul(mb, dmx, "dw_mix_out", 1024, 1024)
    gb["up_a"], _ = _tn_matmul(s5o, dya, "dw_up_a", 512, 1024)
    gb["up_b"], _ = _tn_matmul(gm, dyb, "dw_up_b", 512, 1024)
    et = exchange(scat=("ffn2_w_in",))
    (dza, dmr, dmi, dnr, dni, da, ddsk, dgw, dgb), got = _s5_bwd(za, y2p, ds5, carries, sp, bsz, seq, tb, et[0])
    take(et, got)
    gb["ssm_glu_w"] = (dgw, dgw.astype(BF16))
    et = exchange(scat=("mix_w_out", "up_a"), swap=("ffn2_w_in",))
    (dzuv, dws, dbias, gs["gmlp_ln_g"], gs["gmlp_ln_b"]), got = _gmlp_bwd(
        zuv, dgm, row(ws["gmlp_ln_g"]), row(ws["gmlp_ln_b"]), wsm_b, wsmt_b, bias, et[0])
    take(et, got)
    et = exchange(scat=("up_b", "ssm_glu_w"))
    (dx1,), got = _mixin_bwd(dx1a, dza, dzuv, dgab, wb["mix_w_in"], tm, et[0])
    take(et, got)
    g_mi, _ = _tn_matmul(x1b, dza, "dw_mix_in_a", 1024, 512, 0, 3584)
    g_mi, _ = _tn_matmul(x1b, dzuv, "dw_mix_in_uv", 1024, 512, 1, 3584, g_mi)
    et = exchange(swap=("mix_w_out", "up_a", "up_b", "ssm_glu_w"))
    gb["mix_w_in"], got = _tn_matmul(x1b, dgab, "dw_mix_in_g", 1024, 512, 3, 3584, g_mi, bg=et[0])
    take(et, got)

    d_abr = da[0].sum(axis=0).reshape(SSM_GROUPS, SSM_STATE)
    d_abi = da[1].sum(axis=0).reshape(SSM_GROUPS, SSM_STATE)
    _, vjp = jax.vjp(_s5_discretise, ws["ssm_lambda_re"], ws["ssm_lambda_im"], ws["ssm_log_dt"],
                     ws["ssm_b_re"], ws["ssm_b_im"])
    (gs["ssm_lambda_re"], gs["ssm_lambda_im"], gs["ssm_log_dt"], gs["ssm_b_re"], gs["ssm_b_im"]) = vjp(
        (d_abr, d_abi, _block_diag_in_t(dmr), _block_diag_in_t(dmi)))
    gs["ssm_c_re"] = _block_diag_out_t(dnr)
    gs["ssm_c_im"] = _block_diag_out_t(dni)
    gs["ssm_d"] = ddsk
    gs["ssm_glu_b"] = dgb
    gs["gmlp_w_s"] = dws
    gs["gmlp_b_s"] = dbias.reshape(CHUNK, GMLP_HEADS, GMLP_HEAD_DIM).sum(axis=-1).T
    gs["loss_rows"] = loss_rows

    def small_gather(names):
        return (_gather_small([gs[k].reshape(small_shape[k]) for k in names]), gathered, names) if dist else None

    late = ("ln1_g", "ln1_b")
    et = exchange(scat=("mix_w_in",), extra=small_gather(tuple(k for k in SMALL + ("loss_rows",) if k not in late)))
    (dx0, dh1, df1, gs["ln1_g"], gs["ln1_b"]), got = _ffn_bwd(
        dx1, xh1, rstd1, h1, wb["ffn1_w_in"], wb["ffn1_w_out"], row(ws["ln1_g"]), tm, "ffn1_bwd", et[0])
    take(et, got)
    grad_x = dx0.reshape(bsz, seq, D_MODEL)
    if not dist:
        gb["ffn1_w_out"], _ = _tn_matmul(a1, df1, "dw_ffn1_out", 1408, 1024)
        gb["ffn1_w_in"], _ = _tn_matmul(x0b, dh1, "dw_ffn1_in", 1024, 1408)
        return loss_rows, grad_x, gb, {k: gs[k].reshape(small_shape[k]) for k in SMALL}, sums, other, gathered, None
    et = exchange(swap=("mix_w_in",), extra=small_gather(late))
    gb["ffn1_w_out"], got = _tn_matmul(a1, df1, "dw_ffn1_out", 1408, 1024, bg=et[0])
    take(et, got)
    last = ["ffn1_w_out"] + [LAST_PIECE % q for q in range(LAST_PIECES)]
    for i in range(1, len(last) + 3):
        stage = lambda d: tuple(last[i - d:i - d + 1]) if 0 <= i - d < len(last) else ()
        et = exchange(halves=stage(1), scat2=stage(2), swap2=stage(3))
        if i < len(last):
            gb[last[i]], got = _tn_matmul(x0b, dh1, "dw_" + last[i], D_MODEL // LAST_PIECES, 1408, bg=et[0],
                                          a_cols=(i - 1, 1))
        else:
            got = _run_exchange(et[0], "reduce_last_%d" % (i - len(last)))
        take(et, got)
    return loss_rows, grad_x, gb, gs, sums, other, gathered, ids


def _adamw(w, g, m, v):
    m = ADAM_B1 * m + (1.0 - ADAM_B1) * g
    v = ADAM_B2 * v + (1.0 - ADAM_B2) * (g * g)
    m_hat = m / ADAM_C1
    v_hat = v / ADAM_C2
    delta = -ADAM_LR * (m_hat / (jnp.sqrt(v_hat) + ADAM_EPS) + ADAM_WD * w)
    return delta, m, v


def _sum_blocks(part, recv, shape, axis, chip, name):
    r, c = shape
    rb = r // 8

    def body(chip_ref, p_ref, r_ref, o_ref):
        o_ref[...] = (p_ref[...] + r_ref[0].astype(F32) + r_ref[1].astype(F32) + r_ref[2].astype(F32))

    if axis == 0:
        own = pl.BlockSpec((rb, c), lambda i, k: (k[0] * 8 + i, 0))
    else:
        own = pl.BlockSpec((rb, c), lambda i, k: (i, k[0]))
    grid_spec = pltpu.PrefetchScalarGridSpec(
        num_scalar_prefetch=1, grid=(8,),
        in_specs=[own, pl.BlockSpec((3, rb, c), lambda i, k: (0, i, 0))],
        out_specs=pl.BlockSpec((rb, c), lambda i, k: (i, 0)))
    return pl.pallas_call(body, name=name, out_shape=SDS((r, c), F32), grid_spec=grid_spec,
                          compiler_params=_params(("parallel",)))(chip, part, recv)


def _presum(part, half, shape, axis, ids, name):
    r, c = shape
    rb = r // 4

    def body(ids_ref, p_ref, h_ref, of_ref, ob_ref):
        s = p_ref[...] + h_ref[...].astype(F32)
        ob_ref[...] = s.astype(BF16)

        @pl.when(pl.program_id(1) == 0)
        def _():
            of_ref[...] = s

    if axis == 0:
        p_spec = pl.BlockSpec((rb, c), lambda i, t, ids: (ids[t] * 4 + ids[4] * 2 + i, 0))
        h_spec = pl.BlockSpec((None, rb, c), lambda i, t, ids: (ids[t], i, 0))
    else:
        p_spec = pl.BlockSpec((rb, c), lambda i, t, ids: (ids[4] * 2 + i, ids[t]))
        h_spec = pl.BlockSpec((rb, c), lambda i, t, ids: (i, ids[t]))
    grid_spec = pltpu.PrefetchScalarGridSpec(
        num_scalar_prefetch=1, grid=(2, 4), in_specs=[p_spec, h_spec],
        out_specs=(pl.BlockSpec((rb, c), lambda i, t, ids: (i, 0)),
                   pl.BlockSpec((None, rb, c), lambda i, t, ids: (t, i, 0))))
    return pl.pallas_call(body, name=name, out_shape=(SDS((r // 2, c), F32), SDS((4, r // 2, c), BF16)),
                          grid_spec=grid_spec, compiler_params=_params(("parallel", "arbitrary")))(ids, part, half)


def _sum_half(pre, recv, name):
    hr, c = pre.shape
    rb = hr // 2

    def body(p_ref, r_ref, o_ref):
        o_ref[...] = (p_ref[...] + r_ref[0].astype(F32) + r_ref[1].astype(F32) + r_ref[2].astype(F32))

    spec = pl.BlockSpec((rb, c), lambda i: (i, 0))
    return pl.pallas_call(body, name=name, grid=(2,), out_shape=SDS((hr, c), F32),
                          in_specs=[spec, pl.BlockSpec((3, rb, c), lambda i: (0, i, 0))], out_specs=spec,
                          compiler_params=_params(("parallel",)))(pre, recv)


def _adam_halves(w, mine, oth, m, v, ids, name, piece=0, prev=None):
    r, c = w.shape
    rb = mine.shape[0] // 2

    def body(ids_ref, w_ref, a_ref, b_ref, m_ref, v_ref, *rest):
        g_ref, d_ref, nm_ref, nv_ref = rest[-4:]
        g = jnp.where(pl.program_id(0) // 2 == ids_ref[4], a_ref[...], b_ref[...])
        g_ref[...] = g
        d_ref[...], nm_ref[...], nv_ref[...] = _adamw(w_ref[...], g, m_ref[...], v_ref[...])

    whole = pl.BlockSpec((rb, c), lambda i, ids: (i + 4 * piece, 0))
    part = pl.BlockSpec((rb, c), lambda i, ids: (i % 2, 0))
    in_specs = [whole, part, part, whole, whole]
    args = [w, mine, oth, m, v]
    aliases = {}
    if prev is not None:
        in_specs += [pl.BlockSpec(memory_space=pl.ANY)] * 4
        args += list(prev)
        aliases = {6: 0, 7: 1, 8: 2, 9: 3}
    grid_spec = pltpu.PrefetchScalarGridSpec(num_scalar_prefetch=1, grid=(4,), in_specs=in_specs,
                                             out_specs=(whole,) * 4)
    return pl.pallas_call(body, name=name, out_shape=tuple(SDS((r, c), F32) for _ in range(4)),
                          grid_spec=grid_spec, input_output_aliases=aliases,
                          compiler_params=_params(("parallel",)))(ids, *args)


def _adam_big(w, ga, gb, m, v, name, piece=0, prev=None):
    r, c = w.shape
    pr = ga.shape[0]
    steps = 8 if pr == r else 2
    rb = pr // steps
    off = piece * steps

    def body(w_ref, ga_ref, gb_ref, m_ref, v_ref, *rest):
        g_ref, d_ref, nm_ref, nv_ref = rest[-4:]
        g = ga_ref[...] + gb_ref[...]
        g_ref[...] = g
        d_ref[...], nm_ref[...], nv_ref[...] = _adamw(w_ref[...], g, m_ref[...], v_ref[...])

    whole = pl.BlockSpec((rb, c), lambda i: (i + off, 0))
    part = pl.BlockSpec((rb, c), lambda i: (i, 0))
    in_specs = [whole, part, part, whole, whole]
    args = [w, ga, gb, m, v]
    aliases = {}
    if prev is not None:
        in_specs += [pl.BlockSpec(memory_space=pl.ANY)] * 4
        args += list(prev)
        aliases = {5: 0, 6: 1, 7: 2, 8: 3}
    return pl.pallas_call(
        body, name=name, grid=(steps,), out_shape=tuple(SDS((r, c), F32) for _ in range(4)),
        in_specs=in_specs, out_specs=(whole,) * 4, input_output_aliases=aliases,
        compiler_params=_params(("parallel",)),
    )(*args)


def _adam_small(ws, gathered, ms, vs):
    n = len(ws)

    def body(*refs):
        w_refs, g_refs, m_refs, v_refs = refs[:n], refs[n:2 * n], refs[2 * n:3 * n], refs[3 * n:4 * n]
        outs = refs[4 * n:]
        for i in range(n):
            g = g_refs[i][0]
            for d in range(1, N_DEV):
                g = g + g_refs[i][d]
            delta, nm, nv = _adamw(w_refs[i][...], g, m_refs[i][...], v_refs[i][...])
            outs[i][...] = g
            outs[n + i][...] = delta
            outs[2 * n + i][...] = nm
            outs[3 * n + i][...] = nv

    vmem = pl.BlockSpec(memory_space=pltpu.VMEM)
    shapes = [w.shape for w in ws]
    return pl.pallas_call(
        body, name="adam_small", out_shape=tuple(SDS(s, F32) for s in shapes * 4),
        in_specs=[vmem] * (4 * n), out_specs=tuple([vmem] * (4 * n)),
        compiler_params=pltpu.CompilerParams(vmem_limit_bytes=VMEM_LIMIT_BYTES),
    )(*ws, *gathered, *ms, *vs)


def _sum_loss(gathered):
    def body(g_ref, o_ref):
        tot = g_ref[0]
        for d in range(1, N_DEV):
            tot = tot + g_ref[d]
        o_ref[...] = (0.5 / D_MODEL) * jnp.sum(tot, axis=1, keepdims=True)

    vmem = pl.BlockSpec(memory_space=pltpu.VMEM)
    return pl.pallas_call(body, name="sum_loss", out_shape=SDS((1, 1), F32), in_specs=[vmem],
                          out_specs=vmem)(gathered)


def kernel(x, p, ffn1_w_in, ffn1_w_out, ln1_g, ln1_b, mix_w_in, ssm_lambda_re, ssm_lambda_im, ssm_log_dt, ssm_b_re, ssm_b_im, ssm_c_re, ssm_c_im, ssm_d, ssm_glu_w, ssm_glu_b, gmlp_ln_g, gmlp_ln_b, gmlp_w_s, gmlp_b_s, up_a, up_b, mix_w_out, ln2_g, ln2_b, ffn2_w_in, ffn2_w_out, ln3_g, ln3_b, ple_w_proj, ple_w_gate, loss_target, m_ffn1_w_in, m_ffn1_w_out, m_ln1_g, m_ln1_b, m_mix_w_in, m_ssm_lambda_re, m_ssm_lambda_im, m_ssm_log_dt, m_ssm_b_re, m_ssm_b_im, m_ssm_c_re, m_ssm_c_im, m_ssm_d, m_ssm_glu_w, m_ssm_glu_b, m_gmlp_ln_g, m_gmlp_ln_b, m_gmlp_w_s, m_gmlp_b_s, m_up_a, m_up_b, m_mix_w_out, m_ln2_g, m_ln2_b, m_ffn2_w_in, m_ffn2_w_out, m_ln3_g, m_ln3_b, m_ple_w_proj, m_ple_w_gate, v_ffn1_w_in, v_ffn1_w_out, v_ln1_g, v_ln1_b, v_mix_w_in, v_ssm_lambda_re, v_ssm_lambda_im, v_ssm_log_dt, v_ssm_b_re, v_ssm_b_im, v_ssm_c_re, v_ssm_c_im, v_ssm_d, v_ssm_glu_w, v_ssm_glu_b, v_gmlp_ln_g, v_gmlp_ln_b, v_gmlp_w_s, v_gmlp_b_s, v_up_a, v_up_b, v_mix_w_out, v_ln2_g, v_ln2_b, v_ffn2_w_in, v_ffn2_w_out, v_ln3_g, v_ln3_b, v_ple_w_proj, v_ple_w_gate):
    given = dict(locals())
    order = ("ffn1_w_in", "ffn1_w_out", "ln1_g", "ln1_b", "mix_w_in", "ssm_lambda_re", "ssm_lambda_im",
             "ssm_log_dt", "ssm_b_re", "ssm_b_im", "ssm_c_re", "ssm_c_im", "ssm_d", "ssm_glu_w", "ssm_glu_b",
             "gmlp_ln_g", "gmlp_ln_b", "gmlp_w_s", "gmlp_b_s", "up_a", "up_b", "mix_w_out", "ln2_g", "ln2_b",
             "ffn2_w_in", "ffn2_w_out", "ln3_g", "ln3_b", "ple_w_proj", "ple_w_gate")
    assert set(order) == set(BIG + SMALL)

    shard = {k: given[k][0] for k in BIG}
    shard_b = {k: shard[k].astype(BF16) for k in BIG}
    loss_rows, grad_x, gb, gs, sums, other, gathered, ids = _local_step(
        x, given["p"][0], loss_target, {}, {k: given[k] for k in SMALL}, shard_b)

    out = {}
    for k in BIG:
        moments = (given["m_" + k][0], given["v_" + k][0])
        if k == "ffn1_w_out":
            out[k] = _adam_halves(shard[k], sums[k], other[k], *moments, ids, "adam_" + k)
        elif k == "ffn1_w_in":
            for q in range(LAST_PIECES):
                kq = LAST_PIECE % q
                out[k] = _adam_halves(shard[k], sums[kq], other[kq], *moments, ids, "adam_" + kq, q, out.get(k))
        else:
            out[k] = _adam_big(shard[k], sums[k], other[k], *moments, "adam_" + k)

    res = _adam_small([_small_view(k, given[k]) for k in SMALL], [gathered[k] for k in SMALL],
                      [_small_view(k, given["m_" + k]) for k in SMALL],
                      [_small_view(k, given["v_" + k]) for k in SMALL])
    ns = len(SMALL)
    for i, k in enumerate(SMALL):
        out[k] = tuple(res[j * ns + i].reshape(given[k].shape) for j in range(4))
    loss = _sum_loss(gathered["loss_rows"]).reshape(())

    lead = lambda k, j: out[k][j][None] if k in BIG else out[k][j]
    return (loss, grad_x, *[lead(k, 0) for k in order], *[lead(k, 1) for k in order],
            *[lead(k, 2) for k in order], *[lead(k, 3) for k in order])
```

```python
import math

import jax
import jax.numpy as jnp
from jax import lax
from jax.experimental import pallas as pl
from jax.experimental.pallas import tpu as pltpu

F32 = jnp.float32
BF16 = jnp.bfloat16
MESH = pl.DeviceIdType.MESH
SDS = jax.ShapeDtypeStruct

D_MODEL = 1024
D_FF = 2816
D_SSM = 512
D_GMLP = 512
SSM_GROUPS = 32
SSM_GROUP_CH = 16
SSM_STATE = 64
SSM_LANES = SSM_GROUPS * SSM_STATE
GMLP_HEADS = 8
GMLP_HEAD_DIM = 64
CHUNK = 128
PLE_DIM = 256
LN_EPS = 1e-5
ALPHA = 2.0 ** 0.25

ADAM_LR = 0.001
ADAM_B1 = 0.9
ADAM_B2 = 0.999
ADAM_EPS = 1e-08
ADAM_WD = 0.01
ADAM_STEP = 10
ADAM_C1 = 1.0 - ADAM_B1 ** ADAM_STEP
ADAM_C2 = 1.0 - ADAM_B2 ** ADAM_STEP

N_DEV = 8
VMEM_LIMIT_BYTES = 56 * 1024 * 1024
FFN_COLS = 1408
S5_BLOCKS = 4
S5_BLOCK_IN = D_SSM // S5_BLOCKS
S5_BLOCK_ST = SSM_LANES // S5_BLOCKS
SCAN_LANES = 512
TN_K_BLOCK = 2048
DIRECT_GATHER_BYTES = 16 * 1024
LAST_PIECES = 2
LAST_PIECE = "ffn1_w_in_q%d"
_G0 = math.sqrt(2.0 / math.pi)
_G1 = 0.044715


def _dot(a, b):
    return jnp.dot(a, b, preferred_element_type=F32)


def _dot_nt(a, b):
    return lax.dot_general(a, b, (((1,), (1,)), ((), ())), preferred_element_type=F32)


def _dot_tn(a, b):
    return lax.dot_general(a, b, (((0,), (0,)), ((), ())), preferred_element_type=F32)


def _sigmoid(x):
    return 1.0 / (1.0 + jnp.exp(-x))


def _gelu(x):
    t = jnp.tanh(_G0 * (x + _G1 * x * x * x))
    return 0.5 * x * (1.0 + t)


def _gelu_grad(x):
    t = jnp.tanh(_G0 * (x + _G1 * x * x * x))
    return 0.5 * (1.0 + t) + 0.5 * x * (1.0 - t * t) * _G0 * (1.0 + 3.0 * _G1 * x * x)


def _ln_fwd(r, g, b):
    mu = jnp.mean(r, axis=-1, keepdims=True)
    d = r - mu
    var = jnp.mean(d * d, axis=-1, keepdims=True)
    rstd = lax.rsqrt(var + LN_EPS)
    xh = d * rstd
    return xh * g + b, xh, rstd


def _ln_bwd(dy, xh, rstd, g):
    dxh = dy * g
    m1 = jnp.mean(dxh, axis=-1, keepdims=True)
    m2 = jnp.mean(dxh * xh, axis=-1, keepdims=True)
    return rstd * (dxh - m1 - xh * m2)


def _resident(shape):
    nd = len(shape)
    return pl.BlockSpec(shape, lambda *_: (0,) * nd, pipeline_mode=pl.Buffered(1))


def _fixed(shape):
    nd = len(shape)
    return pl.BlockSpec(shape, lambda *_: (0,) * nd)


def _rows(tm, cols):
    return pl.BlockSpec((tm, cols), lambda i: (i, 0))


def _params(sem):
    return pltpu.CompilerParams(dimension_semantics=sem, vmem_limit_bytes=VMEM_LIMIT_BYTES)


class _Exchange:
    def __init__(self, args, out_shape, sems, start, finish):
        self.args, self.out_shape, self.sems = list(args), list(out_shape), list(sems)
        self.start, self.finish = start, finish
        self.cuts = [(0, len(self.out_shape))]


def _call(body, name, grid, in_specs, out_specs, out_shape, args, scratch=(), sem=None, bg=None, aliases=None):
    aliases = {} if aliases is None else aliases
    if bg is None:
        res = pl.pallas_call(body, name=name, grid=grid, out_shape=tuple(out_shape), in_specs=list(in_specs),
                             out_specs=tuple(out_specs), scratch_shapes=list(scratch),
                             input_output_aliases=aliases, compiler_params=_params(sem))(*args)
        return tuple(res), ()
    n_in, n_out, n_bi, n_bo, n_sc = len(args), len(out_shape), len(bg.args), len(bg.out_shape), len(scratch)

    def wrapped(*refs):
        ins = refs[:n_in]
        b_ins = refs[n_in:n_in + n_bi]
        outs = refs[n_in + n_bi:n_in + n_bi + n_out]
        b_outs = refs[n_in + n_bi + n_out:n_in + n_bi + n_out + n_bo]
        rest = refs[n_in + n_bi + n_out + n_bo:]
        scr, b_sems = rest[:n_sc], rest[n_sc:]
        first = pl.program_id(0) == 0
        last = pl.program_id(0) == grid[0] - 1
        for ax in range(1, len(grid)):
            first = jnp.logical_and(first, pl.program_id(ax) == 0)
            last = jnp.logical_and(last, pl.program_id(ax) == grid[ax] - 1)

        @pl.when(first)
        def _():
            bg.start(b_ins, b_outs, b_sems)

        body(*ins, *outs, *scr)

        @pl.when(last)
        def _():
            bg.finish(b_ins, b_outs, b_sems)

    any_spec = pl.BlockSpec(memory_space=pl.ANY)
    res = pl.pallas_call(
        wrapped, name=name, grid=grid, out_shape=tuple(out_shape) + tuple(bg.out_shape),
        in_specs=list(in_specs) + [any_spec] * n_bi, out_specs=tuple(out_specs) + (any_spec,) * n_bo,
        scratch_shapes=list(scratch) + list(bg.sems), input_output_aliases=aliases,
        compiler_params=_params(tuple("arbitrary" for _ in grid)))(*args, *bg.args)
    return tuple(res[:n_out]), tuple(res[n_out:])


def _run_exchange(ex, name):
    n_i, n_o = len(ex.args), len(ex.out_shape)

    def body(*refs):
        ins, outs, sems = refs[:n_i], refs[n_i:n_i + n_o], refs[n_i + n_o:]
        ex.start(ins, outs, sems)
        ex.finish(ins, outs, sems)

    any_spec = pl.BlockSpec(memory_space=pl.ANY)
    return tuple(pl.pallas_call(body, name=name, out_shape=tuple(ex.out_shape), in_specs=[any_spec] * n_i,
                                out_specs=(any_spec,) * n_o, scratch_shapes=list(ex.sems))(*ex.args))


def _join(exchanges):
    cuts = []
    a = o = q = 0
    for e in exchanges:
        cuts.append((a, a + len(e.args), o, o + len(e.out_shape), q, q + len(e.sems)))
        a, o, q = cuts[-1][1], cuts[-1][3], cuts[-1][5]

    def start(ins, outs, sems):
        for e, (a0, a1, o0, o1, q0, q1) in zip(exchanges, cuts):
            e.start(ins[a0:a1], outs[o0:o1], sems[q0:q1])

    def finish(ins, outs, sems):
        for e, (a0, a1, o0, o1, q0, q1) in zip(exchanges, cuts):
            e.finish(ins[a0:a1], outs[o0:o1], sems[q0:q1])

    joined = _Exchange(sum((e.args for e in exchanges), []), sum((e.out_shape for e in exchanges), []),
                       sum((e.sems for e in exchanges), []), start, finish)
    joined.cuts = [(c[2], c[3]) for c in cuts]
    return joined


def _ffn_proj(x, w_in, tm, name, bg=None):
    t = x.shape[0]
    nch = D_FF // FFN_COLS

    def body(x_ref, win_ref, xb_ref, h_ref, a_ref):
        xb = x_ref[...].astype(BF16)
        xb_ref[...] = xb
        for k in range(nch):
            cg = slice(k * FFN_COLS, (k + 1) * FFN_COLS)
            cu = slice(D_FF + k * FFN_COLS, D_FF + (k + 1) * FFN_COLS)
            hg = _dot(xb, win_ref[:, cg])
            hu = _dot(xb, win_ref[:, cu])
            h_ref[:, cg] = hg.astype(BF16)
            h_ref[:, cu] = hu.astype(BF16)
            a_ref[:, cg] = (hg * _sigmoid(hg) * hu).astype(BF16)

    return _call(
        body, name, (t // tm,),
        [_rows(tm, D_MODEL), _resident((D_MODEL, 2 * D_FF))],
        (_rows(tm, D_MODEL), _rows(tm, 2 * D_FF), _rows(tm, D_FF)),
        (SDS((t, D_MODEL), BF16), SDS((t, 2 * D_FF), BF16), SDS((t, D_FF), BF16)),
        (x, w_in), sem=("parallel",), bg=bg)


def _ffn_out(x, a, w_out, g, b, tm, name, bg=None):
    t = x.shape[0]

    def body(x_ref, a_ref, wout_ref, g_ref, b_ref, xn_ref, xh_ref, rstd_ref):
        f = _dot(a_ref[...], wout_ref[...])
        y, xh, rstd = _ln_fwd(ALPHA * x_ref[...] + 0.5 * f, g_ref[...], b_ref[...])
        xn_ref[...] = y
        xh_ref[...] = xh
        rstd_ref[...] = rstd

    return _call(
        body, name, (t // tm,),
        [_rows(tm, D_MODEL), _rows(tm, D_FF), _resident((D_FF, D_MODEL)), _fixed((1, D_MODEL)), _fixed((1, D_MODEL))],
        (_rows(tm, D_MODEL), _rows(tm, D_MODEL), _rows(tm, 1)),
        (SDS((t, D_MODEL), F32), SDS((t, D_MODEL), F32), SDS((t, 1), F32)),
        (x, a, w_out, g, b), sem=("parallel",), bg=bg)


def _ffn_bwd(dxn, xh, rstd, h, w_in, w_out, g, tm, name, bg=None):
    t = dxn.shape[0]
    nch = D_FF // FFN_COLS

    def body(dxn_ref, xh_ref, rstd_ref, h_ref, win_ref, wout_ref, g_ref,
             dx_ref, dh_ref, df_ref, dg_ref, db_ref):
        @pl.when(pl.program_id(0) == 0)
        def _():
            dg_ref[...] = jnp.zeros_like(dg_ref)
            db_ref[...] = jnp.zeros_like(db_ref)

        dy = dxn_ref[...]
        xhv = xh_ref[...]
        dr = _ln_bwd(dy, xhv, rstd_ref[...], g_ref[...])
        dg_ref[...] += jnp.sum(dy * xhv, axis=0, keepdims=True)
        db_ref[...] += jnp.sum(dy, axis=0, keepdims=True)
        df = (0.5 * dr).astype(BF16)
        df_ref[...] = df
        dx = ALPHA * dr
        for k in range(nch):
            cg = slice(k * FFN_COLS, (k + 1) * FFN_COLS)
            cu = slice(D_FF + k * FFN_COLS, D_FF + (k + 1) * FFN_COLS)
            hg = h_ref[:, cg].astype(F32)
            hu = h_ref[:, cu].astype(F32)
            sg = _sigmoid(hg)
            silu = hg * sg
            da = _dot_nt(df, wout_ref[cg, :])
            dhu = (da * silu).astype(BF16)
            dhg = (da * hu * (sg * (1.0 + hg * (1.0 - sg)))).astype(BF16)
            dh_ref[:, cg] = dhg
            dh_ref[:, cu] = dhu
            dx = dx + _dot_nt(dhg, win_ref[:, cg]) + _dot_nt(dhu, win_ref[:, cu])
        dx_ref[...] = dx

    return _call(
        body, name, (t // tm,),
        [_rows(tm, D_MODEL), _rows(tm, D_MODEL), _rows(tm, 1), _rows(tm, 2 * D_FF),
         _resident((D_MODEL, 2 * D_FF)), _resident((D_FF, D_MODEL)), _fixed((1, D_MODEL))],
        (_rows(tm, D_MODEL), _rows(tm, 2 * D_FF), _rows(tm, D_MODEL),
         _fixed((1, D_MODEL)), _fixed((1, D_MODEL))),
        (SDS((t, D_MODEL), F32), SDS((t, 2 * D_FF), BF16), SDS((t, D_MODEL), BF16),
         SDS((1, D_MODEL), F32), SDS((1, D_MODEL), F32)),
        (dxn, xh, rstd, h, w_in, w_out, g), sem=("arbitrary",), bg=bg)


def _tn_matmul(a, b, name, bm, bn, col_block=0, total_cols=None, prev=None, bg=None, a_cols=None):
    t, m = a.shape
    a_first = 0
    if a_cols is not None:
        a_first, m = a_cols[0], a_cols[1] * bm
    n = b.shape[1]
    total_cols = n if total_cols is None else total_cols
    bk = min(TN_K_BLOCK, t)
    nk = t // bk
    n_in = 2 if prev is None else 4

    def body(*refs):
        a_ref, b_ref = refs[0], refs[1]
        o_ref, ob_ref = refs[n_in], refs[n_in + 1]
        k = pl.program_id(2)

        @pl.when(k == 0)
        def _():
            o_ref[...] = jnp.zeros_like(o_ref)

        o_ref[...] += _dot_tn(a_ref[...], b_ref[...])

        @pl.when(k == nk - 1)
        def _():
            ob_ref[...] = o_ref[...].astype(BF16)

    in_specs = [pl.BlockSpec((bk, bm), lambda i, j, k: (k, i + a_first)),
                pl.BlockSpec((bk, bn), lambda i, j, k: (k, j))]
    args = [a, b]
    aliases = {}
    if prev is not None:
        in_specs += [pl.BlockSpec(memory_space=pl.ANY), pl.BlockSpec(memory_space=pl.ANY)]
        args += list(prev)
        aliases = {2: 0, 3: 1}
    out_spec = pl.BlockSpec((bm, bn), lambda i, j, k: (i, j + col_block))
    return _call(body, name, (m // bm, n // bn, nk), in_specs, (out_spec, out_spec),
                 (SDS((m, total_cols), F32), SDS((m, total_cols), BF16)), args,
                 sem=("parallel", "parallel", "arbitrary"), bg=bg, aliases=aliases)


def _mixin_fwd(x1, w, tm, bg=None):
    t = x1.shape[0]

    def body(x_ref, w_ref, xb_ref, za_ref, zuv_ref, gab_ref):
        xb = x_ref[...].astype(BF16)
        xb_ref[...] = xb
        za_ref[...] = _dot(xb, w_ref[:, 0:512]).astype(BF16)
        zuv_ref[...] = _dot(xb, w_ref[:, 512:1536]).astype(BF16)
        gab_ref[...] = _dot(xb, w_ref[:, 1536:3584]).astype(BF16)

    return _call(
        body, "mixin_fwd", (t // tm,),
        [_rows(tm, D_MODEL), _resident((D_MODEL, 3584))],
        (_rows(tm, D_MODEL), _rows(tm, 512), _rows(tm, 1024), _rows(tm, 2048)),
        (SDS((t, D_MODEL), BF16), SDS((t, 512), BF16), SDS((t, 1024), BF16), SDS((t, 2048), BF16)),
        (x1, w), sem=("parallel",), bg=bg)


def _mixin_bwd(dx1a, dza, dzuv, dgab, w, tm, bg=None):
    t = dx1a.shape[0]

    def body(d_ref, dza_ref, dzuv_ref, dgab_ref, w_ref, dx_ref):
        dx_ref[...] = (d_ref[...] + _dot_nt(dza_ref[...], w_ref[:, 0:512])
                       + _dot_nt(dzuv_ref[...], w_ref[:, 512:1536])
                       + _dot_nt(dgab_ref[...], w_ref[:, 1536:3584]))

    return _call(
        body, "mixin_bwd", (t // tm,),
        [_rows(tm, D_MODEL), _rows(tm, 512), _rows(tm, 1024), _rows(tm, 2048), _resident((D_MODEL, 3584))],
        (_rows(tm, D_MODEL),), (SDS((t, D_MODEL), F32),),
        (dx1a, dza, dzuv, dgab, w), sem=("parallel",), bg=bg)


def _unrolled(lo, hi, body, carry):
    for j in range(lo, hi):
        carry = body(j, carry)
    return carry


def _scan_fwd(hr_ref, hi_ref, a_ref, ap_ref, carry_ref, seg, cin_ref):
    for lc in range(SSM_LANES // SCAN_LANES):
        ls = slice(lc * SCAN_LANES, (lc + 1) * SCAN_LANES)
        a_r = jnp.broadcast_to(a_ref[0:1, ls], (8, SCAN_LANES))
        a_i = jnp.broadcast_to(a_ref[1:2, ls], (8, SCAN_LANES))

        def step(j, hc, ls=ls, a_r=a_r, a_i=a_i):
            h_r, h_i = hc
            rows = pl.ds(j * 8, 8)
            n_r = a_r * h_r - a_i * h_i + hr_ref[rows, ls]
            n_i = a_r * h_i + a_i * h_r + hi_ref[rows, ls]
            hr_ref[rows, ls] = n_r
            hi_ref[rows, ls] = n_i
            return n_r, n_i

        zero = jnp.zeros((8, SCAN_LANES), F32)
        f_r, f_i = _unrolled(0, seg, step, (zero, zero))
        c_r = carry_ref[0:1, ls]
        c_i = carry_ref[1:2, ls]
        p_r = ap_ref[0:1, ls]
        p_i = ap_ref[1:2, ls]
        rows_r, rows_i = [], []
        for s in range(8):
            rows_r.append(c_r)
            rows_i.append(c_i)
            c_r, c_i = (f_r[s:s + 1] + p_r * c_r - p_i * c_i,
                        f_i[s:s + 1] + p_r * c_i + p_i * c_r)
        carry_ref[0:1, ls] = c_r
        carry_ref[1:2, ls] = c_i
        cin_r = jnp.concatenate(rows_r, axis=0)
        cin_i = jnp.concatenate(rows_i, axis=0)
        if cin_ref is not None:
            cin_ref[0, :, ls] = cin_r
            cin_ref[1, :, ls] = cin_i

        def fix(j, cc, ls=ls, a_r=a_r, a_i=a_i):
            c_r, c_i = cc
            c_r, c_i = a_r * c_r - a_i * c_i, a_r * c_i + a_i * c_r
            rows = pl.ds(j * 8, 8)
            hr_ref[rows, ls] = hr_ref[rows, ls] + c_r
            hi_ref[rows, ls] = hi_ref[rows, ls] + c_i
            return c_r, c_i

        _unrolled(0, seg, fix, (cin_r, cin_i))


def _scan_bwd(gr_ref, gi_ref, hr_ref, hi_ref, cin_ref, a_ref, ap_ref, rcarry_ref, da_ref, seg):
    for lc in range(SSM_LANES // SCAN_LANES):
        ls = slice(lc * SCAN_LANES, (lc + 1) * SCAN_LANES)
        a_r = jnp.broadcast_to(a_ref[0:1, ls], (8, SCAN_LANES))
        a_i = jnp.broadcast_to(a_ref[1:2, ls], (8, SCAN_LANES))

        def step(t, gc, ls=ls, a_r=a_r, a_i=a_i):
            g_r, g_i = gc
            rows = pl.ds((seg - 1 - t) * 8, 8)
            n_r = gr_ref[rows, ls] + a_r * g_r + a_i * g_i
            n_i = gi_ref[rows, ls] + a_r * g_i - a_i * g_r
            gr_ref[rows, ls] = n_r
            gi_ref[rows, ls] = n_i
            return n_r, n_i

        zero = jnp.zeros((8, SCAN_LANES), F32)
        f_r, f_i = _unrolled(0, seg, step, (zero, zero))
        c_r = rcarry_ref[0:1, ls]
        c_i = rcarry_ref[1:2, ls]
        p_r = ap_ref[0:1, ls]
        p_i = ap_ref[1:2, ls]
        rows_r, rows_i = [None] * 8, [None] * 8
        for s in range(7, -1, -1):
            rows_r[s] = c_r
            rows_i[s] = c_i
            c_r, c_i = (f_r[s:s + 1] + p_r * c_r + p_i * c_i,
                        f_i[s:s + 1] + p_r * c_i - p_i * c_r)
        rcarry_ref[0:1, ls] = c_r
        rcarry_ref[1:2, ls] = c_i
        cin_r = jnp.concatenate(rows_r, axis=0)
        cin_i = jnp.concatenate(rows_i, axis=0)

        def fix_row(j_rows, hp_r, hp_i, cc, ls=ls, a_r=a_r, a_i=a_i):
            c_r, c_i, acc_r, acc_i = cc
            c_r, c_i = a_r * c_r + a_i * c_i, a_r * c_i - a_i * c_r
            g_r = gr_ref[j_rows, ls] + c_r
            g_i = gi_ref[j_rows, ls] + c_i
            gr_ref[j_rows, ls] = g_r
            gi_ref[j_rows, ls] = g_i
            acc_r = acc_r + g_r * hp_r + g_i * hp_i
            acc_i = acc_i + g_i * hp_r - g_r * hp_i
            return c_r, c_i, acc_r, acc_i

        def fix(t, cc, ls=ls, fix_row=fix_row):
            j = seg - 1 - t
            rows = pl.ds(j * 8, 8)
            prev = pl.ds((j - 1) * 8, 8)
            return fix_row(rows, hr_ref[prev, ls], hi_ref[prev, ls], cc)

        cc = _unrolled(0, seg - 1, fix, (cin_r, cin_i, zero, zero))
        _, _, acc_r, acc_i = fix_row(pl.ds(0, 8), cin_ref[0, :, ls], cin_ref[1, :, ls], cc)
        da_ref[0, :, ls] += acc_r
        da_ref[1, :, ls] += acc_i


def _s5_fwd(za, sp, bsz, seq, tb, bg=None):
    nb = seq // tb
    seg = tb // 8
    t = bsz * seq

    def body(za_ref, perm_ref, permt_ref, mre_ref, mim_ref, nre_ref, nim_ref, a_ref, ap_ref,
             dsk_ref, gw_ref, gb_ref, out_ref, y2_ref, car_ref, hr_ref, hi_ref, carry_ref):
        @pl.when(pl.program_id(1) == 0)
        def _():
            carry_ref[...] = jnp.zeros_like(carry_ref)

        car_ref[0] = carry_ref[...]
        up = _dot(perm_ref[...], za_ref[...])
        upb = up.astype(BF16)
        for bb in range(S5_BLOCKS):
            ub = upb[:, bb * S5_BLOCK_IN:(bb + 1) * S5_BLOCK_IN]
            st = slice(bb * S5_BLOCK_ST, (bb + 1) * S5_BLOCK_ST)
            hr_ref[:, st] = _dot(ub, mre_ref[bb])
            hi_ref[:, st] = _dot(ub, mim_ref[bb])
        _scan_fwd(hr_ref, hi_ref, a_ref, ap_ref, carry_ref, seg, None)
        ys = []
        for bb in range(S5_BLOCKS):
            st = slice(bb * S5_BLOCK_ST, (bb + 1) * S5_BLOCK_ST)
            ys.append(_dot(hr_ref[:, st].astype(BF16), nre_ref[bb])
                      - _dot(hi_ref[:, st].astype(BF16), nim_ref[bb]))
        y2 = jnp.concatenate(ys, axis=1) + dsk_ref[...] * up
        y2_ref[...] = y2
        y3 = _gelu(y2)
        gl = _dot(y3.astype(BF16), gw_ref[...]) + gb_ref[...]
        oa = y3 * _sigmoid(gl)
        out_ref[...] = _dot(permt_ref[...], oa.astype(BF16)).astype(BF16)

    blk = pl.BlockSpec((tb, D_SSM), lambda b, j: (b * nb + j, 0))
    m_shape = (S5_BLOCKS, S5_BLOCK_IN, S5_BLOCK_ST)
    n_shape = (S5_BLOCKS, S5_BLOCK_ST, S5_BLOCK_IN)
    return _call(
        body, "s5_fwd", (bsz, nb),
        [blk, _fixed((tb, tb)), _fixed((tb, tb)), _fixed(m_shape), _fixed(m_shape), _fixed(n_shape),
         _fixed(n_shape), _fixed((2, SSM_LANES)), _fixed((2, SSM_LANES)), _fixed((1, D_SSM)),
         _fixed((D_SSM, D_SSM)), _fixed((1, D_SSM))],
        (blk, blk, pl.BlockSpec((1, 2, SSM_LANES), lambda b, j: (b * nb + j, 0, 0))),
        (SDS((t, D_SSM), BF16), SDS((t, D_SSM), F32), SDS((bsz * nb, 2, SSM_LANES), F32)),
        (za, sp["perm"], sp["permt"], sp["mre"], sp["mim"], sp["nre"], sp["nim"], sp["a"], sp["ap"],
         sp["dskip"], sp["glu_w"], sp["glu_b"]),
        scratch=[pltpu.VMEM((tb, SSM_LANES), F32), pltpu.VMEM((tb, SSM_LANES), F32),
                 pltpu.VMEM((2, SSM_LANES), F32)],
        sem=("arbitrary", "arbitrary"), bg=bg)


def _s5_bwd(za, y2p, doa, carries, sp, bsz, seq, tb, bg=None):
    nb = seq // tb
    seg = tb // 8
    t = bsz * seq

    def body(za_ref, y2_ref, doa_ref, car_ref, perm_ref, permt_ref, mre_ref, mim_ref, mtre_ref, mtim_ref,
             nre_ref, nim_ref, ntre_ref, ntim_ref, a_ref, ap_ref, dsk_ref, gw_ref, gwt_ref, gb_ref,
             dza_ref, dmr_ref, dmi_ref, dnr_ref, dni_ref, da_ref, ddsk_ref, dgw_ref, dgb_ref,
             hr_ref, hi_ref, gr_ref, gi_ref, cin_ref, carry_ref, rcarry_ref):
        first = jnp.logical_and(pl.program_id(0) == 0, pl.program_id(1) == 0)

        @pl.when(first)
        def _():
            for r in (dmr_ref, dmi_ref, dnr_ref, dni_ref, da_ref, ddsk_ref, dgw_ref, dgb_ref):
                r[...] = jnp.zeros_like(r)

        @pl.when(pl.program_id(1) == 0)
        def _():
            rcarry_ref[...] = jnp.zeros_like(rcarry_ref)

        carry_ref[...] = car_ref[0]
        perm = perm_ref[...]
        up = _dot(perm, za_ref[...])
        upb = up.astype(BF16)
        for bb in range(S5_BLOCKS):
            ub = upb[:, bb * S5_BLOCK_IN:(bb + 1) * S5_BLOCK_IN]
            st = slice(bb * S5_BLOCK_ST, (bb + 1) * S5_BLOCK_ST)
            hr_ref[:, st] = _dot(ub, mre_ref[bb])
            hi_ref[:, st] = _dot(ub, mim_ref[bb])
        _scan_fwd(hr_ref, hi_ref, a_ref, ap_ref, carry_ref, seg, cin_ref)

        y2 = y2_ref[...]
        y3 = _gelu(y2)
        y3b = y3.astype(BF16)
        sg = _sigmoid(_dot(y3b, gw_ref[...]) + gb_ref[...])
        d0 = doa_ref[...]
        d_hi = d0.astype(BF16)
        d1 = d0 - d_hi.astype(F32)
        d_mid = d1.astype(BF16)
        d_lo = (d1 - d_mid.astype(F32)).astype(BF16)
        doap = _dot(perm, d_hi) + _dot(perm, d_mid) + _dot(perm, d_lo)
        dgl = doap * y3 * sg * (1.0 - sg)
        dglb = dgl.astype(BF16)
        dy3 = doap * sg + _dot(dglb, gwt_ref[...])
        dgw_ref[...] += _dot_tn(y3b, dglb)
        dgb_ref[...] += jnp.sum(dgl, axis=0, keepdims=True)
        dy2 = dy3 * _gelu_grad(y2)
        ddsk_ref[...] += jnp.sum(dy2 * up, axis=0, keepdims=True)
        dyb = dy2.astype(BF16)
        for bb in range(S5_BLOCKS):
            dyc = dyb[:, bb * S5_BLOCK_IN:(bb + 1) * S5_BLOCK_IN]
            st = slice(bb * S5_BLOCK_ST, (bb + 1) * S5_BLOCK_ST)
            gr_ref[:, st] = _dot(dyc, ntre_ref[bb])
            gi_ref[:, st] = -_dot(dyc, ntim_ref[bb])
            dnr_ref[bb] += _dot_tn(hr_ref[:, st].astype(BF16), dyc)
            dni_ref[bb] += -_dot_tn(hi_ref[:, st].astype(BF16), dyc)
        _scan_bwd(gr_ref, gi_ref, hr_ref, hi_ref, cin_ref, a_ref, ap_ref, rcarry_ref, da_ref, seg)
        dus = []
        for bb in range(S5_BLOCKS):
            st = slice(bb * S5_BLOCK_ST, (bb + 1) * S5_BLOCK_ST)
            grb = gr_ref[:, st].astype(BF16)
            gib = gi_ref[:, st].astype(BF16)
            dus.append(_dot(grb, mtre_ref[bb]) + _dot(gib, mtim_ref[bb]))
            ub = upb[:, bb * S5_BLOCK_IN:(bb + 1) * S5_BLOCK_IN]
            dmr_ref[bb] += _dot_tn(ub, grb)
            dmi_ref[bb] += _dot_tn(ub, gib)
        du = jnp.concatenate(dus, axis=1) + dy2 * dsk_ref[...]
        dza_ref[...] = _dot(permt_ref[...], du.astype(BF16)).astype(BF16)

    def rev(b, j):
        return (b * nb + (nb - 1 - j), 0)

    blk = pl.BlockSpec((tb, D_SSM), rev)
    m_shape = (S5_BLOCKS, S5_BLOCK_IN, S5_BLOCK_ST)
    n_shape = (S5_BLOCKS, S5_BLOCK_ST, S5_BLOCK_IN)
    return _call(
        body, "s5_bwd", (bsz, nb),
        [blk, blk, blk, pl.BlockSpec((1, 2, SSM_LANES), lambda b, j: (b * nb + (nb - 1 - j), 0, 0)),
         _fixed((tb, tb)), _fixed((tb, tb)), _fixed(m_shape), _fixed(m_shape), _fixed(n_shape), _fixed(n_shape),
         _fixed(n_shape), _fixed(n_shape), _fixed(m_shape), _fixed(m_shape),
         _fixed((2, SSM_LANES)), _fixed((2, SSM_LANES)), _fixed((1, D_SSM)),
         _fixed((D_SSM, D_SSM)), _fixed((D_SSM, D_SSM)), _fixed((1, D_SSM))],
        (blk, _fixed(m_shape), _fixed(m_shape), _fixed(n_shape), _fixed(n_shape),
         _fixed((2, 8, SSM_LANES)), _fixed((1, D_SSM)), _fixed((D_SSM, D_SSM)), _fixed((1, D_SSM))),
        (SDS((t, D_SSM), BF16), SDS(m_shape, F32), SDS(m_shape, F32), SDS(n_shape, F32), SDS(n_shape, F32),
         SDS((2, 8, SSM_LANES), F32), SDS((1, D_SSM), F32), SDS((D_SSM, D_SSM), F32), SDS((1, D_SSM), F32)),
        (za, y2p, doa, carries, sp["perm"], sp["permt"], sp["mre"], sp["mim"], sp["mtre"], sp["mtim"],
         sp["nre"], sp["nim"], sp["ntre"], sp["ntim"], sp["a"], sp["ap"], sp["dskip"], sp["glu_w"],
         sp["glu_wt"], sp["glu_b"]),
        scratch=[pltpu.VMEM((tb, SSM_LANES), F32), pltpu.VMEM((tb, SSM_LANES), F32),
                 pltpu.VMEM((tb, SSM_LANES), F32), pltpu.VMEM((tb, SSM_LANES), F32),
                 pltpu.VMEM((2, 8, SSM_LANES), F32), pltpu.VMEM((2, SSM_LANES), F32),
                 pltpu.VMEM((2, SSM_LANES), F32)],
        sem=("arbitrary", "arbitrary"), bg=bg)


def _gmlp_spatial(ws_ref, vb):
    lane = lax.broadcasted_iota(jnp.int32, (CHUNK, 128), 1)
    parts = []
    for j in range(GMLP_HEADS // 2):
        vp = vb[:, 128 * j:128 * (j + 1)]
        parts.append(jnp.where(lane < GMLP_HEAD_DIM, _dot(ws_ref[2 * j], vp), _dot(ws_ref[2 * j + 1], vp)))
    return jnp.concatenate(parts, axis=1)


def _gmlp_fwd(zuv, ln_g, ln_b, wsm, bias, bg=None):
    t = zuv.shape[0]

    def body(z_ref, g_ref, b_ref, ws_ref, bias_ref, out_ref):
        u = _gelu(z_ref[:, 0:D_GMLP].astype(F32))
        v0 = _gelu(z_ref[:, D_GMLP:2 * D_GMLP].astype(F32))
        v, _, _ = _ln_fwd(v0, g_ref[...], b_ref[...])
        s = _gmlp_spatial(ws_ref, v.astype(BF16)) + bias_ref[...]
        out_ref[...] = (u * s).astype(BF16)

    return _call(
        body, "gmlp_fwd", (t // CHUNK,),
        [_rows(CHUNK, 2 * D_GMLP), _fixed((1, D_GMLP)), _fixed((1, D_GMLP)),
         _fixed((GMLP_HEADS, CHUNK, CHUNK)), _fixed((CHUNK, D_GMLP))],
        (_rows(CHUNK, D_GMLP),), (SDS((t, D_GMLP), BF16),),
        (zuv, ln_g, ln_b, wsm, bias), sem=("parallel",), bg=bg)


def _gmlp_bwd(zuv, dgm, ln_g, ln_b, wsm, wsmt, bias, bg=None):
    t = zuv.shape[0]

    def body(z_ref, d_ref, g_ref, b_ref, ws_ref, wst_ref, bias_ref,
             dz_ref, dws_ref, dbias_ref, dg_ref, db_ref):
        @pl.when(pl.program_id(0) == 0)
        def _():
            for r in (dws_ref, dbias_ref, dg_ref, db_ref):
                r[...] = jnp.zeros_like(r)

        zu = z_ref[:, 0:D_GMLP].astype(F32)
        zv = z_ref[:, D_GMLP:2 * D_GMLP].astype(F32)
        u = _gelu(zu)
        v0 = _gelu(zv)
        gam = g_ref[...]
        v, vhat, rstd = _ln_fwd(v0, gam, b_ref[...])
        vb = v.astype(BF16)
        s = _gmlp_spatial(ws_ref, vb) + bias_ref[...]
        d = d_ref[...]
        dz_ref[:, 0:D_GMLP] = (d * s * _gelu_grad(zu)).astype(BF16)
        ds = d * u
        dbias_ref[...] += ds
        dsb = ds.astype(BF16)
        lane = lax.broadcasted_iota(jnp.int32, (CHUNK, 128), 1)
        tril = (lax.broadcasted_iota(jnp.int32, (CHUNK, CHUNK), 0)
                >= lax.broadcasted_iota(jnp.int32, (CHUNK, CHUNK), 1))
        zero_b = jnp.zeros((CHUNK, 128), BF16)
        parts = []
        for j in range(GMLP_HEADS // 2):
            dsp = dsb[:, 128 * j:128 * (j + 1)]
            vp = vb[:, 128 * j:128 * (j + 1)]
            parts.append(jnp.where(lane < GMLP_HEAD_DIM, _dot(wst_ref[2 * j], dsp),
                                   _dot(wst_ref[2 * j + 1], dsp)))
            lo = jnp.where(lane < GMLP_HEAD_DIM, dsp, zero_b)
            hi = jnp.where(lane < GMLP_HEAD_DIM, zero_b, dsp)
            dws_ref[2 * j] += jnp.where(tril, _dot_nt(lo, vp), 0.0)
            dws_ref[2 * j + 1] += jnp.where(tril, _dot_nt(hi, vp), 0.0)
        dv = jnp.concatenate(parts, axis=1)
        dg_ref[...] += jnp.sum(dv * vhat, axis=0, keepdims=True)
        db_ref[...] += jnp.sum(dv, axis=0, keepdims=True)
        dz_ref[:, D_GMLP:2 * D_GMLP] = (_ln_bwd(dv, vhat, rstd, gam) * _gelu_grad(zv)).astype(BF16)

    return _call(
        body, "gmlp_bwd", (t // CHUNK,),
        [_rows(CHUNK, 2 * D_GMLP), _rows(CHUNK, D_GMLP), _fixed((1, D_GMLP)), _fixed((1, D_GMLP)),
         _fixed((GMLP_HEADS, CHUNK, CHUNK)), _fixed((GMLP_HEADS, CHUNK, CHUNK)), _fixed((CHUNK, D_GMLP))],
        (_rows(CHUNK, 2 * D_GMLP), _fixed((GMLP_HEADS, CHUNK, CHUNK)), _fixed((CHUNK, D_GMLP)),
         _fixed((1, D_GMLP)), _fixed((1, D_GMLP))),
        (SDS((t, 2 * D_GMLP), BF16), SDS((GMLP_HEADS, CHUNK, CHUNK), F32), SDS((CHUNK, D_GMLP), F32),
         SDS((1, D_GMLP), F32), SDS((1, D_GMLP), F32)),
        (zuv, dgm, ln_g, ln_b, wsm, wsmt, bias), sem=("arbitrary",), bg=bg)


def _mixout_fwd(x1, s5o, gm, gab, ua, ub, wmo, g, b, tm, bg=None):
    t = x1.shape[0]

    def body(x_ref, s_ref, m_ref, gab_ref, ua_ref, ub_ref, wmo_ref, g_ref, b_ref,
             xn_ref, xh_ref, rstd_ref):
        ya = _dot(s_ref[...], ua_ref[...])
        yb = _dot(m_ref[...], ub_ref[...])
        mix = (_sigmoid(gab_ref[:, 0:D_MODEL].astype(F32)) * ya
               + _sigmoid(gab_ref[:, D_MODEL:2 * D_MODEL].astype(F32)) * yb)
        r = ALPHA * x_ref[...] + _dot(mix.astype(BF16), wmo_ref[...])
        y, xh, rstd = _ln_fwd(r, g_ref[...], b_ref[...])
        xn_ref[...] = y
        xh_ref[...] = xh
        rstd_ref[...] = rstd

    return _call(
        body, "mixout_fwd", (t // tm,),
        [_rows(tm, D_MODEL), _rows(tm, D_SSM), _rows(tm, D_GMLP), _rows(tm, 2 * D_MODEL),
         _resident((D_SSM, D_MODEL)), _resident((D_GMLP, D_MODEL)), _resident((D_MODEL, D_MODEL)),
         _fixed((1, D_MODEL)), _fixed((1, D_MODEL))],
        (_rows(tm, D_MODEL), _rows(tm, D_MODEL), _rows(tm, 1)),
        (SDS((t, D_MODEL), F32), SDS((t, D_MODEL), F32), SDS((t, 1), F32)),
        (x1, s5o, gm, gab, ua, ub, wmo, g, b), sem=("parallel",), bg=bg)


def _mixout_bwd(dx2, xh, rstd, s5o, gm, gab, ua, ub, wmo, g, tm, bg=None):
    t = dx2.shape[0]

    def body(d_ref, xh_ref, rstd_ref, s_ref, m_ref, gab_ref, ua_ref, ub_ref, wmo_ref, g_ref,
             dx1_ref, dmx_ref, mb_ref, dya_ref, dyb_ref, ds5_ref, dgm_ref, dgab_ref, dg_ref, db_ref):
        @pl.when(pl.program_id(0) == 0)
        def _():
            dg_ref[...] = jnp.zeros_like(dg_ref)
            db_ref[...] = jnp.zeros_like(db_ref)

        dy = d_ref[...]
        xhv = xh_ref[...]
        dr = _ln_bwd(dy, xhv, rstd_ref[...], g_ref[...])
        dg_ref[...] += jnp.sum(dy * xhv, axis=0, keepdims=True)
        db_ref[...] += jnp.sum(dy, axis=0, keepdims=True)
        dx1_ref[...] = ALPHA * dr
        drb = dr.astype(BF16)
        dmx_ref[...] = drb
        dm = _dot_nt(drb, wmo_ref[...])
        ya = _dot(s_ref[...], ua_ref[...])
        yb = _dot(m_ref[...], ub_ref[...])
        sa = _sigmoid(gab_ref[:, 0:D_MODEL].astype(F32))
        sb = _sigmoid(gab_ref[:, D_MODEL:2 * D_MODEL].astype(F32))
        mb_ref[...] = (sa * ya + sb * yb).astype(BF16)
        dya = (dm * sa).astype(BF16)
        dyb = (dm * sb).astype(BF16)
        dya_ref[...] = dya
        dyb_ref[...] = dyb
        dgab_ref[:, 0:D_MODEL] = (dm * ya * sa * (1.0 - sa)).astype(BF16)
        dgab_ref[:, D_MODEL:2 * D_MODEL] = (dm * yb * sb * (1.0 - sb)).astype(BF16)
        ds5_ref[...] = _dot_nt(dya, ua_ref[...])
        dgm_ref[...] = _dot_nt(dyb, ub_ref[...])

    return _call(
        body, "mixout_bwd", (t // tm,),
        [_rows(tm, D_MODEL), _rows(tm, D_MODEL), _rows(tm, 1), _rows(tm, D_SSM), _rows(tm, D_GMLP),
         _rows(tm, 2 * D_MODEL), _resident((D_SSM, D_MODEL)), _resident((D_GMLP, D_MODEL)),
         _resident((D_MODEL, D_MODEL)), _fixed((1, D_MODEL))],
        (_rows(tm, D_MODEL), _rows(tm, D_MODEL), _rows(tm, D_MODEL), _rows(tm, D_MODEL),
         _rows(tm, D_MODEL), _rows(tm, D_SSM), _rows(tm, D_GMLP), _rows(tm, 2 * D_MODEL),
         _fixed((1, D_MODEL)), _fixed((1, D_MODEL))),
        (SDS((t, D_MODEL), F32), SDS((t, D_MODEL), BF16), SDS((t, D_MODEL), BF16),
         SDS((t, D_MODEL), BF16), SDS((t, D_MODEL), BF16), SDS((t, D_SSM), F32),
         SDS((t, D_GMLP), F32), SDS((t, 2 * D_MODEL), BF16),
         SDS((1, D_MODEL), F32), SDS((1, D_MODEL), F32)),
        (dx2, xh, rstd, s5o, gm, gab, ua, ub, wmo, g), sem=("arbitrary",), bg=bg)


def _ple_loss(x3, p, tgt, wpg, wpp, tm, bg=None):
    t = x3.shape[0]

    def body(x_ref, p_ref, t_ref, wpg_ref, wpp_ref, dx_ref, xb_ref, pb_ref, dq_ref, de_ref, loss_ref):
        @pl.when(pl.program_id(0) == 0)
        def _():
            loss_ref[...] = jnp.zeros_like(loss_ref)

        x3v = x_ref[...]
        xb = x3v.astype(BF16)
        pb = p_ref[...].astype(BF16)
        xb_ref[...] = xb
        pb_ref[...] = pb
        s = _sigmoid(_dot(xb, wpg_ref[...]))
        e = _dot(pb, wpp_ref[...])
        diff = x3v + s * e - t_ref[...]
        loss_ref[...] += jnp.sum(diff * diff, axis=0, keepdims=True)
        dout = diff * (1.0 / D_MODEL)
        de_ref[...] = (dout * s).astype(BF16)
        dq = (dout * e * s * (1.0 - s)).astype(BF16)
        dq_ref[...] = dq
        dx_ref[...] = dout + _dot_nt(dq, wpg_ref[...])

    return _call(
        body, "ple_loss", (t // tm,),
        [_rows(tm, D_MODEL), _rows(tm, PLE_DIM), _rows(tm, D_MODEL),
         _resident((D_MODEL, D_MODEL)), _resident((PLE_DIM, D_MODEL))],
        (_rows(tm, D_MODEL), _rows(tm, D_MODEL), _rows(tm, PLE_DIM), _rows(tm, D_MODEL),
         _rows(tm, D_MODEL), _fixed((1, D_MODEL))),
        (SDS((t, D_MODEL), F32), SDS((t, D_MODEL), BF16), SDS((t, PLE_DIM), BF16),
         SDS((t, D_MODEL), BF16), SDS((t, D_MODEL), BF16), SDS((1, D_MODEL), F32)),
        (x3, p, tgt, wpg, wpp), sem=("arbitrary",), bg=bg)


def _s5_discretise(lre, lim, log_dt, bre, bim):
    dt = jnp.exp(log_dt)[:, None]
    mag = jnp.exp(lre * dt)
    abr = mag * jnp.cos(lim * dt)
    abi = mag * jnp.sin(lim * dt)
    nr = abr - 1.0
    ni = abi
    den = lre * lre + lim * lim
    cr = ((nr * lre + ni * lim) / den)[..., None]
    ci = ((ni * lre - nr * lim) / den)[..., None]
    return abr, abi, cr * bre - ci * bim, cr * bim + ci * bre


def _block_diag_in(bb):
    v = bb.reshape(S5_BLOCKS, 8, SSM_STATE, SSM_GROUP_CH).transpose(0, 1, 3, 2)
    return jnp.einsum("bgip,gh->bgihp", v, jnp.eye(8, dtype=bb.dtype)).reshape(
        S5_BLOCKS, S5_BLOCK_IN, S5_BLOCK_ST)


def _block_diag_in_t(dm):
    v = dm.reshape(S5_BLOCKS, 8, SSM_GROUP_CH, 8, SSM_STATE)
    d = jnp.einsum("bgihp,gh->bgip", v, jnp.eye(8, dtype=dm.dtype))
    return d.transpose(0, 1, 3, 2).reshape(SSM_GROUPS, SSM_STATE, SSM_GROUP_CH)


def _block_diag_out(cc):
    v = cc.reshape(S5_BLOCKS, 8, SSM_GROUP_CH, SSM_STATE)
    return jnp.einsum("bgip,gh->bgphi", v, jnp.eye(8, dtype=cc.dtype)).reshape(
        S5_BLOCKS, S5_BLOCK_ST, S5_BLOCK_IN)


def _block_diag_out_t(dn):
    v = dn.reshape(S5_BLOCKS, 8, SSM_STATE, 8, SSM_GROUP_CH)
    d = jnp.einsum("bgphi,gh->bgip", v, jnp.eye(8, dtype=dn.dtype))
    return d.reshape(SSM_GROUPS, SSM_GROUP_CH, SSM_STATE)


def _s5_setup(lre, lim, log_dt, bre, bim, cre, cim, d_skip, glu_w, glu_b, tb):
    seg = tb // 8
    abr, abi, bbr, bbi = _s5_discretise(lre, lim, log_dt, bre, bim)
    pr, pi = abr, abi
    for _ in range(int(math.log2(seg))):
        pr, pi = pr * pr - pi * pi, 2.0 * pr * pi
    rows = jnp.arange(tb)
    src = (rows % 8) * seg + rows // 8
    perm = (src[:, None] == jnp.arange(tb)[None, :]).astype(BF16)
    mre = _block_diag_in(bbr)
    mim = _block_diag_in(bbi)
    nre = _block_diag_out(cre)
    nim = _block_diag_out(cim)
    return {
        "perm": perm, "permt": perm.T,
        "mre": mre.astype(BF16), "mim": mim.astype(BF16),
        "mtre": mre.transpose(0, 2, 1).astype(BF16), "mtim": mim.transpose(0, 2, 1).astype(BF16),
        "nre": nre.astype(BF16), "nim": nim.astype(BF16),
        "ntre": nre.transpose(0, 2, 1).astype(BF16), "ntim": nim.transpose(0, 2, 1).astype(BF16),
        "a": jnp.stack([abr.reshape(-1), abi.reshape(-1)]),
        "ap": jnp.stack([pr.reshape(-1), pi.reshape(-1)]),
        "dskip": d_skip.reshape(1, D_SSM), "glu_w": glu_w, "glu_wt": glu_w.T,
        "glu_b": glu_b.reshape(1, D_SSM),
    }


BIG = ("ffn1_w_in", "ffn1_w_out", "mix_w_in", "ssm_glu_w", "up_a", "up_b", "mix_w_out",
       "ffn2_w_in", "ffn2_w_out", "ple_w_proj", "ple_w_gate")
BIG_AXIS = {"ffn1_w_in": 1, "ffn1_w_out": 0, "mix_w_in": 1, "ssm_glu_w": 0, "up_a": 1, "up_b": 1,
            "mix_w_out": 0, "ffn2_w_in": 1, "ffn2_w_out": 0, "ple_w_proj": 1, "ple_w_gate": 0}
SMALL = ("ln1_g", "ln1_b", "ssm_lambda_re", "ssm_lambda_im", "ssm_log_dt", "ssm_b_re", "ssm_b_im",
         "ssm_c_re", "ssm_c_im", "ssm_d", "ssm_glu_b", "gmlp_ln_g", "gmlp_ln_b", "gmlp_w_s",
         "gmlp_b_s", "ln2_g", "ln2_b", "ln3_g", "ln3_b")
SMALL_VIEW = {"ssm_b_re": (SSM_GROUPS, SSM_STATE * SSM_GROUP_CH), "ssm_b_im": (SSM_GROUPS, SSM_STATE * SSM_GROUP_CH)}


def _small_view(k, a):
    return a.reshape(SMALL_VIEW[k]) if k in SMALL_VIEW else a


def _place():
    return lax.axis_index("x"), lax.axis_index("y"), lax.axis_index("c")


def _other_chips(x, y):
    return [(1 - x, y), (x, 1 - y), (1 - x, 1 - y)]


def _window(ref, shard_shape, axis, chip, half):
    r, c = shard_shape
    hr = r // 2
    if axis == 0:
        if half is None:
            return ref.at[pl.ds(chip * r, r), :]
        return ref.at[pl.ds(chip * r + half * hr, hr), :]
    if half is None:
        return ref.at[:, pl.ds(chip * c, c)]
    return ref.at[pl.ds(half * hr, hr), pl.ds(chip * c, c)]


def _gather_weights(shards, axes):
    n = len(shards)
    shapes = [s.shape for s in shards]
    full = [(4 * r, c) if ax == 0 else (r, 4 * c) for (r, c), ax in zip(shapes, axes)]

    def remote(sems, i, k, src, dst, to):
        return pltpu.make_async_remote_copy(src_ref=src, dst_ref=dst, send_sem=sems[0].at[6 * i + k],
                                            recv_sem=sems[1].at[6 * i + k], device_id=to, device_id_type=MESH)

    def own_copies(ins, outs, sems):
        x, y, c = _place()
        me = 2 * x + y
        cps = []
        for i in range(n):
            hr = shapes[i][0] // 2
            mine = ins[i].at[pl.ds(c * hr, hr), :]
            for j, (cx, cy) in enumerate(_other_chips(x, y)):
                cps.append(remote(sems, i, j, mine, _window(outs[i], shapes[i], axes[i], me, c), (cx, cy, c)))
        local = [pltpu.make_async_copy(ins[i], _window(outs[i], shapes[i], axes[i], me, None), sems[2].at[i])
                 for i in range(n)]
        return cps, local

    def start(ins, outs, sems):
        cps, local = own_copies(ins, outs, sems)
        for cp in local + cps:
            cp.start()

    def finish(ins, outs, sems):
        x, y, c = _place()
        sibling = (x, y, 1 - c)
        passed = []
        for j, (cx, cy) in enumerate(_other_chips(x, y)):
            for i in range(n):
                w = _window(outs[i], shapes[i], axes[i], 2 * cx + cy, c)
                remote(sems, i, j, w, w, (cx, cy, c)).wait_recv()
                cp = remote(sems, i, 3 + j, w, w, sibling)
                cp.start()
                passed.append(cp)
        for j, (cx, cy) in enumerate(_other_chips(x, y)):
            for i in range(n):
                w = _window(outs[i], shapes[i], axes[i], 2 * cx + cy, 1 - c)
                remote(sems, i, 3 + j, w, w, sibling).wait_recv()
        cps, local = own_copies(ins, outs, sems)
        for cp in cps + passed:
            cp.wait_send()
        for cp in local:
            cp.wait()

    return _Exchange(shards, [SDS(f, BF16) for f in full],
                     [pltpu.SemaphoreType.DMA((6 * n,)), pltpu.SemaphoreType.DMA((6 * n,)),
                      pltpu.SemaphoreType.DMA((n,))], start, finish)


def _scatter_grads(parts, shapes, axes):
    n = len(parts)

    def copies(ins, outs, sems):
        x, y, c = _place()
        return [pltpu.make_async_remote_copy(
            src_ref=_window(ins[i], shapes[i], axes[i], 2 * cx + cy, None), dst_ref=outs[i].at[j],
            send_sem=sems[0].at[3 * i + j], recv_sem=sems[1].at[3 * i + j],
            device_id=(cx, cy, c), device_id_type=MESH)
            for i in range(n) for j, (cx, cy) in enumerate(_other_chips(x, y))]

    def start(ins, outs, sems):
        for cp in copies(ins, outs, sems):
            cp.start()

    def finish(ins, outs, sems):
        for cp in copies(ins, outs, sems):
            cp.wait()

    return _Exchange(parts, [SDS((3,) + tuple(s), BF16) for s in shapes],
                     [pltpu.SemaphoreType.DMA((3 * n,)), pltpu.SemaphoreType.DMA((3 * n,))], start, finish)


def _swap_halves(parts, shapes, axes):
    n = len(parts)

    def copies(ins, outs, sems):
        x, y, c = _place()
        cps = []
        for i in range(n):
            r, _ = shapes[i]
            hr = r // 2
            if axes[i] == 0:
                cps += [pltpu.make_async_remote_copy(
                    src_ref=ins[i].at[pl.ds(k * r + (1 - c) * hr, hr), :], dst_ref=outs[i].at[k],
                    send_sem=sems[0].at[i], recv_sem=sems[1].at[i], device_id=(x, y, 1 - c),
                    device_id_type=MESH) for k in range(4)]
            else:
                cps.append(pltpu.make_async_remote_copy(
                    src_ref=ins[i].at[pl.ds((1 - c) * hr, hr), :], dst_ref=outs[i],
                    send_sem=sems[0].at[i], recv_sem=sems[1].at[i], device_id=(x, y, 1 - c),
                    device_id_type=MESH))
        return cps

    def start(ins, outs, sems):
        for cp in copies(ins, outs, sems):
            cp.start()

    def finish(ins, outs, sems):
        x, y, c = _place()
        for i in range(n):
            pltpu.make_async_remote_copy(src_ref=outs[i], dst_ref=outs[i], send_sem=sems[0].at[i],
                                         recv_sem=sems[1].at[i], device_id=(x, y, 1 - c),
                                         device_id_type=MESH).wait()

    out = [SDS((4, r // 2, c), BF16) if ax == 0 else SDS((r // 2, 4 * c), BF16)
           for (r, c), ax in zip(shapes, axes)]
    return _Exchange(parts, out, [pltpu.SemaphoreType.DMA((n,)), pltpu.SemaphoreType.DMA((n,))], start, finish)


def _scatter_halves(pres, shapes):
    n = len(pres)

    def copies(ins, outs, sems):
        x, y, c = _place()
        return [pltpu.make_async_remote_copy(
            src_ref=ins[i].at[1 + j], dst_ref=outs[i].at[j], send_sem=sems[0].at[3 * i + j],
            recv_sem=sems[1].at[3 * i + j], device_id=(cx, cy, c), device_id_type=MESH)
            for i in range(n) for j, (cx, cy) in enumerate(_other_chips(x, y))]

    def start(ins, outs, sems):
        for cp in copies(ins, outs, sems):
            cp.start()

    def finish(ins, outs, sems):
        for cp in copies(ins, outs, sems):
            cp.wait()

    return _Exchange(pres, [SDS((3, r // 2, c), BF16) for r, c in shapes],
                     [pltpu.SemaphoreType.DMA((3 * n,)), pltpu.SemaphoreType.DMA((3 * n,))], start, finish)


def _swap_with_sibling(arrs):
    n = len(arrs)

    def copies(ins, outs, sems):
        x, y, c = _place()
        return [pltpu.make_async_remote_copy(src_ref=ins[i], dst_ref=outs[i], send_sem=sems[0].at[i],
                                             recv_sem=sems[1].at[i], device_id=(x, y, 1 - c),
                                             device_id_type=MESH) for i in range(n)]

    def start(ins, outs, sems):
        for cp in copies(ins, outs, sems):
            cp.start()

    def finish(ins, outs, sems):
        for cp in copies(ins, outs, sems):
            cp.wait()

    return _Exchange(arrs, [SDS(a.shape, a.dtype) for a in arrs],
                     [pltpu.SemaphoreType.DMA((n,)), pltpu.SemaphoreType.DMA((n,))], start, finish)


def _gather_small(arrs):
    n = len(arrs)

    def copy(sems, outs, i, k, block, to, src=None):
        px, py, pc = block
        dst = outs[i].at[4 * px + 2 * py + pc]
        return pltpu.make_async_remote_copy(
            src_ref=dst if src is None else src, dst_ref=dst, send_sem=sems[0].at[7 * i + k],
            recv_sem=sems[1].at[7 * i + k], device_id=to, device_id_type=MESH)

    direct = [math.prod(a.shape) * 4 <= DIRECT_GATHER_BYTES for a in arrs]

    def own_copies(ins, outs, sems):
        x, y, c = _place()
        cps = []
        for i in range(n):
            cps.append(copy(sems, outs, i, 0, (x, y, c), (x, y, 1 - c), src=ins[i]))
            for j, (cx, cy) in enumerate(_other_chips(x, y)):
                cps.append(copy(sems, outs, i, 1 + j, (x, y, c), (cx, cy, c), src=ins[i]))
                if direct[i]:
                    cps.append(copy(sems, outs, i, 4 + j, (x, y, c), (cx, cy, 1 - c), src=ins[i]))
        local = [pltpu.make_async_copy(ins[i], outs[i].at[4 * x + 2 * y + c], sems[2].at[i]) for i in range(n)]
        return cps, local

    def start(ins, outs, sems):
        cps, local = own_copies(ins, outs, sems)
        for cp in local + cps:
            cp.start()

    def finish(ins, outs, sems):
        x, y, c = _place()
        passed = []
        for j, (cx, cy) in enumerate(_other_chips(x, y)):
            for i in range(n):
                copy(sems, outs, i, 1 + j, (cx, cy, c), (x, y, c)).wait_recv()
                if not direct[i]:
                    cp = copy(sems, outs, i, 4 + j, (cx, cy, c), (x, y, 1 - c))
                    cp.start()
                    passed.append(cp)
        for i in range(n):
            copy(sems, outs, i, 0, (x, y, 1 - c), (x, y, c)).wait_recv()
            for j, (cx, cy) in enumerate(_other_chips(x, y)):
                copy(sems, outs, i, 4 + j, (cx, cy, 1 - c), (x, y, c)).wait_recv()
        cps, local = own_copies(ins, outs, sems)
        for cp in cps + passed:
            cp.wait_send()
        for cp in local:
            cp.wait()

    return _Exchange(arrs, [SDS((N_DEV,) + a.shape, F32) for a in arrs],
                     [pltpu.SemaphoreType.DMA((7 * n,)), pltpu.SemaphoreType.DMA((7 * n,)),
                      pltpu.SemaphoreType.DMA((n,))], start, finish)


def _local_step(x, p, tgt, wb, ws, shards=None):
    bsz, seq, _ = x.shape
    t = bsz * seq
    tm = min(256, t)
    tb = min(256, seq)
    x0 = x.reshape(t, D_MODEL)
    p0 = p.reshape(t, PLE_DIM)
    tg = tgt.reshape(t, D_MODEL)
    row = lambda v: v.reshape(1, -1)
    dist = shards is not None
    wb = dict(wb)
    recv, sums, other, gathered = {}, {}, {}, {}
    gb = {}
    gs = {}
    shape_of, axis_of = {}, {}
    chip = None
    if dist:
        shape_of = {k: tuple(shards[k].shape) for k in BIG}
        axis_of = dict(BIG_AXIS)
        for q in range(LAST_PIECES):
            shape_of[LAST_PIECE % q] = (D_MODEL // LAST_PIECES, shape_of["ffn1_w_in"][1])
            axis_of[LAST_PIECE % q] = 1
        xi, yi, ci = _place()
        chip = (2 * xi + yi).astype(jnp.int32).reshape(1)
        ids = jnp.stack([2 * xi + yi] + [2 * cx + cy for cx, cy in _other_chips(xi, yi)] + [ci]).astype(jnp.int32)
    halfbuf, pre = {}, {}

    def gather(names):
        return _gather_weights([shards[k] for k in names], [BIG_AXIS[k] for k in names]) if dist else None

    def exchange(scat=(), swap=(), halves=(), scat2=(), swap2=(), extra=None):
        if not dist:
            return None, []
        parts, tags = [], []
        if scat:
            parts.append(_scatter_grads([gb[k][1] for k in scat], [shape_of[k] for k in scat],
                                        [axis_of[k] for k in scat]))
            tags.append((recv, scat))
        if swap:
            for k in swap:
                sums[k] = _sum_blocks(gb[k][0], recv[k], shape_of[k], axis_of[k], chip, "sum_" + k)
            parts.append(_swap_with_sibling([sums[k] for k in swap]))
            tags.append((other, swap))
        if halves:
            parts.append(_swap_halves([gb[k][1] for k in halves], [shape_of[k] for k in halves],
                                      [axis_of[k] for k in halves]))
            tags.append((halfbuf, halves))
        if scat2:
            for k in scat2:
                pre[k] = _presum(gb[k][0], halfbuf[k], shape_of[k], axis_of[k], ids, "presum_" + k)
            parts.append(_scatter_halves([pre[k][1] for k in scat2], [shape_of[k] for k in scat2]))
            tags.append((recv, scat2))
        if swap2:
            for k in swap2:
                sums[k] = _sum_half(pre[k][0], recv[k], "sum_" + k)
            parts.append(_swap_with_sibling([sums[k] for k in swap2]))
            tags.append((other, swap2))
        if extra is not None:
            parts.append(extra[0])
            tags.append((extra[1], extra[2]))
        return (_join(parts), tags) if parts else (None, [])

    def take(ex_tags, got):
        ex, tags = ex_tags
        if ex is not None:
            for (dst, names), (o0, o1) in zip(tags, ex.cuts):
                dst.update(zip(names, got[o0:o1]))

    small_shape = {k: _small_view(k, v).shape for k, v in ws.items()}
    small_shape["loss_rows"] = (1, D_MODEL)
    ws = {k: v if (v.ndim == 2 and k != "ssm_log_dt") else v[0] for k, v in ws.items()}
    tril = jnp.tril(jnp.ones((CHUNK, CHUNK), dtype=bool))
    wsm = jnp.where(tril[None], ws["gmlp_w_s"], 0.0)
    wsm_b = wsm.astype(BF16)
    wsmt_b = wsm.transpose(0, 2, 1).astype(BF16)
    bias = jnp.repeat(ws["gmlp_b_s"].T, GMLP_HEAD_DIM, axis=1)

    tf = min(512, t)
    if dist:
        names = ("ffn1_w_in",)
        wb.update(zip(names, _run_exchange(gather(names), "gather_ffn1_in")))
    names = ("ffn1_w_out", "mix_w_in")
    (x0b, h1, a1), got = _ffn_proj(x0, wb["ffn1_w_in"], tf, "ffn1_proj", gather(names))
    wb.update(zip(names, got))
    names = ("ssm_glu_w", "up_a", "up_b", "mix_w_out")
    (x1, xh1, rstd1), got = _ffn_out(x0, a1, wb["ffn1_w_out"], row(ws["ln1_g"]), row(ws["ln1_b"]), tf,
                                     "ffn1_out", gather(names))
    wb.update(zip(names, got))
    sp = _s5_setup(ws["ssm_lambda_re"], ws["ssm_lambda_im"], ws["ssm_log_dt"], ws["ssm_b_re"],
                   ws["ssm_b_im"], ws["ssm_c_re"], ws["ssm_c_im"], ws["ssm_d"], wb["ssm_glu_w"],
                   ws["ssm_glu_b"], tb)
    names = ("ffn2_w_out",)
    (x1b, za, zuv, gab), got = _mixin_fwd(x1, wb["mix_w_in"], tm, gather(names))
    wb.update(zip(names, got))
    names = ("ffn2_w_in",)
    (s5o, y2p, carries), got = _s5_fwd(za, sp, bsz, seq, tb, gather(names))
    wb.update(zip(names, got))
    names = ("ple_w_gate", "ple_w_proj")
    (gm,), got = _gmlp_fwd(zuv, row(ws["gmlp_ln_g"]), row(ws["gmlp_ln_b"]), wsm_b, bias, gather(names))
    wb.update(zip(names, got))
    (x2, xh2, rstd2), _ = _mixout_fwd(x1, s5o, gm, gab, wb["up_a"], wb["up_b"], wb["mix_w_out"],
                                           row(ws["ln2_g"]), row(ws["ln2_b"]), tm)
    (x2b, h2, a2), _ = _ffn_proj(x2, wb["ffn2_w_in"], tf, "ffn2_proj")
    (x3, xh3, rstd3), _ = _ffn_out(x2, a2, wb["ffn2_w_out"], row(ws["ln3_g"]), row(ws["ln3_b"]), tf, "ffn2_out")
    (dx3, x3b, pb, dq, de, loss_rows), _ = _ple_loss(x3, p0, tg, wb["ple_w_gate"], wb["ple_w_proj"], tm)
    (dx2, dh2, df2, gs["ln3_g"], gs["ln3_b"]), _ = _ffn_bwd(
        dx3, xh3, rstd3, h2, wb["ffn2_w_in"], wb["ffn2_w_out"], row(ws["ln3_g"]), tm, "ffn2_bwd")
    (dx1a, dmx, mb, dya, dyb, ds5, dgm, dgab, gs["ln2_g"], gs["ln2_b"]), _ = _mixout_bwd(
        dx2, xh2, rstd2, s5o, gm, gab, wb["up_a"], wb["up_b"], wb["mix_w_out"], row(ws["ln2_g"]), tm)
    (dza, dmr, dmi, dnr, dni, da, ddsk, dgw, dgb), _ = _s5_bwd(za, y2p, ds5, carries, sp, bsz, seq, tb)
    gb["ssm_glu_w"] = (dgw, dgw.astype(BF16))
    et = exchange(scat=("ssm_glu_w",))
    (dzuv, dws, dbias, gs["gmlp_ln_g"], gs["gmlp_ln_b"]), got = _gmlp_bwd(
        zuv, dgm, row(ws["gmlp_ln_g"]), row(ws["gmlp_ln_b"]), wsm_b, wsmt_b, bias, et[0])
    take(et, got)
    et = exchange(swap=("ssm_glu_w",))
    (dx1,), got = _mixin_bwd(dx1a, dza, dzuv, dgab, wb["mix_w_in"], tm, et[0])
    take(et, got)
    (dx0, dh1, df1, gs["ln1_g"], gs["ln1_b"]), _ = _ffn_bwd(
        dx1, xh1, rstd1, h1, wb["ffn1_w_in"], wb["ffn1_w_out"], row(ws["ln1_g"]), tm, "ffn1_bwd")
    grad_x = dx0.reshape(bsz, seq, D_MODEL)

    d_abr = da[0].sum(axis=0).reshape(SSM_GROUPS, SSM_STATE)
    d_abi = da[1].sum(axis=0).reshape(SSM_GROUPS, SSM_STATE)
    _, vjp = jax.vjp(_s5_discretise, ws["ssm_lambda_re"], ws["ssm_lambda_im"], ws["ssm_log_dt"],
                     ws["ssm_b_re"], ws["ssm_b_im"])
    (gs["ssm_lambda_re"], gs["ssm_lambda_im"], gs["ssm_log_dt"], gs["ssm_b_re"], gs["ssm_b_im"]) = vjp(
        (d_abr, d_abi, _block_diag_in_t(dmr), _block_diag_in_t(dmi)))
    gs["ssm_c_re"] = _block_diag_out_t(dnr)
    gs["ssm_c_im"] = _block_diag_out_t(dni)
    gs["ssm_d"] = ddsk
    gs["ssm_glu_b"] = dgb
    gs["gmlp_w_s"] = dws
    gs["gmlp_b_s"] = dbias.reshape(CHUNK, GMLP_HEADS, GMLP_HEAD_DIM).sum(axis=-1).T
    gs["loss_rows"] = loss_rows

    def mix_in_grad(bg):
        g_mi, _ = _tn_matmul(x1b, dza, "dw_mix_in_a", 1024, 512, 0, 3584)
        g_mi, _ = _tn_matmul(x1b, dzuv, "dw_mix_in_uv", 1024, 512, 1, 3584, g_mi)
        return _tn_matmul(x1b, dgab, "dw_mix_in_g", 1024, 512, 3, 3584, g_mi, bg=bg)

    def last_piece(q):
        return lambda bg: _tn_matmul(x0b, dh1, "dw_" + LAST_PIECE % q, D_MODEL // LAST_PIECES, 1408, bg=bg,
                                     a_cols=(q, 1))

    jobs = [("ffn1_w_out", lambda bg: _tn_matmul(a1, df1, "dw_ffn1_out", 1408, 1024, bg=bg)),
            ("ffn2_w_in", lambda bg: _tn_matmul(x2b, dh2, "dw_ffn2_in", 1024, 1408, bg=bg)),
            (LAST_PIECE % 0, last_piece(0)),
            ("mix_w_in", mix_in_grad),
            (LAST_PIECE % 1, last_piece(1)),
            ("ffn2_w_out", lambda bg: _tn_matmul(a2, df2, "dw_ffn2_out", 1408, 1024, bg=bg)),
            ("mix_w_out", lambda bg: _tn_matmul(mb, dmx, "dw_mix_out", 1024, 1024, bg=bg)),
            ("ple_w_gate", lambda bg: _tn_matmul(x3b, dq, "dw_ple_gate", 1024, 1024, bg=bg)),
            ("up_a", lambda bg: _tn_matmul(s5o, dya, "dw_up_a", 512, 1024, bg=bg)),
            ("up_b", lambda bg: _tn_matmul(gm, dyb, "dw_up_b", 512, 1024, bg=bg)),
            ("ple_w_proj", lambda bg: _tn_matmul(pb, de, "dw_ple_proj", 256, 1024, bg=bg))]
    assert LAST_PIECES == 2
    if not dist:
        for k, job in jobs:
            gb[k], _ = job(None)
        gb["ffn1_w_in"], _ = _tn_matmul(x0b, dh1, "dw_ffn1_in", 1024, 1408)
        return loss_rows, grad_x, gb, {k: gs[k].reshape(small_shape[k]) for k in SMALL}, sums, other, gathered, None
    small = SMALL + ("loss_rows",)
    names = [k for k, _ in jobs]
    for i in range(len(jobs) + 3):
        stage = lambda d: tuple(names[i - d:i - d + 1]) if 0 <= i - d < len(names) else ()
        extra = (_gather_small([gs[k].reshape(small_shape[k]) for k in small]), gathered, small) if i == 1 else None
        et = exchange(halves=stage(1), scat2=stage(2), swap2=stage(3), extra=extra)
        if i < len(jobs):
            gb[names[i]], got = jobs[i][1](et[0])
        else:
            got = _run_exchange(et[0], "reduce_last_%d" % (i - len(jobs)))
        take(et, got)
    return loss_rows, grad_x, gb, gs, sums, other, gathered, ids


def _adamw(w, g, m, v):
    m = ADAM_B1 * m + (1.0 - ADAM_B1) * g
    v = ADAM_B2 * v + (1.0 - ADAM_B2) * (g * g)
    m_hat = m / ADAM_C1
    v_hat = v / ADAM_C2
    delta = -ADAM_LR * (m_hat / (jnp.sqrt(v_hat) + ADAM_EPS) + ADAM_WD * w)
    return delta, m, v


def _sum_blocks(part, recv, shape, axis, chip, name):
    r, c = shape
    rb = r // 8

    def body(chip_ref, p_ref, r_ref, o_ref):
        o_ref[...] = (p_ref[...] + r_ref[0].astype(F32) + r_ref[1].astype(F32) + r_ref[2].astype(F32))

    if axis == 0:
        own = pl.BlockSpec((rb, c), lambda i, k: (k[0] * 8 + i, 0))
    else:
        own = pl.BlockSpec((rb, c), lambda i, k: (i, k[0]))
    grid_spec = pltpu.PrefetchScalarGridSpec(
        num_scalar_prefetch=1, grid=(8,),
        in_specs=[own, pl.BlockSpec((3, rb, c), lambda i, k: (0, i, 0))],
        out_specs=pl.BlockSpec((rb, c), lambda i, k: (i, 0)))
    return pl.pallas_call(body, name=name, out_shape=SDS((r, c), F32), grid_spec=grid_spec,
                          compiler_params=_params(("parallel",)))(chip, part, recv)


def _presum(part, half, shape, axis, ids, name):
    r, c = shape
    rb = r // 4

    def body(ids_ref, p_ref, h_ref, of_ref, ob_ref):
        s = p_ref[...] + h_ref[...].astype(F32)
        ob_ref[...] = s.astype(BF16)

        @pl.when(pl.program_id(1) == 0)
        def _():
            of_ref[...] = s

    if axis == 0:
        p_spec = pl.BlockSpec((rb, c), lambda i, t, ids: (ids[t] * 4 + ids[4] * 2 + i, 0))
        h_spec = pl.BlockSpec((None, rb, c), lambda i, t, ids: (ids[t], i, 0))
    else:
        p_spec = pl.BlockSpec((rb, c), lambda i, t, ids: (ids[4] * 2 + i, ids[t]))
        h_spec = pl.BlockSpec((rb, c), lambda i, t, ids: (i, ids[t]))
    grid_spec = pltpu.PrefetchScalarGridSpec(
        num_scalar_prefetch=1, grid=(2, 4), in_specs=[p_spec, h_spec],
        out_specs=(pl.BlockSpec((rb, c), lambda i, t, ids: (i, 0)),
                   pl.BlockSpec((None, rb, c), lambda i, t, ids: (t, i, 0))))
    return pl.pallas_call(body, name=name, out_shape=(SDS((r // 2, c), F32), SDS((4, r // 2, c), BF16)),
                          grid_spec=grid_spec, compiler_params=_params(("parallel", "arbitrary")))(ids, part, half)


def _sum_half(pre, recv, name):
    hr, c = pre.shape
    rb = hr // 2

    def body(p_ref, r_ref, o_ref):
        o_ref[...] = (p_ref[...] + r_ref[0].astype(F32) + r_ref[1].astype(F32) + r_ref[2].astype(F32))

    spec = pl.BlockSpec((rb, c), lambda i: (i, 0))
    return pl.pallas_call(body, name=name, grid=(2,), out_shape=SDS((hr, c), F32),
                          in_specs=[spec, pl.BlockSpec((3, rb, c), lambda i: (0, i, 0))], out_specs=spec,
                          compiler_params=_params(("parallel",)))(pre, recv)


def _adam_halves(w, mine, oth, m, v, ids, name, piece=0, prev=None):
    r, c = w.shape
    rb = mine.shape[0] // 2

    def body(ids_ref, w_ref, a_ref, b_ref, m_ref, v_ref, *rest):
        g_ref, d_ref, nm_ref, nv_ref = rest[-4:]
        g = jnp.where(pl.program_id(0) // 2 == ids_ref[4], a_ref[...], b_ref[...])
        g_ref[...] = g
        d_ref[...], nm_ref[...], nv_ref[...] = _adamw(w_ref[...], g, m_ref[...], v_ref[...])

    whole = pl.BlockSpec((rb, c), lambda i, ids: (i + 4 * piece, 0))
    part = pl.BlockSpec((rb, c), lambda i, ids: (i % 2, 0))
    in_specs = [whole, part, part, whole, whole]
    args = [w, mine, oth, m, v]
    aliases = {}
    if prev is not None:
        in_specs += [pl.BlockSpec(memory_space=pl.ANY)] * 4
        args += list(prev)
        aliases = {6: 0, 7: 1, 8: 2, 9: 3}
    grid_spec = pltpu.PrefetchScalarGridSpec(num_scalar_prefetch=1, grid=(4,), in_specs=in_specs,
                                             out_specs=(whole,) * 4)
    return pl.pallas_call(body, name=name, out_shape=tuple(SDS((r, c), F32) for _ in range(4)),
                          grid_spec=grid_spec, input_output_aliases=aliases,
                          compiler_params=_params(("parallel",)))(ids, *args)


def _adam_big(w, ga, gb, m, v, name, piece=0, prev=None):
    r, c = w.shape
    pr = ga.shape[0]
    steps = 8 if pr == r else 2
    rb = pr // steps
    off = piece * steps

    def body(w_ref, ga_ref, gb_ref, m_ref, v_ref, *rest):
        g_ref, d_ref, nm_ref, nv_ref = rest[-4:]
        g = ga_ref[...] + gb_ref[...]
        g_ref[...] = g
        d_ref[...], nm_ref[...], nv_ref[...] = _adamw(w_ref[...], g, m_ref[...], v_ref[...])

    whole = pl.BlockSpec((rb, c), lambda i: (i + off, 0))
    part = pl.BlockSpec((rb, c), lambda i: (i, 0))
    in_specs = [whole, part, part, whole, whole]
    args = [w, ga, gb, m, v]
    aliases = {}
    if prev is not None:
        in_specs += [pl.BlockSpec(memory_space=pl.ANY)] * 4
        args += list(prev)
        aliases = {5: 0, 6: 1, 7: 2, 8: 3}
    return pl.pallas_call(
        body, name=name, grid=(steps,), out_shape=tuple(SDS((r, c), F32) for _ in range(4)),
        in_specs=in_specs, out_specs=(whole,) * 4, input_output_aliases=aliases,
        compiler_params=_params(("parallel",)),
    )(*args)


def _adam_small(ws, gathered, ms, vs):
    n = len(ws)

    def body(*refs):
        w_refs, g_refs, m_refs, v_refs = refs[:n], refs[n:2 * n], refs[2 * n:3 * n], refs[3 * n:4 * n]
        outs = refs[4 * n:]
        for i in range(n):
            g = g_refs[i][0]
            for d in range(1, N_DEV):
                g = g + g_refs[i][d]
            delta, nm, nv = _adamw(w_refs[i][...], g, m_refs[i][...], v_refs[i][...])
            outs[i][...] = g
            outs[n + i][...] = delta
            outs[2 * n + i][...] = nm
            outs[3 * n + i][...] = nv

    vmem = pl.BlockSpec(memory_space=pltpu.VMEM)
    shapes = [w.shape for w in ws]
    return pl.pallas_call(
        body, name="adam_small", out_shape=tuple(SDS(s, F32) for s in shapes * 4),
        in_specs=[vmem] * (4 * n), out_specs=tuple([vmem] * (4 * n)),
        compiler_params=pltpu.CompilerParams(vmem_limit_bytes=VMEM_LIMIT_BYTES),
    )(*ws, *gathered, *ms, *vs)


def _sum_loss(gathered):
    def body(g_ref, o_ref):
        tot = g_ref[0]
        for d in range(1, N_DEV):
            tot = tot + g_ref[d]
        o_ref[...] = (0.5 / D_MODEL) * jnp.sum(tot, axis=1, keepdims=True)

    vmem = pl.BlockSpec(memory_space=pltpu.VMEM)
    return pl.pallas_call(body, name="sum_loss", out_shape=SDS((1, 1), F32), in_specs=[vmem],
                          out_specs=vmem)(gathered)


def kernel(x, p, ffn1_w_in, ffn1_w_out, ln1_g, ln1_b, mix_w_in, ssm_lambda_re, ssm_lambda_im, ssm_log_dt, ssm_b_re, ssm_b_im, ssm_c_re, ssm_c_im, ssm_d, ssm_glu_w, ssm_glu_b, gmlp_ln_g, gmlp_ln_b, gmlp_w_s, gmlp_b_s, up_a, up_b, mix_w_out, ln2_g, ln2_b, ffn2_w_in, ffn2_w_out, ln3_g, ln3_b, ple_w_proj, ple_w_gate, loss_target, m_ffn1_w_in, m_ffn1_w_out, m_ln1_g, m_ln1_b, m_mix_w_in, m_ssm_lambda_re, m_ssm_lambda_im, m_ssm_log_dt, m_ssm_b_re, m_ssm_b_im, m_ssm_c_re, m_ssm_c_im, m_ssm_d, m_ssm_glu_w, m_ssm_glu_b, m_gmlp_ln_g, m_gmlp_ln_b, m_gmlp_w_s, m_gmlp_b_s, m_up_a, m_up_b, m_mix_w_out, m_ln2_g, m_ln2_b, m_ffn2_w_in, m_ffn2_w_out, m_ln3_g, m_ln3_b, m_ple_w_proj, m_ple_w_gate, v_ffn1_w_in, v_ffn1_w_out, v_ln1_g, v_ln1_b, v_mix_w_in, v_ssm_lambda_re, v_ssm_lambda_im, v_ssm_log_dt, v_ssm_b_re, v_ssm_b_im, v_ssm_c_re, v_ssm_c_im, v_ssm_d, v_ssm_glu_w, v_ssm_glu_b, v_gmlp_ln_g, v_gmlp_ln_b, v_gmlp_w_s, v_gmlp_b_s, v_up_a, v_up_b, v_mix_w_out, v_ln2_g, v_ln2_b, v_ffn2_w_in, v_ffn2_w_out, v_ln3_g, v_ln3_b, v_ple_w_proj, v_ple_w_gate):
    given = dict(locals())
    order = ("ffn1_w_in", "ffn1_w_out", "ln1_g", "ln1_b", "mix_w_in", "ssm_lambda_re", "ssm_lambda_im",
             "ssm_log_dt", "ssm_b_re", "ssm_b_im", "ssm_c_re", "ssm_c_im", "ssm_d", "ssm_glu_w", "ssm_glu_b",
             "gmlp_ln_g", "gmlp_ln_b", "gmlp_w_s", "gmlp_b_s", "up_a", "up_b", "mix_w_out", "ln2_g", "ln2_b",
             "ffn2_w_in", "ffn2_w_out", "ln3_g", "ln3_b", "ple_w_proj", "ple_w_gate")
    assert set(order) == set(BIG + SMALL)

    shard = {k: given[k][0] for k in BIG}
    shard_b = {k: shard[k].astype(BF16) for k in BIG}
    loss_rows, grad_x, gb, gs, sums, other, gathered, ids = _local_step(
        x, given["p"][0], loss_target, {}, {k: given[k] for k in SMALL}, shard_b)

    out = {}
    for k in BIG:
        moments = (given["m_" + k][0], given["v_" + k][0])
        if k == "ssm_glu_w":
            out[k] = _adam_big(shard[k], sums[k], other[k], *moments, "adam_" + k)
        elif k == "ffn1_w_in":
            for q in range(LAST_PIECES):
                kq = LAST_PIECE % q
                out[k] = _adam_halves(shard[k], sums[kq], other[kq], *moments, ids, "adam_" + kq, q, out.get(k))
        else:
            out[k] = _adam_halves(shard[k], sums[k], other[k], *moments, ids, "adam_" + k)

    res = _adam_small([_small_view(k, given[k]) for k in SMALL], [gathered[k] for k in SMALL],
                      [_small_view(k, given["m_" + k]) for k in SMALL],
                      [_small_view(k, given["v_" + k]) for k in SMALL])
    ns = len(SMALL)
    for i, k in enumerate(SMALL):
        out[k] = tuple(res[j * ns + i].reshape(given[k].shape) for j in range(4))
    loss = _sum_loss(gathered["loss_rows"]).reshape(())

    lead = lambda k, j: out[k][j][None] if k in BIG else out[k][j]
    return (loss, grad_x, *[lead(k, 0) for k in order], *[lead(k, 1) for k in order],
            *[lead(k, 2) for k in order], *[lead(k, 3) for k in order])
```

```python
import math

import jax
import jax.numpy as jnp
from jax import lax
from jax.experimental import pallas as pl
from jax.experimental.pallas import tpu as pltpu

F32 = jnp.float32
BF16 = jnp.bfloat16
MESH = pl.DeviceIdType.MESH
SDS = jax.ShapeDtypeStruct

D_MODEL = 1024
D_FF = 2816
D_SSM = 512
D_GMLP = 512
SSM_GROUPS = 32
SSM_GROUP_CH = 16
SSM_STATE = 64
SSM_LANES = SSM_GROUPS * SSM_STATE
GMLP_HEADS = 8
GMLP_HEAD_DIM = 64
CHUNK = 128
PLE_DIM = 256
LN_EPS = 1e-5
ALPHA = 2.0 ** 0.25

ADAM_LR = 0.001
ADAM_B1 = 0.9
ADAM_B2 = 0.999
ADAM_EPS = 1e-08
ADAM_WD = 0.01
ADAM_STEP = 10
ADAM_C1 = 1.0 - ADAM_B1 ** ADAM_STEP
ADAM_C2 = 1.0 - ADAM_B2 ** ADAM_STEP

N_DEV = 8
VMEM_LIMIT_BYTES = 56 * 1024 * 1024
FFN_COLS = 1408
S5_BLOCKS = 4
S5_BLOCK_IN = D_SSM // S5_BLOCKS
S5_BLOCK_ST = SSM_LANES // S5_BLOCKS
SCAN_LANES = 512
TN_K_BLOCK = 2048
DIRECT_GATHER_BYTES = 0
LAST_PIECES = 2
LAST_PIECE = "ffn1_w_in_q%d"
_G0 = math.sqrt(2.0 / math.pi)
_G1 = 0.044715


def _dot(a, b):
    return jnp.dot(a, b, preferred_element_type=F32)


def _dot_nt(a, b):
    return lax.dot_general(a, b, (((1,), (1,)), ((), ())), preferred_element_type=F32)


def _dot_tn(a, b):
    return lax.dot_general(a, b, (((0,), (0,)), ((), ())), preferred_element_type=F32)


def _sigmoid(x):
    return 1.0 / (1.0 + jnp.exp(-x))


def _gelu(x):
    t = jnp.tanh(_G0 * (x + _G1 * x * x * x))
    return 0.5 * x * (1.0 + t)


def _gelu_grad(x):
    t = jnp.tanh(_G0 * (x + _G1 * x * x * x))
    return 0.5 * (1.0 + t) + 0.5 * x * (1.0 - t * t) * _G0 * (1.0 + 3.0 * _G1 * x * x)


def _ln_fwd(r, g, b):
    mu = jnp.mean(r, axis=-1, keepdims=True)
    d = r - mu
    var = jnp.mean(d * d, axis=-1, keepdims=True)
    rstd = lax.rsqrt(var + LN_EPS)
    xh = d * rstd
    return xh * g + b, xh, rstd


def _ln_bwd(dy, xh, rstd, g):
    dxh = dy * g
    m1 = jnp.mean(dxh, axis=-1, keepdims=True)
    m2 = jnp.mean(dxh * xh, axis=-1, keepdims=True)
    return rstd * (dxh - m1 - xh * m2)


def _resident(shape):
    nd = len(shape)
    return pl.BlockSpec(shape, lambda *_: (0,) * nd, pipeline_mode=pl.Buffered(1))


def _fixed(shape):
    nd = len(shape)
    return pl.BlockSpec(shape, lambda *_: (0,) * nd)


def _rows(tm, cols):
    return pl.BlockSpec((tm, cols), lambda i: (i, 0))


def _params(sem):
    return pltpu.CompilerParams(dimension_semantics=sem, vmem_limit_bytes=VMEM_LIMIT_BYTES)


class _Exchange:
    def __init__(self, args, out_shape, sems, start, finish):
        self.args, self.out_shape, self.sems = list(args), list(out_shape), list(sems)
        self.start, self.finish = start, finish
        self.cuts = [(0, len(self.out_shape))]


def _call(body, name, grid, in_specs, out_specs, out_shape, args, scratch=(), sem=None, bg=None, aliases=None):
    aliases = {} if aliases is None else aliases
    if bg is None:
        res = pl.pallas_call(body, name=name, grid=grid, out_shape=tuple(out_shape), in_specs=list(in_specs),
                             out_specs=tuple(out_specs), scratch_shapes=list(scratch),
                             input_output_aliases=aliases, compiler_params=_params(sem))(*args)
        return tuple(res), ()
    n_in, n_out, n_bi, n_bo, n_sc = len(args), len(out_shape), len(bg.args), len(bg.out_shape), len(scratch)

    def wrapped(*refs):
        ins = refs[:n_in]
        b_ins = refs[n_in:n_in + n_bi]
        outs = refs[n_in + n_bi:n_in + n_bi + n_out]
        b_outs = refs[n_in + n_bi + n_out:n_in + n_bi + n_out + n_bo]
        rest = refs[n_in + n_bi + n_out + n_bo:]
        scr, b_sems = rest[:n_sc], rest[n_sc:]
        first = pl.program_id(0) == 0
        last = pl.program_id(0) == grid[0] - 1
        for ax in range(1, len(grid)):
            first = jnp.logical_and(first, pl.program_id(ax) == 0)
            last = jnp.logical_and(last, pl.program_id(ax) == grid[ax] - 1)

        @pl.when(first)
        def _():
            bg.start(b_ins, b_outs, b_sems)

        body(*ins, *outs, *scr)

        @pl.when(last)
        def _():
            bg.finish(b_ins, b_outs, b_sems)

    any_spec = pl.BlockSpec(memory_space=pl.ANY)
    res = pl.pallas_call(
        wrapped, name=name, grid=grid, out_shape=tuple(out_shape) + tuple(bg.out_shape),
        in_specs=list(in_specs) + [any_spec] * n_bi, out_specs=tuple(out_specs) + (any_spec,) * n_bo,
        scratch_shapes=list(scratch) + list(bg.sems), input_output_aliases=aliases,
        compiler_params=_params(tuple("arbitrary" for _ in grid)))(*args, *bg.args)
    return tuple(res[:n_out]), tuple(res[n_out:])


def _run_exchange(ex, name):
    n_i, n_o = len(ex.args), len(ex.out_shape)

    def body(*refs):
        ins, outs, sems = refs[:n_i], refs[n_i:n_i + n_o], refs[n_i + n_o:]
        ex.start(ins, outs, sems)
        ex.finish(ins, outs, sems)

    any_spec = pl.BlockSpec(memory_space=pl.ANY)
    return tuple(pl.pallas_call(body, name=name, out_shape=tuple(ex.out_shape), in_specs=[any_spec] * n_i,
                                out_specs=(any_spec,) * n_o, scratch_shapes=list(ex.sems))(*ex.args))


def _join(exchanges):
    cuts = []
    a = o = q = 0
    for e in exchanges:
        cuts.append((a, a + len(e.args), o, o + len(e.out_shape), q, q + len(e.sems)))
        a, o, q = cuts[-1][1], cuts[-1][3], cuts[-1][5]

    def start(ins, outs, sems):
        for e, (a0, a1, o0, o1, q0, q1) in zip(exchanges, cuts):
            e.start(ins[a0:a1], outs[o0:o1], sems[q0:q1])

    def finish(ins, outs, sems):
        for e, (a0, a1, o0, o1, q0, q1) in zip(exchanges, cuts):
            e.finish(ins[a0:a1], outs[o0:o1], sems[q0:q1])

    joined = _Exchange(sum((e.args for e in exchanges), []), sum((e.out_shape for e in exchanges), []),
                       sum((e.sems for e in exchanges), []), start, finish)
    joined.cuts = [(c[2], c[3]) for c in cuts]
    return joined


def _ffn_proj(x, w_in, tm, name, bg=None):
    t = x.shape[0]
    nch = D_FF // FFN_COLS

    def body(x_ref, win_ref, xb_ref, h_ref, a_ref):
        xb = x_ref[...].astype(BF16)
        xb_ref[...] = xb
        for k in range(nch):
            cg = slice(k * FFN_COLS, (k + 1) * FFN_COLS)
            cu = slice(D_FF + k * FFN_COLS, D_FF + (k + 1) * FFN_COLS)
            hg = _dot(xb, win_ref[k])
            hu = _dot(xb, win_ref[nch + k])
            h_ref[:, cg] = hg.astype(BF16)
            h_ref[:, cu] = hu.astype(BF16)
            a_ref[:, cg] = (hg * _sigmoid(hg) * hu).astype(BF16)

    return _call(
        body, name, (t // tm,),
        [_rows(tm, D_MODEL), _resident((2 * nch, D_MODEL, FFN_COLS))],
        (_rows(tm, D_MODEL), _rows(tm, 2 * D_FF), _rows(tm, D_FF)),
        (SDS((t, D_MODEL), BF16), SDS((t, 2 * D_FF), BF16), SDS((t, D_FF), BF16)),
        (x, w_in), sem=("parallel",), bg=bg)


def _ffn_out(x, a, w_out, g, b, tm, name, bg=None):
    t = x.shape[0]

    def body(x_ref, a_ref, wout_ref, g_ref, b_ref, xn_ref, xh_ref, rstd_ref):
        f = _dot(a_ref[...], wout_ref[...])
        y, xh, rstd = _ln_fwd(ALPHA * x_ref[...] + 0.5 * f, g_ref[...], b_ref[...])
        xn_ref[...] = y
        xh_ref[...] = xh
        rstd_ref[...] = rstd

    return _call(
        body, name, (t // tm,),
        [_rows(tm, D_MODEL), _rows(tm, D_FF), _resident((D_FF, D_MODEL)), _fixed((1, D_MODEL)), _fixed((1, D_MODEL))],
        (_rows(tm, D_MODEL), _rows(tm, D_MODEL), _rows(tm, 1)),
        (SDS((t, D_MODEL), F32), SDS((t, D_MODEL), F32), SDS((t, 1), F32)),
        (x, a, w_out, g, b), sem=("parallel",), bg=bg)


def _ffn_bwd(dxn, xh, rstd, h, w_in, w_out, g, tm, name, bg=None):
    t = dxn.shape[0]
    nch = D_FF // FFN_COLS

    def body(dxn_ref, xh_ref, rstd_ref, h_ref, win_ref, wout_ref, g_ref,
             dx_ref, dh_ref, df_ref, dg_ref, db_ref):
        @pl.when(pl.program_id(0) == 0)
        def _():
            dg_ref[...] = jnp.zeros_like(dg_ref)
            db_ref[...] = jnp.zeros_like(db_ref)

        dy = dxn_ref[...]
        xhv = xh_ref[...]
        dr = _ln_bwd(dy, xhv, rstd_ref[...], g_ref[...])
        dg_ref[...] += jnp.sum(dy * xhv, axis=0, keepdims=True)
        db_ref[...] += jnp.sum(dy, axis=0, keepdims=True)
        df = (0.5 * dr).astype(BF16)
        df_ref[...] = df
        dx = ALPHA * dr
        for k in range(nch):
            cg = slice(k * FFN_COLS, (k + 1) * FFN_COLS)
            cu = slice(D_FF + k * FFN_COLS, D_FF + (k + 1) * FFN_COLS)
            hg = h_ref[:, cg].astype(F32)
            hu = h_ref[:, cu].astype(F32)
            sg = _sigmoid(hg)
            silu = hg * sg
            da = _dot_nt(df, wout_ref[cg, :])
            dhu = (da * silu).astype(BF16)
            dhg = (da * hu * (sg * (1.0 + hg * (1.0 - sg)))).astype(BF16)
            dh_ref[:, cg] = dhg
            dh_ref[:, cu] = dhu
            dx = dx + _dot_nt(dhg, win_ref[k]) + _dot_nt(dhu, win_ref[nch + k])
        dx_ref[...] = dx

    return _call(
        body, name, (t // tm,),
        [_rows(tm, D_MODEL), _rows(tm, D_MODEL), _rows(tm, 1), _rows(tm, 2 * D_FF),
         _resident((2 * nch, D_MODEL, FFN_COLS)), _resident((D_FF, D_MODEL)), _fixed((1, D_MODEL))],
        (_rows(tm, D_MODEL), _rows(tm, 2 * D_FF), _rows(tm, D_MODEL),
         _fixed((1, D_MODEL)), _fixed((1, D_MODEL))),
        (SDS((t, D_MODEL), F32), SDS((t, 2 * D_FF), BF16), SDS((t, D_MODEL), BF16),
         SDS((1, D_MODEL), F32), SDS((1, D_MODEL), F32)),
        (dxn, xh, rstd, h, w_in, w_out, g), sem=("arbitrary",), bg=bg)


def _tn_matmul(a, b, name, bm, bn, col_block=0, total_cols=None, prev=None, bg=None, a_cols=None):
    t, m = a.shape
    a_first = 0
    if a_cols is not None:
        a_first, m = a_cols[0], a_cols[1] * bm
    n = b.shape[1]
    total_cols = n if total_cols is None else total_cols
    bk = min(TN_K_BLOCK, t)
    nk = t // bk
    n_in = 2 if prev is None else 4

    def body(*refs):
        a_ref, b_ref = refs[0], refs[1]
        o_ref, ob_ref = refs[n_in], refs[n_in + 1]
        k = pl.program_id(2)

        @pl.when(k == 0)
        def _():
            o_ref[...] = jnp.zeros_like(o_ref)

        o_ref[...] += _dot_tn(a_ref[...], b_ref[...])

        @pl.when(k == nk - 1)
        def _():
            ob_ref[...] = o_ref[...].astype(BF16)

    in_specs = [pl.BlockSpec((bk, bm), lambda i, j, k: (k, i + a_first)),
                pl.BlockSpec((bk, bn), lambda i, j, k: (k, j))]
    args = [a, b]
    aliases = {}
    if prev is not None:
        in_specs += [pl.BlockSpec(memory_space=pl.ANY), pl.BlockSpec(memory_space=pl.ANY)]
        args += list(prev)
        aliases = {2: 0, 3: 1}
    out_spec = pl.BlockSpec((bm, bn), lambda i, j, k: (i, j + col_block))
    return _call(body, name, (m // bm, n // bn, nk), in_specs, (out_spec, out_spec),
                 (SDS((m, total_cols), F32), SDS((m, total_cols), BF16)), args,
                 sem=("parallel", "parallel", "arbitrary"), bg=bg, aliases=aliases)


def _mixin_fwd(x1, w, tm, bg=None):
    t = x1.shape[0]

    def body(x_ref, w_ref, xb_ref, za_ref, zuv_ref, gab_ref):
        xb = x_ref[...].astype(BF16)
        xb_ref[...] = xb
        za_ref[...] = _dot(xb, w_ref[:, 0:512]).astype(BF16)
        zuv_ref[...] = _dot(xb, w_ref[:, 512:1536]).astype(BF16)
        gab_ref[...] = _dot(xb, w_ref[:, 1536:3584]).astype(BF16)

    return _call(
        body, "mixin_fwd", (t // tm,),
        [_rows(tm, D_MODEL), _resident((D_MODEL, 3584))],
        (_rows(tm, D_MODEL), _rows(tm, 512), _rows(tm, 1024), _rows(tm, 2048)),
        (SDS((t, D_MODEL), BF16), SDS((t, 512), BF16), SDS((t, 1024), BF16), SDS((t, 2048), BF16)),
        (x1, w), sem=("parallel",), bg=bg)


def _mixin_bwd(dx1a, dza, dzuv, dgab, w, tm, bg=None):
    t = dx1a.shape[0]

    def body(d_ref, dza_ref, dzuv_ref, dgab_ref, w_ref, dx_ref):
        dx_ref[...] = (d_ref[...] + _dot_nt(dza_ref[...], w_ref[:, 0:512])
                       + _dot_nt(dzuv_ref[...], w_ref[:, 512:1536])
                       + _dot_nt(dgab_ref[...], w_ref[:, 1536:3584]))

    return _call(
        body, "mixin_bwd", (t // tm,),
        [_rows(tm, D_MODEL), _rows(tm, 512), _rows(tm, 1024), _rows(tm, 2048), _resident((D_MODEL, 3584))],
        (_rows(tm, D_MODEL),), (SDS((t, D_MODEL), F32),),
        (dx1a, dza, dzuv, dgab, w), sem=("parallel",), bg=bg)


def _unrolled(lo, hi, body, carry):
    for j in range(lo, hi):
        carry = body(j, carry)
    return carry


def _scan_fwd(hr_ref, hi_ref, a_ref, ap_ref, carry_ref, seg, cin_ref):
    for lc in range(SSM_LANES // SCAN_LANES):
        ls = slice(lc * SCAN_LANES, (lc + 1) * SCAN_LANES)
        a_r = jnp.broadcast_to(a_ref[0:1, ls], (8, SCAN_LANES))
        a_i = jnp.broadcast_to(a_ref[1:2, ls], (8, SCAN_LANES))

        def step(j, hc, ls=ls, a_r=a_r, a_i=a_i):
            h_r, h_i = hc
            rows = pl.ds(j * 8, 8)
            n_r = a_r * h_r - a_i * h_i + hr_ref[rows, ls]
            n_i = a_r * h_i + a_i * h_r + hi_ref[rows, ls]
            hr_ref[rows, ls] = n_r
            hi_ref[rows, ls] = n_i
            return n_r, n_i

        zero = jnp.zeros((8, SCAN_LANES), F32)
        f_r, f_i = _unrolled(0, seg, step, (zero, zero))
        c_r = carry_ref[0:1, ls]
        c_i = carry_ref[1:2, ls]
        p_r = ap_ref[0:1, ls]
        p_i = ap_ref[1:2, ls]
        rows_r, rows_i = [], []
        for s in range(8):
            rows_r.append(c_r)
            rows_i.append(c_i)
            c_r, c_i = (f_r[s:s + 1] + p_r * c_r - p_i * c_i,
                        f_i[s:s + 1] + p_r * c_i + p_i * c_r)
        carry_ref[0:1, ls] = c_r
        carry_ref[1:2, ls] = c_i
        cin_r = jnp.concatenate(rows_r, axis=0)
        cin_i = jnp.concatenate(rows_i, axis=0)
        if cin_ref is not None:
            cin_ref[0, :, ls] = cin_r
            cin_ref[1, :, ls] = cin_i

        def fix(j, cc, ls=ls, a_r=a_r, a_i=a_i):
            c_r, c_i = cc
            c_r, c_i = a_r * c_r - a_i * c_i, a_r * c_i + a_i * c_r
            rows = pl.ds(j * 8, 8)
            hr_ref[rows, ls] = hr_ref[rows, ls] + c_r
            hi_ref[rows, ls] = hi_ref[rows, ls] + c_i
            return c_r, c_i

        _unrolled(0, seg, fix, (cin_r, cin_i))


def _scan_bwd(gr_ref, gi_ref, hr_ref, hi_ref, cin_ref, a_ref, ap_ref, rcarry_ref, da_ref, seg):
    for lc in range(SSM_LANES // SCAN_LANES):
        ls = slice(lc * SCAN_LANES, (lc + 1) * SCAN_LANES)
        a_r = jnp.broadcast_to(a_ref[0:1, ls], (8, SCAN_LANES))
        a_i = jnp.broadcast_to(a_ref[1:2, ls], (8, SCAN_LANES))

        def step(t, gc, ls=ls, a_r=a_r, a_i=a_i):
            g_r, g_i = gc
            rows = pl.ds((seg - 1 - t) * 8, 8)
            n_r = gr_ref[rows, ls] + a_r * g_r + a_i * g_i
            n_i = gi_ref[rows, ls] + a_r * g_i - a_i * g_r
            gr_ref[rows, ls] = n_r
            gi_ref[rows, ls] = n_i
            return n_r, n_i

        zero = jnp.zeros((8, SCAN_LANES), F32)
        f_r, f_i = _unrolled(0, seg, step, (zero, zero))
        c_r = rcarry_ref[0:1, ls]
        c_i = rcarry_ref[1:2, ls]
        p_r = ap_ref[0:1, ls]
        p_i = ap_ref[1:2, ls]
        rows_r, rows_i = [None] * 8, [None] * 8
        for s in range(7, -1, -1):
            rows_r[s] = c_r
            rows_i[s] = c_i
            c_r, c_i = (f_r[s:s + 1] + p_r * c_r + p_i * c_i,
                        f_i[s:s + 1] + p_r * c_i - p_i * c_r)
        rcarry_ref[0:1, ls] = c_r
        rcarry_ref[1:2, ls] = c_i
        cin_r = jnp.concatenate(rows_r, axis=0)
        cin_i = jnp.concatenate(rows_i, axis=0)

        def fix_row(j_rows, hp_r, hp_i, cc, ls=ls, a_r=a_r, a_i=a_i):
            c_r, c_i, acc_r, acc_i = cc
            c_r, c_i = a_r * c_r + a_i * c_i, a_r * c_i - a_i * c_r
            g_r = gr_ref[j_rows, ls] + c_r
            g_i = gi_ref[j_rows, ls] + c_i
            gr_ref[j_rows, ls] = g_r
            gi_ref[j_rows, ls] = g_i
            acc_r = acc_r + g_r * hp_r + g_i * hp_i
            acc_i = acc_i + g_i * hp_r - g_r * hp_i
            return c_r, c_i, acc_r, acc_i

        def fix(t, cc, ls=ls, fix_row=fix_row):
            j = seg - 1 - t
            rows = pl.ds(j * 8, 8)
            prev = pl.ds((j - 1) * 8, 8)
            return fix_row(rows, hr_ref[prev, ls], hi_ref[prev, ls], cc)

        cc = _unrolled(0, seg - 1, fix, (cin_r, cin_i, zero, zero))
        _, _, acc_r, acc_i = fix_row(pl.ds(0, 8), cin_ref[0, :, ls], cin_ref[1, :, ls], cc)
        da_ref[0, :, ls] += acc_r
        da_ref[1, :, ls] += acc_i


def _s5_fwd(za, sp, bsz, seq, tb, bg=None):
    nb = seq // tb
    seg = tb // 8
    t = bsz * seq

    def body(za_ref, perm_ref, permt_ref, mre_ref, mim_ref, nre_ref, nim_ref, a_ref, ap_ref,
             dsk_ref, gw_ref, gb_ref, out_ref, y2_ref, car_ref, hr_ref, hi_ref, carry_ref):
        @pl.when(pl.program_id(1) == 0)
        def _():
            carry_ref[...] = jnp.zeros_like(carry_ref)

        car_ref[0] = carry_ref[...]
        up = _dot(perm_ref[...], za_ref[...])
        upb = up.astype(BF16)
        for bb in range(S5_BLOCKS):
            ub = upb[:, bb * S5_BLOCK_IN:(bb + 1) * S5_BLOCK_IN]
            st = slice(bb * S5_BLOCK_ST, (bb + 1) * S5_BLOCK_ST)
            hr_ref[:, st] = _dot(ub, mre_ref[bb])
            hi_ref[:, st] = _dot(ub, mim_ref[bb])
        _scan_fwd(hr_ref, hi_ref, a_ref, ap_ref, carry_ref, seg, None)
        ys = []
        for bb in range(S5_BLOCKS):
            st = slice(bb * S5_BLOCK_ST, (bb + 1) * S5_BLOCK_ST)
            ys.append(_dot(hr_ref[:, st].astype(BF16), nre_ref[bb])
                      - _dot(hi_ref[:, st].astype(BF16), nim_ref[bb]))
        y2 = jnp.concatenate(ys, axis=1) + dsk_ref[...] * up
        y2_ref[...] = y2
        y3 = _gelu(y2)
        gl = _dot(y3.astype(BF16), gw_ref[...]) + gb_ref[...]
        oa = y3 * _sigmoid(gl)
        out_ref[...] = _dot(permt_ref[...], oa.astype(BF16)).astype(BF16)

    blk = pl.BlockSpec((tb, D_SSM), lambda b, j: (b * nb + j, 0))
    m_shape = (S5_BLOCKS, S5_BLOCK_IN, S5_BLOCK_ST)
    n_shape = (S5_BLOCKS, S5_BLOCK_ST, S5_BLOCK_IN)
    return _call(
        body, "s5_fwd", (bsz, nb),
        [blk, _fixed((tb, tb)), _fixed((tb, tb)), _fixed(m_shape), _fixed(m_shape), _fixed(n_shape),
         _fixed(n_shape), _fixed((2, SSM_LANES)), _fixed((2, SSM_LANES)), _fixed((1, D_SSM)),
         _fixed((D_SSM, D_SSM)), _fixed((1, D_SSM))],
        (blk, blk, pl.BlockSpec((1, 2, SSM_LANES), lambda b, j: (b * nb + j, 0, 0))),
        (SDS((t, D_SSM), BF16), SDS((t, D_SSM), F32), SDS((bsz * nb, 2, SSM_LANES), F32)),
        (za, sp["perm"], sp["permt"], sp["mre"], sp["mim"], sp["nre"], sp["nim"], sp["a"], sp["ap"],
         sp["dskip"], sp["glu_w"], sp["glu_b"]),
        scratch=[pltpu.VMEM((tb, SSM_LANES), F32), pltpu.VMEM((tb, SSM_LANES), F32),
                 pltpu.VMEM((2, SSM_LANES), F32)],
        sem=("arbitrary", "arbitrary"), bg=bg)


def _s5_bwd(za, y2p, doa, carries, sp, bsz, seq, tb, bg=None):
    nb = seq // tb
    seg = tb // 8
    t = bsz * seq

    def body(za_ref, y2_ref, doa_ref, car_ref, perm_ref, permt_ref, mre_ref, mim_ref, mtre_ref, mtim_ref,
             nre_ref, nim_ref, ntre_ref, ntim_ref, a_ref, ap_ref, dsk_ref, gw_ref, gwt_ref, gb_ref,
             dza_ref, dmr_ref, dmi_ref, dnr_ref, dni_ref, da_ref, ddsk_ref, dgw_ref, dgb_ref,
             hr_ref, hi_ref, gr_ref, gi_ref, cin_ref, carry_ref, rcarry_ref):
        first = jnp.logical_and(pl.program_id(0) == 0, pl.program_id(1) == 0)

        @pl.when(first)
        def _():
            for r in (dmr_ref, dmi_ref, dnr_ref, dni_ref, da_ref, ddsk_ref, dgw_ref, dgb_ref):
                r[...] = jnp.zeros_like(r)

        @pl.when(pl.program_id(1) == 0)
        def _():
            rcarry_ref[...] = jnp.zeros_like(rcarry_ref)

        carry_ref[...] = car_ref[0]
        perm = perm_ref[...]
        up = _dot(perm, za_ref[...])
        upb = up.astype(BF16)
        for bb in range(S5_BLOCKS):
            ub = upb[:, bb * S5_BLOCK_IN:(bb + 1) * S5_BLOCK_IN]
            st = slice(bb * S5_BLOCK_ST, (bb + 1) * S5_BLOCK_ST)
            hr_ref[:, st] = _dot(ub, mre_ref[bb])
            hi_ref[:, st] = _dot(ub, mim_ref[bb])
        _scan_fwd(hr_ref, hi_ref, a_ref, ap_ref, carry_ref, seg, cin_ref)

        y2 = y2_ref[...]
        y3 = _gelu(y2)
        y3b = y3.astype(BF16)
        sg = _sigmoid(_dot(y3b, gw_ref[...]) + gb_ref[...])
        d0 = doa_ref[...]
        d_hi = d0.astype(BF16)
        d1 = d0 - d_hi.astype(F32)
        d_mid = d1.astype(BF16)
        d_lo = (d1 - d_mid.astype(F32)).astype(BF16)
        doap = _dot(perm, d_hi) + _dot(perm, d_mid) + _dot(perm, d_lo)
        dgl = doap * y3 * sg * (1.0 - sg)
        dglb = dgl.astype(BF16)
        dy3 = doap * sg + _dot(dglb, gwt_ref[...])
        dgw_ref[...] += _dot_tn(y3b, dglb)
        dgb_ref[...] += jnp.sum(dgl, axis=0, keepdims=True)
        dy2 = dy3 * _gelu_grad(y2)
        ddsk_ref[...] += jnp.sum(dy2 * up, axis=0, keepdims=True)
        dyb = dy2.astype(BF16)
        for bb in range(S5_BLOCKS):
            dyc = dyb[:, bb * S5_BLOCK_IN:(bb + 1) * S5_BLOCK_IN]
            st = slice(bb * S5_BLOCK_ST, (bb + 1) * S5_BLOCK_ST)
            gr_ref[:, st] = _dot(dyc, ntre_ref[bb])
            gi_ref[:, st] = -_dot(dyc, ntim_ref[bb])
            dnr_ref[bb] += _dot_tn(hr_ref[:, st].astype(BF16), dyc)
            dni_ref[bb] += -_dot_tn(hi_ref[:, st].astype(BF16), dyc)
        _scan_bwd(gr_ref, gi_ref, hr_ref, hi_ref, cin_ref, a_ref, ap_ref, rcarry_ref, da_ref, seg)
        dus = []
        for bb in range(S5_BLOCKS):
            st = slice(bb * S5_BLOCK_ST, (bb + 1) * S5_BLOCK_ST)
            grb = gr_ref[:, st].astype(BF16)
            gib = gi_ref[:, st].astype(BF16)
            dus.append(_dot(grb, mtre_ref[bb]) + _dot(gib, mtim_ref[bb]))
            ub = upb[:, bb * S5_BLOCK_IN:(bb + 1) * S5_BLOCK_IN]
            dmr_ref[bb] += _dot_tn(ub, grb)
            dmi_ref[bb] += _dot_tn(ub, gib)
        du = jnp.concatenate(dus, axis=1) + dy2 * dsk_ref[...]
        dza_ref[...] = _dot(permt_ref[...], du.astype(BF16)).astype(BF16)

    def rev(b, j):
        return (b * nb + (nb - 1 - j), 0)

    blk = pl.BlockSpec((tb, D_SSM), rev)
    m_shape = (S5_BLOCKS, S5_BLOCK_IN, S5_BLOCK_ST)
    n_shape = (S5_BLOCKS, S5_BLOCK_ST, S5_BLOCK_IN)
    return _call(
        body, "s5_bwd", (bsz, nb),
        [blk, blk, blk, pl.BlockSpec((1, 2, SSM_LANES), lambda b, j: (b * nb + (nb - 1 - j), 0, 0)),
         _fixed((tb, tb)), _fixed((tb, tb)), _fixed(m_shape), _fixed(m_shape), _fixed(n_shape), _fixed(n_shape),
         _fixed(n_shape), _fixed(n_shape), _fixed(m_shape), _fixed(m_shape),
         _fixed((2, SSM_LANES)), _fixed((2, SSM_LANES)), _fixed((1, D_SSM)),
         _fixed((D_SSM, D_SSM)), _fixed((D_SSM, D_SSM)), _fixed((1, D_SSM))],
        (blk, _fixed(m_shape), _fixed(m_shape), _fixed(n_shape), _fixed(n_shape),
         _fixed((2, 8, SSM_LANES)), _fixed((1, D_SSM)), _fixed((D_SSM, D_SSM)), _fixed((1, D_SSM))),
        (SDS((t, D_SSM), BF16), SDS(m_shape, F32), SDS(m_shape, F32), SDS(n_shape, F32), SDS(n_shape, F32),
         SDS((2, 8, SSM_LANES), F32), SDS((1, D_SSM), F32), SDS((D_SSM, D_SSM), F32), SDS((1, D_SSM), F32)),
        (za, y2p, doa, carries, sp["perm"], sp["permt"], sp["mre"], sp["mim"], sp["mtre"], sp["mtim"],
         sp["nre"], sp["nim"], sp["ntre"], sp["ntim"], sp["a"], sp["ap"], sp["dskip"], sp["glu_w"],
         sp["glu_wt"], sp["glu_b"]),
        scratch=[pltpu.VMEM((tb, SSM_LANES), F32), pltpu.VMEM((tb, SSM_LANES), F32),
                 pltpu.VMEM((tb, SSM_LANES), F32), pltpu.VMEM((tb, SSM_LANES), F32),
                 pltpu.VMEM((2, 8, SSM_LANES), F32), pltpu.VMEM((2, SSM_LANES), F32),
                 pltpu.VMEM((2, SSM_LANES), F32)],
        sem=("arbitrary", "arbitrary"), bg=bg)


def _gmlp_spatial(ws_ref, vb):
    lane = lax.broadcasted_iota(jnp.int32, (CHUNK, 128), 1)
    parts = []
    for j in range(GMLP_HEADS // 2):
        vp = vb[:, 128 * j:128 * (j + 1)]
        parts.append(jnp.where(lane < GMLP_HEAD_DIM, _dot(ws_ref[2 * j], vp), _dot(ws_ref[2 * j + 1], vp)))
    return jnp.concatenate(parts, axis=1)


def _gmlp_fwd(zuv, ln_g, ln_b, wsm, bias, bg=None):
    t = zuv.shape[0]

    def body(z_ref, g_ref, b_ref, ws_ref, bias_ref, out_ref):
        u = _gelu(z_ref[:, 0:D_GMLP].astype(F32))
        v0 = _gelu(z_ref[:, D_GMLP:2 * D_GMLP].astype(F32))
        v, _, _ = _ln_fwd(v0, g_ref[...], b_ref[...])
        s = _gmlp_spatial(ws_ref, v.astype(BF16)) + bias_ref[...]
        out_ref[...] = (u * s).astype(BF16)

    return _call(
        body, "gmlp_fwd", (t // CHUNK,),
        [_rows(CHUNK, 2 * D_GMLP), _fixed((1, D_GMLP)), _fixed((1, D_GMLP)),
         _fixed((GMLP_HEADS, CHUNK, CHUNK)), _fixed((CHUNK, D_GMLP))],
        (_rows(CHUNK, D_GMLP),), (SDS((t, D_GMLP), BF16),),
        (zuv, ln_g, ln_b, wsm, bias), sem=("parallel",), bg=bg)


def _gmlp_bwd(zuv, dgm, ln_g, ln_b, wsm, wsmt, bias, bg=None):
    t = zuv.shape[0]

    def body(z_ref, d_ref, g_ref, b_ref, ws_ref, wst_ref, bias_ref,
             dz_ref, dws_ref, dbias_ref, dg_ref, db_ref):
        @pl.when(pl.program_id(0) == 0)
        def _():
            for r in (dws_ref, dbias_ref, dg_ref, db_ref):
                r[...] = jnp.zeros_like(r)

        zu = z_ref[:, 0:D_GMLP].astype(F32)
        zv = z_ref[:, D_GMLP:2 * D_GMLP].astype(F32)
        u = _gelu(zu)
        v0 = _gelu(zv)
        gam = g_ref[...]
        v, vhat, rstd = _ln_fwd(v0, gam, b_ref[...])
        vb = v.astype(BF16)
        s = _gmlp_spatial(ws_ref, vb) + bias_ref[...]
        d = d_ref[...]
        dz_ref[:, 0:D_GMLP] = (d * s * _gelu_grad(zu)).astype(BF16)
        ds = d * u
        dbias_ref[...] += ds
        dsb = ds.astype(BF16)
        lane = lax.broadcasted_iota(jnp.int32, (CHUNK, 128), 1)
        tril = (lax.broadcasted_iota(jnp.int32, (CHUNK, CHUNK), 0)
                >= lax.broadcasted_iota(jnp.int32, (CHUNK, CHUNK), 1))
        zero_b = jnp.zeros((CHUNK, 128), BF16)
        parts = []
        for j in range(GMLP_HEADS // 2):
            dsp = dsb[:, 128 * j:128 * (j + 1)]
            vp = vb[:, 128 * j:128 * (j + 1)]
            parts.append(jnp.where(lane < GMLP_HEAD_DIM, _dot(wst_ref[2 * j], dsp),
                                   _dot(wst_ref[2 * j + 1], dsp)))
            lo = jnp.where(lane < GMLP_HEAD_DIM, dsp, zero_b)
            hi = jnp.where(lane < GMLP_HEAD_DIM, zero_b, dsp)
            dws_ref[2 * j] += jnp.where(tril, _dot_nt(lo, vp), 0.0)
            dws_ref[2 * j + 1] += jnp.where(tril, _dot_nt(hi, vp), 0.0)
        dv = jnp.concatenate(parts, axis=1)
        dg_ref[...] += jnp.sum(dv * vhat, axis=0, keepdims=True)
        db_ref[...] += jnp.sum(dv, axis=0, keepdims=True)
        dz_ref[:, D_GMLP:2 * D_GMLP] = (_ln_bwd(dv, vhat, rstd, gam) * _gelu_grad(zv)).astype(BF16)

    return _call(
        body, "gmlp_bwd", (t // CHUNK,),
        [_rows(CHUNK, 2 * D_GMLP), _rows(CHUNK, D_GMLP), _fixed((1, D_GMLP)), _fixed((1, D_GMLP)),
         _fixed((GMLP_HEADS, CHUNK, CHUNK)), _fixed((GMLP_HEADS, CHUNK, CHUNK)), _fixed((CHUNK, D_GMLP))],
        (_rows(CHUNK, 2 * D_GMLP), _fixed((GMLP_HEADS, CHUNK, CHUNK)), _fixed((CHUNK, D_GMLP)),
         _fixed((1, D_GMLP)), _fixed((1, D_GMLP))),
        (SDS((t, 2 * D_GMLP), BF16), SDS((GMLP_HEADS, CHUNK, CHUNK), F32), SDS((CHUNK, D_GMLP), F32),
         SDS((1, D_GMLP), F32), SDS((1, D_GMLP), F32)),
        (zuv, dgm, ln_g, ln_b, wsm, wsmt, bias), sem=("arbitrary",), bg=bg)


def _mixout_fwd(x1, s5o, gm, gab, ua, ub, wmo, g, b, tm, bg=None):
    t = x1.shape[0]

    def body(x_ref, s_ref, m_ref, gab_ref, ua_ref, ub_ref, wmo_ref, g_ref, b_ref,
             xn_ref, xh_ref, rstd_ref):
        ya = _dot(s_ref[...], ua_ref[...])
        yb = _dot(m_ref[...], ub_ref[...])
        mix = (_sigmoid(gab_ref[:, 0:D_MODEL].astype(F32)) * ya
               + _sigmoid(gab_ref[:, D_MODEL:2 * D_MODEL].astype(F32)) * yb)
        r = ALPHA * x_ref[...] + _dot(mix.astype(BF16), wmo_ref[...])
        y, xh, rstd = _ln_fwd(r, g_ref[...], b_ref[...])
        xn_ref[...] = y
        xh_ref[...] = xh
        rstd_ref[...] = rstd

    return _call(
        body, "mixout_fwd", (t // tm,),
        [_rows(tm, D_MODEL), _rows(tm, D_SSM), _rows(tm, D_GMLP), _rows(tm, 2 * D_MODEL),
         _resident((D_SSM, D_MODEL)), _resident((D_GMLP, D_MODEL)), _resident((D_MODEL, D_MODEL)),
         _fixed((1, D_MODEL)), _fixed((1, D_MODEL))],
        (_rows(tm, D_MODEL), _rows(tm, D_MODEL), _rows(tm, 1)),
        (SDS((t, D_MODEL), F32), SDS((t, D_MODEL), F32), SDS((t, 1), F32)),
        (x1, s5o, gm, gab, ua, ub, wmo, g, b), sem=("parallel",), bg=bg)


def _mixout_bwd(dx2, xh, rstd, s5o, gm, gab, ua, ub, wmo, g, tm, bg=None):
    t = dx2.shape[0]

    def body(d_ref, xh_ref, rstd_ref, s_ref, m_ref, gab_ref, ua_ref, ub_ref, wmo_ref, g_ref,
             dx1_ref, dmx_ref, mb_ref, dya_ref, dyb_ref, ds5_ref, dgm_ref, dgab_ref, dg_ref, db_ref):
        @pl.when(pl.program_id(0) == 0)
        def _():
            dg_ref[...] = jnp.zeros_like(dg_ref)
            db_ref[...] = jnp.zeros_like(db_ref)

        dy = d_ref[...]
        xhv = xh_ref[...]
        dr = _ln_bwd(dy, xhv, rstd_ref[...], g_ref[...])
        dg_ref[...] += jnp.sum(dy * xhv, axis=0, keepdims=True)
        db_ref[...] += jnp.sum(dy, axis=0, keepdims=True)
        dx1_ref[...] = ALPHA * dr
        drb = dr.astype(BF16)
        dmx_ref[...] = drb
        dm = _dot_nt(drb, wmo_ref[...])
        ya = _dot(s_ref[...], ua_ref[...])
        yb = _dot(m_ref[...], ub_ref[...])
        sa = _sigmoid(gab_ref[:, 0:D_MODEL].astype(F32))
        sb = _sigmoid(gab_ref[:, D_MODEL:2 * D_MODEL].astype(F32))
        mb_ref[...] = (sa * ya + sb * yb).astype(BF16)
        dya = (dm * sa).astype(BF16)
        dyb = (dm * sb).astype(BF16)
        dya_ref[...] = dya
        dyb_ref[...] = dyb
        dgab_ref[:, 0:D_MODEL] = (dm * ya * sa * (1.0 - sa)).astype(BF16)
        dgab_ref[:, D_MODEL:2 * D_MODEL] = (dm * yb * sb * (1.0 - sb)).astype(BF16)
        ds5_ref[...] = _dot_nt(dya, ua_ref[...])
        dgm_ref[...] = _dot_nt(dyb, ub_ref[...])

    return _call(
        body, "mixout_bwd", (t // tm,),
        [_rows(tm, D_MODEL), _rows(tm, D_MODEL), _rows(tm, 1), _rows(tm, D_SSM), _rows(tm, D_GMLP),
         _rows(tm, 2 * D_MODEL), _resident((D_SSM, D_MODEL)), _resident((D_GMLP, D_MODEL)),
         _resident((D_MODEL, D_MODEL)), _fixed((1, D_MODEL))],
        (_rows(tm, D_MODEL), _rows(tm, D_MODEL), _rows(tm, D_MODEL), _rows(tm, D_MODEL),
         _rows(tm, D_MODEL), _rows(tm, D_SSM), _rows(tm, D_GMLP), _rows(tm, 2 * D_MODEL),
         _fixed((1, D_MODEL)), _fixed((1, D_MODEL))),
        (SDS((t, D_MODEL), F32), SDS((t, D_MODEL), BF16), SDS((t, D_MODEL), BF16),
         SDS((t, D_MODEL), BF16), SDS((t, D_MODEL), BF16), SDS((t, D_SSM), F32),
         SDS((t, D_GMLP), F32), SDS((t, 2 * D_MODEL), BF16),
         SDS((1, D_MODEL), F32), SDS((1, D_MODEL), F32)),
        (dx2, xh, rstd, s5o, gm, gab, ua, ub, wmo, g), sem=("arbitrary",), bg=bg)


def _ple_loss(x3, p, tgt, wpg, wpp, tm, bg=None):
    t = x3.shape[0]

    def body(x_ref, p_ref, t_ref, wpg_ref, wpp_ref, dx_ref, xb_ref, pb_ref, dq_ref, de_ref, loss_ref):
        @pl.when(pl.program_id(0) == 0)
        def _():
            loss_ref[...] = jnp.zeros_like(loss_ref)

        x3v = x_ref[...]
        xb = x3v.astype(BF16)
        pb = p_ref[...].astype(BF16)
        xb_ref[...] = xb
        pb_ref[...] = pb
        s = _sigmoid(_dot(xb, wpg_ref[...]))
        e = _dot(pb, wpp_ref[...])
        diff = x3v + s * e - t_ref[...]
        loss_ref[...] += jnp.sum(diff * diff, axis=0, keepdims=True)
        dout = diff * (1.0 / D_MODEL)
        de_ref[...] = (dout * s).astype(BF16)
        dq = (dout * e * s * (1.0 - s)).astype(BF16)
        dq_ref[...] = dq
        dx_ref[...] = dout + _dot_nt(dq, wpg_ref[...])

    return _call(
        body, "ple_loss", (t // tm,),
        [_rows(tm, D_MODEL), _rows(tm, PLE_DIM), _rows(tm, D_MODEL),
         _resident((D_MODEL, D_MODEL)), _resident((PLE_DIM, D_MODEL))],
        (_rows(tm, D_MODEL), _rows(tm, D_MODEL), _rows(tm, PLE_DIM), _rows(tm, D_MODEL),
         _rows(tm, D_MODEL), _fixed((1, D_MODEL))),
        (SDS((t, D_MODEL), F32), SDS((t, D_MODEL), BF16), SDS((t, PLE_DIM), BF16),
         SDS((t, D_MODEL), BF16), SDS((t, D_MODEL), BF16), SDS((1, D_MODEL), F32)),
        (x3, p, tgt, wpg, wpp), sem=("arbitrary",), bg=bg)


def _s5_discretise(lre, lim, log_dt, bre, bim):
    dt = jnp.exp(log_dt)[:, None]
    mag = jnp.exp(lre * dt)
    abr = mag * jnp.cos(lim * dt)
    abi = mag * jnp.sin(lim * dt)
    nr = abr - 1.0
    ni = abi
    den = lre * lre + lim * lim
    cr = ((nr * lre + ni * lim) / den)[..., None]
    ci = ((ni * lre - nr * lim) / den)[..., None]
    return abr, abi, cr * bre - ci * bim, cr * bim + ci * bre


def _block_diag_in(bb):
    v = bb.reshape(S5_BLOCKS, 8, SSM_STATE, SSM_GROUP_CH).transpose(0, 1, 3, 2)
    return jnp.einsum("bgip,gh->bgihp", v, jnp.eye(8, dtype=bb.dtype)).reshape(
        S5_BLOCKS, S5_BLOCK_IN, S5_BLOCK_ST)


def _block_diag_in_t(dm):
    v = dm.reshape(S5_BLOCKS, 8, SSM_GROUP_CH, 8, SSM_STATE)
    d = jnp.einsum("bgihp,gh->bgip", v, jnp.eye(8, dtype=dm.dtype))
    return d.transpose(0, 1, 3, 2).reshape(SSM_GROUPS, SSM_STATE, SSM_GROUP_CH)


def _block_diag_out(cc):
    v = cc.reshape(S5_BLOCKS, 8, SSM_GROUP_CH, SSM_STATE)
    return jnp.einsum("bgip,gh->bgphi", v, jnp.eye(8, dtype=cc.dtype)).reshape(
        S5_BLOCKS, S5_BLOCK_ST, S5_BLOCK_IN)


def _block_diag_out_t(dn):
    v = dn.reshape(S5_BLOCKS, 8, SSM_STATE, 8, SSM_GROUP_CH)
    d = jnp.einsum("bgphi,gh->bgip", v, jnp.eye(8, dtype=dn.dtype))
    return d.reshape(SSM_GROUPS, SSM_GROUP_CH, SSM_STATE)


def _s5_setup(lre, lim, log_dt, bre, bim, cre, cim, d_skip, glu_w, glu_b, tb):
    seg = tb // 8
    abr, abi, bbr, bbi = _s5_discretise(lre, lim, log_dt, bre, bim)
    pr, pi = abr, abi
    for _ in range(int(math.log2(seg))):
        pr, pi = pr * pr - pi * pi, 2.0 * pr * pi
    rows = jnp.arange(tb)
    src = (rows % 8) * seg + rows // 8
    perm = (src[:, None] == jnp.arange(tb)[None, :]).astype(BF16)
    mre = _block_diag_in(bbr)
    mim = _block_diag_in(bbi)
    nre = _block_diag_out(cre)
    nim = _block_diag_out(cim)
    return {
        "perm": perm, "permt": perm.T,
        "mre": mre.astype(BF16), "mim": mim.astype(BF16),
        "mtre": mre.transpose(0, 2, 1).astype(BF16), "mtim": mim.transpose(0, 2, 1).astype(BF16),
        "nre": nre.astype(BF16), "nim": nim.astype(BF16),
        "ntre": nre.transpose(0, 2, 1).astype(BF16), "ntim": nim.transpose(0, 2, 1).astype(BF16),
        "a": jnp.stack([abr.reshape(-1), abi.reshape(-1)]),
        "ap": jnp.stack([pr.reshape(-1), pi.reshape(-1)]),
        "dskip": d_skip.reshape(1, D_SSM), "glu_w": glu_w, "glu_wt": glu_w.T,
        "glu_b": glu_b.reshape(1, D_SSM),
    }


BIG = ("ffn1_w_in", "ffn1_w_out", "mix_w_in", "ssm_glu_w", "up_a", "up_b", "mix_w_out",
       "ffn2_w_in", "ffn2_w_out", "ple_w_proj", "ple_w_gate")
BIG_AXIS = {"ffn1_w_in": 1, "ffn1_w_out": 0, "mix_w_in": 1, "ssm_glu_w": 0, "up_a": 1, "up_b": 1,
            "mix_w_out": 0, "ffn2_w_in": 1, "ffn2_w_out": 0, "ple_w_proj": 1, "ple_w_gate": 0}
SHARD_MAJOR = 2
GATHER_AXIS = dict(BIG_AXIS, ffn1_w_in=SHARD_MAJOR, ffn2_w_in=SHARD_MAJOR)
SMALL = ("ln1_g", "ln1_b", "ssm_lambda_re", "ssm_lambda_im", "ssm_log_dt", "ssm_b_re", "ssm_b_im",
         "ssm_c_re", "ssm_c_im", "ssm_d", "ssm_glu_b", "gmlp_ln_g", "gmlp_ln_b", "gmlp_w_s",
         "gmlp_b_s", "ln2_g", "ln2_b", "ln3_g", "ln3_b")
SMALL_VIEW = {"ssm_b_re": (SSM_GROUPS, SSM_STATE * SSM_GROUP_CH), "ssm_b_im": (SSM_GROUPS, SSM_STATE * SSM_GROUP_CH)}


def _small_view(k, a):
    return a.reshape(SMALL_VIEW[k]) if k in SMALL_VIEW else a


def _place():
    return lax.axis_index("x"), lax.axis_index("y"), lax.axis_index("c")


def _other_chips(x, y):
    return [(1 - x, y), (x, 1 - y), (1 - x, 1 - y)]


def _window(ref, shard_shape, axis, chip, half):
    r, c = shard_shape
    hr = r // 2
    if axis == SHARD_MAJOR:
        return ref.at[chip] if half is None else ref.at[chip, pl.ds(half * hr, hr), :]
    if axis == 0:
        if half is None:
            return ref.at[pl.ds(chip * r, r), :]
        return ref.at[pl.ds(chip * r + half * hr, hr), :]
    if half is None:
        return ref.at[:, pl.ds(chip * c, c)]
    return ref.at[pl.ds(half * hr, hr), pl.ds(chip * c, c)]


def _gather_weights(shards, axes):
    n = len(shards)
    shapes = [s.shape for s in shards]
    full = [{0: (4 * r, c), 1: (r, 4 * c), SHARD_MAJOR: (4, r, c)}[ax] for (r, c), ax in zip(shapes, axes)]

    def remote(sems, i, k, src, dst, to):
        return pltpu.make_async_remote_copy(src_ref=src, dst_ref=dst, send_sem=sems[0].at[6 * i + k],
                                            recv_sem=sems[1].at[6 * i + k], device_id=to, device_id_type=MESH)

    def own_copies(ins, outs, sems):
        x, y, c = _place()
        me = 2 * x + y
        cps = []
        for i in range(n):
            hr = shapes[i][0] // 2
            mine = ins[i].at[pl.ds(c * hr, hr), :]
            for j, (cx, cy) in enumerate(_other_chips(x, y)):
                cps.append(remote(sems, i, j, mine, _window(outs[i], shapes[i], axes[i], me, c), (cx, cy, c)))
        local = [pltpu.make_async_copy(ins[i], _window(outs[i], shapes[i], axes[i], me, None), sems[2].at[i])
                 for i in range(n)]
        return cps, local

    def start(ins, outs, sems):
        cps, local = own_copies(ins, outs, sems)
        for cp in local + cps:
            cp.start()

    def finish(ins, outs, sems):
        x, y, c = _place()
        sibling = (x, y, 1 - c)
        passed = []
        for j, (cx, cy) in enumerate(_other_chips(x, y)):
            for i in range(n):
                w = _window(outs[i], shapes[i], axes[i], 2 * cx + cy, c)
                remote(sems, i, j, w, w, (cx, cy, c)).wait_recv()
                cp = remote(sems, i, 3 + j, w, w, sibling)
                cp.start()
                passed.append(cp)
        for j, (cx, cy) in enumerate(_other_chips(x, y)):
            for i in range(n):
                w = _window(outs[i], shapes[i], axes[i], 2 * cx + cy, 1 - c)
                remote(sems, i, 3 + j, w, w, sibling).wait_recv()
        cps, local = own_copies(ins, outs, sems)
        for cp in cps + passed:
            cp.wait_send()
        for cp in local:
            cp.wait()

    return _Exchange(shards, [SDS(f, BF16) for f in full],
                     [pltpu.SemaphoreType.DMA((6 * n,)), pltpu.SemaphoreType.DMA((6 * n,)),
                      pltpu.SemaphoreType.DMA((n,))], start, finish)


def _scatter_grads(parts, shapes, axes):
    n = len(parts)

    def copies(ins, outs, sems):
        x, y, c = _place()
        return [pltpu.make_async_remote_copy(
            src_ref=_window(ins[i], shapes[i], axes[i], 2 * cx + cy, None), dst_ref=outs[i].at[j],
            send_sem=sems[0].at[3 * i + j], recv_sem=sems[1].at[3 * i + j],
            device_id=(cx, cy, c), device_id_type=MESH)
            for i in range(n) for j, (cx, cy) in enumerate(_other_chips(x, y))]

    def start(ins, outs, sems):
        for cp in copies(ins, outs, sems):
            cp.start()

    def finish(ins, outs, sems):
        for cp in copies(ins, outs, sems):
            cp.wait()

    return _Exchange(parts, [SDS((3,) + tuple(s), BF16) for s in shapes],
                     [pltpu.SemaphoreType.DMA((3 * n,)), pltpu.SemaphoreType.DMA((3 * n,))], start, finish)


def _swap_halves(parts, shapes, axes):
    n = len(parts)

    def copies(ins, outs, sems):
        x, y, c = _place()
        cps = []
        for i in range(n):
            r, _ = shapes[i]
            hr = r // 2
            if axes[i] == 0:
                cps += [pltpu.make_async_remote_copy(
                    src_ref=ins[i].at[pl.ds(k * r + (1 - c) * hr, hr), :], dst_ref=outs[i].at[k],
                    send_sem=sems[0].at[i], recv_sem=sems[1].at[i], device_id=(x, y, 1 - c),
                    device_id_type=MESH) for k in range(4)]
            else:
                cps.append(pltpu.make_async_remote_copy(
                    src_ref=ins[i].at[pl.ds((1 - c) * hr, hr), :], dst_ref=outs[i],
                    send_sem=sems[0].at[i], recv_sem=sems[1].at[i], device_id=(x, y, 1 - c),
                    device_id_type=MESH))
        return cps

    def start(ins, outs, sems):
        for cp in copies(ins, outs, sems):
            cp.start()

    def finish(ins, outs, sems):
        x, y, c = _place()
        for i in range(n):
            pltpu.make_async_remote_copy(src_ref=outs[i], dst_ref=outs[i], send_sem=sems[0].at[i],
                                         recv_sem=sems[1].at[i], device_id=(x, y, 1 - c),
                                         device_id_type=MESH).wait()

    out = [SDS((4, r // 2, c), BF16) if ax == 0 else SDS((r // 2, 4 * c), BF16)
           for (r, c), ax in zip(shapes, axes)]
    return _Exchange(parts, out, [pltpu.SemaphoreType.DMA((n,)), pltpu.SemaphoreType.DMA((n,))], start, finish)


def _scatter_halves(pres, shapes):
    n = len(pres)

    def copies(ins, outs, sems):
        x, y, c = _place()
        return [pltpu.make_async_remote_copy(
            src_ref=ins[i].at[1 + j], dst_ref=outs[i].at[j], send_sem=sems[0].at[3 * i + j],
            recv_sem=sems[1].at[3 * i + j], device_id=(cx, cy, c), device_id_type=MESH)
            for i in range(n) for j, (cx, cy) in enumerate(_other_chips(x, y))]

    def start(ins, outs, sems):
        for cp in copies(ins, outs, sems):
            cp.start()

    def finish(ins, outs, sems):
        for cp in copies(ins, outs, sems):
            cp.wait()

    return _Exchange(pres, [SDS((3, r // 2, c), BF16) for r, c in shapes],
                     [pltpu.SemaphoreType.DMA((3 * n,)), pltpu.SemaphoreType.DMA((3 * n,))], start, finish)


def _swap_with_sibling(arrs):
    n = len(arrs)

    def copies(ins, outs, sems):
        x, y, c = _place()
        return [pltpu.make_async_remote_copy(src_ref=ins[i], dst_ref=outs[i], send_sem=sems[0].at[i],
                                             recv_sem=sems[1].at[i], device_id=(x, y, 1 - c),
                                             device_id_type=MESH) for i in range(n)]

    def start(ins, outs, sems):
        for cp in copies(ins, outs, sems):
            cp.start()

    def finish(ins, outs, sems):
        for cp in copies(ins, outs, sems):
            cp.wait()

    return _Exchange(arrs, [SDS(a.shape, a.dtype) for a in arrs],
                     [pltpu.SemaphoreType.DMA((n,)), pltpu.SemaphoreType.DMA((n,))], start, finish)


def _gather_small(arrs):
    n = len(arrs)

    def copy(sems, outs, i, k, block, to, src=None):
        px, py, pc = block
        dst = outs[i].at[4 * px + 2 * py + pc]
        return pltpu.make_async_remote_copy(
            src_ref=dst if src is None else src, dst_ref=dst, send_sem=sems[0].at[7 * i + k],
            recv_sem=sems[1].at[7 * i + k], device_id=to, device_id_type=MESH)

    direct = [math.prod(a.shape) * 4 <= DIRECT_GATHER_BYTES for a in arrs]

    def own_copies(ins, outs, sems):
        x, y, c = _place()
        cps = []
        for i in range(n):
            cps.append(copy(sems, outs, i, 0, (x, y, c), (x, y, 1 - c), src=ins[i]))
            for j, (cx, cy) in enumerate(_other_chips(x, y)):
                cps.append(copy(sems, outs, i, 1 + j, (x, y, c), (cx, cy, c), src=ins[i]))
                if direct[i]:
                    cps.append(copy(sems, outs, i, 4 + j, (x, y, c), (cx, cy, 1 - c), src=ins[i]))
        local = [pltpu.make_async_copy(ins[i], outs[i].at[4 * x + 2 * y + c], sems[2].at[i]) for i in range(n)]
        return cps, local

    def start(ins, outs, sems):
        cps, local = own_copies(ins, outs, sems)
        for cp in local + cps:
            cp.start()

    def finish(ins, outs, sems):
        x, y, c = _place()
        passed = []
        for j, (cx, cy) in enumerate(_other_chips(x, y)):
            for i in range(n):
                copy(sems, outs, i, 1 + j, (cx, cy, c), (x, y, c)).wait_recv()
                if not direct[i]:
                    cp = copy(sems, outs, i, 4 + j, (cx, cy, c), (x, y, 1 - c))
                    cp.start()
                    passed.append(cp)
        for i in range(n):
            copy(sems, outs, i, 0, (x, y, 1 - c), (x, y, c)).wait_recv()
            for j, (cx, cy) in enumerate(_other_chips(x, y)):
                copy(sems, outs, i, 4 + j, (cx, cy, 1 - c), (x, y, c)).wait_recv()
        cps, local = own_copies(ins, outs, sems)
        for cp in cps + passed:
            cp.wait_send()
        for cp in local:
            cp.wait()

    return _Exchange(arrs, [SDS((N_DEV,) + a.shape, F32) for a in arrs],
                     [pltpu.SemaphoreType.DMA((7 * n,)), pltpu.SemaphoreType.DMA((7 * n,)),
                      pltpu.SemaphoreType.DMA((n,))], start, finish)


def _local_step(x, p, tgt, wb, ws, shards=None):
    bsz, seq, _ = x.shape
    t = bsz * seq
    tm = min(256, t)
    tb = min(256, seq)
    x0 = x.reshape(t, D_MODEL)
    p0 = p.reshape(t, PLE_DIM)
    tg = tgt.reshape(t, D_MODEL)
    row = lambda v: v.reshape(1, -1)
    dist = shards is not None
    wb = dict(wb)
    recv, sums, other, gathered = {}, {}, {}, {}
    gb = {}
    gs = {}
    shape_of, axis_of = {}, {}
    chip = None
    if dist:
        shape_of = {k: tuple(shards[k].shape) for k in BIG}
        axis_of = dict(BIG_AXIS)
        for q in range(LAST_PIECES):
            shape_of[LAST_PIECE % q] = (D_MODEL // LAST_PIECES, shape_of["ffn1_w_in"][1])
            axis_of[LAST_PIECE % q] = 1
        xi, yi, ci = _place()
        chip = (2 * xi + yi).astype(jnp.int32).reshape(1)
        ids = jnp.stack([2 * xi + yi] + [2 * cx + cy for cx, cy in _other_chips(xi, yi)] + [ci]).astype(jnp.int32)
    halfbuf, pre = {}, {}

    def gather(names):
        return _gather_weights([shards[k] for k in names], [GATHER_AXIS[k] for k in names]) if dist else None

    def exchange(scat=(), swap=(), halves=(), scat2=(), swap2=(), extra=None):
        if not dist:
            return None, []
        parts, tags = [], []
        if scat:
            parts.append(_scatter_grads([gb[k][1] for k in scat], [shape_of[k] for k in scat],
                                        [axis_of[k] for k in scat]))
            tags.append((recv, scat))
        if swap:
            for k in swap:
                sums[k] = _sum_blocks(gb[k][0], recv[k], shape_of[k], axis_of[k], chip, "sum_" + k)
            parts.append(_swap_with_sibling([sums[k] for k in swap]))
            tags.append((other, swap))
        if halves:
            parts.append(_swap_halves([gb[k][1] for k in halves], [shape_of[k] for k in halves],
                                      [axis_of[k] for k in halves]))
            tags.append((halfbuf, halves))
        if scat2:
            for k in scat2:
                pre[k] = _presum(gb[k][0], halfbuf[k], shape_of[k], axis_of[k], ids, "presum_" + k)
            parts.append(_scatter_halves([pre[k][1] for k in scat2], [shape_of[k] for k in scat2]))
            tags.append((recv, scat2))
        if swap2:
            for k in swap2:
                sums[k] = _sum_half(pre[k][0], recv[k], "sum_" + k)
            parts.append(_swap_with_sibling([sums[k] for k in swap2]))
            tags.append((other, swap2))
        if extra is not None:
            parts.append(extra[0])
            tags.append((extra[1], extra[2]))
        return (_join(parts), tags) if parts else (None, [])

    def take(ex_tags, got):
        ex, tags = ex_tags
        if ex is not None:
            for (dst, names), (o0, o1) in zip(tags, ex.cuts):
                dst.update(zip(names, got[o0:o1]))

    small_shape = {k: _small_view(k, v).shape for k, v in ws.items()}
    small_shape["loss_rows"] = (1, D_MODEL)
    ws = {k: v if (v.ndim == 2 and k != "ssm_log_dt") else v[0] for k, v in ws.items()}
    tril = jnp.tril(jnp.ones((CHUNK, CHUNK), dtype=bool))
    wsm = jnp.where(tril[None], ws["gmlp_w_s"], 0.0)
    wsm_b = wsm.astype(BF16)
    wsmt_b = wsm.transpose(0, 2, 1).astype(BF16)
    bias = jnp.repeat(ws["gmlp_b_s"].T, GMLP_HEAD_DIM, axis=1)

    tf = min(512, t)
    if dist:
        names = ("ffn1_w_in",)
        wb.update(zip(names, _run_exchange(gather(names), "gather_ffn1_in")))
    names = ("ffn1_w_out", "mix_w_in")
    (x0b, h1, a1), got = _ffn_proj(x0, wb["ffn1_w_in"], tf, "ffn1_proj", gather(names))
    wb.update(zip(names, got))
    names = ("ssm_glu_w", "up_a", "up_b", "mix_w_out")
    (x1, xh1, rstd1), got = _ffn_out(x0, a1, wb["ffn1_w_out"], row(ws["ln1_g"]), row(ws["ln1_b"]), tf,
                                     "ffn1_out", gather(names))
    wb.update(zip(names, got))
    sp = _s5_setup(ws["ssm_lambda_re"], ws["ssm_lambda_im"], ws["ssm_log_dt"], ws["ssm_b_re"],
                   ws["ssm_b_im"], ws["ssm_c_re"], ws["ssm_c_im"], ws["ssm_d"], wb["ssm_glu_w"],
                   ws["ssm_glu_b"], tb)
    names = ("ffn2_w_out",)
    (x1b, za, zuv, gab), got = _mixin_fwd(x1, wb["mix_w_in"], tm, gather(names))
    wb.update(zip(names, got))
    names = ("ffn2_w_in",)
    (s5o, y2p, carries), got = _s5_fwd(za, sp, bsz, seq, tb, gather(names))
    wb.update(zip(names, got))
    names = ("ple_w_gate", "ple_w_proj")
    (gm,), got = _gmlp_fwd(zuv, row(ws["gmlp_ln_g"]), row(ws["gmlp_ln_b"]), wsm_b, bias, gather(names))
    wb.update(zip(names, got))
    (x2, xh2, rstd2), _ = _mixout_fwd(x1, s5o, gm, gab, wb["up_a"], wb["up_b"], wb["mix_w_out"],
                                           row(ws["ln2_g"]), row(ws["ln2_b"]), tm)
    (x2b, h2, a2), _ = _ffn_proj(x2, wb["ffn2_w_in"], tf, "ffn2_proj")
    (x3, xh3, rstd3), _ = _ffn_out(x2, a2, wb["ffn2_w_out"], row(ws["ln3_g"]), row(ws["ln3_b"]), tf, "ffn2_out")
    (dx3, x3b, pb, dq, de, loss_rows), _ = _ple_loss(x3, p0, tg, wb["ple_w_gate"], wb["ple_w_proj"], tm)
    gb["ple_w_gate"], _ = _tn_matmul(x3b, dq, "dw_ple_gate", 1024, 1024)
    gb["ple_w_proj"], _ = _tn_matmul(pb, de, "dw_ple_proj", 256, 1024)
    et = exchange(scat=("ple_w_gate", "ple_w_proj"))
    (dx2, dh2, df2, gs["ln3_g"], gs["ln3_b"]), got = _ffn_bwd(
        dx3, xh3, rstd3, h2, wb["ffn2_w_in"], wb["ffn2_w_out"], row(ws["ln3_g"]), tm, "ffn2_bwd", et[0])
    take(et, got)
    gb["ffn2_w_out"], _ = _tn_matmul(a2, df2, "dw_ffn2_out", 1408, 1024)
    et = exchange(scat=("ffn2_w_out",))
    gb["ffn2_w_in"], got = _tn_matmul(x2b, dh2, "dw_ffn2_in", 1024, 1408, bg=et[0])
    take(et, got)
    et = exchange(swap=("ple_w_gate", "ple_w_proj", "ffn2_w_out"))
    (dx1a, dmx, mb, dya, dyb, ds5, dgm, dgab, gs["ln2_g"], gs["ln2_b"]), got = _mixout_bwd(
        dx2, xh2, rstd2, s5o, gm, gab, wb["up_a"], wb["up_b"], wb["mix_w_out"], row(ws["ln2_g"]), tm, et[0])
    take(et, got)
    gb["mix_w_out"], _ = _tn_matmul(mb, dmx, "dw_mix_out", 1024, 1024)
    gb["up_a"], _ = _tn_matmul(s5o, dya, "dw_up_a", 512, 1024)
    gb["up_b"], _ = _tn_matmul(gm, dyb, "dw_up_b", 512, 1024)
    et = exchange(scat=("ffn2_w_in",))
    (dza, dmr, dmi, dnr, dni, da, ddsk, dgw, dgb), got = _s5_bwd(za, y2p, ds5, carries, sp, bsz, seq, tb, et[0])
    take(et, got)
    gb["ssm_glu_w"] = (dgw, dgw.astype(BF16))
    et = exchange(scat=("mix_w_out", "up_a"), swap=("ffn2_w_in",))
    (dzuv, dws, dbias, gs["gmlp_ln_g"], gs["gmlp_ln_b"]), got = _gmlp_bwd(
        zuv, dgm, row(ws["gmlp_ln_g"]), row(ws["gmlp_ln_b"]), wsm_b, wsmt_b, bias, et[0])
    take(et, got)
    et = exchange(scat=("up_b", "ssm_glu_w"))
    (dx1,), got = _mixin_bwd(dx1a, dza, dzuv, dgab, wb["mix_w_in"], tm, et[0])
    take(et, got)
    g_mi, _ = _tn_matmul(x1b, dza, "dw_mix_in_a", 1024, 512, 0, 3584)
    g_mi, _ = _tn_matmul(x1b, dzuv, "dw_mix_in_uv", 1024, 512, 1, 3584, g_mi)
    et = exchange(swap=("mix_w_out", "up_a", "up_b", "ssm_glu_w"))
    gb["mix_w_in"], got = _tn_matmul(x1b, dgab, "dw_mix_in_g", 1024, 512, 3, 3584, g_mi, bg=et[0])
    take(et, got)

    d_abr = da[0].sum(axis=0).reshape(SSM_GROUPS, SSM_STATE)
    d_abi = da[1].sum(axis=0).reshape(SSM_GROUPS, SSM_STATE)
    _, vjp = jax.vjp(_s5_discretise, ws["ssm_lambda_re"], ws["ssm_lambda_im"], ws["ssm_log_dt"],
                     ws["ssm_b_re"], ws["ssm_b_im"])
    (gs["ssm_lambda_re"], gs["ssm_lambda_im"], gs["ssm_log_dt"], gs["ssm_b_re"], gs["ssm_b_im"]) = vjp(
        (d_abr, d_abi, _block_diag_in_t(dmr), _block_diag_in_t(dmi)))
    gs["ssm_c_re"] = _block_diag_out_t(dnr)
    gs["ssm_c_im"] = _block_diag_out_t(dni)
    gs["ssm_d"] = ddsk
    gs["ssm_glu_b"] = dgb
    gs["gmlp_w_s"] = dws
    gs["gmlp_b_s"] = dbias.reshape(CHUNK, GMLP_HEADS, GMLP_HEAD_DIM).sum(axis=-1).T
    gs["loss_rows"] = loss_rows

    def small_gather(names):
        return (_gather_small([gs[k].reshape(small_shape[k]) for k in names]), gathered, names) if dist else None

    late = ("ln1_g", "ln1_b")
    et = exchange(scat=("mix_w_in",), extra=small_gather(tuple(k for k in SMALL + ("loss_rows",) if k not in late)))
    (dx0, dh1, df1, gs["ln1_g"], gs["ln1_b"]), got = _ffn_bwd(
        dx1, xh1, rstd1, h1, wb["ffn1_w_in"], wb["ffn1_w_out"], row(ws["ln1_g"]), tm, "ffn1_bwd", et[0])
    take(et, got)
    grad_x = dx0.reshape(bsz, seq, D_MODEL)
    if not dist:
        gb["ffn1_w_out"], _ = _tn_matmul(a1, df1, "dw_ffn1_out", 1408, 1024)
        gb["ffn1_w_in"], _ = _tn_matmul(x0b, dh1, "dw_ffn1_in", 1024, 1408)
        return loss_rows, grad_x, gb, {k: gs[k].reshape(small_shape[k]) for k in SMALL}, sums, other, gathered, None
    et = exchange(swap=("mix_w_in",), extra=small_gather(late))
    gb["ffn1_w_out"], got = _tn_matmul(a1, df1, "dw_ffn1_out", 1408, 1024, bg=et[0])
    take(et, got)
    last = ["ffn1_w_out"] + [LAST_PIECE % q for q in range(LAST_PIECES)]
    for i in range(1, len(last) + 3):
        stage = lambda d: tuple(last[i - d:i - d + 1]) if 0 <= i - d < len(last) else ()
        et = exchange(halves=stage(1), scat2=stage(2), swap2=stage(3))
        if i < len(last):
            gb[last[i]], got = _tn_matmul(x0b, dh1, "dw_" + last[i], D_MODEL // LAST_PIECES, 1408, bg=et[0],
                                          a_cols=(i - 1, 1))
        else:
            got = _run_exchange(et[0], "reduce_last_%d" % (i - len(last)))
        take(et, got)
    return loss_rows, grad_x, gb, gs, sums, other, gathered, ids


def _adamw(w, g, m, v):
    m = ADAM_B1 * m + (1.0 - ADAM_B1) * g
    v = ADAM_B2 * v + (1.0 - ADAM_B2) * (g * g)
    m_hat = m / ADAM_C1
    v_hat = v / ADAM_C2
    delta = -ADAM_LR * (m_hat / (jnp.sqrt(v_hat) + ADAM_EPS) + ADAM_WD * w)
    return delta, m, v


def _sum_blocks(part, recv, shape, axis, chip, name):
    r, c = shape
    rb = r // 8

    def body(chip_ref, p_ref, r_ref, o_ref):
        o_ref[...] = (p_ref[...] + r_ref[0].astype(F32) + r_ref[1].astype(F32) + r_ref[2].astype(F32))

    if axis == 0:
        own = pl.BlockSpec((rb, c), lambda i, k: (k[0] * 8 + i, 0))
    else:
        own = pl.BlockSpec((rb, c), lambda i, k: (i, k[0]))
    grid_spec = pltpu.PrefetchScalarGridSpec(
        num_scalar_prefetch=1, grid=(8,),
        in_specs=[own, pl.BlockSpec((3, rb, c), lambda i, k: (0, i, 0))],
        out_specs=pl.BlockSpec((rb, c), lambda i, k: (i, 0)))
    return pl.pallas_call(body, name=name, out_shape=SDS((r, c), F32), grid_spec=grid_spec,
                          compiler_params=_params(("parallel",)))(chip, part, recv)


def _presum(part, half, shape, axis, ids, name):
    r, c = shape
    rb = r // 4

    def body(ids_ref, p_ref, h_ref, of_ref, ob_ref):
        s = p_ref[...] + h_ref[...].astype(F32)
        ob_ref[...] = s.astype(BF16)

        @pl.when(pl.program_id(1) == 0)
        def _():
            of_ref[...] = s

    if axis == 0:
        p_spec = pl.BlockSpec((rb, c), lambda i, t, ids: (ids[t] * 4 + ids[4] * 2 + i, 0))
        h_spec = pl.BlockSpec((None, rb, c), lambda i, t, ids: (ids[t], i, 0))
    else:
        p_spec = pl.BlockSpec((rb, c), lambda i, t, ids: (ids[4] * 2 + i, ids[t]))
        h_spec = pl.BlockSpec((rb, c), lambda i, t, ids: (i, ids[t]))
    grid_spec = pltpu.PrefetchScalarGridSpec(
        num_scalar_prefetch=1, grid=(2, 4), in_specs=[p_spec, h_spec],
        out_specs=(pl.BlockSpec((rb, c), lambda i, t, ids: (i, 0)),
                   pl.BlockSpec((None, rb, c), lambda i, t, ids: (t, i, 0))))
    return pl.pallas_call(body, name=name, out_shape=(SDS((r // 2, c), F32), SDS((4, r // 2, c), BF16)),
                          grid_spec=grid_spec, compiler_params=_params(("parallel", "arbitrary")))(ids, part, half)


def _sum_half(pre, recv, name):
    hr, c = pre.shape
    rb = hr // 2

    def body(p_ref, r_ref, o_ref):
        o_ref[...] = (p_ref[...] + r_ref[0].astype(F32) + r_ref[1].astype(F32) + r_ref[2].astype(F32))

    spec = pl.BlockSpec((rb, c), lambda i: (i, 0))
    return pl.pallas_call(body, name=name, grid=(2,), out_shape=SDS((hr, c), F32),
                          in_specs=[spec, pl.BlockSpec((3, rb, c), lambda i: (0, i, 0))], out_specs=spec,
                          compiler_params=_params(("parallel",)))(pre, recv)


def _adam_halves(w, mine, oth, m, v, ids, name, piece=0, prev=None):
    r, c = w.shape
    rb = mine.shape[0] // 2

    def body(ids_ref, w_ref, a_ref, b_ref, m_ref, v_ref, *rest):
        g_ref, d_ref, nm_ref, nv_ref = rest[-4:]
        g = jnp.where(pl.program_id(0) // 2 == ids_ref[4], a_ref[...], b_ref[...])
        g_ref[...] = g
        d_ref[...], nm_ref[...], nv_ref[...] = _adamw(w_ref[...], g, m_ref[...], v_ref[...])

    whole = pl.BlockSpec((rb, c), lambda i, ids: (i + 4 * piece, 0))
    part = pl.BlockSpec((rb, c), lambda i, ids: (i % 2, 0))
    in_specs = [whole, part, part, whole, whole]
    args = [w, mine, oth, m, v]
    aliases = {}
    if prev is not None:
        in_specs += [pl.BlockSpec(memory_space=pl.ANY)] * 4
        args += list(prev)
        aliases = {6: 0, 7: 1, 8: 2, 9: 3}
    grid_spec = pltpu.PrefetchScalarGridSpec(num_scalar_prefetch=1, grid=(4,), in_specs=in_specs,
                                             out_specs=(whole,) * 4)
    return pl.pallas_call(body, name=name, out_shape=tuple(SDS((r, c), F32) for _ in range(4)),
                          grid_spec=grid_spec, input_output_aliases=aliases,
                          compiler_params=_params(("parallel",)))(ids, *args)


def _adam_big(w, ga, gb, m, v, name, piece=0, prev=None):
    r, c = w.shape
    pr = ga.shape[0]
    steps = 8 if pr == r else 2
    rb = pr // steps
    off = piece * steps

    def body(w_ref, ga_ref, gb_ref, m_ref, v_ref, *rest):
        g_ref, d_ref, nm_ref, nv_ref = rest[-4:]
        g = ga_ref[...] + gb_ref[...]
        g_ref[...] = g
        d_ref[...], nm_ref[...], nv_ref[...] = _adamw(w_ref[...], g, m_ref[...], v_ref[...])

    whole = pl.BlockSpec((rb, c), lambda i: (i + off, 0))
    part = pl.BlockSpec((rb, c), lambda i: (i, 0))
    in_specs = [whole, part, part, whole, whole]
    args = [w, ga, gb, m, v]
    aliases = {}
    if prev is not None:
        in_specs += [pl.BlockSpec(memory_space=pl.ANY)] * 4
        args += list(prev)
        aliases = {5: 0, 6: 1, 7: 2, 8: 3}
    return pl.pallas_call(
        body, name=name, grid=(steps,), out_shape=tuple(SDS((r, c), F32) for _ in range(4)),
        in_specs=in_specs, out_specs=(whole,) * 4, input_output_aliases=aliases,
        compiler_params=_params(("parallel",)),
    )(*args)


def _adam_small(ws, gathered, ms, vs):
    n = len(ws)

    def body(*refs):
        w_refs, g_refs, m_refs, v_refs = refs[:n], refs[n:2 * n], refs[2 * n:3 * n], refs[3 * n:4 * n]
        outs = refs[4 * n:]
        for i in range(n):
            g = g_refs[i][0]
            for d in range(1, N_DEV):
                g = g + g_refs[i][d]
            delta, nm, nv = _adamw(w_refs[i][...], g, m_refs[i][...], v_refs[i][...])
            outs[i][...] = g
            outs[n + i][...] = delta
            outs[2 * n + i][...] = nm
            outs[3 * n + i][...] = nv

    vmem = pl.BlockSpec(memory_space=pltpu.VMEM)
    shapes = [w.shape for w in ws]
    return pl.pallas_call(
        body, name="adam_small", out_shape=tuple(SDS(s, F32) for s in shapes * 4),
        in_specs=[vmem] * (4 * n), out_specs=tuple([vmem] * (4 * n)),
        compiler_params=pltpu.CompilerParams(vmem_limit_bytes=VMEM_LIMIT_BYTES),
    )(*ws, *gathered, *ms, *vs)


def _sum_loss(gathered):
    def body(g_ref, o_ref):
        tot = g_ref[0]
        for d in range(1, N_DEV):
            tot = tot + g_ref[d]
        o_ref[...] = (0.5 / D_MODEL) * jnp.sum(tot, axis=1, keepdims=True)

    vmem = pl.BlockSpec(memory_space=pltpu.VMEM)
    return pl.pallas_call(body, name="sum_loss", out_shape=SDS((1, 1), F32), in_specs=[vmem],
                          out_specs=vmem)(gathered)


def kernel(x, p, ffn1_w_in, ffn1_w_out, ln1_g, ln1_b, mix_w_in, ssm_lambda_re, ssm_lambda_im, ssm_log_dt, ssm_b_re, ssm_b_im, ssm_c_re, ssm_c_im, ssm_d, ssm_glu_w, ssm_glu_b, gmlp_ln_g, gmlp_ln_b, gmlp_w_s, gmlp_b_s, up_a, up_b, mix_w_out, ln2_g, ln2_b, ffn2_w_in, ffn2_w_out, ln3_g, ln3_b, ple_w_proj, ple_w_gate, loss_target, m_ffn1_w_in, m_ffn1_w_out, m_ln1_g, m_ln1_b, m_mix_w_in, m_ssm_lambda_re, m_ssm_lambda_im, m_ssm_log_dt, m_ssm_b_re, m_ssm_b_im, m_ssm_c_re, m_ssm_c_im, m_ssm_d, m_ssm_glu_w, m_ssm_glu_b, m_gmlp_ln_g, m_gmlp_ln_b, m_gmlp_w_s, m_gmlp_b_s, m_up_a, m_up_b, m_mix_w_out, m_ln2_g, m_ln2_b, m_ffn2_w_in, m_ffn2_w_out, m_ln3_g, m_ln3_b, m_ple_w_proj, m_ple_w_gate, v_ffn1_w_in, v_ffn1_w_out, v_ln1_g, v_ln1_b, v_mix_w_in, v_ssm_lambda_re, v_ssm_lambda_im, v_ssm_log_dt, v_ssm_b_re, v_ssm_b_im, v_ssm_c_re, v_ssm_c_im, v_ssm_d, v_ssm_glu_w, v_ssm_glu_b, v_gmlp_ln_g, v_gmlp_ln_b, v_gmlp_w_s, v_gmlp_b_s, v_up_a, v_up_b, v_mix_w_out, v_ln2_g, v_ln2_b, v_ffn2_w_in, v_ffn2_w_out, v_ln3_g, v_ln3_b, v_ple_w_proj, v_ple_w_gate):
    given = dict(locals())
    order = ("ffn1_w_in", "ffn1_w_out", "ln1_g", "ln1_b", "mix_w_in", "ssm_lambda_re", "ssm_lambda_im",
             "ssm_log_dt", "ssm_b_re", "ssm_b_im", "ssm_c_re", "ssm_c_im", "ssm_d", "ssm_glu_w", "ssm_glu_b",
             "gmlp_ln_g", "gmlp_ln_b", "gmlp_w_s", "gmlp_b_s", "up_a", "up_b", "mix_w_out", "ln2_g", "ln2_b",
             "ffn2_w_in", "ffn2_w_out", "ln3_g", "ln3_b", "ple_w_proj", "ple_w_gate")
    assert set(order) == set(BIG + SMALL)

    shard = {k: given[k][0] for k in BIG}
    shard_b = {k: shard[k].astype(BF16) for k in BIG}
    loss_rows, grad_x, gb, gs, sums, other, gathered, ids = _local_step(
        x, given["p"][0], loss_target, {}, {k: given[k] for k in SMALL}, shard_b)

    out = {}
    for k in BIG:
        moments = (given["m_" + k][0], given["v_" + k][0])
        if k == "ffn1_w_out":
            out[k] = _adam_halves(shard[k], sums[k], other[k], *moments, ids, "adam_" + k)
        elif k == "ffn1_w_in":
            for q in range(LAST_PIECES):
                kq = LAST_PIECE % q
                out[k] = _adam_halves(shard[k], sums[kq], other[kq], *moments, ids, "adam_" + kq, q, out.get(k))
        else:
            out[k] = _adam_big(shard[k], sums[k], other[k], *moments, "adam_" + k)

    res = _adam_small([_small_view(k, given[k]) for k in SMALL], [gathered[k] for k in SMALL],
                      [_small_view(k, given["m_" + k]) for k in SMALL],
                      [_small_view(k, given["v_" + k]) for k in SMALL])
    ns = len(SMALL)
    for i, k in enumerate(SMALL):
        out[k] = tuple(res[j * ns + i].reshape(given[k].shape) for j in range(4))
    loss = _sum_loss(gathered["loss_rows"]).reshape(())

    lead = lambda k, j: out[k][j][None] if k in BIG else out[k][j]
    return (loss, grad_x, *[lead(k, 0) for k in order], *[lead(k, 1) for k in order],
            *[lead(k, 2) for k in order], *[lead(k, 3) for k in order])
```

```python
import math

import jax
import jax.numpy as jnp
from jax import lax
from jax.experimental import pallas as pl
from jax.experimental.pallas import tpu as pltpu

F32 = jnp.float32
BF16 = jnp.bfloat16
MESH = pl.DeviceIdType.MESH
SDS = jax.ShapeDtypeStruct

D_MODEL = 1024
D_FF = 2816
D_SSM = 512
D_GMLP = 512
SSM_GROUPS = 32
SSM_GROUP_CH = 16
SSM_STATE = 64
SSM_LANES = SSM_GROUPS * SSM_STATE
GMLP_HEADS = 8
GMLP_HEAD_DIM = 64
CHUNK = 128
PLE_DIM = 256
LN_EPS = 1e-5
ALPHA = 2.0 ** 0.25

ADAM_LR = 0.001
ADAM_B1 = 0.9
ADAM_B2 = 0.999
ADAM_EPS = 1e-08
ADAM_WD = 0.01
ADAM_STEP = 10
ADAM_C1 = 1.0 - ADAM_B1 ** ADAM_STEP
ADAM_C2 = 1.0 - ADAM_B2 ** ADAM_STEP

N_DEV = 8
VMEM_LIMIT_BYTES = 56 * 1024 * 1024
FFN_COLS = 1408
S5_BLOCKS = 4
S5_BLOCK_IN = D_SSM // S5_BLOCKS
S5_BLOCK_ST = SSM_LANES // S5_BLOCKS
SCAN_LANES = 512
TN_K_BLOCK = 2048
DIRECT_GATHER_BYTES = 0
LAST_PIECES = 2
LAST_PIECE = "ffn1_w_in_q%d"
_G0 = math.sqrt(2.0 / math.pi)
_G1 = 0.044715


def _dot(a, b):
    return jnp.dot(a, b, preferred_element_type=F32)


def _dot_nt(a, b):
    return lax.dot_general(a, b, (((1,), (1,)), ((), ())), preferred_element_type=F32)


def _dot_tn(a, b):
    return lax.dot_general(a, b, (((0,), (0,)), ((), ())), preferred_element_type=F32)


def _sigmoid(x):
    return 1.0 / (1.0 + jnp.exp(-x))


def _gelu(x):
    t = jnp.tanh(_G0 * (x + _G1 * x * x * x))
    return 0.5 * x * (1.0 + t)


def _gelu_grad(x):
    t = jnp.tanh(_G0 * (x + _G1 * x * x * x))
    return 0.5 * (1.0 + t) + 0.5 * x * (1.0 - t * t) * _G0 * (1.0 + 3.0 * _G1 * x * x)


def _ln_fwd(r, g, b):
    mu = jnp.mean(r, axis=-1, keepdims=True)
    d = r - mu
    var = jnp.mean(d * d, axis=-1, keepdims=True)
    rstd = lax.rsqrt(var + LN_EPS)
    xh = d * rstd
    return xh * g + b, xh, rstd


def _ln_bwd(dy, xh, rstd, g):
    dxh = dy * g
    m1 = jnp.mean(dxh, axis=-1, keepdims=True)
    m2 = jnp.mean(dxh * xh, axis=-1, keepdims=True)
    return rstd * (dxh - m1 - xh * m2)


def _resident(shape):
    nd = len(shape)
    return pl.BlockSpec(shape, lambda *_: (0,) * nd, pipeline_mode=pl.Buffered(1))


def _fixed(shape):
    nd = len(shape)
    return pl.BlockSpec(shape, lambda *_: (0,) * nd)


def _rows(tm, cols):
    return pl.BlockSpec((tm, cols), lambda i: (i, 0))


def _cols(rows, tm):
    return pl.BlockSpec((rows, tm), lambda i: (0, i))


def _params(sem):
    return pltpu.CompilerParams(dimension_semantics=sem, vmem_limit_bytes=VMEM_LIMIT_BYTES)


class _Exchange:
    def __init__(self, args, out_shape, sems, start, finish):
        self.args, self.out_shape, self.sems = list(args), list(out_shape), list(sems)
        self.start, self.finish = start, finish
        self.cuts = [(0, len(self.out_shape))]


def _call(body, name, grid, in_specs, out_specs, out_shape, args, scratch=(), sem=None, bg=None, aliases=None):
    aliases = {} if aliases is None else aliases
    if bg is None:
        res = pl.pallas_call(body, name=name, grid=grid, out_shape=tuple(out_shape), in_specs=list(in_specs),
                             out_specs=tuple(out_specs), scratch_shapes=list(scratch),
                             input_output_aliases=aliases, compiler_params=_params(sem))(*args)
        return tuple(res), ()
    n_in, n_out, n_bi, n_bo, n_sc = len(args), len(out_shape), len(bg.args), len(bg.out_shape), len(scratch)

    def wrapped(*refs):
        ins = refs[:n_in]
        b_ins = refs[n_in:n_in + n_bi]
        outs = refs[n_in + n_bi:n_in + n_bi + n_out]
        b_outs = refs[n_in + n_bi + n_out:n_in + n_bi + n_out + n_bo]
        rest = refs[n_in + n_bi + n_out + n_bo:]
        scr, b_sems = rest[:n_sc], rest[n_sc:]
        first = pl.program_id(0) == 0
        last = pl.program_id(0) == grid[0] - 1
        for ax in range(1, len(grid)):
            first = jnp.logical_and(first, pl.program_id(ax) == 0)
            last = jnp.logical_and(last, pl.program_id(ax) == grid[ax] - 1)

        @pl.when(first)
        def _():
            bg.start(b_ins, b_outs, b_sems)

        body(*ins, *outs, *scr)

        @pl.when(last)
        def _():
            bg.finish(b_ins, b_outs, b_sems)

    any_spec = pl.BlockSpec(memory_space=pl.ANY)
    res = pl.pallas_call(
        wrapped, name=name, grid=grid, out_shape=tuple(out_shape) + tuple(bg.out_shape),
        in_specs=list(in_specs) + [any_spec] * n_bi, out_specs=tuple(out_specs) + (any_spec,) * n_bo,
        scratch_shapes=list(scratch) + list(bg.sems), input_output_aliases=aliases,
        compiler_params=_params(tuple("arbitrary" for _ in grid)))(*args, *bg.args)
    return tuple(res[:n_out]), tuple(res[n_out:])


def _run_exchange(ex, name):
    n_i, n_o = len(ex.args), len(ex.out_shape)

    def body(*refs):
        ins, outs, sems = refs[:n_i], refs[n_i:n_i + n_o], refs[n_i + n_o:]
        ex.start(ins, outs, sems)
        ex.finish(ins, outs, sems)

    any_spec = pl.BlockSpec(memory_space=pl.ANY)
    return tuple(pl.pallas_call(body, name=name, out_shape=tuple(ex.out_shape), in_specs=[any_spec] * n_i,
                                out_specs=(any_spec,) * n_o, scratch_shapes=list(ex.sems))(*ex.args))


def _join(exchanges):
    cuts = []
    a = o = q = 0
    for e in exchanges:
        cuts.append((a, a + len(e.args), o, o + len(e.out_shape), q, q + len(e.sems)))
        a, o, q = cuts[-1][1], cuts[-1][3], cuts[-1][5]

    def start(ins, outs, sems):
        for e, (a0, a1, o0, o1, q0, q1) in zip(exchanges, cuts):
            e.start(ins[a0:a1], outs[o0:o1], sems[q0:q1])

    def finish(ins, outs, sems):
        for e, (a0, a1, o0, o1, q0, q1) in zip(exchanges, cuts):
            e.finish(ins[a0:a1], outs[o0:o1], sems[q0:q1])

    joined = _Exchange(sum((e.args for e in exchanges), []), sum((e.out_shape for e in exchanges), []),
                       sum((e.sems for e in exchanges), []), start, finish)
    joined.cuts = [(c[2], c[3]) for c in cuts]
    return joined


def _ffn_proj(x, w_in, tm, name, bg=None):
    t = x.shape[0]
    nch = D_FF // FFN_COLS

    def body(x_ref, win_ref, xbt_ref, h_ref, a_ref):
        xb = x_ref[...].astype(BF16)
        xbt_ref[...] = xb.T
        for k in range(nch):
            cg = slice(k * FFN_COLS, (k + 1) * FFN_COLS)
            cu = slice(D_FF + k * FFN_COLS, D_FF + (k + 1) * FFN_COLS)
            hg = _dot(xb, win_ref[k])
            hu = _dot(xb, win_ref[nch + k])
            h_ref[:, cg] = hg.astype(BF16)
            h_ref[:, cu] = hu.astype(BF16)
            a_ref[:, cg] = (hg * _sigmoid(hg) * hu).astype(BF16)

    return _call(
        body, name, (t // tm,),
        [_rows(tm, D_MODEL), _resident((2 * nch, D_MODEL, FFN_COLS))],
        (_cols(D_MODEL, tm), _rows(tm, 2 * D_FF), _rows(tm, D_FF)),
        (SDS((D_MODEL, t), BF16), SDS((t, 2 * D_FF), BF16), SDS((t, D_FF), BF16)),
        (x, w_in), sem=("parallel",), bg=bg)


def _ffn_out(x, a, w_out, g, b, tm, name, bg=None):
    t = x.shape[0]

    def body(x_ref, a_ref, wout_ref, g_ref, b_ref, xn_ref, xh_ref, rstd_ref):
        f = _dot(a_ref[...], wout_ref[...])
        y, xh, rstd = _ln_fwd(ALPHA * x_ref[...] + 0.5 * f, g_ref[...], b_ref[...])
        xn_ref[...] = y
        xh_ref[...] = xh
        rstd_ref[...] = rstd

    return _call(
        body, name, (t // tm,),
        [_rows(tm, D_MODEL), _rows(tm, D_FF), _resident((D_FF, D_MODEL)), _fixed((1, D_MODEL)), _fixed((1, D_MODEL))],
        (_rows(tm, D_MODEL), _rows(tm, D_MODEL), _rows(tm, 1)),
        (SDS((t, D_MODEL), F32), SDS((t, D_MODEL), F32), SDS((t, 1), F32)),
        (x, a, w_out, g, b), sem=("parallel",), bg=bg)


def _ffn_bwd(dxn, xh, rstd, h, w_in, w_out, g, tm, name, bg=None):
    t = dxn.shape[0]
    nch = D_FF // FFN_COLS

    def body(dxn_ref, xh_ref, rstd_ref, h_ref, win_ref, wout_ref, g_ref,
             dx_ref, dh_ref, df_ref, dg_ref, db_ref):
        @pl.when(pl.program_id(0) == 0)
        def _():
            dg_ref[...] = jnp.zeros_like(dg_ref)
            db_ref[...] = jnp.zeros_like(db_ref)

        dy = dxn_ref[...]
        xhv = xh_ref[...]
        dr = _ln_bwd(dy, xhv, rstd_ref[...], g_ref[...])
        dg_ref[...] += jnp.sum(dy * xhv, axis=0, keepdims=True)
        db_ref[...] += jnp.sum(dy, axis=0, keepdims=True)
        df = (0.5 * dr).astype(BF16)
        df_ref[...] = df
        dx = ALPHA * dr
        for k in range(nch):
            cg = slice(k * FFN_COLS, (k + 1) * FFN_COLS)
            cu = slice(D_FF + k * FFN_COLS, D_FF + (k + 1) * FFN_COLS)
            hg = h_ref[:, cg].astype(F32)
            hu = h_ref[:, cu].astype(F32)
            sg = _sigmoid(hg)
            silu = hg * sg
            da = _dot_nt(df, wout_ref[cg, :])
            dhu = (da * silu).astype(BF16)
            dhg = (da * hu * (sg * (1.0 + hg * (1.0 - sg)))).astype(BF16)
            dh_ref[:, cg] = dhg
            dh_ref[:, cu] = dhu
            dx = dx + _dot_nt(dhg, win_ref[k]) + _dot_nt(dhu, win_ref[nch + k])
        dx_ref[...] = dx

    return _call(
        body, name, (t // tm,),
        [_rows(tm, D_MODEL), _rows(tm, D_MODEL), _rows(tm, 1), _rows(tm, 2 * D_FF),
         _resident((2 * nch, D_MODEL, FFN_COLS)), _resident((D_FF, D_MODEL)), _fixed((1, D_MODEL))],
        (_rows(tm, D_MODEL), _rows(tm, 2 * D_FF), _rows(tm, D_MODEL),
         _fixed((1, D_MODEL)), _fixed((1, D_MODEL))),
        (SDS((t, D_MODEL), F32), SDS((t, 2 * D_FF), BF16), SDS((t, D_MODEL), BF16),
         SDS((1, D_MODEL), F32), SDS((1, D_MODEL), F32)),
        (dxn, xh, rstd, h, w_in, w_out, g), sem=("arbitrary",), bg=bg)


def _tn_matmul(a, b, name, bm, bn, col_block=0, total_cols=None, prev=None, bg=None, a_cols=None, a_t=False):
    t, m = a.shape[::-1] if a_t else a.shape
    a_first = 0
    if a_cols is not None:
        a_first, m = a_cols[0], a_cols[1] * bm
    n = b.shape[1]
    total_cols = n if total_cols is None else total_cols
    bk = min(TN_K_BLOCK, t)
    nk = t // bk
    n_in = 2 if prev is None else 4

    def body(*refs):
        a_ref, b_ref = refs[0], refs[1]
        o_ref, ob_ref = refs[n_in], refs[n_in + 1]
        k = pl.program_id(2)

        @pl.when(k == 0)
        def _():
            o_ref[...] = jnp.zeros_like(o_ref)

        o_ref[...] += _dot(a_ref[...], b_ref[...]) if a_t else _dot_tn(a_ref[...], b_ref[...])

        @pl.when(k == nk - 1)
        def _():
            ob_ref[...] = o_ref[...].astype(BF16)

    a_spec = (pl.BlockSpec((bm, bk), lambda i, j, k: (i + a_first, k)) if a_t
              else pl.BlockSpec((bk, bm), lambda i, j, k: (k, i + a_first)))
    in_specs = [a_spec, pl.BlockSpec((bk, bn), lambda i, j, k: (k, j))]
    args = [a, b]
    aliases = {}
    if prev is not None:
        in_specs += [pl.BlockSpec(memory_space=pl.ANY), pl.BlockSpec(memory_space=pl.ANY)]
        args += list(prev)
        aliases = {2: 0, 3: 1}
    out_spec = pl.BlockSpec((bm, bn), lambda i, j, k: (i, j + col_block))
    return _call(body, name, (m // bm, n // bn, nk), in_specs, (out_spec, out_spec),
                 (SDS((m, total_cols), F32), SDS((m, total_cols), BF16)), args,
                 sem=("parallel", "parallel", "arbitrary"), bg=bg, aliases=aliases)


def _mixin_fwd(x1, w, tm, bg=None):
    t = x1.shape[0]

    def body(x_ref, w_ref, xbt_ref, za_ref, zuv_ref, gab_ref):
        xb = x_ref[...].astype(BF16)
        xbt_ref[...] = xb.T
        za_ref[...] = _dot(xb, w_ref[:, 0:512]).astype(BF16)
        zuv_ref[...] = _dot(xb, w_ref[:, 512:1536]).astype(BF16)
        gab_ref[...] = _dot(xb, w_ref[:, 1536:3584]).astype(BF16)

    return _call(
        body, "mixin_fwd", (t // tm,),
        [_rows(tm, D_MODEL), _resident((D_MODEL, 3584))],
        (_cols(D_MODEL, tm), _rows(tm, 512), _rows(tm, 1024), _rows(tm, 2048)),
        (SDS((D_MODEL, t), BF16), SDS((t, 512), BF16), SDS((t, 1024), BF16), SDS((t, 2048), BF16)),
        (x1, w), sem=("parallel",), bg=bg)


def _mixin_bwd(dx1a, dza, dzuv, dgab, w, tm, bg=None):
    t = dx1a.shape[0]

    def body(d_ref, dza_ref, dzuv_ref, dgab_ref, w_ref, dx_ref):
        dx_ref[...] = (d_ref[...] + _dot_nt(dza_ref[...], w_ref[:, 0:512])
                       + _dot_nt(dzuv_ref[...], w_ref[:, 512:1536])
                       + _dot_nt(dgab_ref[...], w_ref[:, 1536:3584]))

    return _call(
        body, "mixin_bwd", (t // tm,),
        [_rows(tm, D_MODEL), _rows(tm, 512), _rows(tm, 1024), _rows(tm, 2048), _resident((D_MODEL, 3584))],
        (_rows(tm, D_MODEL),), (SDS((t, D_MODEL), F32),),
        (dx1a, dza, dzuv, dgab, w), sem=("parallel",), bg=bg)


def _unrolled(lo, hi, body, carry):
    for j in range(lo, hi):
        carry = body(j, carry)
    return carry


def _scan_fwd(hr_ref, hi_ref, a_ref, ap_ref, carry_ref, seg, cin_ref):
    for lc in range(SSM_LANES // SCAN_LANES):
        ls = slice(lc * SCAN_LANES, (lc + 1) * SCAN_LANES)
        a_r = jnp.broadcast_to(a_ref[0:1, ls], (8, SCAN_LANES))
        a_i = jnp.broadcast_to(a_ref[1:2, ls], (8, SCAN_LANES))

        def step(j, hc, ls=ls, a_r=a_r, a_i=a_i):
            h_r, h_i = hc
            rows = pl.ds(j * 8, 8)
            n_r = a_r * h_r - a_i * h_i + hr_ref[rows, ls]
            n_i = a_r * h_i + a_i * h_r + hi_ref[rows, ls]
            hr_ref[rows, ls] = n_r
            hi_ref[rows, ls] = n_i
            return n_r, n_i

        zero = jnp.zeros((8, SCAN_LANES), F32)
        f_r, f_i = _unrolled(0, seg, step, (zero, zero))
        c_r = carry_ref[0:1, ls]
        c_i = carry_ref[1:2, ls]
        p_r = ap_ref[0:1, ls]
        p_i = ap_ref[1:2, ls]
        rows_r, rows_i = [], []
        for s in range(8):
            rows_r.append(c_r)
            rows_i.append(c_i)
            c_r, c_i = (f_r[s:s + 1] + p_r * c_r - p_i * c_i,
                        f_i[s:s + 1] + p_r * c_i + p_i * c_r)
        carry_ref[0:1, ls] = c_r
        carry_ref[1:2, ls] = c_i
        cin_r = jnp.concatenate(rows_r, axis=0)
        cin_i = jnp.concatenate(rows_i, axis=0)
        if cin_ref is not None:
            cin_ref[0, :, ls] = cin_r
            cin_ref[1, :, ls] = cin_i

        def fix(j, cc, ls=ls, a_r=a_r, a_i=a_i):
            c_r, c_i = cc
            c_r, c_i = a_r * c_r - a_i * c_i, a_r * c_i + a_i * c_r
            rows = pl.ds(j * 8, 8)
            hr_ref[rows, ls] = hr_ref[rows, ls] + c_r
            hi_ref[rows, ls] = hi_ref[rows, ls] + c_i
            return c_r, c_i

        _unrolled(0, seg, fix, (cin_r, cin_i))


def _scan_bwd(gr_ref, gi_ref, hr_ref, hi_ref, cin_ref, a_ref, ap_ref, rcarry_ref, da_ref, seg):
    for lc in range(SSM_LANES // SCAN_LANES):
        ls = slice(lc * SCAN_LANES, (lc + 1) * SCAN_LANES)
        a_r = jnp.broadcast_to(a_ref[0:1, ls], (8, SCAN_LANES))
        a_i = jnp.broadcast_to(a_ref[1:2, ls], (8, SCAN_LANES))

        def step(t, gc, ls=ls, a_r=a_r, a_i=a_i):
            g_r, g_i = gc
            rows = pl.ds((seg - 1 - t) * 8, 8)
            n_r = gr_ref[rows, ls] + a_r * g_r + a_i * g_i
            n_i = gi_ref[rows, ls] + a_r * g_i - a_i * g_r
            gr_ref[rows, ls] = n_r
            gi_ref[rows, ls] = n_i
            return n_r, n_i

        zero = jnp.zeros((8, SCAN_LANES), F32)
        f_r, f_i = _unrolled(0, seg, step, (zero, zero))
        c_r = rcarry_ref[0:1, ls]
        c_i = rcarry_ref[1:2, ls]
        p_r = ap_ref[0:1, ls]
        p_i = ap_ref[1:2, ls]
        rows_r, rows_i = [None] * 8, [None] * 8
        for s in range(7, -1, -1):
            rows_r[s] = c_r
            rows_i[s] = c_i
            c_r, c_i = (f_r[s:s + 1] + p_r * c_r + p_i * c_i,
                        f_i[s:s + 1] + p_r * c_i - p_i * c_r)
        rcarry_ref[0:1, ls] = c_r
        rcarry_ref[1:2, ls] = c_i
        cin_r = jnp.concatenate(rows_r, axis=0)
        cin_i = jnp.concatenate(rows_i, axis=0)

        def fix_row(j_rows, hp_r, hp_i, cc, ls=ls, a_r=a_r, a_i=a_i):
            c_r, c_i, acc_r, acc_i = cc
            c_r, c_i = a_r * c_r + a_i * c_i, a_r * c_i - a_i * c_r
            g_r = gr_ref[j_rows, ls] + c_r
            g_i = gi_ref[j_rows, ls] + c_i
            gr_ref[j_rows, ls] = g_r
            gi_ref[j_rows, ls] = g_i
            acc_r = acc_r + g_r * hp_r + g_i * hp_i
            acc_i = acc_i + g_i * hp_r - g_r * hp_i
            return c_r, c_i, acc_r, acc_i

        def fix(t, cc, ls=ls, fix_row=fix_row):
            j = seg - 1 - t
            rows = pl.ds(j * 8, 8)
            prev = pl.ds((j - 1) * 8, 8)
            return fix_row(rows, hr_ref[prev, ls], hi_ref[prev, ls], cc)

        cc = _unrolled(0, seg - 1, fix, (cin_r, cin_i, zero, zero))
        _, _, acc_r, acc_i = fix_row(pl.ds(0, 8), cin_ref[0, :, ls], cin_ref[1, :, ls], cc)
        da_ref[0, :, ls] += acc_r
        da_ref[1, :, ls] += acc_i


def _s5_fwd(za, sp, bsz, seq, tb, bg=None):
    nb = seq // tb
    seg = tb // 8
    t = bsz * seq

    def body(za_ref, perm_ref, permt_ref, mre_ref, mim_ref, nre_ref, nim_ref, a_ref, ap_ref,
             dsk_ref, gw_ref, gb_ref, out_ref, outt_ref, y2_ref, car_ref, hr_ref, hi_ref, carry_ref):
        @pl.when(pl.program_id(1) == 0)
        def _():
            carry_ref[...] = jnp.zeros_like(carry_ref)

        car_ref[0] = carry_ref[...]
        up = _dot(perm_ref[...], za_ref[...])
        upb = up.astype(BF16)
        for bb in range(S5_BLOCKS):
            ub = upb[:, bb * S5_BLOCK_IN:(bb + 1) * S5_BLOCK_IN]
            st = slice(bb * S5_BLOCK_ST, (bb + 1) * S5_BLOCK_ST)
            hr_ref[:, st] = _dot(ub, mre_ref[bb])
            hi_ref[:, st] = _dot(ub, mim_ref[bb])
        _scan_fwd(hr_ref, hi_ref, a_ref, ap_ref, carry_ref, seg, None)
        ys = []
        for bb in range(S5_BLOCKS):
            st = slice(bb * S5_BLOCK_ST, (bb + 1) * S5_BLOCK_ST)
            ys.append(_dot(hr_ref[:, st].astype(BF16), nre_ref[bb])
                      - _dot(hi_ref[:, st].astype(BF16), nim_ref[bb]))
        y2 = jnp.concatenate(ys, axis=1) + dsk_ref[...] * up
        y2_ref[...] = y2
        y3 = _gelu(y2)
        gl = _dot(y3.astype(BF16), gw_ref[...]) + gb_ref[...]
        oa = y3 * _sigmoid(gl)
        out = _dot(permt_ref[...], oa.astype(BF16)).astype(BF16)
        out_ref[...] = out
        outt_ref[...] = out.T

    blk = pl.BlockSpec((tb, D_SSM), lambda b, j: (b * nb + j, 0))
    blk_t = pl.BlockSpec((D_SSM, tb), lambda b, j: (0, b * nb + j))
    m_shape = (S5_BLOCKS, S5_BLOCK_IN, S5_BLOCK_ST)
    n_shape = (S5_BLOCKS, S5_BLOCK_ST, S5_BLOCK_IN)
    return _call(
        body, "s5_fwd", (bsz, nb),
        [blk, _fixed((tb, tb)), _fixed((tb, tb)), _fixed(m_shape), _fixed(m_shape), _fixed(n_shape),
         _fixed(n_shape), _fixed((2, SSM_LANES)), _fixed((2, SSM_LANES)), _fixed((1, D_SSM)),
         _fixed((D_SSM, D_SSM)), _fixed((1, D_SSM))],
        (blk, blk_t, blk, pl.BlockSpec((1, 2, SSM_LANES), lambda b, j: (b * nb + j, 0, 0))),
        (SDS((t, D_SSM), BF16), SDS((D_SSM, t), BF16), SDS((t, D_SSM), F32), SDS((bsz * nb, 2, SSM_LANES), F32)),
        (za, sp["perm"], sp["permt"], sp["mre"], sp["mim"], sp["nre"], sp["nim"], sp["a"], sp["ap"],
         sp["dskip"], sp["glu_w"], sp["glu_b"]),
        scratch=[pltpu.VMEM((tb, SSM_LANES), F32), pltpu.VMEM((tb, SSM_LANES), F32),
                 pltpu.VMEM((2, SSM_LANES), F32)],
        sem=("arbitrary", "arbitrary"), bg=bg)


def _s5_bwd(za, y2p, doa, carries, sp, bsz, seq, tb, bg=None):
    nb = seq // tb
    seg = tb // 8
    t = bsz * seq

    def body(za_ref, y2_ref, doa_ref, car_ref, perm_ref, permt_ref, mre_ref, mim_ref, mtre_ref, mtim_ref,
             nre_ref, nim_ref, ntre_ref, ntim_ref, a_ref, ap_ref, dsk_ref, gw_ref, gwt_ref, gb_ref,
             dza_ref, dmr_ref, dmi_ref, dnr_ref, dni_ref, da_ref, ddsk_ref, dgw_ref, dgb_ref,
             hr_ref, hi_ref, gr_ref, gi_ref, cin_ref, carry_ref, rcarry_ref):
        first = jnp.logical_and(pl.program_id(0) == 0, pl.program_id(1) == 0)

        @pl.when(first)
        def _():
            for r in (dmr_ref, dmi_ref, dnr_ref, dni_ref, da_ref, ddsk_ref, dgw_ref, dgb_ref):
                r[...] = jnp.zeros_like(r)

        @pl.when(pl.program_id(1) == 0)
        def _():
            rcarry_ref[...] = jnp.zeros_like(rcarry_ref)

        carry_ref[...] = car_ref[0]
        perm = perm_ref[...]
        up = _dot(perm, za_ref[...])
        upb = up.astype(BF16)
        for bb in range(S5_BLOCKS):
            ub = upb[:, bb * S5_BLOCK_IN:(bb + 1) * S5_BLOCK_IN]
            st = slice(bb * S5_BLOCK_ST, (bb + 1) * S5_BLOCK_ST)
            hr_ref[:, st] = _dot(ub, mre_ref[bb])
            hi_ref[:, st] = _dot(ub, mim_ref[bb])
        _scan_fwd(hr_ref, hi_ref, a_ref, ap_ref, carry_ref, seg, cin_ref)

        y2 = y2_ref[...]
        y3 = _gelu(y2)
        y3b = y3.astype(BF16)
        sg = _sigmoid(_dot(y3b, gw_ref[...]) + gb_ref[...])
        d0 = doa_ref[...]
        d_hi = d0.astype(BF16)
        d1 = d0 - d_hi.astype(F32)
        d_mid = d1.astype(BF16)
        d_lo = (d1 - d_mid.astype(F32)).astype(BF16)
        doap = _dot(perm, d_hi) + _dot(perm, d_mid) + _dot(perm, d_lo)
        dgl = doap * y3 * sg * (1.0 - sg)
        dglb = dgl.astype(BF16)
        dy3 = doap * sg + _dot(dglb, gwt_ref[...])
        dgw_ref[...] += _dot_tn(y3b, dglb)
        dgb_ref[...] += jnp.sum(dgl, axis=0, keepdims=True)
        dy2 = dy3 * _gelu_grad(y2)
        ddsk_ref[...] += jnp.sum(dy2 * up, axis=0, keepdims=True)
        dyb = dy2.astype(BF16)
        for bb in range(S5_BLOCKS):
            dyc = dyb[:, bb * S5_BLOCK_IN:(bb + 1) * S5_BLOCK_IN]
            st = slice(bb * S5_BLOCK_ST, (bb + 1) * S5_BLOCK_ST)
            gr_ref[:, st] = _dot(dyc, ntre_ref[bb])
            gi_ref[:, st] = -_dot(dyc, ntim_ref[bb])
            dnr_ref[bb] += _dot_tn(hr_ref[:, st].astype(BF16), dyc)
            dni_ref[bb] += -_dot_tn(hi_ref[:, st].astype(BF16), dyc)
        _scan_bwd(gr_ref, gi_ref, hr_ref, hi_ref, cin_ref, a_ref, ap_ref, rcarry_ref, da_ref, seg)
        dus = []
        for bb in range(S5_BLOCKS):
            st = slice(bb * S5_BLOCK_ST, (bb + 1) * S5_BLOCK_ST)
            grb = gr_ref[:, st].astype(BF16)
            gib = gi_ref[:, st].astype(BF16)
            dus.append(_dot(grb, mtre_ref[bb]) + _dot(gib, mtim_ref[bb]))
            ub = upb[:, bb * S5_BLOCK_IN:(bb + 1) * S5_BLOCK_IN]
            dmr_ref[bb] += _dot_tn(ub, grb)
            dmi_ref[bb] += _dot_tn(ub, gib)
        du = jnp.concatenate(dus, axis=1) + dy2 * dsk_ref[...]
        dza_ref[...] = _dot(permt_ref[...], du.astype(BF16)).astype(BF16)

    def rev(b, j):
        return (b * nb + (nb - 1 - j), 0)

    blk = pl.BlockSpec((tb, D_SSM), rev)
    m_shape = (S5_BLOCKS, S5_BLOCK_IN, S5_BLOCK_ST)
    n_shape = (S5_BLOCKS, S5_BLOCK_ST, S5_BLOCK_IN)
    return _call(
        body, "s5_bwd", (bsz, nb),
        [blk, blk, blk, pl.BlockSpec((1, 2, SSM_LANES), lambda b, j: (b * nb + (nb - 1 - j), 0, 0)),
         _fixed((tb, tb)), _fixed((tb, tb)), _fixed(m_shape), _fixed(m_shape), _fixed(n_shape), _fixed(n_shape),
         _fixed(n_shape), _fixed(n_shape), _fixed(m_shape), _fixed(m_shape),
         _fixed((2, SSM_LANES)), _fixed((2, SSM_LANES)), _fixed((1, D_SSM)),
         _fixed((D_SSM, D_SSM)), _fixed((D_SSM, D_SSM)), _fixed((1, D_SSM))],
        (blk, _fixed(m_shape), _fixed(m_shape), _fixed(n_shape), _fixed(n_shape),
         _fixed((2, 8, SSM_LANES)), _fixed((1, D_SSM)), _fixed((D_SSM, D_SSM)), _fixed((1, D_SSM))),
        (SDS((t, D_SSM), BF16), SDS(m_shape, F32), SDS(m_shape, F32), SDS(n_shape, F32), SDS(n_shape, F32),
         SDS((2, 8, SSM_LANES), F32), SDS((1, D_SSM), F32), SDS((D_SSM, D_SSM), F32), SDS((1, D_SSM), F32)),
        (za, y2p, doa, carries, sp["perm"], sp["permt"], sp["mre"], sp["mim"], sp["mtre"], sp["mtim"],
         sp["nre"], sp["nim"], sp["ntre"], sp["ntim"], sp["a"], sp["ap"], sp["dskip"], sp["glu_w"],
         sp["glu_wt"], sp["glu_b"]),
        scratch=[pltpu.VMEM((tb, SSM_LANES), F32), pltpu.VMEM((tb, SSM_LANES), F32),
                 pltpu.VMEM((tb, SSM_LANES), F32), pltpu.VMEM((tb, SSM_LANES), F32),
                 pltpu.VMEM((2, 8, SSM_LANES), F32), pltpu.VMEM((2, SSM_LANES), F32),
                 pltpu.VMEM((2, SSM_LANES), F32)],
        sem=("arbitrary", "arbitrary"), bg=bg)


def _gmlp_spatial(ws_ref, vb):
    lane = lax.broadcasted_iota(jnp.int32, (CHUNK, 128), 1)
    parts = []
    for j in range(GMLP_HEADS // 2):
        vp = vb[:, 128 * j:128 * (j + 1)]
        parts.append(jnp.where(lane < GMLP_HEAD_DIM, _dot(ws_ref[2 * j], vp), _dot(ws_ref[2 * j + 1], vp)))
    return jnp.concatenate(parts, axis=1)


def _gmlp_fwd(zuv, ln_g, ln_b, wsm, bias, bg=None):
    t = zuv.shape[0]

    def body(z_ref, g_ref, b_ref, ws_ref, bias_ref, out_ref, outt_ref):
        u = _gelu(z_ref[:, 0:D_GMLP].astype(F32))
        v0 = _gelu(z_ref[:, D_GMLP:2 * D_GMLP].astype(F32))
        v, _, _ = _ln_fwd(v0, g_ref[...], b_ref[...])
        s = _gmlp_spatial(ws_ref, v.astype(BF16)) + bias_ref[...]
        out = (u * s).astype(BF16)
        out_ref[...] = out
        outt_ref[...] = out.T

    return _call(
        body, "gmlp_fwd", (t // CHUNK,),
        [_rows(CHUNK, 2 * D_GMLP), _fixed((1, D_GMLP)), _fixed((1, D_GMLP)),
         _fixed((GMLP_HEADS, CHUNK, CHUNK)), _fixed((CHUNK, D_GMLP))],
        (_rows(CHUNK, D_GMLP), _cols(D_GMLP, CHUNK)), (SDS((t, D_GMLP), BF16), SDS((D_GMLP, t), BF16)),
        (zuv, ln_g, ln_b, wsm, bias), sem=("parallel",), bg=bg)


def _gmlp_bwd(zuv, dgm, ln_g, ln_b, wsm, wsmt, bias, bg=None):
    t = zuv.shape[0]

    def body(z_ref, d_ref, g_ref, b_ref, ws_ref, wst_ref, bias_ref,
             dz_ref, dws_ref, dbias_ref, dg_ref, db_ref):
        @pl.when(pl.program_id(0) == 0)
        def _():
            for r in (dws_ref, dbias_ref, dg_ref, db_ref):
                r[...] = jnp.zeros_like(r)

        zu = z_ref[:, 0:D_GMLP].astype(F32)
        zv = z_ref[:, D_GMLP:2 * D_GMLP].astype(F32)
        u = _gelu(zu)
        v0 = _gelu(zv)
        gam = g_ref[...]
        v, vhat, rstd = _ln_fwd(v0, gam, b_ref[...])
        vb = v.astype(BF16)
        s = _gmlp_spatial(ws_ref, vb) + bias_ref[...]
        d = d_ref[...]
        dz_ref[:, 0:D_GMLP] = (d * s * _gelu_grad(zu)).astype(BF16)
        ds = d * u
        dbias_ref[...] += ds
        dsb = ds.astype(BF16)
        lane = lax.broadcasted_iota(jnp.int32, (CHUNK, 128), 1)
        tril = (lax.broadcasted_iota(jnp.int32, (CHUNK, CHUNK), 0)
                >= lax.broadcasted_iota(jnp.int32, (CHUNK, CHUNK), 1))
        zero_b = jnp.zeros((CHUNK, 128), BF16)
        parts = []
        for j in range(GMLP_HEADS // 2):
            dsp = dsb[:, 128 * j:128 * (j + 1)]
            vp = vb[:, 128 * j:128 * (j + 1)]
            parts.append(jnp.where(lane < GMLP_HEAD_DIM, _dot(wst_ref[2 * j], dsp),
                                   _dot(wst_ref[2 * j + 1], dsp)))
            lo = jnp.where(lane < GMLP_HEAD_DIM, dsp, zero_b)
            hi = jnp.where(lane < GMLP_HEAD_DIM, zero_b, dsp)
            dws_ref[2 * j] += jnp.where(tril, _dot_nt(lo, vp), 0.0)
            dws_ref[2 * j + 1] += jnp.where(tril, _dot_nt(hi, vp), 0.0)
        dv = jnp.concatenate(parts, axis=1)
        dg_ref[...] += jnp.sum(dv * vhat, axis=0, keepdims=True)
        db_ref[...] += jnp.sum(dv, axis=0, keepdims=True)
        dz_ref[:, D_GMLP:2 * D_GMLP] = (_ln_bwd(dv, vhat, rstd, gam) * _gelu_grad(zv)).astype(BF16)

    return _call(
        body, "gmlp_bwd", (t // CHUNK,),
        [_rows(CHUNK, 2 * D_GMLP), _rows(CHUNK, D_GMLP), _fixed((1, D_GMLP)), _fixed((1, D_GMLP)),
         _fixed((GMLP_HEADS, CHUNK, CHUNK)), _fixed((GMLP_HEADS, CHUNK, CHUNK)), _fixed((CHUNK, D_GMLP))],
        (_rows(CHUNK, 2 * D_GMLP), _fixed((GMLP_HEADS, CHUNK, CHUNK)), _fixed((CHUNK, D_GMLP)),
         _fixed((1, D_GMLP)), _fixed((1, D_GMLP))),
        (SDS((t, 2 * D_GMLP), BF16), SDS((GMLP_HEADS, CHUNK, CHUNK), F32), SDS((CHUNK, D_GMLP), F32),
         SDS((1, D_GMLP), F32), SDS((1, D_GMLP), F32)),
        (zuv, dgm, ln_g, ln_b, wsm, wsmt, bias), sem=("arbitrary",), bg=bg)


def _mixout_fwd(x1, s5o, gm, gab, ua, ub, wmo, g, b, tm, bg=None):
    t = x1.shape[0]

    def body(x_ref, s_ref, m_ref, gab_ref, ua_ref, ub_ref, wmo_ref, g_ref, b_ref,
             xn_ref, xh_ref, rstd_ref):
        ya = _dot(s_ref[...], ua_ref[...])
        yb = _dot(m_ref[...], ub_ref[...])
        mix = (_sigmoid(gab_ref[:, 0:D_MODEL].astype(F32)) * ya
               + _sigmoid(gab_ref[:, D_MODEL:2 * D_MODEL].astype(F32)) * yb)
        r = ALPHA * x_ref[...] + _dot(mix.astype(BF16), wmo_ref[...])
        y, xh, rstd = _ln_fwd(r, g_ref[...], b_ref[...])
        xn_ref[...] = y
        xh_ref[...] = xh
        rstd_ref[...] = rstd

    return _call(
        body, "mixout_fwd", (t // tm,),
        [_rows(tm, D_MODEL), _rows(tm, D_SSM), _rows(tm, D_GMLP), _rows(tm, 2 * D_MODEL),
         _resident((D_SSM, D_MODEL)), _resident((D_GMLP, D_MODEL)), _resident((D_MODEL, D_MODEL)),
         _fixed((1, D_MODEL)), _fixed((1, D_MODEL))],
        (_rows(tm, D_MODEL), _rows(tm, D_MODEL), _rows(tm, 1)),
        (SDS((t, D_MODEL), F32), SDS((t, D_MODEL), F32), SDS((t, 1), F32)),
        (x1, s5o, gm, gab, ua, ub, wmo, g, b), sem=("parallel",), bg=bg)


def _mixout_bwd(dx2, xh, rstd, s5o, gm, gab, ua, ub, wmo, g, tm, bg=None):
    t = dx2.shape[0]

    def body(d_ref, xh_ref, rstd_ref, s_ref, m_ref, gab_ref, ua_ref, ub_ref, wmo_ref, g_ref,
             dx1_ref, dmx_ref, mb_ref, dya_ref, dyb_ref, ds5_ref, dgm_ref, dgab_ref, dg_ref, db_ref):
        @pl.when(pl.program_id(0) == 0)
        def _():
            dg_ref[...] = jnp.zeros_like(dg_ref)
            db_ref[...] = jnp.zeros_like(db_ref)

        dy = d_ref[...]
        xhv = xh_ref[...]
        dr = _ln_bwd(dy, xhv, rstd_ref[...], g_ref[...])
        dg_ref[...] += jnp.sum(dy * xhv, axis=0, keepdims=True)
        db_ref[...] += jnp.sum(dy, axis=0, keepdims=True)
        dx1_ref[...] = ALPHA * dr
        drb = dr.astype(BF16)
        dmx_ref[...] = drb
        dm = _dot_nt(drb, wmo_ref[...])
        ya = _dot(s_ref[...], ua_ref[...])
        yb = _dot(m_ref[...], ub_ref[...])
        sa = _sigmoid(gab_ref[:, 0:D_MODEL].astype(F32))
        sb = _sigmoid(gab_ref[:, D_MODEL:2 * D_MODEL].astype(F32))
        mb_ref[...] = (sa * ya + sb * yb).astype(BF16).T
        dya = (dm * sa).astype(BF16)
        dyb = (dm * sb).astype(BF16)
        dya_ref[...] = dya
        dyb_ref[...] = dyb
        dgab_ref[:, 0:D_MODEL] = (dm * ya * sa * (1.0 - sa)).astype(BF16)
        dgab_ref[:, D_MODEL:2 * D_MODEL] = (dm * yb * sb * (1.0 - sb)).astype(BF16)
        ds5_ref[...] = _dot_nt(dya, ua_ref[...])
        dgm_ref[...] = _dot_nt(dyb, ub_ref[...])

    return _call(
        body, "mixout_bwd", (t // tm,),
        [_rows(tm, D_MODEL), _rows(tm, D_MODEL), _rows(tm, 1), _rows(tm, D_SSM), _rows(tm, D_GMLP),
         _rows(tm, 2 * D_MODEL), _resident((D_SSM, D_MODEL)), _resident((D_GMLP, D_MODEL)),
         _resident((D_MODEL, D_MODEL)), _fixed((1, D_MODEL))],
        (_rows(tm, D_MODEL), _rows(tm, D_MODEL), _cols(D_MODEL, tm), _rows(tm, D_MODEL),
         _rows(tm, D_MODEL), _rows(tm, D_SSM), _rows(tm, D_GMLP), _rows(tm, 2 * D_MODEL),
         _fixed((1, D_MODEL)), _fixed((1, D_MODEL))),
        (SDS((t, D_MODEL), F32), SDS((t, D_MODEL), BF16), SDS((D_MODEL, t), BF16),
         SDS((t, D_MODEL), BF16), SDS((t, D_MODEL), BF16), SDS((t, D_SSM), F32),
         SDS((t, D_GMLP), F32), SDS((t, 2 * D_MODEL), BF16),
         SDS((1, D_MODEL), F32), SDS((1, D_MODEL), F32)),
        (dx2, xh, rstd, s5o, gm, gab, ua, ub, wmo, g), sem=("arbitrary",), bg=bg)


def _ple_loss(x3, p, tgt, wpg, wpp, tm, bg=None):
    t = x3.shape[0]

    def body(x_ref, p_ref, t_ref, wpg_ref, wpp_ref, dx_ref, xb_ref, pb_ref, dq_ref, de_ref, loss_ref):
        @pl.when(pl.program_id(0) == 0)
        def _():
            loss_ref[...] = jnp.zeros_like(loss_ref)

        x3v = x_ref[...]
        xb = x3v.astype(BF16)
        pb = p_ref[...].astype(BF16)
        xb_ref[...] = xb.T
        pb_ref[...] = pb.T
        s = _sigmoid(_dot(xb, wpg_ref[...]))
        e = _dot(pb, wpp_ref[...])
        diff = x3v + s * e - t_ref[...]
        loss_ref[...] += jnp.sum(diff * diff, axis=0, keepdims=True)
        dout = diff * (1.0 / D_MODEL)
        de_ref[...] = (dout * s).astype(BF16)
        dq = (dout * e * s * (1.0 - s)).astype(BF16)
        dq_ref[...] = dq
        dx_ref[...] = dout + _dot_nt(dq, wpg_ref[...])

    return _call(
        body, "ple_loss", (t // tm,),
        [_rows(tm, D_MODEL), _rows(tm, PLE_DIM), _rows(tm, D_MODEL),
         _resident((D_MODEL, D_MODEL)), _resident((PLE_DIM, D_MODEL))],
        (_rows(tm, D_MODEL), _cols(D_MODEL, tm), _cols(PLE_DIM, tm), _rows(tm, D_MODEL),
         _rows(tm, D_MODEL), _fixed((1, D_MODEL))),
        (SDS((t, D_MODEL), F32), SDS((D_MODEL, t), BF16), SDS((PLE_DIM, t), BF16),
         SDS((t, D_MODEL), BF16), SDS((t, D_MODEL), BF16), SDS((1, D_MODEL), F32)),
        (x3, p, tgt, wpg, wpp), sem=("arbitrary",), bg=bg)


def _s5_discretise(lre, lim, log_dt, bre, bim):
    dt = jnp.exp(log_dt)[:, None]
    mag = jnp.exp(lre * dt)
    abr = mag * jnp.cos(lim * dt)
    abi = mag * jnp.sin(lim * dt)
    nr = abr - 1.0
    ni = abi
    den = lre * lre + lim * lim
    cr = ((nr * lre + ni * lim) / den)[..., None]
    ci = ((ni * lre - nr * lim) / den)[..., None]
    return abr, abi, cr * bre - ci * bim, cr * bim + ci * bre


def _block_diag_in(bb):
    v = bb.reshape(S5_BLOCKS, 8, SSM_STATE, SSM_GROUP_CH).transpose(0, 1, 3, 2)
    return jnp.einsum("bgip,gh->bgihp", v, jnp.eye(8, dtype=bb.dtype)).reshape(
        S5_BLOCKS, S5_BLOCK_IN, S5_BLOCK_ST)


def _block_diag_in_t(dm):
    v = dm.reshape(S5_BLOCKS, 8, SSM_GROUP_CH, 8, SSM_STATE)
    d = jnp.einsum("bgihp,gh->bgip", v, jnp.eye(8, dtype=dm.dtype))
    return d.transpose(0, 1, 3, 2).reshape(SSM_GROUPS, SSM_STATE, SSM_GROUP_CH)


def _block_diag_out(cc):
    v = cc.reshape(S5_BLOCKS, 8, SSM_GROUP_CH, SSM_STATE)
    return jnp.einsum("bgip,gh->bgphi", v, jnp.eye(8, dtype=cc.dtype)).reshape(
        S5_BLOCKS, S5_BLOCK_ST, S5_BLOCK_IN)


def _block_diag_out_t(dn):
    v = dn.reshape(S5_BLOCKS, 8, SSM_STATE, 8, SSM_GROUP_CH)
    d = jnp.einsum("bgphi,gh->bgip", v, jnp.eye(8, dtype=dn.dtype))
    return d.reshape(SSM_GROUPS, SSM_GROUP_CH, SSM_STATE)


def _s5_setup(lre, lim, log_dt, bre, bim, cre, cim, d_skip, glu_w, glu_b, tb):
    seg = tb // 8
    abr, abi, bbr, bbi = _s5_discretise(lre, lim, log_dt, bre, bim)
    pr, pi = abr, abi
    for _ in range(int(math.log2(seg))):
        pr, pi = pr * pr - pi * pi, 2.0 * pr * pi
    rows = jnp.arange(tb)
    src = (rows % 8) * seg + rows // 8
    perm = (src[:, None] == jnp.arange(tb)[None, :]).astype(BF16)
    mre = _block_diag_in(bbr)
    mim = _block_diag_in(bbi)
    nre = _block_diag_out(cre)
    nim = _block_diag_out(cim)
    return {
        "perm": perm, "permt": perm.T,
        "mre": mre.astype(BF16), "mim": mim.astype(BF16),
        "mtre": mre.transpose(0, 2, 1).astype(BF16), "mtim": mim.transpose(0, 2, 1).astype(BF16),
        "nre": nre.astype(BF16), "nim": nim.astype(BF16),
        "ntre": nre.transpose(0, 2, 1).astype(BF16), "ntim": nim.transpose(0, 2, 1).astype(BF16),
        "a": jnp.stack([abr.reshape(-1), abi.reshape(-1)]),
        "ap": jnp.stack([pr.reshape(-1), pi.reshape(-1)]),
        "dskip": d_skip.reshape(1, D_SSM), "glu_w": glu_w, "glu_wt": glu_w.T,
        "glu_b": glu_b.reshape(1, D_SSM),
    }


BIG = ("ffn1_w_in", "ffn1_w_out", "mix_w_in", "ssm_glu_w", "up_a", "up_b", "mix_w_out",
       "ffn2_w_in", "ffn2_w_out", "ple_w_proj", "ple_w_gate")
BIG_AXIS = {"ffn1_w_in": 1, "ffn1_w_out": 0, "mix_w_in": 1, "ssm_glu_w": 0, "up_a": 1, "up_b": 1,
            "mix_w_out": 0, "ffn2_w_in": 1, "ffn2_w_out": 0, "ple_w_proj": 1, "ple_w_gate": 0}
SHARD_MAJOR = 2
GATHER_AXIS = dict(BIG_AXIS, ffn1_w_in=SHARD_MAJOR, ffn2_w_in=SHARD_MAJOR)
SMALL = ("ln1_g", "ln1_b", "ssm_lambda_re", "ssm_lambda_im", "ssm_log_dt", "ssm_b_re", "ssm_b_im",
         "ssm_c_re", "ssm_c_im", "ssm_d", "ssm_glu_b", "gmlp_ln_g", "gmlp_ln_b", "gmlp_w_s",
         "gmlp_b_s", "ln2_g", "ln2_b", "ln3_g", "ln3_b")
SMALL_VIEW = {"ssm_b_re": (SSM_GROUPS, SSM_STATE * SSM_GROUP_CH), "ssm_b_im": (SSM_GROUPS, SSM_STATE * SSM_GROUP_CH)}


def _small_view(k, a):
    return a.reshape(SMALL_VIEW[k]) if k in SMALL_VIEW else a


def _place():
    return lax.axis_index("x"), lax.axis_index("y"), lax.axis_index("c")


def _other_chips(x, y):
    return [(1 - x, y), (x, 1 - y), (1 - x, 1 - y)]


def _window(ref, shard_shape, axis, chip, half):
    r, c = shard_shape
    hr = r // 2
    if axis == SHARD_MAJOR:
        return ref.at[chip] if half is None else ref.at[chip, pl.ds(half * hr, hr), :]
    if axis == 0:
        if half is None:
            return ref.at[pl.ds(chip * r, r), :]
        return ref.at[pl.ds(chip * r + half * hr, hr), :]
    if half is None:
        return ref.at[:, pl.ds(chip * c, c)]
    return ref.at[pl.ds(half * hr, hr), pl.ds(chip * c, c)]


def _gather_weights(shards, axes):
    n = len(shards)
    shapes = [s.shape for s in shards]
    full = [{0: (4 * r, c), 1: (r, 4 * c), SHARD_MAJOR: (4, r, c)}[ax] for (r, c), ax in zip(shapes, axes)]

    def remote(sems, i, k, src, dst, to):
        return pltpu.make_async_remote_copy(src_ref=src, dst_ref=dst, send_sem=sems[0].at[6 * i + k],
                                            recv_sem=sems[1].at[6 * i + k], device_id=to, device_id_type=MESH)

    def own_copies(ins, outs, sems):
        x, y, c = _place()
        me = 2 * x + y
        cps = []
        for i in range(n):
            hr = shapes[i][0] // 2
            mine = ins[i].at[pl.ds(c * hr, hr), :]
            for j, (cx, cy) in enumerate(_other_chips(x, y)):
                cps.append(remote(sems, i, j, mine, _window(outs[i], shapes[i], axes[i], me, c), (cx, cy, c)))
        local = [pltpu.make_async_copy(ins[i], _window(outs[i], shapes[i], axes[i], me, None), sems[2].at[i])
                 for i in range(n)]
        return cps, local

    def start(ins, outs, sems):
        cps, local = own_copies(ins, outs, sems)
        for cp in local + cps:
            cp.start()

    def finish(ins, outs, sems):
        x, y, c = _place()
        sibling = (x, y, 1 - c)
        passed = []
        for j, (cx, cy) in enumerate(_other_chips(x, y)):
            for i in range(n):
                w = _window(outs[i], shapes[i], axes[i], 2 * cx + cy, c)
                remote(sems, i, j, w, w, (cx, cy, c)).wait_recv()
                cp = remote(sems, i, 3 + j, w, w, sibling)
                cp.start()
                passed.append(cp)
        for j, (cx, cy) in enumerate(_other_chips(x, y)):
            for i in range(n):
                w = _window(outs[i], shapes[i], axes[i], 2 * cx + cy, 1 - c)
                remote(sems, i, 3 + j, w, w, sibling).wait_recv()
        cps, local = own_copies(ins, outs, sems)
        for cp in cps + passed:
            cp.wait_send()
        for cp in local:
            cp.wait()

    return _Exchange(shards, [SDS(f, BF16) for f in full],
                     [pltpu.SemaphoreType.DMA((6 * n,)), pltpu.SemaphoreType.DMA((6 * n,)),
                      pltpu.SemaphoreType.DMA((n,))], start, finish)


def _scatter_grads(parts, shapes, axes):
    n = len(parts)

    def copies(ins, outs, sems):
        x, y, c = _place()
        return [pltpu.make_async_remote_copy(
            src_ref=_window(ins[i], shapes[i], axes[i], 2 * cx + cy, None), dst_ref=outs[i].at[j],
            send_sem=sems[0].at[3 * i + j], recv_sem=sems[1].at[3 * i + j],
            device_id=(cx, cy, c), device_id_type=MESH)
            for i in range(n) for j, (cx, cy) in enumerate(_other_chips(x, y))]

    def start(ins, outs, sems):
        for cp in copies(ins, outs, sems):
            cp.start()

    def finish(ins, outs, sems):
        for cp in copies(ins, outs, sems):
            cp.wait()

    return _Exchange(parts, [SDS((3,) + tuple(s), BF16) for s in shapes],
                     [pltpu.SemaphoreType.DMA((3 * n,)), pltpu.SemaphoreType.DMA((3 * n,))], start, finish)


def _swap_halves(parts, shapes, axes):
    n = len(parts)

    def copies(ins, outs, sems):
        x, y, c = _place()
        cps = []
        for i in range(n):
            r, _ = shapes[i]
            hr = r // 2
            if axes[i] == 0:
                cps += [pltpu.make_async_remote_copy(
                    src_ref=ins[i].at[pl.ds(k * r + (1 - c) * hr, hr), :], dst_ref=outs[i].at[k],
                    send_sem=sems[0].at[i], recv_sem=sems[1].at[i], device_id=(x, y, 1 - c),
                    device_id_type=MESH) for k in range(4)]
            else:
                cps.append(pltpu.make_async_remote_copy(
                    src_ref=ins[i].at[pl.ds((1 - c) * hr, hr), :], dst_ref=outs[i],
                    send_sem=sems[0].at[i], recv_sem=sems[1].at[i], device_id=(x, y, 1 - c),
                    device_id_type=MESH))
        return cps

    def start(ins, outs, sems):
        for cp in copies(ins, outs, sems):
            cp.start()

    def finish(ins, outs, sems):
        x, y, c = _place()
        for i in range(n):
            pltpu.make_async_remote_copy(src_ref=outs[i], dst_ref=outs[i], send_sem=sems[0].at[i],
                                         recv_sem=sems[1].at[i], device_id=(x, y, 1 - c),
                                         device_id_type=MESH).wait()

    out = [SDS((4, r // 2, c), BF16) if ax == 0 else SDS((r // 2, 4 * c), BF16)
           for (r, c), ax in zip(shapes, axes)]
    return _Exchange(parts, out, [pltpu.SemaphoreType.DMA((n,)), pltpu.SemaphoreType.DMA((n,))], start, finish)


def _scatter_halves(pres, shapes):
    n = len(pres)

    def copies(ins, outs, sems):
        x, y, c = _place()
        return [pltpu.make_async_remote_copy(
            src_ref=ins[i].at[1 + j], dst_ref=outs[i].at[j], send_sem=sems[0].at[3 * i + j],
            recv_sem=sems[1].at[3 * i + j], device_id=(cx, cy, c), device_id_type=MESH)
            for i in range(n) for j, (cx, cy) in enumerate(_other_chips(x, y))]

    def start(ins, outs, sems):
        for cp in copies(ins, outs, sems):
            cp.start()

    def finish(ins, outs, sems):
        for cp in copies(ins, outs, sems):
            cp.wait()

    return _Exchange(pres, [SDS((3, r // 2, c), BF16) for r, c in shapes],
                     [pltpu.SemaphoreType.DMA((3 * n,)), pltpu.SemaphoreType.DMA((3 * n,))], start, finish)


def _swap_with_sibling(arrs):
    n = len(arrs)

    def copies(ins, outs, sems):
        x, y, c = _place()
        return [pltpu.make_async_remote_copy(src_ref=ins[i], dst_ref=outs[i], send_sem=sems[0].at[i],
                                             recv_sem=sems[1].at[i], device_id=(x, y, 1 - c),
                                             device_id_type=MESH) for i in range(n)]

    def start(ins, outs, sems):
        for cp in copies(ins, outs, sems):
            cp.start()

    def finish(ins, outs, sems):
        for cp in copies(ins, outs, sems):
            cp.wait()

    return _Exchange(arrs, [SDS(a.shape, a.dtype) for a in arrs],
                     [pltpu.SemaphoreType.DMA((n,)), pltpu.SemaphoreType.DMA((n,))], start, finish)


def _gather_small(arrs):
    n = len(arrs)

    def copy(sems, outs, i, k, block, to, src=None):
        px, py, pc = block
        dst = outs[i].at[4 * px + 2 * py + pc]
        return pltpu.make_async_remote_copy(
            src_ref=dst if src is None else src, dst_ref=dst, send_sem=sems[0].at[7 * i + k],
            recv_sem=sems[1].at[7 * i + k], device_id=to, device_id_type=MESH)

    direct = [math.prod(a.shape) * 4 <= DIRECT_GATHER_BYTES for a in arrs]

    def own_copies(ins, outs, sems):
        x, y, c = _place()
        cps = []
        for i in range(n):
            cps.append(copy(sems, outs, i, 0, (x, y, c), (x, y, 1 - c), src=ins[i]))
            for j, (cx, cy) in enumerate(_other_chips(x, y)):
                cps.append(copy(sems, outs, i, 1 + j, (x, y, c), (cx, cy, c), src=ins[i]))
                if direct[i]:
                    cps.append(copy(sems, outs, i, 4 + j, (x, y, c), (cx, cy, 1 - c), src=ins[i]))
        local = [pltpu.make_async_copy(ins[i], outs[i].at[4 * x + 2 * y + c], sems[2].at[i]) for i in range(n)]
        return cps, local

    def start(ins, outs, sems):
        cps, local = own_copies(ins, outs, sems)
        for cp in local + cps:
            cp.start()

    def finish(ins, outs, sems):
        x, y, c = _place()
        passed = []
        for j, (cx, cy) in enumerate(_other_chips(x, y)):
            for i in range(n):
                copy(sems, outs, i, 1 + j, (cx, cy, c), (x, y, c)).wait_recv()
                if not direct[i]:
                    cp = copy(sems, outs, i, 4 + j, (cx, cy, c), (x, y, 1 - c))
                    cp.start()
                    passed.append(cp)
        for i in range(n):
            copy(sems, outs, i, 0, (x, y, 1 - c), (x, y, c)).wait_recv()
            for j, (cx, cy) in enumerate(_other_chips(x, y)):
                copy(sems, outs, i, 4 + j, (cx, cy, 1 - c), (x, y, c)).wait_recv()
        cps, local = own_copies(ins, outs, sems)
        for cp in cps + passed:
            cp.wait_send()
        for cp in local:
            cp.wait()

    return _Exchange(arrs, [SDS((N_DEV,) + a.shape, F32) for a in arrs],
                     [pltpu.SemaphoreType.DMA((7 * n,)), pltpu.SemaphoreType.DMA((7 * n,)),
                      pltpu.SemaphoreType.DMA((n,))], start, finish)


def _local_step(x, p, tgt, wb, ws, shards=None):
    bsz, seq, _ = x.shape
    t = bsz * seq
    tm = min(256, t)
    tb = min(256, seq)
    x0 = x.reshape(t, D_MODEL)
    p0 = p.reshape(t, PLE_DIM)
    tg = tgt.reshape(t, D_MODEL)
    row = lambda v: v.reshape(1, -1)
    dist = shards is not None
    wb = dict(wb)
    recv, sums, other, gathered = {}, {}, {}, {}
    gb = {}
    gs = {}
    shape_of, axis_of = {}, {}
    chip = None
    if dist:
        shape_of = {k: tuple(shards[k].shape) for k in BIG}
        axis_of = dict(BIG_AXIS)
        for q in range(LAST_PIECES):
            shape_of[LAST_PIECE % q] = (D_MODEL // LAST_PIECES, shape_of["ffn1_w_in"][1])
            axis_of[LAST_PIECE % q] = 1
        xi, yi, ci = _place()
        chip = (2 * xi + yi).astype(jnp.int32).reshape(1)
        ids = jnp.stack([2 * xi + yi] + [2 * cx + cy for cx, cy in _other_chips(xi, yi)] + [ci]).astype(jnp.int32)
    halfbuf, pre = {}, {}

    def gather(names):
        return _gather_weights([shards[k] for k in names], [GATHER_AXIS[k] for k in names]) if dist else None

    def exchange(scat=(), swap=(), halves=(), scat2=(), swap2=(), extra=None):
        if not dist:
            return None, []
        parts, tags = [], []
        if scat:
            parts.append(_scatter_grads([gb[k][1] for k in scat], [shape_of[k] for k in scat],
                                        [axis_of[k] for k in scat]))
            tags.append((recv, scat))
        if swap:
            for k in swap:
                sums[k] = _sum_blocks(gb[k][0], recv[k], shape_of[k], axis_of[k], chip, "sum_" + k)
            parts.append(_swap_with_sibling([sums[k] for k in swap]))
            tags.append((other, swap))
        if halves:
            parts.append(_swap_halves([gb[k][1] for k in halves], [shape_of[k] for k in halves],
                                      [axis_of[k] for k in halves]))
            tags.append((halfbuf, halves))
        if scat2:
            for k in scat2:
                pre[k] = _presum(gb[k][0], halfbuf[k], shape_of[k], axis_of[k], ids, "presum_" + k)
            parts.append(_scatter_halves([pre[k][1] for k in scat2], [shape_of[k] for k in scat2]))
            tags.append((recv, scat2))
        if swap2:
            for k in swap2:
                sums[k] = _sum_half(pre[k][0], recv[k], "sum_" + k)
            parts.append(_swap_with_sibling([sums[k] for k in swap2]))
            tags.append((other, swap2))
        if extra is not None:
            parts.append(extra[0])
            tags.append((extra[1], extra[2]))
        return (_join(parts), tags) if parts else (None, [])

    def take(ex_tags, got):
        ex, tags = ex_tags
        if ex is not None:
            for (dst, names), (o0, o1) in zip(tags, ex.cuts):
                dst.update(zip(names, got[o0:o1]))

    small_shape = {k: _small_view(k, v).shape for k, v in ws.items()}
    small_shape["loss_rows"] = (1, D_MODEL)
    ws = {k: v if (v.ndim == 2 and k != "ssm_log_dt") else v[0] for k, v in ws.items()}
    tril = jnp.tril(jnp.ones((CHUNK, CHUNK), dtype=bool))
    wsm = jnp.where(tril[None], ws["gmlp_w_s"], 0.0)
    wsm_b = wsm.astype(BF16)
    wsmt_b = wsm.transpose(0, 2, 1).astype(BF16)
    bias = jnp.repeat(ws["gmlp_b_s"].T, GMLP_HEAD_DIM, axis=1)

    tf = min(512, t)
    if dist:
        names = ("ffn1_w_in",)
        wb.update(zip(names, _run_exchange(gather(names), "gather_ffn1_in")))
    names = ("ffn1_w_out", "mix_w_in")
    (x0b, h1, a1), got = _ffn_proj(x0, wb["ffn1_w_in"], tf, "ffn1_proj", gather(names))
    wb.update(zip(names, got))
    names = ("ssm_glu_w", "up_a", "up_b", "mix_w_out")
    (x1, xh1, rstd1), got = _ffn_out(x0, a1, wb["ffn1_w_out"], row(ws["ln1_g"]), row(ws["ln1_b"]), tf,
                                     "ffn1_out", gather(names))
    wb.update(zip(names, got))
    sp = _s5_setup(ws["ssm_lambda_re"], ws["ssm_lambda_im"], ws["ssm_log_dt"], ws["ssm_b_re"],
                   ws["ssm_b_im"], ws["ssm_c_re"], ws["ssm_c_im"], ws["ssm_d"], wb["ssm_glu_w"],
                   ws["ssm_glu_b"], tb)
    names = ("ffn2_w_out",)
    (x1b, za, zuv, gab), got = _mixin_fwd(x1, wb["mix_w_in"], tm, gather(names))
    wb.update(zip(names, got))
    names = ("ffn2_w_in",)
    (s5o, s5ot, y2p, carries), got = _s5_fwd(za, sp, bsz, seq, tb, gather(names))
    wb.update(zip(names, got))
    names = ("ple_w_gate", "ple_w_proj")
    (gm, gmt), got = _gmlp_fwd(zuv, row(ws["gmlp_ln_g"]), row(ws["gmlp_ln_b"]), wsm_b, bias, gather(names))
    wb.update(zip(names, got))
    (x2, xh2, rstd2), _ = _mixout_fwd(x1, s5o, gm, gab, wb["up_a"], wb["up_b"], wb["mix_w_out"],
                                           row(ws["ln2_g"]), row(ws["ln2_b"]), tm)
    (x2b, h2, a2), _ = _ffn_proj(x2, wb["ffn2_w_in"], tf, "ffn2_proj")
    (x3, xh3, rstd3), _ = _ffn_out(x2, a2, wb["ffn2_w_out"], row(ws["ln3_g"]), row(ws["ln3_b"]), tf, "ffn2_out")
    (dx3, x3b, pb, dq, de, loss_rows), _ = _ple_loss(x3, p0, tg, wb["ple_w_gate"], wb["ple_w_proj"], tm)
    gb["ple_w_gate"], _ = _tn_matmul(x3b, dq, "dw_ple_gate", 1024, 1024, a_t=True)
    gb["ple_w_proj"], _ = _tn_matmul(pb, de, "dw_ple_proj", 256, 1024, a_t=True)
    et = exchange(scat=("ple_w_gate", "ple_w_proj"))
    (dx2, dh2, df2, gs["ln3_g"], gs["ln3_b"]), got = _ffn_bwd(
        dx3, xh3, rstd3, h2, wb["ffn2_w_in"], wb["ffn2_w_out"], row(ws["ln3_g"]), tm, "ffn2_bwd", et[0])
    take(et, got)
    gb["ffn2_w_out"], _ = _tn_matmul(a2, df2, "dw_ffn2_out", 1408, 1024)
    et = exchange(scat=("ffn2_w_out",))
    gb["ffn2_w_in"], got = _tn_matmul(x2b, dh2, "dw_ffn2_in", 1024, 1408, bg=et[0], a_t=True)
    take(et, got)
    et = exchange(swap=("ple_w_gate", "ple_w_proj", "ffn2_w_out"))
    (dx1a, dmx, mb, dya, dyb, ds5, dgm, dgab, gs["ln2_g"], gs["ln2_b"]), got = _mixout_bwd(
        dx2, xh2, rstd2, s5o, gm, gab, wb["up_a"], wb["up_b"], wb["mix_w_out"], row(ws["ln2_g"]), tm, et[0])
    take(et, got)
    gb["mix_w_out"], _ = _tn_matmul(mb, dmx, "dw_mix_out", 1024, 1024, a_t=True)
    gb["up_a"], _ = _tn_matmul(s5ot, dya, "dw_up_a", 512, 1024, a_t=True)
    gb["up_b"], _ = _tn_matmul(gmt, dyb, "dw_up_b", 512, 1024, a_t=True)
    et = exchange(scat=("ffn2_w_in",))
    (dza, dmr, dmi, dnr, dni, da, ddsk, dgw, dgb), got = _s5_bwd(za, y2p, ds5, carries, sp, bsz, seq, tb, et[0])
    take(et, got)
    gb["ssm_glu_w"] = (dgw, dgw.astype(BF16))
    et = exchange(scat=("mix_w_out", "up_a"), swap=("ffn2_w_in",))
    (dzuv, dws, dbias, gs["gmlp_ln_g"], gs["gmlp_ln_b"]), got = _gmlp_bwd(
        zuv, dgm, row(ws["gmlp_ln_g"]), row(ws["gmlp_ln_b"]), wsm_b, wsmt_b, bias, et[0])
    take(et, got)
    et = exchange(scat=("up_b", "ssm_glu_w"))
    (dx1,), got = _mixin_bwd(dx1a, dza, dzuv, dgab, wb["mix_w_in"], tm, et[0])
    take(et, got)
    g_mi, _ = _tn_matmul(x1b, dza, "dw_mix_in_a", 1024, 512, 0, 3584, a_t=True)
    g_mi, _ = _tn_matmul(x1b, dzuv, "dw_mix_in_uv", 1024, 512, 1, 3584, g_mi, a_t=True)
    et = exchange(swap=("mix_w_out", "up_a", "up_b", "ssm_glu_w"))
    gb["mix_w_in"], got = _tn_matmul(x1b, dgab, "dw_mix_in_g", 1024, 512, 3, 3584, g_mi, bg=et[0], a_t=True)
    take(et, got)

    d_abr = da[0].sum(axis=0).reshape(SSM_GROUPS, SSM_STATE)
    d_abi = da[1].sum(axis=0).reshape(SSM_GROUPS, SSM_STATE)
    _, vjp = jax.vjp(_s5_discretise, ws["ssm_lambda_re"], ws["ssm_lambda_im"], ws["ssm_log_dt"],
                     ws["ssm_b_re"], ws["ssm_b_im"])
    (gs["ssm_lambda_re"], gs["ssm_lambda_im"], gs["ssm_log_dt"], gs["ssm_b_re"], gs["ssm_b_im"]) = vjp(
        (d_abr, d_abi, _block_diag_in_t(dmr), _block_diag_in_t(dmi)))
    gs["ssm_c_re"] = _block_diag_out_t(dnr)
    gs["ssm_c_im"] = _block_diag_out_t(dni)
    gs["ssm_d"] = ddsk
    gs["ssm_glu_b"] = dgb
    gs["gmlp_w_s"] = dws
    gs["gmlp_b_s"] = dbias.reshape(CHUNK, GMLP_HEADS, GMLP_HEAD_DIM).sum(axis=-1).T
    gs["loss_rows"] = loss_rows

    def small_gather(names):
        return (_gather_small([gs[k].reshape(small_shape[k]) for k in names]), gathered, names) if dist else None

    late = ("ln1_g", "ln1_b")
    et = exchange(scat=("mix_w_in",), extra=small_gather(tuple(k for k in SMALL + ("loss_rows",) if k not in late)))
    (dx0, dh1, df1, gs["ln1_g"], gs["ln1_b"]), got = _ffn_bwd(
        dx1, xh1, rstd1, h1, wb["ffn1_w_in"], wb["ffn1_w_out"], row(ws["ln1_g"]), tm, "ffn1_bwd", et[0])
    take(et, got)
    grad_x = dx0.reshape(bsz, seq, D_MODEL)
    if not dist:
        gb["ffn1_w_out"], _ = _tn_matmul(a1, df1, "dw_ffn1_out", 1408, 1024)
        gb["ffn1_w_in"], _ = _tn_matmul(x0b, dh1, "dw_ffn1_in", 1024, 1408, a_t=True)
        return loss_rows, grad_x, gb, {k: gs[k].reshape(small_shape[k]) for k in SMALL}, sums, other, gathered, None
    et = exchange(swap=("mix_w_in",), extra=small_gather(late))
    gb["ffn1_w_out"], got = _tn_matmul(a1, df1, "dw_ffn1_out", 1408, 1024, bg=et[0])
    take(et, got)
    last = ["ffn1_w_out"] + [LAST_PIECE % q for q in range(LAST_PIECES)]
    for i in range(1, len(last) + 3):
        stage = lambda d: tuple(last[i - d:i - d + 1]) if 0 <= i - d < len(last) else ()
        et = exchange(halves=stage(1), scat2=stage(2), swap2=stage(3))
        if i < len(last):
            gb[last[i]], got = _tn_matmul(x0b, dh1, "dw_" + last[i], D_MODEL // LAST_PIECES, 1408, bg=et[0],
                                          a_cols=(i - 1, 1), a_t=True)
        else:
            got = _run_exchange(et[0], "reduce_last_%d" % (i - len(last)))
        take(et, got)
    return loss_rows, grad_x, gb, gs, sums, other, gathered, ids


def _adamw(w, g, m, v):
    m = ADAM_B1 * m + (1.0 - ADAM_B1) * g
    v = ADAM_B2 * v + (1.0 - ADAM_B2) * (g * g)
    m_hat = m / ADAM_C1
    v_hat = v / ADAM_C2
    delta = -ADAM_LR * (m_hat / (jnp.sqrt(v_hat) + ADAM_EPS) + ADAM_WD * w)
    return delta, m, v


def _sum_blocks(part, recv, shape, axis, chip, name):
    r, c = shape
    rb = r // 8

    def body(chip_ref, p_ref, r_ref, o_ref):
        o_ref[...] = (p_ref[...] + r_ref[0].astype(F32) + r_ref[1].astype(F32) + r_ref[2].astype(F32))

    if axis == 0:
        own = pl.BlockSpec((rb, c), lambda i, k: (k[0] * 8 + i, 0))
    else:
        own = pl.BlockSpec((rb, c), lambda i, k: (i, k[0]))
    grid_spec = pltpu.PrefetchScalarGridSpec(
        num_scalar_prefetch=1, grid=(8,),
        in_specs=[own, pl.BlockSpec((3, rb, c), lambda i, k: (0, i, 0))],
        out_specs=pl.BlockSpec((rb, c), lambda i, k: (i, 0)))
    return pl.pallas_call(body, name=name, out_shape=SDS((r, c), F32), grid_spec=grid_spec,
                          compiler_params=_params(("parallel",)))(chip, part, recv)


def _presum(part, half, shape, axis, ids, name):
    r, c = shape
    rb = r // 4

    def body(ids_ref, p_ref, h_ref, of_ref, ob_ref):
        s = p_ref[...] + h_ref[...].astype(F32)
        ob_ref[...] = s.astype(BF16)

        @pl.when(pl.program_id(1) == 0)
        def _():
            of_ref[...] = s

    if axis == 0:
        p_spec = pl.BlockSpec((rb, c), lambda i, t, ids: (ids[t] * 4 + ids[4] * 2 + i, 0))
        h_spec = pl.BlockSpec((None, rb, c), lambda i, t, ids: (ids[t], i, 0))
    else:
        p_spec = pl.BlockSpec((rb, c), lambda i, t, ids: (ids[4] * 2 + i, ids[t]))
        h_spec = pl.BlockSpec((rb, c), lambda i, t, ids: (i, ids[t]))
    grid_spec = pltpu.PrefetchScalarGridSpec(
        num_scalar_prefetch=1, grid=(2, 4), in_specs=[p_spec, h_spec],
        out_specs=(pl.BlockSpec((rb, c), lambda i, t, ids: (i, 0)),
                   pl.BlockSpec((None, rb, c), lambda i, t, ids: (t, i, 0))))
    return pl.pallas_call(body, name=name, out_shape=(SDS((r // 2, c), F32), SDS((4, r // 2, c), BF16)),
                          grid_spec=grid_spec, compiler_params=_params(("parallel", "arbitrary")))(ids, part, half)


def _sum_half(pre, recv, name):
    hr, c = pre.shape
    rb = hr // 2

    def body(p_ref, r_ref, o_ref):
        o_ref[...] = (p_ref[...] + r_ref[0].astype(F32) + r_ref[1].astype(F32) + r_ref[2].astype(F32))

    spec = pl.BlockSpec((rb, c), lambda i: (i, 0))
    return pl.pallas_call(body, name=name, grid=(2,), out_shape=SDS((hr, c), F32),
                          in_specs=[spec, pl.BlockSpec((3, rb, c), lambda i: (0, i, 0))], out_specs=spec,
                          compiler_params=_params(("parallel",)))(pre, recv)


def _adam_halves(w, mine, oth, m, v, ids, name, piece=0, prev=None):
    r, c = w.shape
    rb = mine.shape[0] // 2

    def body(ids_ref, w_ref, a_ref, b_ref, m_ref, v_ref, *rest):
        g_ref, d_ref, nm_ref, nv_ref = rest[-4:]
        g = jnp.where(pl.program_id(0) // 2 == ids_ref[4], a_ref[...], b_ref[...])
        g_ref[...] = g
        d_ref[...], nm_ref[...], nv_ref[...] = _adamw(w_ref[...], g, m_ref[...], v_ref[...])

    whole = pl.BlockSpec((rb, c), lambda i, ids: (i + 4 * piece, 0))
    part = pl.BlockSpec((rb, c), lambda i, ids: (i % 2, 0))
    in_specs = [whole, part, part, whole, whole]
    args = [w, mine, oth, m, v]
    aliases = {}
    if prev is not None:
        in_specs += [pl.BlockSpec(memory_space=pl.ANY)] * 4
        args += list(prev)
        aliases = {6: 0, 7: 1, 8: 2, 9: 3}
    grid_spec = pltpu.PrefetchScalarGridSpec(num_scalar_prefetch=1, grid=(4,), in_specs=in_specs,
                                             out_specs=(whole,) * 4)
    return pl.pallas_call(body, name=name, out_shape=tuple(SDS((r, c), F32) for _ in range(4)),
                          grid_spec=grid_spec, input_output_aliases=aliases,
                          compiler_params=_params(("parallel",)))(ids, *args)


def _adam_big(w, ga, gb, m, v, name, piece=0, prev=None):
    r, c = w.shape
    pr = ga.shape[0]
    steps = 8 if pr == r else 2
    rb = pr // steps
    off = piece * steps

    def body(w_ref, ga_ref, gb_ref, m_ref, v_ref, *rest):
        g_ref, d_ref, nm_ref, nv_ref = rest[-4:]
        g = ga_ref[...] + gb_ref[...]
        g_ref[...] = g
        d_ref[...], nm_ref[...], nv_ref[...] = _adamw(w_ref[...], g, m_ref[...], v_ref[...])

    whole = pl.BlockSpec((rb, c), lambda i: (i + off, 0))
    part = pl.BlockSpec((rb, c), lambda i: (i, 0))
    in_specs = [whole, part, part, whole, whole]
    args = [w, ga, gb, m, v]
    aliases = {}
    if prev is not None:
        in_specs += [pl.BlockSpec(memory_space=pl.ANY)] * 4
        args += list(prev)
        aliases = {5: 0, 6: 1, 7: 2, 8: 3}
    return pl.pallas_call(
        body, name=name, grid=(steps,), out_shape=tuple(SDS((r, c), F32) for _ in range(4)),
        in_specs=in_specs, out_specs=(whole,) * 4, input_output_aliases=aliases,
        compiler_params=_params(("parallel",)),
    )(*args)


def _adam_small(ws, gathered, ms, vs):
    n = len(ws)

    def body(*refs):
        w_refs, g_refs, m_refs, v_refs = refs[:n], refs[n:2 * n], refs[2 * n:3 * n], refs[3 * n:4 * n]
        outs = refs[4 * n:]
        for i in range(n):
            g = g_refs[i][0]
            for d in range(1, N_DEV):
                g = g + g_refs[i][d]
            delta, nm, nv = _adamw(w_refs[i][...], g, m_refs[i][...], v_refs[i][...])
            outs[i][...] = g
            outs[n + i][...] = delta
            outs[2 * n + i][...] = nm
            outs[3 * n + i][...] = nv

    vmem = pl.BlockSpec(memory_space=pltpu.VMEM)
    shapes = [w.shape for w in ws]
    return pl.pallas_call(
        body, name="adam_small", out_shape=tuple(SDS(s, F32) for s in shapes * 4),
        in_specs=[vmem] * (4 * n), out_specs=tuple([vmem] * (4 * n)),
        compiler_params=pltpu.CompilerParams(vmem_limit_bytes=VMEM_LIMIT_BYTES),
    )(*ws, *gathered, *ms, *vs)


def _sum_loss(gathered):
    def body(g_ref, o_ref):
        tot = g_ref[0]
        for d in range(1, N_DEV):
            tot = tot + g_ref[d]
        o_ref[...] = (0.5 / D_MODEL) * jnp.sum(tot, axis=1, keepdims=True)

    vmem = pl.BlockSpec(memory_space=pltpu.VMEM)
    return pl.pallas_call(body, name="sum_loss", out_shape=SDS((1, 1), F32), in_specs=[vmem],
                          out_specs=vmem)(gathered)


def kernel(x, p, ffn1_w_in, ffn1_w_out, ln1_g, ln1_b, mix_w_in, ssm_lambda_re, ssm_lambda_im, ssm_log_dt, ssm_b_re, ssm_b_im, ssm_c_re, ssm_c_im, ssm_d, ssm_glu_w, ssm_glu_b, gmlp_ln_g, gmlp_ln_b, gmlp_w_s, gmlp_b_s, up_a, up_b, mix_w_out, ln2_g, ln2_b, ffn2_w_in, ffn2_w_out, ln3_g, ln3_b, ple_w_proj, ple_w_gate, loss_target, m_ffn1_w_in, m_ffn1_w_out, m_ln1_g, m_ln1_b, m_mix_w_in, m_ssm_lambda_re, m_ssm_lambda_im, m_ssm_log_dt, m_ssm_b_re, m_ssm_b_im, m_ssm_c_re, m_ssm_c_im, m_ssm_d, m_ssm_glu_w, m_ssm_glu_b, m_gmlp_ln_g, m_gmlp_ln_b, m_gmlp_w_s, m_gmlp_b_s, m_up_a, m_up_b, m_mix_w_out, m_ln2_g, m_ln2_b, m_ffn2_w_in, m_ffn2_w_out, m_ln3_g, m_ln3_b, m_ple_w_proj, m_ple_w_gate, v_ffn1_w_in, v_ffn1_w_out, v_ln1_g, v_ln1_b, v_mix_w_in, v_ssm_lambda_re, v_ssm_lambda_im, v_ssm_log_dt, v_ssm_b_re, v_ssm_b_im, v_ssm_c_re, v_ssm_c_im, v_ssm_d, v_ssm_glu_w, v_ssm_glu_b, v_gmlp_ln_g, v_gmlp_ln_b, v_gmlp_w_s, v_gmlp_b_s, v_up_a, v_up_b, v_mix_w_out, v_ln2_g, v_ln2_b, v_ffn2_w_in, v_ffn2_w_out, v_ln3_g, v_ln3_b, v_ple_w_proj, v_ple_w_gate):
    given = dict(locals())
    order = ("ffn1_w_in", "ffn1_w_out", "ln1_g", "ln1_b", "mix_w_in", "ssm_lambda_re", "ssm_lambda_im",
             "ssm_log_dt", "ssm_b_re", "ssm_b_im", "ssm_c_re", "ssm_c_im", "ssm_d", "ssm_glu_w", "ssm_glu_b",
             "gmlp_ln_g", "gmlp_ln_b", "gmlp_w_s", "gmlp_b_s", "up_a", "up_b", "mix_w_out", "ln2_g", "ln2_b",
             "ffn2_w_in", "ffn2_w_out", "ln3_g", "ln3_b", "ple_w_proj", "ple_w_gate")
    assert set(order) == set(BIG + SMALL)

    shard = {k: given[k][0] for k in BIG}
    shard_b = {k: shard[k].astype(BF16) for k in BIG}
    loss_rows, grad_x, gb, gs, sums, other, gathered, ids = _local_step(
        x, given["p"][0], loss_target, {}, {k: given[k] for k in SMALL}, shard_b)

    out = {}
    for k in BIG:
        moments = (given["m_" + k][0], given["v_" + k][0])
        if k == "ffn1_w_out":
            out[k] = _adam_halves(shard[k], sums[k], other[k], *moments, ids, "adam_" + k)
        elif k == "ffn1_w_in":
            for q in range(LAST_PIECES):
                kq = LAST_PIECE % q
                out[k] = _adam_halves(shard[k], sums[kq], other[kq], *moments, ids, "adam_" + kq, q, out.get(k))
        else:
            out[k] = _adam_big(shard[k], sums[k], other[k], *moments, "adam_" + k)

    res = _adam_small([_small_view(k, given[k]) for k in SMALL], [gathered[k] for k in SMALL],
                      [_small_view(k, given["m_" + k]) for k in SMALL],
                      [_small_view(k, given["v_" + k]) for k in SMALL])
    ns = len(SMALL)
    for i, k in enumerate(SMALL):
        out[k] = tuple(res[j * ns + i].reshape(given[k].shape) for j in range(4))
    loss = _sum_loss(gathered["loss_rows"]).reshape(())

    lead = lambda k, j: out[k][j][None] if k in BIG else out[k][j]
    return (loss, grad_x, *[lead(k, 0) for k in order], *[lead(k, 1) for k in order],
            *[lead(k, 2) for k in order], *[lead(k, 3) for k in order])
```

```python
import math

import jax
import jax.numpy as jnp
from jax import lax
from jax.experimental import pallas as pl
from jax.experimental.pallas import tpu as pltpu
from jax.experimental.pallas import tpu_sc as plsc

F32 = jnp.float32
BF16 = jnp.bfloat16
MESH = pl.DeviceIdType.MESH
SDS = jax.ShapeDtypeStruct

D_MODEL = 1024
D_FF = 2816
D_SSM = 512
D_GMLP = 512
SSM_GROUPS = 32
SSM_GROUP_CH = 16
SSM_STATE = 64
SSM_LANES = SSM_GROUPS * SSM_STATE
GMLP_HEADS = 8
GMLP_HEAD_DIM = 64
CHUNK = 128
PLE_DIM = 256
LN_EPS = 1e-5
ALPHA = 2.0 ** 0.25

ADAM_LR = 0.001
ADAM_B1 = 0.9
ADAM_B2 = 0.999
ADAM_EPS = 1e-08
ADAM_WD = 0.01
ADAM_STEP = 10
ADAM_C1 = 1.0 - ADAM_B1 ** ADAM_STEP
ADAM_C2 = 1.0 - ADAM_B2 ** ADAM_STEP

N_DEV = 8
VMEM_LIMIT_BYTES = 56 * 1024 * 1024
FFN_COLS = 1408
S5_BLOCKS = 4
S5_BLOCK_IN = D_SSM // S5_BLOCKS
S5_BLOCK_ST = SSM_LANES // S5_BLOCKS
SCAN_LANES = 512
TN_K_BLOCK = 2048
DIRECT_GATHER_BYTES = 0
LAST_PIECES = 2
LAST_PIECE = "ffn1_w_in_q%d"
_G0 = math.sqrt(2.0 / math.pi)
_G1 = 0.044715


def _dot(a, b):
    return jnp.dot(a, b, preferred_element_type=F32)


def _dot_nt(a, b):
    return lax.dot_general(a, b, (((1,), (1,)), ((), ())), preferred_element_type=F32)


def _dot_tn(a, b):
    return lax.dot_general(a, b, (((0,), (0,)), ((), ())), preferred_element_type=F32)


def _sigmoid(x):
    return 1.0 / (1.0 + jnp.exp(-x))


def _gelu(x):
    t = jnp.tanh(_G0 * (x + _G1 * x * x * x))
    return 0.5 * x * (1.0 + t)


def _gelu_grad(x):
    t = jnp.tanh(_G0 * (x + _G1 * x * x * x))
    return 0.5 * (1.0 + t) + 0.5 * x * (1.0 - t * t) * _G0 * (1.0 + 3.0 * _G1 * x * x)


def _ln_fwd(r, g, b):
    mu = jnp.mean(r, axis=-1, keepdims=True)
    d = r - mu
    var = jnp.mean(d * d, axis=-1, keepdims=True)
    rstd = lax.rsqrt(var + LN_EPS)
    xh = d * rstd
    return xh * g + b, xh, rstd


def _ln_bwd(dy, xh, rstd, g):
    dxh = dy * g
    m1 = jnp.mean(dxh, axis=-1, keepdims=True)
    m2 = jnp.mean(dxh * xh, axis=-1, keepdims=True)
    return rstd * (dxh - m1 - xh * m2)


def _resident(shape):
    nd = len(shape)
    return pl.BlockSpec(shape, lambda *_: (0,) * nd, pipeline_mode=pl.Buffered(1))


def _fixed(shape):
    nd = len(shape)
    return pl.BlockSpec(shape, lambda *_: (0,) * nd)


def _rows(tm, cols):
    return pl.BlockSpec((tm, cols), lambda i: (i, 0))


def _cols(rows, tm):
    return pl.BlockSpec((rows, tm), lambda i: (0, i))


def _params(sem):
    return pltpu.CompilerParams(dimension_semantics=sem, vmem_limit_bytes=VMEM_LIMIT_BYTES)


class _Exchange:
    def __init__(self, args, out_shape, sems, start, finish):
        self.args, self.out_shape, self.sems = list(args), list(out_shape), list(sems)
        self.start, self.finish = start, finish
        self.cuts = [(0, len(self.out_shape))]


def _call(body, name, grid, in_specs, out_specs, out_shape, args, scratch=(), sem=None, bg=None, aliases=None):
    aliases = {} if aliases is None else aliases
    if bg is None:
        res = pl.pallas_call(body, name=name, grid=grid, out_shape=tuple(out_shape), in_specs=list(in_specs),
                             out_specs=tuple(out_specs), scratch_shapes=list(scratch),
                             input_output_aliases=aliases, compiler_params=_params(sem))(*args)
        return tuple(res), ()
    n_in, n_out, n_bi, n_bo, n_sc = len(args), len(out_shape), len(bg.args), len(bg.out_shape), len(scratch)

    def wrapped(*refs):
        ins = refs[:n_in]
        b_ins = refs[n_in:n_in + n_bi]
        outs = refs[n_in + n_bi:n_in + n_bi + n_out]
        b_outs = refs[n_in + n_bi + n_out:n_in + n_bi + n_out + n_bo]
        rest = refs[n_in + n_bi + n_out + n_bo:]
        scr, b_sems = rest[:n_sc], rest[n_sc:]
        first = pl.program_id(0) == 0
        last = pl.program_id(0) == grid[0] - 1
        for ax in range(1, len(grid)):
            first = jnp.logical_and(first, pl.program_id(ax) == 0)
            last = jnp.logical_and(last, pl.program_id(ax) == grid[ax] - 1)

        @pl.when(first)
        def _():
            bg.start(b_ins, b_outs, b_sems)

        body(*ins, *outs, *scr)

        @pl.when(last)
        def _():
            bg.finish(b_ins, b_outs, b_sems)

    any_spec = pl.BlockSpec(memory_space=pl.ANY)
    res = pl.pallas_call(
        wrapped, name=name, grid=grid, out_shape=tuple(out_shape) + tuple(bg.out_shape),
        in_specs=list(in_specs) + [any_spec] * n_bi, out_specs=tuple(out_specs) + (any_spec,) * n_bo,
        scratch_shapes=list(scratch) + list(bg.sems), input_output_aliases=aliases,
        compiler_params=_params(tuple("arbitrary" for _ in grid)))(*args, *bg.args)
    return tuple(res[:n_out]), tuple(res[n_out:])


def _run_exchange(ex, name):
    n_i, n_o = len(ex.args), len(ex.out_shape)

    def body(*refs):
        ins, outs, sems = refs[:n_i], refs[n_i:n_i + n_o], refs[n_i + n_o:]
        ex.start(ins, outs, sems)
        ex.finish(ins, outs, sems)

    any_spec = pl.BlockSpec(memory_space=pl.ANY)
    return tuple(pl.pallas_call(body, name=name, out_shape=tuple(ex.out_shape), in_specs=[any_spec] * n_i,
                                out_specs=(any_spec,) * n_o, scratch_shapes=list(ex.sems))(*ex.args))


def _run_exchange_on_sequencer(ex, name, collective_id):
    n_i, n_o = len(ex.args), len(ex.out_shape)

    def body(*refs):
        ins, outs, sems = refs[:n_i], refs[n_i:n_i + n_o], refs[n_i + n_o:]
        x, y, c = lax.axis_index("x"), lax.axis_index("y"), lax.axis_index("c")
        barrier = pltpu.get_barrier_semaphore()
        for peer in [(x, y, 1 - c), (1 - x, y, c), (x, 1 - y, c), (1 - x, 1 - y, c)]:
            pl.semaphore_signal(barrier, inc=1, device_id=peer, device_id_type=MESH)
        pl.semaphore_wait(barrier, 4)
        ex.start(ins, outs, sems)
        ex.finish(ins, outs, sems)

    return tuple(pl.kernel(body, out_type=tuple(ex.out_shape),
                           mesh=plsc.ScalarSubcoreMesh(axis_name="sequencer", num_cores=1),
                           scratch_types=list(ex.sems), name=name,
                           compiler_params=pltpu.CompilerParams(collective_id=collective_id))(*ex.args))


def _join(exchanges):
    cuts = []
    a = o = q = 0
    for e in exchanges:
        cuts.append((a, a + len(e.args), o, o + len(e.out_shape), q, q + len(e.sems)))
        a, o, q = cuts[-1][1], cuts[-1][3], cuts[-1][5]

    def start(ins, outs, sems):
        for e, (a0, a1, o0, o1, q0, q1) in zip(exchanges, cuts):
            e.start(ins[a0:a1], outs[o0:o1], sems[q0:q1])

    def finish(ins, outs, sems):
        for e, (a0, a1, o0, o1, q0, q1) in zip(exchanges, cuts):
            e.finish(ins[a0:a1], outs[o0:o1], sems[q0:q1])

    joined = _Exchange(sum((e.args for e in exchanges), []), sum((e.out_shape for e in exchanges), []),
                       sum((e.sems for e in exchanges), []), start, finish)
    joined.cuts = [(c[2], c[3]) for c in cuts]
    return joined


def _ffn_proj(x, w_in, tm, name, bg=None):
    t = x.shape[0]
    nch = D_FF // FFN_COLS

    def body(x_ref, win_ref, xbt_ref, h_ref, a_ref):
        xb = x_ref[...].astype(BF16)
        xbt_ref[...] = xb.T
        for k in range(nch):
            cg = slice(k * FFN_COLS, (k + 1) * FFN_COLS)
            cu = slice(D_FF + k * FFN_COLS, D_FF + (k + 1) * FFN_COLS)
            hg = _dot(xb, win_ref[k])
            hu = _dot(xb, win_ref[nch + k])
            h_ref[:, cg] = hg.astype(BF16)
            h_ref[:, cu] = hu.astype(BF16)
            a_ref[:, cg] = (hg * _sigmoid(hg) * hu).astype(BF16)

    return _call(
        body, name, (t // tm,),
        [_rows(tm, D_MODEL), _resident((2 * nch, D_MODEL, FFN_COLS))],
        (_cols(D_MODEL, tm), _rows(tm, 2 * D_FF), _rows(tm, D_FF)),
        (SDS((D_MODEL, t), BF16), SDS((t, 2 * D_FF), BF16), SDS((t, D_FF), BF16)),
        (x, w_in), sem=("parallel",), bg=bg)


def _ffn_out(x, a, w_out, g, b, tm, name, bg=None):
    t = x.shape[0]

    def body(x_ref, a_ref, wout_ref, g_ref, b_ref, xn_ref, xh_ref, rstd_ref):
        f = _dot(a_ref[...], wout_ref[...])
        y, xh, rstd = _ln_fwd(ALPHA * x_ref[...] + 0.5 * f, g_ref[...], b_ref[...])
        xn_ref[...] = y
        xh_ref[...] = xh
        rstd_ref[...] = rstd

    return _call(
        body, name, (t // tm,),
        [_rows(tm, D_MODEL), _rows(tm, D_FF), _resident((D_FF, D_MODEL)), _fixed((1, D_MODEL)), _fixed((1, D_MODEL))],
        (_rows(tm, D_MODEL), _rows(tm, D_MODEL), _rows(tm, 1)),
        (SDS((t, D_MODEL), F32), SDS((t, D_MODEL), F32), SDS((t, 1), F32)),
        (x, a, w_out, g, b), sem=("parallel",), bg=bg)


def _ffn_bwd(dxn, xh, rstd, h, w_in, w_out, g, tm, name, bg=None):
    t = dxn.shape[0]
    nch = D_FF // FFN_COLS

    def body(dxn_ref, xh_ref, rstd_ref, h_ref, win_ref, wout_ref, g_ref,
             dx_ref, dh_ref, df_ref, dg_ref, db_ref):
        @pl.when(pl.program_id(0) == 0)
        def _():
            dg_ref[...] = jnp.zeros_like(dg_ref)
            db_ref[...] = jnp.zeros_like(db_ref)

        dy = dxn_ref[...]
        xhv = xh_ref[...]
        dr = _ln_bwd(dy, xhv, rstd_ref[...], g_ref[...])
        dg_ref[...] += jnp.sum(dy * xhv, axis=0, keepdims=True)
        db_ref[...] += jnp.sum(dy, axis=0, keepdims=True)
        df = (0.5 * dr).astype(BF16)
        df_ref[...] = df
        dx = ALPHA * dr
        for k in range(nch):
            cg = slice(k * FFN_COLS, (k + 1) * FFN_COLS)
            cu = slice(D_FF + k * FFN_COLS, D_FF + (k + 1) * FFN_COLS)
            hg = h_ref[:, cg].astype(F32)
            hu = h_ref[:, cu].astype(F32)
            sg = _sigmoid(hg)
            silu = hg * sg
            da = _dot_nt(df, wout_ref[cg, :])
            dhu = (da * silu).astype(BF16)
            dhg = (da * hu * (sg * (1.0 + hg * (1.0 - sg)))).astype(BF16)
            dh_ref[:, cg] = dhg
            dh_ref[:, cu] = dhu
            dx = dx + _dot_nt(dhg, win_ref[k]) + _dot_nt(dhu, win_ref[nch + k])
        dx_ref[...] = dx

    return _call(
        body, name, (t // tm,),
        [_rows(tm, D_MODEL), _rows(tm, D_MODEL), _rows(tm, 1), _rows(tm, 2 * D_FF),
         _resident((2 * nch, D_MODEL, FFN_COLS)), _resident((D_FF, D_MODEL)), _fixed((1, D_MODEL))],
        (_rows(tm, D_MODEL), _rows(tm, 2 * D_FF), _rows(tm, D_MODEL),
         _fixed((1, D_MODEL)), _fixed((1, D_MODEL))),
        (SDS((t, D_MODEL), F32), SDS((t, 2 * D_FF), BF16), SDS((t, D_MODEL), BF16),
         SDS((1, D_MODEL), F32), SDS((1, D_MODEL), F32)),
        (dxn, xh, rstd, h, w_in, w_out, g), sem=("arbitrary",), bg=bg)


def _tn_matmul(a, b, name, bm, bn, col_block=0, total_cols=None, prev=None, bg=None, a_cols=None, a_t=False):
    t, m = a.shape[::-1] if a_t else a.shape
    a_first = 0
    if a_cols is not None:
        a_first, m = a_cols[0], a_cols[1] * bm
    n = b.shape[1]
    total_cols = n if total_cols is None else total_cols
    bk = min(TN_K_BLOCK, t)
    nk = t // bk
    n_in = 2 if prev is None else 4

    def body(*refs):
        a_ref, b_ref = refs[0], refs[1]
        o_ref, ob_ref = refs[n_in], refs[n_in + 1]
        k = pl.program_id(2)

        @pl.when(k == 0)
        def _():
            o_ref[...] = jnp.zeros_like(o_ref)

        o_ref[...] += _dot(a_ref[...], b_ref[...]) if a_t else _dot_tn(a_ref[...], b_ref[...])

        @pl.when(k == nk - 1)
        def _():
            ob_ref[...] = o_ref[...].astype(BF16)

    a_spec = (pl.BlockSpec((bm, bk), lambda i, j, k: (i + a_first, k)) if a_t
              else pl.BlockSpec((bk, bm), lambda i, j, k: (k, i + a_first)))
    in_specs = [a_spec, pl.BlockSpec((bk, bn), lambda i, j, k: (k, j))]
    args = [a, b]
    aliases = {}
    if prev is not None:
        in_specs += [pl.BlockSpec(memory_space=pl.ANY), pl.BlockSpec(memory_space=pl.ANY)]
        args += list(prev)
        aliases = {2: 0, 3: 1}
    out_spec = pl.BlockSpec((bm, bn), lambda i, j, k: (i, j + col_block))
    return _call(body, name, (m // bm, n // bn, nk), in_specs, (out_spec, out_spec),
                 (SDS((m, total_cols), F32), SDS((m, total_cols), BF16)), args,
                 sem=("parallel", "parallel", "arbitrary"), bg=bg, aliases=aliases)


def _mixin_fwd(x1, w, tm, bg=None):
    t = x1.shape[0]

    def body(x_ref, w_ref, xbt_ref, za_ref, zuv_ref, gab_ref):
        xb = x_ref[...].astype(BF16)
        xbt_ref[...] = xb.T
        za_ref[...] = _dot(xb, w_ref[:, 0:512]).astype(BF16)
        zuv_ref[...] = _dot(xb, w_ref[:, 512:1536]).astype(BF16)
        gab_ref[...] = _dot(xb, w_ref[:, 1536:3584]).astype(BF16)

    return _call(
        body, "mixin_fwd", (t // tm,),
        [_rows(tm, D_MODEL), _resident((D_MODEL, 3584))],
        (_cols(D_MODEL, tm), _rows(tm, 512), _rows(tm, 1024), _rows(tm, 2048)),
        (SDS((D_MODEL, t), BF16), SDS((t, 512), BF16), SDS((t, 1024), BF16), SDS((t, 2048), BF16)),
        (x1, w), sem=("parallel",), bg=bg)


def _mixin_bwd(dx1a, dza, dzuv, dgab, w, tm, bg=None):
    t = dx1a.shape[0]

    def body(d_ref, dza_ref, dzuv_ref, dgab_ref, w_ref, dx_ref):
        dx_ref[...] = (d_ref[...] + _dot_nt(dza_ref[...], w_ref[:, 0:512])
                       + _dot_nt(dzuv_ref[...], w_ref[:, 512:1536])
                       + _dot_nt(dgab_ref[...], w_ref[:, 1536:3584]))

    return _call(
        body, "mixin_bwd", (t // tm,),
        [_rows(tm, D_MODEL), _rows(tm, 512), _rows(tm, 1024), _rows(tm, 2048), _resident((D_MODEL, 3584))],
        (_rows(tm, D_MODEL),), (SDS((t, D_MODEL), F32),),
        (dx1a, dza, dzuv, dgab, w), sem=("parallel",), bg=bg)


def _unrolled(lo, hi, body, carry):
    for j in range(lo, hi):
        carry = body(j, carry)
    return carry


def _scan_fwd(hr_ref, hi_ref, a_ref, ap_ref, carry_ref, seg, cin_ref):
    for lc in range(SSM_LANES // SCAN_LANES):
        ls = slice(lc * SCAN_LANES, (lc + 1) * SCAN_LANES)
        a_r = jnp.broadcast_to(a_ref[0:1, ls], (8, SCAN_LANES))
        a_i = jnp.broadcast_to(a_ref[1:2, ls], (8, SCAN_LANES))

        def step(j, hc, ls=ls, a_r=a_r, a_i=a_i):
            h_r, h_i = hc
            rows = pl.ds(j * 8, 8)
            n_r = a_r * h_r - a_i * h_i + hr_ref[rows, ls]
            n_i = a_r * h_i + a_i * h_r + hi_ref[rows, ls]
            hr_ref[rows, ls] = n_r
            hi_ref[rows, ls] = n_i
            return n_r, n_i

        zero = jnp.zeros((8, SCAN_LANES), F32)
        f_r, f_i = _unrolled(0, seg, step, (zero, zero))
        c_r = carry_ref[0:1, ls]
        c_i = carry_ref[1:2, ls]
        p_r = ap_ref[0:1, ls]
        p_i = ap_ref[1:2, ls]
        rows_r, rows_i = [], []
        for s in range(8):
            rows_r.append(c_r)
            rows_i.append(c_i)
            c_r, c_i = (f_r[s:s + 1] + p_r * c_r - p_i * c_i,
                        f_i[s:s + 1] + p_r * c_i + p_i * c_r)
        carry_ref[0:1, ls] = c_r
        carry_ref[1:2, ls] = c_i
        cin_r = jnp.concatenate(rows_r, axis=0)
        cin_i = jnp.concatenate(rows_i, axis=0)
        if cin_ref is not None:
            cin_ref[0, :, ls] = cin_r
            cin_ref[1, :, ls] = cin_i

        def fix(j, cc, ls=ls, a_r=a_r, a_i=a_i):
            c_r, c_i = cc
            c_r, c_i = a_r * c_r - a_i * c_i, a_r * c_i + a_i * c_r
            rows = pl.ds(j * 8, 8)
            hr_ref[rows, ls] = hr_ref[rows, ls] + c_r
            hi_ref[rows, ls] = hi_ref[rows, ls] + c_i
            return c_r, c_i

        _unrolled(0, seg, fix, (cin_r, cin_i))


def _scan_bwd(gr_ref, gi_ref, hr_ref, hi_ref, cin_ref, a_ref, ap_ref, rcarry_ref, da_ref, seg):
    for lc in range(SSM_LANES // SCAN_LANES):
        ls = slice(lc * SCAN_LANES, (lc + 1) * SCAN_LANES)
        a_r = jnp.broadcast_to(a_ref[0:1, ls], (8, SCAN_LANES))
        a_i = jnp.broadcast_to(a_ref[1:2, ls], (8, SCAN_LANES))

        def step(t, gc, ls=ls, a_r=a_r, a_i=a_i):
            g_r, g_i = gc
            rows = pl.ds((seg - 1 - t) * 8, 8)
            n_r = gr_ref[rows, ls] + a_r * g_r + a_i * g_i
            n_i = gi_ref[rows, ls] + a_r * g_i - a_i * g_r
            gr_ref[rows, ls] = n_r
            gi_ref[rows, ls] = n_i
            return n_r, n_i

        zero = jnp.zeros((8, SCAN_LANES), F32)
        f_r, f_i = _unrolled(0, seg, step, (zero, zero))
        c_r = rcarry_ref[0:1, ls]
        c_i = rcarry_ref[1:2, ls]
        p_r = ap_ref[0:1, ls]
        p_i = ap_ref[1:2, ls]
        rows_r, rows_i = [None] * 8, [None] * 8
        for s in range(7, -1, -1):
            rows_r[s] = c_r
            rows_i[s] = c_i
            c_r, c_i = (f_r[s:s + 1] + p_r * c_r + p_i * c_i,
                        f_i[s:s + 1] + p_r * c_i - p_i * c_r)
        rcarry_ref[0:1, ls] = c_r
        rcarry_ref[1:2, ls] = c_i
        cin_r = jnp.concatenate(rows_r, axis=0)
        cin_i = jnp.concatenate(rows_i, axis=0)

        def fix_row(j_rows, hp_r, hp_i, cc, ls=ls, a_r=a_r, a_i=a_i):
            c_r, c_i, acc_r, acc_i = cc
            c_r, c_i = a_r * c_r + a_i * c_i, a_r * c_i - a_i * c_r
            g_r = gr_ref[j_rows, ls] + c_r
            g_i = gi_ref[j_rows, ls] + c_i
            gr_ref[j_rows, ls] = g_r
            gi_ref[j_rows, ls] = g_i
            acc_r = acc_r + g_r * hp_r + g_i * hp_i
            acc_i = acc_i + g_i * hp_r - g_r * hp_i
            return c_r, c_i, acc_r, acc_i

        def fix(t, cc, ls=ls, fix_row=fix_row):
            j = seg - 1 - t
            rows = pl.ds(j * 8, 8)
            prev = pl.ds((j - 1) * 8, 8)
            return fix_row(rows, hr_ref[prev, ls], hi_ref[prev, ls], cc)

        cc = _unrolled(0, seg - 1, fix, (cin_r, cin_i, zero, zero))
        _, _, acc_r, acc_i = fix_row(pl.ds(0, 8), cin_ref[0, :, ls], cin_ref[1, :, ls], cc)
        da_ref[0, :, ls] += acc_r
        da_ref[1, :, ls] += acc_i


def _s5_fwd(za, sp, bsz, seq, tb, bg=None):
    nb = seq // tb
    seg = tb // 8
    t = bsz * seq

    def body(za_ref, perm_ref, permt_ref, mre_ref, mim_ref, nre_ref, nim_ref, a_ref, ap_ref,
             dsk_ref, gw_ref, gb_ref, out_ref, outt_ref, y2_ref, car_ref, hr_ref, hi_ref, carry_ref):
        @pl.when(pl.program_id(1) == 0)
        def _():
            carry_ref[...] = jnp.zeros_like(carry_ref)

        car_ref[0] = carry_ref[...]
        up = _dot(perm_ref[...], za_ref[...])
        upb = up.astype(BF16)
        for bb in range(S5_BLOCKS):
            ub = upb[:, bb * S5_BLOCK_IN:(bb + 1) * S5_BLOCK_IN]
            st = slice(bb * S5_BLOCK_ST, (bb + 1) * S5_BLOCK_ST)
            hr_ref[:, st] = _dot(ub, mre_ref[bb])
            hi_ref[:, st] = _dot(ub, mim_ref[bb])
        _scan_fwd(hr_ref, hi_ref, a_ref, ap_ref, carry_ref, seg, None)
        ys = []
        for bb in range(S5_BLOCKS):
            st = slice(bb * S5_BLOCK_ST, (bb + 1) * S5_BLOCK_ST)
            ys.append(_dot(hr_ref[:, st].astype(BF16), nre_ref[bb])
                      - _dot(hi_ref[:, st].astype(BF16), nim_ref[bb]))
        y2 = jnp.concatenate(ys, axis=1) + dsk_ref[...] * up
        y2_ref[...] = y2
        y3 = _gelu(y2)
        gl = _dot(y3.astype(BF16), gw_ref[...]) + gb_ref[...]
        oa = y3 * _sigmoid(gl)
        out = _dot(permt_ref[...], oa.astype(BF16)).astype(BF16)
        out_ref[...] = out
        outt_ref[...] = out.T

    blk = pl.BlockSpec((tb, D_SSM), lambda b, j: (b * nb + j, 0))
    blk_t = pl.BlockSpec((D_SSM, tb), lambda b, j: (0, b * nb + j))
    m_shape = (S5_BLOCKS, S5_BLOCK_IN, S5_BLOCK_ST)
    n_shape = (S5_BLOCKS, S5_BLOCK_ST, S5_BLOCK_IN)
    return _call(
        body, "s5_fwd", (bsz, nb),
        [blk, _fixed((tb, tb)), _fixed((tb, tb)), _fixed(m_shape), _fixed(m_shape), _fixed(n_shape),
         _fixed(n_shape), _fixed((2, SSM_LANES)), _fixed((2, SSM_LANES)), _fixed((1, D_SSM)),
         _fixed((D_SSM, D_SSM)), _fixed((1, D_SSM))],
        (blk, blk_t, blk, pl.BlockSpec((1, 2, SSM_LANES), lambda b, j: (b * nb + j, 0, 0))),
        (SDS((t, D_SSM), BF16), SDS((D_SSM, t), BF16), SDS((t, D_SSM), F32), SDS((bsz * nb, 2, SSM_LANES), F32)),
        (za, sp["perm"], sp["permt"], sp["mre"], sp["mim"], sp["nre"], sp["nim"], sp["a"], sp["ap"],
         sp["dskip"], sp["glu_w"], sp["glu_b"]),
        scratch=[pltpu.VMEM((tb, SSM_LANES), F32), pltpu.VMEM((tb, SSM_LANES), F32),
                 pltpu.VMEM((2, SSM_LANES), F32)],
        sem=("arbitrary", "arbitrary"), bg=bg)


def _s5_bwd(za, y2p, doa, carries, sp, bsz, seq, tb, bg=None):
    nb = seq // tb
    seg = tb // 8
    t = bsz * seq

    def body(za_ref, y2_ref, doa_ref, car_ref, perm_ref, permt_ref, mre_ref, mim_ref, mtre_ref, mtim_ref,
             nre_ref, nim_ref, ntre_ref, ntim_ref, a_ref, ap_ref, dsk_ref, gw_ref, gwt_ref, gb_ref,
             dza_ref, dmr_ref, dmi_ref, dnr_ref, dni_ref, da_ref, ddsk_ref, dgw_ref, dgb_ref,
             hr_ref, hi_ref, gr_ref, gi_ref, cin_ref, carry_ref, rcarry_ref):
        first = jnp.logical_and(pl.program_id(0) == 0, pl.program_id(1) == 0)

        @pl.when(first)
        def _():
            for r in (dmr_ref, dmi_ref, dnr_ref, dni_ref, da_ref, ddsk_ref, dgw_ref, dgb_ref):
                r[...] = jnp.zeros_like(r)

        @pl.when(pl.program_id(1) == 0)
        def _():
            rcarry_ref[...] = jnp.zeros_like(rcarry_ref)

        carry_ref[...] = car_ref[0]
        perm = perm_ref[...]
        up = _dot(perm, za_ref[...])
        upb = up.astype(BF16)
        for bb in range(S5_BLOCKS):
            ub = upb[:, bb * S5_BLOCK_IN:(bb + 1) * S5_BLOCK_IN]
            st = slice(bb * S5_BLOCK_ST, (bb + 1) * S5_BLOCK_ST)
            hr_ref[:, st] = _dot(ub, mre_ref[bb])
            hi_ref[:, st] = _dot(ub, mim_ref[bb])
        _scan_fwd(hr_ref, hi_ref, a_ref, ap_ref, carry_ref, seg, cin_ref)

        y2 = y2_ref[...]
        y3 = _gelu(y2)
        y3b = y3.astype(BF16)
        sg = _sigmoid(_dot(y3b, gw_ref[...]) + gb_ref[...])
        d0 = doa_ref[...]
        d_hi = d0.astype(BF16)
        d1 = d0 - d_hi.astype(F32)
        d_mid = d1.astype(BF16)
        d_lo = (d1 - d_mid.astype(F32)).astype(BF16)
        doap = _dot(perm, d_hi) + _dot(perm, d_mid) + _dot(perm, d_lo)
        dgl = doap * y3 * sg * (1.0 - sg)
        dglb = dgl.astype(BF16)
        dy3 = doap * sg + _dot(dglb, gwt_ref[...])
        dgw_ref[...] += _dot_tn(y3b, dglb)
        dgb_ref[...] += jnp.sum(dgl, axis=0, keepdims=True)
        dy2 = dy3 * _gelu_grad(y2)
        ddsk_ref[...] += jnp.sum(dy2 * up, axis=0, keepdims=True)
        dyb = dy2.astype(BF16)
        for bb in range(S5_BLOCKS):
            dyc = dyb[:, bb * S5_BLOCK_IN:(bb + 1) * S5_BLOCK_IN]
            st = slice(bb * S5_BLOCK_ST, (bb + 1) * S5_BLOCK_ST)
            gr_ref[:, st] = _dot(dyc, ntre_ref[bb])
            gi_ref[:, st] = -_dot(dyc, ntim_ref[bb])
            dnr_ref[bb] += _dot_tn(hr_ref[:, st].astype(BF16), dyc)
            dni_ref[bb] += -_dot_tn(hi_ref[:, st].astype(BF16), dyc)
        _scan_bwd(gr_ref, gi_ref, hr_ref, hi_ref, cin_ref, a_ref, ap_ref, rcarry_ref, da_ref, seg)
        dus = []
        for bb in range(S5_BLOCKS):
            st = slice(bb * S5_BLOCK_ST, (bb + 1) * S5_BLOCK_ST)
            grb = gr_ref[:, st].astype(BF16)
            gib = gi_ref[:, st].astype(BF16)
            dus.append(_dot(grb, mtre_ref[bb]) + _dot(gib, mtim_ref[bb]))
            ub = upb[:, bb * S5_BLOCK_IN:(bb + 1) * S5_BLOCK_IN]
            dmr_ref[bb] += _dot_tn(ub, grb)
            dmi_ref[bb] += _dot_tn(ub, gib)
        du = jnp.concatenate(dus, axis=1) + dy2 * dsk_ref[...]
        dza_ref[...] = _dot(permt_ref[...], du.astype(BF16)).astype(BF16)

    def rev(b, j):
        return (b * nb + (nb - 1 - j), 0)

    blk = pl.BlockSpec((tb, D_SSM), rev)
    m_shape = (S5_BLOCKS, S5_BLOCK_IN, S5_BLOCK_ST)
    n_shape = (S5_BLOCKS, S5_BLOCK_ST, S5_BLOCK_IN)
    return _call(
        body, "s5_bwd", (bsz, nb),
        [blk, blk, blk, pl.BlockSpec((1, 2, SSM_LANES), lambda b, j: (b * nb + (nb - 1 - j), 0, 0)),
         _fixed((tb, tb)), _fixed((tb, tb)), _fixed(m_shape), _fixed(m_shape), _fixed(n_shape), _fixed(n_shape),
         _fixed(n_shape), _fixed(n_shape), _fixed(m_shape), _fixed(m_shape),
         _fixed((2, SSM_LANES)), _fixed((2, SSM_LANES)), _fixed((1, D_SSM)),
         _fixed((D_SSM, D_SSM)), _fixed((D_SSM, D_SSM)), _fixed((1, D_SSM))],
        (blk, _fixed(m_shape), _fixed(m_shape), _fixed(n_shape), _fixed(n_shape),
         _fixed((2, 8, SSM_LANES)), _fixed((1, D_SSM)), _fixed((D_SSM, D_SSM)), _fixed((1, D_SSM))),
        (SDS((t, D_SSM), BF16), SDS(m_shape, F32), SDS(m_shape, F32), SDS(n_shape, F32), SDS(n_shape, F32),
         SDS((2, 8, SSM_LANES), F32), SDS((1, D_SSM), F32), SDS((D_SSM, D_SSM), F32), SDS((1, D_SSM), F32)),
        (za, y2p, doa, carries, sp["perm"], sp["permt"], sp["mre"], sp["mim"], sp["mtre"], sp["mtim"],
         sp["nre"], sp["nim"], sp["ntre"], sp["ntim"], sp["a"], sp["ap"], sp["dskip"], sp["glu_w"],
         sp["glu_wt"], sp["glu_b"]),
        scratch=[pltpu.VMEM((tb, SSM_LANES), F32), pltpu.VMEM((tb, SSM_LANES), F32),
                 pltpu.VMEM((tb, SSM_LANES), F32), pltpu.VMEM((tb, SSM_LANES), F32),
                 pltpu.VMEM((2, 8, SSM_LANES), F32), pltpu.VMEM((2, SSM_LANES), F32),
                 pltpu.VMEM((2, SSM_LANES), F32)],
        sem=("arbitrary", "arbitrary"), bg=bg)


def _gmlp_spatial(ws_ref, vb):
    lane = lax.broadcasted_iota(jnp.int32, (CHUNK, 128), 1)
    parts = []
    for j in range(GMLP_HEADS // 2):
        vp = vb[:, 128 * j:128 * (j + 1)]
        parts.append(jnp.where(lane < GMLP_HEAD_DIM, _dot(ws_ref[2 * j], vp), _dot(ws_ref[2 * j + 1], vp)))
    return jnp.concatenate(parts, axis=1)


def _gmlp_fwd(zuv, ln_g, ln_b, wsm, bias, bg=None):
    t = zuv.shape[0]

    def body(z_ref, g_ref, b_ref, ws_ref, bias_ref, out_ref, outt_ref):
        u = _gelu(z_ref[:, 0:D_GMLP].astype(F32))
        v0 = _gelu(z_ref[:, D_GMLP:2 * D_GMLP].astype(F32))
        v, _, _ = _ln_fwd(v0, g_ref[...], b_ref[...])
        s = _gmlp_spatial(ws_ref, v.astype(BF16)) + bias_ref[...]
        out = (u * s).astype(BF16)
        out_ref[...] = out
        outt_ref[...] = out.T

    return _call(
        body, "gmlp_fwd", (t // CHUNK,),
        [_rows(CHUNK, 2 * D_GMLP), _fixed((1, D_GMLP)), _fixed((1, D_GMLP)),
         _fixed((GMLP_HEADS, CHUNK, CHUNK)), _fixed((CHUNK, D_GMLP))],
        (_rows(CHUNK, D_GMLP), _cols(D_GMLP, CHUNK)), (SDS((t, D_GMLP), BF16), SDS((D_GMLP, t), BF16)),
        (zuv, ln_g, ln_b, wsm, bias), sem=("parallel",), bg=bg)


def _gmlp_bwd(zuv, dgm, ln_g, ln_b, wsm, wsmt, bias, bg=None):
    t = zuv.shape[0]

    def body(z_ref, d_ref, g_ref, b_ref, ws_ref, wst_ref, bias_ref,
             dz_ref, dws_ref, dbias_ref, dg_ref, db_ref):
        @pl.when(pl.program_id(0) == 0)
        def _():
            for r in (dws_ref, dbias_ref, dg_ref, db_ref):
                r[...] = jnp.zeros_like(r)

        zu = z_ref[:, 0:D_GMLP].astype(F32)
        zv = z_ref[:, D_GMLP:2 * D_GMLP].astype(F32)
        u = _gelu(zu)
        v0 = _gelu(zv)
        gam = g_ref[...]
        v, vhat, rstd = _ln_fwd(v0, gam, b_ref[...])
        vb = v.astype(BF16)
        s = _gmlp_spatial(ws_ref, vb) + bias_ref[...]
        d = d_ref[...]
        dz_ref[:, 0:D_GMLP] = (d * s * _gelu_grad(zu)).astype(BF16)
        ds = d * u
        dbias_ref[...] += ds
        dsb = ds.astype(BF16)
        lane = lax.broadcasted_iota(jnp.int32, (CHUNK, 128), 1)
        tril = (lax.broadcasted_iota(jnp.int32, (CHUNK, CHUNK), 0)
                >= lax.broadcasted_iota(jnp.int32, (CHUNK, CHUNK), 1))
        zero_b = jnp.zeros((CHUNK, 128), BF16)
        parts = []
        for j in range(GMLP_HEADS // 2):
            dsp = dsb[:, 128 * j:128 * (j + 1)]
            vp = vb[:, 128 * j:128 * (j + 1)]
            parts.append(jnp.where(lane < GMLP_HEAD_DIM, _dot(wst_ref[2 * j], dsp),
                                   _dot(wst_ref[2 * j + 1], dsp)))
            lo = jnp.where(lane < GMLP_HEAD_DIM, dsp, zero_b)
            hi = jnp.where(lane < GMLP_HEAD_DIM, zero_b, dsp)
            dws_ref[2 * j] += jnp.where(tril, _dot_nt(lo, vp), 0.0)
            dws_ref[2 * j + 1] += jnp.where(tril, _dot_nt(hi, vp), 0.0)
        dv = jnp.concatenate(parts, axis=1)
        dg_ref[...] += jnp.sum(dv * vhat, axis=0, keepdims=True)
        db_ref[...] += jnp.sum(dv, axis=0, keepdims=True)
        dz_ref[:, D_GMLP:2 * D_GMLP] = (_ln_bwd(dv, vhat, rstd, gam) * _gelu_grad(zv)).astype(BF16)

    return _call(
        body, "gmlp_bwd", (t // CHUNK,),
        [_rows(CHUNK, 2 * D_GMLP), _rows(CHUNK, D_GMLP), _fixed((1, D_GMLP)), _fixed((1, D_GMLP)),
         _fixed((GMLP_HEADS, CHUNK, CHUNK)), _fixed((GMLP_HEADS, CHUNK, CHUNK)), _fixed((CHUNK, D_GMLP))],
        (_rows(CHUNK, 2 * D_GMLP), _fixed((GMLP_HEADS, CHUNK, CHUNK)), _fixed((CHUNK, D_GMLP)),
         _fixed((1, D_GMLP)), _fixed((1, D_GMLP))),
        (SDS((t, 2 * D_GMLP), BF16), SDS((GMLP_HEADS, CHUNK, CHUNK), F32), SDS((CHUNK, D_GMLP), F32),
         SDS((1, D_GMLP), F32), SDS((1, D_GMLP), F32)),
        (zuv, dgm, ln_g, ln_b, wsm, wsmt, bias), sem=("arbitrary",), bg=bg)


def _mixout_fwd(x1, s5o, gm, gab, ua, ub, wmo, g, b, tm, bg=None):
    t = x1.shape[0]

    def body(x_ref, s_ref, m_ref, gab_ref, ua_ref, ub_ref, wmo_ref, g_ref, b_ref,
             xn_ref, xh_ref, rstd_ref):
        ya = _dot(s_ref[...], ua_ref[...])
        yb = _dot(m_ref[...], ub_ref[...])
        mix = (_sigmoid(gab_ref[:, 0:D_MODEL].astype(F32)) * ya
               + _sigmoid(gab_ref[:, D_MODEL:2 * D_MODEL].astype(F32)) * yb)
        r = ALPHA * x_ref[...] + _dot(mix.astype(BF16), wmo_ref[...])
        y, xh, rstd = _ln_fwd(r, g_ref[...], b_ref[...])
        xn_ref[...] = y
        xh_ref[...] = xh
        rstd_ref[...] = rstd

    return _call(
        body, "mixout_fwd", (t // tm,),
        [_rows(tm, D_MODEL), _rows(tm, D_SSM), _rows(tm, D_GMLP), _rows(tm, 2 * D_MODEL),
         _resident((D_SSM, D_MODEL)), _resident((D_GMLP, D_MODEL)), _resident((D_MODEL, D_MODEL)),
         _fixed((1, D_MODEL)), _fixed((1, D_MODEL))],
        (_rows(tm, D_MODEL), _rows(tm, D_MODEL), _rows(tm, 1)),
        (SDS((t, D_MODEL), F32), SDS((t, D_MODEL), F32), SDS((t, 1), F32)),
        (x1, s5o, gm, gab, ua, ub, wmo, g, b), sem=("parallel",), bg=bg)


def _mixout_bwd(dx2, xh, rstd, s5o, gm, gab, ua, ub, wmo, g, tm, bg=None):
    t = dx2.shape[0]

    def body(d_ref, xh_ref, rstd_ref, s_ref, m_ref, gab_ref, ua_ref, ub_ref, wmo_ref, g_ref,
             dx1_ref, dmx_ref, mb_ref, dya_ref, dyb_ref, ds5_ref, dgm_ref, dgab_ref, dg_ref, db_ref):
        @pl.when(pl.program_id(0) == 0)
        def _():
            dg_ref[...] = jnp.zeros_like(dg_ref)
            db_ref[...] = jnp.zeros_like(db_ref)

        dy = d_ref[...]
        xhv = xh_ref[...]
        dr = _ln_bwd(dy, xhv, rstd_ref[...], g_ref[...])
        dg_ref[...] += jnp.sum(dy * xhv, axis=0, keepdims=True)
        db_ref[...] += jnp.sum(dy, axis=0, keepdims=True)
        dx1_ref[...] = ALPHA * dr
        drb = dr.astype(BF16)
        dmx_ref[...] = drb
        dm = _dot_nt(drb, wmo_ref[...])
        ya = _dot(s_ref[...], ua_ref[...])
        yb = _dot(m_ref[...], ub_ref[...])
        sa = _sigmoid(gab_ref[:, 0:D_MODEL].astype(F32))
        sb = _sigmoid(gab_ref[:, D_MODEL:2 * D_MODEL].astype(F32))
        mb_ref[...] = (sa * ya + sb * yb).astype(BF16).T
        dya = (dm * sa).astype(BF16)
        dyb = (dm * sb).astype(BF16)
        dya_ref[...] = dya
        dyb_ref[...] = dyb
        dgab_ref[:, 0:D_MODEL] = (dm * ya * sa * (1.0 - sa)).astype(BF16)
        dgab_ref[:, D_MODEL:2 * D_MODEL] = (dm * yb * sb * (1.0 - sb)).astype(BF16)
        ds5_ref[...] = _dot_nt(dya, ua_ref[...])
        dgm_ref[...] = _dot_nt(dyb, ub_ref[...])

    return _call(
        body, "mixout_bwd", (t // tm,),
        [_rows(tm, D_MODEL), _rows(tm, D_MODEL), _rows(tm, 1), _rows(tm, D_SSM), _rows(tm, D_GMLP),
         _rows(tm, 2 * D_MODEL), _resident((D_SSM, D_MODEL)), _resident((D_GMLP, D_MODEL)),
         _resident((D_MODEL, D_MODEL)), _fixed((1, D_MODEL))],
        (_rows(tm, D_MODEL), _rows(tm, D_MODEL), _cols(D_MODEL, tm), _rows(tm, D_MODEL),
         _rows(tm, D_MODEL), _rows(tm, D_SSM), _rows(tm, D_GMLP), _rows(tm, 2 * D_MODEL),
         _fixed((1, D_MODEL)), _fixed((1, D_MODEL))),
        (SDS((t, D_MODEL), F32), SDS((t, D_MODEL), BF16), SDS((D_MODEL, t), BF16),
         SDS((t, D_MODEL), BF16), SDS((t, D_MODEL), BF16), SDS((t, D_SSM), F32),
         SDS((t, D_GMLP), F32), SDS((t, 2 * D_MODEL), BF16),
         SDS((1, D_MODEL), F32), SDS((1, D_MODEL), F32)),
        (dx2, xh, rstd, s5o, gm, gab, ua, ub, wmo, g), sem=("arbitrary",), bg=bg)


def _ple_loss(x3, p, tgt, wpg, wpp, tm, bg=None):
    t = x3.shape[0]

    def body(x_ref, p_ref, t_ref, wpg_ref, wpp_ref, dx_ref, xb_ref, pb_ref, dq_ref, de_ref, loss_ref):
        @pl.when(pl.program_id(0) == 0)
        def _():
            loss_ref[...] = jnp.zeros_like(loss_ref)

        x3v = x_ref[...]
        xb = x3v.astype(BF16)
        pb = p_ref[...].astype(BF16)
        xb_ref[...] = xb.T
        pb_ref[...] = pb.T
        s = _sigmoid(_dot(xb, wpg_ref[...]))
        e = _dot(pb, wpp_ref[...])
        diff = x3v + s * e - t_ref[...]
        loss_ref[...] += jnp.sum(diff * diff, axis=0, keepdims=True)
        dout = diff * (1.0 / D_MODEL)
        de_ref[...] = (dout * s).astype(BF16)
        dq = (dout * e * s * (1.0 - s)).astype(BF16)
        dq_ref[...] = dq
        dx_ref[...] = dout + _dot_nt(dq, wpg_ref[...])

    return _call(
        body, "ple_loss", (t // tm,),
        [_rows(tm, D_MODEL), _rows(tm, PLE_DIM), _rows(tm, D_MODEL),
         _resident((D_MODEL, D_MODEL)), _resident((PLE_DIM, D_MODEL))],
        (_rows(tm, D_MODEL), _cols(D_MODEL, tm), _cols(PLE_DIM, tm), _rows(tm, D_MODEL),
         _rows(tm, D_MODEL), _fixed((1, D_MODEL))),
        (SDS((t, D_MODEL), F32), SDS((D_MODEL, t), BF16), SDS((PLE_DIM, t), BF16),
         SDS((t, D_MODEL), BF16), SDS((t, D_MODEL), BF16), SDS((1, D_MODEL), F32)),
        (x3, p, tgt, wpg, wpp), sem=("arbitrary",), bg=bg)


def _s5_discretise(lre, lim, log_dt, bre, bim):
    dt = jnp.exp(log_dt)[:, None]
    mag = jnp.exp(lre * dt)
    abr = mag * jnp.cos(lim * dt)
    abi = mag * jnp.sin(lim * dt)
    nr = abr - 1.0
    ni = abi
    den = lre * lre + lim * lim
    cr = ((nr * lre + ni * lim) / den)[..., None]
    ci = ((ni * lre - nr * lim) / den)[..., None]
    return abr, abi, cr * bre - ci * bim, cr * bim + ci * bre


def _block_diag_in(bb):
    v = bb.reshape(S5_BLOCKS, 8, SSM_STATE, SSM_GROUP_CH).transpose(0, 1, 3, 2)
    return jnp.einsum("bgip,gh->bgihp", v, jnp.eye(8, dtype=bb.dtype)).reshape(
        S5_BLOCKS, S5_BLOCK_IN, S5_BLOCK_ST)


def _block_diag_in_t(dm):
    v = dm.reshape(S5_BLOCKS, 8, SSM_GROUP_CH, 8, SSM_STATE)
    d = jnp.einsum("bgihp,gh->bgip", v, jnp.eye(8, dtype=dm.dtype))
    return d.transpose(0, 1, 3, 2).reshape(SSM_GROUPS, SSM_STATE, SSM_GROUP_CH)


def _block_diag_out(cc):
    v = cc.reshape(S5_BLOCKS, 8, SSM_GROUP_CH, SSM_STATE)
    return jnp.einsum("bgip,gh->bgphi", v, jnp.eye(8, dtype=cc.dtype)).reshape(
        S5_BLOCKS, S5_BLOCK_ST, S5_BLOCK_IN)


def _block_diag_out_t(dn):
    v = dn.reshape(S5_BLOCKS, 8, SSM_STATE, 8, SSM_GROUP_CH)
    d = jnp.einsum("bgphi,gh->bgip", v, jnp.eye(8, dtype=dn.dtype))
    return d.reshape(SSM_GROUPS, SSM_GROUP_CH, SSM_STATE)


def _s5_setup(lre, lim, log_dt, bre, bim, cre, cim, d_skip, glu_w, glu_b, tb):
    seg = tb // 8
    abr, abi, bbr, bbi = _s5_discretise(lre, lim, log_dt, bre, bim)
    pr, pi = abr, abi
    for _ in range(int(math.log2(seg))):
        pr, pi = pr * pr - pi * pi, 2.0 * pr * pi
    rows = jnp.arange(tb)
    src = (rows % 8) * seg + rows // 8
    perm = (src[:, None] == jnp.arange(tb)[None, :]).astype(BF16)
    mre = _block_diag_in(bbr)
    mim = _block_diag_in(bbi)
    nre = _block_diag_out(cre)
    nim = _block_diag_out(cim)
    return {
        "perm": perm, "permt": perm.T,
        "mre": mre.astype(BF16), "mim": mim.astype(BF16),
        "mtre": mre.transpose(0, 2, 1).astype(BF16), "mtim": mim.transpose(0, 2, 1).astype(BF16),
        "nre": nre.astype(BF16), "nim": nim.astype(BF16),
        "ntre": nre.transpose(0, 2, 1).astype(BF16), "ntim": nim.transpose(0, 2, 1).astype(BF16),
        "a": jnp.stack([abr.reshape(-1), abi.reshape(-1)]),
        "ap": jnp.stack([pr.reshape(-1), pi.reshape(-1)]),
        "dskip": d_skip.reshape(1, D_SSM), "glu_w": glu_w, "glu_wt": glu_w.T,
        "glu_b": glu_b.reshape(1, D_SSM),
    }


BIG = ("ffn1_w_in", "ffn1_w_out", "mix_w_in", "ssm_glu_w", "up_a", "up_b", "mix_w_out",
       "ffn2_w_in", "ffn2_w_out", "ple_w_proj", "ple_w_gate")
BIG_AXIS = {"ffn1_w_in": 1, "ffn1_w_out": 0, "mix_w_in": 1, "ssm_glu_w": 0, "up_a": 1, "up_b": 1,
            "mix_w_out": 0, "ffn2_w_in": 1, "ffn2_w_out": 0, "ple_w_proj": 1, "ple_w_gate": 0}
SHARD_MAJOR = 2
GATHER_AXIS = dict(BIG_AXIS, ffn1_w_in=SHARD_MAJOR, ffn2_w_in=SHARD_MAJOR)
GATHER_ORDER = (("ffn1_w_in",), ("ffn1_w_out",), ("mix_w_in",), ("ssm_glu_w", "up_a", "up_b", "mix_w_out"),
                ("ffn2_w_in",), ("ffn2_w_out", "ple_w_gate", "ple_w_proj"))
GATHER_FIRST_ID = 1
SMALL = ("ln1_g", "ln1_b", "ssm_lambda_re", "ssm_lambda_im", "ssm_log_dt", "ssm_b_re", "ssm_b_im",
         "ssm_c_re", "ssm_c_im", "ssm_d", "ssm_glu_b", "gmlp_ln_g", "gmlp_ln_b", "gmlp_w_s",
         "gmlp_b_s", "ln2_g", "ln2_b", "ln3_g", "ln3_b")
SMALL_VIEW = {"ssm_b_re": (SSM_GROUPS, SSM_STATE * SSM_GROUP_CH), "ssm_b_im": (SSM_GROUPS, SSM_STATE * SSM_GROUP_CH)}


def _small_view(k, a):
    return a.reshape(SMALL_VIEW[k]) if k in SMALL_VIEW else a


def _place():
    return lax.axis_index("x"), lax.axis_index("y"), lax.axis_index("c")


def _other_chips(x, y):
    return [(1 - x, y), (x, 1 - y), (1 - x, 1 - y)]


def _window(ref, shard_shape, axis, chip, half):
    r, c = shard_shape
    hr = r // 2
    if axis == SHARD_MAJOR:
        return ref.at[chip] if half is None else ref.at[chip, pl.ds(half * hr, hr), :]
    if axis == 0:
        if half is None:
            return ref.at[pl.ds(chip * r, r), :]
        return ref.at[pl.ds(chip * r + half * hr, hr), :]
    if half is None:
        return ref.at[:, pl.ds(chip * c, c)]
    return ref.at[pl.ds(half * hr, hr), pl.ds(chip * c, c)]


def _gather_weights(shards, axes):
    n = len(shards)
    shapes = [s.shape for s in shards]
    full = [{0: (4 * r, c), 1: (r, 4 * c), SHARD_MAJOR: (4, r, c)}[ax] for (r, c), ax in zip(shapes, axes)]

    def remote(sems, i, k, src, dst, to):
        return pltpu.make_async_remote_copy(src_ref=src, dst_ref=dst, send_sem=sems[0].at[6 * i + k],
                                            recv_sem=sems[1].at[6 * i + k], device_id=to, device_id_type=MESH)

    def own_copies(ins, outs, sems):
        x, y, c = _place()
        me = 2 * x + y
        cps = []
        for i in range(n):
            hr = shapes[i][0] // 2
            mine = ins[i].at[pl.ds(c * hr, hr), :]
            for j, (cx, cy) in enumerate(_other_chips(x, y)):
                cps.append(remote(sems, i, j, mine, _window(outs[i], shapes[i], axes[i], me, c), (cx, cy, c)))
        local = [pltpu.make_async_copy(ins[i], _window(outs[i], shapes[i], axes[i], me, None), sems[2].at[i])
                 for i in range(n)]
        return cps, local

    def start(ins, outs, sems):
        cps, local = own_copies(ins, outs, sems)
        for cp in local + cps:
            cp.start()

    def finish(ins, outs, sems):
        x, y, c = _place()
        sibling = (x, y, 1 - c)
        passed = []
        for j, (cx, cy) in enumerate(_other_chips(x, y)):
            for i in range(n):
                w = _window(outs[i], shapes[i], axes[i], 2 * cx + cy, c)
                remote(sems, i, j, w, w, (cx, cy, c)).wait_recv()
                cp = remote(sems, i, 3 + j, w, w, sibling)
                cp.start()
                passed.append(cp)
        for j, (cx, cy) in enumerate(_other_chips(x, y)):
            for i in range(n):
                w = _window(outs[i], shapes[i], axes[i], 2 * cx + cy, 1 - c)
                remote(sems, i, 3 + j, w, w, sibling).wait_recv()
        cps, local = own_copies(ins, outs, sems)
        for cp in cps + passed:
            cp.wait_send()
        for cp in local:
            cp.wait()

    return _Exchange(shards, [SDS(f, BF16) for f in full],
                     [pltpu.SemaphoreType.DMA((6 * n,)), pltpu.SemaphoreType.DMA((6 * n,)),
                      pltpu.SemaphoreType.DMA((n,))], start, finish)


def _scatter_grads(parts, shapes, axes):
    n = len(parts)

    def copies(ins, outs, sems):
        x, y, c = _place()
        return [pltpu.make_async_remote_copy(
            src_ref=_window(ins[i], shapes[i], axes[i], 2 * cx + cy, None), dst_ref=outs[i].at[j],
            send_sem=sems[0].at[3 * i + j], recv_sem=sems[1].at[3 * i + j],
            device_id=(cx, cy, c), device_id_type=MESH)
            for i in range(n) for j, (cx, cy) in enumerate(_other_chips(x, y))]

    def start(ins, outs, sems):
        for cp in copies(ins, outs, sems):
            cp.start()

    def finish(ins, outs, sems):
        for cp in copies(ins, outs, sems):
            cp.wait()

    return _Exchange(parts, [SDS((3,) + tuple(s), BF16) for s in shapes],
                     [pltpu.SemaphoreType.DMA((3 * n,)), pltpu.SemaphoreType.DMA((3 * n,))], start, finish)


def _swap_halves(parts, shapes, axes):
    n = len(parts)

    def copies(ins, outs, sems):
        x, y, c = _place()
        cps = []
        for i in range(n):
            r, _ = shapes[i]
            hr = r // 2
            if axes[i] == 0:
                cps += [pltpu.make_async_remote_copy(
                    src_ref=ins[i].at[pl.ds(k * r + (1 - c) * hr, hr), :], dst_ref=outs[i].at[k],
                    send_sem=sems[0].at[i], recv_sem=sems[1].at[i], device_id=(x, y, 1 - c),
                    device_id_type=MESH) for k in range(4)]
            else:
                cps.append(pltpu.make_async_remote_copy(
                    src_ref=ins[i].at[pl.ds((1 - c) * hr, hr), :], dst_ref=outs[i],
                    send_sem=sems[0].at[i], recv_sem=sems[1].at[i], device_id=(x, y, 1 - c),
                    device_id_type=MESH))
        return cps

    def start(ins, outs, sems):
        for cp in copies(ins, outs, sems):
            cp.start()

    def finish(ins, outs, sems):
        x, y, c = _place()
        for i in range(n):
            pltpu.make_async_remote_copy(src_ref=outs[i], dst_ref=outs[i], send_sem=sems[0].at[i],
                                         recv_sem=sems[1].at[i], device_id=(x, y, 1 - c),
                                         device_id_type=MESH).wait()

    out = [SDS((4, r // 2, c), BF16) if ax == 0 else SDS((r // 2, 4 * c), BF16)
           for (r, c), ax in zip(shapes, axes)]
    return _Exchange(parts, out, [pltpu.SemaphoreType.DMA((n,)), pltpu.SemaphoreType.DMA((n,))], start, finish)


def _scatter_halves(pres, shapes):
    n = len(pres)

    def copies(ins, outs, sems):
        x, y, c = _place()
        return [pltpu.make_async_remote_copy(
            src_ref=ins[i].at[1 + j], dst_ref=outs[i].at[j], send_sem=sems[0].at[3 * i + j],
            recv_sem=sems[1].at[3 * i + j], device_id=(cx, cy, c), device_id_type=MESH)
            for i in range(n) for j, (cx, cy) in enumerate(_other_chips(x, y))]

    def start(ins, outs, sems):
        for cp in copies(ins, outs, sems):
            cp.start()

    def finish(ins, outs, sems):
        for cp in copies(ins, outs, sems):
            cp.wait()

    return _Exchange(pres, [SDS((3, r // 2, c), BF16) for r, c in shapes],
                     [pltpu.SemaphoreType.DMA((3 * n,)), pltpu.SemaphoreType.DMA((3 * n,))], start, finish)


def _swap_with_sibling(arrs):
    n = len(arrs)

    def copies(ins, outs, sems):
        x, y, c = _place()
        return [pltpu.make_async_remote_copy(src_ref=ins[i], dst_ref=outs[i], send_sem=sems[0].at[i],
                                             recv_sem=sems[1].at[i], device_id=(x, y, 1 - c),
                                             device_id_type=MESH) for i in range(n)]

    def start(ins, outs, sems):
        for cp in copies(ins, outs, sems):
            cp.start()

    def finish(ins, outs, sems):
        for cp in copies(ins, outs, sems):
            cp.wait()

    return _Exchange(arrs, [SDS(a.shape, a.dtype) for a in arrs],
                     [pltpu.SemaphoreType.DMA((n,)), pltpu.SemaphoreType.DMA((n,))], start, finish)


def _gather_small(arrs):
    n = len(arrs)

    def copy(sems, outs, i, k, block, to, src=None):
        px, py, pc = block
        dst = outs[i].at[4 * px + 2 * py + pc]
        return pltpu.make_async_remote_copy(
            src_ref=dst if src is None else src, dst_ref=dst, send_sem=sems[0].at[7 * i + k],
            recv_sem=sems[1].at[7 * i + k], device_id=to, device_id_type=MESH)

    direct = [math.prod(a.shape) * 4 <= DIRECT_GATHER_BYTES for a in arrs]

    def own_copies(ins, outs, sems):
        x, y, c = _place()
        cps = []
        for i in range(n):
            cps.append(copy(sems, outs, i, 0, (x, y, c), (x, y, 1 - c), src=ins[i]))
            for j, (cx, cy) in enumerate(_other_chips(x, y)):
                cps.append(copy(sems, outs, i, 1 + j, (x, y, c), (cx, cy, c), src=ins[i]))
                if direct[i]:
                    cps.append(copy(sems, outs, i, 4 + j, (x, y, c), (cx, cy, 1 - c), src=ins[i]))
        local = [pltpu.make_async_copy(ins[i], outs[i].at[4 * x + 2 * y + c], sems[2].at[i]) for i in range(n)]
        return cps, local

    def start(ins, outs, sems):
        cps, local = own_copies(ins, outs, sems)
        for cp in local + cps:
            cp.start()

    def finish(ins, outs, sems):
        x, y, c = _place()
        passed = []
        for j, (cx, cy) in enumerate(_other_chips(x, y)):
            for i in range(n):
                copy(sems, outs, i, 1 + j, (cx, cy, c), (x, y, c)).wait_recv()
                if not direct[i]:
                    cp = copy(sems, outs, i, 4 + j, (cx, cy, c), (x, y, 1 - c))
                    cp.start()
                    passed.append(cp)
        for i in range(n):
            copy(sems, outs, i, 0, (x, y, 1 - c), (x, y, c)).wait_recv()
            for j, (cx, cy) in enumerate(_other_chips(x, y)):
                copy(sems, outs, i, 4 + j, (cx, cy, 1 - c), (x, y, c)).wait_recv()
        cps, local = own_copies(ins, outs, sems)
        for cp in cps + passed:
            cp.wait_send()
        for cp in local:
            cp.wait()

    return _Exchange(arrs, [SDS((N_DEV,) + a.shape, F32) for a in arrs],
                     [pltpu.SemaphoreType.DMA((7 * n,)), pltpu.SemaphoreType.DMA((7 * n,)),
                      pltpu.SemaphoreType.DMA((n,))], start, finish)


def _local_step(x, p, tgt, wb, ws, shards=None):
    bsz, seq, _ = x.shape
    t = bsz * seq
    tm = min(256, t)
    tb = min(256, seq)
    x0 = x.reshape(t, D_MODEL)
    p0 = p.reshape(t, PLE_DIM)
    tg = tgt.reshape(t, D_MODEL)
    row = lambda v: v.reshape(1, -1)
    dist = shards is not None
    wb = dict(wb)
    recv, sums, other, gathered = {}, {}, {}, {}
    gb = {}
    gs = {}
    shape_of, axis_of = {}, {}
    chip = None
    if dist:
        shape_of = {k: tuple(shards[k].shape) for k in BIG}
        axis_of = dict(BIG_AXIS)
        for q in range(LAST_PIECES):
            shape_of[LAST_PIECE % q] = (D_MODEL // LAST_PIECES, shape_of["ffn1_w_in"][1])
            axis_of[LAST_PIECE % q] = 1
        xi, yi, ci = _place()
        chip = (2 * xi + yi).astype(jnp.int32).reshape(1)
        ids = jnp.stack([2 * xi + yi] + [2 * cx + cy for cx, cy in _other_chips(xi, yi)] + [ci]).astype(jnp.int32)
    halfbuf, pre = {}, {}

    def gather(names):
        return _gather_weights([shards[k] for k in names], [GATHER_AXIS[k] for k in names]) if dist else None

    def exchange(scat=(), swap=(), halves=(), scat2=(), swap2=(), extra=None):
        if not dist:
            return None, []
        parts, tags = [], []
        if scat:
            parts.append(_scatter_grads([gb[k][1] for k in scat], [shape_of[k] for k in scat],
                                        [axis_of[k] for k in scat]))
            tags.append((recv, scat))
        if swap:
            for k in swap:
                sums[k] = _sum_blocks(gb[k][0], recv[k], shape_of[k], axis_of[k], chip, "sum_" + k)
            parts.append(_swap_with_sibling([sums[k] for k in swap]))
            tags.append((other, swap))
        if halves:
            parts.append(_swap_halves([gb[k][1] for k in halves], [shape_of[k] for k in halves],
                                      [axis_of[k] for k in halves]))
            tags.append((halfbuf, halves))
        if scat2:
            for k in scat2:
                pre[k] = _presum(gb[k][0], halfbuf[k], shape_of[k], axis_of[k], ids, "presum_" + k)
            parts.append(_scatter_halves([pre[k][1] for k in scat2], [shape_of[k] for k in scat2]))
            tags.append((recv, scat2))
        if swap2:
            for k in swap2:
                sums[k] = _sum_half(pre[k][0], recv[k], "sum_" + k)
            parts.append(_swap_with_sibling([sums[k] for k in swap2]))
            tags.append((other, swap2))
        if extra is not None:
            parts.append(extra[0])
            tags.append((extra[1], extra[2]))
        return (_join(parts), tags) if parts else (None, [])

    def take(ex_tags, got):
        ex, tags = ex_tags
        if ex is not None:
            for (dst, names), (o0, o1) in zip(tags, ex.cuts):
                dst.update(zip(names, got[o0:o1]))

    small_shape = {k: _small_view(k, v).shape for k, v in ws.items()}
    small_shape["loss_rows"] = (1, D_MODEL)
    ws = {k: v if (v.ndim == 2 and k != "ssm_log_dt") else v[0] for k, v in ws.items()}
    tril = jnp.tril(jnp.ones((CHUNK, CHUNK), dtype=bool))
    wsm = jnp.where(tril[None], ws["gmlp_w_s"], 0.0)
    wsm_b = wsm.astype(BF16)
    wsmt_b = wsm.transpose(0, 2, 1).astype(BF16)
    bias = jnp.repeat(ws["gmlp_b_s"].T, GMLP_HEAD_DIM, axis=1)

    tf = min(512, t)
    if dist:
        for gi, names in enumerate(GATHER_ORDER):
            wb.update(zip(names, _run_exchange_on_sequencer(gather(names), "gather_%d" % gi, GATHER_FIRST_ID + gi)))
    (x0b, h1, a1), _ = _ffn_proj(x0, wb["ffn1_w_in"], tf, "ffn1_proj")
    (x1, xh1, rstd1), _ = _ffn_out(x0, a1, wb["ffn1_w_out"], row(ws["ln1_g"]), row(ws["ln1_b"]), tf, "ffn1_out")
    sp = _s5_setup(ws["ssm_lambda_re"], ws["ssm_lambda_im"], ws["ssm_log_dt"], ws["ssm_b_re"],
                   ws["ssm_b_im"], ws["ssm_c_re"], ws["ssm_c_im"], ws["ssm_d"], wb["ssm_glu_w"],
                   ws["ssm_glu_b"], tb)
    (x1b, za, zuv, gab), _ = _mixin_fwd(x1, wb["mix_w_in"], tm)
    (s5o, s5ot, y2p, carries), _ = _s5_fwd(za, sp, bsz, seq, tb)
    (gm, gmt), _ = _gmlp_fwd(zuv, row(ws["gmlp_ln_g"]), row(ws["gmlp_ln_b"]), wsm_b, bias)
    (x2, xh2, rstd2), _ = _mixout_fwd(x1, s5o, gm, gab, wb["up_a"], wb["up_b"], wb["mix_w_out"],
                                           row(ws["ln2_g"]), row(ws["ln2_b"]), tm)
    (x2b, h2, a2), _ = _ffn_proj(x2, wb["ffn2_w_in"], tf, "ffn2_proj")
    (x3, xh3, rstd3), _ = _ffn_out(x2, a2, wb["ffn2_w_out"], row(ws["ln3_g"]), row(ws["ln3_b"]), tf, "ffn2_out")
    (dx3, x3b, pb, dq, de, loss_rows), _ = _ple_loss(x3, p0, tg, wb["ple_w_gate"], wb["ple_w_proj"], tm)
    gb["ple_w_gate"], _ = _tn_matmul(x3b, dq, "dw_ple_gate", 1024, 1024, a_t=True)
    gb["ple_w_proj"], _ = _tn_matmul(pb, de, "dw_ple_proj", 256, 1024, a_t=True)
    et = exchange(scat=("ple_w_gate", "ple_w_proj"))
    (dx2, dh2, df2, gs["ln3_g"], gs["ln3_b"]), got = _ffn_bwd(
        dx3, xh3, rstd3, h2, wb["ffn2_w_in"], wb["ffn2_w_out"], row(ws["ln3_g"]), tm, "ffn2_bwd", et[0])
    take(et, got)
    gb["ffn2_w_out"], _ = _tn_matmul(a2, df2, "dw_ffn2_out", 1408, 1024)
    et = exchange(scat=("ffn2_w_out",))
    gb["ffn2_w_in"], got = _tn_matmul(x2b, dh2, "dw_ffn2_in", 1024, 1408, bg=et[0], a_t=True)
    take(et, got)
    et = exchange(swap=("ple_w_gate", "ple_w_proj", "ffn2_w_out"))
    (dx1a, dmx, mb, dya, dyb, ds5, dgm, dgab, gs["ln2_g"], gs["ln2_b"]), got = _mixout_bwd(
        dx2, xh2, rstd2, s5o, gm, gab, wb["up_a"], wb["up_b"], wb["mix_w_out"], row(ws["ln2_g"]), tm, et[0])
    take(et, got)
    gb["mix_w_out"], _ = _tn_matmul(mb, dmx, "dw_mix_out", 1024, 1024, a_t=True)
    gb["up_a"], _ = _tn_matmul(s5ot, dya, "dw_up_a", 512, 1024, a_t=True)
    gb["up_b"], _ = _tn_matmul(gmt, dyb, "dw_up_b", 512, 1024, a_t=True)
    et = exchange(scat=("ffn2_w_in",))
    (dza, dmr, dmi, dnr, dni, da, ddsk, dgw, dgb), got = _s5_bwd(za, y2p, ds5, carries, sp, bsz, seq, tb, et[0])
    take(et, got)
    gb["ssm_glu_w"] = (dgw, dgw.astype(BF16))
    et = exchange(scat=("mix_w_out", "up_a"), swap=("ffn2_w_in",))
    (dzuv, dws, dbias, gs["gmlp_ln_g"], gs["gmlp_ln_b"]), got = _gmlp_bwd(
        zuv, dgm, row(ws["gmlp_ln_g"]), row(ws["gmlp_ln_b"]), wsm_b, wsmt_b, bias, et[0])
    take(et, got)
    et = exchange(scat=("up_b", "ssm_glu_w"))
    (dx1,), got = _mixin_bwd(dx1a, dza, dzuv, dgab, wb["mix_w_in"], tm, et[0])
    take(et, got)
    g_mi, _ = _tn_matmul(x1b, dza, "dw_mix_in_a", 1024, 512, 0, 3584, a_t=True)
    g_mi, _ = _tn_matmul(x1b, dzuv, "dw_mix_in_uv", 1024, 512, 1, 3584, g_mi, a_t=True)
    et = exchange(swap=("mix_w_out", "up_a", "up_b", "ssm_glu_w"))
    gb["mix_w_in"], got = _tn_matmul(x1b, dgab, "dw_mix_in_g", 1024, 512, 3, 3584, g_mi, bg=et[0], a_t=True)
    take(et, got)

    d_abr = da[0].sum(axis=0).reshape(SSM_GROUPS, SSM_STATE)
    d_abi = da[1].sum(axis=0).reshape(SSM_GROUPS, SSM_STATE)
    _, vjp = jax.vjp(_s5_discretise, ws["ssm_lambda_re"], ws["ssm_lambda_im"], ws["ssm_log_dt"],
                     ws["ssm_b_re"], ws["ssm_b_im"])
    (gs["ssm_lambda_re"], gs["ssm_lambda_im"], gs["ssm_log_dt"], gs["ssm_b_re"], gs["ssm_b_im"]) = vjp(
        (d_abr, d_abi, _block_diag_in_t(dmr), _block_diag_in_t(dmi)))
    gs["ssm_c_re"] = _block_diag_out_t(dnr)
    gs["ssm_c_im"] = _block_diag_out_t(dni)
    gs["ssm_d"] = ddsk
    gs["ssm_glu_b"] = dgb
    gs["gmlp_w_s"] = dws
    gs["gmlp_b_s"] = dbias.reshape(CHUNK, GMLP_HEADS, GMLP_HEAD_DIM).sum(axis=-1).T
    gs["loss_rows"] = loss_rows

    def small_gather(names):
        return (_gather_small([gs[k].reshape(small_shape[k]) for k in names]), gathered, names) if dist else None

    late = ("ln1_g", "ln1_b")
    et = exchange(scat=("mix_w_in",), extra=small_gather(tuple(k for k in SMALL + ("loss_rows",) if k not in late)))
    (dx0, dh1, df1, gs["ln1_g"], gs["ln1_b"]), got = _ffn_bwd(
        dx1, xh1, rstd1, h1, wb["ffn1_w_in"], wb["ffn1_w_out"], row(ws["ln1_g"]), tm, "ffn1_bwd", et[0])
    take(et, got)
    grad_x = dx0.reshape(bsz, seq, D_MODEL)
    if not dist:
        gb["ffn1_w_out"], _ = _tn_matmul(a1, df1, "dw_ffn1_out", 1408, 1024)
        gb["ffn1_w_in"], _ = _tn_matmul(x0b, dh1, "dw_ffn1_in", 1024, 1408, a_t=True)
        return loss_rows, grad_x, gb, {k: gs[k].reshape(small_shape[k]) for k in SMALL}, sums, other, gathered, None
    et = exchange(swap=("mix_w_in",), extra=small_gather(late))
    gb["ffn1_w_out"], got = _tn_matmul(a1, df1, "dw_ffn1_out", 1408, 1024, bg=et[0])
    take(et, got)
    last = ["ffn1_w_out"] + [LAST_PIECE % q for q in range(LAST_PIECES)]
    for i in range(1, len(last) + 3):
        stage = lambda d: tuple(last[i - d:i - d + 1]) if 0 <= i - d < len(last) else ()
        et = exchange(halves=stage(1), scat2=stage(2), swap2=stage(3))
        if i < len(last):
            gb[last[i]], got = _tn_matmul(x0b, dh1, "dw_" + last[i], D_MODEL // LAST_PIECES, 1408, bg=et[0],
                                          a_cols=(i - 1, 1), a_t=True)
        else:
            got = _run_exchange(et[0], "reduce_last_%d" % (i - len(last)))
        take(et, got)
    return loss_rows, grad_x, gb, gs, sums, other, gathered, ids


def _adamw(w, g, m, v):
    m = ADAM_B1 * m + (1.0 - ADAM_B1) * g
    v = ADAM_B2 * v + (1.0 - ADAM_B2) * (g * g)
    m_hat = m / ADAM_C1
    v_hat = v / ADAM_C2
    delta = -ADAM_LR * (m_hat / (jnp.sqrt(v_hat) + ADAM_EPS) + ADAM_WD * w)
    return delta, m, v


def _sum_blocks(part, recv, shape, axis, chip, name):
    r, c = shape
    rb = r // 8

    def body(chip_ref, p_ref, r_ref, o_ref):
        o_ref[...] = (p_ref[...] + r_ref[0].astype(F32) + r_ref[1].astype(F32) + r_ref[2].astype(F32))

    if axis == 0:
        own = pl.BlockSpec((rb, c), lambda i, k: (k[0] * 8 + i, 0))
    else:
        own = pl.BlockSpec((rb, c), lambda i, k: (i, k[0]))
    grid_spec = pltpu.PrefetchScalarGridSpec(
        num_scalar_prefetch=1, grid=(8,),
        in_specs=[own, pl.BlockSpec((3, rb, c), lambda i, k: (0, i, 0))],
        out_specs=pl.BlockSpec((rb, c), lambda i, k: (i, 0)))
    return pl.pallas_call(body, name=name, out_shape=SDS((r, c), F32), grid_spec=grid_spec,
                          compiler_params=_params(("parallel",)))(chip, part, recv)


def _presum(part, half, shape, axis, ids, name):
    r, c = shape
    rb = r // 4

    def body(ids_ref, p_ref, h_ref, of_ref, ob_ref):
        s = p_ref[...] + h_ref[...].astype(F32)
        ob_ref[...] = s.astype(BF16)

        @pl.when(pl.program_id(1) == 0)
        def _():
            of_ref[...] = s

    if axis == 0:
        p_spec = pl.BlockSpec((rb, c), lambda i, t, ids: (ids[t] * 4 + ids[4] * 2 + i, 0))
        h_spec = pl.BlockSpec((None, rb, c), lambda i, t, ids: (ids[t], i, 0))
    else:
        p_spec = pl.BlockSpec((rb, c), lambda i, t, ids: (ids[4] * 2 + i, ids[t]))
        h_spec = pl.BlockSpec((rb, c), lambda i, t, ids: (i, ids[t]))
    grid_spec = pltpu.PrefetchScalarGridSpec(
        num_scalar_prefetch=1, grid=(2, 4), in_specs=[p_spec, h_spec],
        out_specs=(pl.BlockSpec((rb, c), lambda i, t, ids: (i, 0)),
                   pl.BlockSpec((None, rb, c), lambda i, t, ids: (t, i, 0))))
    return pl.pallas_call(body, name=name, out_shape=(SDS((r // 2, c), F32), SDS((4, r // 2, c), BF16)),
                          grid_spec=grid_spec, compiler_params=_params(("parallel", "arbitrary")))(ids, part, half)


def _sum_half(pre, recv, name):
    hr, c = pre.shape
    rb = hr // 2

    def body(p_ref, r_ref, o_ref):
        o_ref[...] = (p_ref[...] + r_ref[0].astype(F32) + r_ref[1].astype(F32) + r_ref[2].astype(F32))

    spec = pl.BlockSpec((rb, c), lambda i: (i, 0))
    return pl.pallas_call(body, name=name, grid=(2,), out_shape=SDS((hr, c), F32),
                          in_specs=[spec, pl.BlockSpec((3, rb, c), lambda i: (0, i, 0))], out_specs=spec,
                          compiler_params=_params(("parallel",)))(pre, recv)


def _adam_halves(w, mine, oth, m, v, ids, name, piece=0, prev=None):
    r, c = w.shape
    rb = mine.shape[0] // 2

    def body(ids_ref, w_ref, a_ref, b_ref, m_ref, v_ref, *rest):
        g_ref, d_ref, nm_ref, nv_ref = rest[-4:]
        g = jnp.where(pl.program_id(0) // 2 == ids_ref[4], a_ref[...], b_ref[...])
        g_ref[...] = g
        d_ref[...], nm_ref[...], nv_ref[...] = _adamw(w_ref[...], g, m_ref[...], v_ref[...])

    whole = pl.BlockSpec((rb, c), lambda i, ids: (i + 4 * piece, 0))
    part = pl.BlockSpec((rb, c), lambda i, ids: (i % 2, 0))
    in_specs = [whole, part, part, whole, whole]
    args = [w, mine, oth, m, v]
    aliases = {}
    if prev is not None:
        in_specs += [pl.BlockSpec(memory_space=pl.ANY)] * 4
        args += list(prev)
        aliases = {6: 0, 7: 1, 8: 2, 9: 3}
    grid_spec = pltpu.PrefetchScalarGridSpec(num_scalar_prefetch=1, grid=(4,), in_specs=in_specs,
                                             out_specs=(whole,) * 4)
    return pl.pallas_call(body, name=name, out_shape=tuple(SDS((r, c), F32) for _ in range(4)),
                          grid_spec=grid_spec, input_output_aliases=aliases,
                          compiler_params=_params(("parallel",)))(ids, *args)


def _adam_big(w, ga, gb, m, v, name, piece=0, prev=None):
    r, c = w.shape
    pr = ga.shape[0]
    steps = 8 if pr == r else 2
    rb = pr // steps
    off = piece * steps

    def body(w_ref, ga_ref, gb_ref, m_ref, v_ref, *rest):
        g_ref, d_ref, nm_ref, nv_ref = rest[-4:]
        g = ga_ref[...] + gb_ref[...]
        g_ref[...] = g
        d_ref[...], nm_ref[...], nv_ref[...] = _adamw(w_ref[...], g, m_ref[...], v_ref[...])

    whole = pl.BlockSpec((rb, c), lambda i: (i + off, 0))
    part = pl.BlockSpec((rb, c), lambda i: (i, 0))
    in_specs = [whole, part, part, whole, whole]
    args = [w, ga, gb, m, v]
    aliases = {}
    if prev is not None:
        in_specs += [pl.BlockSpec(memory_space=pl.ANY)] * 4
        args += list(prev)
        aliases = {5: 0, 6: 1, 7: 2, 8: 3}
    return pl.pallas_call(
        body, name=name, grid=(steps,), out_shape=tuple(SDS((r, c), F32) for _ in range(4)),
        in_specs=in_specs, out_specs=(whole,) * 4, input_output_aliases=aliases,
        compiler_params=_params(("parallel",)),
    )(*args)


def _adam_small(ws, gathered, ms, vs):
    n = len(ws)

    def body(*refs):
        w_refs, g_refs, m_refs, v_refs = refs[:n], refs[n:2 * n], refs[2 * n:3 * n], refs[3 * n:4 * n]
        outs = refs[4 * n:]
        for i in range(n):
            g = g_refs[i][0]
            for d in range(1, N_DEV):
                g = g + g_refs[i][d]
            delta, nm, nv = _adamw(w_refs[i][...], g, m_refs[i][...], v_refs[i][...])
            outs[i][...] = g
            outs[n + i][...] = delta
            outs[2 * n + i][...] = nm
            outs[3 * n + i][...] = nv

    vmem = pl.BlockSpec(memory_space=pltpu.VMEM)
    shapes = [w.shape for w in ws]
    return pl.pallas_call(
        body, name="adam_small", out_shape=tuple(SDS(s, F32) for s in shapes * 4),
        in_specs=[vmem] * (4 * n), out_specs=tuple([vmem] * (4 * n)),
        compiler_params=pltpu.CompilerParams(vmem_limit_bytes=VMEM_LIMIT_BYTES),
    )(*ws, *gathered, *ms, *vs)


def _sum_loss(gathered):
    def body(g_ref, o_ref):
        tot = g_ref[0]
        for d in range(1, N_DEV):
            tot = tot + g_ref[d]
        o_ref[...] = (0.5 / D_MODEL) * jnp.sum(tot, axis=1, keepdims=True)

    vmem = pl.BlockSpec(memory_space=pltpu.VMEM)
    return pl.pallas_call(body, name="sum_loss", out_shape=SDS((1, 1), F32), in_specs=[vmem],
                          out_specs=vmem)(gathered)


def kernel(x, p, ffn1_w_in, ffn1_w_out, ln1_g, ln1_b, mix_w_in, ssm_lambda_re, ssm_lambda_im, ssm_log_dt, ssm_b_re, ssm_b_im, ssm_c_re, ssm_c_im, ssm_d, ssm_glu_w, ssm_glu_b, gmlp_ln_g, gmlp_ln_b, gmlp_w_s, gmlp_b_s, up_a, up_b, mix_w_out, ln2_g, ln2_b, ffn2_w_in, ffn2_w_out, ln3_g, ln3_b, ple_w_proj, ple_w_gate, loss_target, m_ffn1_w_in, m_ffn1_w_out, m_ln1_g, m_ln1_b, m_mix_w_in, m_ssm_lambda_re, m_ssm_lambda_im, m_ssm_log_dt, m_ssm_b_re, m_ssm_b_im, m_ssm_c_re, m_ssm_c_im, m_ssm_d, m_ssm_glu_w, m_ssm_glu_b, m_gmlp_ln_g, m_gmlp_ln_b, m_gmlp_w_s, m_gmlp_b_s, m_up_a, m_up_b, m_mix_w_out, m_ln2_g, m_ln2_b, m_ffn2_w_in, m_ffn2_w_out, m_ln3_g, m_ln3_b, m_ple_w_proj, m_ple_w_gate, v_ffn1_w_in, v_ffn1_w_out, v_ln1_g, v_ln1_b, v_mix_w_in, v_ssm_lambda_re, v_ssm_lambda_im, v_ssm_log_dt, v_ssm_b_re, v_ssm_b_im, v_ssm_c_re, v_ssm_c_im, v_ssm_d, v_ssm_glu_w, v_ssm_glu_b, v_gmlp_ln_g, v_gmlp_ln_b, v_gmlp_w_s, v_gmlp_b_s, v_up_a, v_up_b, v_mix_w_out, v_ln2_g, v_ln2_b, v_ffn2_w_in, v_ffn2_w_out, v_ln3_g, v_ln3_b, v_ple_w_proj, v_ple_w_gate):
    given = dict(locals())
    order = ("ffn1_w_in", "ffn1_w_out", "ln1_g", "ln1_b", "mix_w_in", "ssm_lambda_re", "ssm_lambda_im",
             "ssm_log_dt", "ssm_b_re", "ssm_b_im", "ssm_c_re", "ssm_c_im", "ssm_d", "ssm_glu_w", "ssm_glu_b",
             "gmlp_ln_g", "gmlp_ln_b", "gmlp_w_s", "gmlp_b_s", "up_a", "up_b", "mix_w_out", "ln2_g", "ln2_b",
             "ffn2_w_in", "ffn2_w_out", "ln3_g", "ln3_b", "ple_w_proj", "ple_w_gate")
    assert set(order) == set(BIG + SMALL)

    shard = {k: given[k][0] for k in BIG}
    shard_b = {k: shard[k].astype(BF16) for k in BIG}
    loss_rows, grad_x, gb, gs, sums, other, gathered, ids = _local_step(
        x, given["p"][0], loss_target, {}, {k: given[k] for k in SMALL}, shard_b)

    out = {}
    for k in BIG:
        moments = (given["m_" + k][0], given["v_" + k][0])
        if k == "ffn1_w_out":
            out[k] = _adam_halves(shard[k], sums[k], other[k], *moments, ids, "adam_" + k)
        elif k == "ffn1_w_in":
            for q in range(LAST_PIECES):
                kq = LAST_PIECE % q
                out[k] = _adam_halves(shard[k], sums[kq], other[kq], *moments, ids, "adam_" + kq, q, out.get(k))
        else:
            out[k] = _adam_big(shard[k], sums[k], other[k], *moments, "adam_" + k)

    res = _adam_small([_small_view(k, given[k]) for k in SMALL], [gathered[k] for k in SMALL],
                      [_small_view(k, given["m_" + k]) for k in SMALL],
                      [_small_view(k, given["v_" + k]) for k in SMALL])
    ns = len(SMALL)
    for i, k in enumerate(SMALL):
        out[k] = tuple(res[j * ns + i].reshape(given[k].shape) for j in range(4))
    loss = _sum_loss(gathered["loss_rows"]).reshape(())

    lead = lambda k, j: out[k][j][None] if k in BIG else out[k][j]
    return (loss, grad_x, *[lead(k, 0) for k in order], *[lead(k, 1) for k in order],
            *[lead(k, 2) for k in order], *[lead(k, 3) for k in order])
```

```python
import math

import jax
import jax.numpy as jnp
from jax import lax
from jax.experimental import pallas as pl
from jax.experimental.pallas import tpu as pltpu
from jax.experimental.pallas import tpu_sc as plsc

F32 = jnp.float32
BF16 = jnp.bfloat16
MESH = pl.DeviceIdType.MESH
SDS = jax.ShapeDtypeStruct

D_MODEL = 1024
D_FF = 2816
D_SSM = 512
D_GMLP = 512
SSM_GROUPS = 32
SSM_GROUP_CH = 16
SSM_STATE = 64
SSM_LANES = SSM_GROUPS * SSM_STATE
GMLP_HEADS = 8
GMLP_HEAD_DIM = 64
CHUNK = 128
PLE_DIM = 256
LN_EPS = 1e-5
ALPHA = 2.0 ** 0.25

ADAM_LR = 0.001
ADAM_B1 = 0.9
ADAM_B2 = 0.999
ADAM_EPS = 1e-08
ADAM_WD = 0.01
ADAM_STEP = 10
ADAM_C1 = 1.0 - ADAM_B1 ** ADAM_STEP
ADAM_C2 = 1.0 - ADAM_B2 ** ADAM_STEP

N_DEV = 8
VMEM_LIMIT_BYTES = 56 * 1024 * 1024
FFN_COLS = 1408
S5_BLOCKS = 4
S5_BLOCK_IN = D_SSM // S5_BLOCKS
S5_BLOCK_ST = SSM_LANES // S5_BLOCKS
SCAN_LANES = 512
TN_K_BLOCK = 2048
DIRECT_GATHER_BYTES = 0
LAST_PIECES = 2
LAST_PIECE = "ffn1_w_in_q%d"
_G0 = math.sqrt(2.0 / math.pi)
_G1 = 0.044715


def _dot(a, b):
    return jnp.dot(a, b, preferred_element_type=F32)


def _dot_nt(a, b):
    return lax.dot_general(a, b, (((1,), (1,)), ((), ())), preferred_element_type=F32)


def _dot_tn(a, b):
    return lax.dot_general(a, b, (((0,), (0,)), ((), ())), preferred_element_type=F32)


def _sigmoid(x):
    return 1.0 / (1.0 + jnp.exp(-x))


def _gelu(x):
    t = jnp.tanh(_G0 * (x + _G1 * x * x * x))
    return 0.5 * x * (1.0 + t)


def _gelu_grad(x):
    t = jnp.tanh(_G0 * (x + _G1 * x * x * x))
    return 0.5 * (1.0 + t) + 0.5 * x * (1.0 - t * t) * _G0 * (1.0 + 3.0 * _G1 * x * x)


def _ln_fwd(r, g, b):
    mu = jnp.mean(r, axis=-1, keepdims=True)
    d = r - mu
    var = jnp.mean(d * d, axis=-1, keepdims=True)
    rstd = lax.rsqrt(var + LN_EPS)
    xh = d * rstd
    return xh * g + b, xh, rstd


def _ln_bwd(dy, xh, rstd, g):
    dxh = dy * g
    m1 = jnp.mean(dxh, axis=-1, keepdims=True)
    m2 = jnp.mean(dxh * xh, axis=-1, keepdims=True)
    return rstd * (dxh - m1 - xh * m2)


def _resident(shape):
    nd = len(shape)
    return pl.BlockSpec(shape, lambda *_: (0,) * nd, pipeline_mode=pl.Buffered(1))


def _fixed(shape):
    nd = len(shape)
    return pl.BlockSpec(shape, lambda *_: (0,) * nd)


def _rows(tm, cols):
    return pl.BlockSpec((tm, cols), lambda i: (i, 0))


def _cols(rows, tm):
    return pl.BlockSpec((rows, tm), lambda i: (0, i))


def _params(sem):
    return pltpu.CompilerParams(dimension_semantics=sem, vmem_limit_bytes=VMEM_LIMIT_BYTES)


class _Exchange:
    def __init__(self, args, out_shape, sems, start, finish):
        self.args, self.out_shape, self.sems = list(args), list(out_shape), list(sems)
        self.start, self.finish = start, finish
        self.cuts = [(0, len(self.out_shape))]


def _call(body, name, grid, in_specs, out_specs, out_shape, args, scratch=(), sem=None, bg=None, aliases=None):
    aliases = {} if aliases is None else aliases
    if bg is None:
        res = pl.pallas_call(body, name=name, grid=grid, out_shape=tuple(out_shape), in_specs=list(in_specs),
                             out_specs=tuple(out_specs), scratch_shapes=list(scratch),
                             input_output_aliases=aliases, compiler_params=_params(sem))(*args)
        return tuple(res), ()
    n_in, n_out, n_bi, n_bo, n_sc = len(args), len(out_shape), len(bg.args), len(bg.out_shape), len(scratch)

    def wrapped(*refs):
        ins = refs[:n_in]
        b_ins = refs[n_in:n_in + n_bi]
        outs = refs[n_in + n_bi:n_in + n_bi + n_out]
        b_outs = refs[n_in + n_bi + n_out:n_in + n_bi + n_out + n_bo]
        rest = refs[n_in + n_bi + n_out + n_bo:]
        scr, b_sems = rest[:n_sc], rest[n_sc:]
        first = pl.program_id(0) == 0
        last = pl.program_id(0) == grid[0] - 1
        for ax in range(1, len(grid)):
            first = jnp.logical_and(first, pl.program_id(ax) == 0)
            last = jnp.logical_and(last, pl.program_id(ax) == grid[ax] - 1)

        @pl.when(first)
        def _():
            bg.start(b_ins, b_outs, b_sems)

        body(*ins, *outs, *scr)

        @pl.when(last)
        def _():
            bg.finish(b_ins, b_outs, b_sems)

    any_spec = pl.BlockSpec(memory_space=pl.ANY)
    res = pl.pallas_call(
        wrapped, name=name, grid=grid, out_shape=tuple(out_shape) + tuple(bg.out_shape),
        in_specs=list(in_specs) + [any_spec] * n_bi, out_specs=tuple(out_specs) + (any_spec,) * n_bo,
        scratch_shapes=list(scratch) + list(bg.sems), input_output_aliases=aliases,
        compiler_params=_params(tuple("arbitrary" for _ in grid)))(*args, *bg.args)
    return tuple(res[:n_out]), tuple(res[n_out:])


def _run_exchange(ex, name):
    n_i, n_o = len(ex.args), len(ex.out_shape)

    def body(*refs):
        ins, outs, sems = refs[:n_i], refs[n_i:n_i + n_o], refs[n_i + n_o:]
        ex.start(ins, outs, sems)
        ex.finish(ins, outs, sems)

    any_spec = pl.BlockSpec(memory_space=pl.ANY)
    return tuple(pl.pallas_call(body, name=name, out_shape=tuple(ex.out_shape), in_specs=[any_spec] * n_i,
                                out_specs=(any_spec,) * n_o, scratch_shapes=list(ex.sems))(*ex.args))


def _run_exchange_on_sequencer(ex, name, collective_id):
    n_i, n_o = len(ex.args), len(ex.out_shape)

    def body(*refs):
        ins, outs, sems = refs[:n_i], refs[n_i:n_i + n_o], refs[n_i + n_o:]
        x, y, c = lax.axis_index("x"), lax.axis_index("y"), lax.axis_index("c")
        barrier = pltpu.get_barrier_semaphore()
        for peer in [(x, y, 1 - c), (1 - x, y, c), (x, 1 - y, c), (1 - x, 1 - y, c)]:
            pl.semaphore_signal(barrier, inc=1, device_id=peer, device_id_type=MESH)
        pl.semaphore_wait(barrier, 4)
        ex.start(ins, outs, sems)
        ex.finish(ins, outs, sems)

    return tuple(pl.kernel(body, out_type=tuple(ex.out_shape),
                           mesh=plsc.ScalarSubcoreMesh(axis_name="sequencer", num_cores=1),
                           scratch_types=list(ex.sems), name=name,
                           compiler_params=pltpu.CompilerParams(collective_id=collective_id))(*ex.args))


def _join(exchanges):
    cuts = []
    a = o = q = 0
    for e in exchanges:
        cuts.append((a, a + len(e.args), o, o + len(e.out_shape), q, q + len(e.sems)))
        a, o, q = cuts[-1][1], cuts[-1][3], cuts[-1][5]

    def start(ins, outs, sems):
        for e, (a0, a1, o0, o1, q0, q1) in zip(exchanges, cuts):
            e.start(ins[a0:a1], outs[o0:o1], sems[q0:q1])

    def finish(ins, outs, sems):
        for e, (a0, a1, o0, o1, q0, q1) in zip(exchanges, cuts):
            e.finish(ins[a0:a1], outs[o0:o1], sems[q0:q1])

    joined = _Exchange(sum((e.args for e in exchanges), []), sum((e.out_shape for e in exchanges), []),
                       sum((e.sems for e in exchanges), []), start, finish)
    joined.cuts = [(c[2], c[3]) for c in cuts]
    return joined


def _ffn_proj(x, w_in, tm, name, bg=None):
    t = x.shape[0]
    nch = D_FF // FFN_COLS

    def body(x_ref, win_ref, xbt_ref, h_ref, a_ref):
        xb = x_ref[...].astype(BF16)
        xbt_ref[...] = xb.T
        for k in range(nch):
            cg = slice(k * FFN_COLS, (k + 1) * FFN_COLS)
            cu = slice(D_FF + k * FFN_COLS, D_FF + (k + 1) * FFN_COLS)
            hg = _dot(xb, win_ref[k])
            hu = _dot(xb, win_ref[nch + k])
            h_ref[:, cg] = hg.astype(BF16)
            h_ref[:, cu] = hu.astype(BF16)
            a_ref[:, cg] = (hg * _sigmoid(hg) * hu).astype(BF16)

    return _call(
        body, name, (t // tm,),
        [_rows(tm, D_MODEL), _resident((2 * nch, D_MODEL, FFN_COLS))],
        (_cols(D_MODEL, tm), _rows(tm, 2 * D_FF), _rows(tm, D_FF)),
        (SDS((D_MODEL, t), BF16), SDS((t, 2 * D_FF), BF16), SDS((t, D_FF), BF16)),
        (x, w_in), sem=("parallel",), bg=bg)


def _ffn_out(x, a, w_out, g, b, tm, name, bg=None):
    t = x.shape[0]

    def body(x_ref, a_ref, wout_ref, g_ref, b_ref, xn_ref, xh_ref, rstd_ref):
        f = _dot(a_ref[...], wout_ref[...])
        y, xh, rstd = _ln_fwd(ALPHA * x_ref[...] + 0.5 * f, g_ref[...], b_ref[...])
        xn_ref[...] = y
        xh_ref[...] = xh
        rstd_ref[...] = rstd

    return _call(
        body, name, (t // tm,),
        [_rows(tm, D_MODEL), _rows(tm, D_FF), _resident((D_FF, D_MODEL)), _fixed((1, D_MODEL)), _fixed((1, D_MODEL))],
        (_rows(tm, D_MODEL), _rows(tm, D_MODEL), _rows(tm, 1)),
        (SDS((t, D_MODEL), F32), SDS((t, D_MODEL), F32), SDS((t, 1), F32)),
        (x, a, w_out, g, b), sem=("parallel",), bg=bg)


def _ffn_bwd(dxn, xh, rstd, h, w_in, w_out, g, tm, name, bg=None):
    t = dxn.shape[0]
    nch = D_FF // FFN_COLS

    def body(dxn_ref, xh_ref, rstd_ref, h_ref, win_ref, wout_ref, g_ref,
             dx_ref, dh_ref, df_ref, dg_ref, db_ref):
        @pl.when(pl.program_id(0) == 0)
        def _():
            dg_ref[...] = jnp.zeros_like(dg_ref)
            db_ref[...] = jnp.zeros_like(db_ref)

        dy = dxn_ref[...]
        xhv = xh_ref[...]
        dr = _ln_bwd(dy, xhv, rstd_ref[...], g_ref[...])
        dg_ref[...] += jnp.sum(dy * xhv, axis=0, keepdims=True)
        db_ref[...] += jnp.sum(dy, axis=0, keepdims=True)
        df = (0.5 * dr).astype(BF16)
        df_ref[...] = df
        dx = ALPHA * dr
        for k in range(nch):
            cg = slice(k * FFN_COLS, (k + 1) * FFN_COLS)
            cu = slice(D_FF + k * FFN_COLS, D_FF + (k + 1) * FFN_COLS)
            hg = h_ref[:, cg].astype(F32)
            hu = h_ref[:, cu].astype(F32)
            sg = _sigmoid(hg)
            silu = hg * sg
            da = _dot_nt(df, wout_ref[cg, :])
            dhu = (da * silu).astype(BF16)
            dhg = (da * hu * (sg * (1.0 + hg * (1.0 - sg)))).astype(BF16)
            dh_ref[:, cg] = dhg
            dh_ref[:, cu] = dhu
            dx = dx + _dot_nt(dhg, win_ref[k]) + _dot_nt(dhu, win_ref[nch + k])
        dx_ref[...] = dx

    return _call(
        body, name, (t // tm,),
        [_rows(tm, D_MODEL), _rows(tm, D_MODEL), _rows(tm, 1), _rows(tm, 2 * D_FF),
         _resident((2 * nch, D_MODEL, FFN_COLS)), _resident((D_FF, D_MODEL)), _fixed((1, D_MODEL))],
        (_rows(tm, D_MODEL), _rows(tm, 2 * D_FF), _rows(tm, D_MODEL),
         _fixed((1, D_MODEL)), _fixed((1, D_MODEL))),
        (SDS((t, D_MODEL), F32), SDS((t, 2 * D_FF), BF16), SDS((t, D_MODEL), BF16),
         SDS((1, D_MODEL), F32), SDS((1, D_MODEL), F32)),
        (dxn, xh, rstd, h, w_in, w_out, g), sem=("arbitrary",), bg=bg)


def _tn_matmul(a, b, name, bm, bn, col_block=0, total_cols=None, prev=None, bg=None, a_cols=None, a_t=False):
    t, m = a.shape[::-1] if a_t else a.shape
    a_first = 0
    if a_cols is not None:
        a_first, m = a_cols[0], a_cols[1] * bm
    n = b.shape[1]
    total_cols = n if total_cols is None else total_cols
    bk = min(TN_K_BLOCK, t)
    nk = t // bk
    n_in = 2 if prev is None else 4

    def body(*refs):
        a_ref, b_ref = refs[0], refs[1]
        o_ref, ob_ref = refs[n_in], refs[n_in + 1]
        k = pl.program_id(2)

        @pl.when(k == 0)
        def _():
            o_ref[...] = jnp.zeros_like(o_ref)

        o_ref[...] += _dot(a_ref[...], b_ref[...]) if a_t else _dot_tn(a_ref[...], b_ref[...])

        @pl.when(k == nk - 1)
        def _():
            ob_ref[...] = o_ref[...].astype(BF16)

    a_spec = (pl.BlockSpec((bm, bk), lambda i, j, k: (i + a_first, k)) if a_t
              else pl.BlockSpec((bk, bm), lambda i, j, k: (k, i + a_first)))
    in_specs = [a_spec, pl.BlockSpec((bk, bn), lambda i, j, k: (k, j))]
    args = [a, b]
    aliases = {}
    if prev is not None:
        in_specs += [pl.BlockSpec(memory_space=pl.ANY), pl.BlockSpec(memory_space=pl.ANY)]
        args += list(prev)
        aliases = {2: 0, 3: 1}
    out_spec = pl.BlockSpec((bm, bn), lambda i, j, k: (i, j + col_block))
    return _call(body, name, (m // bm, n // bn, nk), in_specs, (out_spec, out_spec),
                 (SDS((m, total_cols), F32), SDS((m, total_cols), BF16)), args,
                 sem=("parallel", "parallel", "arbitrary"), bg=bg, aliases=aliases)


def _mixin_fwd(x1, w, tm, bg=None):
    t = x1.shape[0]

    def body(x_ref, w_ref, xbt_ref, za_ref, zuv_ref, gab_ref):
        xb = x_ref[...].astype(BF16)
        xbt_ref[...] = xb.T
        za_ref[...] = _dot(xb, w_ref[:, 0:512]).astype(BF16)
        zuv_ref[...] = _dot(xb, w_ref[:, 512:1536]).astype(BF16)
        gab_ref[...] = _dot(xb, w_ref[:, 1536:3584]).astype(BF16)

    return _call(
        body, "mixin_fwd", (t // tm,),
        [_rows(tm, D_MODEL), _resident((D_MODEL, 3584))],
        (_cols(D_MODEL, tm), _rows(tm, 512), _rows(tm, 1024), _rows(tm, 2048)),
        (SDS((D_MODEL, t), BF16), SDS((t, 512), BF16), SDS((t, 1024), BF16), SDS((t, 2048), BF16)),
        (x1, w), sem=("parallel",), bg=bg)


def _mixin_bwd(dx1a, dza, dzuv, dgab, w, tm, bg=None):
    t = dx1a.shape[0]

    def body(d_ref, dza_ref, dzuv_ref, dgab_ref, w_ref, dx_ref):
        dx_ref[...] = (d_ref[...] + _dot_nt(dza_ref[...], w_ref[:, 0:512])
                       + _dot_nt(dzuv_ref[...], w_ref[:, 512:1536])
                       + _dot_nt(dgab_ref[...], w_ref[:, 1536:3584]))

    return _call(
        body, "mixin_bwd", (t // tm,),
        [_rows(tm, D_MODEL), _rows(tm, 512), _rows(tm, 1024), _rows(tm, 2048), _resident((D_MODEL, 3584))],
        (_rows(tm, D_MODEL),), (SDS((t, D_MODEL), F32),),
        (dx1a, dza, dzuv, dgab, w), sem=("parallel",), bg=bg)


def _unrolled(lo, hi, body, carry):
    for j in range(lo, hi):
        carry = body(j, carry)
    return carry


def _scan_fwd(hr_ref, hi_ref, a_ref, ap_ref, carry_ref, seg, cin_ref):
    for lc in range(SSM_LANES // SCAN_LANES):
        ls = slice(lc * SCAN_LANES, (lc + 1) * SCAN_LANES)
        a_r = jnp.broadcast_to(a_ref[0:1, ls], (8, SCAN_LANES))
        a_i = jnp.broadcast_to(a_ref[1:2, ls], (8, SCAN_LANES))

        def step(j, hc, ls=ls, a_r=a_r, a_i=a_i):
            h_r, h_i = hc
            rows = pl.ds(j * 8, 8)
            n_r = a_r * h_r - a_i * h_i + hr_ref[rows, ls]
            n_i = a_r * h_i + a_i * h_r + hi_ref[rows, ls]
            hr_ref[rows, ls] = n_r
            hi_ref[rows, ls] = n_i
            return n_r, n_i

        zero = jnp.zeros((8, SCAN_LANES), F32)
        f_r, f_i = _unrolled(0, seg, step, (zero, zero))
        c_r = carry_ref[0:1, ls]
        c_i = carry_ref[1:2, ls]
        p_r = ap_ref[0:1, ls]
        p_i = ap_ref[1:2, ls]
        rows_r, rows_i = [], []
        for s in range(8):
            rows_r.append(c_r)
            rows_i.append(c_i)
            c_r, c_i = (f_r[s:s + 1] + p_r * c_r - p_i * c_i,
                        f_i[s:s + 1] + p_r * c_i + p_i * c_r)
        carry_ref[0:1, ls] = c_r
        carry_ref[1:2, ls] = c_i
        cin_r = jnp.concatenate(rows_r, axis=0)
        cin_i = jnp.concatenate(rows_i, axis=0)
        if cin_ref is not None:
            cin_ref[0, :, ls] = cin_r
            cin_ref[1, :, ls] = cin_i

        def fix(j, cc, ls=ls, a_r=a_r, a_i=a_i):
            c_r, c_i = cc
            c_r, c_i = a_r * c_r - a_i * c_i, a_r * c_i + a_i * c_r
            rows = pl.ds(j * 8, 8)
            hr_ref[rows, ls] = hr_ref[rows, ls] + c_r
            hi_ref[rows, ls] = hi_ref[rows, ls] + c_i
            return c_r, c_i

        _unrolled(0, seg, fix, (cin_r, cin_i))


def _scan_bwd(gr_ref, gi_ref, hr_ref, hi_ref, cin_ref, a_ref, ap_ref, rcarry_ref, da_ref, seg):
    for lc in range(SSM_LANES // SCAN_LANES):
        ls = slice(lc * SCAN_LANES, (lc + 1) * SCAN_LANES)
        a_r = jnp.broadcast_to(a_ref[0:1, ls], (8, SCAN_LANES))
        a_i = jnp.broadcast_to(a_ref[1:2, ls], (8, SCAN_LANES))

        def step(t, gc, ls=ls, a_r=a_r, a_i=a_i):
            g_r, g_i = gc
            rows = pl.ds((seg - 1 - t) * 8, 8)
            n_r = gr_ref[rows, ls] + a_r * g_r + a_i * g_i
            n_i = gi_ref[rows, ls] + a_r * g_i - a_i * g_r
            gr_ref[rows, ls] = n_r
            gi_ref[rows, ls] = n_i
            return n_r, n_i

        zero = jnp.zeros((8, SCAN_LANES), F32)
        f_r, f_i = _unrolled(0, seg, step, (zero, zero))
        c_r = rcarry_ref[0:1, ls]
        c_i = rcarry_ref[1:2, ls]
        p_r = ap_ref[0:1, ls]
        p_i = ap_ref[1:2, ls]
        rows_r, rows_i = [None] * 8, [None] * 8
        for s in range(7, -1, -1):
            rows_r[s] = c_r
            rows_i[s] = c_i
            c_r, c_i = (f_r[s:s + 1] + p_r * c_r + p_i * c_i,
                        f_i[s:s + 1] + p_r * c_i - p_i * c_r)
        rcarry_ref[0:1, ls] = c_r
        rcarry_ref[1:2, ls] = c_i
        cin_r = jnp.concatenate(rows_r, axis=0)
        cin_i = jnp.concatenate(rows_i, axis=0)

        def fix_row(j_rows, hp_r, hp_i, cc, ls=ls, a_r=a_r, a_i=a_i):
            c_r, c_i, acc_r, acc_i = cc
            c_r, c_i = a_r * c_r + a_i * c_i, a_r * c_i - a_i * c_r
            g_r = gr_ref[j_rows, ls] + c_r
            g_i = gi_ref[j_rows, ls] + c_i
            gr_ref[j_rows, ls] = g_r
            gi_ref[j_rows, ls] = g_i
            acc_r = acc_r + g_r * hp_r + g_i * hp_i
            acc_i = acc_i + g_i * hp_r - g_r * hp_i
            return c_r, c_i, acc_r, acc_i

        def fix(t, cc, ls=ls, fix_row=fix_row):
            j = seg - 1 - t
            rows = pl.ds(j * 8, 8)
            prev = pl.ds((j - 1) * 8, 8)
            return fix_row(rows, hr_ref[prev, ls], hi_ref[prev, ls], cc)

        cc = _unrolled(0, seg - 1, fix, (cin_r, cin_i, zero, zero))
        _, _, acc_r, acc_i = fix_row(pl.ds(0, 8), cin_ref[0, :, ls], cin_ref[1, :, ls], cc)
        da_ref[0, :, ls] += acc_r
        da_ref[1, :, ls] += acc_i


def _s5_fwd(za, sp, bsz, seq, tb, bg=None):
    nb = seq // tb
    seg = tb // 8
    t = bsz * seq

    def body(za_ref, perm_ref, permt_ref, mre_ref, mim_ref, nre_ref, nim_ref, a_ref, ap_ref,
             dsk_ref, gw_ref, gb_ref, out_ref, outt_ref, y2_ref, car_ref, hr_ref, hi_ref, carry_ref):
        @pl.when(pl.program_id(1) == 0)
        def _():
            carry_ref[...] = jnp.zeros_like(carry_ref)

        car_ref[0] = carry_ref[...]
        up = _dot(perm_ref[...], za_ref[...])
        upb = up.astype(BF16)
        for bb in range(S5_BLOCKS):
            ub = upb[:, bb * S5_BLOCK_IN:(bb + 1) * S5_BLOCK_IN]
            st = slice(bb * S5_BLOCK_ST, (bb + 1) * S5_BLOCK_ST)
            hr_ref[:, st] = _dot(ub, mre_ref[bb])
            hi_ref[:, st] = _dot(ub, mim_ref[bb])
        _scan_fwd(hr_ref, hi_ref, a_ref, ap_ref, carry_ref, seg, None)
        ys = []
        for bb in range(S5_BLOCKS):
            st = slice(bb * S5_BLOCK_ST, (bb + 1) * S5_BLOCK_ST)
            ys.append(_dot(hr_ref[:, st].astype(BF16), nre_ref[bb])
                      - _dot(hi_ref[:, st].astype(BF16), nim_ref[bb]))
        y2 = jnp.concatenate(ys, axis=1) + dsk_ref[...] * up
        y2_ref[...] = y2
        y3 = _gelu(y2)
        gl = _dot(y3.astype(BF16), gw_ref[...]) + gb_ref[...]
        oa = y3 * _sigmoid(gl)
        out = _dot(permt_ref[...], oa.astype(BF16)).astype(BF16)
        out_ref[...] = out
        outt_ref[...] = out.T

    blk = pl.BlockSpec((tb, D_SSM), lambda b, j: (b * nb + j, 0))
    blk_t = pl.BlockSpec((D_SSM, tb), lambda b, j: (0, b * nb + j))
    m_shape = (S5_BLOCKS, S5_BLOCK_IN, S5_BLOCK_ST)
    n_shape = (S5_BLOCKS, S5_BLOCK_ST, S5_BLOCK_IN)
    return _call(
        body, "s5_fwd", (bsz, nb),
        [blk, _fixed((tb, tb)), _fixed((tb, tb)), _fixed(m_shape), _fixed(m_shape), _fixed(n_shape),
         _fixed(n_shape), _fixed((2, SSM_LANES)), _fixed((2, SSM_LANES)), _fixed((1, D_SSM)),
         _fixed((D_SSM, D_SSM)), _fixed((1, D_SSM))],
        (blk, blk_t, blk, pl.BlockSpec((1, 2, SSM_LANES), lambda b, j: (b * nb + j, 0, 0))),
        (SDS((t, D_SSM), BF16), SDS((D_SSM, t), BF16), SDS((t, D_SSM), F32), SDS((bsz * nb, 2, SSM_LANES), F32)),
        (za, sp["perm"], sp["permt"], sp["mre"], sp["mim"], sp["nre"], sp["nim"], sp["a"], sp["ap"],
         sp["dskip"], sp["glu_w"], sp["glu_b"]),
        scratch=[pltpu.VMEM((tb, SSM_LANES), F32), pltpu.VMEM((tb, SSM_LANES), F32),
                 pltpu.VMEM((2, SSM_LANES), F32)],
        sem=("arbitrary", "arbitrary"), bg=bg)


def _s5_bwd(za, y2p, doa, carries, sp, bsz, seq, tb, bg=None):
    nb = seq // tb
    seg = tb // 8
    t = bsz * seq

    def body(za_ref, y2_ref, doa_ref, car_ref, perm_ref, permt_ref, mre_ref, mim_ref, mtre_ref, mtim_ref,
             nre_ref, nim_ref, ntre_ref, ntim_ref, a_ref, ap_ref, dsk_ref, gw_ref, gwt_ref, gb_ref,
             dza_ref, dmr_ref, dmi_ref, dnr_ref, dni_ref, da_ref, ddsk_ref, dgw_ref, dgb_ref,
             hr_ref, hi_ref, gr_ref, gi_ref, cin_ref, carry_ref, rcarry_ref):
        first = jnp.logical_and(pl.program_id(0) == 0, pl.program_id(1) == 0)

        @pl.when(first)
        def _():
            for r in (dmr_ref, dmi_ref, dnr_ref, dni_ref, da_ref, ddsk_ref, dgw_ref, dgb_ref):
                r[...] = jnp.zeros_like(r)

        @pl.when(pl.program_id(1) == 0)
        def _():
            rcarry_ref[...] = jnp.zeros_like(rcarry_ref)

        carry_ref[...] = car_ref[0]
        perm = perm_ref[...]
        up = _dot(perm, za_ref[...])
        upb = up.astype(BF16)
        for bb in range(S5_BLOCKS):
            ub = upb[:, bb * S5_BLOCK_IN:(bb + 1) * S5_BLOCK_IN]
            st = slice(bb * S5_BLOCK_ST, (bb + 1) * S5_BLOCK_ST)
            hr_ref[:, st] = _dot(ub, mre_ref[bb])
            hi_ref[:, st] = _dot(ub, mim_ref[bb])
        _scan_fwd(hr_ref, hi_ref, a_ref, ap_ref, carry_ref, seg, cin_ref)

        y2 = y2_ref[...]
        y3 = _gelu(y2)
        y3b = y3.astype(BF16)
        sg = _sigmoid(_dot(y3b, gw_ref[...]) + gb_ref[...])
        d0 = doa_ref[...]
        d_hi = d0.astype(BF16)
        d1 = d0 - d_hi.astype(F32)
        d_mid = d1.astype(BF16)
        d_lo = (d1 - d_mid.astype(F32)).astype(BF16)
        doap = _dot(perm, d_hi) + _dot(perm, d_mid) + _dot(perm, d_lo)
        dgl = doap * y3 * sg * (1.0 - sg)
        dglb = dgl.astype(BF16)
        dy3 = doap * sg + _dot(dglb, gwt_ref[...])
        dgw_ref[...] += _dot_tn(y3b, dglb)
        dgb_ref[...] += jnp.sum(dgl, axis=0, keepdims=True)
        dy2 = dy3 * _gelu_grad(y2)
        ddsk_ref[...] += jnp.sum(dy2 * up, axis=0, keepdims=True)
        dyb = dy2.astype(BF16)
        for bb in range(S5_BLOCKS):
            dyc = dyb[:, bb * S5_BLOCK_IN:(bb + 1) * S5_BLOCK_IN]
            st = slice(bb * S5_BLOCK_ST, (bb + 1) * S5_BLOCK_ST)
            gr_ref[:, st] = _dot(dyc, ntre_ref[bb])
            gi_ref[:, st] = -_dot(dyc, ntim_ref[bb])
            dnr_ref[bb] += _dot_tn(hr_ref[:, st].astype(BF16), dyc)
            dni_ref[bb] += -_dot_tn(hi_ref[:, st].astype(BF16), dyc)
        _scan_bwd(gr_ref, gi_ref, hr_ref, hi_ref, cin_ref, a_ref, ap_ref, rcarry_ref, da_ref, seg)
        dus = []
        for bb in range(S5_BLOCKS):
            st = slice(bb * S5_BLOCK_ST, (bb + 1) * S5_BLOCK_ST)
            grb = gr_ref[:, st].astype(BF16)
            gib = gi_ref[:, st].astype(BF16)
            dus.append(_dot(grb, mtre_ref[bb]) + _dot(gib, mtim_ref[bb]))
            ub = upb[:, bb * S5_BLOCK_IN:(bb + 1) * S5_BLOCK_IN]
            dmr_ref[bb] += _dot_tn(ub, grb)
            dmi_ref[bb] += _dot_tn(ub, gib)
        du = jnp.concatenate(dus, axis=1) + dy2 * dsk_ref[...]
        dza_ref[...] = _dot(permt_ref[...], du.astype(BF16)).astype(BF16)

    def rev(b, j):
        return (b * nb + (nb - 1 - j), 0)

    blk = pl.BlockSpec((tb, D_SSM), rev)
    m_shape = (S5_BLOCKS, S5_BLOCK_IN, S5_BLOCK_ST)
    n_shape = (S5_BLOCKS, S5_BLOCK_ST, S5_BLOCK_IN)
    return _call(
        body, "s5_bwd", (bsz, nb),
        [blk, blk, blk, pl.BlockSpec((1, 2, SSM_LANES), lambda b, j: (b * nb + (nb - 1 - j), 0, 0)),
         _fixed((tb, tb)), _fixed((tb, tb)), _fixed(m_shape), _fixed(m_shape), _fixed(n_shape), _fixed(n_shape),
         _fixed(n_shape), _fixed(n_shape), _fixed(m_shape), _fixed(m_shape),
         _fixed((2, SSM_LANES)), _fixed((2, SSM_LANES)), _fixed((1, D_SSM)),
         _fixed((D_SSM, D_SSM)), _fixed((D_SSM, D_SSM)), _fixed((1, D_SSM))],
        (blk, _fixed(m_shape), _fixed(m_shape), _fixed(n_shape), _fixed(n_shape),
         _fixed((2, 8, SSM_LANES)), _fixed((1, D_SSM)), _fixed((D_SSM, D_SSM)), _fixed((1, D_SSM))),
        (SDS((t, D_SSM), BF16), SDS(m_shape, F32), SDS(m_shape, F32), SDS(n_shape, F32), SDS(n_shape, F32),
         SDS((2, 8, SSM_LANES), F32), SDS((1, D_SSM), F32), SDS((D_SSM, D_SSM), F32), SDS((1, D_SSM), F32)),
        (za, y2p, doa, carries, sp["perm"], sp["permt"], sp["mre"], sp["mim"], sp["mtre"], sp["mtim"],
         sp["nre"], sp["nim"], sp["ntre"], sp["ntim"], sp["a"], sp["ap"], sp["dskip"], sp["glu_w"],
         sp["glu_wt"], sp["glu_b"]),
        scratch=[pltpu.VMEM((tb, SSM_LANES), F32), pltpu.VMEM((tb, SSM_LANES), F32),
                 pltpu.VMEM((tb, SSM_LANES), F32), pltpu.VMEM((tb, SSM_LANES), F32),
                 pltpu.VMEM((2, 8, SSM_LANES), F32), pltpu.VMEM((2, SSM_LANES), F32),
                 pltpu.VMEM((2, SSM_LANES), F32)],
        sem=("arbitrary", "arbitrary"), bg=bg)


def _gmlp_spatial(ws_ref, vb):
    lane = lax.broadcasted_iota(jnp.int32, (CHUNK, 128), 1)
    parts = []
    for j in range(GMLP_HEADS // 2):
        vp = vb[:, 128 * j:128 * (j + 1)]
        parts.append(jnp.where(lane < GMLP_HEAD_DIM, _dot(ws_ref[2 * j], vp), _dot(ws_ref[2 * j + 1], vp)))
    return jnp.concatenate(parts, axis=1)


def _gmlp_fwd(zuv, ln_g, ln_b, wsm, bias, bg=None):
    t = zuv.shape[0]

    def body(z_ref, g_ref, b_ref, ws_ref, bias_ref, out_ref, outt_ref):
        u = _gelu(z_ref[:, 0:D_GMLP].astype(F32))
        v0 = _gelu(z_ref[:, D_GMLP:2 * D_GMLP].astype(F32))
        v, _, _ = _ln_fwd(v0, g_ref[...], b_ref[...])
        s = _gmlp_spatial(ws_ref, v.astype(BF16)) + bias_ref[...]
        out = (u * s).astype(BF16)
        out_ref[...] = out
        outt_ref[...] = out.T

    return _call(
        body, "gmlp_fwd", (t // CHUNK,),
        [_rows(CHUNK, 2 * D_GMLP), _fixed((1, D_GMLP)), _fixed((1, D_GMLP)),
         _fixed((GMLP_HEADS, CHUNK, CHUNK)), _fixed((CHUNK, D_GMLP))],
        (_rows(CHUNK, D_GMLP), _cols(D_GMLP, CHUNK)), (SDS((t, D_GMLP), BF16), SDS((D_GMLP, t), BF16)),
        (zuv, ln_g, ln_b, wsm, bias), sem=("parallel",), bg=bg)


def _gmlp_bwd(zuv, dgm, ln_g, ln_b, wsm, wsmt, bias, bg=None):
    t = zuv.shape[0]

    def body(z_ref, d_ref, g_ref, b_ref, ws_ref, wst_ref, bias_ref,
             dz_ref, dws_ref, dbias_ref, dg_ref, db_ref):
        @pl.when(pl.program_id(0) == 0)
        def _():
            for r in (dws_ref, dbias_ref, dg_ref, db_ref):
                r[...] = jnp.zeros_like(r)

        zu = z_ref[:, 0:D_GMLP].astype(F32)
        zv = z_ref[:, D_GMLP:2 * D_GMLP].astype(F32)
        u = _gelu(zu)
        v0 = _gelu(zv)
        gam = g_ref[...]
        v, vhat, rstd = _ln_fwd(v0, gam, b_ref[...])
        vb = v.astype(BF16)
        s = _gmlp_spatial(ws_ref, vb) + bias_ref[...]
        d = d_ref[...]
        dz_ref[:, 0:D_GMLP] = (d * s * _gelu_grad(zu)).astype(BF16)
        ds = d * u
        dbias_ref[...] += ds
        dsb = ds.astype(BF16)
        lane = lax.broadcasted_iota(jnp.int32, (CHUNK, 128), 1)
        tril = (lax.broadcasted_iota(jnp.int32, (CHUNK, CHUNK), 0)
                >= lax.broadcasted_iota(jnp.int32, (CHUNK, CHUNK), 1))
        zero_b = jnp.zeros((CHUNK, 128), BF16)
        parts = []
        for j in range(GMLP_HEADS // 2):
            dsp = dsb[:, 128 * j:128 * (j + 1)]
            vp = vb[:, 128 * j:128 * (j + 1)]
            parts.append(jnp.where(lane < GMLP_HEAD_DIM, _dot(wst_ref[2 * j], dsp),
                                   _dot(wst_ref[2 * j + 1], dsp)))
            lo = jnp.where(lane < GMLP_HEAD_DIM, dsp, zero_b)
            hi = jnp.where(lane < GMLP_HEAD_DIM, zero_b, dsp)
            dws_ref[2 * j] += jnp.where(tril, _dot_nt(lo, vp), 0.0)
            dws_ref[2 * j + 1] += jnp.where(tril, _dot_nt(hi, vp), 0.0)
        dv = jnp.concatenate(parts, axis=1)
        dg_ref[...] += jnp.sum(dv * vhat, axis=0, keepdims=True)
        db_ref[...] += jnp.sum(dv, axis=0, keepdims=True)
        dz_ref[:, D_GMLP:2 * D_GMLP] = (_ln_bwd(dv, vhat, rstd, gam) * _gelu_grad(zv)).astype(BF16)

    return _call(
        body, "gmlp_bwd", (t // CHUNK,),
        [_rows(CHUNK, 2 * D_GMLP), _rows(CHUNK, D_GMLP), _fixed((1, D_GMLP)), _fixed((1, D_GMLP)),
         _fixed((GMLP_HEADS, CHUNK, CHUNK)), _fixed((GMLP_HEADS, CHUNK, CHUNK)), _fixed((CHUNK, D_GMLP))],
        (_rows(CHUNK, 2 * D_GMLP), _fixed((GMLP_HEADS, CHUNK, CHUNK)), _fixed((CHUNK, D_GMLP)),
         _fixed((1, D_GMLP)), _fixed((1, D_GMLP))),
        (SDS((t, 2 * D_GMLP), BF16), SDS((GMLP_HEADS, CHUNK, CHUNK), F32), SDS((CHUNK, D_GMLP), F32),
         SDS((1, D_GMLP), F32), SDS((1, D_GMLP), F32)),
        (zuv, dgm, ln_g, ln_b, wsm, wsmt, bias), sem=("arbitrary",), bg=bg)


def _mixout_fwd(x1, s5o, gm, gab, ua, ub, wmo, g, b, tm, bg=None):
    t = x1.shape[0]

    def body(x_ref, s_ref, m_ref, gab_ref, ua_ref, ub_ref, wmo_ref, g_ref, b_ref,
             xn_ref, xh_ref, rstd_ref):
        ya = _dot(s_ref[...], ua_ref[...])
        yb = _dot(m_ref[...], ub_ref[...])
        mix = (_sigmoid(gab_ref[:, 0:D_MODEL].astype(F32)) * ya
               + _sigmoid(gab_ref[:, D_MODEL:2 * D_MODEL].astype(F32)) * yb)
        r = ALPHA * x_ref[...] + _dot(mix.astype(BF16), wmo_ref[...])
        y, xh, rstd = _ln_fwd(r, g_ref[...], b_ref[...])
        xn_ref[...] = y
        xh_ref[...] = xh
        rstd_ref[...] = rstd

    return _call(
        body, "mixout_fwd", (t // tm,),
        [_rows(tm, D_MODEL), _rows(tm, D_SSM), _rows(tm, D_GMLP), _rows(tm, 2 * D_MODEL),
         _resident((D_SSM, D_MODEL)), _resident((D_GMLP, D_MODEL)), _resident((D_MODEL, D_MODEL)),
         _fixed((1, D_MODEL)), _fixed((1, D_MODEL))],
        (_rows(tm, D_MODEL), _rows(tm, D_MODEL), _rows(tm, 1)),
        (SDS((t, D_MODEL), F32), SDS((t, D_MODEL), F32), SDS((t, 1), F32)),
        (x1, s5o, gm, gab, ua, ub, wmo, g, b), sem=("parallel",), bg=bg)


def _mixout_bwd(dx2, xh, rstd, s5o, gm, gab, ua, ub, wmo, g, tm, bg=None):
    t = dx2.shape[0]

    def body(d_ref, xh_ref, rstd_ref, s_ref, m_ref, gab_ref, ua_ref, ub_ref, wmo_ref, g_ref,
             dx1_ref, dmx_ref, mb_ref, dya_ref, dyb_ref, ds5_ref, dgm_ref, dgab_ref, dg_ref, db_ref):
        @pl.when(pl.program_id(0) == 0)
        def _():
            dg_ref[...] = jnp.zeros_like(dg_ref)
            db_ref[...] = jnp.zeros_like(db_ref)

        dy = d_ref[...]
        xhv = xh_ref[...]
        dr = _ln_bwd(dy, xhv, rstd_ref[...], g_ref[...])
        dg_ref[...] += jnp.sum(dy * xhv, axis=0, keepdims=True)
        db_ref[...] += jnp.sum(dy, axis=0, keepdims=True)
        dx1_ref[...] = ALPHA * dr
        drb = dr.astype(BF16)
        dmx_ref[...] = drb
        dm = _dot_nt(drb, wmo_ref[...])
        ya = _dot(s_ref[...], ua_ref[...])
        yb = _dot(m_ref[...], ub_ref[...])
        sa = _sigmoid(gab_ref[:, 0:D_MODEL].astype(F32))
        sb = _sigmoid(gab_ref[:, D_MODEL:2 * D_MODEL].astype(F32))
        mb_ref[...] = (sa * ya + sb * yb).astype(BF16).T
        dya = (dm * sa).astype(BF16)
        dyb = (dm * sb).astype(BF16)
        dya_ref[...] = dya
        dyb_ref[...] = dyb
        dgab_ref[:, 0:D_MODEL] = (dm * ya * sa * (1.0 - sa)).astype(BF16)
        dgab_ref[:, D_MODEL:2 * D_MODEL] = (dm * yb * sb * (1.0 - sb)).astype(BF16)
        ds5_ref[...] = _dot_nt(dya, ua_ref[...])
        dgm_ref[...] = _dot_nt(dyb, ub_ref[...])

    return _call(
        body, "mixout_bwd", (t // tm,),
        [_rows(tm, D_MODEL), _rows(tm, D_MODEL), _rows(tm, 1), _rows(tm, D_SSM), _rows(tm, D_GMLP),
         _rows(tm, 2 * D_MODEL), _resident((D_SSM, D_MODEL)), _resident((D_GMLP, D_MODEL)),
         _resident((D_MODEL, D_MODEL)), _fixed((1, D_MODEL))],
        (_rows(tm, D_MODEL), _rows(tm, D_MODEL), _cols(D_MODEL, tm), _rows(tm, D_MODEL),
         _rows(tm, D_MODEL), _rows(tm, D_SSM), _rows(tm, D_GMLP), _rows(tm, 2 * D_MODEL),
         _fixed((1, D_MODEL)), _fixed((1, D_MODEL))),
        (SDS((t, D_MODEL), F32), SDS((t, D_MODEL), BF16), SDS((D_MODEL, t), BF16),
         SDS((t, D_MODEL), BF16), SDS((t, D_MODEL), BF16), SDS((t, D_SSM), F32),
         SDS((t, D_GMLP), F32), SDS((t, 2 * D_MODEL), BF16),
         SDS((1, D_MODEL), F32), SDS((1, D_MODEL), F32)),
        (dx2, xh, rstd, s5o, gm, gab, ua, ub, wmo, g), sem=("arbitrary",), bg=bg)


def _ple_loss(x3, p, tgt, wpg, wpp, tm, bg=None):
    t = x3.shape[0]

    def body(x_ref, p_ref, t_ref, wpg_ref, wpp_ref, dx_ref, xb_ref, pb_ref, dq_ref, de_ref, loss_ref):
        @pl.when(pl.program_id(0) == 0)
        def _():
            loss_ref[...] = jnp.zeros_like(loss_ref)

        x3v = x_ref[...]
        xb = x3v.astype(BF16)
        pb = p_ref[...].astype(BF16)
        xb_ref[...] = xb.T
        pb_ref[...] = pb.T
        s = _sigmoid(_dot(xb, wpg_ref[...]))
        e = _dot(pb, wpp_ref[...])
        diff = x3v + s * e - t_ref[...]
        loss_ref[...] += jnp.sum(diff * diff, axis=0, keepdims=True)
        dout = diff * (1.0 / D_MODEL)
        de_ref[...] = (dout * s).astype(BF16)
        dq = (dout * e * s * (1.0 - s)).astype(BF16)
        dq_ref[...] = dq
        dx_ref[...] = dout + _dot_nt(dq, wpg_ref[...])

    return _call(
        body, "ple_loss", (t // tm,),
        [_rows(tm, D_MODEL), _rows(tm, PLE_DIM), _rows(tm, D_MODEL),
         _resident((D_MODEL, D_MODEL)), _resident((PLE_DIM, D_MODEL))],
        (_rows(tm, D_MODEL), _cols(D_MODEL, tm), _cols(PLE_DIM, tm), _rows(tm, D_MODEL),
         _rows(tm, D_MODEL), _fixed((1, D_MODEL))),
        (SDS((t, D_MODEL), F32), SDS((D_MODEL, t), BF16), SDS((PLE_DIM, t), BF16),
         SDS((t, D_MODEL), BF16), SDS((t, D_MODEL), BF16), SDS((1, D_MODEL), F32)),
        (x3, p, tgt, wpg, wpp), sem=("arbitrary",), bg=bg)


def _s5_discretise(lre, lim, log_dt, bre, bim):
    dt = jnp.exp(log_dt)[:, None]
    mag = jnp.exp(lre * dt)
    abr = mag * jnp.cos(lim * dt)
    abi = mag * jnp.sin(lim * dt)
    nr = abr - 1.0
    ni = abi
    den = lre * lre + lim * lim
    cr = ((nr * lre + ni * lim) / den)[..., None]
    ci = ((ni * lre - nr * lim) / den)[..., None]
    return abr, abi, cr * bre - ci * bim, cr * bim + ci * bre


def _block_diag_in(bb):
    v = bb.reshape(S5_BLOCKS, 8, SSM_STATE, SSM_GROUP_CH).transpose(0, 1, 3, 2)
    return jnp.einsum("bgip,gh->bgihp", v, jnp.eye(8, dtype=bb.dtype)).reshape(
        S5_BLOCKS, S5_BLOCK_IN, S5_BLOCK_ST)


def _block_diag_in_t(dm):
    v = dm.reshape(S5_BLOCKS, 8, SSM_GROUP_CH, 8, SSM_STATE)
    d = jnp.einsum("bgihp,gh->bgip", v, jnp.eye(8, dtype=dm.dtype))
    return d.transpose(0, 1, 3, 2).reshape(SSM_GROUPS, SSM_STATE, SSM_GROUP_CH)


def _block_diag_out(cc):
    v = cc.reshape(S5_BLOCKS, 8, SSM_GROUP_CH, SSM_STATE)
    return jnp.einsum("bgip,gh->bgphi", v, jnp.eye(8, dtype=cc.dtype)).reshape(
        S5_BLOCKS, S5_BLOCK_ST, S5_BLOCK_IN)


def _block_diag_out_t(dn):
    v = dn.reshape(S5_BLOCKS, 8, SSM_STATE, 8, SSM_GROUP_CH)
    d = jnp.einsum("bgphi,gh->bgip", v, jnp.eye(8, dtype=dn.dtype))
    return d.reshape(SSM_GROUPS, SSM_GROUP_CH, SSM_STATE)


def _s5_setup(lre, lim, log_dt, bre, bim, cre, cim, d_skip, glu_w, glu_b, tb):
    seg = tb // 8
    abr, abi, bbr, bbi = _s5_discretise(lre, lim, log_dt, bre, bim)
    pr, pi = abr, abi
    for _ in range(int(math.log2(seg))):
        pr, pi = pr * pr - pi * pi, 2.0 * pr * pi
    rows = jnp.arange(tb)
    src = (rows % 8) * seg + rows // 8
    perm = (src[:, None] == jnp.arange(tb)[None, :]).astype(BF16)
    mre = _block_diag_in(bbr)
    mim = _block_diag_in(bbi)
    nre = _block_diag_out(cre)
    nim = _block_diag_out(cim)
    return {
        "perm": perm, "permt": perm.T,
        "mre": mre.astype(BF16), "mim": mim.astype(BF16),
        "mtre": mre.transpose(0, 2, 1).astype(BF16), "mtim": mim.transpose(0, 2, 1).astype(BF16),
        "nre": nre.astype(BF16), "nim": nim.astype(BF16),
        "ntre": nre.transpose(0, 2, 1).astype(BF16), "ntim": nim.transpose(0, 2, 1).astype(BF16),
        "a": jnp.stack([abr.reshape(-1), abi.reshape(-1)]),
        "ap": jnp.stack([pr.reshape(-1), pi.reshape(-1)]),
        "dskip": d_skip.reshape(1, D_SSM), "glu_w": glu_w, "glu_wt": glu_w.T,
        "glu_b": glu_b.reshape(1, D_SSM),
    }


BIG = ("ffn1_w_in", "ffn1_w_out", "mix_w_in", "ssm_glu_w", "up_a", "up_b", "mix_w_out",
       "ffn2_w_in", "ffn2_w_out", "ple_w_proj", "ple_w_gate")
BIG_AXIS = {"ffn1_w_in": 1, "ffn1_w_out": 0, "mix_w_in": 1, "ssm_glu_w": 0, "up_a": 1, "up_b": 1,
            "mix_w_out": 0, "ffn2_w_in": 1, "ffn2_w_out": 0, "ple_w_proj": 1, "ple_w_gate": 0}
SHARD_MAJOR = 2
GATHER_AXIS = dict(BIG_AXIS, ffn1_w_in=SHARD_MAJOR, ffn2_w_in=SHARD_MAJOR)
GATHER_ORDER = (("ffn1_w_in",), ("ffn1_w_out",), ("mix_w_in",), ("ssm_glu_w", "up_a", "up_b", "mix_w_out"),
                ("ffn2_w_in",), ("ffn2_w_out", "ple_w_gate", "ple_w_proj"))
GATHER_FIRST_ID = 1
REDUCE_FIRST_ID = 7
SMALL = ("ln1_g", "ln1_b", "ssm_lambda_re", "ssm_lambda_im", "ssm_log_dt", "ssm_b_re", "ssm_b_im",
         "ssm_c_re", "ssm_c_im", "ssm_d", "ssm_glu_b", "gmlp_ln_g", "gmlp_ln_b", "gmlp_w_s",
         "gmlp_b_s", "ln2_g", "ln2_b", "ln3_g", "ln3_b")
SMALL_VIEW = {"ssm_b_re": (SSM_GROUPS, SSM_STATE * SSM_GROUP_CH), "ssm_b_im": (SSM_GROUPS, SSM_STATE * SSM_GROUP_CH)}


def _small_view(k, a):
    return a.reshape(SMALL_VIEW[k]) if k in SMALL_VIEW else a


def _place():
    return lax.axis_index("x"), lax.axis_index("y"), lax.axis_index("c")


def _other_chips(x, y):
    return [(1 - x, y), (x, 1 - y), (1 - x, 1 - y)]


def _window(ref, shard_shape, axis, chip, half):
    r, c = shard_shape
    hr = r // 2
    if axis == SHARD_MAJOR:
        return ref.at[chip] if half is None else ref.at[chip, pl.ds(half * hr, hr), :]
    if axis == 0:
        if half is None:
            return ref.at[pl.ds(chip * r, r), :]
        return ref.at[pl.ds(chip * r + half * hr, hr), :]
    if half is None:
        return ref.at[:, pl.ds(chip * c, c)]
    return ref.at[pl.ds(half * hr, hr), pl.ds(chip * c, c)]


def _gather_weights(shards, axes):
    n = len(shards)
    shapes = [s.shape for s in shards]
    full = [{0: (4 * r, c), 1: (r, 4 * c), SHARD_MAJOR: (4, r, c)}[ax] for (r, c), ax in zip(shapes, axes)]

    def remote(sems, i, k, src, dst, to):
        return pltpu.make_async_remote_copy(src_ref=src, dst_ref=dst, send_sem=sems[0].at[6 * i + k],
                                            recv_sem=sems[1].at[6 * i + k], device_id=to, device_id_type=MESH)

    def own_copies(ins, outs, sems):
        x, y, c = _place()
        me = 2 * x + y
        cps = []
        for i in range(n):
            hr = shapes[i][0] // 2
            mine = ins[i].at[pl.ds(c * hr, hr), :]
            for j, (cx, cy) in enumerate(_other_chips(x, y)):
                cps.append(remote(sems, i, j, mine, _window(outs[i], shapes[i], axes[i], me, c), (cx, cy, c)))
        local = [pltpu.make_async_copy(ins[i], _window(outs[i], shapes[i], axes[i], me, None), sems[2].at[i])
                 for i in range(n)]
        return cps, local

    def start(ins, outs, sems):
        cps, local = own_copies(ins, outs, sems)
        for cp in local + cps:
            cp.start()

    def finish(ins, outs, sems):
        x, y, c = _place()
        sibling = (x, y, 1 - c)
        passed = []
        for j, (cx, cy) in enumerate(_other_chips(x, y)):
            for i in range(n):
                w = _window(outs[i], shapes[i], axes[i], 2 * cx + cy, c)
                remote(sems, i, j, w, w, (cx, cy, c)).wait_recv()
                cp = remote(sems, i, 3 + j, w, w, sibling)
                cp.start()
                passed.append(cp)
        for j, (cx, cy) in enumerate(_other_chips(x, y)):
            for i in range(n):
                w = _window(outs[i], shapes[i], axes[i], 2 * cx + cy, 1 - c)
                remote(sems, i, 3 + j, w, w, sibling).wait_recv()
        cps, local = own_copies(ins, outs, sems)
        for cp in cps + passed:
            cp.wait_send()
        for cp in local:
            cp.wait()

    return _Exchange(shards, [SDS(f, BF16) for f in full],
                     [pltpu.SemaphoreType.DMA((6 * n,)), pltpu.SemaphoreType.DMA((6 * n,)),
                      pltpu.SemaphoreType.DMA((n,))], start, finish)


def _scatter_grads(parts, shapes, axes):
    n = len(parts)

    def copies(ins, outs, sems):
        x, y, c = _place()
        return [pltpu.make_async_remote_copy(
            src_ref=_window(ins[i], shapes[i], axes[i], 2 * cx + cy, None), dst_ref=outs[i].at[j],
            send_sem=sems[0].at[3 * i + j], recv_sem=sems[1].at[3 * i + j],
            device_id=(cx, cy, c), device_id_type=MESH)
            for i in range(n) for j, (cx, cy) in enumerate(_other_chips(x, y))]

    def start(ins, outs, sems):
        for cp in copies(ins, outs, sems):
            cp.start()

    def finish(ins, outs, sems):
        for cp in copies(ins, outs, sems):
            cp.wait()

    return _Exchange(parts, [SDS((3,) + tuple(s), BF16) for s in shapes],
                     [pltpu.SemaphoreType.DMA((3 * n,)), pltpu.SemaphoreType.DMA((3 * n,))], start, finish)


def _swap_halves(parts, shapes, axes):
    n = len(parts)

    def copies(ins, outs, sems):
        x, y, c = _place()
        cps = []
        for i in range(n):
            r, _ = shapes[i]
            hr = r // 2
            if axes[i] == 0:
                cps += [pltpu.make_async_remote_copy(
                    src_ref=ins[i].at[pl.ds(k * r + (1 - c) * hr, hr), :], dst_ref=outs[i].at[k],
                    send_sem=sems[0].at[i], recv_sem=sems[1].at[i], device_id=(x, y, 1 - c),
                    device_id_type=MESH) for k in range(4)]
            else:
                cps.append(pltpu.make_async_remote_copy(
                    src_ref=ins[i].at[pl.ds((1 - c) * hr, hr), :], dst_ref=outs[i],
                    send_sem=sems[0].at[i], recv_sem=sems[1].at[i], device_id=(x, y, 1 - c),
                    device_id_type=MESH))
        return cps

    def start(ins, outs, sems):
        for cp in copies(ins, outs, sems):
            cp.start()

    def finish(ins, outs, sems):
        x, y, c = _place()
        for i in range(n):
            pltpu.make_async_remote_copy(src_ref=outs[i], dst_ref=outs[i], send_sem=sems[0].at[i],
                                         recv_sem=sems[1].at[i], device_id=(x, y, 1 - c),
                                         device_id_type=MESH).wait()

    out = [SDS((4, r // 2, c), BF16) if ax == 0 else SDS((r // 2, 4 * c), BF16)
           for (r, c), ax in zip(shapes, axes)]
    return _Exchange(parts, out, [pltpu.SemaphoreType.DMA((n,)), pltpu.SemaphoreType.DMA((n,))], start, finish)


def _scatter_halves(pres, shapes):
    n = len(pres)

    def copies(ins, outs, sems):
        x, y, c = _place()
        return [pltpu.make_async_remote_copy(
            src_ref=ins[i].at[1 + j], dst_ref=outs[i].at[j], send_sem=sems[0].at[3 * i + j],
            recv_sem=sems[1].at[3 * i + j], device_id=(cx, cy, c), device_id_type=MESH)
            for i in range(n) for j, (cx, cy) in enumerate(_other_chips(x, y))]

    def start(ins, outs, sems):
        for cp in copies(ins, outs, sems):
            cp.start()

    def finish(ins, outs, sems):
        for cp in copies(ins, outs, sems):
            cp.wait()

    return _Exchange(pres, [SDS((3, r // 2, c), BF16) for r, c in shapes],
                     [pltpu.SemaphoreType.DMA((3 * n,)), pltpu.SemaphoreType.DMA((3 * n,))], start, finish)


def _swap_with_sibling(arrs):
    n = len(arrs)

    def copies(ins, outs, sems):
        x, y, c = _place()
        return [pltpu.make_async_remote_copy(src_ref=ins[i], dst_ref=outs[i], send_sem=sems[0].at[i],
                                             recv_sem=sems[1].at[i], device_id=(x, y, 1 - c),
                                             device_id_type=MESH) for i in range(n)]

    def start(ins, outs, sems):
        for cp in copies(ins, outs, sems):
            cp.start()

    def finish(ins, outs, sems):
        for cp in copies(ins, outs, sems):
            cp.wait()

    return _Exchange(arrs, [SDS(a.shape, a.dtype) for a in arrs],
                     [pltpu.SemaphoreType.DMA((n,)), pltpu.SemaphoreType.DMA((n,))], start, finish)


def _gather_small(arrs):
    n = len(arrs)

    def copy(sems, outs, i, k, block, to, src=None):
        px, py, pc = block
        dst = outs[i].at[4 * px + 2 * py + pc]
        return pltpu.make_async_remote_copy(
            src_ref=dst if src is None else src, dst_ref=dst, send_sem=sems[0].at[7 * i + k],
            recv_sem=sems[1].at[7 * i + k], device_id=to, device_id_type=MESH)

    direct = [math.prod(a.shape) * 4 <= DIRECT_GATHER_BYTES for a in arrs]

    def own_copies(ins, outs, sems):
        x, y, c = _place()
        cps = []
        for i in range(n):
            cps.append(copy(sems, outs, i, 0, (x, y, c), (x, y, 1 - c), src=ins[i]))
            for j, (cx, cy) in enumerate(_other_chips(x, y)):
                cps.append(copy(sems, outs, i, 1 + j, (x, y, c), (cx, cy, c), src=ins[i]))
                if direct[i]:
                    cps.append(copy(sems, outs, i, 4 + j, (x, y, c), (cx, cy, 1 - c), src=ins[i]))
        local = [pltpu.make_async_copy(ins[i], outs[i].at[4 * x + 2 * y + c], sems[2].at[i]) for i in range(n)]
        return cps, local

    def start(ins, outs, sems):
        cps, local = own_copies(ins, outs, sems)
        for cp in local + cps:
            cp.start()

    def finish(ins, outs, sems):
        x, y, c = _place()
        passed = []
        for j, (cx, cy) in enumerate(_other_chips(x, y)):
            for i in range(n):
                copy(sems, outs, i, 1 + j, (cx, cy, c), (x, y, c)).wait_recv()
                if not direct[i]:
                    cp = copy(sems, outs, i, 4 + j, (cx, cy, c), (x, y, 1 - c))
                    cp.start()
                    passed.append(cp)
        for i in range(n):
            copy(sems, outs, i, 0, (x, y, 1 - c), (x, y, c)).wait_recv()
            for j, (cx, cy) in enumerate(_other_chips(x, y)):
                copy(sems, outs, i, 4 + j, (cx, cy, 1 - c), (x, y, c)).wait_recv()
        cps, local = own_copies(ins, outs, sems)
        for cp in cps + passed:
            cp.wait_send()
        for cp in local:
            cp.wait()

    return _Exchange(arrs, [SDS((N_DEV,) + a.shape, F32) for a in arrs],
                     [pltpu.SemaphoreType.DMA((7 * n,)), pltpu.SemaphoreType.DMA((7 * n,)),
                      pltpu.SemaphoreType.DMA((n,))], start, finish)


def _local_step(x, p, tgt, wb, ws, shards=None):
    bsz, seq, _ = x.shape
    t = bsz * seq
    tm = min(256, t)
    tb = min(256, seq)
    x0 = x.reshape(t, D_MODEL)
    p0 = p.reshape(t, PLE_DIM)
    tg = tgt.reshape(t, D_MODEL)
    row = lambda v: v.reshape(1, -1)
    dist = shards is not None
    wb = dict(wb)
    recv, sums, other, gathered = {}, {}, {}, {}
    gb = {}
    gs = {}
    shape_of, axis_of = {}, {}
    chip = None
    if dist:
        shape_of = {k: tuple(shards[k].shape) for k in BIG}
        axis_of = dict(BIG_AXIS)
        for q in range(LAST_PIECES):
            shape_of[LAST_PIECE % q] = (D_MODEL // LAST_PIECES, shape_of["ffn1_w_in"][1])
            axis_of[LAST_PIECE % q] = 1
        xi, yi, ci = _place()
        chip = (2 * xi + yi).astype(jnp.int32).reshape(1)
        ids = jnp.stack([2 * xi + yi] + [2 * cx + cy for cx, cy in _other_chips(xi, yi)] + [ci]).astype(jnp.int32)
    halfbuf, pre = {}, {}

    def gather(names):
        return _gather_weights([shards[k] for k in names], [GATHER_AXIS[k] for k in names]) if dist else None

    def exchange(scat=(), swap=(), halves=(), scat2=(), swap2=(), extra=None):
        if not dist:
            return None, []
        parts, tags = [], []
        if scat:
            parts.append(_scatter_grads([gb[k][1] for k in scat], [shape_of[k] for k in scat],
                                        [axis_of[k] for k in scat]))
            tags.append((recv, scat))
        if swap:
            for k in swap:
                sums[k] = _sum_blocks(gb[k][0], recv[k], shape_of[k], axis_of[k], chip, "sum_" + k)
            parts.append(_swap_with_sibling([sums[k] for k in swap]))
            tags.append((other, swap))
        if halves:
            parts.append(_swap_halves([gb[k][1] for k in halves], [shape_of[k] for k in halves],
                                      [axis_of[k] for k in halves]))
            tags.append((halfbuf, halves))
        if scat2:
            for k in scat2:
                pre[k] = _presum(gb[k][0], halfbuf[k], shape_of[k], axis_of[k], ids, "presum_" + k)
            parts.append(_scatter_halves([pre[k][1] for k in scat2], [shape_of[k] for k in scat2]))
            tags.append((recv, scat2))
        if swap2:
            for k in swap2:
                sums[k] = _sum_half(pre[k][0], recv[k], "sum_" + k)
            parts.append(_swap_with_sibling([sums[k] for k in swap2]))
            tags.append((other, swap2))
        if extra is not None:
            parts.append(extra[0])
            tags.append((extra[1], extra[2]))
        return (_join(parts), tags) if parts else (None, [])

    def take(ex_tags, got):
        ex, tags = ex_tags
        if ex is not None:
            for (dst, names), (o0, o1) in zip(tags, ex.cuts):
                dst.update(zip(names, got[o0:o1]))

    launched = []

    def launch(ex_tags):
        if ex_tags[0] is not None:
            n = len(launched)
            launched.append(n)
            take(ex_tags, _run_exchange_on_sequencer(ex_tags[0], "reduce_%d" % n, REDUCE_FIRST_ID + n))

    small_shape = {k: _small_view(k, v).shape for k, v in ws.items()}
    small_shape["loss_rows"] = (1, D_MODEL)
    ws = {k: v if (v.ndim == 2 and k != "ssm_log_dt") else v[0] for k, v in ws.items()}
    tril = jnp.tril(jnp.ones((CHUNK, CHUNK), dtype=bool))
    wsm = jnp.where(tril[None], ws["gmlp_w_s"], 0.0)
    wsm_b = wsm.astype(BF16)
    wsmt_b = wsm.transpose(0, 2, 1).astype(BF16)
    bias = jnp.repeat(ws["gmlp_b_s"].T, GMLP_HEAD_DIM, axis=1)

    tf = min(512, t)
    if dist:
        for gi, names in enumerate(GATHER_ORDER):
            wb.update(zip(names, _run_exchange_on_sequencer(gather(names), "gather_%d" % gi, GATHER_FIRST_ID + gi)))
    (x0b, h1, a1), _ = _ffn_proj(x0, wb["ffn1_w_in"], tf, "ffn1_proj")
    (x1, xh1, rstd1), _ = _ffn_out(x0, a1, wb["ffn1_w_out"], row(ws["ln1_g"]), row(ws["ln1_b"]), tf, "ffn1_out")
    sp = _s5_setup(ws["ssm_lambda_re"], ws["ssm_lambda_im"], ws["ssm_log_dt"], ws["ssm_b_re"],
                   ws["ssm_b_im"], ws["ssm_c_re"], ws["ssm_c_im"], ws["ssm_d"], wb["ssm_glu_w"],
                   ws["ssm_glu_b"], tb)
    (x1b, za, zuv, gab), _ = _mixin_fwd(x1, wb["mix_w_in"], tm)
    (s5o, s5ot, y2p, carries), _ = _s5_fwd(za, sp, bsz, seq, tb)
    (gm, gmt), _ = _gmlp_fwd(zuv, row(ws["gmlp_ln_g"]), row(ws["gmlp_ln_b"]), wsm_b, bias)
    (x2, xh2, rstd2), _ = _mixout_fwd(x1, s5o, gm, gab, wb["up_a"], wb["up_b"], wb["mix_w_out"],
                                           row(ws["ln2_g"]), row(ws["ln2_b"]), tm)
    (x2b, h2, a2), _ = _ffn_proj(x2, wb["ffn2_w_in"], tf, "ffn2_proj")
    (x3, xh3, rstd3), _ = _ffn_out(x2, a2, wb["ffn2_w_out"], row(ws["ln3_g"]), row(ws["ln3_b"]), tf, "ffn2_out")
    (dx3, x3b, pb, dq, de, loss_rows), _ = _ple_loss(x3, p0, tg, wb["ple_w_gate"], wb["ple_w_proj"], tm)
    gb["ple_w_gate"], _ = _tn_matmul(x3b, dq, "dw_ple_gate", 1024, 1024, a_t=True)
    gb["ple_w_proj"], _ = _tn_matmul(pb, de, "dw_ple_proj", 256, 1024, a_t=True)
    launch(exchange(scat=("ple_w_gate", "ple_w_proj")))
    (dx2, dh2, df2, gs["ln3_g"], gs["ln3_b"]), _ = _ffn_bwd(
        dx3, xh3, rstd3, h2, wb["ffn2_w_in"], wb["ffn2_w_out"], row(ws["ln3_g"]), tm, "ffn2_bwd")
    gb["ffn2_w_out"], _ = _tn_matmul(a2, df2, "dw_ffn2_out", 1408, 1024)
    launch(exchange(scat=("ffn2_w_out",)))
    gb["ffn2_w_in"], _ = _tn_matmul(x2b, dh2, "dw_ffn2_in", 1024, 1408, a_t=True)
    launch(exchange(scat=("ffn2_w_in",), swap=("ple_w_gate", "ple_w_proj")))
    (dx1a, dmx, mb, dya, dyb, ds5, dgm, dgab, gs["ln2_g"], gs["ln2_b"]), _ = _mixout_bwd(
        dx2, xh2, rstd2, s5o, gm, gab, wb["up_a"], wb["up_b"], wb["mix_w_out"], row(ws["ln2_g"]), tm)
    gb["mix_w_out"], _ = _tn_matmul(mb, dmx, "dw_mix_out", 1024, 1024, a_t=True)
    gb["up_a"], _ = _tn_matmul(s5ot, dya, "dw_up_a", 512, 1024, a_t=True)
    gb["up_b"], _ = _tn_matmul(gmt, dyb, "dw_up_b", 512, 1024, a_t=True)
    launch(exchange(scat=("mix_w_out", "up_a", "up_b"), swap=("ffn2_w_out",)))
    (dza, dmr, dmi, dnr, dni, da, ddsk, dgw, dgb), _ = _s5_bwd(za, y2p, ds5, carries, sp, bsz, seq, tb)
    gb["ssm_glu_w"] = (dgw, dgw.astype(BF16))
    launch(exchange(scat=("ssm_glu_w",), swap=("ffn2_w_in",)))
    (dzuv, dws, dbias, gs["gmlp_ln_g"], gs["gmlp_ln_b"]), _ = _gmlp_bwd(
        zuv, dgm, row(ws["gmlp_ln_g"]), row(ws["gmlp_ln_b"]), wsm_b, wsmt_b, bias)
    (dx1,), _ = _mixin_bwd(dx1a, dza, dzuv, dgab, wb["mix_w_in"], tm)
    g_mi, _ = _tn_matmul(x1b, dza, "dw_mix_in_a", 1024, 512, 0, 3584, a_t=True)
    g_mi, _ = _tn_matmul(x1b, dzuv, "dw_mix_in_uv", 1024, 512, 1, 3584, g_mi, a_t=True)
    gb["mix_w_in"], _ = _tn_matmul(x1b, dgab, "dw_mix_in_g", 1024, 512, 3, 3584, g_mi, a_t=True)
    launch(exchange(swap=("mix_w_out", "up_a", "up_b", "ssm_glu_w")))

    d_abr = da[0].sum(axis=0).reshape(SSM_GROUPS, SSM_STATE)
    d_abi = da[1].sum(axis=0).reshape(SSM_GROUPS, SSM_STATE)
    _, vjp = jax.vjp(_s5_discretise, ws["ssm_lambda_re"], ws["ssm_lambda_im"], ws["ssm_log_dt"],
                     ws["ssm_b_re"], ws["ssm_b_im"])
    (gs["ssm_lambda_re"], gs["ssm_lambda_im"], gs["ssm_log_dt"], gs["ssm_b_re"], gs["ssm_b_im"]) = vjp(
        (d_abr, d_abi, _block_diag_in_t(dmr), _block_diag_in_t(dmi)))
    gs["ssm_c_re"] = _block_diag_out_t(dnr)
    gs["ssm_c_im"] = _block_diag_out_t(dni)
    gs["ssm_d"] = ddsk
    gs["ssm_glu_b"] = dgb
    gs["gmlp_w_s"] = dws
    gs["gmlp_b_s"] = dbias.reshape(CHUNK, GMLP_HEADS, GMLP_HEAD_DIM).sum(axis=-1).T
    gs["loss_rows"] = loss_rows

    def small_gather(names):
        return (_gather_small([gs[k].reshape(small_shape[k]) for k in names]), gathered, names) if dist else None

    late = ("ln1_g", "ln1_b")
    launch(exchange(scat=("mix_w_in",), extra=small_gather(tuple(k for k in SMALL + ("loss_rows",) if k not in late))))
    (dx0, dh1, df1, gs["ln1_g"], gs["ln1_b"]), _ = _ffn_bwd(
        dx1, xh1, rstd1, h1, wb["ffn1_w_in"], wb["ffn1_w_out"], row(ws["ln1_g"]), tm, "ffn1_bwd")
    grad_x = dx0.reshape(bsz, seq, D_MODEL)
    if not dist:
        gb["ffn1_w_out"], _ = _tn_matmul(a1, df1, "dw_ffn1_out", 1408, 1024)
        gb["ffn1_w_in"], _ = _tn_matmul(x0b, dh1, "dw_ffn1_in", 1024, 1408, a_t=True)
        return loss_rows, grad_x, gb, {k: gs[k].reshape(small_shape[k]) for k in SMALL}, sums, other, gathered, None
    launch(exchange(extra=small_gather(late)))
    gb["ffn1_w_out"], _ = _tn_matmul(a1, df1, "dw_ffn1_out", 1408, 1024)
    last = ["ffn1_w_out"] + [LAST_PIECE % q for q in range(LAST_PIECES)]
    for i in range(1, len(last) + 3):
        stage = lambda d: tuple(last[i - d:i - d + 1]) if 0 <= i - d < len(last) else ()
        launch(exchange(halves=stage(1), scat2=stage(2), swap2=stage(3), swap=("mix_w_in",) if i == 2 else ()))
        if i < len(last):
            gb[last[i]], _ = _tn_matmul(x0b, dh1, "dw_" + last[i], D_MODEL // LAST_PIECES, 1408,
                                        a_cols=(i - 1, 1), a_t=True)
    return loss_rows, grad_x, gb, gs, sums, other, gathered, ids


def _adamw(w, g, m, v):
    m = ADAM_B1 * m + (1.0 - ADAM_B1) * g
    v = ADAM_B2 * v + (1.0 - ADAM_B2) * (g * g)
    m_hat = m / ADAM_C1
    v_hat = v / ADAM_C2
    delta = -ADAM_LR * (m_hat / (jnp.sqrt(v_hat) + ADAM_EPS) + ADAM_WD * w)
    return delta, m, v


def _sum_blocks(part, recv, shape, axis, chip, name):
    r, c = shape
    rb = r // 8

    def body(chip_ref, p_ref, r_ref, o_ref):
        o_ref[...] = (p_ref[...] + r_ref[0].astype(F32) + r_ref[1].astype(F32) + r_ref[2].astype(F32))

    if axis == 0:
        own = pl.BlockSpec((rb, c), lambda i, k: (k[0] * 8 + i, 0))
    else:
        own = pl.BlockSpec((rb, c), lambda i, k: (i, k[0]))
    grid_spec = pltpu.PrefetchScalarGridSpec(
        num_scalar_prefetch=1, grid=(8,),
        in_specs=[own, pl.BlockSpec((3, rb, c), lambda i, k: (0, i, 0))],
        out_specs=pl.BlockSpec((rb, c), lambda i, k: (i, 0)))
    return pl.pallas_call(body, name=name, out_shape=SDS((r, c), F32), grid_spec=grid_spec,
                          compiler_params=_params(("parallel",)))(chip, part, recv)


def _presum(part, half, shape, axis, ids, name):
    r, c = shape
    rb = r // 4

    def body(ids_ref, p_ref, h_ref, of_ref, ob_ref):
        s = p_ref[...] + h_ref[...].astype(F32)
        ob_ref[...] = s.astype(BF16)

        @pl.when(pl.program_id(1) == 0)
        def _():
            of_ref[...] = s

    if axis == 0:
        p_spec = pl.BlockSpec((rb, c), lambda i, t, ids: (ids[t] * 4 + ids[4] * 2 + i, 0))
        h_spec = pl.BlockSpec((None, rb, c), lambda i, t, ids: (ids[t], i, 0))
    else:
        p_spec = pl.BlockSpec((rb, c), lambda i, t, ids: (ids[4] * 2 + i, ids[t]))
        h_spec = pl.BlockSpec((rb, c), lambda i, t, ids: (i, ids[t]))
    grid_spec = pltpu.PrefetchScalarGridSpec(
        num_scalar_prefetch=1, grid=(2, 4), in_specs=[p_spec, h_spec],
        out_specs=(pl.BlockSpec((rb, c), lambda i, t, ids: (i, 0)),
                   pl.BlockSpec((None, rb, c), lambda i, t, ids: (t, i, 0))))
    return pl.pallas_call(body, name=name, out_shape=(SDS((r // 2, c), F32), SDS((4, r // 2, c), BF16)),
                          grid_spec=grid_spec, compiler_params=_params(("parallel", "arbitrary")))(ids, part, half)


def _sum_half(pre, recv, name):
    hr, c = pre.shape
    rb = hr // 2

    def body(p_ref, r_ref, o_ref):
        o_ref[...] = (p_ref[...] + r_ref[0].astype(F32) + r_ref[1].astype(F32) + r_ref[2].astype(F32))

    spec = pl.BlockSpec((rb, c), lambda i: (i, 0))
    return pl.pallas_call(body, name=name, grid=(2,), out_shape=SDS((hr, c), F32),
                          in_specs=[spec, pl.BlockSpec((3, rb, c), lambda i: (0, i, 0))], out_specs=spec,
                          compiler_params=_params(("parallel",)))(pre, recv)


def _adam_halves(w, mine, oth, m, v, ids, name, piece=0, prev=None):
    r, c = w.shape
    rb = mine.shape[0] // 2

    def body(ids_ref, w_ref, a_ref, b_ref, m_ref, v_ref, *rest):
        g_ref, d_ref, nm_ref, nv_ref = rest[-4:]
        g = jnp.where(pl.program_id(0) // 2 == ids_ref[4], a_ref[...], b_ref[...])
        g_ref[...] = g
        d_ref[...], nm_ref[...], nv_ref[...] = _adamw(w_ref[...], g, m_ref[...], v_ref[...])

    whole = pl.BlockSpec((rb, c), lambda i, ids: (i + 4 * piece, 0))
    part = pl.BlockSpec((rb, c), lambda i, ids: (i % 2, 0))
    in_specs = [whole, part, part, whole, whole]
    args = [w, mine, oth, m, v]
    aliases = {}
    if prev is not None:
        in_specs += [pl.BlockSpec(memory_space=pl.ANY)] * 4
        args += list(prev)
        aliases = {6: 0, 7: 1, 8: 2, 9: 3}
    grid_spec = pltpu.PrefetchScalarGridSpec(num_scalar_prefetch=1, grid=(4,), in_specs=in_specs,
                                             out_specs=(whole,) * 4)
    return pl.pallas_call(body, name=name, out_shape=tuple(SDS((r, c), F32) for _ in range(4)),
                          grid_spec=grid_spec, input_output_aliases=aliases,
                          compiler_params=_params(("parallel",)))(ids, *args)


def _adam_big(w, ga, gb, m, v, name, piece=0, prev=None):
    r, c = w.shape
    pr = ga.shape[0]
    steps = 8 if pr == r else 2
    rb = pr // steps
    off = piece * steps

    def body(w_ref, ga_ref, gb_ref, m_ref, v_ref, *rest):
        g_ref, d_ref, nm_ref, nv_ref = rest[-4:]
        g = ga_ref[...] + gb_ref[...]
        g_ref[...] = g
        d_ref[...], nm_ref[...], nv_ref[...] = _adamw(w_ref[...], g, m_ref[...], v_ref[...])

    whole = pl.BlockSpec((rb, c), lambda i: (i + off, 0))
    part = pl.BlockSpec((rb, c), lambda i: (i, 0))
    in_specs = [whole, part, part, whole, whole]
    args = [w, ga, gb, m, v]
    aliases = {}
    if prev is not None:
        in_specs += [pl.BlockSpec(memory_space=pl.ANY)] * 4
        args += list(prev)
        aliases = {5: 0, 6: 1, 7: 2, 8: 3}
    return pl.pallas_call(
        body, name=name, grid=(steps,), out_shape=tuple(SDS((r, c), F32) for _ in range(4)),
        in_specs=in_specs, out_specs=(whole,) * 4, input_output_aliases=aliases,
        compiler_params=_params(("parallel",)),
    )(*args)


def _adam_small(ws, gathered, ms, vs):
    n = len(ws)

    def body(*refs):
        w_refs, g_refs, m_refs, v_refs = refs[:n], refs[n:2 * n], refs[2 * n:3 * n], refs[3 * n:4 * n]
        outs = refs[4 * n:]
        for i in range(n):
            g = g_refs[i][0]
            for d in range(1, N_DEV):
                g = g + g_refs[i][d]
            delta, nm, nv = _adamw(w_refs[i][...], g, m_refs[i][...], v_refs[i][...])
            outs[i][...] = g
            outs[n + i][...] = delta
            outs[2 * n + i][...] = nm
            outs[3 * n + i][...] = nv

    vmem = pl.BlockSpec(memory_space=pltpu.VMEM)
    shapes = [w.shape for w in ws]
    return pl.pallas_call(
        body, name="adam_small", out_shape=tuple(SDS(s, F32) for s in shapes * 4),
        in_specs=[vmem] * (4 * n), out_specs=tuple([vmem] * (4 * n)),
        compiler_params=pltpu.CompilerParams(vmem_limit_bytes=VMEM_LIMIT_BYTES),
    )(*ws, *gathered, *ms, *vs)


def _sum_loss(gathered):
    def body(g_ref, o_ref):
        tot = g_ref[0]
        for d in range(1, N_DEV):
            tot = tot + g_ref[d]
        o_ref[...] = (0.5 / D_MODEL) * jnp.sum(tot, axis=1, keepdims=True)

    vmem = pl.BlockSpec(memory_space=pltpu.VMEM)
    return pl.pallas_call(body, name="sum_loss", out_shape=SDS((1, 1), F32), in_specs=[vmem],
                          out_specs=vmem)(gathered)


def kernel(x, p, ffn1_w_in, ffn1_w_out, ln1_g, ln1_b, mix_w_in, ssm_lambda_re, ssm_lambda_im, ssm_log_dt, ssm_b_re, ssm_b_im, ssm_c_re, ssm_c_im, ssm_d, ssm_glu_w, ssm_glu_b, gmlp_ln_g, gmlp_ln_b, gmlp_w_s, gmlp_b_s, up_a, up_b, mix_w_out, ln2_g, ln2_b, ffn2_w_in, ffn2_w_out, ln3_g, ln3_b, ple_w_proj, ple_w_gate, loss_target, m_ffn1_w_in, m_ffn1_w_out, m_ln1_g, m_ln1_b, m_mix_w_in, m_ssm_lambda_re, m_ssm_lambda_im, m_ssm_log_dt, m_ssm_b_re, m_ssm_b_im, m_ssm_c_re, m_ssm_c_im, m_ssm_d, m_ssm_glu_w, m_ssm_glu_b, m_gmlp_ln_g, m_gmlp_ln_b, m_gmlp_w_s, m_gmlp_b_s, m_up_a, m_up_b, m_mix_w_out, m_ln2_g, m_ln2_b, m_ffn2_w_in, m_ffn2_w_out, m_ln3_g, m_ln3_b, m_ple_w_proj, m_ple_w_gate, v_ffn1_w_in, v_ffn1_w_out, v_ln1_g, v_ln1_b, v_mix_w_in, v_ssm_lambda_re, v_ssm_lambda_im, v_ssm_log_dt, v_ssm_b_re, v_ssm_b_im, v_ssm_c_re, v_ssm_c_im, v_ssm_d, v_ssm_glu_w, v_ssm_glu_b, v_gmlp_ln_g, v_gmlp_ln_b, v_gmlp_w_s, v_gmlp_b_s, v_up_a, v_up_b, v_mix_w_out, v_ln2_g, v_ln2_b, v_ffn2_w_in, v_ffn2_w_out, v_ln3_g, v_ln3_b, v_ple_w_proj, v_ple_w_gate):
    given = dict(locals())
    order = ("ffn1_w_in", "ffn1_w_out", "ln1_g", "ln1_b", "mix_w_in", "ssm_lambda_re", "ssm_lambda_im",
             "ssm_log_dt", "ssm_b_re", "ssm_b_im", "ssm_c_re", "ssm_c_im", "ssm_d", "ssm_glu_w", "ssm_glu_b",
             "gmlp_ln_g", "gmlp_ln_b", "gmlp_w_s", "gmlp_b_s", "up_a", "up_b", "mix_w_out", "ln2_g", "ln2_b",
             "ffn2_w_in", "ffn2_w_out", "ln3_g", "ln3_b", "ple_w_proj", "ple_w_gate")
    assert set(order) == set(BIG + SMALL)

    shard = {k: given[k][0] for k in BIG}
    shard_b = {k: shard[k].astype(BF16) for k in BIG}
    loss_rows, grad_x, gb, gs, sums, other, gathered, ids = _local_step(
        x, given["p"][0], loss_target, {}, {k: given[k] for k in SMALL}, shard_b)

    out = {}
    for k in BIG:
        moments = (given["m_" + k][0], given["v_" + k][0])
        if k == "ffn1_w_out":
            out[k] = _adam_halves(shard[k], sums[k], other[k], *moments, ids, "adam_" + k)
        elif k == "ffn1_w_in":
            for q in range(LAST_PIECES):
                kq = LAST_PIECE % q
                out[k] = _adam_halves(shard[k], sums[kq], other[kq], *moments, ids, "adam_" + kq, q, out.get(k))
        else:
            out[k] = _adam_big(shard[k], sums[k], other[k], *moments, "adam_" + k)

    res = _adam_small([_small_view(k, given[k]) for k in SMALL], [gathered[k] for k in SMALL],
                      [_small_view(k, given["m_" + k]) for k in SMALL],
                      [_small_view(k, given["v_" + k]) for k in SMALL])
    ns = len(SMALL)
    for i, k in enumerate(SMALL):
        out[k] = tuple(res[j * ns + i].reshape(given[k].shape) for j in range(4))
    loss = _sum_loss(gathered["loss_rows"]).reshape(())

    lead = lambda k, j: out[k][j][None] if k in BIG else out[k][j]
    return (loss, grad_x, *[lead(k, 0) for k in order], *[lead(k, 1) for k in order],
            *[lead(k, 2) for k in order], *[lead(k, 3) for k in order])
```

```python
import math

import jax
import jax.numpy as jnp
from jax import lax
from jax.experimental import pallas as pl
from jax.experimental.pallas import tpu as pltpu
from jax.experimental.pallas import tpu_sc as plsc

F32 = jnp.float32
BF16 = jnp.bfloat16
MESH = pl.DeviceIdType.MESH
SDS = jax.ShapeDtypeStruct

D_MODEL = 1024
D_FF = 2816
D_SSM = 512
D_GMLP = 512
SSM_GROUPS = 32
SSM_GROUP_CH = 16
SSM_STATE = 64
SSM_LANES = SSM_GROUPS * SSM_STATE
GMLP_HEADS = 8
GMLP_HEAD_DIM = 64
CHUNK = 128
PLE_DIM = 256
LN_EPS = 1e-5
ALPHA = 2.0 ** 0.25

ADAM_LR = 0.001
ADAM_B1 = 0.9
ADAM_B2 = 0.999
ADAM_EPS = 1e-08
ADAM_WD = 0.01
ADAM_STEP = 10
ADAM_C1 = 1.0 - ADAM_B1 ** ADAM_STEP
ADAM_C2 = 1.0 - ADAM_B2 ** ADAM_STEP

N_DEV = 8
VMEM_LIMIT_BYTES = 56 * 1024 * 1024
FFN_COLS = 1408
S5_BLOCKS = 4
S5_BLOCK_IN = D_SSM // S5_BLOCKS
S5_BLOCK_ST = SSM_LANES // S5_BLOCKS
SCAN_LANES = 512
TN_K_BLOCK = 2048
DIRECT_GATHER_BYTES = 0
LAST_PIECES = 2
LAST_PIECE = "ffn1_w_in_q%d"
_G0 = math.sqrt(2.0 / math.pi)
_G1 = 0.044715


def _dot(a, b):
    return jnp.dot(a, b, preferred_element_type=F32)


def _dot_nt(a, b):
    return lax.dot_general(a, b, (((1,), (1,)), ((), ())), preferred_element_type=F32)


def _dot_tn(a, b):
    return lax.dot_general(a, b, (((0,), (0,)), ((), ())), preferred_element_type=F32)


def _sigmoid(x):
    return 1.0 / (1.0 + jnp.exp(-x))


def _gelu(x):
    t = jnp.tanh(_G0 * (x + _G1 * x * x * x))
    return 0.5 * x * (1.0 + t)


def _gelu_grad(x):
    t = jnp.tanh(_G0 * (x + _G1 * x * x * x))
    return 0.5 * (1.0 + t) + 0.5 * x * (1.0 - t * t) * _G0 * (1.0 + 3.0 * _G1 * x * x)


def _ln_fwd(r, g, b):
    mu = jnp.mean(r, axis=-1, keepdims=True)
    d = r - mu
    var = jnp.mean(d * d, axis=-1, keepdims=True)
    rstd = lax.rsqrt(var + LN_EPS)
    xh = d * rstd
    return xh * g + b, xh, rstd


def _ln_bwd(dy, xh, rstd, g):
    dxh = dy * g
    m1 = jnp.mean(dxh, axis=-1, keepdims=True)
    m2 = jnp.mean(dxh * xh, axis=-1, keepdims=True)
    return rstd * (dxh - m1 - xh * m2)


def _resident(shape):
    nd = len(shape)
    return pl.BlockSpec(shape, lambda *_: (0,) * nd, pipeline_mode=pl.Buffered(1))


def _fixed(shape):
    nd = len(shape)
    return pl.BlockSpec(shape, lambda *_: (0,) * nd)


def _rows(tm, cols):
    return pl.BlockSpec((tm, cols), lambda i: (i, 0))


def _cols(rows, tm):
    return pl.BlockSpec((rows, tm), lambda i: (0, i))


def _params(sem):
    return pltpu.CompilerParams(dimension_semantics=sem, vmem_limit_bytes=VMEM_LIMIT_BYTES)


class _Exchange:
    def __init__(self, args, out_shape, sems, start, finish):
        self.args, self.out_shape, self.sems = list(args), list(out_shape), list(sems)
        self.start, self.finish = start, finish
        self.cuts = [(0, len(self.out_shape))]


def _call(body, name, grid, in_specs, out_specs, out_shape, args, scratch=(), sem=None, bg=None, aliases=None):
    aliases = {} if aliases is None else aliases
    if bg is None:
        res = pl.pallas_call(body, name=name, grid=grid, out_shape=tuple(out_shape), in_specs=list(in_specs),
                             out_specs=tuple(out_specs), scratch_shapes=list(scratch),
                             input_output_aliases=aliases, compiler_params=_params(sem))(*args)
        return tuple(res), ()
    n_in, n_out, n_bi, n_bo, n_sc = len(args), len(out_shape), len(bg.args), len(bg.out_shape), len(scratch)

    def wrapped(*refs):
        ins = refs[:n_in]
        b_ins = refs[n_in:n_in + n_bi]
        outs = refs[n_in + n_bi:n_in + n_bi + n_out]
        b_outs = refs[n_in + n_bi + n_out:n_in + n_bi + n_out + n_bo]
        rest = refs[n_in + n_bi + n_out + n_bo:]
        scr, b_sems = rest[:n_sc], rest[n_sc:]
        first = pl.program_id(0) == 0
        last = pl.program_id(0) == grid[0] - 1
        for ax in range(1, len(grid)):
            first = jnp.logical_and(first, pl.program_id(ax) == 0)
            last = jnp.logical_and(last, pl.program_id(ax) == grid[ax] - 1)

        @pl.when(first)
        def _():
            bg.start(b_ins, b_outs, b_sems)

        body(*ins, *outs, *scr)

        @pl.when(last)
        def _():
            bg.finish(b_ins, b_outs, b_sems)

    any_spec = pl.BlockSpec(memory_space=pl.ANY)
    res = pl.pallas_call(
        wrapped, name=name, grid=grid, out_shape=tuple(out_shape) + tuple(bg.out_shape),
        in_specs=list(in_specs) + [any_spec] * n_bi, out_specs=tuple(out_specs) + (any_spec,) * n_bo,
        scratch_shapes=list(scratch) + list(bg.sems), input_output_aliases=aliases,
        compiler_params=_params(tuple("arbitrary" for _ in grid)))(*args, *bg.args)
    return tuple(res[:n_out]), tuple(res[n_out:])


def _run_exchange(ex, name):
    n_i, n_o = len(ex.args), len(ex.out_shape)

    def body(*refs):
        ins, outs, sems = refs[:n_i], refs[n_i:n_i + n_o], refs[n_i + n_o:]
        ex.start(ins, outs, sems)
        ex.finish(ins, outs, sems)

    any_spec = pl.BlockSpec(memory_space=pl.ANY)
    return tuple(pl.pallas_call(body, name=name, out_shape=tuple(ex.out_shape), in_specs=[any_spec] * n_i,
                                out_specs=(any_spec,) * n_o, scratch_shapes=list(ex.sems))(*ex.args))


def _run_exchange_on_sequencer(ex, name, collective_id):
    n_i, n_o = len(ex.args), len(ex.out_shape)

    def body(*refs):
        ins, outs, sems = refs[:n_i], refs[n_i:n_i + n_o], refs[n_i + n_o:]
        x, y, c = lax.axis_index("x"), lax.axis_index("y"), lax.axis_index("c")
        barrier = pltpu.get_barrier_semaphore()
        for peer in [(x, y, 1 - c), (1 - x, y, c), (x, 1 - y, c), (1 - x, 1 - y, c)]:
            pl.semaphore_signal(barrier, inc=1, device_id=peer, device_id_type=MESH)
        pl.semaphore_wait(barrier, 4)
        ex.start(ins, outs, sems)
        ex.finish(ins, outs, sems)

    return tuple(pl.kernel(body, out_type=tuple(ex.out_shape),
                           mesh=plsc.ScalarSubcoreMesh(axis_name="sequencer", num_cores=1),
                           scratch_types=list(ex.sems), name=name,
                           compiler_params=pltpu.CompilerParams(collective_id=collective_id))(*ex.args))


def _join(exchanges):
    cuts = []
    a = o = q = 0
    for e in exchanges:
        cuts.append((a, a + len(e.args), o, o + len(e.out_shape), q, q + len(e.sems)))
        a, o, q = cuts[-1][1], cuts[-1][3], cuts[-1][5]

    def start(ins, outs, sems):
        for e, (a0, a1, o0, o1, q0, q1) in zip(exchanges, cuts):
            e.start(ins[a0:a1], outs[o0:o1], sems[q0:q1])

    def finish(ins, outs, sems):
        for e, (a0, a1, o0, o1, q0, q1) in zip(exchanges, cuts):
            e.finish(ins[a0:a1], outs[o0:o1], sems[q0:q1])

    joined = _Exchange(sum((e.args for e in exchanges), []), sum((e.out_shape for e in exchanges), []),
                       sum((e.sems for e in exchanges), []), start, finish)
    joined.cuts = [(c[2], c[3]) for c in cuts]
    return joined


def _ffn_proj(x, w_in, tm, name, bg=None):
    t = x.shape[0]
    nch = D_FF // FFN_COLS

    def body(x_ref, win_ref, xbt_ref, h_ref, a_ref):
        xb = x_ref[...].astype(BF16)
        xbt_ref[...] = xb.T
        for k in range(nch):
            cg = slice(k * FFN_COLS, (k + 1) * FFN_COLS)
            cu = slice(D_FF + k * FFN_COLS, D_FF + (k + 1) * FFN_COLS)
            hg = _dot(xb, win_ref[k])
            hu = _dot(xb, win_ref[nch + k])
            h_ref[:, cg] = hg.astype(BF16)
            h_ref[:, cu] = hu.astype(BF16)
            a_ref[:, cg] = (hg * _sigmoid(hg) * hu).astype(BF16)

    return _call(
        body, name, (t // tm,),
        [_rows(tm, D_MODEL), _resident((2 * nch, D_MODEL, FFN_COLS))],
        (_cols(D_MODEL, tm), _rows(tm, 2 * D_FF), _rows(tm, D_FF)),
        (SDS((D_MODEL, t), BF16), SDS((t, 2 * D_FF), BF16), SDS((t, D_FF), BF16)),
        (x, w_in), sem=("parallel",), bg=bg)


def _ffn_out(x, a, w_out, g, b, tm, name, bg=None):
    t = x.shape[0]

    def body(x_ref, a_ref, wout_ref, g_ref, b_ref, xn_ref, xh_ref, rstd_ref):
        f = _dot(a_ref[...], wout_ref[...])
        y, xh, rstd = _ln_fwd(ALPHA * x_ref[...] + 0.5 * f, g_ref[...], b_ref[...])
        xn_ref[...] = y
        xh_ref[...] = xh
        rstd_ref[...] = rstd

    return _call(
        body, name, (t // tm,),
        [_rows(tm, D_MODEL), _rows(tm, D_FF), _resident((D_FF, D_MODEL)), _fixed((1, D_MODEL)), _fixed((1, D_MODEL))],
        (_rows(tm, D_MODEL), _rows(tm, D_MODEL), _rows(tm, 1)),
        (SDS((t, D_MODEL), F32), SDS((t, D_MODEL), F32), SDS((t, 1), F32)),
        (x, a, w_out, g, b), sem=("parallel",), bg=bg)


def _ffn_bwd(dxn, xh, rstd, h, w_in, w_out, g, tm, name, bg=None):
    t = dxn.shape[0]
    nch = D_FF // FFN_COLS

    def body(dxn_ref, xh_ref, rstd_ref, h_ref, win_ref, wout_ref, g_ref,
             dx_ref, dh_ref, df_ref, dg_ref, db_ref):
        @pl.when(pl.program_id(0) == 0)
        def _():
            dg_ref[...] = jnp.zeros_like(dg_ref)
            db_ref[...] = jnp.zeros_like(db_ref)

        dy = dxn_ref[...]
        xhv = xh_ref[...]
        dr = _ln_bwd(dy, xhv, rstd_ref[...], g_ref[...])
        dg_ref[...] += jnp.sum(dy * xhv, axis=0, keepdims=True)
        db_ref[...] += jnp.sum(dy, axis=0, keepdims=True)
        df = (0.5 * dr).astype(BF16)
        df_ref[...] = df
        dx = ALPHA * dr
        for k in range(nch):
            cg = slice(k * FFN_COLS, (k + 1) * FFN_COLS)
            cu = slice(D_FF + k * FFN_COLS, D_FF + (k + 1) * FFN_COLS)
            hg = h_ref[:, cg].astype(F32)
            hu = h_ref[:, cu].astype(F32)
            sg = _sigmoid(hg)
            silu = hg * sg
            da = _dot_nt(df, wout_ref[cg, :])
            dhu = (da * silu).astype(BF16)
            dhg = (da * hu * (sg * (1.0 + hg * (1.0 - sg)))).astype(BF16)
            dh_ref[:, cg] = dhg
            dh_ref[:, cu] = dhu
            dx = dx + _dot_nt(dhg, win_ref[k]) + _dot_nt(dhu, win_ref[nch + k])
        dx_ref[...] = dx

    return _call(
        body, name, (t // tm,),
        [_rows(tm, D_MODEL), _rows(tm, D_MODEL), _rows(tm, 1), _rows(tm, 2 * D_FF),
         _resident((2 * nch, D_MODEL, FFN_COLS)), _resident((D_FF, D_MODEL)), _fixed((1, D_MODEL))],
        (_rows(tm, D_MODEL), _rows(tm, 2 * D_FF), _rows(tm, D_MODEL),
         _fixed((1, D_MODEL)), _fixed((1, D_MODEL))),
        (SDS((t, D_MODEL), F32), SDS((t, 2 * D_FF), BF16), SDS((t, D_MODEL), BF16),
         SDS((1, D_MODEL), F32), SDS((1, D_MODEL), F32)),
        (dxn, xh, rstd, h, w_in, w_out, g), sem=("arbitrary",), bg=bg)


def _tn_matmul(a, b, name, bm, bn, col_block=0, total_cols=None, prev=None, bg=None, a_cols=None, a_t=False):
    t, m = a.shape[::-1] if a_t else a.shape
    a_first = 0
    if a_cols is not None:
        a_first, m = a_cols[0], a_cols[1] * bm
    n = b.shape[1]
    total_cols = n if total_cols is None else total_cols
    bk = min(TN_K_BLOCK, t)
    nk = t // bk
    n_in = 2 if prev is None else 4

    def body(*refs):
        a_ref, b_ref = refs[0], refs[1]
        o_ref, ob_ref = refs[n_in], refs[n_in + 1]
        k = pl.program_id(2)

        @pl.when(k == 0)
        def _():
            o_ref[...] = jnp.zeros_like(o_ref)

        o_ref[...] += _dot(a_ref[...], b_ref[...]) if a_t else _dot_tn(a_ref[...], b_ref[...])

        @pl.when(k == nk - 1)
        def _():
            ob_ref[...] = o_ref[...].astype(BF16)

    a_spec = (pl.BlockSpec((bm, bk), lambda i, j, k: (i + a_first, k)) if a_t
              else pl.BlockSpec((bk, bm), lambda i, j, k: (k, i + a_first)))
    in_specs = [a_spec, pl.BlockSpec((bk, bn), lambda i, j, k: (k, j))]
    args = [a, b]
    aliases = {}
    if prev is not None:
        in_specs += [pl.BlockSpec(memory_space=pl.ANY), pl.BlockSpec(memory_space=pl.ANY)]
        args += list(prev)
        aliases = {2: 0, 3: 1}
    out_spec = pl.BlockSpec((bm, bn), lambda i, j, k: (i, j + col_block))
    return _call(body, name, (m // bm, n // bn, nk), in_specs, (out_spec, out_spec),
                 (SDS((m, total_cols), F32), SDS((m, total_cols), BF16)), args,
                 sem=("parallel", "parallel", "arbitrary"), bg=bg, aliases=aliases)


def _mixin_fwd(x1, w, tm, bg=None):
    t = x1.shape[0]

    def body(x_ref, w_ref, xbt_ref, za_ref, zuv_ref, gab_ref):
        xb = x_ref[...].astype(BF16)
        xbt_ref[...] = xb.T
        za_ref[...] = _dot(xb, w_ref[:, 0:512]).astype(BF16)
        zuv_ref[...] = _dot(xb, w_ref[:, 512:1536]).astype(BF16)
        gab_ref[...] = _dot(xb, w_ref[:, 1536:3584]).astype(BF16)

    return _call(
        body, "mixin_fwd", (t // tm,),
        [_rows(tm, D_MODEL), _resident((D_MODEL, 3584))],
        (_cols(D_MODEL, tm), _rows(tm, 512), _rows(tm, 1024), _rows(tm, 2048)),
        (SDS((D_MODEL, t), BF16), SDS((t, 512), BF16), SDS((t, 1024), BF16), SDS((t, 2048), BF16)),
        (x1, w), sem=("parallel",), bg=bg)


def _mixin_bwd(dx1a, dza, dzuv, dgab, w, tm, bg=None):
    t = dx1a.shape[0]

    def body(d_ref, dza_ref, dzuv_ref, dgab_ref, w_ref, dx_ref):
        dx_ref[...] = (d_ref[...] + _dot_nt(dza_ref[...], w_ref[:, 0:512])
                       + _dot_nt(dzuv_ref[...], w_ref[:, 512:1536])
                       + _dot_nt(dgab_ref[...], w_ref[:, 1536:3584]))

    return _call(
        body, "mixin_bwd", (t // tm,),
        [_rows(tm, D_MODEL), _rows(tm, 512), _rows(tm, 1024), _rows(tm, 2048), _resident((D_MODEL, 3584))],
        (_rows(tm, D_MODEL),), (SDS((t, D_MODEL), F32),),
        (dx1a, dza, dzuv, dgab, w), sem=("parallel",), bg=bg)


def _unrolled(lo, hi, body, carry):
    for j in range(lo, hi):
        carry = body(j, carry)
    return carry


def _scan_fwd(hr_ref, hi_ref, a_ref, ap_ref, carry_ref, seg, cin_ref):
    for lc in range(SSM_LANES // SCAN_LANES):
        ls = slice(lc * SCAN_LANES, (lc + 1) * SCAN_LANES)
        a_r = jnp.broadcast_to(a_ref[0:1, ls], (8, SCAN_LANES))
        a_i = jnp.broadcast_to(a_ref[1:2, ls], (8, SCAN_LANES))

        def step(j, hc, ls=ls, a_r=a_r, a_i=a_i):
            h_r, h_i = hc
            rows = pl.ds(j * 8, 8)
            n_r = a_r * h_r - a_i * h_i + hr_ref[rows, ls]
            n_i = a_r * h_i + a_i * h_r + hi_ref[rows, ls]
            hr_ref[rows, ls] = n_r
            hi_ref[rows, ls] = n_i
            return n_r, n_i

        zero = jnp.zeros((8, SCAN_LANES), F32)
        f_r, f_i = _unrolled(0, seg, step, (zero, zero))
        c_r = carry_ref[0:1, ls]
        c_i = carry_ref[1:2, ls]
        p_r = ap_ref[0:1, ls]
        p_i = ap_ref[1:2, ls]
        rows_r, rows_i = [], []
        for s in range(8):
            rows_r.append(c_r)
            rows_i.append(c_i)
            c_r, c_i = (f_r[s:s + 1] + p_r * c_r - p_i * c_i,
                        f_i[s:s + 1] + p_r * c_i + p_i * c_r)
        carry_ref[0:1, ls] = c_r
        carry_ref[1:2, ls] = c_i
        cin_r = jnp.concatenate(rows_r, axis=0)
        cin_i = jnp.concatenate(rows_i, axis=0)
        if cin_ref is not None:
            cin_ref[0, :, ls] = cin_r
            cin_ref[1, :, ls] = cin_i

        def fix(j, cc, ls=ls, a_r=a_r, a_i=a_i):
            c_r, c_i = cc
            c_r, c_i = a_r * c_r - a_i * c_i, a_r * c_i + a_i * c_r
            rows = pl.ds(j * 8, 8)
            hr_ref[rows, ls] = hr_ref[rows, ls] + c_r
            hi_ref[rows, ls] = hi_ref[rows, ls] + c_i
            return c_r, c_i

        _unrolled(0, seg, fix, (cin_r, cin_i))


def _scan_bwd(gr_ref, gi_ref, hr_ref, hi_ref, cin_ref, a_ref, ap_ref, rcarry_ref, da_ref, seg):
    for lc in range(SSM_LANES // SCAN_LANES):
        ls = slice(lc * SCAN_LANES, (lc + 1) * SCAN_LANES)
        a_r = jnp.broadcast_to(a_ref[0:1, ls], (8, SCAN_LANES))
        a_i = jnp.broadcast_to(a_ref[1:2, ls], (8, SCAN_LANES))

        def step(t, gc, ls=ls, a_r=a_r, a_i=a_i):
            g_r, g_i = gc
            rows = pl.ds((seg - 1 - t) * 8, 8)
            n_r = gr_ref[rows, ls] + a_r * g_r + a_i * g_i
            n_i = gi_ref[rows, ls] + a_r * g_i - a_i * g_r
            gr_ref[rows, ls] = n_r
            gi_ref[rows, ls] = n_i
            return n_r, n_i

        zero = jnp.zeros((8, SCAN_LANES), F32)
        f_r, f_i = _unrolled(0, seg, step, (zero, zero))
        c_r = rcarry_ref[0:1, ls]
        c_i = rcarry_ref[1:2, ls]
        p_r = ap_ref[0:1, ls]
        p_i = ap_ref[1:2, ls]
        rows_r, rows_i = [None] * 8, [None] * 8
        for s in range(7, -1, -1):
            rows_r[s] = c_r
            rows_i[s] = c_i
            c_r, c_i = (f_r[s:s + 1] + p_r * c_r + p_i * c_i,
                        f_i[s:s + 1] + p_r * c_i - p_i * c_r)
        rcarry_ref[0:1, ls] = c_r
        rcarry_ref[1:2, ls] = c_i
        cin_r = jnp.concatenate(rows_r, axis=0)
        cin_i = jnp.concatenate(rows_i, axis=0)

        def fix_row(j_rows, hp_r, hp_i, cc, ls=ls, a_r=a_r, a_i=a_i):
            c_r, c_i, acc_r, acc_i = cc
            c_r, c_i = a_r * c_r + a_i * c_i, a_r * c_i - a_i * c_r
            g_r = gr_ref[j_rows, ls] + c_r
            g_i = gi_ref[j_rows, ls] + c_i
            gr_ref[j_rows, ls] = g_r
            gi_ref[j_rows, ls] = g_i
            acc_r = acc_r + g_r * hp_r + g_i * hp_i
            acc_i = acc_i + g_i * hp_r - g_r * hp_i
            return c_r, c_i, acc_r, acc_i

        def fix(t, cc, ls=ls, fix_row=fix_row):
            j = seg - 1 - t
            rows = pl.ds(j * 8, 8)
            prev = pl.ds((j - 1) * 8, 8)
            return fix_row(rows, hr_ref[prev, ls], hi_ref[prev, ls], cc)

        cc = _unrolled(0, seg - 1, fix, (cin_r, cin_i, zero, zero))
        _, _, acc_r, acc_i = fix_row(pl.ds(0, 8), cin_ref[0, :, ls], cin_ref[1, :, ls], cc)
        da_ref[0, :, ls] += acc_r
        da_ref[1, :, ls] += acc_i


def _s5_fwd(za, sp, bsz, seq, tb, bg=None):
    nb = seq // tb
    seg = tb // 8
    t = bsz * seq

    def body(za_ref, perm_ref, permt_ref, mre_ref, mim_ref, nre_ref, nim_ref, a_ref, ap_ref,
             dsk_ref, gw_ref, gb_ref, out_ref, outt_ref, y2_ref, car_ref, hr_ref, hi_ref, carry_ref):
        @pl.when(pl.program_id(1) == 0)
        def _():
            carry_ref[...] = jnp.zeros_like(carry_ref)

        car_ref[0] = carry_ref[...]
        up = _dot(perm_ref[...], za_ref[...])
        upb = up.astype(BF16)
        for bb in range(S5_BLOCKS):
            ub = upb[:, bb * S5_BLOCK_IN:(bb + 1) * S5_BLOCK_IN]
            st = slice(bb * S5_BLOCK_ST, (bb + 1) * S5_BLOCK_ST)
            hr_ref[:, st] = _dot(ub, mre_ref[bb])
            hi_ref[:, st] = _dot(ub, mim_ref[bb])
        _scan_fwd(hr_ref, hi_ref, a_ref, ap_ref, carry_ref, seg, None)
        ys = []
        for bb in range(S5_BLOCKS):
            st = slice(bb * S5_BLOCK_ST, (bb + 1) * S5_BLOCK_ST)
            ys.append(_dot(hr_ref[:, st].astype(BF16), nre_ref[bb])
                      - _dot(hi_ref[:, st].astype(BF16), nim_ref[bb]))
        y2 = jnp.concatenate(ys, axis=1) + dsk_ref[...] * up
        y2_ref[...] = y2
        y3 = _gelu(y2)
        gl = _dot(y3.astype(BF16), gw_ref[...]) + gb_ref[...]
        oa = y3 * _sigmoid(gl)
        out = _dot(permt_ref[...], oa.astype(BF16)).astype(BF16)
        out_ref[...] = out
        outt_ref[...] = out.T

    blk = pl.BlockSpec((tb, D_SSM), lambda b, j: (b * nb + j, 0))
    blk_t = pl.BlockSpec((D_SSM, tb), lambda b, j: (0, b * nb + j))
    m_shape = (S5_BLOCKS, S5_BLOCK_IN, S5_BLOCK_ST)
    n_shape = (S5_BLOCKS, S5_BLOCK_ST, S5_BLOCK_IN)
    return _call(
        body, "s5_fwd", (bsz, nb),
        [blk, _fixed((tb, tb)), _fixed((tb, tb)), _fixed(m_shape), _fixed(m_shape), _fixed(n_shape),
         _fixed(n_shape), _fixed((2, SSM_LANES)), _fixed((2, SSM_LANES)), _fixed((1, D_SSM)),
         _fixed((D_SSM, D_SSM)), _fixed((1, D_SSM))],
        (blk, blk_t, blk, pl.BlockSpec((1, 2, SSM_LANES), lambda b, j: (b * nb + j, 0, 0))),
        (SDS((t, D_SSM), BF16), SDS((D_SSM, t), BF16), SDS((t, D_SSM), F32), SDS((bsz * nb, 2, SSM_LANES), F32)),
        (za, sp["perm"], sp["permt"], sp["mre"], sp["mim"], sp["nre"], sp["nim"], sp["a"], sp["ap"],
         sp["dskip"], sp["glu_w"], sp["glu_b"]),
        scratch=[pltpu.VMEM((tb, SSM_LANES), F32), pltpu.VMEM((tb, SSM_LANES), F32),
                 pltpu.VMEM((2, SSM_LANES), F32)],
        sem=("arbitrary", "arbitrary"), bg=bg)


def _s5_bwd(za, y2p, doa, carries, sp, bsz, seq, tb, bg=None):
    nb = seq // tb
    seg = tb // 8
    t = bsz * seq

    def body(za_ref, y2_ref, doa_ref, car_ref, perm_ref, permt_ref, mre_ref, mim_ref, mtre_ref, mtim_ref,
             nre_ref, nim_ref, ntre_ref, ntim_ref, a_ref, ap_ref, dsk_ref, gw_ref, gwt_ref, gb_ref,
             dza_ref, dmr_ref, dmi_ref, dnr_ref, dni_ref, da_ref, ddsk_ref, dgw_ref, dgb_ref,
             hr_ref, hi_ref, gr_ref, gi_ref, cin_ref, carry_ref, rcarry_ref):
        first = jnp.logical_and(pl.program_id(0) == 0, pl.program_id(1) == 0)

        @pl.when(first)
        def _():
            for r in (dmr_ref, dmi_ref, dnr_ref, dni_ref, da_ref, ddsk_ref, dgw_ref, dgb_ref):
                r[...] = jnp.zeros_like(r)

        @pl.when(pl.program_id(1) == 0)
        def _():
            rcarry_ref[...] = jnp.zeros_like(rcarry_ref)

        carry_ref[...] = car_ref[0]
        perm = perm_ref[...]
        up = _dot(perm, za_ref[...])
        upb = up.astype(BF16)
        for bb in range(S5_BLOCKS):
            ub = upb[:, bb * S5_BLOCK_IN:(bb + 1) * S5_BLOCK_IN]
            st = slice(bb * S5_BLOCK_ST, (bb + 1) * S5_BLOCK_ST)
            hr_ref[:, st] = _dot(ub, mre_ref[bb])
            hi_ref[:, st] = _dot(ub, mim_ref[bb])
        _scan_fwd(hr_ref, hi_ref, a_ref, ap_ref, carry_ref, seg, cin_ref)

        y2 = y2_ref[...]
        y3 = _gelu(y2)
        y3b = y3.astype(BF16)
        sg = _sigmoid(_dot(y3b, gw_ref[...]) + gb_ref[...])
        d0 = doa_ref[...]
        d_hi = d0.astype(BF16)
        d1 = d0 - d_hi.astype(F32)
        d_mid = d1.astype(BF16)
        d_lo = (d1 - d_mid.astype(F32)).astype(BF16)
        doap = _dot(perm, d_hi) + _dot(perm, d_mid) + _dot(perm, d_lo)
        dgl = doap * y3 * sg * (1.0 - sg)
        dglb = dgl.astype(BF16)
        dy3 = doap * sg + _dot(dglb, gwt_ref[...])
        dgw_ref[...] += _dot_tn(y3b, dglb)
        dgb_ref[...] += jnp.sum(dgl, axis=0, keepdims=True)
        dy2 = dy3 * _gelu_grad(y2)
        ddsk_ref[...] += jnp.sum(dy2 * up, axis=0, keepdims=True)
        dyb = dy2.astype(BF16)
        for bb in range(S5_BLOCKS):
            dyc = dyb[:, bb * S5_BLOCK_IN:(bb + 1) * S5_BLOCK_IN]
            st = slice(bb * S5_BLOCK_ST, (bb + 1) * S5_BLOCK_ST)
            gr_ref[:, st] = _dot(dyc, ntre_ref[bb])
            gi_ref[:, st] = -_dot(dyc, ntim_ref[bb])
            dnr_ref[bb] += _dot_tn(hr_ref[:, st].astype(BF16), dyc)
            dni_ref[bb] += -_dot_tn(hi_ref[:, st].astype(BF16), dyc)
        _scan_bwd(gr_ref, gi_ref, hr_ref, hi_ref, cin_ref, a_ref, ap_ref, rcarry_ref, da_ref, seg)
        dus = []
        for bb in range(S5_BLOCKS):
            st = slice(bb * S5_BLOCK_ST, (bb + 1) * S5_BLOCK_ST)
            grb = gr_ref[:, st].astype(BF16)
            gib = gi_ref[:, st].astype(BF16)
            dus.append(_dot(grb, mtre_ref[bb]) + _dot(gib, mtim_ref[bb]))
            ub = upb[:, bb * S5_BLOCK_IN:(bb + 1) * S5_BLOCK_IN]
            dmr_ref[bb] += _dot_tn(ub, grb)
            dmi_ref[bb] += _dot_tn(ub, gib)
        du = jnp.concatenate(dus, axis=1) + dy2 * dsk_ref[...]
        dza_ref[...] = _dot(permt_ref[...], du.astype(BF16)).astype(BF16)

    def rev(b, j):
        return (b * nb + (nb - 1 - j), 0)

    blk = pl.BlockSpec((tb, D_SSM), rev)
    m_shape = (S5_BLOCKS, S5_BLOCK_IN, S5_BLOCK_ST)
    n_shape = (S5_BLOCKS, S5_BLOCK_ST, S5_BLOCK_IN)
    return _call(
        body, "s5_bwd", (bsz, nb),
        [blk, blk, blk, pl.BlockSpec((1, 2, SSM_LANES), lambda b, j: (b * nb + (nb - 1 - j), 0, 0)),
         _fixed((tb, tb)), _fixed((tb, tb)), _fixed(m_shape), _fixed(m_shape), _fixed(n_shape), _fixed(n_shape),
         _fixed(n_shape), _fixed(n_shape), _fixed(m_shape), _fixed(m_shape),
         _fixed((2, SSM_LANES)), _fixed((2, SSM_LANES)), _fixed((1, D_SSM)),
         _fixed((D_SSM, D_SSM)), _fixed((D_SSM, D_SSM)), _fixed((1, D_SSM))],
        (blk, _fixed(m_shape), _fixed(m_shape), _fixed(n_shape), _fixed(n_shape),
         _fixed((2, 8, SSM_LANES)), _fixed((1, D_SSM)), _fixed((D_SSM, D_SSM)), _fixed((1, D_SSM))),
        (SDS((t, D_SSM), BF16), SDS(m_shape, F32), SDS(m_shape, F32), SDS(n_shape, F32), SDS(n_shape, F32),
         SDS((2, 8, SSM_LANES), F32), SDS((1, D_SSM), F32), SDS((D_SSM, D_SSM), F32), SDS((1, D_SSM), F32)),
        (za, y2p, doa, carries, sp["perm"], sp["permt"], sp["mre"], sp["mim"], sp["mtre"], sp["mtim"],
         sp["nre"], sp["nim"], sp["ntre"], sp["ntim"], sp["a"], sp["ap"], sp["dskip"], sp["glu_w"],
         sp["glu_wt"], sp["glu_b"]),
        scratch=[pltpu.VMEM((tb, SSM_LANES), F32), pltpu.VMEM((tb, SSM_LANES), F32),
                 pltpu.VMEM((tb, SSM_LANES), F32), pltpu.VMEM((tb, SSM_LANES), F32),
                 pltpu.VMEM((2, 8, SSM_LANES), F32), pltpu.VMEM((2, SSM_LANES), F32),
                 pltpu.VMEM((2, SSM_LANES), F32)],
        sem=("arbitrary", "arbitrary"), bg=bg)


def _gmlp_spatial(ws_ref, vb):
    lane = lax.broadcasted_iota(jnp.int32, (CHUNK, 128), 1)
    parts = []
    for j in range(GMLP_HEADS // 2):
        vp = vb[:, 128 * j:128 * (j + 1)]
        parts.append(jnp.where(lane < GMLP_HEAD_DIM, _dot(ws_ref[2 * j], vp), _dot(ws_ref[2 * j + 1], vp)))
    return jnp.concatenate(parts, axis=1)


def _gmlp_fwd(zuv, ln_g, ln_b, wsm, bias, bg=None):
    t = zuv.shape[0]

    def body(z_ref, g_ref, b_ref, ws_ref, bias_ref, out_ref, outt_ref):
        u = _gelu(z_ref[:, 0:D_GMLP].astype(F32))
        v0 = _gelu(z_ref[:, D_GMLP:2 * D_GMLP].astype(F32))
        v, _, _ = _ln_fwd(v0, g_ref[...], b_ref[...])
        s = _gmlp_spatial(ws_ref, v.astype(BF16)) + bias_ref[...]
        out = (u * s).astype(BF16)
        out_ref[...] = out
        outt_ref[...] = out.T

    return _call(
        body, "gmlp_fwd", (t // CHUNK,),
        [_rows(CHUNK, 2 * D_GMLP), _fixed((1, D_GMLP)), _fixed((1, D_GMLP)),
         _fixed((GMLP_HEADS, CHUNK, CHUNK)), _fixed((CHUNK, D_GMLP))],
        (_rows(CHUNK, D_GMLP), _cols(D_GMLP, CHUNK)), (SDS((t, D_GMLP), BF16), SDS((D_GMLP, t), BF16)),
        (zuv, ln_g, ln_b, wsm, bias), sem=("parallel",), bg=bg)


def _gmlp_bwd(zuv, dgm, ln_g, ln_b, wsm, wsmt, bias, bg=None):
    t = zuv.shape[0]

    def body(z_ref, d_ref, g_ref, b_ref, ws_ref, wst_ref, bias_ref,
             dz_ref, dws_ref, dbias_ref, dg_ref, db_ref):
        @pl.when(pl.program_id(0) == 0)
        def _():
            for r in (dws_ref, dbias_ref, dg_ref, db_ref):
                r[...] = jnp.zeros_like(r)

        zu = z_ref[:, 0:D_GMLP].astype(F32)
        zv = z_ref[:, D_GMLP:2 * D_GMLP].astype(F32)
        u = _gelu(zu)
        v0 = _gelu(zv)
        gam = g_ref[...]
        v, vhat, rstd = _ln_fwd(v0, gam, b_ref[...])
        vb = v.astype(BF16)
        s = _gmlp_spatial(ws_ref, vb) + bias_ref[...]
        d = d_ref[...]
        dz_ref[:, 0:D_GMLP] = (d * s * _gelu_grad(zu)).astype(BF16)
        ds = d * u
        dbias_ref[...] += ds
        dsb = ds.astype(BF16)
        lane = lax.broadcasted_iota(jnp.int32, (CHUNK, 128), 1)
        tril = (lax.broadcasted_iota(jnp.int32, (CHUNK, CHUNK), 0)
                >= lax.broadcasted_iota(jnp.int32, (CHUNK, CHUNK), 1))
        zero_b = jnp.zeros((CHUNK, 128), BF16)
        parts = []
        for j in range(GMLP_HEADS // 2):
            dsp = dsb[:, 128 * j:128 * (j + 1)]
            vp = vb[:, 128 * j:128 * (j + 1)]
            parts.append(jnp.where(lane < GMLP_HEAD_DIM, _dot(wst_ref[2 * j], dsp),
                                   _dot(wst_ref[2 * j + 1], dsp)))
            lo = jnp.where(lane < GMLP_HEAD_DIM, dsp, zero_b)
            hi = jnp.where(lane < GMLP_HEAD_DIM, zero_b, dsp)
            dws_ref[2 * j] += jnp.where(tril, _dot_nt(lo, vp), 0.0)
            dws_ref[2 * j + 1] += jnp.where(tril, _dot_nt(hi, vp), 0.0)
        dv = jnp.concatenate(parts, axis=1)
        dg_ref[...] += jnp.sum(dv * vhat, axis=0, keepdims=True)
        db_ref[...] += jnp.sum(dv, axis=0, keepdims=True)
        dz_ref[:, D_GMLP:2 * D_GMLP] = (_ln_bwd(dv, vhat, rstd, gam) * _gelu_grad(zv)).astype(BF16)

    return _call(
        body, "gmlp_bwd", (t // CHUNK,),
        [_rows(CHUNK, 2 * D_GMLP), _rows(CHUNK, D_GMLP), _fixed((1, D_GMLP)), _fixed((1, D_GMLP)),
         _fixed((GMLP_HEADS, CHUNK, CHUNK)), _fixed((GMLP_HEADS, CHUNK, CHUNK)), _fixed((CHUNK, D_GMLP))],
        (_rows(CHUNK, 2 * D_GMLP), _fixed((GMLP_HEADS, CHUNK, CHUNK)), _fixed((CHUNK, D_GMLP)),
         _fixed((1, D_GMLP)), _fixed((1, D_GMLP))),
        (SDS((t, 2 * D_GMLP), BF16), SDS((GMLP_HEADS, CHUNK, CHUNK), F32), SDS((CHUNK, D_GMLP), F32),
         SDS((1, D_GMLP), F32), SDS((1, D_GMLP), F32)),
        (zuv, dgm, ln_g, ln_b, wsm, wsmt, bias), sem=("arbitrary",), bg=bg)


def _mixout_fwd(x1, s5o, gm, gab, ua, ub, wmo, g, b, tm, bg=None):
    t = x1.shape[0]

    def body(x_ref, s_ref, m_ref, gab_ref, ua_ref, ub_ref, wmo_ref, g_ref, b_ref,
             xn_ref, xh_ref, rstd_ref):
        ya = _dot(s_ref[...], ua_ref[...])
        yb = _dot(m_ref[...], ub_ref[...])
        mix = (_sigmoid(gab_ref[:, 0:D_MODEL].astype(F32)) * ya
               + _sigmoid(gab_ref[:, D_MODEL:2 * D_MODEL].astype(F32)) * yb)
        r = ALPHA * x_ref[...] + _dot(mix.astype(BF16), wmo_ref[...])
        y, xh, rstd = _ln_fwd(r, g_ref[...], b_ref[...])
        xn_ref[...] = y
        xh_ref[...] = xh
        rstd_ref[...] = rstd

    return _call(
        body, "mixout_fwd", (t // tm,),
        [_rows(tm, D_MODEL), _rows(tm, D_SSM), _rows(tm, D_GMLP), _rows(tm, 2 * D_MODEL),
         _resident((D_SSM, D_MODEL)), _resident((D_GMLP, D_MODEL)), _resident((D_MODEL, D_MODEL)),
         _fixed((1, D_MODEL)), _fixed((1, D_MODEL))],
        (_rows(tm, D_MODEL), _rows(tm, D_MODEL), _rows(tm, 1)),
        (SDS((t, D_MODEL), F32), SDS((t, D_MODEL), F32), SDS((t, 1), F32)),
        (x1, s5o, gm, gab, ua, ub, wmo, g, b), sem=("parallel",), bg=bg)


def _mixout_bwd(dx2, xh, rstd, s5o, gm, gab, ua, ub, wmo, g, tm, bg=None):
    t = dx2.shape[0]

    def body(d_ref, xh_ref, rstd_ref, s_ref, m_ref, gab_ref, ua_ref, ub_ref, wmo_ref, g_ref,
             dx1_ref, dmx_ref, mb_ref, dya_ref, dyb_ref, ds5_ref, dgm_ref, dgab_ref, dg_ref, db_ref):
        @pl.when(pl.program_id(0) == 0)
        def _():
            dg_ref[...] = jnp.zeros_like(dg_ref)
            db_ref[...] = jnp.zeros_like(db_ref)

        dy = d_ref[...]
        xhv = xh_ref[...]
        dr = _ln_bwd(dy, xhv, rstd_ref[...], g_ref[...])
        dg_ref[...] += jnp.sum(dy * xhv, axis=0, keepdims=True)
        db_ref[...] += jnp.sum(dy, axis=0, keepdims=True)
        dx1_ref[...] = ALPHA * dr
        drb = dr.astype(BF16)
        dmx_ref[...] = drb
        dm = _dot_nt(drb, wmo_ref[...])
        ya = _dot(s_ref[...], ua_ref[...])
        yb = _dot(m_ref[...], ub_ref[...])
        sa = _sigmoid(gab_ref[:, 0:D_MODEL].astype(F32))
        sb = _sigmoid(gab_ref[:, D_MODEL:2 * D_MODEL].astype(F32))
        mb_ref[...] = (sa * ya + sb * yb).astype(BF16).T
        dya = (dm * sa).astype(BF16)
        dyb = (dm * sb).astype(BF16)
        dya_ref[...] = dya
        dyb_ref[...] = dyb
        dgab_ref[:, 0:D_MODEL] = (dm * ya * sa * (1.0 - sa)).astype(BF16)
        dgab_ref[:, D_MODEL:2 * D_MODEL] = (dm * yb * sb * (1.0 - sb)).astype(BF16)
        ds5_ref[...] = _dot_nt(dya, ua_ref[...])
        dgm_ref[...] = _dot_nt(dyb, ub_ref[...])

    return _call(
        body, "mixout_bwd", (t // tm,),
        [_rows(tm, D_MODEL), _rows(tm, D_MODEL), _rows(tm, 1), _rows(tm, D_SSM), _rows(tm, D_GMLP),
         _rows(tm, 2 * D_MODEL), _resident((D_SSM, D_MODEL)), _resident((D_GMLP, D_MODEL)),
         _resident((D_MODEL, D_MODEL)), _fixed((1, D_MODEL))],
        (_rows(tm, D_MODEL), _rows(tm, D_MODEL), _cols(D_MODEL, tm), _rows(tm, D_MODEL),
         _rows(tm, D_MODEL), _rows(tm, D_SSM), _rows(tm, D_GMLP), _rows(tm, 2 * D_MODEL),
         _fixed((1, D_MODEL)), _fixed((1, D_MODEL))),
        (SDS((t, D_MODEL), F32), SDS((t, D_MODEL), BF16), SDS((D_MODEL, t), BF16),
         SDS((t, D_MODEL), BF16), SDS((t, D_MODEL), BF16), SDS((t, D_SSM), F32),
         SDS((t, D_GMLP), F32), SDS((t, 2 * D_MODEL), BF16),
         SDS((1, D_MODEL), F32), SDS((1, D_MODEL), F32)),
        (dx2, xh, rstd, s5o, gm, gab, ua, ub, wmo, g), sem=("arbitrary",), bg=bg)


def _ple_loss(x3, p, tgt, wpg, wpp, tm, bg=None):
    t = x3.shape[0]

    def body(x_ref, p_ref, t_ref, wpg_ref, wpp_ref, dx_ref, xb_ref, pb_ref, dq_ref, de_ref, loss_ref):
        @pl.when(pl.program_id(0) == 0)
        def _():
            loss_ref[...] = jnp.zeros_like(loss_ref)

        x3v = x_ref[...]
        xb = x3v.astype(BF16)
        pb = p_ref[...].astype(BF16)
        xb_ref[...] = xb.T
        pb_ref[...] = pb.T
        s = _sigmoid(_dot(xb, wpg_ref[...]))
        e = _dot(pb, wpp_ref[...])
        diff = x3v + s * e - t_ref[...]
        loss_ref[...] += jnp.sum(diff * diff, axis=0, keepdims=True)
        dout = diff * (1.0 / D_MODEL)
        de_ref[...] = (dout * s).astype(BF16)
        dq = (dout * e * s * (1.0 - s)).astype(BF16)
        dq_ref[...] = dq
        dx_ref[...] = dout + _dot_nt(dq, wpg_ref[...])

    return _call(
        body, "ple_loss", (t // tm,),
        [_rows(tm, D_MODEL), _rows(tm, PLE_DIM), _rows(tm, D_MODEL),
         _resident((D_MODEL, D_MODEL)), _resident((PLE_DIM, D_MODEL))],
        (_rows(tm, D_MODEL), _cols(D_MODEL, tm), _cols(PLE_DIM, tm), _rows(tm, D_MODEL),
         _rows(tm, D_MODEL), _fixed((1, D_MODEL))),
        (SDS((t, D_MODEL), F32), SDS((D_MODEL, t), BF16), SDS((PLE_DIM, t), BF16),
         SDS((t, D_MODEL), BF16), SDS((t, D_MODEL), BF16), SDS((1, D_MODEL), F32)),
        (x3, p, tgt, wpg, wpp), sem=("arbitrary",), bg=bg)


def _s5_discretise(lre, lim, log_dt, bre, bim):
    dt = jnp.exp(log_dt)[:, None]
    mag = jnp.exp(lre * dt)
    abr = mag * jnp.cos(lim * dt)
    abi = mag * jnp.sin(lim * dt)
    nr = abr - 1.0
    ni = abi
    den = lre * lre + lim * lim
    cr = ((nr * lre + ni * lim) / den)[..., None]
    ci = ((ni * lre - nr * lim) / den)[..., None]
    return abr, abi, cr * bre - ci * bim, cr * bim + ci * bre


def _block_diag_in(bb):
    v = bb.reshape(S5_BLOCKS, 8, SSM_STATE, SSM_GROUP_CH).transpose(0, 1, 3, 2)
    return jnp.einsum("bgip,gh->bgihp", v, jnp.eye(8, dtype=bb.dtype)).reshape(
        S5_BLOCKS, S5_BLOCK_IN, S5_BLOCK_ST)


def _block_diag_in_t(dm):
    v = dm.reshape(S5_BLOCKS, 8, SSM_GROUP_CH, 8, SSM_STATE)
    d = jnp.einsum("bgihp,gh->bgip", v, jnp.eye(8, dtype=dm.dtype))
    return d.transpose(0, 1, 3, 2).reshape(SSM_GROUPS, SSM_STATE, SSM_GROUP_CH)


def _block_diag_out(cc):
    v = cc.reshape(S5_BLOCKS, 8, SSM_GROUP_CH, SSM_STATE)
    return jnp.einsum("bgip,gh->bgphi", v, jnp.eye(8, dtype=cc.dtype)).reshape(
        S5_BLOCKS, S5_BLOCK_ST, S5_BLOCK_IN)


def _block_diag_out_t(dn):
    v = dn.reshape(S5_BLOCKS, 8, SSM_STATE, 8, SSM_GROUP_CH)
    d = jnp.einsum("bgphi,gh->bgip", v, jnp.eye(8, dtype=dn.dtype))
    return d.reshape(SSM_GROUPS, SSM_GROUP_CH, SSM_STATE)


def _s5_setup(lre, lim, log_dt, bre, bim, cre, cim, d_skip, glu_w, glu_b, tb):
    seg = tb // 8
    abr, abi, bbr, bbi = _s5_discretise(lre, lim, log_dt, bre, bim)
    pr, pi = abr, abi
    for _ in range(int(math.log2(seg))):
        pr, pi = pr * pr - pi * pi, 2.0 * pr * pi
    rows = jnp.arange(tb)
    src = (rows % 8) * seg + rows // 8
    perm = (src[:, None] == jnp.arange(tb)[None, :]).astype(BF16)
    mre = _block_diag_in(bbr)
    mim = _block_diag_in(bbi)
    nre = _block_diag_out(cre)
    nim = _block_diag_out(cim)
    return {
        "perm": perm, "permt": perm.T,
        "mre": mre.astype(BF16), "mim": mim.astype(BF16),
        "mtre": mre.transpose(0, 2, 1).astype(BF16), "mtim": mim.transpose(0, 2, 1).astype(BF16),
        "nre": nre.astype(BF16), "nim": nim.astype(BF16),
        "ntre": nre.transpose(0, 2, 1).astype(BF16), "ntim": nim.transpose(0, 2, 1).astype(BF16),
        "a": jnp.stack([abr.reshape(-1), abi.reshape(-1)]),
        "ap": jnp.stack([pr.reshape(-1), pi.reshape(-1)]),
        "dskip": d_skip.reshape(1, D_SSM), "glu_w": glu_w, "glu_wt": glu_w.T,
        "glu_b": glu_b.reshape(1, D_SSM),
    }


BIG = ("ffn1_w_in", "ffn1_w_out", "mix_w_in", "ssm_glu_w", "up_a", "up_b", "mix_w_out",
       "ffn2_w_in", "ffn2_w_out", "ple_w_proj", "ple_w_gate")
BIG_AXIS = {"ffn1_w_in": 1, "ffn1_w_out": 0, "mix_w_in": 1, "ssm_glu_w": 0, "up_a": 1, "up_b": 1,
            "mix_w_out": 0, "ffn2_w_in": 1, "ffn2_w_out": 0, "ple_w_proj": 1, "ple_w_gate": 0}
SHARD_MAJOR = 2
GATHER_AXIS = dict(BIG_AXIS, ffn1_w_in=SHARD_MAJOR, ffn2_w_in=SHARD_MAJOR)
GATHER_ORDER = (("ffn1_w_in",), ("ffn1_w_out",), ("mix_w_in",), ("ssm_glu_w", "up_a", "up_b", "mix_w_out"),
                ("ffn2_w_in",), ("ffn2_w_out", "ple_w_gate", "ple_w_proj"))
GATHER_FIRST_ID = 1
REDUCE_FIRST_ID = 7
SMALL = ("ln1_g", "ln1_b", "ssm_lambda_re", "ssm_lambda_im", "ssm_log_dt", "ssm_b_re", "ssm_b_im",
         "ssm_c_re", "ssm_c_im", "ssm_d", "ssm_glu_b", "gmlp_ln_g", "gmlp_ln_b", "gmlp_w_s",
         "gmlp_b_s", "ln2_g", "ln2_b", "ln3_g", "ln3_b")
SMALL_VIEW = {"ssm_b_re": (SSM_GROUPS, SSM_STATE * SSM_GROUP_CH), "ssm_b_im": (SSM_GROUPS, SSM_STATE * SSM_GROUP_CH)}


def _small_view(k, a):
    return a.reshape(SMALL_VIEW[k]) if k in SMALL_VIEW else a


def _place():
    return lax.axis_index("x"), lax.axis_index("y"), lax.axis_index("c")


def _other_chips(x, y):
    return [(1 - x, y), (x, 1 - y), (1 - x, 1 - y)]


def _window(ref, shard_shape, axis, chip, half):
    r, c = shard_shape
    hr = r // 2
    if axis == SHARD_MAJOR:
        return ref.at[chip] if half is None else ref.at[chip, pl.ds(half * hr, hr), :]
    if axis == 0:
        if half is None:
            return ref.at[pl.ds(chip * r, r), :]
        return ref.at[pl.ds(chip * r + half * hr, hr), :]
    if half is None:
        return ref.at[:, pl.ds(chip * c, c)]
    return ref.at[pl.ds(half * hr, hr), pl.ds(chip * c, c)]


def _gather_weights(shards, axes):
    n = len(shards)
    shapes = [s.shape for s in shards]
    full = [{0: (4 * r, c), 1: (r, 4 * c), SHARD_MAJOR: (4, r, c)}[ax] for (r, c), ax in zip(shapes, axes)]

    def remote(sems, i, k, src, dst, to):
        return pltpu.make_async_remote_copy(src_ref=src, dst_ref=dst, send_sem=sems[0].at[6 * i + k],
                                            recv_sem=sems[1].at[6 * i + k], device_id=to, device_id_type=MESH)

    def own_copies(ins, outs, sems):
        x, y, c = _place()
        me = 2 * x + y
        cps = []
        for i in range(n):
            hr = shapes[i][0] // 2
            mine = ins[i].at[pl.ds(c * hr, hr), :]
            for j, (cx, cy) in enumerate(_other_chips(x, y)):
                cps.append(remote(sems, i, j, mine, _window(outs[i], shapes[i], axes[i], me, c), (cx, cy, c)))
        local = [pltpu.make_async_copy(ins[i], _window(outs[i], shapes[i], axes[i], me, None), sems[2].at[i])
                 for i in range(n)]
        return cps, local

    def start(ins, outs, sems):
        cps, local = own_copies(ins, outs, sems)
        for cp in local + cps:
            cp.start()

    def finish(ins, outs, sems):
        x, y, c = _place()
        sibling = (x, y, 1 - c)
        passed = []
        for j, (cx, cy) in enumerate(_other_chips(x, y)):
            for i in range(n):
                w = _window(outs[i], shapes[i], axes[i], 2 * cx + cy, c)
                remote(sems, i, j, w, w, (cx, cy, c)).wait_recv()
                cp = remote(sems, i, 3 + j, w, w, sibling)
                cp.start()
                passed.append(cp)
        for j, (cx, cy) in enumerate(_other_chips(x, y)):
            for i in range(n):
                w = _window(outs[i], shapes[i], axes[i], 2 * cx + cy, 1 - c)
                remote(sems, i, 3 + j, w, w, sibling).wait_recv()
        cps, local = own_copies(ins, outs, sems)
        for cp in cps + passed:
            cp.wait_send()
        for cp in local:
            cp.wait()

    return _Exchange(shards, [SDS(f, BF16) for f in full],
                     [pltpu.SemaphoreType.DMA((6 * n,)), pltpu.SemaphoreType.DMA((6 * n,)),
                      pltpu.SemaphoreType.DMA((n,))], start, finish)


def _scatter_grads(parts, shapes, axes):
    n = len(parts)

    def copies(ins, outs, sems):
        x, y, c = _place()
        return [pltpu.make_async_remote_copy(
            src_ref=_window(ins[i], shapes[i], axes[i], 2 * cx + cy, None), dst_ref=outs[i].at[j],
            send_sem=sems[0].at[3 * i + j], recv_sem=sems[1].at[3 * i + j],
            device_id=(cx, cy, c), device_id_type=MESH)
            for i in range(n) for j, (cx, cy) in enumerate(_other_chips(x, y))]

    def start(ins, outs, sems):
        for cp in copies(ins, outs, sems):
            cp.start()

    def finish(ins, outs, sems):
        for cp in copies(ins, outs, sems):
            cp.wait()

    return _Exchange(parts, [SDS((3,) + tuple(s), BF16) for s in shapes],
                     [pltpu.SemaphoreType.DMA((3 * n,)), pltpu.SemaphoreType.DMA((3 * n,))], start, finish)


def _swap_halves(parts, shapes, axes):
    n = len(parts)

    def copies(ins, outs, sems):
        x, y, c = _place()
        cps = []
        for i in range(n):
            r, _ = shapes[i]
            hr = r // 2
            if axes[i] == 0:
                cps += [pltpu.make_async_remote_copy(
                    src_ref=ins[i].at[pl.ds(k * r + (1 - c) * hr, hr), :], dst_ref=outs[i].at[k],
                    send_sem=sems[0].at[i], recv_sem=sems[1].at[i], device_id=(x, y, 1 - c),
                    device_id_type=MESH) for k in range(4)]
            else:
                cps.append(pltpu.make_async_remote_copy(
                    src_ref=ins[i].at[pl.ds((1 - c) * hr, hr), :], dst_ref=outs[i],
                    send_sem=sems[0].at[i], recv_sem=sems[1].at[i], device_id=(x, y, 1 - c),
                    device_id_type=MESH))
        return cps

    def start(ins, outs, sems):
        for cp in copies(ins, outs, sems):
            cp.start()

    def finish(ins, outs, sems):
        x, y, c = _place()
        for i in range(n):
            pltpu.make_async_remote_copy(src_ref=outs[i], dst_ref=outs[i], send_sem=sems[0].at[i],
                                         recv_sem=sems[1].at[i], device_id=(x, y, 1 - c),
                                         device_id_type=MESH).wait()

    out = [SDS((4, r // 2, c), BF16) if ax == 0 else SDS((r // 2, 4 * c), BF16)
           for (r, c), ax in zip(shapes, axes)]
    return _Exchange(parts, out, [pltpu.SemaphoreType.DMA((n,)), pltpu.SemaphoreType.DMA((n,))], start, finish)


def _scatter_halves(pres, shapes):
    n = len(pres)

    def copies(ins, outs, sems):
        x, y, c = _place()
        return [pltpu.make_async_remote_copy(
            src_ref=ins[i].at[1 + j], dst_ref=outs[i].at[j], send_sem=sems[0].at[3 * i + j],
            recv_sem=sems[1].at[3 * i + j], device_id=(cx, cy, c), device_id_type=MESH)
            for i in range(n) for j, (cx, cy) in enumerate(_other_chips(x, y))]

    def start(ins, outs, sems):
        for cp in copies(ins, outs, sems):
            cp.start()

    def finish(ins, outs, sems):
        for cp in copies(ins, outs, sems):
            cp.wait()

    return _Exchange(pres, [SDS((3, r // 2, c), BF16) for r, c in shapes],
                     [pltpu.SemaphoreType.DMA((3 * n,)), pltpu.SemaphoreType.DMA((3 * n,))], start, finish)


def _swap_with_sibling(arrs):
    n = len(arrs)

    def copies(ins, outs, sems):
        x, y, c = _place()
        return [pltpu.make_async_remote_copy(src_ref=ins[i], dst_ref=outs[i], send_sem=sems[0].at[i],
                                             recv_sem=sems[1].at[i], device_id=(x, y, 1 - c),
                                             device_id_type=MESH) for i in range(n)]

    def start(ins, outs, sems):
        for cp in copies(ins, outs, sems):
            cp.start()

    def finish(ins, outs, sems):
        for cp in copies(ins, outs, sems):
            cp.wait()

    return _Exchange(arrs, [SDS(a.shape, a.dtype) for a in arrs],
                     [pltpu.SemaphoreType.DMA((n,)), pltpu.SemaphoreType.DMA((n,))], start, finish)


def _gather_small(arrs):
    n = len(arrs)

    def copy(sems, outs, i, k, block, to, src=None):
        px, py, pc = block
        dst = outs[i].at[4 * px + 2 * py + pc]
        return pltpu.make_async_remote_copy(
            src_ref=dst if src is None else src, dst_ref=dst, send_sem=sems[0].at[7 * i + k],
            recv_sem=sems[1].at[7 * i + k], device_id=to, device_id_type=MESH)

    direct = [math.prod(a.shape) * 4 <= DIRECT_GATHER_BYTES for a in arrs]

    def own_copies(ins, outs, sems):
        x, y, c = _place()
        cps = []
        for i in range(n):
            cps.append(copy(sems, outs, i, 0, (x, y, c), (x, y, 1 - c), src=ins[i]))
            for j, (cx, cy) in enumerate(_other_chips(x, y)):
                cps.append(copy(sems, outs, i, 1 + j, (x, y, c), (cx, cy, c), src=ins[i]))
                if direct[i]:
                    cps.append(copy(sems, outs, i, 4 + j, (x, y, c), (cx, cy, 1 - c), src=ins[i]))
        local = [pltpu.make_async_copy(ins[i], outs[i].at[4 * x + 2 * y + c], sems[2].at[i]) for i in range(n)]
        return cps, local

    def start(ins, outs, sems):
        cps, local = own_copies(ins, outs, sems)
        for cp in local + cps:
            cp.start()

    def finish(ins, outs, sems):
        x, y, c = _place()
        passed = []
        for j, (cx, cy) in enumerate(_other_chips(x, y)):
            for i in range(n):
                copy(sems, outs, i, 1 + j, (cx, cy, c), (x, y, c)).wait_recv()
                if not direct[i]:
                    cp = copy(sems, outs, i, 4 + j, (cx, cy, c), (x, y, 1 - c))
                    cp.start()
                    passed.append(cp)
        for i in range(n):
            copy(sems, outs, i, 0, (x, y, 1 - c), (x, y, c)).wait_recv()
            for j, (cx, cy) in enumerate(_other_chips(x, y)):
                copy(sems, outs, i, 4 + j, (cx, cy, 1 - c), (x, y, c)).wait_recv()
        cps, local = own_copies(ins, outs, sems)
        for cp in cps + passed:
            cp.wait_send()
        for cp in local:
            cp.wait()

    return _Exchange(arrs, [SDS((N_DEV,) + a.shape, F32) for a in arrs],
                     [pltpu.SemaphoreType.DMA((7 * n,)), pltpu.SemaphoreType.DMA((7 * n,)),
                      pltpu.SemaphoreType.DMA((n,))], start, finish)


def _local_step(x, p, tgt, wb, ws, shards=None):
    bsz, seq, _ = x.shape
    t = bsz * seq
    tm = min(256, t)
    tb = min(256, seq)
    x0 = x.reshape(t, D_MODEL)
    p0 = p.reshape(t, PLE_DIM)
    tg = tgt.reshape(t, D_MODEL)
    row = lambda v: v.reshape(1, -1)
    dist = shards is not None
    wb = dict(wb)
    recv, sums, other, gathered = {}, {}, {}, {}
    gb = {}
    gs = {}
    shape_of, axis_of = {}, {}
    chip = None
    if dist:
        shape_of = {k: tuple(shards[k].shape) for k in BIG}
        axis_of = dict(BIG_AXIS)
        for q in range(LAST_PIECES):
            shape_of[LAST_PIECE % q] = (D_MODEL // LAST_PIECES, shape_of["ffn1_w_in"][1])
            axis_of[LAST_PIECE % q] = 1
        xi, yi, ci = _place()
        chip = (2 * xi + yi).astype(jnp.int32).reshape(1)
        ids = jnp.stack([2 * xi + yi] + [2 * cx + cy for cx, cy in _other_chips(xi, yi)] + [ci]).astype(jnp.int32)
    halfbuf, pre = {}, {}

    def gather(names):
        return _gather_weights([shards[k] for k in names], [GATHER_AXIS[k] for k in names]) if dist else None

    def exchange(scat=(), swap=(), halves=(), scat2=(), swap2=(), extra=None):
        if not dist:
            return None, []
        parts, tags = [], []
        if scat:
            parts.append(_scatter_grads([gb[k][1] for k in scat], [shape_of[k] for k in scat],
                                        [axis_of[k] for k in scat]))
            tags.append((recv, scat))
        if swap:
            for k in swap:
                sums[k] = _sum_blocks(gb[k][0], recv[k], shape_of[k], axis_of[k], chip, "sum_" + k)
            parts.append(_swap_with_sibling([sums[k] for k in swap]))
            tags.append((other, swap))
        if halves:
            parts.append(_swap_halves([gb[k][1] for k in halves], [shape_of[k] for k in halves],
                                      [axis_of[k] for k in halves]))
            tags.append((halfbuf, halves))
        if scat2:
            for k in scat2:
                pre[k] = _presum(gb[k][0], halfbuf[k], shape_of[k], axis_of[k], ids, "presum_" + k)
            parts.append(_scatter_halves([pre[k][1] for k in scat2], [shape_of[k] for k in scat2]))
            tags.append((recv, scat2))
        if swap2:
            for k in swap2:
                sums[k] = _sum_half(pre[k][0], recv[k], "sum_" + k)
            parts.append(_swap_with_sibling([sums[k] for k in swap2]))
            tags.append((other, swap2))
        if extra is not None:
            parts.append(extra[0])
            tags.append((extra[1], extra[2]))
        return (_join(parts), tags) if parts else (None, [])

    def take(ex_tags, got):
        ex, tags = ex_tags
        if ex is not None:
            for (dst, names), (o0, o1) in zip(tags, ex.cuts):
                dst.update(zip(names, got[o0:o1]))

    launched = []

    def launch(ex_tags):
        if ex_tags[0] is not None:
            n = len(launched)
            launched.append(n)
            take(ex_tags, _run_exchange_on_sequencer(ex_tags[0], "reduce_%d" % n, REDUCE_FIRST_ID + n))

    small_shape = {k: _small_view(k, v).shape for k, v in ws.items()}
    small_shape["loss_rows"] = (1, D_MODEL)
    ws = {k: v if (v.ndim == 2 and k != "ssm_log_dt") else v[0] for k, v in ws.items()}
    tril = jnp.tril(jnp.ones((CHUNK, CHUNK), dtype=bool))
    wsm = jnp.where(tril[None], ws["gmlp_w_s"], 0.0)
    wsm_b = wsm.astype(BF16)
    wsmt_b = wsm.transpose(0, 2, 1).astype(BF16)
    bias = jnp.repeat(ws["gmlp_b_s"].T, GMLP_HEAD_DIM, axis=1)

    tf = min(512, t)
    if dist:
        for gi, names in enumerate(GATHER_ORDER):
            wb.update(zip(names, _run_exchange_on_sequencer(gather(names), "gather_%d" % gi, GATHER_FIRST_ID + gi)))
    (x0b, h1, a1), _ = _ffn_proj(x0, wb["ffn1_w_in"], tf, "ffn1_proj")
    (x1, xh1, rstd1), _ = _ffn_out(x0, a1, wb["ffn1_w_out"], row(ws["ln1_g"]), row(ws["ln1_b"]), tf, "ffn1_out")
    sp = _s5_setup(ws["ssm_lambda_re"], ws["ssm_lambda_im"], ws["ssm_log_dt"], ws["ssm_b_re"],
                   ws["ssm_b_im"], ws["ssm_c_re"], ws["ssm_c_im"], ws["ssm_d"], wb["ssm_glu_w"],
                   ws["ssm_glu_b"], tb)
    (x1b, za, zuv, gab), _ = _mixin_fwd(x1, wb["mix_w_in"], tm)
    (s5o, s5ot, y2p, carries), _ = _s5_fwd(za, sp, bsz, seq, tb)
    (gm, gmt), _ = _gmlp_fwd(zuv, row(ws["gmlp_ln_g"]), row(ws["gmlp_ln_b"]), wsm_b, bias)
    (x2, xh2, rstd2), _ = _mixout_fwd(x1, s5o, gm, gab, wb["up_a"], wb["up_b"], wb["mix_w_out"],
                                           row(ws["ln2_g"]), row(ws["ln2_b"]), tm)
    (x2b, h2, a2), _ = _ffn_proj(x2, wb["ffn2_w_in"], tf, "ffn2_proj")
    (x3, xh3, rstd3), _ = _ffn_out(x2, a2, wb["ffn2_w_out"], row(ws["ln3_g"]), row(ws["ln3_b"]), tf, "ffn2_out")
    (dx3, x3b, pb, dq, de, loss_rows), _ = _ple_loss(x3, p0, tg, wb["ple_w_gate"], wb["ple_w_proj"], tm)
    gb["ple_w_gate"], _ = _tn_matmul(x3b, dq, "dw_ple_gate", 1024, 1024, a_t=True)
    gb["ple_w_proj"], _ = _tn_matmul(pb, de, "dw_ple_proj", 256, 1024, a_t=True)
    launch(exchange(scat=("ple_w_gate", "ple_w_proj")))
    (dx2, dh2, df2, gs["ln3_g"], gs["ln3_b"]), _ = _ffn_bwd(
        dx3, xh3, rstd3, h2, wb["ffn2_w_in"], wb["ffn2_w_out"], row(ws["ln3_g"]), tm, "ffn2_bwd")
    gb["ffn2_w_out"], _ = _tn_matmul(a2, df2, "dw_ffn2_out", 1408, 1024)
    launch(exchange(scat=("ffn2_w_out",)))
    gb["ffn2_w_in"], _ = _tn_matmul(x2b, dh2, "dw_ffn2_in", 1024, 1408, a_t=True)
    launch(exchange(scat=("ffn2_w_in",), swap=("ple_w_gate", "ple_w_proj")))
    (dx1a, dmx, mb, dya, dyb, ds5, dgm, dgab, gs["ln2_g"], gs["ln2_b"]), _ = _mixout_bwd(
        dx2, xh2, rstd2, s5o, gm, gab, wb["up_a"], wb["up_b"], wb["mix_w_out"], row(ws["ln2_g"]), tm)
    gb["mix_w_out"], _ = _tn_matmul(mb, dmx, "dw_mix_out", 1024, 1024, a_t=True)
    gb["up_a"], _ = _tn_matmul(s5ot, dya, "dw_up_a", 512, 1024, a_t=True)
    gb["up_b"], _ = _tn_matmul(gmt, dyb, "dw_up_b", 512, 1024, a_t=True)
    launch(exchange(scat=("mix_w_out", "up_a", "up_b"), swap=("ffn2_w_out",)))
    (dza, dmr, dmi, dnr, dni, da, ddsk, dgw, dgb), _ = _s5_bwd(za, y2p, ds5, carries, sp, bsz, seq, tb)
    gb["ssm_glu_w"] = (dgw, dgw.astype(BF16))
    launch(exchange(scat=("ssm_glu_w",), swap=("ffn2_w_in",)))
    (dzuv, dws, dbias, gs["gmlp_ln_g"], gs["gmlp_ln_b"]), _ = _gmlp_bwd(
        zuv, dgm, row(ws["gmlp_ln_g"]), row(ws["gmlp_ln_b"]), wsm_b, wsmt_b, bias)
    (dx1,), _ = _mixin_bwd(dx1a, dza, dzuv, dgab, wb["mix_w_in"], tm)
    g_mi, _ = _tn_matmul(x1b, dza, "dw_mix_in_a", 1024, 512, 0, 3584, a_t=True)
    g_mi, _ = _tn_matmul(x1b, dzuv, "dw_mix_in_uv", 1024, 512, 1, 3584, g_mi, a_t=True)
    gb["mix_w_in"], _ = _tn_matmul(x1b, dgab, "dw_mix_in_g", 1024, 512, 3, 3584, g_mi, a_t=True)
    launch(exchange(swap=("mix_w_out", "up_a", "up_b", "ssm_glu_w")))

    d_abr = da[0].sum(axis=0).reshape(SSM_GROUPS, SSM_STATE)
    d_abi = da[1].sum(axis=0).reshape(SSM_GROUPS, SSM_STATE)
    _, vjp = jax.vjp(_s5_discretise, ws["ssm_lambda_re"], ws["ssm_lambda_im"], ws["ssm_log_dt"],
                     ws["ssm_b_re"], ws["ssm_b_im"])
    (gs["ssm_lambda_re"], gs["ssm_lambda_im"], gs["ssm_log_dt"], gs["ssm_b_re"], gs["ssm_b_im"]) = vjp(
        (d_abr, d_abi, _block_diag_in_t(dmr), _block_diag_in_t(dmi)))
    gs["ssm_c_re"] = _block_diag_out_t(dnr)
    gs["ssm_c_im"] = _block_diag_out_t(dni)
    gs["ssm_d"] = ddsk
    gs["ssm_glu_b"] = dgb
    gs["gmlp_w_s"] = dws
    gs["gmlp_b_s"] = dbias.reshape(CHUNK, GMLP_HEADS, GMLP_HEAD_DIM).sum(axis=-1).T
    gs["loss_rows"] = loss_rows

    def small_gather(names):
        return (_gather_small([gs[k].reshape(small_shape[k]) for k in names]), gathered, names) if dist else None

    late = ("ln1_g", "ln1_b")
    launch(exchange(scat=("mix_w_in",), extra=small_gather(tuple(k for k in SMALL + ("loss_rows",) if k not in late))))
    (dx0, dh1, df1, gs["ln1_g"], gs["ln1_b"]), _ = _ffn_bwd(
        dx1, xh1, rstd1, h1, wb["ffn1_w_in"], wb["ffn1_w_out"], row(ws["ln1_g"]), tm, "ffn1_bwd")
    grad_x = dx0.reshape(bsz, seq, D_MODEL)
    if not dist:
        gb["ffn1_w_out"], _ = _tn_matmul(a1, df1, "dw_ffn1_out", 1408, 1024)
        gb["ffn1_w_in"], _ = _tn_matmul(x0b, dh1, "dw_ffn1_in", 1024, 1408, a_t=True)
        return loss_rows, grad_x, gb, {k: gs[k].reshape(small_shape[k]) for k in SMALL}, sums, other, gathered, None
    launch(exchange(extra=small_gather(late)))
    gb["ffn1_w_out"], _ = _tn_matmul(a1, df1, "dw_ffn1_out", 1408, 1024)
    last = ["ffn1_w_out"] + [LAST_PIECE % q for q in range(LAST_PIECES)]
    for i in range(1, len(last) + 3):
        stage = lambda d: tuple(last[i - d:i - d + 1]) if 0 <= i - d < len(last) else ()
        launch(exchange(halves=stage(1), scat2=stage(2), swap2=stage(3), swap=("mix_w_in",) if i == 2 else ()))
        if i < len(last):
            gb[last[i]], _ = _tn_matmul(x0b, dh1, "dw_" + last[i], D_MODEL // LAST_PIECES, 1408,
                                        a_cols=(i - 1, 1), a_t=True)
    return loss_rows, grad_x, gb, gs, sums, other, gathered, ids


def _adamw(w, g, m, v):
    m = ADAM_B1 * m + (1.0 - ADAM_B1) * g
    v = ADAM_B2 * v + (1.0 - ADAM_B2) * (g * g)
    m_hat = m / ADAM_C1
    v_hat = v / ADAM_C2
    delta = -ADAM_LR * (m_hat / (jnp.sqrt(v_hat) + ADAM_EPS) + ADAM_WD * w)
    return delta, m, v


def _sum_blocks(part, recv, shape, axis, chip, name):
    r, c = shape
    rb = r // 8

    def body(chip_ref, p_ref, r_ref, o_ref):
        o_ref[...] = (p_ref[...] + r_ref[0].astype(F32) + r_ref[1].astype(F32) + r_ref[2].astype(F32))

    if axis == 0:
        own = pl.BlockSpec((rb, c), lambda i, k: (k[0] * 8 + i, 0))
    else:
        own = pl.BlockSpec((rb, c), lambda i, k: (i, k[0]))
    grid_spec = pltpu.PrefetchScalarGridSpec(
        num_scalar_prefetch=1, grid=(8,),
        in_specs=[own, pl.BlockSpec((3, rb, c), lambda i, k: (0, i, 0))],
        out_specs=pl.BlockSpec((rb, c), lambda i, k: (i, 0)))
    return pl.pallas_call(body, name=name, out_shape=SDS((r, c), F32), grid_spec=grid_spec,
                          compiler_params=_params(("parallel",)))(chip, part, recv)


def _presum(part, half, shape, axis, ids, name):
    r, c = shape
    rb = r // 4

    def body(ids_ref, p_ref, h_ref, of_ref, ob_ref):
        s = p_ref[...] + h_ref[...].astype(F32)
        ob_ref[...] = s.astype(BF16)

        @pl.when(pl.program_id(1) == 0)
        def _():
            of_ref[...] = s

    if axis == 0:
        p_spec = pl.BlockSpec((rb, c), lambda i, t, ids: (ids[t] * 4 + ids[4] * 2 + i, 0))
        h_spec = pl.BlockSpec((None, rb, c), lambda i, t, ids: (ids[t], i, 0))
    else:
        p_spec = pl.BlockSpec((rb, c), lambda i, t, ids: (ids[4] * 2 + i, ids[t]))
        h_spec = pl.BlockSpec((rb, c), lambda i, t, ids: (i, ids[t]))
    grid_spec = pltpu.PrefetchScalarGridSpec(
        num_scalar_prefetch=1, grid=(2, 4), in_specs=[p_spec, h_spec],
        out_specs=(pl.BlockSpec((rb, c), lambda i, t, ids: (i, 0)),
                   pl.BlockSpec((None, rb, c), lambda i, t, ids: (t, i, 0))))
    return pl.pallas_call(body, name=name, out_shape=(SDS((r // 2, c), F32), SDS((4, r // 2, c), BF16)),
                          grid_spec=grid_spec, compiler_params=_params(("parallel", "arbitrary")))(ids, part, half)


def _sum_half(pre, recv, name):
    hr, c = pre.shape
    rb = hr // 2

    def body(p_ref, r_ref, o_ref):
        o_ref[...] = (p_ref[...] + r_ref[0].astype(F32) + r_ref[1].astype(F32) + r_ref[2].astype(F32))

    spec = pl.BlockSpec((rb, c), lambda i: (i, 0))
    return pl.pallas_call(body, name=name, grid=(2,), out_shape=SDS((hr, c), F32),
                          in_specs=[spec, pl.BlockSpec((3, rb, c), lambda i: (0, i, 0))], out_specs=spec,
                          compiler_params=_params(("parallel",)))(pre, recv)


def _adam_halves(w, mine, oth, m, v, ids, name, piece=0, prev=None):
    r, c = w.shape
    rb = mine.shape[0] // 2

    def body(ids_ref, w_ref, a_ref, b_ref, m_ref, v_ref, *rest):
        g_ref, d_ref, nm_ref, nv_ref = rest[-4:]
        g = jnp.where(pl.program_id(0) // 2 == ids_ref[4], a_ref[...], b_ref[...])
        g_ref[...] = g
        d_ref[...], nm_ref[...], nv_ref[...] = _adamw(w_ref[...], g, m_ref[...], v_ref[...])

    whole = pl.BlockSpec((rb, c), lambda i, ids: (i + 4 * piece, 0))
    part = pl.BlockSpec((rb, c), lambda i, ids: (i % 2, 0))
    in_specs = [whole, part, part, whole, whole]
    args = [w, mine, oth, m, v]
    aliases = {}
    if prev is not None:
        in_specs += [pl.BlockSpec(memory_space=pl.ANY)] * 4
        args += list(prev)
        aliases = {6: 0, 7: 1, 8: 2, 9: 3}
    grid_spec = pltpu.PrefetchScalarGridSpec(num_scalar_prefetch=1, grid=(4,), in_specs=in_specs,
                                             out_specs=(whole,) * 4)
    return pl.pallas_call(body, name=name, out_shape=tuple(SDS((r, c), F32) for _ in range(4)),
                          grid_spec=grid_spec, input_output_aliases=aliases,
                          compiler_params=_params(("parallel",)))(ids, *args)


def _adam_big(w, ga, gb, m, v, name, piece=0, prev=None, after=None):
    r, c = w.shape
    pr = ga.shape[0]
    steps = 8 if pr == r else 2
    rb = pr // steps
    off = piece * steps

    def body(w_ref, ga_ref, gb_ref, m_ref, v_ref, *rest):
        g_ref, d_ref, nm_ref, nv_ref = rest[-4:]
        g = ga_ref[...] + gb_ref[...]
        g_ref[...] = g
        d_ref[...], nm_ref[...], nv_ref[...] = _adamw(w_ref[...], g, m_ref[...], v_ref[...])

    whole = pl.BlockSpec((rb, c), lambda i: (i + off, 0))
    part = pl.BlockSpec((rb, c), lambda i: (i, 0))
    in_specs = [whole, part, part, whole, whole]
    args = [w, ga, gb, m, v]
    aliases = {}
    if prev is not None:
        in_specs += [pl.BlockSpec(memory_space=pl.ANY)] * 4
        args += list(prev)
        aliases = {5: 0, 6: 1, 7: 2, 8: 3}
    if after is not None:
        in_specs.append(pl.BlockSpec(memory_space=pl.ANY))
        args.append(after)
    return pl.pallas_call(
        body, name=name, grid=(steps,), out_shape=tuple(SDS((r, c), F32) for _ in range(4)),
        in_specs=in_specs, out_specs=(whole,) * 4, input_output_aliases=aliases,
        compiler_params=_params(("parallel",)),
    )(*args)


def _adam_small(ws, gathered, ms, vs):
    n = len(ws)

    def body(*refs):
        w_refs, g_refs, m_refs, v_refs = refs[:n], refs[n:2 * n], refs[2 * n:3 * n], refs[3 * n:4 * n]
        outs = refs[4 * n:]
        for i in range(n):
            g = g_refs[i][0]
            for d in range(1, N_DEV):
                g = g + g_refs[i][d]
            delta, nm, nv = _adamw(w_refs[i][...], g, m_refs[i][...], v_refs[i][...])
            outs[i][...] = g
            outs[n + i][...] = delta
            outs[2 * n + i][...] = nm
            outs[3 * n + i][...] = nv

    vmem = pl.BlockSpec(memory_space=pltpu.VMEM)
    shapes = [w.shape for w in ws]
    return pl.pallas_call(
        body, name="adam_small", out_shape=tuple(SDS(s, F32) for s in shapes * 4),
        in_specs=[vmem] * (4 * n), out_specs=tuple([vmem] * (4 * n)),
        compiler_params=pltpu.CompilerParams(vmem_limit_bytes=VMEM_LIMIT_BYTES),
    )(*ws, *gathered, *ms, *vs)


def _sum_loss(gathered):
    def body(g_ref, o_ref):
        tot = g_ref[0]
        for d in range(1, N_DEV):
            tot = tot + g_ref[d]
        o_ref[...] = (0.5 / D_MODEL) * jnp.sum(tot, axis=1, keepdims=True)

    vmem = pl.BlockSpec(memory_space=pltpu.VMEM)
    return pl.pallas_call(body, name="sum_loss", out_shape=SDS((1, 1), F32), in_specs=[vmem],
                          out_specs=vmem)(gathered)


def kernel(x, p, ffn1_w_in, ffn1_w_out, ln1_g, ln1_b, mix_w_in, ssm_lambda_re, ssm_lambda_im, ssm_log_dt, ssm_b_re, ssm_b_im, ssm_c_re, ssm_c_im, ssm_d, ssm_glu_w, ssm_glu_b, gmlp_ln_g, gmlp_ln_b, gmlp_w_s, gmlp_b_s, up_a, up_b, mix_w_out, ln2_g, ln2_b, ffn2_w_in, ffn2_w_out, ln3_g, ln3_b, ple_w_proj, ple_w_gate, loss_target, m_ffn1_w_in, m_ffn1_w_out, m_ln1_g, m_ln1_b, m_mix_w_in, m_ssm_lambda_re, m_ssm_lambda_im, m_ssm_log_dt, m_ssm_b_re, m_ssm_b_im, m_ssm_c_re, m_ssm_c_im, m_ssm_d, m_ssm_glu_w, m_ssm_glu_b, m_gmlp_ln_g, m_gmlp_ln_b, m_gmlp_w_s, m_gmlp_b_s, m_up_a, m_up_b, m_mix_w_out, m_ln2_g, m_ln2_b, m_ffn2_w_in, m_ffn2_w_out, m_ln3_g, m_ln3_b, m_ple_w_proj, m_ple_w_gate, v_ffn1_w_in, v_ffn1_w_out, v_ln1_g, v_ln1_b, v_mix_w_in, v_ssm_lambda_re, v_ssm_lambda_im, v_ssm_log_dt, v_ssm_b_re, v_ssm_b_im, v_ssm_c_re, v_ssm_c_im, v_ssm_d, v_ssm_glu_w, v_ssm_glu_b, v_gmlp_ln_g, v_gmlp_ln_b, v_gmlp_w_s, v_gmlp_b_s, v_up_a, v_up_b, v_mix_w_out, v_ln2_g, v_ln2_b, v_ffn2_w_in, v_ffn2_w_out, v_ln3_g, v_ln3_b, v_ple_w_proj, v_ple_w_gate):
    given = dict(locals())
    order = ("ffn1_w_in", "ffn1_w_out", "ln1_g", "ln1_b", "mix_w_in", "ssm_lambda_re", "ssm_lambda_im",
             "ssm_log_dt", "ssm_b_re", "ssm_b_im", "ssm_c_re", "ssm_c_im", "ssm_d", "ssm_glu_w", "ssm_glu_b",
             "gmlp_ln_g", "gmlp_ln_b", "gmlp_w_s", "gmlp_b_s", "up_a", "up_b", "mix_w_out", "ln2_g", "ln2_b",
             "ffn2_w_in", "ffn2_w_out", "ln3_g", "ln3_b", "ple_w_proj", "ple_w_gate")
    assert set(order) == set(BIG + SMALL)

    shard = {k: given[k][0] for k in BIG}
    shard_b = {k: shard[k].astype(BF16) for k in BIG}
    loss_rows, grad_x, gb, gs, sums, other, gathered, ids = _local_step(
        x, given["p"][0], loss_target, {}, {k: given[k] for k in SMALL}, shard_b)

    out = {}
    for k in BIG:
        moments = (given["m_" + k][0], given["v_" + k][0])
        if k == "ffn1_w_out":
            out[k] = _adam_halves(shard[k], sums[k], other[k], *moments, ids, "adam_" + k)
        elif k == "ffn1_w_in":
            for q in range(LAST_PIECES):
                kq = LAST_PIECE % q
                out[k] = _adam_halves(shard[k], sums[kq], other[kq], *moments, ids, "adam_" + kq, q, out.get(k))
        else:
            out[k] = _adam_big(shard[k], sums[k], other[k], *moments, "adam_" + k,
                               after=gb[LAST_PIECE % (LAST_PIECES - 1)][0])

    res = _adam_small([_small_view(k, given[k]) for k in SMALL], [gathered[k] for k in SMALL],
                      [_small_view(k, given["m_" + k]) for k in SMALL],
                      [_small_view(k, given["v_" + k]) for k in SMALL])
    ns = len(SMALL)
    for i, k in enumerate(SMALL):
        out[k] = tuple(res[j * ns + i].reshape(given[k].shape) for j in range(4))
    loss = _sum_loss(gathered["loss_rows"]).reshape(())

    lead = lambda k, j: out[k][j][None] if k in BIG else out[k][j]
    return (loss, grad_x, *[lead(k, 0) for k in order], *[lead(k, 1) for k in order],
            *[lead(k, 2) for k in order], *[lead(k, 3) for k in order])
```

```python
import math

import jax
import jax.numpy as jnp
from jax import lax
from jax.experimental import pallas as pl
from jax.experimental.pallas import tpu as pltpu
from jax.experimental.pallas import tpu_sc as plsc

F32 = jnp.float32
BF16 = jnp.bfloat16
MESH = pl.DeviceIdType.MESH
SDS = jax.ShapeDtypeStruct

D_MODEL = 1024
D_FF = 2816
D_SSM = 512
D_GMLP = 512
SSM_GROUPS = 32
SSM_GROUP_CH = 16
SSM_STATE = 64
SSM_LANES = SSM_GROUPS * SSM_STATE
GMLP_HEADS = 8
GMLP_HEAD_DIM = 64
CHUNK = 128
PLE_DIM = 256
LN_EPS = 1e-5
ALPHA = 2.0 ** 0.25

ADAM_LR = 0.001
ADAM_B1 = 0.9
ADAM_B2 = 0.999
ADAM_EPS = 1e-08
ADAM_WD = 0.01
ADAM_STEP = 10
ADAM_C1 = 1.0 - ADAM_B1 ** ADAM_STEP
ADAM_C2 = 1.0 - ADAM_B2 ** ADAM_STEP

N_DEV = 8
VMEM_LIMIT_BYTES = 56 * 1024 * 1024
FFN_COLS = 1408
S5_BLOCKS = 4
S5_BLOCK_IN = D_SSM // S5_BLOCKS
S5_BLOCK_ST = SSM_LANES // S5_BLOCKS
SCAN_LANES = 512
TN_K_BLOCK = 2048
DIRECT_GATHER_BYTES = 0
LAST_PIECES = 2
LAST_PIECE = "ffn1_w_in_q%d"
_G0 = math.sqrt(2.0 / math.pi)
_G1 = 0.044715


def _dot(a, b):
    return jnp.dot(a, b, preferred_element_type=F32)


def _dot_nt(a, b):
    return lax.dot_general(a, b, (((1,), (1,)), ((), ())), preferred_element_type=F32)


def _dot_tn(a, b):
    return lax.dot_general(a, b, (((0,), (0,)), ((), ())), preferred_element_type=F32)


def _sigmoid(x):
    return 1.0 / (1.0 + jnp.exp(-x))


def _gelu(x):
    t = jnp.tanh(_G0 * (x + _G1 * x * x * x))
    return 0.5 * x * (1.0 + t)


def _gelu_grad(x):
    t = jnp.tanh(_G0 * (x + _G1 * x * x * x))
    return 0.5 * (1.0 + t) + 0.5 * x * (1.0 - t * t) * _G0 * (1.0 + 3.0 * _G1 * x * x)


def _ln_fwd(r, g, b):
    mu = jnp.mean(r, axis=-1, keepdims=True)
    d = r - mu
    var = jnp.mean(d * d, axis=-1, keepdims=True)
    rstd = lax.rsqrt(var + LN_EPS)
    xh = d * rstd
    return xh * g + b, xh, rstd


def _ln_bwd(dy, xh, rstd, g):
    dxh = dy * g
    m1 = jnp.mean(dxh, axis=-1, keepdims=True)
    m2 = jnp.mean(dxh * xh, axis=-1, keepdims=True)
    return rstd * (dxh - m1 - xh * m2)


def _resident(shape):
    nd = len(shape)
    return pl.BlockSpec(shape, lambda *_: (0,) * nd, pipeline_mode=pl.Buffered(1))


def _fixed(shape):
    nd = len(shape)
    return pl.BlockSpec(shape, lambda *_: (0,) * nd)


def _rows(tm, cols):
    return pl.BlockSpec((tm, cols), lambda i: (i, 0))


def _cols(rows, tm):
    return pl.BlockSpec((rows, tm), lambda i: (0, i))


def _params(sem):
    return pltpu.CompilerParams(dimension_semantics=sem, vmem_limit_bytes=VMEM_LIMIT_BYTES)


class _Exchange:
    def __init__(self, args, out_shape, sems, start, finish):
        self.args, self.out_shape, self.sems = list(args), list(out_shape), list(sems)
        self.start, self.finish = start, finish
        self.cuts = [(0, len(self.out_shape))]


def _call(body, name, grid, in_specs, out_specs, out_shape, args, scratch=(), sem=None, bg=None, aliases=None):
    aliases = {} if aliases is None else aliases
    if bg is None:
        res = pl.pallas_call(body, name=name, grid=grid, out_shape=tuple(out_shape), in_specs=list(in_specs),
                             out_specs=tuple(out_specs), scratch_shapes=list(scratch),
                             input_output_aliases=aliases, compiler_params=_params(sem))(*args)
        return tuple(res), ()
    n_in, n_out, n_bi, n_bo, n_sc = len(args), len(out_shape), len(bg.args), len(bg.out_shape), len(scratch)

    def wrapped(*refs):
        ins = refs[:n_in]
        b_ins = refs[n_in:n_in + n_bi]
        outs = refs[n_in + n_bi:n_in + n_bi + n_out]
        b_outs = refs[n_in + n_bi + n_out:n_in + n_bi + n_out + n_bo]
        rest = refs[n_in + n_bi + n_out + n_bo:]
        scr, b_sems = rest[:n_sc], rest[n_sc:]
        first = pl.program_id(0) == 0
        last = pl.program_id(0) == grid[0] - 1
        for ax in range(1, len(grid)):
            first = jnp.logical_and(first, pl.program_id(ax) == 0)
            last = jnp.logical_and(last, pl.program_id(ax) == grid[ax] - 1)

        @pl.when(first)
        def _():
            bg.start(b_ins, b_outs, b_sems)

        body(*ins, *outs, *scr)

        @pl.when(last)
        def _():
            bg.finish(b_ins, b_outs, b_sems)

    any_spec = pl.BlockSpec(memory_space=pl.ANY)
    res = pl.pallas_call(
        wrapped, name=name, grid=grid, out_shape=tuple(out_shape) + tuple(bg.out_shape),
        in_specs=list(in_specs) + [any_spec] * n_bi, out_specs=tuple(out_specs) + (any_spec,) * n_bo,
        scratch_shapes=list(scratch) + list(bg.sems), input_output_aliases=aliases,
        compiler_params=_params(tuple("arbitrary" for _ in grid)))(*args, *bg.args)
    return tuple(res[:n_out]), tuple(res[n_out:])


def _run_exchange(ex, name):
    n_i, n_o = len(ex.args), len(ex.out_shape)

    def body(*refs):
        ins, outs, sems = refs[:n_i], refs[n_i:n_i + n_o], refs[n_i + n_o:]
        ex.start(ins, outs, sems)
        ex.finish(ins, outs, sems)

    any_spec = pl.BlockSpec(memory_space=pl.ANY)
    return tuple(pl.pallas_call(body, name=name, out_shape=tuple(ex.out_shape), in_specs=[any_spec] * n_i,
                                out_specs=(any_spec,) * n_o, scratch_shapes=list(ex.sems))(*ex.args))


def _run_exchange_on_sequencer(ex, name, collective_id):
    n_i, n_o = len(ex.args), len(ex.out_shape)

    def body(*refs):
        ins, outs, sems = refs[:n_i], refs[n_i:n_i + n_o], refs[n_i + n_o:]
        x, y, c = lax.axis_index("x"), lax.axis_index("y"), lax.axis_index("c")
        barrier = pltpu.get_barrier_semaphore()
        for peer in [(x, y, 1 - c), (1 - x, y, c), (x, 1 - y, c), (1 - x, 1 - y, c)]:
            pl.semaphore_signal(barrier, inc=1, device_id=peer, device_id_type=MESH)
        pl.semaphore_wait(barrier, 4)
        ex.start(ins, outs, sems)
        ex.finish(ins, outs, sems)

    return tuple(pl.kernel(body, out_type=tuple(ex.out_shape),
                           mesh=plsc.ScalarSubcoreMesh(axis_name="sequencer", num_cores=1),
                           scratch_types=list(ex.sems), name=name,
                           compiler_params=pltpu.CompilerParams(collective_id=collective_id))(*ex.args))


def _join(exchanges):
    cuts = []
    a = o = q = 0
    for e in exchanges:
        cuts.append((a, a + len(e.args), o, o + len(e.out_shape), q, q + len(e.sems)))
        a, o, q = cuts[-1][1], cuts[-1][3], cuts[-1][5]

    def start(ins, outs, sems):
        for e, (a0, a1, o0, o1, q0, q1) in zip(exchanges, cuts):
            e.start(ins[a0:a1], outs[o0:o1], sems[q0:q1])

    def finish(ins, outs, sems):
        for e, (a0, a1, o0, o1, q0, q1) in zip(exchanges, cuts):
            e.finish(ins[a0:a1], outs[o0:o1], sems[q0:q1])

    joined = _Exchange(sum((e.args for e in exchanges), []), sum((e.out_shape for e in exchanges), []),
                       sum((e.sems for e in exchanges), []), start, finish)
    joined.cuts = [(c[2], c[3]) for c in cuts]
    return joined


def _ffn_proj(x, w_in, tm, name, bg=None):
    t = x.shape[0]
    nch = D_FF // FFN_COLS

    def body(x_ref, win_ref, xbt_ref, h_ref, a_ref):
        xb = x_ref[...].astype(BF16)
        xbt_ref[...] = xb.T
        for k in range(nch):
            cg = slice(k * FFN_COLS, (k + 1) * FFN_COLS)
            cu = slice(D_FF + k * FFN_COLS, D_FF + (k + 1) * FFN_COLS)
            hg = _dot(xb, win_ref[k])
            hu = _dot(xb, win_ref[nch + k])
            h_ref[:, cg] = hg.astype(BF16)
            h_ref[:, cu] = hu.astype(BF16)
            a_ref[:, cg] = (hg * _sigmoid(hg) * hu).astype(BF16)

    return _call(
        body, name, (t // tm,),
        [_rows(tm, D_MODEL), _resident((2 * nch, D_MODEL, FFN_COLS))],
        (_cols(D_MODEL, tm), _rows(tm, 2 * D_FF), _rows(tm, D_FF)),
        (SDS((D_MODEL, t), BF16), SDS((t, 2 * D_FF), BF16), SDS((t, D_FF), BF16)),
        (x, w_in), sem=("parallel",), bg=bg)


def _ffn_out(x, a, w_out, g, b, tm, name, bg=None):
    t = x.shape[0]

    def body(x_ref, a_ref, wout_ref, g_ref, b_ref, xn_ref, xh_ref, rstd_ref):
        f = _dot(a_ref[...], wout_ref[...])
        y, xh, rstd = _ln_fwd(ALPHA * x_ref[...] + 0.5 * f, g_ref[...], b_ref[...])
        xn_ref[...] = y
        xh_ref[...] = xh
        rstd_ref[...] = rstd

    return _call(
        body, name, (t // tm,),
        [_rows(tm, D_MODEL), _rows(tm, D_FF), _resident((D_FF, D_MODEL)), _fixed((1, D_MODEL)), _fixed((1, D_MODEL))],
        (_rows(tm, D_MODEL), _rows(tm, D_MODEL), _rows(tm, 1)),
        (SDS((t, D_MODEL), F32), SDS((t, D_MODEL), F32), SDS((t, 1), F32)),
        (x, a, w_out, g, b), sem=("parallel",), bg=bg)


def _ffn_bwd(dxn, xh, rstd, h, w_in, w_out, g, tm, name, bg=None):
    t = dxn.shape[0]
    nch = D_FF // FFN_COLS

    def body(dxn_ref, xh_ref, rstd_ref, h_ref, win_ref, wout_ref, g_ref,
             dx_ref, dh_ref, df_ref, dg_ref, db_ref):
        @pl.when(pl.program_id(0) == 0)
        def _():
            dg_ref[...] = jnp.zeros_like(dg_ref)
            db_ref[...] = jnp.zeros_like(db_ref)

        dy = dxn_ref[...]
        xhv = xh_ref[...]
        dr = _ln_bwd(dy, xhv, rstd_ref[...], g_ref[...])
        dg_ref[...] += jnp.sum(dy * xhv, axis=0, keepdims=True)
        db_ref[...] += jnp.sum(dy, axis=0, keepdims=True)
        df = (0.5 * dr).astype(BF16)
        df_ref[...] = df
        dx = ALPHA * dr
        for k in range(nch):
            cg = slice(k * FFN_COLS, (k + 1) * FFN_COLS)
            cu = slice(D_FF + k * FFN_COLS, D_FF + (k + 1) * FFN_COLS)
            hg = h_ref[:, cg].astype(F32)
            hu = h_ref[:, cu].astype(F32)
            sg = _sigmoid(hg)
            silu = hg * sg
            da = _dot_nt(df, wout_ref[cg, :])
            dhu = (da * silu).astype(BF16)
            dhg = (da * hu * (sg * (1.0 + hg * (1.0 - sg)))).astype(BF16)
            dh_ref[:, cg] = dhg
            dh_ref[:, cu] = dhu
            dx = dx + _dot_nt(dhg, win_ref[k]) + _dot_nt(dhu, win_ref[nch + k])
        dx_ref[...] = dx

    return _call(
        body, name, (t // tm,),
        [_rows(tm, D_MODEL), _rows(tm, D_MODEL), _rows(tm, 1), _rows(tm, 2 * D_FF),
         _resident((2 * nch, D_MODEL, FFN_COLS)), _resident((D_FF, D_MODEL)), _fixed((1, D_MODEL))],
        (_rows(tm, D_MODEL), _rows(tm, 2 * D_FF), _rows(tm, D_MODEL),
         _fixed((1, D_MODEL)), _fixed((1, D_MODEL))),
        (SDS((t, D_MODEL), F32), SDS((t, 2 * D_FF), BF16), SDS((t, D_MODEL), BF16),
         SDS((1, D_MODEL), F32), SDS((1, D_MODEL), F32)),
        (dxn, xh, rstd, h, w_in, w_out, g), sem=("arbitrary",), bg=bg)


def _tn_matmul(a, b, name, bm, bn, col_block=0, total_cols=None, prev=None, bg=None, a_cols=None, a_t=False):
    t, m = a.shape[::-1] if a_t else a.shape
    a_first = 0
    if a_cols is not None:
        a_first, m = a_cols[0], a_cols[1] * bm
    n = b.shape[1]
    total_cols = n if total_cols is None else total_cols
    bk = min(TN_K_BLOCK, t)
    nk = t // bk
    n_in = 2 if prev is None else 4

    def body(*refs):
        a_ref, b_ref = refs[0], refs[1]
        o_ref, ob_ref = refs[n_in], refs[n_in + 1]
        k = pl.program_id(2)

        @pl.when(k == 0)
        def _():
            o_ref[...] = jnp.zeros_like(o_ref)

        o_ref[...] += _dot(a_ref[...], b_ref[...]) if a_t else _dot_tn(a_ref[...], b_ref[...])

        @pl.when(k == nk - 1)
        def _():
            ob_ref[...] = o_ref[...].astype(BF16)

    a_spec = (pl.BlockSpec((bm, bk), lambda i, j, k: (i + a_first, k)) if a_t
              else pl.BlockSpec((bk, bm), lambda i, j, k: (k, i + a_first)))
    in_specs = [a_spec, pl.BlockSpec((bk, bn), lambda i, j, k: (k, j))]
    args = [a, b]
    aliases = {}
    if prev is not None:
        in_specs += [pl.BlockSpec(memory_space=pl.ANY), pl.BlockSpec(memory_space=pl.ANY)]
        args += list(prev)
        aliases = {2: 0, 3: 1}
    out_spec = pl.BlockSpec((bm, bn), lambda i, j, k: (i, j + col_block))
    return _call(body, name, (m // bm, n // bn, nk), in_specs, (out_spec, out_spec),
                 (SDS((m, total_cols), F32), SDS((m, total_cols), BF16)), args,
                 sem=("parallel", "parallel", "arbitrary"), bg=bg, aliases=aliases)


def _mixin_fwd(x1, w, tm, bg=None):
    t = x1.shape[0]

    def body(x_ref, w_ref, xbt_ref, za_ref, zuv_ref, gab_ref):
        xb = x_ref[...].astype(BF16)
        xbt_ref[...] = xb.T
        za_ref[...] = _dot(xb, w_ref[:, 0:512]).astype(BF16)
        zuv_ref[...] = _dot(xb, w_ref[:, 512:1536]).astype(BF16)
        gab_ref[...] = _dot(xb, w_ref[:, 1536:3584]).astype(BF16)

    return _call(
        body, "mixin_fwd", (t // tm,),
        [_rows(tm, D_MODEL), _resident((D_MODEL, 3584))],
        (_cols(D_MODEL, tm), _rows(tm, 512), _rows(tm, 1024), _rows(tm, 2048)),
        (SDS((D_MODEL, t), BF16), SDS((t, 512), BF16), SDS((t, 1024), BF16), SDS((t, 2048), BF16)),
        (x1, w), sem=("parallel",), bg=bg)


def _mixin_bwd(dx1a, dza, dzuv, dgab, w, tm, bg=None):
    t = dx1a.shape[0]

    def body(d_ref, dza_ref, dzuv_ref, dgab_ref, w_ref, dx_ref):
        dx_ref[...] = (d_ref[...] + _dot_nt(dza_ref[...], w_ref[:, 0:512])
                       + _dot_nt(dzuv_ref[...], w_ref[:, 512:1536])
                       + _dot_nt(dgab_ref[...], w_ref[:, 1536:3584]))

    return _call(
        body, "mixin_bwd", (t // tm,),
        [_rows(tm, D_MODEL), _rows(tm, 512), _rows(tm, 1024), _rows(tm, 2048), _resident((D_MODEL, 3584))],
        (_rows(tm, D_MODEL),), (SDS((t, D_MODEL), F32),),
        (dx1a, dza, dzuv, dgab, w), sem=("parallel",), bg=bg)


def _unrolled(lo, hi, body, carry):
    for j in range(lo, hi):
        carry = body(j, carry)
    return carry


def _scan_fwd(hr_ref, hi_ref, a_ref, ap_ref, carry_ref, seg, cin_ref):
    for lc in range(SSM_LANES // SCAN_LANES):
        ls = slice(lc * SCAN_LANES, (lc + 1) * SCAN_LANES)
        a_r = jnp.broadcast_to(a_ref[0:1, ls], (8, SCAN_LANES))
        a_i = jnp.broadcast_to(a_ref[1:2, ls], (8, SCAN_LANES))

        def step(j, hc, ls=ls, a_r=a_r, a_i=a_i):
            h_r, h_i = hc
            rows = pl.ds(j * 8, 8)
            n_r = a_r * h_r - a_i * h_i + hr_ref[rows, ls]
            n_i = a_r * h_i + a_i * h_r + hi_ref[rows, ls]
            hr_ref[rows, ls] = n_r
            hi_ref[rows, ls] = n_i
            return n_r, n_i

        zero = jnp.zeros((8, SCAN_LANES), F32)
        f_r, f_i = _unrolled(0, seg, step, (zero, zero))
        c_r = carry_ref[0:1, ls]
        c_i = carry_ref[1:2, ls]
        p_r = ap_ref[0:1, ls]
        p_i = ap_ref[1:2, ls]
        rows_r, rows_i = [], []
        for s in range(8):
            rows_r.append(c_r)
            rows_i.append(c_i)
            c_r, c_i = (f_r[s:s + 1] + p_r * c_r - p_i * c_i,
                        f_i[s:s + 1] + p_r * c_i + p_i * c_r)
        carry_ref[0:1, ls] = c_r
        carry_ref[1:2, ls] = c_i
        cin_r = jnp.concatenate(rows_r, axis=0)
        cin_i = jnp.concatenate(rows_i, axis=0)
        if cin_ref is not None:
            cin_ref[0, :, ls] = cin_r
            cin_ref[1, :, ls] = cin_i

        def fix(j, cc, ls=ls, a_r=a_r, a_i=a_i):
            c_r, c_i = cc
            c_r, c_i = a_r * c_r - a_i * c_i, a_r * c_i + a_i * c_r
            rows = pl.ds(j * 8, 8)
            hr_ref[rows, ls] = hr_ref[rows, ls] + c_r
            hi_ref[rows, ls] = hi_ref[rows, ls] + c_i
            return c_r, c_i

        _unrolled(0, seg, fix, (cin_r, cin_i))


def _scan_bwd(gr_ref, gi_ref, hr_ref, hi_ref, cin_ref, a_ref, ap_ref, rcarry_ref, da_ref, seg):
    for lc in range(SSM_LANES // SCAN_LANES):
        ls = slice(lc * SCAN_LANES, (lc + 1) * SCAN_LANES)
        a_r = jnp.broadcast_to(a_ref[0:1, ls], (8, SCAN_LANES))
        a_i = jnp.broadcast_to(a_ref[1:2, ls], (8, SCAN_LANES))

        def step(t, gc, ls=ls, a_r=a_r, a_i=a_i):
            g_r, g_i = gc
            rows = pl.ds((seg - 1 - t) * 8, 8)
            n_r = gr_ref[rows, ls] + a_r * g_r + a_i * g_i
            n_i = gi_ref[rows, ls] + a_r * g_i - a_i * g_r
            gr_ref[rows, ls] = n_r
            gi_ref[rows, ls] = n_i
            return n_r, n_i

        zero = jnp.zeros((8, SCAN_LANES), F32)
        f_r, f_i = _unrolled(0, seg, step, (zero, zero))
        c_r = rcarry_ref[0:1, ls]
        c_i = rcarry_ref[1:2, ls]
        p_r = ap_ref[0:1, ls]
        p_i = ap_ref[1:2, ls]
        rows_r, rows_i = [None] * 8, [None] * 8
        for s in range(7, -1, -1):
            rows_r[s] = c_r
            rows_i[s] = c_i
            c_r, c_i = (f_r[s:s + 1] + p_r * c_r + p_i * c_i,
                        f_i[s:s + 1] + p_r * c_i - p_i * c_r)
        rcarry_ref[0:1, ls] = c_r
        rcarry_ref[1:2, ls] = c_i
        cin_r = jnp.concatenate(rows_r, axis=0)
        cin_i = jnp.concatenate(rows_i, axis=0)

        def fix_row(j_rows, hp_r, hp_i, cc, ls=ls, a_r=a_r, a_i=a_i):
            c_r, c_i, acc_r, acc_i = cc
            c_r, c_i = a_r * c_r + a_i * c_i, a_r * c_i - a_i * c_r
            g_r = gr_ref[j_rows, ls] + c_r
            g_i = gi_ref[j_rows, ls] + c_i
            gr_ref[j_rows, ls] = g_r
            gi_ref[j_rows, ls] = g_i
            acc_r = acc_r + g_r * hp_r + g_i * hp_i
            acc_i = acc_i + g_i * hp_r - g_r * hp_i
            return c_r, c_i, acc_r, acc_i

        def fix(t, cc, ls=ls, fix_row=fix_row):
            j = seg - 1 - t
            rows = pl.ds(j * 8, 8)
            prev = pl.ds((j - 1) * 8, 8)
            return fix_row(rows, hr_ref[prev, ls], hi_ref[prev, ls], cc)

        cc = _unrolled(0, seg - 1, fix, (cin_r, cin_i, zero, zero))
        _, _, acc_r, acc_i = fix_row(pl.ds(0, 8), cin_ref[0, :, ls], cin_ref[1, :, ls], cc)
        da_ref[0, :, ls] += acc_r
        da_ref[1, :, ls] += acc_i


def _s5_fwd(za, sp, bsz, seq, tb, bg=None):
    nb = seq // tb
    seg = tb // 8
    t = bsz * seq

    def body(za_ref, perm_ref, permt_ref, mre_ref, mim_ref, nre_ref, nim_ref, a_ref, ap_ref,
             dsk_ref, gw_ref, gb_ref, out_ref, outt_ref, y2_ref, car_ref, hr_ref, hi_ref, carry_ref):
        @pl.when(pl.program_id(1) == 0)
        def _():
            carry_ref[...] = jnp.zeros_like(carry_ref)

        car_ref[0] = carry_ref[...]
        up = _dot(perm_ref[...], za_ref[...])
        upb = up.astype(BF16)
        for bb in range(S5_BLOCKS):
            ub = upb[:, bb * S5_BLOCK_IN:(bb + 1) * S5_BLOCK_IN]
            st = slice(bb * S5_BLOCK_ST, (bb + 1) * S5_BLOCK_ST)
            hr_ref[:, st] = _dot(ub, mre_ref[bb])
            hi_ref[:, st] = _dot(ub, mim_ref[bb])
        _scan_fwd(hr_ref, hi_ref, a_ref, ap_ref, carry_ref, seg, None)
        ys = []
        for bb in range(S5_BLOCKS):
            st = slice(bb * S5_BLOCK_ST, (bb + 1) * S5_BLOCK_ST)
            ys.append(_dot(hr_ref[:, st].astype(BF16), nre_ref[bb])
                      - _dot(hi_ref[:, st].astype(BF16), nim_ref[bb]))
        y2 = jnp.concatenate(ys, axis=1) + dsk_ref[...] * up
        y2_ref[...] = y2
        y3 = _gelu(y2)
        gl = _dot(y3.astype(BF16), gw_ref[...]) + gb_ref[...]
        oa = y3 * _sigmoid(gl)
        out = _dot(permt_ref[...], oa.astype(BF16)).astype(BF16)
        out_ref[...] = out
        outt_ref[...] = out.T

    blk = pl.BlockSpec((tb, D_SSM), lambda b, j: (b * nb + j, 0))
    blk_t = pl.BlockSpec((D_SSM, tb), lambda b, j: (0, b * nb + j))
    m_shape = (S5_BLOCKS, S5_BLOCK_IN, S5_BLOCK_ST)
    n_shape = (S5_BLOCKS, S5_BLOCK_ST, S5_BLOCK_IN)
    return _call(
        body, "s5_fwd", (bsz, nb),
        [blk, _fixed((tb, tb)), _fixed((tb, tb)), _fixed(m_shape), _fixed(m_shape), _fixed(n_shape),
         _fixed(n_shape), _fixed((2, SSM_LANES)), _fixed((2, SSM_LANES)), _fixed((1, D_SSM)),
         _fixed((D_SSM, D_SSM)), _fixed((1, D_SSM))],
        (blk, blk_t, blk, pl.BlockSpec((1, 2, SSM_LANES), lambda b, j: (b * nb + j, 0, 0))),
        (SDS((t, D_SSM), BF16), SDS((D_SSM, t), BF16), SDS((t, D_SSM), F32), SDS((bsz * nb, 2, SSM_LANES), F32)),
        (za, sp["perm"], sp["permt"], sp["mre"], sp["mim"], sp["nre"], sp["nim"], sp["a"], sp["ap"],
         sp["dskip"], sp["glu_w"], sp["glu_b"]),
        scratch=[pltpu.VMEM((tb, SSM_LANES), F32), pltpu.VMEM((tb, SSM_LANES), F32),
                 pltpu.VMEM((2, SSM_LANES), F32)],
        sem=("arbitrary", "arbitrary"), bg=bg)


def _s5_bwd(za, y2p, doa, carries, sp, bsz, seq, tb, bg=None):
    nb = seq // tb
    seg = tb // 8
    t = bsz * seq

    def body(za_ref, y2_ref, doa_ref, car_ref, perm_ref, permt_ref, mre_ref, mim_ref, mtre_ref, mtim_ref,
             nre_ref, nim_ref, ntre_ref, ntim_ref, a_ref, ap_ref, dsk_ref, gw_ref, gwt_ref, gb_ref,
             dza_ref, dmr_ref, dmi_ref, dnr_ref, dni_ref, da_ref, ddsk_ref, dgw_ref, dgb_ref,
             hr_ref, hi_ref, gr_ref, gi_ref, cin_ref, carry_ref, rcarry_ref):
        first = jnp.logical_and(pl.program_id(0) == 0, pl.program_id(1) == 0)

        @pl.when(first)
        def _():
            for r in (dmr_ref, dmi_ref, dnr_ref, dni_ref, da_ref, ddsk_ref, dgw_ref, dgb_ref):
                r[...] = jnp.zeros_like(r)

        @pl.when(pl.program_id(1) == 0)
        def _():
            rcarry_ref[...] = jnp.zeros_like(rcarry_ref)

        carry_ref[...] = car_ref[0]
        perm = perm_ref[...]
        up = _dot(perm, za_ref[...])
        upb = up.astype(BF16)
        for bb in range(S5_BLOCKS):
            ub = upb[:, bb * S5_BLOCK_IN:(bb + 1) * S5_BLOCK_IN]
            st = slice(bb * S5_BLOCK_ST, (bb + 1) * S5_BLOCK_ST)
            hr_ref[:, st] = _dot(ub, mre_ref[bb])
            hi_ref[:, st] = _dot(ub, mim_ref[bb])
        _scan_fwd(hr_ref, hi_ref, a_ref, ap_ref, carry_ref, seg, cin_ref)

        y2 = y2_ref[...]
        y3 = _gelu(y2)
        y3b = y3.astype(BF16)
        sg = _sigmoid(_dot(y3b, gw_ref[...]) + gb_ref[...])
        d0 = doa_ref[...]
        d_hi = d0.astype(BF16)
        d1 = d0 - d_hi.astype(F32)
        d_mid = d1.astype(BF16)
        d_lo = (d1 - d_mid.astype(F32)).astype(BF16)
        doap = _dot(perm, d_hi) + _dot(perm, d_mid) + _dot(perm, d_lo)
        dgl = doap * y3 * sg * (1.0 - sg)
        dglb = dgl.astype(BF16)
        dy3 = doap * sg + _dot(dglb, gwt_ref[...])
        dgw_ref[...] += _dot_tn(y3b, dglb)
        dgb_ref[...] += jnp.sum(dgl, axis=0, keepdims=True)
        dy2 = dy3 * _gelu_grad(y2)
        ddsk_ref[...] += jnp.sum(dy2 * up, axis=0, keepdims=True)
        dyb = dy2.astype(BF16)
        for bb in range(S5_BLOCKS):
            dyc = dyb[:, bb * S5_BLOCK_IN:(bb + 1) * S5_BLOCK_IN]
            st = slice(bb * S5_BLOCK_ST, (bb + 1) * S5_BLOCK_ST)
            gr_ref[:, st] = _dot(dyc, ntre_ref[bb])
            gi_ref[:, st] = -_dot(dyc, ntim_ref[bb])
            dnr_ref[bb] += _dot_tn(hr_ref[:, st].astype(BF16), dyc)
            dni_ref[bb] += -_dot_tn(hi_ref[:, st].astype(BF16), dyc)
        _scan_bwd(gr_ref, gi_ref, hr_ref, hi_ref, cin_ref, a_ref, ap_ref, rcarry_ref, da_ref, seg)
        dus = []
        for bb in range(S5_BLOCKS):
            st = slice(bb * S5_BLOCK_ST, (bb + 1) * S5_BLOCK_ST)
            grb = gr_ref[:, st].astype(BF16)
            gib = gi_ref[:, st].astype(BF16)
            dus.append(_dot(grb, mtre_ref[bb]) + _dot(gib, mtim_ref[bb]))
            ub = upb[:, bb * S5_BLOCK_IN:(bb + 1) * S5_BLOCK_IN]
            dmr_ref[bb] += _dot_tn(ub, grb)
            dmi_ref[bb] += _dot_tn(ub, gib)
        du = jnp.concatenate(dus, axis=1) + dy2 * dsk_ref[...]
        dza_ref[...] = _dot(permt_ref[...], du.astype(BF16)).astype(BF16)

    def rev(b, j):
        return (b * nb + (nb - 1 - j), 0)

    blk = pl.BlockSpec((tb, D_SSM), rev)
    m_shape = (S5_BLOCKS, S5_BLOCK_IN, S5_BLOCK_ST)
    n_shape = (S5_BLOCKS, S5_BLOCK_ST, S5_BLOCK_IN)
    return _call(
        body, "s5_bwd", (bsz, nb),
        [blk, blk, blk, pl.BlockSpec((1, 2, SSM_LANES), lambda b, j: (b * nb + (nb - 1 - j), 0, 0)),
         _fixed((tb, tb)), _fixed((tb, tb)), _fixed(m_shape), _fixed(m_shape), _fixed(n_shape), _fixed(n_shape),
         _fixed(n_shape), _fixed(n_shape), _fixed(m_shape), _fixed(m_shape),
         _fixed((2, SSM_LANES)), _fixed((2, SSM_LANES)), _fixed((1, D_SSM)),
         _fixed((D_SSM, D_SSM)), _fixed((D_SSM, D_SSM)), _fixed((1, D_SSM))],
        (blk, _fixed(m_shape), _fixed(m_shape), _fixed(n_shape), _fixed(n_shape),
         _fixed((2, 8, SSM_LANES)), _fixed((1, D_SSM)), _fixed((D_SSM, D_SSM)), _fixed((1, D_SSM))),
        (SDS((t, D_SSM), BF16), SDS(m_shape, F32), SDS(m_shape, F32), SDS(n_shape, F32), SDS(n_shape, F32),
         SDS((2, 8, SSM_LANES), F32), SDS((1, D_SSM), F32), SDS((D_SSM, D_SSM), F32), SDS((1, D_SSM), F32)),
        (za, y2p, doa, carries, sp["perm"], sp["permt"], sp["mre"], sp["mim"], sp["mtre"], sp["mtim"],
         sp["nre"], sp["nim"], sp["ntre"], sp["ntim"], sp["a"], sp["ap"], sp["dskip"], sp["glu_w"],
         sp["glu_wt"], sp["glu_b"]),
        scratch=[pltpu.VMEM((tb, SSM_LANES), F32), pltpu.VMEM((tb, SSM_LANES), F32),
                 pltpu.VMEM((tb, SSM_LANES), F32), pltpu.VMEM((tb, SSM_LANES), F32),
                 pltpu.VMEM((2, 8, SSM_LANES), F32), pltpu.VMEM((2, SSM_LANES), F32),
                 pltpu.VMEM((2, SSM_LANES), F32)],
        sem=("arbitrary", "arbitrary"), bg=bg)


def _gmlp_spatial(ws_ref, vb):
    lane = lax.broadcasted_iota(jnp.int32, (CHUNK, 128), 1)
    parts = []
    for j in range(GMLP_HEADS // 2):
        vp = vb[:, 128 * j:128 * (j + 1)]
        parts.append(jnp.where(lane < GMLP_HEAD_DIM, _dot(ws_ref[2 * j], vp), _dot(ws_ref[2 * j + 1], vp)))
    return jnp.concatenate(parts, axis=1)


def _gmlp_fwd(zuv, ln_g, ln_b, wsm, bias, bg=None):
    t = zuv.shape[0]

    def body(z_ref, g_ref, b_ref, ws_ref, bias_ref, out_ref, outt_ref):
        u = _gelu(z_ref[:, 0:D_GMLP].astype(F32))
        v0 = _gelu(z_ref[:, D_GMLP:2 * D_GMLP].astype(F32))
        v, _, _ = _ln_fwd(v0, g_ref[...], b_ref[...])
        s = _gmlp_spatial(ws_ref, v.astype(BF16)) + bias_ref[...]
        out = (u * s).astype(BF16)
        out_ref[...] = out
        outt_ref[...] = out.T

    return _call(
        body, "gmlp_fwd", (t // CHUNK,),
        [_rows(CHUNK, 2 * D_GMLP), _fixed((1, D_GMLP)), _fixed((1, D_GMLP)),
         _fixed((GMLP_HEADS, CHUNK, CHUNK)), _fixed((CHUNK, D_GMLP))],
        (_rows(CHUNK, D_GMLP), _cols(D_GMLP, CHUNK)), (SDS((t, D_GMLP), BF16), SDS((D_GMLP, t), BF16)),
        (zuv, ln_g, ln_b, wsm, bias), sem=("parallel",), bg=bg)


def _gmlp_bwd(zuv, dgm, ln_g, ln_b, wsm, wsmt, bias, bg=None):
    t = zuv.shape[0]

    def body(z_ref, d_ref, g_ref, b_ref, ws_ref, wst_ref, bias_ref,
             dz_ref, dws_ref, dbias_ref, dg_ref, db_ref):
        @pl.when(pl.program_id(0) == 0)
        def _():
            for r in (dws_ref, dbias_ref, dg_ref, db_ref):
                r[...] = jnp.zeros_like(r)

        zu = z_ref[:, 0:D_GMLP].astype(F32)
        zv = z_ref[:, D_GMLP:2 * D_GMLP].astype(F32)
        u = _gelu(zu)
        v0 = _gelu(zv)
        gam = g_ref[...]
        v, vhat, rstd = _ln_fwd(v0, gam, b_ref[...])
        vb = v.astype(BF16)
        s = _gmlp_spatial(ws_ref, vb) + bias_ref[...]
        d = d_ref[...]
        dz_ref[:, 0:D_GMLP] = (d * s * _gelu_grad(zu)).astype(BF16)
        ds = d * u
        dbias_ref[...] += ds
        dsb = ds.astype(BF16)
        lane = lax.broadcasted_iota(jnp.int32, (CHUNK, 128), 1)
        tril = (lax.broadcasted_iota(jnp.int32, (CHUNK, CHUNK), 0)
                >= lax.broadcasted_iota(jnp.int32, (CHUNK, CHUNK), 1))
        zero_b = jnp.zeros((CHUNK, 128), BF16)
        parts = []
        for j in range(GMLP_HEADS // 2):
            dsp = dsb[:, 128 * j:128 * (j + 1)]
            vp = vb[:, 128 * j:128 * (j + 1)]
            parts.append(jnp.where(lane < GMLP_HEAD_DIM, _dot(wst_ref[2 * j], dsp),
                                   _dot(wst_ref[2 * j + 1], dsp)))
            lo = jnp.where(lane < GMLP_HEAD_DIM, dsp, zero_b)
            hi = jnp.where(lane < GMLP_HEAD_DIM, zero_b, dsp)
            dws_ref[2 * j] += jnp.where(tril, _dot_nt(lo, vp), 0.0)
            dws_ref[2 * j + 1] += jnp.where(tril, _dot_nt(hi, vp), 0.0)
        dv = jnp.concatenate(parts, axis=1)
        dg_ref[...] += jnp.sum(dv * vhat, axis=0, keepdims=True)
        db_ref[...] += jnp.sum(dv, axis=0, keepdims=True)
        dz_ref[:, D_GMLP:2 * D_GMLP] = (_ln_bwd(dv, vhat, rstd, gam) * _gelu_grad(zv)).astype(BF16)

    return _call(
        body, "gmlp_bwd", (t // CHUNK,),
        [_rows(CHUNK, 2 * D_GMLP), _rows(CHUNK, D_GMLP), _fixed((1, D_GMLP)), _fixed((1, D_GMLP)),
         _fixed((GMLP_HEADS, CHUNK, CHUNK)), _fixed((GMLP_HEADS, CHUNK, CHUNK)), _fixed((CHUNK, D_GMLP))],
        (_rows(CHUNK, 2 * D_GMLP), _fixed((GMLP_HEADS, CHUNK, CHUNK)), _fixed((CHUNK, D_GMLP)),
         _fixed((1, D_GMLP)), _fixed((1, D_GMLP))),
        (SDS((t, 2 * D_GMLP), BF16), SDS((GMLP_HEADS, CHUNK, CHUNK), F32), SDS((CHUNK, D_GMLP), F32),
         SDS((1, D_GMLP), F32), SDS((1, D_GMLP), F32)),
        (zuv, dgm, ln_g, ln_b, wsm, wsmt, bias), sem=("arbitrary",), bg=bg)


def _mixout_fwd(x1, s5o, gm, gab, ua, ub, wmo, g, b, tm, bg=None):
    t = x1.shape[0]

    def body(x_ref, s_ref, m_ref, gab_ref, ua_ref, ub_ref, wmo_ref, g_ref, b_ref,
             xn_ref, xh_ref, rstd_ref):
        ya = _dot(s_ref[...], ua_ref[...])
        yb = _dot(m_ref[...], ub_ref[...])
        mix = (_sigmoid(gab_ref[:, 0:D_MODEL].astype(F32)) * ya
               + _sigmoid(gab_ref[:, D_MODEL:2 * D_MODEL].astype(F32)) * yb)
        r = ALPHA * x_ref[...] + _dot(mix.astype(BF16), wmo_ref[...])
        y, xh, rstd = _ln_fwd(r, g_ref[...], b_ref[...])
        xn_ref[...] = y
        xh_ref[...] = xh
        rstd_ref[...] = rstd

    return _call(
        body, "mixout_fwd", (t // tm,),
        [_rows(tm, D_MODEL), _rows(tm, D_SSM), _rows(tm, D_GMLP), _rows(tm, 2 * D_MODEL),
         _resident((D_SSM, D_MODEL)), _resident((D_GMLP, D_MODEL)), _resident((D_MODEL, D_MODEL)),
         _fixed((1, D_MODEL)), _fixed((1, D_MODEL))],
        (_rows(tm, D_MODEL), _rows(tm, D_MODEL), _rows(tm, 1)),
        (SDS((t, D_MODEL), F32), SDS((t, D_MODEL), F32), SDS((t, 1), F32)),
        (x1, s5o, gm, gab, ua, ub, wmo, g, b), sem=("parallel",), bg=bg)


def _mixout_bwd(dx2, xh, rstd, s5o, gm, gab, ua, ub, wmo, g, tm, bg=None):
    t = dx2.shape[0]

    def body(d_ref, xh_ref, rstd_ref, s_ref, m_ref, gab_ref, ua_ref, ub_ref, wmo_ref, g_ref,
             dx1_ref, dmx_ref, mb_ref, dya_ref, dyb_ref, ds5_ref, dgm_ref, dgab_ref, dg_ref, db_ref):
        @pl.when(pl.program_id(0) == 0)
        def _():
            dg_ref[...] = jnp.zeros_like(dg_ref)
            db_ref[...] = jnp.zeros_like(db_ref)

        dy = d_ref[...]
        xhv = xh_ref[...]
        dr = _ln_bwd(dy, xhv, rstd_ref[...], g_ref[...])
        dg_ref[...] += jnp.sum(dy * xhv, axis=0, keepdims=True)
        db_ref[...] += jnp.sum(dy, axis=0, keepdims=True)
        dx1_ref[...] = ALPHA * dr
        drb = dr.astype(BF16)
        dmx_ref[...] = drb
        dm = _dot_nt(drb, wmo_ref[...])
        ya = _dot(s_ref[...], ua_ref[...])
        yb = _dot(m_ref[...], ub_ref[...])
        sa = _sigmoid(gab_ref[:, 0:D_MODEL].astype(F32))
        sb = _sigmoid(gab_ref[:, D_MODEL:2 * D_MODEL].astype(F32))
        mb_ref[...] = (sa * ya + sb * yb).astype(BF16).T
        dya = (dm * sa).astype(BF16)
        dyb = (dm * sb).astype(BF16)
        dya_ref[...] = dya
        dyb_ref[...] = dyb
        dgab_ref[:, 0:D_MODEL] = (dm * ya * sa * (1.0 - sa)).astype(BF16)
        dgab_ref[:, D_MODEL:2 * D_MODEL] = (dm * yb * sb * (1.0 - sb)).astype(BF16)
        ds5_ref[...] = _dot_nt(dya, ua_ref[...])
        dgm_ref[...] = _dot_nt(dyb, ub_ref[...])

    return _call(
        body, "mixout_bwd", (t // tm,),
        [_rows(tm, D_MODEL), _rows(tm, D_MODEL), _rows(tm, 1), _rows(tm, D_SSM), _rows(tm, D_GMLP),
         _rows(tm, 2 * D_MODEL), _resident((D_SSM, D_MODEL)), _resident((D_GMLP, D_MODEL)),
         _resident((D_MODEL, D_MODEL)), _fixed((1, D_MODEL))],
        (_rows(tm, D_MODEL), _rows(tm, D_MODEL), _cols(D_MODEL, tm), _rows(tm, D_MODEL),
         _rows(tm, D_MODEL), _rows(tm, D_SSM), _rows(tm, D_GMLP), _rows(tm, 2 * D_MODEL),
         _fixed((1, D_MODEL)), _fixed((1, D_MODEL))),
        (SDS((t, D_MODEL), F32), SDS((t, D_MODEL), BF16), SDS((D_MODEL, t), BF16),
         SDS((t, D_MODEL), BF16), SDS((t, D_MODEL), BF16), SDS((t, D_SSM), F32),
         SDS((t, D_GMLP), F32), SDS((t, 2 * D_MODEL), BF16),
         SDS((1, D_MODEL), F32), SDS((1, D_MODEL), F32)),
        (dx2, xh, rstd, s5o, gm, gab, ua, ub, wmo, g), sem=("arbitrary",), bg=bg)


def _ple_loss(x3, p, tgt, wpg, wpp, tm, bg=None):
    t = x3.shape[0]

    def body(x_ref, p_ref, t_ref, wpg_ref, wpp_ref, dx_ref, xb_ref, pb_ref, dq_ref, de_ref, loss_ref):
        @pl.when(pl.program_id(0) == 0)
        def _():
            loss_ref[...] = jnp.zeros_like(loss_ref)

        x3v = x_ref[...]
        xb = x3v.astype(BF16)
        pb = p_ref[...].astype(BF16)
        xb_ref[...] = xb.T
        pb_ref[...] = pb.T
        s = _sigmoid(_dot(xb, wpg_ref[...]))
        e = _dot(pb, wpp_ref[...])
        diff = x3v + s * e - t_ref[...]
        loss_ref[...] += jnp.sum(diff * diff, axis=0, keepdims=True)
        dout = diff * (1.0 / D_MODEL)
        de_ref[...] = (dout * s).astype(BF16)
        dq = (dout * e * s * (1.0 - s)).astype(BF16)
        dq_ref[...] = dq
        dx_ref[...] = dout + _dot_nt(dq, wpg_ref[...])

    return _call(
        body, "ple_loss", (t // tm,),
        [_rows(tm, D_MODEL), _rows(tm, PLE_DIM), _rows(tm, D_MODEL),
         _resident((D_MODEL, D_MODEL)), _resident((PLE_DIM, D_MODEL))],
        (_rows(tm, D_MODEL), _cols(D_MODEL, tm), _cols(PLE_DIM, tm), _rows(tm, D_MODEL),
         _rows(tm, D_MODEL), _fixed((1, D_MODEL))),
        (SDS((t, D_MODEL), F32), SDS((D_MODEL, t), BF16), SDS((PLE_DIM, t), BF16),
         SDS((t, D_MODEL), BF16), SDS((t, D_MODEL), BF16), SDS((1, D_MODEL), F32)),
        (x3, p, tgt, wpg, wpp), sem=("arbitrary",), bg=bg)


def _s5_discretise(lre, lim, log_dt, bre, bim):
    dt = jnp.exp(log_dt)[:, None]
    mag = jnp.exp(lre * dt)
    abr = mag * jnp.cos(lim * dt)
    abi = mag * jnp.sin(lim * dt)
    nr = abr - 1.0
    ni = abi
    den = lre * lre + lim * lim
    cr = ((nr * lre + ni * lim) / den)[..., None]
    ci = ((ni * lre - nr * lim) / den)[..., None]
    return abr, abi, cr * bre - ci * bim, cr * bim + ci * bre


def _block_diag_in(bb):
    v = bb.reshape(S5_BLOCKS, 8, SSM_STATE, SSM_GROUP_CH).transpose(0, 1, 3, 2)
    return jnp.einsum("bgip,gh->bgihp", v, jnp.eye(8, dtype=bb.dtype)).reshape(
        S5_BLOCKS, S5_BLOCK_IN, S5_BLOCK_ST)


def _block_diag_in_t(dm):
    v = dm.reshape(S5_BLOCKS, 8, SSM_GROUP_CH, 8, SSM_STATE)
    d = jnp.einsum("bgihp,gh->bgip", v, jnp.eye(8, dtype=dm.dtype))
    return d.transpose(0, 1, 3, 2).reshape(SSM_GROUPS, SSM_STATE, SSM_GROUP_CH)


def _block_diag_out(cc):
    v = cc.reshape(S5_BLOCKS, 8, SSM_GROUP_CH, SSM_STATE)
    return jnp.einsum("bgip,gh->bgphi", v, jnp.eye(8, dtype=cc.dtype)).reshape(
        S5_BLOCKS, S5_BLOCK_ST, S5_BLOCK_IN)


def _block_diag_out_t(dn):
    v = dn.reshape(S5_BLOCKS, 8, SSM_STATE, 8, SSM_GROUP_CH)
    d = jnp.einsum("bgphi,gh->bgip", v, jnp.eye(8, dtype=dn.dtype))
    return d.reshape(SSM_GROUPS, SSM_GROUP_CH, SSM_STATE)


def _s5_setup(lre, lim, log_dt, bre, bim, cre, cim, d_skip, glu_w, glu_b, tb):
    seg = tb // 8
    abr, abi, bbr, bbi = _s5_discretise(lre, lim, log_dt, bre, bim)
    pr, pi = abr, abi
    for _ in range(int(math.log2(seg))):
        pr, pi = pr * pr - pi * pi, 2.0 * pr * pi
    rows = jnp.arange(tb)
    src = (rows % 8) * seg + rows // 8
    perm = (src[:, None] == jnp.arange(tb)[None, :]).astype(BF16)
    mre = _block_diag_in(bbr)
    mim = _block_diag_in(bbi)
    nre = _block_diag_out(cre)
    nim = _block_diag_out(cim)
    return {
        "perm": perm, "permt": perm.T,
        "mre": mre.astype(BF16), "mim": mim.astype(BF16),
        "mtre": mre.transpose(0, 2, 1).astype(BF16), "mtim": mim.transpose(0, 2, 1).astype(BF16),
        "nre": nre.astype(BF16), "nim": nim.astype(BF16),
        "ntre": nre.transpose(0, 2, 1).astype(BF16), "ntim": nim.transpose(0, 2, 1).astype(BF16),
        "a": jnp.stack([abr.reshape(-1), abi.reshape(-1)]),
        "ap": jnp.stack([pr.reshape(-1), pi.reshape(-1)]),
        "dskip": d_skip.reshape(1, D_SSM), "glu_w": glu_w, "glu_wt": glu_w.T,
        "glu_b": glu_b.reshape(1, D_SSM),
    }


BIG = ("ffn1_w_in", "ffn1_w_out", "mix_w_in", "ssm_glu_w", "up_a", "up_b", "mix_w_out",
       "ffn2_w_in", "ffn2_w_out", "ple_w_proj", "ple_w_gate")
BIG_AXIS = {"ffn1_w_in": 1, "ffn1_w_out": 0, "mix_w_in": 1, "ssm_glu_w": 0, "up_a": 1, "up_b": 1,
            "mix_w_out": 0, "ffn2_w_in": 1, "ffn2_w_out": 0, "ple_w_proj": 1, "ple_w_gate": 0}
SHARD_MAJOR = 2
GATHER_AXIS = dict(BIG_AXIS, ffn1_w_in=SHARD_MAJOR, ffn2_w_in=SHARD_MAJOR)
GATHER_ORDER = (("ffn1_w_in",), ("ffn1_w_out",), ("mix_w_in",), ("ssm_glu_w", "up_a", "up_b", "mix_w_out"),
                ("ffn2_w_in",), ("ffn2_w_out", "ple_w_gate", "ple_w_proj"))
GATHER_FIRST_ID = 1
REDUCE_FIRST_ID = 7
SMALL = ("ln1_g", "ln1_b", "ssm_lambda_re", "ssm_lambda_im", "ssm_log_dt", "ssm_b_re", "ssm_b_im",
         "ssm_c_re", "ssm_c_im", "ssm_d", "ssm_glu_b", "gmlp_ln_g", "gmlp_ln_b", "gmlp_w_s",
         "gmlp_b_s", "ln2_g", "ln2_b", "ln3_g", "ln3_b")
SMALL_VIEW = {"ssm_b_re": (SSM_GROUPS, SSM_STATE * SSM_GROUP_CH), "ssm_b_im": (SSM_GROUPS, SSM_STATE * SSM_GROUP_CH)}


def _small_view(k, a):
    return a.reshape(SMALL_VIEW[k]) if k in SMALL_VIEW else a


def _place():
    return lax.axis_index("x"), lax.axis_index("y"), lax.axis_index("c")


def _other_chips(x, y):
    return [(1 - x, y), (x, 1 - y), (1 - x, 1 - y)]


def _window(ref, shard_shape, axis, chip, half):
    r, c = shard_shape
    hr = r // 2
    if axis == SHARD_MAJOR:
        return ref.at[chip] if half is None else ref.at[chip, pl.ds(half * hr, hr), :]
    if axis == 0:
        if half is None:
            return ref.at[pl.ds(chip * r, r), :]
        return ref.at[pl.ds(chip * r + half * hr, hr), :]
    if half is None:
        return ref.at[:, pl.ds(chip * c, c)]
    return ref.at[pl.ds(half * hr, hr), pl.ds(chip * c, c)]


def _gather_weights(shards, axes):
    n = len(shards)
    shapes = [s.shape for s in shards]
    full = [{0: (4 * r, c), 1: (r, 4 * c), SHARD_MAJOR: (4, r, c)}[ax] for (r, c), ax in zip(shapes, axes)]

    def remote(sems, i, k, src, dst, to):
        return pltpu.make_async_remote_copy(src_ref=src, dst_ref=dst, send_sem=sems[0].at[6 * i + k],
                                            recv_sem=sems[1].at[6 * i + k], device_id=to, device_id_type=MESH)

    def own_copies(ins, outs, sems):
        x, y, c = _place()
        me = 2 * x + y
        cps = []
        for i in range(n):
            hr = shapes[i][0] // 2
            mine = ins[i].at[pl.ds(c * hr, hr), :]
            for j, (cx, cy) in enumerate(_other_chips(x, y)):
                cps.append(remote(sems, i, j, mine, _window(outs[i], shapes[i], axes[i], me, c), (cx, cy, c)))
        local = [pltpu.make_async_copy(ins[i], _window(outs[i], shapes[i], axes[i], me, None), sems[2].at[i])
                 for i in range(n)]
        return cps, local

    def start(ins, outs, sems):
        cps, local = own_copies(ins, outs, sems)
        for cp in local + cps:
            cp.start()

    def finish(ins, outs, sems):
        x, y, c = _place()
        sibling = (x, y, 1 - c)
        passed = []
        for j, (cx, cy) in enumerate(_other_chips(x, y)):
            for i in range(n):
                w = _window(outs[i], shapes[i], axes[i], 2 * cx + cy, c)
                remote(sems, i, j, w, w, (cx, cy, c)).wait_recv()
                cp = remote(sems, i, 3 + j, w, w, sibling)
                cp.start()
                passed.append(cp)
        for j, (cx, cy) in enumerate(_other_chips(x, y)):
            for i in range(n):
                w = _window(outs[i], shapes[i], axes[i], 2 * cx + cy, 1 - c)
                remote(sems, i, 3 + j, w, w, sibling).wait_recv()
        cps, local = own_copies(ins, outs, sems)
        for cp in cps + passed:
            cp.wait_send()
        for cp in local:
            cp.wait()

    return _Exchange(shards, [SDS(f, BF16) for f in full],
                     [pltpu.SemaphoreType.DMA((6 * n,)), pltpu.SemaphoreType.DMA((6 * n,)),
                      pltpu.SemaphoreType.DMA((n,))], start, finish)


def _scatter_grads(parts, shapes, axes):
    n = len(parts)

    def copies(ins, outs, sems):
        x, y, c = _place()
        return [pltpu.make_async_remote_copy(
            src_ref=_window(ins[i], shapes[i], axes[i], 2 * cx + cy, None), dst_ref=outs[i].at[j],
            send_sem=sems[0].at[3 * i + j], recv_sem=sems[1].at[3 * i + j],
            device_id=(cx, cy, c), device_id_type=MESH)
            for i in range(n) for j, (cx, cy) in enumerate(_other_chips(x, y))]

    def start(ins, outs, sems):
        for cp in copies(ins, outs, sems):
            cp.start()

    def finish(ins, outs, sems):
        for cp in copies(ins, outs, sems):
            cp.wait()

    return _Exchange(parts, [SDS((3,) + tuple(s), BF16) for s in shapes],
                     [pltpu.SemaphoreType.DMA((3 * n,)), pltpu.SemaphoreType.DMA((3 * n,))], start, finish)


def _swap_halves(parts, shapes, axes):
    n = len(parts)

    def copies(ins, outs, sems):
        x, y, c = _place()
        cps = []
        for i in range(n):
            r, _ = shapes[i]
            hr = r // 2
            if axes[i] == 0:
                cps += [pltpu.make_async_remote_copy(
                    src_ref=ins[i].at[pl.ds(k * r + (1 - c) * hr, hr), :], dst_ref=outs[i].at[k],
                    send_sem=sems[0].at[i], recv_sem=sems[1].at[i], device_id=(x, y, 1 - c),
                    device_id_type=MESH) for k in range(4)]
            else:
                cps.append(pltpu.make_async_remote_copy(
                    src_ref=ins[i].at[pl.ds((1 - c) * hr, hr), :], dst_ref=outs[i],
                    send_sem=sems[0].at[i], recv_sem=sems[1].at[i], device_id=(x, y, 1 - c),
                    device_id_type=MESH))
        return cps

    def start(ins, outs, sems):
        for cp in copies(ins, outs, sems):
            cp.start()

    def finish(ins, outs, sems):
        x, y, c = _place()
        for i in range(n):
            pltpu.make_async_remote_copy(src_ref=outs[i], dst_ref=outs[i], send_sem=sems[0].at[i],
                                         recv_sem=sems[1].at[i], device_id=(x, y, 1 - c),
                                         device_id_type=MESH).wait()

    out = [SDS((4, r // 2, c), BF16) if ax == 0 else SDS((r // 2, 4 * c), BF16)
           for (r, c), ax in zip(shapes, axes)]
    return _Exchange(parts, out, [pltpu.SemaphoreType.DMA((n,)), pltpu.SemaphoreType.DMA((n,))], start, finish)


def _scatter_halves(pres, shapes):
    n = len(pres)

    def copies(ins, outs, sems):
        x, y, c = _place()
        return [pltpu.make_async_remote_copy(
            src_ref=ins[i].at[1 + j], dst_ref=outs[i].at[j], send_sem=sems[0].at[3 * i + j],
            recv_sem=sems[1].at[3 * i + j], device_id=(cx, cy, c), device_id_type=MESH)
            for i in range(n) for j, (cx, cy) in enumerate(_other_chips(x, y))]

    def start(ins, outs, sems):
        for cp in copies(ins, outs, sems):
            cp.start()

    def finish(ins, outs, sems):
        for cp in copies(ins, outs, sems):
            cp.wait()

    return _Exchange(pres, [SDS((3, r // 2, c), BF16) for r, c in shapes],
                     [pltpu.SemaphoreType.DMA((3 * n,)), pltpu.SemaphoreType.DMA((3 * n,))], start, finish)


def _swap_with_sibling(arrs):
    n = len(arrs)

    def copies(ins, outs, sems):
        x, y, c = _place()
        return [pltpu.make_async_remote_copy(src_ref=ins[i], dst_ref=outs[i], send_sem=sems[0].at[i],
                                             recv_sem=sems[1].at[i], device_id=(x, y, 1 - c),
                                             device_id_type=MESH) for i in range(n)]

    def start(ins, outs, sems):
        for cp in copies(ins, outs, sems):
            cp.start()

    def finish(ins, outs, sems):
        for cp in copies(ins, outs, sems):
            cp.wait()

    return _Exchange(arrs, [SDS(a.shape, a.dtype) for a in arrs],
                     [pltpu.SemaphoreType.DMA((n,)), pltpu.SemaphoreType.DMA((n,))], start, finish)


def _gather_small(arrs):
    n = len(arrs)

    def copy(sems, outs, i, k, block, to, src=None):
        px, py, pc = block
        dst = outs[i].at[4 * px + 2 * py + pc]
        return pltpu.make_async_remote_copy(
            src_ref=dst if src is None else src, dst_ref=dst, send_sem=sems[0].at[7 * i + k],
            recv_sem=sems[1].at[7 * i + k], device_id=to, device_id_type=MESH)

    direct = [math.prod(a.shape) * 4 <= DIRECT_GATHER_BYTES for a in arrs]

    def own_copies(ins, outs, sems):
        x, y, c = _place()
        cps = []
        for i in range(n):
            cps.append(copy(sems, outs, i, 0, (x, y, c), (x, y, 1 - c), src=ins[i]))
            for j, (cx, cy) in enumerate(_other_chips(x, y)):
                cps.append(copy(sems, outs, i, 1 + j, (x, y, c), (cx, cy, c), src=ins[i]))
                if direct[i]:
                    cps.append(copy(sems, outs, i, 4 + j, (x, y, c), (cx, cy, 1 - c), src=ins[i]))
        local = [pltpu.make_async_copy(ins[i], outs[i].at[4 * x + 2 * y + c], sems[2].at[i]) for i in range(n)]
        return cps, local

    def start(ins, outs, sems):
        cps, local = own_copies(ins, outs, sems)
        for cp in local + cps:
            cp.start()

    def finish(ins, outs, sems):
        x, y, c = _place()
        passed = []
        for j, (cx, cy) in enumerate(_other_chips(x, y)):
            for i in range(n):
                copy(sems, outs, i, 1 + j, (cx, cy, c), (x, y, c)).wait_recv()
                if not direct[i]:
                    cp = copy(sems, outs, i, 4 + j, (cx, cy, c), (x, y, 1 - c))
                    cp.start()
                    passed.append(cp)
        for i in range(n):
            copy(sems, outs, i, 0, (x, y, 1 - c), (x, y, c)).wait_recv()
            for j, (cx, cy) in enumerate(_other_chips(x, y)):
                copy(sems, outs, i, 4 + j, (cx, cy, 1 - c), (x, y, c)).wait_recv()
        cps, local = own_copies(ins, outs, sems)
        for cp in cps + passed:
            cp.wait_send()
        for cp in local:
            cp.wait()

    return _Exchange(arrs, [SDS((N_DEV,) + a.shape, F32) for a in arrs],
                     [pltpu.SemaphoreType.DMA((7 * n,)), pltpu.SemaphoreType.DMA((7 * n,)),
                      pltpu.SemaphoreType.DMA((n,))], start, finish)


def _local_step(x, p, tgt, wb, ws, shards=None, opt=None):
    bsz, seq, _ = x.shape
    t = bsz * seq
    tm = min(256, t)
    tb = min(256, seq)
    x0 = x.reshape(t, D_MODEL)
    p0 = p.reshape(t, PLE_DIM)
    tg = tgt.reshape(t, D_MODEL)
    row = lambda v: v.reshape(1, -1)
    dist = shards is not None
    wb = dict(wb)
    recv, sums, other, gathered = {}, {}, {}, {}
    gb = {}
    gs = {}
    shape_of, axis_of = {}, {}
    chip = None
    if dist:
        shape_of = {k: tuple(shards[k].shape) for k in BIG}
        axis_of = dict(BIG_AXIS)
        for q in range(LAST_PIECES):
            shape_of[LAST_PIECE % q] = (D_MODEL // LAST_PIECES, shape_of["ffn1_w_in"][1])
            axis_of[LAST_PIECE % q] = 1
        xi, yi, ci = _place()
        chip = (2 * xi + yi).astype(jnp.int32).reshape(1)
        ids = jnp.stack([2 * xi + yi] + [2 * cx + cy for cx, cy in _other_chips(xi, yi)] + [ci]).astype(jnp.int32)
    halfbuf, pre = {}, {}

    def gather(names):
        return _gather_weights([shards[k] for k in names], [GATHER_AXIS[k] for k in names]) if dist else None

    def exchange(scat=(), swap=(), halves=(), scat2=(), swap2=(), extra=None, after=None):
        if not dist:
            return None, []
        parts, tags = [], []
        if scat:
            parts.append(_scatter_grads([gb[k][1] for k in scat], [shape_of[k] for k in scat],
                                        [axis_of[k] for k in scat]))
            tags.append((recv, scat))
        if swap:
            for k in swap:
                sums[k] = _sum_blocks(gb[k][0], recv[k], shape_of[k], axis_of[k], chip, "sum_" + k, after)
            parts.append(_swap_with_sibling([sums[k] for k in swap]))
            tags.append((other, swap))
        if halves:
            parts.append(_swap_halves([gb[k][1] for k in halves], [shape_of[k] for k in halves],
                                      [axis_of[k] for k in halves]))
            tags.append((halfbuf, halves))
        if scat2:
            for k in scat2:
                pre[k] = _presum(gb[k][0], halfbuf[k], shape_of[k], axis_of[k], ids, "presum_" + k, after)
            parts.append(_scatter_halves([pre[k][1] for k in scat2], [shape_of[k] for k in scat2]))
            tags.append((recv, scat2))
        if swap2:
            for k in swap2:
                sums[k] = _sum_half(pre[k][0], recv[k], "sum_" + k, after)
            parts.append(_swap_with_sibling([sums[k] for k in swap2]))
            tags.append((other, swap2))
        if extra is not None:
            parts.append(extra[0])
            tags.append((extra[1], extra[2]))
        return (_join(parts), tags) if parts else (None, [])

    def take(ex_tags, got):
        ex, tags = ex_tags
        if ex is not None:
            for (dst, names), (o0, o1) in zip(tags, ex.cuts):
                dst.update(zip(names, got[o0:o1]))

    launched = []

    def launch(ex_tags):
        if ex_tags[0] is not None:
            n = len(launched)
            launched.append(n)
            take(ex_tags, _run_exchange_on_sequencer(ex_tags[0], "reduce_%d" % n, REDUCE_FIRST_ID + n))

    small_shape = {k: _small_view(k, v).shape for k, v in ws.items()}
    small_shape["loss_rows"] = (1, D_MODEL)
    ws = {k: v if (v.ndim == 2 and k != "ssm_log_dt") else v[0] for k, v in ws.items()}
    tril = jnp.tril(jnp.ones((CHUNK, CHUNK), dtype=bool))
    wsm = jnp.where(tril[None], ws["gmlp_w_s"], 0.0)
    wsm_b = wsm.astype(BF16)
    wsmt_b = wsm.transpose(0, 2, 1).astype(BF16)
    bias = jnp.repeat(ws["gmlp_b_s"].T, GMLP_HEAD_DIM, axis=1)

    tf = min(512, t)
    if dist:
        for gi, names in enumerate(GATHER_ORDER):
            wb.update(zip(names, _run_exchange_on_sequencer(gather(names), "gather_%d" % gi, GATHER_FIRST_ID + gi)))
    (x0b, h1, a1), _ = _ffn_proj(x0, wb["ffn1_w_in"], tf, "ffn1_proj")
    (x1, xh1, rstd1), _ = _ffn_out(x0, a1, wb["ffn1_w_out"], row(ws["ln1_g"]), row(ws["ln1_b"]), tf, "ffn1_out")
    sp = _s5_setup(ws["ssm_lambda_re"], ws["ssm_lambda_im"], ws["ssm_log_dt"], ws["ssm_b_re"],
                   ws["ssm_b_im"], ws["ssm_c_re"], ws["ssm_c_im"], ws["ssm_d"], wb["ssm_glu_w"],
                   ws["ssm_glu_b"], tb)
    (x1b, za, zuv, gab), _ = _mixin_fwd(x1, wb["mix_w_in"], tm)
    (s5o, s5ot, y2p, carries), _ = _s5_fwd(za, sp, bsz, seq, tb)
    (gm, gmt), _ = _gmlp_fwd(zuv, row(ws["gmlp_ln_g"]), row(ws["gmlp_ln_b"]), wsm_b, bias)
    (x2, xh2, rstd2), _ = _mixout_fwd(x1, s5o, gm, gab, wb["up_a"], wb["up_b"], wb["mix_w_out"],
                                           row(ws["ln2_g"]), row(ws["ln2_b"]), tm)
    (x2b, h2, a2), _ = _ffn_proj(x2, wb["ffn2_w_in"], tf, "ffn2_proj")
    (x3, xh3, rstd3), _ = _ffn_out(x2, a2, wb["ffn2_w_out"], row(ws["ln3_g"]), row(ws["ln3_b"]), tf, "ffn2_out")
    (dx3, x3b, pb, dq, de, loss_rows), _ = _ple_loss(x3, p0, tg, wb["ple_w_gate"], wb["ple_w_proj"], tm)
    gb["ple_w_gate"], _ = _tn_matmul(x3b, dq, "dw_ple_gate", 1024, 1024, a_t=True)
    gb["ple_w_proj"], _ = _tn_matmul(pb, de, "dw_ple_proj", 256, 1024, a_t=True)
    launch(exchange(scat=("ple_w_gate", "ple_w_proj")))
    (dx2, dh2, df2, gs["ln3_g"], gs["ln3_b"]), _ = _ffn_bwd(
        dx3, xh3, rstd3, h2, wb["ffn2_w_in"], wb["ffn2_w_out"], row(ws["ln3_g"]), tm, "ffn2_bwd")
    gb["ffn2_w_out"], _ = _tn_matmul(a2, df2, "dw_ffn2_out", 1408, 1024)
    launch(exchange(scat=("ffn2_w_out",)))
    gb["ffn2_w_in"], _ = _tn_matmul(x2b, dh2, "dw_ffn2_in", 1024, 1408, a_t=True)
    launch(exchange(scat=("ffn2_w_in",), swap=("ple_w_gate", "ple_w_proj"), after=gb["ffn2_w_in"][0]))
    (dx1a, dmx, mb, dya, dyb, ds5, dgm, dgab, gs["ln2_g"], gs["ln2_b"]), _ = _mixout_bwd(
        dx2, xh2, rstd2, s5o, gm, gab, wb["up_a"], wb["up_b"], wb["mix_w_out"], row(ws["ln2_g"]), tm)
    gb["mix_w_out"], _ = _tn_matmul(mb, dmx, "dw_mix_out", 1024, 1024, a_t=True)
    gb["up_a"], _ = _tn_matmul(s5ot, dya, "dw_up_a", 512, 1024, a_t=True)
    gb["up_b"], _ = _tn_matmul(gmt, dyb, "dw_up_b", 512, 1024, a_t=True)
    launch(exchange(scat=("mix_w_out", "up_a", "up_b"), swap=("ffn2_w_out",), after=gb["up_b"][0]))
    (dza, dmr, dmi, dnr, dni, da, ddsk, dgw, dgb), _ = _s5_bwd(za, y2p, ds5, carries, sp, bsz, seq, tb)
    gb["ssm_glu_w"] = (dgw, dgw.astype(BF16))
    launch(exchange(scat=("ssm_glu_w",), swap=("ffn2_w_in",), after=dgw))
    (dzuv, dws, dbias, gs["gmlp_ln_g"], gs["gmlp_ln_b"]), _ = _gmlp_bwd(
        zuv, dgm, row(ws["gmlp_ln_g"]), row(ws["gmlp_ln_b"]), wsm_b, wsmt_b, bias)
    (dx1,), _ = _mixin_bwd(dx1a, dza, dzuv, dgab, wb["mix_w_in"], tm)
    g_mi, _ = _tn_matmul(x1b, dza, "dw_mix_in_a", 1024, 512, 0, 3584, a_t=True)
    g_mi, _ = _tn_matmul(x1b, dzuv, "dw_mix_in_uv", 1024, 512, 1, 3584, g_mi, a_t=True)
    gb["mix_w_in"], _ = _tn_matmul(x1b, dgab, "dw_mix_in_g", 1024, 512, 3, 3584, g_mi, a_t=True)
    launch(exchange(swap=("mix_w_out", "up_a", "up_b", "ssm_glu_w"), after=gb["mix_w_in"][0]))

    d_abr = da[0].sum(axis=0).reshape(SSM_GROUPS, SSM_STATE)
    d_abi = da[1].sum(axis=0).reshape(SSM_GROUPS, SSM_STATE)
    _, vjp = jax.vjp(_s5_discretise, ws["ssm_lambda_re"], ws["ssm_lambda_im"], ws["ssm_log_dt"],
                     ws["ssm_b_re"], ws["ssm_b_im"])
    (gs["ssm_lambda_re"], gs["ssm_lambda_im"], gs["ssm_log_dt"], gs["ssm_b_re"], gs["ssm_b_im"]) = vjp(
        (d_abr, d_abi, _block_diag_in_t(dmr), _block_diag_in_t(dmi)))
    gs["ssm_c_re"] = _block_diag_out_t(dnr)
    gs["ssm_c_im"] = _block_diag_out_t(dni)
    gs["ssm_d"] = ddsk
    gs["ssm_glu_b"] = dgb
    gs["gmlp_w_s"] = dws
    gs["gmlp_b_s"] = dbias.reshape(CHUNK, GMLP_HEADS, GMLP_HEAD_DIM).sum(axis=-1).T
    gs["loss_rows"] = loss_rows

    def small_gather(names):
        return (_gather_small([gs[k].reshape(small_shape[k]) for k in names]), gathered, names) if dist else None

    late = ("ln1_g", "ln1_b")
    launch(exchange(scat=("mix_w_in",), extra=small_gather(tuple(k for k in SMALL + ("loss_rows",) if k not in late))))
    (dx0, dh1, df1, gs["ln1_g"], gs["ln1_b"]), _ = _ffn_bwd(
        dx1, xh1, rstd1, h1, wb["ffn1_w_in"], wb["ffn1_w_out"], row(ws["ln1_g"]), tm, "ffn1_bwd")
    grad_x = dx0.reshape(bsz, seq, D_MODEL)
    if not dist:
        gb["ffn1_w_out"], _ = _tn_matmul(a1, df1, "dw_ffn1_out", 1408, 1024)
        gb["ffn1_w_in"], _ = _tn_matmul(x0b, dh1, "dw_ffn1_in", 1024, 1408, a_t=True)
        return (loss_rows, grad_x, gb, {k: gs[k].reshape(small_shape[k]) for k in SMALL}, sums, other, gathered,
                None, {})
    launch(exchange(extra=small_gather(late)))
    gb["ffn1_w_out"], _ = _tn_matmul(a1, df1, "dw_ffn1_out", 1408, 1024)
    last = ["ffn1_w_out"] + [LAST_PIECE % q for q in range(LAST_PIECES)]
    fillers = (("ffn2_w_in", "mix_w_in", "ple_w_gate"),
               ("ffn2_w_out", "mix_w_out", "up_a", "up_b", "ssm_glu_w", "ple_w_proj"))
    out = {}
    pin = gb["ffn1_w_out"][0]
    for i in range(1, len(last) + 3):
        stage = lambda d: tuple(last[i - d:i - d + 1]) if 0 <= i - d < len(last) else ()
        launch(exchange(halves=stage(1), scat2=stage(2), swap2=stage(3), swap=("mix_w_in",) if i == 2 else (),
                        after=pin))
        if i < len(last):
            gb[last[i]], _ = _tn_matmul(x0b, dh1, "dw_" + last[i], D_MODEL // LAST_PIECES, 1408,
                                        a_cols=(i - 1, 1), a_t=True)
            pin = gb[last[i]][0]
        elif i - len(last) < len(fillers):
            for k in fillers[i - len(last)]:
                w, m, v = opt[k]
                out[k] = _adam_big(w, sums[k], other[k], m, v, "adam_" + k, after=pre[stage(2)[0]][0])
                pin = out[k][1]
    return loss_rows, grad_x, gb, gs, sums, other, gathered, ids, out


def _adamw(w, g, m, v):
    m = ADAM_B1 * m + (1.0 - ADAM_B1) * g
    v = ADAM_B2 * v + (1.0 - ADAM_B2) * (g * g)
    m_hat = m / ADAM_C1
    v_hat = v / ADAM_C2
    delta = -ADAM_LR * (m_hat / (jnp.sqrt(v_hat) + ADAM_EPS) + ADAM_WD * w)
    return delta, m, v


def _pinned(after):
    return ([pl.BlockSpec(memory_space=pl.ANY)], [after]) if after is not None else ([], [])


def _sum_blocks(part, recv, shape, axis, chip, name, after=None):
    r, c = shape
    rb = r // 8

    def body(chip_ref, p_ref, r_ref, *rest):
        rest[-1][...] = (p_ref[...] + r_ref[0].astype(F32) + r_ref[1].astype(F32) + r_ref[2].astype(F32))

    if axis == 0:
        own = pl.BlockSpec((rb, c), lambda i, k: (k[0] * 8 + i, 0))
    else:
        own = pl.BlockSpec((rb, c), lambda i, k: (i, k[0]))
    pin_specs, pin_args = _pinned(after)
    grid_spec = pltpu.PrefetchScalarGridSpec(
        num_scalar_prefetch=1, grid=(8,),
        in_specs=[own, pl.BlockSpec((3, rb, c), lambda i, k: (0, i, 0))] + pin_specs,
        out_specs=pl.BlockSpec((rb, c), lambda i, k: (i, 0)))
    return pl.pallas_call(body, name=name, out_shape=SDS((r, c), F32), grid_spec=grid_spec,
                          compiler_params=_params(("parallel",)))(chip, part, recv, *pin_args)


def _presum(part, half, shape, axis, ids, name, after=None):
    r, c = shape
    rb = r // 4

    def body(ids_ref, p_ref, h_ref, *rest):
        of_ref, ob_ref = rest[-2:]
        s = p_ref[...] + h_ref[...].astype(F32)
        ob_ref[...] = s.astype(BF16)

        @pl.when(pl.program_id(1) == 0)
        def _():
            of_ref[...] = s

    if axis == 0:
        p_spec = pl.BlockSpec((rb, c), lambda i, t, ids: (ids[t] * 4 + ids[4] * 2 + i, 0))
        h_spec = pl.BlockSpec((None, rb, c), lambda i, t, ids: (ids[t], i, 0))
    else:
        p_spec = pl.BlockSpec((rb, c), lambda i, t, ids: (ids[4] * 2 + i, ids[t]))
        h_spec = pl.BlockSpec((rb, c), lambda i, t, ids: (i, ids[t]))
    pin_specs, pin_args = _pinned(after)
    grid_spec = pltpu.PrefetchScalarGridSpec(
        num_scalar_prefetch=1, grid=(2, 4), in_specs=[p_spec, h_spec] + pin_specs,
        out_specs=(pl.BlockSpec((rb, c), lambda i, t, ids: (i, 0)),
                   pl.BlockSpec((None, rb, c), lambda i, t, ids: (t, i, 0))))
    return pl.pallas_call(body, name=name, out_shape=(SDS((r // 2, c), F32), SDS((4, r // 2, c), BF16)),
                          grid_spec=grid_spec,
                          compiler_params=_params(("parallel", "arbitrary")))(ids, part, half, *pin_args)


def _sum_half(pre, recv, name, after=None):
    hr, c = pre.shape
    rb = hr // 2

    def body(p_ref, r_ref, *rest):
        rest[-1][...] = (p_ref[...] + r_ref[0].astype(F32) + r_ref[1].astype(F32) + r_ref[2].astype(F32))

    spec = pl.BlockSpec((rb, c), lambda i: (i, 0))
    pin_specs, pin_args = _pinned(after)
    return pl.pallas_call(body, name=name, grid=(2,), out_shape=SDS((hr, c), F32),
                          in_specs=[spec, pl.BlockSpec((3, rb, c), lambda i: (0, i, 0))] + pin_specs,
                          out_specs=spec, compiler_params=_params(("parallel",)))(pre, recv, *pin_args)


def _adam_halves(w, mine, oth, m, v, ids, name, piece=0, prev=None):
    r, c = w.shape
    rb = mine.shape[0] // 2

    def body(ids_ref, w_ref, a_ref, b_ref, m_ref, v_ref, *rest):
        g_ref, d_ref, nm_ref, nv_ref = rest[-4:]
        g = jnp.where(pl.program_id(0) // 2 == ids_ref[4], a_ref[...], b_ref[...])
        g_ref[...] = g
        d_ref[...], nm_ref[...], nv_ref[...] = _adamw(w_ref[...], g, m_ref[...], v_ref[...])

    whole = pl.BlockSpec((rb, c), lambda i, ids: (i + 4 * piece, 0))
    part = pl.BlockSpec((rb, c), lambda i, ids: (i % 2, 0))
    in_specs = [whole, part, part, whole, whole]
    args = [w, mine, oth, m, v]
    aliases = {}
    if prev is not None:
        in_specs += [pl.BlockSpec(memory_space=pl.ANY)] * 4
        args += list(prev)
        aliases = {6: 0, 7: 1, 8: 2, 9: 3}
    grid_spec = pltpu.PrefetchScalarGridSpec(num_scalar_prefetch=1, grid=(4,), in_specs=in_specs,
                                             out_specs=(whole,) * 4)
    return pl.pallas_call(body, name=name, out_shape=tuple(SDS((r, c), F32) for _ in range(4)),
                          grid_spec=grid_spec, input_output_aliases=aliases,
                          compiler_params=_params(("parallel",)))(ids, *args)


def _adam_big(w, ga, gb, m, v, name, piece=0, prev=None, after=None):
    r, c = w.shape
    pr = ga.shape[0]
    steps = 8 if pr == r else 2
    rb = pr // steps
    off = piece * steps

    def body(w_ref, ga_ref, gb_ref, m_ref, v_ref, *rest):
        g_ref, d_ref, nm_ref, nv_ref = rest[-4:]
        g = ga_ref[...] + gb_ref[...]
        g_ref[...] = g
        d_ref[...], nm_ref[...], nv_ref[...] = _adamw(w_ref[...], g, m_ref[...], v_ref[...])

    whole = pl.BlockSpec((rb, c), lambda i: (i + off, 0))
    part = pl.BlockSpec((rb, c), lambda i: (i, 0))
    in_specs = [whole, part, part, whole, whole]
    args = [w, ga, gb, m, v]
    aliases = {}
    if prev is not None:
        in_specs += [pl.BlockSpec(memory_space=pl.ANY)] * 4
        args += list(prev)
        aliases = {5: 0, 6: 1, 7: 2, 8: 3}
    if after is not None:
        in_specs.append(pl.BlockSpec(memory_space=pl.ANY))
        args.append(after)
    return pl.pallas_call(
        body, name=name, grid=(steps,), out_shape=tuple(SDS((r, c), F32) for _ in range(4)),
        in_specs=in_specs, out_specs=(whole,) * 4, input_output_aliases=aliases,
        compiler_params=_params(("parallel",)),
    )(*args)


def _adam_small(ws, gathered, ms, vs):
    n = len(ws)

    def body(*refs):
        w_refs, g_refs, m_refs, v_refs = refs[:n], refs[n:2 * n], refs[2 * n:3 * n], refs[3 * n:4 * n]
        outs = refs[4 * n:]
        for i in range(n):
            g = g_refs[i][0]
            for d in range(1, N_DEV):
                g = g + g_refs[i][d]
            delta, nm, nv = _adamw(w_refs[i][...], g, m_refs[i][...], v_refs[i][...])
            outs[i][...] = g
            outs[n + i][...] = delta
            outs[2 * n + i][...] = nm
            outs[3 * n + i][...] = nv

    vmem = pl.BlockSpec(memory_space=pltpu.VMEM)
    shapes = [w.shape for w in ws]
    return pl.pallas_call(
        body, name="adam_small", out_shape=tuple(SDS(s, F32) for s in shapes * 4),
        in_specs=[vmem] * (4 * n), out_specs=tuple([vmem] * (4 * n)),
        compiler_params=pltpu.CompilerParams(vmem_limit_bytes=VMEM_LIMIT_BYTES),
    )(*ws, *gathered, *ms, *vs)


def _sum_loss(gathered):
    def body(g_ref, o_ref):
        tot = g_ref[0]
        for d in range(1, N_DEV):
            tot = tot + g_ref[d]
        o_ref[...] = (0.5 / D_MODEL) * jnp.sum(tot, axis=1, keepdims=True)

    vmem = pl.BlockSpec(memory_space=pltpu.VMEM)
    return pl.pallas_call(body, name="sum_loss", out_shape=SDS((1, 1), F32), in_specs=[vmem],
                          out_specs=vmem)(gathered)


def kernel(x, p, ffn1_w_in, ffn1_w_out, ln1_g, ln1_b, mix_w_in, ssm_lambda_re, ssm_lambda_im, ssm_log_dt, ssm_b_re, ssm_b_im, ssm_c_re, ssm_c_im, ssm_d, ssm_glu_w, ssm_glu_b, gmlp_ln_g, gmlp_ln_b, gmlp_w_s, gmlp_b_s, up_a, up_b, mix_w_out, ln2_g, ln2_b, ffn2_w_in, ffn2_w_out, ln3_g, ln3_b, ple_w_proj, ple_w_gate, loss_target, m_ffn1_w_in, m_ffn1_w_out, m_ln1_g, m_ln1_b, m_mix_w_in, m_ssm_lambda_re, m_ssm_lambda_im, m_ssm_log_dt, m_ssm_b_re, m_ssm_b_im, m_ssm_c_re, m_ssm_c_im, m_ssm_d, m_ssm_glu_w, m_ssm_glu_b, m_gmlp_ln_g, m_gmlp_ln_b, m_gmlp_w_s, m_gmlp_b_s, m_up_a, m_up_b, m_mix_w_out, m_ln2_g, m_ln2_b, m_ffn2_w_in, m_ffn2_w_out, m_ln3_g, m_ln3_b, m_ple_w_proj, m_ple_w_gate, v_ffn1_w_in, v_ffn1_w_out, v_ln1_g, v_ln1_b, v_mix_w_in, v_ssm_lambda_re, v_ssm_lambda_im, v_ssm_log_dt, v_ssm_b_re, v_ssm_b_im, v_ssm_c_re, v_ssm_c_im, v_ssm_d, v_ssm_glu_w, v_ssm_glu_b, v_gmlp_ln_g, v_gmlp_ln_b, v_gmlp_w_s, v_gmlp_b_s, v_up_a, v_up_b, v_mix_w_out, v_ln2_g, v_ln2_b, v_ffn2_w_in, v_ffn2_w_out, v_ln3_g, v_ln3_b, v_ple_w_proj, v_ple_w_gate):
    given = dict(locals())
    order = ("ffn1_w_in", "ffn1_w_out", "ln1_g", "ln1_b", "mix_w_in", "ssm_lambda_re", "ssm_lambda_im",
             "ssm_log_dt", "ssm_b_re", "ssm_b_im", "ssm_c_re", "ssm_c_im", "ssm_d", "ssm_glu_w", "ssm_glu_b",
             "gmlp_ln_g", "gmlp_ln_b", "gmlp_w_s", "gmlp_b_s", "up_a", "up_b", "mix_w_out", "ln2_g", "ln2_b",
             "ffn2_w_in", "ffn2_w_out", "ln3_g", "ln3_b", "ple_w_proj", "ple_w_gate")
    assert set(order) == set(BIG + SMALL)

    shard = {k: given[k][0] for k in BIG}
    shard_b = {k: shard[k].astype(BF16) for k in BIG}
    opt = {k: (shard[k], given["m_" + k][0], given["v_" + k][0]) for k in BIG}
    loss_rows, grad_x, gb, gs, sums, other, gathered, ids, out = _local_step(
        x, given["p"][0], loss_target, {}, {k: given[k] for k in SMALL}, shard_b, opt)

    out = dict(out)
    for k in BIG:
        if k in out:
            continue
        moments = (given["m_" + k][0], given["v_" + k][0])
        if k == "ffn1_w_out":
            out[k] = _adam_halves(shard[k], sums[k], other[k], *moments, ids, "adam_" + k)
        elif k == "ffn1_w_in":
            for q in range(LAST_PIECES):
                kq = LAST_PIECE % q
                out[k] = _adam_halves(shard[k], sums[kq], other[kq], *moments, ids, "adam_" + kq, q, out.get(k))
        else:
            out[k] = _adam_big(shard[k], sums[k], other[k], *moments, "adam_" + k,
                               after=gb[LAST_PIECE % (LAST_PIECES - 1)][0])

    res = _adam_small([_small_view(k, given[k]) for k in SMALL], [gathered[k] for k in SMALL],
                      [_small_view(k, given["m_" + k]) for k in SMALL],
                      [_small_view(k, given["v_" + k]) for k in SMALL])
    ns = len(SMALL)
    for i, k in enumerate(SMALL):
        out[k] = tuple(res[j * ns + i].reshape(given[k].shape) for j in range(4))
    loss = _sum_loss(gathered["loss_rows"]).reshape(())

    lead = lambda k, j: out[k][j][None] if k in BIG else out[k][j]
    return (loss, grad_x, *[lead(k, 0) for k in order], *[lead(k, 1) for k in order],
            *[lead(k, 2) for k in order], *[lead(k, 3) for k in order])
```

```python
import math

import jax
import jax.numpy as jnp
from jax import lax
from jax.experimental import pallas as pl
from jax.experimental.pallas import tpu as pltpu
from jax.experimental.pallas import tpu_sc as plsc

F32 = jnp.float32
BF16 = jnp.bfloat16
MESH = pl.DeviceIdType.MESH
SDS = jax.ShapeDtypeStruct

D_MODEL = 1024
D_FF = 2816
D_SSM = 512
D_GMLP = 512
SSM_GROUPS = 32
SSM_GROUP_CH = 16
SSM_STATE = 64
SSM_LANES = SSM_GROUPS * SSM_STATE
GMLP_HEADS = 8
GMLP_HEAD_DIM = 64
CHUNK = 128
PLE_DIM = 256
LN_EPS = 1e-5
ALPHA = 2.0 ** 0.25

ADAM_LR = 0.001
ADAM_B1 = 0.9
ADAM_B2 = 0.999
ADAM_EPS = 1e-08
ADAM_WD = 0.01
ADAM_STEP = 10
ADAM_C1 = 1.0 - ADAM_B1 ** ADAM_STEP
ADAM_C2 = 1.0 - ADAM_B2 ** ADAM_STEP

N_DEV = 8
VMEM_LIMIT_BYTES = 56 * 1024 * 1024
FFN_COLS = 1408
S5_BLOCKS = 4
S5_BLOCK_IN = D_SSM // S5_BLOCKS
S5_BLOCK_ST = SSM_LANES // S5_BLOCKS
SCAN_LANES = 512
TN_K_BLOCK = 2048
DIRECT_GATHER_BYTES = 0
LAST_PIECES = 2
LAST_PIECE = "ffn1_w_in_q%d"
_G0 = math.sqrt(2.0 / math.pi)
_G1 = 0.044715


def _dot(a, b):
    return jnp.dot(a, b, preferred_element_type=F32)


def _dot_nt(a, b):
    return lax.dot_general(a, b, (((1,), (1,)), ((), ())), preferred_element_type=F32)


def _dot_tn(a, b):
    return lax.dot_general(a, b, (((0,), (0,)), ((), ())), preferred_element_type=F32)


def _sigmoid(x):
    return 1.0 / (1.0 + jnp.exp(-x))


def _gelu(x):
    t = jnp.tanh(_G0 * (x + _G1 * x * x * x))
    return 0.5 * x * (1.0 + t)


def _gelu_grad(x):
    t = jnp.tanh(_G0 * (x + _G1 * x * x * x))
    return 0.5 * (1.0 + t) + 0.5 * x * (1.0 - t * t) * _G0 * (1.0 + 3.0 * _G1 * x * x)


def _ln_fwd(r, g, b):
    mu = jnp.mean(r, axis=-1, keepdims=True)
    d = r - mu
    var = jnp.mean(d * d, axis=-1, keepdims=True)
    rstd = lax.rsqrt(var + LN_EPS)
    xh = d * rstd
    return xh * g + b, xh, rstd


def _ln_bwd(dy, xh, rstd, g):
    dxh = dy * g
    m1 = jnp.mean(dxh, axis=-1, keepdims=True)
    m2 = jnp.mean(dxh * xh, axis=-1, keepdims=True)
    return rstd * (dxh - m1 - xh * m2)


def _resident(shape):
    nd = len(shape)
    return pl.BlockSpec(shape, lambda *_: (0,) * nd, pipeline_mode=pl.Buffered(1))


def _fixed(shape):
    nd = len(shape)
    return pl.BlockSpec(shape, lambda *_: (0,) * nd)


def _rows(tm, cols):
    return pl.BlockSpec((tm, cols), lambda i: (i, 0))


def _cols(rows, tm):
    return pl.BlockSpec((rows, tm), lambda i: (0, i))


def _params(sem):
    return pltpu.CompilerParams(dimension_semantics=sem, vmem_limit_bytes=VMEM_LIMIT_BYTES)


class _Exchange:
    def __init__(self, args, out_shape, sems, start, finish):
        self.args, self.out_shape, self.sems = list(args), list(out_shape), list(sems)
        self.start, self.finish = start, finish
        self.cuts = [(0, len(self.out_shape))]


def _call(body, name, grid, in_specs, out_specs, out_shape, args, scratch=(), sem=None, bg=None, aliases=None):
    aliases = {} if aliases is None else aliases
    if bg is not None and not isinstance(bg, _Exchange):
        n_args = len(args)

        def pinned(*refs):
            body(*refs[:n_args], *refs[n_args + 1:])

        res = pl.pallas_call(pinned, name=name, grid=grid, out_shape=tuple(out_shape),
                             in_specs=list(in_specs) + [pl.BlockSpec(memory_space=pl.ANY)],
                             out_specs=tuple(out_specs), scratch_shapes=list(scratch),
                             input_output_aliases=aliases, compiler_params=_params(sem))(*args, bg)
        return tuple(res), ()
    if bg is None:
        res = pl.pallas_call(body, name=name, grid=grid, out_shape=tuple(out_shape), in_specs=list(in_specs),
                             out_specs=tuple(out_specs), scratch_shapes=list(scratch),
                             input_output_aliases=aliases, compiler_params=_params(sem))(*args)
        return tuple(res), ()
    n_in, n_out, n_bi, n_bo, n_sc = len(args), len(out_shape), len(bg.args), len(bg.out_shape), len(scratch)

    def wrapped(*refs):
        ins = refs[:n_in]
        b_ins = refs[n_in:n_in + n_bi]
        outs = refs[n_in + n_bi:n_in + n_bi + n_out]
        b_outs = refs[n_in + n_bi + n_out:n_in + n_bi + n_out + n_bo]
        rest = refs[n_in + n_bi + n_out + n_bo:]
        scr, b_sems = rest[:n_sc], rest[n_sc:]
        first = pl.program_id(0) == 0
        last = pl.program_id(0) == grid[0] - 1
        for ax in range(1, len(grid)):
            first = jnp.logical_and(first, pl.program_id(ax) == 0)
            last = jnp.logical_and(last, pl.program_id(ax) == grid[ax] - 1)

        @pl.when(first)
        def _():
            bg.start(b_ins, b_outs, b_sems)

        body(*ins, *outs, *scr)

        @pl.when(last)
        def _():
            bg.finish(b_ins, b_outs, b_sems)

    any_spec = pl.BlockSpec(memory_space=pl.ANY)
    res = pl.pallas_call(
        wrapped, name=name, grid=grid, out_shape=tuple(out_shape) + tuple(bg.out_shape),
        in_specs=list(in_specs) + [any_spec] * n_bi, out_specs=tuple(out_specs) + (any_spec,) * n_bo,
        scratch_shapes=list(scratch) + list(bg.sems), input_output_aliases=aliases,
        compiler_params=_params(tuple("arbitrary" for _ in grid)))(*args, *bg.args)
    return tuple(res[:n_out]), tuple(res[n_out:])


def _run_exchange(ex, name):
    n_i, n_o = len(ex.args), len(ex.out_shape)

    def body(*refs):
        ins, outs, sems = refs[:n_i], refs[n_i:n_i + n_o], refs[n_i + n_o:]
        ex.start(ins, outs, sems)
        ex.finish(ins, outs, sems)

    any_spec = pl.BlockSpec(memory_space=pl.ANY)
    return tuple(pl.pallas_call(body, name=name, out_shape=tuple(ex.out_shape), in_specs=[any_spec] * n_i,
                                out_specs=(any_spec,) * n_o, scratch_shapes=list(ex.sems))(*ex.args))


def _run_exchange_on_sequencer(ex, name, collective_id):
    n_i, n_o = len(ex.args), len(ex.out_shape)

    def body(*refs):
        ins, outs, sems = refs[:n_i], refs[n_i:n_i + n_o], refs[n_i + n_o:]
        x, y, c = lax.axis_index("x"), lax.axis_index("y"), lax.axis_index("c")
        barrier = pltpu.get_barrier_semaphore()
        for peer in [(x, y, 1 - c), (1 - x, y, c), (x, 1 - y, c), (1 - x, 1 - y, c)]:
            pl.semaphore_signal(barrier, inc=1, device_id=peer, device_id_type=MESH)
        pl.semaphore_wait(barrier, 4)
        ex.start(ins, outs, sems)
        ex.finish(ins, outs, sems)

    return tuple(pl.kernel(body, out_type=tuple(ex.out_shape),
                           mesh=plsc.ScalarSubcoreMesh(axis_name="sequencer", num_cores=1),
                           scratch_types=list(ex.sems), name=name,
                           compiler_params=pltpu.CompilerParams(collective_id=collective_id))(*ex.args))


def _join(exchanges):
    cuts = []
    a = o = q = 0
    for e in exchanges:
        cuts.append((a, a + len(e.args), o, o + len(e.out_shape), q, q + len(e.sems)))
        a, o, q = cuts[-1][1], cuts[-1][3], cuts[-1][5]

    def start(ins, outs, sems):
        for e, (a0, a1, o0, o1, q0, q1) in zip(exchanges, cuts):
            e.start(ins[a0:a1], outs[o0:o1], sems[q0:q1])

    def finish(ins, outs, sems):
        for e, (a0, a1, o0, o1, q0, q1) in zip(exchanges, cuts):
            e.finish(ins[a0:a1], outs[o0:o1], sems[q0:q1])

    joined = _Exchange(sum((e.args for e in exchanges), []), sum((e.out_shape for e in exchanges), []),
                       sum((e.sems for e in exchanges), []), start, finish)
    joined.cuts = [(c[2], c[3]) for c in cuts]
    return joined


def _ffn_proj(x, w_in, tm, name, bg=None):
    t = x.shape[0]
    nch = D_FF // FFN_COLS

    def body(x_ref, win_ref, xbt_ref, h_ref, a_ref):
        xb = x_ref[...].astype(BF16)
        xbt_ref[...] = xb.T
        for k in range(nch):
            cg = slice(k * FFN_COLS, (k + 1) * FFN_COLS)
            cu = slice(D_FF + k * FFN_COLS, D_FF + (k + 1) * FFN_COLS)
            hg = _dot(xb, win_ref[k])
            hu = _dot(xb, win_ref[nch + k])
            h_ref[:, cg] = hg.astype(BF16)
            h_ref[:, cu] = hu.astype(BF16)
            a_ref[:, cg] = (hg * _sigmoid(hg) * hu).astype(BF16)

    return _call(
        body, name, (t // tm,),
        [_rows(tm, D_MODEL), _resident((2 * nch, D_MODEL, FFN_COLS))],
        (_cols(D_MODEL, tm), _rows(tm, 2 * D_FF), _rows(tm, D_FF)),
        (SDS((D_MODEL, t), BF16), SDS((t, 2 * D_FF), BF16), SDS((t, D_FF), BF16)),
        (x, w_in), sem=("parallel",), bg=bg)


def _ffn_out(x, a, w_out, g, b, tm, name, bg=None):
    t = x.shape[0]

    def body(x_ref, a_ref, wout_ref, g_ref, b_ref, xn_ref, xh_ref, rstd_ref):
        f = _dot(a_ref[...], wout_ref[...])
        y, xh, rstd = _ln_fwd(ALPHA * x_ref[...] + 0.5 * f, g_ref[...], b_ref[...])
        xn_ref[...] = y
        xh_ref[...] = xh
        rstd_ref[...] = rstd

    return _call(
        body, name, (t // tm,),
        [_rows(tm, D_MODEL), _rows(tm, D_FF), _resident((D_FF, D_MODEL)), _fixed((1, D_MODEL)), _fixed((1, D_MODEL))],
        (_rows(tm, D_MODEL), _rows(tm, D_MODEL), _rows(tm, 1)),
        (SDS((t, D_MODEL), F32), SDS((t, D_MODEL), F32), SDS((t, 1), F32)),
        (x, a, w_out, g, b), sem=("parallel",), bg=bg)


def _ffn_bwd(dxn, xh, rstd, h, w_in, w_out, g, tm, name, bg=None):
    t = dxn.shape[0]
    nch = D_FF // FFN_COLS

    def body(dxn_ref, xh_ref, rstd_ref, h_ref, win_ref, wout_ref, g_ref,
             dx_ref, dh_ref, df_ref, dg_ref, db_ref):
        @pl.when(pl.program_id(0) == 0)
        def _():
            dg_ref[...] = jnp.zeros_like(dg_ref)
            db_ref[...] = jnp.zeros_like(db_ref)

        dy = dxn_ref[...]
        xhv = xh_ref[...]
        dr = _ln_bwd(dy, xhv, rstd_ref[...], g_ref[...])
        dg_ref[...] += jnp.sum(dy * xhv, axis=0, keepdims=True)
        db_ref[...] += jnp.sum(dy, axis=0, keepdims=True)
        df = (0.5 * dr).astype(BF16)
        df_ref[...] = df
        dx = ALPHA * dr
        for k in range(nch):
            cg = slice(k * FFN_COLS, (k + 1) * FFN_COLS)
            cu = slice(D_FF + k * FFN_COLS, D_FF + (k + 1) * FFN_COLS)
            hg = h_ref[:, cg].astype(F32)
            hu = h_ref[:, cu].astype(F32)
            sg = _sigmoid(hg)
            silu = hg * sg
            da = _dot_nt(df, wout_ref[cg, :])
            dhu = (da * silu).astype(BF16)
            dhg = (da * hu * (sg * (1.0 + hg * (1.0 - sg)))).astype(BF16)
            dh_ref[:, cg] = dhg
            dh_ref[:, cu] = dhu
            dx = dx + _dot_nt(dhg, win_ref[k]) + _dot_nt(dhu, win_ref[nch + k])
        dx_ref[...] = dx

    return _call(
        body, name, (t // tm,),
        [_rows(tm, D_MODEL), _rows(tm, D_MODEL), _rows(tm, 1), _rows(tm, 2 * D_FF),
         _resident((2 * nch, D_MODEL, FFN_COLS)), _resident((D_FF, D_MODEL)), _fixed((1, D_MODEL))],
        (_rows(tm, D_MODEL), _rows(tm, 2 * D_FF), _rows(tm, D_MODEL),
         _fixed((1, D_MODEL)), _fixed((1, D_MODEL))),
        (SDS((t, D_MODEL), F32), SDS((t, 2 * D_FF), BF16), SDS((t, D_MODEL), BF16),
         SDS((1, D_MODEL), F32), SDS((1, D_MODEL), F32)),
        (dxn, xh, rstd, h, w_in, w_out, g), sem=("arbitrary",), bg=bg)


def _tn_matmul(a, b, name, bm, bn, col_block=0, total_cols=None, prev=None, bg=None, a_cols=None, a_t=False):
    t, m = a.shape[::-1] if a_t else a.shape
    a_first = 0
    if a_cols is not None:
        a_first, m = a_cols[0], a_cols[1] * bm
    n = b.shape[1]
    total_cols = n if total_cols is None else total_cols
    bk = min(TN_K_BLOCK, t)
    nk = t // bk
    n_in = 2 if prev is None else 4

    def body(*refs):
        a_ref, b_ref = refs[0], refs[1]
        o_ref, ob_ref = refs[n_in], refs[n_in + 1]
        k = pl.program_id(2)

        @pl.when(k == 0)
        def _():
            o_ref[...] = jnp.zeros_like(o_ref)

        o_ref[...] += _dot(a_ref[...], b_ref[...]) if a_t else _dot_tn(a_ref[...], b_ref[...])

        @pl.when(k == nk - 1)
        def _():
            ob_ref[...] = o_ref[...].astype(BF16)

    a_spec = (pl.BlockSpec((bm, bk), lambda i, j, k: (i + a_first, k)) if a_t
              else pl.BlockSpec((bk, bm), lambda i, j, k: (k, i + a_first)))
    in_specs = [a_spec, pl.BlockSpec((bk, bn), lambda i, j, k: (k, j))]
    args = [a, b]
    aliases = {}
    if prev is not None:
        in_specs += [pl.BlockSpec(memory_space=pl.ANY), pl.BlockSpec(memory_space=pl.ANY)]
        args += list(prev)
        aliases = {2: 0, 3: 1}
    out_spec = pl.BlockSpec((bm, bn), lambda i, j, k: (i, j + col_block))
    return _call(body, name, (m // bm, n // bn, nk), in_specs, (out_spec, out_spec),
                 (SDS((m, total_cols), F32), SDS((m, total_cols), BF16)), args,
                 sem=("parallel", "parallel", "arbitrary"), bg=bg, aliases=aliases)


def _mixin_fwd(x1, w, tm, bg=None):
    t = x1.shape[0]

    def body(x_ref, w_ref, xbt_ref, za_ref, zuv_ref, gab_ref):
        xb = x_ref[...].astype(BF16)
        xbt_ref[...] = xb.T
        za_ref[...] = _dot(xb, w_ref[:, 0:512]).astype(BF16)
        zuv_ref[...] = _dot(xb, w_ref[:, 512:1536]).astype(BF16)
        gab_ref[...] = _dot(xb, w_ref[:, 1536:3584]).astype(BF16)

    return _call(
        body, "mixin_fwd", (t // tm,),
        [_rows(tm, D_MODEL), _resident((D_MODEL, 3584))],
        (_cols(D_MODEL, tm), _rows(tm, 512), _rows(tm, 1024), _rows(tm, 2048)),
        (SDS((D_MODEL, t), BF16), SDS((t, 512), BF16), SDS((t, 1024), BF16), SDS((t, 2048), BF16)),
        (x1, w), sem=("parallel",), bg=bg)


def _mixin_bwd(dx1a, dza, dzuv, dgab, w, tm, bg=None):
    t = dx1a.shape[0]

    def body(d_ref, dza_ref, dzuv_ref, dgab_ref, w_ref, dx_ref):
        dx_ref[...] = (d_ref[...] + _dot_nt(dza_ref[...], w_ref[:, 0:512])
                       + _dot_nt(dzuv_ref[...], w_ref[:, 512:1536])
                       + _dot_nt(dgab_ref[...], w_ref[:, 1536:3584]))

    return _call(
        body, "mixin_bwd", (t // tm,),
        [_rows(tm, D_MODEL), _rows(tm, 512), _rows(tm, 1024), _rows(tm, 2048), _resident((D_MODEL, 3584))],
        (_rows(tm, D_MODEL),), (SDS((t, D_MODEL), F32),),
        (dx1a, dza, dzuv, dgab, w), sem=("parallel",), bg=bg)


def _unrolled(lo, hi, body, carry):
    for j in range(lo, hi):
        carry = body(j, carry)
    return carry


def _scan_fwd(hr_ref, hi_ref, a_ref, ap_ref, carry_ref, seg, cin_ref):
    for lc in range(SSM_LANES // SCAN_LANES):
        ls = slice(lc * SCAN_LANES, (lc + 1) * SCAN_LANES)
        a_r = jnp.broadcast_to(a_ref[0:1, ls], (8, SCAN_LANES))
        a_i = jnp.broadcast_to(a_ref[1:2, ls], (8, SCAN_LANES))

        def step(j, hc, ls=ls, a_r=a_r, a_i=a_i):
            h_r, h_i = hc
            rows = pl.ds(j * 8, 8)
            n_r = a_r * h_r - a_i * h_i + hr_ref[rows, ls]
            n_i = a_r * h_i + a_i * h_r + hi_ref[rows, ls]
            hr_ref[rows, ls] = n_r
            hi_ref[rows, ls] = n_i
            return n_r, n_i

        zero = jnp.zeros((8, SCAN_LANES), F32)
        f_r, f_i = _unrolled(0, seg, step, (zero, zero))
        c_r = carry_ref[0:1, ls]
        c_i = carry_ref[1:2, ls]
        p_r = ap_ref[0:1, ls]
        p_i = ap_ref[1:2, ls]
        rows_r, rows_i = [], []
        for s in range(8):
            rows_r.append(c_r)
            rows_i.append(c_i)
            c_r, c_i = (f_r[s:s + 1] + p_r * c_r - p_i * c_i,
                        f_i[s:s + 1] + p_r * c_i + p_i * c_r)
        carry_ref[0:1, ls] = c_r
        carry_ref[1:2, ls] = c_i
        cin_r = jnp.concatenate(rows_r, axis=0)
        cin_i = jnp.concatenate(rows_i, axis=0)
        if cin_ref is not None:
            cin_ref[0, :, ls] = cin_r
            cin_ref[1, :, ls] = cin_i

        def fix(j, cc, ls=ls, a_r=a_r, a_i=a_i):
            c_r, c_i = cc
            c_r, c_i = a_r * c_r - a_i * c_i, a_r * c_i + a_i * c_r
            rows = pl.ds(j * 8, 8)
            hr_ref[rows, ls] = hr_ref[rows, ls] + c_r
            hi_ref[rows, ls] = hi_ref[rows, ls] + c_i
            return c_r, c_i

        _unrolled(0, seg, fix, (cin_r, cin_i))


def _scan_bwd(gr_ref, gi_ref, hr_ref, hi_ref, cin_ref, a_ref, ap_ref, rcarry_ref, da_ref, seg):
    for lc in range(SSM_LANES // SCAN_LANES):
        ls = slice(lc * SCAN_LANES, (lc + 1) * SCAN_LANES)
        a_r = jnp.broadcast_to(a_ref[0:1, ls], (8, SCAN_LANES))
        a_i = jnp.broadcast_to(a_ref[1:2, ls], (8, SCAN_LANES))

        def step(t, gc, ls=ls, a_r=a_r, a_i=a_i):
            g_r, g_i = gc
            rows = pl.ds((seg - 1 - t) * 8, 8)
            n_r = gr_ref[rows, ls] + a_r * g_r + a_i * g_i
            n_i = gi_ref[rows, ls] + a_r * g_i - a_i * g_r
            gr_ref[rows, ls] = n_r
            gi_ref[rows, ls] = n_i
            return n_r, n_i

        zero = jnp.zeros((8, SCAN_LANES), F32)
        f_r, f_i = _unrolled(0, seg, step, (zero, zero))
        c_r = rcarry_ref[0:1, ls]
        c_i = rcarry_ref[1:2, ls]
        p_r = ap_ref[0:1, ls]
        p_i = ap_ref[1:2, ls]
        rows_r, rows_i = [None] * 8, [None] * 8
        for s in range(7, -1, -1):
            rows_r[s] = c_r
            rows_i[s] = c_i
            c_r, c_i = (f_r[s:s + 1] + p_r * c_r + p_i * c_i,
                        f_i[s:s + 1] + p_r * c_i - p_i * c_r)
        rcarry_ref[0:1, ls] = c_r
        rcarry_ref[1:2, ls] = c_i
        cin_r = jnp.concatenate(rows_r, axis=0)
        cin_i = jnp.concatenate(rows_i, axis=0)

        def fix_row(j_rows, hp_r, hp_i, cc, ls=ls, a_r=a_r, a_i=a_i):
            c_r, c_i, acc_r, acc_i = cc
            c_r, c_i = a_r * c_r + a_i * c_i, a_r * c_i - a_i * c_r
            g_r = gr_ref[j_rows, ls] + c_r
            g_i = gi_ref[j_rows, ls] + c_i
            gr_ref[j_rows, ls] = g_r
            gi_ref[j_rows, ls] = g_i
            acc_r = acc_r + g_r * hp_r + g_i * hp_i
            acc_i = acc_i + g_i * hp_r - g_r * hp_i
            return c_r, c_i, acc_r, acc_i

        def fix(t, cc, ls=ls, fix_row=fix_row):
            j = seg - 1 - t
            rows = pl.ds(j * 8, 8)
            prev = pl.ds((j - 1) * 8, 8)
            return fix_row(rows, hr_ref[prev, ls], hi_ref[prev, ls], cc)

        cc = _unrolled(0, seg - 1, fix, (cin_r, cin_i, zero, zero))
        _, _, acc_r, acc_i = fix_row(pl.ds(0, 8), cin_ref[0, :, ls], cin_ref[1, :, ls], cc)
        da_ref[0, :, ls] += acc_r
        da_ref[1, :, ls] += acc_i


def _s5_fwd(za, sp, bsz, seq, tb, bg=None):
    nb = seq // tb
    seg = tb // 8
    t = bsz * seq

    def body(za_ref, perm_ref, permt_ref, mre_ref, mim_ref, nre_ref, nim_ref, a_ref, ap_ref,
             dsk_ref, gw_ref, gb_ref, out_ref, outt_ref, y2_ref, car_ref, hr_ref, hi_ref, carry_ref):
        @pl.when(pl.program_id(1) == 0)
        def _():
            carry_ref[...] = jnp.zeros_like(carry_ref)

        car_ref[0] = carry_ref[...]
        up = _dot(perm_ref[...], za_ref[...])
        upb = up.astype(BF16)
        for bb in range(S5_BLOCKS):
            ub = upb[:, bb * S5_BLOCK_IN:(bb + 1) * S5_BLOCK_IN]
            st = slice(bb * S5_BLOCK_ST, (bb + 1) * S5_BLOCK_ST)
            hr_ref[:, st] = _dot(ub, mre_ref[bb])
            hi_ref[:, st] = _dot(ub, mim_ref[bb])
        _scan_fwd(hr_ref, hi_ref, a_ref, ap_ref, carry_ref, seg, None)
        ys = []
        for bb in range(S5_BLOCKS):
            st = slice(bb * S5_BLOCK_ST, (bb + 1) * S5_BLOCK_ST)
            ys.append(_dot(hr_ref[:, st].astype(BF16), nre_ref[bb])
                      - _dot(hi_ref[:, st].astype(BF16), nim_ref[bb]))
        y2 = jnp.concatenate(ys, axis=1) + dsk_ref[...] * up
        y2_ref[...] = y2
        y3 = _gelu(y2)
        gl = _dot(y3.astype(BF16), gw_ref[...]) + gb_ref[...]
        oa = y3 * _sigmoid(gl)
        out = _dot(permt_ref[...], oa.astype(BF16)).astype(BF16)
        out_ref[...] = out
        outt_ref[...] = out.T

    blk = pl.BlockSpec((tb, D_SSM), lambda b, j: (b * nb + j, 0))
    blk_t = pl.BlockSpec((D_SSM, tb), lambda b, j: (0, b * nb + j))
    m_shape = (S5_BLOCKS, S5_BLOCK_IN, S5_BLOCK_ST)
    n_shape = (S5_BLOCKS, S5_BLOCK_ST, S5_BLOCK_IN)
    return _call(
        body, "s5_fwd", (bsz, nb),
        [blk, _fixed((tb, tb)), _fixed((tb, tb)), _fixed(m_shape), _fixed(m_shape), _fixed(n_shape),
         _fixed(n_shape), _fixed((2, SSM_LANES)), _fixed((2, SSM_LANES)), _fixed((1, D_SSM)),
         _fixed((D_SSM, D_SSM)), _fixed((1, D_SSM))],
        (blk, blk_t, blk, pl.BlockSpec((1, 2, SSM_LANES), lambda b, j: (b * nb + j, 0, 0))),
        (SDS((t, D_SSM), BF16), SDS((D_SSM, t), BF16), SDS((t, D_SSM), F32), SDS((bsz * nb, 2, SSM_LANES), F32)),
        (za, sp["perm"], sp["permt"], sp["mre"], sp["mim"], sp["nre"], sp["nim"], sp["a"], sp["ap"],
         sp["dskip"], sp["glu_w"], sp["glu_b"]),
        scratch=[pltpu.VMEM((tb, SSM_LANES), F32), pltpu.VMEM((tb, SSM_LANES), F32),
                 pltpu.VMEM((2, SSM_LANES), F32)],
        sem=("arbitrary", "arbitrary"), bg=bg)


def _s5_bwd(za, y2p, doa, carries, sp, bsz, seq, tb, bg=None):
    nb = seq // tb
    seg = tb // 8
    t = bsz * seq

    def body(za_ref, y2_ref, doa_ref, car_ref, perm_ref, permt_ref, mre_ref, mim_ref, mtre_ref, mtim_ref,
             nre_ref, nim_ref, ntre_ref, ntim_ref, a_ref, ap_ref, dsk_ref, gw_ref, gwt_ref, gb_ref,
             dza_ref, dmr_ref, dmi_ref, dnr_ref, dni_ref, da_ref, ddsk_ref, dgw_ref, dgb_ref,
             hr_ref, hi_ref, gr_ref, gi_ref, cin_ref, carry_ref, rcarry_ref):
        first = jnp.logical_and(pl.program_id(0) == 0, pl.program_id(1) == 0)

        @pl.when(first)
        def _():
            for r in (dmr_ref, dmi_ref, dnr_ref, dni_ref, da_ref, ddsk_ref, dgw_ref, dgb_ref):
                r[...] = jnp.zeros_like(r)

        @pl.when(pl.program_id(1) == 0)
        def _():
            rcarry_ref[...] = jnp.zeros_like(rcarry_ref)

        carry_ref[...] = car_ref[0]
        perm = perm_ref[...]
        up = _dot(perm, za_ref[...])
        upb = up.astype(BF16)
        for bb in range(S5_BLOCKS):
            ub = upb[:, bb * S5_BLOCK_IN:(bb + 1) * S5_BLOCK_IN]
            st = slice(bb * S5_BLOCK_ST, (bb + 1) * S5_BLOCK_ST)
            hr_ref[:, st] = _dot(ub, mre_ref[bb])
            hi_ref[:, st] = _dot(ub, mim_ref[bb])
        _scan_fwd(hr_ref, hi_ref, a_ref, ap_ref, carry_ref, seg, cin_ref)

        y2 = y2_ref[...]
        y3 = _gelu(y2)
        y3b = y3.astype(BF16)
        sg = _sigmoid(_dot(y3b, gw_ref[...]) + gb_ref[...])
        d0 = doa_ref[...]
        d_hi = d0.astype(BF16)
        d1 = d0 - d_hi.astype(F32)
        d_mid = d1.astype(BF16)
        d_lo = (d1 - d_mid.astype(F32)).astype(BF16)
        doap = _dot(perm, d_hi) + _dot(perm, d_mid) + _dot(perm, d_lo)
        dgl = doap * y3 * sg * (1.0 - sg)
        dglb = dgl.astype(BF16)
        dy3 = doap * sg + _dot(dglb, gwt_ref[...])
        dgw_ref[...] += _dot_tn(y3b, dglb)
        dgb_ref[...] += jnp.sum(dgl, axis=0, keepdims=True)
        dy2 = dy3 * _gelu_grad(y2)
        ddsk_ref[...] += jnp.sum(dy2 * up, axis=0, keepdims=True)
        dyb = dy2.astype(BF16)
        for bb in range(S5_BLOCKS):
            dyc = dyb[:, bb * S5_BLOCK_IN:(bb + 1) * S5_BLOCK_IN]
            st = slice(bb * S5_BLOCK_ST, (bb + 1) * S5_BLOCK_ST)
            gr_ref[:, st] = _dot(dyc, ntre_ref[bb])
            gi_ref[:, st] = -_dot(dyc, ntim_ref[bb])
            dnr_ref[bb] += _dot_tn(hr_ref[:, st].astype(BF16), dyc)
            dni_ref[bb] += -_dot_tn(hi_ref[:, st].astype(BF16), dyc)
        _scan_bwd(gr_ref, gi_ref, hr_ref, hi_ref, cin_ref, a_ref, ap_ref, rcarry_ref, da_ref, seg)
        dus = []
        for bb in range(S5_BLOCKS):
            st = slice(bb * S5_BLOCK_ST, (bb + 1) * S5_BLOCK_ST)
            grb = gr_ref[:, st].astype(BF16)
            gib = gi_ref[:, st].astype(BF16)
            dus.append(_dot(grb, mtre_ref[bb]) + _dot(gib, mtim_ref[bb]))
            ub = upb[:, bb * S5_BLOCK_IN:(bb + 1) * S5_BLOCK_IN]
            dmr_ref[bb] += _dot_tn(ub, grb)
            dmi_ref[bb] += _dot_tn(ub, gib)
        du = jnp.concatenate(dus, axis=1) + dy2 * dsk_ref[...]
        dza_ref[...] = _dot(permt_ref[...], du.astype(BF16)).astype(BF16)

    def rev(b, j):
        return (b * nb + (nb - 1 - j), 0)

    blk = pl.BlockSpec((tb, D_SSM), rev)
    m_shape = (S5_BLOCKS, S5_BLOCK_IN, S5_BLOCK_ST)
    n_shape = (S5_BLOCKS, S5_BLOCK_ST, S5_BLOCK_IN)
    return _call(
        body, "s5_bwd", (bsz, nb),
        [blk, blk, blk, pl.BlockSpec((1, 2, SSM_LANES), lambda b, j: (b * nb + (nb - 1 - j), 0, 0)),
         _fixed((tb, tb)), _fixed((tb, tb)), _fixed(m_shape), _fixed(m_shape), _fixed(n_shape), _fixed(n_shape),
         _fixed(n_shape), _fixed(n_shape), _fixed(m_shape), _fixed(m_shape),
         _fixed((2, SSM_LANES)), _fixed((2, SSM_LANES)), _fixed((1, D_SSM)),
         _fixed((D_SSM, D_SSM)), _fixed((D_SSM, D_SSM)), _fixed((1, D_SSM))],
        (blk, _fixed(m_shape), _fixed(m_shape), _fixed(n_shape), _fixed(n_shape),
         _fixed((2, 8, SSM_LANES)), _fixed((1, D_SSM)), _fixed((D_SSM, D_SSM)), _fixed((1, D_SSM))),
        (SDS((t, D_SSM), BF16), SDS(m_shape, F32), SDS(m_shape, F32), SDS(n_shape, F32), SDS(n_shape, F32),
         SDS((2, 8, SSM_LANES), F32), SDS((1, D_SSM), F32), SDS((D_SSM, D_SSM), F32), SDS((1, D_SSM), F32)),
        (za, y2p, doa, carries, sp["perm"], sp["permt"], sp["mre"], sp["mim"], sp["mtre"], sp["mtim"],
         sp["nre"], sp["nim"], sp["ntre"], sp["ntim"], sp["a"], sp["ap"], sp["dskip"], sp["glu_w"],
         sp["glu_wt"], sp["glu_b"]),
        scratch=[pltpu.VMEM((tb, SSM_LANES), F32), pltpu.VMEM((tb, SSM_LANES), F32),
                 pltpu.VMEM((tb, SSM_LANES), F32), pltpu.VMEM((tb, SSM_LANES), F32),
                 pltpu.VMEM((2, 8, SSM_LANES), F32), pltpu.VMEM((2, SSM_LANES), F32),
                 pltpu.VMEM((2, SSM_LANES), F32)],
        sem=("arbitrary", "arbitrary"), bg=bg)


def _gmlp_spatial(ws_ref, vb):
    lane = lax.broadcasted_iota(jnp.int32, (CHUNK, 128), 1)
    parts = []
    for j in range(GMLP_HEADS // 2):
        vp = vb[:, 128 * j:128 * (j + 1)]
        parts.append(jnp.where(lane < GMLP_HEAD_DIM, _dot(ws_ref[2 * j], vp), _dot(ws_ref[2 * j + 1], vp)))
    return jnp.concatenate(parts, axis=1)


def _gmlp_fwd(zuv, ln_g, ln_b, wsm, bias, bg=None):
    t = zuv.shape[0]

    def body(z_ref, g_ref, b_ref, ws_ref, bias_ref, out_ref, outt_ref):
        u = _gelu(z_ref[:, 0:D_GMLP].astype(F32))
        v0 = _gelu(z_ref[:, D_GMLP:2 * D_GMLP].astype(F32))
        v, _, _ = _ln_fwd(v0, g_ref[...], b_ref[...])
        s = _gmlp_spatial(ws_ref, v.astype(BF16)) + bias_ref[...]
        out = (u * s).astype(BF16)
        out_ref[...] = out
        outt_ref[...] = out.T

    return _call(
        body, "gmlp_fwd", (t // CHUNK,),
        [_rows(CHUNK, 2 * D_GMLP), _fixed((1, D_GMLP)), _fixed((1, D_GMLP)),
         _fixed((GMLP_HEADS, CHUNK, CHUNK)), _fixed((CHUNK, D_GMLP))],
        (_rows(CHUNK, D_GMLP), _cols(D_GMLP, CHUNK)), (SDS((t, D_GMLP), BF16), SDS((D_GMLP, t), BF16)),
        (zuv, ln_g, ln_b, wsm, bias), sem=("parallel",), bg=bg)


def _gmlp_bwd(zuv, dgm, ln_g, ln_b, wsm, wsmt, bias, bg=None):
    t = zuv.shape[0]

    def body(z_ref, d_ref, g_ref, b_ref, ws_ref, wst_ref, bias_ref,
             dz_ref, dws_ref, dbias_ref, dg_ref, db_ref):
        @pl.when(pl.program_id(0) == 0)
        def _():
            for r in (dws_ref, dbias_ref, dg_ref, db_ref):
                r[...] = jnp.zeros_like(r)

        zu = z_ref[:, 0:D_GMLP].astype(F32)
        zv = z_ref[:, D_GMLP:2 * D_GMLP].astype(F32)
        u = _gelu(zu)
        v0 = _gelu(zv)
        gam = g_ref[...]
        v, vhat, rstd = _ln_fwd(v0, gam, b_ref[...])
        vb = v.astype(BF16)
        s = _gmlp_spatial(ws_ref, vb) + bias_ref[...]
        d = d_ref[...]
        dz_ref[:, 0:D_GMLP] = (d * s * _gelu_grad(zu)).astype(BF16)
        ds = d * u
        dbias_ref[...] += ds
        dsb = ds.astype(BF16)
        lane = lax.broadcasted_iota(jnp.int32, (CHUNK, 128), 1)
        tril = (lax.broadcasted_iota(jnp.int32, (CHUNK, CHUNK), 0)
                >= lax.broadcasted_iota(jnp.int32, (CHUNK, CHUNK), 1))
        zero_b = jnp.zeros((CHUNK, 128), BF16)
        parts = []
        for j in range(GMLP_HEADS // 2):
            dsp = dsb[:, 128 * j:128 * (j + 1)]
            vp = vb[:, 128 * j:128 * (j + 1)]
            parts.append(jnp.where(lane < GMLP_HEAD_DIM, _dot(wst_ref[2 * j], dsp),
                                   _dot(wst_ref[2 * j + 1], dsp)))
            lo = jnp.where(lane < GMLP_HEAD_DIM, dsp, zero_b)
            hi = jnp.where(lane < GMLP_HEAD_DIM, zero_b, dsp)
            dws_ref[2 * j] += jnp.where(tril, _dot_nt(lo, vp), 0.0)
            dws_ref[2 * j + 1] += jnp.where(tril, _dot_nt(hi, vp), 0.0)
        dv = jnp.concatenate(parts, axis=1)
        dg_ref[...] += jnp.sum(dv * vhat, axis=0, keepdims=True)
        db_ref[...] += jnp.sum(dv, axis=0, keepdims=True)
        dz_ref[:, D_GMLP:2 * D_GMLP] = (_ln_bwd(dv, vhat, rstd, gam) * _gelu_grad(zv)).astype(BF16)

    return _call(
        body, "gmlp_bwd", (t // CHUNK,),
        [_rows(CHUNK, 2 * D_GMLP), _rows(CHUNK, D_GMLP), _fixed((1, D_GMLP)), _fixed((1, D_GMLP)),
         _fixed((GMLP_HEADS, CHUNK, CHUNK)), _fixed((GMLP_HEADS, CHUNK, CHUNK)), _fixed((CHUNK, D_GMLP))],
        (_rows(CHUNK, 2 * D_GMLP), _fixed((GMLP_HEADS, CHUNK, CHUNK)), _fixed((CHUNK, D_GMLP)),
         _fixed((1, D_GMLP)), _fixed((1, D_GMLP))),
        (SDS((t, 2 * D_GMLP), BF16), SDS((GMLP_HEADS, CHUNK, CHUNK), F32), SDS((CHUNK, D_GMLP), F32),
         SDS((1, D_GMLP), F32), SDS((1, D_GMLP), F32)),
        (zuv, dgm, ln_g, ln_b, wsm, wsmt, bias), sem=("arbitrary",), bg=bg)


def _mixout_fwd(x1, s5o, gm, gab, ua, ub, wmo, g, b, tm, bg=None):
    t = x1.shape[0]

    def body(x_ref, s_ref, m_ref, gab_ref, ua_ref, ub_ref, wmo_ref, g_ref, b_ref,
             xn_ref, xh_ref, rstd_ref):
        ya = _dot(s_ref[...], ua_ref[...])
        yb = _dot(m_ref[...], ub_ref[...])
        mix = (_sigmoid(gab_ref[:, 0:D_MODEL].astype(F32)) * ya
               + _sigmoid(gab_ref[:, D_MODEL:2 * D_MODEL].astype(F32)) * yb)
        r = ALPHA * x_ref[...] + _dot(mix.astype(BF16), wmo_ref[...])
        y, xh, rstd = _ln_fwd(r, g_ref[...], b_ref[...])
        xn_ref[...] = y
        xh_ref[...] = xh
        rstd_ref[...] = rstd

    return _call(
        body, "mixout_fwd", (t // tm,),
        [_rows(tm, D_MODEL), _rows(tm, D_SSM), _rows(tm, D_GMLP), _rows(tm, 2 * D_MODEL),
         _resident((D_SSM, D_MODEL)), _resident((D_GMLP, D_MODEL)), _resident((D_MODEL, D_MODEL)),
         _fixed((1, D_MODEL)), _fixed((1, D_MODEL))],
        (_rows(tm, D_MODEL), _rows(tm, D_MODEL), _rows(tm, 1)),
        (SDS((t, D_MODEL), F32), SDS((t, D_MODEL), F32), SDS((t, 1), F32)),
        (x1, s5o, gm, gab, ua, ub, wmo, g, b), sem=("parallel",), bg=bg)


def _mixout_bwd(dx2, xh, rstd, s5o, gm, gab, ua, ub, wmo, g, tm, bg=None):
    t = dx2.shape[0]

    def body(d_ref, xh_ref, rstd_ref, s_ref, m_ref, gab_ref, ua_ref, ub_ref, wmo_ref, g_ref,
             dx1_ref, dmx_ref, mb_ref, dya_ref, dyb_ref, ds5_ref, dgm_ref, dgab_ref, dg_ref, db_ref):
        @pl.when(pl.program_id(0) == 0)
        def _():
            dg_ref[...] = jnp.zeros_like(dg_ref)
            db_ref[...] = jnp.zeros_like(db_ref)

        dy = d_ref[...]
        xhv = xh_ref[...]
        dr = _ln_bwd(dy, xhv, rstd_ref[...], g_ref[...])
        dg_ref[...] += jnp.sum(dy * xhv, axis=0, keepdims=True)
        db_ref[...] += jnp.sum(dy, axis=0, keepdims=True)
        dx1_ref[...] = ALPHA * dr
        drb = dr.astype(BF16)
        dmx_ref[...] = drb
        dm = _dot_nt(drb, wmo_ref[...])
        ya = _dot(s_ref[...], ua_ref[...])
        yb = _dot(m_ref[...], ub_ref[...])
        sa = _sigmoid(gab_ref[:, 0:D_MODEL].astype(F32))
        sb = _sigmoid(gab_ref[:, D_MODEL:2 * D_MODEL].astype(F32))
        mb_ref[...] = (sa * ya + sb * yb).astype(BF16).T
        dya = (dm * sa).astype(BF16)
        dyb = (dm * sb).astype(BF16)
        dya_ref[...] = dya
        dyb_ref[...] = dyb
        dgab_ref[:, 0:D_MODEL] = (dm * ya * sa * (1.0 - sa)).astype(BF16)
        dgab_ref[:, D_MODEL:2 * D_MODEL] = (dm * yb * sb * (1.0 - sb)).astype(BF16)
        ds5_ref[...] = _dot_nt(dya, ua_ref[...])
        dgm_ref[...] = _dot_nt(dyb, ub_ref[...])

    return _call(
        body, "mixout_bwd", (t // tm,),
        [_rows(tm, D_MODEL), _rows(tm, D_MODEL), _rows(tm, 1), _rows(tm, D_SSM), _rows(tm, D_GMLP),
         _rows(tm, 2 * D_MODEL), _resident((D_SSM, D_MODEL)), _resident((D_GMLP, D_MODEL)),
         _resident((D_MODEL, D_MODEL)), _fixed((1, D_MODEL))],
        (_rows(tm, D_MODEL), _rows(tm, D_MODEL), _cols(D_MODEL, tm), _rows(tm, D_MODEL),
         _rows(tm, D_MODEL), _rows(tm, D_SSM), _rows(tm, D_GMLP), _rows(tm, 2 * D_MODEL),
         _fixed((1, D_MODEL)), _fixed((1, D_MODEL))),
        (SDS((t, D_MODEL), F32), SDS((t, D_MODEL), BF16), SDS((D_MODEL, t), BF16),
         SDS((t, D_MODEL), BF16), SDS((t, D_MODEL), BF16), SDS((t, D_SSM), F32),
         SDS((t, D_GMLP), F32), SDS((t, 2 * D_MODEL), BF16),
         SDS((1, D_MODEL), F32), SDS((1, D_MODEL), F32)),
        (dx2, xh, rstd, s5o, gm, gab, ua, ub, wmo, g), sem=("arbitrary",), bg=bg)


def _ple_loss(x3, p, tgt, wpg, wpp, tm, bg=None):
    t = x3.shape[0]

    def body(x_ref, p_ref, t_ref, wpg_ref, wpp_ref, dx_ref, xb_ref, pb_ref, dq_ref, de_ref, loss_ref):
        @pl.when(pl.program_id(0) == 0)
        def _():
            loss_ref[...] = jnp.zeros_like(loss_ref)

        x3v = x_ref[...]
        xb = x3v.astype(BF16)
        pb = p_ref[...].astype(BF16)
        xb_ref[...] = xb.T
        pb_ref[...] = pb.T
        s = _sigmoid(_dot(xb, wpg_ref[...]))
        e = _dot(pb, wpp_ref[...])
        diff = x3v + s * e - t_ref[...]
        loss_ref[...] += jnp.sum(diff * diff, axis=0, keepdims=True)
        dout = diff * (1.0 / D_MODEL)
        de_ref[...] = (dout * s).astype(BF16)
        dq = (dout * e * s * (1.0 - s)).astype(BF16)
        dq_ref[...] = dq
        dx_ref[...] = dout + _dot_nt(dq, wpg_ref[...])

    return _call(
        body, "ple_loss", (t // tm,),
        [_rows(tm, D_MODEL), _rows(tm, PLE_DIM), _rows(tm, D_MODEL),
         _resident((D_MODEL, D_MODEL)), _resident((PLE_DIM, D_MODEL))],
        (_rows(tm, D_MODEL), _cols(D_MODEL, tm), _cols(PLE_DIM, tm), _rows(tm, D_MODEL),
         _rows(tm, D_MODEL), _fixed((1, D_MODEL))),
        (SDS((t, D_MODEL), F32), SDS((D_MODEL, t), BF16), SDS((PLE_DIM, t), BF16),
         SDS((t, D_MODEL), BF16), SDS((t, D_MODEL), BF16), SDS((1, D_MODEL), F32)),
        (x3, p, tgt, wpg, wpp), sem=("arbitrary",), bg=bg)


def _s5_discretise(lre, lim, log_dt, bre, bim):
    dt = jnp.exp(log_dt)[:, None]
    mag = jnp.exp(lre * dt)
    abr = mag * jnp.cos(lim * dt)
    abi = mag * jnp.sin(lim * dt)
    nr = abr - 1.0
    ni = abi
    den = lre * lre + lim * lim
    cr = ((nr * lre + ni * lim) / den)[..., None]
    ci = ((ni * lre - nr * lim) / den)[..., None]
    return abr, abi, cr * bre - ci * bim, cr * bim + ci * bre


def _block_diag_in(bb):
    v = bb.reshape(S5_BLOCKS, 8, SSM_STATE, SSM_GROUP_CH).transpose(0, 1, 3, 2)
    return jnp.einsum("bgip,gh->bgihp", v, jnp.eye(8, dtype=bb.dtype)).reshape(
        S5_BLOCKS, S5_BLOCK_IN, S5_BLOCK_ST)


def _block_diag_in_t(dm):
    v = dm.reshape(S5_BLOCKS, 8, SSM_GROUP_CH, 8, SSM_STATE)
    d = jnp.einsum("bgihp,gh->bgip", v, jnp.eye(8, dtype=dm.dtype))
    return d.transpose(0, 1, 3, 2).reshape(SSM_GROUPS, SSM_STATE, SSM_GROUP_CH)


def _block_diag_out(cc):
    v = cc.reshape(S5_BLOCKS, 8, SSM_GROUP_CH, SSM_STATE)
    return jnp.einsum("bgip,gh->bgphi", v, jnp.eye(8, dtype=cc.dtype)).reshape(
        S5_BLOCKS, S5_BLOCK_ST, S5_BLOCK_IN)


def _block_diag_out_t(dn):
    v = dn.reshape(S5_BLOCKS, 8, SSM_STATE, 8, SSM_GROUP_CH)
    d = jnp.einsum("bgphi,gh->bgip", v, jnp.eye(8, dtype=dn.dtype))
    return d.reshape(SSM_GROUPS, SSM_GROUP_CH, SSM_STATE)


def _s5_setup(lre, lim, log_dt, bre, bim, cre, cim, d_skip, glu_w, glu_b, tb):
    seg = tb // 8
    abr, abi, bbr, bbi = _s5_discretise(lre, lim, log_dt, bre, bim)
    pr, pi = abr, abi
    for _ in range(int(math.log2(seg))):
        pr, pi = pr * pr - pi * pi, 2.0 * pr * pi
    rows = jnp.arange(tb)
    src = (rows % 8) * seg + rows // 8
    perm = (src[:, None] == jnp.arange(tb)[None, :]).astype(BF16)
    mre = _block_diag_in(bbr)
    mim = _block_diag_in(bbi)
    nre = _block_diag_out(cre)
    nim = _block_diag_out(cim)
    return {
        "perm": perm, "permt": perm.T,
        "mre": mre.astype(BF16), "mim": mim.astype(BF16),
        "mtre": mre.transpose(0, 2, 1).astype(BF16), "mtim": mim.transpose(0, 2, 1).astype(BF16),
        "nre": nre.astype(BF16), "nim": nim.astype(BF16),
        "ntre": nre.transpose(0, 2, 1).astype(BF16), "ntim": nim.transpose(0, 2, 1).astype(BF16),
        "a": jnp.stack([abr.reshape(-1), abi.reshape(-1)]),
        "ap": jnp.stack([pr.reshape(-1), pi.reshape(-1)]),
        "dskip": d_skip.reshape(1, D_SSM), "glu_w": glu_w, "glu_wt": glu_w.T,
        "glu_b": glu_b.reshape(1, D_SSM),
    }


BIG = ("ffn1_w_in", "ffn1_w_out", "mix_w_in", "ssm_glu_w", "up_a", "up_b", "mix_w_out",
       "ffn2_w_in", "ffn2_w_out", "ple_w_proj", "ple_w_gate")
BIG_AXIS = {"ffn1_w_in": 1, "ffn1_w_out": 0, "mix_w_in": 1, "ssm_glu_w": 0, "up_a": 1, "up_b": 1,
            "mix_w_out": 0, "ffn2_w_in": 1, "ffn2_w_out": 0, "ple_w_proj": 1, "ple_w_gate": 0}
SHARD_MAJOR = 2
GATHER_AXIS = dict(BIG_AXIS, ffn1_w_in=SHARD_MAJOR, ffn2_w_in=SHARD_MAJOR)
GATHER_ORDER = (("ffn1_w_in",), ("ffn1_w_out",), ("mix_w_in",), ("ssm_glu_w", "up_a", "up_b", "mix_w_out"),
                ("ffn2_w_in",), ("ffn2_w_out", "ple_w_gate", "ple_w_proj"))
GATHER_FIRST_ID = 1
REDUCE_FIRST_ID = 7
SMALL = ("ln1_g", "ln1_b", "ssm_lambda_re", "ssm_lambda_im", "ssm_log_dt", "ssm_b_re", "ssm_b_im",
         "ssm_c_re", "ssm_c_im", "ssm_d", "ssm_glu_b", "gmlp_ln_g", "gmlp_ln_b", "gmlp_w_s",
         "gmlp_b_s", "ln2_g", "ln2_b", "ln3_g", "ln3_b")
SMALL_VIEW = {"ssm_b_re": (SSM_GROUPS, SSM_STATE * SSM_GROUP_CH), "ssm_b_im": (SSM_GROUPS, SSM_STATE * SSM_GROUP_CH)}


def _small_view(k, a):
    return a.reshape(SMALL_VIEW[k]) if k in SMALL_VIEW else a


def _place():
    return lax.axis_index("x"), lax.axis_index("y"), lax.axis_index("c")


def _other_chips(x, y):
    return [(1 - x, y), (x, 1 - y), (1 - x, 1 - y)]


def _window(ref, shard_shape, axis, chip, half):
    r, c = shard_shape
    hr = r // 2
    if axis == SHARD_MAJOR:
        return ref.at[chip] if half is None else ref.at[chip, pl.ds(half * hr, hr), :]
    if axis == 0:
        if half is None:
            return ref.at[pl.ds(chip * r, r), :]
        return ref.at[pl.ds(chip * r + half * hr, hr), :]
    if half is None:
        return ref.at[:, pl.ds(chip * c, c)]
    return ref.at[pl.ds(half * hr, hr), pl.ds(chip * c, c)]


def _gather_weights(shards, axes):
    n = len(shards)
    shapes = [s.shape for s in shards]
    full = [{0: (4 * r, c), 1: (r, 4 * c), SHARD_MAJOR: (4, r, c)}[ax] for (r, c), ax in zip(shapes, axes)]

    def remote(sems, i, k, src, dst, to):
        return pltpu.make_async_remote_copy(src_ref=src, dst_ref=dst, send_sem=sems[0].at[6 * i + k],
                                            recv_sem=sems[1].at[6 * i + k], device_id=to, device_id_type=MESH)

    def own_copies(ins, outs, sems):
        x, y, c = _place()
        me = 2 * x + y
        cps = []
        for i in range(n):
            hr = shapes[i][0] // 2
            mine = ins[i].at[pl.ds(c * hr, hr), :]
            for j, (cx, cy) in enumerate(_other_chips(x, y)):
                cps.append(remote(sems, i, j, mine, _window(outs[i], shapes[i], axes[i], me, c), (cx, cy, c)))
        local = [pltpu.make_async_copy(ins[i], _window(outs[i], shapes[i], axes[i], me, None), sems[2].at[i])
                 for i in range(n)]
        return cps, local

    def start(ins, outs, sems):
        cps, local = own_copies(ins, outs, sems)
        for cp in local + cps:
            cp.start()

    def finish(ins, outs, sems):
        x, y, c = _place()
        sibling = (x, y, 1 - c)
        passed = []
        for j, (cx, cy) in enumerate(_other_chips(x, y)):
            for i in range(n):
                w = _window(outs[i], shapes[i], axes[i], 2 * cx + cy, c)
                remote(sems, i, j, w, w, (cx, cy, c)).wait_recv()
                cp = remote(sems, i, 3 + j, w, w, sibling)
                cp.start()
                passed.append(cp)
        for j, (cx, cy) in enumerate(_other_chips(x, y)):
            for i in range(n):
                w = _window(outs[i], shapes[i], axes[i], 2 * cx + cy, 1 - c)
                remote(sems, i, 3 + j, w, w, sibling).wait_recv()
        cps, local = own_copies(ins, outs, sems)
        for cp in cps + passed:
            cp.wait_send()
        for cp in local:
            cp.wait()

    return _Exchange(shards, [SDS(f, BF16) for f in full],
                     [pltpu.SemaphoreType.DMA((6 * n,)), pltpu.SemaphoreType.DMA((6 * n,)),
                      pltpu.SemaphoreType.DMA((n,))], start, finish)


def _scatter_grads(parts, shapes, axes):
    n = len(parts)

    def copies(ins, outs, sems):
        x, y, c = _place()
        return [pltpu.make_async_remote_copy(
            src_ref=_window(ins[i], shapes[i], axes[i], 2 * cx + cy, None), dst_ref=outs[i].at[j],
            send_sem=sems[0].at[3 * i + j], recv_sem=sems[1].at[3 * i + j],
            device_id=(cx, cy, c), device_id_type=MESH)
            for i in range(n) for j, (cx, cy) in enumerate(_other_chips(x, y))]

    def start(ins, outs, sems):
        for cp in copies(ins, outs, sems):
            cp.start()

    def finish(ins, outs, sems):
        for cp in copies(ins, outs, sems):
            cp.wait()

    return _Exchange(parts, [SDS((3,) + tuple(s), BF16) for s in shapes],
                     [pltpu.SemaphoreType.DMA((3 * n,)), pltpu.SemaphoreType.DMA((3 * n,))], start, finish)


def _swap_halves(parts, shapes, axes):
    n = len(parts)

    def copies(ins, outs, sems):
        x, y, c = _place()
        cps = []
        for i in range(n):
            r, _ = shapes[i]
            hr = r // 2
            if axes[i] == 0:
                cps += [pltpu.make_async_remote_copy(
                    src_ref=ins[i].at[pl.ds(k * r + (1 - c) * hr, hr), :], dst_ref=outs[i].at[k],
                    send_sem=sems[0].at[i], recv_sem=sems[1].at[i], device_id=(x, y, 1 - c),
                    device_id_type=MESH) for k in range(4)]
            else:
                cps.append(pltpu.make_async_remote_copy(
                    src_ref=ins[i].at[pl.ds((1 - c) * hr, hr), :], dst_ref=outs[i],
                    send_sem=sems[0].at[i], recv_sem=sems[1].at[i], device_id=(x, y, 1 - c),
                    device_id_type=MESH))
        return cps

    def start(ins, outs, sems):
        for cp in copies(ins, outs, sems):
            cp.start()

    def finish(ins, outs, sems):
        x, y, c = _place()
        for i in range(n):
            pltpu.make_async_remote_copy(src_ref=outs[i], dst_ref=outs[i], send_sem=sems[0].at[i],
                                         recv_sem=sems[1].at[i], device_id=(x, y, 1 - c),
                                         device_id_type=MESH).wait()

    out = [SDS((4, r // 2, c), BF16) if ax == 0 else SDS((r // 2, 4 * c), BF16)
           for (r, c), ax in zip(shapes, axes)]
    return _Exchange(parts, out, [pltpu.SemaphoreType.DMA((n,)), pltpu.SemaphoreType.DMA((n,))], start, finish)


def _scatter_halves(pres, shapes):
    n = len(pres)

    def copies(ins, outs, sems):
        x, y, c = _place()
        return [pltpu.make_async_remote_copy(
            src_ref=ins[i].at[1 + j], dst_ref=outs[i].at[j], send_sem=sems[0].at[3 * i + j],
            recv_sem=sems[1].at[3 * i + j], device_id=(cx, cy, c), device_id_type=MESH)
            for i in range(n) for j, (cx, cy) in enumerate(_other_chips(x, y))]

    def start(ins, outs, sems):
        for cp in copies(ins, outs, sems):
            cp.start()

    def finish(ins, outs, sems):
        for cp in copies(ins, outs, sems):
            cp.wait()

    return _Exchange(pres, [SDS((3, r // 2, c), BF16) for r, c in shapes],
                     [pltpu.SemaphoreType.DMA((3 * n,)), pltpu.SemaphoreType.DMA((3 * n,))], start, finish)


def _swap_with_sibling(arrs):
    n = len(arrs)

    def copies(ins, outs, sems):
        x, y, c = _place()
        return [pltpu.make_async_remote_copy(src_ref=ins[i], dst_ref=outs[i], send_sem=sems[0].at[i],
                                             recv_sem=sems[1].at[i], device_id=(x, y, 1 - c),
                                             device_id_type=MESH) for i in range(n)]

    def start(ins, outs, sems):
        for cp in copies(ins, outs, sems):
            cp.start()

    def finish(ins, outs, sems):
        for cp in copies(ins, outs, sems):
            cp.wait()

    return _Exchange(arrs, [SDS(a.shape, a.dtype) for a in arrs],
                     [pltpu.SemaphoreType.DMA((n,)), pltpu.SemaphoreType.DMA((n,))], start, finish)


def _gather_small(arrs):
    n = len(arrs)

    def copy(sems, outs, i, k, block, to, src=None):
        px, py, pc = block
        dst = outs[i].at[4 * px + 2 * py + pc]
        return pltpu.make_async_remote_copy(
            src_ref=dst if src is None else src, dst_ref=dst, send_sem=sems[0].at[7 * i + k],
            recv_sem=sems[1].at[7 * i + k], device_id=to, device_id_type=MESH)

    direct = [math.prod(a.shape) * 4 <= DIRECT_GATHER_BYTES for a in arrs]

    def own_copies(ins, outs, sems):
        x, y, c = _place()
        cps = []
        for i in range(n):
            cps.append(copy(sems, outs, i, 0, (x, y, c), (x, y, 1 - c), src=ins[i]))
            for j, (cx, cy) in enumerate(_other_chips(x, y)):
                cps.append(copy(sems, outs, i, 1 + j, (x, y, c), (cx, cy, c), src=ins[i]))
                if direct[i]:
                    cps.append(copy(sems, outs, i, 4 + j, (x, y, c), (cx, cy, 1 - c), src=ins[i]))
        local = [pltpu.make_async_copy(ins[i], outs[i].at[4 * x + 2 * y + c], sems[2].at[i]) for i in range(n)]
        return cps, local

    def start(ins, outs, sems):
        cps, local = own_copies(ins, outs, sems)
        for cp in local + cps:
            cp.start()

    def finish(ins, outs, sems):
        x, y, c = _place()
        passed = []
        for j, (cx, cy) in enumerate(_other_chips(x, y)):
            for i in range(n):
                copy(sems, outs, i, 1 + j, (cx, cy, c), (x, y, c)).wait_recv()
                if not direct[i]:
                    cp = copy(sems, outs, i, 4 + j, (cx, cy, c), (x, y, 1 - c))
                    cp.start()
                    passed.append(cp)
        for i in range(n):
            copy(sems, outs, i, 0, (x, y, 1 - c), (x, y, c)).wait_recv()
            for j, (cx, cy) in enumerate(_other_chips(x, y)):
                copy(sems, outs, i, 4 + j, (cx, cy, 1 - c), (x, y, c)).wait_recv()
        cps, local = own_copies(ins, outs, sems)
        for cp in cps + passed:
            cp.wait_send()
        for cp in local:
            cp.wait()

    return _Exchange(arrs, [SDS((N_DEV,) + a.shape, F32) for a in arrs],
                     [pltpu.SemaphoreType.DMA((7 * n,)), pltpu.SemaphoreType.DMA((7 * n,)),
                      pltpu.SemaphoreType.DMA((n,))], start, finish)


def _local_step(x, p, tgt, wb, ws, shards=None, opt=None):
    bsz, seq, _ = x.shape
    t = bsz * seq
    tm = min(256, t)
    tb = min(256, seq)
    x0 = x.reshape(t, D_MODEL)
    p0 = p.reshape(t, PLE_DIM)
    tg = tgt.reshape(t, D_MODEL)
    row = lambda v: v.reshape(1, -1)
    dist = shards is not None
    wb = dict(wb)
    recv, sums, other, gathered = {}, {}, {}, {}
    gb = {}
    gs = {}
    shape_of, axis_of = {}, {}
    chip = None
    if dist:
        shape_of = {k: tuple(shards[k].shape) for k in BIG}
        axis_of = dict(BIG_AXIS)
        for q in range(LAST_PIECES):
            shape_of[LAST_PIECE % q] = (D_MODEL // LAST_PIECES, shape_of["ffn1_w_in"][1])
            axis_of[LAST_PIECE % q] = 1
        xi, yi, ci = _place()
        chip = (2 * xi + yi).astype(jnp.int32).reshape(1)
        ids = jnp.stack([2 * xi + yi] + [2 * cx + cy for cx, cy in _other_chips(xi, yi)] + [ci]).astype(jnp.int32)
    halfbuf, pre = {}, {}

    def gather(names):
        return _gather_weights([shards[k] for k in names], [GATHER_AXIS[k] for k in names]) if dist else None

    def exchange(scat=(), swap=(), halves=(), scat2=(), swap2=(), extra=None, after=None):
        if not dist:
            return None, []
        after = order[0] if after is None else after
        parts, tags = [], []
        if scat:
            parts.append(_scatter_grads([gb[k][1] for k in scat], [shape_of[k] for k in scat],
                                        [axis_of[k] for k in scat]))
            tags.append((recv, scat))
        if swap:
            for k in swap:
                sums[k] = _sum_blocks(gb[k][0], recv[k], shape_of[k], axis_of[k], chip, "sum_" + k, after)
            parts.append(_swap_with_sibling([sums[k] for k in swap]))
            tags.append((other, swap))
        if halves:
            parts.append(_swap_halves([gb[k][1] for k in halves], [shape_of[k] for k in halves],
                                      [axis_of[k] for k in halves]))
            tags.append((halfbuf, halves))
        if scat2:
            for k in scat2:
                pre[k] = _presum(gb[k][0], halfbuf[k], shape_of[k], axis_of[k], ids, "presum_" + k, after)
            parts.append(_scatter_halves([pre[k][1] for k in scat2], [shape_of[k] for k in scat2]))
            tags.append((recv, scat2))
        if swap2:
            for k in swap2:
                sums[k] = _sum_half(pre[k][0], recv[k], "sum_" + k, after)
            parts.append(_swap_with_sibling([sums[k] for k in swap2]))
            tags.append((other, swap2))
        if extra is not None:
            parts.append(extra[0])
            tags.append((extra[1], extra[2]))
        return (_join(parts), tags) if parts else (None, [])

    def take(ex_tags, got):
        ex, tags = ex_tags
        if ex is not None:
            for (dst, names), (o0, o1) in zip(tags, ex.cuts):
                dst.update(zip(names, got[o0:o1]))

    order = [None]

    def ordered(builder, *args, **kw):
        res = builder(*args, bg=order[0] if dist else None, **kw)
        order[0] = res[0][0]
        return res

    launched = []

    def launch(ex_tags):
        if ex_tags[0] is not None:
            n = len(launched)
            launched.append(n)
            take(ex_tags, _run_exchange_on_sequencer(ex_tags[0], "reduce_%d" % n, REDUCE_FIRST_ID + n))

    small_shape = {k: _small_view(k, v).shape for k, v in ws.items()}
    small_shape["loss_rows"] = (1, D_MODEL)
    ws = {k: v if (v.ndim == 2 and k != "ssm_log_dt") else v[0] for k, v in ws.items()}
    tril = jnp.tril(jnp.ones((CHUNK, CHUNK), dtype=bool))
    wsm = jnp.where(tril[None], ws["gmlp_w_s"], 0.0)
    wsm_b = wsm.astype(BF16)
    wsmt_b = wsm.transpose(0, 2, 1).astype(BF16)
    bias = jnp.repeat(ws["gmlp_b_s"].T, GMLP_HEAD_DIM, axis=1)

    tf = min(512, t)
    if dist:
        for gi, names in enumerate(GATHER_ORDER):
            wb.update(zip(names, _run_exchange_on_sequencer(gather(names), "gather_%d" % gi, GATHER_FIRST_ID + gi)))
    (x0b, h1, a1), _ = _ffn_proj(x0, wb["ffn1_w_in"], tf, "ffn1_proj")
    (x1, xh1, rstd1), _ = _ffn_out(x0, a1, wb["ffn1_w_out"], row(ws["ln1_g"]), row(ws["ln1_b"]), tf, "ffn1_out")
    sp = _s5_setup(ws["ssm_lambda_re"], ws["ssm_lambda_im"], ws["ssm_log_dt"], ws["ssm_b_re"],
                   ws["ssm_b_im"], ws["ssm_c_re"], ws["ssm_c_im"], ws["ssm_d"], wb["ssm_glu_w"],
                   ws["ssm_glu_b"], tb)
    (x1b, za, zuv, gab), _ = _mixin_fwd(x1, wb["mix_w_in"], tm)
    (s5o, s5ot, y2p, carries), _ = _s5_fwd(za, sp, bsz, seq, tb)
    (gm, gmt), _ = _gmlp_fwd(zuv, row(ws["gmlp_ln_g"]), row(ws["gmlp_ln_b"]), wsm_b, bias)
    (x2, xh2, rstd2), _ = _mixout_fwd(x1, s5o, gm, gab, wb["up_a"], wb["up_b"], wb["mix_w_out"],
                                           row(ws["ln2_g"]), row(ws["ln2_b"]), tm)
    (x2b, h2, a2), _ = _ffn_proj(x2, wb["ffn2_w_in"], tf, "ffn2_proj")
    (x3, xh3, rstd3), _ = _ffn_out(x2, a2, wb["ffn2_w_out"], row(ws["ln3_g"]), row(ws["ln3_b"]), tf, "ffn2_out")
    (dx3, x3b, pb, dq, de, loss_rows), _ = _ple_loss(x3, p0, tg, wb["ple_w_gate"], wb["ple_w_proj"], tm)
    order[0] = dx3
    gb["ple_w_gate"], _ = ordered(_tn_matmul, x3b, dq, "dw_ple_gate", 1024, 1024, a_t=True)
    gb["ple_w_proj"], _ = ordered(_tn_matmul, pb, de, "dw_ple_proj", 256, 1024, a_t=True)
    launch(exchange(scat=("ple_w_gate", "ple_w_proj")))
    (dx2, dh2, df2, gs["ln3_g"], gs["ln3_b"]), _ = ordered(
        _ffn_bwd, dx3, xh3, rstd3, h2, wb["ffn2_w_in"], wb["ffn2_w_out"], row(ws["ln3_g"]), tm, "ffn2_bwd")
    gb["ffn2_w_out"], _ = ordered(_tn_matmul, a2, df2, "dw_ffn2_out", 1408, 1024)
    launch(exchange(scat=("ffn2_w_out",)))
    gb["ffn2_w_in"], _ = ordered(_tn_matmul, x2b, dh2, "dw_ffn2_in", 1024, 1408, a_t=True)
    launch(exchange(scat=("ffn2_w_in",), swap=("ple_w_gate", "ple_w_proj")))
    (dx1a, dmx, mb, dya, dyb, ds5, dgm, dgab, gs["ln2_g"], gs["ln2_b"]), _ = ordered(
        _mixout_bwd, dx2, xh2, rstd2, s5o, gm, gab, wb["up_a"], wb["up_b"], wb["mix_w_out"], row(ws["ln2_g"]), tm)
    gb["mix_w_out"], _ = ordered(_tn_matmul, mb, dmx, "dw_mix_out", 1024, 1024, a_t=True)
    gb["up_a"], _ = ordered(_tn_matmul, s5ot, dya, "dw_up_a", 512, 1024, a_t=True)
    gb["up_b"], _ = ordered(_tn_matmul, gmt, dyb, "dw_up_b", 512, 1024, a_t=True)
    launch(exchange(scat=("mix_w_out", "up_a", "up_b"), swap=("ffn2_w_out",)))
    (dza, dmr, dmi, dnr, dni, da, ddsk, dgw, dgb), _ = ordered(_s5_bwd, za, y2p, ds5, carries, sp, bsz, seq, tb)
    gb["ssm_glu_w"] = (dgw, dgw.astype(BF16))
    launch(exchange(scat=("ssm_glu_w",), swap=("ffn2_w_in",)))
    (dzuv, dws, dbias, gs["gmlp_ln_g"], gs["gmlp_ln_b"]), _ = ordered(
        _gmlp_bwd, zuv, dgm, row(ws["gmlp_ln_g"]), row(ws["gmlp_ln_b"]), wsm_b, wsmt_b, bias)
    (dx1,), _ = ordered(_mixin_bwd, dx1a, dza, dzuv, dgab, wb["mix_w_in"], tm)
    g_mi, _ = ordered(_tn_matmul, x1b, dza, "dw_mix_in_a", 1024, 512, 0, 3584, a_t=True)
    g_mi, _ = ordered(_tn_matmul, x1b, dzuv, "dw_mix_in_uv", 1024, 512, 1, 3584, g_mi, a_t=True)
    gb["mix_w_in"], _ = ordered(_tn_matmul, x1b, dgab, "dw_mix_in_g", 1024, 512, 3, 3584, g_mi, a_t=True)
    launch(exchange(swap=("mix_w_out", "up_a", "up_b", "ssm_glu_w")))

    d_abr = da[0].sum(axis=0).reshape(SSM_GROUPS, SSM_STATE)
    d_abi = da[1].sum(axis=0).reshape(SSM_GROUPS, SSM_STATE)
    _, vjp = jax.vjp(_s5_discretise, ws["ssm_lambda_re"], ws["ssm_lambda_im"], ws["ssm_log_dt"],
                     ws["ssm_b_re"], ws["ssm_b_im"])
    (gs["ssm_lambda_re"], gs["ssm_lambda_im"], gs["ssm_log_dt"], gs["ssm_b_re"], gs["ssm_b_im"]) = vjp(
        (d_abr, d_abi, _block_diag_in_t(dmr), _block_diag_in_t(dmi)))
    gs["ssm_c_re"] = _block_diag_out_t(dnr)
    gs["ssm_c_im"] = _block_diag_out_t(dni)
    gs["ssm_d"] = ddsk
    gs["ssm_glu_b"] = dgb
    gs["gmlp_w_s"] = dws
    gs["gmlp_b_s"] = dbias.reshape(CHUNK, GMLP_HEADS, GMLP_HEAD_DIM).sum(axis=-1).T
    gs["loss_rows"] = loss_rows

    def small_gather(names):
        return (_gather_small([gs[k].reshape(small_shape[k]) for k in names]), gathered, names) if dist else None

    late = ("ln1_g", "ln1_b")
    launch(exchange(scat=("mix_w_in",), extra=small_gather(tuple(k for k in SMALL + ("loss_rows",) if k not in late))))
    (dx0, dh1, df1, gs["ln1_g"], gs["ln1_b"]), _ = ordered(
        _ffn_bwd, dx1, xh1, rstd1, h1, wb["ffn1_w_in"], wb["ffn1_w_out"], row(ws["ln1_g"]), tm, "ffn1_bwd")
    grad_x = dx0.reshape(bsz, seq, D_MODEL)
    if not dist:
        gb["ffn1_w_out"], _ = _tn_matmul(a1, df1, "dw_ffn1_out", 1408, 1024)
        gb["ffn1_w_in"], _ = _tn_matmul(x0b, dh1, "dw_ffn1_in", 1024, 1408, a_t=True)
        return (loss_rows, grad_x, gb, {k: gs[k].reshape(small_shape[k]) for k in SMALL}, sums, other, gathered,
                None, {})
    launch(exchange(extra=small_gather(late)))
    gb["ffn1_w_out"], _ = ordered(_tn_matmul, a1, df1, "dw_ffn1_out", 1408, 1024)
    last = ["ffn1_w_out"] + [LAST_PIECE % q for q in range(LAST_PIECES)]
    fillers = (("ffn2_w_in", "mix_w_in", "ple_w_gate"),
               ("ffn2_w_out", "mix_w_out", "up_a", "up_b", "ssm_glu_w", "ple_w_proj"))
    out = {}
    for i in range(1, len(last) + 3):
        stage = lambda d: tuple(last[i - d:i - d + 1]) if 0 <= i - d < len(last) else ()
        launch(exchange(halves=stage(1), scat2=stage(2), swap2=stage(3), swap=("mix_w_in",) if i == 2 else ()))
        if i < len(last):
            gb[last[i]], _ = ordered(_tn_matmul, x0b, dh1, "dw_" + last[i], D_MODEL // LAST_PIECES, 1408,
                                     a_cols=(i - 1, 1), a_t=True)
        elif i - len(last) < len(fillers):
            for k in fillers[i - len(last)]:
                w, m, v = opt[k]
                out[k] = _adam_big(w, sums[k], other[k], m, v, "adam_" + k, after=pre[stage(2)[0]][0])
                order[0] = out[k][1]
    return loss_rows, grad_x, gb, gs, sums, other, gathered, ids, out


def _adamw(w, g, m, v):
    m = ADAM_B1 * m + (1.0 - ADAM_B1) * g
    v = ADAM_B2 * v + (1.0 - ADAM_B2) * (g * g)
    m_hat = m / ADAM_C1
    v_hat = v / ADAM_C2
    delta = -ADAM_LR * (m_hat / (jnp.sqrt(v_hat) + ADAM_EPS) + ADAM_WD * w)
    return delta, m, v


def _pinned(after):
    return ([pl.BlockSpec(memory_space=pl.ANY)], [after]) if after is not None else ([], [])


def _sum_blocks(part, recv, shape, axis, chip, name, after=None):
    r, c = shape
    rb = r // 8

    def body(chip_ref, p_ref, r_ref, *rest):
        rest[-1][...] = (p_ref[...] + r_ref[0].astype(F32) + r_ref[1].astype(F32) + r_ref[2].astype(F32))

    if axis == 0:
        own = pl.BlockSpec((rb, c), lambda i, k: (k[0] * 8 + i, 0))
    else:
        own = pl.BlockSpec((rb, c), lambda i, k: (i, k[0]))
    pin_specs, pin_args = _pinned(after)
    grid_spec = pltpu.PrefetchScalarGridSpec(
        num_scalar_prefetch=1, grid=(8,),
        in_specs=[own, pl.BlockSpec((3, rb, c), lambda i, k: (0, i, 0))] + pin_specs,
        out_specs=pl.BlockSpec((rb, c), lambda i, k: (i, 0)))
    return pl.pallas_call(body, name=name, out_shape=SDS((r, c), F32), grid_spec=grid_spec,
                          compiler_params=_params(("parallel",)))(chip, part, recv, *pin_args)


def _presum(part, half, shape, axis, ids, name, after=None):
    r, c = shape
    rb = r // 4

    def body(ids_ref, p_ref, h_ref, *rest):
        of_ref, ob_ref = rest[-2:]
        s = p_ref[...] + h_ref[...].astype(F32)
        ob_ref[...] = s.astype(BF16)

        @pl.when(pl.program_id(1) == 0)
        def _():
            of_ref[...] = s

    if axis == 0:
        p_spec = pl.BlockSpec((rb, c), lambda i, t, ids: (ids[t] * 4 + ids[4] * 2 + i, 0))
        h_spec = pl.BlockSpec((None, rb, c), lambda i, t, ids: (ids[t], i, 0))
    else:
        p_spec = pl.BlockSpec((rb, c), lambda i, t, ids: (ids[4] * 2 + i, ids[t]))
        h_spec = pl.BlockSpec((rb, c), lambda i, t, ids: (i, ids[t]))
    pin_specs, pin_args = _pinned(after)
    grid_spec = pltpu.PrefetchScalarGridSpec(
        num_scalar_prefetch=1, grid=(2, 4), in_specs=[p_spec, h_spec] + pin_specs,
        out_specs=(pl.BlockSpec((rb, c), lambda i, t, ids: (i, 0)),
                   pl.BlockSpec((None, rb, c), lambda i, t, ids: (t, i, 0))))
    return pl.pallas_call(body, name=name, out_shape=(SDS((r // 2, c), F32), SDS((4, r // 2, c), BF16)),
                          grid_spec=grid_spec,
                          compiler_params=_params(("parallel", "arbitrary")))(ids, part, half, *pin_args)


def _sum_half(pre, recv, name, after=None):
    hr, c = pre.shape
    rb = hr // 2

    def body(p_ref, r_ref, *rest):
        rest[-1][...] = (p_ref[...] + r_ref[0].astype(F32) + r_ref[1].astype(F32) + r_ref[2].astype(F32))

    spec = pl.BlockSpec((rb, c), lambda i: (i, 0))
    pin_specs, pin_args = _pinned(after)
    return pl.pallas_call(body, name=name, grid=(2,), out_shape=SDS((hr, c), F32),
                          in_specs=[spec, pl.BlockSpec((3, rb, c), lambda i: (0, i, 0))] + pin_specs,
                          out_specs=spec, compiler_params=_params(("parallel",)))(pre, recv, *pin_args)


def _adam_halves(w, mine, oth, m, v, ids, name, piece=0, prev=None):
    r, c = w.shape
    rb = mine.shape[0] // 2

    def body(ids_ref, w_ref, a_ref, b_ref, m_ref, v_ref, *rest):
        g_ref, d_ref, nm_ref, nv_ref = rest[-4:]
        g = jnp.where(pl.program_id(0) // 2 == ids_ref[4], a_ref[...], b_ref[...])
        g_ref[...] = g
        d_ref[...], nm_ref[...], nv_ref[...] = _adamw(w_ref[...], g, m_ref[...], v_ref[...])

    whole = pl.BlockSpec((rb, c), lambda i, ids: (i + 4 * piece, 0))
    part = pl.BlockSpec((rb, c), lambda i, ids: (i % 2, 0))
    in_specs = [whole, part, part, whole, whole]
    args = [w, mine, oth, m, v]
    aliases = {}
    if prev is not None:
        in_specs += [pl.BlockSpec(memory_space=pl.ANY)] * 4
        args += list(prev)
        aliases = {6: 0, 7: 1, 8: 2, 9: 3}
    grid_spec = pltpu.PrefetchScalarGridSpec(num_scalar_prefetch=1, grid=(4,), in_specs=in_specs,
                                             out_specs=(whole,) * 4)
    return pl.pallas_call(body, name=name, out_shape=tuple(SDS((r, c), F32) for _ in range(4)),
                          grid_spec=grid_spec, input_output_aliases=aliases,
                          compiler_params=_params(("parallel",)))(ids, *args)


def _adam_big(w, ga, gb, m, v, name, piece=0, prev=None, after=None):
    r, c = w.shape
    pr = ga.shape[0]
    steps = 8 if pr == r else 2
    rb = pr // steps
    off = piece * steps

    def body(w_ref, ga_ref, gb_ref, m_ref, v_ref, *rest):
        g_ref, d_ref, nm_ref, nv_ref = rest[-4:]
        g = ga_ref[...] + gb_ref[...]
        g_ref[...] = g
        d_ref[...], nm_ref[...], nv_ref[...] = _adamw(w_ref[...], g, m_ref[...], v_ref[...])

    whole = pl.BlockSpec((rb, c), lambda i: (i + off, 0))
    part = pl.BlockSpec((rb, c), lambda i: (i, 0))
    in_specs = [whole, part, part, whole, whole]
    args = [w, ga, gb, m, v]
    aliases = {}
    if prev is not None:
        in_specs += [pl.BlockSpec(memory_space=pl.ANY)] * 4
        args += list(prev)
        aliases = {5: 0, 6: 1, 7: 2, 8: 3}
    if after is not None:
        in_specs.append(pl.BlockSpec(memory_space=pl.ANY))
        args.append(after)
    return pl.pallas_call(
        body, name=name, grid=(steps,), out_shape=tuple(SDS((r, c), F32) for _ in range(4)),
        in_specs=in_specs, out_specs=(whole,) * 4, input_output_aliases=aliases,
        compiler_params=_params(("parallel",)),
    )(*args)


def _adam_small(ws, gathered, ms, vs):
    n = len(ws)

    def body(*refs):
        w_refs, g_refs, m_refs, v_refs = refs[:n], refs[n:2 * n], refs[2 * n:3 * n], refs[3 * n:4 * n]
        outs = refs[4 * n:]
        for i in range(n):
            g = g_refs[i][0]
            for d in range(1, N_DEV):
                g = g + g_refs[i][d]
            delta, nm, nv = _adamw(w_refs[i][...], g, m_refs[i][...], v_refs[i][...])
            outs[i][...] = g
            outs[n + i][...] = delta
            outs[2 * n + i][...] = nm
            outs[3 * n + i][...] = nv

    vmem = pl.BlockSpec(memory_space=pltpu.VMEM)
    shapes = [w.shape for w in ws]
    return pl.pallas_call(
        body, name="adam_small", out_shape=tuple(SDS(s, F32) for s in shapes * 4),
        in_specs=[vmem] * (4 * n), out_specs=tuple([vmem] * (4 * n)),
        compiler_params=pltpu.CompilerParams(vmem_limit_bytes=VMEM_LIMIT_BYTES),
    )(*ws, *gathered, *ms, *vs)


def _sum_loss(gathered):
    def body(g_ref, o_ref):
        tot = g_ref[0]
        for d in range(1, N_DEV):
            tot = tot + g_ref[d]
        o_ref[...] = (0.5 / D_MODEL) * jnp.sum(tot, axis=1, keepdims=True)

    vmem = pl.BlockSpec(memory_space=pltpu.VMEM)
    return pl.pallas_call(body, name="sum_loss", out_shape=SDS((1, 1), F32), in_specs=[vmem],
                          out_specs=vmem)(gathered)


def kernel(x, p, ffn1_w_in, ffn1_w_out, ln1_g, ln1_b, mix_w_in, ssm_lambda_re, ssm_lambda_im, ssm_log_dt, ssm_b_re, ssm_b_im, ssm_c_re, ssm_c_im, ssm_d, ssm_glu_w, ssm_glu_b, gmlp_ln_g, gmlp_ln_b, gmlp_w_s, gmlp_b_s, up_a, up_b, mix_w_out, ln2_g, ln2_b, ffn2_w_in, ffn2_w_out, ln3_g, ln3_b, ple_w_proj, ple_w_gate, loss_target, m_ffn1_w_in, m_ffn1_w_out, m_ln1_g, m_ln1_b, m_mix_w_in, m_ssm_lambda_re, m_ssm_lambda_im, m_ssm_log_dt, m_ssm_b_re, m_ssm_b_im, m_ssm_c_re, m_ssm_c_im, m_ssm_d, m_ssm_glu_w, m_ssm_glu_b, m_gmlp_ln_g, m_gmlp_ln_b, m_gmlp_w_s, m_gmlp_b_s, m_up_a, m_up_b, m_mix_w_out, m_ln2_g, m_ln2_b, m_ffn2_w_in, m_ffn2_w_out, m_ln3_g, m_ln3_b, m_ple_w_proj, m_ple_w_gate, v_ffn1_w_in, v_ffn1_w_out, v_ln1_g, v_ln1_b, v_mix_w_in, v_ssm_lambda_re, v_ssm_lambda_im, v_ssm_log_dt, v_ssm_b_re, v_ssm_b_im, v_ssm_c_re, v_ssm_c_im, v_ssm_d, v_ssm_glu_w, v_ssm_glu_b, v_gmlp_ln_g, v_gmlp_ln_b, v_gmlp_w_s, v_gmlp_b_s, v_up_a, v_up_b, v_mix_w_out, v_ln2_g, v_ln2_b, v_ffn2_w_in, v_ffn2_w_out, v_ln3_g, v_ln3_b, v_ple_w_proj, v_ple_w_gate):
    given = dict(locals())
    order = ("ffn1_w_in", "ffn1_w_out", "ln1_g", "ln1_b", "mix_w_in", "ssm_lambda_re", "ssm_lambda_im",
             "ssm_log_dt", "ssm_b_re", "ssm_b_im", "ssm_c_re", "ssm_c_im", "ssm_d", "ssm_glu_w", "ssm_glu_b",
             "gmlp_ln_g", "gmlp_ln_b", "gmlp_w_s", "gmlp_b_s", "up_a", "up_b", "mix_w_out", "ln2_g", "ln2_b",
             "ffn2_w_in", "ffn2_w_out", "ln3_g", "ln3_b", "ple_w_proj", "ple_w_gate")
    assert set(order) == set(BIG + SMALL)

    shard = {k: given[k][0] for k in BIG}
    shard_b = {k: shard[k].astype(BF16) for k in BIG}
    opt = {k: (shard[k], given["m_" + k][0], given["v_" + k][0]) for k in BIG}
    loss_rows, grad_x, gb, gs, sums, other, gathered, ids, out = _local_step(
        x, given["p"][0], loss_target, {}, {k: given[k] for k in SMALL}, shard_b, opt)

    out = dict(out)
    for k in BIG:
        if k in out:
            continue
        moments = (given["m_" + k][0], given["v_" + k][0])
        if k == "ffn1_w_out":
            out[k] = _adam_halves(shard[k], sums[k], other[k], *moments, ids, "adam_" + k)
        elif k == "ffn1_w_in":
            for q in range(LAST_PIECES):
                kq = LAST_PIECE % q
                out[k] = _adam_halves(shard[k], sums[kq], other[kq], *moments, ids, "adam_" + kq, q, out.get(k))
        else:
            out[k] = _adam_big(shard[k], sums[k], other[k], *moments, "adam_" + k,
                               after=gb[LAST_PIECE % (LAST_PIECES - 1)][0])

    res = _adam_small([_small_view(k, given[k]) for k in SMALL], [gathered[k] for k in SMALL],
                      [_small_view(k, given["m_" + k]) for k in SMALL],
                      [_small_view(k, given["v_" + k]) for k in SMALL])
    ns = len(SMALL)
    for i, k in enumerate(SMALL):
        out[k] = tuple(res[j * ns + i].reshape(given[k].shape) for j in range(4))
    loss = _sum_loss(gathered["loss_rows"]).reshape(())

    lead = lambda k, j: out[k][j][None] if k in BIG else out[k][j]
    return (loss, grad_x, *[lead(k, 0) for k in order], *[lead(k, 1) for k in order],
            *[lead(k, 2) for k in order], *[lead(k, 3) for k in order])
```

```python
import math

import jax
import jax.numpy as jnp
from jax import lax
from jax.experimental import pallas as pl
from jax.experimental.pallas import tpu as pltpu
from jax.experimental.pallas import tpu_sc as plsc

F32 = jnp.float32
BF16 = jnp.bfloat16
MESH = pl.DeviceIdType.MESH
SDS = jax.ShapeDtypeStruct

D_MODEL = 1024
D_FF = 2816
D_SSM = 512
D_GMLP = 512
SSM_GROUPS = 32
SSM_GROUP_CH = 16
SSM_STATE = 64
SSM_LANES = SSM_GROUPS * SSM_STATE
GMLP_HEADS = 8
GMLP_HEAD_DIM = 64
CHUNK = 128
PLE_DIM = 256
LN_EPS = 1e-5
ALPHA = 2.0 ** 0.25

ADAM_LR = 0.001
ADAM_B1 = 0.9
ADAM_B2 = 0.999
ADAM_EPS = 1e-08
ADAM_WD = 0.01
ADAM_STEP = 10
ADAM_C1 = 1.0 - ADAM_B1 ** ADAM_STEP
ADAM_C2 = 1.0 - ADAM_B2 ** ADAM_STEP

N_DEV = 8
VMEM_LIMIT_BYTES = 56 * 1024 * 1024
FFN_COLS = 1408
S5_BLOCKS = 4
S5_BLOCK_IN = D_SSM // S5_BLOCKS
S5_BLOCK_ST = SSM_LANES // S5_BLOCKS
SCAN_LANES = 512
TN_K_BLOCK = 2048
DIRECT_GATHER_BYTES = 0
LAST_PIECES = 2
LAST_PIECE = "ffn1_w_in_q%d"
_G0 = math.sqrt(2.0 / math.pi)
_G1 = 0.044715


def _dot(a, b):
    return jnp.dot(a, b, preferred_element_type=F32)


def _dot_nt(a, b):
    return lax.dot_general(a, b, (((1,), (1,)), ((), ())), preferred_element_type=F32)


def _dot_tn(a, b):
    return lax.dot_general(a, b, (((0,), (0,)), ((), ())), preferred_element_type=F32)


def _sigmoid(x):
    return 1.0 / (1.0 + jnp.exp(-x))


def _gelu(x):
    t = jnp.tanh(_G0 * (x + _G1 * x * x * x))
    return 0.5 * x * (1.0 + t)


def _gelu_grad(x):
    t = jnp.tanh(_G0 * (x + _G1 * x * x * x))
    return 0.5 * (1.0 + t) + 0.5 * x * (1.0 - t * t) * _G0 * (1.0 + 3.0 * _G1 * x * x)


def _ln_fwd(r, g, b):
    mu = jnp.mean(r, axis=-1, keepdims=True)
    d = r - mu
    var = jnp.mean(d * d, axis=-1, keepdims=True)
    rstd = lax.rsqrt(var + LN_EPS)
    xh = d * rstd
    return xh * g + b, xh, rstd


def _ln_bwd(dy, xh, rstd, g):
    dxh = dy * g
    m1 = jnp.mean(dxh, axis=-1, keepdims=True)
    m2 = jnp.mean(dxh * xh, axis=-1, keepdims=True)
    return rstd * (dxh - m1 - xh * m2)


def _resident(shape):
    nd = len(shape)
    return pl.BlockSpec(shape, lambda *_: (0,) * nd, pipeline_mode=pl.Buffered(1))


def _fixed(shape):
    nd = len(shape)
    return pl.BlockSpec(shape, lambda *_: (0,) * nd)


def _rows(tm, cols):
    return pl.BlockSpec((tm, cols), lambda i: (i, 0))


def _cols(rows, tm):
    return pl.BlockSpec((rows, tm), lambda i: (0, i))


def _params(sem):
    return pltpu.CompilerParams(dimension_semantics=sem, vmem_limit_bytes=VMEM_LIMIT_BYTES)


class _Exchange:
    def __init__(self, args, out_shape, sems, start, finish):
        self.args, self.out_shape, self.sems = list(args), list(out_shape), list(sems)
        self.start, self.finish = start, finish
        self.cuts = [(0, len(self.out_shape))]


def _call(body, name, grid, in_specs, out_specs, out_shape, args, scratch=(), sem=None, bg=None, aliases=None):
    aliases = {} if aliases is None else aliases
    if bg is not None and not isinstance(bg, _Exchange):
        n_args = len(args)

        def pinned(*refs):
            body(*refs[:n_args], *refs[n_args + 1:])

        res = pl.pallas_call(pinned, name=name, grid=grid, out_shape=tuple(out_shape),
                             in_specs=list(in_specs) + [pl.BlockSpec(memory_space=pl.ANY)],
                             out_specs=tuple(out_specs), scratch_shapes=list(scratch),
                             input_output_aliases=aliases, compiler_params=_params(sem))(*args, bg)
        return tuple(res), ()
    if bg is None:
        res = pl.pallas_call(body, name=name, grid=grid, out_shape=tuple(out_shape), in_specs=list(in_specs),
                             out_specs=tuple(out_specs), scratch_shapes=list(scratch),
                             input_output_aliases=aliases, compiler_params=_params(sem))(*args)
        return tuple(res), ()
    n_in, n_out, n_bi, n_bo, n_sc = len(args), len(out_shape), len(bg.args), len(bg.out_shape), len(scratch)

    def wrapped(*refs):
        ins = refs[:n_in]
        b_ins = refs[n_in:n_in + n_bi]
        outs = refs[n_in + n_bi:n_in + n_bi + n_out]
        b_outs = refs[n_in + n_bi + n_out:n_in + n_bi + n_out + n_bo]
        rest = refs[n_in + n_bi + n_out + n_bo:]
        scr, b_sems = rest[:n_sc], rest[n_sc:]
        first = pl.program_id(0) == 0
        last = pl.program_id(0) == grid[0] - 1
        for ax in range(1, len(grid)):
            first = jnp.logical_and(first, pl.program_id(ax) == 0)
            last = jnp.logical_and(last, pl.program_id(ax) == grid[ax] - 1)

        @pl.when(first)
        def _():
            bg.start(b_ins, b_outs, b_sems)

        body(*ins, *outs, *scr)

        @pl.when(last)
        def _():
            bg.finish(b_ins, b_outs, b_sems)

    any_spec = pl.BlockSpec(memory_space=pl.ANY)
    res = pl.pallas_call(
        wrapped, name=name, grid=grid, out_shape=tuple(out_shape) + tuple(bg.out_shape),
        in_specs=list(in_specs) + [any_spec] * n_bi, out_specs=tuple(out_specs) + (any_spec,) * n_bo,
        scratch_shapes=list(scratch) + list(bg.sems), input_output_aliases=aliases,
        compiler_params=_params(tuple("arbitrary" for _ in grid)))(*args, *bg.args)
    return tuple(res[:n_out]), tuple(res[n_out:])


def _run_exchange(ex, name):
    n_i, n_o = len(ex.args), len(ex.out_shape)

    def body(*refs):
        ins, outs, sems = refs[:n_i], refs[n_i:n_i + n_o], refs[n_i + n_o:]
        ex.start(ins, outs, sems)
        ex.finish(ins, outs, sems)

    any_spec = pl.BlockSpec(memory_space=pl.ANY)
    return tuple(pl.pallas_call(body, name=name, out_shape=tuple(ex.out_shape), in_specs=[any_spec] * n_i,
                                out_specs=(any_spec,) * n_o, scratch_shapes=list(ex.sems))(*ex.args))


def _run_exchange_on_sequencer(ex, name, collective_id):
    n_i, n_o = len(ex.args), len(ex.out_shape)

    def body(*refs):
        ins, outs, sems = refs[:n_i], refs[n_i:n_i + n_o], refs[n_i + n_o:]
        x, y, c = lax.axis_index("x"), lax.axis_index("y"), lax.axis_index("c")
        barrier = pltpu.get_barrier_semaphore()
        for peer in [(x, y, 1 - c), (1 - x, y, c), (x, 1 - y, c), (1 - x, 1 - y, c)]:
            pl.semaphore_signal(barrier, inc=1, device_id=peer, device_id_type=MESH)
        pl.semaphore_wait(barrier, 4)
        ex.start(ins, outs, sems)
        ex.finish(ins, outs, sems)

    return tuple(pl.kernel(body, out_type=tuple(ex.out_shape),
                           mesh=plsc.ScalarSubcoreMesh(axis_name="sequencer", num_cores=1),
                           scratch_types=list(ex.sems), name=name,
                           compiler_params=pltpu.CompilerParams(collective_id=collective_id))(*ex.args))


def _join(exchanges):
    cuts = []
    a = o = q = 0
    for e in exchanges:
        cuts.append((a, a + len(e.args), o, o + len(e.out_shape), q, q + len(e.sems)))
        a, o, q = cuts[-1][1], cuts[-1][3], cuts[-1][5]

    def start(ins, outs, sems):
        for e, (a0, a1, o0, o1, q0, q1) in zip(exchanges, cuts):
            e.start(ins[a0:a1], outs[o0:o1], sems[q0:q1])

    def finish(ins, outs, sems):
        for e, (a0, a1, o0, o1, q0, q1) in zip(exchanges, cuts):
            e.finish(ins[a0:a1], outs[o0:o1], sems[q0:q1])

    joined = _Exchange(sum((e.args for e in exchanges), []), sum((e.out_shape for e in exchanges), []),
                       sum((e.sems for e in exchanges), []), start, finish)
    joined.cuts = [(c[2], c[3]) for c in cuts]
    return joined


def _ffn_proj(x, w_in, tm, name, bg=None):
    t = x.shape[0]
    nch = D_FF // FFN_COLS

    def body(x_ref, win_ref, xbt_ref, h_ref, a_ref):
        xb = x_ref[...].astype(BF16)
        xbt_ref[...] = xb.T
        for k in range(nch):
            cg = slice(k * FFN_COLS, (k + 1) * FFN_COLS)
            cu = slice(D_FF + k * FFN_COLS, D_FF + (k + 1) * FFN_COLS)
            hg = _dot(xb, win_ref[k])
            hu = _dot(xb, win_ref[nch + k])
            h_ref[:, cg] = hg.astype(BF16)
            h_ref[:, cu] = hu.astype(BF16)
            a_ref[:, cg] = (hg * _sigmoid(hg) * hu).astype(BF16)

    return _call(
        body, name, (t // tm,),
        [_rows(tm, D_MODEL), _resident((2 * nch, D_MODEL, FFN_COLS))],
        (_cols(D_MODEL, tm), _rows(tm, 2 * D_FF), _rows(tm, D_FF)),
        (SDS((D_MODEL, t), BF16), SDS((t, 2 * D_FF), BF16), SDS((t, D_FF), BF16)),
        (x, w_in), sem=("parallel",), bg=bg)


def _ffn_out(x, a, w_out, g, b, tm, name, bg=None):
    t = x.shape[0]

    def body(x_ref, a_ref, wout_ref, g_ref, b_ref, xn_ref, xh_ref, rstd_ref):
        f = _dot(a_ref[...], wout_ref[...])
        y, xh, rstd = _ln_fwd(ALPHA * x_ref[...] + 0.5 * f, g_ref[...], b_ref[...])
        xn_ref[...] = y
        xh_ref[...] = xh
        rstd_ref[...] = rstd

    return _call(
        body, name, (t // tm,),
        [_rows(tm, D_MODEL), _rows(tm, D_FF), _resident((D_FF, D_MODEL)), _fixed((1, D_MODEL)), _fixed((1, D_MODEL))],
        (_rows(tm, D_MODEL), _rows(tm, D_MODEL), _rows(tm, 1)),
        (SDS((t, D_MODEL), F32), SDS((t, D_MODEL), F32), SDS((t, 1), F32)),
        (x, a, w_out, g, b), sem=("parallel",), bg=bg)


def _ffn_bwd(dxn, xh, rstd, h, w_in, w_out, g, tm, name, bg=None):
    t = dxn.shape[0]
    nch = D_FF // FFN_COLS

    def body(dxn_ref, xh_ref, rstd_ref, h_ref, win_ref, wout_ref, g_ref,
             dx_ref, dh_ref, df_ref, dg_ref, db_ref):
        @pl.when(pl.program_id(0) == 0)
        def _():
            dg_ref[...] = jnp.zeros_like(dg_ref)
            db_ref[...] = jnp.zeros_like(db_ref)

        dy = dxn_ref[...]
        xhv = xh_ref[...]
        dr = _ln_bwd(dy, xhv, rstd_ref[...], g_ref[...])
        dg_ref[...] += jnp.sum(dy * xhv, axis=0, keepdims=True)
        db_ref[...] += jnp.sum(dy, axis=0, keepdims=True)
        df = (0.5 * dr).astype(BF16)
        df_ref[...] = df
        dx = ALPHA * dr
        for k in range(nch):
            cg = slice(k * FFN_COLS, (k + 1) * FFN_COLS)
            cu = slice(D_FF + k * FFN_COLS, D_FF + (k + 1) * FFN_COLS)
            hg = h_ref[:, cg].astype(F32)
            hu = h_ref[:, cu].astype(F32)
            sg = _sigmoid(hg)
            silu = hg * sg
            da = _dot_nt(df, wout_ref[cg, :])
            dhu = (da * silu).astype(BF16)
            dhg = (da * hu * (sg * (1.0 + hg * (1.0 - sg)))).astype(BF16)
            dh_ref[:, cg] = dhg
            dh_ref[:, cu] = dhu
            dx = dx + _dot_nt(dhg, win_ref[k]) + _dot_nt(dhu, win_ref[nch + k])
        dx_ref[...] = dx

    return _call(
        body, name, (t // tm,),
        [_rows(tm, D_MODEL), _rows(tm, D_MODEL), _rows(tm, 1), _rows(tm, 2 * D_FF),
         _resident((2 * nch, D_MODEL, FFN_COLS)), _resident((D_FF, D_MODEL)), _fixed((1, D_MODEL))],
        (_rows(tm, D_MODEL), _rows(tm, 2 * D_FF), _rows(tm, D_MODEL),
         _fixed((1, D_MODEL)), _fixed((1, D_MODEL))),
        (SDS((t, D_MODEL), F32), SDS((t, 2 * D_FF), BF16), SDS((t, D_MODEL), BF16),
         SDS((1, D_MODEL), F32), SDS((1, D_MODEL), F32)),
        (dxn, xh, rstd, h, w_in, w_out, g), sem=("arbitrary",), bg=bg)


def _tn_matmul(a, b, name, bm, bn, col_block=0, total_cols=None, prev=None, bg=None, a_cols=None, a_t=False):
    t, m = a.shape[::-1] if a_t else a.shape
    a_first = 0
    if a_cols is not None:
        a_first, m = a_cols[0], a_cols[1] * bm
    n = b.shape[1]
    total_cols = n if total_cols is None else total_cols
    bk = min(TN_K_BLOCK, t)
    nk = t // bk
    n_in = 2 if prev is None else 4

    def body(*refs):
        a_ref, b_ref = refs[0], refs[1]
        o_ref, ob_ref = refs[n_in], refs[n_in + 1]
        k = pl.program_id(2)

        @pl.when(k == 0)
        def _():
            o_ref[...] = jnp.zeros_like(o_ref)

        o_ref[...] += _dot(a_ref[...], b_ref[...]) if a_t else _dot_tn(a_ref[...], b_ref[...])

        @pl.when(k == nk - 1)
        def _():
            ob_ref[...] = o_ref[...].astype(BF16)

    a_spec = (pl.BlockSpec((bm, bk), lambda i, j, k: (i + a_first, k)) if a_t
              else pl.BlockSpec((bk, bm), lambda i, j, k: (k, i + a_first)))
    in_specs = [a_spec, pl.BlockSpec((bk, bn), lambda i, j, k: (k, j))]
    args = [a, b]
    aliases = {}
    if prev is not None:
        in_specs += [pl.BlockSpec(memory_space=pl.ANY), pl.BlockSpec(memory_space=pl.ANY)]
        args += list(prev)
        aliases = {2: 0, 3: 1}
    out_spec = pl.BlockSpec((bm, bn), lambda i, j, k: (i, j + col_block))
    return _call(body, name, (m // bm, n // bn, nk), in_specs, (out_spec, out_spec),
                 (SDS((m, total_cols), F32), SDS((m, total_cols), BF16)), args,
                 sem=("parallel", "parallel", "arbitrary"), bg=bg, aliases=aliases)


def _mixin_fwd(x1, w, tm, bg=None):
    t = x1.shape[0]

    def body(x_ref, w_ref, xbt_ref, za_ref, zuv_ref, gab_ref):
        xb = x_ref[...].astype(BF16)
        xbt_ref[...] = xb.T
        za_ref[...] = _dot(xb, w_ref[:, 0:512]).astype(BF16)
        zuv_ref[...] = _dot(xb, w_ref[:, 512:1536]).astype(BF16)
        gab_ref[...] = _dot(xb, w_ref[:, 1536:3584]).astype(BF16)

    return _call(
        body, "mixin_fwd", (t // tm,),
        [_rows(tm, D_MODEL), _resident((D_MODEL, 3584))],
        (_cols(D_MODEL, tm), _rows(tm, 512), _rows(tm, 1024), _rows(tm, 2048)),
        (SDS((D_MODEL, t), BF16), SDS((t, 512), BF16), SDS((t, 1024), BF16), SDS((t, 2048), BF16)),
        (x1, w), sem=("parallel",), bg=bg)


def _mixin_bwd(dx1a, dza, dzuv, dgab, w, tm, bg=None):
    t = dx1a.shape[0]

    def body(d_ref, dza_ref, dzuv_ref, dgab_ref, w_ref, dx_ref):
        dx_ref[...] = (d_ref[...] + _dot_nt(dza_ref[...], w_ref[:, 0:512])
                       + _dot_nt(dzuv_ref[...], w_ref[:, 512:1536])
                       + _dot_nt(dgab_ref[...], w_ref[:, 1536:3584]))

    return _call(
        body, "mixin_bwd", (t // tm,),
        [_rows(tm, D_MODEL), _rows(tm, 512), _rows(tm, 1024), _rows(tm, 2048), _resident((D_MODEL, 3584))],
        (_rows(tm, D_MODEL),), (SDS((t, D_MODEL), F32),),
        (dx1a, dza, dzuv, dgab, w), sem=("parallel",), bg=bg)


def _unrolled(lo, hi, body, carry):
    for j in range(lo, hi):
        carry = body(j, carry)
    return carry


def _scan_fwd(hr_ref, hi_ref, a_ref, ap_ref, carry_ref, seg, cin_ref):
    for lc in range(SSM_LANES // SCAN_LANES):
        ls = slice(lc * SCAN_LANES, (lc + 1) * SCAN_LANES)
        a_r = jnp.broadcast_to(a_ref[0:1, ls], (8, SCAN_LANES))
        a_i = jnp.broadcast_to(a_ref[1:2, ls], (8, SCAN_LANES))

        def step(j, hc, ls=ls, a_r=a_r, a_i=a_i):
            h_r, h_i = hc
            rows = pl.ds(j * 8, 8)
            n_r = a_r * h_r - a_i * h_i + hr_ref[rows, ls]
            n_i = a_r * h_i + a_i * h_r + hi_ref[rows, ls]
            hr_ref[rows, ls] = n_r
            hi_ref[rows, ls] = n_i
            return n_r, n_i

        zero = jnp.zeros((8, SCAN_LANES), F32)
        f_r, f_i = _unrolled(0, seg, step, (zero, zero))
        c_r = carry_ref[0:1, ls]
        c_i = carry_ref[1:2, ls]
        p_r = ap_ref[0:1, ls]
        p_i = ap_ref[1:2, ls]
        rows_r, rows_i = [], []
        for s in range(8):
            rows_r.append(c_r)
            rows_i.append(c_i)
            c_r, c_i = (f_r[s:s + 1] + p_r * c_r - p_i * c_i,
                        f_i[s:s + 1] + p_r * c_i + p_i * c_r)
        carry_ref[0:1, ls] = c_r
        carry_ref[1:2, ls] = c_i
        cin_r = jnp.concatenate(rows_r, axis=0)
        cin_i = jnp.concatenate(rows_i, axis=0)
        if cin_ref is not None:
            cin_ref[0, :, ls] = cin_r
            cin_ref[1, :, ls] = cin_i

        def fix(j, cc, ls=ls, a_r=a_r, a_i=a_i):
            c_r, c_i = cc
            c_r, c_i = a_r * c_r - a_i * c_i, a_r * c_i + a_i * c_r
            rows = pl.ds(j * 8, 8)
            hr_ref[rows, ls] = hr_ref[rows, ls] + c_r
            hi_ref[rows, ls] = hi_ref[rows, ls] + c_i
            return c_r, c_i

        _unrolled(0, seg, fix, (cin_r, cin_i))


def _scan_bwd(gr_ref, gi_ref, hr_ref, hi_ref, cin_ref, a_ref, ap_ref, rcarry_ref, da_ref, seg):
    for lc in range(SSM_LANES // SCAN_LANES):
        ls = slice(lc * SCAN_LANES, (lc + 1) * SCAN_LANES)
        a_r = jnp.broadcast_to(a_ref[0:1, ls], (8, SCAN_LANES))
        a_i = jnp.broadcast_to(a_ref[1:2, ls], (8, SCAN_LANES))

        def step(t, gc, ls=ls, a_r=a_r, a_i=a_i):
            g_r, g_i = gc
            rows = pl.ds((seg - 1 - t) * 8, 8)
            n_r = gr_ref[rows, ls] + a_r * g_r + a_i * g_i
            n_i = gi_ref[rows, ls] + a_r * g_i - a_i * g_r
            gr_ref[rows, ls] = n_r
            gi_ref[rows, ls] = n_i
            return n_r, n_i

        zero = jnp.zeros((8, SCAN_LANES), F32)
        f_r, f_i = _unrolled(0, seg, step, (zero, zero))
        c_r = rcarry_ref[0:1, ls]
        c_i = rcarry_ref[1:2, ls]
        p_r = ap_ref[0:1, ls]
        p_i = ap_ref[1:2, ls]
        rows_r, rows_i = [None] * 8, [None] * 8
        for s in range(7, -1, -1):
            rows_r[s] = c_r
            rows_i[s] = c_i
            c_r, c_i = (f_r[s:s + 1] + p_r * c_r + p_i * c_i,
                        f_i[s:s + 1] + p_r * c_i - p_i * c_r)
        rcarry_ref[0:1, ls] = c_r
        rcarry_ref[1:2, ls] = c_i
        cin_r = jnp.concatenate(rows_r, axis=0)
        cin_i = jnp.concatenate(rows_i, axis=0)

        def fix_row(j_rows, hp_r, hp_i, cc, ls=ls, a_r=a_r, a_i=a_i):
            c_r, c_i, acc_r, acc_i = cc
            c_r, c_i = a_r * c_r + a_i * c_i, a_r * c_i - a_i * c_r
            g_r = gr_ref[j_rows, ls] + c_r
            g_i = gi_ref[j_rows, ls] + c_i
            gr_ref[j_rows, ls] = g_r
            gi_ref[j_rows, ls] = g_i
            acc_r = acc_r + g_r * hp_r + g_i * hp_i
            acc_i = acc_i + g_i * hp_r - g_r * hp_i
            return c_r, c_i, acc_r, acc_i

        def fix(t, cc, ls=ls, fix_row=fix_row):
            j = seg - 1 - t
            rows = pl.ds(j * 8, 8)
            prev = pl.ds((j - 1) * 8, 8)
            return fix_row(rows, hr_ref[prev, ls], hi_ref[prev, ls], cc)

        cc = _unrolled(0, seg - 1, fix, (cin_r, cin_i, zero, zero))
        _, _, acc_r, acc_i = fix_row(pl.ds(0, 8), cin_ref[0, :, ls], cin_ref[1, :, ls], cc)
        da_ref[0, :, ls] += acc_r
        da_ref[1, :, ls] += acc_i


def _s5_fwd(za, sp, bsz, seq, tb, bg=None):
    nb = seq // tb
    seg = tb // 8
    t = bsz * seq

    def body(za_ref, perm_ref, permt_ref, mre_ref, mim_ref, nre_ref, nim_ref, a_ref, ap_ref,
             dsk_ref, gw_ref, gb_ref, out_ref, outt_ref, y2_ref, car_ref, hr_ref, hi_ref, carry_ref):
        @pl.when(pl.program_id(1) == 0)
        def _():
            carry_ref[...] = jnp.zeros_like(carry_ref)

        car_ref[0] = carry_ref[...]
        up = _dot(perm_ref[...], za_ref[...])
        upb = up.astype(BF16)
        for bb in range(S5_BLOCKS):
            ub = upb[:, bb * S5_BLOCK_IN:(bb + 1) * S5_BLOCK_IN]
            st = slice(bb * S5_BLOCK_ST, (bb + 1) * S5_BLOCK_ST)
            hr_ref[:, st] = _dot(ub, mre_ref[bb])
            hi_ref[:, st] = _dot(ub, mim_ref[bb])
        _scan_fwd(hr_ref, hi_ref, a_ref, ap_ref, carry_ref, seg, None)
        ys = []
        for bb in range(S5_BLOCKS):
            st = slice(bb * S5_BLOCK_ST, (bb + 1) * S5_BLOCK_ST)
            ys.append(_dot(hr_ref[:, st].astype(BF16), nre_ref[bb])
                      - _dot(hi_ref[:, st].astype(BF16), nim_ref[bb]))
        y2 = jnp.concatenate(ys, axis=1) + dsk_ref[...] * up
        y2_ref[...] = y2
        y3 = _gelu(y2)
        gl = _dot(y3.astype(BF16), gw_ref[...]) + gb_ref[...]
        oa = y3 * _sigmoid(gl)
        out = _dot(permt_ref[...], oa.astype(BF16)).astype(BF16)
        out_ref[...] = out
        outt_ref[...] = out.T

    blk = pl.BlockSpec((tb, D_SSM), lambda b, j: (b * nb + j, 0))
    blk_t = pl.BlockSpec((D_SSM, tb), lambda b, j: (0, b * nb + j))
    m_shape = (S5_BLOCKS, S5_BLOCK_IN, S5_BLOCK_ST)
    n_shape = (S5_BLOCKS, S5_BLOCK_ST, S5_BLOCK_IN)
    return _call(
        body, "s5_fwd", (bsz, nb),
        [blk, _fixed((tb, tb)), _fixed((tb, tb)), _fixed(m_shape), _fixed(m_shape), _fixed(n_shape),
         _fixed(n_shape), _fixed((2, SSM_LANES)), _fixed((2, SSM_LANES)), _fixed((1, D_SSM)),
         _fixed((D_SSM, D_SSM)), _fixed((1, D_SSM))],
        (blk, blk_t, blk, pl.BlockSpec((1, 2, SSM_LANES), lambda b, j: (b * nb + j, 0, 0))),
        (SDS((t, D_SSM), BF16), SDS((D_SSM, t), BF16), SDS((t, D_SSM), F32), SDS((bsz * nb, 2, SSM_LANES), F32)),
        (za, sp["perm"], sp["permt"], sp["mre"], sp["mim"], sp["nre"], sp["nim"], sp["a"], sp["ap"],
         sp["dskip"], sp["glu_w"], sp["glu_b"]),
        scratch=[pltpu.VMEM((tb, SSM_LANES), F32), pltpu.VMEM((tb, SSM_LANES), F32),
                 pltpu.VMEM((2, SSM_LANES), F32)],
        sem=("arbitrary", "arbitrary"), bg=bg)


def _s5_bwd(za, y2p, doa, carries, sp, bsz, seq, tb, bg=None):
    nb = seq // tb
    seg = tb // 8
    t = bsz * seq

    def body(za_ref, y2_ref, doa_ref, car_ref, perm_ref, permt_ref, mre_ref, mim_ref, mtre_ref, mtim_ref,
             nre_ref, nim_ref, ntre_ref, ntim_ref, a_ref, ap_ref, dsk_ref, gw_ref, gwt_ref, gb_ref,
             dza_ref, dmr_ref, dmi_ref, dnr_ref, dni_ref, da_ref, ddsk_ref, dgw_ref, dgb_ref,
             hr_ref, hi_ref, gr_ref, gi_ref, cin_ref, carry_ref, rcarry_ref):
        first = jnp.logical_and(pl.program_id(0) == 0, pl.program_id(1) == 0)

        @pl.when(first)
        def _():
            for r in (dmr_ref, dmi_ref, dnr_ref, dni_ref, da_ref, ddsk_ref, dgw_ref, dgb_ref):
                r[...] = jnp.zeros_like(r)

        @pl.when(pl.program_id(1) == 0)
        def _():
            rcarry_ref[...] = jnp.zeros_like(rcarry_ref)

        carry_ref[...] = car_ref[0]
        perm = perm_ref[...]
        up = _dot(perm, za_ref[...])
        upb = up.astype(BF16)
        for bb in range(S5_BLOCKS):
            ub = upb[:, bb * S5_BLOCK_IN:(bb + 1) * S5_BLOCK_IN]
            st = slice(bb * S5_BLOCK_ST, (bb + 1) * S5_BLOCK_ST)
            hr_ref[:, st] = _dot(ub, mre_ref[bb])
            hi_ref[:, st] = _dot(ub, mim_ref[bb])
        _scan_fwd(hr_ref, hi_ref, a_ref, ap_ref, carry_ref, seg, cin_ref)

        y2 = y2_ref[...]
        y3 = _gelu(y2)
        y3b = y3.astype(BF16)
        sg = _sigmoid(_dot(y3b, gw_ref[...]) + gb_ref[...])
        d0 = doa_ref[...]
        d_hi = d0.astype(BF16)
        d1 = d0 - d_hi.astype(F32)
        d_mid = d1.astype(BF16)
        d_lo = (d1 - d_mid.astype(F32)).astype(BF16)
        doap = _dot(perm, d_hi) + _dot(perm, d_mid) + _dot(perm, d_lo)
        dgl = doap * y3 * sg * (1.0 - sg)
        dglb = dgl.astype(BF16)
        dy3 = doap * sg + _dot(dglb, gwt_ref[...])
        dgw_ref[...] += _dot_tn(y3b, dglb)
        dgb_ref[...] += jnp.sum(dgl, axis=0, keepdims=True)
        dy2 = dy3 * _gelu_grad(y2)
        ddsk_ref[...] += jnp.sum(dy2 * up, axis=0, keepdims=True)
        dyb = dy2.astype(BF16)
        for bb in range(S5_BLOCKS):
            dyc = dyb[:, bb * S5_BLOCK_IN:(bb + 1) * S5_BLOCK_IN]
            st = slice(bb * S5_BLOCK_ST, (bb + 1) * S5_BLOCK_ST)
            gr_ref[:, st] = _dot(dyc, ntre_ref[bb])
            gi_ref[:, st] = -_dot(dyc, ntim_ref[bb])
            dnr_ref[bb] += _dot_tn(hr_ref[:, st].astype(BF16), dyc)
            dni_ref[bb] += -_dot_tn(hi_ref[:, st].astype(BF16), dyc)
        _scan_bwd(gr_ref, gi_ref, hr_ref, hi_ref, cin_ref, a_ref, ap_ref, rcarry_ref, da_ref, seg)
        dus = []
        for bb in range(S5_BLOCKS):
            st = slice(bb * S5_BLOCK_ST, (bb + 1) * S5_BLOCK_ST)
            grb = gr_ref[:, st].astype(BF16)
            gib = gi_ref[:, st].astype(BF16)
            dus.append(_dot(grb, mtre_ref[bb]) + _dot(gib, mtim_ref[bb]))
            ub = upb[:, bb * S5_BLOCK_IN:(bb + 1) * S5_BLOCK_IN]
            dmr_ref[bb] += _dot_tn(ub, grb)
            dmi_ref[bb] += _dot_tn(ub, gib)
        du = jnp.concatenate(dus, axis=1) + dy2 * dsk_ref[...]
        dza_ref[...] = _dot(permt_ref[...], du.astype(BF16)).astype(BF16)

    def rev(b, j):
        return (b * nb + (nb - 1 - j), 0)

    blk = pl.BlockSpec((tb, D_SSM), rev)
    m_shape = (S5_BLOCKS, S5_BLOCK_IN, S5_BLOCK_ST)
    n_shape = (S5_BLOCKS, S5_BLOCK_ST, S5_BLOCK_IN)
    return _call(
        body, "s5_bwd", (bsz, nb),
        [blk, blk, blk, pl.BlockSpec((1, 2, SSM_LANES), lambda b, j: (b * nb + (nb - 1 - j), 0, 0)),
         _fixed((tb, tb)), _fixed((tb, tb)), _fixed(m_shape), _fixed(m_shape), _fixed(n_shape), _fixed(n_shape),
         _fixed(n_shape), _fixed(n_shape), _fixed(m_shape), _fixed(m_shape),
         _fixed((2, SSM_LANES)), _fixed((2, SSM_LANES)), _fixed((1, D_SSM)),
         _fixed((D_SSM, D_SSM)), _fixed((D_SSM, D_SSM)), _fixed((1, D_SSM))],
        (blk, _fixed(m_shape), _fixed(m_shape), _fixed(n_shape), _fixed(n_shape),
         _fixed((2, 8, SSM_LANES)), _fixed((1, D_SSM)), _fixed((D_SSM, D_SSM)), _fixed((1, D_SSM))),
        (SDS((t, D_SSM), BF16), SDS(m_shape, F32), SDS(m_shape, F32), SDS(n_shape, F32), SDS(n_shape, F32),
         SDS((2, 8, SSM_LANES), F32), SDS((1, D_SSM), F32), SDS((D_SSM, D_SSM), F32), SDS((1, D_SSM), F32)),
        (za, y2p, doa, carries, sp["perm"], sp["permt"], sp["mre"], sp["mim"], sp["mtre"], sp["mtim"],
         sp["nre"], sp["nim"], sp["ntre"], sp["ntim"], sp["a"], sp["ap"], sp["dskip"], sp["glu_w"],
         sp["glu_wt"], sp["glu_b"]),
        scratch=[pltpu.VMEM((tb, SSM_LANES), F32), pltpu.VMEM((tb, SSM_LANES), F32),
                 pltpu.VMEM((tb, SSM_LANES), F32), pltpu.VMEM((tb, SSM_LANES), F32),
                 pltpu.VMEM((2, 8, SSM_LANES), F32), pltpu.VMEM((2, SSM_LANES), F32),
                 pltpu.VMEM((2, SSM_LANES), F32)],
        sem=("arbitrary", "arbitrary"), bg=bg)


def _gmlp_spatial(ws_ref, vb):
    lane = lax.broadcasted_iota(jnp.int32, (CHUNK, 128), 1)
    parts = []
    for j in range(GMLP_HEADS // 2):
        vp = vb[:, 128 * j:128 * (j + 1)]
        parts.append(jnp.where(lane < GMLP_HEAD_DIM, _dot(ws_ref[2 * j], vp), _dot(ws_ref[2 * j + 1], vp)))
    return jnp.concatenate(parts, axis=1)


def _gmlp_fwd(zuv, ln_g, ln_b, wsm, bias, bg=None):
    t = zuv.shape[0]

    def body(z_ref, g_ref, b_ref, ws_ref, bias_ref, out_ref, outt_ref):
        u = _gelu(z_ref[:, 0:D_GMLP].astype(F32))
        v0 = _gelu(z_ref[:, D_GMLP:2 * D_GMLP].astype(F32))
        v, _, _ = _ln_fwd(v0, g_ref[...], b_ref[...])
        s = _gmlp_spatial(ws_ref, v.astype(BF16)) + bias_ref[...]
        out = (u * s).astype(BF16)
        out_ref[...] = out
        outt_ref[...] = out.T

    return _call(
        body, "gmlp_fwd", (t // CHUNK,),
        [_rows(CHUNK, 2 * D_GMLP), _fixed((1, D_GMLP)), _fixed((1, D_GMLP)),
         _fixed((GMLP_HEADS, CHUNK, CHUNK)), _fixed((CHUNK, D_GMLP))],
        (_rows(CHUNK, D_GMLP), _cols(D_GMLP, CHUNK)), (SDS((t, D_GMLP), BF16), SDS((D_GMLP, t), BF16)),
        (zuv, ln_g, ln_b, wsm, bias), sem=("parallel",), bg=bg)


def _gmlp_bwd(zuv, dgm, ln_g, ln_b, wsm, wsmt, bias, bg=None):
    t = zuv.shape[0]

    def body(z_ref, d_ref, g_ref, b_ref, ws_ref, wst_ref, bias_ref,
             dz_ref, dws_ref, dbias_ref, dg_ref, db_ref):
        @pl.when(pl.program_id(0) == 0)
        def _():
            for r in (dws_ref, dbias_ref, dg_ref, db_ref):
                r[...] = jnp.zeros_like(r)

        zu = z_ref[:, 0:D_GMLP].astype(F32)
        zv = z_ref[:, D_GMLP:2 * D_GMLP].astype(F32)
        u = _gelu(zu)
        v0 = _gelu(zv)
        gam = g_ref[...]
        v, vhat, rstd = _ln_fwd(v0, gam, b_ref[...])
        vb = v.astype(BF16)
        s = _gmlp_spatial(ws_ref, vb) + bias_ref[...]
        d = d_ref[...]
        dz_ref[:, 0:D_GMLP] = (d * s * _gelu_grad(zu)).astype(BF16)
        ds = d * u
        dbias_ref[...] += ds
        dsb = ds.astype(BF16)
        lane = lax.broadcasted_iota(jnp.int32, (CHUNK, 128), 1)
        tril = (lax.broadcasted_iota(jnp.int32, (CHUNK, CHUNK), 0)
                >= lax.broadcasted_iota(jnp.int32, (CHUNK, CHUNK), 1))
        zero_b = jnp.zeros((CHUNK, 128), BF16)
        parts = []
        for j in range(GMLP_HEADS // 2):
            dsp = dsb[:, 128 * j:128 * (j + 1)]
            vp = vb[:, 128 * j:128 * (j + 1)]
            parts.append(jnp.where(lane < GMLP_HEAD_DIM, _dot(wst_ref[2 * j], dsp),
                                   _dot(wst_ref[2 * j + 1], dsp)))
            lo = jnp.where(lane < GMLP_HEAD_DIM, dsp, zero_b)
            hi = jnp.where(lane < GMLP_HEAD_DIM, zero_b, dsp)
            dws_ref[2 * j] += jnp.where(tril, _dot_nt(lo, vp), 0.0)
            dws_ref[2 * j + 1] += jnp.where(tril, _dot_nt(hi, vp), 0.0)
        dv = jnp.concatenate(parts, axis=1)
        dg_ref[...] += jnp.sum(dv * vhat, axis=0, keepdims=True)
        db_ref[...] += jnp.sum(dv, axis=0, keepdims=True)
        dz_ref[:, D_GMLP:2 * D_GMLP] = (_ln_bwd(dv, vhat, rstd, gam) * _gelu_grad(zv)).astype(BF16)

    return _call(
        body, "gmlp_bwd", (t // CHUNK,),
        [_rows(CHUNK, 2 * D_GMLP), _rows(CHUNK, D_GMLP), _fixed((1, D_GMLP)), _fixed((1, D_GMLP)),
         _fixed((GMLP_HEADS, CHUNK, CHUNK)), _fixed((GMLP_HEADS, CHUNK, CHUNK)), _fixed((CHUNK, D_GMLP))],
        (_rows(CHUNK, 2 * D_GMLP), _fixed((GMLP_HEADS, CHUNK, CHUNK)), _fixed((CHUNK, D_GMLP)),
         _fixed((1, D_GMLP)), _fixed((1, D_GMLP))),
        (SDS((t, 2 * D_GMLP), BF16), SDS((GMLP_HEADS, CHUNK, CHUNK), F32), SDS((CHUNK, D_GMLP), F32),
         SDS((1, D_GMLP), F32), SDS((1, D_GMLP), F32)),
        (zuv, dgm, ln_g, ln_b, wsm, wsmt, bias), sem=("arbitrary",), bg=bg)


def _mixout_fwd(x1, s5o, gm, gab, ua, ub, wmo, g, b, tm, bg=None):
    t = x1.shape[0]

    def body(x_ref, s_ref, m_ref, gab_ref, ua_ref, ub_ref, wmo_ref, g_ref, b_ref,
             xn_ref, xh_ref, rstd_ref):
        ya = _dot(s_ref[...], ua_ref[...])
        yb = _dot(m_ref[...], ub_ref[...])
        mix = (_sigmoid(gab_ref[:, 0:D_MODEL].astype(F32)) * ya
               + _sigmoid(gab_ref[:, D_MODEL:2 * D_MODEL].astype(F32)) * yb)
        r = ALPHA * x_ref[...] + _dot(mix.astype(BF16), wmo_ref[...])
        y, xh, rstd = _ln_fwd(r, g_ref[...], b_ref[...])
        xn_ref[...] = y
        xh_ref[...] = xh
        rstd_ref[...] = rstd

    return _call(
        body, "mixout_fwd", (t // tm,),
        [_rows(tm, D_MODEL), _rows(tm, D_SSM), _rows(tm, D_GMLP), _rows(tm, 2 * D_MODEL),
         _resident((D_SSM, D_MODEL)), _resident((D_GMLP, D_MODEL)), _resident((D_MODEL, D_MODEL)),
         _fixed((1, D_MODEL)), _fixed((1, D_MODEL))],
        (_rows(tm, D_MODEL), _rows(tm, D_MODEL), _rows(tm, 1)),
        (SDS((t, D_MODEL), F32), SDS((t, D_MODEL), F32), SDS((t, 1), F32)),
        (x1, s5o, gm, gab, ua, ub, wmo, g, b), sem=("parallel",), bg=bg)


def _mixout_bwd(dx2, xh, rstd, s5o, gm, gab, ua, ub, wmo, g, tm, bg=None):
    t = dx2.shape[0]

    def body(d_ref, xh_ref, rstd_ref, s_ref, m_ref, gab_ref, ua_ref, ub_ref, wmo_ref, g_ref,
             dx1_ref, dmx_ref, mb_ref, dya_ref, dyb_ref, ds5_ref, dgm_ref, dgab_ref, dg_ref, db_ref):
        @pl.when(pl.program_id(0) == 0)
        def _():
            dg_ref[...] = jnp.zeros_like(dg_ref)
            db_ref[...] = jnp.zeros_like(db_ref)

        dy = d_ref[...]
        xhv = xh_ref[...]
        dr = _ln_bwd(dy, xhv, rstd_ref[...], g_ref[...])
        dg_ref[...] += jnp.sum(dy * xhv, axis=0, keepdims=True)
        db_ref[...] += jnp.sum(dy, axis=0, keepdims=True)
        dx1_ref[...] = ALPHA * dr
        drb = dr.astype(BF16)
        dmx_ref[...] = drb
        dm = _dot_nt(drb, wmo_ref[...])
        ya = _dot(s_ref[...], ua_ref[...])
        yb = _dot(m_ref[...], ub_ref[...])
        sa = _sigmoid(gab_ref[:, 0:D_MODEL].astype(F32))
        sb = _sigmoid(gab_ref[:, D_MODEL:2 * D_MODEL].astype(F32))
        mb_ref[...] = (sa * ya + sb * yb).astype(BF16).T
        dya = (dm * sa).astype(BF16)
        dyb = (dm * sb).astype(BF16)
        dya_ref[...] = dya
        dyb_ref[...] = dyb
        dgab_ref[:, 0:D_MODEL] = (dm * ya * sa * (1.0 - sa)).astype(BF16)
        dgab_ref[:, D_MODEL:2 * D_MODEL] = (dm * yb * sb * (1.0 - sb)).astype(BF16)
        ds5_ref[...] = _dot_nt(dya, ua_ref[...])
        dgm_ref[...] = _dot_nt(dyb, ub_ref[...])

    return _call(
        body, "mixout_bwd", (t // tm,),
        [_rows(tm, D_MODEL), _rows(tm, D_MODEL), _rows(tm, 1), _rows(tm, D_SSM), _rows(tm, D_GMLP),
         _rows(tm, 2 * D_MODEL), _resident((D_SSM, D_MODEL)), _resident((D_GMLP, D_MODEL)),
         _resident((D_MODEL, D_MODEL)), _fixed((1, D_MODEL))],
        (_rows(tm, D_MODEL), _rows(tm, D_MODEL), _cols(D_MODEL, tm), _rows(tm, D_MODEL),
         _rows(tm, D_MODEL), _rows(tm, D_SSM), _rows(tm, D_GMLP), _rows(tm, 2 * D_MODEL),
         _fixed((1, D_MODEL)), _fixed((1, D_MODEL))),
        (SDS((t, D_MODEL), F32), SDS((t, D_MODEL), BF16), SDS((D_MODEL, t), BF16),
         SDS((t, D_MODEL), BF16), SDS((t, D_MODEL), BF16), SDS((t, D_SSM), F32),
         SDS((t, D_GMLP), F32), SDS((t, 2 * D_MODEL), BF16),
         SDS((1, D_MODEL), F32), SDS((1, D_MODEL), F32)),
        (dx2, xh, rstd, s5o, gm, gab, ua, ub, wmo, g), sem=("arbitrary",), bg=bg)


def _ple_loss(x3, p, tgt, wpg, wpp, tm, bg=None):
    t = x3.shape[0]

    def body(x_ref, p_ref, t_ref, wpg_ref, wpp_ref, dx_ref, xb_ref, pb_ref, dq_ref, de_ref, loss_ref):
        @pl.when(pl.program_id(0) == 0)
        def _():
            loss_ref[...] = jnp.zeros_like(loss_ref)

        x3v = x_ref[...]
        xb = x3v.astype(BF16)
        pb = p_ref[...].astype(BF16)
        xb_ref[...] = xb.T
        pb_ref[...] = pb.T
        s = _sigmoid(_dot(xb, wpg_ref[...]))
        e = _dot(pb, wpp_ref[...])
        diff = x3v + s * e - t_ref[...]
        loss_ref[...] += jnp.sum(diff * diff, axis=0, keepdims=True)
        dout = diff * (1.0 / D_MODEL)
        de_ref[...] = (dout * s).astype(BF16)
        dq = (dout * e * s * (1.0 - s)).astype(BF16)
        dq_ref[...] = dq
        dx_ref[...] = dout + _dot_nt(dq, wpg_ref[...])

    return _call(
        body, "ple_loss", (t // tm,),
        [_rows(tm, D_MODEL), _rows(tm, PLE_DIM), _rows(tm, D_MODEL),
         _resident((D_MODEL, D_MODEL)), _resident((PLE_DIM, D_MODEL))],
        (_rows(tm, D_MODEL), _cols(D_MODEL, tm), _cols(PLE_DIM, tm), _rows(tm, D_MODEL),
         _rows(tm, D_MODEL), _fixed((1, D_MODEL))),
        (SDS((t, D_MODEL), F32), SDS((D_MODEL, t), BF16), SDS((PLE_DIM, t), BF16),
         SDS((t, D_MODEL), BF16), SDS((t, D_MODEL), BF16), SDS((1, D_MODEL), F32)),
        (x3, p, tgt, wpg, wpp), sem=("arbitrary",), bg=bg)


def _s5_discretise(lre, lim, log_dt, bre, bim):
    dt = jnp.exp(log_dt)[:, None]
    mag = jnp.exp(lre * dt)
    abr = mag * jnp.cos(lim * dt)
    abi = mag * jnp.sin(lim * dt)
    nr = abr - 1.0
    ni = abi
    den = lre * lre + lim * lim
    cr = ((nr * lre + ni * lim) / den)[..., None]
    ci = ((ni * lre - nr * lim) / den)[..., None]
    return abr, abi, cr * bre - ci * bim, cr * bim + ci * bre


def _block_diag_in(bb):
    v = bb.reshape(S5_BLOCKS, 8, SSM_STATE, SSM_GROUP_CH).transpose(0, 1, 3, 2)
    return jnp.einsum("bgip,gh->bgihp", v, jnp.eye(8, dtype=bb.dtype)).reshape(
        S5_BLOCKS, S5_BLOCK_IN, S5_BLOCK_ST)


def _block_diag_in_t(dm):
    v = dm.reshape(S5_BLOCKS, 8, SSM_GROUP_CH, 8, SSM_STATE)
    d = jnp.einsum("bgihp,gh->bgip", v, jnp.eye(8, dtype=dm.dtype))
    return d.transpose(0, 1, 3, 2).reshape(SSM_GROUPS, SSM_STATE, SSM_GROUP_CH)


def _block_diag_out(cc):
    v = cc.reshape(S5_BLOCKS, 8, SSM_GROUP_CH, SSM_STATE)
    return jnp.einsum("bgip,gh->bgphi", v, jnp.eye(8, dtype=cc.dtype)).reshape(
        S5_BLOCKS, S5_BLOCK_ST, S5_BLOCK_IN)


def _block_diag_out_t(dn):
    v = dn.reshape(S5_BLOCKS, 8, SSM_STATE, 8, SSM_GROUP_CH)
    d = jnp.einsum("bgphi,gh->bgip", v, jnp.eye(8, dtype=dn.dtype))
    return d.reshape(SSM_GROUPS, SSM_GROUP_CH, SSM_STATE)


def _s5_setup(lre, lim, log_dt, bre, bim, cre, cim, d_skip, glu_w, glu_b, tb):
    seg = tb // 8
    abr, abi, bbr, bbi = _s5_discretise(lre, lim, log_dt, bre, bim)
    pr, pi = abr, abi
    for _ in range(int(math.log2(seg))):
        pr, pi = pr * pr - pi * pi, 2.0 * pr * pi
    rows = jnp.arange(tb)
    src = (rows % 8) * seg + rows // 8
    perm = (src[:, None] == jnp.arange(tb)[None, :]).astype(BF16)
    mre = _block_diag_in(bbr)
    mim = _block_diag_in(bbi)
    nre = _block_diag_out(cre)
    nim = _block_diag_out(cim)
    return {
        "perm": perm, "permt": perm.T,
        "mre": mre.astype(BF16), "mim": mim.astype(BF16),
        "mtre": mre.transpose(0, 2, 1).astype(BF16), "mtim": mim.transpose(0, 2, 1).astype(BF16),
        "nre": nre.astype(BF16), "nim": nim.astype(BF16),
        "ntre": nre.transpose(0, 2, 1).astype(BF16), "ntim": nim.transpose(0, 2, 1).astype(BF16),
        "a": jnp.stack([abr.reshape(-1), abi.reshape(-1)]),
        "ap": jnp.stack([pr.reshape(-1), pi.reshape(-1)]),
        "dskip": d_skip.reshape(1, D_SSM), "glu_w": glu_w, "glu_wt": glu_w.T,
        "glu_b": glu_b.reshape(1, D_SSM),
    }


BIG = ("ffn1_w_in", "ffn1_w_out", "mix_w_in", "ssm_glu_w", "up_a", "up_b", "mix_w_out",
       "ffn2_w_in", "ffn2_w_out", "ple_w_proj", "ple_w_gate")
BIG_AXIS = {"ffn1_w_in": 1, "ffn1_w_out": 0, "mix_w_in": 1, "ssm_glu_w": 0, "up_a": 1, "up_b": 1,
            "mix_w_out": 0, "ffn2_w_in": 1, "ffn2_w_out": 0, "ple_w_proj": 1, "ple_w_gate": 0}
SHARD_MAJOR = 2
GATHER_AXIS = dict(BIG_AXIS, ffn1_w_in=SHARD_MAJOR, ffn2_w_in=SHARD_MAJOR)
GATHER_ORDER = (("ffn1_w_in",), ("ffn1_w_out",), ("mix_w_in",), ("ssm_glu_w", "up_a", "up_b", "mix_w_out"),
                ("ffn2_w_in",), ("ffn2_w_out", "ple_w_gate", "ple_w_proj"))
GATHER_FIRST_ID = 1
REDUCE_FIRST_ID = 7
SMALL = ("ln1_g", "ln1_b", "ssm_lambda_re", "ssm_lambda_im", "ssm_log_dt", "ssm_b_re", "ssm_b_im",
         "ssm_c_re", "ssm_c_im", "ssm_d", "ssm_glu_b", "gmlp_ln_g", "gmlp_ln_b", "gmlp_w_s",
         "gmlp_b_s", "ln2_g", "ln2_b", "ln3_g", "ln3_b")
SMALL_VIEW = {"ssm_b_re": (SSM_GROUPS, SSM_STATE * SSM_GROUP_CH), "ssm_b_im": (SSM_GROUPS, SSM_STATE * SSM_GROUP_CH)}


def _small_view(k, a):
    return a.reshape(SMALL_VIEW[k]) if k in SMALL_VIEW else a


def _place():
    return lax.axis_index("x"), lax.axis_index("y"), lax.axis_index("c")


def _other_chips(x, y):
    return [(1 - x, y), (x, 1 - y), (1 - x, 1 - y)]


def _window(ref, shard_shape, axis, chip, half):
    r, c = shard_shape
    hr = r // 2
    if axis == SHARD_MAJOR:
        return ref.at[chip] if half is None else ref.at[chip, pl.ds(half * hr, hr), :]
    if axis == 0:
        if half is None:
            return ref.at[pl.ds(chip * r, r), :]
        return ref.at[pl.ds(chip * r + half * hr, hr), :]
    if half is None:
        return ref.at[:, pl.ds(chip * c, c)]
    return ref.at[pl.ds(half * hr, hr), pl.ds(chip * c, c)]


def _gather_weights(shards, axes):
    n = len(shards)
    shapes = [s.shape for s in shards]
    full = [{0: (4 * r, c), 1: (r, 4 * c), SHARD_MAJOR: (4, r, c)}[ax] for (r, c), ax in zip(shapes, axes)]

    def remote(sems, i, k, src, dst, to):
        return pltpu.make_async_remote_copy(src_ref=src, dst_ref=dst, send_sem=sems[0].at[6 * i + k],
                                            recv_sem=sems[1].at[6 * i + k], device_id=to, device_id_type=MESH)

    def own_copies(ins, outs, sems):
        x, y, c = _place()
        me = 2 * x + y
        cps = []
        for i in range(n):
            hr = shapes[i][0] // 2
            mine = ins[i].at[pl.ds(c * hr, hr), :]
            for j, (cx, cy) in enumerate(_other_chips(x, y)):
                cps.append(remote(sems, i, j, mine, _window(outs[i], shapes[i], axes[i], me, c), (cx, cy, c)))
        local = [pltpu.make_async_copy(ins[i], _window(outs[i], shapes[i], axes[i], me, None), sems[2].at[i])
                 for i in range(n)]
        return cps, local

    def start(ins, outs, sems):
        cps, local = own_copies(ins, outs, sems)
        for cp in local + cps:
            cp.start()

    def finish(ins, outs, sems):
        x, y, c = _place()
        sibling = (x, y, 1 - c)
        passed = []
        for j, (cx, cy) in enumerate(_other_chips(x, y)):
            for i in range(n):
                w = _window(outs[i], shapes[i], axes[i], 2 * cx + cy, c)
                remote(sems, i, j, w, w, (cx, cy, c)).wait_recv()
                cp = remote(sems, i, 3 + j, w, w, sibling)
                cp.start()
                passed.append(cp)
        for j, (cx, cy) in enumerate(_other_chips(x, y)):
            for i in range(n):
                w = _window(outs[i], shapes[i], axes[i], 2 * cx + cy, 1 - c)
                remote(sems, i, 3 + j, w, w, sibling).wait_recv()
        cps, local = own_copies(ins, outs, sems)
        for cp in cps + passed:
            cp.wait_send()
        for cp in local:
            cp.wait()

    return _Exchange(shards, [SDS(f, BF16) for f in full],
                     [pltpu.SemaphoreType.DMA((6 * n,)), pltpu.SemaphoreType.DMA((6 * n,)),
                      pltpu.SemaphoreType.DMA((n,))], start, finish)


def _scatter_grads(parts, shapes, axes):
    n = len(parts)

    def copies(ins, outs, sems):
        x, y, c = _place()
        return [pltpu.make_async_remote_copy(
            src_ref=_window(ins[i], shapes[i], axes[i], 2 * cx + cy, None), dst_ref=outs[i].at[j],
            send_sem=sems[0].at[3 * i + j], recv_sem=sems[1].at[3 * i + j],
            device_id=(cx, cy, c), device_id_type=MESH)
            for i in range(n) for j, (cx, cy) in enumerate(_other_chips(x, y))]

    def start(ins, outs, sems):
        for cp in copies(ins, outs, sems):
            cp.start()

    def finish(ins, outs, sems):
        for cp in copies(ins, outs, sems):
            cp.wait()

    return _Exchange(parts, [SDS((3,) + tuple(s), BF16) for s in shapes],
                     [pltpu.SemaphoreType.DMA((3 * n,)), pltpu.SemaphoreType.DMA((3 * n,))], start, finish)


def _swap_halves(parts, shapes, axes):
    n = len(parts)

    def copies(ins, outs, sems):
        x, y, c = _place()
        cps = []
        for i in range(n):
            r, _ = shapes[i]
            hr = r // 2
            if axes[i] == 0:
                cps += [pltpu.make_async_remote_copy(
                    src_ref=ins[i].at[pl.ds(k * r + (1 - c) * hr, hr), :], dst_ref=outs[i].at[k],
                    send_sem=sems[0].at[i], recv_sem=sems[1].at[i], device_id=(x, y, 1 - c),
                    device_id_type=MESH) for k in range(4)]
            else:
                cps.append(pltpu.make_async_remote_copy(
                    src_ref=ins[i].at[pl.ds((1 - c) * hr, hr), :], dst_ref=outs[i],
                    send_sem=sems[0].at[i], recv_sem=sems[1].at[i], device_id=(x, y, 1 - c),
                    device_id_type=MESH))
        return cps

    def start(ins, outs, sems):
        for cp in copies(ins, outs, sems):
            cp.start()

    def finish(ins, outs, sems):
        x, y, c = _place()
        for i in range(n):
            pltpu.make_async_remote_copy(src_ref=outs[i], dst_ref=outs[i], send_sem=sems[0].at[i],
                                         recv_sem=sems[1].at[i], device_id=(x, y, 1 - c),
                                         device_id_type=MESH).wait()

    out = [SDS((4, r // 2, c), BF16) if ax == 0 else SDS((r // 2, 4 * c), BF16)
           for (r, c), ax in zip(shapes, axes)]
    return _Exchange(parts, out, [pltpu.SemaphoreType.DMA((n,)), pltpu.SemaphoreType.DMA((n,))], start, finish)


def _scatter_halves(pres, shapes):
    n = len(pres)

    def copies(ins, outs, sems):
        x, y, c = _place()
        return [pltpu.make_async_remote_copy(
            src_ref=ins[i].at[1 + j], dst_ref=outs[i].at[j], send_sem=sems[0].at[3 * i + j],
            recv_sem=sems[1].at[3 * i + j], device_id=(cx, cy, c), device_id_type=MESH)
            for i in range(n) for j, (cx, cy) in enumerate(_other_chips(x, y))]

    def start(ins, outs, sems):
        for cp in copies(ins, outs, sems):
            cp.start()

    def finish(ins, outs, sems):
        for cp in copies(ins, outs, sems):
            cp.wait()

    return _Exchange(pres, [SDS((3, r // 2, c), BF16) for r, c in shapes],
                     [pltpu.SemaphoreType.DMA((3 * n,)), pltpu.SemaphoreType.DMA((3 * n,))], start, finish)


def _swap_with_sibling(arrs):
    n = len(arrs)

    def copies(ins, outs, sems):
        x, y, c = _place()
        return [pltpu.make_async_remote_copy(src_ref=ins[i], dst_ref=outs[i], send_sem=sems[0].at[i],
                                             recv_sem=sems[1].at[i], device_id=(x, y, 1 - c),
                                             device_id_type=MESH) for i in range(n)]

    def start(ins, outs, sems):
        for cp in copies(ins, outs, sems):
            cp.start()

    def finish(ins, outs, sems):
        for cp in copies(ins, outs, sems):
            cp.wait()

    return _Exchange(arrs, [SDS(a.shape, a.dtype) for a in arrs],
                     [pltpu.SemaphoreType.DMA((n,)), pltpu.SemaphoreType.DMA((n,))], start, finish)


def _gather_small(arrs):
    n = len(arrs)

    def copy(sems, outs, i, k, block, to, src=None):
        px, py, pc = block
        dst = outs[i].at[4 * px + 2 * py + pc]
        return pltpu.make_async_remote_copy(
            src_ref=dst if src is None else src, dst_ref=dst, send_sem=sems[0].at[7 * i + k],
            recv_sem=sems[1].at[7 * i + k], device_id=to, device_id_type=MESH)

    direct = [math.prod(a.shape) * 4 <= DIRECT_GATHER_BYTES for a in arrs]

    def own_copies(ins, outs, sems):
        x, y, c = _place()
        cps = []
        for i in range(n):
            cps.append(copy(sems, outs, i, 0, (x, y, c), (x, y, 1 - c), src=ins[i]))
            for j, (cx, cy) in enumerate(_other_chips(x, y)):
                cps.append(copy(sems, outs, i, 1 + j, (x, y, c), (cx, cy, c), src=ins[i]))
                if direct[i]:
                    cps.append(copy(sems, outs, i, 4 + j, (x, y, c), (cx, cy, 1 - c), src=ins[i]))
        local = [pltpu.make_async_copy(ins[i], outs[i].at[4 * x + 2 * y + c], sems[2].at[i]) for i in range(n)]
        return cps, local

    def start(ins, outs, sems):
        cps, local = own_copies(ins, outs, sems)
        for cp in local + cps:
            cp.start()

    def finish(ins, outs, sems):
        x, y, c = _place()
        passed = []
        for j, (cx, cy) in enumerate(_other_chips(x, y)):
            for i in range(n):
                copy(sems, outs, i, 1 + j, (cx, cy, c), (x, y, c)).wait_recv()
                if not direct[i]:
                    cp = copy(sems, outs, i, 4 + j, (cx, cy, c), (x, y, 1 - c))
                    cp.start()
                    passed.append(cp)
        for i in range(n):
            copy(sems, outs, i, 0, (x, y, 1 - c), (x, y, c)).wait_recv()
            for j, (cx, cy) in enumerate(_other_chips(x, y)):
                copy(sems, outs, i, 4 + j, (cx, cy, 1 - c), (x, y, c)).wait_recv()
        cps, local = own_copies(ins, outs, sems)
        for cp in cps + passed:
            cp.wait_send()
        for cp in local:
            cp.wait()

    return _Exchange(arrs, [SDS((N_DEV,) + a.shape, F32) for a in arrs],
                     [pltpu.SemaphoreType.DMA((7 * n,)), pltpu.SemaphoreType.DMA((7 * n,)),
                      pltpu.SemaphoreType.DMA((n,))], start, finish)


def _local_step(x, p, tgt, wb, ws, shards=None, opt=None):
    bsz, seq, _ = x.shape
    t = bsz * seq
    tm = min(256, t)
    tb = min(256, seq)
    x0 = x.reshape(t, D_MODEL)
    p0 = p.reshape(t, PLE_DIM)
    tg = tgt.reshape(t, D_MODEL)
    row = lambda v: v.reshape(1, -1)
    dist = shards is not None
    wb = dict(wb)
    recv, sums, other, gathered = {}, {}, {}, {}
    gb = {}
    gs = {}
    shape_of, axis_of = {}, {}
    chip = None
    if dist:
        shape_of = {k: tuple(shards[k].shape) for k in BIG}
        axis_of = dict(BIG_AXIS)
        for q in range(LAST_PIECES):
            shape_of[LAST_PIECE % q] = (D_MODEL // LAST_PIECES, shape_of["ffn1_w_in"][1])
            axis_of[LAST_PIECE % q] = 1
        xi, yi, ci = _place()
        chip = (2 * xi + yi).astype(jnp.int32).reshape(1)
        ids = jnp.stack([2 * xi + yi] + [2 * cx + cy for cx, cy in _other_chips(xi, yi)] + [ci]).astype(jnp.int32)
    halfbuf, pre = {}, {}

    def gather(names):
        return _gather_weights([shards[k] for k in names], [GATHER_AXIS[k] for k in names]) if dist else None

    def exchange(scat=(), swap=(), halves=(), scat2=(), swap2=(), extra=None, after=None):
        if not dist:
            return None, []
        after = order[0] if after is None else after
        parts, tags = [], []
        if scat:
            parts.append(_scatter_grads([gb[k][1] for k in scat], [shape_of[k] for k in scat],
                                        [axis_of[k] for k in scat]))
            tags.append((recv, scat))
        if swap:
            for k in swap:
                sums[k] = order[0] = _sum_blocks(gb[k][0], recv[k], shape_of[k], axis_of[k], chip, "sum_" + k,
                                                 order[0])
            parts.append(_swap_with_sibling([sums[k] for k in swap]))
            tags.append((other, swap))
        if halves:
            parts.append(_swap_halves([gb[k][1] for k in halves], [shape_of[k] for k in halves],
                                      [axis_of[k] for k in halves]))
            tags.append((halfbuf, halves))
        if scat2:
            for k in scat2:
                pre[k] = _presum(gb[k][0], halfbuf[k], shape_of[k], axis_of[k], ids, "presum_" + k, order[0])
                order[0] = pre[k][0]
            parts.append(_scatter_halves([pre[k][1] for k in scat2], [shape_of[k] for k in scat2]))
            tags.append((recv, scat2))
        if swap2:
            for k in swap2:
                sums[k] = order[0] = _sum_half(pre[k][0], recv[k], "sum_" + k, order[0])
            parts.append(_swap_with_sibling([sums[k] for k in swap2]))
            tags.append((other, swap2))
        if extra is not None:
            parts.append(extra[0])
            tags.append((extra[1], extra[2]))
        return (_join(parts), tags) if parts else (None, [])

    def take(ex_tags, got):
        ex, tags = ex_tags
        if ex is not None:
            for (dst, names), (o0, o1) in zip(tags, ex.cuts):
                dst.update(zip(names, got[o0:o1]))

    order = [None]

    def ordered(builder, *args, **kw):
        res = builder(*args, bg=order[0] if dist else None, **kw)
        order[0] = res[0][0]
        return res

    launched = []

    def launch(ex_tags):
        if ex_tags[0] is not None:
            n = len(launched)
            launched.append(n)
            take(ex_tags, _run_exchange_on_sequencer(ex_tags[0], "reduce_%d" % n, REDUCE_FIRST_ID + n))

    small_shape = {k: _small_view(k, v).shape for k, v in ws.items()}
    small_shape["loss_rows"] = (1, D_MODEL)
    ws = {k: v if (v.ndim == 2 and k != "ssm_log_dt") else v[0] for k, v in ws.items()}
    tril = jnp.tril(jnp.ones((CHUNK, CHUNK), dtype=bool))
    wsm = jnp.where(tril[None], ws["gmlp_w_s"], 0.0)
    wsm_b = wsm.astype(BF16)
    wsmt_b = wsm.transpose(0, 2, 1).astype(BF16)
    bias = jnp.repeat(ws["gmlp_b_s"].T, GMLP_HEAD_DIM, axis=1)

    tf = min(512, t)
    if dist:
        for gi, names in enumerate(GATHER_ORDER):
            wb.update(zip(names, _run_exchange_on_sequencer(gather(names), "gather_%d" % gi, GATHER_FIRST_ID + gi)))
    (x0b, h1, a1), _ = _ffn_proj(x0, wb["ffn1_w_in"], tf, "ffn1_proj")
    (x1, xh1, rstd1), _ = _ffn_out(x0, a1, wb["ffn1_w_out"], row(ws["ln1_g"]), row(ws["ln1_b"]), tf, "ffn1_out")
    sp = _s5_setup(ws["ssm_lambda_re"], ws["ssm_lambda_im"], ws["ssm_log_dt"], ws["ssm_b_re"],
                   ws["ssm_b_im"], ws["ssm_c_re"], ws["ssm_c_im"], ws["ssm_d"], wb["ssm_glu_w"],
                   ws["ssm_glu_b"], tb)
    (x1b, za, zuv, gab), _ = _mixin_fwd(x1, wb["mix_w_in"], tm)
    (s5o, s5ot, y2p, carries), _ = _s5_fwd(za, sp, bsz, seq, tb)
    (gm, gmt), _ = _gmlp_fwd(zuv, row(ws["gmlp_ln_g"]), row(ws["gmlp_ln_b"]), wsm_b, bias)
    (x2, xh2, rstd2), _ = _mixout_fwd(x1, s5o, gm, gab, wb["up_a"], wb["up_b"], wb["mix_w_out"],
                                           row(ws["ln2_g"]), row(ws["ln2_b"]), tm)
    (x2b, h2, a2), _ = _ffn_proj(x2, wb["ffn2_w_in"], tf, "ffn2_proj")
    (x3, xh3, rstd3), _ = _ffn_out(x2, a2, wb["ffn2_w_out"], row(ws["ln3_g"]), row(ws["ln3_b"]), tf, "ffn2_out")
    (dx3, x3b, pb, dq, de, loss_rows), _ = _ple_loss(x3, p0, tg, wb["ple_w_gate"], wb["ple_w_proj"], tm)
    order[0] = dx3
    gb["ple_w_gate"], _ = ordered(_tn_matmul, x3b, dq, "dw_ple_gate", 1024, 1024, a_t=True)
    gb["ple_w_proj"], _ = ordered(_tn_matmul, pb, de, "dw_ple_proj", 256, 1024, a_t=True)
    launch(exchange(scat=("ple_w_gate", "ple_w_proj")))
    (dx2, dh2, df2, gs["ln3_g"], gs["ln3_b"]), _ = ordered(
        _ffn_bwd, dx3, xh3, rstd3, h2, wb["ffn2_w_in"], wb["ffn2_w_out"], row(ws["ln3_g"]), tm, "ffn2_bwd")
    gb["ffn2_w_out"], _ = ordered(_tn_matmul, a2, df2, "dw_ffn2_out", 1408, 1024)
    launch(exchange(scat=("ffn2_w_out",)))
    gb["ffn2_w_in"], _ = ordered(_tn_matmul, x2b, dh2, "dw_ffn2_in", 1024, 1408, a_t=True)
    launch(exchange(scat=("ffn2_w_in",), swap=("ple_w_gate", "ple_w_proj")))
    (dx1a, dmx, mb, dya, dyb, ds5, dgm, dgab, gs["ln2_g"], gs["ln2_b"]), _ = ordered(
        _mixout_bwd, dx2, xh2, rstd2, s5o, gm, gab, wb["up_a"], wb["up_b"], wb["mix_w_out"], row(ws["ln2_g"]), tm)
    gb["mix_w_out"], _ = ordered(_tn_matmul, mb, dmx, "dw_mix_out", 1024, 1024, a_t=True)
    gb["up_a"], _ = ordered(_tn_matmul, s5ot, dya, "dw_up_a", 512, 1024, a_t=True)
    gb["up_b"], _ = ordered(_tn_matmul, gmt, dyb, "dw_up_b", 512, 1024, a_t=True)
    launch(exchange(scat=("mix_w_out", "up_a", "up_b"), swap=("ffn2_w_out",)))
    (dza, dmr, dmi, dnr, dni, da, ddsk, dgw, dgb), _ = ordered(_s5_bwd, za, y2p, ds5, carries, sp, bsz, seq, tb)
    gb["ssm_glu_w"] = (dgw, dgw.astype(BF16))
    launch(exchange(scat=("ssm_glu_w",), swap=("ffn2_w_in",)))
    (dzuv, dws, dbias, gs["gmlp_ln_g"], gs["gmlp_ln_b"]), _ = ordered(
        _gmlp_bwd, zuv, dgm, row(ws["gmlp_ln_g"]), row(ws["gmlp_ln_b"]), wsm_b, wsmt_b, bias)
    (dx1,), _ = ordered(_mixin_bwd, dx1a, dza, dzuv, dgab, wb["mix_w_in"], tm)
    g_mi, _ = ordered(_tn_matmul, x1b, dza, "dw_mix_in_a", 1024, 512, 0, 3584, a_t=True)
    g_mi, _ = ordered(_tn_matmul, x1b, dzuv, "dw_mix_in_uv", 1024, 512, 1, 3584, g_mi, a_t=True)
    gb["mix_w_in"], _ = ordered(_tn_matmul, x1b, dgab, "dw_mix_in_g", 1024, 512, 3, 3584, g_mi, a_t=True)
    launch(exchange(swap=("mix_w_out", "up_a", "up_b", "ssm_glu_w")))

    d_abr = da[0].sum(axis=0).reshape(SSM_GROUPS, SSM_STATE)
    d_abi = da[1].sum(axis=0).reshape(SSM_GROUPS, SSM_STATE)
    _, vjp = jax.vjp(_s5_discretise, ws["ssm_lambda_re"], ws["ssm_lambda_im"], ws["ssm_log_dt"],
                     ws["ssm_b_re"], ws["ssm_b_im"])
    (gs["ssm_lambda_re"], gs["ssm_lambda_im"], gs["ssm_log_dt"], gs["ssm_b_re"], gs["ssm_b_im"]) = vjp(
        (d_abr, d_abi, _block_diag_in_t(dmr), _block_diag_in_t(dmi)))
    gs["ssm_c_re"] = _block_diag_out_t(dnr)
    gs["ssm_c_im"] = _block_diag_out_t(dni)
    gs["ssm_d"] = ddsk
    gs["ssm_glu_b"] = dgb
    gs["gmlp_w_s"] = dws
    gs["gmlp_b_s"] = dbias.reshape(CHUNK, GMLP_HEADS, GMLP_HEAD_DIM).sum(axis=-1).T
    gs["loss_rows"] = loss_rows

    def small_gather(names):
        return (_gather_small([gs[k].reshape(small_shape[k]) for k in names]), gathered, names) if dist else None

    late = ("ln1_g", "ln1_b")
    launch(exchange(scat=("mix_w_in",), extra=small_gather(tuple(k for k in SMALL + ("loss_rows",) if k not in late))))
    (dx0, dh1, df1, gs["ln1_g"], gs["ln1_b"]), _ = ordered(
        _ffn_bwd, dx1, xh1, rstd1, h1, wb["ffn1_w_in"], wb["ffn1_w_out"], row(ws["ln1_g"]), tm, "ffn1_bwd")
    grad_x = dx0.reshape(bsz, seq, D_MODEL)
    if not dist:
        gb["ffn1_w_out"], _ = _tn_matmul(a1, df1, "dw_ffn1_out", 1408, 1024)
        gb["ffn1_w_in"], _ = _tn_matmul(x0b, dh1, "dw_ffn1_in", 1024, 1408, a_t=True)
        return (loss_rows, grad_x, gb, {k: gs[k].reshape(small_shape[k]) for k in SMALL}, sums, other, gathered,
                None, {})
    launch(exchange(extra=small_gather(late)))
    gb["ffn1_w_out"], _ = ordered(_tn_matmul, a1, df1, "dw_ffn1_out", 1408, 1024)
    last = ["ffn1_w_out"] + [LAST_PIECE % q for q in range(LAST_PIECES)]
    fillers = (("ffn2_w_in", "mix_w_in", "ple_w_gate"),
               ("ffn2_w_out", "mix_w_out", "up_a", "up_b", "ssm_glu_w", "ple_w_proj"))
    out = {}
    for i in range(1, len(last) + 3):
        stage = lambda d: tuple(last[i - d:i - d + 1]) if 0 <= i - d < len(last) else ()
        launch(exchange(halves=stage(1), scat2=stage(2), swap2=stage(3), swap=("mix_w_in",) if i == 2 else ()))
        if i < len(last):
            gb[last[i]], _ = ordered(_tn_matmul, x0b, dh1, "dw_" + last[i], D_MODEL // LAST_PIECES, 1408,
                                     a_cols=(i - 1, 1), a_t=True)
        elif i - len(last) < len(fillers):
            for k in fillers[i - len(last)]:
                w, m, v = opt[k]
                out[k] = _adam_big(w, sums[k], other[k], m, v, "adam_" + k, after=pre[stage(2)[0]][0])
                order[0] = out[k][1]
    return loss_rows, grad_x, gb, gs, sums, other, gathered, ids, out


def _adamw(w, g, m, v):
    m = ADAM_B1 * m + (1.0 - ADAM_B1) * g
    v = ADAM_B2 * v + (1.0 - ADAM_B2) * (g * g)
    m_hat = m / ADAM_C1
    v_hat = v / ADAM_C2
    delta = -ADAM_LR * (m_hat / (jnp.sqrt(v_hat) + ADAM_EPS) + ADAM_WD * w)
    return delta, m, v


def _pinned(after):
    return ([pl.BlockSpec(memory_space=pl.ANY)], [after]) if after is not None else ([], [])


def _sum_blocks(part, recv, shape, axis, chip, name, after=None):
    r, c = shape
    rb = r // 8

    def body(chip_ref, p_ref, r_ref, *rest):
        rest[-1][...] = (p_ref[...] + r_ref[0].astype(F32) + r_ref[1].astype(F32) + r_ref[2].astype(F32))

    if axis == 0:
        own = pl.BlockSpec((rb, c), lambda i, k: (k[0] * 8 + i, 0))
    else:
        own = pl.BlockSpec((rb, c), lambda i, k: (i, k[0]))
    pin_specs, pin_args = _pinned(after)
    grid_spec = pltpu.PrefetchScalarGridSpec(
        num_scalar_prefetch=1, grid=(8,),
        in_specs=[own, pl.BlockSpec((3, rb, c), lambda i, k: (0, i, 0))] + pin_specs,
        out_specs=pl.BlockSpec((rb, c), lambda i, k: (i, 0)))
    return pl.pallas_call(body, name=name, out_shape=SDS((r, c), F32), grid_spec=grid_spec,
                          compiler_params=_params(("parallel",)))(chip, part, recv, *pin_args)


def _presum(part, half, shape, axis, ids, name, after=None):
    r, c = shape
    rb = r // 4

    def body(ids_ref, p_ref, h_ref, *rest):
        of_ref, ob_ref = rest[-2:]
        s = p_ref[...] + h_ref[...].astype(F32)
        ob_ref[...] = s.astype(BF16)

        @pl.when(pl.program_id(1) == 0)
        def _():
            of_ref[...] = s

    if axis == 0:
        p_spec = pl.BlockSpec((rb, c), lambda i, t, ids: (ids[t] * 4 + ids[4] * 2 + i, 0))
        h_spec = pl.BlockSpec((None, rb, c), lambda i, t, ids: (ids[t], i, 0))
    else:
        p_spec = pl.BlockSpec((rb, c), lambda i, t, ids: (ids[4] * 2 + i, ids[t]))
        h_spec = pl.BlockSpec((rb, c), lambda i, t, ids: (i, ids[t]))
    pin_specs, pin_args = _pinned(after)
    grid_spec = pltpu.PrefetchScalarGridSpec(
        num_scalar_prefetch=1, grid=(2, 4), in_specs=[p_spec, h_spec] + pin_specs,
        out_specs=(pl.BlockSpec((rb, c), lambda i, t, ids: (i, 0)),
                   pl.BlockSpec((None, rb, c), lambda i, t, ids: (t, i, 0))))
    return pl.pallas_call(body, name=name, out_shape=(SDS((r // 2, c), F32), SDS((4, r // 2, c), BF16)),
                          grid_spec=grid_spec,
                          compiler_params=_params(("parallel", "arbitrary")))(ids, part, half, *pin_args)


def _sum_half(pre, recv, name, after=None):
    hr, c = pre.shape
    rb = hr // 2

    def body(p_ref, r_ref, *rest):
        rest[-1][...] = (p_ref[...] + r_ref[0].astype(F32) + r_ref[1].astype(F32) + r_ref[2].astype(F32))

    spec = pl.BlockSpec((rb, c), lambda i: (i, 0))
    pin_specs, pin_args = _pinned(after)
    return pl.pallas_call(body, name=name, grid=(2,), out_shape=SDS((hr, c), F32),
                          in_specs=[spec, pl.BlockSpec((3, rb, c), lambda i: (0, i, 0))] + pin_specs,
                          out_specs=spec, compiler_params=_params(("parallel",)))(pre, recv, *pin_args)


def _adam_halves(w, mine, oth, m, v, ids, name, piece=0, prev=None):
    r, c = w.shape
    rb = mine.shape[0] // 2

    def body(ids_ref, w_ref, a_ref, b_ref, m_ref, v_ref, *rest):
        g_ref, d_ref, nm_ref, nv_ref = rest[-4:]
        g = jnp.where(pl.program_id(0) // 2 == ids_ref[4], a_ref[...], b_ref[...])
        g_ref[...] = g
        d_ref[...], nm_ref[...], nv_ref[...] = _adamw(w_ref[...], g, m_ref[...], v_ref[...])

    whole = pl.BlockSpec((rb, c), lambda i, ids: (i + 4 * piece, 0))
    part = pl.BlockSpec((rb, c), lambda i, ids: (i % 2, 0))
    in_specs = [whole, part, part, whole, whole]
    args = [w, mine, oth, m, v]
    aliases = {}
    if prev is not None:
        in_specs += [pl.BlockSpec(memory_space=pl.ANY)] * 4
        args += list(prev)
        aliases = {6: 0, 7: 1, 8: 2, 9: 3}
    grid_spec = pltpu.PrefetchScalarGridSpec(num_scalar_prefetch=1, grid=(4,), in_specs=in_specs,
                                             out_specs=(whole,) * 4)
    return pl.pallas_call(body, name=name, out_shape=tuple(SDS((r, c), F32) for _ in range(4)),
                          grid_spec=grid_spec, input_output_aliases=aliases,
                          compiler_params=_params(("parallel",)))(ids, *args)


def _adam_big(w, ga, gb, m, v, name, piece=0, prev=None, after=None):
    r, c = w.shape
    pr = ga.shape[0]
    steps = 8 if pr == r else 2
    rb = pr // steps
    off = piece * steps

    def body(w_ref, ga_ref, gb_ref, m_ref, v_ref, *rest):
        g_ref, d_ref, nm_ref, nv_ref = rest[-4:]
        g = ga_ref[...] + gb_ref[...]
        g_ref[...] = g
        d_ref[...], nm_ref[...], nv_ref[...] = _adamw(w_ref[...], g, m_ref[...], v_ref[...])

    whole = pl.BlockSpec((rb, c), lambda i: (i + off, 0))
    part = pl.BlockSpec((rb, c), lambda i: (i, 0))
    in_specs = [whole, part, part, whole, whole]
    args = [w, ga, gb, m, v]
    aliases = {}
    if prev is not None:
        in_specs += [pl.BlockSpec(memory_space=pl.ANY)] * 4
        args += list(prev)
        aliases = {5: 0, 6: 1, 7: 2, 8: 3}
    if after is not None:
        in_specs.append(pl.BlockSpec(memory_space=pl.ANY))
        args.append(after)
    return pl.pallas_call(
        body, name=name, grid=(steps,), out_shape=tuple(SDS((r, c), F32) for _ in range(4)),
        in_specs=in_specs, out_specs=(whole,) * 4, input_output_aliases=aliases,
        compiler_params=_params(("parallel",)),
    )(*args)


def _adam_small(ws, gathered, ms, vs):
    n = len(ws)

    def body(*refs):
        w_refs, g_refs, m_refs, v_refs = refs[:n], refs[n:2 * n], refs[2 * n:3 * n], refs[3 * n:4 * n]
        outs = refs[4 * n:]
        for i in range(n):
            g = g_refs[i][0]
            for d in range(1, N_DEV):
                g = g + g_refs[i][d]
            delta, nm, nv = _adamw(w_refs[i][...], g, m_refs[i][...], v_refs[i][...])
            outs[i][...] = g
            outs[n + i][...] = delta
            outs[2 * n + i][...] = nm
            outs[3 * n + i][...] = nv

    vmem = pl.BlockSpec(memory_space=pltpu.VMEM)
    shapes = [w.shape for w in ws]
    return pl.pallas_call(
        body, name="adam_small", out_shape=tuple(SDS(s, F32) for s in shapes * 4),
        in_specs=[vmem] * (4 * n), out_specs=tuple([vmem] * (4 * n)),
        compiler_params=pltpu.CompilerParams(vmem_limit_bytes=VMEM_LIMIT_BYTES),
    )(*ws, *gathered, *ms, *vs)


def _sum_loss(gathered):
    def body(g_ref, o_ref):
        tot = g_ref[0]
        for d in range(1, N_DEV):
            tot = tot + g_ref[d]
        o_ref[...] = (0.5 / D_MODEL) * jnp.sum(tot, axis=1, keepdims=True)

    vmem = pl.BlockSpec(memory_space=pltpu.VMEM)
    return pl.pallas_call(body, name="sum_loss", out_shape=SDS((1, 1), F32), in_specs=[vmem],
                          out_specs=vmem)(gathered)


def kernel(x, p, ffn1_w_in, ffn1_w_out, ln1_g, ln1_b, mix_w_in, ssm_lambda_re, ssm_lambda_im, ssm_log_dt, ssm_b_re, ssm_b_im, ssm_c_re, ssm_c_im, ssm_d, ssm_glu_w, ssm_glu_b, gmlp_ln_g, gmlp_ln_b, gmlp_w_s, gmlp_b_s, up_a, up_b, mix_w_out, ln2_g, ln2_b, ffn2_w_in, ffn2_w_out, ln3_g, ln3_b, ple_w_proj, ple_w_gate, loss_target, m_ffn1_w_in, m_ffn1_w_out, m_ln1_g, m_ln1_b, m_mix_w_in, m_ssm_lambda_re, m_ssm_lambda_im, m_ssm_log_dt, m_ssm_b_re, m_ssm_b_im, m_ssm_c_re, m_ssm_c_im, m_ssm_d, m_ssm_glu_w, m_ssm_glu_b, m_gmlp_ln_g, m_gmlp_ln_b, m_gmlp_w_s, m_gmlp_b_s, m_up_a, m_up_b, m_mix_w_out, m_ln2_g, m_ln2_b, m_ffn2_w_in, m_ffn2_w_out, m_ln3_g, m_ln3_b, m_ple_w_proj, m_ple_w_gate, v_ffn1_w_in, v_ffn1_w_out, v_ln1_g, v_ln1_b, v_mix_w_in, v_ssm_lambda_re, v_ssm_lambda_im, v_ssm_log_dt, v_ssm_b_re, v_ssm_b_im, v_ssm_c_re, v_ssm_c_im, v_ssm_d, v_ssm_glu_w, v_ssm_glu_b, v_gmlp_ln_g, v_gmlp_ln_b, v_gmlp_w_s, v_gmlp_b_s, v_up_a, v_up_b, v_mix_w_out, v_ln2_g, v_ln2_b, v_ffn2_w_in, v_ffn2_w_out, v_ln3_g, v_ln3_b, v_ple_w_proj, v_ple_w_gate):
    given = dict(locals())
    order = ("ffn1_w_in", "ffn1_w_out", "ln1_g", "ln1_b", "mix_w_in", "ssm_lambda_re", "ssm_lambda_im",
             "ssm_log_dt", "ssm_b_re", "ssm_b_im", "ssm_c_re", "ssm_c_im", "ssm_d", "ssm_glu_w", "ssm_glu_b",
             "gmlp_ln_g", "gmlp_ln_b", "gmlp_w_s", "gmlp_b_s", "up_a", "up_b", "mix_w_out", "ln2_g", "ln2_b",
             "ffn2_w_in", "ffn2_w_out", "ln3_g", "ln3_b", "ple_w_proj", "ple_w_gate")
    assert set(order) == set(BIG + SMALL)

    shard = {k: given[k][0] for k in BIG}
    shard_b = {k: shard[k].astype(BF16) for k in BIG}
    opt = {k: (shard[k], given["m_" + k][0], given["v_" + k][0]) for k in BIG}
    loss_rows, grad_x, gb, gs, sums, other, gathered, ids, out = _local_step(
        x, given["p"][0], loss_target, {}, {k: given[k] for k in SMALL}, shard_b, opt)

    out = dict(out)
    for k in BIG:
        if k in out:
            continue
        moments = (given["m_" + k][0], given["v_" + k][0])
        if k == "ffn1_w_out":
            out[k] = _adam_halves(shard[k], sums[k], other[k], *moments, ids, "adam_" + k)
        elif k == "ffn1_w_in":
            for q in range(LAST_PIECES):
                kq = LAST_PIECE % q
                out[k] = _adam_halves(shard[k], sums[kq], other[kq], *moments, ids, "adam_" + kq, q, out.get(k))
        else:
            out[k] = _adam_big(shard[k], sums[k], other[k], *moments, "adam_" + k,
                               after=gb[LAST_PIECE % (LAST_PIECES - 1)][0])

    res = _adam_small([_small_view(k, given[k]) for k in SMALL], [gathered[k] for k in SMALL],
                      [_small_view(k, given["m_" + k]) for k in SMALL],
                      [_small_view(k, given["v_" + k]) for k in SMALL])
    ns = len(SMALL)
    for i, k in enumerate(SMALL):
        out[k] = tuple(res[j * ns + i].reshape(given[k].shape) for j in range(4))
    loss = _sum_loss(gathered["loss_rows"]).reshape(())

    lead = lambda k, j: out[k][j][None] if k in BIG else out[k][j]
    return (loss, grad_x, *[lead(k, 0) for k in order], *[lead(k, 1) for k in order],
            *[lead(k, 2) for k in order], *[lead(k, 3) for k in order])
```

```python
import math

import jax
import jax.numpy as jnp
from jax import lax
from jax.experimental import pallas as pl
from jax.experimental.pallas import tpu as pltpu
from jax.experimental.pallas import tpu_sc as plsc

F32 = jnp.float32
BF16 = jnp.bfloat16
MESH = pl.DeviceIdType.MESH
SDS = jax.ShapeDtypeStruct

D_MODEL = 1024
D_FF = 2816
D_SSM = 512
D_GMLP = 512
SSM_GROUPS = 32
SSM_GROUP_CH = 16
SSM_STATE = 64
SSM_LANES = SSM_GROUPS * SSM_STATE
GMLP_HEADS = 8
GMLP_HEAD_DIM = 64
CHUNK = 128
PLE_DIM = 256
LN_EPS = 1e-5
ALPHA = 2.0 ** 0.25

ADAM_LR = 0.001
ADAM_B1 = 0.9
ADAM_B2 = 0.999
ADAM_EPS = 1e-08
ADAM_WD = 0.01
ADAM_STEP = 10
ADAM_C1 = 1.0 - ADAM_B1 ** ADAM_STEP
ADAM_C2 = 1.0 - ADAM_B2 ** ADAM_STEP

N_DEV = 8
VMEM_LIMIT_BYTES = 56 * 1024 * 1024
FFN_COLS = 1408
S5_BLOCKS = 4
S5_BLOCK_IN = D_SSM // S5_BLOCKS
S5_BLOCK_ST = SSM_LANES // S5_BLOCKS
SCAN_LANES = 512
TN_K_BLOCK = 2048
DIRECT_GATHER_BYTES = 0
LAST_PIECES = 2
LAST_PIECE = "ffn1_w_in_q%d"
_G0 = math.sqrt(2.0 / math.pi)
_G1 = 0.044715


def _dot(a, b):
    return jnp.dot(a, b, preferred_element_type=F32)


def _dot_nt(a, b):
    return lax.dot_general(a, b, (((1,), (1,)), ((), ())), preferred_element_type=F32)


def _dot_tn(a, b):
    return lax.dot_general(a, b, (((0,), (0,)), ((), ())), preferred_element_type=F32)


def _sigmoid(x):
    return 1.0 / (1.0 + jnp.exp(-x))


def _gelu(x):
    t = jnp.tanh(_G0 * (x + _G1 * x * x * x))
    return 0.5 * x * (1.0 + t)


def _gelu_grad(x):
    t = jnp.tanh(_G0 * (x + _G1 * x * x * x))
    return 0.5 * (1.0 + t) + 0.5 * x * (1.0 - t * t) * _G0 * (1.0 + 3.0 * _G1 * x * x)


def _ln_fwd(r, g, b):
    mu = jnp.mean(r, axis=-1, keepdims=True)
    d = r - mu
    var = jnp.mean(d * d, axis=-1, keepdims=True)
    rstd = lax.rsqrt(var + LN_EPS)
    xh = d * rstd
    return xh * g + b, xh, rstd


def _ln_bwd(dy, xh, rstd, g):
    dxh = dy * g
    m1 = jnp.mean(dxh, axis=-1, keepdims=True)
    m2 = jnp.mean(dxh * xh, axis=-1, keepdims=True)
    return rstd * (dxh - m1 - xh * m2)


def _resident(shape):
    nd = len(shape)
    return pl.BlockSpec(shape, lambda *_: (0,) * nd, pipeline_mode=pl.Buffered(1))


def _fixed(shape):
    nd = len(shape)
    return pl.BlockSpec(shape, lambda *_: (0,) * nd)


def _rows(tm, cols):
    return pl.BlockSpec((tm, cols), lambda i: (i, 0))


def _cols(rows, tm):
    return pl.BlockSpec((rows, tm), lambda i: (0, i))


def _params(sem):
    return pltpu.CompilerParams(dimension_semantics=sem, vmem_limit_bytes=VMEM_LIMIT_BYTES)


class _Exchange:
    def __init__(self, args, out_shape, sems, start, finish):
        self.args, self.out_shape, self.sems = list(args), list(out_shape), list(sems)
        self.start, self.finish = start, finish
        self.cuts = [(0, len(self.out_shape))]


def _call(body, name, grid, in_specs, out_specs, out_shape, args, scratch=(), sem=None, bg=None, aliases=None):
    aliases = {} if aliases is None else aliases
    if bg is not None and not isinstance(bg, _Exchange):
        n_args = len(args)

        def pinned(*refs):
            body(*refs[:n_args], *refs[n_args + 1:])

        res = pl.pallas_call(pinned, name=name, grid=grid, out_shape=tuple(out_shape),
                             in_specs=list(in_specs) + [pl.BlockSpec(memory_space=pl.ANY)],
                             out_specs=tuple(out_specs), scratch_shapes=list(scratch),
                             input_output_aliases=aliases, compiler_params=_params(sem))(*args, bg)
        return tuple(res), ()
    if bg is None:
        res = pl.pallas_call(body, name=name, grid=grid, out_shape=tuple(out_shape), in_specs=list(in_specs),
                             out_specs=tuple(out_specs), scratch_shapes=list(scratch),
                             input_output_aliases=aliases, compiler_params=_params(sem))(*args)
        return tuple(res), ()
    n_in, n_out, n_bi, n_bo, n_sc = len(args), len(out_shape), len(bg.args), len(bg.out_shape), len(scratch)

    def wrapped(*refs):
        ins = refs[:n_in]
        b_ins = refs[n_in:n_in + n_bi]
        outs = refs[n_in + n_bi:n_in + n_bi + n_out]
        b_outs = refs[n_in + n_bi + n_out:n_in + n_bi + n_out + n_bo]
        rest = refs[n_in + n_bi + n_out + n_bo:]
        scr, b_sems = rest[:n_sc], rest[n_sc:]
        first = pl.program_id(0) == 0
        last = pl.program_id(0) == grid[0] - 1
        for ax in range(1, len(grid)):
            first = jnp.logical_and(first, pl.program_id(ax) == 0)
            last = jnp.logical_and(last, pl.program_id(ax) == grid[ax] - 1)

        @pl.when(first)
        def _():
            bg.start(b_ins, b_outs, b_sems)

        body(*ins, *outs, *scr)

        @pl.when(last)
        def _():
            bg.finish(b_ins, b_outs, b_sems)

    any_spec = pl.BlockSpec(memory_space=pl.ANY)
    res = pl.pallas_call(
        wrapped, name=name, grid=grid, out_shape=tuple(out_shape) + tuple(bg.out_shape),
        in_specs=list(in_specs) + [any_spec] * n_bi, out_specs=tuple(out_specs) + (any_spec,) * n_bo,
        scratch_shapes=list(scratch) + list(bg.sems), input_output_aliases=aliases,
        compiler_params=_params(tuple("arbitrary" for _ in grid)))(*args, *bg.args)
    return tuple(res[:n_out]), tuple(res[n_out:])


def _run_exchange(ex, name):
    n_i, n_o = len(ex.args), len(ex.out_shape)

    def body(*refs):
        ins, outs, sems = refs[:n_i], refs[n_i:n_i + n_o], refs[n_i + n_o:]
        ex.start(ins, outs, sems)
        ex.finish(ins, outs, sems)

    any_spec = pl.BlockSpec(memory_space=pl.ANY)
    return tuple(pl.pallas_call(body, name=name, out_shape=tuple(ex.out_shape), in_specs=[any_spec] * n_i,
                                out_specs=(any_spec,) * n_o, scratch_shapes=list(ex.sems))(*ex.args))


def _run_exchange_on_sequencer(ex, name, collective_id):
    n_i, n_o = len(ex.args), len(ex.out_shape)

    def body(*refs):
        ins, outs, sems = refs[:n_i], refs[n_i:n_i + n_o], refs[n_i + n_o:]
        x, y, c = lax.axis_index("x"), lax.axis_index("y"), lax.axis_index("c")
        barrier = pltpu.get_barrier_semaphore()
        for peer in [(x, y, 1 - c), (1 - x, y, c), (x, 1 - y, c), (1 - x, 1 - y, c)]:
            pl.semaphore_signal(barrier, inc=1, device_id=peer, device_id_type=MESH)
        pl.semaphore_wait(barrier, 4)
        ex.start(ins, outs, sems)
        ex.finish(ins, outs, sems)

    return tuple(pl.kernel(body, out_type=tuple(ex.out_shape),
                           mesh=plsc.ScalarSubcoreMesh(axis_name="sequencer", num_cores=1),
                           scratch_types=list(ex.sems), name=name,
                           compiler_params=pltpu.CompilerParams(collective_id=collective_id))(*ex.args))


def _join(exchanges):
    cuts = []
    a = o = q = 0
    for e in exchanges:
        cuts.append((a, a + len(e.args), o, o + len(e.out_shape), q, q + len(e.sems)))
        a, o, q = cuts[-1][1], cuts[-1][3], cuts[-1][5]

    def start(ins, outs, sems):
        for e, (a0, a1, o0, o1, q0, q1) in zip(exchanges, cuts):
            e.start(ins[a0:a1], outs[o0:o1], sems[q0:q1])

    def finish(ins, outs, sems):
        for e, (a0, a1, o0, o1, q0, q1) in zip(exchanges, cuts):
            e.finish(ins[a0:a1], outs[o0:o1], sems[q0:q1])

    joined = _Exchange(sum((e.args for e in exchanges), []), sum((e.out_shape for e in exchanges), []),
                       sum((e.sems for e in exchanges), []), start, finish)
    joined.cuts = [(c[2], c[3]) for c in cuts]
    return joined


def _ffn_proj(x, w_in, tm, name, bg=None):
    t = x.shape[0]
    nch = D_FF // FFN_COLS

    def body(x_ref, win_ref, xbt_ref, h_ref, a_ref):
        xb = x_ref[...].astype(BF16)
        xbt_ref[...] = xb.T
        for k in range(nch):
            cg = slice(k * FFN_COLS, (k + 1) * FFN_COLS)
            cu = slice(D_FF + k * FFN_COLS, D_FF + (k + 1) * FFN_COLS)
            hg = _dot(xb, win_ref[k])
            hu = _dot(xb, win_ref[nch + k])
            h_ref[:, cg] = hg.astype(BF16)
            h_ref[:, cu] = hu.astype(BF16)
            a_ref[:, cg] = (hg * _sigmoid(hg) * hu).astype(BF16)

    return _call(
        body, name, (t // tm,),
        [_rows(tm, D_MODEL), _resident((2 * nch, D_MODEL, FFN_COLS))],
        (_cols(D_MODEL, tm), _rows(tm, 2 * D_FF), _rows(tm, D_FF)),
        (SDS((D_MODEL, t), BF16), SDS((t, 2 * D_FF), BF16), SDS((t, D_FF), BF16)),
        (x, w_in), sem=("parallel",), bg=bg)


def _ffn_out(x, a, w_out, g, b, tm, name, bg=None):
    t = x.shape[0]

    def body(x_ref, a_ref, wout_ref, g_ref, b_ref, xn_ref, xh_ref, rstd_ref):
        f = _dot(a_ref[...], wout_ref[...])
        y, xh, rstd = _ln_fwd(ALPHA * x_ref[...] + 0.5 * f, g_ref[...], b_ref[...])
        xn_ref[...] = y
        xh_ref[...] = xh
        rstd_ref[...] = rstd

    return _call(
        body, name, (t // tm,),
        [_rows(tm, D_MODEL), _rows(tm, D_FF), _resident((D_FF, D_MODEL)), _fixed((1, D_MODEL)), _fixed((1, D_MODEL))],
        (_rows(tm, D_MODEL), _rows(tm, D_MODEL), _rows(tm, 1)),
        (SDS((t, D_MODEL), F32), SDS((t, D_MODEL), F32), SDS((t, 1), F32)),
        (x, a, w_out, g, b), sem=("parallel",), bg=bg)


def _ffn_bwd(dxn, xh, rstd, h, w_in, w_out, g, tm, name, bg=None):
    t = dxn.shape[0]
    nch = D_FF // FFN_COLS

    def body(dxn_ref, xh_ref, rstd_ref, h_ref, win_ref, wout_ref, g_ref,
             dx_ref, dh_ref, df_ref, dg_ref, db_ref):
        @pl.when(pl.program_id(0) == 0)
        def _():
            dg_ref[...] = jnp.zeros_like(dg_ref)
            db_ref[...] = jnp.zeros_like(db_ref)

        dy = dxn_ref[...]
        xhv = xh_ref[...]
        dr = _ln_bwd(dy, xhv, rstd_ref[...], g_ref[...])
        dg_ref[...] += jnp.sum(dy * xhv, axis=0, keepdims=True)
        db_ref[...] += jnp.sum(dy, axis=0, keepdims=True)
        df = (0.5 * dr).astype(BF16)
        df_ref[...] = df
        dx = ALPHA * dr
        for k in range(nch):
            cg = slice(k * FFN_COLS, (k + 1) * FFN_COLS)
            cu = slice(D_FF + k * FFN_COLS, D_FF + (k + 1) * FFN_COLS)
            hg = h_ref[:, cg].astype(F32)
            hu = h_ref[:, cu].astype(F32)
            sg = _sigmoid(hg)
            silu = hg * sg
            da = _dot_nt(df, wout_ref[cg, :])
            dhu = (da * silu).astype(BF16)
            dhg = (da * hu * (sg * (1.0 + hg * (1.0 - sg)))).astype(BF16)
            dh_ref[:, cg] = dhg
            dh_ref[:, cu] = dhu
            dx = dx + _dot_nt(dhg, win_ref[k]) + _dot_nt(dhu, win_ref[nch + k])
        dx_ref[...] = dx

    return _call(
        body, name, (t // tm,),
        [_rows(tm, D_MODEL), _rows(tm, D_MODEL), _rows(tm, 1), _rows(tm, 2 * D_FF),
         _resident((2 * nch, D_MODEL, FFN_COLS)), _resident((D_FF, D_MODEL)), _fixed((1, D_MODEL))],
        (_rows(tm, D_MODEL), _rows(tm, 2 * D_FF), _rows(tm, D_MODEL),
         _fixed((1, D_MODEL)), _fixed((1, D_MODEL))),
        (SDS((t, D_MODEL), F32), SDS((t, 2 * D_FF), BF16), SDS((t, D_MODEL), BF16),
         SDS((1, D_MODEL), F32), SDS((1, D_MODEL), F32)),
        (dxn, xh, rstd, h, w_in, w_out, g), sem=("arbitrary",), bg=bg)


def _tn_matmul(a, b, name, bm, bn, col_block=0, total_cols=None, prev=None, bg=None, a_cols=None, a_t=False):
    t, m = a.shape[::-1] if a_t else a.shape
    a_first = 0
    if a_cols is not None:
        a_first, m = a_cols[0], a_cols[1] * bm
    n = b.shape[1]
    total_cols = n if total_cols is None else total_cols
    bk = min(TN_K_BLOCK, t)
    nk = t // bk
    n_in = 2 if prev is None else 4

    def body(*refs):
        a_ref, b_ref = refs[0], refs[1]
        o_ref, ob_ref = refs[n_in], refs[n_in + 1]
        k = pl.program_id(2)

        @pl.when(k == 0)
        def _():
            o_ref[...] = jnp.zeros_like(o_ref)

        o_ref[...] += _dot(a_ref[...], b_ref[...]) if a_t else _dot_tn(a_ref[...], b_ref[...])

        @pl.when(k == nk - 1)
        def _():
            ob_ref[...] = o_ref[...].astype(BF16)

    a_spec = (pl.BlockSpec((bm, bk), lambda i, j, k: (i + a_first, k)) if a_t
              else pl.BlockSpec((bk, bm), lambda i, j, k: (k, i + a_first)))
    in_specs = [a_spec, pl.BlockSpec((bk, bn), lambda i, j, k: (k, j))]
    args = [a, b]
    aliases = {}
    if prev is not None:
        in_specs += [pl.BlockSpec(memory_space=pl.ANY), pl.BlockSpec(memory_space=pl.ANY)]
        args += list(prev)
        aliases = {2: 0, 3: 1}
    out_spec = pl.BlockSpec((bm, bn), lambda i, j, k: (i, j + col_block))
    return _call(body, name, (m // bm, n // bn, nk), in_specs, (out_spec, out_spec),
                 (SDS((m, total_cols), F32), SDS((m, total_cols), BF16)), args,
                 sem=("parallel", "parallel", "arbitrary"), bg=bg, aliases=aliases)


def _mixin_fwd(x1, w, tm, bg=None):
    t = x1.shape[0]

    def body(x_ref, w_ref, xbt_ref, za_ref, zuv_ref, gab_ref):
        xb = x_ref[...].astype(BF16)
        xbt_ref[...] = xb.T
        za_ref[...] = _dot(xb, w_ref[:, 0:512]).astype(BF16)
        zuv_ref[...] = _dot(xb, w_ref[:, 512:1536]).astype(BF16)
        gab_ref[...] = _dot(xb, w_ref[:, 1536:3584]).astype(BF16)

    return _call(
        body, "mixin_fwd", (t // tm,),
        [_rows(tm, D_MODEL), _resident((D_MODEL, 3584))],
        (_cols(D_MODEL, tm), _rows(tm, 512), _rows(tm, 1024), _rows(tm, 2048)),
        (SDS((D_MODEL, t), BF16), SDS((t, 512), BF16), SDS((t, 1024), BF16), SDS((t, 2048), BF16)),
        (x1, w), sem=("parallel",), bg=bg)


def _mixin_bwd(dx1a, dza, dzuv, dgab, w, tm, bg=None):
    t = dx1a.shape[0]

    def body(d_ref, dza_ref, dzuv_ref, dgab_ref, w_ref, dx_ref):
        dx_ref[...] = (d_ref[...] + _dot_nt(dza_ref[...], w_ref[:, 0:512])
                       + _dot_nt(dzuv_ref[...], w_ref[:, 512:1536])
                       + _dot_nt(dgab_ref[...], w_ref[:, 1536:3584]))

    return _call(
        body, "mixin_bwd", (t // tm,),
        [_rows(tm, D_MODEL), _rows(tm, 512), _rows(tm, 1024), _rows(tm, 2048), _resident((D_MODEL, 3584))],
        (_rows(tm, D_MODEL),), (SDS((t, D_MODEL), F32),),
        (dx1a, dza, dzuv, dgab, w), sem=("parallel",), bg=bg)


def _unrolled(lo, hi, body, carry):
    for j in range(lo, hi):
        carry = body(j, carry)
    return carry


def _scan_fwd(hr_ref, hi_ref, a_ref, ap_ref, carry_ref, seg, cin_ref):
    for lc in range(SSM_LANES // SCAN_LANES):
        ls = slice(lc * SCAN_LANES, (lc + 1) * SCAN_LANES)
        a_r = jnp.broadcast_to(a_ref[0:1, ls], (8, SCAN_LANES))
        a_i = jnp.broadcast_to(a_ref[1:2, ls], (8, SCAN_LANES))

        def step(j, hc, ls=ls, a_r=a_r, a_i=a_i):
            h_r, h_i = hc
            rows = pl.ds(j * 8, 8)
            n_r = a_r * h_r - a_i * h_i + hr_ref[rows, ls]
            n_i = a_r * h_i + a_i * h_r + hi_ref[rows, ls]
            hr_ref[rows, ls] = n_r
            hi_ref[rows, ls] = n_i
            return n_r, n_i

        zero = jnp.zeros((8, SCAN_LANES), F32)
        f_r, f_i = _unrolled(0, seg, step, (zero, zero))
        c_r = carry_ref[0:1, ls]
        c_i = carry_ref[1:2, ls]
        p_r = ap_ref[0:1, ls]
        p_i = ap_ref[1:2, ls]
        rows_r, rows_i = [], []
        for s in range(8):
            rows_r.append(c_r)
            rows_i.append(c_i)
            c_r, c_i = (f_r[s:s + 1] + p_r * c_r - p_i * c_i,
                        f_i[s:s + 1] + p_r * c_i + p_i * c_r)
        carry_ref[0:1, ls] = c_r
        carry_ref[1:2, ls] = c_i
        cin_r = jnp.concatenate(rows_r, axis=0)
        cin_i = jnp.concatenate(rows_i, axis=0)
        if cin_ref is not None:
            cin_ref[0, :, ls] = cin_r
            cin_ref[1, :, ls] = cin_i

        def fix(j, cc, ls=ls, a_r=a_r, a_i=a_i):
            c_r, c_i = cc
            c_r, c_i = a_r * c_r - a_i * c_i, a_r * c_i + a_i * c_r
            rows = pl.ds(j * 8, 8)
            hr_ref[rows, ls] = hr_ref[rows, ls] + c_r
            hi_ref[rows, ls] = hi_ref[rows, ls] + c_i
            return c_r, c_i

        _unrolled(0, seg, fix, (cin_r, cin_i))


def _scan_bwd(gr_ref, gi_ref, hr_ref, hi_ref, cin_ref, a_ref, ap_ref, rcarry_ref, da_ref, seg):
    for lc in range(SSM_LANES // SCAN_LANES):
        ls = slice(lc * SCAN_LANES, (lc + 1) * SCAN_LANES)
        a_r = jnp.broadcast_to(a_ref[0:1, ls], (8, SCAN_LANES))
        a_i = jnp.broadcast_to(a_ref[1:2, ls], (8, SCAN_LANES))

        def step(t, gc, ls=ls, a_r=a_r, a_i=a_i):
            g_r, g_i = gc
            rows = pl.ds((seg - 1 - t) * 8, 8)
            n_r = gr_ref[rows, ls] + a_r * g_r + a_i * g_i
            n_i = gi_ref[rows, ls] + a_r * g_i - a_i * g_r
            gr_ref[rows, ls] = n_r
            gi_ref[rows, ls] = n_i
            return n_r, n_i

        zero = jnp.zeros((8, SCAN_LANES), F32)
        f_r, f_i = _unrolled(0, seg, step, (zero, zero))
        c_r = rcarry_ref[0:1, ls]
        c_i = rcarry_ref[1:2, ls]
        p_r = ap_ref[0:1, ls]
        p_i = ap_ref[1:2, ls]
        rows_r, rows_i = [None] * 8, [None] * 8
        for s in range(7, -1, -1):
            rows_r[s] = c_r
            rows_i[s] = c_i
            c_r, c_i = (f_r[s:s + 1] + p_r * c_r + p_i * c_i,
                        f_i[s:s + 1] + p_r * c_i - p_i * c_r)
        rcarry_ref[0:1, ls] = c_r
        rcarry_ref[1:2, ls] = c_i
        cin_r = jnp.concatenate(rows_r, axis=0)
        cin_i = jnp.concatenate(rows_i, axis=0)

        def fix_row(j_rows, hp_r, hp_i, cc, ls=ls, a_r=a_r, a_i=a_i):
            c_r, c_i, acc_r, acc_i = cc
            c_r, c_i = a_r * c_r + a_i * c_i, a_r * c_i - a_i * c_r
            g_r = gr_ref[j_rows, ls] + c_r
            g_i = gi_ref[j_rows, ls] + c_i
            gr_ref[j_rows, ls] = g_r
            gi_ref[j_rows, ls] = g_i
            acc_r = acc_r + g_r * hp_r + g_i * hp_i
            acc_i = acc_i + g_i * hp_r - g_r * hp_i
            return c_r, c_i, acc_r, acc_i

        def fix(t, cc, ls=ls, fix_row=fix_row):
            j = seg - 1 - t
            rows = pl.ds(j * 8, 8)
            prev = pl.ds((j - 1) * 8, 8)
            return fix_row(rows, hr_ref[prev, ls], hi_ref[prev, ls], cc)

        cc = _unrolled(0, seg - 1, fix, (cin_r, cin_i, zero, zero))
        _, _, acc_r, acc_i = fix_row(pl.ds(0, 8), cin_ref[0, :, ls], cin_ref[1, :, ls], cc)
        da_ref[0, :, ls] += acc_r
        da_ref[1, :, ls] += acc_i


def _s5_fwd(za, sp, bsz, seq, tb, bg=None):
    nb = seq // tb
    seg = tb // 8
    t = bsz * seq

    def body(za_ref, perm_ref, permt_ref, mre_ref, mim_ref, nre_ref, nim_ref, a_ref, ap_ref,
             dsk_ref, gw_ref, gb_ref, out_ref, outt_ref, y2_ref, car_ref, hr_ref, hi_ref, carry_ref):
        @pl.when(pl.program_id(1) == 0)
        def _():
            carry_ref[...] = jnp.zeros_like(carry_ref)

        car_ref[0] = carry_ref[...]
        up = _dot(perm_ref[...], za_ref[...])
        upb = up.astype(BF16)
        for bb in range(S5_BLOCKS):
            ub = upb[:, bb * S5_BLOCK_IN:(bb + 1) * S5_BLOCK_IN]
            st = slice(bb * S5_BLOCK_ST, (bb + 1) * S5_BLOCK_ST)
            hr_ref[:, st] = _dot(ub, mre_ref[bb])
            hi_ref[:, st] = _dot(ub, mim_ref[bb])
        _scan_fwd(hr_ref, hi_ref, a_ref, ap_ref, carry_ref, seg, None)
        ys = []
        for bb in range(S5_BLOCKS):
            st = slice(bb * S5_BLOCK_ST, (bb + 1) * S5_BLOCK_ST)
            ys.append(_dot(hr_ref[:, st].astype(BF16), nre_ref[bb])
                      - _dot(hi_ref[:, st].astype(BF16), nim_ref[bb]))
        y2 = jnp.concatenate(ys, axis=1) + dsk_ref[...] * up
        y2_ref[...] = y2
        y3 = _gelu(y2)
        gl = _dot(y3.astype(BF16), gw_ref[...]) + gb_ref[...]
        oa = y3 * _sigmoid(gl)
        out = _dot(permt_ref[...], oa.astype(BF16)).astype(BF16)
        out_ref[...] = out
        outt_ref[...] = out.T

    blk = pl.BlockSpec((tb, D_SSM), lambda b, j: (b * nb + j, 0))
    blk_t = pl.BlockSpec((D_SSM, tb), lambda b, j: (0, b * nb + j))
    m_shape = (S5_BLOCKS, S5_BLOCK_IN, S5_BLOCK_ST)
    n_shape = (S5_BLOCKS, S5_BLOCK_ST, S5_BLOCK_IN)
    return _call(
        body, "s5_fwd", (bsz, nb),
        [blk, _fixed((tb, tb)), _fixed((tb, tb)), _fixed(m_shape), _fixed(m_shape), _fixed(n_shape),
         _fixed(n_shape), _fixed((2, SSM_LANES)), _fixed((2, SSM_LANES)), _fixed((1, D_SSM)),
         _fixed((D_SSM, D_SSM)), _fixed((1, D_SSM))],
        (blk, blk_t, blk, pl.BlockSpec((1, 2, SSM_LANES), lambda b, j: (b * nb + j, 0, 0))),
        (SDS((t, D_SSM), BF16), SDS((D_SSM, t), BF16), SDS((t, D_SSM), F32), SDS((bsz * nb, 2, SSM_LANES), F32)),
        (za, sp["perm"], sp["permt"], sp["mre"], sp["mim"], sp["nre"], sp["nim"], sp["a"], sp["ap"],
         sp["dskip"], sp["glu_w"], sp["glu_b"]),
        scratch=[pltpu.VMEM((tb, SSM_LANES), F32), pltpu.VMEM((tb, SSM_LANES), F32),
                 pltpu.VMEM((2, SSM_LANES), F32)],
        sem=("arbitrary", "arbitrary"), bg=bg)


def _s5_bwd(za, y2p, doa, carries, sp, bsz, seq, tb, bg=None):
    nb = seq // tb
    seg = tb // 8
    t = bsz * seq

    def body(za_ref, y2_ref, doa_ref, car_ref, perm_ref, permt_ref, mre_ref, mim_ref, mtre_ref, mtim_ref,
             nre_ref, nim_ref, ntre_ref, ntim_ref, a_ref, ap_ref, dsk_ref, gw_ref, gwt_ref, gb_ref,
             dza_ref, dmr_ref, dmi_ref, dnr_ref, dni_ref, da_ref, ddsk_ref, dgw_ref, dgb_ref,
             hr_ref, hi_ref, gr_ref, gi_ref, cin_ref, carry_ref, rcarry_ref):
        first = jnp.logical_and(pl.program_id(0) == 0, pl.program_id(1) == 0)

        @pl.when(first)
        def _():
            for r in (dmr_ref, dmi_ref, dnr_ref, dni_ref, da_ref, ddsk_ref, dgw_ref, dgb_ref):
                r[...] = jnp.zeros_like(r)

        @pl.when(pl.program_id(1) == 0)
        def _():
            rcarry_ref[...] = jnp.zeros_like(rcarry_ref)

        carry_ref[...] = car_ref[0]
        perm = perm_ref[...]
        up = _dot(perm, za_ref[...])
        upb = up.astype(BF16)
        for bb in range(S5_BLOCKS):
            ub = upb[:, bb * S5_BLOCK_IN:(bb + 1) * S5_BLOCK_IN]
            st = slice(bb * S5_BLOCK_ST, (bb + 1) * S5_BLOCK_ST)
            hr_ref[:, st] = _dot(ub, mre_ref[bb])
            hi_ref[:, st] = _dot(ub, mim_ref[bb])
        _scan_fwd(hr_ref, hi_ref, a_ref, ap_ref, carry_ref, seg, cin_ref)

        y2 = y2_ref[...]
        y3 = _gelu(y2)
        y3b = y3.astype(BF16)
        sg = _sigmoid(_dot(y3b, gw_ref[...]) + gb_ref[...])
        d0 = doa_ref[...]
        d_hi = d0.astype(BF16)
        d1 = d0 - d_hi.astype(F32)
        d_mid = d1.astype(BF16)
        d_lo = (d1 - d_mid.astype(F32)).astype(BF16)
        doap = _dot(perm, d_hi) + _dot(perm, d_mid) + _dot(perm, d_lo)
        dgl = doap * y3 * sg * (1.0 - sg)
        dglb = dgl.astype(BF16)
        dy3 = doap * sg + _dot(dglb, gwt_ref[...])
        dgw_ref[...] += _dot_tn(y3b, dglb)
        dgb_ref[...] += jnp.sum(dgl, axis=0, keepdims=True)
        dy2 = dy3 * _gelu_grad(y2)
        ddsk_ref[...] += jnp.sum(dy2 * up, axis=0, keepdims=True)
        dyb = dy2.astype(BF16)
        for bb in range(S5_BLOCKS):
            dyc = dyb[:, bb * S5_BLOCK_IN:(bb + 1) * S5_BLOCK_IN]
            st = slice(bb * S5_BLOCK_ST, (bb + 1) * S5_BLOCK_ST)
            gr_ref[:, st] = _dot(dyc, ntre_ref[bb])
            gi_ref[:, st] = -_dot(dyc, ntim_ref[bb])
            dnr_ref[bb] += _dot_tn(hr_ref[:, st].astype(BF16), dyc)
            dni_ref[bb] += -_dot_tn(hi_ref[:, st].astype(BF16), dyc)
        _scan_bwd(gr_ref, gi_ref, hr_ref, hi_ref, cin_ref, a_ref, ap_ref, rcarry_ref, da_ref, seg)
        dus = []
        for bb in range(S5_BLOCKS):
            st = slice(bb * S5_BLOCK_ST, (bb + 1) * S5_BLOCK_ST)
            grb = gr_ref[:, st].astype(BF16)
            gib = gi_ref[:, st].astype(BF16)
            dus.append(_dot(grb, mtre_ref[bb]) + _dot(gib, mtim_ref[bb]))
            ub = upb[:, bb * S5_BLOCK_IN:(bb + 1) * S5_BLOCK_IN]
            dmr_ref[bb] += _dot_tn(ub, grb)
            dmi_ref[bb] += _dot_tn(ub, gib)
        du = jnp.concatenate(dus, axis=1) + dy2 * dsk_ref[...]
        dza_ref[...] = _dot(permt_ref[...], du.astype(BF16)).astype(BF16)

    def rev(b, j):
        return (b * nb + (nb - 1 - j), 0)

    blk = pl.BlockSpec((tb, D_SSM), rev)
    m_shape = (S5_BLOCKS, S5_BLOCK_IN, S5_BLOCK_ST)
    n_shape = (S5_BLOCKS, S5_BLOCK_ST, S5_BLOCK_IN)
    return _call(
        body, "s5_bwd", (bsz, nb),
        [blk, blk, blk, pl.BlockSpec((1, 2, SSM_LANES), lambda b, j: (b * nb + (nb - 1 - j), 0, 0)),
         _fixed((tb, tb)), _fixed((tb, tb)), _fixed(m_shape), _fixed(m_shape), _fixed(n_shape), _fixed(n_shape),
         _fixed(n_shape), _fixed(n_shape), _fixed(m_shape), _fixed(m_shape),
         _fixed((2, SSM_LANES)), _fixed((2, SSM_LANES)), _fixed((1, D_SSM)),
         _fixed((D_SSM, D_SSM)), _fixed((D_SSM, D_SSM)), _fixed((1, D_SSM))],
        (blk, _fixed(m_shape), _fixed(m_shape), _fixed(n_shape), _fixed(n_shape),
         _fixed((2, 8, SSM_LANES)), _fixed((1, D_SSM)), _fixed((D_SSM, D_SSM)), _fixed((1, D_SSM))),
        (SDS((t, D_SSM), BF16), SDS(m_shape, F32), SDS(m_shape, F32), SDS(n_shape, F32), SDS(n_shape, F32),
         SDS((2, 8, SSM_LANES), F32), SDS((1, D_SSM), F32), SDS((D_SSM, D_SSM), F32), SDS((1, D_SSM), F32)),
        (za, y2p, doa, carries, sp["perm"], sp["permt"], sp["mre"], sp["mim"], sp["mtre"], sp["mtim"],
         sp["nre"], sp["nim"], sp["ntre"], sp["ntim"], sp["a"], sp["ap"], sp["dskip"], sp["glu_w"],
         sp["glu_wt"], sp["glu_b"]),
        scratch=[pltpu.VMEM((tb, SSM_LANES), F32), pltpu.VMEM((tb, SSM_LANES), F32),
                 pltpu.VMEM((tb, SSM_LANES), F32), pltpu.VMEM((tb, SSM_LANES), F32),
                 pltpu.VMEM((2, 8, SSM_LANES), F32), pltpu.VMEM((2, SSM_LANES), F32),
                 pltpu.VMEM((2, SSM_LANES), F32)],
        sem=("arbitrary", "arbitrary"), bg=bg)


def _gmlp_spatial(ws_ref, vb):
    lane = lax.broadcasted_iota(jnp.int32, (CHUNK, 128), 1)
    parts = []
    for j in range(GMLP_HEADS // 2):
        vp = vb[:, 128 * j:128 * (j + 1)]
        parts.append(jnp.where(lane < GMLP_HEAD_DIM, _dot(ws_ref[2 * j], vp), _dot(ws_ref[2 * j + 1], vp)))
    return jnp.concatenate(parts, axis=1)


def _gmlp_fwd(zuv, ln_g, ln_b, wsm, bias, bg=None):
    t = zuv.shape[0]

    def body(z_ref, g_ref, b_ref, ws_ref, bias_ref, out_ref, outt_ref):
        u = _gelu(z_ref[:, 0:D_GMLP].astype(F32))
        v0 = _gelu(z_ref[:, D_GMLP:2 * D_GMLP].astype(F32))
        v, _, _ = _ln_fwd(v0, g_ref[...], b_ref[...])
        s = _gmlp_spatial(ws_ref, v.astype(BF16)) + bias_ref[...]
        out = (u * s).astype(BF16)
        out_ref[...] = out
        outt_ref[...] = out.T

    return _call(
        body, "gmlp_fwd", (t // CHUNK,),
        [_rows(CHUNK, 2 * D_GMLP), _fixed((1, D_GMLP)), _fixed((1, D_GMLP)),
         _fixed((GMLP_HEADS, CHUNK, CHUNK)), _fixed((CHUNK, D_GMLP))],
        (_rows(CHUNK, D_GMLP), _cols(D_GMLP, CHUNK)), (SDS((t, D_GMLP), BF16), SDS((D_GMLP, t), BF16)),
        (zuv, ln_g, ln_b, wsm, bias), sem=("parallel",), bg=bg)


def _gmlp_bwd(zuv, dgm, ln_g, ln_b, wsm, wsmt, bias, bg=None):
    t = zuv.shape[0]

    def body(z_ref, d_ref, g_ref, b_ref, ws_ref, wst_ref, bias_ref,
             dz_ref, dws_ref, dbias_ref, dg_ref, db_ref):
        @pl.when(pl.program_id(0) == 0)
        def _():
            for r in (dws_ref, dbias_ref, dg_ref, db_ref):
                r[...] = jnp.zeros_like(r)

        zu = z_ref[:, 0:D_GMLP].astype(F32)
        zv = z_ref[:, D_GMLP:2 * D_GMLP].astype(F32)
        u = _gelu(zu)
        v0 = _gelu(zv)
        gam = g_ref[...]
        v, vhat, rstd = _ln_fwd(v0, gam, b_ref[...])
        vb = v.astype(BF16)
        s = _gmlp_spatial(ws_ref, vb) + bias_ref[...]
        d = d_ref[...]
        dz_ref[:, 0:D_GMLP] = (d * s * _gelu_grad(zu)).astype(BF16)
        ds = d * u
        dbias_ref[...] += ds
        dsb = ds.astype(BF16)
        lane = lax.broadcasted_iota(jnp.int32, (CHUNK, 128), 1)
        tril = (lax.broadcasted_iota(jnp.int32, (CHUNK, CHUNK), 0)
                >= lax.broadcasted_iota(jnp.int32, (CHUNK, CHUNK), 1))
        zero_b = jnp.zeros((CHUNK, 128), BF16)
        parts = []
        for j in range(GMLP_HEADS // 2):
            dsp = dsb[:, 128 * j:128 * (j + 1)]
            vp = vb[:, 128 * j:128 * (j + 1)]
            parts.append(jnp.where(lane < GMLP_HEAD_DIM, _dot(wst_ref[2 * j], dsp),
                                   _dot(wst_ref[2 * j + 1], dsp)))
            lo = jnp.where(lane < GMLP_HEAD_DIM, dsp, zero_b)
            hi = jnp.where(lane < GMLP_HEAD_DIM, zero_b, dsp)
            dws_ref[2 * j] += jnp.where(tril, _dot_nt(lo, vp), 0.0)
            dws_ref[2 * j + 1] += jnp.where(tril, _dot_nt(hi, vp), 0.0)
        dv = jnp.concatenate(parts, axis=1)
        dg_ref[...] += jnp.sum(dv * vhat, axis=0, keepdims=True)
        db_ref[...] += jnp.sum(dv, axis=0, keepdims=True)
        dz_ref[:, D_GMLP:2 * D_GMLP] = (_ln_bwd(dv, vhat, rstd, gam) * _gelu_grad(zv)).astype(BF16)

    return _call(
        body, "gmlp_bwd", (t // CHUNK,),
        [_rows(CHUNK, 2 * D_GMLP), _rows(CHUNK, D_GMLP), _fixed((1, D_GMLP)), _fixed((1, D_GMLP)),
         _fixed((GMLP_HEADS, CHUNK, CHUNK)), _fixed((GMLP_HEADS, CHUNK, CHUNK)), _fixed((CHUNK, D_GMLP))],
        (_rows(CHUNK, 2 * D_GMLP), _fixed((GMLP_HEADS, CHUNK, CHUNK)), _fixed((CHUNK, D_GMLP)),
         _fixed((1, D_GMLP)), _fixed((1, D_GMLP))),
        (SDS((t, 2 * D_GMLP), BF16), SDS((GMLP_HEADS, CHUNK, CHUNK), F32), SDS((CHUNK, D_GMLP), F32),
         SDS((1, D_GMLP), F32), SDS((1, D_GMLP), F32)),
        (zuv, dgm, ln_g, ln_b, wsm, wsmt, bias), sem=("arbitrary",), bg=bg)


def _mixout_fwd(x1, s5o, gm, gab, ua, ub, wmo, g, b, tm, bg=None):
    t = x1.shape[0]

    def body(x_ref, s_ref, m_ref, gab_ref, ua_ref, ub_ref, wmo_ref, g_ref, b_ref,
             xn_ref, xh_ref, rstd_ref):
        ya = _dot(s_ref[...], ua_ref[...])
        yb = _dot(m_ref[...], ub_ref[...])
        mix = (_sigmoid(gab_ref[:, 0:D_MODEL].astype(F32)) * ya
               + _sigmoid(gab_ref[:, D_MODEL:2 * D_MODEL].astype(F32)) * yb)
        r = ALPHA * x_ref[...] + _dot(mix.astype(BF16), wmo_ref[...])
        y, xh, rstd = _ln_fwd(r, g_ref[...], b_ref[...])
        xn_ref[...] = y
        xh_ref[...] = xh
        rstd_ref[...] = rstd

    return _call(
        body, "mixout_fwd", (t // tm,),
        [_rows(tm, D_MODEL), _rows(tm, D_SSM), _rows(tm, D_GMLP), _rows(tm, 2 * D_MODEL),
         _resident((D_SSM, D_MODEL)), _resident((D_GMLP, D_MODEL)), _resident((D_MODEL, D_MODEL)),
         _fixed((1, D_MODEL)), _fixed((1, D_MODEL))],
        (_rows(tm, D_MODEL), _rows(tm, D_MODEL), _rows(tm, 1)),
        (SDS((t, D_MODEL), F32), SDS((t, D_MODEL), F32), SDS((t, 1), F32)),
        (x1, s5o, gm, gab, ua, ub, wmo, g, b), sem=("parallel",), bg=bg)


def _mixout_bwd(dx2, xh, rstd, s5o, gm, gab, ua, ub, wmo, g, tm, bg=None):
    t = dx2.shape[0]

    def body(d_ref, xh_ref, rstd_ref, s_ref, m_ref, gab_ref, ua_ref, ub_ref, wmo_ref, g_ref,
             dx1_ref, dmx_ref, mb_ref, dya_ref, dyb_ref, ds5_ref, dgm_ref, dgab_ref, dg_ref, db_ref):
        @pl.when(pl.program_id(0) == 0)
        def _():
            dg_ref[...] = jnp.zeros_like(dg_ref)
            db_ref[...] = jnp.zeros_like(db_ref)

        dy = d_ref[...]
        xhv = xh_ref[...]
        dr = _ln_bwd(dy, xhv, rstd_ref[...], g_ref[...])
        dg_ref[...] += jnp.sum(dy * xhv, axis=0, keepdims=True)
        db_ref[...] += jnp.sum(dy, axis=0, keepdims=True)
        dx1_ref[...] = ALPHA * dr
        drb = dr.astype(BF16)
        dmx_ref[...] = drb
        dm = _dot_nt(drb, wmo_ref[...])
        ya = _dot(s_ref[...], ua_ref[...])
        yb = _dot(m_ref[...], ub_ref[...])
        sa = _sigmoid(gab_ref[:, 0:D_MODEL].astype(F32))
        sb = _sigmoid(gab_ref[:, D_MODEL:2 * D_MODEL].astype(F32))
        mb_ref[...] = (sa * ya + sb * yb).astype(BF16).T
        dya = (dm * sa).astype(BF16)
        dyb = (dm * sb).astype(BF16)
        dya_ref[...] = dya
        dyb_ref[...] = dyb
        dgab_ref[:, 0:D_MODEL] = (dm * ya * sa * (1.0 - sa)).astype(BF16)
        dgab_ref[:, D_MODEL:2 * D_MODEL] = (dm * yb * sb * (1.0 - sb)).astype(BF16)
        ds5_ref[...] = _dot_nt(dya, ua_ref[...])
        dgm_ref[...] = _dot_nt(dyb, ub_ref[...])

    return _call(
        body, "mixout_bwd", (t // tm,),
        [_rows(tm, D_MODEL), _rows(tm, D_MODEL), _rows(tm, 1), _rows(tm, D_SSM), _rows(tm, D_GMLP),
         _rows(tm, 2 * D_MODEL), _resident((D_SSM, D_MODEL)), _resident((D_GMLP, D_MODEL)),
         _resident((D_MODEL, D_MODEL)), _fixed((1, D_MODEL))],
        (_rows(tm, D_MODEL), _rows(tm, D_MODEL), _cols(D_MODEL, tm), _rows(tm, D_MODEL),
         _rows(tm, D_MODEL), _rows(tm, D_SSM), _rows(tm, D_GMLP), _rows(tm, 2 * D_MODEL),
         _fixed((1, D_MODEL)), _fixed((1, D_MODEL))),
        (SDS((t, D_MODEL), F32), SDS((t, D_MODEL), BF16), SDS((D_MODEL, t), BF16),
         SDS((t, D_MODEL), BF16), SDS((t, D_MODEL), BF16), SDS((t, D_SSM), F32),
         SDS((t, D_GMLP), F32), SDS((t, 2 * D_MODEL), BF16),
         SDS((1, D_MODEL), F32), SDS((1, D_MODEL), F32)),
        (dx2, xh, rstd, s5o, gm, gab, ua, ub, wmo, g), sem=("arbitrary",), bg=bg)


def _ple_loss(x3, p, tgt, wpg, wpp, tm, bg=None):
    t = x3.shape[0]

    def body(x_ref, p_ref, t_ref, wpg_ref, wpp_ref, dx_ref, xb_ref, pb_ref, dq_ref, de_ref, loss_ref):
        @pl.when(pl.program_id(0) == 0)
        def _():
            loss_ref[...] = jnp.zeros_like(loss_ref)

        x3v = x_ref[...]
        xb = x3v.astype(BF16)
        pb = p_ref[...].astype(BF16)
        xb_ref[...] = xb.T
        pb_ref[...] = pb.T
        s = _sigmoid(_dot(xb, wpg_ref[...]))
        e = _dot(pb, wpp_ref[...])
        diff = x3v + s * e - t_ref[...]
        loss_ref[...] += jnp.sum(diff * diff, axis=0, keepdims=True)
        dout = diff * (1.0 / D_MODEL)
        de_ref[...] = (dout * s).astype(BF16)
        dq = (dout * e * s * (1.0 - s)).astype(BF16)
        dq_ref[...] = dq
        dx_ref[...] = dout + _dot_nt(dq, wpg_ref[...])

    return _call(
        body, "ple_loss", (t // tm,),
        [_rows(tm, D_MODEL), _rows(tm, PLE_DIM), _rows(tm, D_MODEL),
         _resident((D_MODEL, D_MODEL)), _resident((PLE_DIM, D_MODEL))],
        (_rows(tm, D_MODEL), _cols(D_MODEL, tm), _cols(PLE_DIM, tm), _rows(tm, D_MODEL),
         _rows(tm, D_MODEL), _fixed((1, D_MODEL))),
        (SDS((t, D_MODEL), F32), SDS((D_MODEL, t), BF16), SDS((PLE_DIM, t), BF16),
         SDS((t, D_MODEL), BF16), SDS((t, D_MODEL), BF16), SDS((1, D_MODEL), F32)),
        (x3, p, tgt, wpg, wpp), sem=("arbitrary",), bg=bg)


def _s5_discretise(lre, lim, log_dt, bre, bim):
    dt = jnp.exp(log_dt)[:, None]
    mag = jnp.exp(lre * dt)
    abr = mag * jnp.cos(lim * dt)
    abi = mag * jnp.sin(lim * dt)
    nr = abr - 1.0
    ni = abi
    den = lre * lre + lim * lim
    cr = ((nr * lre + ni * lim) / den)[..., None]
    ci = ((ni * lre - nr * lim) / den)[..., None]
    return abr, abi, cr * bre - ci * bim, cr * bim + ci * bre


def _block_diag_in(bb):
    v = bb.reshape(S5_BLOCKS, 8, SSM_STATE, SSM_GROUP_CH).transpose(0, 1, 3, 2)
    return jnp.einsum("bgip,gh->bgihp", v, jnp.eye(8, dtype=bb.dtype)).reshape(
        S5_BLOCKS, S5_BLOCK_IN, S5_BLOCK_ST)


def _block_diag_in_t(dm):
    v = dm.reshape(S5_BLOCKS, 8, SSM_GROUP_CH, 8, SSM_STATE)
    d = jnp.einsum("bgihp,gh->bgip", v, jnp.eye(8, dtype=dm.dtype))
    return d.transpose(0, 1, 3, 2).reshape(SSM_GROUPS, SSM_STATE, SSM_GROUP_CH)


def _block_diag_out(cc):
    v = cc.reshape(S5_BLOCKS, 8, SSM_GROUP_CH, SSM_STATE)
    return jnp.einsum("bgip,gh->bgphi", v, jnp.eye(8, dtype=cc.dtype)).reshape(
        S5_BLOCKS, S5_BLOCK_ST, S5_BLOCK_IN)


def _block_diag_out_t(dn):
    v = dn.reshape(S5_BLOCKS, 8, SSM_STATE, 8, SSM_GROUP_CH)
    d = jnp.einsum("bgphi,gh->bgip", v, jnp.eye(8, dtype=dn.dtype))
    return d.reshape(SSM_GROUPS, SSM_GROUP_CH, SSM_STATE)


def _s5_setup(lre, lim, log_dt, bre, bim, cre, cim, d_skip, glu_w, glu_b, tb):
    seg = tb // 8
    abr, abi, bbr, bbi = _s5_discretise(lre, lim, log_dt, bre, bim)
    pr, pi = abr, abi
    for _ in range(int(math.log2(seg))):
        pr, pi = pr * pr - pi * pi, 2.0 * pr * pi
    rows = jnp.arange(tb)
    src = (rows % 8) * seg + rows // 8
    perm = (src[:, None] == jnp.arange(tb)[None, :]).astype(BF16)
    mre = _block_diag_in(bbr)
    mim = _block_diag_in(bbi)
    nre = _block_diag_out(cre)
    nim = _block_diag_out(cim)
    return {
        "perm": perm, "permt": perm.T,
        "mre": mre.astype(BF16), "mim": mim.astype(BF16),
        "mtre": mre.transpose(0, 2, 1).astype(BF16), "mtim": mim.transpose(0, 2, 1).astype(BF16),
        "nre": nre.astype(BF16), "nim": nim.astype(BF16),
        "ntre": nre.transpose(0, 2, 1).astype(BF16), "ntim": nim.transpose(0, 2, 1).astype(BF16),
        "a": jnp.stack([abr.reshape(-1), abi.reshape(-1)]),
        "ap": jnp.stack([pr.reshape(-1), pi.reshape(-1)]),
        "dskip": d_skip.reshape(1, D_SSM), "glu_w": glu_w, "glu_wt": glu_w.T,
        "glu_b": glu_b.reshape(1, D_SSM),
    }


BIG = ("ffn1_w_in", "ffn1_w_out", "mix_w_in", "ssm_glu_w", "up_a", "up_b", "mix_w_out",
       "ffn2_w_in", "ffn2_w_out", "ple_w_proj", "ple_w_gate")
BIG_AXIS = {"ffn1_w_in": 1, "ffn1_w_out": 0, "mix_w_in": 1, "ssm_glu_w": 0, "up_a": 1, "up_b": 1,
            "mix_w_out": 0, "ffn2_w_in": 1, "ffn2_w_out": 0, "ple_w_proj": 1, "ple_w_gate": 0}
SHARD_MAJOR = 2
GATHER_AXIS = dict(BIG_AXIS, ffn1_w_in=SHARD_MAJOR, ffn2_w_in=SHARD_MAJOR)
GATHER_ORDER = (("ffn1_w_in",), ("ffn1_w_out",), ("mix_w_in",), ("ssm_glu_w", "up_a", "up_b", "mix_w_out"),
                ("ffn2_w_in",), ("ffn2_w_out", "ple_w_gate", "ple_w_proj"))
GATHER_FIRST_ID = 1
REDUCE_FIRST_ID = 7
SMALL = ("ln1_g", "ln1_b", "ssm_lambda_re", "ssm_lambda_im", "ssm_log_dt", "ssm_b_re", "ssm_b_im",
         "ssm_c_re", "ssm_c_im", "ssm_d", "ssm_glu_b", "gmlp_ln_g", "gmlp_ln_b", "gmlp_w_s",
         "gmlp_b_s", "ln2_g", "ln2_b", "ln3_g", "ln3_b")
SMALL_VIEW = {"ssm_b_re": (SSM_GROUPS, SSM_STATE * SSM_GROUP_CH), "ssm_b_im": (SSM_GROUPS, SSM_STATE * SSM_GROUP_CH)}


def _small_view(k, a):
    return a.reshape(SMALL_VIEW[k]) if k in SMALL_VIEW else a


def _place():
    return lax.axis_index("x"), lax.axis_index("y"), lax.axis_index("c")


def _other_chips(x, y):
    return [(1 - x, y), (x, 1 - y), (1 - x, 1 - y)]


def _window(ref, shard_shape, axis, chip, half):
    r, c = shard_shape
    hr = r // 2
    if axis == SHARD_MAJOR:
        return ref.at[chip] if half is None else ref.at[chip, pl.ds(half * hr, hr), :]
    if axis == 0:
        if half is None:
            return ref.at[pl.ds(chip * r, r), :]
        return ref.at[pl.ds(chip * r + half * hr, hr), :]
    if half is None:
        return ref.at[:, pl.ds(chip * c, c)]
    return ref.at[pl.ds(half * hr, hr), pl.ds(chip * c, c)]


def _gather_weights(shards, axes):
    n = len(shards)
    shapes = [s.shape for s in shards]
    full = [{0: (4 * r, c), 1: (r, 4 * c), SHARD_MAJOR: (4, r, c)}[ax] for (r, c), ax in zip(shapes, axes)]

    def remote(sems, i, k, src, dst, to):
        return pltpu.make_async_remote_copy(src_ref=src, dst_ref=dst, send_sem=sems[0].at[6 * i + k],
                                            recv_sem=sems[1].at[6 * i + k], device_id=to, device_id_type=MESH)

    def own_copies(ins, outs, sems):
        x, y, c = _place()
        me = 2 * x + y
        cps = []
        for i in range(n):
            hr = shapes[i][0] // 2
            mine = ins[i].at[pl.ds(c * hr, hr), :]
            for j, (cx, cy) in enumerate(_other_chips(x, y)):
                cps.append(remote(sems, i, j, mine, _window(outs[i], shapes[i], axes[i], me, c), (cx, cy, c)))
        local = [pltpu.make_async_copy(ins[i], _window(outs[i], shapes[i], axes[i], me, None), sems[2].at[i])
                 for i in range(n)]
        return cps, local

    def start(ins, outs, sems):
        cps, local = own_copies(ins, outs, sems)
        for cp in local + cps:
            cp.start()

    def finish(ins, outs, sems):
        x, y, c = _place()
        sibling = (x, y, 1 - c)
        passed = []
        for j, (cx, cy) in enumerate(_other_chips(x, y)):
            for i in range(n):
                w = _window(outs[i], shapes[i], axes[i], 2 * cx + cy, c)
                remote(sems, i, j, w, w, (cx, cy, c)).wait_recv()
                cp = remote(sems, i, 3 + j, w, w, sibling)
                cp.start()
                passed.append(cp)
        for j, (cx, cy) in enumerate(_other_chips(x, y)):
            for i in range(n):
                w = _window(outs[i], shapes[i], axes[i], 2 * cx + cy, 1 - c)
                remote(sems, i, 3 + j, w, w, sibling).wait_recv()
        cps, local = own_copies(ins, outs, sems)
        for cp in cps + passed:
            cp.wait_send()
        for cp in local:
            cp.wait()

    return _Exchange(shards, [SDS(f, BF16) for f in full],
                     [pltpu.SemaphoreType.DMA((6 * n,)), pltpu.SemaphoreType.DMA((6 * n,)),
                      pltpu.SemaphoreType.DMA((n,))], start, finish)


def _scatter_grads(parts, shapes, axes):
    n = len(parts)

    def copies(ins, outs, sems):
        x, y, c = _place()
        return [pltpu.make_async_remote_copy(
            src_ref=_window(ins[i], shapes[i], axes[i], 2 * cx + cy, None), dst_ref=outs[i].at[j],
            send_sem=sems[0].at[3 * i + j], recv_sem=sems[1].at[3 * i + j],
            device_id=(cx, cy, c), device_id_type=MESH)
            for i in range(n) for j, (cx, cy) in enumerate(_other_chips(x, y))]

    def start(ins, outs, sems):
        for cp in copies(ins, outs, sems):
            cp.start()

    def finish(ins, outs, sems):
        for cp in copies(ins, outs, sems):
            cp.wait()

    return _Exchange(parts, [SDS((3,) + tuple(s), BF16) for s in shapes],
                     [pltpu.SemaphoreType.DMA((3 * n,)), pltpu.SemaphoreType.DMA((3 * n,))], start, finish)


def _swap_halves(parts, shapes, axes):
    n = len(parts)

    def copies(ins, outs, sems):
        x, y, c = _place()
        cps = []
        for i in range(n):
            r, _ = shapes[i]
            hr = r // 2
            if axes[i] == 0:
                cps += [pltpu.make_async_remote_copy(
                    src_ref=ins[i].at[pl.ds(k * r + (1 - c) * hr, hr), :], dst_ref=outs[i].at[k],
                    send_sem=sems[0].at[i], recv_sem=sems[1].at[i], device_id=(x, y, 1 - c),
                    device_id_type=MESH) for k in range(4)]
            else:
                cps.append(pltpu.make_async_remote_copy(
                    src_ref=ins[i].at[pl.ds((1 - c) * hr, hr), :], dst_ref=outs[i],
                    send_sem=sems[0].at[i], recv_sem=sems[1].at[i], device_id=(x, y, 1 - c),
                    device_id_type=MESH))
        return cps

    def start(ins, outs, sems):
        for cp in copies(ins, outs, sems):
            cp.start()

    def finish(ins, outs, sems):
        x, y, c = _place()
        for i in range(n):
            pltpu.make_async_remote_copy(src_ref=outs[i], dst_ref=outs[i], send_sem=sems[0].at[i],
                                         recv_sem=sems[1].at[i], device_id=(x, y, 1 - c),
                                         device_id_type=MESH).wait()

    out = [SDS((4, r // 2, c), BF16) if ax == 0 else SDS((r // 2, 4 * c), BF16)
           for (r, c), ax in zip(shapes, axes)]
    return _Exchange(parts, out, [pltpu.SemaphoreType.DMA((n,)), pltpu.SemaphoreType.DMA((n,))], start, finish)


def _scatter_halves(pres, shapes):
    n = len(pres)

    def copies(ins, outs, sems):
        x, y, c = _place()
        return [pltpu.make_async_remote_copy(
            src_ref=ins[i].at[1 + j], dst_ref=outs[i].at[j], send_sem=sems[0].at[3 * i + j],
            recv_sem=sems[1].at[3 * i + j], device_id=(cx, cy, c), device_id_type=MESH)
            for i in range(n) for j, (cx, cy) in enumerate(_other_chips(x, y))]

    def start(ins, outs, sems):
        for cp in copies(ins, outs, sems):
            cp.start()

    def finish(ins, outs, sems):
        for cp in copies(ins, outs, sems):
            cp.wait()

    return _Exchange(pres, [SDS((3, r // 2, c), BF16) for r, c in shapes],
                     [pltpu.SemaphoreType.DMA((3 * n,)), pltpu.SemaphoreType.DMA((3 * n,))], start, finish)


def _swap_with_sibling(arrs):
    n = len(arrs)

    def copies(ins, outs, sems):
        x, y, c = _place()
        return [pltpu.make_async_remote_copy(src_ref=ins[i], dst_ref=outs[i], send_sem=sems[0].at[i],
                                             recv_sem=sems[1].at[i], device_id=(x, y, 1 - c),
                                             device_id_type=MESH) for i in range(n)]

    def start(ins, outs, sems):
        for cp in copies(ins, outs, sems):
            cp.start()

    def finish(ins, outs, sems):
        for cp in copies(ins, outs, sems):
            cp.wait()

    return _Exchange(arrs, [SDS(a.shape, a.dtype) for a in arrs],
                     [pltpu.SemaphoreType.DMA((n,)), pltpu.SemaphoreType.DMA((n,))], start, finish)


def _gather_small(arrs):
    n = len(arrs)

    def copy(sems, outs, i, k, block, to, src=None):
        px, py, pc = block
        dst = outs[i].at[4 * px + 2 * py + pc]
        return pltpu.make_async_remote_copy(
            src_ref=dst if src is None else src, dst_ref=dst, send_sem=sems[0].at[7 * i + k],
            recv_sem=sems[1].at[7 * i + k], device_id=to, device_id_type=MESH)

    direct = [math.prod(a.shape) * 4 <= DIRECT_GATHER_BYTES for a in arrs]

    def own_copies(ins, outs, sems):
        x, y, c = _place()
        cps = []
        for i in range(n):
            cps.append(copy(sems, outs, i, 0, (x, y, c), (x, y, 1 - c), src=ins[i]))
            for j, (cx, cy) in enumerate(_other_chips(x, y)):
                cps.append(copy(sems, outs, i, 1 + j, (x, y, c), (cx, cy, c), src=ins[i]))
                if direct[i]:
                    cps.append(copy(sems, outs, i, 4 + j, (x, y, c), (cx, cy, 1 - c), src=ins[i]))
        local = [pltpu.make_async_copy(ins[i], outs[i].at[4 * x + 2 * y + c], sems[2].at[i]) for i in range(n)]
        return cps, local

    def start(ins, outs, sems):
        cps, local = own_copies(ins, outs, sems)
        for cp in local + cps:
            cp.start()

    def finish(ins, outs, sems):
        x, y, c = _place()
        passed = []
        for j, (cx, cy) in enumerate(_other_chips(x, y)):
            for i in range(n):
                copy(sems, outs, i, 1 + j, (cx, cy, c), (x, y, c)).wait_recv()
                if not direct[i]:
                    cp = copy(sems, outs, i, 4 + j, (cx, cy, c), (x, y, 1 - c))
                    cp.start()
                    passed.append(cp)
        for i in range(n):
            copy(sems, outs, i, 0, (x, y, 1 - c), (x, y, c)).wait_recv()
            for j, (cx, cy) in enumerate(_other_chips(x, y)):
                copy(sems, outs, i, 4 + j, (cx, cy, 1 - c), (x, y, c)).wait_recv()
        cps, local = own_copies(ins, outs, sems)
        for cp in cps + passed:
            cp.wait_send()
        for cp in local:
            cp.wait()

    return _Exchange(arrs, [SDS((N_DEV,) + a.shape, F32) for a in arrs],
                     [pltpu.SemaphoreType.DMA((7 * n,)), pltpu.SemaphoreType.DMA((7 * n,)),
                      pltpu.SemaphoreType.DMA((n,))], start, finish)


def _local_step(x, p, tgt, wb, ws, shards=None, opt=None):
    bsz, seq, _ = x.shape
    t = bsz * seq
    tm = min(256, t)
    tb = min(256, seq)
    x0 = x.reshape(t, D_MODEL)
    p0 = p.reshape(t, PLE_DIM)
    tg = tgt.reshape(t, D_MODEL)
    row = lambda v: v.reshape(1, -1)
    dist = shards is not None
    wb = dict(wb)
    recv, sums, other, gathered = {}, {}, {}, {}
    gb = {}
    gs = {}
    shape_of, axis_of = {}, {}
    chip = None
    if dist:
        shape_of = {k: tuple(shards[k].shape) for k in BIG}
        axis_of = dict(BIG_AXIS)
        for q in range(LAST_PIECES):
            shape_of[LAST_PIECE % q] = (D_MODEL // LAST_PIECES, shape_of["ffn1_w_in"][1])
            axis_of[LAST_PIECE % q] = 1
        xi, yi, ci = _place()
        chip = (2 * xi + yi).astype(jnp.int32).reshape(1)
        ids = jnp.stack([2 * xi + yi] + [2 * cx + cy for cx, cy in _other_chips(xi, yi)] + [ci]).astype(jnp.int32)
    halfbuf, pre = {}, {}

    def gather(names):
        return _gather_weights([shards[k] for k in names], [GATHER_AXIS[k] for k in names]) if dist else None

    def exchange(scat=(), swap=(), halves=(), scat2=(), swap2=(), extra=None, after=None):
        if not dist:
            return None, []
        after = order[0] if after is None else after
        parts, tags = [], []
        if scat:
            parts.append(_scatter_grads([gb[k][1] for k in scat], [shape_of[k] for k in scat],
                                        [axis_of[k] for k in scat]))
            tags.append((recv, scat))
        if swap:
            for k in swap:
                sums[k] = order[0] = _sum_blocks(gb[k][0], recv[k], shape_of[k], axis_of[k], chip, "sum_" + k,
                                                 order[0])
            parts.append(_swap_with_sibling([sums[k] for k in swap]))
            tags.append((other, swap))
        if halves:
            parts.append(_swap_halves([gb[k][1] for k in halves], [shape_of[k] for k in halves],
                                      [axis_of[k] for k in halves]))
            tags.append((halfbuf, halves))
        if scat2:
            for k in scat2:
                pre[k] = _presum(gb[k][0], halfbuf[k], shape_of[k], axis_of[k], ids, "presum_" + k, order[0])
                order[0] = pre[k][0]
            parts.append(_scatter_halves([pre[k][1] for k in scat2], [shape_of[k] for k in scat2]))
            tags.append((recv, scat2))
        if swap2:
            for k in swap2:
                sums[k] = order[0] = _sum_half(pre[k][0], recv[k], "sum_" + k, order[0])
            parts.append(_swap_with_sibling([sums[k] for k in swap2]))
            tags.append((other, swap2))
        if extra is not None:
            parts.append(extra[0])
            tags.append((extra[1], extra[2]))
        return (_join(parts), tags) if parts else (None, [])

    def take(ex_tags, got):
        ex, tags = ex_tags
        if ex is not None:
            for (dst, names), (o0, o1) in zip(tags, ex.cuts):
                dst.update(zip(names, got[o0:o1]))

    order = [None]

    def ordered(builder, *args, **kw):
        res = builder(*args, bg=order[0] if dist else None, **kw)
        order[0] = res[0][0]
        return res

    launched = []

    def launch(ex_tags):
        if ex_tags[0] is not None:
            n = len(launched)
            launched.append(n)
            take(ex_tags, _run_exchange_on_sequencer(ex_tags[0], "reduce_%d" % n, REDUCE_FIRST_ID + n))

    small_shape = {k: _small_view(k, v).shape for k, v in ws.items()}
    small_shape["loss_rows"] = (1, D_MODEL)
    ws = {k: v if (v.ndim == 2 and k != "ssm_log_dt") else v[0] for k, v in ws.items()}
    tril = jnp.tril(jnp.ones((CHUNK, CHUNK), dtype=bool))
    wsm = jnp.where(tril[None], ws["gmlp_w_s"], 0.0)
    wsm_b = wsm.astype(BF16)
    wsmt_b = wsm.transpose(0, 2, 1).astype(BF16)
    bias = jnp.repeat(ws["gmlp_b_s"].T, GMLP_HEAD_DIM, axis=1)

    tf = min(512, t)
    if dist:
        for gi, names in enumerate(GATHER_ORDER):
            wb.update(zip(names, _run_exchange_on_sequencer(gather(names), "gather_%d" % gi, GATHER_FIRST_ID + gi)))
    (x0b, h1, a1), _ = _ffn_proj(x0, wb["ffn1_w_in"], tf, "ffn1_proj")
    (x1, xh1, rstd1), _ = _ffn_out(x0, a1, wb["ffn1_w_out"], row(ws["ln1_g"]), row(ws["ln1_b"]), tf, "ffn1_out")
    sp = _s5_setup(ws["ssm_lambda_re"], ws["ssm_lambda_im"], ws["ssm_log_dt"], ws["ssm_b_re"],
                   ws["ssm_b_im"], ws["ssm_c_re"], ws["ssm_c_im"], ws["ssm_d"], wb["ssm_glu_w"],
                   ws["ssm_glu_b"], tb)
    (x1b, za, zuv, gab), _ = _mixin_fwd(x1, wb["mix_w_in"], tm)
    (s5o, s5ot, y2p, carries), _ = _s5_fwd(za, sp, bsz, seq, tb)
    (gm, gmt), _ = _gmlp_fwd(zuv, row(ws["gmlp_ln_g"]), row(ws["gmlp_ln_b"]), wsm_b, bias)
    (x2, xh2, rstd2), _ = _mixout_fwd(x1, s5o, gm, gab, wb["up_a"], wb["up_b"], wb["mix_w_out"],
                                           row(ws["ln2_g"]), row(ws["ln2_b"]), tm)
    (x2b, h2, a2), _ = _ffn_proj(x2, wb["ffn2_w_in"], tf, "ffn2_proj")
    (x3, xh3, rstd3), _ = _ffn_out(x2, a2, wb["ffn2_w_out"], row(ws["ln3_g"]), row(ws["ln3_b"]), tf, "ffn2_out")
    (dx3, x3b, pb, dq, de, loss_rows), _ = _ple_loss(x3, p0, tg, wb["ple_w_gate"], wb["ple_w_proj"], tm)
    order[0] = dx3
    gb["ple_w_gate"], _ = ordered(_tn_matmul, x3b, dq, "dw_ple_gate", 1024, 1024, a_t=True)
    gb["ple_w_proj"], _ = ordered(_tn_matmul, pb, de, "dw_ple_proj", 256, 1024, a_t=True)
    launch(exchange(scat=("ple_w_gate", "ple_w_proj")))
    (dx2, dh2, df2, gs["ln3_g"], gs["ln3_b"]), _ = ordered(
        _ffn_bwd, dx3, xh3, rstd3, h2, wb["ffn2_w_in"], wb["ffn2_w_out"], row(ws["ln3_g"]), tm, "ffn2_bwd")
    gb["ffn2_w_out"], _ = ordered(_tn_matmul, a2, df2, "dw_ffn2_out", 1408, 1024)
    launch(exchange(scat=("ffn2_w_out",)))
    gb["ffn2_w_in"], _ = ordered(_tn_matmul, x2b, dh2, "dw_ffn2_in", 1024, 1408, a_t=True)
    launch(exchange(scat=("ffn2_w_in",), swap=("ple_w_gate", "ple_w_proj")))
    (dx1a, dmx, mb, dya, dyb, ds5, dgm, dgab, gs["ln2_g"], gs["ln2_b"]), _ = ordered(
        _mixout_bwd, dx2, xh2, rstd2, s5o, gm, gab, wb["up_a"], wb["up_b"], wb["mix_w_out"], row(ws["ln2_g"]), tm)
    gb["mix_w_out"], _ = ordered(_tn_matmul, mb, dmx, "dw_mix_out", 1024, 1024, a_t=True)
    gb["up_a"], _ = ordered(_tn_matmul, s5ot, dya, "dw_up_a", 512, 1024, a_t=True)
    gb["up_b"], _ = ordered(_tn_matmul, gmt, dyb, "dw_up_b", 512, 1024, a_t=True)
    launch(exchange(scat=("mix_w_out", "up_a", "up_b"), swap=("ffn2_w_out",)))
    (dza, dmr, dmi, dnr, dni, da, ddsk, dgw, dgb), _ = ordered(_s5_bwd, za, y2p, ds5, carries, sp, bsz, seq, tb)
    gb["ssm_glu_w"] = (dgw, dgw.astype(BF16))
    launch(exchange(scat=("ssm_glu_w",), swap=("ffn2_w_in",)))
    (dzuv, dws, dbias, gs["gmlp_ln_g"], gs["gmlp_ln_b"]), _ = ordered(
        _gmlp_bwd, zuv, dgm, row(ws["gmlp_ln_g"]), row(ws["gmlp_ln_b"]), wsm_b, wsmt_b, bias)
    (dx1,), _ = ordered(_mixin_bwd, dx1a, dza, dzuv, dgab, wb["mix_w_in"], tm)
    g_mi, _ = ordered(_tn_matmul, x1b, dza, "dw_mix_in_a", 1024, 512, 0, 3584, a_t=True)
    g_mi, _ = ordered(_tn_matmul, x1b, dzuv, "dw_mix_in_uv", 1024, 512, 1, 3584, g_mi, a_t=True)
    gb["mix_w_in"], _ = ordered(_tn_matmul, x1b, dgab, "dw_mix_in_g", 1024, 512, 3, 3584, g_mi, a_t=True)
    launch(exchange(swap=("mix_w_out", "up_a", "up_b", "ssm_glu_w")))

    d_abr = da[0].sum(axis=0).reshape(SSM_GROUPS, SSM_STATE)
    d_abi = da[1].sum(axis=0).reshape(SSM_GROUPS, SSM_STATE)
    _, vjp = jax.vjp(_s5_discretise, ws["ssm_lambda_re"], ws["ssm_lambda_im"], ws["ssm_log_dt"],
                     ws["ssm_b_re"], ws["ssm_b_im"])
    (gs["ssm_lambda_re"], gs["ssm_lambda_im"], gs["ssm_log_dt"], gs["ssm_b_re"], gs["ssm_b_im"]) = vjp(
        (d_abr, d_abi, _block_diag_in_t(dmr), _block_diag_in_t(dmi)))
    gs["ssm_c_re"] = _block_diag_out_t(dnr)
    gs["ssm_c_im"] = _block_diag_out_t(dni)
    gs["ssm_d"] = ddsk
    gs["ssm_glu_b"] = dgb
    gs["gmlp_w_s"] = dws
    gs["gmlp_b_s"] = dbias.reshape(CHUNK, GMLP_HEADS, GMLP_HEAD_DIM).sum(axis=-1).T
    gs["loss_rows"] = loss_rows

    def small_gather(names):
        return (_gather_small([gs[k].reshape(small_shape[k]) for k in names]), gathered, names) if dist else None

    late = ("ln1_g", "ln1_b")
    launch(exchange(scat=("mix_w_in",), extra=small_gather(tuple(k for k in SMALL + ("loss_rows",) if k not in late))))
    (dx0, dh1, df1, gs["ln1_g"], gs["ln1_b"]), _ = ordered(
        _ffn_bwd, dx1, xh1, rstd1, h1, wb["ffn1_w_in"], wb["ffn1_w_out"], row(ws["ln1_g"]), tm, "ffn1_bwd")
    grad_x = dx0.reshape(bsz, seq, D_MODEL)
    if not dist:
        gb["ffn1_w_out"], _ = _tn_matmul(a1, df1, "dw_ffn1_out", 1408, 1024)
        gb["ffn1_w_in"], _ = _tn_matmul(x0b, dh1, "dw_ffn1_in", 1024, 1408, a_t=True)
        return (loss_rows, grad_x, gb, {k: gs[k].reshape(small_shape[k]) for k in SMALL}, sums, other, gathered,
                None, {})
    launch(exchange(extra=small_gather(late)))
    gb["ffn1_w_out"], _ = ordered(_tn_matmul, a1, df1, "dw_ffn1_out", 1408, 1024)
    last = ["ffn1_w_out"] + [LAST_PIECE % q for q in range(LAST_PIECES)]
    fillers = (("ffn2_w_in", "mix_w_in", "ple_w_gate"),
               ("ffn2_w_out", "mix_w_out", "up_a", "up_b", "ssm_glu_w", "ple_w_proj"))
    out = {}
    for i in range(1, len(last) + 3):
        stage = lambda d: tuple(last[i - d:i - d + 1]) if 0 <= i - d < len(last) else ()
        launch(exchange(halves=stage(1), scat2=stage(2), swap2=stage(3), swap=("mix_w_in",) if i == 2 else ()))
        if i < len(last):
            gb[last[i]], _ = ordered(_tn_matmul, x0b, dh1, "dw_" + last[i], D_MODEL // LAST_PIECES, 1408,
                                     a_cols=(i - 1, 1), a_t=True)
        elif i - len(last) < len(fillers):
            for k in fillers[i - len(last)]:
                w, m, v = opt[k]
                out[k] = _adam_big(w, sums[k], other[k], m, v, "adam_" + k, after=order[0])
                order[0] = out[k][1]
    return loss_rows, grad_x, gb, gs, sums, other, gathered, ids, out


def _adamw(w, g, m, v):
    m = ADAM_B1 * m + (1.0 - ADAM_B1) * g
    v = ADAM_B2 * v + (1.0 - ADAM_B2) * (g * g)
    m_hat = m / ADAM_C1
    v_hat = v / ADAM_C2
    delta = -ADAM_LR * (m_hat / (jnp.sqrt(v_hat) + ADAM_EPS) + ADAM_WD * w)
    return delta, m, v


def _pinned(after):
    return ([pl.BlockSpec(memory_space=pl.ANY)], [after]) if after is not None else ([], [])


def _sum_blocks(part, recv, shape, axis, chip, name, after=None):
    r, c = shape
    rb = r // 8

    def body(chip_ref, p_ref, r_ref, *rest):
        rest[-1][...] = (p_ref[...] + r_ref[0].astype(F32) + r_ref[1].astype(F32) + r_ref[2].astype(F32))

    if axis == 0:
        own = pl.BlockSpec((rb, c), lambda i, k: (k[0] * 8 + i, 0))
    else:
        own = pl.BlockSpec((rb, c), lambda i, k: (i, k[0]))
    pin_specs, pin_args = _pinned(after)
    grid_spec = pltpu.PrefetchScalarGridSpec(
        num_scalar_prefetch=1, grid=(8,),
        in_specs=[own, pl.BlockSpec((3, rb, c), lambda i, k: (0, i, 0))] + pin_specs,
        out_specs=pl.BlockSpec((rb, c), lambda i, k: (i, 0)))
    return pl.pallas_call(body, name=name, out_shape=SDS((r, c), F32), grid_spec=grid_spec,
                          compiler_params=_params(("parallel",)))(chip, part, recv, *pin_args)


def _presum(part, half, shape, axis, ids, name, after=None):
    r, c = shape
    rb = r // 4

    def body(ids_ref, p_ref, h_ref, *rest):
        of_ref, ob_ref = rest[-2:]
        s = p_ref[...] + h_ref[...].astype(F32)
        ob_ref[...] = s.astype(BF16)

        @pl.when(pl.program_id(1) == 0)
        def _():
            of_ref[...] = s

    if axis == 0:
        p_spec = pl.BlockSpec((rb, c), lambda i, t, ids: (ids[t] * 4 + ids[4] * 2 + i, 0))
        h_spec = pl.BlockSpec((None, rb, c), lambda i, t, ids: (ids[t], i, 0))
    else:
        p_spec = pl.BlockSpec((rb, c), lambda i, t, ids: (ids[4] * 2 + i, ids[t]))
        h_spec = pl.BlockSpec((rb, c), lambda i, t, ids: (i, ids[t]))
    pin_specs, pin_args = _pinned(after)
    grid_spec = pltpu.PrefetchScalarGridSpec(
        num_scalar_prefetch=1, grid=(2, 4), in_specs=[p_spec, h_spec] + pin_specs,
        out_specs=(pl.BlockSpec((rb, c), lambda i, t, ids: (i, 0)),
                   pl.BlockSpec((None, rb, c), lambda i, t, ids: (t, i, 0))))
    return pl.pallas_call(body, name=name, out_shape=(SDS((r // 2, c), F32), SDS((4, r // 2, c), BF16)),
                          grid_spec=grid_spec,
                          compiler_params=_params(("parallel", "arbitrary")))(ids, part, half, *pin_args)


def _sum_half(pre, recv, name, after=None):
    hr, c = pre.shape
    rb = hr // 2

    def body(p_ref, r_ref, *rest):
        rest[-1][...] = (p_ref[...] + r_ref[0].astype(F32) + r_ref[1].astype(F32) + r_ref[2].astype(F32))

    spec = pl.BlockSpec((rb, c), lambda i: (i, 0))
    pin_specs, pin_args = _pinned(after)
    return pl.pallas_call(body, name=name, grid=(2,), out_shape=SDS((hr, c), F32),
                          in_specs=[spec, pl.BlockSpec((3, rb, c), lambda i: (0, i, 0))] + pin_specs,
                          out_specs=spec, compiler_params=_params(("parallel",)))(pre, recv, *pin_args)


def _adam_halves(w, mine, oth, m, v, ids, name, piece=0, prev=None):
    r, c = w.shape
    rb = mine.shape[0] // 2

    def body(ids_ref, w_ref, a_ref, b_ref, m_ref, v_ref, *rest):
        g_ref, d_ref, nm_ref, nv_ref = rest[-4:]
        g = jnp.where(pl.program_id(0) // 2 == ids_ref[4], a_ref[...], b_ref[...])
        g_ref[...] = g
        d_ref[...], nm_ref[...], nv_ref[...] = _adamw(w_ref[...], g, m_ref[...], v_ref[...])

    whole = pl.BlockSpec((rb, c), lambda i, ids: (i + 4 * piece, 0))
    part = pl.BlockSpec((rb, c), lambda i, ids: (i % 2, 0))
    in_specs = [whole, part, part, whole, whole]
    args = [w, mine, oth, m, v]
    aliases = {}
    if prev is not None:
        in_specs += [pl.BlockSpec(memory_space=pl.ANY)] * 4
        args += list(prev)
        aliases = {6: 0, 7: 1, 8: 2, 9: 3}
    grid_spec = pltpu.PrefetchScalarGridSpec(num_scalar_prefetch=1, grid=(4,), in_specs=in_specs,
                                             out_specs=(whole,) * 4)
    return pl.pallas_call(body, name=name, out_shape=tuple(SDS((r, c), F32) for _ in range(4)),
                          grid_spec=grid_spec, input_output_aliases=aliases,
                          compiler_params=_params(("parallel",)))(ids, *args)


def _adam_big(w, ga, gb, m, v, name, piece=0, prev=None, after=None):
    r, c = w.shape
    pr = ga.shape[0]
    steps = 8 if pr == r else 2
    rb = pr // steps
    off = piece * steps

    def body(w_ref, ga_ref, gb_ref, m_ref, v_ref, *rest):
        g_ref, d_ref, nm_ref, nv_ref = rest[-4:]
        g = ga_ref[...] + gb_ref[...]
        g_ref[...] = g
        d_ref[...], nm_ref[...], nv_ref[...] = _adamw(w_ref[...], g, m_ref[...], v_ref[...])

    whole = pl.BlockSpec((rb, c), lambda i: (i + off, 0))
    part = pl.BlockSpec((rb, c), lambda i: (i, 0))
    in_specs = [whole, part, part, whole, whole]
    args = [w, ga, gb, m, v]
    aliases = {}
    if prev is not None:
        in_specs += [pl.BlockSpec(memory_space=pl.ANY)] * 4
        args += list(prev)
        aliases = {5: 0, 6: 1, 7: 2, 8: 3}
    if after is not None:
        in_specs.append(pl.BlockSpec(memory_space=pl.ANY))
        args.append(after)
    return pl.pallas_call(
        body, name=name, grid=(steps,), out_shape=tuple(SDS((r, c), F32) for _ in range(4)),
        in_specs=in_specs, out_specs=(whole,) * 4, input_output_aliases=aliases,
        compiler_params=_params(("parallel",)),
    )(*args)


def _adam_small(ws, gathered, ms, vs):
    n = len(ws)

    def body(*refs):
        w_refs, g_refs, m_refs, v_refs = refs[:n], refs[n:2 * n], refs[2 * n:3 * n], refs[3 * n:4 * n]
        outs = refs[4 * n:]
        for i in range(n):
            g = g_refs[i][0]
            for d in range(1, N_DEV):
                g = g + g_refs[i][d]
            delta, nm, nv = _adamw(w_refs[i][...], g, m_refs[i][...], v_refs[i][...])
            outs[i][...] = g
            outs[n + i][...] = delta
            outs[2 * n + i][...] = nm
            outs[3 * n + i][...] = nv

    vmem = pl.BlockSpec(memory_space=pltpu.VMEM)
    shapes = [w.shape for w in ws]
    return pl.pallas_call(
        body, name="adam_small", out_shape=tuple(SDS(s, F32) for s in shapes * 4),
        in_specs=[vmem] * (4 * n), out_specs=tuple([vmem] * (4 * n)),
        compiler_params=pltpu.CompilerParams(vmem_limit_bytes=VMEM_LIMIT_BYTES),
    )(*ws, *gathered, *ms, *vs)


def _sum_loss(gathered):
    def body(g_ref, o_ref):
        tot = g_ref[0]
        for d in range(1, N_DEV):
            tot = tot + g_ref[d]
        o_ref[...] = (0.5 / D_MODEL) * jnp.sum(tot, axis=1, keepdims=True)

    vmem = pl.BlockSpec(memory_space=pltpu.VMEM)
    return pl.pallas_call(body, name="sum_loss", out_shape=SDS((1, 1), F32), in_specs=[vmem],
                          out_specs=vmem)(gathered)


def kernel(x, p, ffn1_w_in, ffn1_w_out, ln1_g, ln1_b, mix_w_in, ssm_lambda_re, ssm_lambda_im, ssm_log_dt, ssm_b_re, ssm_b_im, ssm_c_re, ssm_c_im, ssm_d, ssm_glu_w, ssm_glu_b, gmlp_ln_g, gmlp_ln_b, gmlp_w_s, gmlp_b_s, up_a, up_b, mix_w_out, ln2_g, ln2_b, ffn2_w_in, ffn2_w_out, ln3_g, ln3_b, ple_w_proj, ple_w_gate, loss_target, m_ffn1_w_in, m_ffn1_w_out, m_ln1_g, m_ln1_b, m_mix_w_in, m_ssm_lambda_re, m_ssm_lambda_im, m_ssm_log_dt, m_ssm_b_re, m_ssm_b_im, m_ssm_c_re, m_ssm_c_im, m_ssm_d, m_ssm_glu_w, m_ssm_glu_b, m_gmlp_ln_g, m_gmlp_ln_b, m_gmlp_w_s, m_gmlp_b_s, m_up_a, m_up_b, m_mix_w_out, m_ln2_g, m_ln2_b, m_ffn2_w_in, m_ffn2_w_out, m_ln3_g, m_ln3_b, m_ple_w_proj, m_ple_w_gate, v_ffn1_w_in, v_ffn1_w_out, v_ln1_g, v_ln1_b, v_mix_w_in, v_ssm_lambda_re, v_ssm_lambda_im, v_ssm_log_dt, v_ssm_b_re, v_ssm_b_im, v_ssm_c_re, v_ssm_c_im, v_ssm_d, v_ssm_glu_w, v_ssm_glu_b, v_gmlp_ln_g, v_gmlp_ln_b, v_gmlp_w_s, v_gmlp_b_s, v_up_a, v_up_b, v_mix_w_out, v_ln2_g, v_ln2_b, v_ffn2_w_in, v_ffn2_w_out, v_ln3_g, v_ln3_b, v_ple_w_proj, v_ple_w_gate):
    given = dict(locals())
    order = ("ffn1_w_in", "ffn1_w_out", "ln1_g", "ln1_b", "mix_w_in", "ssm_lambda_re", "ssm_lambda_im",
             "ssm_log_dt", "ssm_b_re", "ssm_b_im", "ssm_c_re", "ssm_c_im", "ssm_d", "ssm_glu_w", "ssm_glu_b",
             "gmlp_ln_g", "gmlp_ln_b", "gmlp_w_s", "gmlp_b_s", "up_a", "up_b", "mix_w_out", "ln2_g", "ln2_b",
             "ffn2_w_in", "ffn2_w_out", "ln3_g", "ln3_b", "ple_w_proj", "ple_w_gate")
    assert set(order) == set(BIG + SMALL)

    shard = {k: given[k][0] for k in BIG}
    shard_b = {k: shard[k].astype(BF16) for k in BIG}
    opt = {k: (shard[k], given["m_" + k][0], given["v_" + k][0]) for k in BIG}
    loss_rows, grad_x, gb, gs, sums, other, gathered, ids, out = _local_step(
        x, given["p"][0], loss_target, {}, {k: given[k] for k in SMALL}, shard_b, opt)

    out = dict(out)
    for k in BIG:
        if k in out:
            continue
        moments = (given["m_" + k][0], given["v_" + k][0])
        if k == "ffn1_w_out":
            out[k] = _adam_halves(shard[k], sums[k], other[k], *moments, ids, "adam_" + k)
        elif k == "ffn1_w_in":
            for q in range(LAST_PIECES):
                kq = LAST_PIECE % q
                out[k] = _adam_halves(shard[k], sums[kq], other[kq], *moments, ids, "adam_" + kq, q, out.get(k))
        else:
            out[k] = _adam_big(shard[k], sums[k], other[k], *moments, "adam_" + k,
                               after=gb[LAST_PIECE % (LAST_PIECES - 1)][0])

    res = _adam_small([_small_view(k, given[k]) for k in SMALL], [gathered[k] for k in SMALL],
                      [_small_view(k, given["m_" + k]) for k in SMALL],
                      [_small_view(k, given["v_" + k]) for k in SMALL])
    ns = len(SMALL)
    for i, k in enumerate(SMALL):
        out[k] = tuple(res[j * ns + i].reshape(given[k].shape) for j in range(4))
    loss = _sum_loss(gathered["loss_rows"]).reshape(())

    lead = lambda k, j: out[k][j][None] if k in BIG else out[k][j]
    return (loss, grad_x, *[lead(k, 0) for k in order], *[lead(k, 1) for k in order],
            *[lead(k, 2) for k in order], *[lead(k, 3) for k in order])
```

```python
import math

import jax
import jax.numpy as jnp
from jax import lax
from jax.experimental import pallas as pl
from jax.experimental.pallas import tpu as pltpu
from jax.experimental.pallas import tpu_sc as plsc

F32 = jnp.float32
BF16 = jnp.bfloat16
MESH = pl.DeviceIdType.MESH
SDS = jax.ShapeDtypeStruct

D_MODEL = 1024
D_FF = 2816
D_SSM = 512
D_GMLP = 512
SSM_GROUPS = 32
SSM_GROUP_CH = 16
SSM_STATE = 64
SSM_LANES = SSM_GROUPS * SSM_STATE
GMLP_HEADS = 8
GMLP_HEAD_DIM = 64
CHUNK = 128
PLE_DIM = 256
LN_EPS = 1e-5
ALPHA = 2.0 ** 0.25

ADAM_LR = 0.001
ADAM_B1 = 0.9
ADAM_B2 = 0.999
ADAM_EPS = 1e-08
ADAM_WD = 0.01
ADAM_STEP = 10
ADAM_C1 = 1.0 - ADAM_B1 ** ADAM_STEP
ADAM_C2 = 1.0 - ADAM_B2 ** ADAM_STEP

N_DEV = 8
VMEM_LIMIT_BYTES = 56 * 1024 * 1024
FFN_COLS = 1408
S5_BLOCKS = 4
S5_BLOCK_IN = D_SSM // S5_BLOCKS
S5_BLOCK_ST = SSM_LANES // S5_BLOCKS
SCAN_LANES = 512
TN_K_BLOCK = 2048
DIRECT_GATHER_BYTES = 0
LAST_PIECES = 2
LAST_PIECE = "ffn1_w_in_q%d"
_G0 = math.sqrt(2.0 / math.pi)
_G1 = 0.044715


def _dot(a, b):
    return jnp.dot(a, b, preferred_element_type=F32)


def _dot_nt(a, b):
    return lax.dot_general(a, b, (((1,), (1,)), ((), ())), preferred_element_type=F32)


def _dot_tn(a, b):
    return lax.dot_general(a, b, (((0,), (0,)), ((), ())), preferred_element_type=F32)


def _sigmoid(x):
    return 1.0 / (1.0 + jnp.exp(-x))


def _gelu(x):
    t = jnp.tanh(_G0 * (x + _G1 * x * x * x))
    return 0.5 * x * (1.0 + t)


def _gelu_grad(x):
    t = jnp.tanh(_G0 * (x + _G1 * x * x * x))
    return 0.5 * (1.0 + t) + 0.5 * x * (1.0 - t * t) * _G0 * (1.0 + 3.0 * _G1 * x * x)


def _ln_fwd(r, g, b):
    mu = jnp.mean(r, axis=-1, keepdims=True)
    d = r - mu
    var = jnp.mean(d * d, axis=-1, keepdims=True)
    rstd = lax.rsqrt(var + LN_EPS)
    xh = d * rstd
    return xh * g + b, xh, rstd


def _ln_bwd(dy, xh, rstd, g):
    dxh = dy * g
    m1 = jnp.mean(dxh, axis=-1, keepdims=True)
    m2 = jnp.mean(dxh * xh, axis=-1, keepdims=True)
    return rstd * (dxh - m1 - xh * m2)


def _resident(shape):
    nd = len(shape)
    return pl.BlockSpec(shape, lambda *_: (0,) * nd, pipeline_mode=pl.Buffered(1))


def _fixed(shape):
    nd = len(shape)
    return pl.BlockSpec(shape, lambda *_: (0,) * nd)


def _rows(tm, cols):
    return pl.BlockSpec((tm, cols), lambda i: (i, 0))


def _cols(rows, tm):
    return pl.BlockSpec((rows, tm), lambda i: (0, i))


def _params(sem):
    return pltpu.CompilerParams(dimension_semantics=sem, vmem_limit_bytes=VMEM_LIMIT_BYTES)


class _Exchange:
    def __init__(self, args, out_shape, sems, start, finish):
        self.args, self.out_shape, self.sems = list(args), list(out_shape), list(sems)
        self.start, self.finish = start, finish
        self.cuts = [(0, len(self.out_shape))]


def _call(body, name, grid, in_specs, out_specs, out_shape, args, scratch=(), sem=None, bg=None, aliases=None):
    aliases = {} if aliases is None else aliases
    in_specs, args = list(in_specs), list(args)
    fn = body
    if bg is not None:
        n_args = len(args)

        def fn(*refs):
            body(*refs[:n_args], *refs[n_args + 1:])

        in_specs.append(pl.BlockSpec(memory_space=pl.ANY))
        args.append(bg)
    res = pl.pallas_call(fn, name=name, grid=grid, out_shape=tuple(out_shape), in_specs=in_specs,
                         out_specs=tuple(out_specs), scratch_shapes=list(scratch),
                         input_output_aliases=aliases, compiler_params=_params(sem))(*args)
    return tuple(res), ()


def _run_exchange_on_sequencer(ex, name, collective_id):
    n_i, n_o = len(ex.args), len(ex.out_shape)

    def body(*refs):
        ins, outs, sems = refs[:n_i], refs[n_i:n_i + n_o], refs[n_i + n_o:]
        x, y, c = lax.axis_index("x"), lax.axis_index("y"), lax.axis_index("c")
        barrier = pltpu.get_barrier_semaphore()
        for peer in [(x, y, 1 - c), (1 - x, y, c), (x, 1 - y, c), (1 - x, 1 - y, c)]:
            pl.semaphore_signal(barrier, inc=1, device_id=peer, device_id_type=MESH)
        pl.semaphore_wait(barrier, 4)
        ex.start(ins, outs, sems)
        ex.finish(ins, outs, sems)

    return tuple(pl.kernel(body, out_type=tuple(ex.out_shape),
                           mesh=plsc.ScalarSubcoreMesh(axis_name="sequencer", num_cores=1),
                           scratch_types=list(ex.sems), name=name,
                           compiler_params=pltpu.CompilerParams(collective_id=collective_id))(*ex.args))


def _join(exchanges):
    cuts = []
    a = o = q = 0
    for e in exchanges:
        cuts.append((a, a + len(e.args), o, o + len(e.out_shape), q, q + len(e.sems)))
        a, o, q = cuts[-1][1], cuts[-1][3], cuts[-1][5]

    def start(ins, outs, sems):
        for e, (a0, a1, o0, o1, q0, q1) in zip(exchanges, cuts):
            e.start(ins[a0:a1], outs[o0:o1], sems[q0:q1])

    def finish(ins, outs, sems):
        for e, (a0, a1, o0, o1, q0, q1) in zip(exchanges, cuts):
            e.finish(ins[a0:a1], outs[o0:o1], sems[q0:q1])

    joined = _Exchange(sum((e.args for e in exchanges), []), sum((e.out_shape for e in exchanges), []),
                       sum((e.sems for e in exchanges), []), start, finish)
    joined.cuts = [(c[2], c[3]) for c in cuts]
    return joined


def _ffn_proj(x, w_in, tm, name, bg=None):
    t = x.shape[0]
    nch = D_FF // FFN_COLS

    def body(x_ref, win_ref, xbt_ref, h_ref, a_ref):
        xb = x_ref[...].astype(BF16)
        xbt_ref[...] = xb.T
        for k in range(nch):
            cg = slice(k * FFN_COLS, (k + 1) * FFN_COLS)
            cu = slice(D_FF + k * FFN_COLS, D_FF + (k + 1) * FFN_COLS)
            hg = _dot(xb, win_ref[k])
            hu = _dot(xb, win_ref[nch + k])
            h_ref[:, cg] = hg.astype(BF16)
            h_ref[:, cu] = hu.astype(BF16)
            a_ref[:, cg] = (hg * _sigmoid(hg) * hu).astype(BF16)

    return _call(
        body, name, (t // tm,),
        [_rows(tm, D_MODEL), _resident((2 * nch, D_MODEL, FFN_COLS))],
        (_cols(D_MODEL, tm), _rows(tm, 2 * D_FF), _rows(tm, D_FF)),
        (SDS((D_MODEL, t), BF16), SDS((t, 2 * D_FF), BF16), SDS((t, D_FF), BF16)),
        (x, w_in), sem=("parallel",), bg=bg)


def _ffn_out(x, a, w_out, g, b, tm, name, bg=None):
    t = x.shape[0]

    def body(x_ref, a_ref, wout_ref, g_ref, b_ref, xn_ref, xh_ref, rstd_ref):
        f = _dot(a_ref[...], wout_ref[...])
        y, xh, rstd = _ln_fwd(ALPHA * x_ref[...] + 0.5 * f, g_ref[...], b_ref[...])
        xn_ref[...] = y
        xh_ref[...] = xh
        rstd_ref[...] = rstd

    return _call(
        body, name, (t // tm,),
        [_rows(tm, D_MODEL), _rows(tm, D_FF), _resident((D_FF, D_MODEL)), _fixed((1, D_MODEL)), _fixed((1, D_MODEL))],
        (_rows(tm, D_MODEL), _rows(tm, D_MODEL), _rows(tm, 1)),
        (SDS((t, D_MODEL), F32), SDS((t, D_MODEL), F32), SDS((t, 1), F32)),
        (x, a, w_out, g, b), sem=("parallel",), bg=bg)


def _ffn_bwd(dxn, xh, rstd, h, w_in, w_out, g, tm, name, bg=None):
    t = dxn.shape[0]
    nch = D_FF // FFN_COLS

    def body(dxn_ref, xh_ref, rstd_ref, h_ref, win_ref, wout_ref, g_ref,
             dx_ref, dh_ref, df_ref, dg_ref, db_ref):
        @pl.when(pl.program_id(0) == 0)
        def _():
            dg_ref[...] = jnp.zeros_like(dg_ref)
            db_ref[...] = jnp.zeros_like(db_ref)

        dy = dxn_ref[...]
        xhv = xh_ref[...]
        dr = _ln_bwd(dy, xhv, rstd_ref[...], g_ref[...])
        dg_ref[...] += jnp.sum(dy * xhv, axis=0, keepdims=True)
        db_ref[...] += jnp.sum(dy, axis=0, keepdims=True)
        df = (0.5 * dr).astype(BF16)
        df_ref[...] = df
        dx = ALPHA * dr
        for k in range(nch):
            cg = slice(k * FFN_COLS, (k + 1) * FFN_COLS)
            cu = slice(D_FF + k * FFN_COLS, D_FF + (k + 1) * FFN_COLS)
            hg = h_ref[:, cg].astype(F32)
            hu = h_ref[:, cu].astype(F32)
            sg = _sigmoid(hg)
            silu = hg * sg
            da = _dot_nt(df, wout_ref[cg, :])
            dhu = (da * silu).astype(BF16)
            dhg = (da * hu * (sg * (1.0 + hg * (1.0 - sg)))).astype(BF16)
            dh_ref[:, cg] = dhg
            dh_ref[:, cu] = dhu
            dx = dx + _dot_nt(dhg, win_ref[k]) + _dot_nt(dhu, win_ref[nch + k])
        dx_ref[...] = dx

    return _call(
        body, name, (t // tm,),
        [_rows(tm, D_MODEL), _rows(tm, D_MODEL), _rows(tm, 1), _rows(tm, 2 * D_FF),
         _resident((2 * nch, D_MODEL, FFN_COLS)), _resident((D_FF, D_MODEL)), _fixed((1, D_MODEL))],
        (_rows(tm, D_MODEL), _rows(tm, 2 * D_FF), _rows(tm, D_MODEL),
         _fixed((1, D_MODEL)), _fixed((1, D_MODEL))),
        (SDS((t, D_MODEL), F32), SDS((t, 2 * D_FF), BF16), SDS((t, D_MODEL), BF16),
         SDS((1, D_MODEL), F32), SDS((1, D_MODEL), F32)),
        (dxn, xh, rstd, h, w_in, w_out, g), sem=("arbitrary",), bg=bg)


def _tn_matmul(a, b, name, bm, bn, col_block=0, total_cols=None, prev=None, bg=None, a_cols=None, a_t=False):
    t, m = a.shape[::-1] if a_t else a.shape
    a_first = 0
    if a_cols is not None:
        a_first, m = a_cols[0], a_cols[1] * bm
    n = b.shape[1]
    total_cols = n if total_cols is None else total_cols
    bk = min(TN_K_BLOCK, t)
    nk = t // bk
    n_in = 2 if prev is None else 4

    def body(*refs):
        a_ref, b_ref = refs[0], refs[1]
        o_ref, ob_ref = refs[n_in], refs[n_in + 1]
        k = pl.program_id(2)

        @pl.when(k == 0)
        def _():
            o_ref[...] = jnp.zeros_like(o_ref)

        o_ref[...] += _dot(a_ref[...], b_ref[...]) if a_t else _dot_tn(a_ref[...], b_ref[...])

        @pl.when(k == nk - 1)
        def _():
            ob_ref[...] = o_ref[...].astype(BF16)

    a_spec = (pl.BlockSpec((bm, bk), lambda i, j, k: (i + a_first, k)) if a_t
              else pl.BlockSpec((bk, bm), lambda i, j, k: (k, i + a_first)))
    in_specs = [a_spec, pl.BlockSpec((bk, bn), lambda i, j, k: (k, j))]
    args = [a, b]
    aliases = {}
    if prev is not None:
        in_specs += [pl.BlockSpec(memory_space=pl.ANY), pl.BlockSpec(memory_space=pl.ANY)]
        args += list(prev)
        aliases = {2: 0, 3: 1}
        if any(bg is p for p in prev):
            bg = None
    out_spec = pl.BlockSpec((bm, bn), lambda i, j, k: (i, j + col_block))
    return _call(body, name, (m // bm, n // bn, nk), in_specs, (out_spec, out_spec),
                 (SDS((m, total_cols), F32), SDS((m, total_cols), BF16)), args,
                 sem=("parallel", "parallel", "arbitrary"), bg=bg, aliases=aliases)


def _mixin_fwd(x1, w, tm, bg=None):
    t = x1.shape[0]

    def body(x_ref, w_ref, xbt_ref, za_ref, zuv_ref, gab_ref):
        xb = x_ref[...].astype(BF16)
        xbt_ref[...] = xb.T
        za_ref[...] = _dot(xb, w_ref[:, 0:512]).astype(BF16)
        zuv_ref[...] = _dot(xb, w_ref[:, 512:1536]).astype(BF16)
        gab_ref[...] = _dot(xb, w_ref[:, 1536:3584]).astype(BF16)

    return _call(
        body, "mixin_fwd", (t // tm,),
        [_rows(tm, D_MODEL), _resident((D_MODEL, 3584))],
        (_cols(D_MODEL, tm), _rows(tm, 512), _rows(tm, 1024), _rows(tm, 2048)),
        (SDS((D_MODEL, t), BF16), SDS((t, 512), BF16), SDS((t, 1024), BF16), SDS((t, 2048), BF16)),
        (x1, w), sem=("parallel",), bg=bg)


def _mixin_bwd(dx1a, dza, dzuv, dgab, w, tm, bg=None):
    t = dx1a.shape[0]

    def body(d_ref, dza_ref, dzuv_ref, dgab_ref, w_ref, dx_ref):
        dx_ref[...] = (d_ref[...] + _dot_nt(dza_ref[...], w_ref[:, 0:512])
                       + _dot_nt(dzuv_ref[...], w_ref[:, 512:1536])
                       + _dot_nt(dgab_ref[...], w_ref[:, 1536:3584]))

    return _call(
        body, "mixin_bwd", (t // tm,),
        [_rows(tm, D_MODEL), _rows(tm, 512), _rows(tm, 1024), _rows(tm, 2048), _resident((D_MODEL, 3584))],
        (_rows(tm, D_MODEL),), (SDS((t, D_MODEL), F32),),
        (dx1a, dza, dzuv, dgab, w), sem=("parallel",), bg=bg)


def _unrolled(lo, hi, body, carry):
    for j in range(lo, hi):
        carry = body(j, carry)
    return carry


def _scan_fwd(hr_ref, hi_ref, a_ref, ap_ref, carry_ref, seg, cin_ref):
    for lc in range(SSM_LANES // SCAN_LANES):
        ls = slice(lc * SCAN_LANES, (lc + 1) * SCAN_LANES)
        a_r = jnp.broadcast_to(a_ref[0:1, ls], (8, SCAN_LANES))
        a_i = jnp.broadcast_to(a_ref[1:2, ls], (8, SCAN_LANES))

        def step(j, hc, ls=ls, a_r=a_r, a_i=a_i):
            h_r, h_i = hc
            rows = pl.ds(j * 8, 8)
            n_r = a_r * h_r - a_i * h_i + hr_ref[rows, ls]
            n_i = a_r * h_i + a_i * h_r + hi_ref[rows, ls]
            hr_ref[rows, ls] = n_r
            hi_ref[rows, ls] = n_i
            return n_r, n_i

        zero = jnp.zeros((8, SCAN_LANES), F32)
        f_r, f_i = _unrolled(0, seg, step, (zero, zero))
        c_r = carry_ref[0:1, ls]
        c_i = carry_ref[1:2, ls]
        p_r = ap_ref[0:1, ls]
        p_i = ap_ref[1:2, ls]
        rows_r, rows_i = [], []
        for s in range(8):
            rows_r.append(c_r)
            rows_i.append(c_i)
            c_r, c_i = (f_r[s:s + 1] + p_r * c_r - p_i * c_i,
                        f_i[s:s + 1] + p_r * c_i + p_i * c_r)
        carry_ref[0:1, ls] = c_r
        carry_ref[1:2, ls] = c_i
        cin_r = jnp.concatenate(rows_r, axis=0)
        cin_i = jnp.concatenate(rows_i, axis=0)
        if cin_ref is not None:
            cin_ref[0, :, ls] = cin_r
            cin_ref[1, :, ls] = cin_i

        def fix(j, cc, ls=ls, a_r=a_r, a_i=a_i):
            c_r, c_i = cc
            c_r, c_i = a_r * c_r - a_i * c_i, a_r * c_i + a_i * c_r
            rows = pl.ds(j * 8, 8)
            hr_ref[rows, ls] = hr_ref[rows, ls] + c_r
            hi_ref[rows, ls] = hi_ref[rows, ls] + c_i
            return c_r, c_i

        _unrolled(0, seg, fix, (cin_r, cin_i))


def _scan_bwd(gr_ref, gi_ref, hr_ref, hi_ref, cin_ref, a_ref, ap_ref, rcarry_ref, da_ref, seg):
    for lc in range(SSM_LANES // SCAN_LANES):
        ls = slice(lc * SCAN_LANES, (lc + 1) * SCAN_LANES)
        a_r = jnp.broadcast_to(a_ref[0:1, ls], (8, SCAN_LANES))
        a_i = jnp.broadcast_to(a_ref[1:2, ls], (8, SCAN_LANES))

        def step(t, gc, ls=ls, a_r=a_r, a_i=a_i):
            g_r, g_i = gc
            rows = pl.ds((seg - 1 - t) * 8, 8)
            n_r = gr_ref[rows, ls] + a_r * g_r + a_i * g_i
            n_i = gi_ref[rows, ls] + a_r * g_i - a_i * g_r
            gr_ref[rows, ls] = n_r
            gi_ref[rows, ls] = n_i
            return n_r, n_i

        zero = jnp.zeros((8, SCAN_LANES), F32)
        f_r, f_i = _unrolled(0, seg, step, (zero, zero))
        c_r = rcarry_ref[0:1, ls]
        c_i = rcarry_ref[1:2, ls]
        p_r = ap_ref[0:1, ls]
        p_i = ap_ref[1:2, ls]
        rows_r, rows_i = [None] * 8, [None] * 8
        for s in range(7, -1, -1):
            rows_r[s] = c_r
            rows_i[s] = c_i
            c_r, c_i = (f_r[s:s + 1] + p_r * c_r + p_i * c_i,
                        f_i[s:s + 1] + p_r * c_i - p_i * c_r)
        rcarry_ref[0:1, ls] = c_r
        rcarry_ref[1:2, ls] = c_i
        cin_r = jnp.concatenate(rows_r, axis=0)
        cin_i = jnp.concatenate(rows_i, axis=0)

        def fix_row(j_rows, hp_r, hp_i, cc, ls=ls, a_r=a_r, a_i=a_i):
            c_r, c_i, acc_r, acc_i = cc
            c_r, c_i = a_r * c_r + a_i * c_i, a_r * c_i - a_i * c_r
            g_r = gr_ref[j_rows, ls] + c_r
            g_i = gi_ref[j_rows, ls] + c_i
            gr_ref[j_rows, ls] = g_r
            gi_ref[j_rows, ls] = g_i
            acc_r = acc_r + g_r * hp_r + g_i * hp_i
            acc_i = acc_i + g_i * hp_r - g_r * hp_i
            return c_r, c_i, acc_r, acc_i

        def fix(t, cc, ls=ls, fix_row=fix_row):
            j = seg - 1 - t
            rows = pl.ds(j * 8, 8)
            prev = pl.ds((j - 1) * 8, 8)
            return fix_row(rows, hr_ref[prev, ls], hi_ref[prev, ls], cc)

        cc = _unrolled(0, seg - 1, fix, (cin_r, cin_i, zero, zero))
        _, _, acc_r, acc_i = fix_row(pl.ds(0, 8), cin_ref[0, :, ls], cin_ref[1, :, ls], cc)
        da_ref[0, :, ls] += acc_r
        da_ref[1, :, ls] += acc_i


def _s5_fwd(za, sp, bsz, seq, tb, bg=None):
    nb = seq // tb
    seg = tb // 8
    t = bsz * seq

    def body(za_ref, perm_ref, permt_ref, mre_ref, mim_ref, nre_ref, nim_ref, a_ref, ap_ref,
             dsk_ref, gw_ref, gb_ref, out_ref, outt_ref, y2_ref, car_ref, hr_ref, hi_ref, carry_ref):
        @pl.when(pl.program_id(1) == 0)
        def _():
            carry_ref[...] = jnp.zeros_like(carry_ref)

        car_ref[0] = carry_ref[...]
        up = _dot(perm_ref[...], za_ref[...])
        upb = up.astype(BF16)
        for bb in range(S5_BLOCKS):
            ub = upb[:, bb * S5_BLOCK_IN:(bb + 1) * S5_BLOCK_IN]
            st = slice(bb * S5_BLOCK_ST, (bb + 1) * S5_BLOCK_ST)
            hr_ref[:, st] = _dot(ub, mre_ref[bb])
            hi_ref[:, st] = _dot(ub, mim_ref[bb])
        _scan_fwd(hr_ref, hi_ref, a_ref, ap_ref, carry_ref, seg, None)
        ys = []
        for bb in range(S5_BLOCKS):
            st = slice(bb * S5_BLOCK_ST, (bb + 1) * S5_BLOCK_ST)
            ys.append(_dot(hr_ref[:, st].astype(BF16), nre_ref[bb])
                      - _dot(hi_ref[:, st].astype(BF16), nim_ref[bb]))
        y2 = jnp.concatenate(ys, axis=1) + dsk_ref[...] * up
        y2_ref[...] = y2
        y3 = _gelu(y2)
        gl = _dot(y3.astype(BF16), gw_ref[...]) + gb_ref[...]
        oa = y3 * _sigmoid(gl)
        out = _dot(permt_ref[...], oa.astype(BF16)).astype(BF16)
        out_ref[...] = out
        outt_ref[...] = out.T

    blk = pl.BlockSpec((tb, D_SSM), lambda b, j: (b * nb + j, 0))
    blk_t = pl.BlockSpec((D_SSM, tb), lambda b, j: (0, b * nb + j))
    m_shape = (S5_BLOCKS, S5_BLOCK_IN, S5_BLOCK_ST)
    n_shape = (S5_BLOCKS, S5_BLOCK_ST, S5_BLOCK_IN)
    return _call(
        body, "s5_fwd", (bsz, nb),
        [blk, _fixed((tb, tb)), _fixed((tb, tb)), _fixed(m_shape), _fixed(m_shape), _fixed(n_shape),
         _fixed(n_shape), _fixed((2, SSM_LANES)), _fixed((2, SSM_LANES)), _fixed((1, D_SSM)),
         _fixed((D_SSM, D_SSM)), _fixed((1, D_SSM))],
        (blk, blk_t, blk, pl.BlockSpec((1, 2, SSM_LANES), lambda b, j: (b * nb + j, 0, 0))),
        (SDS((t, D_SSM), BF16), SDS((D_SSM, t), BF16), SDS((t, D_SSM), F32), SDS((bsz * nb, 2, SSM_LANES), F32)),
        (za, sp["perm"], sp["permt"], sp["mre"], sp["mim"], sp["nre"], sp["nim"], sp["a"], sp["ap"],
         sp["dskip"], sp["glu_w"], sp["glu_b"]),
        scratch=[pltpu.VMEM((tb, SSM_LANES), F32), pltpu.VMEM((tb, SSM_LANES), F32),
                 pltpu.VMEM((2, SSM_LANES), F32)],
        sem=("arbitrary", "arbitrary"), bg=bg)


def _s5_bwd(za, y2p, doa, carries, sp, bsz, seq, tb, bg=None):
    nb = seq // tb
    seg = tb // 8
    t = bsz * seq

    def body(za_ref, y2_ref, doa_ref, car_ref, perm_ref, permt_ref, mre_ref, mim_ref, mtre_ref, mtim_ref,
             nre_ref, nim_ref, ntre_ref, ntim_ref, a_ref, ap_ref, dsk_ref, gw_ref, gwt_ref, gb_ref,
             dza_ref, dmr_ref, dmi_ref, dnr_ref, dni_ref, da_ref, ddsk_ref, dgw_ref, dgb_ref,
             hr_ref, hi_ref, gr_ref, gi_ref, cin_ref, carry_ref, rcarry_ref):
        first = jnp.logical_and(pl.program_id(0) == 0, pl.program_id(1) == 0)

        @pl.when(first)
        def _():
            for r in (dmr_ref, dmi_ref, dnr_ref, dni_ref, da_ref, ddsk_ref, dgw_ref, dgb_ref):
                r[...] = jnp.zeros_like(r)

        @pl.when(pl.program_id(1) == 0)
        def _():
            rcarry_ref[...] = jnp.zeros_like(rcarry_ref)

        carry_ref[...] = car_ref[0]
        perm = perm_ref[...]
        up = _dot(perm, za_ref[...])
        upb = up.astype(BF16)
        for bb in range(S5_BLOCKS):
            ub = upb[:, bb * S5_BLOCK_IN:(bb + 1) * S5_BLOCK_IN]
            st = slice(bb * S5_BLOCK_ST, (bb + 1) * S5_BLOCK_ST)
            hr_ref[:, st] = _dot(ub, mre_ref[bb])
            hi_ref[:, st] = _dot(ub, mim_ref[bb])
        _scan_fwd(hr_ref, hi_ref, a_ref, ap_ref, carry_ref, seg, cin_ref)

        y2 = y2_ref[...]
        y3 = _gelu(y2)
        y3b = y3.astype(BF16)
        sg = _sigmoid(_dot(y3b, gw_ref[...]) + gb_ref[...])
        d0 = doa_ref[...]
        d_hi = d0.astype(BF16)
        d1 = d0 - d_hi.astype(F32)
        d_mid = d1.astype(BF16)
        d_lo = (d1 - d_mid.astype(F32)).astype(BF16)
        doap = _dot(perm, d_hi) + _dot(perm, d_mid) + _dot(perm, d_lo)
        dgl = doap * y3 * sg * (1.0 - sg)
        dglb = dgl.astype(BF16)
        dy3 = doap * sg + _dot(dglb, gwt_ref[...])
        dgw_ref[...] += _dot_tn(y3b, dglb)
        dgb_ref[...] += jnp.sum(dgl, axis=0, keepdims=True)
        dy2 = dy3 * _gelu_grad(y2)
        ddsk_ref[...] += jnp.sum(dy2 * up, axis=0, keepdims=True)
        dyb = dy2.astype(BF16)
        for bb in range(S5_BLOCKS):
            dyc = dyb[:, bb * S5_BLOCK_IN:(bb + 1) * S5_BLOCK_IN]
            st = slice(bb * S5_BLOCK_ST, (bb + 1) * S5_BLOCK_ST)
            gr_ref[:, st] = _dot(dyc, ntre_ref[bb])
            gi_ref[:, st] = -_dot(dyc, ntim_ref[bb])
            dnr_ref[bb] += _dot_tn(hr_ref[:, st].astype(BF16), dyc)
            dni_ref[bb] += -_dot_tn(hi_ref[:, st].astype(BF16), dyc)
        _scan_bwd(gr_ref, gi_ref, hr_ref, hi_ref, cin_ref, a_ref, ap_ref, rcarry_ref, da_ref, seg)
        dus = []
        for bb in range(S5_BLOCKS):
            st = slice(bb * S5_BLOCK_ST, (bb + 1) * S5_BLOCK_ST)
            grb = gr_ref[:, st].astype(BF16)
            gib = gi_ref[:, st].astype(BF16)
            dus.append(_dot(grb, mtre_ref[bb]) + _dot(gib, mtim_ref[bb]))
            ub = upb[:, bb * S5_BLOCK_IN:(bb + 1) * S5_BLOCK_IN]
            dmr_ref[bb] += _dot_tn(ub, grb)
            dmi_ref[bb] += _dot_tn(ub, gib)
        du = jnp.concatenate(dus, axis=1) + dy2 * dsk_ref[...]
        dza_ref[...] = _dot(permt_ref[...], du.astype(BF16)).astype(BF16)

    def rev(b, j):
        return (b * nb + (nb - 1 - j), 0)

    blk = pl.BlockSpec((tb, D_SSM), rev)
    m_shape = (S5_BLOCKS, S5_BLOCK_IN, S5_BLOCK_ST)
    n_shape = (S5_BLOCKS, S5_BLOCK_ST, S5_BLOCK_IN)
    return _call(
        body, "s5_bwd", (bsz, nb),
        [blk, blk, blk, pl.BlockSpec((1, 2, SSM_LANES), lambda b, j: (b * nb + (nb - 1 - j), 0, 0)),
         _fixed((tb, tb)), _fixed((tb, tb)), _fixed(m_shape), _fixed(m_shape), _fixed(n_shape), _fixed(n_shape),
         _fixed(n_shape), _fixed(n_shape), _fixed(m_shape), _fixed(m_shape),
         _fixed((2, SSM_LANES)), _fixed((2, SSM_LANES)), _fixed((1, D_SSM)),
         _fixed((D_SSM, D_SSM)), _fixed((D_SSM, D_SSM)), _fixed((1, D_SSM))],
        (blk, _fixed(m_shape), _fixed(m_shape), _fixed(n_shape), _fixed(n_shape),
         _fixed((2, 8, SSM_LANES)), _fixed((1, D_SSM)), _fixed((D_SSM, D_SSM)), _fixed((1, D_SSM))),
        (SDS((t, D_SSM), BF16), SDS(m_shape, F32), SDS(m_shape, F32), SDS(n_shape, F32), SDS(n_shape, F32),
         SDS((2, 8, SSM_LANES), F32), SDS((1, D_SSM), F32), SDS((D_SSM, D_SSM), F32), SDS((1, D_SSM), F32)),
        (za, y2p, doa, carries, sp["perm"], sp["permt"], sp["mre"], sp["mim"], sp["mtre"], sp["mtim"],
         sp["nre"], sp["nim"], sp["ntre"], sp["ntim"], sp["a"], sp["ap"], sp["dskip"], sp["glu_w"],
         sp["glu_wt"], sp["glu_b"]),
        scratch=[pltpu.VMEM((tb, SSM_LANES), F32), pltpu.VMEM((tb, SSM_LANES), F32),
                 pltpu.VMEM((tb, SSM_LANES), F32), pltpu.VMEM((tb, SSM_LANES), F32),
                 pltpu.VMEM((2, 8, SSM_LANES), F32), pltpu.VMEM((2, SSM_LANES), F32),
                 pltpu.VMEM((2, SSM_LANES), F32)],
        sem=("arbitrary", "arbitrary"), bg=bg)


def _gmlp_spatial(ws_ref, vb):
    lane = lax.broadcasted_iota(jnp.int32, (CHUNK, 128), 1)
    parts = []
    for j in range(GMLP_HEADS // 2):
        vp = vb[:, 128 * j:128 * (j + 1)]
        parts.append(jnp.where(lane < GMLP_HEAD_DIM, _dot(ws_ref[2 * j], vp), _dot(ws_ref[2 * j + 1], vp)))
    return jnp.concatenate(parts, axis=1)


def _gmlp_fwd(zuv, ln_g, ln_b, wsm, bias, bg=None):
    t = zuv.shape[0]

    def body(z_ref, g_ref, b_ref, ws_ref, bias_ref, out_ref, outt_ref):
        u = _gelu(z_ref[:, 0:D_GMLP].astype(F32))
        v0 = _gelu(z_ref[:, D_GMLP:2 * D_GMLP].astype(F32))
        v, _, _ = _ln_fwd(v0, g_ref[...], b_ref[...])
        s = _gmlp_spatial(ws_ref, v.astype(BF16)) + bias_ref[...]
        out = (u * s).astype(BF16)
        out_ref[...] = out
        outt_ref[...] = out.T

    return _call(
        body, "gmlp_fwd", (t // CHUNK,),
        [_rows(CHUNK, 2 * D_GMLP), _fixed((1, D_GMLP)), _fixed((1, D_GMLP)),
         _fixed((GMLP_HEADS, CHUNK, CHUNK)), _fixed((CHUNK, D_GMLP))],
        (_rows(CHUNK, D_GMLP), _cols(D_GMLP, CHUNK)), (SDS((t, D_GMLP), BF16), SDS((D_GMLP, t), BF16)),
        (zuv, ln_g, ln_b, wsm, bias), sem=("parallel",), bg=bg)


def _gmlp_bwd(zuv, dgm, ln_g, ln_b, wsm, wsmt, bias, bg=None):
    t = zuv.shape[0]

    def body(z_ref, d_ref, g_ref, b_ref, ws_ref, wst_ref, bias_ref,
             dz_ref, dws_ref, dbias_ref, dg_ref, db_ref):
        @pl.when(pl.program_id(0) == 0)
        def _():
            for r in (dws_ref, dbias_ref, dg_ref, db_ref):
                r[...] = jnp.zeros_like(r)

        zu = z_ref[:, 0:D_GMLP].astype(F32)
        zv = z_ref[:, D_GMLP:2 * D_GMLP].astype(F32)
        u = _gelu(zu)
        v0 = _gelu(zv)
        gam = g_ref[...]
        v, vhat, rstd = _ln_fwd(v0, gam, b_ref[...])
        vb = v.astype(BF16)
        s = _gmlp_spatial(ws_ref, vb) + bias_ref[...]
        d = d_ref[...]
        dz_ref[:, 0:D_GMLP] = (d * s * _gelu_grad(zu)).astype(BF16)
        ds = d * u
        dbias_ref[...] += ds
        dsb = ds.astype(BF16)
        lane = lax.broadcasted_iota(jnp.int32, (CHUNK, 128), 1)
        tril = (lax.broadcasted_iota(jnp.int32, (CHUNK, CHUNK), 0)
                >= lax.broadcasted_iota(jnp.int32, (CHUNK, CHUNK), 1))
        zero_b = jnp.zeros((CHUNK, 128), BF16)
        parts = []
        for j in range(GMLP_HEADS // 2):
            dsp = dsb[:, 128 * j:128 * (j + 1)]
            vp = vb[:, 128 * j:128 * (j + 1)]
            parts.append(jnp.where(lane < GMLP_HEAD_DIM, _dot(wst_ref[2 * j], dsp),
                                   _dot(wst_ref[2 * j + 1], dsp)))
            lo = jnp.where(lane < GMLP_HEAD_DIM, dsp, zero_b)
            hi = jnp.where(lane < GMLP_HEAD_DIM, zero_b, dsp)
            dws_ref[2 * j] += jnp.where(tril, _dot_nt(lo, vp), 0.0)
            dws_ref[2 * j + 1] += jnp.where(tril, _dot_nt(hi, vp), 0.0)
        dv = jnp.concatenate(parts, axis=1)
        dg_ref[...] += jnp.sum(dv * vhat, axis=0, keepdims=True)
        db_ref[...] += jnp.sum(dv, axis=0, keepdims=True)
        dz_ref[:, D_GMLP:2 * D_GMLP] = (_ln_bwd(dv, vhat, rstd, gam) * _gelu_grad(zv)).astype(BF16)

    return _call(
        body, "gmlp_bwd", (t // CHUNK,),
        [_rows(CHUNK, 2 * D_GMLP), _rows(CHUNK, D_GMLP), _fixed((1, D_GMLP)), _fixed((1, D_GMLP)),
         _fixed((GMLP_HEADS, CHUNK, CHUNK)), _fixed((GMLP_HEADS, CHUNK, CHUNK)), _fixed((CHUNK, D_GMLP))],
        (_rows(CHUNK, 2 * D_GMLP), _fixed((GMLP_HEADS, CHUNK, CHUNK)), _fixed((CHUNK, D_GMLP)),
         _fixed((1, D_GMLP)), _fixed((1, D_GMLP))),
        (SDS((t, 2 * D_GMLP), BF16), SDS((GMLP_HEADS, CHUNK, CHUNK), F32), SDS((CHUNK, D_GMLP), F32),
         SDS((1, D_GMLP), F32), SDS((1, D_GMLP), F32)),
        (zuv, dgm, ln_g, ln_b, wsm, wsmt, bias), sem=("arbitrary",), bg=bg)


def _mixout_fwd(x1, s5o, gm, gab, ua, ub, wmo, g, b, tm, bg=None):
    t = x1.shape[0]

    def body(x_ref, s_ref, m_ref, gab_ref, ua_ref, ub_ref, wmo_ref, g_ref, b_ref,
             xn_ref, xh_ref, rstd_ref):
        ya = _dot(s_ref[...], ua_ref[...])
        yb = _dot(m_ref[...], ub_ref[...])
        mix = (_sigmoid(gab_ref[:, 0:D_MODEL].astype(F32)) * ya
               + _sigmoid(gab_ref[:, D_MODEL:2 * D_MODEL].astype(F32)) * yb)
        r = ALPHA * x_ref[...] + _dot(mix.astype(BF16), wmo_ref[...])
        y, xh, rstd = _ln_fwd(r, g_ref[...], b_ref[...])
        xn_ref[...] = y
        xh_ref[...] = xh
        rstd_ref[...] = rstd

    return _call(
        body, "mixout_fwd", (t // tm,),
        [_rows(tm, D_MODEL), _rows(tm, D_SSM), _rows(tm, D_GMLP), _rows(tm, 2 * D_MODEL),
         _resident((D_SSM, D_MODEL)), _resident((D_GMLP, D_MODEL)), _resident((D_MODEL, D_MODEL)),
         _fixed((1, D_MODEL)), _fixed((1, D_MODEL))],
        (_rows(tm, D_MODEL), _rows(tm, D_MODEL), _rows(tm, 1)),
        (SDS((t, D_MODEL), F32), SDS((t, D_MODEL), F32), SDS((t, 1), F32)),
        (x1, s5o, gm, gab, ua, ub, wmo, g, b), sem=("parallel",), bg=bg)


def _mixout_bwd(dx2, xh, rstd, s5o, gm, gab, ua, ub, wmo, g, tm, bg=None):
    t = dx2.shape[0]

    def body(d_ref, xh_ref, rstd_ref, s_ref, m_ref, gab_ref, ua_ref, ub_ref, wmo_ref, g_ref,
             dx1_ref, dmx_ref, mb_ref, dya_ref, dyb_ref, ds5_ref, dgm_ref, dgab_ref, dg_ref, db_ref):
        @pl.when(pl.program_id(0) == 0)
        def _():
            dg_ref[...] = jnp.zeros_like(dg_ref)
            db_ref[...] = jnp.zeros_like(db_ref)

        dy = d_ref[...]
        xhv = xh_ref[...]
        dr = _ln_bwd(dy, xhv, rstd_ref[...], g_ref[...])
        dg_ref[...] += jnp.sum(dy * xhv, axis=0, keepdims=True)
        db_ref[...] += jnp.sum(dy, axis=0, keepdims=True)
        dx1_ref[...] = ALPHA * dr
        drb = dr.astype(BF16)
        dmx_ref[...] = drb
        dm = _dot_nt(drb, wmo_ref[...])
        ya = _dot(s_ref[...], ua_ref[...])
        yb = _dot(m_ref[...], ub_ref[...])
        sa = _sigmoid(gab_ref[:, 0:D_MODEL].astype(F32))
        sb = _sigmoid(gab_ref[:, D_MODEL:2 * D_MODEL].astype(F32))
        mb_ref[...] = (sa * ya + sb * yb).astype(BF16).T
        dya = (dm * sa).astype(BF16)
        dyb = (dm * sb).astype(BF16)
        dya_ref[...] = dya
        dyb_ref[...] = dyb
        dgab_ref[:, 0:D_MODEL] = (dm * ya * sa * (1.0 - sa)).astype(BF16)
        dgab_ref[:, D_MODEL:2 * D_MODEL] = (dm * yb * sb * (1.0 - sb)).astype(BF16)
        ds5_ref[...] = _dot_nt(dya, ua_ref[...])
        dgm_ref[...] = _dot_nt(dyb, ub_ref[...])

    return _call(
        body, "mixout_bwd", (t // tm,),
        [_rows(tm, D_MODEL), _rows(tm, D_MODEL), _rows(tm, 1), _rows(tm, D_SSM), _rows(tm, D_GMLP),
         _rows(tm, 2 * D_MODEL), _resident((D_SSM, D_MODEL)), _resident((D_GMLP, D_MODEL)),
         _resident((D_MODEL, D_MODEL)), _fixed((1, D_MODEL))],
        (_rows(tm, D_MODEL), _rows(tm, D_MODEL), _cols(D_MODEL, tm), _rows(tm, D_MODEL),
         _rows(tm, D_MODEL), _rows(tm, D_SSM), _rows(tm, D_GMLP), _rows(tm, 2 * D_MODEL),
         _fixed((1, D_MODEL)), _fixed((1, D_MODEL))),
        (SDS((t, D_MODEL), F32), SDS((t, D_MODEL), BF16), SDS((D_MODEL, t), BF16),
         SDS((t, D_MODEL), BF16), SDS((t, D_MODEL), BF16), SDS((t, D_SSM), F32),
         SDS((t, D_GMLP), F32), SDS((t, 2 * D_MODEL), BF16),
         SDS((1, D_MODEL), F32), SDS((1, D_MODEL), F32)),
        (dx2, xh, rstd, s5o, gm, gab, ua, ub, wmo, g), sem=("arbitrary",), bg=bg)


def _ple_loss(x3, p, tgt, wpg, wpp, tm, bg=None):
    t = x3.shape[0]

    def body(x_ref, p_ref, t_ref, wpg_ref, wpp_ref, dx_ref, xb_ref, pb_ref, dq_ref, de_ref, loss_ref):
        @pl.when(pl.program_id(0) == 0)
        def _():
            loss_ref[...] = jnp.zeros_like(loss_ref)

        x3v = x_ref[...]
        xb = x3v.astype(BF16)
        pb = p_ref[...].astype(BF16)
        xb_ref[...] = xb.T
        pb_ref[...] = pb.T
        s = _sigmoid(_dot(xb, wpg_ref[...]))
        e = _dot(pb, wpp_ref[...])
        diff = x3v + s * e - t_ref[...]
        loss_ref[...] += jnp.sum(diff * diff, axis=0, keepdims=True)
        dout = diff * (1.0 / D_MODEL)
        de_ref[...] = (dout * s).astype(BF16)
        dq = (dout * e * s * (1.0 - s)).astype(BF16)
        dq_ref[...] = dq
        dx_ref[...] = dout + _dot_nt(dq, wpg_ref[...])

    return _call(
        body, "ple_loss", (t // tm,),
        [_rows(tm, D_MODEL), _rows(tm, PLE_DIM), _rows(tm, D_MODEL),
         _resident((D_MODEL, D_MODEL)), _resident((PLE_DIM, D_MODEL))],
        (_rows(tm, D_MODEL), _cols(D_MODEL, tm), _cols(PLE_DIM, tm), _rows(tm, D_MODEL),
         _rows(tm, D_MODEL), _fixed((1, D_MODEL))),
        (SDS((t, D_MODEL), F32), SDS((D_MODEL, t), BF16), SDS((PLE_DIM, t), BF16),
         SDS((t, D_MODEL), BF16), SDS((t, D_MODEL), BF16), SDS((1, D_MODEL), F32)),
        (x3, p, tgt, wpg, wpp), sem=("arbitrary",), bg=bg)


def _s5_discretise(lre, lim, log_dt, bre, bim):
    dt = jnp.exp(log_dt)[:, None]
    mag = jnp.exp(lre * dt)
    abr = mag * jnp.cos(lim * dt)
    abi = mag * jnp.sin(lim * dt)
    nr = abr - 1.0
    ni = abi
    den = lre * lre + lim * lim
    cr = ((nr * lre + ni * lim) / den)[..., None]
    ci = ((ni * lre - nr * lim) / den)[..., None]
    return abr, abi, cr * bre - ci * bim, cr * bim + ci * bre


def _block_diag_in(bb):
    v = bb.reshape(S5_BLOCKS, 8, SSM_STATE, SSM_GROUP_CH).transpose(0, 1, 3, 2)
    return jnp.einsum("bgip,gh->bgihp", v, jnp.eye(8, dtype=bb.dtype)).reshape(
        S5_BLOCKS, S5_BLOCK_IN, S5_BLOCK_ST)


def _block_diag_in_t(dm):
    v = dm.reshape(S5_BLOCKS, 8, SSM_GROUP_CH, 8, SSM_STATE)
    d = jnp.einsum("bgihp,gh->bgip", v, jnp.eye(8, dtype=dm.dtype))
    return d.transpose(0, 1, 3, 2).reshape(SSM_GROUPS, SSM_STATE, SSM_GROUP_CH)


def _block_diag_out(cc):
    v = cc.reshape(S5_BLOCKS, 8, SSM_GROUP_CH, SSM_STATE)
    return jnp.einsum("bgip,gh->bgphi", v, jnp.eye(8, dtype=cc.dtype)).reshape(
        S5_BLOCKS, S5_BLOCK_ST, S5_BLOCK_IN)


def _block_diag_out_t(dn):
    v = dn.reshape(S5_BLOCKS, 8, SSM_STATE, 8, SSM_GROUP_CH)
    d = jnp.einsum("bgphi,gh->bgip", v, jnp.eye(8, dtype=dn.dtype))
    return d.reshape(SSM_GROUPS, SSM_GROUP_CH, SSM_STATE)


def _s5_setup(lre, lim, log_dt, bre, bim, cre, cim, d_skip, glu_w, glu_b, tb):
    seg = tb // 8
    abr, abi, bbr, bbi = _s5_discretise(lre, lim, log_dt, bre, bim)
    pr, pi = abr, abi
    for _ in range(int(math.log2(seg))):
        pr, pi = pr * pr - pi * pi, 2.0 * pr * pi
    rows = jnp.arange(tb)
    src = (rows % 8) * seg + rows // 8
    perm = (src[:, None] == jnp.arange(tb)[None, :]).astype(BF16)
    mre = _block_diag_in(bbr)
    mim = _block_diag_in(bbi)
    nre = _block_diag_out(cre)
    nim = _block_diag_out(cim)
    return {
        "perm": perm, "permt": perm.T,
        "mre": mre.astype(BF16), "mim": mim.astype(BF16),
        "mtre": mre.transpose(0, 2, 1).astype(BF16), "mtim": mim.transpose(0, 2, 1).astype(BF16),
        "nre": nre.astype(BF16), "nim": nim.astype(BF16),
        "ntre": nre.transpose(0, 2, 1).astype(BF16), "ntim": nim.transpose(0, 2, 1).astype(BF16),
        "a": jnp.stack([abr.reshape(-1), abi.reshape(-1)]),
        "ap": jnp.stack([pr.reshape(-1), pi.reshape(-1)]),
        "dskip": d_skip.reshape(1, D_SSM), "glu_w": glu_w, "glu_wt": glu_w.T,
        "glu_b": glu_b.reshape(1, D_SSM),
    }


BIG = ("ffn1_w_in", "ffn1_w_out", "mix_w_in", "ssm_glu_w", "up_a", "up_b", "mix_w_out",
       "ffn2_w_in", "ffn2_w_out", "ple_w_proj", "ple_w_gate")
BIG_AXIS = {"ffn1_w_in": 1, "ffn1_w_out": 0, "mix_w_in": 1, "ssm_glu_w": 0, "up_a": 1, "up_b": 1,
            "mix_w_out": 0, "ffn2_w_in": 1, "ffn2_w_out": 0, "ple_w_proj": 1, "ple_w_gate": 0}
SHARD_MAJOR = 2
GATHER_AXIS = dict(BIG_AXIS, ffn1_w_in=SHARD_MAJOR, ffn2_w_in=SHARD_MAJOR)
GATHER_ORDER = (("ffn1_w_in",), ("ffn1_w_out",), ("mix_w_in",), ("ssm_glu_w", "up_a", "up_b", "mix_w_out"),
                ("ffn2_w_in",), ("ffn2_w_out", "ple_w_gate", "ple_w_proj"))
GATHER_FIRST_ID = 1
REDUCE_FIRST_ID = 7
SMALL = ("ln1_g", "ln1_b", "ssm_lambda_re", "ssm_lambda_im", "ssm_log_dt", "ssm_b_re", "ssm_b_im",
         "ssm_c_re", "ssm_c_im", "ssm_d", "ssm_glu_b", "gmlp_ln_g", "gmlp_ln_b", "gmlp_w_s",
         "gmlp_b_s", "ln2_g", "ln2_b", "ln3_g", "ln3_b")
SMALL_VIEW = {"ssm_b_re": (SSM_GROUPS, SSM_STATE * SSM_GROUP_CH), "ssm_b_im": (SSM_GROUPS, SSM_STATE * SSM_GROUP_CH)}


def _small_view(k, a):
    return a.reshape(SMALL_VIEW[k]) if k in SMALL_VIEW else a


def _place():
    return lax.axis_index("x"), lax.axis_index("y"), lax.axis_index("c")


def _other_chips(x, y):
    return [(1 - x, y), (x, 1 - y), (1 - x, 1 - y)]


def _window(ref, shard_shape, axis, chip, half):
    r, c = shard_shape
    hr = r // 2
    if axis == SHARD_MAJOR:
        return ref.at[chip] if half is None else ref.at[chip, pl.ds(half * hr, hr), :]
    if axis == 0:
        if half is None:
            return ref.at[pl.ds(chip * r, r), :]
        return ref.at[pl.ds(chip * r + half * hr, hr), :]
    if half is None:
        return ref.at[:, pl.ds(chip * c, c)]
    return ref.at[pl.ds(half * hr, hr), pl.ds(chip * c, c)]


def _gather_weights(shards, axes):
    n = len(shards)
    shapes = [s.shape for s in shards]
    full = [{0: (4 * r, c), 1: (r, 4 * c), SHARD_MAJOR: (4, r, c)}[ax] for (r, c), ax in zip(shapes, axes)]

    def remote(sems, i, k, src, dst, to):
        return pltpu.make_async_remote_copy(src_ref=src, dst_ref=dst, send_sem=sems[0].at[6 * i + k],
                                            recv_sem=sems[1].at[6 * i + k], device_id=to, device_id_type=MESH)

    def own_copies(ins, outs, sems):
        x, y, c = _place()
        me = 2 * x + y
        cps = []
        for i in range(n):
            hr = shapes[i][0] // 2
            mine = ins[i].at[pl.ds(c * hr, hr), :]
            for j, (cx, cy) in enumerate(_other_chips(x, y)):
                cps.append(remote(sems, i, j, mine, _window(outs[i], shapes[i], axes[i], me, c), (cx, cy, c)))
        local = [pltpu.make_async_copy(ins[i], _window(outs[i], shapes[i], axes[i], me, None), sems[2].at[i])
                 for i in range(n)]
        return cps, local

    def start(ins, outs, sems):
        cps, local = own_copies(ins, outs, sems)
        for cp in local + cps:
            cp.start()

    def finish(ins, outs, sems):
        x, y, c = _place()
        sibling = (x, y, 1 - c)
        passed = []
        for j, (cx, cy) in enumerate(_other_chips(x, y)):
            for i in range(n):
                w = _window(outs[i], shapes[i], axes[i], 2 * cx + cy, c)
                remote(sems, i, j, w, w, (cx, cy, c)).wait_recv()
                cp = remote(sems, i, 3 + j, w, w, sibling)
                cp.start()
                passed.append(cp)
        for j, (cx, cy) in enumerate(_other_chips(x, y)):
            for i in range(n):
                w = _window(outs[i], shapes[i], axes[i], 2 * cx + cy, 1 - c)
                remote(sems, i, 3 + j, w, w, sibling).wait_recv()
        cps, local = own_copies(ins, outs, sems)
        for cp in cps + passed:
            cp.wait_send()
        for cp in local:
            cp.wait()

    return _Exchange(shards, [SDS(f, BF16) for f in full],
                     [pltpu.SemaphoreType.DMA((6 * n,)), pltpu.SemaphoreType.DMA((6 * n,)),
                      pltpu.SemaphoreType.DMA((n,))], start, finish)


def _scatter_grads(parts, shapes, axes):
    n = len(parts)

    def copies(ins, outs, sems):
        x, y, c = _place()
        return [pltpu.make_async_remote_copy(
            src_ref=_window(ins[i], shapes[i], axes[i], 2 * cx + cy, None), dst_ref=outs[i].at[j],
            send_sem=sems[0].at[3 * i + j], recv_sem=sems[1].at[3 * i + j],
            device_id=(cx, cy, c), device_id_type=MESH)
            for i in range(n) for j, (cx, cy) in enumerate(_other_chips(x, y))]

    def start(ins, outs, sems):
        for cp in copies(ins, outs, sems):
            cp.start()

    def finish(ins, outs, sems):
        for cp in copies(ins, outs, sems):
            cp.wait()

    return _Exchange(parts, [SDS((3,) + tuple(s), BF16) for s in shapes],
                     [pltpu.SemaphoreType.DMA((3 * n,)), pltpu.SemaphoreType.DMA((3 * n,))], start, finish)


def _swap_halves(parts, shapes, axes):
    n = len(parts)

    def copies(ins, outs, sems):
        x, y, c = _place()
        cps = []
        for i in range(n):
            r, _ = shapes[i]
            hr = r // 2
            if axes[i] == 0:
                cps += [pltpu.make_async_remote_copy(
                    src_ref=ins[i].at[pl.ds(k * r + (1 - c) * hr, hr), :], dst_ref=outs[i].at[k],
                    send_sem=sems[0].at[i], recv_sem=sems[1].at[i], device_id=(x, y, 1 - c),
                    device_id_type=MESH) for k in range(4)]
            else:
                cps.append(pltpu.make_async_remote_copy(
                    src_ref=ins[i].at[pl.ds((1 - c) * hr, hr), :], dst_ref=outs[i],
                    send_sem=sems[0].at[i], recv_sem=sems[1].at[i], device_id=(x, y, 1 - c),
                    device_id_type=MESH))
        return cps

    def start(ins, outs, sems):
        for cp in copies(ins, outs, sems):
            cp.start()

    def finish(ins, outs, sems):
        x, y, c = _place()
        for i in range(n):
            pltpu.make_async_remote_copy(src_ref=outs[i], dst_ref=outs[i], send_sem=sems[0].at[i],
                                         recv_sem=sems[1].at[i], device_id=(x, y, 1 - c),
                                         device_id_type=MESH).wait()

    out = [SDS((4, r // 2, c), BF16) if ax == 0 else SDS((r // 2, 4 * c), BF16)
           for (r, c), ax in zip(shapes, axes)]
    return _Exchange(parts, out, [pltpu.SemaphoreType.DMA((n,)), pltpu.SemaphoreType.DMA((n,))], start, finish)


def _scatter_halves(pres, shapes):
    n = len(pres)

    def copies(ins, outs, sems):
        x, y, c = _place()
        return [pltpu.make_async_remote_copy(
            src_ref=ins[i].at[1 + j], dst_ref=outs[i].at[j], send_sem=sems[0].at[3 * i + j],
            recv_sem=sems[1].at[3 * i + j], device_id=(cx, cy, c), device_id_type=MESH)
            for i in range(n) for j, (cx, cy) in enumerate(_other_chips(x, y))]

    def start(ins, outs, sems):
        for cp in copies(ins, outs, sems):
            cp.start()

    def finish(ins, outs, sems):
        for cp in copies(ins, outs, sems):
            cp.wait()

    return _Exchange(pres, [SDS((3, r // 2, c), BF16) for r, c in shapes],
                     [pltpu.SemaphoreType.DMA((3 * n,)), pltpu.SemaphoreType.DMA((3 * n,))], start, finish)


def _swap_with_sibling(arrs):
    n = len(arrs)

    def copies(ins, outs, sems):
        x, y, c = _place()
        return [pltpu.make_async_remote_copy(src_ref=ins[i], dst_ref=outs[i], send_sem=sems[0].at[i],
                                             recv_sem=sems[1].at[i], device_id=(x, y, 1 - c),
                                             device_id_type=MESH) for i in range(n)]

    def start(ins, outs, sems):
        for cp in copies(ins, outs, sems):
            cp.start()

    def finish(ins, outs, sems):
        for cp in copies(ins, outs, sems):
            cp.wait()

    return _Exchange(arrs, [SDS(a.shape, a.dtype) for a in arrs],
                     [pltpu.SemaphoreType.DMA((n,)), pltpu.SemaphoreType.DMA((n,))], start, finish)


def _gather_small(arrs):
    n = len(arrs)

    def copy(sems, outs, i, k, block, to, src=None):
        px, py, pc = block
        dst = outs[i].at[4 * px + 2 * py + pc]
        return pltpu.make_async_remote_copy(
            src_ref=dst if src is None else src, dst_ref=dst, send_sem=sems[0].at[7 * i + k],
            recv_sem=sems[1].at[7 * i + k], device_id=to, device_id_type=MESH)

    direct = [math.prod(a.shape) * 4 <= DIRECT_GATHER_BYTES for a in arrs]

    def own_copies(ins, outs, sems):
        x, y, c = _place()
        cps = []
        for i in range(n):
            cps.append(copy(sems, outs, i, 0, (x, y, c), (x, y, 1 - c), src=ins[i]))
            for j, (cx, cy) in enumerate(_other_chips(x, y)):
                cps.append(copy(sems, outs, i, 1 + j, (x, y, c), (cx, cy, c), src=ins[i]))
                if direct[i]:
                    cps.append(copy(sems, outs, i, 4 + j, (x, y, c), (cx, cy, 1 - c), src=ins[i]))
        local = [pltpu.make_async_copy(ins[i], outs[i].at[4 * x + 2 * y + c], sems[2].at[i]) for i in range(n)]
        return cps, local

    def start(ins, outs, sems):
        cps, local = own_copies(ins, outs, sems)
        for cp in local + cps:
            cp.start()

    def finish(ins, outs, sems):
        x, y, c = _place()
        passed = []
        for j, (cx, cy) in enumerate(_other_chips(x, y)):
            for i in range(n):
                copy(sems, outs, i, 1 + j, (cx, cy, c), (x, y, c)).wait_recv()
                if not direct[i]:
                    cp = copy(sems, outs, i, 4 + j, (cx, cy, c), (x, y, 1 - c))
                    cp.start()
                    passed.append(cp)
        for i in range(n):
            copy(sems, outs, i, 0, (x, y, 1 - c), (x, y, c)).wait_recv()
            for j, (cx, cy) in enumerate(_other_chips(x, y)):
                copy(sems, outs, i, 4 + j, (cx, cy, 1 - c), (x, y, c)).wait_recv()
        cps, local = own_copies(ins, outs, sems)
        for cp in cps + passed:
            cp.wait_send()
        for cp in local:
            cp.wait()

    return _Exchange(arrs, [SDS((N_DEV,) + a.shape, F32) for a in arrs],
                     [pltpu.SemaphoreType.DMA((7 * n,)), pltpu.SemaphoreType.DMA((7 * n,)),
                      pltpu.SemaphoreType.DMA((n,))], start, finish)


def _local_step(x, p, tgt, wb, ws, shards=None, opt=None):
    bsz, seq, _ = x.shape
    t = bsz * seq
    tm = min(256, t)
    tb = min(256, seq)
    x0 = x.reshape(t, D_MODEL)
    p0 = p.reshape(t, PLE_DIM)
    tg = tgt.reshape(t, D_MODEL)
    row = lambda v: v.reshape(1, -1)
    dist = shards is not None
    wb = dict(wb)
    recv, sums, other, gathered = {}, {}, {}, {}
    gb = {}
    gs = {}
    shape_of, axis_of = {}, {}
    chip = None
    if dist:
        shape_of = {k: tuple(shards[k].shape) for k in BIG}
        axis_of = dict(BIG_AXIS)
        for q in range(LAST_PIECES):
            shape_of[LAST_PIECE % q] = (D_MODEL // LAST_PIECES, shape_of["ffn1_w_in"][1])
            axis_of[LAST_PIECE % q] = 1
        xi, yi, ci = _place()
        chip = (2 * xi + yi).astype(jnp.int32).reshape(1)
        ids = jnp.stack([2 * xi + yi] + [2 * cx + cy for cx, cy in _other_chips(xi, yi)] + [ci]).astype(jnp.int32)
    halfbuf, pre = {}, {}

    def gather(names):
        return _gather_weights([shards[k] for k in names], [GATHER_AXIS[k] for k in names]) if dist else None

    def exchange(scat=(), swap=(), halves=(), scat2=(), swap2=(), extra=None, after=None):
        if not dist:
            return None, []
        after = order[0] if after is None else after
        parts, tags = [], []
        if scat:
            parts.append(_scatter_grads([gb[k][1] for k in scat], [shape_of[k] for k in scat],
                                        [axis_of[k] for k in scat]))
            tags.append((recv, scat))
        if swap:
            for k in swap:
                sums[k] = order[0] = _sum_blocks(gb[k][0], recv[k], shape_of[k], axis_of[k], chip, "sum_" + k,
                                                 order[0])
            parts.append(_swap_with_sibling([sums[k] for k in swap]))
            tags.append((other, swap))
        if halves:
            parts.append(_swap_halves([gb[k][1] for k in halves], [shape_of[k] for k in halves],
                                      [axis_of[k] for k in halves]))
            tags.append((halfbuf, halves))
        if scat2:
            for k in scat2:
                pre[k] = _presum(gb[k][0], halfbuf[k], shape_of[k], axis_of[k], ids, "presum_" + k, order[0])
                order[0] = pre[k][0]
            parts.append(_scatter_halves([pre[k][1] for k in scat2], [shape_of[k] for k in scat2]))
            tags.append((recv, scat2))
        if swap2:
            for k in swap2:
                sums[k] = order[0] = _sum_half(pre[k][0], recv[k], "sum_" + k, order[0])
            parts.append(_swap_with_sibling([sums[k] for k in swap2]))
            tags.append((other, swap2))
        if extra is not None:
            parts.append(extra[0])
            tags.append((extra[1], extra[2]))
        return (_join(parts), tags) if parts else (None, [])

    def take(ex_tags, got):
        ex, tags = ex_tags
        if ex is not None:
            for (dst, names), (o0, o1) in zip(tags, ex.cuts):
                dst.update(zip(names, got[o0:o1]))

    order = [None]

    def ordered(builder, *args, **kw):
        res = builder(*args, bg=order[0] if dist else None, **kw)
        order[0] = res[0][0]
        return res

    launched = []

    def launch(ex_tags):
        if ex_tags[0] is not None:
            n = len(launched)
            launched.append(n)
            take(ex_tags, _run_exchange_on_sequencer(ex_tags[0], "reduce_%d" % n, REDUCE_FIRST_ID + n))

    small_shape = {k: _small_view(k, v).shape for k, v in ws.items()}
    small_shape["loss_rows"] = (1, D_MODEL)
    ws = {k: v if (v.ndim == 2 and k != "ssm_log_dt") else v[0] for k, v in ws.items()}
    tril = jnp.tril(jnp.ones((CHUNK, CHUNK), dtype=bool))
    wsm = jnp.where(tril[None], ws["gmlp_w_s"], 0.0)
    wsm_b = wsm.astype(BF16)
    wsmt_b = wsm.transpose(0, 2, 1).astype(BF16)
    bias = jnp.repeat(ws["gmlp_b_s"].T, GMLP_HEAD_DIM, axis=1)

    tf = min(512, t)
    if dist:
        for gi, names in enumerate(GATHER_ORDER):
            wb.update(zip(names, _run_exchange_on_sequencer(gather(names), "gather_%d" % gi, GATHER_FIRST_ID + gi)))
    (x0b, h1, a1), _ = _ffn_proj(x0, wb["ffn1_w_in"], tf, "ffn1_proj")
    (x1, xh1, rstd1), _ = _ffn_out(x0, a1, wb["ffn1_w_out"], row(ws["ln1_g"]), row(ws["ln1_b"]), tf, "ffn1_out")
    sp = _s5_setup(ws["ssm_lambda_re"], ws["ssm_lambda_im"], ws["ssm_log_dt"], ws["ssm_b_re"],
                   ws["ssm_b_im"], ws["ssm_c_re"], ws["ssm_c_im"], ws["ssm_d"], wb["ssm_glu_w"],
                   ws["ssm_glu_b"], tb)
    (x1b, za, zuv, gab), _ = _mixin_fwd(x1, wb["mix_w_in"], tm)
    (s5o, s5ot, y2p, carries), _ = _s5_fwd(za, sp, bsz, seq, tb)
    (gm, gmt), _ = _gmlp_fwd(zuv, row(ws["gmlp_ln_g"]), row(ws["gmlp_ln_b"]), wsm_b, bias)
    (x2, xh2, rstd2), _ = _mixout_fwd(x1, s5o, gm, gab, wb["up_a"], wb["up_b"], wb["mix_w_out"],
                                           row(ws["ln2_g"]), row(ws["ln2_b"]), tm)
    (x2b, h2, a2), _ = _ffn_proj(x2, wb["ffn2_w_in"], tf, "ffn2_proj")
    (x3, xh3, rstd3), _ = _ffn_out(x2, a2, wb["ffn2_w_out"], row(ws["ln3_g"]), row(ws["ln3_b"]), tf, "ffn2_out")
    (dx3, x3b, pb, dq, de, loss_rows), _ = _ple_loss(x3, p0, tg, wb["ple_w_gate"], wb["ple_w_proj"], tm)
    order[0] = dx3
    gb["ple_w_gate"], _ = ordered(_tn_matmul, x3b, dq, "dw_ple_gate", 1024, 1024, a_t=True)
    gb["ple_w_proj"], _ = ordered(_tn_matmul, pb, de, "dw_ple_proj", 256, 1024, a_t=True)
    launch(exchange(scat=("ple_w_gate", "ple_w_proj")))
    (dx2, dh2, df2, gs["ln3_g"], gs["ln3_b"]), _ = ordered(
        _ffn_bwd, dx3, xh3, rstd3, h2, wb["ffn2_w_in"], wb["ffn2_w_out"], row(ws["ln3_g"]), tm, "ffn2_bwd")
    gb["ffn2_w_out"], _ = ordered(_tn_matmul, a2, df2, "dw_ffn2_out", 1408, 1024)
    launch(exchange(scat=("ffn2_w_out",)))
    gb["ffn2_w_in"], _ = ordered(_tn_matmul, x2b, dh2, "dw_ffn2_in", 1024, 1408, a_t=True)
    launch(exchange(scat=("ffn2_w_in",), swap=("ple_w_gate", "ple_w_proj")))
    (dx1a, dmx, mb, dya, dyb, ds5, dgm, dgab, gs["ln2_g"], gs["ln2_b"]), _ = ordered(
        _mixout_bwd, dx2, xh2, rstd2, s5o, gm, gab, wb["up_a"], wb["up_b"], wb["mix_w_out"], row(ws["ln2_g"]), tm)
    gb["mix_w_out"], _ = ordered(_tn_matmul, mb, dmx, "dw_mix_out", 1024, 1024, a_t=True)
    gb["up_a"], _ = ordered(_tn_matmul, s5ot, dya, "dw_up_a", 512, 1024, a_t=True)
    gb["up_b"], _ = ordered(_tn_matmul, gmt, dyb, "dw_up_b", 512, 1024, a_t=True)
    launch(exchange(scat=("mix_w_out", "up_a", "up_b"), swap=("ffn2_w_out",)))
    (dza, dmr, dmi, dnr, dni, da, ddsk, dgw, dgb), _ = ordered(_s5_bwd, za, y2p, ds5, carries, sp, bsz, seq, tb)
    gb["ssm_glu_w"] = (dgw, dgw.astype(BF16))
    launch(exchange(scat=("ssm_glu_w",), swap=("ffn2_w_in",)))
    (dzuv, dws, dbias, gs["gmlp_ln_g"], gs["gmlp_ln_b"]), _ = ordered(
        _gmlp_bwd, zuv, dgm, row(ws["gmlp_ln_g"]), row(ws["gmlp_ln_b"]), wsm_b, wsmt_b, bias)
    (dx1,), _ = ordered(_mixin_bwd, dx1a, dza, dzuv, dgab, wb["mix_w_in"], tm)
    g_mi, _ = ordered(_tn_matmul, x1b, dza, "dw_mix_in_a", 1024, 512, 0, 3584, a_t=True)
    g_mi, _ = ordered(_tn_matmul, x1b, dzuv, "dw_mix_in_uv", 1024, 512, 1, 3584, g_mi, a_t=True)
    gb["mix_w_in"], _ = ordered(_tn_matmul, x1b, dgab, "dw_mix_in_g", 1024, 512, 3, 3584, g_mi, a_t=True)
    launch(exchange(swap=("mix_w_out", "up_a", "up_b", "ssm_glu_w")))

    d_abr = da[0].sum(axis=0).reshape(SSM_GROUPS, SSM_STATE)
    d_abi = da[1].sum(axis=0).reshape(SSM_GROUPS, SSM_STATE)
    _, vjp = jax.vjp(_s5_discretise, ws["ssm_lambda_re"], ws["ssm_lambda_im"], ws["ssm_log_dt"],
                     ws["ssm_b_re"], ws["ssm_b_im"])
    (gs["ssm_lambda_re"], gs["ssm_lambda_im"], gs["ssm_log_dt"], gs["ssm_b_re"], gs["ssm_b_im"]) = vjp(
        (d_abr, d_abi, _block_diag_in_t(dmr), _block_diag_in_t(dmi)))
    gs["ssm_c_re"] = _block_diag_out_t(dnr)
    gs["ssm_c_im"] = _block_diag_out_t(dni)
    gs["ssm_d"] = ddsk
    gs["ssm_glu_b"] = dgb
    gs["gmlp_w_s"] = dws
    gs["gmlp_b_s"] = dbias.reshape(CHUNK, GMLP_HEADS, GMLP_HEAD_DIM).sum(axis=-1).T
    gs["loss_rows"] = loss_rows

    def small_gather(names):
        return (_gather_small([gs[k].reshape(small_shape[k]) for k in names]), gathered, names) if dist else None

    late = ("ln1_g", "ln1_b")
    launch(exchange(scat=("mix_w_in",), extra=small_gather(tuple(k for k in SMALL + ("loss_rows",) if k not in late))))
    (dx0, dh1, df1, gs["ln1_g"], gs["ln1_b"]), _ = ordered(
        _ffn_bwd, dx1, xh1, rstd1, h1, wb["ffn1_w_in"], wb["ffn1_w_out"], row(ws["ln1_g"]), tm, "ffn1_bwd")
    grad_x = dx0.reshape(bsz, seq, D_MODEL)
    if not dist:
        gb["ffn1_w_out"], _ = _tn_matmul(a1, df1, "dw_ffn1_out", 1408, 1024)
        gb["ffn1_w_in"], _ = _tn_matmul(x0b, dh1, "dw_ffn1_in", 1024, 1408, a_t=True)
        return (loss_rows, grad_x, gb, {k: gs[k].reshape(small_shape[k]) for k in SMALL}, sums, other, gathered,
                None, {})
    launch(exchange(extra=small_gather(late)))
    gb["ffn1_w_out"], _ = ordered(_tn_matmul, a1, df1, "dw_ffn1_out", 1408, 1024)
    last = ["ffn1_w_out"] + [LAST_PIECE % q for q in range(LAST_PIECES)]
    fillers = (("ffn2_w_in", "mix_w_in", "ple_w_gate"),
               ("ffn2_w_out", "mix_w_out", "up_a", "up_b", "ssm_glu_w", "ple_w_proj"))
    out = {}
    for i in range(1, len(last) + 3):
        stage = lambda d: tuple(last[i - d:i - d + 1]) if 0 <= i - d < len(last) else ()
        launch(exchange(halves=stage(1), scat2=stage(2), swap2=stage(3), swap=("mix_w_in",) if i == 2 else ()))
        if i < len(last):
            gb[last[i]], _ = ordered(_tn_matmul, x0b, dh1, "dw_" + last[i], D_MODEL // LAST_PIECES, 1408,
                                     a_cols=(i - 1, 1), a_t=True)
        elif i - len(last) < len(fillers):
            for k in fillers[i - len(last)]:
                w, m, v = opt[k]
                out[k] = _adam_big(w, sums[k], other[k], m, v, "adam_" + k, after=order[0])
                order[0] = out[k][1]
    return loss_rows, grad_x, gb, gs, sums, other, gathered, ids, out


def _adamw(w, g, m, v):
    m = ADAM_B1 * m + (1.0 - ADAM_B1) * g
    v = ADAM_B2 * v + (1.0 - ADAM_B2) * (g * g)
    m_hat = m / ADAM_C1
    v_hat = v / ADAM_C2
    delta = -ADAM_LR * (m_hat / (jnp.sqrt(v_hat) + ADAM_EPS) + ADAM_WD * w)
    return delta, m, v


def _pinned(after):
    return ([pl.BlockSpec(memory_space=pl.ANY)], [after]) if after is not None else ([], [])


def _sum_blocks(part, recv, shape, axis, chip, name, after=None):
    r, c = shape
    rb = r // 8

    def body(chip_ref, p_ref, r_ref, *rest):
        rest[-1][...] = (p_ref[...] + r_ref[0].astype(F32) + r_ref[1].astype(F32) + r_ref[2].astype(F32))

    if axis == 0:
        own = pl.BlockSpec((rb, c), lambda i, k: (k[0] * 8 + i, 0))
    else:
        own = pl.BlockSpec((rb, c), lambda i, k: (i, k[0]))
    pin_specs, pin_args = _pinned(after)
    grid_spec = pltpu.PrefetchScalarGridSpec(
        num_scalar_prefetch=1, grid=(8,),
        in_specs=[own, pl.BlockSpec((3, rb, c), lambda i, k: (0, i, 0))] + pin_specs,
        out_specs=pl.BlockSpec((rb, c), lambda i, k: (i, 0)))
    return pl.pallas_call(body, name=name, out_shape=SDS((r, c), F32), grid_spec=grid_spec,
                          compiler_params=_params(("parallel",)))(chip, part, recv, *pin_args)


def _presum(part, half, shape, axis, ids, name, after=None):
    r, c = shape
    rb = r // 4

    def body(ids_ref, p_ref, h_ref, *rest):
        of_ref, ob_ref = rest[-2:]
        s = p_ref[...] + h_ref[...].astype(F32)
        ob_ref[...] = s.astype(BF16)

        @pl.when(pl.program_id(1) == 0)
        def _():
            of_ref[...] = s

    if axis == 0:
        p_spec = pl.BlockSpec((rb, c), lambda i, t, ids: (ids[t] * 4 + ids[4] * 2 + i, 0))
        h_spec = pl.BlockSpec((None, rb, c), lambda i, t, ids: (ids[t], i, 0))
    else:
        p_spec = pl.BlockSpec((rb, c), lambda i, t, ids: (ids[4] * 2 + i, ids[t]))
        h_spec = pl.BlockSpec((rb, c), lambda i, t, ids: (i, ids[t]))
    pin_specs, pin_args = _pinned(after)
    grid_spec = pltpu.PrefetchScalarGridSpec(
        num_scalar_prefetch=1, grid=(2, 4), in_specs=[p_spec, h_spec] + pin_specs,
        out_specs=(pl.BlockSpec((rb, c), lambda i, t, ids: (i, 0)),
                   pl.BlockSpec((None, rb, c), lambda i, t, ids: (t, i, 0))))
    return pl.pallas_call(body, name=name, out_shape=(SDS((r // 2, c), F32), SDS((4, r // 2, c), BF16)),
                          grid_spec=grid_spec,
                          compiler_params=_params(("parallel", "arbitrary")))(ids, part, half, *pin_args)


def _sum_half(pre, recv, name, after=None):
    hr, c = pre.shape
    rb = hr // 2

    def body(p_ref, r_ref, *rest):
        rest[-1][...] = (p_ref[...] + r_ref[0].astype(F32) + r_ref[1].astype(F32) + r_ref[2].astype(F32))

    spec = pl.BlockSpec((rb, c), lambda i: (i, 0))
    pin_specs, pin_args = _pinned(after)
    return pl.pallas_call(body, name=name, grid=(2,), out_shape=SDS((hr, c), F32),
                          in_specs=[spec, pl.BlockSpec((3, rb, c), lambda i: (0, i, 0))] + pin_specs,
                          out_specs=spec, compiler_params=_params(("parallel",)))(pre, recv, *pin_args)


def _adam_halves(w, mine, oth, m, v, ids, name, piece=0, prev=None):
    r, c = w.shape
    rb = mine.shape[0] // 2

    def body(ids_ref, w_ref, a_ref, b_ref, m_ref, v_ref, *rest):
        g_ref, d_ref, nm_ref, nv_ref = rest[-4:]
        g = jnp.where(pl.program_id(0) // 2 == ids_ref[4], a_ref[...], b_ref[...])
        g_ref[...] = g
        d_ref[...], nm_ref[...], nv_ref[...] = _adamw(w_ref[...], g, m_ref[...], v_ref[...])

    whole = pl.BlockSpec((rb, c), lambda i, ids: (i + 4 * piece, 0))
    part = pl.BlockSpec((rb, c), lambda i, ids: (i % 2, 0))
    in_specs = [whole, part, part, whole, whole]
    args = [w, mine, oth, m, v]
    aliases = {}
    if prev is not None:
        in_specs += [pl.BlockSpec(memory_space=pl.ANY)] * 4
        args += list(prev)
        aliases = {6: 0, 7: 1, 8: 2, 9: 3}
    grid_spec = pltpu.PrefetchScalarGridSpec(num_scalar_prefetch=1, grid=(4,), in_specs=in_specs,
                                             out_specs=(whole,) * 4)
    return pl.pallas_call(body, name=name, out_shape=tuple(SDS((r, c), F32) for _ in range(4)),
                          grid_spec=grid_spec, input_output_aliases=aliases,
                          compiler_params=_params(("parallel",)))(ids, *args)


def _adam_big(w, ga, gb, m, v, name, piece=0, prev=None, after=None):
    r, c = w.shape
    pr = ga.shape[0]
    steps = 8 if pr == r else 2
    rb = pr // steps
    off = piece * steps

    def body(w_ref, ga_ref, gb_ref, m_ref, v_ref, *rest):
        g_ref, d_ref, nm_ref, nv_ref = rest[-4:]
        g = ga_ref[...] + gb_ref[...]
        g_ref[...] = g
        d_ref[...], nm_ref[...], nv_ref[...] = _adamw(w_ref[...], g, m_ref[...], v_ref[...])

    whole = pl.BlockSpec((rb, c), lambda i: (i + off, 0))
    part = pl.BlockSpec((rb, c), lambda i: (i, 0))
    in_specs = [whole, part, part, whole, whole]
    args = [w, ga, gb, m, v]
    aliases = {}
    if prev is not None:
        in_specs += [pl.BlockSpec(memory_space=pl.ANY)] * 4
        args += list(prev)
        aliases = {5: 0, 6: 1, 7: 2, 8: 3}
    if after is not None:
        in_specs.append(pl.BlockSpec(memory_space=pl.ANY))
        args.append(after)
    return pl.pallas_call(
        body, name=name, grid=(steps,), out_shape=tuple(SDS((r, c), F32) for _ in range(4)),
        in_specs=in_specs, out_specs=(whole,) * 4, input_output_aliases=aliases,
        compiler_params=_params(("parallel",)),
    )(*args)


def _adam_small(ws, gathered, ms, vs):
    n = len(ws)

    def body(*refs):
        w_refs, g_refs, m_refs, v_refs = refs[:n], refs[n:2 * n], refs[2 * n:3 * n], refs[3 * n:4 * n]
        outs = refs[4 * n:]
        for i in range(n):
            g = g_refs[i][0]
            for d in range(1, N_DEV):
                g = g + g_refs[i][d]
            delta, nm, nv = _adamw(w_refs[i][...], g, m_refs[i][...], v_refs[i][...])
            outs[i][...] = g
            outs[n + i][...] = delta
            outs[2 * n + i][...] = nm
            outs[3 * n + i][...] = nv

    vmem = pl.BlockSpec(memory_space=pltpu.VMEM)
    shapes = [w.shape for w in ws]
    return pl.pallas_call(
        body, name="adam_small", out_shape=tuple(SDS(s, F32) for s in shapes * 4),
        in_specs=[vmem] * (4 * n), out_specs=tuple([vmem] * (4 * n)),
        compiler_params=pltpu.CompilerParams(vmem_limit_bytes=VMEM_LIMIT_BYTES),
    )(*ws, *gathered, *ms, *vs)


def _sum_loss(gathered):
    def body(g_ref, o_ref):
        tot = g_ref[0]
        for d in range(1, N_DEV):
            tot = tot + g_ref[d]
        o_ref[...] = (0.5 / D_MODEL) * jnp.sum(tot, axis=1, keepdims=True)

    vmem = pl.BlockSpec(memory_space=pltpu.VMEM)
    return pl.pallas_call(body, name="sum_loss", out_shape=SDS((1, 1), F32), in_specs=[vmem],
                          out_specs=vmem)(gathered)


def kernel(x, p, ffn1_w_in, ffn1_w_out, ln1_g, ln1_b, mix_w_in, ssm_lambda_re, ssm_lambda_im, ssm_log_dt, ssm_b_re, ssm_b_im, ssm_c_re, ssm_c_im, ssm_d, ssm_glu_w, ssm_glu_b, gmlp_ln_g, gmlp_ln_b, gmlp_w_s, gmlp_b_s, up_a, up_b, mix_w_out, ln2_g, ln2_b, ffn2_w_in, ffn2_w_out, ln3_g, ln3_b, ple_w_proj, ple_w_gate, loss_target, m_ffn1_w_in, m_ffn1_w_out, m_ln1_g, m_ln1_b, m_mix_w_in, m_ssm_lambda_re, m_ssm_lambda_im, m_ssm_log_dt, m_ssm_b_re, m_ssm_b_im, m_ssm_c_re, m_ssm_c_im, m_ssm_d, m_ssm_glu_w, m_ssm_glu_b, m_gmlp_ln_g, m_gmlp_ln_b, m_gmlp_w_s, m_gmlp_b_s, m_up_a, m_up_b, m_mix_w_out, m_ln2_g, m_ln2_b, m_ffn2_w_in, m_ffn2_w_out, m_ln3_g, m_ln3_b, m_ple_w_proj, m_ple_w_gate, v_ffn1_w_in, v_ffn1_w_out, v_ln1_g, v_ln1_b, v_mix_w_in, v_ssm_lambda_re, v_ssm_lambda_im, v_ssm_log_dt, v_ssm_b_re, v_ssm_b_im, v_ssm_c_re, v_ssm_c_im, v_ssm_d, v_ssm_glu_w, v_ssm_glu_b, v_gmlp_ln_g, v_gmlp_ln_b, v_gmlp_w_s, v_gmlp_b_s, v_up_a, v_up_b, v_mix_w_out, v_ln2_g, v_ln2_b, v_ffn2_w_in, v_ffn2_w_out, v_ln3_g, v_ln3_b, v_ple_w_proj, v_ple_w_gate):
    given = dict(locals())
    order = ("ffn1_w_in", "ffn1_w_out", "ln1_g", "ln1_b", "mix_w_in", "ssm_lambda_re", "ssm_lambda_im",
             "ssm_log_dt", "ssm_b_re", "ssm_b_im", "ssm_c_re", "ssm_c_im", "ssm_d", "ssm_glu_w", "ssm_glu_b",
             "gmlp_ln_g", "gmlp_ln_b", "gmlp_w_s", "gmlp_b_s", "up_a", "up_b", "mix_w_out", "ln2_g", "ln2_b",
             "ffn2_w_in", "ffn2_w_out", "ln3_g", "ln3_b", "ple_w_proj", "ple_w_gate")
    assert set(order) == set(BIG + SMALL)

    shard = {k: given[k][0] for k in BIG}
    shard_b = {k: shard[k].astype(BF16) for k in BIG}
    opt = {k: (shard[k], given["m_" + k][0], given["v_" + k][0]) for k in BIG}
    loss_rows, grad_x, gb, gs, sums, other, gathered, ids, out = _local_step(
        x, given["p"][0], loss_target, {}, {k: given[k] for k in SMALL}, shard_b, opt)

    out = dict(out)
    for k in BIG:
        if k in out:
            continue
        moments = (given["m_" + k][0], given["v_" + k][0])
        if k == "ffn1_w_out":
            out[k] = _adam_halves(shard[k], sums[k], other[k], *moments, ids, "adam_" + k)
        elif k == "ffn1_w_in":
            for q in range(LAST_PIECES):
                kq = LAST_PIECE % q
                out[k] = _adam_halves(shard[k], sums[kq], other[kq], *moments, ids, "adam_" + kq, q, out.get(k))
        else:
            out[k] = _adam_big(shard[k], sums[k], other[k], *moments, "adam_" + k,
                               after=gb[LAST_PIECE % (LAST_PIECES - 1)][0])

    res = _adam_small([_small_view(k, given[k]) for k in SMALL], [gathered[k] for k in SMALL],
                      [_small_view(k, given["m_" + k]) for k in SMALL],
                      [_small_view(k, given["v_" + k]) for k in SMALL])
    ns = len(SMALL)
    for i, k in enumerate(SMALL):
        out[k] = tuple(res[j * ns + i].reshape(given[k].shape) for j in range(4))
    loss = _sum_loss(gathered["loss_rows"]).reshape(())

    lead = lambda k, j: out[k][j][None] if k in BIG else out[k][j]
    return (loss, grad_x, *[lead(k, 0) for k in order], *[lead(k, 1) for k in order],
            *[lead(k, 2) for k in order], *[lead(k, 3) for k in order])
```

```python
import math

import jax
import jax.numpy as jnp
from jax import lax
from jax.experimental import pallas as pl
from jax.experimental.pallas import tpu as pltpu
from jax.experimental.pallas import tpu_sc as plsc

F32 = jnp.float32
BF16 = jnp.bfloat16
MESH = pl.DeviceIdType.MESH
SDS = jax.ShapeDtypeStruct

D_MODEL = 1024
D_FF = 2816
D_SSM = 512
D_GMLP = 512
SSM_GROUPS = 32
SSM_GROUP_CH = 16
SSM_STATE = 64
SSM_LANES = SSM_GROUPS * SSM_STATE
GMLP_HEADS = 8
GMLP_HEAD_DIM = 64
CHUNK = 128
PLE_DIM = 256
LN_EPS = 1e-5
ALPHA = 2.0 ** 0.25

ADAM_LR = 0.001
ADAM_B1 = 0.9
ADAM_B2 = 0.999
ADAM_EPS = 1e-08
ADAM_WD = 0.01
ADAM_STEP = 10
ADAM_C1 = 1.0 - ADAM_B1 ** ADAM_STEP
ADAM_C2 = 1.0 - ADAM_B2 ** ADAM_STEP

N_DEV = 8
VMEM_LIMIT_BYTES = 56 * 1024 * 1024
FFN_COLS = 1408
S5_BLOCKS = 4
S5_BLOCK_IN = D_SSM // S5_BLOCKS
S5_BLOCK_ST = SSM_LANES // S5_BLOCKS
SCAN_LANES = 512
TN_K_BLOCK = 2048
DIRECT_GATHER_BYTES = 0
LAST_PIECES = 2
LAST_PIECE = "ffn1_w_in_q%d"
_G0 = math.sqrt(2.0 / math.pi)
_G1 = 0.044715


def _dot(a, b):
    return jnp.dot(a, b, preferred_element_type=F32)


def _dot_nt(a, b):
    return lax.dot_general(a, b, (((1,), (1,)), ((), ())), preferred_element_type=F32)


def _dot_tn(a, b):
    return lax.dot_general(a, b, (((0,), (0,)), ((), ())), preferred_element_type=F32)


def _sigmoid(x):
    return 1.0 / (1.0 + jnp.exp(-x))


def _gelu(x):
    t = jnp.tanh(_G0 * (x + _G1 * x * x * x))
    return 0.5 * x * (1.0 + t)


def _gelu_grad(x):
    t = jnp.tanh(_G0 * (x + _G1 * x * x * x))
    return 0.5 * (1.0 + t) + 0.5 * x * (1.0 - t * t) * _G0 * (1.0 + 3.0 * _G1 * x * x)


def _ln_fwd(r, g, b):
    mu = jnp.mean(r, axis=-1, keepdims=True)
    d = r - mu
    var = jnp.mean(d * d, axis=-1, keepdims=True)
    rstd = lax.rsqrt(var + LN_EPS)
    xh = d * rstd
    return xh * g + b, xh, rstd


def _ln_bwd(dy, xh, rstd, g):
    dxh = dy * g
    m1 = jnp.mean(dxh, axis=-1, keepdims=True)
    m2 = jnp.mean(dxh * xh, axis=-1, keepdims=True)
    return rstd * (dxh - m1 - xh * m2)


def _resident(shape):
    nd = len(shape)
    return pl.BlockSpec(shape, lambda *_: (0,) * nd, pipeline_mode=pl.Buffered(1))


def _fixed(shape):
    nd = len(shape)
    return pl.BlockSpec(shape, lambda *_: (0,) * nd)


def _rows(tm, cols):
    return pl.BlockSpec((tm, cols), lambda i: (i, 0))


def _cols(rows, tm):
    return pl.BlockSpec((rows, tm), lambda i: (0, i))


def _params(sem):
    return pltpu.CompilerParams(dimension_semantics=sem, vmem_limit_bytes=VMEM_LIMIT_BYTES)


class _Exchange:
    def __init__(self, args, out_shape, sems, start, finish):
        self.args, self.out_shape, self.sems = list(args), list(out_shape), list(sems)
        self.start, self.finish = start, finish
        self.cuts = [(0, len(self.out_shape))]


def _call(body, name, grid, in_specs, out_specs, out_shape, args, scratch=(), sem=None, bg=None, aliases=None):
    aliases = {} if aliases is None else aliases
    in_specs, args = list(in_specs), list(args)
    fn = body
    if bg is not None:
        n_args = len(args)

        def fn(*refs):
            body(*refs[:n_args], *refs[n_args + 1:])

        in_specs.append(pl.BlockSpec(memory_space=pl.ANY))
        args.append(bg)
    res = pl.pallas_call(fn, name=name, grid=grid, out_shape=tuple(out_shape), in_specs=in_specs,
                         out_specs=tuple(out_specs), scratch_shapes=list(scratch),
                         input_output_aliases=aliases, compiler_params=_params(sem))(*args)
    return tuple(res), ()


def _run_exchange_on_sequencer(ex, name, collective_id):
    n_i, n_o = len(ex.args), len(ex.out_shape)

    def body(*refs):
        ins, outs, sems = refs[:n_i], refs[n_i:n_i + n_o], refs[n_i + n_o:]
        x, y, c = lax.axis_index("x"), lax.axis_index("y"), lax.axis_index("c")
        barrier = pltpu.get_barrier_semaphore()
        for peer in [(x, y, 1 - c), (1 - x, y, c), (x, 1 - y, c), (1 - x, 1 - y, c)]:
            pl.semaphore_signal(barrier, inc=1, device_id=peer, device_id_type=MESH)
        pl.semaphore_wait(barrier, 4)
        ex.start(ins, outs, sems)
        ex.finish(ins, outs, sems)

    return tuple(pl.kernel(body, out_type=tuple(ex.out_shape),
                           mesh=plsc.ScalarSubcoreMesh(axis_name="sequencer", num_cores=1),
                           scratch_types=list(ex.sems), name=name,
                           compiler_params=pltpu.CompilerParams(collective_id=collective_id))(*ex.args))


def _join(exchanges):
    cuts = []
    a = o = q = 0
    for e in exchanges:
        cuts.append((a, a + len(e.args), o, o + len(e.out_shape), q, q + len(e.sems)))
        a, o, q = cuts[-1][1], cuts[-1][3], cuts[-1][5]

    def start(ins, outs, sems):
        for e, (a0, a1, o0, o1, q0, q1) in zip(exchanges, cuts):
            e.start(ins[a0:a1], outs[o0:o1], sems[q0:q1])

    def finish(ins, outs, sems):
        for e, (a0, a1, o0, o1, q0, q1) in zip(exchanges, cuts):
            e.finish(ins[a0:a1], outs[o0:o1], sems[q0:q1])

    joined = _Exchange(sum((e.args for e in exchanges), []), sum((e.out_shape for e in exchanges), []),
                       sum((e.sems for e in exchanges), []), start, finish)
    joined.cuts = [(c[2], c[3]) for c in cuts]
    return joined


def _ffn_proj(x, w_in, tm, name, bg=None):
    t = x.shape[0]
    nch = D_FF // FFN_COLS

    def body(x_ref, win_ref, xbt_ref, h_ref, a_ref):
        xb = x_ref[...].astype(BF16)
        xbt_ref[...] = xb.T
        for k in range(nch):
            cg = slice(k * FFN_COLS, (k + 1) * FFN_COLS)
            cu = slice(D_FF + k * FFN_COLS, D_FF + (k + 1) * FFN_COLS)
            hg = _dot(xb, win_ref[k])
            hu = _dot(xb, win_ref[nch + k])
            h_ref[:, cg] = hg.astype(BF16)
            h_ref[:, cu] = hu.astype(BF16)
            a_ref[:, cg] = (hg * _sigmoid(hg) * hu).astype(BF16)

    return _call(
        body, name, (t // tm,),
        [_rows(tm, D_MODEL), _resident((2 * nch, D_MODEL, FFN_COLS))],
        (_cols(D_MODEL, tm), _rows(tm, 2 * D_FF), _rows(tm, D_FF)),
        (SDS((D_MODEL, t), BF16), SDS((t, 2 * D_FF), BF16), SDS((t, D_FF), BF16)),
        (x, w_in), sem=("parallel",), bg=bg)


def _ffn_out(x, a, w_out, g, b, tm, name, bg=None):
    t = x.shape[0]

    def body(x_ref, a_ref, wout_ref, g_ref, b_ref, xn_ref, xh_ref, rstd_ref):
        f = _dot(a_ref[...], wout_ref[...])
        y, xh, rstd = _ln_fwd(ALPHA * x_ref[...] + 0.5 * f, g_ref[...], b_ref[...])
        xn_ref[...] = y
        xh_ref[...] = xh
        rstd_ref[...] = rstd

    return _call(
        body, name, (t // tm,),
        [_rows(tm, D_MODEL), _rows(tm, D_FF), _resident((D_FF, D_MODEL)), _fixed((1, D_MODEL)), _fixed((1, D_MODEL))],
        (_rows(tm, D_MODEL), _rows(tm, D_MODEL), _rows(tm, 1)),
        (SDS((t, D_MODEL), F32), SDS((t, D_MODEL), F32), SDS((t, 1), F32)),
        (x, a, w_out, g, b), sem=("parallel",), bg=bg)


def _ffn_bwd(dxn, xh, rstd, h, w_in, w_out, g, tm, name, bg=None):
    t = dxn.shape[0]
    nch = D_FF // FFN_COLS

    def body(dxn_ref, xh_ref, rstd_ref, h_ref, win_ref, wout_ref, g_ref,
             dx_ref, dh_ref, df_ref, dg_ref, db_ref):
        @pl.when(pl.program_id(0) == 0)
        def _():
            dg_ref[...] = jnp.zeros_like(dg_ref)
            db_ref[...] = jnp.zeros_like(db_ref)

        dy = dxn_ref[...]
        xhv = xh_ref[...]
        dr = _ln_bwd(dy, xhv, rstd_ref[...], g_ref[...])
        dg_ref[...] += jnp.sum(dy * xhv, axis=0, keepdims=True)
        db_ref[...] += jnp.sum(dy, axis=0, keepdims=True)
        df = (0.5 * dr).astype(BF16)
        df_ref[...] = df
        dx = ALPHA * dr
        das = [_dot_nt(df, wout_ref[k * FFN_COLS:(k + 1) * FFN_COLS, :]) for k in range(nch)]
        for k in range(nch):
            cg = slice(k * FFN_COLS, (k + 1) * FFN_COLS)
            cu = slice(D_FF + k * FFN_COLS, D_FF + (k + 1) * FFN_COLS)
            hg = h_ref[:, cg].astype(F32)
            hu = h_ref[:, cu].astype(F32)
            sg = _sigmoid(hg)
            silu = hg * sg
            da = das[k]
            dhu = (da * silu).astype(BF16)
            dhg = (da * hu * (sg * (1.0 + hg * (1.0 - sg)))).astype(BF16)
            dh_ref[:, cg] = dhg
            dh_ref[:, cu] = dhu
            dx = dx + _dot_nt(dhg, win_ref[k]) + _dot_nt(dhu, win_ref[nch + k])
        dx_ref[...] = dx

    return _call(
        body, name, (t // tm,),
        [_rows(tm, D_MODEL), _rows(tm, D_MODEL), _rows(tm, 1), _rows(tm, 2 * D_FF),
         _resident((2 * nch, D_MODEL, FFN_COLS)), _resident((D_FF, D_MODEL)), _fixed((1, D_MODEL))],
        (_rows(tm, D_MODEL), _rows(tm, 2 * D_FF), _rows(tm, D_MODEL),
         _fixed((1, D_MODEL)), _fixed((1, D_MODEL))),
        (SDS((t, D_MODEL), F32), SDS((t, 2 * D_FF), BF16), SDS((t, D_MODEL), BF16),
         SDS((1, D_MODEL), F32), SDS((1, D_MODEL), F32)),
        (dxn, xh, rstd, h, w_in, w_out, g), sem=("arbitrary",), bg=bg)


def _tn_matmul(a, b, name, bm, bn, col_block=0, total_cols=None, prev=None, bg=None, a_cols=None, a_t=False):
    t, m = a.shape[::-1] if a_t else a.shape
    a_first = 0
    if a_cols is not None:
        a_first, m = a_cols[0], a_cols[1] * bm
    n = b.shape[1]
    total_cols = n if total_cols is None else total_cols
    bk = min(TN_K_BLOCK, t)
    nk = t // bk
    n_in = 2 if prev is None else 4

    def body(*refs):
        a_ref, b_ref = refs[0], refs[1]
        o_ref, ob_ref = refs[n_in], refs[n_in + 1]
        k = pl.program_id(2)

        @pl.when(k == 0)
        def _():
            o_ref[...] = jnp.zeros_like(o_ref)

        o_ref[...] += _dot(a_ref[...], b_ref[...]) if a_t else _dot_tn(a_ref[...], b_ref[...])

        @pl.when(k == nk - 1)
        def _():
            ob_ref[...] = o_ref[...].astype(BF16)

    a_spec = (pl.BlockSpec((bm, bk), lambda i, j, k: (i + a_first, k)) if a_t
              else pl.BlockSpec((bk, bm), lambda i, j, k: (k, i + a_first)))
    in_specs = [a_spec, pl.BlockSpec((bk, bn), lambda i, j, k: (k, j))]
    args = [a, b]
    aliases = {}
    if prev is not None:
        in_specs += [pl.BlockSpec(memory_space=pl.ANY), pl.BlockSpec(memory_space=pl.ANY)]
        args += list(prev)
        aliases = {2: 0, 3: 1}
        if any(bg is p for p in prev):
            bg = None
    out_spec = pl.BlockSpec((bm, bn), lambda i, j, k: (i, j + col_block))
    return _call(body, name, (m // bm, n // bn, nk), in_specs, (out_spec, out_spec),
                 (SDS((m, total_cols), F32), SDS((m, total_cols), BF16)), args,
                 sem=("parallel", "parallel", "arbitrary"), bg=bg, aliases=aliases)


def _mixin_fwd(x1, w, tm, bg=None):
    t = x1.shape[0]

    def body(x_ref, w_ref, xbt_ref, za_ref, zuv_ref, gab_ref):
        xb = x_ref[...].astype(BF16)
        xbt_ref[...] = xb.T
        za_ref[...] = _dot(xb, w_ref[:, 0:512]).astype(BF16)
        zuv_ref[...] = _dot(xb, w_ref[:, 512:1536]).astype(BF16)
        gab_ref[...] = _dot(xb, w_ref[:, 1536:3584]).astype(BF16)

    return _call(
        body, "mixin_fwd", (t // tm,),
        [_rows(tm, D_MODEL), _resident((D_MODEL, 3584))],
        (_cols(D_MODEL, tm), _rows(tm, 512), _rows(tm, 1024), _rows(tm, 2048)),
        (SDS((D_MODEL, t), BF16), SDS((t, 512), BF16), SDS((t, 1024), BF16), SDS((t, 2048), BF16)),
        (x1, w), sem=("parallel",), bg=bg)


def _mixin_bwd(dx1a, dza, dzuv, dgab, w, tm, bg=None):
    t = dx1a.shape[0]

    def body(d_ref, dza_ref, dzuv_ref, dgab_ref, w_ref, dx_ref):
        dx_ref[...] = (d_ref[...] + _dot_nt(dza_ref[...], w_ref[:, 0:512])
                       + _dot_nt(dzuv_ref[...], w_ref[:, 512:1536])
                       + _dot_nt(dgab_ref[...], w_ref[:, 1536:3584]))

    return _call(
        body, "mixin_bwd", (t // tm,),
        [_rows(tm, D_MODEL), _rows(tm, 512), _rows(tm, 1024), _rows(tm, 2048), _resident((D_MODEL, 3584))],
        (_rows(tm, D_MODEL),), (SDS((t, D_MODEL), F32),),
        (dx1a, dza, dzuv, dgab, w), sem=("parallel",), bg=bg)


def _unrolled(lo, hi, body, carry):
    for j in range(lo, hi):
        carry = body(j, carry)
    return carry


def _scan_fwd(hr_ref, hi_ref, a_ref, ap_ref, carry_ref, seg, cin_ref):
    for lc in range(SSM_LANES // SCAN_LANES):
        ls = slice(lc * SCAN_LANES, (lc + 1) * SCAN_LANES)
        a_r = jnp.broadcast_to(a_ref[0:1, ls], (8, SCAN_LANES))
        a_i = jnp.broadcast_to(a_ref[1:2, ls], (8, SCAN_LANES))

        def step(j, hc, ls=ls, a_r=a_r, a_i=a_i):
            h_r, h_i = hc
            rows = pl.ds(j * 8, 8)
            n_r = a_r * h_r - a_i * h_i + hr_ref[rows, ls]
            n_i = a_r * h_i + a_i * h_r + hi_ref[rows, ls]
            hr_ref[rows, ls] = n_r
            hi_ref[rows, ls] = n_i
            return n_r, n_i

        zero = jnp.zeros((8, SCAN_LANES), F32)
        f_r, f_i = _unrolled(0, seg, step, (zero, zero))
        c_r = carry_ref[0:1, ls]
        c_i = carry_ref[1:2, ls]
        p_r = ap_ref[0:1, ls]
        p_i = ap_ref[1:2, ls]
        rows_r, rows_i = [], []
        for s in range(8):
            rows_r.append(c_r)
            rows_i.append(c_i)
            c_r, c_i = (f_r[s:s + 1] + p_r * c_r - p_i * c_i,
                        f_i[s:s + 1] + p_r * c_i + p_i * c_r)
        carry_ref[0:1, ls] = c_r
        carry_ref[1:2, ls] = c_i
        cin_r = jnp.concatenate(rows_r, axis=0)
        cin_i = jnp.concatenate(rows_i, axis=0)
        if cin_ref is not None:
            cin_ref[0, :, ls] = cin_r
            cin_ref[1, :, ls] = cin_i

        def fix(j, cc, ls=ls, a_r=a_r, a_i=a_i):
            c_r, c_i = cc
            c_r, c_i = a_r * c_r - a_i * c_i, a_r * c_i + a_i * c_r
            rows = pl.ds(j * 8, 8)
            hr_ref[rows, ls] = hr_ref[rows, ls] + c_r
            hi_ref[rows, ls] = hi_ref[rows, ls] + c_i
            return c_r, c_i

        _unrolled(0, seg, fix, (cin_r, cin_i))


def _scan_bwd(gr_ref, gi_ref, hr_ref, hi_ref, cin_ref, a_ref, ap_ref, rcarry_ref, da_ref, seg):
    for lc in range(SSM_LANES // SCAN_LANES):
        ls = slice(lc * SCAN_LANES, (lc + 1) * SCAN_LANES)
        a_r = jnp.broadcast_to(a_ref[0:1, ls], (8, SCAN_LANES))
        a_i = jnp.broadcast_to(a_ref[1:2, ls], (8, SCAN_LANES))

        def step(t, gc, ls=ls, a_r=a_r, a_i=a_i):
            g_r, g_i = gc
            rows = pl.ds((seg - 1 - t) * 8, 8)
            n_r = gr_ref[rows, ls] + a_r * g_r + a_i * g_i
            n_i = gi_ref[rows, ls] + a_r * g_i - a_i * g_r
            gr_ref[rows, ls] = n_r
            gi_ref[rows, ls] = n_i
            return n_r, n_i

        zero = jnp.zeros((8, SCAN_LANES), F32)
        f_r, f_i = _unrolled(0, seg, step, (zero, zero))
        c_r = rcarry_ref[0:1, ls]
        c_i = rcarry_ref[1:2, ls]
        p_r = ap_ref[0:1, ls]
        p_i = ap_ref[1:2, ls]
        rows_r, rows_i = [None] * 8, [None] * 8
        for s in range(7, -1, -1):
            rows_r[s] = c_r
            rows_i[s] = c_i
            c_r, c_i = (f_r[s:s + 1] + p_r * c_r + p_i * c_i,
                        f_i[s:s + 1] + p_r * c_i - p_i * c_r)
        rcarry_ref[0:1, ls] = c_r
        rcarry_ref[1:2, ls] = c_i
        cin_r = jnp.concatenate(rows_r, axis=0)
        cin_i = jnp.concatenate(rows_i, axis=0)

        def fix_row(j_rows, hp_r, hp_i, cc, ls=ls, a_r=a_r, a_i=a_i):
            c_r, c_i, acc_r, acc_i = cc
            c_r, c_i = a_r * c_r + a_i * c_i, a_r * c_i - a_i * c_r
            g_r = gr_ref[j_rows, ls] + c_r
            g_i = gi_ref[j_rows, ls] + c_i
            gr_ref[j_rows, ls] = g_r
            gi_ref[j_rows, ls] = g_i
            acc_r = acc_r + g_r * hp_r + g_i * hp_i
            acc_i = acc_i + g_i * hp_r - g_r * hp_i
            return c_r, c_i, acc_r, acc_i

        def fix(t, cc, ls=ls, fix_row=fix_row):
            j = seg - 1 - t
            rows = pl.ds(j * 8, 8)
            prev = pl.ds((j - 1) * 8, 8)
            return fix_row(rows, hr_ref[prev, ls], hi_ref[prev, ls], cc)

        cc = _unrolled(0, seg - 1, fix, (cin_r, cin_i, zero, zero))
        _, _, acc_r, acc_i = fix_row(pl.ds(0, 8), cin_ref[0, :, ls], cin_ref[1, :, ls], cc)
        da_ref[0, :, ls] += acc_r
        da_ref[1, :, ls] += acc_i


def _s5_fwd(za, sp, bsz, seq, tb, bg=None):
    nb = seq // tb
    seg = tb // 8
    t = bsz * seq

    def body(za_ref, perm_ref, permt_ref, mre_ref, mim_ref, nre_ref, nim_ref, a_ref, ap_ref,
             dsk_ref, gw_ref, gb_ref, out_ref, outt_ref, y2_ref, car_ref, hr_ref, hi_ref, carry_ref):
        @pl.when(pl.program_id(1) == 0)
        def _():
            carry_ref[...] = jnp.zeros_like(carry_ref)

        car_ref[0] = carry_ref[...]
        up = _dot(perm_ref[...], za_ref[...])
        upb = up.astype(BF16)
        for bb in range(S5_BLOCKS):
            ub = upb[:, bb * S5_BLOCK_IN:(bb + 1) * S5_BLOCK_IN]
            st = slice(bb * S5_BLOCK_ST, (bb + 1) * S5_BLOCK_ST)
            hr_ref[:, st] = _dot(ub, mre_ref[bb])
            hi_ref[:, st] = _dot(ub, mim_ref[bb])
        _scan_fwd(hr_ref, hi_ref, a_ref, ap_ref, carry_ref, seg, None)
        ys = []
        for bb in range(S5_BLOCKS):
            st = slice(bb * S5_BLOCK_ST, (bb + 1) * S5_BLOCK_ST)
            ys.append(_dot(hr_ref[:, st].astype(BF16), nre_ref[bb])
                      - _dot(hi_ref[:, st].astype(BF16), nim_ref[bb]))
        y2 = jnp.concatenate(ys, axis=1) + dsk_ref[...] * up
        y2_ref[...] = y2
        y3 = _gelu(y2)
        gl = _dot(y3.astype(BF16), gw_ref[...]) + gb_ref[...]
        oa = y3 * _sigmoid(gl)
        out = _dot(permt_ref[...], oa.astype(BF16)).astype(BF16)
        out_ref[...] = out
        outt_ref[...] = out.T

    blk = pl.BlockSpec((tb, D_SSM), lambda b, j: (b * nb + j, 0))
    blk_t = pl.BlockSpec((D_SSM, tb), lambda b, j: (0, b * nb + j))
    m_shape = (S5_BLOCKS, S5_BLOCK_IN, S5_BLOCK_ST)
    n_shape = (S5_BLOCKS, S5_BLOCK_ST, S5_BLOCK_IN)
    return _call(
        body, "s5_fwd", (bsz, nb),
        [blk, _fixed((tb, tb)), _fixed((tb, tb)), _fixed(m_shape), _fixed(m_shape), _fixed(n_shape),
         _fixed(n_shape), _fixed((2, SSM_LANES)), _fixed((2, SSM_LANES)), _fixed((1, D_SSM)),
         _fixed((D_SSM, D_SSM)), _fixed((1, D_SSM))],
        (blk, blk_t, blk, pl.BlockSpec((1, 2, SSM_LANES), lambda b, j: (b * nb + j, 0, 0))),
        (SDS((t, D_SSM), BF16), SDS((D_SSM, t), BF16), SDS((t, D_SSM), F32), SDS((bsz * nb, 2, SSM_LANES), F32)),
        (za, sp["perm"], sp["permt"], sp["mre"], sp["mim"], sp["nre"], sp["nim"], sp["a"], sp["ap"],
         sp["dskip"], sp["glu_w"], sp["glu_b"]),
        scratch=[pltpu.VMEM((tb, SSM_LANES), F32), pltpu.VMEM((tb, SSM_LANES), F32),
                 pltpu.VMEM((2, SSM_LANES), F32)],
        sem=("arbitrary", "arbitrary"), bg=bg)


def _s5_bwd(za, y2p, doa, carries, sp, bsz, seq, tb, bg=None):
    nb = seq // tb
    seg = tb // 8
    t = bsz * seq

    def body(za_ref, y2_ref, doa_ref, car_ref, perm_ref, permt_ref, mre_ref, mim_ref, mtre_ref, mtim_ref,
             nre_ref, nim_ref, ntre_ref, ntim_ref, a_ref, ap_ref, dsk_ref, gw_ref, gwt_ref, gb_ref,
             dza_ref, dmr_ref, dmi_ref, dnr_ref, dni_ref, da_ref, ddsk_ref, dgw_ref, dgb_ref,
             hr_ref, hi_ref, gr_ref, gi_ref, cin_ref, carry_ref, rcarry_ref):
        first = jnp.logical_and(pl.program_id(0) == 0, pl.program_id(1) == 0)

        @pl.when(first)
        def _():
            for r in (dmr_ref, dmi_ref, dnr_ref, dni_ref, da_ref, ddsk_ref, dgw_ref, dgb_ref):
                r[...] = jnp.zeros_like(r)

        @pl.when(pl.program_id(1) == 0)
        def _():
            rcarry_ref[...] = jnp.zeros_like(rcarry_ref)

        carry_ref[...] = car_ref[0]
        perm = perm_ref[...]
        up = _dot(perm, za_ref[...])
        upb = up.astype(BF16)
        for bb in range(S5_BLOCKS):
            ub = upb[:, bb * S5_BLOCK_IN:(bb + 1) * S5_BLOCK_IN]
            st = slice(bb * S5_BLOCK_ST, (bb + 1) * S5_BLOCK_ST)
            hr_ref[:, st] = _dot(ub, mre_ref[bb])
            hi_ref[:, st] = _dot(ub, mim_ref[bb])
        _scan_fwd(hr_ref, hi_ref, a_ref, ap_ref, carry_ref, seg, cin_ref)

        y2 = y2_ref[...]
        y3 = _gelu(y2)
        y3b = y3.astype(BF16)
        sg = _sigmoid(_dot(y3b, gw_ref[...]) + gb_ref[...])
        d0 = doa_ref[...]
        d_hi = d0.astype(BF16)
        d1 = d0 - d_hi.astype(F32)
        d_mid = d1.astype(BF16)
        d_lo = (d1 - d_mid.astype(F32)).astype(BF16)
        doap = _dot(perm, d_hi) + _dot(perm, d_mid) + _dot(perm, d_lo)
        dgl = doap * y3 * sg * (1.0 - sg)
        dglb = dgl.astype(BF16)
        dy3 = doap * sg + _dot(dglb, gwt_ref[...])
        dgw_ref[...] += _dot_tn(y3b, dglb)
        dgb_ref[...] += jnp.sum(dgl, axis=0, keepdims=True)
        dy2 = dy3 * _gelu_grad(y2)
        ddsk_ref[...] += jnp.sum(dy2 * up, axis=0, keepdims=True)
        dyb = dy2.astype(BF16)
        for bb in range(S5_BLOCKS):
            dyc = dyb[:, bb * S5_BLOCK_IN:(bb + 1) * S5_BLOCK_IN]
            st = slice(bb * S5_BLOCK_ST, (bb + 1) * S5_BLOCK_ST)
            gr_ref[:, st] = _dot(dyc, ntre_ref[bb])
            gi_ref[:, st] = -_dot(dyc, ntim_ref[bb])
            dnr_ref[bb] += _dot_tn(hr_ref[:, st].astype(BF16), dyc)
            dni_ref[bb] += -_dot_tn(hi_ref[:, st].astype(BF16), dyc)
        _scan_bwd(gr_ref, gi_ref, hr_ref, hi_ref, cin_ref, a_ref, ap_ref, rcarry_ref, da_ref, seg)
        dus = []
        for bb in range(S5_BLOCKS):
            st = slice(bb * S5_BLOCK_ST, (bb + 1) * S5_BLOCK_ST)
            grb = gr_ref[:, st].astype(BF16)
            gib = gi_ref[:, st].astype(BF16)
            dus.append(_dot(grb, mtre_ref[bb]) + _dot(gib, mtim_ref[bb]))
            ub = upb[:, bb * S5_BLOCK_IN:(bb + 1) * S5_BLOCK_IN]
            dmr_ref[bb] += _dot_tn(ub, grb)
            dmi_ref[bb] += _dot_tn(ub, gib)
        du = jnp.concatenate(dus, axis=1) + dy2 * dsk_ref[...]
        dza_ref[...] = _dot(permt_ref[...], du.astype(BF16)).astype(BF16)

    def rev(b, j):
        return (b * nb + (nb - 1 - j), 0)

    blk = pl.BlockSpec((tb, D_SSM), rev)
    m_shape = (S5_BLOCKS, S5_BLOCK_IN, S5_BLOCK_ST)
    n_shape = (S5_BLOCKS, S5_BLOCK_ST, S5_BLOCK_IN)
    return _call(
        body, "s5_bwd", (bsz, nb),
        [blk, blk, blk, pl.BlockSpec((1, 2, SSM_LANES), lambda b, j: (b * nb + (nb - 1 - j), 0, 0)),
         _fixed((tb, tb)), _fixed((tb, tb)), _fixed(m_shape), _fixed(m_shape), _fixed(n_shape), _fixed(n_shape),
         _fixed(n_shape), _fixed(n_shape), _fixed(m_shape), _fixed(m_shape),
         _fixed((2, SSM_LANES)), _fixed((2, SSM_LANES)), _fixed((1, D_SSM)),
         _fixed((D_SSM, D_SSM)), _fixed((D_SSM, D_SSM)), _fixed((1, D_SSM))],
        (blk, _fixed(m_shape), _fixed(m_shape), _fixed(n_shape), _fixed(n_shape),
         _fixed((2, 8, SSM_LANES)), _fixed((1, D_SSM)), _fixed((D_SSM, D_SSM)), _fixed((1, D_SSM))),
        (SDS((t, D_SSM), BF16), SDS(m_shape, F32), SDS(m_shape, F32), SDS(n_shape, F32), SDS(n_shape, F32),
         SDS((2, 8, SSM_LANES), F32), SDS((1, D_SSM), F32), SDS((D_SSM, D_SSM), F32), SDS((1, D_SSM), F32)),
        (za, y2p, doa, carries, sp["perm"], sp["permt"], sp["mre"], sp["mim"], sp["mtre"], sp["mtim"],
         sp["nre"], sp["nim"], sp["ntre"], sp["ntim"], sp["a"], sp["ap"], sp["dskip"], sp["glu_w"],
         sp["glu_wt"], sp["glu_b"]),
        scratch=[pltpu.VMEM((tb, SSM_LANES), F32), pltpu.VMEM((tb, SSM_LANES), F32),
                 pltpu.VMEM((tb, SSM_LANES), F32), pltpu.VMEM((tb, SSM_LANES), F32),
                 pltpu.VMEM((2, 8, SSM_LANES), F32), pltpu.VMEM((2, SSM_LANES), F32),
                 pltpu.VMEM((2, SSM_LANES), F32)],
        sem=("arbitrary", "arbitrary"), bg=bg)


def _gmlp_spatial(ws_ref, vb):
    lane = lax.broadcasted_iota(jnp.int32, (CHUNK, 128), 1)
    parts = []
    for j in range(GMLP_HEADS // 2):
        vp = vb[:, 128 * j:128 * (j + 1)]
        parts.append(jnp.where(lane < GMLP_HEAD_DIM, _dot(ws_ref[2 * j], vp), _dot(ws_ref[2 * j + 1], vp)))
    return jnp.concatenate(parts, axis=1)


def _gmlp_fwd(zuv, ln_g, ln_b, wsm, bias, bg=None):
    t = zuv.shape[0]

    def body(z_ref, g_ref, b_ref, ws_ref, bias_ref, out_ref, outt_ref):
        u = _gelu(z_ref[:, 0:D_GMLP].astype(F32))
        v0 = _gelu(z_ref[:, D_GMLP:2 * D_GMLP].astype(F32))
        v, _, _ = _ln_fwd(v0, g_ref[...], b_ref[...])
        s = _gmlp_spatial(ws_ref, v.astype(BF16)) + bias_ref[...]
        out = (u * s).astype(BF16)
        out_ref[...] = out
        outt_ref[...] = out.T

    return _call(
        body, "gmlp_fwd", (t // CHUNK,),
        [_rows(CHUNK, 2 * D_GMLP), _fixed((1, D_GMLP)), _fixed((1, D_GMLP)),
         _fixed((GMLP_HEADS, CHUNK, CHUNK)), _fixed((CHUNK, D_GMLP))],
        (_rows(CHUNK, D_GMLP), _cols(D_GMLP, CHUNK)), (SDS((t, D_GMLP), BF16), SDS((D_GMLP, t), BF16)),
        (zuv, ln_g, ln_b, wsm, bias), sem=("parallel",), bg=bg)


def _gmlp_bwd(zuv, dgm, ln_g, ln_b, wsm, wsmt, bias, bg=None):
    t = zuv.shape[0]

    def body(z_ref, d_ref, g_ref, b_ref, ws_ref, wst_ref, bias_ref,
             dz_ref, dws_ref, dbias_ref, dg_ref, db_ref):
        @pl.when(pl.program_id(0) == 0)
        def _():
            for r in (dws_ref, dbias_ref, dg_ref, db_ref):
                r[...] = jnp.zeros_like(r)

        zu = z_ref[:, 0:D_GMLP].astype(F32)
        zv = z_ref[:, D_GMLP:2 * D_GMLP].astype(F32)
        u = _gelu(zu)
        v0 = _gelu(zv)
        gam = g_ref[...]
        v, vhat, rstd = _ln_fwd(v0, gam, b_ref[...])
        vb = v.astype(BF16)
        s = _gmlp_spatial(ws_ref, vb) + bias_ref[...]
        d = d_ref[...]
        dz_ref[:, 0:D_GMLP] = (d * s * _gelu_grad(zu)).astype(BF16)
        ds = d * u
        dbias_ref[...] += ds
        dsb = ds.astype(BF16)
        lane = lax.broadcasted_iota(jnp.int32, (CHUNK, 128), 1)
        tril = (lax.broadcasted_iota(jnp.int32, (CHUNK, CHUNK), 0)
                >= lax.broadcasted_iota(jnp.int32, (CHUNK, CHUNK), 1))
        zero_b = jnp.zeros((CHUNK, 128), BF16)
        parts = []
        for j in range(GMLP_HEADS // 2):
            dsp = dsb[:, 128 * j:128 * (j + 1)]
            vp = vb[:, 128 * j:128 * (j + 1)]
            parts.append(jnp.where(lane < GMLP_HEAD_DIM, _dot(wst_ref[2 * j], dsp),
                                   _dot(wst_ref[2 * j + 1], dsp)))
            lo = jnp.where(lane < GMLP_HEAD_DIM, dsp, zero_b)
            hi = jnp.where(lane < GMLP_HEAD_DIM, zero_b, dsp)
            dws_ref[2 * j] += jnp.where(tril, _dot_nt(lo, vp), 0.0)
            dws_ref[2 * j + 1] += jnp.where(tril, _dot_nt(hi, vp), 0.0)
        dv = jnp.concatenate(parts, axis=1)
        dg_ref[...] += jnp.sum(dv * vhat, axis=0, keepdims=True)
        db_ref[...] += jnp.sum(dv, axis=0, keepdims=True)
        dz_ref[:, D_GMLP:2 * D_GMLP] = (_ln_bwd(dv, vhat, rstd, gam) * _gelu_grad(zv)).astype(BF16)

    return _call(
        body, "gmlp_bwd", (t // CHUNK,),
        [_rows(CHUNK, 2 * D_GMLP), _rows(CHUNK, D_GMLP), _fixed((1, D_GMLP)), _fixed((1, D_GMLP)),
         _fixed((GMLP_HEADS, CHUNK, CHUNK)), _fixed((GMLP_HEADS, CHUNK, CHUNK)), _fixed((CHUNK, D_GMLP))],
        (_rows(CHUNK, 2 * D_GMLP), _fixed((GMLP_HEADS, CHUNK, CHUNK)), _fixed((CHUNK, D_GMLP)),
         _fixed((1, D_GMLP)), _fixed((1, D_GMLP))),
        (SDS((t, 2 * D_GMLP), BF16), SDS((GMLP_HEADS, CHUNK, CHUNK), F32), SDS((CHUNK, D_GMLP), F32),
         SDS((1, D_GMLP), F32), SDS((1, D_GMLP), F32)),
        (zuv, dgm, ln_g, ln_b, wsm, wsmt, bias), sem=("arbitrary",), bg=bg)


def _mixout_fwd(x1, s5o, gm, gab, ua, ub, wmo, g, b, tm, bg=None):
    t = x1.shape[0]

    def body(x_ref, s_ref, m_ref, gab_ref, ua_ref, ub_ref, wmo_ref, g_ref, b_ref,
             xn_ref, xh_ref, rstd_ref):
        ya = _dot(s_ref[...], ua_ref[...])
        yb = _dot(m_ref[...], ub_ref[...])
        mix = (_sigmoid(gab_ref[:, 0:D_MODEL].astype(F32)) * ya
               + _sigmoid(gab_ref[:, D_MODEL:2 * D_MODEL].astype(F32)) * yb)
        r = ALPHA * x_ref[...] + _dot(mix.astype(BF16), wmo_ref[...])
        y, xh, rstd = _ln_fwd(r, g_ref[...], b_ref[...])
        xn_ref[...] = y
        xh_ref[...] = xh
        rstd_ref[...] = rstd

    return _call(
        body, "mixout_fwd", (t // tm,),
        [_rows(tm, D_MODEL), _rows(tm, D_SSM), _rows(tm, D_GMLP), _rows(tm, 2 * D_MODEL),
         _resident((D_SSM, D_MODEL)), _resident((D_GMLP, D_MODEL)), _resident((D_MODEL, D_MODEL)),
         _fixed((1, D_MODEL)), _fixed((1, D_MODEL))],
        (_rows(tm, D_MODEL), _rows(tm, D_MODEL), _rows(tm, 1)),
        (SDS((t, D_MODEL), F32), SDS((t, D_MODEL), F32), SDS((t, 1), F32)),
        (x1, s5o, gm, gab, ua, ub, wmo, g, b), sem=("parallel",), bg=bg)


def _mixout_bwd(dx2, xh, rstd, s5o, gm, gab, ua, ub, wmo, g, tm, bg=None):
    t = dx2.shape[0]

    def body(d_ref, xh_ref, rstd_ref, s_ref, m_ref, gab_ref, ua_ref, ub_ref, wmo_ref, g_ref,
             dx1_ref, dmx_ref, mb_ref, dya_ref, dyb_ref, ds5_ref, dgm_ref, dgab_ref, dg_ref, db_ref):
        @pl.when(pl.program_id(0) == 0)
        def _():
            dg_ref[...] = jnp.zeros_like(dg_ref)
            db_ref[...] = jnp.zeros_like(db_ref)

        dy = d_ref[...]
        xhv = xh_ref[...]
        dr = _ln_bwd(dy, xhv, rstd_ref[...], g_ref[...])
        dg_ref[...] += jnp.sum(dy * xhv, axis=0, keepdims=True)
        db_ref[...] += jnp.sum(dy, axis=0, keepdims=True)
        dx1_ref[...] = ALPHA * dr
        drb = dr.astype(BF16)
        dmx_ref[...] = drb
        dm = _dot_nt(drb, wmo_ref[...])
        ya = _dot(s_ref[...], ua_ref[...])
        yb = _dot(m_ref[...], ub_ref[...])
        sa = _sigmoid(gab_ref[:, 0:D_MODEL].astype(F32))
        sb = _sigmoid(gab_ref[:, D_MODEL:2 * D_MODEL].astype(F32))
        mb_ref[...] = (sa * ya + sb * yb).astype(BF16).T
        dya = (dm * sa).astype(BF16)
        dyb = (dm * sb).astype(BF16)
        dya_ref[...] = dya
        dyb_ref[...] = dyb
        dgab_ref[:, 0:D_MODEL] = (dm * ya * sa * (1.0 - sa)).astype(BF16)
        dgab_ref[:, D_MODEL:2 * D_MODEL] = (dm * yb * sb * (1.0 - sb)).astype(BF16)
        ds5_ref[...] = _dot_nt(dya, ua_ref[...])
        dgm_ref[...] = _dot_nt(dyb, ub_ref[...])

    return _call(
        body, "mixout_bwd", (t // tm,),
        [_rows(tm, D_MODEL), _rows(tm, D_MODEL), _rows(tm, 1), _rows(tm, D_SSM), _rows(tm, D_GMLP),
         _rows(tm, 2 * D_MODEL), _resident((D_SSM, D_MODEL)), _resident((D_GMLP, D_MODEL)),
         _resident((D_MODEL, D_MODEL)), _fixed((1, D_MODEL))],
        (_rows(tm, D_MODEL), _rows(tm, D_MODEL), _cols(D_MODEL, tm), _rows(tm, D_MODEL),
         _rows(tm, D_MODEL), _rows(tm, D_SSM), _rows(tm, D_GMLP), _rows(tm, 2 * D_MODEL),
         _fixed((1, D_MODEL)), _fixed((1, D_MODEL))),
        (SDS((t, D_MODEL), F32), SDS((t, D_MODEL), BF16), SDS((D_MODEL, t), BF16),
         SDS((t, D_MODEL), BF16), SDS((t, D_MODEL), BF16), SDS((t, D_SSM), F32),
         SDS((t, D_GMLP), F32), SDS((t, 2 * D_MODEL), BF16),
         SDS((1, D_MODEL), F32), SDS((1, D_MODEL), F32)),
        (dx2, xh, rstd, s5o, gm, gab, ua, ub, wmo, g), sem=("arbitrary",), bg=bg)


def _ple_loss(x3, p, tgt, wpg, wpp, tm, bg=None):
    t = x3.shape[0]

    def body(x_ref, p_ref, t_ref, wpg_ref, wpp_ref, dx_ref, xb_ref, pb_ref, dq_ref, de_ref, loss_ref):
        @pl.when(pl.program_id(0) == 0)
        def _():
            loss_ref[...] = jnp.zeros_like(loss_ref)

        x3v = x_ref[...]
        xb = x3v.astype(BF16)
        pb = p_ref[...].astype(BF16)
        xb_ref[...] = xb.T
        pb_ref[...] = pb.T
        s = _sigmoid(_dot(xb, wpg_ref[...]))
        e = _dot(pb, wpp_ref[...])
        diff = x3v + s * e - t_ref[...]
        loss_ref[...] += jnp.sum(diff * diff, axis=0, keepdims=True)
        dout = diff * (1.0 / D_MODEL)
        de_ref[...] = (dout * s).astype(BF16)
        dq = (dout * e * s * (1.0 - s)).astype(BF16)
        dq_ref[...] = dq
        dx_ref[...] = dout + _dot_nt(dq, wpg_ref[...])

    return _call(
        body, "ple_loss", (t // tm,),
        [_rows(tm, D_MODEL), _rows(tm, PLE_DIM), _rows(tm, D_MODEL),
         _resident((D_MODEL, D_MODEL)), _resident((PLE_DIM, D_MODEL))],
        (_rows(tm, D_MODEL), _cols(D_MODEL, tm), _cols(PLE_DIM, tm), _rows(tm, D_MODEL),
         _rows(tm, D_MODEL), _fixed((1, D_MODEL))),
        (SDS((t, D_MODEL), F32), SDS((D_MODEL, t), BF16), SDS((PLE_DIM, t), BF16),
         SDS((t, D_MODEL), BF16), SDS((t, D_MODEL), BF16), SDS((1, D_MODEL), F32)),
        (x3, p, tgt, wpg, wpp), sem=("arbitrary",), bg=bg)


def _s5_discretise(lre, lim, log_dt, bre, bim):
    dt = jnp.exp(log_dt)[:, None]
    mag = jnp.exp(lre * dt)
    abr = mag * jnp.cos(lim * dt)
    abi = mag * jnp.sin(lim * dt)
    nr = abr - 1.0
    ni = abi
    den = lre * lre + lim * lim
    cr = ((nr * lre + ni * lim) / den)[..., None]
    ci = ((ni * lre - nr * lim) / den)[..., None]
    return abr, abi, cr * bre - ci * bim, cr * bim + ci * bre


def _block_diag_in(bb):
    v = bb.reshape(S5_BLOCKS, 8, SSM_STATE, SSM_GROUP_CH).transpose(0, 1, 3, 2)
    return jnp.einsum("bgip,gh->bgihp", v, jnp.eye(8, dtype=bb.dtype)).reshape(
        S5_BLOCKS, S5_BLOCK_IN, S5_BLOCK_ST)


def _block_diag_in_t(dm):
    v = dm.reshape(S5_BLOCKS, 8, SSM_GROUP_CH, 8, SSM_STATE)
    d = jnp.einsum("bgihp,gh->bgip", v, jnp.eye(8, dtype=dm.dtype))
    return d.transpose(0, 1, 3, 2).reshape(SSM_GROUPS, SSM_STATE, SSM_GROUP_CH)


def _block_diag_out(cc):
    v = cc.reshape(S5_BLOCKS, 8, SSM_GROUP_CH, SSM_STATE)
    return jnp.einsum("bgip,gh->bgphi", v, jnp.eye(8, dtype=cc.dtype)).reshape(
        S5_BLOCKS, S5_BLOCK_ST, S5_BLOCK_IN)


def _block_diag_out_t(dn):
    v = dn.reshape(S5_BLOCKS, 8, SSM_STATE, 8, SSM_GROUP_CH)
    d = jnp.einsum("bgphi,gh->bgip", v, jnp.eye(8, dtype=dn.dtype))
    return d.reshape(SSM_GROUPS, SSM_GROUP_CH, SSM_STATE)


def _s5_setup(lre, lim, log_dt, bre, bim, cre, cim, d_skip, glu_w, glu_b, tb):
    seg = tb // 8
    abr, abi, bbr, bbi = _s5_discretise(lre, lim, log_dt, bre, bim)
    pr, pi = abr, abi
    for _ in range(int(math.log2(seg))):
        pr, pi = pr * pr - pi * pi, 2.0 * pr * pi
    rows = jnp.arange(tb)
    src = (rows % 8) * seg + rows // 8
    perm = (src[:, None] == jnp.arange(tb)[None, :]).astype(BF16)
    mre = _block_diag_in(bbr)
    mim = _block_diag_in(bbi)
    nre = _block_diag_out(cre)
    nim = _block_diag_out(cim)
    return {
        "perm": perm, "permt": perm.T,
        "mre": mre.astype(BF16), "mim": mim.astype(BF16),
        "mtre": mre.transpose(0, 2, 1).astype(BF16), "mtim": mim.transpose(0, 2, 1).astype(BF16),
        "nre": nre.astype(BF16), "nim": nim.astype(BF16),
        "ntre": nre.transpose(0, 2, 1).astype(BF16), "ntim": nim.transpose(0, 2, 1).astype(BF16),
        "a": jnp.stack([abr.reshape(-1), abi.reshape(-1)]),
        "ap": jnp.stack([pr.reshape(-1), pi.reshape(-1)]),
        "dskip": d_skip.reshape(1, D_SSM), "glu_w": glu_w, "glu_wt": glu_w.T,
        "glu_b": glu_b.reshape(1, D_SSM),
    }


BIG = ("ffn1_w_in", "ffn1_w_out", "mix_w_in", "ssm_glu_w", "up_a", "up_b", "mix_w_out",
       "ffn2_w_in", "ffn2_w_out", "ple_w_proj", "ple_w_gate")
BIG_AXIS = {"ffn1_w_in": 1, "ffn1_w_out": 0, "mix_w_in": 1, "ssm_glu_w": 0, "up_a": 1, "up_b": 1,
            "mix_w_out": 0, "ffn2_w_in": 1, "ffn2_w_out": 0, "ple_w_proj": 1, "ple_w_gate": 0}
SHARD_MAJOR = 2
GATHER_AXIS = dict(BIG_AXIS, ffn1_w_in=SHARD_MAJOR, ffn2_w_in=SHARD_MAJOR)
GATHER_ORDER = (("ffn1_w_in",), ("ffn1_w_out",), ("mix_w_in",), ("ssm_glu_w", "up_a", "up_b", "mix_w_out"),
                ("ffn2_w_in",), ("ffn2_w_out", "ple_w_gate", "ple_w_proj"))
GATHER_FIRST_ID = 1
REDUCE_FIRST_ID = 7
SMALL = ("ln1_g", "ln1_b", "ssm_lambda_re", "ssm_lambda_im", "ssm_log_dt", "ssm_b_re", "ssm_b_im",
         "ssm_c_re", "ssm_c_im", "ssm_d", "ssm_glu_b", "gmlp_ln_g", "gmlp_ln_b", "gmlp_w_s",
         "gmlp_b_s", "ln2_g", "ln2_b", "ln3_g", "ln3_b")
SMALL_VIEW = {"ssm_b_re": (SSM_GROUPS, SSM_STATE * SSM_GROUP_CH), "ssm_b_im": (SSM_GROUPS, SSM_STATE * SSM_GROUP_CH)}


def _small_view(k, a):
    return a.reshape(SMALL_VIEW[k]) if k in SMALL_VIEW else a


def _place():
    return lax.axis_index("x"), lax.axis_index("y"), lax.axis_index("c")


def _other_chips(x, y):
    return [(1 - x, y), (x, 1 - y), (1 - x, 1 - y)]


def _window(ref, shard_shape, axis, chip, half):
    r, c = shard_shape
    hr = r // 2
    if axis == SHARD_MAJOR:
        return ref.at[chip] if half is None else ref.at[chip, pl.ds(half * hr, hr), :]
    if axis == 0:
        if half is None:
            return ref.at[pl.ds(chip * r, r), :]
        return ref.at[pl.ds(chip * r + half * hr, hr), :]
    if half is None:
        return ref.at[:, pl.ds(chip * c, c)]
    return ref.at[pl.ds(half * hr, hr), pl.ds(chip * c, c)]


def _gather_weights(shards, axes):
    n = len(shards)
    shapes = [s.shape for s in shards]
    full = [{0: (4 * r, c), 1: (r, 4 * c), SHARD_MAJOR: (4, r, c)}[ax] for (r, c), ax in zip(shapes, axes)]

    def remote(sems, i, k, src, dst, to):
        return pltpu.make_async_remote_copy(src_ref=src, dst_ref=dst, send_sem=sems[0].at[6 * i + k],
                                            recv_sem=sems[1].at[6 * i + k], device_id=to, device_id_type=MESH)

    def own_copies(ins, outs, sems):
        x, y, c = _place()
        me = 2 * x + y
        cps = []
        for i in range(n):
            hr = shapes[i][0] // 2
            mine = ins[i].at[pl.ds(c * hr, hr), :]
            for j, (cx, cy) in enumerate(_other_chips(x, y)):
                cps.append(remote(sems, i, j, mine, _window(outs[i], shapes[i], axes[i], me, c), (cx, cy, c)))
        local = [pltpu.make_async_copy(ins[i], _window(outs[i], shapes[i], axes[i], me, None), sems[2].at[i])
                 for i in range(n)]
        return cps, local

    def start(ins, outs, sems):
        cps, local = own_copies(ins, outs, sems)
        for cp in local + cps:
            cp.start()

    def finish(ins, outs, sems):
        x, y, c = _place()
        sibling = (x, y, 1 - c)
        passed = []
        for j, (cx, cy) in enumerate(_other_chips(x, y)):
            for i in range(n):
                w = _window(outs[i], shapes[i], axes[i], 2 * cx + cy, c)
                remote(sems, i, j, w, w, (cx, cy, c)).wait_recv()
                cp = remote(sems, i, 3 + j, w, w, sibling)
                cp.start()
                passed.append(cp)
        for j, (cx, cy) in enumerate(_other_chips(x, y)):
            for i in range(n):
                w = _window(outs[i], shapes[i], axes[i], 2 * cx + cy, 1 - c)
                remote(sems, i, 3 + j, w, w, sibling).wait_recv()
        cps, local = own_copies(ins, outs, sems)
        for cp in cps + passed:
            cp.wait_send()
        for cp in local:
            cp.wait()

    return _Exchange(shards, [SDS(f, BF16) for f in full],
                     [pltpu.SemaphoreType.DMA((6 * n,)), pltpu.SemaphoreType.DMA((6 * n,)),
                      pltpu.SemaphoreType.DMA((n,))], start, finish)


def _scatter_grads(parts, shapes, axes):
    n = len(parts)

    def copies(ins, outs, sems):
        x, y, c = _place()
        return [pltpu.make_async_remote_copy(
            src_ref=_window(ins[i], shapes[i], axes[i], 2 * cx + cy, None), dst_ref=outs[i].at[j],
            send_sem=sems[0].at[3 * i + j], recv_sem=sems[1].at[3 * i + j],
            device_id=(cx, cy, c), device_id_type=MESH)
            for i in range(n) for j, (cx, cy) in enumerate(_other_chips(x, y))]

    def start(ins, outs, sems):
        for cp in copies(ins, outs, sems):
            cp.start()

    def finish(ins, outs, sems):
        for cp in copies(ins, outs, sems):
            cp.wait()

    return _Exchange(parts, [SDS((3,) + tuple(s), BF16) for s in shapes],
                     [pltpu.SemaphoreType.DMA((3 * n,)), pltpu.SemaphoreType.DMA((3 * n,))], start, finish)


def _swap_halves(parts, shapes, axes):
    n = len(parts)

    def copies(ins, outs, sems):
        x, y, c = _place()
        cps = []
        for i in range(n):
            r, _ = shapes[i]
            hr = r // 2
            if axes[i] == 0:
                cps += [pltpu.make_async_remote_copy(
                    src_ref=ins[i].at[pl.ds(k * r + (1 - c) * hr, hr), :], dst_ref=outs[i].at[k],
                    send_sem=sems[0].at[i], recv_sem=sems[1].at[i], device_id=(x, y, 1 - c),
                    device_id_type=MESH) for k in range(4)]
            else:
                cps.append(pltpu.make_async_remote_copy(
                    src_ref=ins[i].at[pl.ds((1 - c) * hr, hr), :], dst_ref=outs[i],
                    send_sem=sems[0].at[i], recv_sem=sems[1].at[i], device_id=(x, y, 1 - c),
                    device_id_type=MESH))
        return cps

    def start(ins, outs, sems):
        for cp in copies(ins, outs, sems):
            cp.start()

    def finish(ins, outs, sems):
        x, y, c = _place()
        for i in range(n):
            pltpu.make_async_remote_copy(src_ref=outs[i], dst_ref=outs[i], send_sem=sems[0].at[i],
                                         recv_sem=sems[1].at[i], device_id=(x, y, 1 - c),
                                         device_id_type=MESH).wait()

    out = [SDS((4, r // 2, c), BF16) if ax == 0 else SDS((r // 2, 4 * c), BF16)
           for (r, c), ax in zip(shapes, axes)]
    return _Exchange(parts, out, [pltpu.SemaphoreType.DMA((n,)), pltpu.SemaphoreType.DMA((n,))], start, finish)


def _scatter_halves(pres, shapes):
    n = len(pres)

    def copies(ins, outs, sems):
        x, y, c = _place()
        return [pltpu.make_async_remote_copy(
            src_ref=ins[i].at[1 + j], dst_ref=outs[i].at[j], send_sem=sems[0].at[3 * i + j],
            recv_sem=sems[1].at[3 * i + j], device_id=(cx, cy, c), device_id_type=MESH)
            for i in range(n) for j, (cx, cy) in enumerate(_other_chips(x, y))]

    def start(ins, outs, sems):
        for cp in copies(ins, outs, sems):
            cp.start()

    def finish(ins, outs, sems):
        for cp in copies(ins, outs, sems):
            cp.wait()

    return _Exchange(pres, [SDS((3, r // 2, c), BF16) for r, c in shapes],
                     [pltpu.SemaphoreType.DMA((3 * n,)), pltpu.SemaphoreType.DMA((3 * n,))], start, finish)


def _swap_with_sibling(arrs):
    n = len(arrs)

    def copies(ins, outs, sems):
        x, y, c = _place()
        return [pltpu.make_async_remote_copy(src_ref=ins[i], dst_ref=outs[i], send_sem=sems[0].at[i],
                                             recv_sem=sems[1].at[i], device_id=(x, y, 1 - c),
                                             device_id_type=MESH) for i in range(n)]

    def start(ins, outs, sems):
        for cp in copies(ins, outs, sems):
            cp.start()

    def finish(ins, outs, sems):
        for cp in copies(ins, outs, sems):
            cp.wait()

    return _Exchange(arrs, [SDS(a.shape, a.dtype) for a in arrs],
                     [pltpu.SemaphoreType.DMA((n,)), pltpu.SemaphoreType.DMA((n,))], start, finish)


def _gather_small(arrs):
    n = len(arrs)

    def copy(sems, outs, i, k, block, to, src=None):
        px, py, pc = block
        dst = outs[i].at[4 * px + 2 * py + pc]
        return pltpu.make_async_remote_copy(
            src_ref=dst if src is None else src, dst_ref=dst, send_sem=sems[0].at[7 * i + k],
            recv_sem=sems[1].at[7 * i + k], device_id=to, device_id_type=MESH)

    direct = [math.prod(a.shape) * 4 <= DIRECT_GATHER_BYTES for a in arrs]

    def own_copies(ins, outs, sems):
        x, y, c = _place()
        cps = []
        for i in range(n):
            cps.append(copy(sems, outs, i, 0, (x, y, c), (x, y, 1 - c), src=ins[i]))
            for j, (cx, cy) in enumerate(_other_chips(x, y)):
                cps.append(copy(sems, outs, i, 1 + j, (x, y, c), (cx, cy, c), src=ins[i]))
                if direct[i]:
                    cps.append(copy(sems, outs, i, 4 + j, (x, y, c), (cx, cy, 1 - c), src=ins[i]))
        local = [pltpu.make_async_copy(ins[i], outs[i].at[4 * x + 2 * y + c], sems[2].at[i]) for i in range(n)]
        return cps, local

    def start(ins, outs, sems):
        cps, local = own_copies(ins, outs, sems)
        for cp in local + cps:
            cp.start()

    def finish(ins, outs, sems):
        x, y, c = _place()
        passed = []
        for j, (cx, cy) in enumerate(_other_chips(x, y)):
            for i in range(n):
                copy(sems, outs, i, 1 + j, (cx, cy, c), (x, y, c)).wait_recv()
                if not direct[i]:
                    cp = copy(sems, outs, i, 4 + j, (cx, cy, c), (x, y, 1 - c))
                    cp.start()
                    passed.append(cp)
        for i in range(n):
            copy(sems, outs, i, 0, (x, y, 1 - c), (x, y, c)).wait_recv()
            for j, (cx, cy) in enumerate(_other_chips(x, y)):
                copy(sems, outs, i, 4 + j, (cx, cy, 1 - c), (x, y, c)).wait_recv()
        cps, local = own_copies(ins, outs, sems)
        for cp in cps + passed:
            cp.wait_send()
        for cp in local:
            cp.wait()

    return _Exchange(arrs, [SDS((N_DEV,) + a.shape, F32) for a in arrs],
                     [pltpu.SemaphoreType.DMA((7 * n,)), pltpu.SemaphoreType.DMA((7 * n,)),
                      pltpu.SemaphoreType.DMA((n,))], start, finish)


def _local_step(x, p, tgt, wb, ws, shards=None, opt=None):
    bsz, seq, _ = x.shape
    t = bsz * seq
    tm = min(256, t)
    tb = min(256, seq)
    x0 = x.reshape(t, D_MODEL)
    p0 = p.reshape(t, PLE_DIM)
    tg = tgt.reshape(t, D_MODEL)
    row = lambda v: v.reshape(1, -1)
    dist = shards is not None
    wb = dict(wb)
    recv, sums, other, gathered = {}, {}, {}, {}
    gb = {}
    gs = {}
    shape_of, axis_of = {}, {}
    chip = None
    if dist:
        shape_of = {k: tuple(shards[k].shape) for k in BIG}
        axis_of = dict(BIG_AXIS)
        for q in range(LAST_PIECES):
            shape_of[LAST_PIECE % q] = (D_MODEL // LAST_PIECES, shape_of["ffn1_w_in"][1])
            axis_of[LAST_PIECE % q] = 1
        xi, yi, ci = _place()
        chip = (2 * xi + yi).astype(jnp.int32).reshape(1)
        ids = jnp.stack([2 * xi + yi] + [2 * cx + cy for cx, cy in _other_chips(xi, yi)] + [ci]).astype(jnp.int32)
    halfbuf, pre = {}, {}

    def gather(names):
        return _gather_weights([shards[k] for k in names], [GATHER_AXIS[k] for k in names]) if dist else None

    def exchange(scat=(), swap=(), halves=(), scat2=(), swap2=(), extra=None, after=None):
        if not dist:
            return None, []
        after = order[0] if after is None else after
        parts, tags = [], []
        if scat:
            parts.append(_scatter_grads([gb[k][1] for k in scat], [shape_of[k] for k in scat],
                                        [axis_of[k] for k in scat]))
            tags.append((recv, scat))
        if swap:
            for k in swap:
                sums[k] = order[0] = _sum_blocks(gb[k][0], recv[k], shape_of[k], axis_of[k], chip, "sum_" + k,
                                                 order[0])
            parts.append(_swap_with_sibling([sums[k] for k in swap]))
            tags.append((other, swap))
        if halves:
            parts.append(_swap_halves([gb[k][1] for k in halves], [shape_of[k] for k in halves],
                                      [axis_of[k] for k in halves]))
            tags.append((halfbuf, halves))
        if scat2:
            for k in scat2:
                pre[k] = _presum(gb[k][0], halfbuf[k], shape_of[k], axis_of[k], ids, "presum_" + k, order[0])
                order[0] = pre[k][0]
            parts.append(_scatter_halves([pre[k][1] for k in scat2], [shape_of[k] for k in scat2]))
            tags.append((recv, scat2))
        if swap2:
            for k in swap2:
                sums[k] = order[0] = _sum_half(pre[k][0], recv[k], "sum_" + k, order[0])
            parts.append(_swap_with_sibling([sums[k] for k in swap2]))
            tags.append((other, swap2))
        if extra is not None:
            parts.append(extra[0])
            tags.append((extra[1], extra[2]))
        return (_join(parts), tags) if parts else (None, [])

    def take(ex_tags, got):
        ex, tags = ex_tags
        if ex is not None:
            for (dst, names), (o0, o1) in zip(tags, ex.cuts):
                dst.update(zip(names, got[o0:o1]))

    order = [None]

    def ordered(builder, *args, **kw):
        res = builder(*args, bg=order[0] if dist else None, **kw)
        order[0] = res[0][0]
        return res

    launched = []

    def launch(ex_tags):
        if ex_tags[0] is not None:
            n = len(launched)
            launched.append(n)
            take(ex_tags, _run_exchange_on_sequencer(ex_tags[0], "reduce_%d" % n, REDUCE_FIRST_ID + n))

    small_shape = {k: _small_view(k, v).shape for k, v in ws.items()}
    small_shape["loss_rows"] = (1, D_MODEL)
    ws = {k: v if (v.ndim == 2 and k != "ssm_log_dt") else v[0] for k, v in ws.items()}
    tril = jnp.tril(jnp.ones((CHUNK, CHUNK), dtype=bool))
    wsm = jnp.where(tril[None], ws["gmlp_w_s"], 0.0)
    wsm_b = wsm.astype(BF16)
    wsmt_b = wsm.transpose(0, 2, 1).astype(BF16)
    bias = jnp.repeat(ws["gmlp_b_s"].T, GMLP_HEAD_DIM, axis=1)

    tf = min(512, t)
    if dist:
        for gi, names in enumerate(GATHER_ORDER):
            wb.update(zip(names, _run_exchange_on_sequencer(gather(names), "gather_%d" % gi, GATHER_FIRST_ID + gi)))
    (x0b, h1, a1), _ = _ffn_proj(x0, wb["ffn1_w_in"], tf, "ffn1_proj")
    (x1, xh1, rstd1), _ = _ffn_out(x0, a1, wb["ffn1_w_out"], row(ws["ln1_g"]), row(ws["ln1_b"]), tf, "ffn1_out")
    sp = _s5_setup(ws["ssm_lambda_re"], ws["ssm_lambda_im"], ws["ssm_log_dt"], ws["ssm_b_re"],
                   ws["ssm_b_im"], ws["ssm_c_re"], ws["ssm_c_im"], ws["ssm_d"], wb["ssm_glu_w"],
                   ws["ssm_glu_b"], tb)
    (x1b, za, zuv, gab), _ = _mixin_fwd(x1, wb["mix_w_in"], tm)
    (s5o, s5ot, y2p, carries), _ = _s5_fwd(za, sp, bsz, seq, tb)
    (gm, gmt), _ = _gmlp_fwd(zuv, row(ws["gmlp_ln_g"]), row(ws["gmlp_ln_b"]), wsm_b, bias)
    (x2, xh2, rstd2), _ = _mixout_fwd(x1, s5o, gm, gab, wb["up_a"], wb["up_b"], wb["mix_w_out"],
                                           row(ws["ln2_g"]), row(ws["ln2_b"]), tm)
    (x2b, h2, a2), _ = _ffn_proj(x2, wb["ffn2_w_in"], tf, "ffn2_proj")
    (x3, xh3, rstd3), _ = _ffn_out(x2, a2, wb["ffn2_w_out"], row(ws["ln3_g"]), row(ws["ln3_b"]), tf, "ffn2_out")
    (dx3, x3b, pb, dq, de, loss_rows), _ = _ple_loss(x3, p0, tg, wb["ple_w_gate"], wb["ple_w_proj"], tm)
    order[0] = dx3
    gb["ple_w_gate"], _ = ordered(_tn_matmul, x3b, dq, "dw_ple_gate", 1024, 1024, a_t=True)
    gb["ple_w_proj"], _ = ordered(_tn_matmul, pb, de, "dw_ple_proj", 256, 1024, a_t=True)
    launch(exchange(scat=("ple_w_gate", "ple_w_proj")))
    (dx2, dh2, df2, gs["ln3_g"], gs["ln3_b"]), _ = ordered(
        _ffn_bwd, dx3, xh3, rstd3, h2, wb["ffn2_w_in"], wb["ffn2_w_out"], row(ws["ln3_g"]), tm, "ffn2_bwd")
    gb["ffn2_w_out"], _ = ordered(_tn_matmul, a2, df2, "dw_ffn2_out", 1408, 1024)
    launch(exchange(scat=("ffn2_w_out",)))
    gb["ffn2_w_in"], _ = ordered(_tn_matmul, x2b, dh2, "dw_ffn2_in", 1024, 1408, a_t=True)
    launch(exchange(scat=("ffn2_w_in",), swap=("ple_w_gate", "ple_w_proj")))
    (dx1a, dmx, mb, dya, dyb, ds5, dgm, dgab, gs["ln2_g"], gs["ln2_b"]), _ = ordered(
        _mixout_bwd, dx2, xh2, rstd2, s5o, gm, gab, wb["up_a"], wb["up_b"], wb["mix_w_out"], row(ws["ln2_g"]), tm)
    gb["mix_w_out"], _ = ordered(_tn_matmul, mb, dmx, "dw_mix_out", 1024, 1024, a_t=True)
    gb["up_a"], _ = ordered(_tn_matmul, s5ot, dya, "dw_up_a", 512, 1024, a_t=True)
    gb["up_b"], _ = ordered(_tn_matmul, gmt, dyb, "dw_up_b", 512, 1024, a_t=True)
    launch(exchange(scat=("mix_w_out", "up_a", "up_b"), swap=("ffn2_w_out",)))
    (dza, dmr, dmi, dnr, dni, da, ddsk, dgw, dgb), _ = ordered(_s5_bwd, za, y2p, ds5, carries, sp, bsz, seq, tb)
    gb["ssm_glu_w"] = (dgw, dgw.astype(BF16))
    launch(exchange(scat=("ssm_glu_w",), swap=("ffn2_w_in",)))
    (dzuv, dws, dbias, gs["gmlp_ln_g"], gs["gmlp_ln_b"]), _ = ordered(
        _gmlp_bwd, zuv, dgm, row(ws["gmlp_ln_g"]), row(ws["gmlp_ln_b"]), wsm_b, wsmt_b, bias)
    (dx1,), _ = ordered(_mixin_bwd, dx1a, dza, dzuv, dgab, wb["mix_w_in"], tm)
    g_mi, _ = ordered(_tn_matmul, x1b, dza, "dw_mix_in_a", 1024, 512, 0, 3584, a_t=True)
    g_mi, _ = ordered(_tn_matmul, x1b, dzuv, "dw_mix_in_uv", 1024, 512, 1, 3584, g_mi, a_t=True)
    gb["mix_w_in"], _ = ordered(_tn_matmul, x1b, dgab, "dw_mix_in_g", 1024, 512, 3, 3584, g_mi, a_t=True)
    launch(exchange(swap=("mix_w_out", "up_a", "up_b", "ssm_glu_w")))

    d_abr = da[0].sum(axis=0).reshape(SSM_GROUPS, SSM_STATE)
    d_abi = da[1].sum(axis=0).reshape(SSM_GROUPS, SSM_STATE)
    _, vjp = jax.vjp(_s5_discretise, ws["ssm_lambda_re"], ws["ssm_lambda_im"], ws["ssm_log_dt"],
                     ws["ssm_b_re"], ws["ssm_b_im"])
    (gs["ssm_lambda_re"], gs["ssm_lambda_im"], gs["ssm_log_dt"], gs["ssm_b_re"], gs["ssm_b_im"]) = vjp(
        (d_abr, d_abi, _block_diag_in_t(dmr), _block_diag_in_t(dmi)))
    gs["ssm_c_re"] = _block_diag_out_t(dnr)
    gs["ssm_c_im"] = _block_diag_out_t(dni)
    gs["ssm_d"] = ddsk
    gs["ssm_glu_b"] = dgb
    gs["gmlp_w_s"] = dws
    gs["gmlp_b_s"] = dbias.reshape(CHUNK, GMLP_HEADS, GMLP_HEAD_DIM).sum(axis=-1).T
    gs["loss_rows"] = loss_rows

    def small_gather(names):
        return (_gather_small([gs[k].reshape(small_shape[k]) for k in names]), gathered, names) if dist else None

    late = ("ln1_g", "ln1_b")
    launch(exchange(scat=("mix_w_in",), extra=small_gather(tuple(k for k in SMALL + ("loss_rows",) if k not in late))))
    (dx0, dh1, df1, gs["ln1_g"], gs["ln1_b"]), _ = ordered(
        _ffn_bwd, dx1, xh1, rstd1, h1, wb["ffn1_w_in"], wb["ffn1_w_out"], row(ws["ln1_g"]), tm, "ffn1_bwd")
    grad_x = dx0.reshape(bsz, seq, D_MODEL)
    if not dist:
        gb["ffn1_w_out"], _ = _tn_matmul(a1, df1, "dw_ffn1_out", 1408, 1024)
        gb["ffn1_w_in"], _ = _tn_matmul(x0b, dh1, "dw_ffn1_in", 1024, 1408, a_t=True)
        return (loss_rows, grad_x, gb, {k: gs[k].reshape(small_shape[k]) for k in SMALL}, sums, other, gathered,
                None, {})
    launch(exchange(extra=small_gather(late)))
    gb["ffn1_w_out"], _ = ordered(_tn_matmul, a1, df1, "dw_ffn1_out", 1408, 1024)
    last = ["ffn1_w_out"] + [LAST_PIECE % q for q in range(LAST_PIECES)]
    fillers = (("ffn2_w_in", "mix_w_in", "ple_w_gate"),
               ("ffn2_w_out", "mix_w_out", "up_a", "up_b", "ssm_glu_w", "ple_w_proj"))
    out = {}
    for i in range(1, len(last) + 3):
        stage = lambda d: tuple(last[i - d:i - d + 1]) if 0 <= i - d < len(last) else ()
        launch(exchange(halves=stage(1), scat2=stage(2), swap2=stage(3), swap=("mix_w_in",) if i == 2 else ()))
        if i < len(last):
            gb[last[i]], _ = ordered(_tn_matmul, x0b, dh1, "dw_" + last[i], D_MODEL // LAST_PIECES, 1408,
                                     a_cols=(i - 1, 1), a_t=True)
        elif i - len(last) < len(fillers):
            for k in fillers[i - len(last)]:
                w, m, v = opt[k]
                out[k] = _adam_big(w, sums[k], other[k], m, v, "adam_" + k, after=order[0])
                order[0] = out[k][1]
    return loss_rows, grad_x, gb, gs, sums, other, gathered, ids, out


def _adamw(w, g, m, v):
    m = ADAM_B1 * m + (1.0 - ADAM_B1) * g
    v = ADAM_B2 * v + (1.0 - ADAM_B2) * (g * g)
    m_hat = m / ADAM_C1
    v_hat = v / ADAM_C2
    delta = -ADAM_LR * (m_hat / (jnp.sqrt(v_hat) + ADAM_EPS) + ADAM_WD * w)
    return delta, m, v


def _pinned(after):
    return ([pl.BlockSpec(memory_space=pl.ANY)], [after]) if after is not None else ([], [])


def _sum_blocks(part, recv, shape, axis, chip, name, after=None):
    r, c = shape
    rb = r // 8

    def body(chip_ref, p_ref, r_ref, *rest):
        rest[-1][...] = (p_ref[...] + r_ref[0].astype(F32) + r_ref[1].astype(F32) + r_ref[2].astype(F32))

    if axis == 0:
        own = pl.BlockSpec((rb, c), lambda i, k: (k[0] * 8 + i, 0))
    else:
        own = pl.BlockSpec((rb, c), lambda i, k: (i, k[0]))
    pin_specs, pin_args = _pinned(after)
    grid_spec = pltpu.PrefetchScalarGridSpec(
        num_scalar_prefetch=1, grid=(8,),
        in_specs=[own, pl.BlockSpec((3, rb, c), lambda i, k: (0, i, 0))] + pin_specs,
        out_specs=pl.BlockSpec((rb, c), lambda i, k: (i, 0)))
    return pl.pallas_call(body, name=name, out_shape=SDS((r, c), F32), grid_spec=grid_spec,
                          compiler_params=_params(("parallel",)))(chip, part, recv, *pin_args)


def _presum(part, half, shape, axis, ids, name, after=None):
    r, c = shape
    rb = r // 4

    def body(ids_ref, p_ref, h_ref, *rest):
        of_ref, ob_ref = rest[-2:]
        s = p_ref[...] + h_ref[...].astype(F32)
        ob_ref[...] = s.astype(BF16)

        @pl.when(pl.program_id(1) == 0)
        def _():
            of_ref[...] = s

    if axis == 0:
        p_spec = pl.BlockSpec((rb, c), lambda i, t, ids: (ids[t] * 4 + ids[4] * 2 + i, 0))
        h_spec = pl.BlockSpec((None, rb, c), lambda i, t, ids: (ids[t], i, 0))
    else:
        p_spec = pl.BlockSpec((rb, c), lambda i, t, ids: (ids[4] * 2 + i, ids[t]))
        h_spec = pl.BlockSpec((rb, c), lambda i, t, ids: (i, ids[t]))
    pin_specs, pin_args = _pinned(after)
    grid_spec = pltpu.PrefetchScalarGridSpec(
        num_scalar_prefetch=1, grid=(2, 4), in_specs=[p_spec, h_spec] + pin_specs,
        out_specs=(pl.BlockSpec((rb, c), lambda i, t, ids: (i, 0)),
                   pl.BlockSpec((None, rb, c), lambda i, t, ids: (t, i, 0))))
    return pl.pallas_call(body, name=name, out_shape=(SDS((r // 2, c), F32), SDS((4, r // 2, c), BF16)),
                          grid_spec=grid_spec,
                          compiler_params=_params(("parallel", "arbitrary")))(ids, part, half, *pin_args)


def _sum_half(pre, recv, name, after=None):
    hr, c = pre.shape
    rb = hr // 2

    def body(p_ref, r_ref, *rest):
        rest[-1][...] = (p_ref[...] + r_ref[0].astype(F32) + r_ref[1].astype(F32) + r_ref[2].astype(F32))

    spec = pl.BlockSpec((rb, c), lambda i: (i, 0))
    pin_specs, pin_args = _pinned(after)
    return pl.pallas_call(body, name=name, grid=(2,), out_shape=SDS((hr, c), F32),
                          in_specs=[spec, pl.BlockSpec((3, rb, c), lambda i: (0, i, 0))] + pin_specs,
                          out_specs=spec, compiler_params=_params(("parallel",)))(pre, recv, *pin_args)


def _adam_halves(w, mine, oth, m, v, ids, name, piece=0, prev=None):
    r, c = w.shape
    rb = mine.shape[0] // 2

    def body(ids_ref, w_ref, a_ref, b_ref, m_ref, v_ref, *rest):
        g_ref, d_ref, nm_ref, nv_ref = rest[-4:]
        g = jnp.where(pl.program_id(0) // 2 == ids_ref[4], a_ref[...], b_ref[...])
        g_ref[...] = g
        d_ref[...], nm_ref[...], nv_ref[...] = _adamw(w_ref[...], g, m_ref[...], v_ref[...])

    whole = pl.BlockSpec((rb, c), lambda i, ids: (i + 4 * piece, 0))
    part = pl.BlockSpec((rb, c), lambda i, ids: (i % 2, 0))
    in_specs = [whole, part, part, whole, whole]
    args = [w, mine, oth, m, v]
    aliases = {}
    if prev is not None:
        in_specs += [pl.BlockSpec(memory_space=pl.ANY)] * 4
        args += list(prev)
        aliases = {6: 0, 7: 1, 8: 2, 9: 3}
    grid_spec = pltpu.PrefetchScalarGridSpec(num_scalar_prefetch=1, grid=(4,), in_specs=in_specs,
                                             out_specs=(whole,) * 4)
    return pl.pallas_call(body, name=name, out_shape=tuple(SDS((r, c), F32) for _ in range(4)),
                          grid_spec=grid_spec, input_output_aliases=aliases,
                          compiler_params=_params(("parallel",)))(ids, *args)


def _adam_big(w, ga, gb, m, v, name, piece=0, prev=None, after=None):
    r, c = w.shape
    pr = ga.shape[0]
    steps = 8 if pr == r else 2
    rb = pr // steps
    off = piece * steps

    def body(w_ref, ga_ref, gb_ref, m_ref, v_ref, *rest):
        g_ref, d_ref, nm_ref, nv_ref = rest[-4:]
        g = ga_ref[...] + gb_ref[...]
        g_ref[...] = g
        d_ref[...], nm_ref[...], nv_ref[...] = _adamw(w_ref[...], g, m_ref[...], v_ref[...])

    whole = pl.BlockSpec((rb, c), lambda i: (i + off, 0))
    part = pl.BlockSpec((rb, c), lambda i: (i, 0))
    in_specs = [whole, part, part, whole, whole]
    args = [w, ga, gb, m, v]
    aliases = {}
    if prev is not None:
        in_specs += [pl.BlockSpec(memory_space=pl.ANY)] * 4
        args += list(prev)
        aliases = {5: 0, 6: 1, 7: 2, 8: 3}
    if after is not None:
        in_specs.append(pl.BlockSpec(memory_space=pl.ANY))
        args.append(after)
    return pl.pallas_call(
        body, name=name, grid=(steps,), out_shape=tuple(SDS((r, c), F32) for _ in range(4)),
        in_specs=in_specs, out_specs=(whole,) * 4, input_output_aliases=aliases,
        compiler_params=_params(("parallel",)),
    )(*args)


def _adam_small(ws, gathered, ms, vs):
    n = len(ws)

    def body(*refs):
        w_refs, g_refs, m_refs, v_refs = refs[:n], refs[n:2 * n], refs[2 * n:3 * n], refs[3 * n:4 * n]
        outs = refs[4 * n:]
        for i in range(n):
            g = g_refs[i][0]
            for d in range(1, N_DEV):
                g = g + g_refs[i][d]
            delta, nm, nv = _adamw(w_refs[i][...], g, m_refs[i][...], v_refs[i][...])
            outs[i][...] = g
            outs[n + i][...] = delta
            outs[2 * n + i][...] = nm
            outs[3 * n + i][...] = nv

    vmem = pl.BlockSpec(memory_space=pltpu.VMEM)
    shapes = [w.shape for w in ws]
    return pl.pallas_call(
        body, name="adam_small", out_shape=tuple(SDS(s, F32) for s in shapes * 4),
        in_specs=[vmem] * (4 * n), out_specs=tuple([vmem] * (4 * n)),
        compiler_params=pltpu.CompilerParams(vmem_limit_bytes=VMEM_LIMIT_BYTES),
    )(*ws, *gathered, *ms, *vs)


def _sum_loss(gathered):
    def body(g_ref, o_ref):
        tot = g_ref[0]
        for d in range(1, N_DEV):
            tot = tot + g_ref[d]
        o_ref[...] = (0.5 / D_MODEL) * jnp.sum(tot, axis=1, keepdims=True)

    vmem = pl.BlockSpec(memory_space=pltpu.VMEM)
    return pl.pallas_call(body, name="sum_loss", out_shape=SDS((1, 1), F32), in_specs=[vmem],
                          out_specs=vmem)(gathered)


def kernel(x, p, ffn1_w_in, ffn1_w_out, ln1_g, ln1_b, mix_w_in, ssm_lambda_re, ssm_lambda_im, ssm_log_dt, ssm_b_re, ssm_b_im, ssm_c_re, ssm_c_im, ssm_d, ssm_glu_w, ssm_glu_b, gmlp_ln_g, gmlp_ln_b, gmlp_w_s, gmlp_b_s, up_a, up_b, mix_w_out, ln2_g, ln2_b, ffn2_w_in, ffn2_w_out, ln3_g, ln3_b, ple_w_proj, ple_w_gate, loss_target, m_ffn1_w_in, m_ffn1_w_out, m_ln1_g, m_ln1_b, m_mix_w_in, m_ssm_lambda_re, m_ssm_lambda_im, m_ssm_log_dt, m_ssm_b_re, m_ssm_b_im, m_ssm_c_re, m_ssm_c_im, m_ssm_d, m_ssm_glu_w, m_ssm_glu_b, m_gmlp_ln_g, m_gmlp_ln_b, m_gmlp_w_s, m_gmlp_b_s, m_up_a, m_up_b, m_mix_w_out, m_ln2_g, m_ln2_b, m_ffn2_w_in, m_ffn2_w_out, m_ln3_g, m_ln3_b, m_ple_w_proj, m_ple_w_gate, v_ffn1_w_in, v_ffn1_w_out, v_ln1_g, v_ln1_b, v_mix_w_in, v_ssm_lambda_re, v_ssm_lambda_im, v_ssm_log_dt, v_ssm_b_re, v_ssm_b_im, v_ssm_c_re, v_ssm_c_im, v_ssm_d, v_ssm_glu_w, v_ssm_glu_b, v_gmlp_ln_g, v_gmlp_ln_b, v_gmlp_w_s, v_gmlp_b_s, v_up_a, v_up_b, v_mix_w_out, v_ln2_g, v_ln2_b, v_ffn2_w_in, v_ffn2_w_out, v_ln3_g, v_ln3_b, v_ple_w_proj, v_ple_w_gate):
    given = dict(locals())
    order = ("ffn1_w_in", "ffn1_w_out", "ln1_g", "ln1_b", "mix_w_in", "ssm_lambda_re", "ssm_lambda_im",
             "ssm_log_dt", "ssm_b_re", "ssm_b_im", "ssm_c_re", "ssm_c_im", "ssm_d", "ssm_glu_w", "ssm_glu_b",
             "gmlp_ln_g", "gmlp_ln_b", "gmlp_w_s", "gmlp_b_s", "up_a", "up_b", "mix_w_out", "ln2_g", "ln2_b",
             "ffn2_w_in", "ffn2_w_out", "ln3_g", "ln3_b", "ple_w_proj", "ple_w_gate")
    assert set(order) == set(BIG + SMALL)

    shard = {k: given[k][0] for k in BIG}
    shard_b = {k: shard[k].astype(BF16) for k in BIG}
    opt = {k: (shard[k], given["m_" + k][0], given["v_" + k][0]) for k in BIG}
    loss_rows, grad_x, gb, gs, sums, other, gathered, ids, out = _local_step(
        x, given["p"][0], loss_target, {}, {k: given[k] for k in SMALL}, shard_b, opt)

    out = dict(out)
    for k in BIG:
        if k in out:
            continue
        moments = (given["m_" + k][0], given["v_" + k][0])
        if k == "ffn1_w_out":
            out[k] = _adam_halves(shard[k], sums[k], other[k], *moments, ids, "adam_" + k)
        elif k == "ffn1_w_in":
            for q in range(LAST_PIECES):
                kq = LAST_PIECE % q
                out[k] = _adam_halves(shard[k], sums[kq], other[kq], *moments, ids, "adam_" + kq, q, out.get(k))
        else:
            out[k] = _adam_big(shard[k], sums[k], other[k], *moments, "adam_" + k,
                               after=gb[LAST_PIECE % (LAST_PIECES - 1)][0])

    res = _adam_small([_small_view(k, given[k]) for k in SMALL], [gathered[k] for k in SMALL],
                      [_small_view(k, given["m_" + k]) for k in SMALL],
                      [_small_view(k, given["v_" + k]) for k in SMALL])
    ns = len(SMALL)
    for i, k in enumerate(SMALL):
        out[k] = tuple(res[j * ns + i].reshape(given[k].shape) for j in range(4))
    loss = _sum_loss(gathered["loss_rows"]).reshape(())

    lead = lambda k, j: out[k][j][None] if k in BIG else out[k][j]
    return (loss, grad_x, *[lead(k, 0) for k in order], *[lead(k, 1) for k in order],
            *[lead(k, 2) for k in order], *[lead(k, 3) for k in order])
```

```python
import math

import jax
import jax.numpy as jnp
from jax import lax
from jax.experimental import pallas as pl
from jax.experimental.pallas import tpu as pltpu
from jax.experimental.pallas import tpu_sc as plsc

F32 = jnp.float32
BF16 = jnp.bfloat16
MESH = pl.DeviceIdType.MESH
SDS = jax.ShapeDtypeStruct

D_MODEL = 1024
D_FF = 2816
D_SSM = 512
D_GMLP = 512
SSM_GROUPS = 32
SSM_GROUP_CH = 16
SSM_STATE = 64
SSM_LANES = SSM_GROUPS * SSM_STATE
GMLP_HEADS = 8
GMLP_HEAD_DIM = 64
CHUNK = 128
PLE_DIM = 256
LN_EPS = 1e-5
ALPHA = 2.0 ** 0.25

ADAM_LR = 0.001
ADAM_B1 = 0.9
ADAM_B2 = 0.999
ADAM_EPS = 1e-08
ADAM_WD = 0.01
ADAM_STEP = 10
ADAM_C1 = 1.0 - ADAM_B1 ** ADAM_STEP
ADAM_C2 = 1.0 - ADAM_B2 ** ADAM_STEP

N_DEV = 8
VMEM_LIMIT_BYTES = 56 * 1024 * 1024
FFN_COLS = 1408
S5_BLOCKS = 4
S5_BLOCK_IN = D_SSM // S5_BLOCKS
S5_BLOCK_ST = SSM_LANES // S5_BLOCKS
SCAN_LANES = 256
TN_K_BLOCK = 2048
DIRECT_GATHER_BYTES = 0
LAST_PIECES = 2
LAST_PIECE = "ffn1_w_in_q%d"
_G0 = math.sqrt(2.0 / math.pi)
_G1 = 0.044715


def _dot(a, b):
    return jnp.dot(a, b, preferred_element_type=F32)


def _dot_nt(a, b):
    return lax.dot_general(a, b, (((1,), (1,)), ((), ())), preferred_element_type=F32)


def _dot_tn(a, b):
    return lax.dot_general(a, b, (((0,), (0,)), ((), ())), preferred_element_type=F32)


def _sigmoid(x):
    return 1.0 / (1.0 + jnp.exp(-x))


def _gelu(x):
    t = jnp.tanh(_G0 * (x + _G1 * x * x * x))
    return 0.5 * x * (1.0 + t)


def _gelu_grad(x):
    t = jnp.tanh(_G0 * (x + _G1 * x * x * x))
    return 0.5 * (1.0 + t) + 0.5 * x * (1.0 - t * t) * _G0 * (1.0 + 3.0 * _G1 * x * x)


def _ln_fwd(r, g, b):
    mu = jnp.mean(r, axis=-1, keepdims=True)
    d = r - mu
    var = jnp.mean(d * d, axis=-1, keepdims=True)
    rstd = lax.rsqrt(var + LN_EPS)
    xh = d * rstd
    return xh * g + b, xh, rstd


def _ln_bwd(dy, xh, rstd, g):
    dxh = dy * g
    m1 = jnp.mean(dxh, axis=-1, keepdims=True)
    m2 = jnp.mean(dxh * xh, axis=-1, keepdims=True)
    return rstd * (dxh - m1 - xh * m2)


def _resident(shape):
    nd = len(shape)
    return pl.BlockSpec(shape, lambda *_: (0,) * nd, pipeline_mode=pl.Buffered(1))


def _fixed(shape):
    nd = len(shape)
    return pl.BlockSpec(shape, lambda *_: (0,) * nd)


def _rows(tm, cols):
    return pl.BlockSpec((tm, cols), lambda i: (i, 0))


def _cols(rows, tm):
    return pl.BlockSpec((rows, tm), lambda i: (0, i))


def _params(sem):
    return pltpu.CompilerParams(dimension_semantics=sem, vmem_limit_bytes=VMEM_LIMIT_BYTES)


class _Exchange:
    def __init__(self, args, out_shape, sems, start, finish):
        self.args, self.out_shape, self.sems = list(args), list(out_shape), list(sems)
        self.start, self.finish = start, finish
        self.cuts = [(0, len(self.out_shape))]


def _call(body, name, grid, in_specs, out_specs, out_shape, args, scratch=(), sem=None, bg=None, aliases=None):
    aliases = {} if aliases is None else aliases
    in_specs, args = list(in_specs), list(args)
    fn = body
    if bg is not None:
        n_args = len(args)

        def fn(*refs):
            body(*refs[:n_args], *refs[n_args + 1:])

        in_specs.append(pl.BlockSpec(memory_space=pl.ANY))
        args.append(bg)
    res = pl.pallas_call(fn, name=name, grid=grid, out_shape=tuple(out_shape), in_specs=in_specs,
                         out_specs=tuple(out_specs), scratch_shapes=list(scratch),
                         input_output_aliases=aliases, compiler_params=_params(sem))(*args)
    return tuple(res), ()


def _run_exchange_on_sequencer(ex, name, collective_id):
    n_i, n_o = len(ex.args), len(ex.out_shape)

    def body(*refs):
        ins, outs, sems = refs[:n_i], refs[n_i:n_i + n_o], refs[n_i + n_o:]
        x, y, c = lax.axis_index("x"), lax.axis_index("y"), lax.axis_index("c")
        barrier = pltpu.get_barrier_semaphore()
        for peer in [(x, y, 1 - c), (1 - x, y, c), (x, 1 - y, c), (1 - x, 1 - y, c)]:
            pl.semaphore_signal(barrier, inc=1, device_id=peer, device_id_type=MESH)
        pl.semaphore_wait(barrier, 4)
        ex.start(ins, outs, sems)
        ex.finish(ins, outs, sems)

    return tuple(pl.kernel(body, out_type=tuple(ex.out_shape),
                           mesh=plsc.ScalarSubcoreMesh(axis_name="sequencer", num_cores=1),
                           scratch_types=list(ex.sems), name=name,
                           compiler_params=pltpu.CompilerParams(collective_id=collective_id))(*ex.args))


def _join(exchanges):
    cuts = []
    a = o = q = 0
    for e in exchanges:
        cuts.append((a, a + len(e.args), o, o + len(e.out_shape), q, q + len(e.sems)))
        a, o, q = cuts[-1][1], cuts[-1][3], cuts[-1][5]

    def start(ins, outs, sems):
        for e, (a0, a1, o0, o1, q0, q1) in zip(exchanges, cuts):
            e.start(ins[a0:a1], outs[o0:o1], sems[q0:q1])

    def finish(ins, outs, sems):
        for e, (a0, a1, o0, o1, q0, q1) in zip(exchanges, cuts):
            e.finish(ins[a0:a1], outs[o0:o1], sems[q0:q1])

    joined = _Exchange(sum((e.args for e in exchanges), []), sum((e.out_shape for e in exchanges), []),
                       sum((e.sems for e in exchanges), []), start, finish)
    joined.cuts = [(c[2], c[3]) for c in cuts]
    return joined


def _ffn_proj(x, w_in, tm, name, bg=None):
    t = x.shape[0]
    nch = D_FF // FFN_COLS

    def body(x_ref, win_ref, xbt_ref, h_ref, a_ref):
        xb = x_ref[...].astype(BF16)
        xbt_ref[...] = xb.T
        for k in range(nch):
            cg = slice(k * FFN_COLS, (k + 1) * FFN_COLS)
            cu = slice(D_FF + k * FFN_COLS, D_FF + (k + 1) * FFN_COLS)
            hg = _dot(xb, win_ref[k])
            hu = _dot(xb, win_ref[nch + k])
            h_ref[:, cg] = hg.astype(BF16)
            h_ref[:, cu] = hu.astype(BF16)
            a_ref[:, cg] = (hg * _sigmoid(hg) * hu).astype(BF16)

    return _call(
        body, name, (t // tm,),
        [_rows(tm, D_MODEL), _resident((2 * nch, D_MODEL, FFN_COLS))],
        (_cols(D_MODEL, tm), _rows(tm, 2 * D_FF), _rows(tm, D_FF)),
        (SDS((D_MODEL, t), BF16), SDS((t, 2 * D_FF), BF16), SDS((t, D_FF), BF16)),
        (x, w_in), sem=("parallel",), bg=bg)


def _ffn_out(x, a, w_out, g, b, tm, name, bg=None):
    t = x.shape[0]

    def body(x_ref, a_ref, wout_ref, g_ref, b_ref, xn_ref, xh_ref, rstd_ref):
        f = _dot(a_ref[...], wout_ref[...])
        y, xh, rstd = _ln_fwd(ALPHA * x_ref[...] + 0.5 * f, g_ref[...], b_ref[...])
        xn_ref[...] = y
        xh_ref[...] = xh
        rstd_ref[...] = rstd

    return _call(
        body, name, (t // tm,),
        [_rows(tm, D_MODEL), _rows(tm, D_FF), _resident((D_FF, D_MODEL)), _fixed((1, D_MODEL)), _fixed((1, D_MODEL))],
        (_rows(tm, D_MODEL), _rows(tm, D_MODEL), _rows(tm, 1)),
        (SDS((t, D_MODEL), F32), SDS((t, D_MODEL), F32), SDS((t, 1), F32)),
        (x, a, w_out, g, b), sem=("parallel",), bg=bg)


def _ffn_bwd(dxn, xh, rstd, h, w_in, w_out, g, tm, name, bg=None):
    t = dxn.shape[0]
    nch = D_FF // FFN_COLS

    def body(dxn_ref, xh_ref, rstd_ref, h_ref, win_ref, wout_ref, g_ref,
             dx_ref, dh_ref, df_ref, dg_ref, db_ref):
        @pl.when(pl.program_id(0) == 0)
        def _():
            dg_ref[...] = jnp.zeros_like(dg_ref)
            db_ref[...] = jnp.zeros_like(db_ref)

        dy = dxn_ref[...]
        xhv = xh_ref[...]
        dr = _ln_bwd(dy, xhv, rstd_ref[...], g_ref[...])
        dg_ref[...] += jnp.sum(dy * xhv, axis=0, keepdims=True)
        db_ref[...] += jnp.sum(dy, axis=0, keepdims=True)
        df = (0.5 * dr).astype(BF16)
        df_ref[...] = df
        dx = ALPHA * dr
        das = [_dot_nt(df, wout_ref[k * FFN_COLS:(k + 1) * FFN_COLS, :]) for k in range(nch)]
        for k in range(nch):
            cg = slice(k * FFN_COLS, (k + 1) * FFN_COLS)
            cu = slice(D_FF + k * FFN_COLS, D_FF + (k + 1) * FFN_COLS)
            hg = h_ref[:, cg].astype(F32)
            hu = h_ref[:, cu].astype(F32)
            sg = _sigmoid(hg)
            silu = hg * sg
            da = das[k]
            dhu = (da * silu).astype(BF16)
            dhg = (da * hu * (sg * (1.0 + hg * (1.0 - sg)))).astype(BF16)
            dh_ref[:, cg] = dhg
            dh_ref[:, cu] = dhu
            dx = dx + _dot_nt(dhg, win_ref[k]) + _dot_nt(dhu, win_ref[nch + k])
        dx_ref[...] = dx

    return _call(
        body, name, (t // tm,),
        [_rows(tm, D_MODEL), _rows(tm, D_MODEL), _rows(tm, 1), _rows(tm, 2 * D_FF),
         _resident((2 * nch, D_MODEL, FFN_COLS)), _resident((D_FF, D_MODEL)), _fixed((1, D_MODEL))],
        (_rows(tm, D_MODEL), _rows(tm, 2 * D_FF), _rows(tm, D_MODEL),
         _fixed((1, D_MODEL)), _fixed((1, D_MODEL))),
        (SDS((t, D_MODEL), F32), SDS((t, 2 * D_FF), BF16), SDS((t, D_MODEL), BF16),
         SDS((1, D_MODEL), F32), SDS((1, D_MODEL), F32)),
        (dxn, xh, rstd, h, w_in, w_out, g), sem=("arbitrary",), bg=bg)


def _tn_matmul(a, b, name, bm, bn, col_block=0, total_cols=None, prev=None, bg=None, a_cols=None, a_t=False):
    t, m = a.shape[::-1] if a_t else a.shape
    a_first = 0
    if a_cols is not None:
        a_first, m = a_cols[0], a_cols[1] * bm
    n = b.shape[1]
    total_cols = n if total_cols is None else total_cols
    bk = min(TN_K_BLOCK, t)
    nk = t // bk
    n_in = 2 if prev is None else 4

    def body(*refs):
        a_ref, b_ref = refs[0], refs[1]
        o_ref, ob_ref = refs[n_in], refs[n_in + 1]
        k = pl.program_id(2)

        @pl.when(k == 0)
        def _():
            o_ref[...] = jnp.zeros_like(o_ref)

        o_ref[...] += _dot(a_ref[...], b_ref[...]) if a_t else _dot_tn(a_ref[...], b_ref[...])

        @pl.when(k == nk - 1)
        def _():
            ob_ref[...] = o_ref[...].astype(BF16)

    a_spec = (pl.BlockSpec((bm, bk), lambda i, j, k: (i + a_first, k)) if a_t
              else pl.BlockSpec((bk, bm), lambda i, j, k: (k, i + a_first)))
    in_specs = [a_spec, pl.BlockSpec((bk, bn), lambda i, j, k: (k, j))]
    args = [a, b]
    aliases = {}
    if prev is not None:
        in_specs += [pl.BlockSpec(memory_space=pl.ANY), pl.BlockSpec(memory_space=pl.ANY)]
        args += list(prev)
        aliases = {2: 0, 3: 1}
        if any(bg is p for p in prev):
            bg = None
    out_spec = pl.BlockSpec((bm, bn), lambda i, j, k: (i, j + col_block))
    return _call(body, name, (m // bm, n // bn, nk), in_specs, (out_spec, out_spec),
                 (SDS((m, total_cols), F32), SDS((m, total_cols), BF16)), args,
                 sem=("parallel", "parallel", "arbitrary"), bg=bg, aliases=aliases)


def _mixin_fwd(x1, w, tm, bg=None):
    t = x1.shape[0]

    def body(x_ref, w_ref, xbt_ref, za_ref, zuv_ref, gab_ref):
        xb = x_ref[...].astype(BF16)
        xbt_ref[...] = xb.T
        za_ref[...] = _dot(xb, w_ref[:, 0:512]).astype(BF16)
        zuv_ref[...] = _dot(xb, w_ref[:, 512:1536]).astype(BF16)
        gab_ref[...] = _dot(xb, w_ref[:, 1536:3584]).astype(BF16)

    return _call(
        body, "mixin_fwd", (t // tm,),
        [_rows(tm, D_MODEL), _resident((D_MODEL, 3584))],
        (_cols(D_MODEL, tm), _rows(tm, 512), _rows(tm, 1024), _rows(tm, 2048)),
        (SDS((D_MODEL, t), BF16), SDS((t, 512), BF16), SDS((t, 1024), BF16), SDS((t, 2048), BF16)),
        (x1, w), sem=("parallel",), bg=bg)


def _mixin_bwd(dx1a, dza, dzuv, dgab, w, tm, bg=None):
    t = dx1a.shape[0]

    def body(d_ref, dza_ref, dzuv_ref, dgab_ref, w_ref, dx_ref):
        dx_ref[...] = (d_ref[...] + _dot_nt(dza_ref[...], w_ref[:, 0:512])
                       + _dot_nt(dzuv_ref[...], w_ref[:, 512:1536])
                       + _dot_nt(dgab_ref[...], w_ref[:, 1536:3584]))

    return _call(
        body, "mixin_bwd", (t // tm,),
        [_rows(tm, D_MODEL), _rows(tm, 512), _rows(tm, 1024), _rows(tm, 2048), _resident((D_MODEL, 3584))],
        (_rows(tm, D_MODEL),), (SDS((t, D_MODEL), F32),),
        (dx1a, dza, dzuv, dgab, w), sem=("parallel",), bg=bg)


def _unrolled(lo, hi, body, carry):
    for j in range(lo, hi):
        carry = body(j, carry)
    return carry


def _scan_fwd(hr_ref, hi_ref, a_ref, ap_ref, carry_ref, seg, cin_ref):
    for lc in range(SSM_LANES // SCAN_LANES):
        ls = slice(lc * SCAN_LANES, (lc + 1) * SCAN_LANES)
        a_r = jnp.broadcast_to(a_ref[0:1, ls], (8, SCAN_LANES))
        a_i = jnp.broadcast_to(a_ref[1:2, ls], (8, SCAN_LANES))

        def step(j, hc, ls=ls, a_r=a_r, a_i=a_i):
            h_r, h_i = hc
            rows = pl.ds(j * 8, 8)
            n_r = a_r * h_r - a_i * h_i + hr_ref[rows, ls]
            n_i = a_r * h_i + a_i * h_r + hi_ref[rows, ls]
            hr_ref[rows, ls] = n_r
            hi_ref[rows, ls] = n_i
            return n_r, n_i

        zero = jnp.zeros((8, SCAN_LANES), F32)
        f_r, f_i = _unrolled(0, seg, step, (zero, zero))
        c_r = carry_ref[0:1, ls]
        c_i = carry_ref[1:2, ls]
        p_r = ap_ref[0:1, ls]
        p_i = ap_ref[1:2, ls]
        rows_r, rows_i = [], []
        for s in range(8):
            rows_r.append(c_r)
            rows_i.append(c_i)
            c_r, c_i = (f_r[s:s + 1] + p_r * c_r - p_i * c_i,
                        f_i[s:s + 1] + p_r * c_i + p_i * c_r)
        carry_ref[0:1, ls] = c_r
        carry_ref[1:2, ls] = c_i
        cin_r = jnp.concatenate(rows_r, axis=0)
        cin_i = jnp.concatenate(rows_i, axis=0)
        if cin_ref is not None:
            cin_ref[0, :, ls] = cin_r
            cin_ref[1, :, ls] = cin_i

        def fix(j, cc, ls=ls, a_r=a_r, a_i=a_i):
            c_r, c_i = cc
            c_r, c_i = a_r * c_r - a_i * c_i, a_r * c_i + a_i * c_r
            rows = pl.ds(j * 8, 8)
            hr_ref[rows, ls] = hr_ref[rows, ls] + c_r
            hi_ref[rows, ls] = hi_ref[rows, ls] + c_i
            return c_r, c_i

        _unrolled(0, seg, fix, (cin_r, cin_i))


def _scan_bwd(gr_ref, gi_ref, hr_ref, hi_ref, cin_ref, a_ref, ap_ref, rcarry_ref, da_ref, seg):
    for lc in range(SSM_LANES // SCAN_LANES):
        ls = slice(lc * SCAN_LANES, (lc + 1) * SCAN_LANES)
        a_r = jnp.broadcast_to(a_ref[0:1, ls], (8, SCAN_LANES))
        a_i = jnp.broadcast_to(a_ref[1:2, ls], (8, SCAN_LANES))

        def step(t, gc, ls=ls, a_r=a_r, a_i=a_i):
            g_r, g_i = gc
            rows = pl.ds((seg - 1 - t) * 8, 8)
            n_r = gr_ref[rows, ls] + a_r * g_r + a_i * g_i
            n_i = gi_ref[rows, ls] + a_r * g_i - a_i * g_r
            gr_ref[rows, ls] = n_r
            gi_ref[rows, ls] = n_i
            return n_r, n_i

        zero = jnp.zeros((8, SCAN_LANES), F32)
        f_r, f_i = _unrolled(0, seg, step, (zero, zero))
        c_r = rcarry_ref[0:1, ls]
        c_i = rcarry_ref[1:2, ls]
        p_r = ap_ref[0:1, ls]
        p_i = ap_ref[1:2, ls]
        rows_r, rows_i = [None] * 8, [None] * 8
        for s in range(7, -1, -1):
            rows_r[s] = c_r
            rows_i[s] = c_i
            c_r, c_i = (f_r[s:s + 1] + p_r * c_r + p_i * c_i,
                        f_i[s:s + 1] + p_r * c_i - p_i * c_r)
        rcarry_ref[0:1, ls] = c_r
        rcarry_ref[1:2, ls] = c_i
        cin_r = jnp.concatenate(rows_r, axis=0)
        cin_i = jnp.concatenate(rows_i, axis=0)

        def fix_row(j_rows, hp_r, hp_i, cc, ls=ls, a_r=a_r, a_i=a_i):
            c_r, c_i, acc_r, acc_i = cc
            c_r, c_i = a_r * c_r + a_i * c_i, a_r * c_i - a_i * c_r
            g_r = gr_ref[j_rows, ls] + c_r
            g_i = gi_ref[j_rows, ls] + c_i
            gr_ref[j_rows, ls] = g_r
            gi_ref[j_rows, ls] = g_i
            acc_r = acc_r + g_r * hp_r + g_i * hp_i
            acc_i = acc_i + g_i * hp_r - g_r * hp_i
            return c_r, c_i, acc_r, acc_i

        def fix(t, cc, ls=ls, fix_row=fix_row):
            j = seg - 1 - t
            rows = pl.ds(j * 8, 8)
            prev = pl.ds((j - 1) * 8, 8)
            return fix_row(rows, hr_ref[prev, ls], hi_ref[prev, ls], cc)

        cc = _unrolled(0, seg - 1, fix, (cin_r, cin_i, zero, zero))
        _, _, acc_r, acc_i = fix_row(pl.ds(0, 8), cin_ref[0, :, ls], cin_ref[1, :, ls], cc)
        da_ref[0, :, ls] += acc_r
        da_ref[1, :, ls] += acc_i


def _s5_fwd(za, sp, bsz, seq, tb, bg=None):
    nb = seq // tb
    seg = tb // 8
    t = bsz * seq

    def body(za_ref, perm_ref, permt_ref, mre_ref, mim_ref, nre_ref, nim_ref, a_ref, ap_ref,
             dsk_ref, gw_ref, gb_ref, out_ref, outt_ref, y2_ref, car_ref, hr_ref, hi_ref, carry_ref):
        @pl.when(pl.program_id(1) == 0)
        def _():
            carry_ref[...] = jnp.zeros_like(carry_ref)

        car_ref[0] = carry_ref[...]
        up = _dot(perm_ref[...], za_ref[...])
        upb = up.astype(BF16)
        for bb in range(S5_BLOCKS):
            ub = upb[:, bb * S5_BLOCK_IN:(bb + 1) * S5_BLOCK_IN]
            st = slice(bb * S5_BLOCK_ST, (bb + 1) * S5_BLOCK_ST)
            hr_ref[:, st] = _dot(ub, mre_ref[bb])
            hi_ref[:, st] = _dot(ub, mim_ref[bb])
        _scan_fwd(hr_ref, hi_ref, a_ref, ap_ref, carry_ref, seg, None)
        ys = []
        for bb in range(S5_BLOCKS):
            st = slice(bb * S5_BLOCK_ST, (bb + 1) * S5_BLOCK_ST)
            ys.append(_dot(hr_ref[:, st].astype(BF16), nre_ref[bb])
                      - _dot(hi_ref[:, st].astype(BF16), nim_ref[bb]))
        y2 = jnp.concatenate(ys, axis=1) + dsk_ref[...] * up
        y2_ref[...] = y2
        y3 = _gelu(y2)
        gl = _dot(y3.astype(BF16), gw_ref[...]) + gb_ref[...]
        oa = y3 * _sigmoid(gl)
        out = _dot(permt_ref[...], oa.astype(BF16)).astype(BF16)
        out_ref[...] = out
        outt_ref[...] = out.T

    blk = pl.BlockSpec((tb, D_SSM), lambda b, j: (b * nb + j, 0))
    blk_t = pl.BlockSpec((D_SSM, tb), lambda b, j: (0, b * nb + j))
    m_shape = (S5_BLOCKS, S5_BLOCK_IN, S5_BLOCK_ST)
    n_shape = (S5_BLOCKS, S5_BLOCK_ST, S5_BLOCK_IN)
    return _call(
        body, "s5_fwd", (bsz, nb),
        [blk, _fixed((tb, tb)), _fixed((tb, tb)), _fixed(m_shape), _fixed(m_shape), _fixed(n_shape),
         _fixed(n_shape), _fixed((2, SSM_LANES)), _fixed((2, SSM_LANES)), _fixed((1, D_SSM)),
         _fixed((D_SSM, D_SSM)), _fixed((1, D_SSM))],
        (blk, blk_t, blk, pl.BlockSpec((1, 2, SSM_LANES), lambda b, j: (b * nb + j, 0, 0))),
        (SDS((t, D_SSM), BF16), SDS((D_SSM, t), BF16), SDS((t, D_SSM), F32), SDS((bsz * nb, 2, SSM_LANES), F32)),
        (za, sp["perm"], sp["permt"], sp["mre"], sp["mim"], sp["nre"], sp["nim"], sp["a"], sp["ap"],
         sp["dskip"], sp["glu_w"], sp["glu_b"]),
        scratch=[pltpu.VMEM((tb, SSM_LANES), F32), pltpu.VMEM((tb, SSM_LANES), F32),
                 pltpu.VMEM((2, SSM_LANES), F32)],
        sem=("arbitrary", "arbitrary"), bg=bg)


def _s5_bwd(za, y2p, doa, carries, sp, bsz, seq, tb, bg=None):
    nb = seq // tb
    seg = tb // 8
    t = bsz * seq

    def body(za_ref, y2_ref, doa_ref, car_ref, perm_ref, permt_ref, mre_ref, mim_ref, mtre_ref, mtim_ref,
             nre_ref, nim_ref, ntre_ref, ntim_ref, a_ref, ap_ref, dsk_ref, gw_ref, gwt_ref, gb_ref,
             dza_ref, dmr_ref, dmi_ref, dnr_ref, dni_ref, da_ref, ddsk_ref, dgw_ref, dgb_ref,
             hr_ref, hi_ref, gr_ref, gi_ref, cin_ref, carry_ref, rcarry_ref):
        first = jnp.logical_and(pl.program_id(0) == 0, pl.program_id(1) == 0)

        @pl.when(first)
        def _():
            for r in (dmr_ref, dmi_ref, dnr_ref, dni_ref, da_ref, ddsk_ref, dgw_ref, dgb_ref):
                r[...] = jnp.zeros_like(r)

        @pl.when(pl.program_id(1) == 0)
        def _():
            rcarry_ref[...] = jnp.zeros_like(rcarry_ref)

        carry_ref[...] = car_ref[0]
        perm = perm_ref[...]
        up = _dot(perm, za_ref[...])
        upb = up.astype(BF16)
        for bb in range(S5_BLOCKS):
            ub = upb[:, bb * S5_BLOCK_IN:(bb + 1) * S5_BLOCK_IN]
            st = slice(bb * S5_BLOCK_ST, (bb + 1) * S5_BLOCK_ST)
            hr_ref[:, st] = _dot(ub, mre_ref[bb])
            hi_ref[:, st] = _dot(ub, mim_ref[bb])
        _scan_fwd(hr_ref, hi_ref, a_ref, ap_ref, carry_ref, seg, cin_ref)

        y2 = y2_ref[...]
        y3 = _gelu(y2)
        y3b = y3.astype(BF16)
        sg = _sigmoid(_dot(y3b, gw_ref[...]) + gb_ref[...])
        d0 = doa_ref[...]
        d_hi = d0.astype(BF16)
        d1 = d0 - d_hi.astype(F32)
        d_mid = d1.astype(BF16)
        d_lo = (d1 - d_mid.astype(F32)).astype(BF16)
        doap = _dot(perm, d_hi) + _dot(perm, d_mid) + _dot(perm, d_lo)
        dgl = doap * y3 * sg * (1.0 - sg)
        dglb = dgl.astype(BF16)
        dy3 = doap * sg + _dot(dglb, gwt_ref[...])
        dgw_ref[...] += _dot_tn(y3b, dglb)
        dgb_ref[...] += jnp.sum(dgl, axis=0, keepdims=True)
        dy2 = dy3 * _gelu_grad(y2)
        ddsk_ref[...] += jnp.sum(dy2 * up, axis=0, keepdims=True)
        dyb = dy2.astype(BF16)
        for bb in range(S5_BLOCKS):
            dyc = dyb[:, bb * S5_BLOCK_IN:(bb + 1) * S5_BLOCK_IN]
            st = slice(bb * S5_BLOCK_ST, (bb + 1) * S5_BLOCK_ST)
            gr_ref[:, st] = _dot(dyc, ntre_ref[bb])
            gi_ref[:, st] = -_dot(dyc, ntim_ref[bb])
            dnr_ref[bb] += _dot_tn(hr_ref[:, st].astype(BF16), dyc)
            dni_ref[bb] += -_dot_tn(hi_ref[:, st].astype(BF16), dyc)
        _scan_bwd(gr_ref, gi_ref, hr_ref, hi_ref, cin_ref, a_ref, ap_ref, rcarry_ref, da_ref, seg)
        dus = []
        for bb in range(S5_BLOCKS):
            st = slice(bb * S5_BLOCK_ST, (bb + 1) * S5_BLOCK_ST)
            grb = gr_ref[:, st].astype(BF16)
            gib = gi_ref[:, st].astype(BF16)
            dus.append(_dot(grb, mtre_ref[bb]) + _dot(gib, mtim_ref[bb]))
            ub = upb[:, bb * S5_BLOCK_IN:(bb + 1) * S5_BLOCK_IN]
            dmr_ref[bb] += _dot_tn(ub, grb)
            dmi_ref[bb] += _dot_tn(ub, gib)
        du = jnp.concatenate(dus, axis=1) + dy2 * dsk_ref[...]
        dza_ref[...] = _dot(permt_ref[...], du.astype(BF16)).astype(BF16)

    def rev(b, j):
        return (b * nb + (nb - 1 - j), 0)

    blk = pl.BlockSpec((tb, D_SSM), rev)
    m_shape = (S5_BLOCKS, S5_BLOCK_IN, S5_BLOCK_ST)
    n_shape = (S5_BLOCKS, S5_BLOCK_ST, S5_BLOCK_IN)
    return _call(
        body, "s5_bwd", (bsz, nb),
        [blk, blk, blk, pl.BlockSpec((1, 2, SSM_LANES), lambda b, j: (b * nb + (nb - 1 - j), 0, 0)),
         _fixed((tb, tb)), _fixed((tb, tb)), _fixed(m_shape), _fixed(m_shape), _fixed(n_shape), _fixed(n_shape),
         _fixed(n_shape), _fixed(n_shape), _fixed(m_shape), _fixed(m_shape),
         _fixed((2, SSM_LANES)), _fixed((2, SSM_LANES)), _fixed((1, D_SSM)),
         _fixed((D_SSM, D_SSM)), _fixed((D_SSM, D_SSM)), _fixed((1, D_SSM))],
        (blk, _fixed(m_shape), _fixed(m_shape), _fixed(n_shape), _fixed(n_shape),
         _fixed((2, 8, SSM_LANES)), _fixed((1, D_SSM)), _fixed((D_SSM, D_SSM)), _fixed((1, D_SSM))),
        (SDS((t, D_SSM), BF16), SDS(m_shape, F32), SDS(m_shape, F32), SDS(n_shape, F32), SDS(n_shape, F32),
         SDS((2, 8, SSM_LANES), F32), SDS((1, D_SSM), F32), SDS((D_SSM, D_SSM), F32), SDS((1, D_SSM), F32)),
        (za, y2p, doa, carries, sp["perm"], sp["permt"], sp["mre"], sp["mim"], sp["mtre"], sp["mtim"],
         sp["nre"], sp["nim"], sp["ntre"], sp["ntim"], sp["a"], sp["ap"], sp["dskip"], sp["glu_w"],
         sp["glu_wt"], sp["glu_b"]),
        scratch=[pltpu.VMEM((tb, SSM_LANES), F32), pltpu.VMEM((tb, SSM_LANES), F32),
                 pltpu.VMEM((tb, SSM_LANES), F32), pltpu.VMEM((tb, SSM_LANES), F32),
                 pltpu.VMEM((2, 8, SSM_LANES), F32), pltpu.VMEM((2, SSM_LANES), F32),
                 pltpu.VMEM((2, SSM_LANES), F32)],
        sem=("arbitrary", "arbitrary"), bg=bg)


def _gmlp_spatial(ws_ref, vb):
    lane = lax.broadcasted_iota(jnp.int32, (CHUNK, 128), 1)
    parts = []
    for j in range(GMLP_HEADS // 2):
        vp = vb[:, 128 * j:128 * (j + 1)]
        parts.append(jnp.where(lane < GMLP_HEAD_DIM, _dot(ws_ref[2 * j], vp), _dot(ws_ref[2 * j + 1], vp)))
    return jnp.concatenate(parts, axis=1)


def _gmlp_fwd(zuv, ln_g, ln_b, wsm, bias, bg=None):
    t = zuv.shape[0]

    def body(z_ref, g_ref, b_ref, ws_ref, bias_ref, out_ref, outt_ref):
        u = _gelu(z_ref[:, 0:D_GMLP].astype(F32))
        v0 = _gelu(z_ref[:, D_GMLP:2 * D_GMLP].astype(F32))
        v, _, _ = _ln_fwd(v0, g_ref[...], b_ref[...])
        s = _gmlp_spatial(ws_ref, v.astype(BF16)) + bias_ref[...]
        out = (u * s).astype(BF16)
        out_ref[...] = out
        outt_ref[...] = out.T

    return _call(
        body, "gmlp_fwd", (t // CHUNK,),
        [_rows(CHUNK, 2 * D_GMLP), _fixed((1, D_GMLP)), _fixed((1, D_GMLP)),
         _fixed((GMLP_HEADS, CHUNK, CHUNK)), _fixed((CHUNK, D_GMLP))],
        (_rows(CHUNK, D_GMLP), _cols(D_GMLP, CHUNK)), (SDS((t, D_GMLP), BF16), SDS((D_GMLP, t), BF16)),
        (zuv, ln_g, ln_b, wsm, bias), sem=("parallel",), bg=bg)


def _gmlp_bwd(zuv, dgm, ln_g, ln_b, wsm, wsmt, bias, bg=None):
    t = zuv.shape[0]

    def body(z_ref, d_ref, g_ref, b_ref, ws_ref, wst_ref, bias_ref,
             dz_ref, dws_ref, dbias_ref, dg_ref, db_ref):
        @pl.when(pl.program_id(0) == 0)
        def _():
            for r in (dws_ref, dbias_ref, dg_ref, db_ref):
                r[...] = jnp.zeros_like(r)

        zu = z_ref[:, 0:D_GMLP].astype(F32)
        zv = z_ref[:, D_GMLP:2 * D_GMLP].astype(F32)
        u = _gelu(zu)
        v0 = _gelu(zv)
        gam = g_ref[...]
        v, vhat, rstd = _ln_fwd(v0, gam, b_ref[...])
        vb = v.astype(BF16)
        s = _gmlp_spatial(ws_ref, vb) + bias_ref[...]
        d = d_ref[...]
        dz_ref[:, 0:D_GMLP] = (d * s * _gelu_grad(zu)).astype(BF16)
        ds = d * u
        dbias_ref[...] += ds
        dsb = ds.astype(BF16)
        lane = lax.broadcasted_iota(jnp.int32, (CHUNK, 128), 1)
        tril = (lax.broadcasted_iota(jnp.int32, (CHUNK, CHUNK), 0)
                >= lax.broadcasted_iota(jnp.int32, (CHUNK, CHUNK), 1))
        zero_b = jnp.zeros((CHUNK, 128), BF16)
        parts = []
        for j in range(GMLP_HEADS // 2):
            dsp = dsb[:, 128 * j:128 * (j + 1)]
            vp = vb[:, 128 * j:128 * (j + 1)]
            parts.append(jnp.where(lane < GMLP_HEAD_DIM, _dot(wst_ref[2 * j], dsp),
                                   _dot(wst_ref[2 * j + 1], dsp)))
            lo = jnp.where(lane < GMLP_HEAD_DIM, dsp, zero_b)
            hi = jnp.where(lane < GMLP_HEAD_DIM, zero_b, dsp)
            dws_ref[2 * j] += jnp.where(tril, _dot_nt(lo, vp), 0.0)
            dws_ref[2 * j + 1] += jnp.where(tril, _dot_nt(hi, vp), 0.0)
        dv = jnp.concatenate(parts, axis=1)
        dg_ref[...] += jnp.sum(dv * vhat, axis=0, keepdims=True)
        db_ref[...] += jnp.sum(dv, axis=0, keepdims=True)
        dz_ref[:, D_GMLP:2 * D_GMLP] = (_ln_bwd(dv, vhat, rstd, gam) * _gelu_grad(zv)).astype(BF16)

    return _call(
        body, "gmlp_bwd", (t // CHUNK,),
        [_rows(CHUNK, 2 * D_GMLP), _rows(CHUNK, D_GMLP), _fixed((1, D_GMLP)), _fixed((1, D_GMLP)),
         _fixed((GMLP_HEADS, CHUNK, CHUNK)), _fixed((GMLP_HEADS, CHUNK, CHUNK)), _fixed((CHUNK, D_GMLP))],
        (_rows(CHUNK, 2 * D_GMLP), _fixed((GMLP_HEADS, CHUNK, CHUNK)), _fixed((CHUNK, D_GMLP)),
         _fixed((1, D_GMLP)), _fixed((1, D_GMLP))),
        (SDS((t, 2 * D_GMLP), BF16), SDS((GMLP_HEADS, CHUNK, CHUNK), F32), SDS((CHUNK, D_GMLP), F32),
         SDS((1, D_GMLP), F32), SDS((1, D_GMLP), F32)),
        (zuv, dgm, ln_g, ln_b, wsm, wsmt, bias), sem=("arbitrary",), bg=bg)


def _mixout_fwd(x1, s5o, gm, gab, ua, ub, wmo, g, b, tm, bg=None):
    t = x1.shape[0]

    def body(x_ref, s_ref, m_ref, gab_ref, ua_ref, ub_ref, wmo_ref, g_ref, b_ref,
             xn_ref, xh_ref, rstd_ref):
        ya = _dot(s_ref[...], ua_ref[...])
        yb = _dot(m_ref[...], ub_ref[...])
        mix = (_sigmoid(gab_ref[:, 0:D_MODEL].astype(F32)) * ya
               + _sigmoid(gab_ref[:, D_MODEL:2 * D_MODEL].astype(F32)) * yb)
        r = ALPHA * x_ref[...] + _dot(mix.astype(BF16), wmo_ref[...])
        y, xh, rstd = _ln_fwd(r, g_ref[...], b_ref[...])
        xn_ref[...] = y
        xh_ref[...] = xh
        rstd_ref[...] = rstd

    return _call(
        body, "mixout_fwd", (t // tm,),
        [_rows(tm, D_MODEL), _rows(tm, D_SSM), _rows(tm, D_GMLP), _rows(tm, 2 * D_MODEL),
         _resident((D_SSM, D_MODEL)), _resident((D_GMLP, D_MODEL)), _resident((D_MODEL, D_MODEL)),
         _fixed((1, D_MODEL)), _fixed((1, D_MODEL))],
        (_rows(tm, D_MODEL), _rows(tm, D_MODEL), _rows(tm, 1)),
        (SDS((t, D_MODEL), F32), SDS((t, D_MODEL), F32), SDS((t, 1), F32)),
        (x1, s5o, gm, gab, ua, ub, wmo, g, b), sem=("parallel",), bg=bg)


def _mixout_bwd(dx2, xh, rstd, s5o, gm, gab, ua, ub, wmo, g, tm, bg=None):
    t = dx2.shape[0]

    def body(d_ref, xh_ref, rstd_ref, s_ref, m_ref, gab_ref, ua_ref, ub_ref, wmo_ref, g_ref,
             dx1_ref, dmx_ref, mb_ref, dya_ref, dyb_ref, ds5_ref, dgm_ref, dgab_ref, dg_ref, db_ref):
        @pl.when(pl.program_id(0) == 0)
        def _():
            dg_ref[...] = jnp.zeros_like(dg_ref)
            db_ref[...] = jnp.zeros_like(db_ref)

        dy = d_ref[...]
        xhv = xh_ref[...]
        dr = _ln_bwd(dy, xhv, rstd_ref[...], g_ref[...])
        dg_ref[...] += jnp.sum(dy * xhv, axis=0, keepdims=True)
        db_ref[...] += jnp.sum(dy, axis=0, keepdims=True)
        dx1_ref[...] = ALPHA * dr
        drb = dr.astype(BF16)
        dmx_ref[...] = drb
        dm = _dot_nt(drb, wmo_ref[...])
        ya = _dot(s_ref[...], ua_ref[...])
        yb = _dot(m_ref[...], ub_ref[...])
        sa = _sigmoid(gab_ref[:, 0:D_MODEL].astype(F32))
        sb = _sigmoid(gab_ref[:, D_MODEL:2 * D_MODEL].astype(F32))
        mb_ref[...] = (sa * ya + sb * yb).astype(BF16).T
        dya = (dm * sa).astype(BF16)
        dyb = (dm * sb).astype(BF16)
        dya_ref[...] = dya
        dyb_ref[...] = dyb
        dgab_ref[:, 0:D_MODEL] = (dm * ya * sa * (1.0 - sa)).astype(BF16)
        dgab_ref[:, D_MODEL:2 * D_MODEL] = (dm * yb * sb * (1.0 - sb)).astype(BF16)
        ds5_ref[...] = _dot_nt(dya, ua_ref[...])
        dgm_ref[...] = _dot_nt(dyb, ub_ref[...])

    return _call(
        body, "mixout_bwd", (t // tm,),
        [_rows(tm, D_MODEL), _rows(tm, D_MODEL), _rows(tm, 1), _rows(tm, D_SSM), _rows(tm, D_GMLP),
         _rows(tm, 2 * D_MODEL), _resident((D_SSM, D_MODEL)), _resident((D_GMLP, D_MODEL)),
         _resident((D_MODEL, D_MODEL)), _fixed((1, D_MODEL))],
        (_rows(tm, D_MODEL), _rows(tm, D_MODEL), _cols(D_MODEL, tm), _rows(tm, D_MODEL),
         _rows(tm, D_MODEL), _rows(tm, D_SSM), _rows(tm, D_GMLP), _rows(tm, 2 * D_MODEL),
         _fixed((1, D_MODEL)), _fixed((1, D_MODEL))),
        (SDS((t, D_MODEL), F32), SDS((t, D_MODEL), BF16), SDS((D_MODEL, t), BF16),
         SDS((t, D_MODEL), BF16), SDS((t, D_MODEL), BF16), SDS((t, D_SSM), F32),
         SDS((t, D_GMLP), F32), SDS((t, 2 * D_MODEL), BF16),
         SDS((1, D_MODEL), F32), SDS((1, D_MODEL), F32)),
        (dx2, xh, rstd, s5o, gm, gab, ua, ub, wmo, g), sem=("arbitrary",), bg=bg)


def _ple_loss(x3, p, tgt, wpg, wpp, tm, bg=None):
    t = x3.shape[0]

    def body(x_ref, p_ref, t_ref, wpg_ref, wpp_ref, dx_ref, xb_ref, pb_ref, dq_ref, de_ref, loss_ref):
        @pl.when(pl.program_id(0) == 0)
        def _():
            loss_ref[...] = jnp.zeros_like(loss_ref)

        x3v = x_ref[...]
        xb = x3v.astype(BF16)
        pb = p_ref[...].astype(BF16)
        xb_ref[...] = xb.T
        pb_ref[...] = pb.T
        s = _sigmoid(_dot(xb, wpg_ref[...]))
        e = _dot(pb, wpp_ref[...])
        diff = x3v + s * e - t_ref[...]
        loss_ref[...] += jnp.sum(diff * diff, axis=0, keepdims=True)
        dout = diff * (1.0 / D_MODEL)
        de_ref[...] = (dout * s).astype(BF16)
        dq = (dout * e * s * (1.0 - s)).astype(BF16)
        dq_ref[...] = dq
        dx_ref[...] = dout + _dot_nt(dq, wpg_ref[...])

    return _call(
        body, "ple_loss", (t // tm,),
        [_rows(tm, D_MODEL), _rows(tm, PLE_DIM), _rows(tm, D_MODEL),
         _resident((D_MODEL, D_MODEL)), _resident((PLE_DIM, D_MODEL))],
        (_rows(tm, D_MODEL), _cols(D_MODEL, tm), _cols(PLE_DIM, tm), _rows(tm, D_MODEL),
         _rows(tm, D_MODEL), _fixed((1, D_MODEL))),
        (SDS((t, D_MODEL), F32), SDS((D_MODEL, t), BF16), SDS((PLE_DIM, t), BF16),
         SDS((t, D_MODEL), BF16), SDS((t, D_MODEL), BF16), SDS((1, D_MODEL), F32)),
        (x3, p, tgt, wpg, wpp), sem=("arbitrary",), bg=bg)


def _s5_discretise(lre, lim, log_dt, bre, bim):
    dt = jnp.exp(log_dt)[:, None]
    mag = jnp.exp(lre * dt)
    abr = mag * jnp.cos(lim * dt)
    abi = mag * jnp.sin(lim * dt)
    nr = abr - 1.0
    ni = abi
    den = lre * lre + lim * lim
    cr = ((nr * lre + ni * lim) / den)[..., None]
    ci = ((ni * lre - nr * lim) / den)[..., None]
    return abr, abi, cr * bre - ci * bim, cr * bim + ci * bre


def _block_diag_in(bb):
    v = bb.reshape(S5_BLOCKS, 8, SSM_STATE, SSM_GROUP_CH).transpose(0, 1, 3, 2)
    return jnp.einsum("bgip,gh->bgihp", v, jnp.eye(8, dtype=bb.dtype)).reshape(
        S5_BLOCKS, S5_BLOCK_IN, S5_BLOCK_ST)


def _block_diag_in_t(dm):
    v = dm.reshape(S5_BLOCKS, 8, SSM_GROUP_CH, 8, SSM_STATE)
    d = jnp.einsum("bgihp,gh->bgip", v, jnp.eye(8, dtype=dm.dtype))
    return d.transpose(0, 1, 3, 2).reshape(SSM_GROUPS, SSM_STATE, SSM_GROUP_CH)


def _block_diag_out(cc):
    v = cc.reshape(S5_BLOCKS, 8, SSM_GROUP_CH, SSM_STATE)
    return jnp.einsum("bgip,gh->bgphi", v, jnp.eye(8, dtype=cc.dtype)).reshape(
        S5_BLOCKS, S5_BLOCK_ST, S5_BLOCK_IN)


def _block_diag_out_t(dn):
    v = dn.reshape(S5_BLOCKS, 8, SSM_STATE, 8, SSM_GROUP_CH)
    d = jnp.einsum("bgphi,gh->bgip", v, jnp.eye(8, dtype=dn.dtype))
    return d.reshape(SSM_GROUPS, SSM_GROUP_CH, SSM_STATE)


def _s5_setup(lre, lim, log_dt, bre, bim, cre, cim, d_skip, glu_w, glu_b, tb):
    seg = tb // 8
    abr, abi, bbr, bbi = _s5_discretise(lre, lim, log_dt, bre, bim)
    pr, pi = abr, abi
    for _ in range(int(math.log2(seg))):
        pr, pi = pr * pr - pi * pi, 2.0 * pr * pi
    rows = jnp.arange(tb)
    src = (rows % 8) * seg + rows // 8
    perm = (src[:, None] == jnp.arange(tb)[None, :]).astype(BF16)
    mre = _block_diag_in(bbr)
    mim = _block_diag_in(bbi)
    nre = _block_diag_out(cre)
    nim = _block_diag_out(cim)
    return {
        "perm": perm, "permt": perm.T,
        "mre": mre.astype(BF16), "mim": mim.astype(BF16),
        "mtre": mre.transpose(0, 2, 1).astype(BF16), "mtim": mim.transpose(0, 2, 1).astype(BF16),
        "nre": nre.astype(BF16), "nim": nim.astype(BF16),
        "ntre": nre.transpose(0, 2, 1).astype(BF16), "ntim": nim.transpose(0, 2, 1).astype(BF16),
        "a": jnp.stack([abr.reshape(-1), abi.reshape(-1)]),
        "ap": jnp.stack([pr.reshape(-1), pi.reshape(-1)]),
        "dskip": d_skip.reshape(1, D_SSM), "glu_w": glu_w, "glu_wt": glu_w.T,
        "glu_b": glu_b.reshape(1, D_SSM),
    }


BIG = ("ffn1_w_in", "ffn1_w_out", "mix_w_in", "ssm_glu_w", "up_a", "up_b", "mix_w_out",
       "ffn2_w_in", "ffn2_w_out", "ple_w_proj", "ple_w_gate")
BIG_AXIS = {"ffn1_w_in": 1, "ffn1_w_out": 0, "mix_w_in": 1, "ssm_glu_w": 0, "up_a": 1, "up_b": 1,
            "mix_w_out": 0, "ffn2_w_in": 1, "ffn2_w_out": 0, "ple_w_proj": 1, "ple_w_gate": 0}
SHARD_MAJOR = 2
GATHER_AXIS = dict(BIG_AXIS, ffn1_w_in=SHARD_MAJOR, ffn2_w_in=SHARD_MAJOR)
GATHER_ORDER = (("ffn1_w_in",), ("ffn1_w_out",), ("mix_w_in",), ("ssm_glu_w", "up_a", "up_b", "mix_w_out"),
                ("ffn2_w_in",), ("ffn2_w_out", "ple_w_gate", "ple_w_proj"))
GATHER_FIRST_ID = 1
REDUCE_FIRST_ID = 7
SMALL = ("ln1_g", "ln1_b", "ssm_lambda_re", "ssm_lambda_im", "ssm_log_dt", "ssm_b_re", "ssm_b_im",
         "ssm_c_re", "ssm_c_im", "ssm_d", "ssm_glu_b", "gmlp_ln_g", "gmlp_ln_b", "gmlp_w_s",
         "gmlp_b_s", "ln2_g", "ln2_b", "ln3_g", "ln3_b")
SMALL_VIEW = {"ssm_b_re": (SSM_GROUPS, SSM_STATE * SSM_GROUP_CH), "ssm_b_im": (SSM_GROUPS, SSM_STATE * SSM_GROUP_CH)}


def _small_view(k, a):
    return a.reshape(SMALL_VIEW[k]) if k in SMALL_VIEW else a


def _place():
    return lax.axis_index("x"), lax.axis_index("y"), lax.axis_index("c")


def _other_chips(x, y):
    return [(1 - x, y), (x, 1 - y), (1 - x, 1 - y)]


def _window(ref, shard_shape, axis, chip, half):
    r, c = shard_shape
    hr = r // 2
    if axis == SHARD_MAJOR:
        return ref.at[chip] if half is None else ref.at[chip, pl.ds(half * hr, hr), :]
    if axis == 0:
        if half is None:
            return ref.at[pl.ds(chip * r, r), :]
        return ref.at[pl.ds(chip * r + half * hr, hr), :]
    if half is None:
        return ref.at[:, pl.ds(chip * c, c)]
    return ref.at[pl.ds(half * hr, hr), pl.ds(chip * c, c)]


def _gather_weights(shards, axes):
    n = len(shards)
    shapes = [s.shape for s in shards]
    full = [{0: (4 * r, c), 1: (r, 4 * c), SHARD_MAJOR: (4, r, c)}[ax] for (r, c), ax in zip(shapes, axes)]

    def remote(sems, i, k, src, dst, to):
        return pltpu.make_async_remote_copy(src_ref=src, dst_ref=dst, send_sem=sems[0].at[6 * i + k],
                                            recv_sem=sems[1].at[6 * i + k], device_id=to, device_id_type=MESH)

    def own_copies(ins, outs, sems):
        x, y, c = _place()
        me = 2 * x + y
        cps = []
        for i in range(n):
            hr = shapes[i][0] // 2
            mine = ins[i].at[pl.ds(c * hr, hr), :]
            for j, (cx, cy) in enumerate(_other_chips(x, y)):
                cps.append(remote(sems, i, j, mine, _window(outs[i], shapes[i], axes[i], me, c), (cx, cy, c)))
        local = [pltpu.make_async_copy(ins[i], _window(outs[i], shapes[i], axes[i], me, None), sems[2].at[i])
                 for i in range(n)]
        return cps, local

    def start(ins, outs, sems):
        cps, local = own_copies(ins, outs, sems)
        for cp in local + cps:
            cp.start()

    def finish(ins, outs, sems):
        x, y, c = _place()
        sibling = (x, y, 1 - c)
        passed = []
        for j, (cx, cy) in enumerate(_other_chips(x, y)):
            for i in range(n):
                w = _window(outs[i], shapes[i], axes[i], 2 * cx + cy, c)
                remote(sems, i, j, w, w, (cx, cy, c)).wait_recv()
                cp = remote(sems, i, 3 + j, w, w, sibling)
                cp.start()
                passed.append(cp)
        for j, (cx, cy) in enumerate(_other_chips(x, y)):
            for i in range(n):
                w = _window(outs[i], shapes[i], axes[i], 2 * cx + cy, 1 - c)
                remote(sems, i, 3 + j, w, w, sibling).wait_recv()
        cps, local = own_copies(ins, outs, sems)
        for cp in cps + passed:
            cp.wait_send()
        for cp in local:
            cp.wait()

    return _Exchange(shards, [SDS(f, BF16) for f in full],
                     [pltpu.SemaphoreType.DMA((6 * n,)), pltpu.SemaphoreType.DMA((6 * n,)),
                      pltpu.SemaphoreType.DMA((n,))], start, finish)


def _scatter_grads(parts, shapes, axes):
    n = len(parts)

    def copies(ins, outs, sems):
        x, y, c = _place()
        return [pltpu.make_async_remote_copy(
            src_ref=_window(ins[i], shapes[i], axes[i], 2 * cx + cy, None), dst_ref=outs[i].at[j],
            send_sem=sems[0].at[3 * i + j], recv_sem=sems[1].at[3 * i + j],
            device_id=(cx, cy, c), device_id_type=MESH)
            for i in range(n) for j, (cx, cy) in enumerate(_other_chips(x, y))]

    def start(ins, outs, sems):
        for cp in copies(ins, outs, sems):
            cp.start()

    def finish(ins, outs, sems):
        for cp in copies(ins, outs, sems):
            cp.wait()

    return _Exchange(parts, [SDS((3,) + tuple(s), BF16) for s in shapes],
                     [pltpu.SemaphoreType.DMA((3 * n,)), pltpu.SemaphoreType.DMA((3 * n,))], start, finish)


def _swap_halves(parts, shapes, axes):
    n = len(parts)

    def copies(ins, outs, sems):
        x, y, c = _place()
        cps = []
        for i in range(n):
            r, _ = shapes[i]
            hr = r // 2
            if axes[i] == 0:
                cps += [pltpu.make_async_remote_copy(
                    src_ref=ins[i].at[pl.ds(k * r + (1 - c) * hr, hr), :], dst_ref=outs[i].at[k],
                    send_sem=sems[0].at[i], recv_sem=sems[1].at[i], device_id=(x, y, 1 - c),
                    device_id_type=MESH) for k in range(4)]
            else:
                cps.append(pltpu.make_async_remote_copy(
                    src_ref=ins[i].at[pl.ds((1 - c) * hr, hr), :], dst_ref=outs[i],
                    send_sem=sems[0].at[i], recv_sem=sems[1].at[i], device_id=(x, y, 1 - c),
                    device_id_type=MESH))
        return cps

    def start(ins, outs, sems):
        for cp in copies(ins, outs, sems):
            cp.start()

    def finish(ins, outs, sems):
        x, y, c = _place()
        for i in range(n):
            pltpu.make_async_remote_copy(src_ref=outs[i], dst_ref=outs[i], send_sem=sems[0].at[i],
                                         recv_sem=sems[1].at[i], device_id=(x, y, 1 - c),
                                         device_id_type=MESH).wait()

    out = [SDS((4, r // 2, c), BF16) if ax == 0 else SDS((r // 2, 4 * c), BF16)
           for (r, c), ax in zip(shapes, axes)]
    return _Exchange(parts, out, [pltpu.SemaphoreType.DMA((n,)), pltpu.SemaphoreType.DMA((n,))], start, finish)


def _scatter_halves(pres, shapes):
    n = len(pres)

    def copies(ins, outs, sems):
        x, y, c = _place()
        return [pltpu.make_async_remote_copy(
            src_ref=ins[i].at[1 + j], dst_ref=outs[i].at[j], send_sem=sems[0].at[3 * i + j],
            recv_sem=sems[1].at[3 * i + j], device_id=(cx, cy, c), device_id_type=MESH)
            for i in range(n) for j, (cx, cy) in enumerate(_other_chips(x, y))]

    def start(ins, outs, sems):
        for cp in copies(ins, outs, sems):
            cp.start()

    def finish(ins, outs, sems):
        for cp in copies(ins, outs, sems):
            cp.wait()

    return _Exchange(pres, [SDS((3, r // 2, c), BF16) for r, c in shapes],
                     [pltpu.SemaphoreType.DMA((3 * n,)), pltpu.SemaphoreType.DMA((3 * n,))], start, finish)


def _swap_with_sibling(arrs):
    n = len(arrs)

    def copies(ins, outs, sems):
        x, y, c = _place()
        return [pltpu.make_async_remote_copy(src_ref=ins[i], dst_ref=outs[i], send_sem=sems[0].at[i],
                                             recv_sem=sems[1].at[i], device_id=(x, y, 1 - c),
                                             device_id_type=MESH) for i in range(n)]

    def start(ins, outs, sems):
        for cp in copies(ins, outs, sems):
            cp.start()

    def finish(ins, outs, sems):
        for cp in copies(ins, outs, sems):
            cp.wait()

    return _Exchange(arrs, [SDS(a.shape, a.dtype) for a in arrs],
                     [pltpu.SemaphoreType.DMA((n,)), pltpu.SemaphoreType.DMA((n,))], start, finish)


def _gather_small(arrs):
    n = len(arrs)

    def copy(sems, outs, i, k, block, to, src=None):
        px, py, pc = block
        dst = outs[i].at[4 * px + 2 * py + pc]
        return pltpu.make_async_remote_copy(
            src_ref=dst if src is None else src, dst_ref=dst, send_sem=sems[0].at[7 * i + k],
            recv_sem=sems[1].at[7 * i + k], device_id=to, device_id_type=MESH)

    direct = [math.prod(a.shape) * 4 <= DIRECT_GATHER_BYTES for a in arrs]

    def own_copies(ins, outs, sems):
        x, y, c = _place()
        cps = []
        for i in range(n):
            cps.append(copy(sems, outs, i, 0, (x, y, c), (x, y, 1 - c), src=ins[i]))
            for j, (cx, cy) in enumerate(_other_chips(x, y)):
                cps.append(copy(sems, outs, i, 1 + j, (x, y, c), (cx, cy, c), src=ins[i]))
                if direct[i]:
                    cps.append(copy(sems, outs, i, 4 + j, (x, y, c), (cx, cy, 1 - c), src=ins[i]))
        local = [pltpu.make_async_copy(ins[i], outs[i].at[4 * x + 2 * y + c], sems[2].at[i]) for i in range(n)]
        return cps, local

    def start(ins, outs, sems):
        cps, local = own_copies(ins, outs, sems)
        for cp in local + cps:
            cp.start()

    def finish(ins, outs, sems):
        x, y, c = _place()
        passed = []
        for j, (cx, cy) in enumerate(_other_chips(x, y)):
            for i in range(n):
                copy(sems, outs, i, 1 + j, (cx, cy, c), (x, y, c)).wait_recv()
                if not direct[i]:
                    cp = copy(sems, outs, i, 4 + j, (cx, cy, c), (x, y, 1 - c))
                    cp.start()
                    passed.append(cp)
        for i in range(n):
            copy(sems, outs, i, 0, (x, y, 1 - c), (x, y, c)).wait_recv()
            for j, (cx, cy) in enumerate(_other_chips(x, y)):
                copy(sems, outs, i, 4 + j, (cx, cy, 1 - c), (x, y, c)).wait_recv()
        cps, local = own_copies(ins, outs, sems)
        for cp in cps + passed:
            cp.wait_send()
        for cp in local:
            cp.wait()

    return _Exchange(arrs, [SDS((N_DEV,) + a.shape, F32) for a in arrs],
                     [pltpu.SemaphoreType.DMA((7 * n,)), pltpu.SemaphoreType.DMA((7 * n,)),
                      pltpu.SemaphoreType.DMA((n,))], start, finish)


def _local_step(x, p, tgt, wb, ws, shards=None, opt=None):
    bsz, seq, _ = x.shape
    t = bsz * seq
    tm = min(256, t)
    tb = min(256, seq)
    x0 = x.reshape(t, D_MODEL)
    p0 = p.reshape(t, PLE_DIM)
    tg = tgt.reshape(t, D_MODEL)
    row = lambda v: v.reshape(1, -1)
    dist = shards is not None
    wb = dict(wb)
    recv, sums, other, gathered = {}, {}, {}, {}
    gb = {}
    gs = {}
    shape_of, axis_of = {}, {}
    chip = None
    if dist:
        shape_of = {k: tuple(shards[k].shape) for k in BIG}
        axis_of = dict(BIG_AXIS)
        for q in range(LAST_PIECES):
            shape_of[LAST_PIECE % q] = (D_MODEL // LAST_PIECES, shape_of["ffn1_w_in"][1])
            axis_of[LAST_PIECE % q] = 1
        xi, yi, ci = _place()
        chip = (2 * xi + yi).astype(jnp.int32).reshape(1)
        ids = jnp.stack([2 * xi + yi] + [2 * cx + cy for cx, cy in _other_chips(xi, yi)] + [ci]).astype(jnp.int32)
    halfbuf, pre = {}, {}

    def gather(names):
        return _gather_weights([shards[k] for k in names], [GATHER_AXIS[k] for k in names]) if dist else None

    def exchange(scat=(), swap=(), halves=(), scat2=(), swap2=(), extra=None, after=None):
        if not dist:
            return None, []
        after = order[0] if after is None else after
        parts, tags = [], []
        if scat:
            parts.append(_scatter_grads([gb[k][1] for k in scat], [shape_of[k] for k in scat],
                                        [axis_of[k] for k in scat]))
            tags.append((recv, scat))
        if swap:
            for k in swap:
                sums[k] = order[0] = _sum_blocks(gb[k][0], recv[k], shape_of[k], axis_of[k], chip, "sum_" + k,
                                                 order[0])
            parts.append(_swap_with_sibling([sums[k] for k in swap]))
            tags.append((other, swap))
        if halves:
            parts.append(_swap_halves([gb[k][1] for k in halves], [shape_of[k] for k in halves],
                                      [axis_of[k] for k in halves]))
            tags.append((halfbuf, halves))
        if scat2:
            for k in scat2:
                pre[k] = _presum(gb[k][0], halfbuf[k], shape_of[k], axis_of[k], ids, "presum_" + k, order[0])
                order[0] = pre[k][0]
            parts.append(_scatter_halves([pre[k][1] for k in scat2], [shape_of[k] for k in scat2]))
            tags.append((recv, scat2))
        if swap2:
            for k in swap2:
                sums[k] = order[0] = _sum_half(pre[k][0], recv[k], "sum_" + k, order[0])
            parts.append(_swap_with_sibling([sums[k] for k in swap2]))
            tags.append((other, swap2))
        if extra is not None:
            parts.append(extra[0])
            tags.append((extra[1], extra[2]))
        return (_join(parts), tags) if parts else (None, [])

    def take(ex_tags, got):
        ex, tags = ex_tags
        if ex is not None:
            for (dst, names), (o0, o1) in zip(tags, ex.cuts):
                dst.update(zip(names, got[o0:o1]))

    order = [None]

    def ordered(builder, *args, **kw):
        res = builder(*args, bg=order[0] if dist else None, **kw)
        order[0] = res[0][0]
        return res

    launched = []

    def launch(ex_tags):
        if ex_tags[0] is not None:
            n = len(launched)
            launched.append(n)
            take(ex_tags, _run_exchange_on_sequencer(ex_tags[0], "reduce_%d" % n, REDUCE_FIRST_ID + n))

    small_shape = {k: _small_view(k, v).shape for k, v in ws.items()}
    small_shape["loss_rows"] = (1, D_MODEL)
    ws = {k: v if (v.ndim == 2 and k != "ssm_log_dt") else v[0] for k, v in ws.items()}
    tril = jnp.tril(jnp.ones((CHUNK, CHUNK), dtype=bool))
    wsm = jnp.where(tril[None], ws["gmlp_w_s"], 0.0)
    wsm_b = wsm.astype(BF16)
    wsmt_b = wsm.transpose(0, 2, 1).astype(BF16)
    bias = jnp.repeat(ws["gmlp_b_s"].T, GMLP_HEAD_DIM, axis=1)

    tf = min(512, t)
    if dist:
        for gi, names in enumerate(GATHER_ORDER):
            wb.update(zip(names, _run_exchange_on_sequencer(gather(names), "gather_%d" % gi, GATHER_FIRST_ID + gi)))
    (x0b, h1, a1), _ = _ffn_proj(x0, wb["ffn1_w_in"], tf, "ffn1_proj")
    (x1, xh1, rstd1), _ = _ffn_out(x0, a1, wb["ffn1_w_out"], row(ws["ln1_g"]), row(ws["ln1_b"]), tf, "ffn1_out")
    sp = _s5_setup(ws["ssm_lambda_re"], ws["ssm_lambda_im"], ws["ssm_log_dt"], ws["ssm_b_re"],
                   ws["ssm_b_im"], ws["ssm_c_re"], ws["ssm_c_im"], ws["ssm_d"], wb["ssm_glu_w"],
                   ws["ssm_glu_b"], tb)
    (x1b, za, zuv, gab), _ = _mixin_fwd(x1, wb["mix_w_in"], tm)
    (s5o, s5ot, y2p, carries), _ = _s5_fwd(za, sp, bsz, seq, tb)
    (gm, gmt), _ = _gmlp_fwd(zuv, row(ws["gmlp_ln_g"]), row(ws["gmlp_ln_b"]), wsm_b, bias)
    (x2, xh2, rstd2), _ = _mixout_fwd(x1, s5o, gm, gab, wb["up_a"], wb["up_b"], wb["mix_w_out"],
                                           row(ws["ln2_g"]), row(ws["ln2_b"]), tm)
    (x2b, h2, a2), _ = _ffn_proj(x2, wb["ffn2_w_in"], tf, "ffn2_proj")
    (x3, xh3, rstd3), _ = _ffn_out(x2, a2, wb["ffn2_w_out"], row(ws["ln3_g"]), row(ws["ln3_b"]), tf, "ffn2_out")
    (dx3, x3b, pb, dq, de, loss_rows), _ = _ple_loss(x3, p0, tg, wb["ple_w_gate"], wb["ple_w_proj"], tm)
    order[0] = dx3
    gb["ple_w_gate"], _ = ordered(_tn_matmul, x3b, dq, "dw_ple_gate", 1024, 1024, a_t=True)
    gb["ple_w_proj"], _ = ordered(_tn_matmul, pb, de, "dw_ple_proj", 256, 1024, a_t=True)
    launch(exchange(scat=("ple_w_gate", "ple_w_proj")))
    (dx2, dh2, df2, gs["ln3_g"], gs["ln3_b"]), _ = ordered(
        _ffn_bwd, dx3, xh3, rstd3, h2, wb["ffn2_w_in"], wb["ffn2_w_out"], row(ws["ln3_g"]), tm, "ffn2_bwd")
    gb["ffn2_w_out"], _ = ordered(_tn_matmul, a2, df2, "dw_ffn2_out", 1408, 1024)
    launch(exchange(scat=("ffn2_w_out",)))
    gb["ffn2_w_in"], _ = ordered(_tn_matmul, x2b, dh2, "dw_ffn2_in", 1024, 1408, a_t=True)
    launch(exchange(scat=("ffn2_w_in",), swap=("ple_w_gate", "ple_w_proj")))
    (dx1a, dmx, mb, dya, dyb, ds5, dgm, dgab, gs["ln2_g"], gs["ln2_b"]), _ = ordered(
        _mixout_bwd, dx2, xh2, rstd2, s5o, gm, gab, wb["up_a"], wb["up_b"], wb["mix_w_out"], row(ws["ln2_g"]), tm)
    gb["mix_w_out"], _ = ordered(_tn_matmul, mb, dmx, "dw_mix_out", 1024, 1024, a_t=True)
    gb["up_a"], _ = ordered(_tn_matmul, s5ot, dya, "dw_up_a", 512, 1024, a_t=True)
    gb["up_b"], _ = ordered(_tn_matmul, gmt, dyb, "dw_up_b", 512, 1024, a_t=True)
    launch(exchange(scat=("mix_w_out", "up_a", "up_b"), swap=("ffn2_w_out",)))
    (dza, dmr, dmi, dnr, dni, da, ddsk, dgw, dgb), _ = ordered(_s5_bwd, za, y2p, ds5, carries, sp, bsz, seq, tb)
    gb["ssm_glu_w"] = (dgw, dgw.astype(BF16))
    launch(exchange(scat=("ssm_glu_w",), swap=("ffn2_w_in",)))
    (dzuv, dws, dbias, gs["gmlp_ln_g"], gs["gmlp_ln_b"]), _ = ordered(
        _gmlp_bwd, zuv, dgm, row(ws["gmlp_ln_g"]), row(ws["gmlp_ln_b"]), wsm_b, wsmt_b, bias)
    (dx1,), _ = ordered(_mixin_bwd, dx1a, dza, dzuv, dgab, wb["mix_w_in"], tm)
    g_mi, _ = ordered(_tn_matmul, x1b, dza, "dw_mix_in_a", 1024, 512, 0, 3584, a_t=True)
    g_mi, _ = ordered(_tn_matmul, x1b, dzuv, "dw_mix_in_uv", 1024, 512, 1, 3584, g_mi, a_t=True)
    gb["mix_w_in"], _ = ordered(_tn_matmul, x1b, dgab, "dw_mix_in_g", 1024, 512, 3, 3584, g_mi, a_t=True)
    launch(exchange(swap=("mix_w_out", "up_a", "up_b", "ssm_glu_w")))

    d_abr = da[0].sum(axis=0).reshape(SSM_GROUPS, SSM_STATE)
    d_abi = da[1].sum(axis=0).reshape(SSM_GROUPS, SSM_STATE)
    _, vjp = jax.vjp(_s5_discretise, ws["ssm_lambda_re"], ws["ssm_lambda_im"], ws["ssm_log_dt"],
                     ws["ssm_b_re"], ws["ssm_b_im"])
    (gs["ssm_lambda_re"], gs["ssm_lambda_im"], gs["ssm_log_dt"], gs["ssm_b_re"], gs["ssm_b_im"]) = vjp(
        (d_abr, d_abi, _block_diag_in_t(dmr), _block_diag_in_t(dmi)))
    gs["ssm_c_re"] = _block_diag_out_t(dnr)
    gs["ssm_c_im"] = _block_diag_out_t(dni)
    gs["ssm_d"] = ddsk
    gs["ssm_glu_b"] = dgb
    gs["gmlp_w_s"] = dws
    gs["gmlp_b_s"] = dbias.reshape(CHUNK, GMLP_HEADS, GMLP_HEAD_DIM).sum(axis=-1).T
    gs["loss_rows"] = loss_rows

    def small_gather(names):
        return (_gather_small([gs[k].reshape(small_shape[k]) for k in names]), gathered, names) if dist else None

    late = ("ln1_g", "ln1_b")
    launch(exchange(scat=("mix_w_in",), extra=small_gather(tuple(k for k in SMALL + ("loss_rows",) if k not in late))))
    (dx0, dh1, df1, gs["ln1_g"], gs["ln1_b"]), _ = ordered(
        _ffn_bwd, dx1, xh1, rstd1, h1, wb["ffn1_w_in"], wb["ffn1_w_out"], row(ws["ln1_g"]), tm, "ffn1_bwd")
    grad_x = dx0.reshape(bsz, seq, D_MODEL)
    if not dist:
        gb["ffn1_w_out"], _ = _tn_matmul(a1, df1, "dw_ffn1_out", 1408, 1024)
        gb["ffn1_w_in"], _ = _tn_matmul(x0b, dh1, "dw_ffn1_in", 1024, 1408, a_t=True)
        return (loss_rows, grad_x, gb, {k: gs[k].reshape(small_shape[k]) for k in SMALL}, sums, other, gathered,
                None, {})
    launch(exchange(extra=small_gather(late)))
    gb["ffn1_w_out"], _ = ordered(_tn_matmul, a1, df1, "dw_ffn1_out", 1408, 1024)
    last = ["ffn1_w_out"] + [LAST_PIECE % q for q in range(LAST_PIECES)]
    fillers = (("ffn2_w_in", "mix_w_in", "ple_w_gate"),
               ("ffn2_w_out", "mix_w_out", "up_a", "up_b", "ssm_glu_w", "ple_w_proj"))
    out = {}
    for i in range(1, len(last) + 3):
        stage = lambda d: tuple(last[i - d:i - d + 1]) if 0 <= i - d < len(last) else ()
        launch(exchange(halves=stage(1), scat2=stage(2), swap2=stage(3), swap=("mix_w_in",) if i == 2 else ()))
        if i < len(last):
            gb[last[i]], _ = ordered(_tn_matmul, x0b, dh1, "dw_" + last[i], D_MODEL // LAST_PIECES, 1408,
                                     a_cols=(i - 1, 1), a_t=True)
        elif i - len(last) < len(fillers):
            for k in fillers[i - len(last)]:
                w, m, v = opt[k]
                out[k] = _adam_big(w, sums[k], other[k], m, v, "adam_" + k, after=order[0])
                order[0] = out[k][1]
    return loss_rows, grad_x, gb, gs, sums, other, gathered, ids, out


def _adamw(w, g, m, v):
    m = ADAM_B1 * m + (1.0 - ADAM_B1) * g
    v = ADAM_B2 * v + (1.0 - ADAM_B2) * (g * g)
    m_hat = m / ADAM_C1
    v_hat = v / ADAM_C2
    delta = -ADAM_LR * (m_hat / (jnp.sqrt(v_hat) + ADAM_EPS) + ADAM_WD * w)
    return delta, m, v


def _pinned(after):
    return ([pl.BlockSpec(memory_space=pl.ANY)], [after]) if after is not None else ([], [])


def _sum_blocks(part, recv, shape, axis, chip, name, after=None):
    r, c = shape
    rb = r // 8

    def body(chip_ref, p_ref, r_ref, *rest):
        rest[-1][...] = (p_ref[...] + r_ref[0].astype(F32) + r_ref[1].astype(F32) + r_ref[2].astype(F32))

    if axis == 0:
        own = pl.BlockSpec((rb, c), lambda i, k: (k[0] * 8 + i, 0))
    else:
        own = pl.BlockSpec((rb, c), lambda i, k: (i, k[0]))
    pin_specs, pin_args = _pinned(after)
    grid_spec = pltpu.PrefetchScalarGridSpec(
        num_scalar_prefetch=1, grid=(8,),
        in_specs=[own, pl.BlockSpec((3, rb, c), lambda i, k: (0, i, 0))] + pin_specs,
        out_specs=pl.BlockSpec((rb, c), lambda i, k: (i, 0)))
    return pl.pallas_call(body, name=name, out_shape=SDS((r, c), F32), grid_spec=grid_spec,
                          compiler_params=_params(("parallel",)))(chip, part, recv, *pin_args)


def _presum(part, half, shape, axis, ids, name, after=None):
    r, c = shape
    rb = r // 4

    def body(ids_ref, p_ref, h_ref, *rest):
        of_ref, ob_ref = rest[-2:]
        s = p_ref[...] + h_ref[...].astype(F32)
        ob_ref[...] = s.astype(BF16)

        @pl.when(pl.program_id(1) == 0)
        def _():
            of_ref[...] = s

    if axis == 0:
        p_spec = pl.BlockSpec((rb, c), lambda i, t, ids: (ids[t] * 4 + ids[4] * 2 + i, 0))
        h_spec = pl.BlockSpec((None, rb, c), lambda i, t, ids: (ids[t], i, 0))
    else:
        p_spec = pl.BlockSpec((rb, c), lambda i, t, ids: (ids[4] * 2 + i, ids[t]))
        h_spec = pl.BlockSpec((rb, c), lambda i, t, ids: (i, ids[t]))
    pin_specs, pin_args = _pinned(after)
    grid_spec = pltpu.PrefetchScalarGridSpec(
        num_scalar_prefetch=1, grid=(2, 4), in_specs=[p_spec, h_spec] + pin_specs,
        out_specs=(pl.BlockSpec((rb, c), lambda i, t, ids: (i, 0)),
                   pl.BlockSpec((None, rb, c), lambda i, t, ids: (t, i, 0))))
    return pl.pallas_call(body, name=name, out_shape=(SDS((r // 2, c), F32), SDS((4, r // 2, c), BF16)),
                          grid_spec=grid_spec,
                          compiler_params=_params(("parallel", "arbitrary")))(ids, part, half, *pin_args)


def _sum_half(pre, recv, name, after=None):
    hr, c = pre.shape
    rb = hr // 2

    def body(p_ref, r_ref, *rest):
        rest[-1][...] = (p_ref[...] + r_ref[0].astype(F32) + r_ref[1].astype(F32) + r_ref[2].astype(F32))

    spec = pl.BlockSpec((rb, c), lambda i: (i, 0))
    pin_specs, pin_args = _pinned(after)
    return pl.pallas_call(body, name=name, grid=(2,), out_shape=SDS((hr, c), F32),
                          in_specs=[spec, pl.BlockSpec((3, rb, c), lambda i: (0, i, 0))] + pin_specs,
                          out_specs=spec, compiler_params=_params(("parallel",)))(pre, recv, *pin_args)


def _adam_halves(w, mine, oth, m, v, ids, name, piece=0, prev=None):
    r, c = w.shape
    rb = mine.shape[0] // 2

    def body(ids_ref, w_ref, a_ref, b_ref, m_ref, v_ref, *rest):
        g_ref, d_ref, nm_ref, nv_ref = rest[-4:]
        g = jnp.where(pl.program_id(0) // 2 == ids_ref[4], a_ref[...], b_ref[...])
        g_ref[...] = g
        d_ref[...], nm_ref[...], nv_ref[...] = _adamw(w_ref[...], g, m_ref[...], v_ref[...])

    whole = pl.BlockSpec((rb, c), lambda i, ids: (i + 4 * piece, 0))
    part = pl.BlockSpec((rb, c), lambda i, ids: (i % 2, 0))
    in_specs = [whole, part, part, whole, whole]
    args = [w, mine, oth, m, v]
    aliases = {}
    if prev is not None:
        in_specs += [pl.BlockSpec(memory_space=pl.ANY)] * 4
        args += list(prev)
        aliases = {6: 0, 7: 1, 8: 2, 9: 3}
    grid_spec = pltpu.PrefetchScalarGridSpec(num_scalar_prefetch=1, grid=(4,), in_specs=in_specs,
                                             out_specs=(whole,) * 4)
    return pl.pallas_call(body, name=name, out_shape=tuple(SDS((r, c), F32) for _ in range(4)),
                          grid_spec=grid_spec, input_output_aliases=aliases,
                          compiler_params=_params(("parallel",)))(ids, *args)


def _adam_big(w, ga, gb, m, v, name, piece=0, prev=None, after=None):
    r, c = w.shape
    pr = ga.shape[0]
    steps = 8 if pr == r else 2
    rb = pr // steps
    off = piece * steps

    def body(w_ref, ga_ref, gb_ref, m_ref, v_ref, *rest):
        g_ref, d_ref, nm_ref, nv_ref = rest[-4:]
        g = ga_ref[...] + gb_ref[...]
        g_ref[...] = g
        d_ref[...], nm_ref[...], nv_ref[...] = _adamw(w_ref[...], g, m_ref[...], v_ref[...])

    whole = pl.BlockSpec((rb, c), lambda i: (i + off, 0))
    part = pl.BlockSpec((rb, c), lambda i: (i, 0))
    in_specs = [whole, part, part, whole, whole]
    args = [w, ga, gb, m, v]
    aliases = {}
    if prev is not None:
        in_specs += [pl.BlockSpec(memory_space=pl.ANY)] * 4
        args += list(prev)
        aliases = {5: 0, 6: 1, 7: 2, 8: 3}
    if after is not None:
        in_specs.append(pl.BlockSpec(memory_space=pl.ANY))
        args.append(after)
    return pl.pallas_call(
        body, name=name, grid=(steps,), out_shape=tuple(SDS((r, c), F32) for _ in range(4)),
        in_specs=in_specs, out_specs=(whole,) * 4, input_output_aliases=aliases,
        compiler_params=_params(("parallel",)),
    )(*args)


def _adam_small(ws, gathered, ms, vs):
    n = len(ws)

    def body(*refs):
        w_refs, g_refs, m_refs, v_refs = refs[:n], refs[n:2 * n], refs[2 * n:3 * n], refs[3 * n:4 * n]
        outs = refs[4 * n:]
        for i in range(n):
            g = g_refs[i][0]
            for d in range(1, N_DEV):
                g = g + g_refs[i][d]
            delta, nm, nv = _adamw(w_refs[i][...], g, m_refs[i][...], v_refs[i][...])
            outs[i][...] = g
            outs[n + i][...] = delta
            outs[2 * n + i][...] = nm
            outs[3 * n + i][...] = nv

    vmem = pl.BlockSpec(memory_space=pltpu.VMEM)
    shapes = [w.shape for w in ws]
    return pl.pallas_call(
        body, name="adam_small", out_shape=tuple(SDS(s, F32) for s in shapes * 4),
        in_specs=[vmem] * (4 * n), out_specs=tuple([vmem] * (4 * n)),
        compiler_params=pltpu.CompilerParams(vmem_limit_bytes=VMEM_LIMIT_BYTES),
    )(*ws, *gathered, *ms, *vs)


def _sum_loss(gathered):
    def body(g_ref, o_ref):
        tot = g_ref[0]
        for d in range(1, N_DEV):
            tot = tot + g_ref[d]
        o_ref[...] = (0.5 / D_MODEL) * jnp.sum(tot, axis=1, keepdims=True)

    vmem = pl.BlockSpec(memory_space=pltpu.VMEM)
    return pl.pallas_call(body, name="sum_loss", out_shape=SDS((1, 1), F32), in_specs=[vmem],
                          out_specs=vmem)(gathered)


def kernel(x, p, ffn1_w_in, ffn1_w_out, ln1_g, ln1_b, mix_w_in, ssm_lambda_re, ssm_lambda_im, ssm_log_dt, ssm_b_re, ssm_b_im, ssm_c_re, ssm_c_im, ssm_d, ssm_glu_w, ssm_glu_b, gmlp_ln_g, gmlp_ln_b, gmlp_w_s, gmlp_b_s, up_a, up_b, mix_w_out, ln2_g, ln2_b, ffn2_w_in, ffn2_w_out, ln3_g, ln3_b, ple_w_proj, ple_w_gate, loss_target, m_ffn1_w_in, m_ffn1_w_out, m_ln1_g, m_ln1_b, m_mix_w_in, m_ssm_lambda_re, m_ssm_lambda_im, m_ssm_log_dt, m_ssm_b_re, m_ssm_b_im, m_ssm_c_re, m_ssm_c_im, m_ssm_d, m_ssm_glu_w, m_ssm_glu_b, m_gmlp_ln_g, m_gmlp_ln_b, m_gmlp_w_s, m_gmlp_b_s, m_up_a, m_up_b, m_mix_w_out, m_ln2_g, m_ln2_b, m_ffn2_w_in, m_ffn2_w_out, m_ln3_g, m_ln3_b, m_ple_w_proj, m_ple_w_gate, v_ffn1_w_in, v_ffn1_w_out, v_ln1_g, v_ln1_b, v_mix_w_in, v_ssm_lambda_re, v_ssm_lambda_im, v_ssm_log_dt, v_ssm_b_re, v_ssm_b_im, v_ssm_c_re, v_ssm_c_im, v_ssm_d, v_ssm_glu_w, v_ssm_glu_b, v_gmlp_ln_g, v_gmlp_ln_b, v_gmlp_w_s, v_gmlp_b_s, v_up_a, v_up_b, v_mix_w_out, v_ln2_g, v_ln2_b, v_ffn2_w_in, v_ffn2_w_out, v_ln3_g, v_ln3_b, v_ple_w_proj, v_ple_w_gate):
    given = dict(locals())
    order = ("ffn1_w_in", "ffn1_w_out", "ln1_g", "ln1_b", "mix_w_in", "ssm_lambda_re", "ssm_lambda_im",
             "ssm_log_dt", "ssm_b_re", "ssm_b_im", "ssm_c_re", "ssm_c_im", "ssm_d", "ssm_glu_w", "ssm_glu_b",
             "gmlp_ln_g", "gmlp_ln_b", "gmlp_w_s", "gmlp_b_s", "up_a", "up_b", "mix_w_out", "ln2_g", "ln2_b",
             "ffn2_w_in", "ffn2_w_out", "ln3_g", "ln3_b", "ple_w_proj", "ple_w_gate")
    assert set(order) == set(BIG + SMALL)

    shard = {k: given[k][0] for k in BIG}
    shard_b = {k: shard[k].astype(BF16) for k in BIG}
    opt = {k: (shard[k], given["m_" + k][0], given["v_" + k][0]) for k in BIG}
    loss_rows, grad_x, gb, gs, sums, other, gathered, ids, out = _local_step(
        x, given["p"][0], loss_target, {}, {k: given[k] for k in SMALL}, shard_b, opt)

    out = dict(out)
    for k in BIG:
        if k in out:
            continue
        moments = (given["m_" + k][0], given["v_" + k][0])
        if k == "ffn1_w_out":
            out[k] = _adam_halves(shard[k], sums[k], other[k], *moments, ids, "adam_" + k)
        elif k == "ffn1_w_in":
            for q in range(LAST_PIECES):
                kq = LAST_PIECE % q
                out[k] = _adam_halves(shard[k], sums[kq], other[kq], *moments, ids, "adam_" + kq, q, out.get(k))
        else:
            out[k] = _adam_big(shard[k], sums[k], other[k], *moments, "adam_" + k,
                               after=gb[LAST_PIECE % (LAST_PIECES - 1)][0])

    res = _adam_small([_small_view(k, given[k]) for k in SMALL], [gathered[k] for k in SMALL],
                      [_small_view(k, given["m_" + k]) for k in SMALL],
                      [_small_view(k, given["v_" + k]) for k in SMALL])
    ns = len(SMALL)
    for i, k in enumerate(SMALL):
        out[k] = tuple(res[j * ns + i].reshape(given[k].shape) for j in range(4))
    loss = _sum_loss(gathered["loss_rows"]).reshape(())

    lead = lambda k, j: out[k][j][None] if k in BIG else out[k][j]
    return (loss, grad_x, *[lead(k, 0) for k in order], *[lead(k, 1) for k in order],
            *[lead(k, 2) for k in order], *[lead(k, 3) for k in order])
```

```python
import math

import jax
import jax.numpy as jnp
from jax import lax
from jax.experimental import pallas as pl
from jax.experimental.pallas import tpu as pltpu
from jax.experimental.pallas import tpu_sc as plsc

F32 = jnp.float32
BF16 = jnp.bfloat16
MESH = pl.DeviceIdType.MESH
SDS = jax.ShapeDtypeStruct

D_MODEL = 1024
D_FF = 2816
D_SSM = 512
D_GMLP = 512
SSM_GROUPS = 32
SSM_GROUP_CH = 16
SSM_STATE = 64
SSM_LANES = SSM_GROUPS * SSM_STATE
GMLP_HEADS = 8
GMLP_HEAD_DIM = 64
CHUNK = 128
PLE_DIM = 256
LN_EPS = 1e-5
ALPHA = 2.0 ** 0.25

ADAM_LR = 0.001
ADAM_B1 = 0.9
ADAM_B2 = 0.999
ADAM_EPS = 1e-08
ADAM_WD = 0.01
ADAM_STEP = 10
ADAM_C1 = 1.0 - ADAM_B1 ** ADAM_STEP
ADAM_C2 = 1.0 - ADAM_B2 ** ADAM_STEP

N_DEV = 8
VMEM_LIMIT_BYTES = 56 * 1024 * 1024
FFN_COLS = 1408
S5_BLOCKS = 4
S5_BLOCK_IN = D_SSM // S5_BLOCKS
S5_BLOCK_ST = SSM_LANES // S5_BLOCKS
SCAN_LANES = 512
TN_K_BLOCK = 2048
DIRECT_GATHER_BYTES = 0
LAST_PIECES = 2
LAST_PIECE = "ffn1_w_in_q%d"
_G0 = math.sqrt(2.0 / math.pi)
_G1 = 0.044715


def _dot(a, b):
    return jnp.dot(a, b, preferred_element_type=F32)


def _dot_nt(a, b):
    return lax.dot_general(a, b, (((1,), (1,)), ((), ())), preferred_element_type=F32)


def _dot_tn(a, b):
    return lax.dot_general(a, b, (((0,), (0,)), ((), ())), preferred_element_type=F32)


def _sigmoid(x):
    return 1.0 / (1.0 + jnp.exp(-x))


def _gelu(x):
    t = jnp.tanh(_G0 * (x + _G1 * x * x * x))
    return 0.5 * x * (1.0 + t)


def _gelu_grad(x):
    t = jnp.tanh(_G0 * (x + _G1 * x * x * x))
    return 0.5 * (1.0 + t) + 0.5 * x * (1.0 - t * t) * _G0 * (1.0 + 3.0 * _G1 * x * x)


def _ln_fwd(r, g, b):
    mu = jnp.mean(r, axis=-1, keepdims=True)
    d = r - mu
    var = jnp.mean(d * d, axis=-1, keepdims=True)
    rstd = lax.rsqrt(var + LN_EPS)
    xh = d * rstd
    return xh * g + b, xh, rstd


def _ln_bwd(dy, xh, rstd, g):
    dxh = dy * g
    m1 = jnp.mean(dxh, axis=-1, keepdims=True)
    m2 = jnp.mean(dxh * xh, axis=-1, keepdims=True)
    return rstd * (dxh - m1 - xh * m2)


def _resident(shape):
    nd = len(shape)
    return pl.BlockSpec(shape, lambda *_: (0,) * nd, pipeline_mode=pl.Buffered(1))


def _fixed(shape):
    nd = len(shape)
    return pl.BlockSpec(shape, lambda *_: (0,) * nd)


def _rows(tm, cols):
    return pl.BlockSpec((tm, cols), lambda i: (i, 0))


def _cols(rows, tm):
    return pl.BlockSpec((rows, tm), lambda i: (0, i))


def _params(sem):
    return pltpu.CompilerParams(dimension_semantics=sem, vmem_limit_bytes=VMEM_LIMIT_BYTES)


class _Exchange:
    def __init__(self, args, out_shape, sems, start, finish):
        self.args, self.out_shape, self.sems = list(args), list(out_shape), list(sems)
        self.start, self.finish = start, finish
        self.cuts = [(0, len(self.out_shape))]


def _call(body, name, grid, in_specs, out_specs, out_shape, args, scratch=(), sem=None, bg=None, aliases=None):
    aliases = {} if aliases is None else aliases
    in_specs, args = list(in_specs), list(args)
    fn = body
    if bg is not None:
        n_args = len(args)

        def fn(*refs):
            body(*refs[:n_args], *refs[n_args + 1:])

        in_specs.append(pl.BlockSpec(memory_space=pl.ANY))
        args.append(bg)
    res = pl.pallas_call(fn, name=name, grid=grid, out_shape=tuple(out_shape), in_specs=in_specs,
                         out_specs=tuple(out_specs), scratch_shapes=list(scratch),
                         input_output_aliases=aliases, compiler_params=_params(sem))(*args)
    return tuple(res), ()


def _run_exchange_on_sequencer(ex, name, collective_id):
    n_i, n_o = len(ex.args), len(ex.out_shape)

    def body(*refs):
        ins, outs, sems = refs[:n_i], refs[n_i:n_i + n_o], refs[n_i + n_o:]
        x, y, c = lax.axis_index("x"), lax.axis_index("y"), lax.axis_index("c")
        barrier = pltpu.get_barrier_semaphore()
        for peer in [(x, y, 1 - c), (1 - x, y, c), (x, 1 - y, c), (1 - x, 1 - y, c)]:
            pl.semaphore_signal(barrier, inc=1, device_id=peer, device_id_type=MESH)
        pl.semaphore_wait(barrier, 4)
        ex.start(ins, outs, sems)
        ex.finish(ins, outs, sems)

    return tuple(pl.kernel(body, out_type=tuple(ex.out_shape),
                           mesh=plsc.ScalarSubcoreMesh(axis_name="sequencer", num_cores=1),
                           scratch_types=list(ex.sems), name=name,
                           compiler_params=pltpu.CompilerParams(collective_id=collective_id))(*ex.args))


def _join(exchanges):
    cuts = []
    a = o = q = 0
    for e in exchanges:
        cuts.append((a, a + len(e.args), o, o + len(e.out_shape), q, q + len(e.sems)))
        a, o, q = cuts[-1][1], cuts[-1][3], cuts[-1][5]

    def start(ins, outs, sems):
        for e, (a0, a1, o0, o1, q0, q1) in zip(exchanges, cuts):
            e.start(ins[a0:a1], outs[o0:o1], sems[q0:q1])

    def finish(ins, outs, sems):
        for e, (a0, a1, o0, o1, q0, q1) in zip(exchanges, cuts):
            e.finish(ins[a0:a1], outs[o0:o1], sems[q0:q1])

    joined = _Exchange(sum((e.args for e in exchanges), []), sum((e.out_shape for e in exchanges), []),
                       sum((e.sems for e in exchanges), []), start, finish)
    joined.cuts = [(c[2], c[3]) for c in cuts]
    return joined


def _ffn_proj(x, w_in, tm, name, bg=None):
    t = x.shape[0]
    nch = D_FF // FFN_COLS

    def body(x_ref, win_ref, xbt_ref, h_ref, a_ref):
        xb = x_ref[...].astype(BF16)
        xbt_ref[...] = xb.T
        for k in range(nch):
            cg = slice(k * FFN_COLS, (k + 1) * FFN_COLS)
            cu = slice(D_FF + k * FFN_COLS, D_FF + (k + 1) * FFN_COLS)
            hg = _dot(xb, win_ref[k])
            hu = _dot(xb, win_ref[nch + k])
            h_ref[:, cg] = hg.astype(BF16)
            h_ref[:, cu] = hu.astype(BF16)
            a_ref[:, cg] = (hg * _sigmoid(hg) * hu).astype(BF16)

    return _call(
        body, name, (t // tm,),
        [_rows(tm, D_MODEL), _resident((2 * nch, D_MODEL, FFN_COLS))],
        (_cols(D_MODEL, tm), _rows(tm, 2 * D_FF), _rows(tm, D_FF)),
        (SDS((D_MODEL, t), BF16), SDS((t, 2 * D_FF), BF16), SDS((t, D_FF), BF16)),
        (x, w_in), sem=("parallel",), bg=bg)


def _ffn_out(x, a, w_out, g, b, tm, name, bg=None):
    t = x.shape[0]

    def body(x_ref, a_ref, wout_ref, g_ref, b_ref, xn_ref, xh_ref, rstd_ref):
        f = _dot(a_ref[...], wout_ref[...])
        y, xh, rstd = _ln_fwd(ALPHA * x_ref[...] + 0.5 * f, g_ref[...], b_ref[...])
        xn_ref[...] = y
        xh_ref[...] = xh
        rstd_ref[...] = rstd

    return _call(
        body, name, (t // tm,),
        [_rows(tm, D_MODEL), _rows(tm, D_FF), _resident((D_FF, D_MODEL)), _fixed((1, D_MODEL)), _fixed((1, D_MODEL))],
        (_rows(tm, D_MODEL), _rows(tm, D_MODEL), _rows(tm, 1)),
        (SDS((t, D_MODEL), F32), SDS((t, D_MODEL), F32), SDS((t, 1), F32)),
        (x, a, w_out, g, b), sem=("parallel",), bg=bg)


def _ffn_bwd(dxn, xh, rstd, h, w_in, w_out, g, tm, name, bg=None):
    t = dxn.shape[0]
    nch = D_FF // FFN_COLS

    def body(dxn_ref, xh_ref, rstd_ref, h_ref, win_ref, wout_ref, g_ref,
             dx_ref, dh_ref, df_ref, dg_ref, db_ref):
        @pl.when(pl.program_id(0) == 0)
        def _():
            dg_ref[...] = jnp.zeros_like(dg_ref)
            db_ref[...] = jnp.zeros_like(db_ref)

        dy = dxn_ref[...]
        xhv = xh_ref[...]
        dr = _ln_bwd(dy, xhv, rstd_ref[...], g_ref[...])
        dg_ref[...] += jnp.sum(dy * xhv, axis=0, keepdims=True)
        db_ref[...] += jnp.sum(dy, axis=0, keepdims=True)
        df = (0.5 * dr).astype(BF16)
        df_ref[...] = df
        dx = ALPHA * dr
        das = [_dot_nt(df, wout_ref[k * FFN_COLS:(k + 1) * FFN_COLS, :]) for k in range(nch)]
        for k in range(nch):
            cg = slice(k * FFN_COLS, (k + 1) * FFN_COLS)
            cu = slice(D_FF + k * FFN_COLS, D_FF + (k + 1) * FFN_COLS)
            hg = h_ref[:, cg].astype(F32)
            hu = h_ref[:, cu].astype(F32)
            sg = _sigmoid(hg)
            silu = hg * sg
            da = das[k]
            dhu = (da * silu).astype(BF16)
            dhg = (da * hu * (sg * (1.0 + hg * (1.0 - sg)))).astype(BF16)
            dh_ref[:, cg] = dhg
            dh_ref[:, cu] = dhu
            dx = dx + _dot_nt(dhg, win_ref[k]) + _dot_nt(dhu, win_ref[nch + k])
        dx_ref[...] = dx

    return _call(
        body, name, (t // tm,),
        [_rows(tm, D_MODEL), _rows(tm, D_MODEL), _rows(tm, 1), _rows(tm, 2 * D_FF),
         _resident((2 * nch, D_MODEL, FFN_COLS)), _resident((D_FF, D_MODEL)), _fixed((1, D_MODEL))],
        (_rows(tm, D_MODEL), _rows(tm, 2 * D_FF), _rows(tm, D_MODEL),
         _fixed((1, D_MODEL)), _fixed((1, D_MODEL))),
        (SDS((t, D_MODEL), F32), SDS((t, 2 * D_FF), BF16), SDS((t, D_MODEL), BF16),
         SDS((1, D_MODEL), F32), SDS((1, D_MODEL), F32)),
        (dxn, xh, rstd, h, w_in, w_out, g), sem=("arbitrary",), bg=bg)


def _tn_matmul(a, b, name, bm, bn, col_block=0, total_cols=None, prev=None, bg=None, a_cols=None, a_t=False):
    t, m = a.shape[::-1] if a_t else a.shape
    a_first = 0
    if a_cols is not None:
        a_first, m = a_cols[0], a_cols[1] * bm
    n = b.shape[1]
    total_cols = n if total_cols is None else total_cols
    bk = min(TN_K_BLOCK, t)
    nk = t // bk
    n_in = 2 if prev is None else 4

    def body(*refs):
        a_ref, b_ref = refs[0], refs[1]
        o_ref, ob_ref = refs[n_in], refs[n_in + 1]
        k = pl.program_id(2)

        @pl.when(k == 0)
        def _():
            o_ref[...] = jnp.zeros_like(o_ref)

        o_ref[...] += _dot(a_ref[...], b_ref[...]) if a_t else _dot_tn(a_ref[...], b_ref[...])

        @pl.when(k == nk - 1)
        def _():
            ob_ref[...] = o_ref[...].astype(BF16)

    a_spec = (pl.BlockSpec((bm, bk), lambda i, j, k: (i + a_first, k)) if a_t
              else pl.BlockSpec((bk, bm), lambda i, j, k: (k, i + a_first)))
    in_specs = [a_spec, pl.BlockSpec((bk, bn), lambda i, j, k: (k, j))]
    args = [a, b]
    aliases = {}
    if prev is not None:
        in_specs += [pl.BlockSpec(memory_space=pl.ANY), pl.BlockSpec(memory_space=pl.ANY)]
        args += list(prev)
        aliases = {2: 0, 3: 1}
        if any(bg is p for p in prev):
            bg = None
    out_spec = pl.BlockSpec((bm, bn), lambda i, j, k: (i, j + col_block))
    return _call(body, name, (m // bm, n // bn, nk), in_specs, (out_spec, out_spec),
                 (SDS((m, total_cols), F32), SDS((m, total_cols), BF16)), args,
                 sem=("parallel", "parallel", "arbitrary"), bg=bg, aliases=aliases)


def _mixin_fwd(x1, w, tm, bg=None):
    t = x1.shape[0]

    def body(x_ref, w_ref, xbt_ref, za_ref, zuv_ref, gab_ref):
        xb = x_ref[...].astype(BF16)
        xbt_ref[...] = xb.T
        za_ref[...] = _dot(xb, w_ref[:, 0:512]).astype(BF16)
        zuv_ref[...] = _dot(xb, w_ref[:, 512:1536]).astype(BF16)
        gab_ref[...] = _dot(xb, w_ref[:, 1536:3584]).astype(BF16)

    return _call(
        body, "mixin_fwd", (t // tm,),
        [_rows(tm, D_MODEL), _resident((D_MODEL, 3584))],
        (_cols(D_MODEL, tm), _rows(tm, 512), _rows(tm, 1024), _rows(tm, 2048)),
        (SDS((D_MODEL, t), BF16), SDS((t, 512), BF16), SDS((t, 1024), BF16), SDS((t, 2048), BF16)),
        (x1, w), sem=("parallel",), bg=bg)


def _mixin_bwd(dx1a, dza, dzuv, dgab, w, tm, bg=None):
    t = dx1a.shape[0]

    def body(d_ref, dza_ref, dzuv_ref, dgab_ref, w_ref, dx_ref):
        dx_ref[...] = (d_ref[...] + _dot_nt(dza_ref[...], w_ref[:, 0:512])
                       + _dot_nt(dzuv_ref[...], w_ref[:, 512:1536])
                       + _dot_nt(dgab_ref[...], w_ref[:, 1536:3584]))

    return _call(
        body, "mixin_bwd", (t // tm,),
        [_rows(tm, D_MODEL), _rows(tm, 512), _rows(tm, 1024), _rows(tm, 2048), _resident((D_MODEL, 3584))],
        (_rows(tm, D_MODEL),), (SDS((t, D_MODEL), F32),),
        (dx1a, dza, dzuv, dgab, w), sem=("parallel",), bg=bg)


def _unrolled(lo, hi, body, carry):
    for j in range(lo, hi):
        carry = body(j, carry)
    return carry


def _scan_fwd(hr_ref, hi_ref, a_ref, ap_ref, carry_ref, seg, cin_ref):
    for lc in range(SSM_LANES // SCAN_LANES):
        ls = slice(lc * SCAN_LANES, (lc + 1) * SCAN_LANES)
        a_r = jnp.broadcast_to(a_ref[0:1, ls], (8, SCAN_LANES))
        a_i = jnp.broadcast_to(a_ref[1:2, ls], (8, SCAN_LANES))

        def step(j, hc, ls=ls, a_r=a_r, a_i=a_i):
            h_r, h_i = hc
            rows = pl.ds(j * 8, 8)
            n_r = a_r * h_r - a_i * h_i + hr_ref[rows, ls]
            n_i = a_r * h_i + a_i * h_r + hi_ref[rows, ls]
            hr_ref[rows, ls] = n_r
            hi_ref[rows, ls] = n_i
            return n_r, n_i

        zero = jnp.zeros((8, SCAN_LANES), F32)
        f_r, f_i = _unrolled(0, seg, step, (zero, zero))
        c_r = carry_ref[0:1, ls]
        c_i = carry_ref[1:2, ls]
        p_r = ap_ref[0:1, ls]
        p_i = ap_ref[1:2, ls]
        rows_r, rows_i = [], []
        for s in range(8):
            rows_r.append(c_r)
            rows_i.append(c_i)
            c_r, c_i = (f_r[s:s + 1] + p_r * c_r - p_i * c_i,
                        f_i[s:s + 1] + p_r * c_i + p_i * c_r)
        carry_ref[0:1, ls] = c_r
        carry_ref[1:2, ls] = c_i
        cin_r = jnp.concatenate(rows_r, axis=0)
        cin_i = jnp.concatenate(rows_i, axis=0)
        if cin_ref is not None:
            cin_ref[0, :, ls] = cin_r
            cin_ref[1, :, ls] = cin_i

        def fix(j, cc, ls=ls, a_r=a_r, a_i=a_i):
            c_r, c_i = cc
            c_r, c_i = a_r * c_r - a_i * c_i, a_r * c_i + a_i * c_r
            rows = pl.ds(j * 8, 8)
            hr_ref[rows, ls] = hr_ref[rows, ls] + c_r
            hi_ref[rows, ls] = hi_ref[rows, ls] + c_i
            return c_r, c_i

        _unrolled(0, seg, fix, (cin_r, cin_i))


def _scan_bwd(gr_ref, gi_ref, hr_ref, hi_ref, cin_ref, a_ref, ap_ref, rcarry_ref, da_ref, seg):
    for lc in range(SSM_LANES // SCAN_LANES):
        ls = slice(lc * SCAN_LANES, (lc + 1) * SCAN_LANES)
        a_r = jnp.broadcast_to(a_ref[0:1, ls], (8, SCAN_LANES))
        a_i = jnp.broadcast_to(a_ref[1:2, ls], (8, SCAN_LANES))

        def step(t, gc, ls=ls, a_r=a_r, a_i=a_i):
            g_r, g_i = gc
            rows = pl.ds((seg - 1 - t) * 8, 8)
            n_r = gr_ref[rows, ls] + a_r * g_r + a_i * g_i
            n_i = gi_ref[rows, ls] + a_r * g_i - a_i * g_r
            gr_ref[rows, ls] = n_r
            gi_ref[rows, ls] = n_i
            return n_r, n_i

        zero = jnp.zeros((8, SCAN_LANES), F32)
        f_r, f_i = _unrolled(0, seg, step, (zero, zero))
        c_r = rcarry_ref[0:1, ls]
        c_i = rcarry_ref[1:2, ls]
        p_r = ap_ref[0:1, ls]
        p_i = ap_ref[1:2, ls]
        rows_r, rows_i = [None] * 8, [None] * 8
        for s in range(7, -1, -1):
            rows_r[s] = c_r
            rows_i[s] = c_i
            c_r, c_i = (f_r[s:s + 1] + p_r * c_r + p_i * c_i,
                        f_i[s:s + 1] + p_r * c_i - p_i * c_r)
        rcarry_ref[0:1, ls] = c_r
        rcarry_ref[1:2, ls] = c_i
        cin_r = jnp.concatenate(rows_r, axis=0)
        cin_i = jnp.concatenate(rows_i, axis=0)

        def fix_row(j_rows, hp_r, hp_i, cc, ls=ls, a_r=a_r, a_i=a_i):
            c_r, c_i, acc_r, acc_i = cc
            c_r, c_i = a_r * c_r + a_i * c_i, a_r * c_i - a_i * c_r
            g_r = gr_ref[j_rows, ls] + c_r
            g_i = gi_ref[j_rows, ls] + c_i
            gr_ref[j_rows, ls] = g_r
            gi_ref[j_rows, ls] = g_i
            acc_r = acc_r + g_r * hp_r + g_i * hp_i
            acc_i = acc_i + g_i * hp_r - g_r * hp_i
            return c_r, c_i, acc_r, acc_i

        def fix(t, cc, ls=ls, fix_row=fix_row):
            j = seg - 1 - t
            rows = pl.ds(j * 8, 8)
            prev = pl.ds((j - 1) * 8, 8)
            return fix_row(rows, hr_ref[prev, ls], hi_ref[prev, ls], cc)

        cc = _unrolled(0, seg - 1, fix, (cin_r, cin_i, zero, zero))
        _, _, acc_r, acc_i = fix_row(pl.ds(0, 8), cin_ref[0, :, ls], cin_ref[1, :, ls], cc)
        da_ref[0, :, ls] += acc_r
        da_ref[1, :, ls] += acc_i


def _s5_fwd(za, sp, bsz, seq, tb, bg=None):
    nb = seq // tb
    seg = tb // 8
    t = bsz * seq

    def body(za_ref, perm_ref, permt_ref, mre_ref, mim_ref, nre_ref, nim_ref, a_ref, ap_ref,
             dsk_ref, gw_ref, gb_ref, out_ref, outt_ref, y2_ref, car_ref, hr_ref, hi_ref, carry_ref):
        @pl.when(pl.program_id(1) == 0)
        def _():
            carry_ref[...] = jnp.zeros_like(carry_ref)

        car_ref[0] = carry_ref[...]
        up = _dot(perm_ref[...], za_ref[...])
        upb = up.astype(BF16)
        for bb in range(S5_BLOCKS):
            ub = upb[:, bb * S5_BLOCK_IN:(bb + 1) * S5_BLOCK_IN]
            st = slice(bb * S5_BLOCK_ST, (bb + 1) * S5_BLOCK_ST)
            hr_ref[:, st] = _dot(ub, mre_ref[bb])
            hi_ref[:, st] = _dot(ub, mim_ref[bb])
        _scan_fwd(hr_ref, hi_ref, a_ref, ap_ref, carry_ref, seg, None)
        ys = []
        for bb in range(S5_BLOCKS):
            st = slice(bb * S5_BLOCK_ST, (bb + 1) * S5_BLOCK_ST)
            ys.append(_dot(hr_ref[:, st].astype(BF16), nre_ref[bb])
                      - _dot(hi_ref[:, st].astype(BF16), nim_ref[bb]))
        y2 = jnp.concatenate(ys, axis=1) + dsk_ref[...] * up
        y2_ref[...] = y2
        y3 = _gelu(y2)
        gl = _dot(y3.astype(BF16), gw_ref[...]) + gb_ref[...]
        oa = y3 * _sigmoid(gl)
        out = _dot(permt_ref[...], oa.astype(BF16)).astype(BF16)
        out_ref[...] = out
        outt_ref[...] = out.T

    blk = pl.BlockSpec((tb, D_SSM), lambda b, j: (b * nb + j, 0))
    blk_t = pl.BlockSpec((D_SSM, tb), lambda b, j: (0, b * nb + j))
    m_shape = (S5_BLOCKS, S5_BLOCK_IN, S5_BLOCK_ST)
    n_shape = (S5_BLOCKS, S5_BLOCK_ST, S5_BLOCK_IN)
    return _call(
        body, "s5_fwd", (bsz, nb),
        [blk, _fixed((tb, tb)), _fixed((tb, tb)), _fixed(m_shape), _fixed(m_shape), _fixed(n_shape),
         _fixed(n_shape), _fixed((2, SSM_LANES)), _fixed((2, SSM_LANES)), _fixed((1, D_SSM)),
         _fixed((D_SSM, D_SSM)), _fixed((1, D_SSM))],
        (blk, blk_t, blk, pl.BlockSpec((1, 2, SSM_LANES), lambda b, j: (b * nb + j, 0, 0))),
        (SDS((t, D_SSM), BF16), SDS((D_SSM, t), BF16), SDS((t, D_SSM), F32), SDS((bsz * nb, 2, SSM_LANES), F32)),
        (za, sp["perm"], sp["permt"], sp["mre"], sp["mim"], sp["nre"], sp["nim"], sp["a"], sp["ap"],
         sp["dskip"], sp["glu_w"], sp["glu_b"]),
        scratch=[pltpu.VMEM((tb, SSM_LANES), F32), pltpu.VMEM((tb, SSM_LANES), F32),
                 pltpu.VMEM((2, SSM_LANES), F32)],
        sem=("arbitrary", "arbitrary"), bg=bg)


def _s5_bwd(za, y2p, doa, carries, sp, bsz, seq, tb, bg=None):
    nb = seq // tb
    seg = tb // 8
    t = bsz * seq

    def body(za_ref, y2_ref, doa_ref, car_ref, perm_ref, permt_ref, mre_ref, mim_ref, mtre_ref, mtim_ref,
             nre_ref, nim_ref, ntre_ref, ntim_ref, a_ref, ap_ref, dsk_ref, gw_ref, gwt_ref, gb_ref,
             dza_ref, dmr_ref, dmi_ref, dnr_ref, dni_ref, da_ref, ddsk_ref, dgw_ref, dgb_ref,
             hr_ref, hi_ref, gr_ref, gi_ref, cin_ref, carry_ref, rcarry_ref):
        first = jnp.logical_and(pl.program_id(0) == 0, pl.program_id(1) == 0)

        @pl.when(first)
        def _():
            for r in (dmr_ref, dmi_ref, dnr_ref, dni_ref, da_ref, ddsk_ref, dgw_ref, dgb_ref):
                r[...] = jnp.zeros_like(r)

        @pl.when(pl.program_id(1) == 0)
        def _():
            rcarry_ref[...] = jnp.zeros_like(rcarry_ref)

        carry_ref[...] = car_ref[0]
        perm = perm_ref[...]
        up = _dot(perm, za_ref[...])
        upb = up.astype(BF16)
        for bb in range(S5_BLOCKS):
            ub = upb[:, bb * S5_BLOCK_IN:(bb + 1) * S5_BLOCK_IN]
            st = slice(bb * S5_BLOCK_ST, (bb + 1) * S5_BLOCK_ST)
            hr_ref[:, st] = _dot(ub, mre_ref[bb])
            hi_ref[:, st] = _dot(ub, mim_ref[bb])
        _scan_fwd(hr_ref, hi_ref, a_ref, ap_ref, carry_ref, seg, cin_ref)

        y2 = y2_ref[...]
        y3 = _gelu(y2)
        y3b = y3.astype(BF16)
        sg = _sigmoid(_dot(y3b, gw_ref[...]) + gb_ref[...])
        d0 = doa_ref[...]
        d_hi = d0.astype(BF16)
        d1 = d0 - d_hi.astype(F32)
        d_mid = d1.astype(BF16)
        d_lo = (d1 - d_mid.astype(F32)).astype(BF16)
        doap = _dot(perm, d_hi) + _dot(perm, d_mid) + _dot(perm, d_lo)
        dgl = doap * y3 * sg * (1.0 - sg)
        dglb = dgl.astype(BF16)
        dy3 = doap * sg + _dot(dglb, gwt_ref[...])
        dgw_ref[...] += _dot_tn(y3b, dglb)
        dgb_ref[...] += jnp.sum(dgl, axis=0, keepdims=True)
        dy2 = dy3 * _gelu_grad(y2)
        ddsk_ref[...] += jnp.sum(dy2 * up, axis=0, keepdims=True)
        dyb = dy2.astype(BF16)
        for bb in range(S5_BLOCKS):
            dyc = dyb[:, bb * S5_BLOCK_IN:(bb + 1) * S5_BLOCK_IN]
            st = slice(bb * S5_BLOCK_ST, (bb + 1) * S5_BLOCK_ST)
            gr_ref[:, st] = _dot(dyc, ntre_ref[bb])
            gi_ref[:, st] = -_dot(dyc, ntim_ref[bb])
            dnr_ref[bb] += _dot_tn(hr_ref[:, st].astype(BF16), dyc)
            dni_ref[bb] += -_dot_tn(hi_ref[:, st].astype(BF16), dyc)
        _scan_bwd(gr_ref, gi_ref, hr_ref, hi_ref, cin_ref, a_ref, ap_ref, rcarry_ref, da_ref, seg)
        dus = []
        for bb in range(S5_BLOCKS):
            st = slice(bb * S5_BLOCK_ST, (bb + 1) * S5_BLOCK_ST)
            grb = gr_ref[:, st].astype(BF16)
            gib = gi_ref[:, st].astype(BF16)
            dus.append(_dot(grb, mtre_ref[bb]) + _dot(gib, mtim_ref[bb]))
            ub = upb[:, bb * S5_BLOCK_IN:(bb + 1) * S5_BLOCK_IN]
            dmr_ref[bb] += _dot_tn(ub, grb)
            dmi_ref[bb] += _dot_tn(ub, gib)
        du = jnp.concatenate(dus, axis=1) + dy2 * dsk_ref[...]
        dza_ref[...] = _dot(permt_ref[...], du.astype(BF16)).astype(BF16)

    def rev(b, j):
        return (b * nb + (nb - 1 - j), 0)

    blk = pl.BlockSpec((tb, D_SSM), rev)
    m_shape = (S5_BLOCKS, S5_BLOCK_IN, S5_BLOCK_ST)
    n_shape = (S5_BLOCKS, S5_BLOCK_ST, S5_BLOCK_IN)
    return _call(
        body, "s5_bwd", (bsz, nb),
        [blk, blk, blk, pl.BlockSpec((1, 2, SSM_LANES), lambda b, j: (b * nb + (nb - 1 - j), 0, 0)),
         _fixed((tb, tb)), _fixed((tb, tb)), _fixed(m_shape), _fixed(m_shape), _fixed(n_shape), _fixed(n_shape),
         _fixed(n_shape), _fixed(n_shape), _fixed(m_shape), _fixed(m_shape),
         _fixed((2, SSM_LANES)), _fixed((2, SSM_LANES)), _fixed((1, D_SSM)),
         _fixed((D_SSM, D_SSM)), _fixed((D_SSM, D_SSM)), _fixed((1, D_SSM))],
        (blk, _fixed(m_shape), _fixed(m_shape), _fixed(n_shape), _fixed(n_shape),
         _fixed((2, 8, SSM_LANES)), _fixed((1, D_SSM)), _fixed((D_SSM, D_SSM)), _fixed((1, D_SSM))),
        (SDS((t, D_SSM), BF16), SDS(m_shape, F32), SDS(m_shape, F32), SDS(n_shape, F32), SDS(n_shape, F32),
         SDS((2, 8, SSM_LANES), F32), SDS((1, D_SSM), F32), SDS((D_SSM, D_SSM), F32), SDS((1, D_SSM), F32)),
        (za, y2p, doa, carries, sp["perm"], sp["permt"], sp["mre"], sp["mim"], sp["mtre"], sp["mtim"],
         sp["nre"], sp["nim"], sp["ntre"], sp["ntim"], sp["a"], sp["ap"], sp["dskip"], sp["glu_w"],
         sp["glu_wt"], sp["glu_b"]),
        scratch=[pltpu.VMEM((tb, SSM_LANES), F32), pltpu.VMEM((tb, SSM_LANES), F32),
                 pltpu.VMEM((tb, SSM_LANES), F32), pltpu.VMEM((tb, SSM_LANES), F32),
                 pltpu.VMEM((2, 8, SSM_LANES), F32), pltpu.VMEM((2, SSM_LANES), F32),
                 pltpu.VMEM((2, SSM_LANES), F32)],
        sem=("arbitrary", "arbitrary"), bg=bg)


def _gmlp_spatial(ws_ref, vb):
    lane = lax.broadcasted_iota(jnp.int32, (CHUNK, 128), 1)
    parts = []
    for j in range(GMLP_HEADS // 2):
        vp = vb[:, 128 * j:128 * (j + 1)]
        parts.append(jnp.where(lane < GMLP_HEAD_DIM, _dot(ws_ref[2 * j], vp), _dot(ws_ref[2 * j + 1], vp)))
    return jnp.concatenate(parts, axis=1)


def _gmlp_fwd(zuv, ln_g, ln_b, wsm, bias, bg=None):
    t = zuv.shape[0]

    def body(z_ref, g_ref, b_ref, ws_ref, bias_ref, out_ref, outt_ref):
        u = _gelu(z_ref[:, 0:D_GMLP].astype(F32))
        v0 = _gelu(z_ref[:, D_GMLP:2 * D_GMLP].astype(F32))
        v, _, _ = _ln_fwd(v0, g_ref[...], b_ref[...])
        s = _gmlp_spatial(ws_ref, v.astype(BF16)) + bias_ref[...]
        out = (u * s).astype(BF16)
        out_ref[...] = out
        outt_ref[...] = out.T

    return _call(
        body, "gmlp_fwd", (t // CHUNK,),
        [_rows(CHUNK, 2 * D_GMLP), _fixed((1, D_GMLP)), _fixed((1, D_GMLP)),
         _fixed((GMLP_HEADS, CHUNK, CHUNK)), _fixed((CHUNK, D_GMLP))],
        (_rows(CHUNK, D_GMLP), _cols(D_GMLP, CHUNK)), (SDS((t, D_GMLP), BF16), SDS((D_GMLP, t), BF16)),
        (zuv, ln_g, ln_b, wsm, bias), sem=("parallel",), bg=bg)


def _gmlp_bwd(zuv, dgm, ln_g, ln_b, wsm, wsmt, bias, bg=None):
    t = zuv.shape[0]

    def body(z_ref, d_ref, g_ref, b_ref, ws_ref, wst_ref, bias_ref,
             dz_ref, dws_ref, dbias_ref, dg_ref, db_ref):
        @pl.when(pl.program_id(0) == 0)
        def _():
            for r in (dws_ref, dbias_ref, dg_ref, db_ref):
                r[...] = jnp.zeros_like(r)

        zu = z_ref[:, 0:D_GMLP].astype(F32)
        zv = z_ref[:, D_GMLP:2 * D_GMLP].astype(F32)
        u = _gelu(zu)
        v0 = _gelu(zv)
        gam = g_ref[...]
        v, vhat, rstd = _ln_fwd(v0, gam, b_ref[...])
        vb = v.astype(BF16)
        s = _gmlp_spatial(ws_ref, vb) + bias_ref[...]
        d = d_ref[...]
        dz_ref[:, 0:D_GMLP] = (d * s * _gelu_grad(zu)).astype(BF16)
        ds = d * u
        dbias_ref[...] += ds
        dsb = ds.astype(BF16)
        lane = lax.broadcasted_iota(jnp.int32, (CHUNK, 128), 1)
        tril = (lax.broadcasted_iota(jnp.int32, (CHUNK, CHUNK), 0)
                >= lax.broadcasted_iota(jnp.int32, (CHUNK, CHUNK), 1))
        zero_b = jnp.zeros((CHUNK, 128), BF16)
        parts = []
        for j in range(GMLP_HEADS // 2):
            dsp = dsb[:, 128 * j:128 * (j + 1)]
            vp = vb[:, 128 * j:128 * (j + 1)]
            parts.append(jnp.where(lane < GMLP_HEAD_DIM, _dot(wst_ref[2 * j], dsp),
                                   _dot(wst_ref[2 * j + 1], dsp)))
            lo = jnp.where(lane < GMLP_HEAD_DIM, dsp, zero_b)
            hi = jnp.where(lane < GMLP_HEAD_DIM, zero_b, dsp)
            dws_ref[2 * j] += jnp.where(tril, _dot_nt(lo, vp), 0.0)
            dws_ref[2 * j + 1] += jnp.where(tril, _dot_nt(hi, vp), 0.0)
        dv = jnp.concatenate(parts, axis=1)
        dg_ref[...] += jnp.sum(dv * vhat, axis=0, keepdims=True)
        db_ref[...] += jnp.sum(dv, axis=0, keepdims=True)
        dz_ref[:, D_GMLP:2 * D_GMLP] = (_ln_bwd(dv, vhat, rstd, gam) * _gelu_grad(zv)).astype(BF16)

    return _call(
        body, "gmlp_bwd", (t // CHUNK,),
        [_rows(CHUNK, 2 * D_GMLP), _rows(CHUNK, D_GMLP), _fixed((1, D_GMLP)), _fixed((1, D_GMLP)),
         _fixed((GMLP_HEADS, CHUNK, CHUNK)), _fixed((GMLP_HEADS, CHUNK, CHUNK)), _fixed((CHUNK, D_GMLP))],
        (_rows(CHUNK, 2 * D_GMLP), _fixed((GMLP_HEADS, CHUNK, CHUNK)), _fixed((CHUNK, D_GMLP)),
         _fixed((1, D_GMLP)), _fixed((1, D_GMLP))),
        (SDS((t, 2 * D_GMLP), BF16), SDS((GMLP_HEADS, CHUNK, CHUNK), F32), SDS((CHUNK, D_GMLP), F32),
         SDS((1, D_GMLP), F32), SDS((1, D_GMLP), F32)),
        (zuv, dgm, ln_g, ln_b, wsm, wsmt, bias), sem=("arbitrary",), bg=bg)


def _mixout_fwd(x1, s5o, gm, gab, ua, ub, wmo, g, b, tm, bg=None):
    t = x1.shape[0]

    def body(x_ref, s_ref, m_ref, gab_ref, ua_ref, ub_ref, wmo_ref, g_ref, b_ref,
             xn_ref, xh_ref, rstd_ref):
        ya = _dot(s_ref[...], ua_ref[...])
        yb = _dot(m_ref[...], ub_ref[...])
        mix = (_sigmoid(gab_ref[:, 0:D_MODEL].astype(F32)) * ya
               + _sigmoid(gab_ref[:, D_MODEL:2 * D_MODEL].astype(F32)) * yb)
        r = ALPHA * x_ref[...] + _dot(mix.astype(BF16), wmo_ref[...])
        y, xh, rstd = _ln_fwd(r, g_ref[...], b_ref[...])
        xn_ref[...] = y
        xh_ref[...] = xh
        rstd_ref[...] = rstd

    return _call(
        body, "mixout_fwd", (t // tm,),
        [_rows(tm, D_MODEL), _rows(tm, D_SSM), _rows(tm, D_GMLP), _rows(tm, 2 * D_MODEL),
         _resident((D_SSM, D_MODEL)), _resident((D_GMLP, D_MODEL)), _resident((D_MODEL, D_MODEL)),
         _fixed((1, D_MODEL)), _fixed((1, D_MODEL))],
        (_rows(tm, D_MODEL), _rows(tm, D_MODEL), _rows(tm, 1)),
        (SDS((t, D_MODEL), F32), SDS((t, D_MODEL), F32), SDS((t, 1), F32)),
        (x1, s5o, gm, gab, ua, ub, wmo, g, b), sem=("parallel",), bg=bg)


def _mixout_bwd(dx2, xh, rstd, s5o, gm, gab, ua, ub, wmo, g, tm, bg=None):
    t = dx2.shape[0]

    def body(d_ref, xh_ref, rstd_ref, s_ref, m_ref, gab_ref, ua_ref, ub_ref, wmo_ref, g_ref,
             dx1_ref, dmx_ref, mb_ref, dya_ref, dyb_ref, ds5_ref, dgm_ref, dgab_ref, dg_ref, db_ref):
        @pl.when(pl.program_id(0) == 0)
        def _():
            dg_ref[...] = jnp.zeros_like(dg_ref)
            db_ref[...] = jnp.zeros_like(db_ref)

        dy = d_ref[...]
        xhv = xh_ref[...]
        dr = _ln_bwd(dy, xhv, rstd_ref[...], g_ref[...])
        dg_ref[...] += jnp.sum(dy * xhv, axis=0, keepdims=True)
        db_ref[...] += jnp.sum(dy, axis=0, keepdims=True)
        dx1_ref[...] = ALPHA * dr
        drb = dr.astype(BF16)
        dmx_ref[...] = drb
        dm = _dot_nt(drb, wmo_ref[...])
        ya = _dot(s_ref[...], ua_ref[...])
        yb = _dot(m_ref[...], ub_ref[...])
        sa = _sigmoid(gab_ref[:, 0:D_MODEL].astype(F32))
        sb = _sigmoid(gab_ref[:, D_MODEL:2 * D_MODEL].astype(F32))
        mb_ref[...] = (sa * ya + sb * yb).astype(BF16).T
        dya = (dm * sa).astype(BF16)
        dyb = (dm * sb).astype(BF16)
        dya_ref[...] = dya
        dyb_ref[...] = dyb
        dgab_ref[:, 0:D_MODEL] = (dm * ya * sa * (1.0 - sa)).astype(BF16)
        dgab_ref[:, D_MODEL:2 * D_MODEL] = (dm * yb * sb * (1.0 - sb)).astype(BF16)
        ds5_ref[...] = _dot_nt(dya, ua_ref[...])
        dgm_ref[...] = _dot_nt(dyb, ub_ref[...])

    return _call(
        body, "mixout_bwd", (t // tm,),
        [_rows(tm, D_MODEL), _rows(tm, D_MODEL), _rows(tm, 1), _rows(tm, D_SSM), _rows(tm, D_GMLP),
         _rows(tm, 2 * D_MODEL), _resident((D_SSM, D_MODEL)), _resident((D_GMLP, D_MODEL)),
         _resident((D_MODEL, D_MODEL)), _fixed((1, D_MODEL))],
        (_rows(tm, D_MODEL), _rows(tm, D_MODEL), _cols(D_MODEL, tm), _rows(tm, D_MODEL),
         _rows(tm, D_MODEL), _rows(tm, D_SSM), _rows(tm, D_GMLP), _rows(tm, 2 * D_MODEL),
         _fixed((1, D_MODEL)), _fixed((1, D_MODEL))),
        (SDS((t, D_MODEL), F32), SDS((t, D_MODEL), BF16), SDS((D_MODEL, t), BF16),
         SDS((t, D_MODEL), BF16), SDS((t, D_MODEL), BF16), SDS((t, D_SSM), F32),
         SDS((t, D_GMLP), F32), SDS((t, 2 * D_MODEL), BF16),
         SDS((1, D_MODEL), F32), SDS((1, D_MODEL), F32)),
        (dx2, xh, rstd, s5o, gm, gab, ua, ub, wmo, g), sem=("arbitrary",), bg=bg)


def _ple_loss(x3, p, tgt, wpg, wpp, tm, bg=None):
    t = x3.shape[0]

    def body(x_ref, p_ref, t_ref, wpg_ref, wpp_ref, dx_ref, xb_ref, pb_ref, dq_ref, de_ref, loss_ref):
        @pl.when(pl.program_id(0) == 0)
        def _():
            loss_ref[...] = jnp.zeros_like(loss_ref)

        x3v = x_ref[...]
        xb = x3v.astype(BF16)
        pb = p_ref[...].astype(BF16)
        xb_ref[...] = xb.T
        pb_ref[...] = pb.T
        s = _sigmoid(_dot(xb, wpg_ref[...]))
        e = _dot(pb, wpp_ref[...])
        diff = x3v + s * e - t_ref[...]
        loss_ref[...] += jnp.sum(diff * diff, axis=0, keepdims=True)
        dout = diff * (1.0 / D_MODEL)
        de_ref[...] = (dout * s).astype(BF16)
        dq = (dout * e * s * (1.0 - s)).astype(BF16)
        dq_ref[...] = dq
        dx_ref[...] = dout + _dot_nt(dq, wpg_ref[...])

    return _call(
        body, "ple_loss", (t // tm,),
        [_rows(tm, D_MODEL), _rows(tm, PLE_DIM), _rows(tm, D_MODEL),
         _resident((D_MODEL, D_MODEL)), _resident((PLE_DIM, D_MODEL))],
        (_rows(tm, D_MODEL), _cols(D_MODEL, tm), _cols(PLE_DIM, tm), _rows(tm, D_MODEL),
         _rows(tm, D_MODEL), _fixed((1, D_MODEL))),
        (SDS((t, D_MODEL), F32), SDS((D_MODEL, t), BF16), SDS((PLE_DIM, t), BF16),
         SDS((t, D_MODEL), BF16), SDS((t, D_MODEL), BF16), SDS((1, D_MODEL), F32)),
        (x3, p, tgt, wpg, wpp), sem=("arbitrary",), bg=bg)


def _s5_discretise(lre, lim, log_dt, bre, bim):
    dt = jnp.exp(log_dt)[:, None]
    mag = jnp.exp(lre * dt)
    abr = mag * jnp.cos(lim * dt)
    abi = mag * jnp.sin(lim * dt)
    nr = abr - 1.0
    ni = abi
    den = lre * lre + lim * lim
    cr = ((nr * lre + ni * lim) / den)[..., None]
    ci = ((ni * lre - nr * lim) / den)[..., None]
    return abr, abi, cr * bre - ci * bim, cr * bim + ci * bre


def _block_diag_in(bb):
    v = bb.reshape(S5_BLOCKS, 8, SSM_STATE, SSM_GROUP_CH).transpose(0, 1, 3, 2)
    return jnp.einsum("bgip,gh->bgihp", v, jnp.eye(8, dtype=bb.dtype)).reshape(
        S5_BLOCKS, S5_BLOCK_IN, S5_BLOCK_ST)


def _block_diag_in_t(dm):
    v = dm.reshape(S5_BLOCKS, 8, SSM_GROUP_CH, 8, SSM_STATE)
    d = jnp.einsum("bgihp,gh->bgip", v, jnp.eye(8, dtype=dm.dtype))
    return d.transpose(0, 1, 3, 2).reshape(SSM_GROUPS, SSM_STATE, SSM_GROUP_CH)


def _block_diag_out(cc):
    v = cc.reshape(S5_BLOCKS, 8, SSM_GROUP_CH, SSM_STATE)
    return jnp.einsum("bgip,gh->bgphi", v, jnp.eye(8, dtype=cc.dtype)).reshape(
        S5_BLOCKS, S5_BLOCK_ST, S5_BLOCK_IN)


def _block_diag_out_t(dn):
    v = dn.reshape(S5_BLOCKS, 8, SSM_STATE, 8, SSM_GROUP_CH)
    d = jnp.einsum("bgphi,gh->bgip", v, jnp.eye(8, dtype=dn.dtype))
    return d.reshape(SSM_GROUPS, SSM_GROUP_CH, SSM_STATE)


def _s5_setup(lre, lim, log_dt, bre, bim, cre, cim, d_skip, glu_w, glu_b, tb):
    seg = tb // 8
    abr, abi, bbr, bbi = _s5_discretise(lre, lim, log_dt, bre, bim)
    pr, pi = abr, abi
    for _ in range(int(math.log2(seg))):
        pr, pi = pr * pr - pi * pi, 2.0 * pr * pi
    rows = jnp.arange(tb)
    src = (rows % 8) * seg + rows // 8
    perm = (src[:, None] == jnp.arange(tb)[None, :]).astype(BF16)
    mre = _block_diag_in(bbr)
    mim = _block_diag_in(bbi)
    nre = _block_diag_out(cre)
    nim = _block_diag_out(cim)
    return {
        "perm": perm, "permt": perm.T,
        "mre": mre.astype(BF16), "mim": mim.astype(BF16),
        "mtre": mre.transpose(0, 2, 1).astype(BF16), "mtim": mim.transpose(0, 2, 1).astype(BF16),
        "nre": nre.astype(BF16), "nim": nim.astype(BF16),
        "ntre": nre.transpose(0, 2, 1).astype(BF16), "ntim": nim.transpose(0, 2, 1).astype(BF16),
        "a": jnp.stack([abr.reshape(-1), abi.reshape(-1)]),
        "ap": jnp.stack([pr.reshape(-1), pi.reshape(-1)]),
        "dskip": d_skip.reshape(1, D_SSM), "glu_w": glu_w, "glu_wt": glu_w.T,
        "glu_b": glu_b.reshape(1, D_SSM),
    }


BIG = ("ffn1_w_in", "ffn1_w_out", "mix_w_in", "ssm_glu_w", "up_a", "up_b", "mix_w_out",
       "ffn2_w_in", "ffn2_w_out", "ple_w_proj", "ple_w_gate")
BIG_AXIS = {"ffn1_w_in": 1, "ffn1_w_out": 0, "mix_w_in": 1, "ssm_glu_w": 0, "up_a": 1, "up_b": 1,
            "mix_w_out": 0, "ffn2_w_in": 1, "ffn2_w_out": 0, "ple_w_proj": 1, "ple_w_gate": 0}
SHARD_MAJOR = 2
GATHER_AXIS = dict(BIG_AXIS, ffn1_w_in=SHARD_MAJOR, ffn2_w_in=SHARD_MAJOR)
GATHER_ORDER = (("ffn1_w_in",), ("ffn1_w_out",), ("mix_w_in",), ("ssm_glu_w", "up_a", "up_b", "mix_w_out"),
                ("ffn2_w_in",), ("ffn2_w_out", "ple_w_gate", "ple_w_proj"))
GATHER_FIRST_ID = 1
REDUCE_FIRST_ID = 7
SMALL = ("ln1_g", "ln1_b", "ssm_lambda_re", "ssm_lambda_im", "ssm_log_dt", "ssm_b_re", "ssm_b_im",
         "ssm_c_re", "ssm_c_im", "ssm_d", "ssm_glu_b", "gmlp_ln_g", "gmlp_ln_b", "gmlp_w_s",
         "gmlp_b_s", "ln2_g", "ln2_b", "ln3_g", "ln3_b")
SMALL_VIEW = {"ssm_b_re": (SSM_GROUPS, SSM_STATE * SSM_GROUP_CH), "ssm_b_im": (SSM_GROUPS, SSM_STATE * SSM_GROUP_CH)}


def _small_view(k, a):
    return a.reshape(SMALL_VIEW[k]) if k in SMALL_VIEW else a


def _place():
    return lax.axis_index("x"), lax.axis_index("y"), lax.axis_index("c")


def _other_chips(x, y):
    return [(1 - x, y), (x, 1 - y), (1 - x, 1 - y)]


def _window(ref, shard_shape, axis, chip, half):
    r, c = shard_shape
    hr = r // 2
    if axis == SHARD_MAJOR:
        return ref.at[chip] if half is None else ref.at[chip, pl.ds(half * hr, hr), :]
    if axis == 0:
        if half is None:
            return ref.at[pl.ds(chip * r, r), :]
        return ref.at[pl.ds(chip * r + half * hr, hr), :]
    if half is None:
        return ref.at[:, pl.ds(chip * c, c)]
    return ref.at[pl.ds(half * hr, hr), pl.ds(chip * c, c)]


def _gather_weights(shards, axes):
    n = len(shards)
    shapes = [s.shape for s in shards]
    full = [{0: (4 * r, c), 1: (r, 4 * c), SHARD_MAJOR: (4, r, c)}[ax] for (r, c), ax in zip(shapes, axes)]

    def remote(sems, i, k, src, dst, to):
        return pltpu.make_async_remote_copy(src_ref=src, dst_ref=dst, send_sem=sems[0].at[6 * i + k],
                                            recv_sem=sems[1].at[6 * i + k], device_id=to, device_id_type=MESH)

    def own_copies(ins, outs, sems):
        x, y, c = _place()
        me = 2 * x + y
        cps = []
        for i in range(n):
            hr = shapes[i][0] // 2
            mine = ins[i].at[pl.ds(c * hr, hr), :]
            for j, (cx, cy) in enumerate(_other_chips(x, y)):
                cps.append(remote(sems, i, j, mine, _window(outs[i], shapes[i], axes[i], me, c), (cx, cy, c)))
        local = [pltpu.make_async_copy(ins[i], _window(outs[i], shapes[i], axes[i], me, None), sems[2].at[i])
                 for i in range(n)]
        return cps, local

    def start(ins, outs, sems):
        cps, local = own_copies(ins, outs, sems)
        for cp in local + cps:
            cp.start()

    def finish(ins, outs, sems):
        x, y, c = _place()
        sibling = (x, y, 1 - c)
        passed = []
        for j, (cx, cy) in enumerate(_other_chips(x, y)):
            for i in range(n):
                w = _window(outs[i], shapes[i], axes[i], 2 * cx + cy, c)
                remote(sems, i, j, w, w, (cx, cy, c)).wait_recv()
                cp = remote(sems, i, 3 + j, w, w, sibling)
                cp.start()
                passed.append(cp)
        for j, (cx, cy) in enumerate(_other_chips(x, y)):
            for i in range(n):
                w = _window(outs[i], shapes[i], axes[i], 2 * cx + cy, 1 - c)
                remote(sems, i, 3 + j, w, w, sibling).wait_recv()
        cps, local = own_copies(ins, outs, sems)
        for cp in cps + passed:
            cp.wait_send()
        for cp in local:
            cp.wait()

    return _Exchange(shards, [SDS(f, BF16) for f in full],
                     [pltpu.SemaphoreType.DMA((6 * n,)), pltpu.SemaphoreType.DMA((6 * n,)),
                      pltpu.SemaphoreType.DMA((n,))], start, finish)


def _scatter_grads(parts, shapes, axes):
    n = len(parts)

    def copies(ins, outs, sems):
        x, y, c = _place()
        return [pltpu.make_async_remote_copy(
            src_ref=_window(ins[i], shapes[i], axes[i], 2 * cx + cy, None), dst_ref=outs[i].at[j],
            send_sem=sems[0].at[3 * i + j], recv_sem=sems[1].at[3 * i + j],
            device_id=(cx, cy, c), device_id_type=MESH)
            for i in range(n) for j, (cx, cy) in enumerate(_other_chips(x, y))]

    def start(ins, outs, sems):
        for cp in copies(ins, outs, sems):
            cp.start()

    def finish(ins, outs, sems):
        for cp in copies(ins, outs, sems):
            cp.wait()

    return _Exchange(parts, [SDS((3,) + tuple(s), BF16) for s in shapes],
                     [pltpu.SemaphoreType.DMA((3 * n,)), pltpu.SemaphoreType.DMA((3 * n,))], start, finish)


def _swap_halves(parts, shapes, axes):
    n = len(parts)

    def copies(ins, outs, sems):
        x, y, c = _place()
        cps = []
        for i in range(n):
            r, _ = shapes[i]
            hr = r // 2
            if axes[i] == 0:
                cps += [pltpu.make_async_remote_copy(
                    src_ref=ins[i].at[pl.ds(k * r + (1 - c) * hr, hr), :], dst_ref=outs[i].at[k],
                    send_sem=sems[0].at[i], recv_sem=sems[1].at[i], device_id=(x, y, 1 - c),
                    device_id_type=MESH) for k in range(4)]
            else:
                cps.append(pltpu.make_async_remote_copy(
                    src_ref=ins[i].at[pl.ds((1 - c) * hr, hr), :], dst_ref=outs[i],
                    send_sem=sems[0].at[i], recv_sem=sems[1].at[i], device_id=(x, y, 1 - c),
                    device_id_type=MESH))
        return cps

    def start(ins, outs, sems):
        for cp in copies(ins, outs, sems):
            cp.start()

    def finish(ins, outs, sems):
        x, y, c = _place()
        for i in range(n):
            pltpu.make_async_remote_copy(src_ref=outs[i], dst_ref=outs[i], send_sem=sems[0].at[i],
                                         recv_sem=sems[1].at[i], device_id=(x, y, 1 - c),
                                         device_id_type=MESH).wait()

    out = [SDS((4, r // 2, c), BF16) if ax == 0 else SDS((r // 2, 4 * c), BF16)
           for (r, c), ax in zip(shapes, axes)]
    return _Exchange(parts, out, [pltpu.SemaphoreType.DMA((n,)), pltpu.SemaphoreType.DMA((n,))], start, finish)


def _scatter_halves(pres, shapes):
    n = len(pres)

    def copies(ins, outs, sems):
        x, y, c = _place()
        return [pltpu.make_async_remote_copy(
            src_ref=ins[i].at[1 + j], dst_ref=outs[i].at[j], send_sem=sems[0].at[3 * i + j],
            recv_sem=sems[1].at[3 * i + j], device_id=(cx, cy, c), device_id_type=MESH)
            for i in range(n) for j, (cx, cy) in enumerate(_other_chips(x, y))]

    def start(ins, outs, sems):
        for cp in copies(ins, outs, sems):
            cp.start()

    def finish(ins, outs, sems):
        for cp in copies(ins, outs, sems):
            cp.wait()

    return _Exchange(pres, [SDS((3, r // 2, c), BF16) for r, c in shapes],
                     [pltpu.SemaphoreType.DMA((3 * n,)), pltpu.SemaphoreType.DMA((3 * n,))], start, finish)


def _swap_with_sibling(arrs):
    n = len(arrs)

    def copies(ins, outs, sems):
        x, y, c = _place()
        return [pltpu.make_async_remote_copy(src_ref=ins[i], dst_ref=outs[i], send_sem=sems[0].at[i],
                                             recv_sem=sems[1].at[i], device_id=(x, y, 1 - c),
                                             device_id_type=MESH) for i in range(n)]

    def start(ins, outs, sems):
        for cp in copies(ins, outs, sems):
            cp.start()

    def finish(ins, outs, sems):
        for cp in copies(ins, outs, sems):
            cp.wait()

    return _Exchange(arrs, [SDS(a.shape, a.dtype) for a in arrs],
                     [pltpu.SemaphoreType.DMA((n,)), pltpu.SemaphoreType.DMA((n,))], start, finish)


def _gather_small(arrs):
    n = len(arrs)

    def copy(sems, outs, i, k, block, to, src=None):
        px, py, pc = block
        dst = outs[i].at[4 * px + 2 * py + pc]
        return pltpu.make_async_remote_copy(
            src_ref=dst if src is None else src, dst_ref=dst, send_sem=sems[0].at[7 * i + k],
            recv_sem=sems[1].at[7 * i + k], device_id=to, device_id_type=MESH)

    direct = [math.prod(a.shape) * 4 <= DIRECT_GATHER_BYTES for a in arrs]

    def own_copies(ins, outs, sems):
        x, y, c = _place()
        cps = []
        for i in range(n):
            cps.append(copy(sems, outs, i, 0, (x, y, c), (x, y, 1 - c), src=ins[i]))
            for j, (cx, cy) in enumerate(_other_chips(x, y)):
                cps.append(copy(sems, outs, i, 1 + j, (x, y, c), (cx, cy, c), src=ins[i]))
                if direct[i]:
                    cps.append(copy(sems, outs, i, 4 + j, (x, y, c), (cx, cy, 1 - c), src=ins[i]))
        local = [pltpu.make_async_copy(ins[i], outs[i].at[4 * x + 2 * y + c], sems[2].at[i]) for i in range(n)]
        return cps, local

    def start(ins, outs, sems):
        cps, local = own_copies(ins, outs, sems)
        for cp in local + cps:
            cp.start()

    def finish(ins, outs, sems):
        x, y, c = _place()
        passed = []
        for j, (cx, cy) in enumerate(_other_chips(x, y)):
            for i in range(n):
                copy(sems, outs, i, 1 + j, (cx, cy, c), (x, y, c)).wait_recv()
                if not direct[i]:
                    cp = copy(sems, outs, i, 4 + j, (cx, cy, c), (x, y, 1 - c))
                    cp.start()
                    passed.append(cp)
        for i in range(n):
            copy(sems, outs, i, 0, (x, y, 1 - c), (x, y, c)).wait_recv()
            for j, (cx, cy) in enumerate(_other_chips(x, y)):
                copy(sems, outs, i, 4 + j, (cx, cy, 1 - c), (x, y, c)).wait_recv()
        cps, local = own_copies(ins, outs, sems)
        for cp in cps + passed:
            cp.wait_send()
        for cp in local:
            cp.wait()

    return _Exchange(arrs, [SDS((N_DEV,) + a.shape, F32) for a in arrs],
                     [pltpu.SemaphoreType.DMA((7 * n,)), pltpu.SemaphoreType.DMA((7 * n,)),
                      pltpu.SemaphoreType.DMA((n,))], start, finish)


def _local_step(x, p, tgt, wb, ws, shards=None, opt=None):
    bsz, seq, _ = x.shape
    t = bsz * seq
    tm = min(256, t)
    tb = min(256, seq)
    x0 = x.reshape(t, D_MODEL)
    p0 = p.reshape(t, PLE_DIM)
    tg = tgt.reshape(t, D_MODEL)
    row = lambda v: v.reshape(1, -1)
    dist = shards is not None
    wb = dict(wb)
    recv, sums, other, gathered = {}, {}, {}, {}
    gb = {}
    gs = {}
    shape_of, axis_of = {}, {}
    chip = None
    if dist:
        shape_of = {k: tuple(shards[k].shape) for k in BIG}
        axis_of = dict(BIG_AXIS)
        for q in range(LAST_PIECES):
            shape_of[LAST_PIECE % q] = (D_MODEL // LAST_PIECES, shape_of["ffn1_w_in"][1])
            axis_of[LAST_PIECE % q] = 1
        xi, yi, ci = _place()
        chip = (2 * xi + yi).astype(jnp.int32).reshape(1)
        ids = jnp.stack([2 * xi + yi] + [2 * cx + cy for cx, cy in _other_chips(xi, yi)] + [ci]).astype(jnp.int32)
    halfbuf, pre = {}, {}

    def gather(names):
        return _gather_weights([shards[k] for k in names], [GATHER_AXIS[k] for k in names]) if dist else None

    def exchange(scat=(), swap=(), halves=(), scat2=(), swap2=(), extra=None, after=None):
        if not dist:
            return None, []
        after = order[0] if after is None else after
        parts, tags = [], []
        if scat:
            parts.append(_scatter_grads([gb[k][1] for k in scat], [shape_of[k] for k in scat],
                                        [axis_of[k] for k in scat]))
            tags.append((recv, scat))
        if swap:
            for k in swap:
                sums[k] = order[0] = _sum_blocks(gb[k][0], recv[k], shape_of[k], axis_of[k], chip, "sum_" + k,
                                                 order[0])
            parts.append(_swap_with_sibling([sums[k] for k in swap]))
            tags.append((other, swap))
        if halves:
            parts.append(_swap_halves([gb[k][1] for k in halves], [shape_of[k] for k in halves],
                                      [axis_of[k] for k in halves]))
            tags.append((halfbuf, halves))
        if scat2:
            for k in scat2:
                pre[k] = _presum(gb[k][0], halfbuf[k], shape_of[k], axis_of[k], ids, "presum_" + k, order[0])
                order[0] = pre[k][0]
            parts.append(_scatter_halves([pre[k][1] for k in scat2], [shape_of[k] for k in scat2]))
            tags.append((recv, scat2))
        if swap2:
            for k in swap2:
                sums[k] = order[0] = _sum_half(pre[k][0], recv[k], "sum_" + k, order[0])
            parts.append(_swap_with_sibling([sums[k] for k in swap2]))
            tags.append((other, swap2))
        if extra is not None:
            parts.append(extra[0])
            tags.append((extra[1], extra[2]))
        return (_join(parts), tags) if parts else (None, [])

    def take(ex_tags, got):
        ex, tags = ex_tags
        if ex is not None:
            for (dst, names), (o0, o1) in zip(tags, ex.cuts):
                dst.update(zip(names, got[o0:o1]))

    order = [None]

    def ordered(builder, *args, **kw):
        res = builder(*args, bg=order[0] if dist else None, **kw)
        order[0] = res[0][0]
        return res

    launched = []

    def launch(ex_tags):
        if ex_tags[0] is not None:
            n = len(launched)
            launched.append(n)
            take(ex_tags, _run_exchange_on_sequencer(ex_tags[0], "reduce_%d" % n, REDUCE_FIRST_ID + n))

    small_shape = {k: _small_view(k, v).shape for k, v in ws.items()}
    small_shape["loss_rows"] = (1, D_MODEL)
    ws = {k: v if (v.ndim == 2 and k != "ssm_log_dt") else v[0] for k, v in ws.items()}
    tril = jnp.tril(jnp.ones((CHUNK, CHUNK), dtype=bool))
    wsm = jnp.where(tril[None], ws["gmlp_w_s"], 0.0)
    wsm_b = wsm.astype(BF16)
    wsmt_b = wsm.transpose(0, 2, 1).astype(BF16)
    bias = jnp.repeat(ws["gmlp_b_s"].T, GMLP_HEAD_DIM, axis=1)

    tf = min(512, t)
    if dist:
        for gi, names in enumerate(GATHER_ORDER):
            wb.update(zip(names, _run_exchange_on_sequencer(gather(names), "gather_%d" % gi, GATHER_FIRST_ID + gi)))
    (x0b, h1, a1), _ = _ffn_proj(x0, wb["ffn1_w_in"], tf, "ffn1_proj")
    (x1, xh1, rstd1), _ = _ffn_out(x0, a1, wb["ffn1_w_out"], row(ws["ln1_g"]), row(ws["ln1_b"]), tf, "ffn1_out")
    sp = _s5_setup(ws["ssm_lambda_re"], ws["ssm_lambda_im"], ws["ssm_log_dt"], ws["ssm_b_re"],
                   ws["ssm_b_im"], ws["ssm_c_re"], ws["ssm_c_im"], ws["ssm_d"], wb["ssm_glu_w"],
                   ws["ssm_glu_b"], tb)
    (x1b, za, zuv, gab), _ = _mixin_fwd(x1, wb["mix_w_in"], tf)
    (s5o, s5ot, y2p, carries), _ = _s5_fwd(za, sp, bsz, seq, tb)
    (gm, gmt), _ = _gmlp_fwd(zuv, row(ws["gmlp_ln_g"]), row(ws["gmlp_ln_b"]), wsm_b, bias)
    (x2, xh2, rstd2), _ = _mixout_fwd(x1, s5o, gm, gab, wb["up_a"], wb["up_b"], wb["mix_w_out"],
                                           row(ws["ln2_g"]), row(ws["ln2_b"]), tf)
    (x2b, h2, a2), _ = _ffn_proj(x2, wb["ffn2_w_in"], tf, "ffn2_proj")
    (x3, xh3, rstd3), _ = _ffn_out(x2, a2, wb["ffn2_w_out"], row(ws["ln3_g"]), row(ws["ln3_b"]), tf, "ffn2_out")
    (dx3, x3b, pb, dq, de, loss_rows), _ = _ple_loss(x3, p0, tg, wb["ple_w_gate"], wb["ple_w_proj"], tf)
    order[0] = dx3
    gb["ple_w_gate"], _ = ordered(_tn_matmul, x3b, dq, "dw_ple_gate", 1024, 1024, a_t=True)
    gb["ple_w_proj"], _ = ordered(_tn_matmul, pb, de, "dw_ple_proj", 256, 1024, a_t=True)
    launch(exchange(scat=("ple_w_gate", "ple_w_proj")))
    (dx2, dh2, df2, gs["ln3_g"], gs["ln3_b"]), _ = ordered(
        _ffn_bwd, dx3, xh3, rstd3, h2, wb["ffn2_w_in"], wb["ffn2_w_out"], row(ws["ln3_g"]), tm, "ffn2_bwd")
    gb["ffn2_w_out"], _ = ordered(_tn_matmul, a2, df2, "dw_ffn2_out", 1408, 1024)
    launch(exchange(scat=("ffn2_w_out",)))
    gb["ffn2_w_in"], _ = ordered(_tn_matmul, x2b, dh2, "dw_ffn2_in", 1024, 1408, a_t=True)
    launch(exchange(scat=("ffn2_w_in",), swap=("ple_w_gate", "ple_w_proj")))
    (dx1a, dmx, mb, dya, dyb, ds5, dgm, dgab, gs["ln2_g"], gs["ln2_b"]), _ = ordered(
        _mixout_bwd, dx2, xh2, rstd2, s5o, gm, gab, wb["up_a"], wb["up_b"], wb["mix_w_out"], row(ws["ln2_g"]), tm)
    gb["mix_w_out"], _ = ordered(_tn_matmul, mb, dmx, "dw_mix_out", 1024, 1024, a_t=True)
    gb["up_a"], _ = ordered(_tn_matmul, s5ot, dya, "dw_up_a", 512, 1024, a_t=True)
    gb["up_b"], _ = ordered(_tn_matmul, gmt, dyb, "dw_up_b", 512, 1024, a_t=True)
    launch(exchange(scat=("mix_w_out", "up_a", "up_b"), swap=("ffn2_w_out",)))
    (dza, dmr, dmi, dnr, dni, da, ddsk, dgw, dgb), _ = ordered(_s5_bwd, za, y2p, ds5, carries, sp, bsz, seq, tb)
    gb["ssm_glu_w"] = (dgw, dgw.astype(BF16))
    launch(exchange(scat=("ssm_glu_w",), swap=("ffn2_w_in",)))
    (dzuv, dws, dbias, gs["gmlp_ln_g"], gs["gmlp_ln_b"]), _ = ordered(
        _gmlp_bwd, zuv, dgm, row(ws["gmlp_ln_g"]), row(ws["gmlp_ln_b"]), wsm_b, wsmt_b, bias)
    (dx1,), _ = ordered(_mixin_bwd, dx1a, dza, dzuv, dgab, wb["mix_w_in"], tf)
    g_mi, _ = ordered(_tn_matmul, x1b, dza, "dw_mix_in_a", 1024, 512, 0, 3584, a_t=True)
    g_mi, _ = ordered(_tn_matmul, x1b, dzuv, "dw_mix_in_uv", 1024, 512, 1, 3584, g_mi, a_t=True)
    gb["mix_w_in"], _ = ordered(_tn_matmul, x1b, dgab, "dw_mix_in_g", 1024, 512, 3, 3584, g_mi, a_t=True)
    launch(exchange(swap=("mix_w_out", "up_a", "up_b", "ssm_glu_w")))

    d_abr = da[0].sum(axis=0).reshape(SSM_GROUPS, SSM_STATE)
    d_abi = da[1].sum(axis=0).reshape(SSM_GROUPS, SSM_STATE)
    _, vjp = jax.vjp(_s5_discretise, ws["ssm_lambda_re"], ws["ssm_lambda_im"], ws["ssm_log_dt"],
                     ws["ssm_b_re"], ws["ssm_b_im"])
    (gs["ssm_lambda_re"], gs["ssm_lambda_im"], gs["ssm_log_dt"], gs["ssm_b_re"], gs["ssm_b_im"]) = vjp(
        (d_abr, d_abi, _block_diag_in_t(dmr), _block_diag_in_t(dmi)))
    gs["ssm_c_re"] = _block_diag_out_t(dnr)
    gs["ssm_c_im"] = _block_diag_out_t(dni)
    gs["ssm_d"] = ddsk
    gs["ssm_glu_b"] = dgb
    gs["gmlp_w_s"] = dws
    gs["gmlp_b_s"] = dbias.reshape(CHUNK, GMLP_HEADS, GMLP_HEAD_DIM).sum(axis=-1).T
    gs["loss_rows"] = loss_rows

    def small_gather(names):
        return (_gather_small([gs[k].reshape(small_shape[k]) for k in names]), gathered, names) if dist else None

    late = ("ln1_g", "ln1_b")
    launch(exchange(scat=("mix_w_in",), extra=small_gather(tuple(k for k in SMALL + ("loss_rows",) if k not in late))))
    (dx0, dh1, df1, gs["ln1_g"], gs["ln1_b"]), _ = ordered(
        _ffn_bwd, dx1, xh1, rstd1, h1, wb["ffn1_w_in"], wb["ffn1_w_out"], row(ws["ln1_g"]), tm, "ffn1_bwd")
    grad_x = dx0.reshape(bsz, seq, D_MODEL)
    if not dist:
        gb["ffn1_w_out"], _ = _tn_matmul(a1, df1, "dw_ffn1_out", 1408, 1024)
        gb["ffn1_w_in"], _ = _tn_matmul(x0b, dh1, "dw_ffn1_in", 1024, 1408, a_t=True)
        return (loss_rows, grad_x, gb, {k: gs[k].reshape(small_shape[k]) for k in SMALL}, sums, other, gathered,
                None, {})
    launch(exchange(extra=small_gather(late)))
    gb["ffn1_w_out"], _ = ordered(_tn_matmul, a1, df1, "dw_ffn1_out", 1408, 1024)
    last = ["ffn1_w_out"] + [LAST_PIECE % q for q in range(LAST_PIECES)]
    fillers = (("ffn2_w_in", "mix_w_in", "ple_w_gate"),
               ("ffn2_w_out", "mix_w_out", "up_a", "up_b", "ssm_glu_w", "ple_w_proj"))
    out = {}
    for i in range(1, len(last) + 3):
        stage = lambda d: tuple(last[i - d:i - d + 1]) if 0 <= i - d < len(last) else ()
        launch(exchange(halves=stage(1), scat2=stage(2), swap2=stage(3), swap=("mix_w_in",) if i == 2 else ()))
        if i < len(last):
            gb[last[i]], _ = ordered(_tn_matmul, x0b, dh1, "dw_" + last[i], D_MODEL // LAST_PIECES, 1408,
                                     a_cols=(i - 1, 1), a_t=True)
        elif i - len(last) < len(fillers):
            for k in fillers[i - len(last)]:
                w, m, v = opt[k]
                out[k] = _adam_big(w, sums[k], other[k], m, v, "adam_" + k, after=order[0])
                order[0] = out[k][1]
    return loss_rows, grad_x, gb, gs, sums, other, gathered, ids, out


def _adamw(w, g, m, v):
    m = ADAM_B1 * m + (1.0 - ADAM_B1) * g
    v = ADAM_B2 * v + (1.0 - ADAM_B2) * (g * g)
    m_hat = m / ADAM_C1
    v_hat = v / ADAM_C2
    delta = -ADAM_LR * (m_hat / (jnp.sqrt(v_hat) + ADAM_EPS) + ADAM_WD * w)
    return delta, m, v


def _pinned(after):
    return ([pl.BlockSpec(memory_space=pl.ANY)], [after]) if after is not None else ([], [])


def _sum_blocks(part, recv, shape, axis, chip, name, after=None):
    r, c = shape
    rb = r // 8

    def body(chip_ref, p_ref, r_ref, *rest):
        rest[-1][...] = (p_ref[...] + r_ref[0].astype(F32) + r_ref[1].astype(F32) + r_ref[2].astype(F32))

    if axis == 0:
        own = pl.BlockSpec((rb, c), lambda i, k: (k[0] * 8 + i, 0))
    else:
        own = pl.BlockSpec((rb, c), lambda i, k: (i, k[0]))
    pin_specs, pin_args = _pinned(after)
    grid_spec = pltpu.PrefetchScalarGridSpec(
        num_scalar_prefetch=1, grid=(8,),
        in_specs=[own, pl.BlockSpec((3, rb, c), lambda i, k: (0, i, 0))] + pin_specs,
        out_specs=pl.BlockSpec((rb, c), lambda i, k: (i, 0)))
    return pl.pallas_call(body, name=name, out_shape=SDS((r, c), F32), grid_spec=grid_spec,
                          compiler_params=_params(("parallel",)))(chip, part, recv, *pin_args)


def _presum(part, half, shape, axis, ids, name, after=None):
    r, c = shape
    rb = r // 4

    def body(ids_ref, p_ref, h_ref, *rest):
        of_ref, ob_ref = rest[-2:]
        s = p_ref[...] + h_ref[...].astype(F32)
        ob_ref[...] = s.astype(BF16)

        @pl.when(pl.program_id(1) == 0)
        def _():
            of_ref[...] = s

    if axis == 0:
        p_spec = pl.BlockSpec((rb, c), lambda i, t, ids: (ids[t] * 4 + ids[4] * 2 + i, 0))
        h_spec = pl.BlockSpec((None, rb, c), lambda i, t, ids: (ids[t], i, 0))
    else:
        p_spec = pl.BlockSpec((rb, c), lambda i, t, ids: (ids[4] * 2 + i, ids[t]))
        h_spec = pl.BlockSpec((rb, c), lambda i, t, ids: (i, ids[t]))
    pin_specs, pin_args = _pinned(after)
    grid_spec = pltpu.PrefetchScalarGridSpec(
        num_scalar_prefetch=1, grid=(2, 4), in_specs=[p_spec, h_spec] + pin_specs,
        out_specs=(pl.BlockSpec((rb, c), lambda i, t, ids: (i, 0)),
                   pl.BlockSpec((None, rb, c), lambda i, t, ids: (t, i, 0))))
    return pl.pallas_call(body, name=name, out_shape=(SDS((r // 2, c), F32), SDS((4, r // 2, c), BF16)),
                          grid_spec=grid_spec,
                          compiler_params=_params(("parallel", "arbitrary")))(ids, part, half, *pin_args)


def _sum_half(pre, recv, name, after=None):
    hr, c = pre.shape
    rb = hr // 2

    def body(p_ref, r_ref, *rest):
        rest[-1][...] = (p_ref[...] + r_ref[0].astype(F32) + r_ref[1].astype(F32) + r_ref[2].astype(F32))

    spec = pl.BlockSpec((rb, c), lambda i: (i, 0))
    pin_specs, pin_args = _pinned(after)
    return pl.pallas_call(body, name=name, grid=(2,), out_shape=SDS((hr, c), F32),
                          in_specs=[spec, pl.BlockSpec((3, rb, c), lambda i: (0, i, 0))] + pin_specs,
                          out_specs=spec, compiler_params=_params(("parallel",)))(pre, recv, *pin_args)


def _adam_halves(w, mine, oth, m, v, ids, name, piece=0, prev=None):
    r, c = w.shape
    rb = mine.shape[0] // 2

    def body(ids_ref, w_ref, a_ref, b_ref, m_ref, v_ref, *rest):
        g_ref, d_ref, nm_ref, nv_ref = rest[-4:]
        g = jnp.where(pl.program_id(0) // 2 == ids_ref[4], a_ref[...], b_ref[...])
        g_ref[...] = g
        d_ref[...], nm_ref[...], nv_ref[...] = _adamw(w_ref[...], g, m_ref[...], v_ref[...])

    whole = pl.BlockSpec((rb, c), lambda i, ids: (i + 4 * piece, 0))
    part = pl.BlockSpec((rb, c), lambda i, ids: (i % 2, 0))
    in_specs = [whole, part, part, whole, whole]
    args = [w, mine, oth, m, v]
    aliases = {}
    if prev is not None:
        in_specs += [pl.BlockSpec(memory_space=pl.ANY)] * 4
        args += list(prev)
        aliases = {6: 0, 7: 1, 8: 2, 9: 3}
    grid_spec = pltpu.PrefetchScalarGridSpec(num_scalar_prefetch=1, grid=(4,), in_specs=in_specs,
                                             out_specs=(whole,) * 4)
    return pl.pallas_call(body, name=name, out_shape=tuple(SDS((r, c), F32) for _ in range(4)),
                          grid_spec=grid_spec, input_output_aliases=aliases,
                          compiler_params=_params(("parallel",)))(ids, *args)


def _adam_big(w, ga, gb, m, v, name, piece=0, prev=None, after=None):
    r, c = w.shape
    pr = ga.shape[0]
    steps = 8 if pr == r else 2
    rb = pr // steps
    off = piece * steps

    def body(w_ref, ga_ref, gb_ref, m_ref, v_ref, *rest):
        g_ref, d_ref, nm_ref, nv_ref = rest[-4:]
        g = ga_ref[...] + gb_ref[...]
        g_ref[...] = g
        d_ref[...], nm_ref[...], nv_ref[...] = _adamw(w_ref[...], g, m_ref[...], v_ref[...])

    whole = pl.BlockSpec((rb, c), lambda i: (i + off, 0))
    part = pl.BlockSpec((rb, c), lambda i: (i, 0))
    in_specs = [whole, part, part, whole, whole]
    args = [w, ga, gb, m, v]
    aliases = {}
    if prev is not None:
        in_specs += [pl.BlockSpec(memory_space=pl.ANY)] * 4
        args += list(prev)
        aliases = {5: 0, 6: 1, 7: 2, 8: 3}
    if after is not None:
        in_specs.append(pl.BlockSpec(memory_space=pl.ANY))
        args.append(after)
    return pl.pallas_call(
        body, name=name, grid=(steps,), out_shape=tuple(SDS((r, c), F32) for _ in range(4)),
        in_specs=in_specs, out_specs=(whole,) * 4, input_output_aliases=aliases,
        compiler_params=_params(("parallel",)),
    )(*args)


def _adam_small(ws, gathered, ms, vs):
    n = len(ws)

    def body(*refs):
        w_refs, g_refs, m_refs, v_refs = refs[:n], refs[n:2 * n], refs[2 * n:3 * n], refs[3 * n:4 * n]
        outs = refs[4 * n:]
        for i in range(n):
            g = g_refs[i][0]
            for d in range(1, N_DEV):
                g = g + g_refs[i][d]
            delta, nm, nv = _adamw(w_refs[i][...], g, m_refs[i][...], v_refs[i][...])
            outs[i][...] = g
            outs[n + i][...] = delta
            outs[2 * n + i][...] = nm
            outs[3 * n + i][...] = nv

    vmem = pl.BlockSpec(memory_space=pltpu.VMEM)
    shapes = [w.shape for w in ws]
    return pl.pallas_call(
        body, name="adam_small", out_shape=tuple(SDS(s, F32) for s in shapes * 4),
        in_specs=[vmem] * (4 * n), out_specs=tuple([vmem] * (4 * n)),
        compiler_params=pltpu.CompilerParams(vmem_limit_bytes=VMEM_LIMIT_BYTES),
    )(*ws, *gathered, *ms, *vs)


def _sum_loss(gathered):
    def body(g_ref, o_ref):
        tot = g_ref[0]
        for d in range(1, N_DEV):
            tot = tot + g_ref[d]
        o_ref[...] = (0.5 / D_MODEL) * jnp.sum(tot, axis=1, keepdims=True)

    vmem = pl.BlockSpec(memory_space=pltpu.VMEM)
    return pl.pallas_call(body, name="sum_loss", out_shape=SDS((1, 1), F32), in_specs=[vmem],
                          out_specs=vmem)(gathered)


def kernel(x, p, ffn1_w_in, ffn1_w_out, ln1_g, ln1_b, mix_w_in, ssm_lambda_re, ssm_lambda_im, ssm_log_dt, ssm_b_re, ssm_b_im, ssm_c_re, ssm_c_im, ssm_d, ssm_glu_w, ssm_glu_b, gmlp_ln_g, gmlp_ln_b, gmlp_w_s, gmlp_b_s, up_a, up_b, mix_w_out, ln2_g, ln2_b, ffn2_w_in, ffn2_w_out, ln3_g, ln3_b, ple_w_proj, ple_w_gate, loss_target, m_ffn1_w_in, m_ffn1_w_out, m_ln1_g, m_ln1_b, m_mix_w_in, m_ssm_lambda_re, m_ssm_lambda_im, m_ssm_log_dt, m_ssm_b_re, m_ssm_b_im, m_ssm_c_re, m_ssm_c_im, m_ssm_d, m_ssm_glu_w, m_ssm_glu_b, m_gmlp_ln_g, m_gmlp_ln_b, m_gmlp_w_s, m_gmlp_b_s, m_up_a, m_up_b, m_mix_w_out, m_ln2_g, m_ln2_b, m_ffn2_w_in, m_ffn2_w_out, m_ln3_g, m_ln3_b, m_ple_w_proj, m_ple_w_gate, v_ffn1_w_in, v_ffn1_w_out, v_ln1_g, v_ln1_b, v_mix_w_in, v_ssm_lambda_re, v_ssm_lambda_im, v_ssm_log_dt, v_ssm_b_re, v_ssm_b_im, v_ssm_c_re, v_ssm_c_im, v_ssm_d, v_ssm_glu_w, v_ssm_glu_b, v_gmlp_ln_g, v_gmlp_ln_b, v_gmlp_w_s, v_gmlp_b_s, v_up_a, v_up_b, v_mix_w_out, v_ln2_g, v_ln2_b, v_ffn2_w_in, v_ffn2_w_out, v_ln3_g, v_ln3_b, v_ple_w_proj, v_ple_w_gate):
    given = dict(locals())
    order = ("ffn1_w_in", "ffn1_w_out", "ln1_g", "ln1_b", "mix_w_in", "ssm_lambda_re", "ssm_lambda_im",
             "ssm_log_dt", "ssm_b_re", "ssm_b_im", "ssm_c_re", "ssm_c_im", "ssm_d", "ssm_glu_w", "ssm_glu_b",
             "gmlp_ln_g", "gmlp_ln_b", "gmlp_w_s", "gmlp_b_s", "up_a", "up_b", "mix_w_out", "ln2_g", "ln2_b",
             "ffn2_w_in", "ffn2_w_out", "ln3_g", "ln3_b", "ple_w_proj", "ple_w_gate")
    assert set(order) == set(BIG + SMALL)

    shard = {k: given[k][0] for k in BIG}
    shard_b = {k: shard[k].astype(BF16) for k in BIG}
    opt = {k: (shard[k], given["m_" + k][0], given["v_" + k][0]) for k in BIG}
    loss_rows, grad_x, gb, gs, sums, other, gathered, ids, out = _local_step(
        x, given["p"][0], loss_target, {}, {k: given[k] for k in SMALL}, shard_b, opt)

    out = dict(out)
    for k in BIG:
        if k in out:
            continue
        moments = (given["m_" + k][0], given["v_" + k][0])
        if k == "ffn1_w_out":
            out[k] = _adam_halves(shard[k], sums[k], other[k], *moments, ids, "adam_" + k)
        elif k == "ffn1_w_in":
            for q in range(LAST_PIECES):
                kq = LAST_PIECE % q
                out[k] = _adam_halves(shard[k], sums[kq], other[kq], *moments, ids, "adam_" + kq, q, out.get(k))
        else:
            out[k] = _adam_big(shard[k], sums[k], other[k], *moments, "adam_" + k,
                               after=gb[LAST_PIECE % (LAST_PIECES - 1)][0])

    res = _adam_small([_small_view(k, given[k]) for k in SMALL], [gathered[k] for k in SMALL],
                      [_small_view(k, given["m_" + k]) for k in SMALL],
                      [_small_view(k, given["v_" + k]) for k in SMALL])
    ns = len(SMALL)
    for i, k in enumerate(SMALL):
        out[k] = tuple(res[j * ns + i].reshape(given[k].shape) for j in range(4))
    loss = _sum_loss(gathered["loss_rows"]).reshape(())

    lead = lambda k, j: out[k][j][None] if k in BIG else out[k][j]
    return (loss, grad_x, *[lead(k, 0) for k in order], *[lead(k, 1) for k in order],
            *[lead(k, 2) for k in order], *[lead(k, 3) for k in order])
```

```python
import math

import jax
import jax.numpy as jnp
from jax import lax
from jax.experimental import pallas as pl
from jax.experimental.pallas import tpu as pltpu
from jax.experimental.pallas import tpu_sc as plsc

F32 = jnp.float32
BF16 = jnp.bfloat16
MESH = pl.DeviceIdType.MESH
SDS = jax.ShapeDtypeStruct

D_MODEL = 1024
D_FF = 2816
D_SSM = 512
D_GMLP = 512
SSM_GROUPS = 32
SSM_GROUP_CH = 16
SSM_STATE = 64
SSM_LANES = SSM_GROUPS * SSM_STATE
GMLP_HEADS = 8
GMLP_HEAD_DIM = 64
CHUNK = 128
PLE_DIM = 256
LN_EPS = 1e-5
ALPHA = 2.0 ** 0.25

ADAM_LR = 0.001
ADAM_B1 = 0.9
ADAM_B2 = 0.999
ADAM_EPS = 1e-08
ADAM_WD = 0.01
ADAM_STEP = 10
ADAM_C1 = 1.0 - ADAM_B1 ** ADAM_STEP
ADAM_C2 = 1.0 - ADAM_B2 ** ADAM_STEP

N_DEV = 8
VMEM_LIMIT_BYTES = 56 * 1024 * 1024
FFN_COLS = 1408
S5_BLOCKS = 4
S5_BLOCK_IN = D_SSM // S5_BLOCKS
S5_BLOCK_ST = SSM_LANES // S5_BLOCKS
SCAN_LANES = 512
TN_K_BLOCK = 2048
ROW_STEPS = 4
DIRECT_GATHER_BYTES = 0
LAST_PIECES = 2
LAST_PIECE = "ffn1_w_in_q%d"
_G0 = math.sqrt(2.0 / math.pi)
_G1 = 0.044715


def _dot(a, b):
    return jnp.dot(a, b, preferred_element_type=F32)


def _dot_nt(a, b):
    return lax.dot_general(a, b, (((1,), (1,)), ((), ())), preferred_element_type=F32)


def _dot_tn(a, b):
    return lax.dot_general(a, b, (((0,), (0,)), ((), ())), preferred_element_type=F32)


def _sigmoid(x):
    return 1.0 / (1.0 + jnp.exp(-x))


def _gelu(x):
    t = jnp.tanh(_G0 * (x + _G1 * x * x * x))
    return 0.5 * x * (1.0 + t)


def _gelu_grad(x):
    t = jnp.tanh(_G0 * (x + _G1 * x * x * x))
    return 0.5 * (1.0 + t) + 0.5 * x * (1.0 - t * t) * _G0 * (1.0 + 3.0 * _G1 * x * x)


def _ln_fwd(r, g, b):
    mu = jnp.mean(r, axis=-1, keepdims=True)
    d = r - mu
    var = jnp.mean(d * d, axis=-1, keepdims=True)
    rstd = lax.rsqrt(var + LN_EPS)
    xh = d * rstd
    return xh * g + b, xh, rstd


def _ln_bwd(dy, xh, rstd, g):
    dxh = dy * g
    m1 = jnp.mean(dxh, axis=-1, keepdims=True)
    m2 = jnp.mean(dxh * xh, axis=-1, keepdims=True)
    return rstd * (dxh - m1 - xh * m2)


def _resident(shape):
    nd = len(shape)
    return pl.BlockSpec(shape, lambda *_: (0,) * nd, pipeline_mode=pl.Buffered(1))


def _fixed(shape):
    nd = len(shape)
    return pl.BlockSpec(shape, lambda *_: (0,) * nd)


def _rows(tm, cols):
    return pl.BlockSpec((tm, cols), lambda i: (i, 0))


def _cols(rows, tm):
    return pl.BlockSpec((rows, tm), lambda i: (0, i))


def _params(sem):
    return pltpu.CompilerParams(dimension_semantics=sem, vmem_limit_bytes=VMEM_LIMIT_BYTES)


class _Exchange:
    def __init__(self, args, out_shape, sems, start, finish):
        self.args, self.out_shape, self.sems = list(args), list(out_shape), list(sems)
        self.start, self.finish = start, finish
        self.cuts = [(0, len(self.out_shape))]


def _call(body, name, grid, in_specs, out_specs, out_shape, args, scratch=(), sem=None, bg=None, aliases=None):
    aliases = {} if aliases is None else aliases
    in_specs, args = list(in_specs), list(args)
    fn = body
    if bg is not None:
        n_args = len(args)

        def fn(*refs):
            body(*refs[:n_args], *refs[n_args + 1:])

        in_specs.append(pl.BlockSpec(memory_space=pl.ANY))
        args.append(bg)
    res = pl.pallas_call(fn, name=name, grid=grid, out_shape=tuple(out_shape), in_specs=in_specs,
                         out_specs=tuple(out_specs), scratch_shapes=list(scratch),
                         input_output_aliases=aliases, compiler_params=_params(sem))(*args)
    return tuple(res), ()


def _run_exchange_on_sequencer(ex, name, collective_id):
    n_i, n_o = len(ex.args), len(ex.out_shape)

    def body(*refs):
        ins, outs, sems = refs[:n_i], refs[n_i:n_i + n_o], refs[n_i + n_o:]
        x, y, c = lax.axis_index("x"), lax.axis_index("y"), lax.axis_index("c")
        barrier = pltpu.get_barrier_semaphore()
        for peer in [(x, y, 1 - c), (1 - x, y, c), (x, 1 - y, c), (1 - x, 1 - y, c)]:
            pl.semaphore_signal(barrier, inc=1, device_id=peer, device_id_type=MESH)
        pl.semaphore_wait(barrier, 4)
        ex.start(ins, outs, sems)
        ex.finish(ins, outs, sems)

    return tuple(pl.kernel(body, out_type=tuple(ex.out_shape),
                           mesh=plsc.ScalarSubcoreMesh(axis_name="sequencer", num_cores=1),
                           scratch_types=list(ex.sems), name=name,
                           compiler_params=pltpu.CompilerParams(collective_id=collective_id))(*ex.args))


def _join(exchanges):
    cuts = []
    a = o = q = 0
    for e in exchanges:
        cuts.append((a, a + len(e.args), o, o + len(e.out_shape), q, q + len(e.sems)))
        a, o, q = cuts[-1][1], cuts[-1][3], cuts[-1][5]

    def start(ins, outs, sems):
        for e, (a0, a1, o0, o1, q0, q1) in zip(exchanges, cuts):
            e.start(ins[a0:a1], outs[o0:o1], sems[q0:q1])

    def finish(ins, outs, sems):
        for e, (a0, a1, o0, o1, q0, q1) in zip(exchanges, cuts):
            e.finish(ins[a0:a1], outs[o0:o1], sems[q0:q1])

    joined = _Exchange(sum((e.args for e in exchanges), []), sum((e.out_shape for e in exchanges), []),
                       sum((e.sems for e in exchanges), []), start, finish)
    joined.cuts = [(c[2], c[3]) for c in cuts]
    return joined


def _ffn_proj(x, w_in, tm, name, bg=None):
    t = x.shape[0]
    nch = D_FF // FFN_COLS

    def body(x_ref, win_ref, xbt_ref, h_ref, a_ref):
        xb = x_ref[...].astype(BF16)
        xbt_ref[...] = xb.T
        for k in range(nch):
            cg = slice(k * FFN_COLS, (k + 1) * FFN_COLS)
            cu = slice(D_FF + k * FFN_COLS, D_FF + (k + 1) * FFN_COLS)
            hg = _dot(xb, win_ref[k])
            hu = _dot(xb, win_ref[nch + k])
            h_ref[:, cg] = hg.astype(BF16)
            h_ref[:, cu] = hu.astype(BF16)
            a_ref[:, cg] = (hg * _sigmoid(hg) * hu).astype(BF16)

    return _call(
        body, name, (t // tm,),
        [_rows(tm, D_MODEL), _resident((2 * nch, D_MODEL, FFN_COLS))],
        (_cols(D_MODEL, tm), _rows(tm, 2 * D_FF), _rows(tm, D_FF)),
        (SDS((D_MODEL, t), BF16), SDS((t, 2 * D_FF), BF16), SDS((t, D_FF), BF16)),
        (x, w_in), sem=("parallel",), bg=bg)


def _ffn_out(x, a, w_out, g, b, tm, name, bg=None):
    t = x.shape[0]

    def body(x_ref, a_ref, wout_ref, g_ref, b_ref, xn_ref, xh_ref, rstd_ref):
        f = _dot(a_ref[...], wout_ref[...])
        y, xh, rstd = _ln_fwd(ALPHA * x_ref[...] + 0.5 * f, g_ref[...], b_ref[...])
        xn_ref[...] = y
        xh_ref[...] = xh
        rstd_ref[...] = rstd

    return _call(
        body, name, (t // tm,),
        [_rows(tm, D_MODEL), _rows(tm, D_FF), _resident((D_FF, D_MODEL)), _fixed((1, D_MODEL)), _fixed((1, D_MODEL))],
        (_rows(tm, D_MODEL), _rows(tm, D_MODEL), _rows(tm, 1)),
        (SDS((t, D_MODEL), F32), SDS((t, D_MODEL), F32), SDS((t, 1), F32)),
        (x, a, w_out, g, b), sem=("parallel",), bg=bg)


def _ffn_bwd(dxn, xh, rstd, h, w_in, w_out, g, tm, name, bg=None):
    t = dxn.shape[0]
    nch = D_FF // FFN_COLS

    def body(dxn_ref, xh_ref, rstd_ref, h_ref, win_ref, wout_ref, g_ref,
             dx_ref, dh_ref, df_ref, dg_ref, db_ref):
        @pl.when(pl.program_id(0) == 0)
        def _():
            dg_ref[...] = jnp.zeros_like(dg_ref)
            db_ref[...] = jnp.zeros_like(db_ref)

        dy = dxn_ref[...]
        xhv = xh_ref[...]
        dr = _ln_bwd(dy, xhv, rstd_ref[...], g_ref[...])
        dg_ref[...] += jnp.sum(dy * xhv, axis=0, keepdims=True)
        db_ref[...] += jnp.sum(dy, axis=0, keepdims=True)
        df = (0.5 * dr).astype(BF16)
        df_ref[...] = df
        dx = ALPHA * dr
        das = [_dot_nt(df, wout_ref[k * FFN_COLS:(k + 1) * FFN_COLS, :]) for k in range(nch)]
        for k in range(nch):
            cg = slice(k * FFN_COLS, (k + 1) * FFN_COLS)
            cu = slice(D_FF + k * FFN_COLS, D_FF + (k + 1) * FFN_COLS)
            hg = h_ref[:, cg].astype(F32)
            hu = h_ref[:, cu].astype(F32)
            sg = _sigmoid(hg)
            silu = hg * sg
            da = das[k]
            dhu = (da * silu).astype(BF16)
            dhg = (da * hu * (sg * (1.0 + hg * (1.0 - sg)))).astype(BF16)
            dh_ref[:, cg] = dhg
            dh_ref[:, cu] = dhu
            dx = dx + _dot_nt(dhg, win_ref[k]) + _dot_nt(dhu, win_ref[nch + k])
        dx_ref[...] = dx

    return _call(
        body, name, (t // tm,),
        [_rows(tm, D_MODEL), _rows(tm, D_MODEL), _rows(tm, 1), _rows(tm, 2 * D_FF),
         _resident((2 * nch, D_MODEL, FFN_COLS)), _resident((D_FF, D_MODEL)), _fixed((1, D_MODEL))],
        (_rows(tm, D_MODEL), _rows(tm, 2 * D_FF), _rows(tm, D_MODEL),
         _fixed((1, D_MODEL)), _fixed((1, D_MODEL))),
        (SDS((t, D_MODEL), F32), SDS((t, 2 * D_FF), BF16), SDS((t, D_MODEL), BF16),
         SDS((1, D_MODEL), F32), SDS((1, D_MODEL), F32)),
        (dxn, xh, rstd, h, w_in, w_out, g), sem=("arbitrary",), bg=bg)


def _tn_matmul(a, b, name, bm, bn, col_block=0, total_cols=None, prev=None, bg=None, a_cols=None, a_t=False):
    t, m = a.shape[::-1] if a_t else a.shape
    a_first = 0
    if a_cols is not None:
        a_first, m = a_cols[0], a_cols[1] * bm
    n = b.shape[1]
    total_cols = n if total_cols is None else total_cols
    bk = min(TN_K_BLOCK, t)
    nk = t // bk
    n_in = 2 if prev is None else 4

    def body(*refs):
        a_ref, b_ref = refs[0], refs[1]
        o_ref, ob_ref = refs[n_in], refs[n_in + 1]
        k = pl.program_id(2)

        @pl.when(k == 0)
        def _():
            o_ref[...] = jnp.zeros_like(o_ref)

        o_ref[...] += _dot(a_ref[...], b_ref[...]) if a_t else _dot_tn(a_ref[...], b_ref[...])

        @pl.when(k == nk - 1)
        def _():
            ob_ref[...] = o_ref[...].astype(BF16)

    a_spec = (pl.BlockSpec((bm, bk), lambda i, j, k: (i + a_first, k)) if a_t
              else pl.BlockSpec((bk, bm), lambda i, j, k: (k, i + a_first)))
    in_specs = [a_spec, pl.BlockSpec((bk, bn), lambda i, j, k: (k, j))]
    args = [a, b]
    aliases = {}
    if prev is not None:
        in_specs += [pl.BlockSpec(memory_space=pl.ANY), pl.BlockSpec(memory_space=pl.ANY)]
        args += list(prev)
        aliases = {2: 0, 3: 1}
        if any(bg is p for p in prev):
            bg = None
    out_spec = pl.BlockSpec((bm, bn), lambda i, j, k: (i, j + col_block))
    return _call(body, name, (m // bm, n // bn, nk), in_specs, (out_spec, out_spec),
                 (SDS((m, total_cols), F32), SDS((m, total_cols), BF16)), args,
                 sem=("parallel", "parallel", "arbitrary"), bg=bg, aliases=aliases)


def _mixin_fwd(x1, w, tm, bg=None):
    t = x1.shape[0]

    def body(x_ref, w_ref, xbt_ref, za_ref, zuv_ref, gab_ref):
        xb = x_ref[...].astype(BF16)
        xbt_ref[...] = xb.T
        za_ref[...] = _dot(xb, w_ref[:, 0:512]).astype(BF16)
        zuv_ref[...] = _dot(xb, w_ref[:, 512:1536]).astype(BF16)
        gab_ref[...] = _dot(xb, w_ref[:, 1536:3584]).astype(BF16)

    return _call(
        body, "mixin_fwd", (t // tm,),
        [_rows(tm, D_MODEL), _resident((D_MODEL, 3584))],
        (_cols(D_MODEL, tm), _rows(tm, 512), _rows(tm, 1024), _rows(tm, 2048)),
        (SDS((D_MODEL, t), BF16), SDS((t, 512), BF16), SDS((t, 1024), BF16), SDS((t, 2048), BF16)),
        (x1, w), sem=("parallel",), bg=bg)


def _mixin_bwd(dx1a, dza, dzuv, dgab, w, tm, bg=None):
    t = dx1a.shape[0]

    def body(d_ref, dza_ref, dzuv_ref, dgab_ref, w_ref, dx_ref):
        dx_ref[...] = (d_ref[...] + _dot_nt(dza_ref[...], w_ref[:, 0:512])
                       + _dot_nt(dzuv_ref[...], w_ref[:, 512:1536])
                       + _dot_nt(dgab_ref[...], w_ref[:, 1536:3584]))

    return _call(
        body, "mixin_bwd", (t // tm,),
        [_rows(tm, D_MODEL), _rows(tm, 512), _rows(tm, 1024), _rows(tm, 2048), _resident((D_MODEL, 3584))],
        (_rows(tm, D_MODEL),), (SDS((t, D_MODEL), F32),),
        (dx1a, dza, dzuv, dgab, w), sem=("parallel",), bg=bg)


def _unrolled(lo, hi, body, carry):
    for j in range(lo, hi):
        carry = body(j, carry)
    return carry


def _scan_fwd(hr_ref, hi_ref, a_ref, ap_ref, carry_ref, seg, cin_ref):
    for lc in range(SSM_LANES // SCAN_LANES):
        ls = slice(lc * SCAN_LANES, (lc + 1) * SCAN_LANES)
        a_r = jnp.broadcast_to(a_ref[0:1, ls], (8, SCAN_LANES))
        a_i = jnp.broadcast_to(a_ref[1:2, ls], (8, SCAN_LANES))

        def step(j, hc, ls=ls, a_r=a_r, a_i=a_i):
            h_r, h_i = hc
            rows = pl.ds(j * 8, 8)
            n_r = a_r * h_r - a_i * h_i + hr_ref[rows, ls]
            n_i = a_r * h_i + a_i * h_r + hi_ref[rows, ls]
            hr_ref[rows, ls] = n_r
            hi_ref[rows, ls] = n_i
            return n_r, n_i

        zero = jnp.zeros((8, SCAN_LANES), F32)
        f_r, f_i = _unrolled(0, seg, step, (zero, zero))
        c_r = carry_ref[0:1, ls]
        c_i = carry_ref[1:2, ls]
        p_r = ap_ref[0:1, ls]
        p_i = ap_ref[1:2, ls]
        rows_r, rows_i = [], []
        for s in range(8):
            rows_r.append(c_r)
            rows_i.append(c_i)
            c_r, c_i = (f_r[s:s + 1] + p_r * c_r - p_i * c_i,
                        f_i[s:s + 1] + p_r * c_i + p_i * c_r)
        carry_ref[0:1, ls] = c_r
        carry_ref[1:2, ls] = c_i
        cin_r = jnp.concatenate(rows_r, axis=0)
        cin_i = jnp.concatenate(rows_i, axis=0)
        if cin_ref is not None:
            cin_ref[0, :, ls] = cin_r
            cin_ref[1, :, ls] = cin_i

        def fix(j, cc, ls=ls, a_r=a_r, a_i=a_i):
            c_r, c_i = cc
            c_r, c_i = a_r * c_r - a_i * c_i, a_r * c_i + a_i * c_r
            rows = pl.ds(j * 8, 8)
            hr_ref[rows, ls] = hr_ref[rows, ls] + c_r
            hi_ref[rows, ls] = hi_ref[rows, ls] + c_i
            return c_r, c_i

        _unrolled(0, seg, fix, (cin_r, cin_i))


def _scan_bwd(gr_ref, gi_ref, hr_ref, hi_ref, cin_ref, a_ref, ap_ref, rcarry_ref, da_ref, seg):
    for lc in range(SSM_LANES // SCAN_LANES):
        ls = slice(lc * SCAN_LANES, (lc + 1) * SCAN_LANES)
        a_r = jnp.broadcast_to(a_ref[0:1, ls], (8, SCAN_LANES))
        a_i = jnp.broadcast_to(a_ref[1:2, ls], (8, SCAN_LANES))

        def step(t, gc, ls=ls, a_r=a_r, a_i=a_i):
            g_r, g_i = gc
            rows = pl.ds((seg - 1 - t) * 8, 8)
            n_r = gr_ref[rows, ls] + a_r * g_r + a_i * g_i
            n_i = gi_ref[rows, ls] + a_r * g_i - a_i * g_r
            gr_ref[rows, ls] = n_r
            gi_ref[rows, ls] = n_i
            return n_r, n_i

        zero = jnp.zeros((8, SCAN_LANES), F32)
        f_r, f_i = _unrolled(0, seg, step, (zero, zero))
        c_r = rcarry_ref[0:1, ls]
        c_i = rcarry_ref[1:2, ls]
        p_r = ap_ref[0:1, ls]
        p_i = ap_ref[1:2, ls]
        rows_r, rows_i = [None] * 8, [None] * 8
        for s in range(7, -1, -1):
            rows_r[s] = c_r
            rows_i[s] = c_i
            c_r, c_i = (f_r[s:s + 1] + p_r * c_r + p_i * c_i,
                        f_i[s:s + 1] + p_r * c_i - p_i * c_r)
        rcarry_ref[0:1, ls] = c_r
        rcarry_ref[1:2, ls] = c_i
        cin_r = jnp.concatenate(rows_r, axis=0)
        cin_i = jnp.concatenate(rows_i, axis=0)

        def fix_row(j_rows, hp_r, hp_i, cc, ls=ls, a_r=a_r, a_i=a_i):
            c_r, c_i, acc_r, acc_i = cc
            c_r, c_i = a_r * c_r + a_i * c_i, a_r * c_i - a_i * c_r
            g_r = gr_ref[j_rows, ls] + c_r
            g_i = gi_ref[j_rows, ls] + c_i
            gr_ref[j_rows, ls] = g_r
            gi_ref[j_rows, ls] = g_i
            acc_r = acc_r + g_r * hp_r + g_i * hp_i
            acc_i = acc_i + g_i * hp_r - g_r * hp_i
            return c_r, c_i, acc_r, acc_i

        def fix(t, cc, ls=ls, fix_row=fix_row):
            j = seg - 1 - t
            rows = pl.ds(j * 8, 8)
            prev = pl.ds((j - 1) * 8, 8)
            return fix_row(rows, hr_ref[prev, ls], hi_ref[prev, ls], cc)

        cc = _unrolled(0, seg - 1, fix, (cin_r, cin_i, zero, zero))
        _, _, acc_r, acc_i = fix_row(pl.ds(0, 8), cin_ref[0, :, ls], cin_ref[1, :, ls], cc)
        da_ref[0, :, ls] += acc_r
        da_ref[1, :, ls] += acc_i


def _s5_fwd(za, sp, bsz, seq, tb, bg=None):
    nb = seq // tb
    seg = tb // 8
    t = bsz * seq

    def body(za_ref, perm_ref, permt_ref, mre_ref, mim_ref, nre_ref, nim_ref, a_ref, ap_ref,
             dsk_ref, gw_ref, gb_ref, out_ref, outt_ref, y2_ref, car_ref, hr_ref, hi_ref, carry_ref):
        @pl.when(pl.program_id(1) == 0)
        def _():
            carry_ref[...] = jnp.zeros_like(carry_ref)

        car_ref[0] = carry_ref[...]
        up = _dot(perm_ref[...], za_ref[...])
        upb = up.astype(BF16)
        for bb in range(S5_BLOCKS):
            ub = upb[:, bb * S5_BLOCK_IN:(bb + 1) * S5_BLOCK_IN]
            st = slice(bb * S5_BLOCK_ST, (bb + 1) * S5_BLOCK_ST)
            hr_ref[:, st] = _dot(ub, mre_ref[bb])
            hi_ref[:, st] = _dot(ub, mim_ref[bb])
        _scan_fwd(hr_ref, hi_ref, a_ref, ap_ref, carry_ref, seg, None)
        ys = []
        for bb in range(S5_BLOCKS):
            st = slice(bb * S5_BLOCK_ST, (bb + 1) * S5_BLOCK_ST)
            ys.append(_dot(hr_ref[:, st].astype(BF16), nre_ref[bb])
                      - _dot(hi_ref[:, st].astype(BF16), nim_ref[bb]))
        y2 = jnp.concatenate(ys, axis=1) + dsk_ref[...] * up
        y2_ref[...] = y2
        y3 = _gelu(y2)
        gl = _dot(y3.astype(BF16), gw_ref[...]) + gb_ref[...]
        oa = y3 * _sigmoid(gl)
        out = _dot(permt_ref[...], oa.astype(BF16)).astype(BF16)
        out_ref[...] = out
        outt_ref[...] = out.T

    blk = pl.BlockSpec((tb, D_SSM), lambda b, j: (b * nb + j, 0))
    blk_t = pl.BlockSpec((D_SSM, tb), lambda b, j: (0, b * nb + j))
    m_shape = (S5_BLOCKS, S5_BLOCK_IN, S5_BLOCK_ST)
    n_shape = (S5_BLOCKS, S5_BLOCK_ST, S5_BLOCK_IN)
    return _call(
        body, "s5_fwd", (bsz, nb),
        [blk, _fixed((tb, tb)), _fixed((tb, tb)), _fixed(m_shape), _fixed(m_shape), _fixed(n_shape),
         _fixed(n_shape), _fixed((2, SSM_LANES)), _fixed((2, SSM_LANES)), _fixed((1, D_SSM)),
         _fixed((D_SSM, D_SSM)), _fixed((1, D_SSM))],
        (blk, blk_t, blk, pl.BlockSpec((1, 2, SSM_LANES), lambda b, j: (b * nb + j, 0, 0))),
        (SDS((t, D_SSM), BF16), SDS((D_SSM, t), BF16), SDS((t, D_SSM), F32), SDS((bsz * nb, 2, SSM_LANES), F32)),
        (za, sp["perm"], sp["permt"], sp["mre"], sp["mim"], sp["nre"], sp["nim"], sp["a"], sp["ap"],
         sp["dskip"], sp["glu_w"], sp["glu_b"]),
        scratch=[pltpu.VMEM((tb, SSM_LANES), F32), pltpu.VMEM((tb, SSM_LANES), F32),
                 pltpu.VMEM((2, SSM_LANES), F32)],
        sem=("arbitrary", "arbitrary"), bg=bg)


def _s5_bwd(za, y2p, doa, carries, sp, bsz, seq, tb, bg=None):
    nb = seq // tb
    seg = tb // 8
    t = bsz * seq

    def body(za_ref, y2_ref, doa_ref, car_ref, perm_ref, permt_ref, mre_ref, mim_ref, mtre_ref, mtim_ref,
             nre_ref, nim_ref, ntre_ref, ntim_ref, a_ref, ap_ref, dsk_ref, gw_ref, gwt_ref, gb_ref,
             dza_ref, dmr_ref, dmi_ref, dnr_ref, dni_ref, da_ref, ddsk_ref, dgw_ref, dgb_ref,
             hr_ref, hi_ref, gr_ref, gi_ref, cin_ref, carry_ref, rcarry_ref):
        first = jnp.logical_and(pl.program_id(0) == 0, pl.program_id(1) == 0)

        @pl.when(first)
        def _():
            for r in (dmr_ref, dmi_ref, dnr_ref, dni_ref, da_ref, ddsk_ref, dgw_ref, dgb_ref):
                r[...] = jnp.zeros_like(r)

        @pl.when(pl.program_id(1) == 0)
        def _():
            rcarry_ref[...] = jnp.zeros_like(rcarry_ref)

        carry_ref[...] = car_ref[0]
        perm = perm_ref[...]
        up = _dot(perm, za_ref[...])
        upb = up.astype(BF16)
        for bb in range(S5_BLOCKS):
            ub = upb[:, bb * S5_BLOCK_IN:(bb + 1) * S5_BLOCK_IN]
            st = slice(bb * S5_BLOCK_ST, (bb + 1) * S5_BLOCK_ST)
            hr_ref[:, st] = _dot(ub, mre_ref[bb])
            hi_ref[:, st] = _dot(ub, mim_ref[bb])
        _scan_fwd(hr_ref, hi_ref, a_ref, ap_ref, carry_ref, seg, cin_ref)

        y2 = y2_ref[...]
        y3 = _gelu(y2)
        y3b = y3.astype(BF16)
        sg = _sigmoid(_dot(y3b, gw_ref[...]) + gb_ref[...])
        d0 = doa_ref[...]
        d_hi = d0.astype(BF16)
        d1 = d0 - d_hi.astype(F32)
        d_mid = d1.astype(BF16)
        d_lo = (d1 - d_mid.astype(F32)).astype(BF16)
        doap = _dot(perm, d_hi) + _dot(perm, d_mid) + _dot(perm, d_lo)
        dgl = doap * y3 * sg * (1.0 - sg)
        dglb = dgl.astype(BF16)
        dy3 = doap * sg + _dot(dglb, gwt_ref[...])
        dgw_ref[...] += _dot_tn(y3b, dglb)
        dgb_ref[...] += jnp.sum(dgl, axis=0, keepdims=True)
        dy2 = dy3 * _gelu_grad(y2)
        ddsk_ref[...] += jnp.sum(dy2 * up, axis=0, keepdims=True)
        dyb = dy2.astype(BF16)
        for bb in range(S5_BLOCKS):
            dyc = dyb[:, bb * S5_BLOCK_IN:(bb + 1) * S5_BLOCK_IN]
            st = slice(bb * S5_BLOCK_ST, (bb + 1) * S5_BLOCK_ST)
            gr_ref[:, st] = _dot(dyc, ntre_ref[bb])
            gi_ref[:, st] = -_dot(dyc, ntim_ref[bb])
            dnr_ref[bb] += _dot_tn(hr_ref[:, st].astype(BF16), dyc)
            dni_ref[bb] += -_dot_tn(hi_ref[:, st].astype(BF16), dyc)
        _scan_bwd(gr_ref, gi_ref, hr_ref, hi_ref, cin_ref, a_ref, ap_ref, rcarry_ref, da_ref, seg)
        dus = []
        for bb in range(S5_BLOCKS):
            st = slice(bb * S5_BLOCK_ST, (bb + 1) * S5_BLOCK_ST)
            grb = gr_ref[:, st].astype(BF16)
            gib = gi_ref[:, st].astype(BF16)
            dus.append(_dot(grb, mtre_ref[bb]) + _dot(gib, mtim_ref[bb]))
            ub = upb[:, bb * S5_BLOCK_IN:(bb + 1) * S5_BLOCK_IN]
            dmr_ref[bb] += _dot_tn(ub, grb)
            dmi_ref[bb] += _dot_tn(ub, gib)
        du = jnp.concatenate(dus, axis=1) + dy2 * dsk_ref[...]
        dza_ref[...] = _dot(permt_ref[...], du.astype(BF16)).astype(BF16)

    def rev(b, j):
        return (b * nb + (nb - 1 - j), 0)

    blk = pl.BlockSpec((tb, D_SSM), rev)
    m_shape = (S5_BLOCKS, S5_BLOCK_IN, S5_BLOCK_ST)
    n_shape = (S5_BLOCKS, S5_BLOCK_ST, S5_BLOCK_IN)
    return _call(
        body, "s5_bwd", (bsz, nb),
        [blk, blk, blk, pl.BlockSpec((1, 2, SSM_LANES), lambda b, j: (b * nb + (nb - 1 - j), 0, 0)),
         _fixed((tb, tb)), _fixed((tb, tb)), _fixed(m_shape), _fixed(m_shape), _fixed(n_shape), _fixed(n_shape),
         _fixed(n_shape), _fixed(n_shape), _fixed(m_shape), _fixed(m_shape),
         _fixed((2, SSM_LANES)), _fixed((2, SSM_LANES)), _fixed((1, D_SSM)),
         _fixed((D_SSM, D_SSM)), _fixed((D_SSM, D_SSM)), _fixed((1, D_SSM))],
        (blk, _fixed(m_shape), _fixed(m_shape), _fixed(n_shape), _fixed(n_shape),
         _fixed((2, 8, SSM_LANES)), _fixed((1, D_SSM)), _fixed((D_SSM, D_SSM)), _fixed((1, D_SSM))),
        (SDS((t, D_SSM), BF16), SDS(m_shape, F32), SDS(m_shape, F32), SDS(n_shape, F32), SDS(n_shape, F32),
         SDS((2, 8, SSM_LANES), F32), SDS((1, D_SSM), F32), SDS((D_SSM, D_SSM), F32), SDS((1, D_SSM), F32)),
        (za, y2p, doa, carries, sp["perm"], sp["permt"], sp["mre"], sp["mim"], sp["mtre"], sp["mtim"],
         sp["nre"], sp["nim"], sp["ntre"], sp["ntim"], sp["a"], sp["ap"], sp["dskip"], sp["glu_w"],
         sp["glu_wt"], sp["glu_b"]),
        scratch=[pltpu.VMEM((tb, SSM_LANES), F32), pltpu.VMEM((tb, SSM_LANES), F32),
                 pltpu.VMEM((tb, SSM_LANES), F32), pltpu.VMEM((tb, SSM_LANES), F32),
                 pltpu.VMEM((2, 8, SSM_LANES), F32), pltpu.VMEM((2, SSM_LANES), F32),
                 pltpu.VMEM((2, SSM_LANES), F32)],
        sem=("arbitrary", "arbitrary"), bg=bg)


def _gmlp_spatial(ws_ref, vb):
    lane = lax.broadcasted_iota(jnp.int32, (CHUNK, 128), 1)
    parts = []
    for j in range(GMLP_HEADS // 2):
        vp = vb[:, 128 * j:128 * (j + 1)]
        parts.append(jnp.where(lane < GMLP_HEAD_DIM, _dot(ws_ref[2 * j], vp), _dot(ws_ref[2 * j + 1], vp)))
    return jnp.concatenate(parts, axis=1)


def _gmlp_fwd(zuv, ln_g, ln_b, wsm, bias, bg=None):
    t = zuv.shape[0]

    def body(z_ref, g_ref, b_ref, ws_ref, bias_ref, out_ref, outt_ref):
        u = _gelu(z_ref[:, 0:D_GMLP].astype(F32))
        v0 = _gelu(z_ref[:, D_GMLP:2 * D_GMLP].astype(F32))
        v, _, _ = _ln_fwd(v0, g_ref[...], b_ref[...])
        s = _gmlp_spatial(ws_ref, v.astype(BF16)) + bias_ref[...]
        out = (u * s).astype(BF16)
        out_ref[...] = out
        outt_ref[...] = out.T

    return _call(
        body, "gmlp_fwd", (t // CHUNK,),
        [_rows(CHUNK, 2 * D_GMLP), _fixed((1, D_GMLP)), _fixed((1, D_GMLP)),
         _fixed((GMLP_HEADS, CHUNK, CHUNK)), _fixed((CHUNK, D_GMLP))],
        (_rows(CHUNK, D_GMLP), _cols(D_GMLP, CHUNK)), (SDS((t, D_GMLP), BF16), SDS((D_GMLP, t), BF16)),
        (zuv, ln_g, ln_b, wsm, bias), sem=("parallel",), bg=bg)


def _gmlp_bwd(zuv, dgm, ln_g, ln_b, wsm, wsmt, bias, bg=None):
    t = zuv.shape[0]

    def body(z_ref, d_ref, g_ref, b_ref, ws_ref, wst_ref, bias_ref,
             dz_ref, dws_ref, dbias_ref, dg_ref, db_ref):
        @pl.when(pl.program_id(0) == 0)
        def _():
            for r in (dws_ref, dbias_ref, dg_ref, db_ref):
                r[...] = jnp.zeros_like(r)

        zu = z_ref[:, 0:D_GMLP].astype(F32)
        zv = z_ref[:, D_GMLP:2 * D_GMLP].astype(F32)
        u = _gelu(zu)
        v0 = _gelu(zv)
        gam = g_ref[...]
        v, vhat, rstd = _ln_fwd(v0, gam, b_ref[...])
        vb = v.astype(BF16)
        s = _gmlp_spatial(ws_ref, vb) + bias_ref[...]
        d = d_ref[...]
        dz_ref[:, 0:D_GMLP] = (d * s * _gelu_grad(zu)).astype(BF16)
        ds = d * u
        dbias_ref[...] += ds
        dsb = ds.astype(BF16)
        lane = lax.broadcasted_iota(jnp.int32, (CHUNK, 128), 1)
        tril = (lax.broadcasted_iota(jnp.int32, (CHUNK, CHUNK), 0)
                >= lax.broadcasted_iota(jnp.int32, (CHUNK, CHUNK), 1))
        zero_b = jnp.zeros((CHUNK, 128), BF16)
        parts = []
        for j in range(GMLP_HEADS // 2):
            dsp = dsb[:, 128 * j:128 * (j + 1)]
            vp = vb[:, 128 * j:128 * (j + 1)]
            parts.append(jnp.where(lane < GMLP_HEAD_DIM, _dot(wst_ref[2 * j], dsp),
                                   _dot(wst_ref[2 * j + 1], dsp)))
            lo = jnp.where(lane < GMLP_HEAD_DIM, dsp, zero_b)
            hi = jnp.where(lane < GMLP_HEAD_DIM, zero_b, dsp)
            dws_ref[2 * j] += jnp.where(tril, _dot_nt(lo, vp), 0.0)
            dws_ref[2 * j + 1] += jnp.where(tril, _dot_nt(hi, vp), 0.0)
        dv = jnp.concatenate(parts, axis=1)
        dg_ref[...] += jnp.sum(dv * vhat, axis=0, keepdims=True)
        db_ref[...] += jnp.sum(dv, axis=0, keepdims=True)
        dz_ref[:, D_GMLP:2 * D_GMLP] = (_ln_bwd(dv, vhat, rstd, gam) * _gelu_grad(zv)).astype(BF16)

    return _call(
        body, "gmlp_bwd", (t // CHUNK,),
        [_rows(CHUNK, 2 * D_GMLP), _rows(CHUNK, D_GMLP), _fixed((1, D_GMLP)), _fixed((1, D_GMLP)),
         _fixed((GMLP_HEADS, CHUNK, CHUNK)), _fixed((GMLP_HEADS, CHUNK, CHUNK)), _fixed((CHUNK, D_GMLP))],
        (_rows(CHUNK, 2 * D_GMLP), _fixed((GMLP_HEADS, CHUNK, CHUNK)), _fixed((CHUNK, D_GMLP)),
         _fixed((1, D_GMLP)), _fixed((1, D_GMLP))),
        (SDS((t, 2 * D_GMLP), BF16), SDS((GMLP_HEADS, CHUNK, CHUNK), F32), SDS((CHUNK, D_GMLP), F32),
         SDS((1, D_GMLP), F32), SDS((1, D_GMLP), F32)),
        (zuv, dgm, ln_g, ln_b, wsm, wsmt, bias), sem=("arbitrary",), bg=bg)


def _mixout_fwd(x1, s5o, gm, gab, ua, ub, wmo, g, b, tm, bg=None):
    t = x1.shape[0]

    def body(x_ref, s_ref, m_ref, gab_ref, ua_ref, ub_ref, wmo_ref, g_ref, b_ref,
             xn_ref, xh_ref, rstd_ref):
        ya = _dot(s_ref[...], ua_ref[...])
        yb = _dot(m_ref[...], ub_ref[...])
        mix = (_sigmoid(gab_ref[:, 0:D_MODEL].astype(F32)) * ya
               + _sigmoid(gab_ref[:, D_MODEL:2 * D_MODEL].astype(F32)) * yb)
        r = ALPHA * x_ref[...] + _dot(mix.astype(BF16), wmo_ref[...])
        y, xh, rstd = _ln_fwd(r, g_ref[...], b_ref[...])
        xn_ref[...] = y
        xh_ref[...] = xh
        rstd_ref[...] = rstd

    return _call(
        body, "mixout_fwd", (t // tm,),
        [_rows(tm, D_MODEL), _rows(tm, D_SSM), _rows(tm, D_GMLP), _rows(tm, 2 * D_MODEL),
         _resident((D_SSM, D_MODEL)), _resident((D_GMLP, D_MODEL)), _resident((D_MODEL, D_MODEL)),
         _fixed((1, D_MODEL)), _fixed((1, D_MODEL))],
        (_rows(tm, D_MODEL), _rows(tm, D_MODEL), _rows(tm, 1)),
        (SDS((t, D_MODEL), F32), SDS((t, D_MODEL), F32), SDS((t, 1), F32)),
        (x1, s5o, gm, gab, ua, ub, wmo, g, b), sem=("parallel",), bg=bg)


def _mixout_bwd(dx2, xh, rstd, s5o, gm, gab, ua, ub, wmo, g, tm, bg=None):
    t = dx2.shape[0]

    def body(d_ref, xh_ref, rstd_ref, s_ref, m_ref, gab_ref, ua_ref, ub_ref, wmo_ref, g_ref,
             dx1_ref, dmx_ref, mb_ref, dya_ref, dyb_ref, ds5_ref, dgm_ref, dgab_ref, dg_ref, db_ref):
        @pl.when(pl.program_id(0) == 0)
        def _():
            dg_ref[...] = jnp.zeros_like(dg_ref)
            db_ref[...] = jnp.zeros_like(db_ref)

        dy = d_ref[...]
        xhv = xh_ref[...]
        dr = _ln_bwd(dy, xhv, rstd_ref[...], g_ref[...])
        dg_ref[...] += jnp.sum(dy * xhv, axis=0, keepdims=True)
        db_ref[...] += jnp.sum(dy, axis=0, keepdims=True)
        dx1_ref[...] = ALPHA * dr
        drb = dr.astype(BF16)
        dmx_ref[...] = drb
        dm = _dot_nt(drb, wmo_ref[...])
        ya = _dot(s_ref[...], ua_ref[...])
        yb = _dot(m_ref[...], ub_ref[...])
        sa = _sigmoid(gab_ref[:, 0:D_MODEL].astype(F32))
        sb = _sigmoid(gab_ref[:, D_MODEL:2 * D_MODEL].astype(F32))
        mb_ref[...] = (sa * ya + sb * yb).astype(BF16).T
        dya = (dm * sa).astype(BF16)
        dyb = (dm * sb).astype(BF16)
        dya_ref[...] = dya
        dyb_ref[...] = dyb
        dgab_ref[:, 0:D_MODEL] = (dm * ya * sa * (1.0 - sa)).astype(BF16)
        dgab_ref[:, D_MODEL:2 * D_MODEL] = (dm * yb * sb * (1.0 - sb)).astype(BF16)
        ds5_ref[...] = _dot_nt(dya, ua_ref[...])
        dgm_ref[...] = _dot_nt(dyb, ub_ref[...])

    return _call(
        body, "mixout_bwd", (t // tm,),
        [_rows(tm, D_MODEL), _rows(tm, D_MODEL), _rows(tm, 1), _rows(tm, D_SSM), _rows(tm, D_GMLP),
         _rows(tm, 2 * D_MODEL), _resident((D_SSM, D_MODEL)), _resident((D_GMLP, D_MODEL)),
         _resident((D_MODEL, D_MODEL)), _fixed((1, D_MODEL))],
        (_rows(tm, D_MODEL), _rows(tm, D_MODEL), _cols(D_MODEL, tm), _rows(tm, D_MODEL),
         _rows(tm, D_MODEL), _rows(tm, D_SSM), _rows(tm, D_GMLP), _rows(tm, 2 * D_MODEL),
         _fixed((1, D_MODEL)), _fixed((1, D_MODEL))),
        (SDS((t, D_MODEL), F32), SDS((t, D_MODEL), BF16), SDS((D_MODEL, t), BF16),
         SDS((t, D_MODEL), BF16), SDS((t, D_MODEL), BF16), SDS((t, D_SSM), F32),
         SDS((t, D_GMLP), F32), SDS((t, 2 * D_MODEL), BF16),
         SDS((1, D_MODEL), F32), SDS((1, D_MODEL), F32)),
        (dx2, xh, rstd, s5o, gm, gab, ua, ub, wmo, g), sem=("arbitrary",), bg=bg)


def _ple_loss(x3, p, tgt, wpg, wpp, tm, bg=None):
    t = x3.shape[0]

    def body(x_ref, p_ref, t_ref, wpg_ref, wpp_ref, dx_ref, xb_ref, pb_ref, dq_ref, de_ref, loss_ref):
        @pl.when(pl.program_id(0) == 0)
        def _():
            loss_ref[...] = jnp.zeros_like(loss_ref)

        x3v = x_ref[...]
        xb = x3v.astype(BF16)
        pb = p_ref[...].astype(BF16)
        xb_ref[...] = xb.T
        pb_ref[...] = pb.T
        s = _sigmoid(_dot(xb, wpg_ref[...]))
        e = _dot(pb, wpp_ref[...])
        diff = x3v + s * e - t_ref[...]
        loss_ref[...] += jnp.sum(diff * diff, axis=0, keepdims=True)
        dout = diff * (1.0 / D_MODEL)
        de_ref[...] = (dout * s).astype(BF16)
        dq = (dout * e * s * (1.0 - s)).astype(BF16)
        dq_ref[...] = dq
        dx_ref[...] = dout + _dot_nt(dq, wpg_ref[...])

    return _call(
        body, "ple_loss", (t // tm,),
        [_rows(tm, D_MODEL), _rows(tm, PLE_DIM), _rows(tm, D_MODEL),
         _resident((D_MODEL, D_MODEL)), _resident((PLE_DIM, D_MODEL))],
        (_rows(tm, D_MODEL), _cols(D_MODEL, tm), _cols(PLE_DIM, tm), _rows(tm, D_MODEL),
         _rows(tm, D_MODEL), _fixed((1, D_MODEL))),
        (SDS((t, D_MODEL), F32), SDS((D_MODEL, t), BF16), SDS((PLE_DIM, t), BF16),
         SDS((t, D_MODEL), BF16), SDS((t, D_MODEL), BF16), SDS((1, D_MODEL), F32)),
        (x3, p, tgt, wpg, wpp), sem=("arbitrary",), bg=bg)


def _s5_discretise(lre, lim, log_dt, bre, bim):
    dt = jnp.exp(log_dt)[:, None]
    mag = jnp.exp(lre * dt)
    abr = mag * jnp.cos(lim * dt)
    abi = mag * jnp.sin(lim * dt)
    nr = abr - 1.0
    ni = abi
    den = lre * lre + lim * lim
    cr = ((nr * lre + ni * lim) / den)[..., None]
    ci = ((ni * lre - nr * lim) / den)[..., None]
    return abr, abi, cr * bre - ci * bim, cr * bim + ci * bre


def _block_diag_in(bb):
    v = bb.reshape(S5_BLOCKS, 8, SSM_STATE, SSM_GROUP_CH).transpose(0, 1, 3, 2)
    return jnp.einsum("bgip,gh->bgihp", v, jnp.eye(8, dtype=bb.dtype)).reshape(
        S5_BLOCKS, S5_BLOCK_IN, S5_BLOCK_ST)


def _block_diag_in_t(dm):
    v = dm.reshape(S5_BLOCKS, 8, SSM_GROUP_CH, 8, SSM_STATE)
    d = jnp.einsum("bgihp,gh->bgip", v, jnp.eye(8, dtype=dm.dtype))
    return d.transpose(0, 1, 3, 2).reshape(SSM_GROUPS, SSM_STATE, SSM_GROUP_CH)


def _block_diag_out(cc):
    v = cc.reshape(S5_BLOCKS, 8, SSM_GROUP_CH, SSM_STATE)
    return jnp.einsum("bgip,gh->bgphi", v, jnp.eye(8, dtype=cc.dtype)).reshape(
        S5_BLOCKS, S5_BLOCK_ST, S5_BLOCK_IN)


def _block_diag_out_t(dn):
    v = dn.reshape(S5_BLOCKS, 8, SSM_STATE, 8, SSM_GROUP_CH)
    d = jnp.einsum("bgphi,gh->bgip", v, jnp.eye(8, dtype=dn.dtype))
    return d.reshape(SSM_GROUPS, SSM_GROUP_CH, SSM_STATE)


def _s5_setup(lre, lim, log_dt, bre, bim, cre, cim, d_skip, glu_w, glu_b, tb):
    seg = tb // 8
    abr, abi, bbr, bbi = _s5_discretise(lre, lim, log_dt, bre, bim)
    pr, pi = abr, abi
    for _ in range(int(math.log2(seg))):
        pr, pi = pr * pr - pi * pi, 2.0 * pr * pi
    rows = jnp.arange(tb)
    src = (rows % 8) * seg + rows // 8
    perm = (src[:, None] == jnp.arange(tb)[None, :]).astype(BF16)
    mre = _block_diag_in(bbr)
    mim = _block_diag_in(bbi)
    nre = _block_diag_out(cre)
    nim = _block_diag_out(cim)
    return {
        "perm": perm, "permt": perm.T,
        "mre": mre.astype(BF16), "mim": mim.astype(BF16),
        "mtre": mre.transpose(0, 2, 1).astype(BF16), "mtim": mim.transpose(0, 2, 1).astype(BF16),
        "nre": nre.astype(BF16), "nim": nim.astype(BF16),
        "ntre": nre.transpose(0, 2, 1).astype(BF16), "ntim": nim.transpose(0, 2, 1).astype(BF16),
        "a": jnp.stack([abr.reshape(-1), abi.reshape(-1)]),
        "ap": jnp.stack([pr.reshape(-1), pi.reshape(-1)]),
        "dskip": d_skip.reshape(1, D_SSM), "glu_w": glu_w, "glu_wt": glu_w.T,
        "glu_b": glu_b.reshape(1, D_SSM),
    }


BIG = ("ffn1_w_in", "ffn1_w_out", "mix_w_in", "ssm_glu_w", "up_a", "up_b", "mix_w_out",
       "ffn2_w_in", "ffn2_w_out", "ple_w_proj", "ple_w_gate")
BIG_AXIS = {"ffn1_w_in": 1, "ffn1_w_out": 0, "mix_w_in": 1, "ssm_glu_w": 0, "up_a": 1, "up_b": 1,
            "mix_w_out": 0, "ffn2_w_in": 1, "ffn2_w_out": 0, "ple_w_proj": 1, "ple_w_gate": 0}
SHARD_MAJOR = 2
GATHER_AXIS = dict(BIG_AXIS, ffn1_w_in=SHARD_MAJOR, ffn2_w_in=SHARD_MAJOR)
GATHER_ORDER = (("ffn1_w_in",), ("ffn1_w_out",), ("mix_w_in",), ("ssm_glu_w", "up_a", "up_b", "mix_w_out"),
                ("ffn2_w_in",), ("ffn2_w_out", "ple_w_gate", "ple_w_proj"))
GATHER_FIRST_ID = 1
REDUCE_FIRST_ID = 7
SMALL = ("ln1_g", "ln1_b", "ssm_lambda_re", "ssm_lambda_im", "ssm_log_dt", "ssm_b_re", "ssm_b_im",
         "ssm_c_re", "ssm_c_im", "ssm_d", "ssm_glu_b", "gmlp_ln_g", "gmlp_ln_b", "gmlp_w_s",
         "gmlp_b_s", "ln2_g", "ln2_b", "ln3_g", "ln3_b")
SMALL_VIEW = {"ssm_b_re": (SSM_GROUPS, SSM_STATE * SSM_GROUP_CH), "ssm_b_im": (SSM_GROUPS, SSM_STATE * SSM_GROUP_CH)}


def _small_view(k, a):
    return a.reshape(SMALL_VIEW[k]) if k in SMALL_VIEW else a


def _place():
    return lax.axis_index("x"), lax.axis_index("y"), lax.axis_index("c")


def _other_chips(x, y):
    return [(1 - x, y), (x, 1 - y), (1 - x, 1 - y)]


def _window(ref, shard_shape, axis, chip, half):
    r, c = shard_shape
    hr = r // 2
    if axis == SHARD_MAJOR:
        return ref.at[chip] if half is None else ref.at[chip, pl.ds(half * hr, hr), :]
    if axis == 0:
        if half is None:
            return ref.at[pl.ds(chip * r, r), :]
        return ref.at[pl.ds(chip * r + half * hr, hr), :]
    if half is None:
        return ref.at[:, pl.ds(chip * c, c)]
    return ref.at[pl.ds(half * hr, hr), pl.ds(chip * c, c)]


def _gather_weights(shards, axes):
    n = len(shards)
    shapes = [s.shape for s in shards]
    full = [{0: (4 * r, c), 1: (r, 4 * c), SHARD_MAJOR: (4, r, c)}[ax] for (r, c), ax in zip(shapes, axes)]

    def remote(sems, i, k, src, dst, to):
        return pltpu.make_async_remote_copy(src_ref=src, dst_ref=dst, send_sem=sems[0].at[6 * i + k],
                                            recv_sem=sems[1].at[6 * i + k], device_id=to, device_id_type=MESH)

    def own_copies(ins, outs, sems):
        x, y, c = _place()
        me = 2 * x + y
        cps = []
        for i in range(n):
            hr = shapes[i][0] // 2
            mine = ins[i].at[pl.ds(c * hr, hr), :]
            for j, (cx, cy) in enumerate(_other_chips(x, y)):
                cps.append(remote(sems, i, j, mine, _window(outs[i], shapes[i], axes[i], me, c), (cx, cy, c)))
        local = [pltpu.make_async_copy(ins[i], _window(outs[i], shapes[i], axes[i], me, None), sems[2].at[i])
                 for i in range(n)]
        return cps, local

    def start(ins, outs, sems):
        cps, local = own_copies(ins, outs, sems)
        for cp in local + cps:
            cp.start()

    def finish(ins, outs, sems):
        x, y, c = _place()
        sibling = (x, y, 1 - c)
        passed = []
        for j, (cx, cy) in enumerate(_other_chips(x, y)):
            for i in range(n):
                w = _window(outs[i], shapes[i], axes[i], 2 * cx + cy, c)
                remote(sems, i, j, w, w, (cx, cy, c)).wait_recv()
                cp = remote(sems, i, 3 + j, w, w, sibling)
                cp.start()
                passed.append(cp)
        for j, (cx, cy) in enumerate(_other_chips(x, y)):
            for i in range(n):
                w = _window(outs[i], shapes[i], axes[i], 2 * cx + cy, 1 - c)
                remote(sems, i, 3 + j, w, w, sibling).wait_recv()
        cps, local = own_copies(ins, outs, sems)
        for cp in cps + passed:
            cp.wait_send()
        for cp in local:
            cp.wait()

    return _Exchange(shards, [SDS(f, BF16) for f in full],
                     [pltpu.SemaphoreType.DMA((6 * n,)), pltpu.SemaphoreType.DMA((6 * n,)),
                      pltpu.SemaphoreType.DMA((n,))], start, finish)


def _scatter_grads(parts, shapes, axes):
    n = len(parts)

    def copies(ins, outs, sems):
        x, y, c = _place()
        return [pltpu.make_async_remote_copy(
            src_ref=_window(ins[i], shapes[i], axes[i], 2 * cx + cy, None), dst_ref=outs[i].at[j],
            send_sem=sems[0].at[3 * i + j], recv_sem=sems[1].at[3 * i + j],
            device_id=(cx, cy, c), device_id_type=MESH)
            for i in range(n) for j, (cx, cy) in enumerate(_other_chips(x, y))]

    def start(ins, outs, sems):
        for cp in copies(ins, outs, sems):
            cp.start()

    def finish(ins, outs, sems):
        for cp in copies(ins, outs, sems):
            cp.wait()

    return _Exchange(parts, [SDS((3,) + tuple(s), BF16) for s in shapes],
                     [pltpu.SemaphoreType.DMA((3 * n,)), pltpu.SemaphoreType.DMA((3 * n,))], start, finish)


def _swap_halves(parts, shapes, axes):
    n = len(parts)

    def copies(ins, outs, sems):
        x, y, c = _place()
        cps = []
        for i in range(n):
            r, _ = shapes[i]
            hr = r // 2
            if axes[i] == 0:
                cps += [pltpu.make_async_remote_copy(
                    src_ref=ins[i].at[pl.ds(k * r + (1 - c) * hr, hr), :], dst_ref=outs[i].at[k],
                    send_sem=sems[0].at[i], recv_sem=sems[1].at[i], device_id=(x, y, 1 - c),
                    device_id_type=MESH) for k in range(4)]
            else:
                cps.append(pltpu.make_async_remote_copy(
                    src_ref=ins[i].at[pl.ds((1 - c) * hr, hr), :], dst_ref=outs[i],
                    send_sem=sems[0].at[i], recv_sem=sems[1].at[i], device_id=(x, y, 1 - c),
                    device_id_type=MESH))
        return cps

    def start(ins, outs, sems):
        for cp in copies(ins, outs, sems):
            cp.start()

    def finish(ins, outs, sems):
        x, y, c = _place()
        for i in range(n):
            pltpu.make_async_remote_copy(src_ref=outs[i], dst_ref=outs[i], send_sem=sems[0].at[i],
                                         recv_sem=sems[1].at[i], device_id=(x, y, 1 - c),
                                         device_id_type=MESH).wait()

    out = [SDS((4, r // 2, c), BF16) if ax == 0 else SDS((r // 2, 4 * c), BF16)
           for (r, c), ax in zip(shapes, axes)]
    return _Exchange(parts, out, [pltpu.SemaphoreType.DMA((n,)), pltpu.SemaphoreType.DMA((n,))], start, finish)


def _scatter_halves(pres, shapes):
    n = len(pres)

    def copies(ins, outs, sems):
        x, y, c = _place()
        return [pltpu.make_async_remote_copy(
            src_ref=ins[i].at[1 + j], dst_ref=outs[i].at[j], send_sem=sems[0].at[3 * i + j],
            recv_sem=sems[1].at[3 * i + j], device_id=(cx, cy, c), device_id_type=MESH)
            for i in range(n) for j, (cx, cy) in enumerate(_other_chips(x, y))]

    def start(ins, outs, sems):
        for cp in copies(ins, outs, sems):
            cp.start()

    def finish(ins, outs, sems):
        for cp in copies(ins, outs, sems):
            cp.wait()

    return _Exchange(pres, [SDS((3, r // 2, c), BF16) for r, c in shapes],
                     [pltpu.SemaphoreType.DMA((3 * n,)), pltpu.SemaphoreType.DMA((3 * n,))], start, finish)


def _swap_with_sibling(arrs):
    n = len(arrs)

    def copies(ins, outs, sems):
        x, y, c = _place()
        return [pltpu.make_async_remote_copy(src_ref=ins[i], dst_ref=outs[i], send_sem=sems[0].at[i],
                                             recv_sem=sems[1].at[i], device_id=(x, y, 1 - c),
                                             device_id_type=MESH) for i in range(n)]

    def start(ins, outs, sems):
        for cp in copies(ins, outs, sems):
            cp.start()

    def finish(ins, outs, sems):
        for cp in copies(ins, outs, sems):
            cp.wait()

    return _Exchange(arrs, [SDS(a.shape, a.dtype) for a in arrs],
                     [pltpu.SemaphoreType.DMA((n,)), pltpu.SemaphoreType.DMA((n,))], start, finish)


def _gather_small(arrs):
    n = len(arrs)

    def copy(sems, outs, i, k, block, to, src=None):
        px, py, pc = block
        dst = outs[i].at[4 * px + 2 * py + pc]
        return pltpu.make_async_remote_copy(
            src_ref=dst if src is None else src, dst_ref=dst, send_sem=sems[0].at[7 * i + k],
            recv_sem=sems[1].at[7 * i + k], device_id=to, device_id_type=MESH)

    direct = [math.prod(a.shape) * 4 <= DIRECT_GATHER_BYTES for a in arrs]

    def own_copies(ins, outs, sems):
        x, y, c = _place()
        cps = []
        for i in range(n):
            cps.append(copy(sems, outs, i, 0, (x, y, c), (x, y, 1 - c), src=ins[i]))
            for j, (cx, cy) in enumerate(_other_chips(x, y)):
                cps.append(copy(sems, outs, i, 1 + j, (x, y, c), (cx, cy, c), src=ins[i]))
                if direct[i]:
                    cps.append(copy(sems, outs, i, 4 + j, (x, y, c), (cx, cy, 1 - c), src=ins[i]))
        local = [pltpu.make_async_copy(ins[i], outs[i].at[4 * x + 2 * y + c], sems[2].at[i]) for i in range(n)]
        return cps, local

    def start(ins, outs, sems):
        cps, local = own_copies(ins, outs, sems)
        for cp in local + cps:
            cp.start()

    def finish(ins, outs, sems):
        x, y, c = _place()
        passed = []
        for j, (cx, cy) in enumerate(_other_chips(x, y)):
            for i in range(n):
                copy(sems, outs, i, 1 + j, (cx, cy, c), (x, y, c)).wait_recv()
                if not direct[i]:
                    cp = copy(sems, outs, i, 4 + j, (cx, cy, c), (x, y, 1 - c))
                    cp.start()
                    passed.append(cp)
        for i in range(n):
            copy(sems, outs, i, 0, (x, y, 1 - c), (x, y, c)).wait_recv()
            for j, (cx, cy) in enumerate(_other_chips(x, y)):
                copy(sems, outs, i, 4 + j, (cx, cy, 1 - c), (x, y, c)).wait_recv()
        cps, local = own_copies(ins, outs, sems)
        for cp in cps + passed:
            cp.wait_send()
        for cp in local:
            cp.wait()

    return _Exchange(arrs, [SDS((N_DEV,) + a.shape, F32) for a in arrs],
                     [pltpu.SemaphoreType.DMA((7 * n,)), pltpu.SemaphoreType.DMA((7 * n,)),
                      pltpu.SemaphoreType.DMA((n,))], start, finish)


def _local_step(x, p, tgt, wb, ws, shards=None, opt=None):
    bsz, seq, _ = x.shape
    t = bsz * seq
    tm = min(256, t)
    tb = min(256, seq)
    x0 = x.reshape(t, D_MODEL)
    p0 = p.reshape(t, PLE_DIM)
    tg = tgt.reshape(t, D_MODEL)
    row = lambda v: v.reshape(1, -1)
    dist = shards is not None
    wb = dict(wb)
    recv, sums, other, gathered = {}, {}, {}, {}
    gb = {}
    gs = {}
    shape_of, axis_of = {}, {}
    chip = None
    if dist:
        shape_of = {k: tuple(shards[k].shape) for k in BIG}
        axis_of = dict(BIG_AXIS)
        for q in range(LAST_PIECES):
            shape_of[LAST_PIECE % q] = (D_MODEL // LAST_PIECES, shape_of["ffn1_w_in"][1])
            axis_of[LAST_PIECE % q] = 1
        xi, yi, ci = _place()
        chip = (2 * xi + yi).astype(jnp.int32).reshape(1)
        ids = jnp.stack([2 * xi + yi] + [2 * cx + cy for cx, cy in _other_chips(xi, yi)] + [ci]).astype(jnp.int32)
    halfbuf, pre = {}, {}

    def gather(names):
        return _gather_weights([shards[k] for k in names], [GATHER_AXIS[k] for k in names]) if dist else None

    def exchange(scat=(), swap=(), halves=(), scat2=(), swap2=(), extra=None, after=None):
        if not dist:
            return None, []
        after = order[0] if after is None else after
        parts, tags = [], []
        if scat:
            parts.append(_scatter_grads([gb[k][1] for k in scat], [shape_of[k] for k in scat],
                                        [axis_of[k] for k in scat]))
            tags.append((recv, scat))
        if swap:
            for k in swap:
                sums[k] = order[0] = _sum_blocks(gb[k][0], recv[k], shape_of[k], axis_of[k], chip, "sum_" + k,
                                                 order[0])
            parts.append(_swap_with_sibling([sums[k] for k in swap]))
            tags.append((other, swap))
        if halves:
            parts.append(_swap_halves([gb[k][1] for k in halves], [shape_of[k] for k in halves],
                                      [axis_of[k] for k in halves]))
            tags.append((halfbuf, halves))
        if scat2:
            for k in scat2:
                pre[k] = _presum(gb[k][0], halfbuf[k], shape_of[k], axis_of[k], ids, "presum_" + k, order[0])
                order[0] = pre[k][0]
            parts.append(_scatter_halves([pre[k][1] for k in scat2], [shape_of[k] for k in scat2]))
            tags.append((recv, scat2))
        if swap2:
            for k in swap2:
                sums[k] = order[0] = _sum_half(pre[k][0], recv[k], "sum_" + k, order[0])
            parts.append(_swap_with_sibling([sums[k] for k in swap2]))
            tags.append((other, swap2))
        if extra is not None:
            parts.append(extra[0])
            tags.append((extra[1], extra[2]))
        return (_join(parts), tags) if parts else (None, [])

    def take(ex_tags, got):
        ex, tags = ex_tags
        if ex is not None:
            for (dst, names), (o0, o1) in zip(tags, ex.cuts):
                dst.update(zip(names, got[o0:o1]))

    order = [None]

    def ordered(builder, *args, **kw):
        res = builder(*args, bg=order[0] if dist else None, **kw)
        order[0] = res[0][0]
        return res

    launched = []

    def launch(ex_tags):
        if ex_tags[0] is not None:
            n = len(launched)
            launched.append(n)
            take(ex_tags, _run_exchange_on_sequencer(ex_tags[0], "reduce_%d" % n, REDUCE_FIRST_ID + n))

    small_shape = {k: _small_view(k, v).shape for k, v in ws.items()}
    small_shape["loss_rows"] = (1, D_MODEL)
    ws = {k: v if (v.ndim == 2 and k != "ssm_log_dt") else v[0] for k, v in ws.items()}
    tril = jnp.tril(jnp.ones((CHUNK, CHUNK), dtype=bool))
    wsm = jnp.where(tril[None], ws["gmlp_w_s"], 0.0)
    wsm_b = wsm.astype(BF16)
    wsmt_b = wsm.transpose(0, 2, 1).astype(BF16)
    bias = jnp.repeat(ws["gmlp_b_s"].T, GMLP_HEAD_DIM, axis=1)

    tf = min(512, t)
    if dist:
        for gi, names in enumerate(GATHER_ORDER):
            wb.update(zip(names, _run_exchange_on_sequencer(gather(names), "gather_%d" % gi, GATHER_FIRST_ID + gi)))
    (x0b, h1, a1), _ = _ffn_proj(x0, wb["ffn1_w_in"], tf, "ffn1_proj")
    (x1, xh1, rstd1), _ = _ffn_out(x0, a1, wb["ffn1_w_out"], row(ws["ln1_g"]), row(ws["ln1_b"]), tf, "ffn1_out")
    sp = _s5_setup(ws["ssm_lambda_re"], ws["ssm_lambda_im"], ws["ssm_log_dt"], ws["ssm_b_re"],
                   ws["ssm_b_im"], ws["ssm_c_re"], ws["ssm_c_im"], ws["ssm_d"], wb["ssm_glu_w"],
                   ws["ssm_glu_b"], tb)
    (x1b, za, zuv, gab), _ = _mixin_fwd(x1, wb["mix_w_in"], tf)
    (s5o, s5ot, y2p, carries), _ = _s5_fwd(za, sp, bsz, seq, tb)
    (gm, gmt), _ = _gmlp_fwd(zuv, row(ws["gmlp_ln_g"]), row(ws["gmlp_ln_b"]), wsm_b, bias)
    (x2, xh2, rstd2), _ = _mixout_fwd(x1, s5o, gm, gab, wb["up_a"], wb["up_b"], wb["mix_w_out"],
                                           row(ws["ln2_g"]), row(ws["ln2_b"]), tf)
    (x2b, h2, a2), _ = _ffn_proj(x2, wb["ffn2_w_in"], tf, "ffn2_proj")
    (x3, xh3, rstd3), _ = _ffn_out(x2, a2, wb["ffn2_w_out"], row(ws["ln3_g"]), row(ws["ln3_b"]), tf, "ffn2_out")
    (dx3, x3b, pb, dq, de, loss_rows), _ = _ple_loss(x3, p0, tg, wb["ple_w_gate"], wb["ple_w_proj"], tf)
    order[0] = dx3
    gb["ple_w_gate"], _ = ordered(_tn_matmul, x3b, dq, "dw_ple_gate", 1024, 1024, a_t=True)
    gb["ple_w_proj"], _ = ordered(_tn_matmul, pb, de, "dw_ple_proj", 256, 1024, a_t=True)
    launch(exchange(scat=("ple_w_gate", "ple_w_proj")))
    (dx2, dh2, df2, gs["ln3_g"], gs["ln3_b"]), _ = ordered(
        _ffn_bwd, dx3, xh3, rstd3, h2, wb["ffn2_w_in"], wb["ffn2_w_out"], row(ws["ln3_g"]), tm, "ffn2_bwd")
    gb["ffn2_w_out"], _ = ordered(_tn_matmul, a2, df2, "dw_ffn2_out", 1408, 1024)
    launch(exchange(scat=("ffn2_w_out",)))
    gb["ffn2_w_in"], _ = ordered(_tn_matmul, x2b, dh2, "dw_ffn2_in", 1024, 1408, a_t=True)
    launch(exchange(scat=("ffn2_w_in",), swap=("ple_w_gate", "ple_w_proj")))
    (dx1a, dmx, mb, dya, dyb, ds5, dgm, dgab, gs["ln2_g"], gs["ln2_b"]), _ = ordered(
        _mixout_bwd, dx2, xh2, rstd2, s5o, gm, gab, wb["up_a"], wb["up_b"], wb["mix_w_out"], row(ws["ln2_g"]), tm)
    gb["mix_w_out"], _ = ordered(_tn_matmul, mb, dmx, "dw_mix_out", 1024, 1024, a_t=True)
    gb["up_a"], _ = ordered(_tn_matmul, s5ot, dya, "dw_up_a", 512, 1024, a_t=True)
    gb["up_b"], _ = ordered(_tn_matmul, gmt, dyb, "dw_up_b", 512, 1024, a_t=True)
    launch(exchange(scat=("mix_w_out", "up_a", "up_b"), swap=("ffn2_w_out",)))
    (dza, dmr, dmi, dnr, dni, da, ddsk, dgw, dgb), _ = ordered(_s5_bwd, za, y2p, ds5, carries, sp, bsz, seq, tb)
    gb["ssm_glu_w"] = (dgw, dgw.astype(BF16))
    launch(exchange(scat=("ssm_glu_w",), swap=("ffn2_w_in",)))
    (dzuv, dws, dbias, gs["gmlp_ln_g"], gs["gmlp_ln_b"]), _ = ordered(
        _gmlp_bwd, zuv, dgm, row(ws["gmlp_ln_g"]), row(ws["gmlp_ln_b"]), wsm_b, wsmt_b, bias)
    (dx1,), _ = ordered(_mixin_bwd, dx1a, dza, dzuv, dgab, wb["mix_w_in"], tf)
    g_mi, _ = ordered(_tn_matmul, x1b, dza, "dw_mix_in_a", 1024, 512, 0, 3584, a_t=True)
    g_mi, _ = ordered(_tn_matmul, x1b, dzuv, "dw_mix_in_uv", 1024, 512, 1, 3584, g_mi, a_t=True)
    gb["mix_w_in"], _ = ordered(_tn_matmul, x1b, dgab, "dw_mix_in_g", 1024, 512, 3, 3584, g_mi, a_t=True)
    launch(exchange(swap=("mix_w_out", "up_a", "up_b", "ssm_glu_w")))

    d_abr = da[0].sum(axis=0).reshape(SSM_GROUPS, SSM_STATE)
    d_abi = da[1].sum(axis=0).reshape(SSM_GROUPS, SSM_STATE)
    _, vjp = jax.vjp(_s5_discretise, ws["ssm_lambda_re"], ws["ssm_lambda_im"], ws["ssm_log_dt"],
                     ws["ssm_b_re"], ws["ssm_b_im"])
    (gs["ssm_lambda_re"], gs["ssm_lambda_im"], gs["ssm_log_dt"], gs["ssm_b_re"], gs["ssm_b_im"]) = vjp(
        (d_abr, d_abi, _block_diag_in_t(dmr), _block_diag_in_t(dmi)))
    gs["ssm_c_re"] = _block_diag_out_t(dnr)
    gs["ssm_c_im"] = _block_diag_out_t(dni)
    gs["ssm_d"] = ddsk
    gs["ssm_glu_b"] = dgb
    gs["gmlp_w_s"] = dws
    gs["gmlp_b_s"] = dbias.reshape(CHUNK, GMLP_HEADS, GMLP_HEAD_DIM).sum(axis=-1).T
    gs["loss_rows"] = loss_rows

    def small_gather(names):
        return (_gather_small([gs[k].reshape(small_shape[k]) for k in names]), gathered, names) if dist else None

    late = ("ln1_g", "ln1_b")
    launch(exchange(scat=("mix_w_in",), extra=small_gather(tuple(k for k in SMALL + ("loss_rows",) if k not in late))))
    (dx0, dh1, df1, gs["ln1_g"], gs["ln1_b"]), _ = ordered(
        _ffn_bwd, dx1, xh1, rstd1, h1, wb["ffn1_w_in"], wb["ffn1_w_out"], row(ws["ln1_g"]), tm, "ffn1_bwd")
    grad_x = dx0.reshape(bsz, seq, D_MODEL)
    if not dist:
        gb["ffn1_w_out"], _ = _tn_matmul(a1, df1, "dw_ffn1_out", 1408, 1024)
        gb["ffn1_w_in"], _ = _tn_matmul(x0b, dh1, "dw_ffn1_in", 1024, 1408, a_t=True)
        return (loss_rows, grad_x, gb, {k: gs[k].reshape(small_shape[k]) for k in SMALL}, sums, other, gathered,
                None, {})
    launch(exchange(extra=small_gather(late)))
    gb["ffn1_w_out"], _ = ordered(_tn_matmul, a1, df1, "dw_ffn1_out", 1408, 1024)
    last = ["ffn1_w_out"] + [LAST_PIECE % q for q in range(LAST_PIECES)]
    fillers = (("ffn2_w_in", "mix_w_in", "ple_w_gate"),
               ("ffn2_w_out", "mix_w_out", "up_a", "up_b", "ssm_glu_w", "ple_w_proj"))
    out = {}
    for i in range(1, len(last) + 3):
        stage = lambda d: tuple(last[i - d:i - d + 1]) if 0 <= i - d < len(last) else ()
        launch(exchange(halves=stage(1), scat2=stage(2), swap2=stage(3), swap=("mix_w_in",) if i == 2 else ()))
        if i < len(last):
            gb[last[i]], _ = ordered(_tn_matmul, x0b, dh1, "dw_" + last[i], D_MODEL // LAST_PIECES, 1408,
                                     a_cols=(i - 1, 1), a_t=True)
        elif i - len(last) < len(fillers):
            for k in fillers[i - len(last)]:
                w, m, v = opt[k]
                out[k] = _adam_big(w, sums[k], other[k], m, v, "adam_" + k, after=order[0])
                order[0] = out[k][1]
    return loss_rows, grad_x, gb, gs, sums, other, gathered, ids, out


def _adamw(w, g, m, v):
    m = ADAM_B1 * m + (1.0 - ADAM_B1) * g
    v = ADAM_B2 * v + (1.0 - ADAM_B2) * (g * g)
    m_hat = m / ADAM_C1
    v_hat = v / ADAM_C2
    delta = -ADAM_LR * (m_hat / (jnp.sqrt(v_hat) + ADAM_EPS) + ADAM_WD * w)
    return delta, m, v


def _pinned(after):
    return ([pl.BlockSpec(memory_space=pl.ANY)], [after]) if after is not None else ([], [])


def _sum_blocks(part, recv, shape, axis, chip, name, after=None):
    r, c = shape
    rb = r // ROW_STEPS

    def body(chip_ref, p_ref, r_ref, *rest):
        rest[-1][...] = (p_ref[...] + r_ref[0].astype(F32) + r_ref[1].astype(F32) + r_ref[2].astype(F32))

    if axis == 0:
        own = pl.BlockSpec((rb, c), lambda i, k: (k[0] * ROW_STEPS + i, 0))
    else:
        own = pl.BlockSpec((rb, c), lambda i, k: (i, k[0]))
    pin_specs, pin_args = _pinned(after)
    grid_spec = pltpu.PrefetchScalarGridSpec(
        num_scalar_prefetch=1, grid=(ROW_STEPS,),
        in_specs=[own, pl.BlockSpec((3, rb, c), lambda i, k: (0, i, 0))] + pin_specs,
        out_specs=pl.BlockSpec((rb, c), lambda i, k: (i, 0)))
    return pl.pallas_call(body, name=name, out_shape=SDS((r, c), F32), grid_spec=grid_spec,
                          compiler_params=_params(("parallel",)))(chip, part, recv, *pin_args)


def _presum(part, half, shape, axis, ids, name, after=None):
    r, c = shape
    rb = r // 4

    def body(ids_ref, p_ref, h_ref, *rest):
        of_ref, ob_ref = rest[-2:]
        s = p_ref[...] + h_ref[...].astype(F32)
        ob_ref[...] = s.astype(BF16)

        @pl.when(pl.program_id(1) == 0)
        def _():
            of_ref[...] = s

    if axis == 0:
        p_spec = pl.BlockSpec((rb, c), lambda i, t, ids: (ids[t] * 4 + ids[4] * 2 + i, 0))
        h_spec = pl.BlockSpec((None, rb, c), lambda i, t, ids: (ids[t], i, 0))
    else:
        p_spec = pl.BlockSpec((rb, c), lambda i, t, ids: (ids[4] * 2 + i, ids[t]))
        h_spec = pl.BlockSpec((rb, c), lambda i, t, ids: (i, ids[t]))
    pin_specs, pin_args = _pinned(after)
    grid_spec = pltpu.PrefetchScalarGridSpec(
        num_scalar_prefetch=1, grid=(2, 4), in_specs=[p_spec, h_spec] + pin_specs,
        out_specs=(pl.BlockSpec((rb, c), lambda i, t, ids: (i, 0)),
                   pl.BlockSpec((None, rb, c), lambda i, t, ids: (t, i, 0))))
    return pl.pallas_call(body, name=name, out_shape=(SDS((r // 2, c), F32), SDS((4, r // 2, c), BF16)),
                          grid_spec=grid_spec,
                          compiler_params=_params(("parallel", "arbitrary")))(ids, part, half, *pin_args)


def _sum_half(pre, recv, name, after=None):
    hr, c = pre.shape
    rb = hr // 2

    def body(p_ref, r_ref, *rest):
        rest[-1][...] = (p_ref[...] + r_ref[0].astype(F32) + r_ref[1].astype(F32) + r_ref[2].astype(F32))

    spec = pl.BlockSpec((rb, c), lambda i: (i, 0))
    pin_specs, pin_args = _pinned(after)
    return pl.pallas_call(body, name=name, grid=(2,), out_shape=SDS((hr, c), F32),
                          in_specs=[spec, pl.BlockSpec((3, rb, c), lambda i: (0, i, 0))] + pin_specs,
                          out_specs=spec, compiler_params=_params(("parallel",)))(pre, recv, *pin_args)


def _adam_halves(w, mine, oth, m, v, ids, name, piece=0, prev=None):
    r, c = w.shape
    rb = mine.shape[0] // 2

    def body(ids_ref, w_ref, a_ref, b_ref, m_ref, v_ref, *rest):
        g_ref, d_ref, nm_ref, nv_ref = rest[-4:]
        g = jnp.where(pl.program_id(0) // 2 == ids_ref[4], a_ref[...], b_ref[...])
        g_ref[...] = g
        d_ref[...], nm_ref[...], nv_ref[...] = _adamw(w_ref[...], g, m_ref[...], v_ref[...])

    whole = pl.BlockSpec((rb, c), lambda i, ids: (i + 4 * piece, 0))
    part = pl.BlockSpec((rb, c), lambda i, ids: (i % 2, 0))
    in_specs = [whole, part, part, whole, whole]
    args = [w, mine, oth, m, v]
    aliases = {}
    if prev is not None:
        in_specs += [pl.BlockSpec(memory_space=pl.ANY)] * 4
        args += list(prev)
        aliases = {6: 0, 7: 1, 8: 2, 9: 3}
    grid_spec = pltpu.PrefetchScalarGridSpec(num_scalar_prefetch=1, grid=(4,), in_specs=in_specs,
                                             out_specs=(whole,) * 4)
    return pl.pallas_call(body, name=name, out_shape=tuple(SDS((r, c), F32) for _ in range(4)),
                          grid_spec=grid_spec, input_output_aliases=aliases,
                          compiler_params=_params(("parallel",)))(ids, *args)


def _adam_big(w, ga, gb, m, v, name, piece=0, prev=None, after=None):
    r, c = w.shape
    pr = ga.shape[0]
    steps = ROW_STEPS if pr == r else 2
    rb = pr // steps
    off = piece * steps

    def body(w_ref, ga_ref, gb_ref, m_ref, v_ref, *rest):
        g_ref, d_ref, nm_ref, nv_ref = rest[-4:]
        g = ga_ref[...] + gb_ref[...]
        g_ref[...] = g
        d_ref[...], nm_ref[...], nv_ref[...] = _adamw(w_ref[...], g, m_ref[...], v_ref[...])

    whole = pl.BlockSpec((rb, c), lambda i: (i + off, 0))
    part = pl.BlockSpec((rb, c), lambda i: (i, 0))
    in_specs = [whole, part, part, whole, whole]
    args = [w, ga, gb, m, v]
    aliases = {}
    if prev is not None:
        in_specs += [pl.BlockSpec(memory_space=pl.ANY)] * 4
        args += list(prev)
        aliases = {5: 0, 6: 1, 7: 2, 8: 3}
    if after is not None:
        in_specs.append(pl.BlockSpec(memory_space=pl.ANY))
        args.append(after)
    return pl.pallas_call(
        body, name=name, grid=(steps,), out_shape=tuple(SDS((r, c), F32) for _ in range(4)),
        in_specs=in_specs, out_specs=(whole,) * 4, input_output_aliases=aliases,
        compiler_params=_params(("parallel",)),
    )(*args)


def _adam_small(ws, gathered, ms, vs):
    n = len(ws)

    def body(*refs):
        w_refs, g_refs, m_refs, v_refs = refs[:n], refs[n:2 * n], refs[2 * n:3 * n], refs[3 * n:4 * n]
        outs = refs[4 * n:]
        for i in range(n):
            g = g_refs[i][0]
            for d in range(1, N_DEV):
                g = g + g_refs[i][d]
            delta, nm, nv = _adamw(w_refs[i][...], g, m_refs[i][...], v_refs[i][...])
            outs[i][...] = g
            outs[n + i][...] = delta
            outs[2 * n + i][...] = nm
            outs[3 * n + i][...] = nv

    vmem = pl.BlockSpec(memory_space=pltpu.VMEM)
    shapes = [w.shape for w in ws]
    return pl.pallas_call(
        body, name="adam_small", out_shape=tuple(SDS(s, F32) for s in shapes * 4),
        in_specs=[vmem] * (4 * n), out_specs=tuple([vmem] * (4 * n)),
        compiler_params=pltpu.CompilerParams(vmem_limit_bytes=VMEM_LIMIT_BYTES),
    )(*ws, *gathered, *ms, *vs)


def _sum_loss(gathered):
    def body(g_ref, o_ref):
        tot = g_ref[0]
        for d in range(1, N_DEV):
            tot = tot + g_ref[d]
        o_ref[...] = (0.5 / D_MODEL) * jnp.sum(tot, axis=1, keepdims=True)

    vmem = pl.BlockSpec(memory_space=pltpu.VMEM)
    return pl.pallas_call(body, name="sum_loss", out_shape=SDS((1, 1), F32), in_specs=[vmem],
                          out_specs=vmem)(gathered)


def kernel(x, p, ffn1_w_in, ffn1_w_out, ln1_g, ln1_b, mix_w_in, ssm_lambda_re, ssm_lambda_im, ssm_log_dt, ssm_b_re, ssm_b_im, ssm_c_re, ssm_c_im, ssm_d, ssm_glu_w, ssm_glu_b, gmlp_ln_g, gmlp_ln_b, gmlp_w_s, gmlp_b_s, up_a, up_b, mix_w_out, ln2_g, ln2_b, ffn2_w_in, ffn2_w_out, ln3_g, ln3_b, ple_w_proj, ple_w_gate, loss_target, m_ffn1_w_in, m_ffn1_w_out, m_ln1_g, m_ln1_b, m_mix_w_in, m_ssm_lambda_re, m_ssm_lambda_im, m_ssm_log_dt, m_ssm_b_re, m_ssm_b_im, m_ssm_c_re, m_ssm_c_im, m_ssm_d, m_ssm_glu_w, m_ssm_glu_b, m_gmlp_ln_g, m_gmlp_ln_b, m_gmlp_w_s, m_gmlp_b_s, m_up_a, m_up_b, m_mix_w_out, m_ln2_g, m_ln2_b, m_ffn2_w_in, m_ffn2_w_out, m_ln3_g, m_ln3_b, m_ple_w_proj, m_ple_w_gate, v_ffn1_w_in, v_ffn1_w_out, v_ln1_g, v_ln1_b, v_mix_w_in, v_ssm_lambda_re, v_ssm_lambda_im, v_ssm_log_dt, v_ssm_b_re, v_ssm_b_im, v_ssm_c_re, v_ssm_c_im, v_ssm_d, v_ssm_glu_w, v_ssm_glu_b, v_gmlp_ln_g, v_gmlp_ln_b, v_gmlp_w_s, v_gmlp_b_s, v_up_a, v_up_b, v_mix_w_out, v_ln2_g, v_ln2_b, v_ffn2_w_in, v_ffn2_w_out, v_ln3_g, v_ln3_b, v_ple_w_proj, v_ple_w_gate):
    given = dict(locals())
    order = ("ffn1_w_in", "ffn1_w_out", "ln1_g", "ln1_b", "mix_w_in", "ssm_lambda_re", "ssm_lambda_im",
             "ssm_log_dt", "ssm_b_re", "ssm_b_im", "ssm_c_re", "ssm_c_im", "ssm_d", "ssm_glu_w", "ssm_glu_b",
             "gmlp_ln_g", "gmlp_ln_b", "gmlp_w_s", "gmlp_b_s", "up_a", "up_b", "mix_w_out", "ln2_g", "ln2_b",
             "ffn2_w_in", "ffn2_w_out", "ln3_g", "ln3_b", "ple_w_proj", "ple_w_gate")
    assert set(order) == set(BIG + SMALL)

    shard = {k: given[k][0] for k in BIG}
    shard_b = {k: shard[k].astype(BF16) for k in BIG}
    opt = {k: (shard[k], given["m_" + k][0], given["v_" + k][0]) for k in BIG}
    loss_rows, grad_x, gb, gs, sums, other, gathered, ids, out = _local_step(
        x, given["p"][0], loss_target, {}, {k: given[k] for k in SMALL}, shard_b, opt)

    out = dict(out)
    for k in BIG:
        if k in out:
            continue
        moments = (given["m_" + k][0], given["v_" + k][0])
        if k == "ffn1_w_out":
            out[k] = _adam_halves(shard[k], sums[k], other[k], *moments, ids, "adam_" + k)
        elif k == "ffn1_w_in":
            for q in range(LAST_PIECES):
                kq = LAST_PIECE % q
                out[k] = _adam_halves(shard[k], sums[kq], other[kq], *moments, ids, "adam_" + kq, q, out.get(k))
        else:
            out[k] = _adam_big(shard[k], sums[k], other[k], *moments, "adam_" + k,
                               after=gb[LAST_PIECE % (LAST_PIECES - 1)][0])

    res = _adam_small([_small_view(k, given[k]) for k in SMALL], [gathered[k] for k in SMALL],
                      [_small_view(k, given["m_" + k]) for k in SMALL],
                      [_small_view(k, given["v_" + k]) for k in SMALL])
    ns = len(SMALL)
    for i, k in enumerate(SMALL):
        out[k] = tuple(res[j * ns + i].reshape(given[k].shape) for j in range(4))
    loss = _sum_loss(gathered["loss_rows"]).reshape(())

    lead = lambda k, j: out[k][j][None] if k in BIG else out[k][j]
    return (loss, grad_x, *[lead(k, 0) for k in order], *[lead(k, 1) for k in order],
            *[lead(k, 2) for k in order], *[lead(k, 3) for k in order])
```

```python
import math

import jax
import jax.numpy as jnp
from jax import lax
from jax.experimental import pallas as pl
from jax.experimental.pallas import tpu as pltpu
from jax.experimental.pallas import tpu_sc as plsc

F32 = jnp.float32
BF16 = jnp.bfloat16
MESH = pl.DeviceIdType.MESH
SDS = jax.ShapeDtypeStruct

D_MODEL = 1024
D_FF = 2816
D_SSM = 512
D_GMLP = 512
SSM_GROUPS = 32
SSM_GROUP_CH = 16
SSM_STATE = 64
SSM_LANES = SSM_GROUPS * SSM_STATE
GMLP_HEADS = 8
GMLP_HEAD_DIM = 64
CHUNK = 128
PLE_DIM = 256
LN_EPS = 1e-5
ALPHA = 2.0 ** 0.25

ADAM_LR = 0.001
ADAM_B1 = 0.9
ADAM_B2 = 0.999
ADAM_EPS = 1e-08
ADAM_WD = 0.01
ADAM_STEP = 10
ADAM_C1 = 1.0 - ADAM_B1 ** ADAM_STEP
ADAM_C2 = 1.0 - ADAM_B2 ** ADAM_STEP

N_DEV = 8
VMEM_LIMIT_BYTES = 56 * 1024 * 1024
FFN_COLS = 1408
S5_BLOCKS = 4
S5_BLOCK_IN = D_SSM // S5_BLOCKS
S5_BLOCK_ST = SSM_LANES // S5_BLOCKS
SCAN_LANES = 512
TN_K_BLOCK = 2048
TN_SMALL_BLOCK = 1024 * 1024
GMLP_CHUNKS = 2
ROW_STEPS = 4
DIRECT_GATHER_BYTES = 0
LAST_PIECES = 2
LAST_PIECE = "ffn1_w_in_q%d"
_G0 = math.sqrt(2.0 / math.pi)
_G1 = 0.044715


def _dot(a, b):
    return jnp.dot(a, b, preferred_element_type=F32)


def _dot_nt(a, b):
    return lax.dot_general(a, b, (((1,), (1,)), ((), ())), preferred_element_type=F32)


def _dot_tn(a, b):
    return lax.dot_general(a, b, (((0,), (0,)), ((), ())), preferred_element_type=F32)


def _sigmoid(x):
    return 1.0 / (1.0 + jnp.exp(-x))


def _gelu(x):
    t = jnp.tanh(_G0 * (x + _G1 * x * x * x))
    return 0.5 * x * (1.0 + t)


def _gelu_grad(x):
    t = jnp.tanh(_G0 * (x + _G1 * x * x * x))
    return 0.5 * (1.0 + t) + 0.5 * x * (1.0 - t * t) * _G0 * (1.0 + 3.0 * _G1 * x * x)


def _ln_fwd(r, g, b):
    mu = jnp.mean(r, axis=-1, keepdims=True)
    d = r - mu
    var = jnp.mean(d * d, axis=-1, keepdims=True)
    rstd = lax.rsqrt(var + LN_EPS)
    xh = d * rstd
    return xh * g + b, xh, rstd


def _ln_bwd(dy, xh, rstd, g):
    dxh = dy * g
    m1 = jnp.mean(dxh, axis=-1, keepdims=True)
    m2 = jnp.mean(dxh * xh, axis=-1, keepdims=True)
    return rstd * (dxh - m1 - xh * m2)


def _resident(shape):
    nd = len(shape)
    return pl.BlockSpec(shape, lambda *_: (0,) * nd, pipeline_mode=pl.Buffered(1))


def _fixed(shape):
    nd = len(shape)
    return pl.BlockSpec(shape, lambda *_: (0,) * nd)


def _rows(tm, cols):
    return pl.BlockSpec((tm, cols), lambda i: (i, 0))


def _cols(rows, tm):
    return pl.BlockSpec((rows, tm), lambda i: (0, i))


def _params(sem):
    return pltpu.CompilerParams(dimension_semantics=sem, vmem_limit_bytes=VMEM_LIMIT_BYTES)


class _Exchange:
    def __init__(self, args, out_shape, sems, start, finish):
        self.args, self.out_shape, self.sems = list(args), list(out_shape), list(sems)
        self.start, self.finish = start, finish
        self.cuts = [(0, len(self.out_shape))]


def _call(body, name, grid, in_specs, out_specs, out_shape, args, scratch=(), sem=None, bg=None, aliases=None):
    aliases = {} if aliases is None else aliases
    in_specs, args = list(in_specs), list(args)
    fn = body
    if bg is not None:
        n_args = len(args)

        def fn(*refs):
            body(*refs[:n_args], *refs[n_args + 1:])

        in_specs.append(pl.BlockSpec(memory_space=pl.ANY))
        args.append(bg)
    res = pl.pallas_call(fn, name=name, grid=grid, out_shape=tuple(out_shape), in_specs=in_specs,
                         out_specs=tuple(out_specs), scratch_shapes=list(scratch),
                         input_output_aliases=aliases, compiler_params=_params(sem))(*args)
    return tuple(res), ()


def _run_exchange_on_sequencer(ex, name, collective_id):
    n_i, n_o = len(ex.args), len(ex.out_shape)

    def body(*refs):
        ins, outs, sems = refs[:n_i], refs[n_i:n_i + n_o], refs[n_i + n_o:]
        x, y, c = lax.axis_index("x"), lax.axis_index("y"), lax.axis_index("c")
        barrier = pltpu.get_barrier_semaphore()
        for peer in [(x, y, 1 - c), (1 - x, y, c), (x, 1 - y, c), (1 - x, 1 - y, c)]:
            pl.semaphore_signal(barrier, inc=1, device_id=peer, device_id_type=MESH)
        pl.semaphore_wait(barrier, 4)
        ex.start(ins, outs, sems)
        ex.finish(ins, outs, sems)

    return tuple(pl.kernel(body, out_type=tuple(ex.out_shape),
                           mesh=plsc.ScalarSubcoreMesh(axis_name="sequencer", num_cores=1),
                           scratch_types=list(ex.sems), name=name,
                           compiler_params=pltpu.CompilerParams(collective_id=collective_id))(*ex.args))


def _join(exchanges):
    cuts = []
    a = o = q = 0
    for e in exchanges:
        cuts.append((a, a + len(e.args), o, o + len(e.out_shape), q, q + len(e.sems)))
        a, o, q = cuts[-1][1], cuts[-1][3], cuts[-1][5]

    def start(ins, outs, sems):
        for e, (a0, a1, o0, o1, q0, q1) in zip(exchanges, cuts):
            e.start(ins[a0:a1], outs[o0:o1], sems[q0:q1])

    def finish(ins, outs, sems):
        for e, (a0, a1, o0, o1, q0, q1) in zip(exchanges, cuts):
            e.finish(ins[a0:a1], outs[o0:o1], sems[q0:q1])

    joined = _Exchange(sum((e.args for e in exchanges), []), sum((e.out_shape for e in exchanges), []),
                       sum((e.sems for e in exchanges), []), start, finish)
    joined.cuts = [(c[2], c[3]) for c in cuts]
    return joined


def _ffn_proj(x, w_in, tm, name, bg=None):
    t = x.shape[0]
    nch = D_FF // FFN_COLS

    def body(x_ref, win_ref, xbt_ref, h_ref, a_ref):
        xb = x_ref[...].astype(BF16)
        xbt_ref[...] = xb.T
        for k in range(nch):
            cg = slice(k * FFN_COLS, (k + 1) * FFN_COLS)
            cu = slice(D_FF + k * FFN_COLS, D_FF + (k + 1) * FFN_COLS)
            hg = _dot(xb, win_ref[k])
            hu = _dot(xb, win_ref[nch + k])
            h_ref[:, cg] = hg.astype(BF16)
            h_ref[:, cu] = hu.astype(BF16)
            a_ref[:, cg] = (hg * _sigmoid(hg) * hu).astype(BF16)

    return _call(
        body, name, (t // tm,),
        [_rows(tm, D_MODEL), _resident((2 * nch, D_MODEL, FFN_COLS))],
        (_cols(D_MODEL, tm), _rows(tm, 2 * D_FF), _rows(tm, D_FF)),
        (SDS((D_MODEL, t), BF16), SDS((t, 2 * D_FF), BF16), SDS((t, D_FF), BF16)),
        (x, w_in), sem=("parallel",), bg=bg)


def _ffn_out(x, a, w_out, g, b, tm, name, bg=None):
    t = x.shape[0]

    def body(x_ref, a_ref, wout_ref, g_ref, b_ref, xn_ref, xh_ref, rstd_ref):
        f = _dot(a_ref[...], wout_ref[...])
        y, xh, rstd = _ln_fwd(ALPHA * x_ref[...] + 0.5 * f, g_ref[...], b_ref[...])
        xn_ref[...] = y
        xh_ref[...] = xh
        rstd_ref[...] = rstd

    return _call(
        body, name, (t // tm,),
        [_rows(tm, D_MODEL), _rows(tm, D_FF), _resident((D_FF, D_MODEL)), _fixed((1, D_MODEL)), _fixed((1, D_MODEL))],
        (_rows(tm, D_MODEL), _rows(tm, D_MODEL), _rows(tm, 1)),
        (SDS((t, D_MODEL), F32), SDS((t, D_MODEL), F32), SDS((t, 1), F32)),
        (x, a, w_out, g, b), sem=("parallel",), bg=bg)


def _ffn_bwd(dxn, xh, rstd, h, w_in, w_out, g, tm, name, bg=None):
    t = dxn.shape[0]
    nch = D_FF // FFN_COLS

    def body(dxn_ref, xh_ref, rstd_ref, h_ref, win_ref, wout_ref, g_ref,
             dx_ref, dh_ref, df_ref, dg_ref, db_ref):
        @pl.when(pl.program_id(0) == 0)
        def _():
            dg_ref[...] = jnp.zeros_like(dg_ref)
            db_ref[...] = jnp.zeros_like(db_ref)

        dy = dxn_ref[...]
        xhv = xh_ref[...]
        dr = _ln_bwd(dy, xhv, rstd_ref[...], g_ref[...])
        dg_ref[...] += jnp.sum(dy * xhv, axis=0, keepdims=True)
        db_ref[...] += jnp.sum(dy, axis=0, keepdims=True)
        df = (0.5 * dr).astype(BF16)
        df_ref[...] = df
        dx = ALPHA * dr
        das = [_dot_nt(df, wout_ref[k * FFN_COLS:(k + 1) * FFN_COLS, :]) for k in range(nch)]
        for k in range(nch):
            cg = slice(k * FFN_COLS, (k + 1) * FFN_COLS)
            cu = slice(D_FF + k * FFN_COLS, D_FF + (k + 1) * FFN_COLS)
            hg = h_ref[:, cg].astype(F32)
            hu = h_ref[:, cu].astype(F32)
            sg = _sigmoid(hg)
            silu = hg * sg
            da = das[k]
            dhu = (da * silu).astype(BF16)
            dhg = (da * hu * (sg * (1.0 + hg * (1.0 - sg)))).astype(BF16)
            dh_ref[:, cg] = dhg
            dh_ref[:, cu] = dhu
            dx = dx + _dot_nt(dhg, win_ref[k]) + _dot_nt(dhu, win_ref[nch + k])
        dx_ref[...] = dx

    return _call(
        body, name, (t // tm,),
        [_rows(tm, D_MODEL), _rows(tm, D_MODEL), _rows(tm, 1), _rows(tm, 2 * D_FF),
         _resident((2 * nch, D_MODEL, FFN_COLS)), _resident((D_FF, D_MODEL)), _fixed((1, D_MODEL))],
        (_rows(tm, D_MODEL), _rows(tm, 2 * D_FF), _rows(tm, D_MODEL),
         _fixed((1, D_MODEL)), _fixed((1, D_MODEL))),
        (SDS((t, D_MODEL), F32), SDS((t, 2 * D_FF), BF16), SDS((t, D_MODEL), BF16),
         SDS((1, D_MODEL), F32), SDS((1, D_MODEL), F32)),
        (dxn, xh, rstd, h, w_in, w_out, g), sem=("arbitrary",), bg=bg)


def _tn_matmul(a, b, name, bm, bn, col_block=0, total_cols=None, prev=None, bg=None, a_cols=None, a_t=False):
    t, m = a.shape[::-1] if a_t else a.shape
    a_first = 0
    if a_cols is not None:
        a_first, m = a_cols[0], a_cols[1] * bm
    n = b.shape[1]
    total_cols = n if total_cols is None else total_cols
    bk = min(TN_K_BLOCK if bm * bn > TN_SMALL_BLOCK else 2 * TN_K_BLOCK, t)
    nk = t // bk
    n_in = 2 if prev is None else 4

    def body(*refs):
        a_ref, b_ref = refs[0], refs[1]
        o_ref, ob_ref = refs[n_in], refs[n_in + 1]
        k = pl.program_id(2)

        @pl.when(k == 0)
        def _():
            o_ref[...] = jnp.zeros_like(o_ref)

        o_ref[...] += _dot(a_ref[...], b_ref[...]) if a_t else _dot_tn(a_ref[...], b_ref[...])

        @pl.when(k == nk - 1)
        def _():
            ob_ref[...] = o_ref[...].astype(BF16)

    a_spec = (pl.BlockSpec((bm, bk), lambda i, j, k: (i + a_first, k)) if a_t
              else pl.BlockSpec((bk, bm), lambda i, j, k: (k, i + a_first)))
    in_specs = [a_spec, pl.BlockSpec((bk, bn), lambda i, j, k: (k, j))]
    args = [a, b]
    aliases = {}
    if prev is not None:
        in_specs += [pl.BlockSpec(memory_space=pl.ANY), pl.BlockSpec(memory_space=pl.ANY)]
        args += list(prev)
        aliases = {2: 0, 3: 1}
        if any(bg is p for p in prev):
            bg = None
    out_spec = pl.BlockSpec((bm, bn), lambda i, j, k: (i, j + col_block))
    return _call(body, name, (m // bm, n // bn, nk), in_specs, (out_spec, out_spec),
                 (SDS((m, total_cols), F32), SDS((m, total_cols), BF16)), args,
                 sem=("parallel", "parallel", "arbitrary"), bg=bg, aliases=aliases)


def _mixin_fwd(x1, w, tm, bg=None):
    t = x1.shape[0]

    def body(x_ref, w_ref, xbt_ref, za_ref, zuv_ref, gab_ref):
        xb = x_ref[...].astype(BF16)
        xbt_ref[...] = xb.T
        za_ref[...] = _dot(xb, w_ref[:, 0:512]).astype(BF16)
        zuv_ref[...] = _dot(xb, w_ref[:, 512:1536]).astype(BF16)
        gab_ref[...] = _dot(xb, w_ref[:, 1536:3584]).astype(BF16)

    return _call(
        body, "mixin_fwd", (t // tm,),
        [_rows(tm, D_MODEL), _resident((D_MODEL, 3584))],
        (_cols(D_MODEL, tm), _rows(tm, 512), _rows(tm, 1024), _rows(tm, 2048)),
        (SDS((D_MODEL, t), BF16), SDS((t, 512), BF16), SDS((t, 1024), BF16), SDS((t, 2048), BF16)),
        (x1, w), sem=("parallel",), bg=bg)


def _mixin_bwd(dx1a, dza, dzuv, dgab, w, tm, bg=None):
    t = dx1a.shape[0]

    def body(d_ref, dza_ref, dzuv_ref, dgab_ref, w_ref, dx_ref):
        dx_ref[...] = (d_ref[...] + _dot_nt(dza_ref[...], w_ref[:, 0:512])
                       + _dot_nt(dzuv_ref[...], w_ref[:, 512:1536])
                       + _dot_nt(dgab_ref[...], w_ref[:, 1536:3584]))

    return _call(
        body, "mixin_bwd", (t // tm,),
        [_rows(tm, D_MODEL), _rows(tm, 512), _rows(tm, 1024), _rows(tm, 2048), _resident((D_MODEL, 3584))],
        (_rows(tm, D_MODEL),), (SDS((t, D_MODEL), F32),),
        (dx1a, dza, dzuv, dgab, w), sem=("parallel",), bg=bg)


def _unrolled(lo, hi, body, carry):
    for j in range(lo, hi):
        carry = body(j, carry)
    return carry


def _scan_fwd(hr_ref, hi_ref, a_ref, ap_ref, carry_ref, seg, cin_ref):
    for lc in range(SSM_LANES // SCAN_LANES):
        ls = slice(lc * SCAN_LANES, (lc + 1) * SCAN_LANES)
        a_r = jnp.broadcast_to(a_ref[0:1, ls], (8, SCAN_LANES))
        a_i = jnp.broadcast_to(a_ref[1:2, ls], (8, SCAN_LANES))

        def step(j, hc, ls=ls, a_r=a_r, a_i=a_i):
            h_r, h_i = hc
            rows = pl.ds(j * 8, 8)
            n_r = a_r * h_r - a_i * h_i + hr_ref[rows, ls]
            n_i = a_r * h_i + a_i * h_r + hi_ref[rows, ls]
            hr_ref[rows, ls] = n_r
            hi_ref[rows, ls] = n_i
            return n_r, n_i

        zero = jnp.zeros((8, SCAN_LANES), F32)
        f_r, f_i = _unrolled(0, seg, step, (zero, zero))
        c_r = carry_ref[0:1, ls]
        c_i = carry_ref[1:2, ls]
        p_r = ap_ref[0:1, ls]
        p_i = ap_ref[1:2, ls]
        rows_r, rows_i = [], []
        for s in range(8):
            rows_r.append(c_r)
            rows_i.append(c_i)
            c_r, c_i = (f_r[s:s + 1] + p_r * c_r - p_i * c_i,
                        f_i[s:s + 1] + p_r * c_i + p_i * c_r)
        carry_ref[0:1, ls] = c_r
        carry_ref[1:2, ls] = c_i
        cin_r = jnp.concatenate(rows_r, axis=0)
        cin_i = jnp.concatenate(rows_i, axis=0)
        if cin_ref is not None:
            cin_ref[0, :, ls] = cin_r
            cin_ref[1, :, ls] = cin_i

        def fix(j, cc, ls=ls, a_r=a_r, a_i=a_i):
            c_r, c_i = cc
            c_r, c_i = a_r * c_r - a_i * c_i, a_r * c_i + a_i * c_r
            rows = pl.ds(j * 8, 8)
            hr_ref[rows, ls] = hr_ref[rows, ls] + c_r
            hi_ref[rows, ls] = hi_ref[rows, ls] + c_i
            return c_r, c_i

        _unrolled(0, seg, fix, (cin_r, cin_i))


def _scan_bwd(gr_ref, gi_ref, hr_ref, hi_ref, cin_ref, a_ref, ap_ref, rcarry_ref, da_ref, seg):
    for lc in range(SSM_LANES // SCAN_LANES):
        ls = slice(lc * SCAN_LANES, (lc + 1) * SCAN_LANES)
        a_r = jnp.broadcast_to(a_ref[0:1, ls], (8, SCAN_LANES))
        a_i = jnp.broadcast_to(a_ref[1:2, ls], (8, SCAN_LANES))

        def step(t, gc, ls=ls, a_r=a_r, a_i=a_i):
            g_r, g_i = gc
            rows = pl.ds((seg - 1 - t) * 8, 8)
            n_r = gr_ref[rows, ls] + a_r * g_r + a_i * g_i
            n_i = gi_ref[rows, ls] + a_r * g_i - a_i * g_r
            gr_ref[rows, ls] = n_r
            gi_ref[rows, ls] = n_i
            return n_r, n_i

        zero = jnp.zeros((8, SCAN_LANES), F32)
        f_r, f_i = _unrolled(0, seg, step, (zero, zero))
        c_r = rcarry_ref[0:1, ls]
        c_i = rcarry_ref[1:2, ls]
        p_r = ap_ref[0:1, ls]
        p_i = ap_ref[1:2, ls]
        rows_r, rows_i = [None] * 8, [None] * 8
        for s in range(7, -1, -1):
            rows_r[s] = c_r
            rows_i[s] = c_i
            c_r, c_i = (f_r[s:s + 1] + p_r * c_r + p_i * c_i,
                        f_i[s:s + 1] + p_r * c_i - p_i * c_r)
        rcarry_ref[0:1, ls] = c_r
        rcarry_ref[1:2, ls] = c_i
        cin_r = jnp.concatenate(rows_r, axis=0)
        cin_i = jnp.concatenate(rows_i, axis=0)

        def fix_row(j_rows, hp_r, hp_i, cc, ls=ls, a_r=a_r, a_i=a_i):
            c_r, c_i, acc_r, acc_i = cc
            c_r, c_i = a_r * c_r + a_i * c_i, a_r * c_i - a_i * c_r
            g_r = gr_ref[j_rows, ls] + c_r
            g_i = gi_ref[j_rows, ls] + c_i
            gr_ref[j_rows, ls] = g_r
            gi_ref[j_rows, ls] = g_i
            acc_r = acc_r + g_r * hp_r + g_i * hp_i
            acc_i = acc_i + g_i * hp_r - g_r * hp_i
            return c_r, c_i, acc_r, acc_i

        def fix(t, cc, ls=ls, fix_row=fix_row):
            j = seg - 1 - t
            rows = pl.ds(j * 8, 8)
            prev = pl.ds((j - 1) * 8, 8)
            return fix_row(rows, hr_ref[prev, ls], hi_ref[prev, ls], cc)

        cc = _unrolled(0, seg - 1, fix, (cin_r, cin_i, zero, zero))
        _, _, acc_r, acc_i = fix_row(pl.ds(0, 8), cin_ref[0, :, ls], cin_ref[1, :, ls], cc)
        da_ref[0, :, ls] += acc_r
        da_ref[1, :, ls] += acc_i


def _s5_fwd(za, sp, bsz, seq, tb, bg=None):
    nb = seq // tb
    seg = tb // 8
    t = bsz * seq

    def body(za_ref, perm_ref, permt_ref, mre_ref, mim_ref, nre_ref, nim_ref, a_ref, ap_ref,
             dsk_ref, gw_ref, gb_ref, out_ref, outt_ref, y2_ref, car_ref, hr_ref, hi_ref, carry_ref):
        @pl.when(pl.program_id(1) == 0)
        def _():
            carry_ref[...] = jnp.zeros_like(carry_ref)

        car_ref[0] = carry_ref[...]
        up = _dot(perm_ref[...], za_ref[...])
        upb = up.astype(BF16)
        for bb in range(S5_BLOCKS):
            ub = upb[:, bb * S5_BLOCK_IN:(bb + 1) * S5_BLOCK_IN]
            st = slice(bb * S5_BLOCK_ST, (bb + 1) * S5_BLOCK_ST)
            hr_ref[:, st] = _dot(ub, mre_ref[bb])
            hi_ref[:, st] = _dot(ub, mim_ref[bb])
        _scan_fwd(hr_ref, hi_ref, a_ref, ap_ref, carry_ref, seg, None)
        ys = []
        for bb in range(S5_BLOCKS):
            st = slice(bb * S5_BLOCK_ST, (bb + 1) * S5_BLOCK_ST)
            ys.append(_dot(hr_ref[:, st].astype(BF16), nre_ref[bb])
                      - _dot(hi_ref[:, st].astype(BF16), nim_ref[bb]))
        y2 = jnp.concatenate(ys, axis=1) + dsk_ref[...] * up
        y2_ref[...] = y2
        y3 = _gelu(y2)
        gl = _dot(y3.astype(BF16), gw_ref[...]) + gb_ref[...]
        oa = y3 * _sigmoid(gl)
        out = _dot(permt_ref[...], oa.astype(BF16)).astype(BF16)
        out_ref[...] = out
        outt_ref[...] = out.T

    blk = pl.BlockSpec((tb, D_SSM), lambda b, j: (b * nb + j, 0))
    blk_t = pl.BlockSpec((D_SSM, tb), lambda b, j: (0, b * nb + j))
    m_shape = (S5_BLOCKS, S5_BLOCK_IN, S5_BLOCK_ST)
    n_shape = (S5_BLOCKS, S5_BLOCK_ST, S5_BLOCK_IN)
    return _call(
        body, "s5_fwd", (bsz, nb),
        [blk, _fixed((tb, tb)), _fixed((tb, tb)), _fixed(m_shape), _fixed(m_shape), _fixed(n_shape),
         _fixed(n_shape), _fixed((2, SSM_LANES)), _fixed((2, SSM_LANES)), _fixed((1, D_SSM)),
         _fixed((D_SSM, D_SSM)), _fixed((1, D_SSM))],
        (blk, blk_t, blk, pl.BlockSpec((1, 2, SSM_LANES), lambda b, j: (b * nb + j, 0, 0))),
        (SDS((t, D_SSM), BF16), SDS((D_SSM, t), BF16), SDS((t, D_SSM), F32), SDS((bsz * nb, 2, SSM_LANES), F32)),
        (za, sp["perm"], sp["permt"], sp["mre"], sp["mim"], sp["nre"], sp["nim"], sp["a"], sp["ap"],
         sp["dskip"], sp["glu_w"], sp["glu_b"]),
        scratch=[pltpu.VMEM((tb, SSM_LANES), F32), pltpu.VMEM((tb, SSM_LANES), F32),
                 pltpu.VMEM((2, SSM_LANES), F32)],
        sem=("arbitrary", "arbitrary"), bg=bg)


def _s5_bwd(za, y2p, doa, carries, sp, bsz, seq, tb, bg=None):
    nb = seq // tb
    seg = tb // 8
    t = bsz * seq

    def body(za_ref, y2_ref, doa_ref, car_ref, perm_ref, permt_ref, mre_ref, mim_ref, mtre_ref, mtim_ref,
             nre_ref, nim_ref, ntre_ref, ntim_ref, a_ref, ap_ref, dsk_ref, gw_ref, gwt_ref, gb_ref,
             dza_ref, dmr_ref, dmi_ref, dnr_ref, dni_ref, da_ref, ddsk_ref, dgw_ref, dgb_ref,
             hr_ref, hi_ref, gr_ref, gi_ref, cin_ref, carry_ref, rcarry_ref):
        first = jnp.logical_and(pl.program_id(0) == 0, pl.program_id(1) == 0)

        @pl.when(first)
        def _():
            for r in (dmr_ref, dmi_ref, dnr_ref, dni_ref, da_ref, ddsk_ref, dgw_ref, dgb_ref):
                r[...] = jnp.zeros_like(r)

        @pl.when(pl.program_id(1) == 0)
        def _():
            rcarry_ref[...] = jnp.zeros_like(rcarry_ref)

        carry_ref[...] = car_ref[0]
        perm = perm_ref[...]
        up = _dot(perm, za_ref[...])
        upb = up.astype(BF16)
        for bb in range(S5_BLOCKS):
            ub = upb[:, bb * S5_BLOCK_IN:(bb + 1) * S5_BLOCK_IN]
            st = slice(bb * S5_BLOCK_ST, (bb + 1) * S5_BLOCK_ST)
            hr_ref[:, st] = _dot(ub, mre_ref[bb])
            hi_ref[:, st] = _dot(ub, mim_ref[bb])
        _scan_fwd(hr_ref, hi_ref, a_ref, ap_ref, carry_ref, seg, cin_ref)

        y2 = y2_ref[...]
        y3 = _gelu(y2)
        y3b = y3.astype(BF16)
        sg = _sigmoid(_dot(y3b, gw_ref[...]) + gb_ref[...])
        d0 = doa_ref[...]
        d_hi = d0.astype(BF16)
        d1 = d0 - d_hi.astype(F32)
        d_mid = d1.astype(BF16)
        d_lo = (d1 - d_mid.astype(F32)).astype(BF16)
        doap = _dot(perm, d_hi) + _dot(perm, d_mid) + _dot(perm, d_lo)
        dgl = doap * y3 * sg * (1.0 - sg)
        dglb = dgl.astype(BF16)
        dy3 = doap * sg + _dot(dglb, gwt_ref[...])
        dgw_ref[...] += _dot_tn(y3b, dglb)
        dgb_ref[...] += jnp.sum(dgl, axis=0, keepdims=True)
        dy2 = dy3 * _gelu_grad(y2)
        ddsk_ref[...] += jnp.sum(dy2 * up, axis=0, keepdims=True)
        dyb = dy2.astype(BF16)
        for bb in range(S5_BLOCKS):
            dyc = dyb[:, bb * S5_BLOCK_IN:(bb + 1) * S5_BLOCK_IN]
            st = slice(bb * S5_BLOCK_ST, (bb + 1) * S5_BLOCK_ST)
            gr_ref[:, st] = _dot(dyc, ntre_ref[bb])
            gi_ref[:, st] = -_dot(dyc, ntim_ref[bb])
            dnr_ref[bb] += _dot_tn(hr_ref[:, st].astype(BF16), dyc)
            dni_ref[bb] += -_dot_tn(hi_ref[:, st].astype(BF16), dyc)
        _scan_bwd(gr_ref, gi_ref, hr_ref, hi_ref, cin_ref, a_ref, ap_ref, rcarry_ref, da_ref, seg)
        dus = []
        for bb in range(S5_BLOCKS):
            st = slice(bb * S5_BLOCK_ST, (bb + 1) * S5_BLOCK_ST)
            grb = gr_ref[:, st].astype(BF16)
            gib = gi_ref[:, st].astype(BF16)
            dus.append(_dot(grb, mtre_ref[bb]) + _dot(gib, mtim_ref[bb]))
            ub = upb[:, bb * S5_BLOCK_IN:(bb + 1) * S5_BLOCK_IN]
            dmr_ref[bb] += _dot_tn(ub, grb)
            dmi_ref[bb] += _dot_tn(ub, gib)
        du = jnp.concatenate(dus, axis=1) + dy2 * dsk_ref[...]
        dza_ref[...] = _dot(permt_ref[...], du.astype(BF16)).astype(BF16)

    def rev(b, j):
        return (b * nb + (nb - 1 - j), 0)

    blk = pl.BlockSpec((tb, D_SSM), rev)
    m_shape = (S5_BLOCKS, S5_BLOCK_IN, S5_BLOCK_ST)
    n_shape = (S5_BLOCKS, S5_BLOCK_ST, S5_BLOCK_IN)
    return _call(
        body, "s5_bwd", (bsz, nb),
        [blk, blk, blk, pl.BlockSpec((1, 2, SSM_LANES), lambda b, j: (b * nb + (nb - 1 - j), 0, 0)),
         _fixed((tb, tb)), _fixed((tb, tb)), _fixed(m_shape), _fixed(m_shape), _fixed(n_shape), _fixed(n_shape),
         _fixed(n_shape), _fixed(n_shape), _fixed(m_shape), _fixed(m_shape),
         _fixed((2, SSM_LANES)), _fixed((2, SSM_LANES)), _fixed((1, D_SSM)),
         _fixed((D_SSM, D_SSM)), _fixed((D_SSM, D_SSM)), _fixed((1, D_SSM))],
        (blk, _fixed(m_shape), _fixed(m_shape), _fixed(n_shape), _fixed(n_shape),
         _fixed((2, 8, SSM_LANES)), _fixed((1, D_SSM)), _fixed((D_SSM, D_SSM)), _fixed((1, D_SSM))),
        (SDS((t, D_SSM), BF16), SDS(m_shape, F32), SDS(m_shape, F32), SDS(n_shape, F32), SDS(n_shape, F32),
         SDS((2, 8, SSM_LANES), F32), SDS((1, D_SSM), F32), SDS((D_SSM, D_SSM), F32), SDS((1, D_SSM), F32)),
        (za, y2p, doa, carries, sp["perm"], sp["permt"], sp["mre"], sp["mim"], sp["mtre"], sp["mtim"],
         sp["nre"], sp["nim"], sp["ntre"], sp["ntim"], sp["a"], sp["ap"], sp["dskip"], sp["glu_w"],
         sp["glu_wt"], sp["glu_b"]),
        scratch=[pltpu.VMEM((tb, SSM_LANES), F32), pltpu.VMEM((tb, SSM_LANES), F32),
                 pltpu.VMEM((tb, SSM_LANES), F32), pltpu.VMEM((tb, SSM_LANES), F32),
                 pltpu.VMEM((2, 8, SSM_LANES), F32), pltpu.VMEM((2, SSM_LANES), F32),
                 pltpu.VMEM((2, SSM_LANES), F32)],
        sem=("arbitrary", "arbitrary"), bg=bg)


def _gmlp_spatial(ws_ref, vb):
    lane = lax.broadcasted_iota(jnp.int32, (CHUNK, 128), 1)
    parts = []
    for j in range(GMLP_HEADS // 2):
        vp = vb[:, 128 * j:128 * (j + 1)]
        parts.append(jnp.where(lane < GMLP_HEAD_DIM, _dot(ws_ref[2 * j], vp), _dot(ws_ref[2 * j + 1], vp)))
    return jnp.concatenate(parts, axis=1)


def _gmlp_fwd(zuv, ln_g, ln_b, wsm, bias, bg=None):
    t = zuv.shape[0]

    def body(z_ref, g_ref, b_ref, ws_ref, bias_ref, out_ref, outt_ref):
        for ch in range(GMLP_CHUNKS):
            rows = slice(ch * CHUNK, (ch + 1) * CHUNK)
            u = _gelu(z_ref[rows, 0:D_GMLP].astype(F32))
            v0 = _gelu(z_ref[rows, D_GMLP:2 * D_GMLP].astype(F32))
            v, _, _ = _ln_fwd(v0, g_ref[...], b_ref[...])
            s = _gmlp_spatial(ws_ref, v.astype(BF16)) + bias_ref[...]
            out = (u * s).astype(BF16)
            out_ref[rows, :] = out
            outt_ref[:, rows] = out.T

    step = GMLP_CHUNKS * CHUNK
    return _call(
        body, "gmlp_fwd", (t // step,),
        [_rows(step, 2 * D_GMLP), _fixed((1, D_GMLP)), _fixed((1, D_GMLP)),
         _fixed((GMLP_HEADS, CHUNK, CHUNK)), _fixed((CHUNK, D_GMLP))],
        (_rows(step, D_GMLP), _cols(D_GMLP, step)), (SDS((t, D_GMLP), BF16), SDS((D_GMLP, t), BF16)),
        (zuv, ln_g, ln_b, wsm, bias), sem=("parallel",), bg=bg)


def _gmlp_bwd(zuv, dgm, ln_g, ln_b, wsm, wsmt, bias, bg=None):
    t = zuv.shape[0]

    def body(z_ref, d_ref, g_ref, b_ref, ws_ref, wst_ref, bias_ref,
             dz_ref, dws_ref, dbias_ref, dg_ref, db_ref):
        @pl.when(pl.program_id(0) == 0)
        def _():
            for r in (dws_ref, dbias_ref, dg_ref, db_ref):
                r[...] = jnp.zeros_like(r)

        zu = z_ref[:, 0:D_GMLP].astype(F32)
        zv = z_ref[:, D_GMLP:2 * D_GMLP].astype(F32)
        u = _gelu(zu)
        v0 = _gelu(zv)
        gam = g_ref[...]
        v, vhat, rstd = _ln_fwd(v0, gam, b_ref[...])
        vb = v.astype(BF16)
        s = _gmlp_spatial(ws_ref, vb) + bias_ref[...]
        d = d_ref[...]
        dz_ref[:, 0:D_GMLP] = (d * s * _gelu_grad(zu)).astype(BF16)
        ds = d * u
        dbias_ref[...] += ds
        dsb = ds.astype(BF16)
        lane = lax.broadcasted_iota(jnp.int32, (CHUNK, 128), 1)
        tril = (lax.broadcasted_iota(jnp.int32, (CHUNK, CHUNK), 0)
                >= lax.broadcasted_iota(jnp.int32, (CHUNK, CHUNK), 1))
        zero_b = jnp.zeros((CHUNK, 128), BF16)
        parts = []
        for j in range(GMLP_HEADS // 2):
            dsp = dsb[:, 128 * j:128 * (j + 1)]
            vp = vb[:, 128 * j:128 * (j + 1)]
            parts.append(jnp.where(lane < GMLP_HEAD_DIM, _dot(wst_ref[2 * j], dsp),
                                   _dot(wst_ref[2 * j + 1], dsp)))
            lo = jnp.where(lane < GMLP_HEAD_DIM, dsp, zero_b)
            hi = jnp.where(lane < GMLP_HEAD_DIM, zero_b, dsp)
            dws_ref[2 * j] += jnp.where(tril, _dot_nt(lo, vp), 0.0)
            dws_ref[2 * j + 1] += jnp.where(tril, _dot_nt(hi, vp), 0.0)
        dv = jnp.concatenate(parts, axis=1)
        dg_ref[...] += jnp.sum(dv * vhat, axis=0, keepdims=True)
        db_ref[...] += jnp.sum(dv, axis=0, keepdims=True)
        dz_ref[:, D_GMLP:2 * D_GMLP] = (_ln_bwd(dv, vhat, rstd, gam) * _gelu_grad(zv)).astype(BF16)

    return _call(
        body, "gmlp_bwd", (t // CHUNK,),
        [_rows(CHUNK, 2 * D_GMLP), _rows(CHUNK, D_GMLP), _fixed((1, D_GMLP)), _fixed((1, D_GMLP)),
         _fixed((GMLP_HEADS, CHUNK, CHUNK)), _fixed((GMLP_HEADS, CHUNK, CHUNK)), _fixed((CHUNK, D_GMLP))],
        (_rows(CHUNK, 2 * D_GMLP), _fixed((GMLP_HEADS, CHUNK, CHUNK)), _fixed((CHUNK, D_GMLP)),
         _fixed((1, D_GMLP)), _fixed((1, D_GMLP))),
        (SDS((t, 2 * D_GMLP), BF16), SDS((GMLP_HEADS, CHUNK, CHUNK), F32), SDS((CHUNK, D_GMLP), F32),
         SDS((1, D_GMLP), F32), SDS((1, D_GMLP), F32)),
        (zuv, dgm, ln_g, ln_b, wsm, wsmt, bias), sem=("arbitrary",), bg=bg)


def _mixout_fwd(x1, s5o, gm, gab, ua, ub, wmo, g, b, tm, bg=None):
    t = x1.shape[0]

    def body(x_ref, s_ref, m_ref, gab_ref, ua_ref, ub_ref, wmo_ref, g_ref, b_ref,
             xn_ref, xh_ref, rstd_ref):
        ya = _dot(s_ref[...], ua_ref[...])
        yb = _dot(m_ref[...], ub_ref[...])
        mix = (_sigmoid(gab_ref[:, 0:D_MODEL].astype(F32)) * ya
               + _sigmoid(gab_ref[:, D_MODEL:2 * D_MODEL].astype(F32)) * yb)
        r = ALPHA * x_ref[...] + _dot(mix.astype(BF16), wmo_ref[...])
        y, xh, rstd = _ln_fwd(r, g_ref[...], b_ref[...])
        xn_ref[...] = y
        xh_ref[...] = xh
        rstd_ref[...] = rstd

    return _call(
        body, "mixout_fwd", (t // tm,),
        [_rows(tm, D_MODEL), _rows(tm, D_SSM), _rows(tm, D_GMLP), _rows(tm, 2 * D_MODEL),
         _resident((D_SSM, D_MODEL)), _resident((D_GMLP, D_MODEL)), _resident((D_MODEL, D_MODEL)),
         _fixed((1, D_MODEL)), _fixed((1, D_MODEL))],
        (_rows(tm, D_MODEL), _rows(tm, D_MODEL), _rows(tm, 1)),
        (SDS((t, D_MODEL), F32), SDS((t, D_MODEL), F32), SDS((t, 1), F32)),
        (x1, s5o, gm, gab, ua, ub, wmo, g, b), sem=("parallel",), bg=bg)


def _mixout_bwd(dx2, xh, rstd, s5o, gm, gab, ua, ub, wmo, g, tm, bg=None):
    t = dx2.shape[0]

    def body(d_ref, xh_ref, rstd_ref, s_ref, m_ref, gab_ref, ua_ref, ub_ref, wmo_ref, g_ref,
             dx1_ref, dmx_ref, mb_ref, dya_ref, dyb_ref, ds5_ref, dgm_ref, dgab_ref, dg_ref, db_ref):
        @pl.when(pl.program_id(0) == 0)
        def _():
            dg_ref[...] = jnp.zeros_like(dg_ref)
            db_ref[...] = jnp.zeros_like(db_ref)

        dy = d_ref[...]
        xhv = xh_ref[...]
        dr = _ln_bwd(dy, xhv, rstd_ref[...], g_ref[...])
        dg_ref[...] += jnp.sum(dy * xhv, axis=0, keepdims=True)
        db_ref[...] += jnp.sum(dy, axis=0, keepdims=True)
        dx1_ref[...] = ALPHA * dr
        drb = dr.astype(BF16)
        dmx_ref[...] = drb
        dm = _dot_nt(drb, wmo_ref[...])
        ya = _dot(s_ref[...], ua_ref[...])
        yb = _dot(m_ref[...], ub_ref[...])
        sa = _sigmoid(gab_ref[:, 0:D_MODEL].astype(F32))
        sb = _sigmoid(gab_ref[:, D_MODEL:2 * D_MODEL].astype(F32))
        mb_ref[...] = (sa * ya + sb * yb).astype(BF16).T
        dya = (dm * sa).astype(BF16)
        dyb = (dm * sb).astype(BF16)
        dya_ref[...] = dya
        dyb_ref[...] = dyb
        dgab_ref[:, 0:D_MODEL] = (dm * ya * sa * (1.0 - sa)).astype(BF16)
        dgab_ref[:, D_MODEL:2 * D_MODEL] = (dm * yb * sb * (1.0 - sb)).astype(BF16)
        ds5_ref[...] = _dot_nt(dya, ua_ref[...])
        dgm_ref[...] = _dot_nt(dyb, ub_ref[...])

    return _call(
        body, "mixout_bwd", (t // tm,),
        [_rows(tm, D_MODEL), _rows(tm, D_MODEL), _rows(tm, 1), _rows(tm, D_SSM), _rows(tm, D_GMLP),
         _rows(tm, 2 * D_MODEL), _resident((D_SSM, D_MODEL)), _resident((D_GMLP, D_MODEL)),
         _resident((D_MODEL, D_MODEL)), _fixed((1, D_MODEL))],
        (_rows(tm, D_MODEL), _rows(tm, D_MODEL), _cols(D_MODEL, tm), _rows(tm, D_MODEL),
         _rows(tm, D_MODEL), _rows(tm, D_SSM), _rows(tm, D_GMLP), _rows(tm, 2 * D_MODEL),
         _fixed((1, D_MODEL)), _fixed((1, D_MODEL))),
        (SDS((t, D_MODEL), F32), SDS((t, D_MODEL), BF16), SDS((D_MODEL, t), BF16),
         SDS((t, D_MODEL), BF16), SDS((t, D_MODEL), BF16), SDS((t, D_SSM), F32),
         SDS((t, D_GMLP), F32), SDS((t, 2 * D_MODEL), BF16),
         SDS((1, D_MODEL), F32), SDS((1, D_MODEL), F32)),
        (dx2, xh, rstd, s5o, gm, gab, ua, ub, wmo, g), sem=("arbitrary",), bg=bg)


def _ple_loss(x3, p, tgt, wpg, wpp, tm, bg=None):
    t = x3.shape[0]

    def body(x_ref, p_ref, t_ref, wpg_ref, wpp_ref, dx_ref, xb_ref, pb_ref, dq_ref, de_ref, loss_ref):
        @pl.when(pl.program_id(0) == 0)
        def _():
            loss_ref[...] = jnp.zeros_like(loss_ref)

        x3v = x_ref[...]
        xb = x3v.astype(BF16)
        pb = p_ref[...].astype(BF16)
        xb_ref[...] = xb.T
        pb_ref[...] = pb.T
        s = _sigmoid(_dot(xb, wpg_ref[...]))
        e = _dot(pb, wpp_ref[...])
        diff = x3v + s * e - t_ref[...]
        loss_ref[...] += jnp.sum(diff * diff, axis=0, keepdims=True)
        dout = diff * (1.0 / D_MODEL)
        de_ref[...] = (dout * s).astype(BF16)
        dq = (dout * e * s * (1.0 - s)).astype(BF16)
        dq_ref[...] = dq
        dx_ref[...] = dout + _dot_nt(dq, wpg_ref[...])

    return _call(
        body, "ple_loss", (t // tm,),
        [_rows(tm, D_MODEL), _rows(tm, PLE_DIM), _rows(tm, D_MODEL),
         _resident((D_MODEL, D_MODEL)), _resident((PLE_DIM, D_MODEL))],
        (_rows(tm, D_MODEL), _cols(D_MODEL, tm), _cols(PLE_DIM, tm), _rows(tm, D_MODEL),
         _rows(tm, D_MODEL), _fixed((1, D_MODEL))),
        (SDS((t, D_MODEL), F32), SDS((D_MODEL, t), BF16), SDS((PLE_DIM, t), BF16),
         SDS((t, D_MODEL), BF16), SDS((t, D_MODEL), BF16), SDS((1, D_MODEL), F32)),
        (x3, p, tgt, wpg, wpp), sem=("arbitrary",), bg=bg)


def _s5_discretise(lre, lim, log_dt, bre, bim):
    dt = jnp.exp(log_dt)[:, None]
    mag = jnp.exp(lre * dt)
    abr = mag * jnp.cos(lim * dt)
    abi = mag * jnp.sin(lim * dt)
    nr = abr - 1.0
    ni = abi
    den = lre * lre + lim * lim
    cr = ((nr * lre + ni * lim) / den)[..., None]
    ci = ((ni * lre - nr * lim) / den)[..., None]
    return abr, abi, cr * bre - ci * bim, cr * bim + ci * bre


def _block_diag_in(bb):
    v = bb.reshape(S5_BLOCKS, 8, SSM_STATE, SSM_GROUP_CH).transpose(0, 1, 3, 2)
    return jnp.einsum("bgip,gh->bgihp", v, jnp.eye(8, dtype=bb.dtype)).reshape(
        S5_BLOCKS, S5_BLOCK_IN, S5_BLOCK_ST)


def _block_diag_in_t(dm):
    v = dm.reshape(S5_BLOCKS, 8, SSM_GROUP_CH, 8, SSM_STATE)
    d = jnp.einsum("bgihp,gh->bgip", v, jnp.eye(8, dtype=dm.dtype))
    return d.transpose(0, 1, 3, 2).reshape(SSM_GROUPS, SSM_STATE, SSM_GROUP_CH)


def _block_diag_out(cc):
    v = cc.reshape(S5_BLOCKS, 8, SSM_GROUP_CH, SSM_STATE)
    return jnp.einsum("bgip,gh->bgphi", v, jnp.eye(8, dtype=cc.dtype)).reshape(
        S5_BLOCKS, S5_BLOCK_ST, S5_BLOCK_IN)


def _block_diag_out_t(dn):
    v = dn.reshape(S5_BLOCKS, 8, SSM_STATE, 8, SSM_GROUP_CH)
    d = jnp.einsum("bgphi,gh->bgip", v, jnp.eye(8, dtype=dn.dtype))
    return d.reshape(SSM_GROUPS, SSM_GROUP_CH, SSM_STATE)


def _s5_setup(lre, lim, log_dt, bre, bim, cre, cim, d_skip, glu_w, glu_b, tb):
    seg = tb // 8
    abr, abi, bbr, bbi = _s5_discretise(lre, lim, log_dt, bre, bim)
    pr, pi = abr, abi
    for _ in range(int(math.log2(seg))):
        pr, pi = pr * pr - pi * pi, 2.0 * pr * pi
    rows = jnp.arange(tb)
    src = (rows % 8) * seg + rows // 8
    perm = (src[:, None] == jnp.arange(tb)[None, :]).astype(BF16)
    mre = _block_diag_in(bbr)
    mim = _block_diag_in(bbi)
    nre = _block_diag_out(cre)
    nim = _block_diag_out(cim)
    return {
        "perm": perm, "permt": perm.T,
        "mre": mre.astype(BF16), "mim": mim.astype(BF16),
        "mtre": mre.transpose(0, 2, 1).astype(BF16), "mtim": mim.transpose(0, 2, 1).astype(BF16),
        "nre": nre.astype(BF16), "nim": nim.astype(BF16),
        "ntre": nre.transpose(0, 2, 1).astype(BF16), "ntim": nim.transpose(0, 2, 1).astype(BF16),
        "a": jnp.stack([abr.reshape(-1), abi.reshape(-1)]),
        "ap": jnp.stack([pr.reshape(-1), pi.reshape(-1)]),
        "dskip": d_skip.reshape(1, D_SSM), "glu_w": glu_w, "glu_wt": glu_w.T,
        "glu_b": glu_b.reshape(1, D_SSM),
    }


BIG = ("ffn1_w_in", "ffn1_w_out", "mix_w_in", "ssm_glu_w", "up_a", "up_b", "mix_w_out",
       "ffn2_w_in", "ffn2_w_out", "ple_w_proj", "ple_w_gate")
BIG_AXIS = {"ffn1_w_in": 1, "ffn1_w_out": 0, "mix_w_in": 1, "ssm_glu_w": 0, "up_a": 1, "up_b": 1,
            "mix_w_out": 0, "ffn2_w_in": 1, "ffn2_w_out": 0, "ple_w_proj": 1, "ple_w_gate": 0}
SHARD_MAJOR = 2
GATHER_AXIS = dict(BIG_AXIS, ffn1_w_in=SHARD_MAJOR, ffn2_w_in=SHARD_MAJOR)
GATHER_ORDER = (("ffn1_w_in",), ("ffn1_w_out",), ("mix_w_in",), ("ssm_glu_w", "up_a", "up_b", "mix_w_out"),
                ("ffn2_w_in",), ("ffn2_w_out", "ple_w_gate", "ple_w_proj"))
GATHER_FIRST_ID = 1
REDUCE_FIRST_ID = 7
SMALL = ("ln1_g", "ln1_b", "ssm_lambda_re", "ssm_lambda_im", "ssm_log_dt", "ssm_b_re", "ssm_b_im",
         "ssm_c_re", "ssm_c_im", "ssm_d", "ssm_glu_b", "gmlp_ln_g", "gmlp_ln_b", "gmlp_w_s",
         "gmlp_b_s", "ln2_g", "ln2_b", "ln3_g", "ln3_b")
SMALL_VIEW = {"ssm_b_re": (SSM_GROUPS, SSM_STATE * SSM_GROUP_CH), "ssm_b_im": (SSM_GROUPS, SSM_STATE * SSM_GROUP_CH)}


def _small_view(k, a):
    return a.reshape(SMALL_VIEW[k]) if k in SMALL_VIEW else a


def _place():
    return lax.axis_index("x"), lax.axis_index("y"), lax.axis_index("c")


def _other_chips(x, y):
    return [(1 - x, y), (x, 1 - y), (1 - x, 1 - y)]


def _window(ref, shard_shape, axis, chip, half):
    r, c = shard_shape
    hr = r // 2
    if axis == SHARD_MAJOR:
        return ref.at[chip] if half is None else ref.at[chip, pl.ds(half * hr, hr), :]
    if axis == 0:
        if half is None:
            return ref.at[pl.ds(chip * r, r), :]
        return ref.at[pl.ds(chip * r + half * hr, hr), :]
    if half is None:
        return ref.at[:, pl.ds(chip * c, c)]
    return ref.at[pl.ds(half * hr, hr), pl.ds(chip * c, c)]


def _gather_weights(shards, axes):
    n = len(shards)
    shapes = [s.shape for s in shards]
    full = [{0: (4 * r, c), 1: (r, 4 * c), SHARD_MAJOR: (4, r, c)}[ax] for (r, c), ax in zip(shapes, axes)]

    def remote(sems, i, k, src, dst, to):
        return pltpu.make_async_remote_copy(src_ref=src, dst_ref=dst, send_sem=sems[0].at[6 * i + k],
                                            recv_sem=sems[1].at[6 * i + k], device_id=to, device_id_type=MESH)

    def own_copies(ins, outs, sems):
        x, y, c = _place()
        me = 2 * x + y
        cps = []
        for i in range(n):
            hr = shapes[i][0] // 2
            mine = ins[i].at[pl.ds(c * hr, hr), :]
            for j, (cx, cy) in enumerate(_other_chips(x, y)):
                cps.append(remote(sems, i, j, mine, _window(outs[i], shapes[i], axes[i], me, c), (cx, cy, c)))
        local = [pltpu.make_async_copy(ins[i], _window(outs[i], shapes[i], axes[i], me, None), sems[2].at[i])
                 for i in range(n)]
        return cps, local

    def start(ins, outs, sems):
        cps, local = own_copies(ins, outs, sems)
        for cp in local + cps:
            cp.start()

    def finish(ins, outs, sems):
        x, y, c = _place()
        sibling = (x, y, 1 - c)
        passed = []
        for j, (cx, cy) in enumerate(_other_chips(x, y)):
            for i in range(n):
                w = _window(outs[i], shapes[i], axes[i], 2 * cx + cy, c)
                remote(sems, i, j, w, w, (cx, cy, c)).wait_recv()
                cp = remote(sems, i, 3 + j, w, w, sibling)
                cp.start()
                passed.append(cp)
        for j, (cx, cy) in enumerate(_other_chips(x, y)):
            for i in range(n):
                w = _window(outs[i], shapes[i], axes[i], 2 * cx + cy, 1 - c)
                remote(sems, i, 3 + j, w, w, sibling).wait_recv()
        cps, local = own_copies(ins, outs, sems)
        for cp in cps + passed:
            cp.wait_send()
        for cp in local:
            cp.wait()

    return _Exchange(shards, [SDS(f, BF16) for f in full],
                     [pltpu.SemaphoreType.DMA((6 * n,)), pltpu.SemaphoreType.DMA((6 * n,)),
                      pltpu.SemaphoreType.DMA((n,))], start, finish)


def _scatter_grads(parts, shapes, axes):
    n = len(parts)

    def copies(ins, outs, sems):
        x, y, c = _place()
        return [pltpu.make_async_remote_copy(
            src_ref=_window(ins[i], shapes[i], axes[i], 2 * cx + cy, None), dst_ref=outs[i].at[j],
            send_sem=sems[0].at[3 * i + j], recv_sem=sems[1].at[3 * i + j],
            device_id=(cx, cy, c), device_id_type=MESH)
            for i in range(n) for j, (cx, cy) in enumerate(_other_chips(x, y))]

    def start(ins, outs, sems):
        for cp in copies(ins, outs, sems):
            cp.start()

    def finish(ins, outs, sems):
        for cp in copies(ins, outs, sems):
            cp.wait()

    return _Exchange(parts, [SDS((3,) + tuple(s), BF16) for s in shapes],
                     [pltpu.SemaphoreType.DMA((3 * n,)), pltpu.SemaphoreType.DMA((3 * n,))], start, finish)


def _swap_halves(parts, shapes, axes):
    n = len(parts)

    def copies(ins, outs, sems):
        x, y, c = _place()
        cps = []
        for i in range(n):
            r, _ = shapes[i]
            hr = r // 2
            if axes[i] == 0:
                cps += [pltpu.make_async_remote_copy(
                    src_ref=ins[i].at[pl.ds(k * r + (1 - c) * hr, hr), :], dst_ref=outs[i].at[k],
                    send_sem=sems[0].at[i], recv_sem=sems[1].at[i], device_id=(x, y, 1 - c),
                    device_id_type=MESH) for k in range(4)]
            else:
                cps.append(pltpu.make_async_remote_copy(
                    src_ref=ins[i].at[pl.ds((1 - c) * hr, hr), :], dst_ref=outs[i],
                    send_sem=sems[0].at[i], recv_sem=sems[1].at[i], device_id=(x, y, 1 - c),
                    device_id_type=MESH))
        return cps

    def start(ins, outs, sems):
        for cp in copies(ins, outs, sems):
            cp.start()

    def finish(ins, outs, sems):
        x, y, c = _place()
        for i in range(n):
            pltpu.make_async_remote_copy(src_ref=outs[i], dst_ref=outs[i], send_sem=sems[0].at[i],
                                         recv_sem=sems[1].at[i], device_id=(x, y, 1 - c),
                                         device_id_type=MESH).wait()

    out = [SDS((4, r // 2, c), BF16) if ax == 0 else SDS((r // 2, 4 * c), BF16)
           for (r, c), ax in zip(shapes, axes)]
    return _Exchange(parts, out, [pltpu.SemaphoreType.DMA((n,)), pltpu.SemaphoreType.DMA((n,))], start, finish)


def _scatter_halves(pres, shapes):
    n = len(pres)

    def copies(ins, outs, sems):
        x, y, c = _place()
        return [pltpu.make_async_remote_copy(
            src_ref=ins[i].at[1 + j], dst_ref=outs[i].at[j], send_sem=sems[0].at[3 * i + j],
            recv_sem=sems[1].at[3 * i + j], device_id=(cx, cy, c), device_id_type=MESH)
            for i in range(n) for j, (cx, cy) in enumerate(_other_chips(x, y))]

    def start(ins, outs, sems):
        for cp in copies(ins, outs, sems):
            cp.start()

    def finish(ins, outs, sems):
        for cp in copies(ins, outs, sems):
            cp.wait()

    return _Exchange(pres, [SDS((3, r // 2, c), BF16) for r, c in shapes],
                     [pltpu.SemaphoreType.DMA((3 * n,)), pltpu.SemaphoreType.DMA((3 * n,))], start, finish)


def _swap_with_sibling(arrs):
    n = len(arrs)

    def copies(ins, outs, sems):
        x, y, c = _place()
        return [pltpu.make_async_remote_copy(src_ref=ins[i], dst_ref=outs[i], send_sem=sems[0].at[i],
                                             recv_sem=sems[1].at[i], device_id=(x, y, 1 - c),
                                             device_id_type=MESH) for i in range(n)]

    def start(ins, outs, sems):
        for cp in copies(ins, outs, sems):
            cp.start()

    def finish(ins, outs, sems):
        for cp in copies(ins, outs, sems):
            cp.wait()

    return _Exchange(arrs, [SDS(a.shape, a.dtype) for a in arrs],
                     [pltpu.SemaphoreType.DMA((n,)), pltpu.SemaphoreType.DMA((n,))], start, finish)


def _gather_small(arrs):
    n = len(arrs)

    def copy(sems, outs, i, k, block, to, src=None):
        px, py, pc = block
        dst = outs[i].at[4 * px + 2 * py + pc]
        return pltpu.make_async_remote_copy(
            src_ref=dst if src is None else src, dst_ref=dst, send_sem=sems[0].at[7 * i + k],
            recv_sem=sems[1].at[7 * i + k], device_id=to, device_id_type=MESH)

    direct = [math.prod(a.shape) * 4 <= DIRECT_GATHER_BYTES for a in arrs]

    def own_copies(ins, outs, sems):
        x, y, c = _place()
        cps = []
        for i in range(n):
            cps.append(copy(sems, outs, i, 0, (x, y, c), (x, y, 1 - c), src=ins[i]))
            for j, (cx, cy) in enumerate(_other_chips(x, y)):
                cps.append(copy(sems, outs, i, 1 + j, (x, y, c), (cx, cy, c), src=ins[i]))
                if direct[i]:
                    cps.append(copy(sems, outs, i, 4 + j, (x, y, c), (cx, cy, 1 - c), src=ins[i]))
        local = [pltpu.make_async_copy(ins[i], outs[i].at[4 * x + 2 * y + c], sems[2].at[i]) for i in range(n)]
        return cps, local

    def start(ins, outs, sems):
        cps, local = own_copies(ins, outs, sems)
        for cp in local + cps:
            cp.start()

    def finish(ins, outs, sems):
        x, y, c = _place()
        passed = []
        for j, (cx, cy) in enumerate(_other_chips(x, y)):
            for i in range(n):
                copy(sems, outs, i, 1 + j, (cx, cy, c), (x, y, c)).wait_recv()
                if not direct[i]:
                    cp = copy(sems, outs, i, 4 + j, (cx, cy, c), (x, y, 1 - c))
                    cp.start()
                    passed.append(cp)
        for i in range(n):
            copy(sems, outs, i, 0, (x, y, 1 - c), (x, y, c)).wait_recv()
            for j, (cx, cy) in enumerate(_other_chips(x, y)):
                copy(sems, outs, i, 4 + j, (cx, cy, 1 - c), (x, y, c)).wait_recv()
        cps, local = own_copies(ins, outs, sems)
        for cp in cps + passed:
            cp.wait_send()
        for cp in local:
            cp.wait()

    return _Exchange(arrs, [SDS((N_DEV,) + a.shape, F32) for a in arrs],
                     [pltpu.SemaphoreType.DMA((7 * n,)), pltpu.SemaphoreType.DMA((7 * n,)),
                      pltpu.SemaphoreType.DMA((n,))], start, finish)


def _local_step(x, p, tgt, wb, ws, shards=None, opt=None):
    bsz, seq, _ = x.shape
    t = bsz * seq
    tm = min(256, t)
    tb = min(256, seq)
    x0 = x.reshape(t, D_MODEL)
    p0 = p.reshape(t, PLE_DIM)
    tg = tgt.reshape(t, D_MODEL)
    row = lambda v: v.reshape(1, -1)
    dist = shards is not None
    wb = dict(wb)
    recv, sums, other, gathered = {}, {}, {}, {}
    gb = {}
    gs = {}
    shape_of, axis_of = {}, {}
    chip = None
    if dist:
        shape_of = {k: tuple(shards[k].shape) for k in BIG}
        axis_of = dict(BIG_AXIS)
        for q in range(LAST_PIECES):
            shape_of[LAST_PIECE % q] = (D_MODEL // LAST_PIECES, shape_of["ffn1_w_in"][1])
            axis_of[LAST_PIECE % q] = 1
        xi, yi, ci = _place()
        chip = (2 * xi + yi).astype(jnp.int32).reshape(1)
        ids = jnp.stack([2 * xi + yi] + [2 * cx + cy for cx, cy in _other_chips(xi, yi)] + [ci]).astype(jnp.int32)
    halfbuf, pre = {}, {}

    def gather(names):
        return _gather_weights([shards[k] for k in names], [GATHER_AXIS[k] for k in names]) if dist else None

    def exchange(scat=(), swap=(), halves=(), scat2=(), swap2=(), extra=None, after=None):
        if not dist:
            return None, []
        after = order[0] if after is None else after
        parts, tags = [], []
        if scat:
            parts.append(_scatter_grads([gb[k][1] for k in scat], [shape_of[k] for k in scat],
                                        [axis_of[k] for k in scat]))
            tags.append((recv, scat))
        if swap:
            for k in swap:
                sums[k] = order[0] = _sum_blocks(gb[k][0], recv[k], shape_of[k], axis_of[k], chip, "sum_" + k,
                                                 order[0])
            parts.append(_swap_with_sibling([sums[k] for k in swap]))
            tags.append((other, swap))
        if halves:
            parts.append(_swap_halves([gb[k][1] for k in halves], [shape_of[k] for k in halves],
                                      [axis_of[k] for k in halves]))
            tags.append((halfbuf, halves))
        if scat2:
            for k in scat2:
                pre[k] = _presum(gb[k][0], halfbuf[k], shape_of[k], axis_of[k], ids, "presum_" + k, order[0])
                order[0] = pre[k][0]
            parts.append(_scatter_halves([pre[k][1] for k in scat2], [shape_of[k] for k in scat2]))
            tags.append((recv, scat2))
        if swap2:
            for k in swap2:
                sums[k] = order[0] = _sum_half(pre[k][0], recv[k], "sum_" + k, order[0])
            parts.append(_swap_with_sibling([sums[k] for k in swap2]))
            tags.append((other, swap2))
        if extra is not None:
            parts.append(extra[0])
            tags.append((extra[1], extra[2]))
        return (_join(parts), tags) if parts else (None, [])

    def take(ex_tags, got):
        ex, tags = ex_tags
        if ex is not None:
            for (dst, names), (o0, o1) in zip(tags, ex.cuts):
                dst.update(zip(names, got[o0:o1]))

    order = [None]

    def ordered(builder, *args, **kw):
        res = builder(*args, bg=order[0] if dist else None, **kw)
        order[0] = res[0][0]
        return res

    launched = []

    def launch(ex_tags):
        if ex_tags[0] is not None:
            n = len(launched)
            launched.append(n)
            take(ex_tags, _run_exchange_on_sequencer(ex_tags[0], "reduce_%d" % n, REDUCE_FIRST_ID + n))

    small_shape = {k: _small_view(k, v).shape for k, v in ws.items()}
    small_shape["loss_rows"] = (1, D_MODEL)
    ws = {k: v if (v.ndim == 2 and k != "ssm_log_dt") else v[0] for k, v in ws.items()}
    tril = jnp.tril(jnp.ones((CHUNK, CHUNK), dtype=bool))
    wsm = jnp.where(tril[None], ws["gmlp_w_s"], 0.0)
    wsm_b = wsm.astype(BF16)
    wsmt_b = wsm.transpose(0, 2, 1).astype(BF16)
    bias = jnp.repeat(ws["gmlp_b_s"].T, GMLP_HEAD_DIM, axis=1)

    tf = min(512, t)
    if dist:
        for gi, names in enumerate(GATHER_ORDER):
            wb.update(zip(names, _run_exchange_on_sequencer(gather(names), "gather_%d" % gi, GATHER_FIRST_ID + gi)))
    (x0b, h1, a1), _ = _ffn_proj(x0, wb["ffn1_w_in"], tf, "ffn1_proj")
    (x1, xh1, rstd1), _ = _ffn_out(x0, a1, wb["ffn1_w_out"], row(ws["ln1_g"]), row(ws["ln1_b"]), tf, "ffn1_out")
    sp = _s5_setup(ws["ssm_lambda_re"], ws["ssm_lambda_im"], ws["ssm_log_dt"], ws["ssm_b_re"],
                   ws["ssm_b_im"], ws["ssm_c_re"], ws["ssm_c_im"], ws["ssm_d"], wb["ssm_glu_w"],
                   ws["ssm_glu_b"], tb)
    (x1b, za, zuv, gab), _ = _mixin_fwd(x1, wb["mix_w_in"], tf)
    (s5o, s5ot, y2p, carries), _ = _s5_fwd(za, sp, bsz, seq, tb)
    (gm, gmt), _ = _gmlp_fwd(zuv, row(ws["gmlp_ln_g"]), row(ws["gmlp_ln_b"]), wsm_b, bias)
    (x2, xh2, rstd2), _ = _mixout_fwd(x1, s5o, gm, gab, wb["up_a"], wb["up_b"], wb["mix_w_out"],
                                           row(ws["ln2_g"]), row(ws["ln2_b"]), tf)
    (x2b, h2, a2), _ = _ffn_proj(x2, wb["ffn2_w_in"], tf, "ffn2_proj")
    (x3, xh3, rstd3), _ = _ffn_out(x2, a2, wb["ffn2_w_out"], row(ws["ln3_g"]), row(ws["ln3_b"]), tf, "ffn2_out")
    (dx3, x3b, pb, dq, de, loss_rows), _ = _ple_loss(x3, p0, tg, wb["ple_w_gate"], wb["ple_w_proj"], tf)
    order[0] = dx3
    gb["ple_w_gate"], _ = ordered(_tn_matmul, x3b, dq, "dw_ple_gate", 1024, 1024, a_t=True)
    gb["ple_w_proj"], _ = ordered(_tn_matmul, pb, de, "dw_ple_proj", 256, 1024, a_t=True)
    launch(exchange(scat=("ple_w_gate", "ple_w_proj")))
    (dx2, dh2, df2, gs["ln3_g"], gs["ln3_b"]), _ = ordered(
        _ffn_bwd, dx3, xh3, rstd3, h2, wb["ffn2_w_in"], wb["ffn2_w_out"], row(ws["ln3_g"]), tm, "ffn2_bwd")
    gb["ffn2_w_out"], _ = ordered(_tn_matmul, a2, df2, "dw_ffn2_out", 1408, 1024)
    launch(exchange(scat=("ffn2_w_out",)))
    gb["ffn2_w_in"], _ = ordered(_tn_matmul, x2b, dh2, "dw_ffn2_in", 1024, 1408, a_t=True)
    launch(exchange(scat=("ffn2_w_in",), swap=("ple_w_gate", "ple_w_proj")))
    (dx1a, dmx, mb, dya, dyb, ds5, dgm, dgab, gs["ln2_g"], gs["ln2_b"]), _ = ordered(
        _mixout_bwd, dx2, xh2, rstd2, s5o, gm, gab, wb["up_a"], wb["up_b"], wb["mix_w_out"], row(ws["ln2_g"]), tm)
    gb["mix_w_out"], _ = ordered(_tn_matmul, mb, dmx, "dw_mix_out", 1024, 1024, a_t=True)
    gb["up_a"], _ = ordered(_tn_matmul, s5ot, dya, "dw_up_a", 512, 1024, a_t=True)
    gb["up_b"], _ = ordered(_tn_matmul, gmt, dyb, "dw_up_b", 512, 1024, a_t=True)
    launch(exchange(scat=("mix_w_out", "up_a", "up_b"), swap=("ffn2_w_out",)))
    (dza, dmr, dmi, dnr, dni, da, ddsk, dgw, dgb), _ = ordered(_s5_bwd, za, y2p, ds5, carries, sp, bsz, seq, tb)
    gb["ssm_glu_w"] = (dgw, dgw.astype(BF16))
    launch(exchange(scat=("ssm_glu_w",), swap=("ffn2_w_in",)))
    (dzuv, dws, dbias, gs["gmlp_ln_g"], gs["gmlp_ln_b"]), _ = ordered(
        _gmlp_bwd, zuv, dgm, row(ws["gmlp_ln_g"]), row(ws["gmlp_ln_b"]), wsm_b, wsmt_b, bias)
    (dx1,), _ = ordered(_mixin_bwd, dx1a, dza, dzuv, dgab, wb["mix_w_in"], tf)
    g_mi, _ = ordered(_tn_matmul, x1b, dza, "dw_mix_in_a", 1024, 512, 0, 3584, a_t=True)
    g_mi, _ = ordered(_tn_matmul, x1b, dzuv, "dw_mix_in_uv", 1024, 512, 1, 3584, g_mi, a_t=True)
    gb["mix_w_in"], _ = ordered(_tn_matmul, x1b, dgab, "dw_mix_in_g", 1024, 512, 3, 3584, g_mi, a_t=True)
    launch(exchange(swap=("mix_w_out", "up_a", "up_b", "ssm_glu_w")))

    d_abr = da[0].sum(axis=0).reshape(SSM_GROUPS, SSM_STATE)
    d_abi = da[1].sum(axis=0).reshape(SSM_GROUPS, SSM_STATE)
    _, vjp = jax.vjp(_s5_discretise, ws["ssm_lambda_re"], ws["ssm_lambda_im"], ws["ssm_log_dt"],
                     ws["ssm_b_re"], ws["ssm_b_im"])
    (gs["ssm_lambda_re"], gs["ssm_lambda_im"], gs["ssm_log_dt"], gs["ssm_b_re"], gs["ssm_b_im"]) = vjp(
        (d_abr, d_abi, _block_diag_in_t(dmr), _block_diag_in_t(dmi)))
    gs["ssm_c_re"] = _block_diag_out_t(dnr)
    gs["ssm_c_im"] = _block_diag_out_t(dni)
    gs["ssm_d"] = ddsk
    gs["ssm_glu_b"] = dgb
    gs["gmlp_w_s"] = dws
    gs["gmlp_b_s"] = dbias.reshape(CHUNK, GMLP_HEADS, GMLP_HEAD_DIM).sum(axis=-1).T
    gs["loss_rows"] = loss_rows

    def small_gather(names):
        return (_gather_small([gs[k].reshape(small_shape[k]) for k in names]), gathered, names) if dist else None

    late = ("ln1_g", "ln1_b")
    launch(exchange(scat=("mix_w_in",), extra=small_gather(tuple(k for k in SMALL + ("loss_rows",) if k not in late))))
    (dx0, dh1, df1, gs["ln1_g"], gs["ln1_b"]), _ = ordered(
        _ffn_bwd, dx1, xh1, rstd1, h1, wb["ffn1_w_in"], wb["ffn1_w_out"], row(ws["ln1_g"]), tm, "ffn1_bwd")
    grad_x = dx0.reshape(bsz, seq, D_MODEL)
    if not dist:
        gb["ffn1_w_out"], _ = _tn_matmul(a1, df1, "dw_ffn1_out", 1408, 1024)
        gb["ffn1_w_in"], _ = _tn_matmul(x0b, dh1, "dw_ffn1_in", 1024, 1408, a_t=True)
        return (loss_rows, grad_x, gb, {k: gs[k].reshape(small_shape[k]) for k in SMALL}, sums, other, gathered,
                None, {})
    launch(exchange(extra=small_gather(late)))
    gb["ffn1_w_out"], _ = ordered(_tn_matmul, a1, df1, "dw_ffn1_out", 1408, 1024)
    last = ["ffn1_w_out"] + [LAST_PIECE % q for q in range(LAST_PIECES)]
    fillers = (("ffn2_w_in", "mix_w_in", "ple_w_gate"),
               ("ffn2_w_out", "mix_w_out", "up_a", "up_b", "ssm_glu_w", "ple_w_proj"))
    out = {}
    for i in range(1, len(last) + 3):
        stage = lambda d: tuple(last[i - d:i - d + 1]) if 0 <= i - d < len(last) else ()
        launch(exchange(halves=stage(1), scat2=stage(2), swap2=stage(3), swap=("mix_w_in",) if i == 2 else ()))
        if i < len(last):
            gb[last[i]], _ = ordered(_tn_matmul, x0b, dh1, "dw_" + last[i], D_MODEL // LAST_PIECES, 1408,
                                     a_cols=(i - 1, 1), a_t=True)
        elif i - len(last) < len(fillers):
            for k in fillers[i - len(last)]:
                w, m, v = opt[k]
                out[k] = _adam_big(w, sums[k], other[k], m, v, "adam_" + k, after=order[0])
                order[0] = out[k][1]
    return loss_rows, grad_x, gb, gs, sums, other, gathered, ids, out


def _adamw(w, g, m, v):
    m = ADAM_B1 * m + (1.0 - ADAM_B1) * g
    v = ADAM_B2 * v + (1.0 - ADAM_B2) * (g * g)
    m_hat = m / ADAM_C1
    v_hat = v / ADAM_C2
    delta = -ADAM_LR * (m_hat / (jnp.sqrt(v_hat) + ADAM_EPS) + ADAM_WD * w)
    return delta, m, v


def _pinned(after):
    return ([pl.BlockSpec(memory_space=pl.ANY)], [after]) if after is not None else ([], [])


def _sum_blocks(part, recv, shape, axis, chip, name, after=None):
    r, c = shape
    rb = r // ROW_STEPS

    def body(chip_ref, p_ref, r_ref, *rest):
        rest[-1][...] = (p_ref[...] + r_ref[0].astype(F32) + r_ref[1].astype(F32) + r_ref[2].astype(F32))

    if axis == 0:
        own = pl.BlockSpec((rb, c), lambda i, k: (k[0] * ROW_STEPS + i, 0))
    else:
        own = pl.BlockSpec((rb, c), lambda i, k: (i, k[0]))
    pin_specs, pin_args = _pinned(after)
    grid_spec = pltpu.PrefetchScalarGridSpec(
        num_scalar_prefetch=1, grid=(ROW_STEPS,),
        in_specs=[own, pl.BlockSpec((3, rb, c), lambda i, k: (0, i, 0))] + pin_specs,
        out_specs=pl.BlockSpec((rb, c), lambda i, k: (i, 0)))
    return pl.pallas_call(body, name=name, out_shape=SDS((r, c), F32), grid_spec=grid_spec,
                          compiler_params=_params(("parallel",)))(chip, part, recv, *pin_args)


def _presum(part, half, shape, axis, ids, name, after=None):
    r, c = shape
    rb = r // 4

    def body(ids_ref, p_ref, h_ref, *rest):
        of_ref, ob_ref = rest[-2:]
        s = p_ref[...] + h_ref[...].astype(F32)
        ob_ref[...] = s.astype(BF16)

        @pl.when(pl.program_id(1) == 0)
        def _():
            of_ref[...] = s

    if axis == 0:
        p_spec = pl.BlockSpec((rb, c), lambda i, t, ids: (ids[t] * 4 + ids[4] * 2 + i, 0))
        h_spec = pl.BlockSpec((None, rb, c), lambda i, t, ids: (ids[t], i, 0))
    else:
        p_spec = pl.BlockSpec((rb, c), lambda i, t, ids: (ids[4] * 2 + i, ids[t]))
        h_spec = pl.BlockSpec((rb, c), lambda i, t, ids: (i, ids[t]))
    pin_specs, pin_args = _pinned(after)
    grid_spec = pltpu.PrefetchScalarGridSpec(
        num_scalar_prefetch=1, grid=(2, 4), in_specs=[p_spec, h_spec] + pin_specs,
        out_specs=(pl.BlockSpec((rb, c), lambda i, t, ids: (i, 0)),
                   pl.BlockSpec((None, rb, c), lambda i, t, ids: (t, i, 0))))
    return pl.pallas_call(body, name=name, out_shape=(SDS((r // 2, c), F32), SDS((4, r // 2, c), BF16)),
                          grid_spec=grid_spec,
                          compiler_params=_params(("parallel", "arbitrary")))(ids, part, half, *pin_args)


def _sum_half(pre, recv, name, after=None):
    hr, c = pre.shape
    rb = hr // 2

    def body(p_ref, r_ref, *rest):
        rest[-1][...] = (p_ref[...] + r_ref[0].astype(F32) + r_ref[1].astype(F32) + r_ref[2].astype(F32))

    spec = pl.BlockSpec((rb, c), lambda i: (i, 0))
    pin_specs, pin_args = _pinned(after)
    return pl.pallas_call(body, name=name, grid=(2,), out_shape=SDS((hr, c), F32),
                          in_specs=[spec, pl.BlockSpec((3, rb, c), lambda i: (0, i, 0))] + pin_specs,
                          out_specs=spec, compiler_params=_params(("parallel",)))(pre, recv, *pin_args)


def _adam_halves(w, mine, oth, m, v, ids, name, piece=0, prev=None):
    r, c = w.shape
    rb = mine.shape[0] // 2

    def body(ids_ref, w_ref, a_ref, b_ref, m_ref, v_ref, *rest):
        g_ref, d_ref, nm_ref, nv_ref = rest[-4:]
        g = jnp.where(pl.program_id(0) // 2 == ids_ref[4], a_ref[...], b_ref[...])
        g_ref[...] = g
        d_ref[...], nm_ref[...], nv_ref[...] = _adamw(w_ref[...], g, m_ref[...], v_ref[...])

    whole = pl.BlockSpec((rb, c), lambda i, ids: (i + 4 * piece, 0))
    part = pl.BlockSpec((rb, c), lambda i, ids: (i % 2, 0))
    in_specs = [whole, part, part, whole, whole]
    args = [w, mine, oth, m, v]
    aliases = {}
    if prev is not None:
        in_specs += [pl.BlockSpec(memory_space=pl.ANY)] * 4
        args += list(prev)
        aliases = {6: 0, 7: 1, 8: 2, 9: 3}
    grid_spec = pltpu.PrefetchScalarGridSpec(num_scalar_prefetch=1, grid=(4,), in_specs=in_specs,
                                             out_specs=(whole,) * 4)
    return pl.pallas_call(body, name=name, out_shape=tuple(SDS((r, c), F32) for _ in range(4)),
                          grid_spec=grid_spec, input_output_aliases=aliases,
                          compiler_params=_params(("parallel",)))(ids, *args)


def _adam_big(w, ga, gb, m, v, name, piece=0, prev=None, after=None):
    r, c = w.shape
    pr = ga.shape[0]
    steps = ROW_STEPS if pr == r else 2
    rb = pr // steps
    off = piece * steps

    def body(w_ref, ga_ref, gb_ref, m_ref, v_ref, *rest):
        g_ref, d_ref, nm_ref, nv_ref = rest[-4:]
        g = ga_ref[...] + gb_ref[...]
        g_ref[...] = g
        d_ref[...], nm_ref[...], nv_ref[...] = _adamw(w_ref[...], g, m_ref[...], v_ref[...])

    whole = pl.BlockSpec((rb, c), lambda i: (i + off, 0))
    part = pl.BlockSpec((rb, c), lambda i: (i, 0))
    in_specs = [whole, part, part, whole, whole]
    args = [w, ga, gb, m, v]
    aliases = {}
    if prev is not None:
        in_specs += [pl.BlockSpec(memory_space=pl.ANY)] * 4
        args += list(prev)
        aliases = {5: 0, 6: 1, 7: 2, 8: 3}
    if after is not None:
        in_specs.append(pl.BlockSpec(memory_space=pl.ANY))
        args.append(after)
    return pl.pallas_call(
        body, name=name, grid=(steps,), out_shape=tuple(SDS((r, c), F32) for _ in range(4)),
        in_specs=in_specs, out_specs=(whole,) * 4, input_output_aliases=aliases,
        compiler_params=_params(("parallel",)),
    )(*args)


def _adam_small(ws, gathered, ms, vs):
    n = len(ws)

    def body(*refs):
        w_refs, g_refs, m_refs, v_refs = refs[:n], refs[n:2 * n], refs[2 * n:3 * n], refs[3 * n:4 * n]
        outs = refs[4 * n:]
        for i in range(n):
            g = g_refs[i][0]
            for d in range(1, N_DEV):
                g = g + g_refs[i][d]
            delta, nm, nv = _adamw(w_refs[i][...], g, m_refs[i][...], v_refs[i][...])
            outs[i][...] = g
            outs[n + i][...] = delta
            outs[2 * n + i][...] = nm
            outs[3 * n + i][...] = nv

    vmem = pl.BlockSpec(memory_space=pltpu.VMEM)
    shapes = [w.shape for w in ws]
    return pl.pallas_call(
        body, name="adam_small", out_shape=tuple(SDS(s, F32) for s in shapes * 4),
        in_specs=[vmem] * (4 * n), out_specs=tuple([vmem] * (4 * n)),
        compiler_params=pltpu.CompilerParams(vmem_limit_bytes=VMEM_LIMIT_BYTES),
    )(*ws, *gathered, *ms, *vs)


def _sum_loss(gathered):
    def body(g_ref, o_ref):
        tot = g_ref[0]
        for d in range(1, N_DEV):
            tot = tot + g_ref[d]
        o_ref[...] = (0.5 / D_MODEL) * jnp.sum(tot, axis=1, keepdims=True)

    vmem = pl.BlockSpec(memory_space=pltpu.VMEM)
    return pl.pallas_call(body, name="sum_loss", out_shape=SDS((1, 1), F32), in_specs=[vmem],
                          out_specs=vmem)(gathered)


def kernel(x, p, ffn1_w_in, ffn1_w_out, ln1_g, ln1_b, mix_w_in, ssm_lambda_re, ssm_lambda_im, ssm_log_dt, ssm_b_re, ssm_b_im, ssm_c_re, ssm_c_im, ssm_d, ssm_glu_w, ssm_glu_b, gmlp_ln_g, gmlp_ln_b, gmlp_w_s, gmlp_b_s, up_a, up_b, mix_w_out, ln2_g, ln2_b, ffn2_w_in, ffn2_w_out, ln3_g, ln3_b, ple_w_proj, ple_w_gate, loss_target, m_ffn1_w_in, m_ffn1_w_out, m_ln1_g, m_ln1_b, m_mix_w_in, m_ssm_lambda_re, m_ssm_lambda_im, m_ssm_log_dt, m_ssm_b_re, m_ssm_b_im, m_ssm_c_re, m_ssm_c_im, m_ssm_d, m_ssm_glu_w, m_ssm_glu_b, m_gmlp_ln_g, m_gmlp_ln_b, m_gmlp_w_s, m_gmlp_b_s, m_up_a, m_up_b, m_mix_w_out, m_ln2_g, m_ln2_b, m_ffn2_w_in, m_ffn2_w_out, m_ln3_g, m_ln3_b, m_ple_w_proj, m_ple_w_gate, v_ffn1_w_in, v_ffn1_w_out, v_ln1_g, v_ln1_b, v_mix_w_in, v_ssm_lambda_re, v_ssm_lambda_im, v_ssm_log_dt, v_ssm_b_re, v_ssm_b_im, v_ssm_c_re, v_ssm_c_im, v_ssm_d, v_ssm_glu_w, v_ssm_glu_b, v_gmlp_ln_g, v_gmlp_ln_b, v_gmlp_w_s, v_gmlp_b_s, v_up_a, v_up_b, v_mix_w_out, v_ln2_g, v_ln2_b, v_ffn2_w_in, v_ffn2_w_out, v_ln3_g, v_ln3_b, v_ple_w_proj, v_ple_w_gate):
    given = dict(locals())
    order = ("ffn1_w_in", "ffn1_w_out", "ln1_g", "ln1_b", "mix_w_in", "ssm_lambda_re", "ssm_lambda_im",
             "ssm_log_dt", "ssm_b_re", "ssm_b_im", "ssm_c_re", "ssm_c_im", "ssm_d", "ssm_glu_w", "ssm_glu_b",
             "gmlp_ln_g", "gmlp_ln_b", "gmlp_w_s", "gmlp_b_s", "up_a", "up_b", "mix_w_out", "ln2_g", "ln2_b",
             "ffn2_w_in", "ffn2_w_out", "ln3_g", "ln3_b", "ple_w_proj", "ple_w_gate")
    assert set(order) == set(BIG + SMALL)

    shard = {k: given[k][0] for k in BIG}
    shard_b = {k: shard[k].astype(BF16) for k in BIG}
    opt = {k: (shard[k], given["m_" + k][0], given["v_" + k][0]) for k in BIG}
    loss_rows, grad_x, gb, gs, sums, other, gathered, ids, out = _local_step(
        x, given["p"][0], loss_target, {}, {k: given[k] for k in SMALL}, shard_b, opt)

    out = dict(out)
    for k in BIG:
        if k in out:
            continue
        moments = (given["m_" + k][0], given["v_" + k][0])
        if k == "ffn1_w_out":
            out[k] = _adam_halves(shard[k], sums[k], other[k], *moments, ids, "adam_" + k)
        elif k == "ffn1_w_in":
            for q in range(LAST_PIECES):
                kq = LAST_PIECE % q
                out[k] = _adam_halves(shard[k], sums[kq], other[kq], *moments, ids, "adam_" + kq, q, out.get(k))
        else:
            out[k] = _adam_big(shard[k], sums[k], other[k], *moments, "adam_" + k,
                               after=gb[LAST_PIECE % (LAST_PIECES - 1)][0])

    res = _adam_small([_small_view(k, given[k]) for k in SMALL], [gathered[k] for k in SMALL],
                      [_small_view(k, given["m_" + k]) for k in SMALL],
                      [_small_view(k, given["v_" + k]) for k in SMALL])
    ns = len(SMALL)
    for i, k in enumerate(SMALL):
        out[k] = tuple(res[j * ns + i].reshape(given[k].shape) for j in range(4))
    loss = _sum_loss(gathered["loss_rows"]).reshape(())

    lead = lambda k, j: out[k][j][None] if k in BIG else out[k][j]
    return (loss, grad_x, *[lead(k, 0) for k in order], *[lead(k, 1) for k in order],
            *[lead(k, 2) for k in order], *[lead(k, 3) for k in order])
```

```python
import math

import jax
import jax.numpy as jnp
from jax import lax
from jax.experimental import pallas as pl
from jax.experimental.pallas import tpu as pltpu
from jax.experimental.pallas import tpu_sc as plsc

F32 = jnp.float32
BF16 = jnp.bfloat16
MESH = pl.DeviceIdType.MESH
SDS = jax.ShapeDtypeStruct

D_MODEL = 1024
D_FF = 2816
D_SSM = 512
D_GMLP = 512
SSM_GROUPS = 32
SSM_GROUP_CH = 16
SSM_STATE = 64
SSM_LANES = SSM_GROUPS * SSM_STATE
GMLP_HEADS = 8
GMLP_HEAD_DIM = 64
CHUNK = 128
PLE_DIM = 256
LN_EPS = 1e-5
ALPHA = 2.0 ** 0.25

ADAM_LR = 0.001
ADAM_B1 = 0.9
ADAM_B2 = 0.999
ADAM_EPS = 1e-08
ADAM_WD = 0.01
ADAM_STEP = 10
ADAM_C1 = 1.0 - ADAM_B1 ** ADAM_STEP
ADAM_C2 = 1.0 - ADAM_B2 ** ADAM_STEP

N_DEV = 8
VMEM_LIMIT_BYTES = 56 * 1024 * 1024
FFN_COLS = 1408
S5_BLOCKS = 4
S5_BLOCK_IN = D_SSM // S5_BLOCKS
S5_BLOCK_ST = SSM_LANES // S5_BLOCKS
SCAN_LANES = 512
TN_K_BLOCK = 2048
TN_SMALL_BLOCK = 1024 * 1024
GMLP_CHUNKS = 4
ROW_STEPS = 4
DIRECT_GATHER_BYTES = 0
LAST_PIECES = 2
LAST_PIECE = "ffn1_w_in_q%d"
_G0 = math.sqrt(2.0 / math.pi)
_G1 = 0.044715


def _dot(a, b):
    return jnp.dot(a, b, preferred_element_type=F32)


def _dot_nt(a, b):
    return lax.dot_general(a, b, (((1,), (1,)), ((), ())), preferred_element_type=F32)


def _dot_tn(a, b):
    return lax.dot_general(a, b, (((0,), (0,)), ((), ())), preferred_element_type=F32)


def _sigmoid(x):
    return 1.0 / (1.0 + jnp.exp(-x))


def _gelu(x):
    t = jnp.tanh(_G0 * (x + _G1 * x * x * x))
    return 0.5 * x * (1.0 + t)


def _gelu_grad(x):
    t = jnp.tanh(_G0 * (x + _G1 * x * x * x))
    return 0.5 * (1.0 + t) + 0.5 * x * (1.0 - t * t) * _G0 * (1.0 + 3.0 * _G1 * x * x)


def _ln_fwd(r, g, b):
    mu = jnp.mean(r, axis=-1, keepdims=True)
    d = r - mu
    var = jnp.mean(d * d, axis=-1, keepdims=True)
    rstd = lax.rsqrt(var + LN_EPS)
    xh = d * rstd
    return xh * g + b, xh, rstd


def _ln_bwd(dy, xh, rstd, g):
    dxh = dy * g
    m1 = jnp.mean(dxh, axis=-1, keepdims=True)
    m2 = jnp.mean(dxh * xh, axis=-1, keepdims=True)
    return rstd * (dxh - m1 - xh * m2)


def _resident(shape):
    nd = len(shape)
    return pl.BlockSpec(shape, lambda *_: (0,) * nd, pipeline_mode=pl.Buffered(1))


def _fixed(shape):
    nd = len(shape)
    return pl.BlockSpec(shape, lambda *_: (0,) * nd)


def _rows(tm, cols):
    return pl.BlockSpec((tm, cols), lambda i: (i, 0))


def _cols(rows, tm):
    return pl.BlockSpec((rows, tm), lambda i: (0, i))


def _params(sem):
    return pltpu.CompilerParams(dimension_semantics=sem, vmem_limit_bytes=VMEM_LIMIT_BYTES)


class _Exchange:
    def __init__(self, args, out_shape, sems, start, finish):
        self.args, self.out_shape, self.sems = list(args), list(out_shape), list(sems)
        self.start, self.finish = start, finish
        self.cuts = [(0, len(self.out_shape))]


def _call(body, name, grid, in_specs, out_specs, out_shape, args, scratch=(), sem=None, bg=None, aliases=None):
    aliases = {} if aliases is None else aliases
    in_specs, args = list(in_specs), list(args)
    fn = body
    if bg is not None:
        n_args = len(args)

        def fn(*refs):
            body(*refs[:n_args], *refs[n_args + 1:])

        in_specs.append(pl.BlockSpec(memory_space=pl.ANY))
        args.append(bg)
    res = pl.pallas_call(fn, name=name, grid=grid, out_shape=tuple(out_shape), in_specs=in_specs,
                         out_specs=tuple(out_specs), scratch_shapes=list(scratch),
                         input_output_aliases=aliases, compiler_params=_params(sem))(*args)
    return tuple(res), ()


def _run_exchange_on_sequencer(ex, name, collective_id):
    n_i, n_o = len(ex.args), len(ex.out_shape)

    def body(*refs):
        ins, outs, sems = refs[:n_i], refs[n_i:n_i + n_o], refs[n_i + n_o:]
        x, y, c = lax.axis_index("x"), lax.axis_index("y"), lax.axis_index("c")
        barrier = pltpu.get_barrier_semaphore()
        for peer in [(x, y, 1 - c), (1 - x, y, c), (x, 1 - y, c), (1 - x, 1 - y, c)]:
            pl.semaphore_signal(barrier, inc=1, device_id=peer, device_id_type=MESH)
        pl.semaphore_wait(barrier, 4)
        ex.start(ins, outs, sems)
        ex.finish(ins, outs, sems)

    return tuple(pl.kernel(body, out_type=tuple(ex.out_shape),
                           mesh=plsc.ScalarSubcoreMesh(axis_name="sequencer", num_cores=1),
                           scratch_types=list(ex.sems), name=name,
                           compiler_params=pltpu.CompilerParams(collective_id=collective_id))(*ex.args))


def _join(exchanges):
    cuts = []
    a = o = q = 0
    for e in exchanges:
        cuts.append((a, a + len(e.args), o, o + len(e.out_shape), q, q + len(e.sems)))
        a, o, q = cuts[-1][1], cuts[-1][3], cuts[-1][5]

    def start(ins, outs, sems):
        for e, (a0, a1, o0, o1, q0, q1) in zip(exchanges, cuts):
            e.start(ins[a0:a1], outs[o0:o1], sems[q0:q1])

    def finish(ins, outs, sems):
        for e, (a0, a1, o0, o1, q0, q1) in zip(exchanges, cuts):
            e.finish(ins[a0:a1], outs[o0:o1], sems[q0:q1])

    joined = _Exchange(sum((e.args for e in exchanges), []), sum((e.out_shape for e in exchanges), []),
                       sum((e.sems for e in exchanges), []), start, finish)
    joined.cuts = [(c[2], c[3]) for c in cuts]
    return joined


def _ffn_proj(x, w_in, tm, name, bg=None):
    t = x.shape[0]
    nch = D_FF // FFN_COLS

    def body(x_ref, win_ref, xbt_ref, h_ref, a_ref):
        xb = x_ref[...].astype(BF16)
        xbt_ref[...] = xb.T
        for k in range(nch):
            cg = slice(k * FFN_COLS, (k + 1) * FFN_COLS)
            cu = slice(D_FF + k * FFN_COLS, D_FF + (k + 1) * FFN_COLS)
            hg = _dot(xb, win_ref[k])
            hu = _dot(xb, win_ref[nch + k])
            h_ref[:, cg] = hg.astype(BF16)
            h_ref[:, cu] = hu.astype(BF16)
            a_ref[:, cg] = (hg * _sigmoid(hg) * hu).astype(BF16)

    return _call(
        body, name, (t // tm,),
        [_rows(tm, D_MODEL), _resident((2 * nch, D_MODEL, FFN_COLS))],
        (_cols(D_MODEL, tm), _rows(tm, 2 * D_FF), _rows(tm, D_FF)),
        (SDS((D_MODEL, t), BF16), SDS((t, 2 * D_FF), BF16), SDS((t, D_FF), BF16)),
        (x, w_in), sem=("parallel",), bg=bg)


def _ffn_out(x, a, w_out, g, b, tm, name, bg=None):
    t = x.shape[0]

    def body(x_ref, a_ref, wout_ref, g_ref, b_ref, xn_ref, xh_ref, rstd_ref):
        f = _dot(a_ref[...], wout_ref[...])
        y, xh, rstd = _ln_fwd(ALPHA * x_ref[...] + 0.5 * f, g_ref[...], b_ref[...])
        xn_ref[...] = y
        xh_ref[...] = xh
        rstd_ref[...] = rstd

    return _call(
        body, name, (t // tm,),
        [_rows(tm, D_MODEL), _rows(tm, D_FF), _resident((D_FF, D_MODEL)), _fixed((1, D_MODEL)), _fixed((1, D_MODEL))],
        (_rows(tm, D_MODEL), _rows(tm, D_MODEL), _rows(tm, 1)),
        (SDS((t, D_MODEL), F32), SDS((t, D_MODEL), F32), SDS((t, 1), F32)),
        (x, a, w_out, g, b), sem=("parallel",), bg=bg)


def _ffn_bwd(dxn, xh, rstd, h, w_in, w_out, g, tm, name, bg=None):
    t = dxn.shape[0]
    nch = D_FF // FFN_COLS

    def body(dxn_ref, xh_ref, rstd_ref, h_ref, win_ref, wout_ref, g_ref,
             dx_ref, dh_ref, df_ref, dg_ref, db_ref):
        @pl.when(pl.program_id(0) == 0)
        def _():
            dg_ref[...] = jnp.zeros_like(dg_ref)
            db_ref[...] = jnp.zeros_like(db_ref)

        dy = dxn_ref[...]
        xhv = xh_ref[...]
        dr = _ln_bwd(dy, xhv, rstd_ref[...], g_ref[...])
        dg_ref[...] += jnp.sum(dy * xhv, axis=0, keepdims=True)
        db_ref[...] += jnp.sum(dy, axis=0, keepdims=True)
        df = (0.5 * dr).astype(BF16)
        df_ref[...] = df
        dx = ALPHA * dr
        das = [_dot_nt(df, wout_ref[k * FFN_COLS:(k + 1) * FFN_COLS, :]) for k in range(nch)]
        for k in range(nch):
            cg = slice(k * FFN_COLS, (k + 1) * FFN_COLS)
            cu = slice(D_FF + k * FFN_COLS, D_FF + (k + 1) * FFN_COLS)
            hg = h_ref[:, cg].astype(F32)
            hu = h_ref[:, cu].astype(F32)
            sg = _sigmoid(hg)
            silu = hg * sg
            da = das[k]
            dhu = (da * silu).astype(BF16)
            dhg = (da * hu * (sg * (1.0 + hg * (1.0 - sg)))).astype(BF16)
            dh_ref[:, cg] = dhg
            dh_ref[:, cu] = dhu
            dx = dx + _dot_nt(dhg, win_ref[k]) + _dot_nt(dhu, win_ref[nch + k])
        dx_ref[...] = dx

    return _call(
        body, name, (t // tm,),
        [_rows(tm, D_MODEL), _rows(tm, D_MODEL), _rows(tm, 1), _rows(tm, 2 * D_FF),
         _resident((2 * nch, D_MODEL, FFN_COLS)), _resident((D_FF, D_MODEL)), _fixed((1, D_MODEL))],
        (_rows(tm, D_MODEL), _rows(tm, 2 * D_FF), _rows(tm, D_MODEL),
         _fixed((1, D_MODEL)), _fixed((1, D_MODEL))),
        (SDS((t, D_MODEL), F32), SDS((t, 2 * D_FF), BF16), SDS((t, D_MODEL), BF16),
         SDS((1, D_MODEL), F32), SDS((1, D_MODEL), F32)),
        (dxn, xh, rstd, h, w_in, w_out, g), sem=("arbitrary",), bg=bg)


def _tn_matmul(a, b, name, bm, bn, col_block=0, total_cols=None, prev=None, bg=None, a_cols=None, a_t=False):
    t, m = a.shape[::-1] if a_t else a.shape
    a_first = 0
    if a_cols is not None:
        a_first, m = a_cols[0], a_cols[1] * bm
    n = b.shape[1]
    total_cols = n if total_cols is None else total_cols
    whole = bm * bn <= TN_SMALL_BLOCK and (m // bm) * (n // bn) >= 2
    bk = min(2 * TN_K_BLOCK if whole else TN_K_BLOCK, t)
    nk = t // bk
    n_in = 2 if prev is None else 4

    def body(*refs):
        a_ref, b_ref = refs[0], refs[1]
        o_ref, ob_ref = refs[n_in], refs[n_in + 1]
        k = pl.program_id(2)

        @pl.when(k == 0)
        def _():
            o_ref[...] = jnp.zeros_like(o_ref)

        o_ref[...] += _dot(a_ref[...], b_ref[...]) if a_t else _dot_tn(a_ref[...], b_ref[...])

        @pl.when(k == nk - 1)
        def _():
            ob_ref[...] = o_ref[...].astype(BF16)

    a_spec = (pl.BlockSpec((bm, bk), lambda i, j, k: (i + a_first, k)) if a_t
              else pl.BlockSpec((bk, bm), lambda i, j, k: (k, i + a_first)))
    in_specs = [a_spec, pl.BlockSpec((bk, bn), lambda i, j, k: (k, j))]
    args = [a, b]
    aliases = {}
    if prev is not None:
        in_specs += [pl.BlockSpec(memory_space=pl.ANY), pl.BlockSpec(memory_space=pl.ANY)]
        args += list(prev)
        aliases = {2: 0, 3: 1}
        if any(bg is p for p in prev):
            bg = None
    out_spec = pl.BlockSpec((bm, bn), lambda i, j, k: (i, j + col_block))
    return _call(body, name, (m // bm, n // bn, nk), in_specs, (out_spec, out_spec),
                 (SDS((m, total_cols), F32), SDS((m, total_cols), BF16)), args,
                 sem=("parallel", "parallel", "arbitrary"), bg=bg, aliases=aliases)


def _mixin_fwd(x1, w, tm, bg=None):
    t = x1.shape[0]

    def body(x_ref, w_ref, xbt_ref, za_ref, zuv_ref, gab_ref):
        xb = x_ref[...].astype(BF16)
        xbt_ref[...] = xb.T
        za_ref[...] = _dot(xb, w_ref[:, 0:512]).astype(BF16)
        zuv_ref[...] = _dot(xb, w_ref[:, 512:1536]).astype(BF16)
        gab_ref[...] = _dot(xb, w_ref[:, 1536:3584]).astype(BF16)

    return _call(
        body, "mixin_fwd", (t // tm,),
        [_rows(tm, D_MODEL), _resident((D_MODEL, 3584))],
        (_cols(D_MODEL, tm), _rows(tm, 512), _rows(tm, 1024), _rows(tm, 2048)),
        (SDS((D_MODEL, t), BF16), SDS((t, 512), BF16), SDS((t, 1024), BF16), SDS((t, 2048), BF16)),
        (x1, w), sem=("parallel",), bg=bg)


def _mixin_bwd(dx1a, dza, dzuv, dgab, w, tm, bg=None):
    t = dx1a.shape[0]

    def body(d_ref, dza_ref, dzuv_ref, dgab_ref, w_ref, dx_ref):
        dx_ref[...] = (d_ref[...] + _dot_nt(dza_ref[...], w_ref[:, 0:512])
                       + _dot_nt(dzuv_ref[...], w_ref[:, 512:1536])
                       + _dot_nt(dgab_ref[...], w_ref[:, 1536:3584]))

    return _call(
        body, "mixin_bwd", (t // tm,),
        [_rows(tm, D_MODEL), _rows(tm, 512), _rows(tm, 1024), _rows(tm, 2048), _resident((D_MODEL, 3584))],
        (_rows(tm, D_MODEL),), (SDS((t, D_MODEL), F32),),
        (dx1a, dza, dzuv, dgab, w), sem=("parallel",), bg=bg)


def _unrolled(lo, hi, body, carry):
    for j in range(lo, hi):
        carry = body(j, carry)
    return carry


def _scan_fwd(hr_ref, hi_ref, a_ref, ap_ref, carry_ref, seg, cin_ref):
    for lc in range(SSM_LANES // SCAN_LANES):
        ls = slice(lc * SCAN_LANES, (lc + 1) * SCAN_LANES)
        a_r = jnp.broadcast_to(a_ref[0:1, ls], (8, SCAN_LANES))
        a_i = jnp.broadcast_to(a_ref[1:2, ls], (8, SCAN_LANES))

        def step(j, hc, ls=ls, a_r=a_r, a_i=a_i):
            h_r, h_i = hc
            rows = pl.ds(j * 8, 8)
            n_r = a_r * h_r - a_i * h_i + hr_ref[rows, ls]
            n_i = a_r * h_i + a_i * h_r + hi_ref[rows, ls]
            hr_ref[rows, ls] = n_r
            hi_ref[rows, ls] = n_i
            return n_r, n_i

        zero = jnp.zeros((8, SCAN_LANES), F32)
        f_r, f_i = _unrolled(0, seg, step, (zero, zero))
        c_r = carry_ref[0:1, ls]
        c_i = carry_ref[1:2, ls]
        p_r = ap_ref[0:1, ls]
        p_i = ap_ref[1:2, ls]
        rows_r, rows_i = [], []
        for s in range(8):
            rows_r.append(c_r)
            rows_i.append(c_i)
            c_r, c_i = (f_r[s:s + 1] + p_r * c_r - p_i * c_i,
                        f_i[s:s + 1] + p_r * c_i + p_i * c_r)
        carry_ref[0:1, ls] = c_r
        carry_ref[1:2, ls] = c_i
        cin_r = jnp.concatenate(rows_r, axis=0)
        cin_i = jnp.concatenate(rows_i, axis=0)
        if cin_ref is not None:
            cin_ref[0, :, ls] = cin_r
            cin_ref[1, :, ls] = cin_i

        def fix(j, cc, ls=ls, a_r=a_r, a_i=a_i):
            c_r, c_i = cc
            c_r, c_i = a_r * c_r - a_i * c_i, a_r * c_i + a_i * c_r
            rows = pl.ds(j * 8, 8)
            hr_ref[rows, ls] = hr_ref[rows, ls] + c_r
            hi_ref[rows, ls] = hi_ref[rows, ls] + c_i
            return c_r, c_i

        _unrolled(0, seg, fix, (cin_r, cin_i))


def _scan_bwd(gr_ref, gi_ref, hr_ref, hi_ref, cin_ref, a_ref, ap_ref, rcarry_ref, da_ref, seg):
    for lc in range(SSM_LANES // SCAN_LANES):
        ls = slice(lc * SCAN_LANES, (lc + 1) * SCAN_LANES)
        a_r = jnp.broadcast_to(a_ref[0:1, ls], (8, SCAN_LANES))
        a_i = jnp.broadcast_to(a_ref[1:2, ls], (8, SCAN_LANES))

        def step(t, gc, ls=ls, a_r=a_r, a_i=a_i):
            g_r, g_i = gc
            rows = pl.ds((seg - 1 - t) * 8, 8)
            n_r = gr_ref[rows, ls] + a_r * g_r + a_i * g_i
            n_i = gi_ref[rows, ls] + a_r * g_i - a_i * g_r
            gr_ref[rows, ls] = n_r
            gi_ref[rows, ls] = n_i
            return n_r, n_i

        zero = jnp.zeros((8, SCAN_LANES), F32)
        f_r, f_i = _unrolled(0, seg, step, (zero, zero))
        c_r = rcarry_ref[0:1, ls]
        c_i = rcarry_ref[1:2, ls]
        p_r = ap_ref[0:1, ls]
        p_i = ap_ref[1:2, ls]
        rows_r, rows_i = [None] * 8, [None] * 8
        for s in range(7, -1, -1):
            rows_r[s] = c_r
            rows_i[s] = c_i
            c_r, c_i = (f_r[s:s + 1] + p_r * c_r + p_i * c_i,
                        f_i[s:s + 1] + p_r * c_i - p_i * c_r)
        rcarry_ref[0:1, ls] = c_r
        rcarry_ref[1:2, ls] = c_i
        cin_r = jnp.concatenate(rows_r, axis=0)
        cin_i = jnp.concatenate(rows_i, axis=0)

        def fix_row(j_rows, hp_r, hp_i, cc, ls=ls, a_r=a_r, a_i=a_i):
            c_r, c_i, acc_r, acc_i = cc
            c_r, c_i = a_r * c_r + a_i * c_i, a_r * c_i - a_i * c_r
            g_r = gr_ref[j_rows, ls] + c_r
            g_i = gi_ref[j_rows, ls] + c_i
            gr_ref[j_rows, ls] = g_r
            gi_ref[j_rows, ls] = g_i
            acc_r = acc_r + g_r * hp_r + g_i * hp_i
            acc_i = acc_i + g_i * hp_r - g_r * hp_i
            return c_r, c_i, acc_r, acc_i

        def fix(t, cc, ls=ls, fix_row=fix_row):
            j = seg - 1 - t
            rows = pl.ds(j * 8, 8)
            prev = pl.ds((j - 1) * 8, 8)
            return fix_row(rows, hr_ref[prev, ls], hi_ref[prev, ls], cc)

        cc = _unrolled(0, seg - 1, fix, (cin_r, cin_i, zero, zero))
        _, _, acc_r, acc_i = fix_row(pl.ds(0, 8), cin_ref[0, :, ls], cin_ref[1, :, ls], cc)
        da_ref[0, :, ls] += acc_r
        da_ref[1, :, ls] += acc_i


def _s5_fwd(za, sp, bsz, seq, tb, bg=None):
    nb = seq // tb
    seg = tb // 8
    t = bsz * seq

    def body(za_ref, perm_ref, permt_ref, mre_ref, mim_ref, nre_ref, nim_ref, a_ref, ap_ref,
             dsk_ref, gw_ref, gb_ref, out_ref, outt_ref, y2_ref, car_ref, hr_ref, hi_ref, carry_ref):
        @pl.when(pl.program_id(1) == 0)
        def _():
            carry_ref[...] = jnp.zeros_like(carry_ref)

        car_ref[0] = carry_ref[...]
        up = _dot(perm_ref[...], za_ref[...])
        upb = up.astype(BF16)
        for bb in range(S5_BLOCKS):
            ub = upb[:, bb * S5_BLOCK_IN:(bb + 1) * S5_BLOCK_IN]
            st = slice(bb * S5_BLOCK_ST, (bb + 1) * S5_BLOCK_ST)
            hr_ref[:, st] = _dot(ub, mre_ref[bb])
            hi_ref[:, st] = _dot(ub, mim_ref[bb])
        _scan_fwd(hr_ref, hi_ref, a_ref, ap_ref, carry_ref, seg, None)
        ys = []
        for bb in range(S5_BLOCKS):
            st = slice(bb * S5_BLOCK_ST, (bb + 1) * S5_BLOCK_ST)
            ys.append(_dot(hr_ref[:, st].astype(BF16), nre_ref[bb])
                      - _dot(hi_ref[:, st].astype(BF16), nim_ref[bb]))
        y2 = jnp.concatenate(ys, axis=1) + dsk_ref[...] * up
        y2_ref[...] = y2
        y3 = _gelu(y2)
        gl = _dot(y3.astype(BF16), gw_ref[...]) + gb_ref[...]
        oa = y3 * _sigmoid(gl)
        out = _dot(permt_ref[...], oa.astype(BF16)).astype(BF16)
        out_ref[...] = out
        outt_ref[...] = out.T

    blk = pl.BlockSpec((tb, D_SSM), lambda b, j: (b * nb + j, 0))
    blk_t = pl.BlockSpec((D_SSM, tb), lambda b, j: (0, b * nb + j))
    m_shape = (S5_BLOCKS, S5_BLOCK_IN, S5_BLOCK_ST)
    n_shape = (S5_BLOCKS, S5_BLOCK_ST, S5_BLOCK_IN)
    return _call(
        body, "s5_fwd", (bsz, nb),
        [blk, _fixed((tb, tb)), _fixed((tb, tb)), _fixed(m_shape), _fixed(m_shape), _fixed(n_shape),
         _fixed(n_shape), _fixed((2, SSM_LANES)), _fixed((2, SSM_LANES)), _fixed((1, D_SSM)),
         _fixed((D_SSM, D_SSM)), _fixed((1, D_SSM))],
        (blk, blk_t, blk, pl.BlockSpec((1, 2, SSM_LANES), lambda b, j: (b * nb + j, 0, 0))),
        (SDS((t, D_SSM), BF16), SDS((D_SSM, t), BF16), SDS((t, D_SSM), F32), SDS((bsz * nb, 2, SSM_LANES), F32)),
        (za, sp["perm"], sp["permt"], sp["mre"], sp["mim"], sp["nre"], sp["nim"], sp["a"], sp["ap"],
         sp["dskip"], sp["glu_w"], sp["glu_b"]),
        scratch=[pltpu.VMEM((tb, SSM_LANES), F32), pltpu.VMEM((tb, SSM_LANES), F32),
                 pltpu.VMEM((2, SSM_LANES), F32)],
        sem=("arbitrary", "arbitrary"), bg=bg)


def _s5_bwd(za, y2p, doa, carries, sp, bsz, seq, tb, bg=None):
    nb = seq // tb
    seg = tb // 8
    t = bsz * seq

    def body(za_ref, y2_ref, doa_ref, car_ref, perm_ref, permt_ref, mre_ref, mim_ref, mtre_ref, mtim_ref,
             nre_ref, nim_ref, ntre_ref, ntim_ref, a_ref, ap_ref, dsk_ref, gw_ref, gwt_ref, gb_ref,
             dza_ref, dmr_ref, dmi_ref, dnr_ref, dni_ref, da_ref, ddsk_ref, dgw_ref, dgb_ref,
             hr_ref, hi_ref, gr_ref, gi_ref, cin_ref, carry_ref, rcarry_ref):
        first = jnp.logical_and(pl.program_id(0) == 0, pl.program_id(1) == 0)

        @pl.when(first)
        def _():
            for r in (dmr_ref, dmi_ref, dnr_ref, dni_ref, da_ref, ddsk_ref, dgw_ref, dgb_ref):
                r[...] = jnp.zeros_like(r)

        @pl.when(pl.program_id(1) == 0)
        def _():
            rcarry_ref[...] = jnp.zeros_like(rcarry_ref)

        carry_ref[...] = car_ref[0]
        perm = perm_ref[...]
        up = _dot(perm, za_ref[...])
        upb = up.astype(BF16)
        for bb in range(S5_BLOCKS):
            ub = upb[:, bb * S5_BLOCK_IN:(bb + 1) * S5_BLOCK_IN]
            st = slice(bb * S5_BLOCK_ST, (bb + 1) * S5_BLOCK_ST)
            hr_ref[:, st] = _dot(ub, mre_ref[bb])
            hi_ref[:, st] = _dot(ub, mim_ref[bb])
        _scan_fwd(hr_ref, hi_ref, a_ref, ap_ref, carry_ref, seg, cin_ref)

        y2 = y2_ref[...]
        y3 = _gelu(y2)
        y3b = y3.astype(BF16)
        sg = _sigmoid(_dot(y3b, gw_ref[...]) + gb_ref[...])
        d0 = doa_ref[...]
        d_hi = d0.astype(BF16)
        d1 = d0 - d_hi.astype(F32)
        d_mid = d1.astype(BF16)
        d_lo = (d1 - d_mid.astype(F32)).astype(BF16)
        doap = _dot(perm, d_hi) + _dot(perm, d_mid) + _dot(perm, d_lo)
        dgl = doap * y3 * sg * (1.0 - sg)
        dglb = dgl.astype(BF16)
        dy3 = doap * sg + _dot(dglb, gwt_ref[...])
        dgw_ref[...] += _dot_tn(y3b, dglb)
        dgb_ref[...] += jnp.sum(dgl, axis=0, keepdims=True)
        dy2 = dy3 * _gelu_grad(y2)
        ddsk_ref[...] += jnp.sum(dy2 * up, axis=0, keepdims=True)
        dyb = dy2.astype(BF16)
        for bb in range(S5_BLOCKS):
            dyc = dyb[:, bb * S5_BLOCK_IN:(bb + 1) * S5_BLOCK_IN]
            st = slice(bb * S5_BLOCK_ST, (bb + 1) * S5_BLOCK_ST)
            gr_ref[:, st] = _dot(dyc, ntre_ref[bb])
            gi_ref[:, st] = -_dot(dyc, ntim_ref[bb])
            dnr_ref[bb] += _dot_tn(hr_ref[:, st].astype(BF16), dyc)
            dni_ref[bb] += -_dot_tn(hi_ref[:, st].astype(BF16), dyc)
        _scan_bwd(gr_ref, gi_ref, hr_ref, hi_ref, cin_ref, a_ref, ap_ref, rcarry_ref, da_ref, seg)
        dus = []
        for bb in range(S5_BLOCKS):
            st = slice(bb * S5_BLOCK_ST, (bb + 1) * S5_BLOCK_ST)
            grb = gr_ref[:, st].astype(BF16)
            gib = gi_ref[:, st].astype(BF16)
            dus.append(_dot(grb, mtre_ref[bb]) + _dot(gib, mtim_ref[bb]))
            ub = upb[:, bb * S5_BLOCK_IN:(bb + 1) * S5_BLOCK_IN]
            dmr_ref[bb] += _dot_tn(ub, grb)
            dmi_ref[bb] += _dot_tn(ub, gib)
        du = jnp.concatenate(dus, axis=1) + dy2 * dsk_ref[...]
        dza_ref[...] = _dot(permt_ref[...], du.astype(BF16)).astype(BF16)

    def rev(b, j):
        return (b * nb + (nb - 1 - j), 0)

    blk = pl.BlockSpec((tb, D_SSM), rev)
    m_shape = (S5_BLOCKS, S5_BLOCK_IN, S5_BLOCK_ST)
    n_shape = (S5_BLOCKS, S5_BLOCK_ST, S5_BLOCK_IN)
    return _call(
        body, "s5_bwd", (bsz, nb),
        [blk, blk, blk, pl.BlockSpec((1, 2, SSM_LANES), lambda b, j: (b * nb + (nb - 1 - j), 0, 0)),
         _fixed((tb, tb)), _fixed((tb, tb)), _fixed(m_shape), _fixed(m_shape), _fixed(n_shape), _fixed(n_shape),
         _fixed(n_shape), _fixed(n_shape), _fixed(m_shape), _fixed(m_shape),
         _fixed((2, SSM_LANES)), _fixed((2, SSM_LANES)), _fixed((1, D_SSM)),
         _fixed((D_SSM, D_SSM)), _fixed((D_SSM, D_SSM)), _fixed((1, D_SSM))],
        (blk, _fixed(m_shape), _fixed(m_shape), _fixed(n_shape), _fixed(n_shape),
         _fixed((2, 8, SSM_LANES)), _fixed((1, D_SSM)), _fixed((D_SSM, D_SSM)), _fixed((1, D_SSM))),
        (SDS((t, D_SSM), BF16), SDS(m_shape, F32), SDS(m_shape, F32), SDS(n_shape, F32), SDS(n_shape, F32),
         SDS((2, 8, SSM_LANES), F32), SDS((1, D_SSM), F32), SDS((D_SSM, D_SSM), F32), SDS((1, D_SSM), F32)),
        (za, y2p, doa, carries, sp["perm"], sp["permt"], sp["mre"], sp["mim"], sp["mtre"], sp["mtim"],
         sp["nre"], sp["nim"], sp["ntre"], sp["ntim"], sp["a"], sp["ap"], sp["dskip"], sp["glu_w"],
         sp["glu_wt"], sp["glu_b"]),
        scratch=[pltpu.VMEM((tb, SSM_LANES), F32), pltpu.VMEM((tb, SSM_LANES), F32),
                 pltpu.VMEM((tb, SSM_LANES), F32), pltpu.VMEM((tb, SSM_LANES), F32),
                 pltpu.VMEM((2, 8, SSM_LANES), F32), pltpu.VMEM((2, SSM_LANES), F32),
                 pltpu.VMEM((2, SSM_LANES), F32)],
        sem=("arbitrary", "arbitrary"), bg=bg)


def _gmlp_spatial(ws_ref, vb):
    lane = lax.broadcasted_iota(jnp.int32, (CHUNK, 128), 1)
    parts = []
    for j in range(GMLP_HEADS // 2):
        vp = vb[:, 128 * j:128 * (j + 1)]
        parts.append(jnp.where(lane < GMLP_HEAD_DIM, _dot(ws_ref[2 * j], vp), _dot(ws_ref[2 * j + 1], vp)))
    return jnp.concatenate(parts, axis=1)


def _gmlp_fwd(zuv, ln_g, ln_b, wsm, bias, bg=None):
    t = zuv.shape[0]

    def body(z_ref, g_ref, b_ref, ws_ref, bias_ref, out_ref, outt_ref):
        for ch in range(GMLP_CHUNKS):
            rows = slice(ch * CHUNK, (ch + 1) * CHUNK)
            u = _gelu(z_ref[rows, 0:D_GMLP].astype(F32))
            v0 = _gelu(z_ref[rows, D_GMLP:2 * D_GMLP].astype(F32))
            v, _, _ = _ln_fwd(v0, g_ref[...], b_ref[...])
            s = _gmlp_spatial(ws_ref, v.astype(BF16)) + bias_ref[...]
            out = (u * s).astype(BF16)
            out_ref[rows, :] = out
            outt_ref[:, rows] = out.T

    step = GMLP_CHUNKS * CHUNK
    return _call(
        body, "gmlp_fwd", (t // step,),
        [_rows(step, 2 * D_GMLP), _fixed((1, D_GMLP)), _fixed((1, D_GMLP)),
         _fixed((GMLP_HEADS, CHUNK, CHUNK)), _fixed((CHUNK, D_GMLP))],
        (_rows(step, D_GMLP), _cols(D_GMLP, step)), (SDS((t, D_GMLP), BF16), SDS((D_GMLP, t), BF16)),
        (zuv, ln_g, ln_b, wsm, bias), sem=("parallel",), bg=bg)


def _gmlp_bwd(zuv, dgm, ln_g, ln_b, wsm, wsmt, bias, bg=None):
    t = zuv.shape[0]

    def body(z_ref, d_ref, g_ref, b_ref, ws_ref, wst_ref, bias_ref,
             dz_ref, dws_ref, dbias_ref, dg_ref, db_ref):
        @pl.when(pl.program_id(0) == 0)
        def _():
            for r in (dws_ref, dbias_ref, dg_ref, db_ref):
                r[...] = jnp.zeros_like(r)

        gam = g_ref[...]
        lane = lax.broadcasted_iota(jnp.int32, (CHUNK, 128), 1)
        tril = (lax.broadcasted_iota(jnp.int32, (CHUNK, CHUNK), 0)
                >= lax.broadcasted_iota(jnp.int32, (CHUNK, CHUNK), 1))
        zero_b = jnp.zeros((CHUNK, 128), BF16)
        for ch in range(GMLP_CHUNKS):
            rows = slice(ch * CHUNK, (ch + 1) * CHUNK)
            zu = z_ref[rows, 0:D_GMLP].astype(F32)
            zv = z_ref[rows, D_GMLP:2 * D_GMLP].astype(F32)
            u = _gelu(zu)
            v0 = _gelu(zv)
            v, vhat, rstd = _ln_fwd(v0, gam, b_ref[...])
            vb = v.astype(BF16)
            s = _gmlp_spatial(ws_ref, vb) + bias_ref[...]
            d = d_ref[rows, :]
            dz_ref[rows, 0:D_GMLP] = (d * s * _gelu_grad(zu)).astype(BF16)
            ds = d * u
            dbias_ref[...] += ds
            dsb = ds.astype(BF16)
            parts = []
            for j in range(GMLP_HEADS // 2):
                dsp = dsb[:, 128 * j:128 * (j + 1)]
                vp = vb[:, 128 * j:128 * (j + 1)]
                parts.append(jnp.where(lane < GMLP_HEAD_DIM, _dot(wst_ref[2 * j], dsp),
                                       _dot(wst_ref[2 * j + 1], dsp)))
                lo = jnp.where(lane < GMLP_HEAD_DIM, dsp, zero_b)
                hi = jnp.where(lane < GMLP_HEAD_DIM, zero_b, dsp)
                dws_ref[2 * j] += jnp.where(tril, _dot_nt(lo, vp), 0.0)
                dws_ref[2 * j + 1] += jnp.where(tril, _dot_nt(hi, vp), 0.0)
            dv = jnp.concatenate(parts, axis=1)
            dg_ref[...] += jnp.sum(dv * vhat, axis=0, keepdims=True)
            db_ref[...] += jnp.sum(dv, axis=0, keepdims=True)
            dz_ref[rows, D_GMLP:2 * D_GMLP] = (_ln_bwd(dv, vhat, rstd, gam) * _gelu_grad(zv)).astype(BF16)

    step = GMLP_CHUNKS * CHUNK
    return _call(
        body, "gmlp_bwd", (t // step,),
        [_rows(step, 2 * D_GMLP), _rows(step, D_GMLP), _fixed((1, D_GMLP)), _fixed((1, D_GMLP)),
         _fixed((GMLP_HEADS, CHUNK, CHUNK)), _fixed((GMLP_HEADS, CHUNK, CHUNK)), _fixed((CHUNK, D_GMLP))],
        (_rows(step, 2 * D_GMLP), _fixed((GMLP_HEADS, CHUNK, CHUNK)), _fixed((CHUNK, D_GMLP)),
         _fixed((1, D_GMLP)), _fixed((1, D_GMLP))),
        (SDS((t, 2 * D_GMLP), BF16), SDS((GMLP_HEADS, CHUNK, CHUNK), F32), SDS((CHUNK, D_GMLP), F32),
         SDS((1, D_GMLP), F32), SDS((1, D_GMLP), F32)),
        (zuv, dgm, ln_g, ln_b, wsm, wsmt, bias), sem=("arbitrary",), bg=bg)


def _mixout_fwd(x1, s5o, gm, gab, ua, ub, wmo, g, b, tm, bg=None):
    t = x1.shape[0]

    def body(x_ref, s_ref, m_ref, gab_ref, ua_ref, ub_ref, wmo_ref, g_ref, b_ref,
             xn_ref, xh_ref, rstd_ref):
        ya = _dot(s_ref[...], ua_ref[...])
        yb = _dot(m_ref[...], ub_ref[...])
        mix = (_sigmoid(gab_ref[:, 0:D_MODEL].astype(F32)) * ya
               + _sigmoid(gab_ref[:, D_MODEL:2 * D_MODEL].astype(F32)) * yb)
        r = ALPHA * x_ref[...] + _dot(mix.astype(BF16), wmo_ref[...])
        y, xh, rstd = _ln_fwd(r, g_ref[...], b_ref[...])
        xn_ref[...] = y
        xh_ref[...] = xh
        rstd_ref[...] = rstd

    return _call(
        body, "mixout_fwd", (t // tm,),
        [_rows(tm, D_MODEL), _rows(tm, D_SSM), _rows(tm, D_GMLP), _rows(tm, 2 * D_MODEL),
         _resident((D_SSM, D_MODEL)), _resident((D_GMLP, D_MODEL)), _resident((D_MODEL, D_MODEL)),
         _fixed((1, D_MODEL)), _fixed((1, D_MODEL))],
        (_rows(tm, D_MODEL), _rows(tm, D_MODEL), _rows(tm, 1)),
        (SDS((t, D_MODEL), F32), SDS((t, D_MODEL), F32), SDS((t, 1), F32)),
        (x1, s5o, gm, gab, ua, ub, wmo, g, b), sem=("parallel",), bg=bg)


def _mixout_bwd(dx2, xh, rstd, s5o, gm, gab, ua, ub, wmo, g, tm, bg=None):
    t = dx2.shape[0]

    def body(d_ref, xh_ref, rstd_ref, s_ref, m_ref, gab_ref, ua_ref, ub_ref, wmo_ref, g_ref,
             dx1_ref, dmx_ref, mb_ref, dya_ref, dyb_ref, ds5_ref, dgm_ref, dgab_ref, dg_ref, db_ref):
        @pl.when(pl.program_id(0) == 0)
        def _():
            dg_ref[...] = jnp.zeros_like(dg_ref)
            db_ref[...] = jnp.zeros_like(db_ref)

        dy = d_ref[...]
        xhv = xh_ref[...]
        dr = _ln_bwd(dy, xhv, rstd_ref[...], g_ref[...])
        dg_ref[...] += jnp.sum(dy * xhv, axis=0, keepdims=True)
        db_ref[...] += jnp.sum(dy, axis=0, keepdims=True)
        dx1_ref[...] = ALPHA * dr
        drb = dr.astype(BF16)
        dmx_ref[...] = drb
        dm = _dot_nt(drb, wmo_ref[...])
        ya = _dot(s_ref[...], ua_ref[...])
        yb = _dot(m_ref[...], ub_ref[...])
        sa = _sigmoid(gab_ref[:, 0:D_MODEL].astype(F32))
        sb = _sigmoid(gab_ref[:, D_MODEL:2 * D_MODEL].astype(F32))
        mb_ref[...] = (sa * ya + sb * yb).astype(BF16).T
        dya = (dm * sa).astype(BF16)
        dyb = (dm * sb).astype(BF16)
        dya_ref[...] = dya
        dyb_ref[...] = dyb
        dgab_ref[:, 0:D_MODEL] = (dm * ya * sa * (1.0 - sa)).astype(BF16)
        dgab_ref[:, D_MODEL:2 * D_MODEL] = (dm * yb * sb * (1.0 - sb)).astype(BF16)
        ds5_ref[...] = _dot_nt(dya, ua_ref[...])
        dgm_ref[...] = _dot_nt(dyb, ub_ref[...])

    return _call(
        body, "mixout_bwd", (t // tm,),
        [_rows(tm, D_MODEL), _rows(tm, D_MODEL), _rows(tm, 1), _rows(tm, D_SSM), _rows(tm, D_GMLP),
         _rows(tm, 2 * D_MODEL), _resident((D_SSM, D_MODEL)), _resident((D_GMLP, D_MODEL)),
         _resident((D_MODEL, D_MODEL)), _fixed((1, D_MODEL))],
        (_rows(tm, D_MODEL), _rows(tm, D_MODEL), _cols(D_MODEL, tm), _rows(tm, D_MODEL),
         _rows(tm, D_MODEL), _rows(tm, D_SSM), _rows(tm, D_GMLP), _rows(tm, 2 * D_MODEL),
         _fixed((1, D_MODEL)), _fixed((1, D_MODEL))),
        (SDS((t, D_MODEL), F32), SDS((t, D_MODEL), BF16), SDS((D_MODEL, t), BF16),
         SDS((t, D_MODEL), BF16), SDS((t, D_MODEL), BF16), SDS((t, D_SSM), F32),
         SDS((t, D_GMLP), F32), SDS((t, 2 * D_MODEL), BF16),
         SDS((1, D_MODEL), F32), SDS((1, D_MODEL), F32)),
        (dx2, xh, rstd, s5o, gm, gab, ua, ub, wmo, g), sem=("arbitrary",), bg=bg)


def _ple_loss(x3, p, tgt, wpg, wpp, tm, bg=None):
    t = x3.shape[0]

    def body(x_ref, p_ref, t_ref, wpg_ref, wpp_ref, dx_ref, xb_ref, pb_ref, dq_ref, de_ref, loss_ref):
        @pl.when(pl.program_id(0) == 0)
        def _():
            loss_ref[...] = jnp.zeros_like(loss_ref)

        x3v = x_ref[...]
        xb = x3v.astype(BF16)
        pb = p_ref[...].astype(BF16)
        xb_ref[...] = xb.T
        pb_ref[...] = pb.T
        s = _sigmoid(_dot(xb, wpg_ref[...]))
        e = _dot(pb, wpp_ref[...])
        diff = x3v + s * e - t_ref[...]
        loss_ref[...] += jnp.sum(diff * diff, axis=0, keepdims=True)
        dout = diff * (1.0 / D_MODEL)
        de_ref[...] = (dout * s).astype(BF16)
        dq = (dout * e * s * (1.0 - s)).astype(BF16)
        dq_ref[...] = dq
        dx_ref[...] = dout + _dot_nt(dq, wpg_ref[...])

    return _call(
        body, "ple_loss", (t // tm,),
        [_rows(tm, D_MODEL), _rows(tm, PLE_DIM), _rows(tm, D_MODEL),
         _resident((D_MODEL, D_MODEL)), _resident((PLE_DIM, D_MODEL))],
        (_rows(tm, D_MODEL), _cols(D_MODEL, tm), _cols(PLE_DIM, tm), _rows(tm, D_MODEL),
         _rows(tm, D_MODEL), _fixed((1, D_MODEL))),
        (SDS((t, D_MODEL), F32), SDS((D_MODEL, t), BF16), SDS((PLE_DIM, t), BF16),
         SDS((t, D_MODEL), BF16), SDS((t, D_MODEL), BF16), SDS((1, D_MODEL), F32)),
        (x3, p, tgt, wpg, wpp), sem=("arbitrary",), bg=bg)


def _s5_discretise(lre, lim, log_dt, bre, bim):
    dt = jnp.exp(log_dt)[:, None]
    mag = jnp.exp(lre * dt)
    abr = mag * jnp.cos(lim * dt)
    abi = mag * jnp.sin(lim * dt)
    nr = abr - 1.0
    ni = abi
    den = lre * lre + lim * lim
    cr = ((nr * lre + ni * lim) / den)[..., None]
    ci = ((ni * lre - nr * lim) / den)[..., None]
    return abr, abi, cr * bre - ci * bim, cr * bim + ci * bre


def _block_diag_in(bb):
    v = bb.reshape(S5_BLOCKS, 8, SSM_STATE, SSM_GROUP_CH).transpose(0, 1, 3, 2)
    return jnp.einsum("bgip,gh->bgihp", v, jnp.eye(8, dtype=bb.dtype)).reshape(
        S5_BLOCKS, S5_BLOCK_IN, S5_BLOCK_ST)


def _block_diag_in_t(dm):
    v = dm.reshape(S5_BLOCKS, 8, SSM_GROUP_CH, 8, SSM_STATE)
    d = jnp.einsum("bgihp,gh->bgip", v, jnp.eye(8, dtype=dm.dtype))
    return d.transpose(0, 1, 3, 2).reshape(SSM_GROUPS, SSM_STATE, SSM_GROUP_CH)


def _block_diag_out(cc):
    v = cc.reshape(S5_BLOCKS, 8, SSM_GROUP_CH, SSM_STATE)
    return jnp.einsum("bgip,gh->bgphi", v, jnp.eye(8, dtype=cc.dtype)).reshape(
        S5_BLOCKS, S5_BLOCK_ST, S5_BLOCK_IN)


def _block_diag_out_t(dn):
    v = dn.reshape(S5_BLOCKS, 8, SSM_STATE, 8, SSM_GROUP_CH)
    d = jnp.einsum("bgphi,gh->bgip", v, jnp.eye(8, dtype=dn.dtype))
    return d.reshape(SSM_GROUPS, SSM_GROUP_CH, SSM_STATE)


def _s5_setup(lre, lim, log_dt, bre, bim, cre, cim, d_skip, glu_w, glu_b, tb):
    seg = tb // 8
    abr, abi, bbr, bbi = _s5_discretise(lre, lim, log_dt, bre, bim)
    pr, pi = abr, abi
    for _ in range(int(math.log2(seg))):
        pr, pi = pr * pr - pi * pi, 2.0 * pr * pi
    rows = jnp.arange(tb)
    src = (rows % 8) * seg + rows // 8
    perm = (src[:, None] == jnp.arange(tb)[None, :]).astype(BF16)
    mre = _block_diag_in(bbr)
    mim = _block_diag_in(bbi)
    nre = _block_diag_out(cre)
    nim = _block_diag_out(cim)
    return {
        "perm": perm, "permt": perm.T,
        "mre": mre.astype(BF16), "mim": mim.astype(BF16),
        "mtre": mre.transpose(0, 2, 1).astype(BF16), "mtim": mim.transpose(0, 2, 1).astype(BF16),
        "nre": nre.astype(BF16), "nim": nim.astype(BF16),
        "ntre": nre.transpose(0, 2, 1).astype(BF16), "ntim": nim.transpose(0, 2, 1).astype(BF16),
        "a": jnp.stack([abr.reshape(-1), abi.reshape(-1)]),
        "ap": jnp.stack([pr.reshape(-1), pi.reshape(-1)]),
        "dskip": d_skip.reshape(1, D_SSM), "glu_w": glu_w, "glu_wt": glu_w.T,
        "glu_b": glu_b.reshape(1, D_SSM),
    }


BIG = ("ffn1_w_in", "ffn1_w_out", "mix_w_in", "ssm_glu_w", "up_a", "up_b", "mix_w_out",
       "ffn2_w_in", "ffn2_w_out", "ple_w_proj", "ple_w_gate")
BIG_AXIS = {"ffn1_w_in": 1, "ffn1_w_out": 0, "mix_w_in": 1, "ssm_glu_w": 0, "up_a": 1, "up_b": 1,
            "mix_w_out": 0, "ffn2_w_in": 1, "ffn2_w_out": 0, "ple_w_proj": 1, "ple_w_gate": 0}
SHARD_MAJOR = 2
GATHER_AXIS = dict(BIG_AXIS, ffn1_w_in=SHARD_MAJOR, ffn2_w_in=SHARD_MAJOR)
GATHER_ORDER = (("ffn1_w_in",), ("ffn1_w_out",), ("mix_w_in",), ("ssm_glu_w", "up_a", "up_b", "mix_w_out"),
                ("ffn2_w_in",), ("ffn2_w_out", "ple_w_gate", "ple_w_proj"))
GATHER_FIRST_ID = 1
REDUCE_FIRST_ID = 7
SMALL = ("ln1_g", "ln1_b", "ssm_lambda_re", "ssm_lambda_im", "ssm_log_dt", "ssm_b_re", "ssm_b_im",
         "ssm_c_re", "ssm_c_im", "ssm_d", "ssm_glu_b", "gmlp_ln_g", "gmlp_ln_b", "gmlp_w_s",
         "gmlp_b_s", "ln2_g", "ln2_b", "ln3_g", "ln3_b")
SMALL_VIEW = {"ssm_b_re": (SSM_GROUPS, SSM_STATE * SSM_GROUP_CH), "ssm_b_im": (SSM_GROUPS, SSM_STATE * SSM_GROUP_CH)}


def _small_view(k, a):
    return a.reshape(SMALL_VIEW[k]) if k in SMALL_VIEW else a


def _place():
    return lax.axis_index("x"), lax.axis_index("y"), lax.axis_index("c")


def _other_chips(x, y):
    return [(1 - x, y), (x, 1 - y), (1 - x, 1 - y)]


def _window(ref, shard_shape, axis, chip, half):
    r, c = shard_shape
    hr = r // 2
    if axis == SHARD_MAJOR:
        return ref.at[chip] if half is None else ref.at[chip, pl.ds(half * hr, hr), :]
    if axis == 0:
        if half is None:
            return ref.at[pl.ds(chip * r, r), :]
        return ref.at[pl.ds(chip * r + half * hr, hr), :]
    if half is None:
        return ref.at[:, pl.ds(chip * c, c)]
    return ref.at[pl.ds(half * hr, hr), pl.ds(chip * c, c)]


def _gather_weights(shards, axes):
    n = len(shards)
    shapes = [s.shape for s in shards]
    full = [{0: (4 * r, c), 1: (r, 4 * c), SHARD_MAJOR: (4, r, c)}[ax] for (r, c), ax in zip(shapes, axes)]

    def remote(sems, i, k, src, dst, to):
        return pltpu.make_async_remote_copy(src_ref=src, dst_ref=dst, send_sem=sems[0].at[6 * i + k],
                                            recv_sem=sems[1].at[6 * i + k], device_id=to, device_id_type=MESH)

    def own_copies(ins, outs, sems):
        x, y, c = _place()
        me = 2 * x + y
        cps = []
        for i in range(n):
            hr = shapes[i][0] // 2
            mine = ins[i].at[pl.ds(c * hr, hr), :]
            for j, (cx, cy) in enumerate(_other_chips(x, y)):
                cps.append(remote(sems, i, j, mine, _window(outs[i], shapes[i], axes[i], me, c), (cx, cy, c)))
        local = [pltpu.make_async_copy(ins[i], _window(outs[i], shapes[i], axes[i], me, None), sems[2].at[i])
                 for i in range(n)]
        return cps, local

    def start(ins, outs, sems):
        cps, local = own_copies(ins, outs, sems)
        for cp in local + cps:
            cp.start()

    def finish(ins, outs, sems):
        x, y, c = _place()
        sibling = (x, y, 1 - c)
        passed = []
        for j, (cx, cy) in enumerate(_other_chips(x, y)):
            for i in range(n):
                w = _window(outs[i], shapes[i], axes[i], 2 * cx + cy, c)
                remote(sems, i, j, w, w, (cx, cy, c)).wait_recv()
                cp = remote(sems, i, 3 + j, w, w, sibling)
                cp.start()
                passed.append(cp)
        for j, (cx, cy) in enumerate(_other_chips(x, y)):
            for i in range(n):
                w = _window(outs[i], shapes[i], axes[i], 2 * cx + cy, 1 - c)
                remote(sems, i, 3 + j, w, w, sibling).wait_recv()
        cps, local = own_copies(ins, outs, sems)
        for cp in cps + passed:
            cp.wait_send()
        for cp in local:
            cp.wait()

    return _Exchange(shards, [SDS(f, BF16) for f in full],
                     [pltpu.SemaphoreType.DMA((6 * n,)), pltpu.SemaphoreType.DMA((6 * n,)),
                      pltpu.SemaphoreType.DMA((n,))], start, finish)


def _scatter_grads(parts, shapes, axes):
    n = len(parts)

    def copies(ins, outs, sems):
        x, y, c = _place()
        return [pltpu.make_async_remote_copy(
            src_ref=_window(ins[i], shapes[i], axes[i], 2 * cx + cy, None), dst_ref=outs[i].at[j],
            send_sem=sems[0].at[3 * i + j], recv_sem=sems[1].at[3 * i + j],
            device_id=(cx, cy, c), device_id_type=MESH)
            for i in range(n) for j, (cx, cy) in enumerate(_other_chips(x, y))]

    def start(ins, outs, sems):
        for cp in copies(ins, outs, sems):
            cp.start()

    def finish(ins, outs, sems):
        for cp in copies(ins, outs, sems):
            cp.wait()

    return _Exchange(parts, [SDS((3,) + tuple(s), BF16) for s in shapes],
                     [pltpu.SemaphoreType.DMA((3 * n,)), pltpu.SemaphoreType.DMA((3 * n,))], start, finish)


def _swap_halves(parts, shapes, axes):
    n = len(parts)

    def copies(ins, outs, sems):
        x, y, c = _place()
        cps = []
        for i in range(n):
            r, _ = shapes[i]
            hr = r // 2
            if axes[i] == 0:
                cps += [pltpu.make_async_remote_copy(
                    src_ref=ins[i].at[pl.ds(k * r + (1 - c) * hr, hr), :], dst_ref=outs[i].at[k],
                    send_sem=sems[0].at[i], recv_sem=sems[1].at[i], device_id=(x, y, 1 - c),
                    device_id_type=MESH) for k in range(4)]
            else:
                cps.append(pltpu.make_async_remote_copy(
                    src_ref=ins[i].at[pl.ds((1 - c) * hr, hr), :], dst_ref=outs[i],
                    send_sem=sems[0].at[i], recv_sem=sems[1].at[i], device_id=(x, y, 1 - c),
                    device_id_type=MESH))
        return cps

    def start(ins, outs, sems):
        for cp in copies(ins, outs, sems):
            cp.start()

    def finish(ins, outs, sems):
        x, y, c = _place()
        for i in range(n):
            pltpu.make_async_remote_copy(src_ref=outs[i], dst_ref=outs[i], send_sem=sems[0].at[i],
                                         recv_sem=sems[1].at[i], device_id=(x, y, 1 - c),
                                         device_id_type=MESH).wait()

    out = [SDS((4, r // 2, c), BF16) if ax == 0 else SDS((r // 2, 4 * c), BF16)
           for (r, c), ax in zip(shapes, axes)]
    return _Exchange(parts, out, [pltpu.SemaphoreType.DMA((n,)), pltpu.SemaphoreType.DMA((n,))], start, finish)


def _scatter_halves(pres, shapes):
    n = len(pres)

    def copies(ins, outs, sems):
        x, y, c = _place()
        return [pltpu.make_async_remote_copy(
            src_ref=ins[i].at[1 + j], dst_ref=outs[i].at[j], send_sem=sems[0].at[3 * i + j],
            recv_sem=sems[1].at[3 * i + j], device_id=(cx, cy, c), device_id_type=MESH)
            for i in range(n) for j, (cx, cy) in enumerate(_other_chips(x, y))]

    def start(ins, outs, sems):
        for cp in copies(ins, outs, sems):
            cp.start()

    def finish(ins, outs, sems):
        for cp in copies(ins, outs, sems):
            cp.wait()

    return _Exchange(pres, [SDS((3, r // 2, c), BF16) for r, c in shapes],
                     [pltpu.SemaphoreType.DMA((3 * n,)), pltpu.SemaphoreType.DMA((3 * n,))], start, finish)


def _swap_with_sibling(arrs):
    n = len(arrs)

    def copies(ins, outs, sems):
        x, y, c = _place()
        return [pltpu.make_async_remote_copy(src_ref=ins[i], dst_ref=outs[i], send_sem=sems[0].at[i],
                                             recv_sem=sems[1].at[i], device_id=(x, y, 1 - c),
                                             device_id_type=MESH) for i in range(n)]

    def start(ins, outs, sems):
        for cp in copies(ins, outs, sems):
            cp.start()

    def finish(ins, outs, sems):
        for cp in copies(ins, outs, sems):
            cp.wait()

    return _Exchange(arrs, [SDS(a.shape, a.dtype) for a in arrs],
                     [pltpu.SemaphoreType.DMA((n,)), pltpu.SemaphoreType.DMA((n,))], start, finish)


def _gather_small(arrs):
    n = len(arrs)

    def copy(sems, outs, i, k, block, to, src=None):
        px, py, pc = block
        dst = outs[i].at[4 * px + 2 * py + pc]
        return pltpu.make_async_remote_copy(
            src_ref=dst if src is None else src, dst_ref=dst, send_sem=sems[0].at[7 * i + k],
            recv_sem=sems[1].at[7 * i + k], device_id=to, device_id_type=MESH)

    direct = [math.prod(a.shape) * 4 <= DIRECT_GATHER_BYTES for a in arrs]

    def own_copies(ins, outs, sems):
        x, y, c = _place()
        cps = []
        for i in range(n):
            cps.append(copy(sems, outs, i, 0, (x, y, c), (x, y, 1 - c), src=ins[i]))
            for j, (cx, cy) in enumerate(_other_chips(x, y)):
                cps.append(copy(sems, outs, i, 1 + j, (x, y, c), (cx, cy, c), src=ins[i]))
                if direct[i]:
                    cps.append(copy(sems, outs, i, 4 + j, (x, y, c), (cx, cy, 1 - c), src=ins[i]))
        local = [pltpu.make_async_copy(ins[i], outs[i].at[4 * x + 2 * y + c], sems[2].at[i]) for i in range(n)]
        return cps, local

    def start(ins, outs, sems):
        cps, local = own_copies(ins, outs, sems)
        for cp in local + cps:
            cp.start()

    def finish(ins, outs, sems):
        x, y, c = _place()
        passed = []
        for j, (cx, cy) in enumerate(_other_chips(x, y)):
            for i in range(n):
                copy(sems, outs, i, 1 + j, (cx, cy, c), (x, y, c)).wait_recv()
                if not direct[i]:
                    cp = copy(sems, outs, i, 4 + j, (cx, cy, c), (x, y, 1 - c))
                    cp.start()
                    passed.append(cp)
        for i in range(n):
            copy(sems, outs, i, 0, (x, y, 1 - c), (x, y, c)).wait_recv()
            for j, (cx, cy) in enumerate(_other_chips(x, y)):
                copy(sems, outs, i, 4 + j, (cx, cy, 1 - c), (x, y, c)).wait_recv()
        cps, local = own_copies(ins, outs, sems)
        for cp in cps + passed:
            cp.wait_send()
        for cp in local:
            cp.wait()

    return _Exchange(arrs, [SDS((N_DEV,) + a.shape, F32) for a in arrs],
                     [pltpu.SemaphoreType.DMA((7 * n,)), pltpu.SemaphoreType.DMA((7 * n,)),
                      pltpu.SemaphoreType.DMA((n,))], start, finish)


def _local_step(x, p, tgt, wb, ws, shards=None, opt=None):
    bsz, seq, _ = x.shape
    t = bsz * seq
    tm = min(256, t)
    tb = min(256, seq)
    x0 = x.reshape(t, D_MODEL)
    p0 = p.reshape(t, PLE_DIM)
    tg = tgt.reshape(t, D_MODEL)
    row = lambda v: v.reshape(1, -1)
    dist = shards is not None
    wb = dict(wb)
    recv, sums, other, gathered = {}, {}, {}, {}
    gb = {}
    gs = {}
    shape_of, axis_of = {}, {}
    chip = None
    if dist:
        shape_of = {k: tuple(shards[k].shape) for k in BIG}
        axis_of = dict(BIG_AXIS)
        for q in range(LAST_PIECES):
            shape_of[LAST_PIECE % q] = (D_MODEL // LAST_PIECES, shape_of["ffn1_w_in"][1])
            axis_of[LAST_PIECE % q] = 1
        xi, yi, ci = _place()
        chip = (2 * xi + yi).astype(jnp.int32).reshape(1)
        ids = jnp.stack([2 * xi + yi] + [2 * cx + cy for cx, cy in _other_chips(xi, yi)] + [ci]).astype(jnp.int32)
    halfbuf, pre = {}, {}

    def gather(names):
        return _gather_weights([shards[k] for k in names], [GATHER_AXIS[k] for k in names]) if dist else None

    def exchange(scat=(), swap=(), halves=(), scat2=(), swap2=(), extra=None, after=None):
        if not dist:
            return None, []
        after = order[0] if after is None else after
        parts, tags = [], []
        if scat:
            parts.append(_scatter_grads([gb[k][1] for k in scat], [shape_of[k] for k in scat],
                                        [axis_of[k] for k in scat]))
            tags.append((recv, scat))
        if swap:
            for k in swap:
                sums[k] = order[0] = _sum_blocks(gb[k][0], recv[k], shape_of[k], axis_of[k], chip, "sum_" + k,
                                                 order[0])
            parts.append(_swap_with_sibling([sums[k] for k in swap]))
            tags.append((other, swap))
        if halves:
            parts.append(_swap_halves([gb[k][1] for k in halves], [shape_of[k] for k in halves],
                                      [axis_of[k] for k in halves]))
            tags.append((halfbuf, halves))
        if scat2:
            for k in scat2:
                pre[k] = _presum(gb[k][0], halfbuf[k], shape_of[k], axis_of[k], ids, "presum_" + k, order[0])
                order[0] = pre[k][0]
            parts.append(_scatter_halves([pre[k][1] for k in scat2], [shape_of[k] for k in scat2]))
            tags.append((recv, scat2))
        if swap2:
            for k in swap2:
                sums[k] = order[0] = _sum_half(pre[k][0], recv[k], "sum_" + k, order[0])
            parts.append(_swap_with_sibling([sums[k] for k in swap2]))
            tags.append((other, swap2))
        if extra is not None:
            parts.append(extra[0])
            tags.append((extra[1], extra[2]))
        return (_join(parts), tags) if parts else (None, [])

    def take(ex_tags, got):
        ex, tags = ex_tags
        if ex is not None:
            for (dst, names), (o0, o1) in zip(tags, ex.cuts):
                dst.update(zip(names, got[o0:o1]))

    order = [None]

    def ordered(builder, *args, **kw):
        res = builder(*args, bg=order[0] if dist else None, **kw)
        order[0] = res[0][0]
        return res

    launched = []

    def launch(ex_tags):
        if ex_tags[0] is not None:
            n = len(launched)
            launched.append(n)
            take(ex_tags, _run_exchange_on_sequencer(ex_tags[0], "reduce_%d" % n, REDUCE_FIRST_ID + n))

    small_shape = {k: _small_view(k, v).shape for k, v in ws.items()}
    small_shape["loss_rows"] = (1, D_MODEL)
    ws = {k: v if (v.ndim == 2 and k != "ssm_log_dt") else v[0] for k, v in ws.items()}
    tril = jnp.tril(jnp.ones((CHUNK, CHUNK), dtype=bool))
    wsm = jnp.where(tril[None], ws["gmlp_w_s"], 0.0)
    wsm_b = wsm.astype(BF16)
    wsmt_b = wsm.transpose(0, 2, 1).astype(BF16)
    bias = jnp.repeat(ws["gmlp_b_s"].T, GMLP_HEAD_DIM, axis=1)

    tf = min(512, t)
    if dist:
        for gi, names in enumerate(GATHER_ORDER):
            wb.update(zip(names, _run_exchange_on_sequencer(gather(names), "gather_%d" % gi, GATHER_FIRST_ID + gi)))
    (x0b, h1, a1), _ = _ffn_proj(x0, wb["ffn1_w_in"], tf, "ffn1_proj")
    (x1, xh1, rstd1), _ = _ffn_out(x0, a1, wb["ffn1_w_out"], row(ws["ln1_g"]), row(ws["ln1_b"]), tf, "ffn1_out")
    sp = _s5_setup(ws["ssm_lambda_re"], ws["ssm_lambda_im"], ws["ssm_log_dt"], ws["ssm_b_re"],
                   ws["ssm_b_im"], ws["ssm_c_re"], ws["ssm_c_im"], ws["ssm_d"], wb["ssm_glu_w"],
                   ws["ssm_glu_b"], tb)
    (x1b, za, zuv, gab), _ = _mixin_fwd(x1, wb["mix_w_in"], tf)
    (s5o, s5ot, y2p, carries), _ = _s5_fwd(za, sp, bsz, seq, tb)
    (gm, gmt), _ = _gmlp_fwd(zuv, row(ws["gmlp_ln_g"]), row(ws["gmlp_ln_b"]), wsm_b, bias)
    (x2, xh2, rstd2), _ = _mixout_fwd(x1, s5o, gm, gab, wb["up_a"], wb["up_b"], wb["mix_w_out"],
                                           row(ws["ln2_g"]), row(ws["ln2_b"]), tf)
    (x2b, h2, a2), _ = _ffn_proj(x2, wb["ffn2_w_in"], tf, "ffn2_proj")
    (x3, xh3, rstd3), _ = _ffn_out(x2, a2, wb["ffn2_w_out"], row(ws["ln3_g"]), row(ws["ln3_b"]), tf, "ffn2_out")
    (dx3, x3b, pb, dq, de, loss_rows), _ = _ple_loss(x3, p0, tg, wb["ple_w_gate"], wb["ple_w_proj"], tf)
    order[0] = dx3
    gb["ple_w_gate"], _ = ordered(_tn_matmul, x3b, dq, "dw_ple_gate", 1024, 1024, a_t=True)
    gb["ple_w_proj"], _ = ordered(_tn_matmul, pb, de, "dw_ple_proj", 256, 1024, a_t=True)
    launch(exchange(scat=("ple_w_gate", "ple_w_proj")))
    (dx2, dh2, df2, gs["ln3_g"], gs["ln3_b"]), _ = ordered(
        _ffn_bwd, dx3, xh3, rstd3, h2, wb["ffn2_w_in"], wb["ffn2_w_out"], row(ws["ln3_g"]), tm, "ffn2_bwd")
    gb["ffn2_w_out"], _ = ordered(_tn_matmul, a2, df2, "dw_ffn2_out", 1408, 1024)
    launch(exchange(scat=("ffn2_w_out",)))
    gb["ffn2_w_in"], _ = ordered(_tn_matmul, x2b, dh2, "dw_ffn2_in", 1024, 1408, a_t=True)
    launch(exchange(scat=("ffn2_w_in",), swap=("ple_w_gate", "ple_w_proj")))
    (dx1a, dmx, mb, dya, dyb, ds5, dgm, dgab, gs["ln2_g"], gs["ln2_b"]), _ = ordered(
        _mixout_bwd, dx2, xh2, rstd2, s5o, gm, gab, wb["up_a"], wb["up_b"], wb["mix_w_out"], row(ws["ln2_g"]), tm)
    gb["mix_w_out"], _ = ordered(_tn_matmul, mb, dmx, "dw_mix_out", 1024, 1024, a_t=True)
    gb["up_a"], _ = ordered(_tn_matmul, s5ot, dya, "dw_up_a", 512, 1024, a_t=True)
    gb["up_b"], _ = ordered(_tn_matmul, gmt, dyb, "dw_up_b", 512, 1024, a_t=True)
    launch(exchange(scat=("mix_w_out", "up_a", "up_b"), swap=("ffn2_w_out",)))
    (dza, dmr, dmi, dnr, dni, da, ddsk, dgw, dgb), _ = ordered(_s5_bwd, za, y2p, ds5, carries, sp, bsz, seq, tb)
    gb["ssm_glu_w"] = (dgw, dgw.astype(BF16))
    launch(exchange(scat=("ssm_glu_w",), swap=("ffn2_w_in",)))
    (dzuv, dws, dbias, gs["gmlp_ln_g"], gs["gmlp_ln_b"]), _ = ordered(
        _gmlp_bwd, zuv, dgm, row(ws["gmlp_ln_g"]), row(ws["gmlp_ln_b"]), wsm_b, wsmt_b, bias)
    (dx1,), _ = ordered(_mixin_bwd, dx1a, dza, dzuv, dgab, wb["mix_w_in"], tf)
    g_mi, _ = ordered(_tn_matmul, x1b, dza, "dw_mix_in_a", 1024, 512, 0, 3584, a_t=True)
    g_mi, _ = ordered(_tn_matmul, x1b, dzuv, "dw_mix_in_uv", 1024, 512, 1, 3584, g_mi, a_t=True)
    gb["mix_w_in"], _ = ordered(_tn_matmul, x1b, dgab, "dw_mix_in_g", 1024, 512, 3, 3584, g_mi, a_t=True)
    launch(exchange(swap=("mix_w_out", "up_a", "up_b", "ssm_glu_w")))

    d_abr = da[0].sum(axis=0).reshape(SSM_GROUPS, SSM_STATE)
    d_abi = da[1].sum(axis=0).reshape(SSM_GROUPS, SSM_STATE)
    _, vjp = jax.vjp(_s5_discretise, ws["ssm_lambda_re"], ws["ssm_lambda_im"], ws["ssm_log_dt"],
                     ws["ssm_b_re"], ws["ssm_b_im"])
    (gs["ssm_lambda_re"], gs["ssm_lambda_im"], gs["ssm_log_dt"], gs["ssm_b_re"], gs["ssm_b_im"]) = vjp(
        (d_abr, d_abi, _block_diag_in_t(dmr), _block_diag_in_t(dmi)))
    gs["ssm_c_re"] = _block_diag_out_t(dnr)
    gs["ssm_c_im"] = _block_diag_out_t(dni)
    gs["ssm_d"] = ddsk
    gs["ssm_glu_b"] = dgb
    gs["gmlp_w_s"] = dws
    gs["gmlp_b_s"] = dbias.reshape(CHUNK, GMLP_HEADS, GMLP_HEAD_DIM).sum(axis=-1).T
    gs["loss_rows"] = loss_rows

    def small_gather(names):
        return (_gather_small([gs[k].reshape(small_shape[k]) for k in names]), gathered, names) if dist else None

    late = ("ln1_g", "ln1_b")
    launch(exchange(scat=("mix_w_in",), extra=small_gather(tuple(k for k in SMALL + ("loss_rows",) if k not in late))))
    (dx0, dh1, df1, gs["ln1_g"], gs["ln1_b"]), _ = ordered(
        _ffn_bwd, dx1, xh1, rstd1, h1, wb["ffn1_w_in"], wb["ffn1_w_out"], row(ws["ln1_g"]), tm, "ffn1_bwd")
    grad_x = dx0.reshape(bsz, seq, D_MODEL)
    if not dist:
        gb["ffn1_w_out"], _ = _tn_matmul(a1, df1, "dw_ffn1_out", 1408, 1024)
        gb["ffn1_w_in"], _ = _tn_matmul(x0b, dh1, "dw_ffn1_in", 1024, 1408, a_t=True)
        return (loss_rows, grad_x, gb, {k: gs[k].reshape(small_shape[k]) for k in SMALL}, sums, other, gathered,
                None, {})
    launch(exchange(extra=small_gather(late)))
    gb["ffn1_w_out"], _ = ordered(_tn_matmul, a1, df1, "dw_ffn1_out", 1408, 1024)
    last = ["ffn1_w_out"] + [LAST_PIECE % q for q in range(LAST_PIECES)]
    fillers = (("ffn2_w_in", "mix_w_in", "ple_w_gate"),
               ("ffn2_w_out", "mix_w_out", "up_a", "up_b", "ssm_glu_w", "ple_w_proj"))
    out = {}
    for i in range(1, len(last) + 3):
        stage = lambda d: tuple(last[i - d:i - d + 1]) if 0 <= i - d < len(last) else ()
        launch(exchange(halves=stage(1), scat2=stage(2), swap2=stage(3), swap=("mix_w_in",) if i == 2 else ()))
        if i < len(last):
            gb[last[i]], _ = ordered(_tn_matmul, x0b, dh1, "dw_" + last[i], D_MODEL // LAST_PIECES, 1408,
                                     a_cols=(i - 1, 1), a_t=True)
        elif i - len(last) < len(fillers):
            for k in fillers[i - len(last)]:
                w, m, v = opt[k]
                out[k] = _adam_big(w, sums[k], other[k], m, v, "adam_" + k, after=order[0])
                order[0] = out[k][1]
    return loss_rows, grad_x, gb, gs, sums, other, gathered, ids, out


def _adamw(w, g, m, v):
    m = ADAM_B1 * m + (1.0 - ADAM_B1) * g
    v = ADAM_B2 * v + (1.0 - ADAM_B2) * (g * g)
    m_hat = m / ADAM_C1
    v_hat = v / ADAM_C2
    delta = -ADAM_LR * (m_hat / (jnp.sqrt(v_hat) + ADAM_EPS) + ADAM_WD * w)
    return delta, m, v


def _pinned(after):
    return ([pl.BlockSpec(memory_space=pl.ANY)], [after]) if after is not None else ([], [])


def _sum_blocks(part, recv, shape, axis, chip, name, after=None):
    r, c = shape
    rb = r // ROW_STEPS

    def body(chip_ref, p_ref, r_ref, *rest):
        rest[-1][...] = (p_ref[...] + r_ref[0].astype(F32) + r_ref[1].astype(F32) + r_ref[2].astype(F32))

    if axis == 0:
        own = pl.BlockSpec((rb, c), lambda i, k: (k[0] * ROW_STEPS + i, 0))
    else:
        own = pl.BlockSpec((rb, c), lambda i, k: (i, k[0]))
    pin_specs, pin_args = _pinned(after)
    grid_spec = pltpu.PrefetchScalarGridSpec(
        num_scalar_prefetch=1, grid=(ROW_STEPS,),
        in_specs=[own, pl.BlockSpec((3, rb, c), lambda i, k: (0, i, 0))] + pin_specs,
        out_specs=pl.BlockSpec((rb, c), lambda i, k: (i, 0)))
    return pl.pallas_call(body, name=name, out_shape=SDS((r, c), F32), grid_spec=grid_spec,
                          compiler_params=_params(("parallel",)))(chip, part, recv, *pin_args)


def _presum(part, half, shape, axis, ids, name, after=None):
    r, c = shape
    rb = r // 4

    def body(ids_ref, p_ref, h_ref, *rest):
        of_ref, ob_ref = rest[-2:]
        s = p_ref[...] + h_ref[...].astype(F32)
        ob_ref[...] = s.astype(BF16)

        @pl.when(pl.program_id(1) == 0)
        def _():
            of_ref[...] = s

    if axis == 0:
        p_spec = pl.BlockSpec((rb, c), lambda i, t, ids: (ids[t] * 4 + ids[4] * 2 + i, 0))
        h_spec = pl.BlockSpec((None, rb, c), lambda i, t, ids: (ids[t], i, 0))
    else:
        p_spec = pl.BlockSpec((rb, c), lambda i, t, ids: (ids[4] * 2 + i, ids[t]))
        h_spec = pl.BlockSpec((rb, c), lambda i, t, ids: (i, ids[t]))
    pin_specs, pin_args = _pinned(after)
    grid_spec = pltpu.PrefetchScalarGridSpec(
        num_scalar_prefetch=1, grid=(2, 4), in_specs=[p_spec, h_spec] + pin_specs,
        out_specs=(pl.BlockSpec((rb, c), lambda i, t, ids: (i, 0)),
                   pl.BlockSpec((None, rb, c), lambda i, t, ids: (t, i, 0))))
    return pl.pallas_call(body, name=name, out_shape=(SDS((r // 2, c), F32), SDS((4, r // 2, c), BF16)),
                          grid_spec=grid_spec,
                          compiler_params=_params(("parallel", "arbitrary")))(ids, part, half, *pin_args)


def _sum_half(pre, recv, name, after=None):
    hr, c = pre.shape
    rb = hr // 2

    def body(p_ref, r_ref, *rest):
        rest[-1][...] = (p_ref[...] + r_ref[0].astype(F32) + r_ref[1].astype(F32) + r_ref[2].astype(F32))

    spec = pl.BlockSpec((rb, c), lambda i: (i, 0))
    pin_specs, pin_args = _pinned(after)
    return pl.pallas_call(body, name=name, grid=(2,), out_shape=SDS((hr, c), F32),
                          in_specs=[spec, pl.BlockSpec((3, rb, c), lambda i: (0, i, 0))] + pin_specs,
                          out_specs=spec, compiler_params=_params(("parallel",)))(pre, recv, *pin_args)


def _adam_halves(w, mine, oth, m, v, ids, name, piece=0, prev=None):
    r, c = w.shape
    rb = mine.shape[0] // 2

    def body(ids_ref, w_ref, a_ref, b_ref, m_ref, v_ref, *rest):
        g_ref, d_ref, nm_ref, nv_ref = rest[-4:]
        g = jnp.where(pl.program_id(0) // 2 == ids_ref[4], a_ref[...], b_ref[...])
        g_ref[...] = g
        d_ref[...], nm_ref[...], nv_ref[...] = _adamw(w_ref[...], g, m_ref[...], v_ref[...])

    whole = pl.BlockSpec((rb, c), lambda i, ids: (i + 4 * piece, 0))
    part = pl.BlockSpec((rb, c), lambda i, ids: (i % 2, 0))
    in_specs = [whole, part, part, whole, whole]
    args = [w, mine, oth, m, v]
    aliases = {}
    if prev is not None:
        in_specs += [pl.BlockSpec(memory_space=pl.ANY)] * 4
        args += list(prev)
        aliases = {6: 0, 7: 1, 8: 2, 9: 3}
    grid_spec = pltpu.PrefetchScalarGridSpec(num_scalar_prefetch=1, grid=(4,), in_specs=in_specs,
                                             out_specs=(whole,) * 4)
    return pl.pallas_call(body, name=name, out_shape=tuple(SDS((r, c), F32) for _ in range(4)),
                          grid_spec=grid_spec, input_output_aliases=aliases,
                          compiler_params=_params(("parallel",)))(ids, *args)


def _adam_big(w, ga, gb, m, v, name, piece=0, prev=None, after=None):
    r, c = w.shape
    pr = ga.shape[0]
    steps = ROW_STEPS if pr == r else 2
    rb = pr // steps
    off = piece * steps

    def body(w_ref, ga_ref, gb_ref, m_ref, v_ref, *rest):
        g_ref, d_ref, nm_ref, nv_ref = rest[-4:]
        g = ga_ref[...] + gb_ref[...]
        g_ref[...] = g
        d_ref[...], nm_ref[...], nv_ref[...] = _adamw(w_ref[...], g, m_ref[...], v_ref[...])

    whole = pl.BlockSpec((rb, c), lambda i: (i + off, 0))
    part = pl.BlockSpec((rb, c), lambda i: (i, 0))
    in_specs = [whole, part, part, whole, whole]
    args = [w, ga, gb, m, v]
    aliases = {}
    if prev is not None:
        in_specs += [pl.BlockSpec(memory_space=pl.ANY)] * 4
        args += list(prev)
        aliases = {5: 0, 6: 1, 7: 2, 8: 3}
    if after is not None:
        in_specs.append(pl.BlockSpec(memory_space=pl.ANY))
        args.append(after)
    return pl.pallas_call(
        body, name=name, grid=(steps,), out_shape=tuple(SDS((r, c), F32) for _ in range(4)),
        in_specs=in_specs, out_specs=(whole,) * 4, input_output_aliases=aliases,
        compiler_params=_params(("parallel",)),
    )(*args)


def _adam_small(ws, gathered, ms, vs):
    n = len(ws)

    def body(*refs):
        w_refs, g_refs, m_refs, v_refs = refs[:n], refs[n:2 * n], refs[2 * n:3 * n], refs[3 * n:4 * n]
        outs = refs[4 * n:]
        for i in range(n):
            g = g_refs[i][0]
            for d in range(1, N_DEV):
                g = g + g_refs[i][d]
            delta, nm, nv = _adamw(w_refs[i][...], g, m_refs[i][...], v_refs[i][...])
            outs[i][...] = g
            outs[n + i][...] = delta
            outs[2 * n + i][...] = nm
            outs[3 * n + i][...] = nv

    vmem = pl.BlockSpec(memory_space=pltpu.VMEM)
    shapes = [w.shape for w in ws]
    return pl.pallas_call(
        body, name="adam_small", out_shape=tuple(SDS(s, F32) for s in shapes * 4),
        in_specs=[vmem] * (4 * n), out_specs=tuple([vmem] * (4 * n)),
        compiler_params=pltpu.CompilerParams(vmem_limit_bytes=VMEM_LIMIT_BYTES),
    )(*ws, *gathered, *ms, *vs)


def _sum_loss(gathered):
    def body(g_ref, o_ref):
        tot = g_ref[0]
        for d in range(1, N_DEV):
            tot = tot + g_ref[d]
        o_ref[...] = (0.5 / D_MODEL) * jnp.sum(tot, axis=1, keepdims=True)

    vmem = pl.BlockSpec(memory_space=pltpu.VMEM)
    return pl.pallas_call(body, name="sum_loss", out_shape=SDS((1, 1), F32), in_specs=[vmem],
                          out_specs=vmem)(gathered)


def kernel(x, p, ffn1_w_in, ffn1_w_out, ln1_g, ln1_b, mix_w_in, ssm_lambda_re, ssm_lambda_im, ssm_log_dt, ssm_b_re, ssm_b_im, ssm_c_re, ssm_c_im, ssm_d, ssm_glu_w, ssm_glu_b, gmlp_ln_g, gmlp_ln_b, gmlp_w_s, gmlp_b_s, up_a, up_b, mix_w_out, ln2_g, ln2_b, ffn2_w_in, ffn2_w_out, ln3_g, ln3_b, ple_w_proj, ple_w_gate, loss_target, m_ffn1_w_in, m_ffn1_w_out, m_ln1_g, m_ln1_b, m_mix_w_in, m_ssm_lambda_re, m_ssm_lambda_im, m_ssm_log_dt, m_ssm_b_re, m_ssm_b_im, m_ssm_c_re, m_ssm_c_im, m_ssm_d, m_ssm_glu_w, m_ssm_glu_b, m_gmlp_ln_g, m_gmlp_ln_b, m_gmlp_w_s, m_gmlp_b_s, m_up_a, m_up_b, m_mix_w_out, m_ln2_g, m_ln2_b, m_ffn2_w_in, m_ffn2_w_out, m_ln3_g, m_ln3_b, m_ple_w_proj, m_ple_w_gate, v_ffn1_w_in, v_ffn1_w_out, v_ln1_g, v_ln1_b, v_mix_w_in, v_ssm_lambda_re, v_ssm_lambda_im, v_ssm_log_dt, v_ssm_b_re, v_ssm_b_im, v_ssm_c_re, v_ssm_c_im, v_ssm_d, v_ssm_glu_w, v_ssm_glu_b, v_gmlp_ln_g, v_gmlp_ln_b, v_gmlp_w_s, v_gmlp_b_s, v_up_a, v_up_b, v_mix_w_out, v_ln2_g, v_ln2_b, v_ffn2_w_in, v_ffn2_w_out, v_ln3_g, v_ln3_b, v_ple_w_proj, v_ple_w_gate):
    given = dict(locals())
    order = ("ffn1_w_in", "ffn1_w_out", "ln1_g", "ln1_b", "mix_w_in", "ssm_lambda_re", "ssm_lambda_im",
             "ssm_log_dt", "ssm_b_re", "ssm_b_im", "ssm_c_re", "ssm_c_im", "ssm_d", "ssm_glu_w", "ssm_glu_b",
             "gmlp_ln_g", "gmlp_ln_b", "gmlp_w_s", "gmlp_b_s", "up_a", "up_b", "mix_w_out", "ln2_g", "ln2_b",
             "ffn2_w_in", "ffn2_w_out", "ln3_g", "ln3_b", "ple_w_proj", "ple_w_gate")
    assert set(order) == set(BIG + SMALL)

    shard = {k: given[k][0] for k in BIG}
    shard_b = {k: shard[k].astype(BF16) for k in BIG}
    opt = {k: (shard[k], given["m_" + k][0], given["v_" + k][0]) for k in BIG}
    loss_rows, grad_x, gb, gs, sums, other, gathered, ids, out = _local_step(
        x, given["p"][0], loss_target, {}, {k: given[k] for k in SMALL}, shard_b, opt)

    out = dict(out)
    for k in BIG:
        if k in out:
            continue
        moments = (given["m_" + k][0], given["v_" + k][0])
        if k == "ffn1_w_out":
            out[k] = _adam_halves(shard[k], sums[k], other[k], *moments, ids, "adam_" + k)
        elif k == "ffn1_w_in":
            for q in range(LAST_PIECES):
                kq = LAST_PIECE % q
                out[k] = _adam_halves(shard[k], sums[kq], other[kq], *moments, ids, "adam_" + kq, q, out.get(k))
        else:
            out[k] = _adam_big(shard[k], sums[k], other[k], *moments, "adam_" + k,
                               after=gb[LAST_PIECE % (LAST_PIECES - 1)][0])

    res = _adam_small([_small_view(k, given[k]) for k in SMALL], [gathered[k] for k in SMALL],
                      [_small_view(k, given["m_" + k]) for k in SMALL],
                      [_small_view(k, given["v_" + k]) for k in SMALL])
    ns = len(SMALL)
    for i, k in enumerate(SMALL):
        out[k] = tuple(res[j * ns + i].reshape(given[k].shape) for j in range(4))
    loss = _sum_loss(gathered["loss_rows"]).reshape(())

    lead = lambda k, j: out[k][j][None] if k in BIG else out[k][j]
    return (loss, grad_x, *[lead(k, 0) for k in order], *[lead(k, 1) for k in order],
            *[lead(k, 2) for k in order], *[lead(k, 3) for k in order])
```

```python
import math

import jax
import jax.numpy as jnp
from jax import lax
from jax.experimental import pallas as pl
from jax.experimental.pallas import tpu as pltpu
from jax.experimental.pallas import tpu_sc as plsc

F32 = jnp.float32
BF16 = jnp.bfloat16
MESH = pl.DeviceIdType.MESH
SDS = jax.ShapeDtypeStruct

D_MODEL = 1024
D_FF = 2816
D_SSM = 512
D_GMLP = 512
SSM_GROUPS = 32
SSM_GROUP_CH = 16
SSM_STATE = 64
SSM_LANES = SSM_GROUPS * SSM_STATE
GMLP_HEADS = 8
GMLP_HEAD_DIM = 64
CHUNK = 128
PLE_DIM = 256
LN_EPS = 1e-5
ALPHA = 2.0 ** 0.25

ADAM_LR = 0.001
ADAM_B1 = 0.9
ADAM_B2 = 0.999
ADAM_EPS = 1e-08
ADAM_WD = 0.01
ADAM_STEP = 10
ADAM_C1 = 1.0 - ADAM_B1 ** ADAM_STEP
ADAM_C2 = 1.0 - ADAM_B2 ** ADAM_STEP

N_DEV = 8
VMEM_LIMIT_BYTES = 56 * 1024 * 1024
FFN_COLS = 1408
S5_BLOCKS = 4
S5_BLOCK_IN = D_SSM // S5_BLOCKS
S5_BLOCK_ST = SSM_LANES // S5_BLOCKS
SCAN_LANES = 512
S5_TIME_BLOCK = 512
TN_K_BLOCK = 2048
TN_SMALL_BLOCK = 1024 * 1024
GMLP_CHUNKS = 4
ROW_STEPS = 4
DIRECT_GATHER_BYTES = 0
LAST_PIECES = 2
LAST_PIECE = "ffn1_w_in_q%d"
_G0 = math.sqrt(2.0 / math.pi)
_G1 = 0.044715


def _dot(a, b):
    return jnp.dot(a, b, preferred_element_type=F32)


def _dot_nt(a, b):
    return lax.dot_general(a, b, (((1,), (1,)), ((), ())), preferred_element_type=F32)


def _dot_tn(a, b):
    return lax.dot_general(a, b, (((0,), (0,)), ((), ())), preferred_element_type=F32)


def _sigmoid(x):
    return 1.0 / (1.0 + jnp.exp(-x))


def _gelu(x):
    t = jnp.tanh(_G0 * (x + _G1 * x * x * x))
    return 0.5 * x * (1.0 + t)


def _gelu_grad(x):
    t = jnp.tanh(_G0 * (x + _G1 * x * x * x))
    return 0.5 * (1.0 + t) + 0.5 * x * (1.0 - t * t) * _G0 * (1.0 + 3.0 * _G1 * x * x)


def _ln_fwd(r, g, b):
    mu = jnp.mean(r, axis=-1, keepdims=True)
    d = r - mu
    var = jnp.mean(d * d, axis=-1, keepdims=True)
    rstd = lax.rsqrt(var + LN_EPS)
    xh = d * rstd
    return xh * g + b, xh, rstd


def _ln_bwd(dy, xh, rstd, g):
    dxh = dy * g
    m1 = jnp.mean(dxh, axis=-1, keepdims=True)
    m2 = jnp.mean(dxh * xh, axis=-1, keepdims=True)
    return rstd * (dxh - m1 - xh * m2)


def _resident(shape):
    nd = len(shape)
    return pl.BlockSpec(shape, lambda *_: (0,) * nd, pipeline_mode=pl.Buffered(1))


def _fixed(shape):
    nd = len(shape)
    return pl.BlockSpec(shape, lambda *_: (0,) * nd)


def _rows(tm, cols):
    return pl.BlockSpec((tm, cols), lambda i: (i, 0))


def _cols(rows, tm):
    return pl.BlockSpec((rows, tm), lambda i: (0, i))


def _params(sem):
    return pltpu.CompilerParams(dimension_semantics=sem, vmem_limit_bytes=VMEM_LIMIT_BYTES)


class _Exchange:
    def __init__(self, args, out_shape, sems, start, finish):
        self.args, self.out_shape, self.sems = list(args), list(out_shape), list(sems)
        self.start, self.finish = start, finish
        self.cuts = [(0, len(self.out_shape))]


def _call(body, name, grid, in_specs, out_specs, out_shape, args, scratch=(), sem=None, bg=None, aliases=None):
    aliases = {} if aliases is None else aliases
    in_specs, args = list(in_specs), list(args)
    fn = body
    if bg is not None:
        n_args = len(args)

        def fn(*refs):
            body(*refs[:n_args], *refs[n_args + 1:])

        in_specs.append(pl.BlockSpec(memory_space=pl.ANY))
        args.append(bg)
    res = pl.pallas_call(fn, name=name, grid=grid, out_shape=tuple(out_shape), in_specs=in_specs,
                         out_specs=tuple(out_specs), scratch_shapes=list(scratch),
                         input_output_aliases=aliases, compiler_params=_params(sem))(*args)
    return tuple(res), ()


def _run_exchange_on_sequencer(ex, name, collective_id):
    n_i, n_o = len(ex.args), len(ex.out_shape)

    def body(*refs):
        ins, outs, sems = refs[:n_i], refs[n_i:n_i + n_o], refs[n_i + n_o:]
        x, y, c = lax.axis_index("x"), lax.axis_index("y"), lax.axis_index("c")
        barrier = pltpu.get_barrier_semaphore()
        for peer in [(x, y, 1 - c), (1 - x, y, c), (x, 1 - y, c), (1 - x, 1 - y, c)]:
            pl.semaphore_signal(barrier, inc=1, device_id=peer, device_id_type=MESH)
        pl.semaphore_wait(barrier, 4)
        ex.start(ins, outs, sems)
        ex.finish(ins, outs, sems)

    return tuple(pl.kernel(body, out_type=tuple(ex.out_shape),
                           mesh=plsc.ScalarSubcoreMesh(axis_name="sequencer", num_cores=1),
                           scratch_types=list(ex.sems), name=name,
                           compiler_params=pltpu.CompilerParams(collective_id=collective_id))(*ex.args))


def _join(exchanges):
    cuts = []
    a = o = q = 0
    for e in exchanges:
        cuts.append((a, a + len(e.args), o, o + len(e.out_shape), q, q + len(e.sems)))
        a, o, q = cuts[-1][1], cuts[-1][3], cuts[-1][5]

    def start(ins, outs, sems):
        for e, (a0, a1, o0, o1, q0, q1) in zip(exchanges, cuts):
            e.start(ins[a0:a1], outs[o0:o1], sems[q0:q1])

    def finish(ins, outs, sems):
        for e, (a0, a1, o0, o1, q0, q1) in zip(exchanges, cuts):
            e.finish(ins[a0:a1], outs[o0:o1], sems[q0:q1])

    joined = _Exchange(sum((e.args for e in exchanges), []), sum((e.out_shape for e in exchanges), []),
                       sum((e.sems for e in exchanges), []), start, finish)
    joined.cuts = [(c[2], c[3]) for c in cuts]
    return joined


def _ffn_proj(x, w_in, tm, name, bg=None):
    t = x.shape[0]
    nch = D_FF // FFN_COLS

    def body(x_ref, win_ref, xbt_ref, h_ref, a_ref):
        xb = x_ref[...].astype(BF16)
        xbt_ref[...] = xb.T
        for k in range(nch):
            cg = slice(k * FFN_COLS, (k + 1) * FFN_COLS)
            cu = slice(D_FF + k * FFN_COLS, D_FF + (k + 1) * FFN_COLS)
            hg = _dot(xb, win_ref[k])
            hu = _dot(xb, win_ref[nch + k])
            h_ref[:, cg] = hg.astype(BF16)
            h_ref[:, cu] = hu.astype(BF16)
            a_ref[:, cg] = (hg * _sigmoid(hg) * hu).astype(BF16)

    return _call(
        body, name, (t // tm,),
        [_rows(tm, D_MODEL), _resident((2 * nch, D_MODEL, FFN_COLS))],
        (_cols(D_MODEL, tm), _rows(tm, 2 * D_FF), _rows(tm, D_FF)),
        (SDS((D_MODEL, t), BF16), SDS((t, 2 * D_FF), BF16), SDS((t, D_FF), BF16)),
        (x, w_in), sem=("parallel",), bg=bg)


def _ffn_out(x, a, w_out, g, b, tm, name, bg=None):
    t = x.shape[0]

    def body(x_ref, a_ref, wout_ref, g_ref, b_ref, xn_ref, xh_ref, rstd_ref):
        f = _dot(a_ref[...], wout_ref[...])
        y, xh, rstd = _ln_fwd(ALPHA * x_ref[...] + 0.5 * f, g_ref[...], b_ref[...])
        xn_ref[...] = y
        xh_ref[...] = xh
        rstd_ref[...] = rstd

    return _call(
        body, name, (t // tm,),
        [_rows(tm, D_MODEL), _rows(tm, D_FF), _resident((D_FF, D_MODEL)), _fixed((1, D_MODEL)), _fixed((1, D_MODEL))],
        (_rows(tm, D_MODEL), _rows(tm, D_MODEL), _rows(tm, 1)),
        (SDS((t, D_MODEL), F32), SDS((t, D_MODEL), F32), SDS((t, 1), F32)),
        (x, a, w_out, g, b), sem=("parallel",), bg=bg)


def _ffn_bwd(dxn, xh, rstd, h, w_in, w_out, g, tm, name, bg=None):
    t = dxn.shape[0]
    nch = D_FF // FFN_COLS

    def body(dxn_ref, xh_ref, rstd_ref, h_ref, win_ref, wout_ref, g_ref,
             dx_ref, dh_ref, df_ref, dg_ref, db_ref):
        @pl.when(pl.program_id(0) == 0)
        def _():
            dg_ref[...] = jnp.zeros_like(dg_ref)
            db_ref[...] = jnp.zeros_like(db_ref)

        dy = dxn_ref[...]
        xhv = xh_ref[...]
        dr = _ln_bwd(dy, xhv, rstd_ref[...], g_ref[...])
        dg_ref[...] += jnp.sum(dy * xhv, axis=0, keepdims=True)
        db_ref[...] += jnp.sum(dy, axis=0, keepdims=True)
        df = (0.5 * dr).astype(BF16)
        df_ref[...] = df
        dx = ALPHA * dr
        das = [_dot_nt(df, wout_ref[k * FFN_COLS:(k + 1) * FFN_COLS, :]) for k in range(nch)]
        for k in range(nch):
            cg = slice(k * FFN_COLS, (k + 1) * FFN_COLS)
            cu = slice(D_FF + k * FFN_COLS, D_FF + (k + 1) * FFN_COLS)
            hg = h_ref[:, cg].astype(F32)
            hu = h_ref[:, cu].astype(F32)
            sg = _sigmoid(hg)
            silu = hg * sg
            da = das[k]
            dhu = (da * silu).astype(BF16)
            dhg = (da * hu * (sg * (1.0 + hg * (1.0 - sg)))).astype(BF16)
            dh_ref[:, cg] = dhg
            dh_ref[:, cu] = dhu
            dx = dx + _dot_nt(dhg, win_ref[k]) + _dot_nt(dhu, win_ref[nch + k])
        dx_ref[...] = dx

    return _call(
        body, name, (t // tm,),
        [_rows(tm, D_MODEL), _rows(tm, D_MODEL), _rows(tm, 1), _rows(tm, 2 * D_FF),
         _resident((2 * nch, D_MODEL, FFN_COLS)), _resident((D_FF, D_MODEL)), _fixed((1, D_MODEL))],
        (_rows(tm, D_MODEL), _rows(tm, 2 * D_FF), _rows(tm, D_MODEL),
         _fixed((1, D_MODEL)), _fixed((1, D_MODEL))),
        (SDS((t, D_MODEL), F32), SDS((t, 2 * D_FF), BF16), SDS((t, D_MODEL), BF16),
         SDS((1, D_MODEL), F32), SDS((1, D_MODEL), F32)),
        (dxn, xh, rstd, h, w_in, w_out, g), sem=("arbitrary",), bg=bg)


def _tn_matmul(a, b, name, bm, bn, col_block=0, total_cols=None, prev=None, bg=None, a_cols=None, a_t=False):
    t, m = a.shape[::-1] if a_t else a.shape
    a_first = 0
    if a_cols is not None:
        a_first, m = a_cols[0], a_cols[1] * bm
    n = b.shape[1]
    total_cols = n if total_cols is None else total_cols
    whole = bm * bn <= TN_SMALL_BLOCK and (m // bm) * (n // bn) >= 2
    bk = min(2 * TN_K_BLOCK if whole else TN_K_BLOCK, t)
    nk = t // bk
    n_in = 2 if prev is None else 4

    def body(*refs):
        a_ref, b_ref = refs[0], refs[1]
        o_ref, ob_ref = refs[n_in], refs[n_in + 1]
        k = pl.program_id(2)

        @pl.when(k == 0)
        def _():
            o_ref[...] = jnp.zeros_like(o_ref)

        o_ref[...] += _dot(a_ref[...], b_ref[...]) if a_t else _dot_tn(a_ref[...], b_ref[...])

        @pl.when(k == nk - 1)
        def _():
            ob_ref[...] = o_ref[...].astype(BF16)

    a_spec = (pl.BlockSpec((bm, bk), lambda i, j, k: (i + a_first, k)) if a_t
              else pl.BlockSpec((bk, bm), lambda i, j, k: (k, i + a_first)))
    in_specs = [a_spec, pl.BlockSpec((bk, bn), lambda i, j, k: (k, j))]
    args = [a, b]
    aliases = {}
    if prev is not None:
        in_specs += [pl.BlockSpec(memory_space=pl.ANY), pl.BlockSpec(memory_space=pl.ANY)]
        args += list(prev)
        aliases = {2: 0, 3: 1}
        if any(bg is p for p in prev):
            bg = None
    out_spec = pl.BlockSpec((bm, bn), lambda i, j, k: (i, j + col_block))
    return _call(body, name, (m // bm, n // bn, nk), in_specs, (out_spec, out_spec),
                 (SDS((m, total_cols), F32), SDS((m, total_cols), BF16)), args,
                 sem=("parallel", "parallel", "arbitrary"), bg=bg, aliases=aliases)


def _mixin_fwd(x1, w, tm, bg=None):
    t = x1.shape[0]

    def body(x_ref, w_ref, xbt_ref, za_ref, zuv_ref, gab_ref):
        xb = x_ref[...].astype(BF16)
        xbt_ref[...] = xb.T
        za_ref[...] = _dot(xb, w_ref[:, 0:512]).astype(BF16)
        zuv_ref[...] = _dot(xb, w_ref[:, 512:1536]).astype(BF16)
        gab_ref[...] = _dot(xb, w_ref[:, 1536:3584]).astype(BF16)

    return _call(
        body, "mixin_fwd", (t // tm,),
        [_rows(tm, D_MODEL), _resident((D_MODEL, 3584))],
        (_cols(D_MODEL, tm), _rows(tm, 512), _rows(tm, 1024), _rows(tm, 2048)),
        (SDS((D_MODEL, t), BF16), SDS((t, 512), BF16), SDS((t, 1024), BF16), SDS((t, 2048), BF16)),
        (x1, w), sem=("parallel",), bg=bg)


def _mixin_bwd(dx1a, dza, dzuv, dgab, w, tm, bg=None):
    t = dx1a.shape[0]

    def body(d_ref, dza_ref, dzuv_ref, dgab_ref, w_ref, dx_ref):
        dx_ref[...] = (d_ref[...] + _dot_nt(dza_ref[...], w_ref[:, 0:512])
                       + _dot_nt(dzuv_ref[...], w_ref[:, 512:1536])
                       + _dot_nt(dgab_ref[...], w_ref[:, 1536:3584]))

    return _call(
        body, "mixin_bwd", (t // tm,),
        [_rows(tm, D_MODEL), _rows(tm, 512), _rows(tm, 1024), _rows(tm, 2048), _resident((D_MODEL, 3584))],
        (_rows(tm, D_MODEL),), (SDS((t, D_MODEL), F32),),
        (dx1a, dza, dzuv, dgab, w), sem=("parallel",), bg=bg)


def _unrolled(lo, hi, body, carry):
    for j in range(lo, hi):
        carry = body(j, carry)
    return carry


def _scan_fwd(hr_ref, hi_ref, a_ref, ap_ref, carry_ref, seg, cin_ref):
    for lc in range(SSM_LANES // SCAN_LANES):
        ls = slice(lc * SCAN_LANES, (lc + 1) * SCAN_LANES)
        a_r = jnp.broadcast_to(a_ref[0:1, ls], (8, SCAN_LANES))
        a_i = jnp.broadcast_to(a_ref[1:2, ls], (8, SCAN_LANES))

        def step(j, hc, ls=ls, a_r=a_r, a_i=a_i):
            h_r, h_i = hc
            rows = pl.ds(j * 8, 8)
            n_r = a_r * h_r - a_i * h_i + hr_ref[rows, ls]
            n_i = a_r * h_i + a_i * h_r + hi_ref[rows, ls]
            hr_ref[rows, ls] = n_r
            hi_ref[rows, ls] = n_i
            return n_r, n_i

        zero = jnp.zeros((8, SCAN_LANES), F32)
        f_r, f_i = _unrolled(0, seg, step, (zero, zero))
        c_r = carry_ref[0:1, ls]
        c_i = carry_ref[1:2, ls]
        p_r = ap_ref[0:1, ls]
        p_i = ap_ref[1:2, ls]
        rows_r, rows_i = [], []
        for s in range(8):
            rows_r.append(c_r)
            rows_i.append(c_i)
            c_r, c_i = (f_r[s:s + 1] + p_r * c_r - p_i * c_i,
                        f_i[s:s + 1] + p_r * c_i + p_i * c_r)
        carry_ref[0:1, ls] = c_r
        carry_ref[1:2, ls] = c_i
        cin_r = jnp.concatenate(rows_r, axis=0)
        cin_i = jnp.concatenate(rows_i, axis=0)
        if cin_ref is not None:
            cin_ref[0, :, ls] = cin_r
            cin_ref[1, :, ls] = cin_i

        def fix(j, cc, ls=ls, a_r=a_r, a_i=a_i):
            c_r, c_i = cc
            c_r, c_i = a_r * c_r - a_i * c_i, a_r * c_i + a_i * c_r
            rows = pl.ds(j * 8, 8)
            hr_ref[rows, ls] = hr_ref[rows, ls] + c_r
            hi_ref[rows, ls] = hi_ref[rows, ls] + c_i
            return c_r, c_i

        _unrolled(0, seg, fix, (cin_r, cin_i))


def _scan_bwd(gr_ref, gi_ref, hr_ref, hi_ref, cin_ref, a_ref, ap_ref, rcarry_ref, da_ref, seg):
    for lc in range(SSM_LANES // SCAN_LANES):
        ls = slice(lc * SCAN_LANES, (lc + 1) * SCAN_LANES)
        a_r = jnp.broadcast_to(a_ref[0:1, ls], (8, SCAN_LANES))
        a_i = jnp.broadcast_to(a_ref[1:2, ls], (8, SCAN_LANES))

        def step(t, gc, ls=ls, a_r=a_r, a_i=a_i):
            g_r, g_i = gc
            rows = pl.ds((seg - 1 - t) * 8, 8)
            n_r = gr_ref[rows, ls] + a_r * g_r + a_i * g_i
            n_i = gi_ref[rows, ls] + a_r * g_i - a_i * g_r
            gr_ref[rows, ls] = n_r
            gi_ref[rows, ls] = n_i
            return n_r, n_i

        zero = jnp.zeros((8, SCAN_LANES), F32)
        f_r, f_i = _unrolled(0, seg, step, (zero, zero))
        c_r = rcarry_ref[0:1, ls]
        c_i = rcarry_ref[1:2, ls]
        p_r = ap_ref[0:1, ls]
        p_i = ap_ref[1:2, ls]
        rows_r, rows_i = [None] * 8, [None] * 8
        for s in range(7, -1, -1):
            rows_r[s] = c_r
            rows_i[s] = c_i
            c_r, c_i = (f_r[s:s + 1] + p_r * c_r + p_i * c_i,
                        f_i[s:s + 1] + p_r * c_i - p_i * c_r)
        rcarry_ref[0:1, ls] = c_r
        rcarry_ref[1:2, ls] = c_i
        cin_r = jnp.concatenate(rows_r, axis=0)
        cin_i = jnp.concatenate(rows_i, axis=0)

        def fix_row(j_rows, hp_r, hp_i, cc, ls=ls, a_r=a_r, a_i=a_i):
            c_r, c_i, acc_r, acc_i = cc
            c_r, c_i = a_r * c_r + a_i * c_i, a_r * c_i - a_i * c_r
            g_r = gr_ref[j_rows, ls] + c_r
            g_i = gi_ref[j_rows, ls] + c_i
            gr_ref[j_rows, ls] = g_r
            gi_ref[j_rows, ls] = g_i
            acc_r = acc_r + g_r * hp_r + g_i * hp_i
            acc_i = acc_i + g_i * hp_r - g_r * hp_i
            return c_r, c_i, acc_r, acc_i

        def fix(t, cc, ls=ls, fix_row=fix_row):
            j = seg - 1 - t
            rows = pl.ds(j * 8, 8)
            prev = pl.ds((j - 1) * 8, 8)
            return fix_row(rows, hr_ref[prev, ls], hi_ref[prev, ls], cc)

        cc = _unrolled(0, seg - 1, fix, (cin_r, cin_i, zero, zero))
        _, _, acc_r, acc_i = fix_row(pl.ds(0, 8), cin_ref[0, :, ls], cin_ref[1, :, ls], cc)
        da_ref[0, :, ls] += acc_r
        da_ref[1, :, ls] += acc_i


def _s5_fwd(za, sp, bsz, seq, tb, bg=None):
    nb = seq // tb
    seg = tb // 8
    t = bsz * seq

    def body(za_ref, perm_ref, permt_ref, mre_ref, mim_ref, nre_ref, nim_ref, a_ref, ap_ref,
             dsk_ref, gw_ref, gb_ref, out_ref, outt_ref, y2_ref, car_ref, hr_ref, hi_ref, carry_ref):
        @pl.when(pl.program_id(1) == 0)
        def _():
            carry_ref[...] = jnp.zeros_like(carry_ref)

        car_ref[0] = carry_ref[...]
        up = _dot(perm_ref[...], za_ref[...])
        upb = up.astype(BF16)
        for bb in range(S5_BLOCKS):
            ub = upb[:, bb * S5_BLOCK_IN:(bb + 1) * S5_BLOCK_IN]
            st = slice(bb * S5_BLOCK_ST, (bb + 1) * S5_BLOCK_ST)
            hr_ref[:, st] = _dot(ub, mre_ref[bb])
            hi_ref[:, st] = _dot(ub, mim_ref[bb])
        _scan_fwd(hr_ref, hi_ref, a_ref, ap_ref, carry_ref, seg, None)
        ys = []
        for bb in range(S5_BLOCKS):
            st = slice(bb * S5_BLOCK_ST, (bb + 1) * S5_BLOCK_ST)
            ys.append(_dot(hr_ref[:, st].astype(BF16), nre_ref[bb])
                      - _dot(hi_ref[:, st].astype(BF16), nim_ref[bb]))
        y2 = jnp.concatenate(ys, axis=1) + dsk_ref[...] * up
        y2_ref[...] = y2
        y3 = _gelu(y2)
        gl = _dot(y3.astype(BF16), gw_ref[...]) + gb_ref[...]
        oa = y3 * _sigmoid(gl)
        out = _dot(permt_ref[...], oa.astype(BF16)).astype(BF16)
        out_ref[...] = out
        outt_ref[...] = out.T

    blk = pl.BlockSpec((tb, D_SSM), lambda b, j: (b * nb + j, 0))
    blk_t = pl.BlockSpec((D_SSM, tb), lambda b, j: (0, b * nb + j))
    m_shape = (S5_BLOCKS, S5_BLOCK_IN, S5_BLOCK_ST)
    n_shape = (S5_BLOCKS, S5_BLOCK_ST, S5_BLOCK_IN)
    return _call(
        body, "s5_fwd", (bsz, nb),
        [blk, _fixed((tb, tb)), _fixed((tb, tb)), _fixed(m_shape), _fixed(m_shape), _fixed(n_shape),
         _fixed(n_shape), _fixed((2, SSM_LANES)), _fixed((2, SSM_LANES)), _fixed((1, D_SSM)),
         _fixed((D_SSM, D_SSM)), _fixed((1, D_SSM))],
        (blk, blk_t, blk, pl.BlockSpec((1, 2, SSM_LANES), lambda b, j: (b * nb + j, 0, 0))),
        (SDS((t, D_SSM), BF16), SDS((D_SSM, t), BF16), SDS((t, D_SSM), F32), SDS((bsz * nb, 2, SSM_LANES), F32)),
        (za, sp["perm"], sp["permt"], sp["mre"], sp["mim"], sp["nre"], sp["nim"], sp["a"], sp["ap"],
         sp["dskip"], sp["glu_w"], sp["glu_b"]),
        scratch=[pltpu.VMEM((tb, SSM_LANES), F32), pltpu.VMEM((tb, SSM_LANES), F32),
                 pltpu.VMEM((2, SSM_LANES), F32)],
        sem=("arbitrary", "arbitrary"), bg=bg)


def _s5_bwd(za, y2p, doa, carries, sp, bsz, seq, tb, bg=None):
    nb = seq // tb
    seg = tb // 8
    t = bsz * seq

    def body(za_ref, y2_ref, doa_ref, car_ref, perm_ref, permt_ref, mre_ref, mim_ref, mtre_ref, mtim_ref,
             nre_ref, nim_ref, ntre_ref, ntim_ref, a_ref, ap_ref, dsk_ref, gw_ref, gwt_ref, gb_ref,
             dza_ref, dmr_ref, dmi_ref, dnr_ref, dni_ref, da_ref, ddsk_ref, dgw_ref, dgb_ref,
             hr_ref, hi_ref, gr_ref, gi_ref, cin_ref, carry_ref, rcarry_ref):
        first = jnp.logical_and(pl.program_id(0) == 0, pl.program_id(1) == 0)

        @pl.when(first)
        def _():
            for r in (dmr_ref, dmi_ref, dnr_ref, dni_ref, da_ref, ddsk_ref, dgw_ref, dgb_ref):
                r[...] = jnp.zeros_like(r)

        @pl.when(pl.program_id(1) == 0)
        def _():
            rcarry_ref[...] = jnp.zeros_like(rcarry_ref)

        carry_ref[...] = car_ref[0]
        perm = perm_ref[...]
        up = _dot(perm, za_ref[...])
        upb = up.astype(BF16)
        for bb in range(S5_BLOCKS):
            ub = upb[:, bb * S5_BLOCK_IN:(bb + 1) * S5_BLOCK_IN]
            st = slice(bb * S5_BLOCK_ST, (bb + 1) * S5_BLOCK_ST)
            hr_ref[:, st] = _dot(ub, mre_ref[bb])
            hi_ref[:, st] = _dot(ub, mim_ref[bb])
        _scan_fwd(hr_ref, hi_ref, a_ref, ap_ref, carry_ref, seg, cin_ref)

        y2 = y2_ref[...]
        y3 = _gelu(y2)
        y3b = y3.astype(BF16)
        sg = _sigmoid(_dot(y3b, gw_ref[...]) + gb_ref[...])
        d0 = doa_ref[...]
        d_hi = d0.astype(BF16)
        d1 = d0 - d_hi.astype(F32)
        d_mid = d1.astype(BF16)
        d_lo = (d1 - d_mid.astype(F32)).astype(BF16)
        doap = _dot(perm, d_hi) + _dot(perm, d_mid) + _dot(perm, d_lo)
        dgl = doap * y3 * sg * (1.0 - sg)
        dglb = dgl.astype(BF16)
        dy3 = doap * sg + _dot(dglb, gwt_ref[...])
        dgw_ref[...] += _dot_tn(y3b, dglb)
        dgb_ref[...] += jnp.sum(dgl, axis=0, keepdims=True)
        dy2 = dy3 * _gelu_grad(y2)
        ddsk_ref[...] += jnp.sum(dy2 * up, axis=0, keepdims=True)
        dyb = dy2.astype(BF16)
        for bb in range(S5_BLOCKS):
            dyc = dyb[:, bb * S5_BLOCK_IN:(bb + 1) * S5_BLOCK_IN]
            st = slice(bb * S5_BLOCK_ST, (bb + 1) * S5_BLOCK_ST)
            gr_ref[:, st] = _dot(dyc, ntre_ref[bb])
            gi_ref[:, st] = -_dot(dyc, ntim_ref[bb])
            dnr_ref[bb] += _dot_tn(hr_ref[:, st].astype(BF16), dyc)
            dni_ref[bb] += -_dot_tn(hi_ref[:, st].astype(BF16), dyc)
        _scan_bwd(gr_ref, gi_ref, hr_ref, hi_ref, cin_ref, a_ref, ap_ref, rcarry_ref, da_ref, seg)
        dus = []
        for bb in range(S5_BLOCKS):
            st = slice(bb * S5_BLOCK_ST, (bb + 1) * S5_BLOCK_ST)
            grb = gr_ref[:, st].astype(BF16)
            gib = gi_ref[:, st].astype(BF16)
            dus.append(_dot(grb, mtre_ref[bb]) + _dot(gib, mtim_ref[bb]))
            ub = upb[:, bb * S5_BLOCK_IN:(bb + 1) * S5_BLOCK_IN]
            dmr_ref[bb] += _dot_tn(ub, grb)
            dmi_ref[bb] += _dot_tn(ub, gib)
        du = jnp.concatenate(dus, axis=1) + dy2 * dsk_ref[...]
        dza_ref[...] = _dot(permt_ref[...], du.astype(BF16)).astype(BF16)

    def rev(b, j):
        return (b * nb + (nb - 1 - j), 0)

    blk = pl.BlockSpec((tb, D_SSM), rev)
    m_shape = (S5_BLOCKS, S5_BLOCK_IN, S5_BLOCK_ST)
    n_shape = (S5_BLOCKS, S5_BLOCK_ST, S5_BLOCK_IN)
    return _call(
        body, "s5_bwd", (bsz, nb),
        [blk, blk, blk, pl.BlockSpec((1, 2, SSM_LANES), lambda b, j: (b * nb + (nb - 1 - j), 0, 0)),
         _fixed((tb, tb)), _fixed((tb, tb)), _fixed(m_shape), _fixed(m_shape), _fixed(n_shape), _fixed(n_shape),
         _fixed(n_shape), _fixed(n_shape), _fixed(m_shape), _fixed(m_shape),
         _fixed((2, SSM_LANES)), _fixed((2, SSM_LANES)), _fixed((1, D_SSM)),
         _fixed((D_SSM, D_SSM)), _fixed((D_SSM, D_SSM)), _fixed((1, D_SSM))],
        (blk, _fixed(m_shape), _fixed(m_shape), _fixed(n_shape), _fixed(n_shape),
         _fixed((2, 8, SSM_LANES)), _fixed((1, D_SSM)), _fixed((D_SSM, D_SSM)), _fixed((1, D_SSM))),
        (SDS((t, D_SSM), BF16), SDS(m_shape, F32), SDS(m_shape, F32), SDS(n_shape, F32), SDS(n_shape, F32),
         SDS((2, 8, SSM_LANES), F32), SDS((1, D_SSM), F32), SDS((D_SSM, D_SSM), F32), SDS((1, D_SSM), F32)),
        (za, y2p, doa, carries, sp["perm"], sp["permt"], sp["mre"], sp["mim"], sp["mtre"], sp["mtim"],
         sp["nre"], sp["nim"], sp["ntre"], sp["ntim"], sp["a"], sp["ap"], sp["dskip"], sp["glu_w"],
         sp["glu_wt"], sp["glu_b"]),
        scratch=[pltpu.VMEM((tb, SSM_LANES), F32), pltpu.VMEM((tb, SSM_LANES), F32),
                 pltpu.VMEM((tb, SSM_LANES), F32), pltpu.VMEM((tb, SSM_LANES), F32),
                 pltpu.VMEM((2, 8, SSM_LANES), F32), pltpu.VMEM((2, SSM_LANES), F32),
                 pltpu.VMEM((2, SSM_LANES), F32)],
        sem=("arbitrary", "arbitrary"), bg=bg)


def _gmlp_spatial(ws_ref, vb):
    lane = lax.broadcasted_iota(jnp.int32, (CHUNK, 128), 1)
    parts = []
    for j in range(GMLP_HEADS // 2):
        vp = vb[:, 128 * j:128 * (j + 1)]
        parts.append(jnp.where(lane < GMLP_HEAD_DIM, _dot(ws_ref[2 * j], vp), _dot(ws_ref[2 * j + 1], vp)))
    return jnp.concatenate(parts, axis=1)


def _gmlp_fwd(zuv, ln_g, ln_b, wsm, bias, bg=None):
    t = zuv.shape[0]

    def body(z_ref, g_ref, b_ref, ws_ref, bias_ref, out_ref, outt_ref):
        for ch in range(GMLP_CHUNKS):
            rows = slice(ch * CHUNK, (ch + 1) * CHUNK)
            u = _gelu(z_ref[rows, 0:D_GMLP].astype(F32))
            v0 = _gelu(z_ref[rows, D_GMLP:2 * D_GMLP].astype(F32))
            v, _, _ = _ln_fwd(v0, g_ref[...], b_ref[...])
            s = _gmlp_spatial(ws_ref, v.astype(BF16)) + bias_ref[...]
            out = (u * s).astype(BF16)
            out_ref[rows, :] = out
            outt_ref[:, rows] = out.T

    step = GMLP_CHUNKS * CHUNK
    return _call(
        body, "gmlp_fwd", (t // step,),
        [_rows(step, 2 * D_GMLP), _fixed((1, D_GMLP)), _fixed((1, D_GMLP)),
         _fixed((GMLP_HEADS, CHUNK, CHUNK)), _fixed((CHUNK, D_GMLP))],
        (_rows(step, D_GMLP), _cols(D_GMLP, step)), (SDS((t, D_GMLP), BF16), SDS((D_GMLP, t), BF16)),
        (zuv, ln_g, ln_b, wsm, bias), sem=("parallel",), bg=bg)


def _gmlp_bwd(zuv, dgm, ln_g, ln_b, wsm, wsmt, bias, bg=None):
    t = zuv.shape[0]

    def body(z_ref, d_ref, g_ref, b_ref, ws_ref, wst_ref, bias_ref,
             dz_ref, dws_ref, dbias_ref, dg_ref, db_ref):
        @pl.when(pl.program_id(0) == 0)
        def _():
            for r in (dws_ref, dbias_ref, dg_ref, db_ref):
                r[...] = jnp.zeros_like(r)

        gam = g_ref[...]
        lane = lax.broadcasted_iota(jnp.int32, (CHUNK, 128), 1)
        tril = (lax.broadcasted_iota(jnp.int32, (CHUNK, CHUNK), 0)
                >= lax.broadcasted_iota(jnp.int32, (CHUNK, CHUNK), 1))
        zero_b = jnp.zeros((CHUNK, 128), BF16)
        for ch in range(GMLP_CHUNKS):
            rows = slice(ch * CHUNK, (ch + 1) * CHUNK)
            zu = z_ref[rows, 0:D_GMLP].astype(F32)
            zv = z_ref[rows, D_GMLP:2 * D_GMLP].astype(F32)
            u = _gelu(zu)
            v0 = _gelu(zv)
            v, vhat, rstd = _ln_fwd(v0, gam, b_ref[...])
            vb = v.astype(BF16)
            s = _gmlp_spatial(ws_ref, vb) + bias_ref[...]
            d = d_ref[rows, :]
            dz_ref[rows, 0:D_GMLP] = (d * s * _gelu_grad(zu)).astype(BF16)
            ds = d * u
            dbias_ref[...] += ds
            dsb = ds.astype(BF16)
            parts = []
            for j in range(GMLP_HEADS // 2):
                dsp = dsb[:, 128 * j:128 * (j + 1)]
                vp = vb[:, 128 * j:128 * (j + 1)]
                parts.append(jnp.where(lane < GMLP_HEAD_DIM, _dot(wst_ref[2 * j], dsp),
                                       _dot(wst_ref[2 * j + 1], dsp)))
                lo = jnp.where(lane < GMLP_HEAD_DIM, dsp, zero_b)
                hi = jnp.where(lane < GMLP_HEAD_DIM, zero_b, dsp)
                dws_ref[2 * j] += jnp.where(tril, _dot_nt(lo, vp), 0.0)
                dws_ref[2 * j + 1] += jnp.where(tril, _dot_nt(hi, vp), 0.0)
            dv = jnp.concatenate(parts, axis=1)
            dg_ref[...] += jnp.sum(dv * vhat, axis=0, keepdims=True)
            db_ref[...] += jnp.sum(dv, axis=0, keepdims=True)
            dz_ref[rows, D_GMLP:2 * D_GMLP] = (_ln_bwd(dv, vhat, rstd, gam) * _gelu_grad(zv)).astype(BF16)

    step = GMLP_CHUNKS * CHUNK
    return _call(
        body, "gmlp_bwd", (t // step,),
        [_rows(step, 2 * D_GMLP), _rows(step, D_GMLP), _fixed((1, D_GMLP)), _fixed((1, D_GMLP)),
         _fixed((GMLP_HEADS, CHUNK, CHUNK)), _fixed((GMLP_HEADS, CHUNK, CHUNK)), _fixed((CHUNK, D_GMLP))],
        (_rows(step, 2 * D_GMLP), _fixed((GMLP_HEADS, CHUNK, CHUNK)), _fixed((CHUNK, D_GMLP)),
         _fixed((1, D_GMLP)), _fixed((1, D_GMLP))),
        (SDS((t, 2 * D_GMLP), BF16), SDS((GMLP_HEADS, CHUNK, CHUNK), F32), SDS((CHUNK, D_GMLP), F32),
         SDS((1, D_GMLP), F32), SDS((1, D_GMLP), F32)),
        (zuv, dgm, ln_g, ln_b, wsm, wsmt, bias), sem=("arbitrary",), bg=bg)


def _mixout_fwd(x1, s5o, gm, gab, ua, ub, wmo, g, b, tm, bg=None):
    t = x1.shape[0]

    def body(x_ref, s_ref, m_ref, gab_ref, ua_ref, ub_ref, wmo_ref, g_ref, b_ref,
             xn_ref, xh_ref, rstd_ref):
        ya = _dot(s_ref[...], ua_ref[...])
        yb = _dot(m_ref[...], ub_ref[...])
        mix = (_sigmoid(gab_ref[:, 0:D_MODEL].astype(F32)) * ya
               + _sigmoid(gab_ref[:, D_MODEL:2 * D_MODEL].astype(F32)) * yb)
        r = ALPHA * x_ref[...] + _dot(mix.astype(BF16), wmo_ref[...])
        y, xh, rstd = _ln_fwd(r, g_ref[...], b_ref[...])
        xn_ref[...] = y
        xh_ref[...] = xh
        rstd_ref[...] = rstd

    return _call(
        body, "mixout_fwd", (t // tm,),
        [_rows(tm, D_MODEL), _rows(tm, D_SSM), _rows(tm, D_GMLP), _rows(tm, 2 * D_MODEL),
         _resident((D_SSM, D_MODEL)), _resident((D_GMLP, D_MODEL)), _resident((D_MODEL, D_MODEL)),
         _fixed((1, D_MODEL)), _fixed((1, D_MODEL))],
        (_rows(tm, D_MODEL), _rows(tm, D_MODEL), _rows(tm, 1)),
        (SDS((t, D_MODEL), F32), SDS((t, D_MODEL), F32), SDS((t, 1), F32)),
        (x1, s5o, gm, gab, ua, ub, wmo, g, b), sem=("parallel",), bg=bg)


def _mixout_bwd(dx2, xh, rstd, s5o, gm, gab, ua, ub, wmo, g, tm, bg=None):
    t = dx2.shape[0]

    def body(d_ref, xh_ref, rstd_ref, s_ref, m_ref, gab_ref, ua_ref, ub_ref, wmo_ref, g_ref,
             dx1_ref, dmx_ref, mb_ref, dya_ref, dyb_ref, ds5_ref, dgm_ref, dgab_ref, dg_ref, db_ref):
        @pl.when(pl.program_id(0) == 0)
        def _():
            dg_ref[...] = jnp.zeros_like(dg_ref)
            db_ref[...] = jnp.zeros_like(db_ref)

        dy = d_ref[...]
        xhv = xh_ref[...]
        dr = _ln_bwd(dy, xhv, rstd_ref[...], g_ref[...])
        dg_ref[...] += jnp.sum(dy * xhv, axis=0, keepdims=True)
        db_ref[...] += jnp.sum(dy, axis=0, keepdims=True)
        dx1_ref[...] = ALPHA * dr
        drb = dr.astype(BF16)
        dmx_ref[...] = drb
        dm = _dot_nt(drb, wmo_ref[...])
        ya = _dot(s_ref[...], ua_ref[...])
        yb = _dot(m_ref[...], ub_ref[...])
        sa = _sigmoid(gab_ref[:, 0:D_MODEL].astype(F32))
        sb = _sigmoid(gab_ref[:, D_MODEL:2 * D_MODEL].astype(F32))
        mb_ref[...] = (sa * ya + sb * yb).astype(BF16).T
        dya = (dm * sa).astype(BF16)
        dyb = (dm * sb).astype(BF16)
        dya_ref[...] = dya
        dyb_ref[...] = dyb
        dgab_ref[:, 0:D_MODEL] = (dm * ya * sa * (1.0 - sa)).astype(BF16)
        dgab_ref[:, D_MODEL:2 * D_MODEL] = (dm * yb * sb * (1.0 - sb)).astype(BF16)
        ds5_ref[...] = _dot_nt(dya, ua_ref[...])
        dgm_ref[...] = _dot_nt(dyb, ub_ref[...])

    return _call(
        body, "mixout_bwd", (t // tm,),
        [_rows(tm, D_MODEL), _rows(tm, D_MODEL), _rows(tm, 1), _rows(tm, D_SSM), _rows(tm, D_GMLP),
         _rows(tm, 2 * D_MODEL), _resident((D_SSM, D_MODEL)), _resident((D_GMLP, D_MODEL)),
         _resident((D_MODEL, D_MODEL)), _fixed((1, D_MODEL))],
        (_rows(tm, D_MODEL), _rows(tm, D_MODEL), _cols(D_MODEL, tm), _rows(tm, D_MODEL),
         _rows(tm, D_MODEL), _rows(tm, D_SSM), _rows(tm, D_GMLP), _rows(tm, 2 * D_MODEL),
         _fixed((1, D_MODEL)), _fixed((1, D_MODEL))),
        (SDS((t, D_MODEL), F32), SDS((t, D_MODEL), BF16), SDS((D_MODEL, t), BF16),
         SDS((t, D_MODEL), BF16), SDS((t, D_MODEL), BF16), SDS((t, D_SSM), F32),
         SDS((t, D_GMLP), F32), SDS((t, 2 * D_MODEL), BF16),
         SDS((1, D_MODEL), F32), SDS((1, D_MODEL), F32)),
        (dx2, xh, rstd, s5o, gm, gab, ua, ub, wmo, g), sem=("arbitrary",), bg=bg)


def _ple_loss(x3, p, tgt, wpg, wpp, tm, bg=None):
    t = x3.shape[0]

    def body(x_ref, p_ref, t_ref, wpg_ref, wpp_ref, dx_ref, xb_ref, pb_ref, dq_ref, de_ref, loss_ref):
        @pl.when(pl.program_id(0) == 0)
        def _():
            loss_ref[...] = jnp.zeros_like(loss_ref)

        x3v = x_ref[...]
        xb = x3v.astype(BF16)
        pb = p_ref[...].astype(BF16)
        xb_ref[...] = xb.T
        pb_ref[...] = pb.T
        s = _sigmoid(_dot(xb, wpg_ref[...]))
        e = _dot(pb, wpp_ref[...])
        diff = x3v + s * e - t_ref[...]
        loss_ref[...] += jnp.sum(diff * diff, axis=0, keepdims=True)
        dout = diff * (1.0 / D_MODEL)
        de_ref[...] = (dout * s).astype(BF16)
        dq = (dout * e * s * (1.0 - s)).astype(BF16)
        dq_ref[...] = dq
        dx_ref[...] = dout + _dot_nt(dq, wpg_ref[...])

    return _call(
        body, "ple_loss", (t // tm,),
        [_rows(tm, D_MODEL), _rows(tm, PLE_DIM), _rows(tm, D_MODEL),
         _resident((D_MODEL, D_MODEL)), _resident((PLE_DIM, D_MODEL))],
        (_rows(tm, D_MODEL), _cols(D_MODEL, tm), _cols(PLE_DIM, tm), _rows(tm, D_MODEL),
         _rows(tm, D_MODEL), _fixed((1, D_MODEL))),
        (SDS((t, D_MODEL), F32), SDS((D_MODEL, t), BF16), SDS((PLE_DIM, t), BF16),
         SDS((t, D_MODEL), BF16), SDS((t, D_MODEL), BF16), SDS((1, D_MODEL), F32)),
        (x3, p, tgt, wpg, wpp), sem=("arbitrary",), bg=bg)


def _s5_discretise(lre, lim, log_dt, bre, bim):
    dt = jnp.exp(log_dt)[:, None]
    mag = jnp.exp(lre * dt)
    abr = mag * jnp.cos(lim * dt)
    abi = mag * jnp.sin(lim * dt)
    nr = abr - 1.0
    ni = abi
    den = lre * lre + lim * lim
    cr = ((nr * lre + ni * lim) / den)[..., None]
    ci = ((ni * lre - nr * lim) / den)[..., None]
    return abr, abi, cr * bre - ci * bim, cr * bim + ci * bre


def _block_diag_in(bb):
    v = bb.reshape(S5_BLOCKS, 8, SSM_STATE, SSM_GROUP_CH).transpose(0, 1, 3, 2)
    return jnp.einsum("bgip,gh->bgihp", v, jnp.eye(8, dtype=bb.dtype)).reshape(
        S5_BLOCKS, S5_BLOCK_IN, S5_BLOCK_ST)


def _block_diag_in_t(dm):
    v = dm.reshape(S5_BLOCKS, 8, SSM_GROUP_CH, 8, SSM_STATE)
    d = jnp.einsum("bgihp,gh->bgip", v, jnp.eye(8, dtype=dm.dtype))
    return d.transpose(0, 1, 3, 2).reshape(SSM_GROUPS, SSM_STATE, SSM_GROUP_CH)


def _block_diag_out(cc):
    v = cc.reshape(S5_BLOCKS, 8, SSM_GROUP_CH, SSM_STATE)
    return jnp.einsum("bgip,gh->bgphi", v, jnp.eye(8, dtype=cc.dtype)).reshape(
        S5_BLOCKS, S5_BLOCK_ST, S5_BLOCK_IN)


def _block_diag_out_t(dn):
    v = dn.reshape(S5_BLOCKS, 8, SSM_STATE, 8, SSM_GROUP_CH)
    d = jnp.einsum("bgphi,gh->bgip", v, jnp.eye(8, dtype=dn.dtype))
    return d.reshape(SSM_GROUPS, SSM_GROUP_CH, SSM_STATE)


def _s5_setup(lre, lim, log_dt, bre, bim, cre, cim, d_skip, glu_w, glu_b, tb):
    seg = tb // 8
    abr, abi, bbr, bbi = _s5_discretise(lre, lim, log_dt, bre, bim)
    pr, pi = abr, abi
    for _ in range(int(math.log2(seg))):
        pr, pi = pr * pr - pi * pi, 2.0 * pr * pi
    rows = jnp.arange(tb)
    src = (rows % 8) * seg + rows // 8
    perm = (src[:, None] == jnp.arange(tb)[None, :]).astype(BF16)
    mre = _block_diag_in(bbr)
    mim = _block_diag_in(bbi)
    nre = _block_diag_out(cre)
    nim = _block_diag_out(cim)
    return {
        "perm": perm, "permt": perm.T,
        "mre": mre.astype(BF16), "mim": mim.astype(BF16),
        "mtre": mre.transpose(0, 2, 1).astype(BF16), "mtim": mim.transpose(0, 2, 1).astype(BF16),
        "nre": nre.astype(BF16), "nim": nim.astype(BF16),
        "ntre": nre.transpose(0, 2, 1).astype(BF16), "ntim": nim.transpose(0, 2, 1).astype(BF16),
        "a": jnp.stack([abr.reshape(-1), abi.reshape(-1)]),
        "ap": jnp.stack([pr.reshape(-1), pi.reshape(-1)]),
        "dskip": d_skip.reshape(1, D_SSM), "glu_w": glu_w, "glu_wt": glu_w.T,
        "glu_b": glu_b.reshape(1, D_SSM),
    }


BIG = ("ffn1_w_in", "ffn1_w_out", "mix_w_in", "ssm_glu_w", "up_a", "up_b", "mix_w_out",
       "ffn2_w_in", "ffn2_w_out", "ple_w_proj", "ple_w_gate")
BIG_AXIS = {"ffn1_w_in": 1, "ffn1_w_out": 0, "mix_w_in": 1, "ssm_glu_w": 0, "up_a": 1, "up_b": 1,
            "mix_w_out": 0, "ffn2_w_in": 1, "ffn2_w_out": 0, "ple_w_proj": 1, "ple_w_gate": 0}
SHARD_MAJOR = 2
GATHER_AXIS = dict(BIG_AXIS, ffn1_w_in=SHARD_MAJOR, ffn2_w_in=SHARD_MAJOR)
GATHER_ORDER = (("ffn1_w_in",), ("ffn1_w_out",), ("mix_w_in",), ("ssm_glu_w", "up_a", "up_b", "mix_w_out"),
                ("ffn2_w_in",), ("ffn2_w_out", "ple_w_gate", "ple_w_proj"))
GATHER_FIRST_ID = 1
REDUCE_FIRST_ID = 7
SMALL = ("ln1_g", "ln1_b", "ssm_lambda_re", "ssm_lambda_im", "ssm_log_dt", "ssm_b_re", "ssm_b_im",
         "ssm_c_re", "ssm_c_im", "ssm_d", "ssm_glu_b", "gmlp_ln_g", "gmlp_ln_b", "gmlp_w_s",
         "gmlp_b_s", "ln2_g", "ln2_b", "ln3_g", "ln3_b")
SMALL_VIEW = {"ssm_b_re": (SSM_GROUPS, SSM_STATE * SSM_GROUP_CH), "ssm_b_im": (SSM_GROUPS, SSM_STATE * SSM_GROUP_CH)}


def _small_view(k, a):
    return a.reshape(SMALL_VIEW[k]) if k in SMALL_VIEW else a


def _place():
    return lax.axis_index("x"), lax.axis_index("y"), lax.axis_index("c")


def _other_chips(x, y):
    return [(1 - x, y), (x, 1 - y), (1 - x, 1 - y)]


def _window(ref, shard_shape, axis, chip, half):
    r, c = shard_shape
    hr = r // 2
    if axis == SHARD_MAJOR:
        return ref.at[chip] if half is None else ref.at[chip, pl.ds(half * hr, hr), :]
    if axis == 0:
        if half is None:
            return ref.at[pl.ds(chip * r, r), :]
        return ref.at[pl.ds(chip * r + half * hr, hr), :]
    if half is None:
        return ref.at[:, pl.ds(chip * c, c)]
    return ref.at[pl.ds(half * hr, hr), pl.ds(chip * c, c)]


def _gather_weights(shards, axes):
    n = len(shards)
    shapes = [s.shape for s in shards]
    full = [{0: (4 * r, c), 1: (r, 4 * c), SHARD_MAJOR: (4, r, c)}[ax] for (r, c), ax in zip(shapes, axes)]

    def remote(sems, i, k, src, dst, to):
        return pltpu.make_async_remote_copy(src_ref=src, dst_ref=dst, send_sem=sems[0].at[6 * i + k],
                                            recv_sem=sems[1].at[6 * i + k], device_id=to, device_id_type=MESH)

    def own_copies(ins, outs, sems):
        x, y, c = _place()
        me = 2 * x + y
        cps = []
        for i in range(n):
            hr = shapes[i][0] // 2
            mine = ins[i].at[pl.ds(c * hr, hr), :]
            for j, (cx, cy) in enumerate(_other_chips(x, y)):
                cps.append(remote(sems, i, j, mine, _window(outs[i], shapes[i], axes[i], me, c), (cx, cy, c)))
        local = [pltpu.make_async_copy(ins[i], _window(outs[i], shapes[i], axes[i], me, None), sems[2].at[i])
                 for i in range(n)]
        return cps, local

    def start(ins, outs, sems):
        cps, local = own_copies(ins, outs, sems)
        for cp in local + cps:
            cp.start()

    def finish(ins, outs, sems):
        x, y, c = _place()
        sibling = (x, y, 1 - c)
        passed = []
        for j, (cx, cy) in enumerate(_other_chips(x, y)):
            for i in range(n):
                w = _window(outs[i], shapes[i], axes[i], 2 * cx + cy, c)
                remote(sems, i, j, w, w, (cx, cy, c)).wait_recv()
                cp = remote(sems, i, 3 + j, w, w, sibling)
                cp.start()
                passed.append(cp)
        for j, (cx, cy) in enumerate(_other_chips(x, y)):
            for i in range(n):
                w = _window(outs[i], shapes[i], axes[i], 2 * cx + cy, 1 - c)
                remote(sems, i, 3 + j, w, w, sibling).wait_recv()
        cps, local = own_copies(ins, outs, sems)
        for cp in cps + passed:
            cp.wait_send()
        for cp in local:
            cp.wait()

    return _Exchange(shards, [SDS(f, BF16) for f in full],
                     [pltpu.SemaphoreType.DMA((6 * n,)), pltpu.SemaphoreType.DMA((6 * n,)),
                      pltpu.SemaphoreType.DMA((n,))], start, finish)


def _scatter_grads(parts, shapes, axes):
    n = len(parts)

    def copies(ins, outs, sems):
        x, y, c = _place()
        return [pltpu.make_async_remote_copy(
            src_ref=_window(ins[i], shapes[i], axes[i], 2 * cx + cy, None), dst_ref=outs[i].at[j],
            send_sem=sems[0].at[3 * i + j], recv_sem=sems[1].at[3 * i + j],
            device_id=(cx, cy, c), device_id_type=MESH)
            for i in range(n) for j, (cx, cy) in enumerate(_other_chips(x, y))]

    def start(ins, outs, sems):
        for cp in copies(ins, outs, sems):
            cp.start()

    def finish(ins, outs, sems):
        for cp in copies(ins, outs, sems):
            cp.wait()

    return _Exchange(parts, [SDS((3,) + tuple(s), BF16) for s in shapes],
                     [pltpu.SemaphoreType.DMA((3 * n,)), pltpu.SemaphoreType.DMA((3 * n,))], start, finish)


def _swap_halves(parts, shapes, axes):
    n = len(parts)

    def copies(ins, outs, sems):
        x, y, c = _place()
        cps = []
        for i in range(n):
            r, _ = shapes[i]
            hr = r // 2
            if axes[i] == 0:
                cps += [pltpu.make_async_remote_copy(
                    src_ref=ins[i].at[pl.ds(k * r + (1 - c) * hr, hr), :], dst_ref=outs[i].at[k],
                    send_sem=sems[0].at[i], recv_sem=sems[1].at[i], device_id=(x, y, 1 - c),
                    device_id_type=MESH) for k in range(4)]
            else:
                cps.append(pltpu.make_async_remote_copy(
                    src_ref=ins[i].at[pl.ds((1 - c) * hr, hr), :], dst_ref=outs[i],
                    send_sem=sems[0].at[i], recv_sem=sems[1].at[i], device_id=(x, y, 1 - c),
                    device_id_type=MESH))
        return cps

    def start(ins, outs, sems):
        for cp in copies(ins, outs, sems):
            cp.start()

    def finish(ins, outs, sems):
        x, y, c = _place()
        for i in range(n):
            pltpu.make_async_remote_copy(src_ref=outs[i], dst_ref=outs[i], send_sem=sems[0].at[i],
                                         recv_sem=sems[1].at[i], device_id=(x, y, 1 - c),
                                         device_id_type=MESH).wait()

    out = [SDS((4, r // 2, c), BF16) if ax == 0 else SDS((r // 2, 4 * c), BF16)
           for (r, c), ax in zip(shapes, axes)]
    return _Exchange(parts, out, [pltpu.SemaphoreType.DMA((n,)), pltpu.SemaphoreType.DMA((n,))], start, finish)


def _scatter_halves(pres, shapes):
    n = len(pres)

    def copies(ins, outs, sems):
        x, y, c = _place()
        return [pltpu.make_async_remote_copy(
            src_ref=ins[i].at[1 + j], dst_ref=outs[i].at[j], send_sem=sems[0].at[3 * i + j],
            recv_sem=sems[1].at[3 * i + j], device_id=(cx, cy, c), device_id_type=MESH)
            for i in range(n) for j, (cx, cy) in enumerate(_other_chips(x, y))]

    def start(ins, outs, sems):
        for cp in copies(ins, outs, sems):
            cp.start()

    def finish(ins, outs, sems):
        for cp in copies(ins, outs, sems):
            cp.wait()

    return _Exchange(pres, [SDS((3, r // 2, c), BF16) for r, c in shapes],
                     [pltpu.SemaphoreType.DMA((3 * n,)), pltpu.SemaphoreType.DMA((3 * n,))], start, finish)


def _swap_with_sibling(arrs):
    n = len(arrs)

    def copies(ins, outs, sems):
        x, y, c = _place()
        return [pltpu.make_async_remote_copy(src_ref=ins[i], dst_ref=outs[i], send_sem=sems[0].at[i],
                                             recv_sem=sems[1].at[i], device_id=(x, y, 1 - c),
                                             device_id_type=MESH) for i in range(n)]

    def start(ins, outs, sems):
        for cp in copies(ins, outs, sems):
            cp.start()

    def finish(ins, outs, sems):
        for cp in copies(ins, outs, sems):
            cp.wait()

    return _Exchange(arrs, [SDS(a.shape, a.dtype) for a in arrs],
                     [pltpu.SemaphoreType.DMA((n,)), pltpu.SemaphoreType.DMA((n,))], start, finish)


def _gather_small(arrs):
    n = len(arrs)

    def copy(sems, outs, i, k, block, to, src=None):
        px, py, pc = block
        dst = outs[i].at[4 * px + 2 * py + pc]
        return pltpu.make_async_remote_copy(
            src_ref=dst if src is None else src, dst_ref=dst, send_sem=sems[0].at[7 * i + k],
            recv_sem=sems[1].at[7 * i + k], device_id=to, device_id_type=MESH)

    direct = [math.prod(a.shape) * 4 <= DIRECT_GATHER_BYTES for a in arrs]

    def own_copies(ins, outs, sems):
        x, y, c = _place()
        cps = []
        for i in range(n):
            cps.append(copy(sems, outs, i, 0, (x, y, c), (x, y, 1 - c), src=ins[i]))
            for j, (cx, cy) in enumerate(_other_chips(x, y)):
                cps.append(copy(sems, outs, i, 1 + j, (x, y, c), (cx, cy, c), src=ins[i]))
                if direct[i]:
                    cps.append(copy(sems, outs, i, 4 + j, (x, y, c), (cx, cy, 1 - c), src=ins[i]))
        local = [pltpu.make_async_copy(ins[i], outs[i].at[4 * x + 2 * y + c], sems[2].at[i]) for i in range(n)]
        return cps, local

    def start(ins, outs, sems):
        cps, local = own_copies(ins, outs, sems)
        for cp in local + cps:
            cp.start()

    def finish(ins, outs, sems):
        x, y, c = _place()
        passed = []
        for j, (cx, cy) in enumerate(_other_chips(x, y)):
            for i in range(n):
                copy(sems, outs, i, 1 + j, (cx, cy, c), (x, y, c)).wait_recv()
                if not direct[i]:
                    cp = copy(sems, outs, i, 4 + j, (cx, cy, c), (x, y, 1 - c))
                    cp.start()
                    passed.append(cp)
        for i in range(n):
            copy(sems, outs, i, 0, (x, y, 1 - c), (x, y, c)).wait_recv()
            for j, (cx, cy) in enumerate(_other_chips(x, y)):
                copy(sems, outs, i, 4 + j, (cx, cy, 1 - c), (x, y, c)).wait_recv()
        cps, local = own_copies(ins, outs, sems)
        for cp in cps + passed:
            cp.wait_send()
        for cp in local:
            cp.wait()

    return _Exchange(arrs, [SDS((N_DEV,) + a.shape, F32) for a in arrs],
                     [pltpu.SemaphoreType.DMA((7 * n,)), pltpu.SemaphoreType.DMA((7 * n,)),
                      pltpu.SemaphoreType.DMA((n,))], start, finish)


def _local_step(x, p, tgt, wb, ws, shards=None, opt=None):
    bsz, seq, _ = x.shape
    t = bsz * seq
    tm = min(256, t)
    tb = min(S5_TIME_BLOCK, seq)
    x0 = x.reshape(t, D_MODEL)
    p0 = p.reshape(t, PLE_DIM)
    tg = tgt.reshape(t, D_MODEL)
    row = lambda v: v.reshape(1, -1)
    dist = shards is not None
    wb = dict(wb)
    recv, sums, other, gathered = {}, {}, {}, {}
    gb = {}
    gs = {}
    shape_of, axis_of = {}, {}
    chip = None
    if dist:
        shape_of = {k: tuple(shards[k].shape) for k in BIG}
        axis_of = dict(BIG_AXIS)
        for q in range(LAST_PIECES):
            shape_of[LAST_PIECE % q] = (D_MODEL // LAST_PIECES, shape_of["ffn1_w_in"][1])
            axis_of[LAST_PIECE % q] = 1
        xi, yi, ci = _place()
        chip = (2 * xi + yi).astype(jnp.int32).reshape(1)
        ids = jnp.stack([2 * xi + yi] + [2 * cx + cy for cx, cy in _other_chips(xi, yi)] + [ci]).astype(jnp.int32)
    halfbuf, pre = {}, {}

    def gather(names):
        return _gather_weights([shards[k] for k in names], [GATHER_AXIS[k] for k in names]) if dist else None

    def exchange(scat=(), swap=(), halves=(), scat2=(), swap2=(), extra=None, after=None):
        if not dist:
            return None, []
        after = order[0] if after is None else after
        parts, tags = [], []
        if scat:
            parts.append(_scatter_grads([gb[k][1] for k in scat], [shape_of[k] for k in scat],
                                        [axis_of[k] for k in scat]))
            tags.append((recv, scat))
        if swap:
            for k in swap:
                sums[k] = order[0] = _sum_blocks(gb[k][0], recv[k], shape_of[k], axis_of[k], chip, "sum_" + k,
                                                 order[0])
            parts.append(_swap_with_sibling([sums[k] for k in swap]))
            tags.append((other, swap))
        if halves:
            parts.append(_swap_halves([gb[k][1] for k in halves], [shape_of[k] for k in halves],
                                      [axis_of[k] for k in halves]))
            tags.append((halfbuf, halves))
        if scat2:
            for k in scat2:
                pre[k] = _presum(gb[k][0], halfbuf[k], shape_of[k], axis_of[k], ids, "presum_" + k, order[0])
                order[0] = pre[k][0]
            parts.append(_scatter_halves([pre[k][1] for k in scat2], [shape_of[k] for k in scat2]))
            tags.append((recv, scat2))
        if swap2:
            for k in swap2:
                sums[k] = order[0] = _sum_half(pre[k][0], recv[k], "sum_" + k, order[0])
            parts.append(_swap_with_sibling([sums[k] for k in swap2]))
            tags.append((other, swap2))
        if extra is not None:
            parts.append(extra[0])
            tags.append((extra[1], extra[2]))
        return (_join(parts), tags) if parts else (None, [])

    def take(ex_tags, got):
        ex, tags = ex_tags
        if ex is not None:
            for (dst, names), (o0, o1) in zip(tags, ex.cuts):
                dst.update(zip(names, got[o0:o1]))

    order = [None]

    def ordered(builder, *args, **kw):
        res = builder(*args, bg=order[0] if dist else None, **kw)
        order[0] = res[0][0]
        return res

    launched = []

    def launch(ex_tags):
        if ex_tags[0] is not None:
            n = len(launched)
            launched.append(n)
            take(ex_tags, _run_exchange_on_sequencer(ex_tags[0], "reduce_%d" % n, REDUCE_FIRST_ID + n))

    small_shape = {k: _small_view(k, v).shape for k, v in ws.items()}
    small_shape["loss_rows"] = (1, D_MODEL)
    ws = {k: v if (v.ndim == 2 and k != "ssm_log_dt") else v[0] for k, v in ws.items()}
    tril = jnp.tril(jnp.ones((CHUNK, CHUNK), dtype=bool))
    wsm = jnp.where(tril[None], ws["gmlp_w_s"], 0.0)
    wsm_b = wsm.astype(BF16)
    wsmt_b = wsm.transpose(0, 2, 1).astype(BF16)
    bias = jnp.repeat(ws["gmlp_b_s"].T, GMLP_HEAD_DIM, axis=1)

    tf = min(512, t)
    if dist:
        for gi, names in enumerate(GATHER_ORDER):
            wb.update(zip(names, _run_exchange_on_sequencer(gather(names), "gather_%d" % gi, GATHER_FIRST_ID + gi)))
    (x0b, h1, a1), _ = _ffn_proj(x0, wb["ffn1_w_in"], tf, "ffn1_proj")
    (x1, xh1, rstd1), _ = _ffn_out(x0, a1, wb["ffn1_w_out"], row(ws["ln1_g"]), row(ws["ln1_b"]), tf, "ffn1_out")
    sp = _s5_setup(ws["ssm_lambda_re"], ws["ssm_lambda_im"], ws["ssm_log_dt"], ws["ssm_b_re"],
                   ws["ssm_b_im"], ws["ssm_c_re"], ws["ssm_c_im"], ws["ssm_d"], wb["ssm_glu_w"],
                   ws["ssm_glu_b"], tb)
    (x1b, za, zuv, gab), _ = _mixin_fwd(x1, wb["mix_w_in"], tf)
    (s5o, s5ot, y2p, carries), _ = _s5_fwd(za, sp, bsz, seq, tb)
    (gm, gmt), _ = _gmlp_fwd(zuv, row(ws["gmlp_ln_g"]), row(ws["gmlp_ln_b"]), wsm_b, bias)
    (x2, xh2, rstd2), _ = _mixout_fwd(x1, s5o, gm, gab, wb["up_a"], wb["up_b"], wb["mix_w_out"],
                                           row(ws["ln2_g"]), row(ws["ln2_b"]), tf)
    (x2b, h2, a2), _ = _ffn_proj(x2, wb["ffn2_w_in"], tf, "ffn2_proj")
    (x3, xh3, rstd3), _ = _ffn_out(x2, a2, wb["ffn2_w_out"], row(ws["ln3_g"]), row(ws["ln3_b"]), tf, "ffn2_out")
    (dx3, x3b, pb, dq, de, loss_rows), _ = _ple_loss(x3, p0, tg, wb["ple_w_gate"], wb["ple_w_proj"], tf)
    order[0] = dx3
    gb["ple_w_gate"], _ = ordered(_tn_matmul, x3b, dq, "dw_ple_gate", 1024, 1024, a_t=True)
    gb["ple_w_proj"], _ = ordered(_tn_matmul, pb, de, "dw_ple_proj", 256, 1024, a_t=True)
    launch(exchange(scat=("ple_w_gate", "ple_w_proj")))
    (dx2, dh2, df2, gs["ln3_g"], gs["ln3_b"]), _ = ordered(
        _ffn_bwd, dx3, xh3, rstd3, h2, wb["ffn2_w_in"], wb["ffn2_w_out"], row(ws["ln3_g"]), tm, "ffn2_bwd")
    gb["ffn2_w_out"], _ = ordered(_tn_matmul, a2, df2, "dw_ffn2_out", 1408, 1024)
    launch(exchange(scat=("ffn2_w_out",)))
    gb["ffn2_w_in"], _ = ordered(_tn_matmul, x2b, dh2, "dw_ffn2_in", 1024, 1408, a_t=True)
    launch(exchange(scat=("ffn2_w_in",), swap=("ple_w_gate", "ple_w_proj")))
    (dx1a, dmx, mb, dya, dyb, ds5, dgm, dgab, gs["ln2_g"], gs["ln2_b"]), _ = ordered(
        _mixout_bwd, dx2, xh2, rstd2, s5o, gm, gab, wb["up_a"], wb["up_b"], wb["mix_w_out"], row(ws["ln2_g"]), tf)
    gb["mix_w_out"], _ = ordered(_tn_matmul, mb, dmx, "dw_mix_out", 1024, 1024, a_t=True)
    gb["up_a"], _ = ordered(_tn_matmul, s5ot, dya, "dw_up_a", 512, 1024, a_t=True)
    gb["up_b"], _ = ordered(_tn_matmul, gmt, dyb, "dw_up_b", 512, 1024, a_t=True)
    launch(exchange(scat=("mix_w_out", "up_a", "up_b"), swap=("ffn2_w_out",)))
    (dza, dmr, dmi, dnr, dni, da, ddsk, dgw, dgb), _ = ordered(_s5_bwd, za, y2p, ds5, carries, sp, bsz, seq, tb)
    gb["ssm_glu_w"] = (dgw, dgw.astype(BF16))
    launch(exchange(scat=("ssm_glu_w",), swap=("ffn2_w_in",)))
    (dzuv, dws, dbias, gs["gmlp_ln_g"], gs["gmlp_ln_b"]), _ = ordered(
        _gmlp_bwd, zuv, dgm, row(ws["gmlp_ln_g"]), row(ws["gmlp_ln_b"]), wsm_b, wsmt_b, bias)
    (dx1,), _ = ordered(_mixin_bwd, dx1a, dza, dzuv, dgab, wb["mix_w_in"], tf)
    g_mi, _ = ordered(_tn_matmul, x1b, dza, "dw_mix_in_a", 1024, 512, 0, 3584, a_t=True)
    g_mi, _ = ordered(_tn_matmul, x1b, dzuv, "dw_mix_in_uv", 1024, 512, 1, 3584, g_mi, a_t=True)
    gb["mix_w_in"], _ = ordered(_tn_matmul, x1b, dgab, "dw_mix_in_g", 1024, 512, 3, 3584, g_mi, a_t=True)
    launch(exchange(swap=("mix_w_out", "up_a", "up_b", "ssm_glu_w")))

    d_abr = da[0].sum(axis=0).reshape(SSM_GROUPS, SSM_STATE)
    d_abi = da[1].sum(axis=0).reshape(SSM_GROUPS, SSM_STATE)
    _, vjp = jax.vjp(_s5_discretise, ws["ssm_lambda_re"], ws["ssm_lambda_im"], ws["ssm_log_dt"],
                     ws["ssm_b_re"], ws["ssm_b_im"])
    (gs["ssm_lambda_re"], gs["ssm_lambda_im"], gs["ssm_log_dt"], gs["ssm_b_re"], gs["ssm_b_im"]) = vjp(
        (d_abr, d_abi, _block_diag_in_t(dmr), _block_diag_in_t(dmi)))
    gs["ssm_c_re"] = _block_diag_out_t(dnr)
    gs["ssm_c_im"] = _block_diag_out_t(dni)
    gs["ssm_d"] = ddsk
    gs["ssm_glu_b"] = dgb
    gs["gmlp_w_s"] = dws
    gs["gmlp_b_s"] = dbias.reshape(CHUNK, GMLP_HEADS, GMLP_HEAD_DIM).sum(axis=-1).T
    gs["loss_rows"] = loss_rows

    def small_gather(names):
        return (_gather_small([gs[k].reshape(small_shape[k]) for k in names]), gathered, names) if dist else None

    late = ("ln1_g", "ln1_b")
    launch(exchange(scat=("mix_w_in",), extra=small_gather(tuple(k for k in SMALL + ("loss_rows",) if k not in late))))
    (dx0, dh1, df1, gs["ln1_g"], gs["ln1_b"]), _ = ordered(
        _ffn_bwd, dx1, xh1, rstd1, h1, wb["ffn1_w_in"], wb["ffn1_w_out"], row(ws["ln1_g"]), tm, "ffn1_bwd")
    grad_x = dx0.reshape(bsz, seq, D_MODEL)
    if not dist:
        gb["ffn1_w_out"], _ = _tn_matmul(a1, df1, "dw_ffn1_out", 1408, 1024)
        gb["ffn1_w_in"], _ = _tn_matmul(x0b, dh1, "dw_ffn1_in", 1024, 1408, a_t=True)
        return (loss_rows, grad_x, gb, {k: gs[k].reshape(small_shape[k]) for k in SMALL}, sums, other, gathered,
                None, {})
    launch(exchange(extra=small_gather(late)))
    gb["ffn1_w_out"], _ = ordered(_tn_matmul, a1, df1, "dw_ffn1_out", 1408, 1024)
    last = ["ffn1_w_out"] + [LAST_PIECE % q for q in range(LAST_PIECES)]
    fillers = (("ffn2_w_in", "mix_w_in", "ple_w_gate"),
               ("ffn2_w_out", "mix_w_out", "up_a", "up_b", "ssm_glu_w", "ple_w_proj"))
    out = {}
    for i in range(1, len(last) + 3):
        stage = lambda d: tuple(last[i - d:i - d + 1]) if 0 <= i - d < len(last) else ()
        launch(exchange(halves=stage(1), scat2=stage(2), swap2=stage(3), swap=("mix_w_in",) if i == 2 else ()))
        if i < len(last):
            gb[last[i]], _ = ordered(_tn_matmul, x0b, dh1, "dw_" + last[i], D_MODEL // LAST_PIECES, 1408,
                                     a_cols=(i - 1, 1), a_t=True)
        elif i - len(last) < len(fillers):
            for k in fillers[i - len(last)]:
                w, m, v = opt[k]
                out[k] = _adam_big(w, sums[k], other[k], m, v, "adam_" + k, after=order[0])
                order[0] = out[k][1]
    return loss_rows, grad_x, gb, gs, sums, other, gathered, ids, out


def _adamw(w, g, m, v):
    m = ADAM_B1 * m + (1.0 - ADAM_B1) * g
    v = ADAM_B2 * v + (1.0 - ADAM_B2) * (g * g)
    m_hat = m / ADAM_C1
    v_hat = v / ADAM_C2
    delta = -ADAM_LR * (m_hat / (jnp.sqrt(v_hat) + ADAM_EPS) + ADAM_WD * w)
    return delta, m, v


def _pinned(after):
    return ([pl.BlockSpec(memory_space=pl.ANY)], [after]) if after is not None else ([], [])


def _sum_blocks(part, recv, shape, axis, chip, name, after=None):
    r, c = shape
    rb = r // ROW_STEPS

    def body(chip_ref, p_ref, r_ref, *rest):
        rest[-1][...] = (p_ref[...] + r_ref[0].astype(F32) + r_ref[1].astype(F32) + r_ref[2].astype(F32))

    if axis == 0:
        own = pl.BlockSpec((rb, c), lambda i, k: (k[0] * ROW_STEPS + i, 0))
    else:
        own = pl.BlockSpec((rb, c), lambda i, k: (i, k[0]))
    pin_specs, pin_args = _pinned(after)
    grid_spec = pltpu.PrefetchScalarGridSpec(
        num_scalar_prefetch=1, grid=(ROW_STEPS,),
        in_specs=[own, pl.BlockSpec((3, rb, c), lambda i, k: (0, i, 0))] + pin_specs,
        out_specs=pl.BlockSpec((rb, c), lambda i, k: (i, 0)))
    return pl.pallas_call(body, name=name, out_shape=SDS((r, c), F32), grid_spec=grid_spec,
                          compiler_params=_params(("parallel",)))(chip, part, recv, *pin_args)


def _presum(part, half, shape, axis, ids, name, after=None):
    r, c = shape
    rb = r // 4

    def body(ids_ref, p_ref, h_ref, *rest):
        of_ref, ob_ref = rest[-2:]
        s = p_ref[...] + h_ref[...].astype(F32)
        ob_ref[...] = s.astype(BF16)

        @pl.when(pl.program_id(1) == 0)
        def _():
            of_ref[...] = s

    if axis == 0:
        p_spec = pl.BlockSpec((rb, c), lambda i, t, ids: (ids[t] * 4 + ids[4] * 2 + i, 0))
        h_spec = pl.BlockSpec((None, rb, c), lambda i, t, ids: (ids[t], i, 0))
    else:
        p_spec = pl.BlockSpec((rb, c), lambda i, t, ids: (ids[4] * 2 + i, ids[t]))
        h_spec = pl.BlockSpec((rb, c), lambda i, t, ids: (i, ids[t]))
    pin_specs, pin_args = _pinned(after)
    grid_spec = pltpu.PrefetchScalarGridSpec(
        num_scalar_prefetch=1, grid=(2, 4), in_specs=[p_spec, h_spec] + pin_specs,
        out_specs=(pl.BlockSpec((rb, c), lambda i, t, ids: (i, 0)),
                   pl.BlockSpec((None, rb, c), lambda i, t, ids: (t, i, 0))))
    return pl.pallas_call(body, name=name, out_shape=(SDS((r // 2, c), F32), SDS((4, r // 2, c), BF16)),
                          grid_spec=grid_spec,
                          compiler_params=_params(("parallel", "arbitrary")))(ids, part, half, *pin_args)


def _sum_half(pre, recv, name, after=None):
    hr, c = pre.shape
    rb = hr // 2

    def body(p_ref, r_ref, *rest):
        rest[-1][...] = (p_ref[...] + r_ref[0].astype(F32) + r_ref[1].astype(F32) + r_ref[2].astype(F32))

    spec = pl.BlockSpec((rb, c), lambda i: (i, 0))
    pin_specs, pin_args = _pinned(after)
    return pl.pallas_call(body, name=name, grid=(2,), out_shape=SDS((hr, c), F32),
                          in_specs=[spec, pl.BlockSpec((3, rb, c), lambda i: (0, i, 0))] + pin_specs,
                          out_specs=spec, compiler_params=_params(("parallel",)))(pre, recv, *pin_args)


def _adam_halves(w, mine, oth, m, v, ids, name, piece=0, prev=None):
    r, c = w.shape
    rb = mine.shape[0] // 2

    def body(ids_ref, w_ref, a_ref, b_ref, m_ref, v_ref, *rest):
        g_ref, d_ref, nm_ref, nv_ref = rest[-4:]
        g = jnp.where(pl.program_id(0) // 2 == ids_ref[4], a_ref[...], b_ref[...])
        g_ref[...] = g
        d_ref[...], nm_ref[...], nv_ref[...] = _adamw(w_ref[...], g, m_ref[...], v_ref[...])

    whole = pl.BlockSpec((rb, c), lambda i, ids: (i + 4 * piece, 0))
    part = pl.BlockSpec((rb, c), lambda i, ids: (i % 2, 0))
    in_specs = [whole, part, part, whole, whole]
    args = [w, mine, oth, m, v]
    aliases = {}
    if prev is not None:
        in_specs += [pl.BlockSpec(memory_space=pl.ANY)] * 4
        args += list(prev)
        aliases = {6: 0, 7: 1, 8: 2, 9: 3}
    grid_spec = pltpu.PrefetchScalarGridSpec(num_scalar_prefetch=1, grid=(4,), in_specs=in_specs,
                                             out_specs=(whole,) * 4)
    return pl.pallas_call(body, name=name, out_shape=tuple(SDS((r, c), F32) for _ in range(4)),
                          grid_spec=grid_spec, input_output_aliases=aliases,
                          compiler_params=_params(("parallel",)))(ids, *args)


def _adam_big(w, ga, gb, m, v, name, piece=0, prev=None, after=None):
    r, c = w.shape
    pr = ga.shape[0]
    steps = ROW_STEPS if pr == r else 2
    rb = pr // steps
    off = piece * steps

    def body(w_ref, ga_ref, gb_ref, m_ref, v_ref, *rest):
        g_ref, d_ref, nm_ref, nv_ref = rest[-4:]
        g = ga_ref[...] + gb_ref[...]
        g_ref[...] = g
        d_ref[...], nm_ref[...], nv_ref[...] = _adamw(w_ref[...], g, m_ref[...], v_ref[...])

    whole = pl.BlockSpec((rb, c), lambda i: (i + off, 0))
    part = pl.BlockSpec((rb, c), lambda i: (i, 0))
    in_specs = [whole, part, part, whole, whole]
    args = [w, ga, gb, m, v]
    aliases = {}
    if prev is not None:
        in_specs += [pl.BlockSpec(memory_space=pl.ANY)] * 4
        args += list(prev)
        aliases = {5: 0, 6: 1, 7: 2, 8: 3}
    if after is not None:
        in_specs.append(pl.BlockSpec(memory_space=pl.ANY))
        args.append(after)
    return pl.pallas_call(
        body, name=name, grid=(steps,), out_shape=tuple(SDS((r, c), F32) for _ in range(4)),
        in_specs=in_specs, out_specs=(whole,) * 4, input_output_aliases=aliases,
        compiler_params=_params(("parallel",)),
    )(*args)


def _adam_small(ws, gathered, ms, vs):
    n = len(ws)

    def body(*refs):
        w_refs, g_refs, m_refs, v_refs = refs[:n], refs[n:2 * n], refs[2 * n:3 * n], refs[3 * n:4 * n]
        outs = refs[4 * n:]
        for i in range(n):
            g = g_refs[i][0]
            for d in range(1, N_DEV):
                g = g + g_refs[i][d]
            delta, nm, nv = _adamw(w_refs[i][...], g, m_refs[i][...], v_refs[i][...])
            outs[i][...] = g
            outs[n + i][...] = delta
            outs[2 * n + i][...] = nm
            outs[3 * n + i][...] = nv

    vmem = pl.BlockSpec(memory_space=pltpu.VMEM)
    shapes = [w.shape for w in ws]
    return pl.pallas_call(
        body, name="adam_small", out_shape=tuple(SDS(s, F32) for s in shapes * 4),
        in_specs=[vmem] * (4 * n), out_specs=tuple([vmem] * (4 * n)),
        compiler_params=pltpu.CompilerParams(vmem_limit_bytes=VMEM_LIMIT_BYTES),
    )(*ws, *gathered, *ms, *vs)


def _sum_loss(gathered):
    def body(g_ref, o_ref):
        tot = g_ref[0]
        for d in range(1, N_DEV):
            tot = tot + g_ref[d]
        o_ref[...] = (0.5 / D_MODEL) * jnp.sum(tot, axis=1, keepdims=True)

    vmem = pl.BlockSpec(memory_space=pltpu.VMEM)
    return pl.pallas_call(body, name="sum_loss", out_shape=SDS((1, 1), F32), in_specs=[vmem],
                          out_specs=vmem)(gathered)


def kernel(x, p, ffn1_w_in, ffn1_w_out, ln1_g, ln1_b, mix_w_in, ssm_lambda_re, ssm_lambda_im, ssm_log_dt, ssm_b_re, ssm_b_im, ssm_c_re, ssm_c_im, ssm_d, ssm_glu_w, ssm_glu_b, gmlp_ln_g, gmlp_ln_b, gmlp_w_s, gmlp_b_s, up_a, up_b, mix_w_out, ln2_g, ln2_b, ffn2_w_in, ffn2_w_out, ln3_g, ln3_b, ple_w_proj, ple_w_gate, loss_target, m_ffn1_w_in, m_ffn1_w_out, m_ln1_g, m_ln1_b, m_mix_w_in, m_ssm_lambda_re, m_ssm_lambda_im, m_ssm_log_dt, m_ssm_b_re, m_ssm_b_im, m_ssm_c_re, m_ssm_c_im, m_ssm_d, m_ssm_glu_w, m_ssm_glu_b, m_gmlp_ln_g, m_gmlp_ln_b, m_gmlp_w_s, m_gmlp_b_s, m_up_a, m_up_b, m_mix_w_out, m_ln2_g, m_ln2_b, m_ffn2_w_in, m_ffn2_w_out, m_ln3_g, m_ln3_b, m_ple_w_proj, m_ple_w_gate, v_ffn1_w_in, v_ffn1_w_out, v_ln1_g, v_ln1_b, v_mix_w_in, v_ssm_lambda_re, v_ssm_lambda_im, v_ssm_log_dt, v_ssm_b_re, v_ssm_b_im, v_ssm_c_re, v_ssm_c_im, v_ssm_d, v_ssm_glu_w, v_ssm_glu_b, v_gmlp_ln_g, v_gmlp_ln_b, v_gmlp_w_s, v_gmlp_b_s, v_up_a, v_up_b, v_mix_w_out, v_ln2_g, v_ln2_b, v_ffn2_w_in, v_ffn2_w_out, v_ln3_g, v_ln3_b, v_ple_w_proj, v_ple_w_gate):
    given = dict(locals())
    order = ("ffn1_w_in", "ffn1_w_out", "ln1_g", "ln1_b", "mix_w_in", "ssm_lambda_re", "ssm_lambda_im",
             "ssm_log_dt", "ssm_b_re", "ssm_b_im", "ssm_c_re", "ssm_c_im", "ssm_d", "ssm_glu_w", "ssm_glu_b",
             "gmlp_ln_g", "gmlp_ln_b", "gmlp_w_s", "gmlp_b_s", "up_a", "up_b", "mix_w_out", "ln2_g", "ln2_b",
             "ffn2_w_in", "ffn2_w_out", "ln3_g", "ln3_b", "ple_w_proj", "ple_w_gate")
    assert set(order) == set(BIG + SMALL)

    shard = {k: given[k][0] for k in BIG}
    shard_b = {k: shard[k].astype(BF16) for k in BIG}
    opt = {k: (shard[k], given["m_" + k][0], given["v_" + k][0]) for k in BIG}
    loss_rows, grad_x, gb, gs, sums, other, gathered, ids, out = _local_step(
        x, given["p"][0], loss_target, {}, {k: given[k] for k in SMALL}, shard_b, opt)

    out = dict(out)
    for k in BIG:
        if k in out:
            continue
        moments = (given["m_" + k][0], given["v_" + k][0])
        if k == "ffn1_w_out":
            out[k] = _adam_halves(shard[k], sums[k], other[k], *moments, ids, "adam_" + k)
        elif k == "ffn1_w_in":
            for q in range(LAST_PIECES):
                kq = LAST_PIECE % q
                out[k] = _adam_halves(shard[k], sums[kq], other[kq], *moments, ids, "adam_" + kq, q, out.get(k))
        else:
            out[k] = _adam_big(shard[k], sums[k], other[k], *moments, "adam_" + k,
                               after=gb[LAST_PIECE % (LAST_PIECES - 1)][0])

    res = _adam_small([_small_view(k, given[k]) for k in SMALL], [gathered[k] for k in SMALL],
                      [_small_view(k, given["m_" + k]) for k in SMALL],
                      [_small_view(k, given["v_" + k]) for k in SMALL])
    ns = len(SMALL)
    for i, k in enumerate(SMALL):
        out[k] = tuple(res[j * ns + i].reshape(given[k].shape) for j in range(4))
    loss = _sum_loss(gathered["loss_rows"]).reshape(())

    lead = lambda k, j: out[k][j][None] if k in BIG else out[k][j]
    return (loss, grad_x, *[lead(k, 0) for k in order], *[lead(k, 1) for k in order],
            *[lead(k, 2) for k in order], *[lead(k, 3) for k in order])
```

```python
import math

import jax
import jax.numpy as jnp
from jax import lax
from jax.experimental import pallas as pl
from jax.experimental.pallas import tpu as pltpu
from jax.experimental.pallas import tpu_sc as plsc

F32 = jnp.float32
BF16 = jnp.bfloat16
MESH = pl.DeviceIdType.MESH
SDS = jax.ShapeDtypeStruct

D_MODEL = 1024
D_FF = 2816
D_SSM = 512
D_GMLP = 512
SSM_GROUPS = 32
SSM_GROUP_CH = 16
SSM_STATE = 64
SSM_LANES = SSM_GROUPS * SSM_STATE
GMLP_HEADS = 8
GMLP_HEAD_DIM = 64
CHUNK = 128
PLE_DIM = 256
LN_EPS = 1e-5
ALPHA = 2.0 ** 0.25

ADAM_LR = 0.001
ADAM_B1 = 0.9
ADAM_B2 = 0.999
ADAM_EPS = 1e-08
ADAM_WD = 0.01
ADAM_STEP = 10
ADAM_C1 = 1.0 - ADAM_B1 ** ADAM_STEP
ADAM_C2 = 1.0 - ADAM_B2 ** ADAM_STEP

N_DEV = 8
VMEM_LIMIT_BYTES = 56 * 1024 * 1024
FFN_COLS = 1408
S5_BLOCKS = 4
S5_BLOCK_IN = D_SSM // S5_BLOCKS
S5_BLOCK_ST = SSM_LANES // S5_BLOCKS
SCAN_LANES = 512
S5_TIME_BLOCK = 512
TN_K_BLOCK = 2048
TN_SMALL_BLOCK = 1024 * 1024
GMLP_CHUNKS = 4
ROW_STEPS = 4
LAST_PIECES = 2
LAST_PIECE = "ffn1_w_in_q%d"
_G0 = math.sqrt(2.0 / math.pi)
_G1 = 0.044715


def _dot(a, b):
    return jnp.dot(a, b, preferred_element_type=F32)


def _dot_nt(a, b):
    return lax.dot_general(a, b, (((1,), (1,)), ((), ())), preferred_element_type=F32)


def _dot_tn(a, b):
    return lax.dot_general(a, b, (((0,), (0,)), ((), ())), preferred_element_type=F32)


def _sigmoid(x):
    return 1.0 / (1.0 + jnp.exp(-x))


def _gelu(x):
    t = jnp.tanh(_G0 * (x + _G1 * x * x * x))
    return 0.5 * x * (1.0 + t)


def _gelu_grad(x):
    t = jnp.tanh(_G0 * (x + _G1 * x * x * x))
    return 0.5 * (1.0 + t) + 0.5 * x * (1.0 - t * t) * _G0 * (1.0 + 3.0 * _G1 * x * x)


def _ln_fwd(r, g, b):
    mu = jnp.mean(r, axis=-1, keepdims=True)
    d = r - mu
    var = jnp.mean(d * d, axis=-1, keepdims=True)
    rstd = lax.rsqrt(var + LN_EPS)
    xh = d * rstd
    return xh * g + b, xh, rstd


def _ln_bwd(dy, xh, rstd, g):
    dxh = dy * g
    m1 = jnp.mean(dxh, axis=-1, keepdims=True)
    m2 = jnp.mean(dxh * xh, axis=-1, keepdims=True)
    return rstd * (dxh - m1 - xh * m2)


def _resident(shape):
    nd = len(shape)
    return pl.BlockSpec(shape, lambda *_: (0,) * nd, pipeline_mode=pl.Buffered(1))


def _fixed(shape):
    nd = len(shape)
    return pl.BlockSpec(shape, lambda *_: (0,) * nd)


def _rows(tm, cols):
    return pl.BlockSpec((tm, cols), lambda i: (i, 0))


def _cols(rows, tm):
    return pl.BlockSpec((rows, tm), lambda i: (0, i))


def _params(sem):
    return pltpu.CompilerParams(dimension_semantics=sem, vmem_limit_bytes=VMEM_LIMIT_BYTES)


class _Exchange:
    def __init__(self, args, out_shape, sems, start, finish):
        self.args, self.out_shape, self.sems = list(args), list(out_shape), list(sems)
        self.start, self.finish = start, finish
        self.cuts = [(0, len(self.out_shape))]


def _call(body, name, grid, in_specs, out_specs, out_shape, args, scratch=(), sem=None, bg=None, aliases=None):
    aliases = {} if aliases is None else aliases
    in_specs, args = list(in_specs), list(args)
    fn = body
    if bg is not None:
        n_args = len(args)

        def fn(*refs):
            body(*refs[:n_args], *refs[n_args + 1:])

        in_specs.append(pl.BlockSpec(memory_space=pl.ANY))
        args.append(bg)
    res = pl.pallas_call(fn, name=name, grid=grid, out_shape=tuple(out_shape), in_specs=in_specs,
                         out_specs=tuple(out_specs), scratch_shapes=list(scratch),
                         input_output_aliases=aliases, compiler_params=_params(sem))(*args)
    return tuple(res), ()


def _run_exchange_on_sequencer(ex, name, collective_id):
    n_i, n_o = len(ex.args), len(ex.out_shape)

    def body(*refs):
        ins, outs, sems = refs[:n_i], refs[n_i:n_i + n_o], refs[n_i + n_o:]
        x, y, c = lax.axis_index("x"), lax.axis_index("y"), lax.axis_index("c")
        barrier = pltpu.get_barrier_semaphore()
        for peer in [(x, y, 1 - c), (1 - x, y, c), (x, 1 - y, c), (1 - x, 1 - y, c)]:
            pl.semaphore_signal(barrier, inc=1, device_id=peer, device_id_type=MESH)
        pl.semaphore_wait(barrier, 4)
        ex.start(ins, outs, sems)
        ex.finish(ins, outs, sems)

    return tuple(pl.kernel(body, out_type=tuple(ex.out_shape),
                           mesh=plsc.ScalarSubcoreMesh(axis_name="sequencer", num_cores=1),
                           scratch_types=list(ex.sems), name=name,
                           compiler_params=pltpu.CompilerParams(collective_id=collective_id))(*ex.args))


def _join(exchanges):
    cuts = []
    a = o = q = 0
    for e in exchanges:
        cuts.append((a, a + len(e.args), o, o + len(e.out_shape), q, q + len(e.sems)))
        a, o, q = cuts[-1][1], cuts[-1][3], cuts[-1][5]

    def start(ins, outs, sems):
        for e, (a0, a1, o0, o1, q0, q1) in zip(exchanges, cuts):
            e.start(ins[a0:a1], outs[o0:o1], sems[q0:q1])

    def finish(ins, outs, sems):
        for e, (a0, a1, o0, o1, q0, q1) in zip(exchanges, cuts):
            e.finish(ins[a0:a1], outs[o0:o1], sems[q0:q1])

    joined = _Exchange(sum((e.args for e in exchanges), []), sum((e.out_shape for e in exchanges), []),
                       sum((e.sems for e in exchanges), []), start, finish)
    joined.cuts = [(c[2], c[3]) for c in cuts]
    return joined


def _ffn_proj(x, w_in, tm, name, bg=None):
    t = x.shape[0]
    nch = D_FF // FFN_COLS

    def body(x_ref, win_ref, xbt_ref, h_ref, a_ref):
        xb = x_ref[...].astype(BF16)
        xbt_ref[...] = xb.T
        for k in range(nch):
            cg = slice(k * FFN_COLS, (k + 1) * FFN_COLS)
            cu = slice(D_FF + k * FFN_COLS, D_FF + (k + 1) * FFN_COLS)
            hg = _dot(xb, win_ref[k])
            hu = _dot(xb, win_ref[nch + k])
            h_ref[:, cg] = hg.astype(BF16)
            h_ref[:, cu] = hu.astype(BF16)
            a_ref[:, cg] = (hg * _sigmoid(hg) * hu).astype(BF16)

    return _call(
        body, name, (t // tm,),
        [_rows(tm, D_MODEL), _resident((2 * nch, D_MODEL, FFN_COLS))],
        (_cols(D_MODEL, tm), _rows(tm, 2 * D_FF), _rows(tm, D_FF)),
        (SDS((D_MODEL, t), BF16), SDS((t, 2 * D_FF), BF16), SDS((t, D_FF), BF16)),
        (x, w_in), sem=("parallel",), bg=bg)


def _ffn_out(x, a, w_out, g, b, tm, name, bg=None):
    t = x.shape[0]

    def body(x_ref, a_ref, wout_ref, g_ref, b_ref, xn_ref, xh_ref, rstd_ref):
        f = _dot(a_ref[...], wout_ref[...])
        y, xh, rstd = _ln_fwd(ALPHA * x_ref[...] + 0.5 * f, g_ref[...], b_ref[...])
        xn_ref[...] = y
        xh_ref[...] = xh
        rstd_ref[...] = rstd

    return _call(
        body, name, (t // tm,),
        [_rows(tm, D_MODEL), _rows(tm, D_FF), _resident((D_FF, D_MODEL)), _fixed((1, D_MODEL)), _fixed((1, D_MODEL))],
        (_rows(tm, D_MODEL), _rows(tm, D_MODEL), _rows(tm, 1)),
        (SDS((t, D_MODEL), F32), SDS((t, D_MODEL), F32), SDS((t, 1), F32)),
        (x, a, w_out, g, b), sem=("parallel",), bg=bg)


def _ffn_bwd(dxn, xh, rstd, h, w_in, w_out, g, tm, name, bg=None):
    t = dxn.shape[0]
    nch = D_FF // FFN_COLS

    def body(dxn_ref, xh_ref, rstd_ref, h_ref, win_ref, wout_ref, g_ref,
             dx_ref, dh_ref, df_ref, dg_ref, db_ref):
        @pl.when(pl.program_id(0) == 0)
        def _():
            dg_ref[...] = jnp.zeros_like(dg_ref)
            db_ref[...] = jnp.zeros_like(db_ref)

        dy = dxn_ref[...]
        xhv = xh_ref[...]
        dr = _ln_bwd(dy, xhv, rstd_ref[...], g_ref[...])
        dg_ref[...] += jnp.sum(dy * xhv, axis=0, keepdims=True)
        db_ref[...] += jnp.sum(dy, axis=0, keepdims=True)
        df = (0.5 * dr).astype(BF16)
        df_ref[...] = df
        dx = ALPHA * dr
        das = [_dot_nt(df, wout_ref[k * FFN_COLS:(k + 1) * FFN_COLS, :]) for k in range(nch)]
        for k in range(nch):
            cg = slice(k * FFN_COLS, (k + 1) * FFN_COLS)
            cu = slice(D_FF + k * FFN_COLS, D_FF + (k + 1) * FFN_COLS)
            hg = h_ref[:, cg].astype(F32)
            hu = h_ref[:, cu].astype(F32)
            sg = _sigmoid(hg)
            silu = hg * sg
            da = das[k]
            dhu = (da * silu).astype(BF16)
            dhg = (da * hu * (sg * (1.0 + hg * (1.0 - sg)))).astype(BF16)
            dh_ref[:, cg] = dhg
            dh_ref[:, cu] = dhu
            dx = dx + _dot_nt(dhg, win_ref[k]) + _dot_nt(dhu, win_ref[nch + k])
        dx_ref[...] = dx

    return _call(
        body, name, (t // tm,),
        [_rows(tm, D_MODEL), _rows(tm, D_MODEL), _rows(tm, 1), _rows(tm, 2 * D_FF),
         _resident((2 * nch, D_MODEL, FFN_COLS)), _resident((D_FF, D_MODEL)), _fixed((1, D_MODEL))],
        (_rows(tm, D_MODEL), _rows(tm, 2 * D_FF), _rows(tm, D_MODEL),
         _fixed((1, D_MODEL)), _fixed((1, D_MODEL))),
        (SDS((t, D_MODEL), F32), SDS((t, 2 * D_FF), BF16), SDS((t, D_MODEL), BF16),
         SDS((1, D_MODEL), F32), SDS((1, D_MODEL), F32)),
        (dxn, xh, rstd, h, w_in, w_out, g), sem=("arbitrary",), bg=bg)


def _tn_matmul(a, b, name, bm, bn, col_block=0, total_cols=None, prev=None, bg=None, a_cols=None, a_t=False):
    t, m = a.shape[::-1] if a_t else a.shape
    a_first = 0
    if a_cols is not None:
        a_first, m = a_cols[0], a_cols[1] * bm
    n = b.shape[1]
    total_cols = n if total_cols is None else total_cols
    whole = bm * bn <= TN_SMALL_BLOCK and (m // bm) * (n // bn) >= 2
    bk = min(2 * TN_K_BLOCK if whole else TN_K_BLOCK, t)
    nk = t // bk
    n_in = 2 if prev is None else 4

    def body(*refs):
        a_ref, b_ref = refs[0], refs[1]
        o_ref, ob_ref = refs[n_in], refs[n_in + 1]
        k = pl.program_id(2)

        @pl.when(k == 0)
        def _():
            o_ref[...] = jnp.zeros_like(o_ref)

        o_ref[...] += _dot(a_ref[...], b_ref[...]) if a_t else _dot_tn(a_ref[...], b_ref[...])

        @pl.when(k == nk - 1)
        def _():
            ob_ref[...] = o_ref[...].astype(BF16)

    a_spec = (pl.BlockSpec((bm, bk), lambda i, j, k: (i + a_first, k)) if a_t
              else pl.BlockSpec((bk, bm), lambda i, j, k: (k, i + a_first)))
    in_specs = [a_spec, pl.BlockSpec((bk, bn), lambda i, j, k: (k, j))]
    args = [a, b]
    aliases = {}
    if prev is not None:
        in_specs += [pl.BlockSpec(memory_space=pl.ANY), pl.BlockSpec(memory_space=pl.ANY)]
        args += list(prev)
        aliases = {2: 0, 3: 1}
        if any(bg is p for p in prev):
            bg = None
    out_spec = pl.BlockSpec((bm, bn), lambda i, j, k: (i, j + col_block))
    return _call(body, name, (m // bm, n // bn, nk), in_specs, (out_spec, out_spec),
                 (SDS((m, total_cols), F32), SDS((m, total_cols), BF16)), args,
                 sem=("parallel", "parallel", "arbitrary"), bg=bg, aliases=aliases)


def _mixin_fwd(x1, w, tm, bg=None):
    t = x1.shape[0]

    def body(x_ref, w_ref, xbt_ref, za_ref, zuv_ref, gab_ref):
        xb = x_ref[...].astype(BF16)
        xbt_ref[...] = xb.T
        za_ref[...] = _dot(xb, w_ref[:, 0:512]).astype(BF16)
        zuv_ref[...] = _dot(xb, w_ref[:, 512:1536]).astype(BF16)
        gab_ref[...] = _dot(xb, w_ref[:, 1536:3584]).astype(BF16)

    return _call(
        body, "mixin_fwd", (t // tm,),
        [_rows(tm, D_MODEL), _resident((D_MODEL, 3584))],
        (_cols(D_MODEL, tm), _rows(tm, 512), _rows(tm, 1024), _rows(tm, 2048)),
        (SDS((D_MODEL, t), BF16), SDS((t, 512), BF16), SDS((t, 1024), BF16), SDS((t, 2048), BF16)),
        (x1, w), sem=("parallel",), bg=bg)


def _mixin_bwd(dx1a, dza, dzuv, dgab, w, tm, bg=None):
    t = dx1a.shape[0]

    def body(d_ref, dza_ref, dzuv_ref, dgab_ref, w_ref, dx_ref):
        dx_ref[...] = (d_ref[...] + _dot_nt(dza_ref[...], w_ref[:, 0:512])
                       + _dot_nt(dzuv_ref[...], w_ref[:, 512:1536])
                       + _dot_nt(dgab_ref[...], w_ref[:, 1536:3584]))

    return _call(
        body, "mixin_bwd", (t // tm,),
        [_rows(tm, D_MODEL), _rows(tm, 512), _rows(tm, 1024), _rows(tm, 2048), _resident((D_MODEL, 3584))],
        (_rows(tm, D_MODEL),), (SDS((t, D_MODEL), F32),),
        (dx1a, dza, dzuv, dgab, w), sem=("parallel",), bg=bg)


def _unrolled(lo, hi, body, carry):
    for j in range(lo, hi):
        carry = body(j, carry)
    return carry


def _scan_fwd(hr_ref, hi_ref, a_ref, ap_ref, carry_ref, seg, cin_ref):
    for lc in range(SSM_LANES // SCAN_LANES):
        ls = slice(lc * SCAN_LANES, (lc + 1) * SCAN_LANES)
        a_r = jnp.broadcast_to(a_ref[0:1, ls], (8, SCAN_LANES))
        a_i = jnp.broadcast_to(a_ref[1:2, ls], (8, SCAN_LANES))

        def step(j, hc, ls=ls, a_r=a_r, a_i=a_i):
            h_r, h_i = hc
            rows = pl.ds(j * 8, 8)
            n_r = a_r * h_r - a_i * h_i + hr_ref[rows, ls]
            n_i = a_r * h_i + a_i * h_r + hi_ref[rows, ls]
            hr_ref[rows, ls] = n_r
            hi_ref[rows, ls] = n_i
            return n_r, n_i

        zero = jnp.zeros((8, SCAN_LANES), F32)
        f_r, f_i = _unrolled(0, seg, step, (zero, zero))
        c_r = carry_ref[0:1, ls]
        c_i = carry_ref[1:2, ls]
        p_r = ap_ref[0:1, ls]
        p_i = ap_ref[1:2, ls]
        rows_r, rows_i = [], []
        for s in range(8):
            rows_r.append(c_r)
            rows_i.append(c_i)
            c_r, c_i = (f_r[s:s + 1] + p_r * c_r - p_i * c_i,
                        f_i[s:s + 1] + p_r * c_i + p_i * c_r)
        carry_ref[0:1, ls] = c_r
        carry_ref[1:2, ls] = c_i
        cin_r = jnp.concatenate(rows_r, axis=0)
        cin_i = jnp.concatenate(rows_i, axis=0)
        if cin_ref is not None:
            cin_ref[0, :, ls] = cin_r
            cin_ref[1, :, ls] = cin_i

        def fix(j, cc, ls=ls, a_r=a_r, a_i=a_i):
            c_r, c_i = cc
            c_r, c_i = a_r * c_r - a_i * c_i, a_r * c_i + a_i * c_r
            rows = pl.ds(j * 8, 8)
            hr_ref[rows, ls] = hr_ref[rows, ls] + c_r
            hi_ref[rows, ls] = hi_ref[rows, ls] + c_i
            return c_r, c_i

        _unrolled(0, seg, fix, (cin_r, cin_i))


def _scan_bwd(gr_ref, gi_ref, hr_ref, hi_ref, cin_ref, a_ref, ap_ref, rcarry_ref, da_ref, seg):
    for lc in range(SSM_LANES // SCAN_LANES):
        ls = slice(lc * SCAN_LANES, (lc + 1) * SCAN_LANES)
        a_r = jnp.broadcast_to(a_ref[0:1, ls], (8, SCAN_LANES))
        a_i = jnp.broadcast_to(a_ref[1:2, ls], (8, SCAN_LANES))

        def step(t, gc, ls=ls, a_r=a_r, a_i=a_i):
            g_r, g_i = gc
            rows = pl.ds((seg - 1 - t) * 8, 8)
            n_r = gr_ref[rows, ls] + a_r * g_r + a_i * g_i
            n_i = gi_ref[rows, ls] + a_r * g_i - a_i * g_r
            gr_ref[rows, ls] = n_r
            gi_ref[rows, ls] = n_i
            return n_r, n_i

        zero = jnp.zeros((8, SCAN_LANES), F32)
        f_r, f_i = _unrolled(0, seg, step, (zero, zero))
        c_r = rcarry_ref[0:1, ls]
        c_i = rcarry_ref[1:2, ls]
        p_r = ap_ref[0:1, ls]
        p_i = ap_ref[1:2, ls]
        rows_r, rows_i = [None] * 8, [None] * 8
        for s in range(7, -1, -1):
            rows_r[s] = c_r
            rows_i[s] = c_i
            c_r, c_i = (f_r[s:s + 1] + p_r * c_r + p_i * c_i,
                        f_i[s:s + 1] + p_r * c_i - p_i * c_r)
        rcarry_ref[0:1, ls] = c_r
        rcarry_ref[1:2, ls] = c_i
        cin_r = jnp.concatenate(rows_r, axis=0)
        cin_i = jnp.concatenate(rows_i, axis=0)

        def fix_row(j_rows, hp_r, hp_i, cc, ls=ls, a_r=a_r, a_i=a_i):
            c_r, c_i, acc_r, acc_i = cc
            c_r, c_i = a_r * c_r + a_i * c_i, a_r * c_i - a_i * c_r
            g_r = gr_ref[j_rows, ls] + c_r
            g_i = gi_ref[j_rows, ls] + c_i
            gr_ref[j_rows, ls] = g_r
            gi_ref[j_rows, ls] = g_i
            acc_r = acc_r + g_r * hp_r + g_i * hp_i
            acc_i = acc_i + g_i * hp_r - g_r * hp_i
            return c_r, c_i, acc_r, acc_i

        def fix(t, cc, ls=ls, fix_row=fix_row):
            j = seg - 1 - t
            rows = pl.ds(j * 8, 8)
            prev = pl.ds((j - 1) * 8, 8)
            return fix_row(rows, hr_ref[prev, ls], hi_ref[prev, ls], cc)

        cc = _unrolled(0, seg - 1, fix, (cin_r, cin_i, zero, zero))
        _, _, acc_r, acc_i = fix_row(pl.ds(0, 8), cin_ref[0, :, ls], cin_ref[1, :, ls], cc)
        da_ref[0, :, ls] += acc_r
        da_ref[1, :, ls] += acc_i


def _s5_fwd(za, sp, bsz, seq, tb, bg=None):
    nb = seq // tb
    seg = tb // 8
    t = bsz * seq

    def body(za_ref, perm_ref, permt_ref, mre_ref, mim_ref, nre_ref, nim_ref, a_ref, ap_ref,
             dsk_ref, gw_ref, gb_ref, out_ref, outt_ref, y2_ref, car_ref, hr_ref, hi_ref, carry_ref):
        @pl.when(pl.program_id(1) == 0)
        def _():
            carry_ref[...] = jnp.zeros_like(carry_ref)

        car_ref[0] = carry_ref[...]
        up = _dot(perm_ref[...], za_ref[...])
        upb = up.astype(BF16)
        for bb in range(S5_BLOCKS):
            ub = upb[:, bb * S5_BLOCK_IN:(bb + 1) * S5_BLOCK_IN]
            st = slice(bb * S5_BLOCK_ST, (bb + 1) * S5_BLOCK_ST)
            hr_ref[:, st] = _dot(ub, mre_ref[bb])
            hi_ref[:, st] = _dot(ub, mim_ref[bb])
        _scan_fwd(hr_ref, hi_ref, a_ref, ap_ref, carry_ref, seg, None)
        ys = []
        for bb in range(S5_BLOCKS):
            st = slice(bb * S5_BLOCK_ST, (bb + 1) * S5_BLOCK_ST)
            ys.append(_dot(hr_ref[:, st].astype(BF16), nre_ref[bb])
                      - _dot(hi_ref[:, st].astype(BF16), nim_ref[bb]))
        y2 = jnp.concatenate(ys, axis=1) + dsk_ref[...] * up
        y2_ref[...] = y2
        y3 = _gelu(y2)
        gl = _dot(y3.astype(BF16), gw_ref[...]) + gb_ref[...]
        oa = y3 * _sigmoid(gl)
        out = _dot(permt_ref[...], oa.astype(BF16)).astype(BF16)
        out_ref[...] = out
        outt_ref[...] = out.T

    blk = pl.BlockSpec((tb, D_SSM), lambda b, j: (b * nb + j, 0))
    blk_t = pl.BlockSpec((D_SSM, tb), lambda b, j: (0, b * nb + j))
    m_shape = (S5_BLOCKS, S5_BLOCK_IN, S5_BLOCK_ST)
    n_shape = (S5_BLOCKS, S5_BLOCK_ST, S5_BLOCK_IN)
    return _call(
        body, "s5_fwd", (bsz, nb),
        [blk, _fixed((tb, tb)), _fixed((tb, tb)), _fixed(m_shape), _fixed(m_shape), _fixed(n_shape),
         _fixed(n_shape), _fixed((2, SSM_LANES)), _fixed((2, SSM_LANES)), _fixed((1, D_SSM)),
         _fixed((D_SSM, D_SSM)), _fixed((1, D_SSM))],
        (blk, blk_t, blk, pl.BlockSpec((1, 2, SSM_LANES), lambda b, j: (b * nb + j, 0, 0))),
        (SDS((t, D_SSM), BF16), SDS((D_SSM, t), BF16), SDS((t, D_SSM), F32), SDS((bsz * nb, 2, SSM_LANES), F32)),
        (za, sp["perm"], sp["permt"], sp["mre"], sp["mim"], sp["nre"], sp["nim"], sp["a"], sp["ap"],
         sp["dskip"], sp["glu_w"], sp["glu_b"]),
        scratch=[pltpu.VMEM((tb, SSM_LANES), F32), pltpu.VMEM((tb, SSM_LANES), F32),
                 pltpu.VMEM((2, SSM_LANES), F32)],
        sem=("arbitrary", "arbitrary"), bg=bg)


def _s5_bwd(za, y2p, doa, carries, sp, bsz, seq, tb, bg=None):
    nb = seq // tb
    seg = tb // 8
    t = bsz * seq

    def body(za_ref, y2_ref, doa_ref, car_ref, perm_ref, permt_ref, mre_ref, mim_ref, mtre_ref, mtim_ref,
             nre_ref, nim_ref, ntre_ref, ntim_ref, a_ref, ap_ref, dsk_ref, gw_ref, gwt_ref, gb_ref,
             dza_ref, dmr_ref, dmi_ref, dnr_ref, dni_ref, da_ref, ddsk_ref, dgw_ref, dgb_ref,
             hr_ref, hi_ref, gr_ref, gi_ref, cin_ref, carry_ref, rcarry_ref):
        first = jnp.logical_and(pl.program_id(0) == 0, pl.program_id(1) == 0)

        @pl.when(first)
        def _():
            for r in (dmr_ref, dmi_ref, dnr_ref, dni_ref, da_ref, ddsk_ref, dgw_ref, dgb_ref):
                r[...] = jnp.zeros_like(r)

        @pl.when(pl.program_id(1) == 0)
        def _():
            rcarry_ref[...] = jnp.zeros_like(rcarry_ref)

        carry_ref[...] = car_ref[0]
        perm = perm_ref[...]
        up = _dot(perm, za_ref[...])
        upb = up.astype(BF16)
        for bb in range(S5_BLOCKS):
            ub = upb[:, bb * S5_BLOCK_IN:(bb + 1) * S5_BLOCK_IN]
            st = slice(bb * S5_BLOCK_ST, (bb + 1) * S5_BLOCK_ST)
            hr_ref[:, st] = _dot(ub, mre_ref[bb])
            hi_ref[:, st] = _dot(ub, mim_ref[bb])
        _scan_fwd(hr_ref, hi_ref, a_ref, ap_ref, carry_ref, seg, cin_ref)

        y2 = y2_ref[...]
        y3 = _gelu(y2)
        y3b = y3.astype(BF16)
        sg = _sigmoid(_dot(y3b, gw_ref[...]) + gb_ref[...])
        d0 = doa_ref[...]
        d_hi = d0.astype(BF16)
        d1 = d0 - d_hi.astype(F32)
        d_mid = d1.astype(BF16)
        d_lo = (d1 - d_mid.astype(F32)).astype(BF16)
        doap = _dot(perm, d_hi) + _dot(perm, d_mid) + _dot(perm, d_lo)
        dgl = doap * y3 * sg * (1.0 - sg)
        dglb = dgl.astype(BF16)
        dy3 = doap * sg + _dot(dglb, gwt_ref[...])
        dgw_ref[...] += _dot_tn(y3b, dglb)
        dgb_ref[...] += jnp.sum(dgl, axis=0, keepdims=True)
        dy2 = dy3 * _gelu_grad(y2)
        ddsk_ref[...] += jnp.sum(dy2 * up, axis=0, keepdims=True)
        dyb = dy2.astype(BF16)
        for bb in range(S5_BLOCKS):
            dyc = dyb[:, bb * S5_BLOCK_IN:(bb + 1) * S5_BLOCK_IN]
            st = slice(bb * S5_BLOCK_ST, (bb + 1) * S5_BLOCK_ST)
            gr_ref[:, st] = _dot(dyc, ntre_ref[bb])
            gi_ref[:, st] = -_dot(dyc, ntim_ref[bb])
            dnr_ref[bb] += _dot_tn(hr_ref[:, st].astype(BF16), dyc)
            dni_ref[bb] += -_dot_tn(hi_ref[:, st].astype(BF16), dyc)
        _scan_bwd(gr_ref, gi_ref, hr_ref, hi_ref, cin_ref, a_ref, ap_ref, rcarry_ref, da_ref, seg)
        dus = []
        for bb in range(S5_BLOCKS):
            st = slice(bb * S5_BLOCK_ST, (bb + 1) * S5_BLOCK_ST)
            grb = gr_ref[:, st].astype(BF16)
            gib = gi_ref[:, st].astype(BF16)
            dus.append(_dot(grb, mtre_ref[bb]) + _dot(gib, mtim_ref[bb]))
            ub = upb[:, bb * S5_BLOCK_IN:(bb + 1) * S5_BLOCK_IN]
            dmr_ref[bb] += _dot_tn(ub, grb)
            dmi_ref[bb] += _dot_tn(ub, gib)
        du = jnp.concatenate(dus, axis=1) + dy2 * dsk_ref[...]
        dza_ref[...] = _dot(permt_ref[...], du.astype(BF16)).astype(BF16)

    def rev(b, j):
        return (b * nb + (nb - 1 - j), 0)

    blk = pl.BlockSpec((tb, D_SSM), rev)
    m_shape = (S5_BLOCKS, S5_BLOCK_IN, S5_BLOCK_ST)
    n_shape = (S5_BLOCKS, S5_BLOCK_ST, S5_BLOCK_IN)
    return _call(
        body, "s5_bwd", (bsz, nb),
        [blk, blk, blk, pl.BlockSpec((1, 2, SSM_LANES), lambda b, j: (b * nb + (nb - 1 - j), 0, 0)),
         _fixed((tb, tb)), _fixed((tb, tb)), _fixed(m_shape), _fixed(m_shape), _fixed(n_shape), _fixed(n_shape),
         _fixed(n_shape), _fixed(n_shape), _fixed(m_shape), _fixed(m_shape),
         _fixed((2, SSM_LANES)), _fixed((2, SSM_LANES)), _fixed((1, D_SSM)),
         _fixed((D_SSM, D_SSM)), _fixed((D_SSM, D_SSM)), _fixed((1, D_SSM))],
        (blk, _fixed(m_shape), _fixed(m_shape), _fixed(n_shape), _fixed(n_shape),
         _fixed((2, 8, SSM_LANES)), _fixed((1, D_SSM)), _fixed((D_SSM, D_SSM)), _fixed((1, D_SSM))),
        (SDS((t, D_SSM), BF16), SDS(m_shape, F32), SDS(m_shape, F32), SDS(n_shape, F32), SDS(n_shape, F32),
         SDS((2, 8, SSM_LANES), F32), SDS((1, D_SSM), F32), SDS((D_SSM, D_SSM), F32), SDS((1, D_SSM), F32)),
        (za, y2p, doa, carries, sp["perm"], sp["permt"], sp["mre"], sp["mim"], sp["mtre"], sp["mtim"],
         sp["nre"], sp["nim"], sp["ntre"], sp["ntim"], sp["a"], sp["ap"], sp["dskip"], sp["glu_w"],
         sp["glu_wt"], sp["glu_b"]),
        scratch=[pltpu.VMEM((tb, SSM_LANES), F32), pltpu.VMEM((tb, SSM_LANES), F32),
                 pltpu.VMEM((tb, SSM_LANES), F32), pltpu.VMEM((tb, SSM_LANES), F32),
                 pltpu.VMEM((2, 8, SSM_LANES), F32), pltpu.VMEM((2, SSM_LANES), F32),
                 pltpu.VMEM((2, SSM_LANES), F32)],
        sem=("arbitrary", "arbitrary"), bg=bg)


def _gmlp_spatial(ws_ref, vb):
    lane = lax.broadcasted_iota(jnp.int32, (CHUNK, 128), 1)
    parts = []
    for j in range(GMLP_HEADS // 2):
        vp = vb[:, 128 * j:128 * (j + 1)]
        parts.append(jnp.where(lane < GMLP_HEAD_DIM, _dot(ws_ref[2 * j], vp), _dot(ws_ref[2 * j + 1], vp)))
    return jnp.concatenate(parts, axis=1)


def _gmlp_fwd(zuv, ln_g, ln_b, wsm, bias, bg=None):
    t = zuv.shape[0]

    def body(z_ref, g_ref, b_ref, ws_ref, bias_ref, out_ref, outt_ref):
        for ch in range(GMLP_CHUNKS):
            rows = slice(ch * CHUNK, (ch + 1) * CHUNK)
            u = _gelu(z_ref[rows, 0:D_GMLP].astype(F32))
            v0 = _gelu(z_ref[rows, D_GMLP:2 * D_GMLP].astype(F32))
            v, _, _ = _ln_fwd(v0, g_ref[...], b_ref[...])
            s = _gmlp_spatial(ws_ref, v.astype(BF16)) + bias_ref[...]
            out = (u * s).astype(BF16)
            out_ref[rows, :] = out
            outt_ref[:, rows] = out.T

    step = GMLP_CHUNKS * CHUNK
    return _call(
        body, "gmlp_fwd", (t // step,),
        [_rows(step, 2 * D_GMLP), _fixed((1, D_GMLP)), _fixed((1, D_GMLP)),
         _fixed((GMLP_HEADS, CHUNK, CHUNK)), _fixed((CHUNK, D_GMLP))],
        (_rows(step, D_GMLP), _cols(D_GMLP, step)), (SDS((t, D_GMLP), BF16), SDS((D_GMLP, t), BF16)),
        (zuv, ln_g, ln_b, wsm, bias), sem=("parallel",), bg=bg)


def _gmlp_bwd(zuv, dgm, ln_g, ln_b, wsm, wsmt, bias, bg=None):
    t = zuv.shape[0]

    def body(z_ref, d_ref, g_ref, b_ref, ws_ref, wst_ref, bias_ref,
             dz_ref, dws_ref, dbias_ref, dg_ref, db_ref):
        @pl.when(pl.program_id(0) == 0)
        def _():
            for r in (dws_ref, dbias_ref, dg_ref, db_ref):
                r[...] = jnp.zeros_like(r)

        gam = g_ref[...]
        lane = lax.broadcasted_iota(jnp.int32, (CHUNK, 128), 1)
        tril = (lax.broadcasted_iota(jnp.int32, (CHUNK, CHUNK), 0)
                >= lax.broadcasted_iota(jnp.int32, (CHUNK, CHUNK), 1))
        zero_b = jnp.zeros((CHUNK, 128), BF16)
        for ch in range(GMLP_CHUNKS):
            rows = slice(ch * CHUNK, (ch + 1) * CHUNK)
            zu = z_ref[rows, 0:D_GMLP].astype(F32)
            zv = z_ref[rows, D_GMLP:2 * D_GMLP].astype(F32)
            u = _gelu(zu)
            v0 = _gelu(zv)
            v, vhat, rstd = _ln_fwd(v0, gam, b_ref[...])
            vb = v.astype(BF16)
            s = _gmlp_spatial(ws_ref, vb) + bias_ref[...]
            d = d_ref[rows, :]
            dz_ref[rows, 0:D_GMLP] = (d * s * _gelu_grad(zu)).astype(BF16)
            ds = d * u
            dbias_ref[...] += ds
            dsb = ds.astype(BF16)
            parts = []
            for j in range(GMLP_HEADS // 2):
                dsp = dsb[:, 128 * j:128 * (j + 1)]
                vp = vb[:, 128 * j:128 * (j + 1)]
                parts.append(jnp.where(lane < GMLP_HEAD_DIM, _dot(wst_ref[2 * j], dsp),
                                       _dot(wst_ref[2 * j + 1], dsp)))
                lo = jnp.where(lane < GMLP_HEAD_DIM, dsp, zero_b)
                hi = jnp.where(lane < GMLP_HEAD_DIM, zero_b, dsp)
                dws_ref[2 * j] += jnp.where(tril, _dot_nt(lo, vp), 0.0)
                dws_ref[2 * j + 1] += jnp.where(tril, _dot_nt(hi, vp), 0.0)
            dv = jnp.concatenate(parts, axis=1)
            dg_ref[...] += jnp.sum(dv * vhat, axis=0, keepdims=True)
            db_ref[...] += jnp.sum(dv, axis=0, keepdims=True)
            dz_ref[rows, D_GMLP:2 * D_GMLP] = (_ln_bwd(dv, vhat, rstd, gam) * _gelu_grad(zv)).astype(BF16)

    step = GMLP_CHUNKS * CHUNK
    return _call(
        body, "gmlp_bwd", (t // step,),
        [_rows(step, 2 * D_GMLP), _rows(step, D_GMLP), _fixed((1, D_GMLP)), _fixed((1, D_GMLP)),
         _fixed((GMLP_HEADS, CHUNK, CHUNK)), _fixed((GMLP_HEADS, CHUNK, CHUNK)), _fixed((CHUNK, D_GMLP))],
        (_rows(step, 2 * D_GMLP), _fixed((GMLP_HEADS, CHUNK, CHUNK)), _fixed((CHUNK, D_GMLP)),
         _fixed((1, D_GMLP)), _fixed((1, D_GMLP))),
        (SDS((t, 2 * D_GMLP), BF16), SDS((GMLP_HEADS, CHUNK, CHUNK), F32), SDS((CHUNK, D_GMLP), F32),
         SDS((1, D_GMLP), F32), SDS((1, D_GMLP), F32)),
        (zuv, dgm, ln_g, ln_b, wsm, wsmt, bias), sem=("arbitrary",), bg=bg)


def _mixout_fwd(x1, s5o, gm, gab, ua, ub, wmo, g, b, tm, bg=None):
    t = x1.shape[0]

    def body(x_ref, s_ref, m_ref, gab_ref, ua_ref, ub_ref, wmo_ref, g_ref, b_ref,
             xn_ref, xh_ref, rstd_ref):
        ya = _dot(s_ref[...], ua_ref[...])
        yb = _dot(m_ref[...], ub_ref[...])
        mix = (_sigmoid(gab_ref[:, 0:D_MODEL].astype(F32)) * ya
               + _sigmoid(gab_ref[:, D_MODEL:2 * D_MODEL].astype(F32)) * yb)
        r = ALPHA * x_ref[...] + _dot(mix.astype(BF16), wmo_ref[...])
        y, xh, rstd = _ln_fwd(r, g_ref[...], b_ref[...])
        xn_ref[...] = y
        xh_ref[...] = xh
        rstd_ref[...] = rstd

    return _call(
        body, "mixout_fwd", (t // tm,),
        [_rows(tm, D_MODEL), _rows(tm, D_SSM), _rows(tm, D_GMLP), _rows(tm, 2 * D_MODEL),
         _resident((D_SSM, D_MODEL)), _resident((D_GMLP, D_MODEL)), _resident((D_MODEL, D_MODEL)),
         _fixed((1, D_MODEL)), _fixed((1, D_MODEL))],
        (_rows(tm, D_MODEL), _rows(tm, D_MODEL), _rows(tm, 1)),
        (SDS((t, D_MODEL), F32), SDS((t, D_MODEL), F32), SDS((t, 1), F32)),
        (x1, s5o, gm, gab, ua, ub, wmo, g, b), sem=("parallel",), bg=bg)


def _mixout_bwd(dx2, xh, rstd, s5o, gm, gab, ua, ub, wmo, g, tm, bg=None):
    t = dx2.shape[0]

    def body(d_ref, xh_ref, rstd_ref, s_ref, m_ref, gab_ref, ua_ref, ub_ref, wmo_ref, g_ref,
             dx1_ref, dmx_ref, mb_ref, dya_ref, dyb_ref, ds5_ref, dgm_ref, dgab_ref, dg_ref, db_ref):
        @pl.when(pl.program_id(0) == 0)
        def _():
            dg_ref[...] = jnp.zeros_like(dg_ref)
            db_ref[...] = jnp.zeros_like(db_ref)

        dy = d_ref[...]
        xhv = xh_ref[...]
        dr = _ln_bwd(dy, xhv, rstd_ref[...], g_ref[...])
        dg_ref[...] += jnp.sum(dy * xhv, axis=0, keepdims=True)
        db_ref[...] += jnp.sum(dy, axis=0, keepdims=True)
        dx1_ref[...] = ALPHA * dr
        drb = dr.astype(BF16)
        dmx_ref[...] = drb
        dm = _dot_nt(drb, wmo_ref[...])
        ya = _dot(s_ref[...], ua_ref[...])
        yb = _dot(m_ref[...], ub_ref[...])
        sa = _sigmoid(gab_ref[:, 0:D_MODEL].astype(F32))
        sb = _sigmoid(gab_ref[:, D_MODEL:2 * D_MODEL].astype(F32))
        mb_ref[...] = (sa * ya + sb * yb).astype(BF16).T
        dya = (dm * sa).astype(BF16)
        dyb = (dm * sb).astype(BF16)
        dya_ref[...] = dya
        dyb_ref[...] = dyb
        dgab_ref[:, 0:D_MODEL] = (dm * ya * sa * (1.0 - sa)).astype(BF16)
        dgab_ref[:, D_MODEL:2 * D_MODEL] = (dm * yb * sb * (1.0 - sb)).astype(BF16)
        ds5_ref[...] = _dot_nt(dya, ua_ref[...])
        dgm_ref[...] = _dot_nt(dyb, ub_ref[...])

    return _call(
        body, "mixout_bwd", (t // tm,),
        [_rows(tm, D_MODEL), _rows(tm, D_MODEL), _rows(tm, 1), _rows(tm, D_SSM), _rows(tm, D_GMLP),
         _rows(tm, 2 * D_MODEL), _resident((D_SSM, D_MODEL)), _resident((D_GMLP, D_MODEL)),
         _resident((D_MODEL, D_MODEL)), _fixed((1, D_MODEL))],
        (_rows(tm, D_MODEL), _rows(tm, D_MODEL), _cols(D_MODEL, tm), _rows(tm, D_MODEL),
         _rows(tm, D_MODEL), _rows(tm, D_SSM), _rows(tm, D_GMLP), _rows(tm, 2 * D_MODEL),
         _fixed((1, D_MODEL)), _fixed((1, D_MODEL))),
        (SDS((t, D_MODEL), F32), SDS((t, D_MODEL), BF16), SDS((D_MODEL, t), BF16),
         SDS((t, D_MODEL), BF16), SDS((t, D_MODEL), BF16), SDS((t, D_SSM), F32),
         SDS((t, D_GMLP), F32), SDS((t, 2 * D_MODEL), BF16),
         SDS((1, D_MODEL), F32), SDS((1, D_MODEL), F32)),
        (dx2, xh, rstd, s5o, gm, gab, ua, ub, wmo, g), sem=("arbitrary",), bg=bg)


def _ple_loss(x3, p, tgt, wpg, wpp, tm, bg=None):
    t = x3.shape[0]

    def body(x_ref, p_ref, t_ref, wpg_ref, wpp_ref, dx_ref, xb_ref, pb_ref, dq_ref, de_ref, loss_ref):
        @pl.when(pl.program_id(0) == 0)
        def _():
            loss_ref[...] = jnp.zeros_like(loss_ref)

        x3v = x_ref[...]
        xb = x3v.astype(BF16)
        pb = p_ref[...].astype(BF16)
        xb_ref[...] = xb.T
        pb_ref[...] = pb.T
        s = _sigmoid(_dot(xb, wpg_ref[...]))
        e = _dot(pb, wpp_ref[...])
        diff = x3v + s * e - t_ref[...]
        loss_ref[...] += jnp.sum(diff * diff, axis=0, keepdims=True)
        dout = diff * (1.0 / D_MODEL)
        de_ref[...] = (dout * s).astype(BF16)
        dq = (dout * e * s * (1.0 - s)).astype(BF16)
        dq_ref[...] = dq
        dx_ref[...] = dout + _dot_nt(dq, wpg_ref[...])

    return _call(
        body, "ple_loss", (t // tm,),
        [_rows(tm, D_MODEL), _rows(tm, PLE_DIM), _rows(tm, D_MODEL),
         _resident((D_MODEL, D_MODEL)), _resident((PLE_DIM, D_MODEL))],
        (_rows(tm, D_MODEL), _cols(D_MODEL, tm), _cols(PLE_DIM, tm), _rows(tm, D_MODEL),
         _rows(tm, D_MODEL), _fixed((1, D_MODEL))),
        (SDS((t, D_MODEL), F32), SDS((D_MODEL, t), BF16), SDS((PLE_DIM, t), BF16),
         SDS((t, D_MODEL), BF16), SDS((t, D_MODEL), BF16), SDS((1, D_MODEL), F32)),
        (x3, p, tgt, wpg, wpp), sem=("arbitrary",), bg=bg)


def _s5_discretise(lre, lim, log_dt, bre, bim):
    dt = jnp.exp(log_dt)[:, None]
    mag = jnp.exp(lre * dt)
    abr = mag * jnp.cos(lim * dt)
    abi = mag * jnp.sin(lim * dt)
    nr = abr - 1.0
    ni = abi
    den = lre * lre + lim * lim
    cr = ((nr * lre + ni * lim) / den)[..., None]
    ci = ((ni * lre - nr * lim) / den)[..., None]
    return abr, abi, cr * bre - ci * bim, cr * bim + ci * bre


def _block_diag_in(bb):
    v = bb.reshape(S5_BLOCKS, 8, SSM_STATE, SSM_GROUP_CH).transpose(0, 1, 3, 2)
    return jnp.einsum("bgip,gh->bgihp", v, jnp.eye(8, dtype=bb.dtype)).reshape(
        S5_BLOCKS, S5_BLOCK_IN, S5_BLOCK_ST)


def _block_diag_in_t(dm):
    v = dm.reshape(S5_BLOCKS, 8, SSM_GROUP_CH, 8, SSM_STATE)
    d = jnp.einsum("bgihp,gh->bgip", v, jnp.eye(8, dtype=dm.dtype))
    return d.transpose(0, 1, 3, 2).reshape(SSM_GROUPS, SSM_STATE, SSM_GROUP_CH)


def _block_diag_out(cc):
    v = cc.reshape(S5_BLOCKS, 8, SSM_GROUP_CH, SSM_STATE)
    return jnp.einsum("bgip,gh->bgphi", v, jnp.eye(8, dtype=cc.dtype)).reshape(
        S5_BLOCKS, S5_BLOCK_ST, S5_BLOCK_IN)


def _block_diag_out_t(dn):
    v = dn.reshape(S5_BLOCKS, 8, SSM_STATE, 8, SSM_GROUP_CH)
    d = jnp.einsum("bgphi,gh->bgip", v, jnp.eye(8, dtype=dn.dtype))
    return d.reshape(SSM_GROUPS, SSM_GROUP_CH, SSM_STATE)


def _s5_setup(lre, lim, log_dt, bre, bim, cre, cim, d_skip, glu_w, glu_b, tb):
    seg = tb // 8
    abr, abi, bbr, bbi = _s5_discretise(lre, lim, log_dt, bre, bim)
    pr, pi = abr, abi
    for _ in range(int(math.log2(seg))):
        pr, pi = pr * pr - pi * pi, 2.0 * pr * pi
    rows = jnp.arange(tb)
    src = (rows % 8) * seg + rows // 8
    perm = (src[:, None] == jnp.arange(tb)[None, :]).astype(BF16)
    mre = _block_diag_in(bbr)
    mim = _block_diag_in(bbi)
    nre = _block_diag_out(cre)
    nim = _block_diag_out(cim)
    return {
        "perm": perm, "permt": perm.T,
        "mre": mre.astype(BF16), "mim": mim.astype(BF16),
        "mtre": mre.transpose(0, 2, 1).astype(BF16), "mtim": mim.transpose(0, 2, 1).astype(BF16),
        "nre": nre.astype(BF16), "nim": nim.astype(BF16),
        "ntre": nre.transpose(0, 2, 1).astype(BF16), "ntim": nim.transpose(0, 2, 1).astype(BF16),
        "a": jnp.stack([abr.reshape(-1), abi.reshape(-1)]),
        "ap": jnp.stack([pr.reshape(-1), pi.reshape(-1)]),
        "dskip": d_skip.reshape(1, D_SSM), "glu_w": glu_w, "glu_wt": glu_w.T,
        "glu_b": glu_b.reshape(1, D_SSM),
    }


BIG = ("ffn1_w_in", "ffn1_w_out", "mix_w_in", "ssm_glu_w", "up_a", "up_b", "mix_w_out",
       "ffn2_w_in", "ffn2_w_out", "ple_w_proj", "ple_w_gate")
BIG_AXIS = {"ffn1_w_in": 1, "ffn1_w_out": 0, "mix_w_in": 1, "ssm_glu_w": 0, "up_a": 1, "up_b": 1,
            "mix_w_out": 0, "ffn2_w_in": 1, "ffn2_w_out": 0, "ple_w_proj": 1, "ple_w_gate": 0}
SHARD_MAJOR = 2
GATHER_AXIS = dict(BIG_AXIS, ffn1_w_in=SHARD_MAJOR, ffn2_w_in=SHARD_MAJOR)
GATHER_ORDER = (("ffn1_w_in",), ("ffn1_w_out",), ("mix_w_in",), ("ssm_glu_w", "up_a", "up_b", "mix_w_out"),
                ("ffn2_w_in",), ("ffn2_w_out", "ple_w_gate", "ple_w_proj"))
GATHER_FIRST_ID = 1
REDUCE_FIRST_ID = 7
SMALL = ("ln1_g", "ln1_b", "ssm_lambda_re", "ssm_lambda_im", "ssm_log_dt", "ssm_b_re", "ssm_b_im",
         "ssm_c_re", "ssm_c_im", "ssm_d", "ssm_glu_b", "gmlp_ln_g", "gmlp_ln_b", "gmlp_w_s",
         "gmlp_b_s", "ln2_g", "ln2_b", "ln3_g", "ln3_b")
SMALL_VIEW = {"ssm_b_re": (SSM_GROUPS, SSM_STATE * SSM_GROUP_CH), "ssm_b_im": (SSM_GROUPS, SSM_STATE * SSM_GROUP_CH)}


def _small_view(k, a):
    return a.reshape(SMALL_VIEW[k]) if k in SMALL_VIEW else a


def _place():
    return lax.axis_index("x"), lax.axis_index("y"), lax.axis_index("c")


def _other_chips(x, y):
    return [(1 - x, y), (x, 1 - y), (1 - x, 1 - y)]


def _window(ref, shard_shape, axis, chip, half):
    r, c = shard_shape
    hr = r // 2
    if axis == SHARD_MAJOR:
        return ref.at[chip] if half is None else ref.at[chip, pl.ds(half * hr, hr), :]
    if axis == 0:
        if half is None:
            return ref.at[pl.ds(chip * r, r), :]
        return ref.at[pl.ds(chip * r + half * hr, hr), :]
    if half is None:
        return ref.at[:, pl.ds(chip * c, c)]
    return ref.at[pl.ds(half * hr, hr), pl.ds(chip * c, c)]


def _gather_weights(shards, axes):
    n = len(shards)
    shapes = [s.shape for s in shards]
    full = [{0: (4 * r, c), 1: (r, 4 * c), SHARD_MAJOR: (4, r, c)}[ax] for (r, c), ax in zip(shapes, axes)]

    def remote(sems, i, k, src, dst, to):
        return pltpu.make_async_remote_copy(src_ref=src, dst_ref=dst, send_sem=sems[0].at[6 * i + k],
                                            recv_sem=sems[1].at[6 * i + k], device_id=to, device_id_type=MESH)

    def own_copies(ins, outs, sems):
        x, y, c = _place()
        me = 2 * x + y
        cps = []
        for i in range(n):
            hr = shapes[i][0] // 2
            mine = ins[i].at[pl.ds(c * hr, hr), :]
            for j, (cx, cy) in enumerate(_other_chips(x, y)):
                cps.append(remote(sems, i, j, mine, _window(outs[i], shapes[i], axes[i], me, c), (cx, cy, c)))
        local = [pltpu.make_async_copy(ins[i], _window(outs[i], shapes[i], axes[i], me, None), sems[2].at[i])
                 for i in range(n)]
        return cps, local

    def start(ins, outs, sems):
        cps, local = own_copies(ins, outs, sems)
        for cp in local + cps:
            cp.start()

    def finish(ins, outs, sems):
        x, y, c = _place()
        sibling = (x, y, 1 - c)
        passed = []
        for j, (cx, cy) in enumerate(_other_chips(x, y)):
            for i in range(n):
                w = _window(outs[i], shapes[i], axes[i], 2 * cx + cy, c)
                remote(sems, i, j, w, w, (cx, cy, c)).wait_recv()
                cp = remote(sems, i, 3 + j, w, w, sibling)
                cp.start()
                passed.append(cp)
        for j, (cx, cy) in enumerate(_other_chips(x, y)):
            for i in range(n):
                w = _window(outs[i], shapes[i], axes[i], 2 * cx + cy, 1 - c)
                remote(sems, i, 3 + j, w, w, sibling).wait_recv()
        cps, local = own_copies(ins, outs, sems)
        for cp in cps + passed:
            cp.wait_send()
        for cp in local:
            cp.wait()

    return _Exchange(shards, [SDS(f, BF16) for f in full],
                     [pltpu.SemaphoreType.DMA((6 * n,)), pltpu.SemaphoreType.DMA((6 * n,)),
                      pltpu.SemaphoreType.DMA((n,))], start, finish)


def _scatter_grads(parts, shapes, axes):
    n = len(parts)

    def copies(ins, outs, sems):
        x, y, c = _place()
        return [pltpu.make_async_remote_copy(
            src_ref=_window(ins[i], shapes[i], axes[i], 2 * cx + cy, None), dst_ref=outs[i].at[j],
            send_sem=sems[0].at[3 * i + j], recv_sem=sems[1].at[3 * i + j],
            device_id=(cx, cy, c), device_id_type=MESH)
            for i in range(n) for j, (cx, cy) in enumerate(_other_chips(x, y))]

    def start(ins, outs, sems):
        for cp in copies(ins, outs, sems):
            cp.start()

    def finish(ins, outs, sems):
        for cp in copies(ins, outs, sems):
            cp.wait()

    return _Exchange(parts, [SDS((3,) + tuple(s), BF16) for s in shapes],
                     [pltpu.SemaphoreType.DMA((3 * n,)), pltpu.SemaphoreType.DMA((3 * n,))], start, finish)


def _swap_halves(parts, shapes, axes):
    n = len(parts)

    def copies(ins, outs, sems):
        x, y, c = _place()
        cps = []
        for i in range(n):
            r, _ = shapes[i]
            hr = r // 2
            if axes[i] == 0:
                cps += [pltpu.make_async_remote_copy(
                    src_ref=ins[i].at[pl.ds(k * r + (1 - c) * hr, hr), :], dst_ref=outs[i].at[k],
                    send_sem=sems[0].at[i], recv_sem=sems[1].at[i], device_id=(x, y, 1 - c),
                    device_id_type=MESH) for k in range(4)]
            else:
                cps.append(pltpu.make_async_remote_copy(
                    src_ref=ins[i].at[pl.ds((1 - c) * hr, hr), :], dst_ref=outs[i],
                    send_sem=sems[0].at[i], recv_sem=sems[1].at[i], device_id=(x, y, 1 - c),
                    device_id_type=MESH))
        return cps

    def start(ins, outs, sems):
        for cp in copies(ins, outs, sems):
            cp.start()

    def finish(ins, outs, sems):
        x, y, c = _place()
        for i in range(n):
            pltpu.make_async_remote_copy(src_ref=outs[i], dst_ref=outs[i], send_sem=sems[0].at[i],
                                         recv_sem=sems[1].at[i], device_id=(x, y, 1 - c),
                                         device_id_type=MESH).wait()

    out = [SDS((4, r // 2, c), BF16) if ax == 0 else SDS((r // 2, 4 * c), BF16)
           for (r, c), ax in zip(shapes, axes)]
    return _Exchange(parts, out, [pltpu.SemaphoreType.DMA((n,)), pltpu.SemaphoreType.DMA((n,))], start, finish)


def _scatter_halves(pres, shapes):
    n = len(pres)

    def copies(ins, outs, sems):
        x, y, c = _place()
        return [pltpu.make_async_remote_copy(
            src_ref=ins[i].at[1 + j], dst_ref=outs[i].at[j], send_sem=sems[0].at[3 * i + j],
            recv_sem=sems[1].at[3 * i + j], device_id=(cx, cy, c), device_id_type=MESH)
            for i in range(n) for j, (cx, cy) in enumerate(_other_chips(x, y))]

    def start(ins, outs, sems):
        for cp in copies(ins, outs, sems):
            cp.start()

    def finish(ins, outs, sems):
        for cp in copies(ins, outs, sems):
            cp.wait()

    return _Exchange(pres, [SDS((3, r // 2, c), BF16) for r, c in shapes],
                     [pltpu.SemaphoreType.DMA((3 * n,)), pltpu.SemaphoreType.DMA((3 * n,))], start, finish)


def _swap_with_sibling(arrs):
    n = len(arrs)

    def copies(ins, outs, sems):
        x, y, c = _place()
        return [pltpu.make_async_remote_copy(src_ref=ins[i], dst_ref=outs[i], send_sem=sems[0].at[i],
                                             recv_sem=sems[1].at[i], device_id=(x, y, 1 - c),
                                             device_id_type=MESH) for i in range(n)]

    def start(ins, outs, sems):
        for cp in copies(ins, outs, sems):
            cp.start()

    def finish(ins, outs, sems):
        for cp in copies(ins, outs, sems):
            cp.wait()

    return _Exchange(arrs, [SDS(a.shape, a.dtype) for a in arrs],
                     [pltpu.SemaphoreType.DMA((n,)), pltpu.SemaphoreType.DMA((n,))], start, finish)


def _gather_small(arrs):
    n = len(arrs)

    def copy(sems, outs, i, k, block, to, src=None):
        px, py, pc = block
        dst = outs[i].at[4 * px + 2 * py + pc]
        return pltpu.make_async_remote_copy(
            src_ref=dst if src is None else src, dst_ref=dst, send_sem=sems[0].at[7 * i + k],
            recv_sem=sems[1].at[7 * i + k], device_id=to, device_id_type=MESH)

    def own_copies(ins, outs, sems):
        x, y, c = _place()
        cps = []
        for i in range(n):
            cps.append(copy(sems, outs, i, 0, (x, y, c), (x, y, 1 - c), src=ins[i]))
            for j, (cx, cy) in enumerate(_other_chips(x, y)):
                cps.append(copy(sems, outs, i, 1 + j, (x, y, c), (cx, cy, c), src=ins[i]))
        local = [pltpu.make_async_copy(ins[i], outs[i].at[4 * x + 2 * y + c], sems[2].at[i]) for i in range(n)]
        return cps, local

    def start(ins, outs, sems):
        cps, local = own_copies(ins, outs, sems)
        for cp in local + cps:
            cp.start()

    def finish(ins, outs, sems):
        x, y, c = _place()
        passed = []
        for j, (cx, cy) in enumerate(_other_chips(x, y)):
            for i in range(n):
                copy(sems, outs, i, 1 + j, (cx, cy, c), (x, y, c)).wait_recv()
                cp = copy(sems, outs, i, 4 + j, (cx, cy, c), (x, y, 1 - c))
                cp.start()
                passed.append(cp)
        for i in range(n):
            copy(sems, outs, i, 0, (x, y, 1 - c), (x, y, c)).wait_recv()
            for j, (cx, cy) in enumerate(_other_chips(x, y)):
                copy(sems, outs, i, 4 + j, (cx, cy, 1 - c), (x, y, c)).wait_recv()
        cps, local = own_copies(ins, outs, sems)
        for cp in cps + passed:
            cp.wait_send()
        for cp in local:
            cp.wait()

    return _Exchange(arrs, [SDS((N_DEV,) + a.shape, F32) for a in arrs],
                     [pltpu.SemaphoreType.DMA((7 * n,)), pltpu.SemaphoreType.DMA((7 * n,)),
                      pltpu.SemaphoreType.DMA((n,))], start, finish)


def _local_step(x, p, tgt, wb, ws, shards=None, opt=None):
    bsz, seq, _ = x.shape
    t = bsz * seq
    tm = min(256, t)
    tb = min(S5_TIME_BLOCK, seq)
    x0 = x.reshape(t, D_MODEL)
    p0 = p.reshape(t, PLE_DIM)
    tg = tgt.reshape(t, D_MODEL)
    row = lambda v: v.reshape(1, -1)
    dist = shards is not None
    wb = dict(wb)
    recv, sums, other, gathered = {}, {}, {}, {}
    gb = {}
    gs = {}
    shape_of, axis_of = {}, {}
    chip = None
    if dist:
        shape_of = {k: tuple(shards[k].shape) for k in BIG}
        axis_of = dict(BIG_AXIS)
        for q in range(LAST_PIECES):
            shape_of[LAST_PIECE % q] = (D_MODEL // LAST_PIECES, shape_of["ffn1_w_in"][1])
            axis_of[LAST_PIECE % q] = 1
        xi, yi, ci = _place()
        chip = (2 * xi + yi).astype(jnp.int32).reshape(1)
        ids = jnp.stack([2 * xi + yi] + [2 * cx + cy for cx, cy in _other_chips(xi, yi)] + [ci]).astype(jnp.int32)
    halfbuf, pre = {}, {}

    def gather(names):
        return _gather_weights([shards[k] for k in names], [GATHER_AXIS[k] for k in names]) if dist else None

    def exchange(scat=(), swap=(), halves=(), scat2=(), swap2=(), extra=None, after=None):
        if not dist:
            return None, []
        after = order[0] if after is None else after
        parts, tags = [], []
        if scat:
            parts.append(_scatter_grads([gb[k][1] for k in scat], [shape_of[k] for k in scat],
                                        [axis_of[k] for k in scat]))
            tags.append((recv, scat))
        if swap:
            for k in swap:
                sums[k] = order[0] = _sum_blocks(gb[k][0], recv[k], shape_of[k], axis_of[k], chip, "sum_" + k,
                                                 order[0])
            parts.append(_swap_with_sibling([sums[k] for k in swap]))
            tags.append((other, swap))
        if halves:
            parts.append(_swap_halves([gb[k][1] for k in halves], [shape_of[k] for k in halves],
                                      [axis_of[k] for k in halves]))
            tags.append((halfbuf, halves))
        if scat2:
            for k in scat2:
                pre[k] = _presum(gb[k][0], halfbuf[k], shape_of[k], axis_of[k], ids, "presum_" + k, order[0])
                order[0] = pre[k][0]
            parts.append(_scatter_halves([pre[k][1] for k in scat2], [shape_of[k] for k in scat2]))
            tags.append((recv, scat2))
        if swap2:
            for k in swap2:
                sums[k] = order[0] = _sum_half(pre[k][0], recv[k], "sum_" + k, order[0])
            parts.append(_swap_with_sibling([sums[k] for k in swap2]))
            tags.append((other, swap2))
        if extra is not None:
            parts.append(extra[0])
            tags.append((extra[1], extra[2]))
        return (_join(parts), tags) if parts else (None, [])

    def take(ex_tags, got):
        ex, tags = ex_tags
        if ex is not None:
            for (dst, names), (o0, o1) in zip(tags, ex.cuts):
                dst.update(zip(names, got[o0:o1]))

    order = [None]

    def ordered(builder, *args, **kw):
        res = builder(*args, bg=order[0] if dist else None, **kw)
        order[0] = res[0][0]
        return res

    launched = []

    def launch(ex_tags):
        if ex_tags[0] is not None:
            n = len(launched)
            launched.append(n)
            take(ex_tags, _run_exchange_on_sequencer(ex_tags[0], "reduce_%d" % n, REDUCE_FIRST_ID + n))

    small_shape = {k: _small_view(k, v).shape for k, v in ws.items()}
    small_shape["loss_rows"] = (1, D_MODEL)
    ws = {k: v if (v.ndim == 2 and k != "ssm_log_dt") else v[0] for k, v in ws.items()}
    tril = jnp.tril(jnp.ones((CHUNK, CHUNK), dtype=bool))
    wsm = jnp.where(tril[None], ws["gmlp_w_s"], 0.0)
    wsm_b = wsm.astype(BF16)
    wsmt_b = wsm.transpose(0, 2, 1).astype(BF16)
    bias = jnp.repeat(ws["gmlp_b_s"].T, GMLP_HEAD_DIM, axis=1)

    tf = min(512, t)
    if dist:
        for gi, names in enumerate(GATHER_ORDER):
            wb.update(zip(names, _run_exchange_on_sequencer(gather(names), "gather_%d" % gi, GATHER_FIRST_ID + gi)))
    (x0b, h1, a1), _ = _ffn_proj(x0, wb["ffn1_w_in"], tf, "ffn1_proj")
    (x1, xh1, rstd1), _ = _ffn_out(x0, a1, wb["ffn1_w_out"], row(ws["ln1_g"]), row(ws["ln1_b"]), tf, "ffn1_out")
    sp = _s5_setup(ws["ssm_lambda_re"], ws["ssm_lambda_im"], ws["ssm_log_dt"], ws["ssm_b_re"],
                   ws["ssm_b_im"], ws["ssm_c_re"], ws["ssm_c_im"], ws["ssm_d"], wb["ssm_glu_w"],
                   ws["ssm_glu_b"], tb)
    (x1b, za, zuv, gab), _ = _mixin_fwd(x1, wb["mix_w_in"], tf)
    (s5o, s5ot, y2p, carries), _ = _s5_fwd(za, sp, bsz, seq, tb)
    (gm, gmt), _ = _gmlp_fwd(zuv, row(ws["gmlp_ln_g"]), row(ws["gmlp_ln_b"]), wsm_b, bias)
    (x2, xh2, rstd2), _ = _mixout_fwd(x1, s5o, gm, gab, wb["up_a"], wb["up_b"], wb["mix_w_out"],
                                           row(ws["ln2_g"]), row(ws["ln2_b"]), tf)
    (x2b, h2, a2), _ = _ffn_proj(x2, wb["ffn2_w_in"], tf, "ffn2_proj")
    (x3, xh3, rstd3), _ = _ffn_out(x2, a2, wb["ffn2_w_out"], row(ws["ln3_g"]), row(ws["ln3_b"]), tf, "ffn2_out")
    (dx3, x3b, pb, dq, de, loss_rows), _ = _ple_loss(x3, p0, tg, wb["ple_w_gate"], wb["ple_w_proj"], tf)
    order[0] = dx3
    gb["ple_w_gate"], _ = ordered(_tn_matmul, x3b, dq, "dw_ple_gate", 1024, 1024, a_t=True)
    gb["ple_w_proj"], _ = ordered(_tn_matmul, pb, de, "dw_ple_proj", 256, 1024, a_t=True)
    launch(exchange(scat=("ple_w_gate", "ple_w_proj")))
    (dx2, dh2, df2, gs["ln3_g"], gs["ln3_b"]), _ = ordered(
        _ffn_bwd, dx3, xh3, rstd3, h2, wb["ffn2_w_in"], wb["ffn2_w_out"], row(ws["ln3_g"]), tm, "ffn2_bwd")
    gb["ffn2_w_out"], _ = ordered(_tn_matmul, a2, df2, "dw_ffn2_out", 1408, 1024)
    launch(exchange(scat=("ffn2_w_out",)))
    gb["ffn2_w_in"], _ = ordered(_tn_matmul, x2b, dh2, "dw_ffn2_in", 1024, 1408, a_t=True)
    launch(exchange(scat=("ffn2_w_in",), swap=("ple_w_gate", "ple_w_proj")))
    (dx1a, dmx, mb, dya, dyb, ds5, dgm, dgab, gs["ln2_g"], gs["ln2_b"]), _ = ordered(
        _mixout_bwd, dx2, xh2, rstd2, s5o, gm, gab, wb["up_a"], wb["up_b"], wb["mix_w_out"], row(ws["ln2_g"]), tf)
    gb["mix_w_out"], _ = ordered(_tn_matmul, mb, dmx, "dw_mix_out", 1024, 1024, a_t=True)
    gb["up_a"], _ = ordered(_tn_matmul, s5ot, dya, "dw_up_a", 512, 1024, a_t=True)
    gb["up_b"], _ = ordered(_tn_matmul, gmt, dyb, "dw_up_b", 512, 1024, a_t=True)
    launch(exchange(scat=("mix_w_out", "up_a", "up_b"), swap=("ffn2_w_out",)))
    (dza, dmr, dmi, dnr, dni, da, ddsk, dgw, dgb), _ = ordered(_s5_bwd, za, y2p, ds5, carries, sp, bsz, seq, tb)
    gb["ssm_glu_w"] = (dgw, dgw.astype(BF16))
    launch(exchange(scat=("ssm_glu_w",), swap=("ffn2_w_in",)))
    (dzuv, dws, dbias, gs["gmlp_ln_g"], gs["gmlp_ln_b"]), _ = ordered(
        _gmlp_bwd, zuv, dgm, row(ws["gmlp_ln_g"]), row(ws["gmlp_ln_b"]), wsm_b, wsmt_b, bias)
    (dx1,), _ = ordered(_mixin_bwd, dx1a, dza, dzuv, dgab, wb["mix_w_in"], tf)
    g_mi, _ = ordered(_tn_matmul, x1b, dza, "dw_mix_in_a", 1024, 512, 0, 3584, a_t=True)
    g_mi, _ = ordered(_tn_matmul, x1b, dzuv, "dw_mix_in_uv", 1024, 512, 1, 3584, g_mi, a_t=True)
    gb["mix_w_in"], _ = ordered(_tn_matmul, x1b, dgab, "dw_mix_in_g", 1024, 512, 3, 3584, g_mi, a_t=True)
    launch(exchange(swap=("mix_w_out", "up_a", "up_b", "ssm_glu_w")))

    d_abr = da[0].sum(axis=0).reshape(SSM_GROUPS, SSM_STATE)
    d_abi = da[1].sum(axis=0).reshape(SSM_GROUPS, SSM_STATE)
    _, vjp = jax.vjp(_s5_discretise, ws["ssm_lambda_re"], ws["ssm_lambda_im"], ws["ssm_log_dt"],
                     ws["ssm_b_re"], ws["ssm_b_im"])
    (gs["ssm_lambda_re"], gs["ssm_lambda_im"], gs["ssm_log_dt"], gs["ssm_b_re"], gs["ssm_b_im"]) = vjp(
        (d_abr, d_abi, _block_diag_in_t(dmr), _block_diag_in_t(dmi)))
    gs["ssm_c_re"] = _block_diag_out_t(dnr)
    gs["ssm_c_im"] = _block_diag_out_t(dni)
    gs["ssm_d"] = ddsk
    gs["ssm_glu_b"] = dgb
    gs["gmlp_w_s"] = dws
    gs["gmlp_b_s"] = dbias.reshape(CHUNK, GMLP_HEADS, GMLP_HEAD_DIM).sum(axis=-1).T
    gs["loss_rows"] = loss_rows

    def small_gather(names):
        return (_gather_small([gs[k].reshape(small_shape[k]) for k in names]), gathered, names) if dist else None

    late = ("ln1_g", "ln1_b")
    launch(exchange(scat=("mix_w_in",), extra=small_gather(tuple(k for k in SMALL + ("loss_rows",) if k not in late))))
    (dx0, dh1, df1, gs["ln1_g"], gs["ln1_b"]), _ = ordered(
        _ffn_bwd, dx1, xh1, rstd1, h1, wb["ffn1_w_in"], wb["ffn1_w_out"], row(ws["ln1_g"]), tm, "ffn1_bwd")
    grad_x = dx0.reshape(bsz, seq, D_MODEL)
    if not dist:
        gb["ffn1_w_out"], _ = _tn_matmul(a1, df1, "dw_ffn1_out", 1408, 1024)
        gb["ffn1_w_in"], _ = _tn_matmul(x0b, dh1, "dw_ffn1_in", 1024, 1408, a_t=True)
        return (loss_rows, grad_x, gb, {k: gs[k].reshape(small_shape[k]) for k in SMALL}, sums, other, gathered,
                None, {})
    launch(exchange(extra=small_gather(late)))
    gb["ffn1_w_out"], _ = ordered(_tn_matmul, a1, df1, "dw_ffn1_out", 1408, 1024)
    last = ["ffn1_w_out"] + [LAST_PIECE % q for q in range(LAST_PIECES)]
    fillers = (("ffn2_w_in", "mix_w_in", "ple_w_gate"),
               ("ffn2_w_out", "mix_w_out", "up_a", "up_b", "ssm_glu_w", "ple_w_proj"))
    out = {}
    for i in range(1, len(last) + 3):
        stage = lambda d: tuple(last[i - d:i - d + 1]) if 0 <= i - d < len(last) else ()
        launch(exchange(halves=stage(1), scat2=stage(2), swap2=stage(3), swap=("mix_w_in",) if i == 2 else ()))
        if i < len(last):
            gb[last[i]], _ = ordered(_tn_matmul, x0b, dh1, "dw_" + last[i], D_MODEL // LAST_PIECES, 1408,
                                     a_cols=(i - 1, 1), a_t=True)
        elif i - len(last) < len(fillers):
            for k in fillers[i - len(last)]:
                w, m, v = opt[k]
                out[k] = _adam_big(w, sums[k], other[k], m, v, "adam_" + k, after=order[0])
                order[0] = out[k][1]
    return loss_rows, grad_x, gb, gs, sums, other, gathered, ids, out


def _adamw(w, g, m, v):
    m = ADAM_B1 * m + (1.0 - ADAM_B1) * g
    v = ADAM_B2 * v + (1.0 - ADAM_B2) * (g * g)
    m_hat = m / ADAM_C1
    v_hat = v / ADAM_C2
    delta = -ADAM_LR * (m_hat / (jnp.sqrt(v_hat) + ADAM_EPS) + ADAM_WD * w)
    return delta, m, v


def _pinned(after):
    return ([pl.BlockSpec(memory_space=pl.ANY)], [after]) if after is not None else ([], [])


def _sum_blocks(part, recv, shape, axis, chip, name, after=None):
    r, c = shape
    rb = r // ROW_STEPS

    def body(chip_ref, p_ref, r_ref, *rest):
        rest[-1][...] = (p_ref[...] + r_ref[0].astype(F32) + r_ref[1].astype(F32) + r_ref[2].astype(F32))

    if axis == 0:
        own = pl.BlockSpec((rb, c), lambda i, k: (k[0] * ROW_STEPS + i, 0))
    else:
        own = pl.BlockSpec((rb, c), lambda i, k: (i, k[0]))
    pin_specs, pin_args = _pinned(after)
    grid_spec = pltpu.PrefetchScalarGridSpec(
        num_scalar_prefetch=1, grid=(ROW_STEPS,),
        in_specs=[own, pl.BlockSpec((3, rb, c), lambda i, k: (0, i, 0))] + pin_specs,
        out_specs=pl.BlockSpec((rb, c), lambda i, k: (i, 0)))
    return pl.pallas_call(body, name=name, out_shape=SDS((r, c), F32), grid_spec=grid_spec,
                          compiler_params=_params(("parallel",)))(chip, part, recv, *pin_args)


def _presum(part, half, shape, axis, ids, name, after=None):
    r, c = shape
    rb = r // 2

    def body(ids_ref, p_ref, h_ref, *rest):
        of_ref, ob_ref = rest[-2:]
        s = p_ref[...] + h_ref[...].astype(F32)
        ob_ref[...] = s.astype(BF16)

        @pl.when(pl.program_id(1) == 0)
        def _():
            of_ref[...] = s

    if axis == 0:
        p_spec = pl.BlockSpec((rb, c), lambda i, t, ids: (ids[t] * 2 + ids[4] + i, 0))
        h_spec = pl.BlockSpec((None, rb, c), lambda i, t, ids: (ids[t], i, 0))
    else:
        p_spec = pl.BlockSpec((rb, c), lambda i, t, ids: (ids[4] + i, ids[t]))
        h_spec = pl.BlockSpec((rb, c), lambda i, t, ids: (i, ids[t]))
    pin_specs, pin_args = _pinned(after)
    grid_spec = pltpu.PrefetchScalarGridSpec(
        num_scalar_prefetch=1, grid=(1, 4), in_specs=[p_spec, h_spec] + pin_specs,
        out_specs=(pl.BlockSpec((rb, c), lambda i, t, ids: (i, 0)),
                   pl.BlockSpec((None, rb, c), lambda i, t, ids: (t, i, 0))))
    return pl.pallas_call(body, name=name, out_shape=(SDS((r // 2, c), F32), SDS((4, r // 2, c), BF16)),
                          grid_spec=grid_spec,
                          compiler_params=_params(("parallel", "arbitrary")))(ids, part, half, *pin_args)


def _sum_half(pre, recv, name, after=None):
    hr, c = pre.shape
    rb = hr

    def body(p_ref, r_ref, *rest):
        rest[-1][...] = (p_ref[...] + r_ref[0].astype(F32) + r_ref[1].astype(F32) + r_ref[2].astype(F32))

    spec = pl.BlockSpec((rb, c), lambda i: (i, 0))
    pin_specs, pin_args = _pinned(after)
    return pl.pallas_call(body, name=name, grid=(1,), out_shape=SDS((hr, c), F32),
                          in_specs=[spec, pl.BlockSpec((3, rb, c), lambda i: (0, i, 0))] + pin_specs,
                          out_specs=spec, compiler_params=_params(("parallel",)))(pre, recv, *pin_args)


def _adam_halves(w, mine, oth, m, v, ids, name, piece=0, prev=None):
    r, c = w.shape
    rb = mine.shape[0] // 2

    def body(ids_ref, w_ref, a_ref, b_ref, m_ref, v_ref, *rest):
        g_ref, d_ref, nm_ref, nv_ref = rest[-4:]
        g = jnp.where(pl.program_id(0) // 2 == ids_ref[4], a_ref[...], b_ref[...])
        g_ref[...] = g
        d_ref[...], nm_ref[...], nv_ref[...] = _adamw(w_ref[...], g, m_ref[...], v_ref[...])

    whole = pl.BlockSpec((rb, c), lambda i, ids: (i + 4 * piece, 0))
    part = pl.BlockSpec((rb, c), lambda i, ids: (i % 2, 0))
    in_specs = [whole, part, part, whole, whole]
    args = [w, mine, oth, m, v]
    aliases = {}
    if prev is not None:
        in_specs += [pl.BlockSpec(memory_space=pl.ANY)] * 4
        args += list(prev)
        aliases = {6: 0, 7: 1, 8: 2, 9: 3}
    grid_spec = pltpu.PrefetchScalarGridSpec(num_scalar_prefetch=1, grid=(4,), in_specs=in_specs,
                                             out_specs=(whole,) * 4)
    return pl.pallas_call(body, name=name, out_shape=tuple(SDS((r, c), F32) for _ in range(4)),
                          grid_spec=grid_spec, input_output_aliases=aliases,
                          compiler_params=_params(("parallel",)))(ids, *args)


def _adam_big(w, ga, gb, m, v, name, piece=0, prev=None, after=None):
    r, c = w.shape
    pr = ga.shape[0]
    steps = ROW_STEPS if pr == r else 2
    rb = pr // steps
    off = piece * steps

    def body(w_ref, ga_ref, gb_ref, m_ref, v_ref, *rest):
        g_ref, d_ref, nm_ref, nv_ref = rest[-4:]
        g = ga_ref[...] + gb_ref[...]
        g_ref[...] = g
        d_ref[...], nm_ref[...], nv_ref[...] = _adamw(w_ref[...], g, m_ref[...], v_ref[...])

    whole = pl.BlockSpec((rb, c), lambda i: (i + off, 0))
    part = pl.BlockSpec((rb, c), lambda i: (i, 0))
    in_specs = [whole, part, part, whole, whole]
    args = [w, ga, gb, m, v]
    aliases = {}
    if prev is not None:
        in_specs += [pl.BlockSpec(memory_space=pl.ANY)] * 4
        args += list(prev)
        aliases = {5: 0, 6: 1, 7: 2, 8: 3}
    if after is not None:
        in_specs.append(pl.BlockSpec(memory_space=pl.ANY))
        args.append(after)
    return pl.pallas_call(
        body, name=name, grid=(steps,), out_shape=tuple(SDS((r, c), F32) for _ in range(4)),
        in_specs=in_specs, out_specs=(whole,) * 4, input_output_aliases=aliases,
        compiler_params=_params(("parallel",)),
    )(*args)


def _adam_small(ws, gathered, ms, vs):
    n = len(ws)

    def body(*refs):
        w_refs, g_refs, m_refs, v_refs = refs[:n], refs[n:2 * n], refs[2 * n:3 * n], refs[3 * n:4 * n]
        outs = refs[4 * n:]
        for i in range(n):
            g = g_refs[i][0]
            for d in range(1, N_DEV):
                g = g + g_refs[i][d]
            delta, nm, nv = _adamw(w_refs[i][...], g, m_refs[i][...], v_refs[i][...])
            outs[i][...] = g
            outs[n + i][...] = delta
            outs[2 * n + i][...] = nm
            outs[3 * n + i][...] = nv

    vmem = pl.BlockSpec(memory_space=pltpu.VMEM)
    shapes = [w.shape for w in ws]
    return pl.pallas_call(
        body, name="adam_small", out_shape=tuple(SDS(s, F32) for s in shapes * 4),
        in_specs=[vmem] * (4 * n), out_specs=tuple([vmem] * (4 * n)),
        compiler_params=pltpu.CompilerParams(vmem_limit_bytes=VMEM_LIMIT_BYTES),
    )(*ws, *gathered, *ms, *vs)


def _sum_loss(gathered):
    def body(g_ref, o_ref):
        tot = g_ref[0]
        for d in range(1, N_DEV):
            tot = tot + g_ref[d]
        o_ref[...] = (0.5 / D_MODEL) * jnp.sum(tot, axis=1, keepdims=True)

    vmem = pl.BlockSpec(memory_space=pltpu.VMEM)
    return pl.pallas_call(body, name="sum_loss", out_shape=SDS((1, 1), F32), in_specs=[vmem],
                          out_specs=vmem)(gathered)


def kernel(x, p, ffn1_w_in, ffn1_w_out, ln1_g, ln1_b, mix_w_in, ssm_lambda_re, ssm_lambda_im, ssm_log_dt, ssm_b_re, ssm_b_im, ssm_c_re, ssm_c_im, ssm_d, ssm_glu_w, ssm_glu_b, gmlp_ln_g, gmlp_ln_b, gmlp_w_s, gmlp_b_s, up_a, up_b, mix_w_out, ln2_g, ln2_b, ffn2_w_in, ffn2_w_out, ln3_g, ln3_b, ple_w_proj, ple_w_gate, loss_target, m_ffn1_w_in, m_ffn1_w_out, m_ln1_g, m_ln1_b, m_mix_w_in, m_ssm_lambda_re, m_ssm_lambda_im, m_ssm_log_dt, m_ssm_b_re, m_ssm_b_im, m_ssm_c_re, m_ssm_c_im, m_ssm_d, m_ssm_glu_w, m_ssm_glu_b, m_gmlp_ln_g, m_gmlp_ln_b, m_gmlp_w_s, m_gmlp_b_s, m_up_a, m_up_b, m_mix_w_out, m_ln2_g, m_ln2_b, m_ffn2_w_in, m_ffn2_w_out, m_ln3_g, m_ln3_b, m_ple_w_proj, m_ple_w_gate, v_ffn1_w_in, v_ffn1_w_out, v_ln1_g, v_ln1_b, v_mix_w_in, v_ssm_lambda_re, v_ssm_lambda_im, v_ssm_log_dt, v_ssm_b_re, v_ssm_b_im, v_ssm_c_re, v_ssm_c_im, v_ssm_d, v_ssm_glu_w, v_ssm_glu_b, v_gmlp_ln_g, v_gmlp_ln_b, v_gmlp_w_s, v_gmlp_b_s, v_up_a, v_up_b, v_mix_w_out, v_ln2_g, v_ln2_b, v_ffn2_w_in, v_ffn2_w_out, v_ln3_g, v_ln3_b, v_ple_w_proj, v_ple_w_gate):
    given = dict(locals())
    order = ("ffn1_w_in", "ffn1_w_out", "ln1_g", "ln1_b", "mix_w_in", "ssm_lambda_re", "ssm_lambda_im",
             "ssm_log_dt", "ssm_b_re", "ssm_b_im", "ssm_c_re", "ssm_c_im", "ssm_d", "ssm_glu_w", "ssm_glu_b",
             "gmlp_ln_g", "gmlp_ln_b", "gmlp_w_s", "gmlp_b_s", "up_a", "up_b", "mix_w_out", "ln2_g", "ln2_b",
             "ffn2_w_in", "ffn2_w_out", "ln3_g", "ln3_b", "ple_w_proj", "ple_w_gate")
    assert set(order) == set(BIG + SMALL)

    shard = {k: given[k][0] for k in BIG}
    shard_b = {k: shard[k].astype(BF16) for k in BIG}
    opt = {k: (shard[k], given["m_" + k][0], given["v_" + k][0]) for k in BIG}
    loss_rows, grad_x, gb, gs, sums, other, gathered, ids, out = _local_step(
        x, given["p"][0], loss_target, {}, {k: given[k] for k in SMALL}, shard_b, opt)

    out = dict(out)
    for k in BIG:
        if k in out:
            continue
        moments = (given["m_" + k][0], given["v_" + k][0])
        if k == "ffn1_w_out":
            out[k] = _adam_halves(shard[k], sums[k], other[k], *moments, ids, "adam_" + k)
        elif k == "ffn1_w_in":
            for q in range(LAST_PIECES):
                kq = LAST_PIECE % q
                out[k] = _adam_halves(shard[k], sums[kq], other[kq], *moments, ids, "adam_" + kq, q, out.get(k))
        else:
            out[k] = _adam_big(shard[k], sums[k], other[k], *moments, "adam_" + k,
                               after=gb[LAST_PIECE % (LAST_PIECES - 1)][0])

    res = _adam_small([_small_view(k, given[k]) for k in SMALL], [gathered[k] for k in SMALL],
                      [_small_view(k, given["m_" + k]) for k in SMALL],
                      [_small_view(k, given["v_" + k]) for k in SMALL])
    ns = len(SMALL)
    for i, k in enumerate(SMALL):
        out[k] = tuple(res[j * ns + i].reshape(given[k].shape) for j in range(4))
    loss = _sum_loss(gathered["loss_rows"]).reshape(())

    lead = lambda k, j: out[k][j][None] if k in BIG else out[k][j]
    return (loss, grad_x, *[lead(k, 0) for k in order], *[lead(k, 1) for k in order],
            *[lead(k, 2) for k in order], *[lead(k, 3) for k in order])
```

```python
import math

import jax
import jax.numpy as jnp
from jax import lax
from jax.experimental import pallas as pl
from jax.experimental.pallas import tpu as pltpu
from jax.experimental.pallas import tpu_sc as plsc

F32 = jnp.float32
BF16 = jnp.bfloat16
MESH = pl.DeviceIdType.MESH
SDS = jax.ShapeDtypeStruct

D_MODEL = 1024
D_FF = 2816
D_SSM = 512
D_GMLP = 512
SSM_GROUPS = 32
SSM_GROUP_CH = 16
SSM_STATE = 64
SSM_LANES = SSM_GROUPS * SSM_STATE
GMLP_HEADS = 8
GMLP_HEAD_DIM = 64
CHUNK = 128
PLE_DIM = 256
LN_EPS = 1e-5
ALPHA = 2.0 ** 0.25

ADAM_LR = 0.001
ADAM_B1 = 0.9
ADAM_B2 = 0.999
ADAM_EPS = 1e-08
ADAM_WD = 0.01
ADAM_STEP = 10
ADAM_C1 = 1.0 - ADAM_B1 ** ADAM_STEP
ADAM_C2 = 1.0 - ADAM_B2 ** ADAM_STEP

N_DEV = 8
VMEM_LIMIT_BYTES = 56 * 1024 * 1024
FFN_COLS = 1408
S5_BLOCKS = 4
S5_BLOCK_IN = D_SSM // S5_BLOCKS
S5_BLOCK_ST = SSM_LANES // S5_BLOCKS
SCAN_LANES = 512
S5_TIME_BLOCK = 512
TN_K_BLOCK = 2048
TN_SMALL_BLOCK = 1024 * 1024
GMLP_CHUNKS = 4
ROW_STEPS = 4
LAST_PIECES = 2
LAST_PIECE = "ffn1_w_in_q%d"
_G0 = math.sqrt(2.0 / math.pi)
_G1 = 0.044715


def _dot(a, b):
    return jnp.dot(a, b, preferred_element_type=F32)


def _dot_nt(a, b):
    return lax.dot_general(a, b, (((1,), (1,)), ((), ())), preferred_element_type=F32)


def _dot_tn(a, b):
    return lax.dot_general(a, b, (((0,), (0,)), ((), ())), preferred_element_type=F32)


def _sigmoid(x):
    return 1.0 / (1.0 + jnp.exp(-x))


def _gelu(x):
    t = jnp.tanh(_G0 * (x + _G1 * x * x * x))
    return 0.5 * x * (1.0 + t)


def _gelu_grad(x):
    t = jnp.tanh(_G0 * (x + _G1 * x * x * x))
    return 0.5 * (1.0 + t) + 0.5 * x * (1.0 - t * t) * _G0 * (1.0 + 3.0 * _G1 * x * x)


def _ln_fwd(r, g, b):
    mu = jnp.mean(r, axis=-1, keepdims=True)
    d = r - mu
    var = jnp.mean(d * d, axis=-1, keepdims=True)
    rstd = lax.rsqrt(var + LN_EPS)
    xh = d * rstd
    return xh * g + b, xh, rstd


def _ln_bwd(dy, xh, rstd, g):
    dxh = dy * g
    m1 = jnp.mean(dxh, axis=-1, keepdims=True)
    m2 = jnp.mean(dxh * xh, axis=-1, keepdims=True)
    return rstd * (dxh - m1 - xh * m2)


def _resident(shape):
    nd = len(shape)
    return pl.BlockSpec(shape, lambda *_: (0,) * nd, pipeline_mode=pl.Buffered(1))


def _fixed(shape):
    nd = len(shape)
    return pl.BlockSpec(shape, lambda *_: (0,) * nd)


def _rows(tm, cols):
    return pl.BlockSpec((tm, cols), lambda i: (i, 0))


def _cols(rows, tm):
    return pl.BlockSpec((rows, tm), lambda i: (0, i))


def _params(sem):
    return pltpu.CompilerParams(dimension_semantics=sem, vmem_limit_bytes=VMEM_LIMIT_BYTES)


class _Exchange:
    def __init__(self, args, out_shape, sems, start, finish):
        self.args, self.out_shape, self.sems = list(args), list(out_shape), list(sems)
        self.start, self.finish = start, finish
        self.cuts = [(0, len(self.out_shape))]


def _call(body, name, grid, in_specs, out_specs, out_shape, args, scratch=(), sem=None, bg=None, aliases=None):
    aliases = {} if aliases is None else aliases
    in_specs, args = list(in_specs), list(args)
    fn = body
    if bg is not None:
        n_args = len(args)

        def fn(*refs):
            body(*refs[:n_args], *refs[n_args + 1:])

        in_specs.append(pl.BlockSpec(memory_space=pl.ANY))
        args.append(bg)
    res = pl.pallas_call(fn, name=name, grid=grid, out_shape=tuple(out_shape), in_specs=in_specs,
                         out_specs=tuple(out_specs), scratch_shapes=list(scratch),
                         input_output_aliases=aliases, compiler_params=_params(sem))(*args)
    return tuple(res), ()


def _run_exchange_on_sequencer(ex, name, collective_id):
    n_i, n_o = len(ex.args), len(ex.out_shape)

    def body(*refs):
        ins, outs, sems = refs[:n_i], refs[n_i:n_i + n_o], refs[n_i + n_o:]
        x, y, c = lax.axis_index("x"), lax.axis_index("y"), lax.axis_index("c")
        barrier = pltpu.get_barrier_semaphore()
        for peer in [(x, y, 1 - c), (1 - x, y, c), (x, 1 - y, c), (1 - x, 1 - y, c)]:
            pl.semaphore_signal(barrier, inc=1, device_id=peer, device_id_type=MESH)
        pl.semaphore_wait(barrier, 4)
        ex.start(ins, outs, sems)
        ex.finish(ins, outs, sems)

    return tuple(pl.kernel(body, out_type=tuple(ex.out_shape),
                           mesh=plsc.ScalarSubcoreMesh(axis_name="sequencer", num_cores=1),
                           scratch_types=list(ex.sems), name=name,
                           compiler_params=pltpu.CompilerParams(collective_id=collective_id))(*ex.args))


def _join(exchanges):
    cuts = []
    a = o = q = 0
    for e in exchanges:
        cuts.append((a, a + len(e.args), o, o + len(e.out_shape), q, q + len(e.sems)))
        a, o, q = cuts[-1][1], cuts[-1][3], cuts[-1][5]

    def start(ins, outs, sems):
        for e, (a0, a1, o0, o1, q0, q1) in zip(exchanges, cuts):
            e.start(ins[a0:a1], outs[o0:o1], sems[q0:q1])

    def finish(ins, outs, sems):
        for e, (a0, a1, o0, o1, q0, q1) in zip(exchanges, cuts):
            e.finish(ins[a0:a1], outs[o0:o1], sems[q0:q1])

    joined = _Exchange(sum((e.args for e in exchanges), []), sum((e.out_shape for e in exchanges), []),
                       sum((e.sems for e in exchanges), []), start, finish)
    joined.cuts = [(c[2], c[3]) for c in cuts]
    return joined


def _ffn_proj(x, w_in, tm, name, bg=None):
    t = x.shape[0]
    nch = D_FF // FFN_COLS

    def body(x_ref, win_ref, xbt_ref, h_ref, a_ref):
        xb = x_ref[...].astype(BF16)
        xbt_ref[...] = xb.T
        for k in range(nch):
            cg = slice(k * FFN_COLS, (k + 1) * FFN_COLS)
            cu = slice(D_FF + k * FFN_COLS, D_FF + (k + 1) * FFN_COLS)
            hg = _dot(xb, win_ref[k])
            hu = _dot(xb, win_ref[nch + k])
            h_ref[:, cg] = hg.astype(BF16)
            h_ref[:, cu] = hu.astype(BF16)
            a_ref[:, cg] = (hg * _sigmoid(hg) * hu).astype(BF16)

    return _call(
        body, name, (t // tm,),
        [_rows(tm, D_MODEL), _resident((2 * nch, D_MODEL, FFN_COLS))],
        (_cols(D_MODEL, tm), _rows(tm, 2 * D_FF), _rows(tm, D_FF)),
        (SDS((D_MODEL, t), BF16), SDS((t, 2 * D_FF), BF16), SDS((t, D_FF), BF16)),
        (x, w_in), sem=("parallel",), bg=bg)


def _ffn_out(x, a, w_out, g, b, tm, name, bg=None):
    t = x.shape[0]

    def body(x_ref, a_ref, wout_ref, g_ref, b_ref, xn_ref, xh_ref, rstd_ref):
        f = _dot(a_ref[...], wout_ref[...])
        y, xh, rstd = _ln_fwd(ALPHA * x_ref[...] + 0.5 * f, g_ref[...], b_ref[...])
        xn_ref[...] = y
        xh_ref[...] = xh
        rstd_ref[...] = rstd

    return _call(
        body, name, (t // tm,),
        [_rows(tm, D_MODEL), _rows(tm, D_FF), _resident((D_FF, D_MODEL)), _fixed((1, D_MODEL)), _fixed((1, D_MODEL))],
        (_rows(tm, D_MODEL), _rows(tm, D_MODEL), _rows(tm, 1)),
        (SDS((t, D_MODEL), F32), SDS((t, D_MODEL), F32), SDS((t, 1), F32)),
        (x, a, w_out, g, b), sem=("parallel",), bg=bg)


def _ffn_out_loss(x, a, w_out, g, b, p, tgt, wpg, wpp, tm):
    t = x.shape[0]

    def body(x_ref, a_ref, wout_ref, g_ref, b_ref, p_ref, t_ref, wpg_ref, wpp_ref,
             xh_ref, rstd_ref, dx_ref, xbt_ref, pbt_ref, dq_ref, de_ref, loss_ref):
        @pl.when(pl.program_id(0) == 0)
        def _():
            loss_ref[...] = jnp.zeros_like(loss_ref)

        f = _dot(a_ref[...], wout_ref[...])
        x3v, xh, rstd = _ln_fwd(ALPHA * x_ref[...] + 0.5 * f, g_ref[...], b_ref[...])
        xh_ref[...] = xh
        rstd_ref[...] = rstd
        xb = x3v.astype(BF16)
        pb = p_ref[...].astype(BF16)
        xbt_ref[...] = xb.T
        pbt_ref[...] = pb.T
        s = _sigmoid(_dot(xb, wpg_ref[...]))
        e = _dot(pb, wpp_ref[...])
        diff = x3v + s * e - t_ref[...]
        loss_ref[...] += jnp.sum(diff * diff, axis=0, keepdims=True)
        dout = diff * (1.0 / D_MODEL)
        de_ref[...] = (dout * s).astype(BF16)
        dq = (dout * e * s * (1.0 - s)).astype(BF16)
        dq_ref[...] = dq
        dx_ref[...] = dout + _dot_nt(dq, wpg_ref[...])

    return _call(
        body, "ffn2_out_loss", (t // tm,),
        [_rows(tm, D_MODEL), _rows(tm, D_FF), _resident((D_FF, D_MODEL)), _fixed((1, D_MODEL)), _fixed((1, D_MODEL)),
         _rows(tm, PLE_DIM), _rows(tm, D_MODEL), _resident((D_MODEL, D_MODEL)), _resident((PLE_DIM, D_MODEL))],
        (_rows(tm, D_MODEL), _rows(tm, 1), _rows(tm, D_MODEL), _cols(D_MODEL, tm), _cols(PLE_DIM, tm),
         _rows(tm, D_MODEL), _rows(tm, D_MODEL), _fixed((1, D_MODEL))),
        (SDS((t, D_MODEL), F32), SDS((t, 1), F32), SDS((t, D_MODEL), F32), SDS((D_MODEL, t), BF16),
         SDS((PLE_DIM, t), BF16), SDS((t, D_MODEL), BF16), SDS((t, D_MODEL), BF16), SDS((1, D_MODEL), F32)),
        (x, a, w_out, g, b, p, tgt, wpg, wpp), sem=("arbitrary",))


def _ffn_bwd(dxn, xh, rstd, h, w_in, w_out, g, tm, name, bg=None):
    t = dxn.shape[0]
    nch = D_FF // FFN_COLS

    def body(dxn_ref, xh_ref, rstd_ref, h_ref, win_ref, wout_ref, g_ref,
             dx_ref, dh_ref, df_ref, dg_ref, db_ref):
        @pl.when(pl.program_id(0) == 0)
        def _():
            dg_ref[...] = jnp.zeros_like(dg_ref)
            db_ref[...] = jnp.zeros_like(db_ref)

        dy = dxn_ref[...]
        xhv = xh_ref[...]
        dr = _ln_bwd(dy, xhv, rstd_ref[...], g_ref[...])
        dg_ref[...] += jnp.sum(dy * xhv, axis=0, keepdims=True)
        db_ref[...] += jnp.sum(dy, axis=0, keepdims=True)
        df = (0.5 * dr).astype(BF16)
        df_ref[...] = df
        dx = ALPHA * dr
        das = [_dot_nt(df, wout_ref[k * FFN_COLS:(k + 1) * FFN_COLS, :]) for k in range(nch)]
        for k in range(nch):
            cg = slice(k * FFN_COLS, (k + 1) * FFN_COLS)
            cu = slice(D_FF + k * FFN_COLS, D_FF + (k + 1) * FFN_COLS)
            hg = h_ref[:, cg].astype(F32)
            hu = h_ref[:, cu].astype(F32)
            sg = _sigmoid(hg)
            silu = hg * sg
            da = das[k]
            dhu = (da * silu).astype(BF16)
            dhg = (da * hu * (sg * (1.0 + hg * (1.0 - sg)))).astype(BF16)
            dh_ref[:, cg] = dhg
            dh_ref[:, cu] = dhu
            dx = dx + _dot_nt(dhg, win_ref[k]) + _dot_nt(dhu, win_ref[nch + k])
        dx_ref[...] = dx

    return _call(
        body, name, (t // tm,),
        [_rows(tm, D_MODEL), _rows(tm, D_MODEL), _rows(tm, 1), _rows(tm, 2 * D_FF),
         _resident((2 * nch, D_MODEL, FFN_COLS)), _resident((D_FF, D_MODEL)), _fixed((1, D_MODEL))],
        (_rows(tm, D_MODEL), _rows(tm, 2 * D_FF), _rows(tm, D_MODEL),
         _fixed((1, D_MODEL)), _fixed((1, D_MODEL))),
        (SDS((t, D_MODEL), F32), SDS((t, 2 * D_FF), BF16), SDS((t, D_MODEL), BF16),
         SDS((1, D_MODEL), F32), SDS((1, D_MODEL), F32)),
        (dxn, xh, rstd, h, w_in, w_out, g), sem=("arbitrary",), bg=bg)


def _tn_matmul(a, b, name, bm, bn, col_block=0, total_cols=None, prev=None, bg=None, a_cols=None, a_t=False):
    t, m = a.shape[::-1] if a_t else a.shape
    a_first = 0
    if a_cols is not None:
        a_first, m = a_cols[0], a_cols[1] * bm
    n = b.shape[1]
    total_cols = n if total_cols is None else total_cols
    whole = bm * bn <= TN_SMALL_BLOCK and (m // bm) * (n // bn) >= 2
    bk = min(2 * TN_K_BLOCK if whole else TN_K_BLOCK, t)
    nk = t // bk
    n_in = 2 if prev is None else 4

    def body(*refs):
        a_ref, b_ref = refs[0], refs[1]
        o_ref, ob_ref = refs[n_in], refs[n_in + 1]
        k = pl.program_id(2)

        @pl.when(k == 0)
        def _():
            o_ref[...] = jnp.zeros_like(o_ref)

        o_ref[...] += _dot(a_ref[...], b_ref[...]) if a_t else _dot_tn(a_ref[...], b_ref[...])

        @pl.when(k == nk - 1)
        def _():
            ob_ref[...] = o_ref[...].astype(BF16)

    a_spec = (pl.BlockSpec((bm, bk), lambda i, j, k: (i + a_first, k)) if a_t
              else pl.BlockSpec((bk, bm), lambda i, j, k: (k, i + a_first)))
    in_specs = [a_spec, pl.BlockSpec((bk, bn), lambda i, j, k: (k, j))]
    args = [a, b]
    aliases = {}
    if prev is not None:
        in_specs += [pl.BlockSpec(memory_space=pl.ANY), pl.BlockSpec(memory_space=pl.ANY)]
        args += list(prev)
        aliases = {2: 0, 3: 1}
        if any(bg is p for p in prev):
            bg = None
    out_spec = pl.BlockSpec((bm, bn), lambda i, j, k: (i, j + col_block))
    return _call(body, name, (m // bm, n // bn, nk), in_specs, (out_spec, out_spec),
                 (SDS((m, total_cols), F32), SDS((m, total_cols), BF16)), args,
                 sem=("parallel", "parallel", "arbitrary"), bg=bg, aliases=aliases)


def _mixin_fwd(x1, w, tm, bg=None):
    t = x1.shape[0]

    def body(x_ref, w_ref, xbt_ref, za_ref, zuv_ref, gab_ref):
        xb = x_ref[...].astype(BF16)
        xbt_ref[...] = xb.T
        za_ref[...] = _dot(xb, w_ref[:, 0:512]).astype(BF16)
        zuv_ref[...] = _dot(xb, w_ref[:, 512:1536]).astype(BF16)
        gab_ref[...] = _dot(xb, w_ref[:, 1536:3584]).astype(BF16)

    return _call(
        body, "mixin_fwd", (t // tm,),
        [_rows(tm, D_MODEL), _resident((D_MODEL, 3584))],
        (_cols(D_MODEL, tm), _rows(tm, 512), _rows(tm, 1024), _rows(tm, 2048)),
        (SDS((D_MODEL, t), BF16), SDS((t, 512), BF16), SDS((t, 1024), BF16), SDS((t, 2048), BF16)),
        (x1, w), sem=("parallel",), bg=bg)


def _mixin_bwd(dx1a, dza, dzuv, dgab, w, tm, bg=None):
    t = dx1a.shape[0]

    def body(d_ref, dza_ref, dzuv_ref, dgab_ref, w_ref, dx_ref):
        dx_ref[...] = (d_ref[...] + _dot_nt(dza_ref[...], w_ref[:, 0:512])
                       + _dot_nt(dzuv_ref[...], w_ref[:, 512:1536])
                       + _dot_nt(dgab_ref[...], w_ref[:, 1536:3584]))

    return _call(
        body, "mixin_bwd", (t // tm,),
        [_rows(tm, D_MODEL), _rows(tm, 512), _rows(tm, 1024), _rows(tm, 2048), _resident((D_MODEL, 3584))],
        (_rows(tm, D_MODEL),), (SDS((t, D_MODEL), F32),),
        (dx1a, dza, dzuv, dgab, w), sem=("parallel",), bg=bg)


def _unrolled(lo, hi, body, carry):
    for j in range(lo, hi):
        carry = body(j, carry)
    return carry


def _scan_fwd(hr_ref, hi_ref, a_ref, ap_ref, carry_ref, seg, cin_ref):
    for lc in range(SSM_LANES // SCAN_LANES):
        ls = slice(lc * SCAN_LANES, (lc + 1) * SCAN_LANES)
        a_r = jnp.broadcast_to(a_ref[0:1, ls], (8, SCAN_LANES))
        a_i = jnp.broadcast_to(a_ref[1:2, ls], (8, SCAN_LANES))

        def step(j, hc, ls=ls, a_r=a_r, a_i=a_i):
            h_r, h_i = hc
            rows = pl.ds(j * 8, 8)
            n_r = a_r * h_r - a_i * h_i + hr_ref[rows, ls]
            n_i = a_r * h_i + a_i * h_r + hi_ref[rows, ls]
            hr_ref[rows, ls] = n_r
            hi_ref[rows, ls] = n_i
            return n_r, n_i

        zero = jnp.zeros((8, SCAN_LANES), F32)
        f_r, f_i = _unrolled(0, seg, step, (zero, zero))
        c_r = carry_ref[0:1, ls]
        c_i = carry_ref[1:2, ls]
        p_r = ap_ref[0:1, ls]
        p_i = ap_ref[1:2, ls]
        rows_r, rows_i = [], []
        for s in range(8):
            rows_r.append(c_r)
            rows_i.append(c_i)
            c_r, c_i = (f_r[s:s + 1] + p_r * c_r - p_i * c_i,
                        f_i[s:s + 1] + p_r * c_i + p_i * c_r)
        carry_ref[0:1, ls] = c_r
        carry_ref[1:2, ls] = c_i
        cin_r = jnp.concatenate(rows_r, axis=0)
        cin_i = jnp.concatenate(rows_i, axis=0)
        if cin_ref is not None:
            cin_ref[0, :, ls] = cin_r
            cin_ref[1, :, ls] = cin_i

        def fix(j, cc, ls=ls, a_r=a_r, a_i=a_i):
            c_r, c_i = cc
            c_r, c_i = a_r * c_r - a_i * c_i, a_r * c_i + a_i * c_r
            rows = pl.ds(j * 8, 8)
            hr_ref[rows, ls] = hr_ref[rows, ls] + c_r
            hi_ref[rows, ls] = hi_ref[rows, ls] + c_i
            return c_r, c_i

        _unrolled(0, seg, fix, (cin_r, cin_i))


def _scan_bwd(gr_ref, gi_ref, hr_ref, hi_ref, cin_ref, a_ref, ap_ref, rcarry_ref, da_ref, seg):
    for lc in range(SSM_LANES // SCAN_LANES):
        ls = slice(lc * SCAN_LANES, (lc + 1) * SCAN_LANES)
        a_r = jnp.broadcast_to(a_ref[0:1, ls], (8, SCAN_LANES))
        a_i = jnp.broadcast_to(a_ref[1:2, ls], (8, SCAN_LANES))

        def step(t, gc, ls=ls, a_r=a_r, a_i=a_i):
            g_r, g_i = gc
            rows = pl.ds((seg - 1 - t) * 8, 8)
            n_r = gr_ref[rows, ls] + a_r * g_r + a_i * g_i
            n_i = gi_ref[rows, ls] + a_r * g_i - a_i * g_r
            gr_ref[rows, ls] = n_r
            gi_ref[rows, ls] = n_i
            return n_r, n_i

        zero = jnp.zeros((8, SCAN_LANES), F32)
        f_r, f_i = _unrolled(0, seg, step, (zero, zero))
        c_r = rcarry_ref[0:1, ls]
        c_i = rcarry_ref[1:2, ls]
        p_r = ap_ref[0:1, ls]
        p_i = ap_ref[1:2, ls]
        rows_r, rows_i = [None] * 8, [None] * 8
        for s in range(7, -1, -1):
            rows_r[s] = c_r
            rows_i[s] = c_i
            c_r, c_i = (f_r[s:s + 1] + p_r * c_r + p_i * c_i,
                        f_i[s:s + 1] + p_r * c_i - p_i * c_r)
        rcarry_ref[0:1, ls] = c_r
        rcarry_ref[1:2, ls] = c_i
        cin_r = jnp.concatenate(rows_r, axis=0)
        cin_i = jnp.concatenate(rows_i, axis=0)

        def fix_row(j_rows, hp_r, hp_i, cc, ls=ls, a_r=a_r, a_i=a_i):
            c_r, c_i, acc_r, acc_i = cc
            c_r, c_i = a_r * c_r + a_i * c_i, a_r * c_i - a_i * c_r
            g_r = gr_ref[j_rows, ls] + c_r
            g_i = gi_ref[j_rows, ls] + c_i
            gr_ref[j_rows, ls] = g_r
            gi_ref[j_rows, ls] = g_i
            acc_r = acc_r + g_r * hp_r + g_i * hp_i
            acc_i = acc_i + g_i * hp_r - g_r * hp_i
            return c_r, c_i, acc_r, acc_i

        def fix(t, cc, ls=ls, fix_row=fix_row):
            j = seg - 1 - t
            rows = pl.ds(j * 8, 8)
            prev = pl.ds((j - 1) * 8, 8)
            return fix_row(rows, hr_ref[prev, ls], hi_ref[prev, ls], cc)

        cc = _unrolled(0, seg - 1, fix, (cin_r, cin_i, zero, zero))
        _, _, acc_r, acc_i = fix_row(pl.ds(0, 8), cin_ref[0, :, ls], cin_ref[1, :, ls], cc)
        da_ref[0, :, ls] += acc_r
        da_ref[1, :, ls] += acc_i


def _s5_fwd(za, sp, bsz, seq, tb, bg=None):
    nb = seq // tb
    seg = tb // 8
    t = bsz * seq

    def body(za_ref, perm_ref, permt_ref, mre_ref, mim_ref, nre_ref, nim_ref, a_ref, ap_ref,
             dsk_ref, gw_ref, gb_ref, out_ref, outt_ref, y2_ref, car_ref, hr_ref, hi_ref, carry_ref):
        @pl.when(pl.program_id(1) == 0)
        def _():
            carry_ref[...] = jnp.zeros_like(carry_ref)

        car_ref[0] = carry_ref[...]
        up = _dot(perm_ref[...], za_ref[...])
        upb = up.astype(BF16)
        for bb in range(S5_BLOCKS):
            ub = upb[:, bb * S5_BLOCK_IN:(bb + 1) * S5_BLOCK_IN]
            st = slice(bb * S5_BLOCK_ST, (bb + 1) * S5_BLOCK_ST)
            hr_ref[:, st] = _dot(ub, mre_ref[bb])
            hi_ref[:, st] = _dot(ub, mim_ref[bb])
        _scan_fwd(hr_ref, hi_ref, a_ref, ap_ref, carry_ref, seg, None)
        ys = []
        for bb in range(S5_BLOCKS):
            st = slice(bb * S5_BLOCK_ST, (bb + 1) * S5_BLOCK_ST)
            ys.append(_dot(hr_ref[:, st].astype(BF16), nre_ref[bb])
                      - _dot(hi_ref[:, st].astype(BF16), nim_ref[bb]))
        y2 = jnp.concatenate(ys, axis=1) + dsk_ref[...] * up
        y2_ref[...] = y2
        y3 = _gelu(y2)
        gl = _dot(y3.astype(BF16), gw_ref[...]) + gb_ref[...]
        oa = y3 * _sigmoid(gl)
        out = _dot(permt_ref[...], oa.astype(BF16)).astype(BF16)
        out_ref[...] = out
        outt_ref[...] = out.T

    blk = pl.BlockSpec((tb, D_SSM), lambda b, j: (b * nb + j, 0))
    blk_t = pl.BlockSpec((D_SSM, tb), lambda b, j: (0, b * nb + j))
    m_shape = (S5_BLOCKS, S5_BLOCK_IN, S5_BLOCK_ST)
    n_shape = (S5_BLOCKS, S5_BLOCK_ST, S5_BLOCK_IN)
    return _call(
        body, "s5_fwd", (bsz, nb),
        [blk, _fixed((tb, tb)), _fixed((tb, tb)), _fixed(m_shape), _fixed(m_shape), _fixed(n_shape),
         _fixed(n_shape), _fixed((2, SSM_LANES)), _fixed((2, SSM_LANES)), _fixed((1, D_SSM)),
         _fixed((D_SSM, D_SSM)), _fixed((1, D_SSM))],
        (blk, blk_t, blk, pl.BlockSpec((1, 2, SSM_LANES), lambda b, j: (b * nb + j, 0, 0))),
        (SDS((t, D_SSM), BF16), SDS((D_SSM, t), BF16), SDS((t, D_SSM), F32), SDS((bsz * nb, 2, SSM_LANES), F32)),
        (za, sp["perm"], sp["permt"], sp["mre"], sp["mim"], sp["nre"], sp["nim"], sp["a"], sp["ap"],
         sp["dskip"], sp["glu_w"], sp["glu_b"]),
        scratch=[pltpu.VMEM((tb, SSM_LANES), F32), pltpu.VMEM((tb, SSM_LANES), F32),
                 pltpu.VMEM((2, SSM_LANES), F32)],
        sem=("arbitrary", "arbitrary"), bg=bg)


def _s5_bwd(za, y2p, doa, carries, sp, bsz, seq, tb, bg=None):
    nb = seq // tb
    seg = tb // 8
    t = bsz * seq

    def body(za_ref, y2_ref, doa_ref, car_ref, perm_ref, permt_ref, mre_ref, mim_ref, mtre_ref, mtim_ref,
             nre_ref, nim_ref, ntre_ref, ntim_ref, a_ref, ap_ref, dsk_ref, gw_ref, gwt_ref, gb_ref,
             dza_ref, dmr_ref, dmi_ref, dnr_ref, dni_ref, da_ref, ddsk_ref, dgw_ref, dgb_ref,
             hr_ref, hi_ref, gr_ref, gi_ref, cin_ref, carry_ref, rcarry_ref):
        first = jnp.logical_and(pl.program_id(0) == 0, pl.program_id(1) == 0)

        @pl.when(first)
        def _():
            for r in (dmr_ref, dmi_ref, dnr_ref, dni_ref, da_ref, ddsk_ref, dgw_ref, dgb_ref):
                r[...] = jnp.zeros_like(r)

        @pl.when(pl.program_id(1) == 0)
        def _():
            rcarry_ref[...] = jnp.zeros_like(rcarry_ref)

        carry_ref[...] = car_ref[0]
        perm = perm_ref[...]
        up = _dot(perm, za_ref[...])
        upb = up.astype(BF16)
        for bb in range(S5_BLOCKS):
            ub = upb[:, bb * S5_BLOCK_IN:(bb + 1) * S5_BLOCK_IN]
            st = slice(bb * S5_BLOCK_ST, (bb + 1) * S5_BLOCK_ST)
            hr_ref[:, st] = _dot(ub, mre_ref[bb])
            hi_ref[:, st] = _dot(ub, mim_ref[bb])
        _scan_fwd(hr_ref, hi_ref, a_ref, ap_ref, carry_ref, seg, cin_ref)

        y2 = y2_ref[...]
        y3 = _gelu(y2)
        y3b = y3.astype(BF16)
        sg = _sigmoid(_dot(y3b, gw_ref[...]) + gb_ref[...])
        d0 = doa_ref[...]
        d_hi = d0.astype(BF16)
        d1 = d0 - d_hi.astype(F32)
        d_mid = d1.astype(BF16)
        d_lo = (d1 - d_mid.astype(F32)).astype(BF16)
        doap = _dot(perm, d_hi) + _dot(perm, d_mid) + _dot(perm, d_lo)
        dgl = doap * y3 * sg * (1.0 - sg)
        dglb = dgl.astype(BF16)
        dy3 = doap * sg + _dot(dglb, gwt_ref[...])
        dgw_ref[...] += _dot_tn(y3b, dglb)
        dgb_ref[...] += jnp.sum(dgl, axis=0, keepdims=True)
        dy2 = dy3 * _gelu_grad(y2)
        ddsk_ref[...] += jnp.sum(dy2 * up, axis=0, keepdims=True)
        dyb = dy2.astype(BF16)
        for bb in range(S5_BLOCKS):
            dyc = dyb[:, bb * S5_BLOCK_IN:(bb + 1) * S5_BLOCK_IN]
            st = slice(bb * S5_BLOCK_ST, (bb + 1) * S5_BLOCK_ST)
            gr_ref[:, st] = _dot(dyc, ntre_ref[bb])
            gi_ref[:, st] = -_dot(dyc, ntim_ref[bb])
            dnr_ref[bb] += _dot_tn(hr_ref[:, st].astype(BF16), dyc)
            dni_ref[bb] += -_dot_tn(hi_ref[:, st].astype(BF16), dyc)
        _scan_bwd(gr_ref, gi_ref, hr_ref, hi_ref, cin_ref, a_ref, ap_ref, rcarry_ref, da_ref, seg)
        dus = []
        for bb in range(S5_BLOCKS):
            st = slice(bb * S5_BLOCK_ST, (bb + 1) * S5_BLOCK_ST)
            grb = gr_ref[:, st].astype(BF16)
            gib = gi_ref[:, st].astype(BF16)
            dus.append(_dot(grb, mtre_ref[bb]) + _dot(gib, mtim_ref[bb]))
            ub = upb[:, bb * S5_BLOCK_IN:(bb + 1) * S5_BLOCK_IN]
            dmr_ref[bb] += _dot_tn(ub, grb)
            dmi_ref[bb] += _dot_tn(ub, gib)
        du = jnp.concatenate(dus, axis=1) + dy2 * dsk_ref[...]
        dza_ref[...] = _dot(permt_ref[...], du.astype(BF16)).astype(BF16)

    def rev(b, j):
        return (b * nb + (nb - 1 - j), 0)

    blk = pl.BlockSpec((tb, D_SSM), rev)
    m_shape = (S5_BLOCKS, S5_BLOCK_IN, S5_BLOCK_ST)
    n_shape = (S5_BLOCKS, S5_BLOCK_ST, S5_BLOCK_IN)
    return _call(
        body, "s5_bwd", (bsz, nb),
        [blk, blk, blk, pl.BlockSpec((1, 2, SSM_LANES), lambda b, j: (b * nb + (nb - 1 - j), 0, 0)),
         _fixed((tb, tb)), _fixed((tb, tb)), _fixed(m_shape), _fixed(m_shape), _fixed(n_shape), _fixed(n_shape),
         _fixed(n_shape), _fixed(n_shape), _fixed(m_shape), _fixed(m_shape),
         _fixed((2, SSM_LANES)), _fixed((2, SSM_LANES)), _fixed((1, D_SSM)),
         _fixed((D_SSM, D_SSM)), _fixed((D_SSM, D_SSM)), _fixed((1, D_SSM))],
        (blk, _fixed(m_shape), _fixed(m_shape), _fixed(n_shape), _fixed(n_shape),
         _fixed((2, 8, SSM_LANES)), _fixed((1, D_SSM)), _fixed((D_SSM, D_SSM)), _fixed((1, D_SSM))),
        (SDS((t, D_SSM), BF16), SDS(m_shape, F32), SDS(m_shape, F32), SDS(n_shape, F32), SDS(n_shape, F32),
         SDS((2, 8, SSM_LANES), F32), SDS((1, D_SSM), F32), SDS((D_SSM, D_SSM), F32), SDS((1, D_SSM), F32)),
        (za, y2p, doa, carries, sp["perm"], sp["permt"], sp["mre"], sp["mim"], sp["mtre"], sp["mtim"],
         sp["nre"], sp["nim"], sp["ntre"], sp["ntim"], sp["a"], sp["ap"], sp["dskip"], sp["glu_w"],
         sp["glu_wt"], sp["glu_b"]),
        scratch=[pltpu.VMEM((tb, SSM_LANES), F32), pltpu.VMEM((tb, SSM_LANES), F32),
                 pltpu.VMEM((tb, SSM_LANES), F32), pltpu.VMEM((tb, SSM_LANES), F32),
                 pltpu.VMEM((2, 8, SSM_LANES), F32), pltpu.VMEM((2, SSM_LANES), F32),
                 pltpu.VMEM((2, SSM_LANES), F32)],
        sem=("arbitrary", "arbitrary"), bg=bg)


def _gmlp_spatial(ws_ref, vb):
    lane = lax.broadcasted_iota(jnp.int32, (CHUNK, 128), 1)
    parts = []
    for j in range(GMLP_HEADS // 2):
        vp = vb[:, 128 * j:128 * (j + 1)]
        parts.append(jnp.where(lane < GMLP_HEAD_DIM, _dot(ws_ref[2 * j], vp), _dot(ws_ref[2 * j + 1], vp)))
    return jnp.concatenate(parts, axis=1)


def _gmlp_fwd(zuv, ln_g, ln_b, wsm, bias, bg=None):
    t = zuv.shape[0]

    def body(z_ref, g_ref, b_ref, ws_ref, bias_ref, out_ref, outt_ref):
        for ch in range(GMLP_CHUNKS):
            rows = slice(ch * CHUNK, (ch + 1) * CHUNK)
            u = _gelu(z_ref[rows, 0:D_GMLP].astype(F32))
            v0 = _gelu(z_ref[rows, D_GMLP:2 * D_GMLP].astype(F32))
            v, _, _ = _ln_fwd(v0, g_ref[...], b_ref[...])
            s = _gmlp_spatial(ws_ref, v.astype(BF16)) + bias_ref[...]
            out = (u * s).astype(BF16)
            out_ref[rows, :] = out
            outt_ref[:, rows] = out.T

    step = GMLP_CHUNKS * CHUNK
    return _call(
        body, "gmlp_fwd", (t // step,),
        [_rows(step, 2 * D_GMLP), _fixed((1, D_GMLP)), _fixed((1, D_GMLP)),
         _fixed((GMLP_HEADS, CHUNK, CHUNK)), _fixed((CHUNK, D_GMLP))],
        (_rows(step, D_GMLP), _cols(D_GMLP, step)), (SDS((t, D_GMLP), BF16), SDS((D_GMLP, t), BF16)),
        (zuv, ln_g, ln_b, wsm, bias), sem=("parallel",), bg=bg)


def _gmlp_bwd(zuv, dgm, ln_g, ln_b, wsm, wsmt, bias, bg=None):
    t = zuv.shape[0]

    def body(z_ref, d_ref, g_ref, b_ref, ws_ref, wst_ref, bias_ref,
             dz_ref, dws_ref, dbias_ref, dg_ref, db_ref):
        @pl.when(pl.program_id(0) == 0)
        def _():
            for r in (dws_ref, dbias_ref, dg_ref, db_ref):
                r[...] = jnp.zeros_like(r)

        gam = g_ref[...]
        lane = lax.broadcasted_iota(jnp.int32, (CHUNK, 128), 1)
        tril = (lax.broadcasted_iota(jnp.int32, (CHUNK, CHUNK), 0)
                >= lax.broadcasted_iota(jnp.int32, (CHUNK, CHUNK), 1))
        zero_b = jnp.zeros((CHUNK, 128), BF16)
        for ch in range(GMLP_CHUNKS):
            rows = slice(ch * CHUNK, (ch + 1) * CHUNK)
            zu = z_ref[rows, 0:D_GMLP].astype(F32)
            zv = z_ref[rows, D_GMLP:2 * D_GMLP].astype(F32)
            u = _gelu(zu)
            v0 = _gelu(zv)
            v, vhat, rstd = _ln_fwd(v0, gam, b_ref[...])
            vb = v.astype(BF16)
            s = _gmlp_spatial(ws_ref, vb) + bias_ref[...]
            d = d_ref[rows, :]
            dz_ref[rows, 0:D_GMLP] = (d * s * _gelu_grad(zu)).astype(BF16)
            ds = d * u
            dbias_ref[...] += ds
            dsb = ds.astype(BF16)
            parts = []
            for j in range(GMLP_HEADS // 2):
                dsp = dsb[:, 128 * j:128 * (j + 1)]
                vp = vb[:, 128 * j:128 * (j + 1)]
                parts.append(jnp.where(lane < GMLP_HEAD_DIM, _dot(wst_ref[2 * j], dsp),
                                       _dot(wst_ref[2 * j + 1], dsp)))
                lo = jnp.where(lane < GMLP_HEAD_DIM, dsp, zero_b)
                hi = jnp.where(lane < GMLP_HEAD_DIM, zero_b, dsp)
                dws_ref[2 * j] += jnp.where(tril, _dot_nt(lo, vp), 0.0)
                dws_ref[2 * j + 1] += jnp.where(tril, _dot_nt(hi, vp), 0.0)
            dv = jnp.concatenate(parts, axis=1)
            dg_ref[...] += jnp.sum(dv * vhat, axis=0, keepdims=True)
            db_ref[...] += jnp.sum(dv, axis=0, keepdims=True)
            dz_ref[rows, D_GMLP:2 * D_GMLP] = (_ln_bwd(dv, vhat, rstd, gam) * _gelu_grad(zv)).astype(BF16)

    step = GMLP_CHUNKS * CHUNK
    return _call(
        body, "gmlp_bwd", (t // step,),
        [_rows(step, 2 * D_GMLP), _rows(step, D_GMLP), _fixed((1, D_GMLP)), _fixed((1, D_GMLP)),
         _fixed((GMLP_HEADS, CHUNK, CHUNK)), _fixed((GMLP_HEADS, CHUNK, CHUNK)), _fixed((CHUNK, D_GMLP))],
        (_rows(step, 2 * D_GMLP), _fixed((GMLP_HEADS, CHUNK, CHUNK)), _fixed((CHUNK, D_GMLP)),
         _fixed((1, D_GMLP)), _fixed((1, D_GMLP))),
        (SDS((t, 2 * D_GMLP), BF16), SDS((GMLP_HEADS, CHUNK, CHUNK), F32), SDS((CHUNK, D_GMLP), F32),
         SDS((1, D_GMLP), F32), SDS((1, D_GMLP), F32)),
        (zuv, dgm, ln_g, ln_b, wsm, wsmt, bias), sem=("arbitrary",), bg=bg)


def _mixout_fwd(x1, s5o, gm, gab, ua, ub, wmo, g, b, tm, bg=None):
    t = x1.shape[0]

    def body(x_ref, s_ref, m_ref, gab_ref, ua_ref, ub_ref, wmo_ref, g_ref, b_ref,
             xn_ref, xh_ref, rstd_ref):
        ya = _dot(s_ref[...], ua_ref[...])
        yb = _dot(m_ref[...], ub_ref[...])
        mix = (_sigmoid(gab_ref[:, 0:D_MODEL].astype(F32)) * ya
               + _sigmoid(gab_ref[:, D_MODEL:2 * D_MODEL].astype(F32)) * yb)
        r = ALPHA * x_ref[...] + _dot(mix.astype(BF16), wmo_ref[...])
        y, xh, rstd = _ln_fwd(r, g_ref[...], b_ref[...])
        xn_ref[...] = y
        xh_ref[...] = xh
        rstd_ref[...] = rstd

    return _call(
        body, "mixout_fwd", (t // tm,),
        [_rows(tm, D_MODEL), _rows(tm, D_SSM), _rows(tm, D_GMLP), _rows(tm, 2 * D_MODEL),
         _resident((D_SSM, D_MODEL)), _resident((D_GMLP, D_MODEL)), _resident((D_MODEL, D_MODEL)),
         _fixed((1, D_MODEL)), _fixed((1, D_MODEL))],
        (_rows(tm, D_MODEL), _rows(tm, D_MODEL), _rows(tm, 1)),
        (SDS((t, D_MODEL), F32), SDS((t, D_MODEL), F32), SDS((t, 1), F32)),
        (x1, s5o, gm, gab, ua, ub, wmo, g, b), sem=("parallel",), bg=bg)


def _mixout_bwd(dx2, xh, rstd, s5o, gm, gab, ua, ub, wmo, g, tm, bg=None):
    t = dx2.shape[0]

    def body(d_ref, xh_ref, rstd_ref, s_ref, m_ref, gab_ref, ua_ref, ub_ref, wmo_ref, g_ref,
             dx1_ref, dmx_ref, mb_ref, dya_ref, dyb_ref, ds5_ref, dgm_ref, dgab_ref, dg_ref, db_ref):
        @pl.when(pl.program_id(0) == 0)
        def _():
            dg_ref[...] = jnp.zeros_like(dg_ref)
            db_ref[...] = jnp.zeros_like(db_ref)

        dy = d_ref[...]
        xhv = xh_ref[...]
        dr = _ln_bwd(dy, xhv, rstd_ref[...], g_ref[...])
        dg_ref[...] += jnp.sum(dy * xhv, axis=0, keepdims=True)
        db_ref[...] += jnp.sum(dy, axis=0, keepdims=True)
        dx1_ref[...] = ALPHA * dr
        drb = dr.astype(BF16)
        dmx_ref[...] = drb
        dm = _dot_nt(drb, wmo_ref[...])
        ya = _dot(s_ref[...], ua_ref[...])
        yb = _dot(m_ref[...], ub_ref[...])
        sa = _sigmoid(gab_ref[:, 0:D_MODEL].astype(F32))
        sb = _sigmoid(gab_ref[:, D_MODEL:2 * D_MODEL].astype(F32))
        mb_ref[...] = (sa * ya + sb * yb).astype(BF16).T
        dya = (dm * sa).astype(BF16)
        dyb = (dm * sb).astype(BF16)
        dya_ref[...] = dya
        dyb_ref[...] = dyb
        dgab_ref[:, 0:D_MODEL] = (dm * ya * sa * (1.0 - sa)).astype(BF16)
        dgab_ref[:, D_MODEL:2 * D_MODEL] = (dm * yb * sb * (1.0 - sb)).astype(BF16)
        ds5_ref[...] = _dot_nt(dya, ua_ref[...])
        dgm_ref[...] = _dot_nt(dyb, ub_ref[...])

    return _call(
        body, "mixout_bwd", (t // tm,),
        [_rows(tm, D_MODEL), _rows(tm, D_MODEL), _rows(tm, 1), _rows(tm, D_SSM), _rows(tm, D_GMLP),
         _rows(tm, 2 * D_MODEL), _resident((D_SSM, D_MODEL)), _resident((D_GMLP, D_MODEL)),
         _resident((D_MODEL, D_MODEL)), _fixed((1, D_MODEL))],
        (_rows(tm, D_MODEL), _rows(tm, D_MODEL), _cols(D_MODEL, tm), _rows(tm, D_MODEL),
         _rows(tm, D_MODEL), _rows(tm, D_SSM), _rows(tm, D_GMLP), _rows(tm, 2 * D_MODEL),
         _fixed((1, D_MODEL)), _fixed((1, D_MODEL))),
        (SDS((t, D_MODEL), F32), SDS((t, D_MODEL), BF16), SDS((D_MODEL, t), BF16),
         SDS((t, D_MODEL), BF16), SDS((t, D_MODEL), BF16), SDS((t, D_SSM), F32),
         SDS((t, D_GMLP), F32), SDS((t, 2 * D_MODEL), BF16),
         SDS((1, D_MODEL), F32), SDS((1, D_MODEL), F32)),
        (dx2, xh, rstd, s5o, gm, gab, ua, ub, wmo, g), sem=("arbitrary",), bg=bg)


def _s5_discretise(lre, lim, log_dt, bre, bim):
    dt = jnp.exp(log_dt)[:, None]
    mag = jnp.exp(lre * dt)
    abr = mag * jnp.cos(lim * dt)
    abi = mag * jnp.sin(lim * dt)
    nr = abr - 1.0
    ni = abi
    den = lre * lre + lim * lim
    cr = ((nr * lre + ni * lim) / den)[..., None]
    ci = ((ni * lre - nr * lim) / den)[..., None]
    return abr, abi, cr * bre - ci * bim, cr * bim + ci * bre


def _block_diag_in(bb):
    v = bb.reshape(S5_BLOCKS, 8, SSM_STATE, SSM_GROUP_CH).transpose(0, 1, 3, 2)
    return jnp.einsum("bgip,gh->bgihp", v, jnp.eye(8, dtype=bb.dtype)).reshape(
        S5_BLOCKS, S5_BLOCK_IN, S5_BLOCK_ST)


def _block_diag_in_t(dm):
    v = dm.reshape(S5_BLOCKS, 8, SSM_GROUP_CH, 8, SSM_STATE)
    d = jnp.einsum("bgihp,gh->bgip", v, jnp.eye(8, dtype=dm.dtype))
    return d.transpose(0, 1, 3, 2).reshape(SSM_GROUPS, SSM_STATE, SSM_GROUP_CH)


def _block_diag_out(cc):
    v = cc.reshape(S5_BLOCKS, 8, SSM_GROUP_CH, SSM_STATE)
    return jnp.einsum("bgip,gh->bgphi", v, jnp.eye(8, dtype=cc.dtype)).reshape(
        S5_BLOCKS, S5_BLOCK_ST, S5_BLOCK_IN)


def _block_diag_out_t(dn):
    v = dn.reshape(S5_BLOCKS, 8, SSM_STATE, 8, SSM_GROUP_CH)
    d = jnp.einsum("bgphi,gh->bgip", v, jnp.eye(8, dtype=dn.dtype))
    return d.reshape(SSM_GROUPS, SSM_GROUP_CH, SSM_STATE)


def _s5_setup(lre, lim, log_dt, bre, bim, cre, cim, d_skip, glu_w, glu_b, tb):
    seg = tb // 8
    abr, abi, bbr, bbi = _s5_discretise(lre, lim, log_dt, bre, bim)
    pr, pi = abr, abi
    for _ in range(int(math.log2(seg))):
        pr, pi = pr * pr - pi * pi, 2.0 * pr * pi
    rows = jnp.arange(tb)
    src = (rows % 8) * seg + rows // 8
    perm = (src[:, None] == jnp.arange(tb)[None, :]).astype(BF16)
    mre = _block_diag_in(bbr)
    mim = _block_diag_in(bbi)
    nre = _block_diag_out(cre)
    nim = _block_diag_out(cim)
    return {
        "perm": perm, "permt": perm.T,
        "mre": mre.astype(BF16), "mim": mim.astype(BF16),
        "mtre": mre.transpose(0, 2, 1).astype(BF16), "mtim": mim.transpose(0, 2, 1).astype(BF16),
        "nre": nre.astype(BF16), "nim": nim.astype(BF16),
        "ntre": nre.transpose(0, 2, 1).astype(BF16), "ntim": nim.transpose(0, 2, 1).astype(BF16),
        "a": jnp.stack([abr.reshape(-1), abi.reshape(-1)]),
        "ap": jnp.stack([pr.reshape(-1), pi.reshape(-1)]),
        "dskip": d_skip.reshape(1, D_SSM), "glu_w": glu_w, "glu_wt": glu_w.T,
        "glu_b": glu_b.reshape(1, D_SSM),
    }


BIG = ("ffn1_w_in", "ffn1_w_out", "mix_w_in", "ssm_glu_w", "up_a", "up_b", "mix_w_out",
       "ffn2_w_in", "ffn2_w_out", "ple_w_proj", "ple_w_gate")
BIG_AXIS = {"ffn1_w_in": 1, "ffn1_w_out": 0, "mix_w_in": 1, "ssm_glu_w": 0, "up_a": 1, "up_b": 1,
            "mix_w_out": 0, "ffn2_w_in": 1, "ffn2_w_out": 0, "ple_w_proj": 1, "ple_w_gate": 0}
SHARD_MAJOR = 2
GATHER_AXIS = dict(BIG_AXIS, ffn1_w_in=SHARD_MAJOR, ffn2_w_in=SHARD_MAJOR)
GATHER_ORDER = (("ffn1_w_in",), ("ffn1_w_out",), ("mix_w_in",), ("ssm_glu_w", "up_a", "up_b", "mix_w_out"),
                ("ffn2_w_in",), ("ffn2_w_out", "ple_w_gate", "ple_w_proj"))
GATHER_FIRST_ID = 1
REDUCE_FIRST_ID = 7
SMALL = ("ln1_g", "ln1_b", "ssm_lambda_re", "ssm_lambda_im", "ssm_log_dt", "ssm_b_re", "ssm_b_im",
         "ssm_c_re", "ssm_c_im", "ssm_d", "ssm_glu_b", "gmlp_ln_g", "gmlp_ln_b", "gmlp_w_s",
         "gmlp_b_s", "ln2_g", "ln2_b", "ln3_g", "ln3_b")
SMALL_VIEW = {"ssm_b_re": (SSM_GROUPS, SSM_STATE * SSM_GROUP_CH), "ssm_b_im": (SSM_GROUPS, SSM_STATE * SSM_GROUP_CH)}


def _small_view(k, a):
    return a.reshape(SMALL_VIEW[k]) if k in SMALL_VIEW else a


def _place():
    return lax.axis_index("x"), lax.axis_index("y"), lax.axis_index("c")


def _other_chips(x, y):
    return [(1 - x, y), (x, 1 - y), (1 - x, 1 - y)]


def _window(ref, shard_shape, axis, chip, half):
    r, c = shard_shape
    hr = r // 2
    if axis == SHARD_MAJOR:
        return ref.at[chip] if half is None else ref.at[chip, pl.ds(half * hr, hr), :]
    if axis == 0:
        if half is None:
            return ref.at[pl.ds(chip * r, r), :]
        return ref.at[pl.ds(chip * r + half * hr, hr), :]
    if half is None:
        return ref.at[:, pl.ds(chip * c, c)]
    return ref.at[pl.ds(half * hr, hr), pl.ds(chip * c, c)]


def _gather_weights(shards, axes):
    n = len(shards)
    shapes = [s.shape for s in shards]
    full = [{0: (4 * r, c), 1: (r, 4 * c), SHARD_MAJOR: (4, r, c)}[ax] for (r, c), ax in zip(shapes, axes)]

    def remote(sems, i, k, src, dst, to):
        return pltpu.make_async_remote_copy(src_ref=src, dst_ref=dst, send_sem=sems[0].at[6 * i + k],
                                            recv_sem=sems[1].at[6 * i + k], device_id=to, device_id_type=MESH)

    def own_copies(ins, outs, sems):
        x, y, c = _place()
        me = 2 * x + y
        cps = []
        for i in range(n):
            hr = shapes[i][0] // 2
            mine = ins[i].at[pl.ds(c * hr, hr), :]
            for j, (cx, cy) in enumerate(_other_chips(x, y)):
                cps.append(remote(sems, i, j, mine, _window(outs[i], shapes[i], axes[i], me, c), (cx, cy, c)))
        local = [pltpu.make_async_copy(ins[i], _window(outs[i], shapes[i], axes[i], me, None), sems[2].at[i])
                 for i in range(n)]
        return cps, local

    def start(ins, outs, sems):
        cps, local = own_copies(ins, outs, sems)
        for cp in local + cps:
            cp.start()

    def finish(ins, outs, sems):
        x, y, c = _place()
        sibling = (x, y, 1 - c)
        passed = []
        for j, (cx, cy) in enumerate(_other_chips(x, y)):
            for i in range(n):
                w = _window(outs[i], shapes[i], axes[i], 2 * cx + cy, c)
                remote(sems, i, j, w, w, (cx, cy, c)).wait_recv()
                cp = remote(sems, i, 3 + j, w, w, sibling)
                cp.start()
                passed.append(cp)
        for j, (cx, cy) in enumerate(_other_chips(x, y)):
            for i in range(n):
                w = _window(outs[i], shapes[i], axes[i], 2 * cx + cy, 1 - c)
                remote(sems, i, 3 + j, w, w, sibling).wait_recv()
        cps, local = own_copies(ins, outs, sems)
        for cp in cps + passed:
            cp.wait_send()
        for cp in local:
            cp.wait()

    return _Exchange(shards, [SDS(f, BF16) for f in full],
                     [pltpu.SemaphoreType.DMA((6 * n,)), pltpu.SemaphoreType.DMA((6 * n,)),
                      pltpu.SemaphoreType.DMA((n,))], start, finish)


def _scatter_grads(parts, shapes, axes):
    n = len(parts)

    def copies(ins, outs, sems):
        x, y, c = _place()
        return [pltpu.make_async_remote_copy(
            src_ref=_window(ins[i], shapes[i], axes[i], 2 * cx + cy, None), dst_ref=outs[i].at[j],
            send_sem=sems[0].at[3 * i + j], recv_sem=sems[1].at[3 * i + j],
            device_id=(cx, cy, c), device_id_type=MESH)
            for i in range(n) for j, (cx, cy) in enumerate(_other_chips(x, y))]

    def start(ins, outs, sems):
        for cp in copies(ins, outs, sems):
            cp.start()

    def finish(ins, outs, sems):
        for cp in copies(ins, outs, sems):
            cp.wait()

    return _Exchange(parts, [SDS((3,) + tuple(s), BF16) for s in shapes],
                     [pltpu.SemaphoreType.DMA((3 * n,)), pltpu.SemaphoreType.DMA((3 * n,))], start, finish)


def _swap_halves(parts, shapes, axes):
    n = len(parts)

    def copies(ins, outs, sems):
        x, y, c = _place()
        cps = []
        for i in range(n):
            r, _ = shapes[i]
            hr = r // 2
            if axes[i] == 0:
                cps += [pltpu.make_async_remote_copy(
                    src_ref=ins[i].at[pl.ds(k * r + (1 - c) * hr, hr), :], dst_ref=outs[i].at[k],
                    send_sem=sems[0].at[i], recv_sem=sems[1].at[i], device_id=(x, y, 1 - c),
                    device_id_type=MESH) for k in range(4)]
            else:
                cps.append(pltpu.make_async_remote_copy(
                    src_ref=ins[i].at[pl.ds((1 - c) * hr, hr), :], dst_ref=outs[i],
                    send_sem=sems[0].at[i], recv_sem=sems[1].at[i], device_id=(x, y, 1 - c),
                    device_id_type=MESH))
        return cps

    def start(ins, outs, sems):
        for cp in copies(ins, outs, sems):
            cp.start()

    def finish(ins, outs, sems):
        x, y, c = _place()
        for i in range(n):
            pltpu.make_async_remote_copy(src_ref=outs[i], dst_ref=outs[i], send_sem=sems[0].at[i],
                                         recv_sem=sems[1].at[i], device_id=(x, y, 1 - c),
                                         device_id_type=MESH).wait()

    out = [SDS((4, r // 2, c), BF16) if ax == 0 else SDS((r // 2, 4 * c), BF16)
           for (r, c), ax in zip(shapes, axes)]
    return _Exchange(parts, out, [pltpu.SemaphoreType.DMA((n,)), pltpu.SemaphoreType.DMA((n,))], start, finish)


def _scatter_halves(pres, shapes):
    n = len(pres)

    def copies(ins, outs, sems):
        x, y, c = _place()
        return [pltpu.make_async_remote_copy(
            src_ref=ins[i].at[1 + j], dst_ref=outs[i].at[j], send_sem=sems[0].at[3 * i + j],
            recv_sem=sems[1].at[3 * i + j], device_id=(cx, cy, c), device_id_type=MESH)
            for i in range(n) for j, (cx, cy) in enumerate(_other_chips(x, y))]

    def start(ins, outs, sems):
        for cp in copies(ins, outs, sems):
            cp.start()

    def finish(ins, outs, sems):
        for cp in copies(ins, outs, sems):
            cp.wait()

    return _Exchange(pres, [SDS((3, r // 2, c), BF16) for r, c in shapes],
                     [pltpu.SemaphoreType.DMA((3 * n,)), pltpu.SemaphoreType.DMA((3 * n,))], start, finish)


def _swap_with_sibling(arrs):
    n = len(arrs)

    def copies(ins, outs, sems):
        x, y, c = _place()
        return [pltpu.make_async_remote_copy(src_ref=ins[i], dst_ref=outs[i], send_sem=sems[0].at[i],
                                             recv_sem=sems[1].at[i], device_id=(x, y, 1 - c),
                                             device_id_type=MESH) for i in range(n)]

    def start(ins, outs, sems):
        for cp in copies(ins, outs, sems):
            cp.start()

    def finish(ins, outs, sems):
        for cp in copies(ins, outs, sems):
            cp.wait()

    return _Exchange(arrs, [SDS(a.shape, a.dtype) for a in arrs],
                     [pltpu.SemaphoreType.DMA((n,)), pltpu.SemaphoreType.DMA((n,))], start, finish)


def _gather_small(arrs):
    n = len(arrs)

    def copy(sems, outs, i, k, block, to, src=None):
        px, py, pc = block
        dst = outs[i].at[4 * px + 2 * py + pc]
        return pltpu.make_async_remote_copy(
            src_ref=dst if src is None else src, dst_ref=dst, send_sem=sems[0].at[7 * i + k],
            recv_sem=sems[1].at[7 * i + k], device_id=to, device_id_type=MESH)

    def own_copies(ins, outs, sems):
        x, y, c = _place()
        cps = []
        for i in range(n):
            cps.append(copy(sems, outs, i, 0, (x, y, c), (x, y, 1 - c), src=ins[i]))
            for j, (cx, cy) in enumerate(_other_chips(x, y)):
                cps.append(copy(sems, outs, i, 1 + j, (x, y, c), (cx, cy, c), src=ins[i]))
        local = [pltpu.make_async_copy(ins[i], outs[i].at[4 * x + 2 * y + c], sems[2].at[i]) for i in range(n)]
        return cps, local

    def start(ins, outs, sems):
        cps, local = own_copies(ins, outs, sems)
        for cp in local + cps:
            cp.start()

    def finish(ins, outs, sems):
        x, y, c = _place()
        passed = []
        for j, (cx, cy) in enumerate(_other_chips(x, y)):
            for i in range(n):
                copy(sems, outs, i, 1 + j, (cx, cy, c), (x, y, c)).wait_recv()
                cp = copy(sems, outs, i, 4 + j, (cx, cy, c), (x, y, 1 - c))
                cp.start()
                passed.append(cp)
        for i in range(n):
            copy(sems, outs, i, 0, (x, y, 1 - c), (x, y, c)).wait_recv()
            for j, (cx, cy) in enumerate(_other_chips(x, y)):
                copy(sems, outs, i, 4 + j, (cx, cy, 1 - c), (x, y, c)).wait_recv()
        cps, local = own_copies(ins, outs, sems)
        for cp in cps + passed:
            cp.wait_send()
        for cp in local:
            cp.wait()

    return _Exchange(arrs, [SDS((N_DEV,) + a.shape, F32) for a in arrs],
                     [pltpu.SemaphoreType.DMA((7 * n,)), pltpu.SemaphoreType.DMA((7 * n,)),
                      pltpu.SemaphoreType.DMA((n,))], start, finish)


def _local_step(x, p, tgt, wb, ws, shards=None, opt=None):
    bsz, seq, _ = x.shape
    t = bsz * seq
    tm = min(256, t)
    tb = min(S5_TIME_BLOCK, seq)
    x0 = x.reshape(t, D_MODEL)
    p0 = p.reshape(t, PLE_DIM)
    tg = tgt.reshape(t, D_MODEL)
    row = lambda v: v.reshape(1, -1)
    dist = shards is not None
    wb = dict(wb)
    recv, sums, other, gathered = {}, {}, {}, {}
    gb = {}
    gs = {}
    shape_of, axis_of = {}, {}
    chip = None
    if dist:
        shape_of = {k: tuple(shards[k].shape) for k in BIG}
        axis_of = dict(BIG_AXIS)
        for q in range(LAST_PIECES):
            shape_of[LAST_PIECE % q] = (D_MODEL // LAST_PIECES, shape_of["ffn1_w_in"][1])
            axis_of[LAST_PIECE % q] = 1
        xi, yi, ci = _place()
        chip = (2 * xi + yi).astype(jnp.int32).reshape(1)
        ids = jnp.stack([2 * xi + yi] + [2 * cx + cy for cx, cy in _other_chips(xi, yi)] + [ci]).astype(jnp.int32)
    halfbuf, pre = {}, {}

    def gather(names):
        return _gather_weights([shards[k] for k in names], [GATHER_AXIS[k] for k in names]) if dist else None

    def exchange(scat=(), swap=(), halves=(), scat2=(), swap2=(), extra=None, after=None):
        if not dist:
            return None, []
        after = order[0] if after is None else after
        parts, tags = [], []
        if scat:
            parts.append(_scatter_grads([gb[k][1] for k in scat], [shape_of[k] for k in scat],
                                        [axis_of[k] for k in scat]))
            tags.append((recv, scat))
        if swap:
            for k in swap:
                sums[k] = order[0] = _sum_blocks(gb[k][0], recv[k], shape_of[k], axis_of[k], chip, "sum_" + k,
                                                 order[0])
            parts.append(_swap_with_sibling([sums[k] for k in swap]))
            tags.append((other, swap))
        if halves:
            parts.append(_swap_halves([gb[k][1] for k in halves], [shape_of[k] for k in halves],
                                      [axis_of[k] for k in halves]))
            tags.append((halfbuf, halves))
        if scat2:
            for k in scat2:
                pre[k] = _presum(gb[k][0], halfbuf[k], shape_of[k], axis_of[k], ids, "presum_" + k, order[0])
                order[0] = pre[k][0]
            parts.append(_scatter_halves([pre[k][1] for k in scat2], [shape_of[k] for k in scat2]))
            tags.append((recv, scat2))
        if swap2:
            for k in swap2:
                sums[k] = order[0] = _sum_half(pre[k][0], recv[k], "sum_" + k, order[0])
            parts.append(_swap_with_sibling([sums[k] for k in swap2]))
            tags.append((other, swap2))
        if extra is not None:
            parts.append(extra[0])
            tags.append((extra[1], extra[2]))
        return (_join(parts), tags) if parts else (None, [])

    def take(ex_tags, got):
        ex, tags = ex_tags
        if ex is not None:
            for (dst, names), (o0, o1) in zip(tags, ex.cuts):
                dst.update(zip(names, got[o0:o1]))

    order = [None]

    def ordered(builder, *args, **kw):
        res = builder(*args, bg=order[0] if dist else None, **kw)
        order[0] = res[0][0]
        return res

    launched = []

    def launch(ex_tags):
        if ex_tags[0] is not None:
            n = len(launched)
            launched.append(n)
            take(ex_tags, _run_exchange_on_sequencer(ex_tags[0], "reduce_%d" % n, REDUCE_FIRST_ID + n))

    small_shape = {k: _small_view(k, v).shape for k, v in ws.items()}
    small_shape["loss_rows"] = (1, D_MODEL)
    ws = {k: v if (v.ndim == 2 and k != "ssm_log_dt") else v[0] for k, v in ws.items()}
    tril = jnp.tril(jnp.ones((CHUNK, CHUNK), dtype=bool))
    wsm = jnp.where(tril[None], ws["gmlp_w_s"], 0.0)
    wsm_b = wsm.astype(BF16)
    wsmt_b = wsm.transpose(0, 2, 1).astype(BF16)
    bias = jnp.repeat(ws["gmlp_b_s"].T, GMLP_HEAD_DIM, axis=1)

    tf = min(512, t)
    if dist:
        for gi, names in enumerate(GATHER_ORDER):
            wb.update(zip(names, _run_exchange_on_sequencer(gather(names), "gather_%d" % gi, GATHER_FIRST_ID + gi)))
    (x0b, h1, a1), _ = _ffn_proj(x0, wb["ffn1_w_in"], tf, "ffn1_proj")
    (x1, xh1, rstd1), _ = _ffn_out(x0, a1, wb["ffn1_w_out"], row(ws["ln1_g"]), row(ws["ln1_b"]), tf, "ffn1_out")
    sp = _s5_setup(ws["ssm_lambda_re"], ws["ssm_lambda_im"], ws["ssm_log_dt"], ws["ssm_b_re"],
                   ws["ssm_b_im"], ws["ssm_c_re"], ws["ssm_c_im"], ws["ssm_d"], wb["ssm_glu_w"],
                   ws["ssm_glu_b"], tb)
    (x1b, za, zuv, gab), _ = _mixin_fwd(x1, wb["mix_w_in"], tf)
    (s5o, s5ot, y2p, carries), _ = _s5_fwd(za, sp, bsz, seq, tb)
    (gm, gmt), _ = _gmlp_fwd(zuv, row(ws["gmlp_ln_g"]), row(ws["gmlp_ln_b"]), wsm_b, bias)
    (x2, xh2, rstd2), _ = _mixout_fwd(x1, s5o, gm, gab, wb["up_a"], wb["up_b"], wb["mix_w_out"],
                                           row(ws["ln2_g"]), row(ws["ln2_b"]), tf)
    (x2b, h2, a2), _ = _ffn_proj(x2, wb["ffn2_w_in"], tf, "ffn2_proj")
    (xh3, rstd3, dx3, x3b, pb, dq, de, loss_rows), _ = _ffn_out_loss(
        x2, a2, wb["ffn2_w_out"], row(ws["ln3_g"]), row(ws["ln3_b"]), p0, tg, wb["ple_w_gate"], wb["ple_w_proj"], tf)
    order[0] = dx3
    gb["ple_w_gate"], _ = ordered(_tn_matmul, x3b, dq, "dw_ple_gate", 1024, 1024, a_t=True)
    gb["ple_w_proj"], _ = ordered(_tn_matmul, pb, de, "dw_ple_proj", 256, 1024, a_t=True)
    launch(exchange(scat=("ple_w_gate", "ple_w_proj")))
    (dx2, dh2, df2, gs["ln3_g"], gs["ln3_b"]), _ = ordered(
        _ffn_bwd, dx3, xh3, rstd3, h2, wb["ffn2_w_in"], wb["ffn2_w_out"], row(ws["ln3_g"]), tm, "ffn2_bwd")
    gb["ffn2_w_out"], _ = ordered(_tn_matmul, a2, df2, "dw_ffn2_out", 1408, 1024)
    launch(exchange(scat=("ffn2_w_out",)))
    gb["ffn2_w_in"], _ = ordered(_tn_matmul, x2b, dh2, "dw_ffn2_in", 1024, 1408, a_t=True)
    launch(exchange(scat=("ffn2_w_in",), swap=("ple_w_gate", "ple_w_proj")))
    (dx1a, dmx, mb, dya, dyb, ds5, dgm, dgab, gs["ln2_g"], gs["ln2_b"]), _ = ordered(
        _mixout_bwd, dx2, xh2, rstd2, s5o, gm, gab, wb["up_a"], wb["up_b"], wb["mix_w_out"], row(ws["ln2_g"]), tf)
    gb["mix_w_out"], _ = ordered(_tn_matmul, mb, dmx, "dw_mix_out", 1024, 1024, a_t=True)
    gb["up_a"], _ = ordered(_tn_matmul, s5ot, dya, "dw_up_a", 512, 1024, a_t=True)
    gb["up_b"], _ = ordered(_tn_matmul, gmt, dyb, "dw_up_b", 512, 1024, a_t=True)
    launch(exchange(scat=("mix_w_out", "up_a", "up_b"), swap=("ffn2_w_out",)))
    (dza, dmr, dmi, dnr, dni, da, ddsk, dgw, dgb), _ = ordered(_s5_bwd, za, y2p, ds5, carries, sp, bsz, seq, tb)
    gb["ssm_glu_w"] = (dgw, dgw.astype(BF16))
    launch(exchange(scat=("ssm_glu_w",), swap=("ffn2_w_in",)))
    (dzuv, dws, dbias, gs["gmlp_ln_g"], gs["gmlp_ln_b"]), _ = ordered(
        _gmlp_bwd, zuv, dgm, row(ws["gmlp_ln_g"]), row(ws["gmlp_ln_b"]), wsm_b, wsmt_b, bias)
    (dx1,), _ = ordered(_mixin_bwd, dx1a, dza, dzuv, dgab, wb["mix_w_in"], tf)
    g_mi, _ = ordered(_tn_matmul, x1b, dza, "dw_mix_in_a", 1024, 512, 0, 3584, a_t=True)
    g_mi, _ = ordered(_tn_matmul, x1b, dzuv, "dw_mix_in_uv", 1024, 512, 1, 3584, g_mi, a_t=True)
    gb["mix_w_in"], _ = ordered(_tn_matmul, x1b, dgab, "dw_mix_in_g", 1024, 512, 3, 3584, g_mi, a_t=True)
    launch(exchange(swap=("mix_w_out", "up_a", "up_b", "ssm_glu_w")))

    d_abr = da[0].sum(axis=0).reshape(SSM_GROUPS, SSM_STATE)
    d_abi = da[1].sum(axis=0).reshape(SSM_GROUPS, SSM_STATE)
    _, vjp = jax.vjp(_s5_discretise, ws["ssm_lambda_re"], ws["ssm_lambda_im"], ws["ssm_log_dt"],
                     ws["ssm_b_re"], ws["ssm_b_im"])
    (gs["ssm_lambda_re"], gs["ssm_lambda_im"], gs["ssm_log_dt"], gs["ssm_b_re"], gs["ssm_b_im"]) = vjp(
        (d_abr, d_abi, _block_diag_in_t(dmr), _block_diag_in_t(dmi)))
    gs["ssm_c_re"] = _block_diag_out_t(dnr)
    gs["ssm_c_im"] = _block_diag_out_t(dni)
    gs["ssm_d"] = ddsk
    gs["ssm_glu_b"] = dgb
    gs["gmlp_w_s"] = dws
    gs["gmlp_b_s"] = dbias.reshape(CHUNK, GMLP_HEADS, GMLP_HEAD_DIM).sum(axis=-1).T
    gs["loss_rows"] = loss_rows

    def small_gather(names):
        return (_gather_small([gs[k].reshape(small_shape[k]) for k in names]), gathered, names) if dist else None

    late = ("ln1_g", "ln1_b")
    launch(exchange(scat=("mix_w_in",), extra=small_gather(tuple(k for k in SMALL + ("loss_rows",) if k not in late))))
    (dx0, dh1, df1, gs["ln1_g"], gs["ln1_b"]), _ = ordered(
        _ffn_bwd, dx1, xh1, rstd1, h1, wb["ffn1_w_in"], wb["ffn1_w_out"], row(ws["ln1_g"]), tm, "ffn1_bwd")
    grad_x = dx0.reshape(bsz, seq, D_MODEL)
    if not dist:
        gb["ffn1_w_out"], _ = _tn_matmul(a1, df1, "dw_ffn1_out", 1408, 1024)
        gb["ffn1_w_in"], _ = _tn_matmul(x0b, dh1, "dw_ffn1_in", 1024, 1408, a_t=True)
        return (loss_rows, grad_x, gb, {k: gs[k].reshape(small_shape[k]) for k in SMALL}, sums, other, gathered,
                None, {})
    launch(exchange(extra=small_gather(late)))
    gb["ffn1_w_out"], _ = ordered(_tn_matmul, a1, df1, "dw_ffn1_out", 1408, 1024)
    last = ["ffn1_w_out"] + [LAST_PIECE % q for q in range(LAST_PIECES)]
    fillers = (("ffn2_w_in", "mix_w_in", "ple_w_gate"),
               ("ffn2_w_out", "mix_w_out", "up_a", "up_b", "ssm_glu_w", "ple_w_proj"))
    out = {}
    for i in range(1, len(last) + 3):
        stage = lambda d: tuple(last[i - d:i - d + 1]) if 0 <= i - d < len(last) else ()
        launch(exchange(halves=stage(1), scat2=stage(2), swap2=stage(3), swap=("mix_w_in",) if i == 2 else ()))
        if i < len(last):
            gb[last[i]], _ = ordered(_tn_matmul, x0b, dh1, "dw_" + last[i], D_MODEL // LAST_PIECES, 1408,
                                     a_cols=(i - 1, 1), a_t=True)
        elif i - len(last) < len(fillers):
            for k in fillers[i - len(last)]:
                w, m, v = opt[k]
                out[k] = _adam_big(w, sums[k], other[k], m, v, "adam_" + k, after=order[0])
                order[0] = out[k][1]
    return loss_rows, grad_x, gb, gs, sums, other, gathered, ids, out


def _adamw(w, g, m, v):
    m = ADAM_B1 * m + (1.0 - ADAM_B1) * g
    v = ADAM_B2 * v + (1.0 - ADAM_B2) * (g * g)
    m_hat = m / ADAM_C1
    v_hat = v / ADAM_C2
    delta = -ADAM_LR * (m_hat / (jnp.sqrt(v_hat) + ADAM_EPS) + ADAM_WD * w)
    return delta, m, v


def _pinned(after):
    return ([pl.BlockSpec(memory_space=pl.ANY)], [after]) if after is not None else ([], [])


def _sum_blocks(part, recv, shape, axis, chip, name, after=None):
    r, c = shape
    rb = r // ROW_STEPS

    def body(chip_ref, p_ref, r_ref, *rest):
        rest[-1][...] = (p_ref[...] + r_ref[0].astype(F32) + r_ref[1].astype(F32) + r_ref[2].astype(F32))

    if axis == 0:
        own = pl.BlockSpec((rb, c), lambda i, k: (k[0] * ROW_STEPS + i, 0))
    else:
        own = pl.BlockSpec((rb, c), lambda i, k: (i, k[0]))
    pin_specs, pin_args = _pinned(after)
    grid_spec = pltpu.PrefetchScalarGridSpec(
        num_scalar_prefetch=1, grid=(ROW_STEPS,),
        in_specs=[own, pl.BlockSpec((3, rb, c), lambda i, k: (0, i, 0))] + pin_specs,
        out_specs=pl.BlockSpec((rb, c), lambda i, k: (i, 0)))
    return pl.pallas_call(body, name=name, out_shape=SDS((r, c), F32), grid_spec=grid_spec,
                          compiler_params=_params(("parallel",)))(chip, part, recv, *pin_args)


def _presum(part, half, shape, axis, ids, name, after=None):
    r, c = shape
    rb = r // 2

    def body(ids_ref, p_ref, h_ref, *rest):
        of_ref, ob_ref = rest[-2:]
        s = p_ref[...] + h_ref[...].astype(F32)
        ob_ref[...] = s.astype(BF16)

        @pl.when(pl.program_id(1) == 0)
        def _():
            of_ref[...] = s

    if axis == 0:
        p_spec = pl.BlockSpec((rb, c), lambda i, t, ids: (ids[t] * 2 + ids[4] + i, 0))
        h_spec = pl.BlockSpec((None, rb, c), lambda i, t, ids: (ids[t], i, 0))
    else:
        p_spec = pl.BlockSpec((rb, c), lambda i, t, ids: (ids[4] + i, ids[t]))
        h_spec = pl.BlockSpec((rb, c), lambda i, t, ids: (i, ids[t]))
    pin_specs, pin_args = _pinned(after)
    grid_spec = pltpu.PrefetchScalarGridSpec(
        num_scalar_prefetch=1, grid=(1, 4), in_specs=[p_spec, h_spec] + pin_specs,
        out_specs=(pl.BlockSpec((rb, c), lambda i, t, ids: (i, 0)),
                   pl.BlockSpec((None, rb, c), lambda i, t, ids: (t, i, 0))))
    return pl.pallas_call(body, name=name, out_shape=(SDS((r // 2, c), F32), SDS((4, r // 2, c), BF16)),
                          grid_spec=grid_spec,
                          compiler_params=_params(("parallel", "arbitrary")))(ids, part, half, *pin_args)


def _sum_half(pre, recv, name, after=None):
    hr, c = pre.shape
    rb = hr

    def body(p_ref, r_ref, *rest):
        rest[-1][...] = (p_ref[...] + r_ref[0].astype(F32) + r_ref[1].astype(F32) + r_ref[2].astype(F32))

    spec = pl.BlockSpec((rb, c), lambda i: (i, 0))
    pin_specs, pin_args = _pinned(after)
    return pl.pallas_call(body, name=name, grid=(1,), out_shape=SDS((hr, c), F32),
                          in_specs=[spec, pl.BlockSpec((3, rb, c), lambda i: (0, i, 0))] + pin_specs,
                          out_specs=spec, compiler_params=_params(("parallel",)))(pre, recv, *pin_args)


def _adam_halves(w, mine, oth, m, v, ids, name, piece=0, prev=None):
    r, c = w.shape
    rb = mine.shape[0] // 2

    def body(ids_ref, w_ref, a_ref, b_ref, m_ref, v_ref, *rest):
        g_ref, d_ref, nm_ref, nv_ref = rest[-4:]
        g = jnp.where(pl.program_id(0) // 2 == ids_ref[4], a_ref[...], b_ref[...])
        g_ref[...] = g
        d_ref[...], nm_ref[...], nv_ref[...] = _adamw(w_ref[...], g, m_ref[...], v_ref[...])

    whole = pl.BlockSpec((rb, c), lambda i, ids: (i + 4 * piece, 0))
    part = pl.BlockSpec((rb, c), lambda i, ids: (i % 2, 0))
    in_specs = [whole, part, part, whole, whole]
    args = [w, mine, oth, m, v]
    aliases = {}
    if prev is not None:
        in_specs += [pl.BlockSpec(memory_space=pl.ANY)] * 4
        args += list(prev)
        aliases = {6: 0, 7: 1, 8: 2, 9: 3}
    grid_spec = pltpu.PrefetchScalarGridSpec(num_scalar_prefetch=1, grid=(4,), in_specs=in_specs,
                                             out_specs=(whole,) * 4)
    return pl.pallas_call(body, name=name, out_shape=tuple(SDS((r, c), F32) for _ in range(4)),
                          grid_spec=grid_spec, input_output_aliases=aliases,
                          compiler_params=_params(("parallel",)))(ids, *args)


def _adam_big(w, ga, gb, m, v, name, piece=0, prev=None, after=None):
    r, c = w.shape
    pr = ga.shape[0]
    steps = ROW_STEPS if pr == r else 2
    rb = pr // steps
    off = piece * steps

    def body(w_ref, ga_ref, gb_ref, m_ref, v_ref, *rest):
        g_ref, d_ref, nm_ref, nv_ref = rest[-4:]
        g = ga_ref[...] + gb_ref[...]
        g_ref[...] = g
        d_ref[...], nm_ref[...], nv_ref[...] = _adamw(w_ref[...], g, m_ref[...], v_ref[...])

    whole = pl.BlockSpec((rb, c), lambda i: (i + off, 0))
    part = pl.BlockSpec((rb, c), lambda i: (i, 0))
    in_specs = [whole, part, part, whole, whole]
    args = [w, ga, gb, m, v]
    aliases = {}
    if prev is not None:
        in_specs += [pl.BlockSpec(memory_space=pl.ANY)] * 4
        args += list(prev)
        aliases = {5: 0, 6: 1, 7: 2, 8: 3}
    if after is not None:
        in_specs.append(pl.BlockSpec(memory_space=pl.ANY))
        args.append(after)
    return pl.pallas_call(
        body, name=name, grid=(steps,), out_shape=tuple(SDS((r, c), F32) for _ in range(4)),
        in_specs=in_specs, out_specs=(whole,) * 4, input_output_aliases=aliases,
        compiler_params=_params(("parallel",)),
    )(*args)


def _adam_small(ws, gathered, ms, vs):
    n = len(ws)

    def body(*refs):
        w_refs, g_refs, m_refs, v_refs = refs[:n], refs[n:2 * n], refs[2 * n:3 * n], refs[3 * n:4 * n]
        outs = refs[4 * n:]
        for i in range(n):
            g = g_refs[i][0]
            for d in range(1, N_DEV):
                g = g + g_refs[i][d]
            delta, nm, nv = _adamw(w_refs[i][...], g, m_refs[i][...], v_refs[i][...])
            outs[i][...] = g
            outs[n + i][...] = delta
            outs[2 * n + i][...] = nm
            outs[3 * n + i][...] = nv

    vmem = pl.BlockSpec(memory_space=pltpu.VMEM)
    shapes = [w.shape for w in ws]
    return pl.pallas_call(
        body, name="adam_small", out_shape=tuple(SDS(s, F32) for s in shapes * 4),
        in_specs=[vmem] * (4 * n), out_specs=tuple([vmem] * (4 * n)),
        compiler_params=pltpu.CompilerParams(vmem_limit_bytes=VMEM_LIMIT_BYTES),
    )(*ws, *gathered, *ms, *vs)


def _sum_loss(gathered):
    def body(g_ref, o_ref):
        tot = g_ref[0]
        for d in range(1, N_DEV):
            tot = tot + g_ref[d]
        o_ref[...] = (0.5 / D_MODEL) * jnp.sum(tot, axis=1, keepdims=True)

    vmem = pl.BlockSpec(memory_space=pltpu.VMEM)
    return pl.pallas_call(body, name="sum_loss", out_shape=SDS((1, 1), F32), in_specs=[vmem],
                          out_specs=vmem)(gathered)


def kernel(x, p, ffn1_w_in, ffn1_w_out, ln1_g, ln1_b, mix_w_in, ssm_lambda_re, ssm_lambda_im, ssm_log_dt, ssm_b_re, ssm_b_im, ssm_c_re, ssm_c_im, ssm_d, ssm_glu_w, ssm_glu_b, gmlp_ln_g, gmlp_ln_b, gmlp_w_s, gmlp_b_s, up_a, up_b, mix_w_out, ln2_g, ln2_b, ffn2_w_in, ffn2_w_out, ln3_g, ln3_b, ple_w_proj, ple_w_gate, loss_target, m_ffn1_w_in, m_ffn1_w_out, m_ln1_g, m_ln1_b, m_mix_w_in, m_ssm_lambda_re, m_ssm_lambda_im, m_ssm_log_dt, m_ssm_b_re, m_ssm_b_im, m_ssm_c_re, m_ssm_c_im, m_ssm_d, m_ssm_glu_w, m_ssm_glu_b, m_gmlp_ln_g, m_gmlp_ln_b, m_gmlp_w_s, m_gmlp_b_s, m_up_a, m_up_b, m_mix_w_out, m_ln2_g, m_ln2_b, m_ffn2_w_in, m_ffn2_w_out, m_ln3_g, m_ln3_b, m_ple_w_proj, m_ple_w_gate, v_ffn1_w_in, v_ffn1_w_out, v_ln1_g, v_ln1_b, v_mix_w_in, v_ssm_lambda_re, v_ssm_lambda_im, v_ssm_log_dt, v_ssm_b_re, v_ssm_b_im, v_ssm_c_re, v_ssm_c_im, v_ssm_d, v_ssm_glu_w, v_ssm_glu_b, v_gmlp_ln_g, v_gmlp_ln_b, v_gmlp_w_s, v_gmlp_b_s, v_up_a, v_up_b, v_mix_w_out, v_ln2_g, v_ln2_b, v_ffn2_w_in, v_ffn2_w_out, v_ln3_g, v_ln3_b, v_ple_w_proj, v_ple_w_gate):
    given = dict(locals())
    order = ("ffn1_w_in", "ffn1_w_out", "ln1_g", "ln1_b", "mix_w_in", "ssm_lambda_re", "ssm_lambda_im",
             "ssm_log_dt", "ssm_b_re", "ssm_b_im", "ssm_c_re", "ssm_c_im", "ssm_d", "ssm_glu_w", "ssm_glu_b",
             "gmlp_ln_g", "gmlp_ln_b", "gmlp_w_s", "gmlp_b_s", "up_a", "up_b", "mix_w_out", "ln2_g", "ln2_b",
             "ffn2_w_in", "ffn2_w_out", "ln3_g", "ln3_b", "ple_w_proj", "ple_w_gate")
    assert set(order) == set(BIG + SMALL)

    shard = {k: given[k][0] for k in BIG}
    shard_b = {k: shard[k].astype(BF16) for k in BIG}
    opt = {k: (shard[k], given["m_" + k][0], given["v_" + k][0]) for k in BIG}
    loss_rows, grad_x, gb, gs, sums, other, gathered, ids, out = _local_step(
        x, given["p"][0], loss_target, {}, {k: given[k] for k in SMALL}, shard_b, opt)

    out = dict(out)
    for k in BIG:
        if k in out:
            continue
        moments = (given["m_" + k][0], given["v_" + k][0])
        if k == "ffn1_w_out":
            out[k] = _adam_halves(shard[k], sums[k], other[k], *moments, ids, "adam_" + k)
        elif k == "ffn1_w_in":
            for q in range(LAST_PIECES):
                kq = LAST_PIECE % q
                out[k] = _adam_halves(shard[k], sums[kq], other[kq], *moments, ids, "adam_" + kq, q, out.get(k))
        else:
            out[k] = _adam_big(shard[k], sums[k], other[k], *moments, "adam_" + k,
                               after=gb[LAST_PIECE % (LAST_PIECES - 1)][0])

    res = _adam_small([_small_view(k, given[k]) for k in SMALL], [gathered[k] for k in SMALL],
                      [_small_view(k, given["m_" + k]) for k in SMALL],
                      [_small_view(k, given["v_" + k]) for k in SMALL])
    ns = len(SMALL)
    for i, k in enumerate(SMALL):
        out[k] = tuple(res[j * ns + i].reshape(given[k].shape) for j in range(4))
    loss = _sum_loss(gathered["loss_rows"]).reshape(())

    lead = lambda k, j: out[k][j][None] if k in BIG else out[k][j]
    return (loss, grad_x, *[lead(k, 0) for k in order], *[lead(k, 1) for k in order],
            *[lead(k, 2) for k in order], *[lead(k, 3) for k in order])
```

```python
import math

import jax
import jax.numpy as jnp
from jax import lax
from jax.experimental import pallas as pl
from jax.experimental.pallas import tpu as pltpu
from jax.experimental.pallas import tpu_sc as plsc

F32 = jnp.float32
BF16 = jnp.bfloat16
MESH = pl.DeviceIdType.MESH
SDS = jax.ShapeDtypeStruct

D_MODEL = 1024
D_FF = 2816
D_SSM = 512
D_GMLP = 512
SSM_GROUPS = 32
SSM_GROUP_CH = 16
SSM_STATE = 64
SSM_LANES = SSM_GROUPS * SSM_STATE
GMLP_HEADS = 8
GMLP_HEAD_DIM = 64
CHUNK = 128
PLE_DIM = 256
LN_EPS = 1e-5
ALPHA = 2.0 ** 0.25

ADAM_LR = 0.001
ADAM_B1 = 0.9
ADAM_B2 = 0.999
ADAM_EPS = 1e-08
ADAM_WD = 0.01
ADAM_STEP = 10
ADAM_C1 = 1.0 - ADAM_B1 ** ADAM_STEP
ADAM_C2 = 1.0 - ADAM_B2 ** ADAM_STEP

N_DEV = 8
VMEM_LIMIT_BYTES = 56 * 1024 * 1024
FFN_COLS = 1408
S5_BLOCKS = 4
S5_BLOCK_IN = D_SSM // S5_BLOCKS
S5_BLOCK_ST = SSM_LANES // S5_BLOCKS
SCAN_LANES = 512
S5_TIME_BLOCK = 512
TN_K_BLOCK = 2048
TN_SMALL_BLOCK = 1024 * 1024
GMLP_CHUNKS = 8
ROW_STEPS = 4
LAST_PIECES = 2
LAST_PIECE = "ffn1_w_in_q%d"
_G0 = math.sqrt(2.0 / math.pi)
_G1 = 0.044715


def _dot(a, b):
    return jnp.dot(a, b, preferred_element_type=F32)


def _dot_nt(a, b):
    return lax.dot_general(a, b, (((1,), (1,)), ((), ())), preferred_element_type=F32)


def _dot_tn(a, b):
    return lax.dot_general(a, b, (((0,), (0,)), ((), ())), preferred_element_type=F32)


def _sigmoid(x):
    return 1.0 / (1.0 + jnp.exp(-x))


def _gelu(x):
    t = jnp.tanh(_G0 * (x + _G1 * x * x * x))
    return 0.5 * x * (1.0 + t)


def _gelu_grad(x):
    t = jnp.tanh(_G0 * (x + _G1 * x * x * x))
    return 0.5 * (1.0 + t) + 0.5 * x * (1.0 - t * t) * _G0 * (1.0 + 3.0 * _G1 * x * x)


def _ln_fwd(r, g, b):
    mu = jnp.mean(r, axis=-1, keepdims=True)
    d = r - mu
    var = jnp.mean(d * d, axis=-1, keepdims=True)
    rstd = lax.rsqrt(var + LN_EPS)
    xh = d * rstd
    return xh * g + b, xh, rstd


def _ln_bwd(dy, xh, rstd, g):
    dxh = dy * g
    m1 = jnp.mean(dxh, axis=-1, keepdims=True)
    m2 = jnp.mean(dxh * xh, axis=-1, keepdims=True)
    return rstd * (dxh - m1 - xh * m2)


def _resident(shape):
    nd = len(shape)
    return pl.BlockSpec(shape, lambda *_: (0,) * nd, pipeline_mode=pl.Buffered(1))


def _fixed(shape):
    nd = len(shape)
    return pl.BlockSpec(shape, lambda *_: (0,) * nd)


def _rows(tm, cols):
    return pl.BlockSpec((tm, cols), lambda i: (i, 0))


def _cols(rows, tm):
    return pl.BlockSpec((rows, tm), lambda i: (0, i))


def _params(sem):
    return pltpu.CompilerParams(dimension_semantics=sem, vmem_limit_bytes=VMEM_LIMIT_BYTES)


class _Exchange:
    def __init__(self, args, out_shape, sems, start, finish):
        self.args, self.out_shape, self.sems = list(args), list(out_shape), list(sems)
        self.start, self.finish = start, finish
        self.cuts = [(0, len(self.out_shape))]


def _call(body, name, grid, in_specs, out_specs, out_shape, args, scratch=(), sem=None, bg=None, aliases=None):
    aliases = {} if aliases is None else aliases
    in_specs, args = list(in_specs), list(args)
    fn = body
    if bg is not None:
        n_args = len(args)

        def fn(*refs):
            body(*refs[:n_args], *refs[n_args + 1:])

        in_specs.append(pl.BlockSpec(memory_space=pl.ANY))
        args.append(bg)
    res = pl.pallas_call(fn, name=name, grid=grid, out_shape=tuple(out_shape), in_specs=in_specs,
                         out_specs=tuple(out_specs), scratch_shapes=list(scratch),
                         input_output_aliases=aliases, compiler_params=_params(sem))(*args)
    return tuple(res), ()


def _run_exchange_on_sequencer(ex, name, collective_id):
    n_i, n_o = len(ex.args), len(ex.out_shape)

    def body(*refs):
        ins, outs, sems = refs[:n_i], refs[n_i:n_i + n_o], refs[n_i + n_o:]
        x, y, c = lax.axis_index("x"), lax.axis_index("y"), lax.axis_index("c")
        barrier = pltpu.get_barrier_semaphore()
        for peer in [(x, y, 1 - c), (1 - x, y, c), (x, 1 - y, c), (1 - x, 1 - y, c)]:
            pl.semaphore_signal(barrier, inc=1, device_id=peer, device_id_type=MESH)
        pl.semaphore_wait(barrier, 4)
        ex.start(ins, outs, sems)
        ex.finish(ins, outs, sems)

    return tuple(pl.kernel(body, out_type=tuple(ex.out_shape),
                           mesh=plsc.ScalarSubcoreMesh(axis_name="sequencer", num_cores=1),
                           scratch_types=list(ex.sems), name=name,
                           compiler_params=pltpu.CompilerParams(collective_id=collective_id))(*ex.args))


def _join(exchanges):
    cuts = []
    a = o = q = 0
    for e in exchanges:
        cuts.append((a, a + len(e.args), o, o + len(e.out_shape), q, q + len(e.sems)))
        a, o, q = cuts[-1][1], cuts[-1][3], cuts[-1][5]

    def start(ins, outs, sems):
        for e, (a0, a1, o0, o1, q0, q1) in zip(exchanges, cuts):
            e.start(ins[a0:a1], outs[o0:o1], sems[q0:q1])

    def finish(ins, outs, sems):
        for e, (a0, a1, o0, o1, q0, q1) in zip(exchanges, cuts):
            e.finish(ins[a0:a1], outs[o0:o1], sems[q0:q1])

    joined = _Exchange(sum((e.args for e in exchanges), []), sum((e.out_shape for e in exchanges), []),
                       sum((e.sems for e in exchanges), []), start, finish)
    joined.cuts = [(c[2], c[3]) for c in cuts]
    return joined


def _ffn_proj(x, w_in, tm, name, bg=None):
    t = x.shape[0]
    nch = D_FF // FFN_COLS

    def body(x_ref, win_ref, xbt_ref, h_ref, a_ref):
        xb = x_ref[...].astype(BF16)
        xbt_ref[...] = xb.T
        for k in range(nch):
            cg = slice(k * FFN_COLS, (k + 1) * FFN_COLS)
            cu = slice(D_FF + k * FFN_COLS, D_FF + (k + 1) * FFN_COLS)
            hg = _dot(xb, win_ref[k])
            hu = _dot(xb, win_ref[nch + k])
            h_ref[:, cg] = hg.astype(BF16)
            h_ref[:, cu] = hu.astype(BF16)
            a_ref[:, cg] = (hg * _sigmoid(hg) * hu).astype(BF16)

    return _call(
        body, name, (t // tm,),
        [_rows(tm, D_MODEL), _resident((2 * nch, D_MODEL, FFN_COLS))],
        (_cols(D_MODEL, tm), _rows(tm, 2 * D_FF), _rows(tm, D_FF)),
        (SDS((D_MODEL, t), BF16), SDS((t, 2 * D_FF), BF16), SDS((t, D_FF), BF16)),
        (x, w_in), sem=("parallel",), bg=bg)


def _ffn_out(x, a, w_out, g, b, tm, name, bg=None):
    t = x.shape[0]

    def body(x_ref, a_ref, wout_ref, g_ref, b_ref, xn_ref, xh_ref, rstd_ref):
        f = _dot(a_ref[...], wout_ref[...])
        y, xh, rstd = _ln_fwd(ALPHA * x_ref[...] + 0.5 * f, g_ref[...], b_ref[...])
        xn_ref[...] = y
        xh_ref[...] = xh
        rstd_ref[...] = rstd

    return _call(
        body, name, (t // tm,),
        [_rows(tm, D_MODEL), _rows(tm, D_FF), _resident((D_FF, D_MODEL)), _fixed((1, D_MODEL)), _fixed((1, D_MODEL))],
        (_rows(tm, D_MODEL), _rows(tm, D_MODEL), _rows(tm, 1)),
        (SDS((t, D_MODEL), F32), SDS((t, D_MODEL), F32), SDS((t, 1), F32)),
        (x, a, w_out, g, b), sem=("parallel",), bg=bg)


def _ffn_out_loss(x, a, w_out, g, b, p, tgt, wpg, wpp, tm):
    t = x.shape[0]

    def body(x_ref, a_ref, wout_ref, g_ref, b_ref, p_ref, t_ref, wpg_ref, wpp_ref,
             xh_ref, rstd_ref, dx_ref, xbt_ref, pbt_ref, dq_ref, de_ref, loss_ref):
        @pl.when(pl.program_id(0) == 0)
        def _():
            loss_ref[...] = jnp.zeros_like(loss_ref)

        f = _dot(a_ref[...], wout_ref[...])
        x3v, xh, rstd = _ln_fwd(ALPHA * x_ref[...] + 0.5 * f, g_ref[...], b_ref[...])
        xh_ref[...] = xh
        rstd_ref[...] = rstd
        xb = x3v.astype(BF16)
        pb = p_ref[...].astype(BF16)
        xbt_ref[...] = xb.T
        pbt_ref[...] = pb.T
        s = _sigmoid(_dot(xb, wpg_ref[...]))
        e = _dot(pb, wpp_ref[...])
        diff = x3v + s * e - t_ref[...]
        loss_ref[...] += jnp.sum(diff * diff, axis=0, keepdims=True)
        dout = diff * (1.0 / D_MODEL)
        de_ref[...] = (dout * s).astype(BF16)
        dq = (dout * e * s * (1.0 - s)).astype(BF16)
        dq_ref[...] = dq
        dx_ref[...] = dout + _dot_nt(dq, wpg_ref[...])

    return _call(
        body, "ffn2_out_loss", (t // tm,),
        [_rows(tm, D_MODEL), _rows(tm, D_FF), _resident((D_FF, D_MODEL)), _fixed((1, D_MODEL)), _fixed((1, D_MODEL)),
         _rows(tm, PLE_DIM), _rows(tm, D_MODEL), _resident((D_MODEL, D_MODEL)), _resident((PLE_DIM, D_MODEL))],
        (_rows(tm, D_MODEL), _rows(tm, 1), _rows(tm, D_MODEL), _cols(D_MODEL, tm), _cols(PLE_DIM, tm),
         _rows(tm, D_MODEL), _rows(tm, D_MODEL), _fixed((1, D_MODEL))),
        (SDS((t, D_MODEL), F32), SDS((t, 1), F32), SDS((t, D_MODEL), F32), SDS((D_MODEL, t), BF16),
         SDS((PLE_DIM, t), BF16), SDS((t, D_MODEL), BF16), SDS((t, D_MODEL), BF16), SDS((1, D_MODEL), F32)),
        (x, a, w_out, g, b, p, tgt, wpg, wpp), sem=("arbitrary",))


def _ffn_bwd(dxn, xh, rstd, h, w_in, w_out, g, tm, name, bg=None):
    t = dxn.shape[0]
    nch = D_FF // FFN_COLS

    def body(dxn_ref, xh_ref, rstd_ref, h_ref, win_ref, wout_ref, g_ref,
             dx_ref, dh_ref, df_ref, dg_ref, db_ref):
        @pl.when(pl.program_id(0) == 0)
        def _():
            dg_ref[...] = jnp.zeros_like(dg_ref)
            db_ref[...] = jnp.zeros_like(db_ref)

        dy = dxn_ref[...]
        xhv = xh_ref[...]
        dr = _ln_bwd(dy, xhv, rstd_ref[...], g_ref[...])
        dg_ref[...] += jnp.sum(dy * xhv, axis=0, keepdims=True)
        db_ref[...] += jnp.sum(dy, axis=0, keepdims=True)
        df = (0.5 * dr).astype(BF16)
        df_ref[...] = df
        dx = ALPHA * dr
        das = [_dot_nt(df, wout_ref[k * FFN_COLS:(k + 1) * FFN_COLS, :]) for k in range(nch)]
        for k in range(nch):
            cg = slice(k * FFN_COLS, (k + 1) * FFN_COLS)
            cu = slice(D_FF + k * FFN_COLS, D_FF + (k + 1) * FFN_COLS)
            hg = h_ref[:, cg].astype(F32)
            hu = h_ref[:, cu].astype(F32)
            sg = _sigmoid(hg)
            silu = hg * sg
            da = das[k]
            dhu = (da * silu).astype(BF16)
            dhg = (da * hu * (sg * (1.0 + hg * (1.0 - sg)))).astype(BF16)
            dh_ref[:, cg] = dhg
            dh_ref[:, cu] = dhu
            dx = dx + _dot_nt(dhg, win_ref[k]) + _dot_nt(dhu, win_ref[nch + k])
        dx_ref[...] = dx

    return _call(
        body, name, (t // tm,),
        [_rows(tm, D_MODEL), _rows(tm, D_MODEL), _rows(tm, 1), _rows(tm, 2 * D_FF),
         _resident((2 * nch, D_MODEL, FFN_COLS)), _resident((D_FF, D_MODEL)), _fixed((1, D_MODEL))],
        (_rows(tm, D_MODEL), _rows(tm, 2 * D_FF), _rows(tm, D_MODEL),
         _fixed((1, D_MODEL)), _fixed((1, D_MODEL))),
        (SDS((t, D_MODEL), F32), SDS((t, 2 * D_FF), BF16), SDS((t, D_MODEL), BF16),
         SDS((1, D_MODEL), F32), SDS((1, D_MODEL), F32)),
        (dxn, xh, rstd, h, w_in, w_out, g), sem=("arbitrary",), bg=bg)


def _tn_matmul(a, b, name, bm, bn, col_block=0, total_cols=None, prev=None, bg=None, a_cols=None, a_t=False):
    t, m = a.shape[::-1] if a_t else a.shape
    a_first = 0
    if a_cols is not None:
        a_first, m = a_cols[0], a_cols[1] * bm
    n = b.shape[1]
    total_cols = n if total_cols is None else total_cols
    whole = bm * bn <= TN_SMALL_BLOCK and (m // bm) * (n // bn) >= 2
    bk = min(2 * TN_K_BLOCK if whole else TN_K_BLOCK, t)
    nk = t // bk
    n_in = 2 if prev is None else 4

    def body(*refs):
        a_ref, b_ref = refs[0], refs[1]
        o_ref, ob_ref = refs[n_in], refs[n_in + 1]
        k = pl.program_id(2)

        @pl.when(k == 0)
        def _():
            o_ref[...] = jnp.zeros_like(o_ref)

        o_ref[...] += _dot(a_ref[...], b_ref[...]) if a_t else _dot_tn(a_ref[...], b_ref[...])

        @pl.when(k == nk - 1)
        def _():
            ob_ref[...] = o_ref[...].astype(BF16)

    a_spec = (pl.BlockSpec((bm, bk), lambda i, j, k: (i + a_first, k)) if a_t
              else pl.BlockSpec((bk, bm), lambda i, j, k: (k, i + a_first)))
    in_specs = [a_spec, pl.BlockSpec((bk, bn), lambda i, j, k: (k, j))]
    args = [a, b]
    aliases = {}
    if prev is not None:
        in_specs += [pl.BlockSpec(memory_space=pl.ANY), pl.BlockSpec(memory_space=pl.ANY)]
        args += list(prev)
        aliases = {2: 0, 3: 1}
        if any(bg is p for p in prev):
            bg = None
    out_spec = pl.BlockSpec((bm, bn), lambda i, j, k: (i, j + col_block))
    return _call(body, name, (m // bm, n // bn, nk), in_specs, (out_spec, out_spec),
                 (SDS((m, total_cols), F32), SDS((m, total_cols), BF16)), args,
                 sem=("parallel", "parallel", "arbitrary"), bg=bg, aliases=aliases)


def _mixin_fwd(x1, w, tm, bg=None):
    t = x1.shape[0]

    def body(x_ref, w_ref, xbt_ref, za_ref, zuv_ref, gab_ref):
        xb = x_ref[...].astype(BF16)
        xbt_ref[...] = xb.T
        za_ref[...] = _dot(xb, w_ref[:, 0:512]).astype(BF16)
        zuv_ref[...] = _dot(xb, w_ref[:, 512:1536]).astype(BF16)
        gab_ref[...] = _dot(xb, w_ref[:, 1536:3584]).astype(BF16)

    return _call(
        body, "mixin_fwd", (t // tm,),
        [_rows(tm, D_MODEL), _resident((D_MODEL, 3584))],
        (_cols(D_MODEL, tm), _rows(tm, 512), _rows(tm, 1024), _rows(tm, 2048)),
        (SDS((D_MODEL, t), BF16), SDS((t, 512), BF16), SDS((t, 1024), BF16), SDS((t, 2048), BF16)),
        (x1, w), sem=("parallel",), bg=bg)


def _mixin_bwd(dx1a, dza, dzuv, dgab, w, tm, bg=None):
    t = dx1a.shape[0]

    def body(d_ref, dza_ref, dzuv_ref, dgab_ref, w_ref, dx_ref):
        dx_ref[...] = (d_ref[...] + _dot_nt(dza_ref[...], w_ref[:, 0:512])
                       + _dot_nt(dzuv_ref[...], w_ref[:, 512:1536])
                       + _dot_nt(dgab_ref[...], w_ref[:, 1536:3584]))

    return _call(
        body, "mixin_bwd", (t // tm,),
        [_rows(tm, D_MODEL), _rows(tm, 512), _rows(tm, 1024), _rows(tm, 2048), _resident((D_MODEL, 3584))],
        (_rows(tm, D_MODEL),), (SDS((t, D_MODEL), F32),),
        (dx1a, dza, dzuv, dgab, w), sem=("parallel",), bg=bg)


def _unrolled(lo, hi, body, carry):
    for j in range(lo, hi):
        carry = body(j, carry)
    return carry


def _scan_fwd(hr_ref, hi_ref, a_ref, ap_ref, carry_ref, seg, cin_ref):
    for lc in range(SSM_LANES // SCAN_LANES):
        ls = slice(lc * SCAN_LANES, (lc + 1) * SCAN_LANES)
        a_r = jnp.broadcast_to(a_ref[0:1, ls], (8, SCAN_LANES))
        a_i = jnp.broadcast_to(a_ref[1:2, ls], (8, SCAN_LANES))

        def step(j, hc, ls=ls, a_r=a_r, a_i=a_i):
            h_r, h_i = hc
            rows = pl.ds(j * 8, 8)
            n_r = a_r * h_r - a_i * h_i + hr_ref[rows, ls]
            n_i = a_r * h_i + a_i * h_r + hi_ref[rows, ls]
            hr_ref[rows, ls] = n_r
            hi_ref[rows, ls] = n_i
            return n_r, n_i

        zero = jnp.zeros((8, SCAN_LANES), F32)
        f_r, f_i = _unrolled(0, seg, step, (zero, zero))
        c_r = carry_ref[0:1, ls]
        c_i = carry_ref[1:2, ls]
        p_r = ap_ref[0:1, ls]
        p_i = ap_ref[1:2, ls]
        rows_r, rows_i = [], []
        for s in range(8):
            rows_r.append(c_r)
            rows_i.append(c_i)
            c_r, c_i = (f_r[s:s + 1] + p_r * c_r - p_i * c_i,
                        f_i[s:s + 1] + p_r * c_i + p_i * c_r)
        carry_ref[0:1, ls] = c_r
        carry_ref[1:2, ls] = c_i
        cin_r = jnp.concatenate(rows_r, axis=0)
        cin_i = jnp.concatenate(rows_i, axis=0)
        if cin_ref is not None:
            cin_ref[0, :, ls] = cin_r
            cin_ref[1, :, ls] = cin_i

        def fix(j, cc, ls=ls, a_r=a_r, a_i=a_i):
            c_r, c_i = cc
            c_r, c_i = a_r * c_r - a_i * c_i, a_r * c_i + a_i * c_r
            rows = pl.ds(j * 8, 8)
            hr_ref[rows, ls] = hr_ref[rows, ls] + c_r
            hi_ref[rows, ls] = hi_ref[rows, ls] + c_i
            return c_r, c_i

        _unrolled(0, seg, fix, (cin_r, cin_i))


def _scan_bwd(gr_ref, gi_ref, hr_ref, hi_ref, cin_ref, a_ref, ap_ref, rcarry_ref, da_ref, seg):
    for lc in range(SSM_LANES // SCAN_LANES):
        ls = slice(lc * SCAN_LANES, (lc + 1) * SCAN_LANES)
        a_r = jnp.broadcast_to(a_ref[0:1, ls], (8, SCAN_LANES))
        a_i = jnp.broadcast_to(a_ref[1:2, ls], (8, SCAN_LANES))

        def step(t, gc, ls=ls, a_r=a_r, a_i=a_i):
            g_r, g_i = gc
            rows = pl.ds((seg - 1 - t) * 8, 8)
            n_r = gr_ref[rows, ls] + a_r * g_r + a_i * g_i
            n_i = gi_ref[rows, ls] + a_r * g_i - a_i * g_r
            gr_ref[rows, ls] = n_r
            gi_ref[rows, ls] = n_i
            return n_r, n_i

        zero = jnp.zeros((8, SCAN_LANES), F32)
        f_r, f_i = _unrolled(0, seg, step, (zero, zero))
        c_r = rcarry_ref[0:1, ls]
        c_i = rcarry_ref[1:2, ls]
        p_r = ap_ref[0:1, ls]
        p_i = ap_ref[1:2, ls]
        rows_r, rows_i = [None] * 8, [None] * 8
        for s in range(7, -1, -1):
            rows_r[s] = c_r
            rows_i[s] = c_i
            c_r, c_i = (f_r[s:s + 1] + p_r * c_r + p_i * c_i,
                        f_i[s:s + 1] + p_r * c_i - p_i * c_r)
        rcarry_ref[0:1, ls] = c_r
        rcarry_ref[1:2, ls] = c_i
        cin_r = jnp.concatenate(rows_r, axis=0)
        cin_i = jnp.concatenate(rows_i, axis=0)

        def fix_row(j_rows, hp_r, hp_i, cc, ls=ls, a_r=a_r, a_i=a_i):
            c_r, c_i, acc_r, acc_i = cc
            c_r, c_i = a_r * c_r + a_i * c_i, a_r * c_i - a_i * c_r
            g_r = gr_ref[j_rows, ls] + c_r
            g_i = gi_ref[j_rows, ls] + c_i
            gr_ref[j_rows, ls] = g_r
            gi_ref[j_rows, ls] = g_i
            acc_r = acc_r + g_r * hp_r + g_i * hp_i
            acc_i = acc_i + g_i * hp_r - g_r * hp_i
            return c_r, c_i, acc_r, acc_i

        def fix(t, cc, ls=ls, fix_row=fix_row):
            j = seg - 1 - t
            rows = pl.ds(j * 8, 8)
            prev = pl.ds((j - 1) * 8, 8)
            return fix_row(rows, hr_ref[prev, ls], hi_ref[prev, ls], cc)

        cc = _unrolled(0, seg - 1, fix, (cin_r, cin_i, zero, zero))
        _, _, acc_r, acc_i = fix_row(pl.ds(0, 8), cin_ref[0, :, ls], cin_ref[1, :, ls], cc)
        da_ref[0, :, ls] += acc_r
        da_ref[1, :, ls] += acc_i


def _s5_fwd(za, sp, bsz, seq, tb, bg=None):
    nb = seq // tb
    seg = tb // 8
    t = bsz * seq

    def body(za_ref, perm_ref, permt_ref, mre_ref, mim_ref, nre_ref, nim_ref, a_ref, ap_ref,
             dsk_ref, gw_ref, gb_ref, out_ref, outt_ref, y2_ref, car_ref, hr_ref, hi_ref, carry_ref):
        @pl.when(pl.program_id(1) == 0)
        def _():
            carry_ref[...] = jnp.zeros_like(carry_ref)

        car_ref[0] = carry_ref[...]
        up = _dot(perm_ref[...], za_ref[...])
        upb = up.astype(BF16)
        for bb in range(S5_BLOCKS):
            ub = upb[:, bb * S5_BLOCK_IN:(bb + 1) * S5_BLOCK_IN]
            st = slice(bb * S5_BLOCK_ST, (bb + 1) * S5_BLOCK_ST)
            hr_ref[:, st] = _dot(ub, mre_ref[bb])
            hi_ref[:, st] = _dot(ub, mim_ref[bb])
        _scan_fwd(hr_ref, hi_ref, a_ref, ap_ref, carry_ref, seg, None)
        ys = []
        for bb in range(S5_BLOCKS):
            st = slice(bb * S5_BLOCK_ST, (bb + 1) * S5_BLOCK_ST)
            ys.append(_dot(hr_ref[:, st].astype(BF16), nre_ref[bb])
                      - _dot(hi_ref[:, st].astype(BF16), nim_ref[bb]))
        y2 = jnp.concatenate(ys, axis=1) + dsk_ref[...] * up
        y2_ref[...] = y2
        y3 = _gelu(y2)
        gl = _dot(y3.astype(BF16), gw_ref[...]) + gb_ref[...]
        oa = y3 * _sigmoid(gl)
        out = _dot(permt_ref[...], oa.astype(BF16)).astype(BF16)
        out_ref[...] = out
        outt_ref[...] = out.T

    blk = pl.BlockSpec((tb, D_SSM), lambda b, j: (b * nb + j, 0))
    blk_t = pl.BlockSpec((D_SSM, tb), lambda b, j: (0, b * nb + j))
    m_shape = (S5_BLOCKS, S5_BLOCK_IN, S5_BLOCK_ST)
    n_shape = (S5_BLOCKS, S5_BLOCK_ST, S5_BLOCK_IN)
    return _call(
        body, "s5_fwd", (bsz, nb),
        [blk, _fixed((tb, tb)), _fixed((tb, tb)), _fixed(m_shape), _fixed(m_shape), _fixed(n_shape),
         _fixed(n_shape), _fixed((2, SSM_LANES)), _fixed((2, SSM_LANES)), _fixed((1, D_SSM)),
         _fixed((D_SSM, D_SSM)), _fixed((1, D_SSM))],
        (blk, blk_t, blk, pl.BlockSpec((1, 2, SSM_LANES), lambda b, j: (b * nb + j, 0, 0))),
        (SDS((t, D_SSM), BF16), SDS((D_SSM, t), BF16), SDS((t, D_SSM), F32), SDS((bsz * nb, 2, SSM_LANES), F32)),
        (za, sp["perm"], sp["permt"], sp["mre"], sp["mim"], sp["nre"], sp["nim"], sp["a"], sp["ap"],
         sp["dskip"], sp["glu_w"], sp["glu_b"]),
        scratch=[pltpu.VMEM((tb, SSM_LANES), F32), pltpu.VMEM((tb, SSM_LANES), F32),
                 pltpu.VMEM((2, SSM_LANES), F32)],
        sem=("arbitrary", "arbitrary"), bg=bg)


def _s5_bwd(za, y2p, doa, carries, sp, bsz, seq, tb, bg=None):
    nb = seq // tb
    seg = tb // 8
    t = bsz * seq

    def body(za_ref, y2_ref, doa_ref, car_ref, perm_ref, permt_ref, mre_ref, mim_ref, mtre_ref, mtim_ref,
             nre_ref, nim_ref, ntre_ref, ntim_ref, a_ref, ap_ref, dsk_ref, gw_ref, gwt_ref, gb_ref,
             dza_ref, dmr_ref, dmi_ref, dnr_ref, dni_ref, da_ref, ddsk_ref, dgw_ref, dgb_ref,
             hr_ref, hi_ref, gr_ref, gi_ref, cin_ref, carry_ref, rcarry_ref):
        first = jnp.logical_and(pl.program_id(0) == 0, pl.program_id(1) == 0)

        @pl.when(first)
        def _():
            for r in (dmr_ref, dmi_ref, dnr_ref, dni_ref, da_ref, ddsk_ref, dgw_ref, dgb_ref):
                r[...] = jnp.zeros_like(r)

        @pl.when(pl.program_id(1) == 0)
        def _():
            rcarry_ref[...] = jnp.zeros_like(rcarry_ref)

        carry_ref[...] = car_ref[0]
        perm = perm_ref[...]
        up = _dot(perm, za_ref[...])
        upb = up.astype(BF16)
        for bb in range(S5_BLOCKS):
            ub = upb[:, bb * S5_BLOCK_IN:(bb + 1) * S5_BLOCK_IN]
            st = slice(bb * S5_BLOCK_ST, (bb + 1) * S5_BLOCK_ST)
            hr_ref[:, st] = _dot(ub, mre_ref[bb])
            hi_ref[:, st] = _dot(ub, mim_ref[bb])
        _scan_fwd(hr_ref, hi_ref, a_ref, ap_ref, carry_ref, seg, cin_ref)

        y2 = y2_ref[...]
        y3 = _gelu(y2)
        y3b = y3.astype(BF16)
        sg = _sigmoid(_dot(y3b, gw_ref[...]) + gb_ref[...])
        d0 = doa_ref[...]
        d_hi = d0.astype(BF16)
        d1 = d0 - d_hi.astype(F32)
        d_mid = d1.astype(BF16)
        d_lo = (d1 - d_mid.astype(F32)).astype(BF16)
        doap = _dot(perm, d_hi) + _dot(perm, d_mid) + _dot(perm, d_lo)
        dgl = doap * y3 * sg * (1.0 - sg)
        dglb = dgl.astype(BF16)
        dy3 = doap * sg + _dot(dglb, gwt_ref[...])
        dgw_ref[...] += _dot_tn(y3b, dglb)
        dgb_ref[...] += jnp.sum(dgl, axis=0, keepdims=True)
        dy2 = dy3 * _gelu_grad(y2)
        ddsk_ref[...] += jnp.sum(dy2 * up, axis=0, keepdims=True)
        dyb = dy2.astype(BF16)
        for bb in range(S5_BLOCKS):
            dyc = dyb[:, bb * S5_BLOCK_IN:(bb + 1) * S5_BLOCK_IN]
            st = slice(bb * S5_BLOCK_ST, (bb + 1) * S5_BLOCK_ST)
            gr_ref[:, st] = _dot(dyc, ntre_ref[bb])
            gi_ref[:, st] = -_dot(dyc, ntim_ref[bb])
            dnr_ref[bb] += _dot_tn(hr_ref[:, st].astype(BF16), dyc)
            dni_ref[bb] += -_dot_tn(hi_ref[:, st].astype(BF16), dyc)
        _scan_bwd(gr_ref, gi_ref, hr_ref, hi_ref, cin_ref, a_ref, ap_ref, rcarry_ref, da_ref, seg)
        dus = []
        for bb in range(S5_BLOCKS):
            st = slice(bb * S5_BLOCK_ST, (bb + 1) * S5_BLOCK_ST)
            grb = gr_ref[:, st].astype(BF16)
            gib = gi_ref[:, st].astype(BF16)
            dus.append(_dot(grb, mtre_ref[bb]) + _dot(gib, mtim_ref[bb]))
            ub = upb[:, bb * S5_BLOCK_IN:(bb + 1) * S5_BLOCK_IN]
            dmr_ref[bb] += _dot_tn(ub, grb)
            dmi_ref[bb] += _dot_tn(ub, gib)
        du = jnp.concatenate(dus, axis=1) + dy2 * dsk_ref[...]
        dza_ref[...] = _dot(permt_ref[...], du.astype(BF16)).astype(BF16)

    def rev(b, j):
        return (b * nb + (nb - 1 - j), 0)

    blk = pl.BlockSpec((tb, D_SSM), rev)
    m_shape = (S5_BLOCKS, S5_BLOCK_IN, S5_BLOCK_ST)
    n_shape = (S5_BLOCKS, S5_BLOCK_ST, S5_BLOCK_IN)
    return _call(
        body, "s5_bwd", (bsz, nb),
        [blk, blk, blk, pl.BlockSpec((1, 2, SSM_LANES), lambda b, j: (b * nb + (nb - 1 - j), 0, 0)),
         _fixed((tb, tb)), _fixed((tb, tb)), _fixed(m_shape), _fixed(m_shape), _fixed(n_shape), _fixed(n_shape),
         _fixed(n_shape), _fixed(n_shape), _fixed(m_shape), _fixed(m_shape),
         _fixed((2, SSM_LANES)), _fixed((2, SSM_LANES)), _fixed((1, D_SSM)),
         _fixed((D_SSM, D_SSM)), _fixed((D_SSM, D_SSM)), _fixed((1, D_SSM))],
        (blk, _fixed(m_shape), _fixed(m_shape), _fixed(n_shape), _fixed(n_shape),
         _fixed((2, 8, SSM_LANES)), _fixed((1, D_SSM)), _fixed((D_SSM, D_SSM)), _fixed((1, D_SSM))),
        (SDS((t, D_SSM), BF16), SDS(m_shape, F32), SDS(m_shape, F32), SDS(n_shape, F32), SDS(n_shape, F32),
         SDS((2, 8, SSM_LANES), F32), SDS((1, D_SSM), F32), SDS((D_SSM, D_SSM), F32), SDS((1, D_SSM), F32)),
        (za, y2p, doa, carries, sp["perm"], sp["permt"], sp["mre"], sp["mim"], sp["mtre"], sp["mtim"],
         sp["nre"], sp["nim"], sp["ntre"], sp["ntim"], sp["a"], sp["ap"], sp["dskip"], sp["glu_w"],
         sp["glu_wt"], sp["glu_b"]),
        scratch=[pltpu.VMEM((tb, SSM_LANES), F32), pltpu.VMEM((tb, SSM_LANES), F32),
                 pltpu.VMEM((tb, SSM_LANES), F32), pltpu.VMEM((tb, SSM_LANES), F32),
                 pltpu.VMEM((2, 8, SSM_LANES), F32), pltpu.VMEM((2, SSM_LANES), F32),
                 pltpu.VMEM((2, SSM_LANES), F32)],
        sem=("arbitrary", "arbitrary"), bg=bg)


def _gmlp_spatial(ws_ref, vb):
    lane = lax.broadcasted_iota(jnp.int32, (CHUNK, 128), 1)
    parts = []
    for j in range(GMLP_HEADS // 2):
        vp = vb[:, 128 * j:128 * (j + 1)]
        parts.append(jnp.where(lane < GMLP_HEAD_DIM, _dot(ws_ref[2 * j], vp), _dot(ws_ref[2 * j + 1], vp)))
    return jnp.concatenate(parts, axis=1)


def _gmlp_fwd(zuv, ln_g, ln_b, wsm, bias, bg=None):
    t = zuv.shape[0]
    chunks = min(GMLP_CHUNKS, t // CHUNK)

    def body(z_ref, g_ref, b_ref, ws_ref, bias_ref, out_ref, outt_ref):
        for ch in range(chunks):
            rows = slice(ch * CHUNK, (ch + 1) * CHUNK)
            u = _gelu(z_ref[rows, 0:D_GMLP].astype(F32))
            v0 = _gelu(z_ref[rows, D_GMLP:2 * D_GMLP].astype(F32))
            v, _, _ = _ln_fwd(v0, g_ref[...], b_ref[...])
            s = _gmlp_spatial(ws_ref, v.astype(BF16)) + bias_ref[...]
            out = (u * s).astype(BF16)
            out_ref[rows, :] = out
            outt_ref[:, rows] = out.T

    step = chunks * CHUNK
    return _call(
        body, "gmlp_fwd", (t // step,),
        [_rows(step, 2 * D_GMLP), _fixed((1, D_GMLP)), _fixed((1, D_GMLP)),
         _fixed((GMLP_HEADS, CHUNK, CHUNK)), _fixed((CHUNK, D_GMLP))],
        (_rows(step, D_GMLP), _cols(D_GMLP, step)), (SDS((t, D_GMLP), BF16), SDS((D_GMLP, t), BF16)),
        (zuv, ln_g, ln_b, wsm, bias), sem=("parallel",), bg=bg)


def _gmlp_bwd(zuv, dgm, ln_g, ln_b, wsm, wsmt, bias, bg=None):
    t = zuv.shape[0]
    chunks = min(GMLP_CHUNKS, t // CHUNK)

    def body(z_ref, d_ref, g_ref, b_ref, ws_ref, wst_ref, bias_ref,
             dz_ref, dws_ref, dbias_ref, dg_ref, db_ref):
        @pl.when(pl.program_id(0) == 0)
        def _():
            for r in (dws_ref, dbias_ref, dg_ref, db_ref):
                r[...] = jnp.zeros_like(r)

        gam = g_ref[...]
        lane = lax.broadcasted_iota(jnp.int32, (CHUNK, 128), 1)
        tril = (lax.broadcasted_iota(jnp.int32, (CHUNK, CHUNK), 0)
                >= lax.broadcasted_iota(jnp.int32, (CHUNK, CHUNK), 1))
        zero_b = jnp.zeros((CHUNK, 128), BF16)
        for ch in range(chunks):
            rows = slice(ch * CHUNK, (ch + 1) * CHUNK)
            zu = z_ref[rows, 0:D_GMLP].astype(F32)
            zv = z_ref[rows, D_GMLP:2 * D_GMLP].astype(F32)
            u = _gelu(zu)
            v0 = _gelu(zv)
            v, vhat, rstd = _ln_fwd(v0, gam, b_ref[...])
            vb = v.astype(BF16)
            s = _gmlp_spatial(ws_ref, vb) + bias_ref[...]
            d = d_ref[rows, :]
            dz_ref[rows, 0:D_GMLP] = (d * s * _gelu_grad(zu)).astype(BF16)
            ds = d * u
            dbias_ref[...] += ds
            dsb = ds.astype(BF16)
            parts = []
            for j in range(GMLP_HEADS // 2):
                dsp = dsb[:, 128 * j:128 * (j + 1)]
                vp = vb[:, 128 * j:128 * (j + 1)]
                parts.append(jnp.where(lane < GMLP_HEAD_DIM, _dot(wst_ref[2 * j], dsp),
                                       _dot(wst_ref[2 * j + 1], dsp)))
                lo = jnp.where(lane < GMLP_HEAD_DIM, dsp, zero_b)
                hi = jnp.where(lane < GMLP_HEAD_DIM, zero_b, dsp)
                dws_ref[2 * j] += jnp.where(tril, _dot_nt(lo, vp), 0.0)
                dws_ref[2 * j + 1] += jnp.where(tril, _dot_nt(hi, vp), 0.0)
            dv = jnp.concatenate(parts, axis=1)
            dg_ref[...] += jnp.sum(dv * vhat, axis=0, keepdims=True)
            db_ref[...] += jnp.sum(dv, axis=0, keepdims=True)
            dz_ref[rows, D_GMLP:2 * D_GMLP] = (_ln_bwd(dv, vhat, rstd, gam) * _gelu_grad(zv)).astype(BF16)

    step = chunks * CHUNK
    return _call(
        body, "gmlp_bwd", (t // step,),
        [_rows(step, 2 * D_GMLP), _rows(step, D_GMLP), _fixed((1, D_GMLP)), _fixed((1, D_GMLP)),
         _fixed((GMLP_HEADS, CHUNK, CHUNK)), _fixed((GMLP_HEADS, CHUNK, CHUNK)), _fixed((CHUNK, D_GMLP))],
        (_rows(step, 2 * D_GMLP), _fixed((GMLP_HEADS, CHUNK, CHUNK)), _fixed((CHUNK, D_GMLP)),
         _fixed((1, D_GMLP)), _fixed((1, D_GMLP))),
        (SDS((t, 2 * D_GMLP), BF16), SDS((GMLP_HEADS, CHUNK, CHUNK), F32), SDS((CHUNK, D_GMLP), F32),
         SDS((1, D_GMLP), F32), SDS((1, D_GMLP), F32)),
        (zuv, dgm, ln_g, ln_b, wsm, wsmt, bias), sem=("arbitrary",), bg=bg)


def _mixout_fwd(x1, s5o, gm, gab, ua, ub, wmo, g, b, tm, bg=None):
    t = x1.shape[0]

    def body(x_ref, s_ref, m_ref, gab_ref, ua_ref, ub_ref, wmo_ref, g_ref, b_ref,
             xn_ref, xh_ref, rstd_ref):
        ya = _dot(s_ref[...], ua_ref[...])
        yb = _dot(m_ref[...], ub_ref[...])
        mix = (_sigmoid(gab_ref[:, 0:D_MODEL].astype(F32)) * ya
               + _sigmoid(gab_ref[:, D_MODEL:2 * D_MODEL].astype(F32)) * yb)
        r = ALPHA * x_ref[...] + _dot(mix.astype(BF16), wmo_ref[...])
        y, xh, rstd = _ln_fwd(r, g_ref[...], b_ref[...])
        xn_ref[...] = y
        xh_ref[...] = xh
        rstd_ref[...] = rstd

    return _call(
        body, "mixout_fwd", (t // tm,),
        [_rows(tm, D_MODEL), _rows(tm, D_SSM), _rows(tm, D_GMLP), _rows(tm, 2 * D_MODEL),
         _resident((D_SSM, D_MODEL)), _resident((D_GMLP, D_MODEL)), _resident((D_MODEL, D_MODEL)),
         _fixed((1, D_MODEL)), _fixed((1, D_MODEL))],
        (_rows(tm, D_MODEL), _rows(tm, D_MODEL), _rows(tm, 1)),
        (SDS((t, D_MODEL), F32), SDS((t, D_MODEL), F32), SDS((t, 1), F32)),
        (x1, s5o, gm, gab, ua, ub, wmo, g, b), sem=("parallel",), bg=bg)


def _mixout_bwd(dx2, xh, rstd, s5o, gm, gab, ua, ub, wmo, g, tm, bg=None):
    t = dx2.shape[0]

    def body(d_ref, xh_ref, rstd_ref, s_ref, m_ref, gab_ref, ua_ref, ub_ref, wmo_ref, g_ref,
             dx1_ref, dmx_ref, mb_ref, dya_ref, dyb_ref, ds5_ref, dgm_ref, dgab_ref, dg_ref, db_ref):
        @pl.when(pl.program_id(0) == 0)
        def _():
            dg_ref[...] = jnp.zeros_like(dg_ref)
            db_ref[...] = jnp.zeros_like(db_ref)

        dy = d_ref[...]
        xhv = xh_ref[...]
        dr = _ln_bwd(dy, xhv, rstd_ref[...], g_ref[...])
        dg_ref[...] += jnp.sum(dy * xhv, axis=0, keepdims=True)
        db_ref[...] += jnp.sum(dy, axis=0, keepdims=True)
        dx1_ref[...] = ALPHA * dr
        drb = dr.astype(BF16)
        dmx_ref[...] = drb
        dm = _dot_nt(drb, wmo_ref[...])
        ya = _dot(s_ref[...], ua_ref[...])
        yb = _dot(m_ref[...], ub_ref[...])
        sa = _sigmoid(gab_ref[:, 0:D_MODEL].astype(F32))
        sb = _sigmoid(gab_ref[:, D_MODEL:2 * D_MODEL].astype(F32))
        mb_ref[...] = (sa * ya + sb * yb).astype(BF16).T
        dya = (dm * sa).astype(BF16)
        dyb = (dm * sb).astype(BF16)
        dya_ref[...] = dya
        dyb_ref[...] = dyb
        dgab_ref[:, 0:D_MODEL] = (dm * ya * sa * (1.0 - sa)).astype(BF16)
        dgab_ref[:, D_MODEL:2 * D_MODEL] = (dm * yb * sb * (1.0 - sb)).astype(BF16)
        ds5_ref[...] = _dot_nt(dya, ua_ref[...])
        dgm_ref[...] = _dot_nt(dyb, ub_ref[...])

    return _call(
        body, "mixout_bwd", (t // tm,),
        [_rows(tm, D_MODEL), _rows(tm, D_MODEL), _rows(tm, 1), _rows(tm, D_SSM), _rows(tm, D_GMLP),
         _rows(tm, 2 * D_MODEL), _resident((D_SSM, D_MODEL)), _resident((D_GMLP, D_MODEL)),
         _resident((D_MODEL, D_MODEL)), _fixed((1, D_MODEL))],
        (_rows(tm, D_MODEL), _rows(tm, D_MODEL), _cols(D_MODEL, tm), _rows(tm, D_MODEL),
         _rows(tm, D_MODEL), _rows(tm, D_SSM), _rows(tm, D_GMLP), _rows(tm, 2 * D_MODEL),
         _fixed((1, D_MODEL)), _fixed((1, D_MODEL))),
        (SDS((t, D_MODEL), F32), SDS((t, D_MODEL), BF16), SDS((D_MODEL, t), BF16),
         SDS((t, D_MODEL), BF16), SDS((t, D_MODEL), BF16), SDS((t, D_SSM), F32),
         SDS((t, D_GMLP), F32), SDS((t, 2 * D_MODEL), BF16),
         SDS((1, D_MODEL), F32), SDS((1, D_MODEL), F32)),
        (dx2, xh, rstd, s5o, gm, gab, ua, ub, wmo, g), sem=("arbitrary",), bg=bg)


def _s5_discretise(lre, lim, log_dt, bre, bim):
    dt = jnp.exp(log_dt)[:, None]
    mag = jnp.exp(lre * dt)
    abr = mag * jnp.cos(lim * dt)
    abi = mag * jnp.sin(lim * dt)
    nr = abr - 1.0
    ni = abi
    den = lre * lre + lim * lim
    cr = ((nr * lre + ni * lim) / den)[..., None]
    ci = ((ni * lre - nr * lim) / den)[..., None]
    return abr, abi, cr * bre - ci * bim, cr * bim + ci * bre


def _block_diag_in(bb):
    v = bb.reshape(S5_BLOCKS, 8, SSM_STATE, SSM_GROUP_CH).transpose(0, 1, 3, 2)
    return jnp.einsum("bgip,gh->bgihp", v, jnp.eye(8, dtype=bb.dtype)).reshape(
        S5_BLOCKS, S5_BLOCK_IN, S5_BLOCK_ST)


def _block_diag_in_t(dm):
    v = dm.reshape(S5_BLOCKS, 8, SSM_GROUP_CH, 8, SSM_STATE)
    d = jnp.einsum("bgihp,gh->bgip", v, jnp.eye(8, dtype=dm.dtype))
    return d.transpose(0, 1, 3, 2).reshape(SSM_GROUPS, SSM_STATE, SSM_GROUP_CH)


def _block_diag_out(cc):
    v = cc.reshape(S5_BLOCKS, 8, SSM_GROUP_CH, SSM_STATE)
    return jnp.einsum("bgip,gh->bgphi", v, jnp.eye(8, dtype=cc.dtype)).reshape(
        S5_BLOCKS, S5_BLOCK_ST, S5_BLOCK_IN)


def _block_diag_out_t(dn):
    v = dn.reshape(S5_BLOCKS, 8, SSM_STATE, 8, SSM_GROUP_CH)
    d = jnp.einsum("bgphi,gh->bgip", v, jnp.eye(8, dtype=dn.dtype))
    return d.reshape(SSM_GROUPS, SSM_GROUP_CH, SSM_STATE)


def _s5_setup(lre, lim, log_dt, bre, bim, cre, cim, d_skip, glu_w, glu_b, tb):
    seg = tb // 8
    abr, abi, bbr, bbi = _s5_discretise(lre, lim, log_dt, bre, bim)
    pr, pi = abr, abi
    for _ in range(int(math.log2(seg))):
        pr, pi = pr * pr - pi * pi, 2.0 * pr * pi
    rows = jnp.arange(tb)
    src = (rows % 8) * seg + rows // 8
    perm = (src[:, None] == jnp.arange(tb)[None, :]).astype(BF16)
    mre = _block_diag_in(bbr)
    mim = _block_diag_in(bbi)
    nre = _block_diag_out(cre)
    nim = _block_diag_out(cim)
    return {
        "perm": perm, "permt": perm.T,
        "mre": mre.astype(BF16), "mim": mim.astype(BF16),
        "mtre": mre.transpose(0, 2, 1).astype(BF16), "mtim": mim.transpose(0, 2, 1).astype(BF16),
        "nre": nre.astype(BF16), "nim": nim.astype(BF16),
        "ntre": nre.transpose(0, 2, 1).astype(BF16), "ntim": nim.transpose(0, 2, 1).astype(BF16),
        "a": jnp.stack([abr.reshape(-1), abi.reshape(-1)]),
        "ap": jnp.stack([pr.reshape(-1), pi.reshape(-1)]),
        "dskip": d_skip.reshape(1, D_SSM), "glu_w": glu_w, "glu_wt": glu_w.T,
        "glu_b": glu_b.reshape(1, D_SSM),
    }


BIG = ("ffn1_w_in", "ffn1_w_out", "mix_w_in", "ssm_glu_w", "up_a", "up_b", "mix_w_out",
       "ffn2_w_in", "ffn2_w_out", "ple_w_proj", "ple_w_gate")
BIG_AXIS = {"ffn1_w_in": 1, "ffn1_w_out": 0, "mix_w_in": 1, "ssm_glu_w": 0, "up_a": 1, "up_b": 1,
            "mix_w_out": 0, "ffn2_w_in": 1, "ffn2_w_out": 0, "ple_w_proj": 1, "ple_w_gate": 0}
SHARD_MAJOR = 2
GATHER_AXIS = dict(BIG_AXIS, ffn1_w_in=SHARD_MAJOR, ffn2_w_in=SHARD_MAJOR)
GATHER_ORDER = (("ffn1_w_in",), ("ffn1_w_out",), ("mix_w_in",), ("ssm_glu_w", "up_a", "up_b", "mix_w_out"),
                ("ffn2_w_in",), ("ffn2_w_out", "ple_w_gate", "ple_w_proj"))
GATHER_FIRST_ID = 1
REDUCE_FIRST_ID = 7
SMALL = ("ln1_g", "ln1_b", "ssm_lambda_re", "ssm_lambda_im", "ssm_log_dt", "ssm_b_re", "ssm_b_im",
         "ssm_c_re", "ssm_c_im", "ssm_d", "ssm_glu_b", "gmlp_ln_g", "gmlp_ln_b", "gmlp_w_s",
         "gmlp_b_s", "ln2_g", "ln2_b", "ln3_g", "ln3_b")
SMALL_VIEW = {"ssm_b_re": (SSM_GROUPS, SSM_STATE * SSM_GROUP_CH), "ssm_b_im": (SSM_GROUPS, SSM_STATE * SSM_GROUP_CH)}


def _small_view(k, a):
    return a.reshape(SMALL_VIEW[k]) if k in SMALL_VIEW else a


def _place():
    return lax.axis_index("x"), lax.axis_index("y"), lax.axis_index("c")


def _other_chips(x, y):
    return [(1 - x, y), (x, 1 - y), (1 - x, 1 - y)]


def _window(ref, shard_shape, axis, chip, half):
    r, c = shard_shape
    hr = r // 2
    if axis == SHARD_MAJOR:
        return ref.at[chip] if half is None else ref.at[chip, pl.ds(half * hr, hr), :]
    if axis == 0:
        if half is None:
            return ref.at[pl.ds(chip * r, r), :]
        return ref.at[pl.ds(chip * r + half * hr, hr), :]
    if half is None:
        return ref.at[:, pl.ds(chip * c, c)]
    return ref.at[pl.ds(half * hr, hr), pl.ds(chip * c, c)]


def _gather_weights(shards, axes):
    n = len(shards)
    shapes = [s.shape for s in shards]
    full = [{0: (4 * r, c), 1: (r, 4 * c), SHARD_MAJOR: (4, r, c)}[ax] for (r, c), ax in zip(shapes, axes)]

    def remote(sems, i, k, src, dst, to):
        return pltpu.make_async_remote_copy(src_ref=src, dst_ref=dst, send_sem=sems[0].at[6 * i + k],
                                            recv_sem=sems[1].at[6 * i + k], device_id=to, device_id_type=MESH)

    def own_copies(ins, outs, sems):
        x, y, c = _place()
        me = 2 * x + y
        cps = []
        for i in range(n):
            hr = shapes[i][0] // 2
            mine = ins[i].at[pl.ds(c * hr, hr), :]
            for j, (cx, cy) in enumerate(_other_chips(x, y)):
                cps.append(remote(sems, i, j, mine, _window(outs[i], shapes[i], axes[i], me, c), (cx, cy, c)))
        local = [pltpu.make_async_copy(ins[i], _window(outs[i], shapes[i], axes[i], me, None), sems[2].at[i])
                 for i in range(n)]
        return cps, local

    def start(ins, outs, sems):
        cps, local = own_copies(ins, outs, sems)
        for cp in local + cps:
            cp.start()

    def finish(ins, outs, sems):
        x, y, c = _place()
        sibling = (x, y, 1 - c)
        passed = []
        for j, (cx, cy) in enumerate(_other_chips(x, y)):
            for i in range(n):
                w = _window(outs[i], shapes[i], axes[i], 2 * cx + cy, c)
                remote(sems, i, j, w, w, (cx, cy, c)).wait_recv()
                cp = remote(sems, i, 3 + j, w, w, sibling)
                cp.start()
                passed.append(cp)
        for j, (cx, cy) in enumerate(_other_chips(x, y)):
            for i in range(n):
                w = _window(outs[i], shapes[i], axes[i], 2 * cx + cy, 1 - c)
                remote(sems, i, 3 + j, w, w, sibling).wait_recv()
        cps, local = own_copies(ins, outs, sems)
        for cp in cps + passed:
            cp.wait_send()
        for cp in local:
            cp.wait()

    return _Exchange(shards, [SDS(f, BF16) for f in full],
                     [pltpu.SemaphoreType.DMA((6 * n,)), pltpu.SemaphoreType.DMA((6 * n,)),
                      pltpu.SemaphoreType.DMA((n,))], start, finish)


def _scatter_grads(parts, shapes, axes):
    n = len(parts)

    def copies(ins, outs, sems):
        x, y, c = _place()
        return [pltpu.make_async_remote_copy(
            src_ref=_window(ins[i], shapes[i], axes[i], 2 * cx + cy, None), dst_ref=outs[i].at[j],
            send_sem=sems[0].at[3 * i + j], recv_sem=sems[1].at[3 * i + j],
            device_id=(cx, cy, c), device_id_type=MESH)
            for i in range(n) for j, (cx, cy) in enumerate(_other_chips(x, y))]

    def start(ins, outs, sems):
        for cp in copies(ins, outs, sems):
            cp.start()

    def finish(ins, outs, sems):
        for cp in copies(ins, outs, sems):
            cp.wait()

    return _Exchange(parts, [SDS((3,) + tuple(s), BF16) for s in shapes],
                     [pltpu.SemaphoreType.DMA((3 * n,)), pltpu.SemaphoreType.DMA((3 * n,))], start, finish)


def _swap_halves(parts, shapes, axes):
    n = len(parts)

    def copies(ins, outs, sems):
        x, y, c = _place()
        cps = []
        for i in range(n):
            r, _ = shapes[i]
            hr = r // 2
            if axes[i] == 0:
                cps += [pltpu.make_async_remote_copy(
                    src_ref=ins[i].at[pl.ds(k * r + (1 - c) * hr, hr), :], dst_ref=outs[i].at[k],
                    send_sem=sems[0].at[i], recv_sem=sems[1].at[i], device_id=(x, y, 1 - c),
                    device_id_type=MESH) for k in range(4)]
            else:
                cps.append(pltpu.make_async_remote_copy(
                    src_ref=ins[i].at[pl.ds((1 - c) * hr, hr), :], dst_ref=outs[i],
                    send_sem=sems[0].at[i], recv_sem=sems[1].at[i], device_id=(x, y, 1 - c),
                    device_id_type=MESH))
        return cps

    def start(ins, outs, sems):
        for cp in copies(ins, outs, sems):
            cp.start()

    def finish(ins, outs, sems):
        x, y, c = _place()
        for i in range(n):
            pltpu.make_async_remote_copy(src_ref=outs[i], dst_ref=outs[i], send_sem=sems[0].at[i],
                                         recv_sem=sems[1].at[i], device_id=(x, y, 1 - c),
                                         device_id_type=MESH).wait()

    out = [SDS((4, r // 2, c), BF16) if ax == 0 else SDS((r // 2, 4 * c), BF16)
           for (r, c), ax in zip(shapes, axes)]
    return _Exchange(parts, out, [pltpu.SemaphoreType.DMA((n,)), pltpu.SemaphoreType.DMA((n,))], start, finish)


def _scatter_halves(pres, shapes):
    n = len(pres)

    def copies(ins, outs, sems):
        x, y, c = _place()
        return [pltpu.make_async_remote_copy(
            src_ref=ins[i].at[1 + j], dst_ref=outs[i].at[j], send_sem=sems[0].at[3 * i + j],
            recv_sem=sems[1].at[3 * i + j], device_id=(cx, cy, c), device_id_type=MESH)
            for i in range(n) for j, (cx, cy) in enumerate(_other_chips(x, y))]

    def start(ins, outs, sems):
        for cp in copies(ins, outs, sems):
            cp.start()

    def finish(ins, outs, sems):
        for cp in copies(ins, outs, sems):
            cp.wait()

    return _Exchange(pres, [SDS((3, r // 2, c), BF16) for r, c in shapes],
                     [pltpu.SemaphoreType.DMA((3 * n,)), pltpu.SemaphoreType.DMA((3 * n,))], start, finish)


def _swap_with_sibling(arrs):
    n = len(arrs)

    def copies(ins, outs, sems):
        x, y, c = _place()
        return [pltpu.make_async_remote_copy(src_ref=ins[i], dst_ref=outs[i], send_sem=sems[0].at[i],
                                             recv_sem=sems[1].at[i], device_id=(x, y, 1 - c),
                                             device_id_type=MESH) for i in range(n)]

    def start(ins, outs, sems):
        for cp in copies(ins, outs, sems):
            cp.start()

    def finish(ins, outs, sems):
        for cp in copies(ins, outs, sems):
            cp.wait()

    return _Exchange(arrs, [SDS(a.shape, a.dtype) for a in arrs],
                     [pltpu.SemaphoreType.DMA((n,)), pltpu.SemaphoreType.DMA((n,))], start, finish)


def _gather_small(arrs):
    n = len(arrs)

    def copy(sems, outs, i, k, block, to, src=None):
        px, py, pc = block
        dst = outs[i].at[4 * px + 2 * py + pc]
        return pltpu.make_async_remote_copy(
            src_ref=dst if src is None else src, dst_ref=dst, send_sem=sems[0].at[7 * i + k],
            recv_sem=sems[1].at[7 * i + k], device_id=to, device_id_type=MESH)

    def own_copies(ins, outs, sems):
        x, y, c = _place()
        cps = []
        for i in range(n):
            cps.append(copy(sems, outs, i, 0, (x, y, c), (x, y, 1 - c), src=ins[i]))
            for j, (cx, cy) in enumerate(_other_chips(x, y)):
                cps.append(copy(sems, outs, i, 1 + j, (x, y, c), (cx, cy, c), src=ins[i]))
        local = [pltpu.make_async_copy(ins[i], outs[i].at[4 * x + 2 * y + c], sems[2].at[i]) for i in range(n)]
        return cps, local

    def start(ins, outs, sems):
        cps, local = own_copies(ins, outs, sems)
        for cp in local + cps:
            cp.start()

    def finish(ins, outs, sems):
        x, y, c = _place()
        passed = []
        for j, (cx, cy) in enumerate(_other_chips(x, y)):
            for i in range(n):
                copy(sems, outs, i, 1 + j, (cx, cy, c), (x, y, c)).wait_recv()
                cp = copy(sems, outs, i, 4 + j, (cx, cy, c), (x, y, 1 - c))
                cp.start()
                passed.append(cp)
        for i in range(n):
            copy(sems, outs, i, 0, (x, y, 1 - c), (x, y, c)).wait_recv()
            for j, (cx, cy) in enumerate(_other_chips(x, y)):
                copy(sems, outs, i, 4 + j, (cx, cy, 1 - c), (x, y, c)).wait_recv()
        cps, local = own_copies(ins, outs, sems)
        for cp in cps + passed:
            cp.wait_send()
        for cp in local:
            cp.wait()

    return _Exchange(arrs, [SDS((N_DEV,) + a.shape, F32) for a in arrs],
                     [pltpu.SemaphoreType.DMA((7 * n,)), pltpu.SemaphoreType.DMA((7 * n,)),
                      pltpu.SemaphoreType.DMA((n,))], start, finish)


def _local_step(x, p, tgt, wb, ws, shards=None, opt=None):
    bsz, seq, _ = x.shape
    t = bsz * seq
    tm = min(256, t)
    tb = min(S5_TIME_BLOCK, seq)
    x0 = x.reshape(t, D_MODEL)
    p0 = p.reshape(t, PLE_DIM)
    tg = tgt.reshape(t, D_MODEL)
    row = lambda v: v.reshape(1, -1)
    dist = shards is not None
    wb = dict(wb)
    recv, sums, other, gathered = {}, {}, {}, {}
    gb = {}
    gs = {}
    shape_of, axis_of = {}, {}
    chip = None
    if dist:
        shape_of = {k: tuple(shards[k].shape) for k in BIG}
        axis_of = dict(BIG_AXIS)
        for q in range(LAST_PIECES):
            shape_of[LAST_PIECE % q] = (D_MODEL // LAST_PIECES, shape_of["ffn1_w_in"][1])
            axis_of[LAST_PIECE % q] = 1
        xi, yi, ci = _place()
        chip = (2 * xi + yi).astype(jnp.int32).reshape(1)
        ids = jnp.stack([2 * xi + yi] + [2 * cx + cy for cx, cy in _other_chips(xi, yi)] + [ci]).astype(jnp.int32)
    halfbuf, pre = {}, {}

    def gather(names):
        return _gather_weights([shards[k] for k in names], [GATHER_AXIS[k] for k in names]) if dist else None

    def exchange(scat=(), swap=(), halves=(), scat2=(), swap2=(), extra=None, after=None):
        if not dist:
            return None, []
        after = order[0] if after is None else after
        parts, tags = [], []
        if scat:
            parts.append(_scatter_grads([gb[k][1] for k in scat], [shape_of[k] for k in scat],
                                        [axis_of[k] for k in scat]))
            tags.append((recv, scat))
        if swap:
            for k in swap:
                sums[k] = order[0] = _sum_blocks(gb[k][0], recv[k], shape_of[k], axis_of[k], chip, "sum_" + k,
                                                 order[0])
            parts.append(_swap_with_sibling([sums[k] for k in swap]))
            tags.append((other, swap))
        if halves:
            parts.append(_swap_halves([gb[k][1] for k in halves], [shape_of[k] for k in halves],
                                      [axis_of[k] for k in halves]))
            tags.append((halfbuf, halves))
        if scat2:
            for k in scat2:
                pre[k] = _presum(gb[k][0], halfbuf[k], shape_of[k], axis_of[k], ids, "presum_" + k, order[0])
                order[0] = pre[k][0]
            parts.append(_scatter_halves([pre[k][1] for k in scat2], [shape_of[k] for k in scat2]))
            tags.append((recv, scat2))
        if swap2:
            for k in swap2:
                sums[k] = order[0] = _sum_half(pre[k][0], recv[k], "sum_" + k, order[0])
            parts.append(_swap_with_sibling([sums[k] for k in swap2]))
            tags.append((other, swap2))
        if extra is not None:
            parts.append(extra[0])
            tags.append((extra[1], extra[2]))
        return (_join(parts), tags) if parts else (None, [])

    def take(ex_tags, got):
        ex, tags = ex_tags
        if ex is not None:
            for (dst, names), (o0, o1) in zip(tags, ex.cuts):
                dst.update(zip(names, got[o0:o1]))

    order = [None]

    def ordered(builder, *args, **kw):
        res = builder(*args, bg=order[0] if dist else None, **kw)
        order[0] = res[0][0]
        return res

    launched = []

    def launch(ex_tags):
        if ex_tags[0] is not None:
            n = len(launched)
            launched.append(n)
            take(ex_tags, _run_exchange_on_sequencer(ex_tags[0], "reduce_%d" % n, REDUCE_FIRST_ID + n))

    small_shape = {k: _small_view(k, v).shape for k, v in ws.items()}
    small_shape["loss_rows"] = (1, D_MODEL)
    ws = {k: v if (v.ndim == 2 and k != "ssm_log_dt") else v[0] for k, v in ws.items()}
    tril = jnp.tril(jnp.ones((CHUNK, CHUNK), dtype=bool))
    wsm = jnp.where(tril[None], ws["gmlp_w_s"], 0.0)
    wsm_b = wsm.astype(BF16)
    wsmt_b = wsm.transpose(0, 2, 1).astype(BF16)
    bias = jnp.repeat(ws["gmlp_b_s"].T, GMLP_HEAD_DIM, axis=1)

    tf = min(512, t)
    if dist:
        for gi, names in enumerate(GATHER_ORDER):
            wb.update(zip(names, _run_exchange_on_sequencer(gather(names), "gather_%d" % gi, GATHER_FIRST_ID + gi)))
    (x0b, h1, a1), _ = _ffn_proj(x0, wb["ffn1_w_in"], tf, "ffn1_proj")
    (x1, xh1, rstd1), _ = _ffn_out(x0, a1, wb["ffn1_w_out"], row(ws["ln1_g"]), row(ws["ln1_b"]), tf, "ffn1_out")
    sp = _s5_setup(ws["ssm_lambda_re"], ws["ssm_lambda_im"], ws["ssm_log_dt"], ws["ssm_b_re"],
                   ws["ssm_b_im"], ws["ssm_c_re"], ws["ssm_c_im"], ws["ssm_d"], wb["ssm_glu_w"],
                   ws["ssm_glu_b"], tb)
    (x1b, za, zuv, gab), _ = _mixin_fwd(x1, wb["mix_w_in"], tf)
    (s5o, s5ot, y2p, carries), _ = _s5_fwd(za, sp, bsz, seq, tb)
    (gm, gmt), _ = _gmlp_fwd(zuv, row(ws["gmlp_ln_g"]), row(ws["gmlp_ln_b"]), wsm_b, bias)
    (x2, xh2, rstd2), _ = _mixout_fwd(x1, s5o, gm, gab, wb["up_a"], wb["up_b"], wb["mix_w_out"],
                                           row(ws["ln2_g"]), row(ws["ln2_b"]), tf)
    (x2b, h2, a2), _ = _ffn_proj(x2, wb["ffn2_w_in"], tf, "ffn2_proj")
    (xh3, rstd3, dx3, x3b, pb, dq, de, loss_rows), _ = _ffn_out_loss(
        x2, a2, wb["ffn2_w_out"], row(ws["ln3_g"]), row(ws["ln3_b"]), p0, tg, wb["ple_w_gate"], wb["ple_w_proj"], tf)
    order[0] = dx3
    gb["ple_w_gate"], _ = ordered(_tn_matmul, x3b, dq, "dw_ple_gate", 1024, 1024, a_t=True)
    gb["ple_w_proj"], _ = ordered(_tn_matmul, pb, de, "dw_ple_proj", 256, 1024, a_t=True)
    launch(exchange(scat=("ple_w_gate", "ple_w_proj")))
    (dx2, dh2, df2, gs["ln3_g"], gs["ln3_b"]), _ = ordered(
        _ffn_bwd, dx3, xh3, rstd3, h2, wb["ffn2_w_in"], wb["ffn2_w_out"], row(ws["ln3_g"]), tm, "ffn2_bwd")
    gb["ffn2_w_out"], _ = ordered(_tn_matmul, a2, df2, "dw_ffn2_out", 1408, 1024)
    launch(exchange(scat=("ffn2_w_out",)))
    gb["ffn2_w_in"], _ = ordered(_tn_matmul, x2b, dh2, "dw_ffn2_in", 1024, 1408, a_t=True)
    launch(exchange(scat=("ffn2_w_in",), swap=("ple_w_gate", "ple_w_proj")))
    (dx1a, dmx, mb, dya, dyb, ds5, dgm, dgab, gs["ln2_g"], gs["ln2_b"]), _ = ordered(
        _mixout_bwd, dx2, xh2, rstd2, s5o, gm, gab, wb["up_a"], wb["up_b"], wb["mix_w_out"], row(ws["ln2_g"]), tf)
    gb["mix_w_out"], _ = ordered(_tn_matmul, mb, dmx, "dw_mix_out", 1024, 1024, a_t=True)
    gb["up_a"], _ = ordered(_tn_matmul, s5ot, dya, "dw_up_a", 512, 1024, a_t=True)
    gb["up_b"], _ = ordered(_tn_matmul, gmt, dyb, "dw_up_b", 512, 1024, a_t=True)
    launch(exchange(scat=("mix_w_out", "up_a", "up_b"), swap=("ffn2_w_out",)))
    (dza, dmr, dmi, dnr, dni, da, ddsk, dgw, dgb), _ = ordered(_s5_bwd, za, y2p, ds5, carries, sp, bsz, seq, tb)
    gb["ssm_glu_w"] = (dgw, dgw.astype(BF16))
    launch(exchange(scat=("ssm_glu_w",), swap=("ffn2_w_in",)))
    (dzuv, dws, dbias, gs["gmlp_ln_g"], gs["gmlp_ln_b"]), _ = ordered(
        _gmlp_bwd, zuv, dgm, row(ws["gmlp_ln_g"]), row(ws["gmlp_ln_b"]), wsm_b, wsmt_b, bias)
    (dx1,), _ = ordered(_mixin_bwd, dx1a, dza, dzuv, dgab, wb["mix_w_in"], tf)
    g_mi, _ = ordered(_tn_matmul, x1b, dza, "dw_mix_in_a", 1024, 512, 0, 3584, a_t=True)
    g_mi, _ = ordered(_tn_matmul, x1b, dzuv, "dw_mix_in_uv", 1024, 512, 1, 3584, g_mi, a_t=True)
    gb["mix_w_in"], _ = ordered(_tn_matmul, x1b, dgab, "dw_mix_in_g", 1024, 512, 3, 3584, g_mi, a_t=True)
    launch(exchange(swap=("mix_w_out", "up_a", "up_b", "ssm_glu_w")))

    d_abr = da[0].sum(axis=0).reshape(SSM_GROUPS, SSM_STATE)
    d_abi = da[1].sum(axis=0).reshape(SSM_GROUPS, SSM_STATE)
    _, vjp = jax.vjp(_s5_discretise, ws["ssm_lambda_re"], ws["ssm_lambda_im"], ws["ssm_log_dt"],
                     ws["ssm_b_re"], ws["ssm_b_im"])
    (gs["ssm_lambda_re"], gs["ssm_lambda_im"], gs["ssm_log_dt"], gs["ssm_b_re"], gs["ssm_b_im"]) = vjp(
        (d_abr, d_abi, _block_diag_in_t(dmr), _block_diag_in_t(dmi)))
    gs["ssm_c_re"] = _block_diag_out_t(dnr)
    gs["ssm_c_im"] = _block_diag_out_t(dni)
    gs["ssm_d"] = ddsk
    gs["ssm_glu_b"] = dgb
    gs["gmlp_w_s"] = dws
    gs["gmlp_b_s"] = dbias.reshape(CHUNK, GMLP_HEADS, GMLP_HEAD_DIM).sum(axis=-1).T
    gs["loss_rows"] = loss_rows

    def small_gather(names):
        return (_gather_small([gs[k].reshape(small_shape[k]) for k in names]), gathered, names) if dist else None

    late = ("ln1_g", "ln1_b")
    launch(exchange(scat=("mix_w_in",), extra=small_gather(tuple(k for k in SMALL + ("loss_rows",) if k not in late))))
    (dx0, dh1, df1, gs["ln1_g"], gs["ln1_b"]), _ = ordered(
        _ffn_bwd, dx1, xh1, rstd1, h1, wb["ffn1_w_in"], wb["ffn1_w_out"], row(ws["ln1_g"]), tm, "ffn1_bwd")
    grad_x = dx0.reshape(bsz, seq, D_MODEL)
    if not dist:
        gb["ffn1_w_out"], _ = _tn_matmul(a1, df1, "dw_ffn1_out", 1408, 1024)
        gb["ffn1_w_in"], _ = _tn_matmul(x0b, dh1, "dw_ffn1_in", 1024, 1408, a_t=True)
        return (loss_rows, grad_x, gb, {k: gs[k].reshape(small_shape[k]) for k in SMALL}, sums, other, gathered,
                None, {})
    launch(exchange(extra=small_gather(late)))
    gb["ffn1_w_out"], _ = ordered(_tn_matmul, a1, df1, "dw_ffn1_out", 1408, 1024)
    last = ["ffn1_w_out"] + [LAST_PIECE % q for q in range(LAST_PIECES)]
    fillers = (("ffn2_w_in", "mix_w_in", "ple_w_gate"),
               ("ffn2_w_out", "mix_w_out", "up_a", "up_b", "ssm_glu_w", "ple_w_proj"))
    out = {}
    for i in range(1, len(last) + 3):
        stage = lambda d: tuple(last[i - d:i - d + 1]) if 0 <= i - d < len(last) else ()
        launch(exchange(halves=stage(1), scat2=stage(2), swap2=stage(3), swap=("mix_w_in",) if i == 2 else ()))
        if i < len(last):
            gb[last[i]], _ = ordered(_tn_matmul, x0b, dh1, "dw_" + last[i], D_MODEL // LAST_PIECES, 1408,
                                     a_cols=(i - 1, 1), a_t=True)
        elif i - len(last) < len(fillers):
            for k in fillers[i - len(last)]:
                w, m, v = opt[k]
                out[k] = _adam_big(w, sums[k], other[k], m, v, "adam_" + k, after=order[0])
                order[0] = out[k][1]
    return loss_rows, grad_x, gb, gs, sums, other, gathered, ids, out


def _adamw(w, g, m, v):
    m = ADAM_B1 * m + (1.0 - ADAM_B1) * g
    v = ADAM_B2 * v + (1.0 - ADAM_B2) * (g * g)
    m_hat = m / ADAM_C1
    v_hat = v / ADAM_C2
    delta = -ADAM_LR * (m_hat / (jnp.sqrt(v_hat) + ADAM_EPS) + ADAM_WD * w)
    return delta, m, v


def _pinned(after):
    return ([pl.BlockSpec(memory_space=pl.ANY)], [after]) if after is not None else ([], [])


def _sum_blocks(part, recv, shape, axis, chip, name, after=None):
    r, c = shape
    rb = r // ROW_STEPS

    def body(chip_ref, p_ref, r_ref, *rest):
        rest[-1][...] = (p_ref[...] + r_ref[0].astype(F32) + r_ref[1].astype(F32) + r_ref[2].astype(F32))

    if axis == 0:
        own = pl.BlockSpec((rb, c), lambda i, k: (k[0] * ROW_STEPS + i, 0))
    else:
        own = pl.BlockSpec((rb, c), lambda i, k: (i, k[0]))
    pin_specs, pin_args = _pinned(after)
    grid_spec = pltpu.PrefetchScalarGridSpec(
        num_scalar_prefetch=1, grid=(ROW_STEPS,),
        in_specs=[own, pl.BlockSpec((3, rb, c), lambda i, k: (0, i, 0))] + pin_specs,
        out_specs=pl.BlockSpec((rb, c), lambda i, k: (i, 0)))
    return pl.pallas_call(body, name=name, out_shape=SDS((r, c), F32), grid_spec=grid_spec,
                          compiler_params=_params(("parallel",)))(chip, part, recv, *pin_args)


def _presum(part, half, shape, axis, ids, name, after=None):
    r, c = shape
    rb = r // 2

    def body(ids_ref, p_ref, h_ref, *rest):
        of_ref, ob_ref = rest[-2:]
        s = p_ref[...] + h_ref[...].astype(F32)
        ob_ref[...] = s.astype(BF16)

        @pl.when(pl.program_id(1) == 0)
        def _():
            of_ref[...] = s

    if axis == 0:
        p_spec = pl.BlockSpec((rb, c), lambda i, t, ids: (ids[t] * 2 + ids[4] + i, 0))
        h_spec = pl.BlockSpec((None, rb, c), lambda i, t, ids: (ids[t], i, 0))
    else:
        p_spec = pl.BlockSpec((rb, c), lambda i, t, ids: (ids[4] + i, ids[t]))
        h_spec = pl.BlockSpec((rb, c), lambda i, t, ids: (i, ids[t]))
    pin_specs, pin_args = _pinned(after)
    grid_spec = pltpu.PrefetchScalarGridSpec(
        num_scalar_prefetch=1, grid=(1, 4), in_specs=[p_spec, h_spec] + pin_specs,
        out_specs=(pl.BlockSpec((rb, c), lambda i, t, ids: (i, 0)),
                   pl.BlockSpec((None, rb, c), lambda i, t, ids: (t, i, 0))))
    return pl.pallas_call(body, name=name, out_shape=(SDS((r // 2, c), F32), SDS((4, r // 2, c), BF16)),
                          grid_spec=grid_spec,
                          compiler_params=_params(("parallel", "arbitrary")))(ids, part, half, *pin_args)


def _sum_half(pre, recv, name, after=None):
    hr, c = pre.shape
    rb = hr

    def body(p_ref, r_ref, *rest):
        rest[-1][...] = (p_ref[...] + r_ref[0].astype(F32) + r_ref[1].astype(F32) + r_ref[2].astype(F32))

    spec = pl.BlockSpec((rb, c), lambda i: (i, 0))
    pin_specs, pin_args = _pinned(after)
    return pl.pallas_call(body, name=name, grid=(1,), out_shape=SDS((hr, c), F32),
                          in_specs=[spec, pl.BlockSpec((3, rb, c), lambda i: (0, i, 0))] + pin_specs,
                          out_specs=spec, compiler_params=_params(("parallel",)))(pre, recv, *pin_args)


def _adam_halves(w, mine, oth, m, v, ids, name, piece=0, prev=None):
    r, c = w.shape
    rb = mine.shape[0] // 2

    def body(ids_ref, w_ref, a_ref, b_ref, m_ref, v_ref, *rest):
        g_ref, d_ref, nm_ref, nv_ref = rest[-4:]
        g = jnp.where(pl.program_id(0) // 2 == ids_ref[4], a_ref[...], b_ref[...])
        g_ref[...] = g
        d_ref[...], nm_ref[...], nv_ref[...] = _adamw(w_ref[...], g, m_ref[...], v_ref[...])

    whole = pl.BlockSpec((rb, c), lambda i, ids: (i + 4 * piece, 0))
    part = pl.BlockSpec((rb, c), lambda i, ids: (i % 2, 0))
    in_specs = [whole, part, part, whole, whole]
    args = [w, mine, oth, m, v]
    aliases = {}
    if prev is not None:
        in_specs += [pl.BlockSpec(memory_space=pl.ANY)] * 4
        args += list(prev)
        aliases = {6: 0, 7: 1, 8: 2, 9: 3}
    grid_spec = pltpu.PrefetchScalarGridSpec(num_scalar_prefetch=1, grid=(4,), in_specs=in_specs,
                                             out_specs=(whole,) * 4)
    return pl.pallas_call(body, name=name, out_shape=tuple(SDS((r, c), F32) for _ in range(4)),
                          grid_spec=grid_spec, input_output_aliases=aliases,
                          compiler_params=_params(("parallel",)))(ids, *args)


def _adam_big(w, ga, gb, m, v, name, piece=0, prev=None, after=None):
    r, c = w.shape
    pr = ga.shape[0]
    steps = ROW_STEPS if pr == r else 2
    rb = pr // steps
    off = piece * steps

    def body(w_ref, ga_ref, gb_ref, m_ref, v_ref, *rest):
        g_ref, d_ref, nm_ref, nv_ref = rest[-4:]
        g = ga_ref[...] + gb_ref[...]
        g_ref[...] = g
        d_ref[...], nm_ref[...], nv_ref[...] = _adamw(w_ref[...], g, m_ref[...], v_ref[...])

    whole = pl.BlockSpec((rb, c), lambda i: (i + off, 0))
    part = pl.BlockSpec((rb, c), lambda i: (i, 0))
    in_specs = [whole, part, part, whole, whole]
    args = [w, ga, gb, m, v]
    aliases = {}
    if prev is not None:
        in_specs += [pl.BlockSpec(memory_space=pl.ANY)] * 4
        args += list(prev)
        aliases = {5: 0, 6: 1, 7: 2, 8: 3}
    if after is not None:
        in_specs.append(pl.BlockSpec(memory_space=pl.ANY))
        args.append(after)
    return pl.pallas_call(
        body, name=name, grid=(steps,), out_shape=tuple(SDS((r, c), F32) for _ in range(4)),
        in_specs=in_specs, out_specs=(whole,) * 4, input_output_aliases=aliases,
        compiler_params=_params(("parallel",)),
    )(*args)


def _adam_small(ws, gathered, ms, vs):
    n = len(ws)

    def body(*refs):
        w_refs, g_refs, m_refs, v_refs = refs[:n], refs[n:2 * n], refs[2 * n:3 * n], refs[3 * n:4 * n]
        outs = refs[4 * n:]
        for i in range(n):
            g = g_refs[i][0]
            for d in range(1, N_DEV):
                g = g + g_refs[i][d]
            delta, nm, nv = _adamw(w_refs[i][...], g, m_refs[i][...], v_refs[i][...])
            outs[i][...] = g
            outs[n + i][...] = delta
            outs[2 * n + i][...] = nm
            outs[3 * n + i][...] = nv

    vmem = pl.BlockSpec(memory_space=pltpu.VMEM)
    shapes = [w.shape for w in ws]
    return pl.pallas_call(
        body, name="adam_small", out_shape=tuple(SDS(s, F32) for s in shapes * 4),
        in_specs=[vmem] * (4 * n), out_specs=tuple([vmem] * (4 * n)),
        compiler_params=pltpu.CompilerParams(vmem_limit_bytes=VMEM_LIMIT_BYTES),
    )(*ws, *gathered, *ms, *vs)


def _sum_loss(gathered):
    def body(g_ref, o_ref):
        tot = g_ref[0]
        for d in range(1, N_DEV):
            tot = tot + g_ref[d]
        o_ref[...] = (0.5 / D_MODEL) * jnp.sum(tot, axis=1, keepdims=True)

    vmem = pl.BlockSpec(memory_space=pltpu.VMEM)
    return pl.pallas_call(body, name="sum_loss", out_shape=SDS((1, 1), F32), in_specs=[vmem],
                          out_specs=vmem)(gathered)


def kernel(x, p, ffn1_w_in, ffn1_w_out, ln1_g, ln1_b, mix_w_in, ssm_lambda_re, ssm_lambda_im, ssm_log_dt, ssm_b_re, ssm_b_im, ssm_c_re, ssm_c_im, ssm_d, ssm_glu_w, ssm_glu_b, gmlp_ln_g, gmlp_ln_b, gmlp_w_s, gmlp_b_s, up_a, up_b, mix_w_out, ln2_g, ln2_b, ffn2_w_in, ffn2_w_out, ln3_g, ln3_b, ple_w_proj, ple_w_gate, loss_target, m_ffn1_w_in, m_ffn1_w_out, m_ln1_g, m_ln1_b, m_mix_w_in, m_ssm_lambda_re, m_ssm_lambda_im, m_ssm_log_dt, m_ssm_b_re, m_ssm_b_im, m_ssm_c_re, m_ssm_c_im, m_ssm_d, m_ssm_glu_w, m_ssm_glu_b, m_gmlp_ln_g, m_gmlp_ln_b, m_gmlp_w_s, m_gmlp_b_s, m_up_a, m_up_b, m_mix_w_out, m_ln2_g, m_ln2_b, m_ffn2_w_in, m_ffn2_w_out, m_ln3_g, m_ln3_b, m_ple_w_proj, m_ple_w_gate, v_ffn1_w_in, v_ffn1_w_out, v_ln1_g, v_ln1_b, v_mix_w_in, v_ssm_lambda_re, v_ssm_lambda_im, v_ssm_log_dt, v_ssm_b_re, v_ssm_b_im, v_ssm_c_re, v_ssm_c_im, v_ssm_d, v_ssm_glu_w, v_ssm_glu_b, v_gmlp_ln_g, v_gmlp_ln_b, v_gmlp_w_s, v_gmlp_b_s, v_up_a, v_up_b, v_mix_w_out, v_ln2_g, v_ln2_b, v_ffn2_w_in, v_ffn2_w_out, v_ln3_g, v_ln3_b, v_ple_w_proj, v_ple_w_gate):
    given = dict(locals())
    order = ("ffn1_w_in", "ffn1_w_out", "ln1_g", "ln1_b", "mix_w_in", "ssm_lambda_re", "ssm_lambda_im",
             "ssm_log_dt", "ssm_b_re", "ssm_b_im", "ssm_c_re", "ssm_c_im", "ssm_d", "ssm_glu_w", "ssm_glu_b",
             "gmlp_ln_g", "gmlp_ln_b", "gmlp_w_s", "gmlp_b_s", "up_a", "up_b", "mix_w_out", "ln2_g", "ln2_b",
             "ffn2_w_in", "ffn2_w_out", "ln3_g", "ln3_b", "ple_w_proj", "ple_w_gate")
    assert set(order) == set(BIG + SMALL)

    shard = {k: given[k][0] for k in BIG}
    shard_b = {k: shard[k].astype(BF16) for k in BIG}
    opt = {k: (shard[k], given["m_" + k][0], given["v_" + k][0]) for k in BIG}
    loss_rows, grad_x, gb, gs, sums, other, gathered, ids, out = _local_step(
        x, given["p"][0], loss_target, {}, {k: given[k] for k in SMALL}, shard_b, opt)

    out = dict(out)
    for k in BIG:
        if k in out:
            continue
        moments = (given["m_" + k][0], given["v_" + k][0])
        if k == "ffn1_w_out":
            out[k] = _adam_halves(shard[k], sums[k], other[k], *moments, ids, "adam_" + k)
        elif k == "ffn1_w_in":
            for q in range(LAST_PIECES):
                kq = LAST_PIECE % q
                out[k] = _adam_halves(shard[k], sums[kq], other[kq], *moments, ids, "adam_" + kq, q, out.get(k))
        else:
            out[k] = _adam_big(shard[k], sums[k], other[k], *moments, "adam_" + k,
                               after=gb[LAST_PIECE % (LAST_PIECES - 1)][0])

    res = _adam_small([_small_view(k, given[k]) for k in SMALL], [gathered[k] for k in SMALL],
                      [_small_view(k, given["m_" + k]) for k in SMALL],
                      [_small_view(k, given["v_" + k]) for k in SMALL])
    ns = len(SMALL)
    for i, k in enumerate(SMALL):
        out[k] = tuple(res[j * ns + i].reshape(given[k].shape) for j in range(4))
    loss = _sum_loss(gathered["loss_rows"]).reshape(())

    lead = lambda k, j: out[k][j][None] if k in BIG else out[k][j]
    return (loss, grad_x, *[lead(k, 0) for k in order], *[lead(k, 1) for k in order],
            *[lead(k, 2) for k in order], *[lead(k, 3) for k in order])
```

```python
import math

import jax
import jax.numpy as jnp
from jax import lax
from jax.experimental import pallas as pl
from jax.experimental.pallas import tpu as pltpu
from jax.experimental.pallas import tpu_sc as plsc

F32 = jnp.float32
BF16 = jnp.bfloat16
MESH = pl.DeviceIdType.MESH
SDS = jax.ShapeDtypeStruct

D_MODEL = 1024
D_FF = 2816
D_SSM = 512
D_GMLP = 512
SSM_GROUPS = 32
SSM_GROUP_CH = 16
SSM_STATE = 64
SSM_LANES = SSM_GROUPS * SSM_STATE
GMLP_HEADS = 8
GMLP_HEAD_DIM = 64
CHUNK = 128
PLE_DIM = 256
LN_EPS = 1e-5
ALPHA = 2.0 ** 0.25

ADAM_LR = 0.001
ADAM_B1 = 0.9
ADAM_B2 = 0.999
ADAM_EPS = 1e-08
ADAM_WD = 0.01
ADAM_STEP = 10
ADAM_C1 = 1.0 - ADAM_B1 ** ADAM_STEP
ADAM_C2 = 1.0 - ADAM_B2 ** ADAM_STEP

N_DEV = 8
VMEM_LIMIT_BYTES = 56 * 1024 * 1024
FFN_COLS = 1408
S5_BLOCKS = 4
S5_BLOCK_IN = D_SSM // S5_BLOCKS
S5_BLOCK_ST = SSM_LANES // S5_BLOCKS
SCAN_LANES = 512
S5_TIME_BLOCK = 512
TN_K_BLOCK = 2048
TN_SMALL_BLOCK = 1024 * 1024
GMLP_CHUNKS = 8
ROW_STEPS = 4
LAST_PIECES = 2
LAST_PIECE = "ffn1_w_in_q%d"
_G0 = math.sqrt(2.0 / math.pi)
_G1 = 0.044715


def _dot(a, b):
    return jnp.dot(a, b, preferred_element_type=F32)


def _dot_nt(a, b):
    return lax.dot_general(a, b, (((1,), (1,)), ((), ())), preferred_element_type=F32)


def _dot_tn(a, b):
    return lax.dot_general(a, b, (((0,), (0,)), ((), ())), preferred_element_type=F32)


def _sigmoid(x):
    return 1.0 / (1.0 + jnp.exp(-x))


def _gelu(x):
    t = jnp.tanh(_G0 * (x + _G1 * x * x * x))
    return 0.5 * x * (1.0 + t)


def _gelu_grad(x):
    t = jnp.tanh(_G0 * (x + _G1 * x * x * x))
    return 0.5 * (1.0 + t) + 0.5 * x * (1.0 - t * t) * _G0 * (1.0 + 3.0 * _G1 * x * x)


def _ln_fwd(r, g, b):
    mu = jnp.mean(r, axis=-1, keepdims=True)
    d = r - mu
    var = jnp.mean(d * d, axis=-1, keepdims=True)
    rstd = lax.rsqrt(var + LN_EPS)
    xh = d * rstd
    return xh * g + b, xh, rstd


def _ln_bwd(dy, xh, rstd, g):
    dxh = dy * g
    m1 = jnp.mean(dxh, axis=-1, keepdims=True)
    m2 = jnp.mean(dxh * xh, axis=-1, keepdims=True)
    return rstd * (dxh - m1 - xh * m2)


def _resident(shape):
    nd = len(shape)
    return pl.BlockSpec(shape, lambda *_: (0,) * nd, pipeline_mode=pl.Buffered(1))


def _fixed(shape):
    nd = len(shape)
    return pl.BlockSpec(shape, lambda *_: (0,) * nd)


def _rows(tm, cols):
    return pl.BlockSpec((tm, cols), lambda i: (i, 0))


def _cols(rows, tm):
    return pl.BlockSpec((rows, tm), lambda i: (0, i))


def _params(sem):
    return pltpu.CompilerParams(dimension_semantics=sem, vmem_limit_bytes=VMEM_LIMIT_BYTES)


class _Exchange:
    def __init__(self, args, out_shape, sems, start, finish):
        self.args, self.out_shape, self.sems = list(args), list(out_shape), list(sems)
        self.start, self.finish = start, finish
        self.cuts = [(0, len(self.out_shape))]


def _call(body, name, grid, in_specs, out_specs, out_shape, args, scratch=(), sem=None, bg=None, aliases=None):
    aliases = {} if aliases is None else aliases
    in_specs, args = list(in_specs), list(args)
    fn = body
    if bg is not None:
        n_args = len(args)

        def fn(*refs):
            body(*refs[:n_args], *refs[n_args + 1:])

        in_specs.append(pl.BlockSpec(memory_space=pl.ANY))
        args.append(bg)
    res = pl.pallas_call(fn, name=name, grid=grid, out_shape=tuple(out_shape), in_specs=in_specs,
                         out_specs=tuple(out_specs), scratch_shapes=list(scratch),
                         input_output_aliases=aliases, compiler_params=_params(sem))(*args)
    return tuple(res), ()


def _run_exchange_on_sequencer(ex, name, collective_id):
    n_i, n_o = len(ex.args), len(ex.out_shape)

    def body(*refs):
        ins, outs, sems = refs[:n_i], refs[n_i:n_i + n_o], refs[n_i + n_o:]
        x, y, c = lax.axis_index("x"), lax.axis_index("y"), lax.axis_index("c")
        barrier = pltpu.get_barrier_semaphore()
        for peer in [(x, y, 1 - c), (1 - x, y, c), (x, 1 - y, c), (1 - x, 1 - y, c)]:
            pl.semaphore_signal(barrier, inc=1, device_id=peer, device_id_type=MESH)
        pl.semaphore_wait(barrier, 4)
        ex.start(ins, outs, sems)
        ex.finish(ins, outs, sems)

    return tuple(pl.kernel(body, out_type=tuple(ex.out_shape),
                           mesh=plsc.ScalarSubcoreMesh(axis_name="sequencer", num_cores=1),
                           scratch_types=list(ex.sems), name=name,
                           compiler_params=pltpu.CompilerParams(collective_id=collective_id))(*ex.args))


def _join(exchanges):
    cuts = []
    a = o = q = 0
    for e in exchanges:
        cuts.append((a, a + len(e.args), o, o + len(e.out_shape), q, q + len(e.sems)))
        a, o, q = cuts[-1][1], cuts[-1][3], cuts[-1][5]

    def start(ins, outs, sems):
        for e, (a0, a1, o0, o1, q0, q1) in zip(exchanges, cuts):
            e.start(ins[a0:a1], outs[o0:o1], sems[q0:q1])

    def finish(ins, outs, sems):
        for e, (a0, a1, o0, o1, q0, q1) in zip(exchanges, cuts):
            e.finish(ins[a0:a1], outs[o0:o1], sems[q0:q1])

    joined = _Exchange(sum((e.args for e in exchanges), []), sum((e.out_shape for e in exchanges), []),
                       sum((e.sems for e in exchanges), []), start, finish)
    joined.cuts = [(c[2], c[3]) for c in cuts]
    return joined


def _ffn_proj(x, w_in, tm, name, bg=None):
    t = x.shape[0]
    nch = D_FF // FFN_COLS

    def body(x_ref, win_ref, xbt_ref, h_ref, a_ref):
        xb = x_ref[...].astype(BF16)
        xbt_ref[...] = xb.T
        for k in range(nch):
            cg = slice(k * FFN_COLS, (k + 1) * FFN_COLS)
            cu = slice(D_FF + k * FFN_COLS, D_FF + (k + 1) * FFN_COLS)
            hg = _dot(xb, win_ref[k])
            hu = _dot(xb, win_ref[nch + k])
            h_ref[:, cg] = hg.astype(BF16)
            h_ref[:, cu] = hu.astype(BF16)
            a_ref[:, cg] = (hg * _sigmoid(hg) * hu).astype(BF16)

    return _call(
        body, name, (t // tm,),
        [_rows(tm, D_MODEL), _resident((2 * nch, D_MODEL, FFN_COLS))],
        (_cols(D_MODEL, tm), _rows(tm, 2 * D_FF), _rows(tm, D_FF)),
        (SDS((D_MODEL, t), BF16), SDS((t, 2 * D_FF), BF16), SDS((t, D_FF), BF16)),
        (x, w_in), sem=("parallel",), bg=bg)


def _ffn_out(x, a, w_out, g, b, tm, name, bg=None):
    t = x.shape[0]

    def body(x_ref, a_ref, wout_ref, g_ref, b_ref, xn_ref, xh_ref, rstd_ref):
        f = _dot(a_ref[...], wout_ref[...])
        y, xh, rstd = _ln_fwd(ALPHA * x_ref[...] + 0.5 * f, g_ref[...], b_ref[...])
        xn_ref[...] = y
        xh_ref[...] = xh
        rstd_ref[...] = rstd

    return _call(
        body, name, (t // tm,),
        [_rows(tm, D_MODEL), _rows(tm, D_FF), _resident((D_FF, D_MODEL)), _fixed((1, D_MODEL)), _fixed((1, D_MODEL))],
        (_rows(tm, D_MODEL), _rows(tm, D_MODEL), _rows(tm, 1)),
        (SDS((t, D_MODEL), F32), SDS((t, D_MODEL), F32), SDS((t, 1), F32)),
        (x, a, w_out, g, b), sem=("parallel",), bg=bg)


def _ffn_out_loss(x, a, w_out, g, b, p, tgt, wpg, wpp, tm):
    t = x.shape[0]

    def body(x_ref, a_ref, wout_ref, g_ref, b_ref, p_ref, t_ref, wpg_ref, wpp_ref,
             xh_ref, rstd_ref, dx_ref, xbt_ref, pbt_ref, dq_ref, de_ref, loss_ref):
        @pl.when(pl.program_id(0) == 0)
        def _():
            loss_ref[...] = jnp.zeros_like(loss_ref)

        f = _dot(a_ref[...], wout_ref[...])
        x3v, xh, rstd = _ln_fwd(ALPHA * x_ref[...] + 0.5 * f, g_ref[...], b_ref[...])
        xh_ref[...] = xh
        rstd_ref[...] = rstd
        xb = x3v.astype(BF16)
        pb = p_ref[...].astype(BF16)
        xbt_ref[...] = xb.T
        pbt_ref[...] = pb.T
        s = _sigmoid(_dot(xb, wpg_ref[...]))
        e = _dot(pb, wpp_ref[...])
        diff = x3v + s * e - t_ref[...]
        loss_ref[...] += jnp.sum(diff * diff, axis=0, keepdims=True)
        dout = diff * (1.0 / D_MODEL)
        de_ref[...] = (dout * s).astype(BF16)
        dq = (dout * e * s * (1.0 - s)).astype(BF16)
        dq_ref[...] = dq
        dx_ref[...] = dout + _dot_nt(dq, wpg_ref[...])

    return _call(
        body, "ffn2_out_loss", (t // tm,),
        [_rows(tm, D_MODEL), _rows(tm, D_FF), _resident((D_FF, D_MODEL)), _fixed((1, D_MODEL)), _fixed((1, D_MODEL)),
         _rows(tm, PLE_DIM), _rows(tm, D_MODEL), _resident((D_MODEL, D_MODEL)), _resident((PLE_DIM, D_MODEL))],
        (_rows(tm, D_MODEL), _rows(tm, 1), _rows(tm, D_MODEL), _cols(D_MODEL, tm), _cols(PLE_DIM, tm),
         _rows(tm, D_MODEL), _rows(tm, D_MODEL), _fixed((1, D_MODEL))),
        (SDS((t, D_MODEL), F32), SDS((t, 1), F32), SDS((t, D_MODEL), F32), SDS((D_MODEL, t), BF16),
         SDS((PLE_DIM, t), BF16), SDS((t, D_MODEL), BF16), SDS((t, D_MODEL), BF16), SDS((1, D_MODEL), F32)),
        (x, a, w_out, g, b, p, tgt, wpg, wpp), sem=("arbitrary",))


def _ffn_bwd(dxn, xh, rstd, h, w_in, w_out, g, tm, name, bg=None):
    t = dxn.shape[0]
    nch = D_FF // FFN_COLS

    def body(dxn_ref, xh_ref, rstd_ref, h_ref, win_ref, wout_ref, g_ref,
             dx_ref, dh_ref, df_ref, dg_ref, db_ref):
        @pl.when(pl.program_id(0) == 0)
        def _():
            dg_ref[...] = jnp.zeros_like(dg_ref)
            db_ref[...] = jnp.zeros_like(db_ref)

        dy = dxn_ref[...]
        xhv = xh_ref[...]
        dr = _ln_bwd(dy, xhv, rstd_ref[...], g_ref[...])
        dg_ref[...] += jnp.sum(dy * xhv, axis=0, keepdims=True)
        db_ref[...] += jnp.sum(dy, axis=0, keepdims=True)
        df = (0.5 * dr).astype(BF16)
        df_ref[...] = df
        dx = ALPHA * dr
        das = [_dot_nt(df, wout_ref[k * FFN_COLS:(k + 1) * FFN_COLS, :]) for k in range(nch)]
        for k in range(nch):
            cg = slice(k * FFN_COLS, (k + 1) * FFN_COLS)
            cu = slice(D_FF + k * FFN_COLS, D_FF + (k + 1) * FFN_COLS)
            hg = h_ref[:, cg].astype(F32)
            hu = h_ref[:, cu].astype(F32)
            sg = _sigmoid(hg)
            silu = hg * sg
            da = das[k]
            dhu = (da * silu).astype(BF16)
            dhg = (da * hu * (sg * (1.0 + hg * (1.0 - sg)))).astype(BF16)
            dh_ref[:, cg] = dhg
            dh_ref[:, cu] = dhu
            dx = dx + _dot_nt(dhg, win_ref[k]) + _dot_nt(dhu, win_ref[nch + k])
        dx_ref[...] = dx

    return _call(
        body, name, (t // tm,),
        [_rows(tm, D_MODEL), _rows(tm, D_MODEL), _rows(tm, 1), _rows(tm, 2 * D_FF),
         _resident((2 * nch, D_MODEL, FFN_COLS)), _resident((D_FF, D_MODEL)), _fixed((1, D_MODEL))],
        (_rows(tm, D_MODEL), _rows(tm, 2 * D_FF), _rows(tm, D_MODEL),
         _fixed((1, D_MODEL)), _fixed((1, D_MODEL))),
        (SDS((t, D_MODEL), F32), SDS((t, 2 * D_FF), BF16), SDS((t, D_MODEL), BF16),
         SDS((1, D_MODEL), F32), SDS((1, D_MODEL), F32)),
        (dxn, xh, rstd, h, w_in, w_out, g), sem=("arbitrary",), bg=bg)


def _tn_matmul(a, b, name, bm, bn, col_block=0, total_cols=None, prev=None, bg=None, a_cols=None, a_t=False):
    t, m = a.shape[::-1] if a_t else a.shape
    a_first = 0
    if a_cols is not None:
        a_first, m = a_cols[0], a_cols[1] * bm
    n = b.shape[1]
    total_cols = n if total_cols is None else total_cols
    whole = bm * bn <= TN_SMALL_BLOCK and (m // bm) * (n // bn) >= 2
    bk = min(2 * TN_K_BLOCK if whole else TN_K_BLOCK, t)
    nk = t // bk
    n_in = 2 if prev is None else 4

    def body(*refs):
        a_ref, b_ref = refs[0], refs[1]
        o_ref, ob_ref = refs[n_in], refs[n_in + 1]
        k = pl.program_id(2)

        @pl.when(k == 0)
        def _():
            o_ref[...] = jnp.zeros_like(o_ref)

        o_ref[...] += _dot(a_ref[...], b_ref[...]) if a_t else _dot_tn(a_ref[...], b_ref[...])

        @pl.when(k == nk - 1)
        def _():
            ob_ref[...] = o_ref[...].astype(BF16)

    a_spec = (pl.BlockSpec((bm, bk), lambda i, j, k: (i + a_first, k)) if a_t
              else pl.BlockSpec((bk, bm), lambda i, j, k: (k, i + a_first)))
    in_specs = [a_spec, pl.BlockSpec((bk, bn), lambda i, j, k: (k, j))]
    args = [a, b]
    aliases = {}
    if prev is not None:
        in_specs += [pl.BlockSpec(memory_space=pl.ANY), pl.BlockSpec(memory_space=pl.ANY)]
        args += list(prev)
        aliases = {2: 0, 3: 1}
        if any(bg is p for p in prev):
            bg = None
    out_spec = pl.BlockSpec((bm, bn), lambda i, j, k: (i, j + col_block))
    return _call(body, name, (m // bm, n // bn, nk), in_specs, (out_spec, out_spec),
                 (SDS((m, total_cols), F32), SDS((m, total_cols), BF16)), args,
                 sem=("parallel", "parallel", "arbitrary"), bg=bg, aliases=aliases)


def _mixin_fwd(x1, w, tm, bg=None):
    t = x1.shape[0]

    def body(x_ref, w_ref, xbt_ref, za_ref, zuv_ref, gab_ref):
        xb = x_ref[...].astype(BF16)
        xbt_ref[...] = xb.T
        za_ref[...] = _dot(xb, w_ref[:, 0:512]).astype(BF16)
        zuv_ref[...] = _dot(xb, w_ref[:, 512:1536]).astype(BF16)
        gab_ref[...] = _dot(xb, w_ref[:, 1536:3584]).astype(BF16)

    return _call(
        body, "mixin_fwd", (t // tm,),
        [_rows(tm, D_MODEL), _resident((D_MODEL, 3584))],
        (_cols(D_MODEL, tm), _rows(tm, 512), _rows(tm, 1024), _rows(tm, 2048)),
        (SDS((D_MODEL, t), BF16), SDS((t, 512), BF16), SDS((t, 1024), BF16), SDS((t, 2048), BF16)),
        (x1, w), sem=("parallel",), bg=bg)


def _mixin_bwd(dx1a, dza, dzuv, dgab, w, tm, bg=None):
    t = dx1a.shape[0]

    def body(d_ref, dza_ref, dzuv_ref, dgab_ref, w_ref, dx_ref):
        dx_ref[...] = (d_ref[...] + _dot_nt(dza_ref[...], w_ref[:, 0:512])
                       + _dot_nt(dzuv_ref[...], w_ref[:, 512:1536])
                       + _dot_nt(dgab_ref[...], w_ref[:, 1536:3584]))

    return _call(
        body, "mixin_bwd", (t // tm,),
        [_rows(tm, D_MODEL), _rows(tm, 512), _rows(tm, 1024), _rows(tm, 2048), _resident((D_MODEL, 3584))],
        (_rows(tm, D_MODEL),), (SDS((t, D_MODEL), F32),),
        (dx1a, dza, dzuv, dgab, w), sem=("parallel",), bg=bg)


def _unrolled(lo, hi, body, carry):
    for j in range(lo, hi):
        carry = body(j, carry)
    return carry


def _scan_fwd(hr_ref, hi_ref, a_ref, ap_ref, carry_ref, seg, cin_ref):
    for lc in range(SSM_LANES // SCAN_LANES):
        ls = slice(lc * SCAN_LANES, (lc + 1) * SCAN_LANES)
        a_r = jnp.broadcast_to(a_ref[0:1, ls], (8, SCAN_LANES))
        a_i = jnp.broadcast_to(a_ref[1:2, ls], (8, SCAN_LANES))

        def step(j, hc, ls=ls, a_r=a_r, a_i=a_i):
            h_r, h_i = hc
            rows = pl.ds(j * 8, 8)
            n_r = a_r * h_r - a_i * h_i + hr_ref[rows, ls]
            n_i = a_r * h_i + a_i * h_r + hi_ref[rows, ls]
            hr_ref[rows, ls] = n_r
            hi_ref[rows, ls] = n_i
            return n_r, n_i

        zero = jnp.zeros((8, SCAN_LANES), F32)
        f_r, f_i = _unrolled(0, seg, step, (zero, zero))
        c_r = carry_ref[0:1, ls]
        c_i = carry_ref[1:2, ls]
        p_r = ap_ref[0:1, ls]
        p_i = ap_ref[1:2, ls]
        rows_r, rows_i = [], []
        for s in range(8):
            rows_r.append(c_r)
            rows_i.append(c_i)
            c_r, c_i = (f_r[s:s + 1] + p_r * c_r - p_i * c_i,
                        f_i[s:s + 1] + p_r * c_i + p_i * c_r)
        carry_ref[0:1, ls] = c_r
        carry_ref[1:2, ls] = c_i
        cin_r = jnp.concatenate(rows_r, axis=0)
        cin_i = jnp.concatenate(rows_i, axis=0)
        if cin_ref is not None:
            cin_ref[0, :, ls] = cin_r
            cin_ref[1, :, ls] = cin_i

        def fix(j, cc, ls=ls, a_r=a_r, a_i=a_i):
            c_r, c_i = cc
            c_r, c_i = a_r * c_r - a_i * c_i, a_r * c_i + a_i * c_r
            rows = pl.ds(j * 8, 8)
            hr_ref[rows, ls] = hr_ref[rows, ls] + c_r
            hi_ref[rows, ls] = hi_ref[rows, ls] + c_i
            return c_r, c_i

        _unrolled(0, seg, fix, (cin_r, cin_i))


def _scan_bwd(gr_ref, gi_ref, hr_ref, hi_ref, cin_ref, a_ref, ap_ref, rcarry_ref, da_ref, seg):
    for lc in range(SSM_LANES // SCAN_LANES):
        ls = slice(lc * SCAN_LANES, (lc + 1) * SCAN_LANES)
        a_r = jnp.broadcast_to(a_ref[0:1, ls], (8, SCAN_LANES))
        a_i = jnp.broadcast_to(a_ref[1:2, ls], (8, SCAN_LANES))

        def step(t, gc, ls=ls, a_r=a_r, a_i=a_i):
            g_r, g_i = gc
            rows = pl.ds((seg - 1 - t) * 8, 8)
            n_r = gr_ref[rows, ls] + a_r * g_r + a_i * g_i
            n_i = gi_ref[rows, ls] + a_r * g_i - a_i * g_r
            gr_ref[rows, ls] = n_r
            gi_ref[rows, ls] = n_i
            return n_r, n_i

        zero = jnp.zeros((8, SCAN_LANES), F32)
        f_r, f_i = _unrolled(0, seg, step, (zero, zero))
        c_r = rcarry_ref[0:1, ls]
        c_i = rcarry_ref[1:2, ls]
        p_r = ap_ref[0:1, ls]
        p_i = ap_ref[1:2, ls]
        rows_r, rows_i = [None] * 8, [None] * 8
        for s in range(7, -1, -1):
            rows_r[s] = c_r
            rows_i[s] = c_i
            c_r, c_i = (f_r[s:s + 1] + p_r * c_r + p_i * c_i,
                        f_i[s:s + 1] + p_r * c_i - p_i * c_r)
        rcarry_ref[0:1, ls] = c_r
        rcarry_ref[1:2, ls] = c_i
        cin_r = jnp.concatenate(rows_r, axis=0)
        cin_i = jnp.concatenate(rows_i, axis=0)

        def fix_row(j_rows, hp_r, hp_i, cc, ls=ls, a_r=a_r, a_i=a_i):
            c_r, c_i, acc_r, acc_i = cc
            c_r, c_i = a_r * c_r + a_i * c_i, a_r * c_i - a_i * c_r
            g_r = gr_ref[j_rows, ls] + c_r
            g_i = gi_ref[j_rows, ls] + c_i
            gr_ref[j_rows, ls] = g_r
            gi_ref[j_rows, ls] = g_i
            acc_r = acc_r + g_r * hp_r + g_i * hp_i
            acc_i = acc_i + g_i * hp_r - g_r * hp_i
            return c_r, c_i, acc_r, acc_i

        def fix(t, cc, ls=ls, fix_row=fix_row):
            j = seg - 1 - t
            rows = pl.ds(j * 8, 8)
            prev = pl.ds((j - 1) * 8, 8)
            return fix_row(rows, hr_ref[prev, ls], hi_ref[prev, ls], cc)

        cc = _unrolled(0, seg - 1, fix, (cin_r, cin_i, zero, zero))
        _, _, acc_r, acc_i = fix_row(pl.ds(0, 8), cin_ref[0, :, ls], cin_ref[1, :, ls], cc)
        da_ref[0, :, ls] += acc_r
        da_ref[1, :, ls] += acc_i


def _s5_fwd(za, sp, bsz, seq, tb, bg=None):
    nb = seq // tb
    seg = tb // 8
    t = bsz * seq

    def body(za_ref, perm_ref, permt_ref, mre_ref, mim_ref, nre_ref, nim_ref, a_ref, ap_ref,
             dsk_ref, gw_ref, gb_ref, out_ref, outt_ref, y2_ref, car_ref, hr_ref, hi_ref, carry_ref):
        @pl.when(pl.program_id(1) == 0)
        def _():
            carry_ref[...] = jnp.zeros_like(carry_ref)

        car_ref[0] = carry_ref[...]
        up = _dot(perm_ref[...], za_ref[...])
        upb = up.astype(BF16)
        for bb in range(S5_BLOCKS):
            ub = upb[:, bb * S5_BLOCK_IN:(bb + 1) * S5_BLOCK_IN]
            st = slice(bb * S5_BLOCK_ST, (bb + 1) * S5_BLOCK_ST)
            hr_ref[:, st] = _dot(ub, mre_ref[bb])
            hi_ref[:, st] = _dot(ub, mim_ref[bb])
        _scan_fwd(hr_ref, hi_ref, a_ref, ap_ref, carry_ref, seg, None)
        ys = []
        for bb in range(S5_BLOCKS):
            st = slice(bb * S5_BLOCK_ST, (bb + 1) * S5_BLOCK_ST)
            ys.append(_dot(hr_ref[:, st].astype(BF16), nre_ref[bb])
                      - _dot(hi_ref[:, st].astype(BF16), nim_ref[bb]))
        y2 = jnp.concatenate(ys, axis=1) + dsk_ref[...] * up
        y2_ref[...] = y2
        y3 = _gelu(y2)
        gl = _dot(y3.astype(BF16), gw_ref[...]) + gb_ref[...]
        oa = y3 * _sigmoid(gl)
        out = _dot(permt_ref[...], oa.astype(BF16)).astype(BF16)
        out_ref[...] = out
        outt_ref[...] = out.T

    blk = pl.BlockSpec((tb, D_SSM), lambda b, j: (b * nb + j, 0))
    blk_t = pl.BlockSpec((D_SSM, tb), lambda b, j: (0, b * nb + j))
    m_shape = (S5_BLOCKS, S5_BLOCK_IN, S5_BLOCK_ST)
    n_shape = (S5_BLOCKS, S5_BLOCK_ST, S5_BLOCK_IN)
    return _call(
        body, "s5_fwd", (bsz, nb),
        [blk, _fixed((tb, tb)), _fixed((tb, tb)), _fixed(m_shape), _fixed(m_shape), _fixed(n_shape),
         _fixed(n_shape), _fixed((2, SSM_LANES)), _fixed((2, SSM_LANES)), _fixed((1, D_SSM)),
         _fixed((D_SSM, D_SSM)), _fixed((1, D_SSM))],
        (blk, blk_t, blk, pl.BlockSpec((1, 2, SSM_LANES), lambda b, j: (b * nb + j, 0, 0))),
        (SDS((t, D_SSM), BF16), SDS((D_SSM, t), BF16), SDS((t, D_SSM), F32), SDS((bsz * nb, 2, SSM_LANES), F32)),
        (za, sp["perm"], sp["permt"], sp["mre"], sp["mim"], sp["nre"], sp["nim"], sp["a"], sp["ap"],
         sp["dskip"], sp["glu_w"], sp["glu_b"]),
        scratch=[pltpu.VMEM((tb, SSM_LANES), F32), pltpu.VMEM((tb, SSM_LANES), F32),
                 pltpu.VMEM((2, SSM_LANES), F32)],
        sem=("arbitrary", "arbitrary"), bg=bg)


def _s5_bwd(za, y2p, doa, carries, sp, bsz, seq, tb, bg=None):
    nb = seq // tb
    seg = tb // 8
    t = bsz * seq

    def body(za_ref, y2_ref, doa_ref, car_ref, perm_ref, permt_ref, mre_ref, mim_ref, mtre_ref, mtim_ref,
             nre_ref, nim_ref, ntre_ref, ntim_ref, a_ref, ap_ref, dsk_ref, gw_ref, gwt_ref, gb_ref,
             dza_ref, dmr_ref, dmi_ref, dnr_ref, dni_ref, da_ref, ddsk_ref, dgw_ref, dgb_ref,
             hr_ref, hi_ref, gr_ref, gi_ref, cin_ref, carry_ref, rcarry_ref):
        first = jnp.logical_and(pl.program_id(0) == 0, pl.program_id(1) == 0)

        @pl.when(first)
        def _():
            for r in (dmr_ref, dmi_ref, dnr_ref, dni_ref, da_ref, ddsk_ref, dgw_ref, dgb_ref):
                r[...] = jnp.zeros_like(r)

        @pl.when(pl.program_id(1) == 0)
        def _():
            rcarry_ref[...] = jnp.zeros_like(rcarry_ref)

        carry_ref[...] = car_ref[0]
        perm = perm_ref[...]
        up = _dot(perm, za_ref[...])
        upb = up.astype(BF16)
        for bb in range(S5_BLOCKS):
            ub = upb[:, bb * S5_BLOCK_IN:(bb + 1) * S5_BLOCK_IN]
            st = slice(bb * S5_BLOCK_ST, (bb + 1) * S5_BLOCK_ST)
            hr_ref[:, st] = _dot(ub, mre_ref[bb])
            hi_ref[:, st] = _dot(ub, mim_ref[bb])
        _scan_fwd(hr_ref, hi_ref, a_ref, ap_ref, carry_ref, seg, cin_ref)

        y2 = y2_ref[...]
        y3 = _gelu(y2)
        y3b = y3.astype(BF16)
        sg = _sigmoid(_dot(y3b, gw_ref[...]) + gb_ref[...])
        d0 = doa_ref[...]
        d_hi = d0.astype(BF16)
        d1 = d0 - d_hi.astype(F32)
        d_mid = d1.astype(BF16)
        d_lo = (d1 - d_mid.astype(F32)).astype(BF16)
        doap = _dot(perm, d_hi) + _dot(perm, d_mid) + _dot(perm, d_lo)
        dgl = doap * y3 * sg * (1.0 - sg)
        dglb = dgl.astype(BF16)
        dy3 = doap * sg + _dot(dglb, gwt_ref[...])
        dgw_ref[...] += _dot_tn(y3b, dglb)
        dgb_ref[...] += jnp.sum(dgl, axis=0, keepdims=True)
        dy2 = dy3 * _gelu_grad(y2)
        ddsk_ref[...] += jnp.sum(dy2 * up, axis=0, keepdims=True)
        dyb = dy2.astype(BF16)
        for bb in range(S5_BLOCKS):
            dyc = dyb[:, bb * S5_BLOCK_IN:(bb + 1) * S5_BLOCK_IN]
            st = slice(bb * S5_BLOCK_ST, (bb + 1) * S5_BLOCK_ST)
            gr_ref[:, st] = _dot(dyc, ntre_ref[bb])
            gi_ref[:, st] = -_dot(dyc, ntim_ref[bb])
            dnr_ref[bb] += _dot_tn(hr_ref[:, st].astype(BF16), dyc)
            dni_ref[bb] += -_dot_tn(hi_ref[:, st].astype(BF16), dyc)
        _scan_bwd(gr_ref, gi_ref, hr_ref, hi_ref, cin_ref, a_ref, ap_ref, rcarry_ref, da_ref, seg)
        dus = []
        for bb in range(S5_BLOCKS):
            st = slice(bb * S5_BLOCK_ST, (bb + 1) * S5_BLOCK_ST)
            grb = gr_ref[:, st].astype(BF16)
            gib = gi_ref[:, st].astype(BF16)
            dus.append(_dot(grb, mtre_ref[bb]) + _dot(gib, mtim_ref[bb]))
            ub = upb[:, bb * S5_BLOCK_IN:(bb + 1) * S5_BLOCK_IN]
            dmr_ref[bb] += _dot_tn(ub, grb)
            dmi_ref[bb] += _dot_tn(ub, gib)
        du = jnp.concatenate(dus, axis=1) + dy2 * dsk_ref[...]
        dza_ref[...] = _dot(permt_ref[...], du.astype(BF16)).astype(BF16)

    def rev(b, j):
        return (b * nb + (nb - 1 - j), 0)

    blk = pl.BlockSpec((tb, D_SSM), rev)
    m_shape = (S5_BLOCKS, S5_BLOCK_IN, S5_BLOCK_ST)
    n_shape = (S5_BLOCKS, S5_BLOCK_ST, S5_BLOCK_IN)
    return _call(
        body, "s5_bwd", (bsz, nb),
        [blk, blk, blk, pl.BlockSpec((1, 2, SSM_LANES), lambda b, j: (b * nb + (nb - 1 - j), 0, 0)),
         _fixed((tb, tb)), _fixed((tb, tb)), _fixed(m_shape), _fixed(m_shape), _fixed(n_shape), _fixed(n_shape),
         _fixed(n_shape), _fixed(n_shape), _fixed(m_shape), _fixed(m_shape),
         _fixed((2, SSM_LANES)), _fixed((2, SSM_LANES)), _fixed((1, D_SSM)),
         _fixed((D_SSM, D_SSM)), _fixed((D_SSM, D_SSM)), _fixed((1, D_SSM))],
        (blk, _fixed(m_shape), _fixed(m_shape), _fixed(n_shape), _fixed(n_shape),
         _fixed((2, 8, SSM_LANES)), _fixed((1, D_SSM)), _fixed((D_SSM, D_SSM)), _fixed((1, D_SSM))),
        (SDS((t, D_SSM), BF16), SDS(m_shape, F32), SDS(m_shape, F32), SDS(n_shape, F32), SDS(n_shape, F32),
         SDS((2, 8, SSM_LANES), F32), SDS((1, D_SSM), F32), SDS((D_SSM, D_SSM), F32), SDS((1, D_SSM), F32)),
        (za, y2p, doa, carries, sp["perm"], sp["permt"], sp["mre"], sp["mim"], sp["mtre"], sp["mtim"],
         sp["nre"], sp["nim"], sp["ntre"], sp["ntim"], sp["a"], sp["ap"], sp["dskip"], sp["glu_w"],
         sp["glu_wt"], sp["glu_b"]),
        scratch=[pltpu.VMEM((tb, SSM_LANES), F32), pltpu.VMEM((tb, SSM_LANES), F32),
                 pltpu.VMEM((tb, SSM_LANES), F32), pltpu.VMEM((tb, SSM_LANES), F32),
                 pltpu.VMEM((2, 8, SSM_LANES), F32), pltpu.VMEM((2, SSM_LANES), F32),
                 pltpu.VMEM((2, SSM_LANES), F32)],
        sem=("arbitrary", "arbitrary"), bg=bg)


def _gmlp_spatial(ws_ref, vb):
    lane = lax.broadcasted_iota(jnp.int32, (CHUNK, 128), 1)
    parts = []
    for j in range(GMLP_HEADS // 2):
        vp = vb[:, 128 * j:128 * (j + 1)]
        parts.append(jnp.where(lane < GMLP_HEAD_DIM, _dot(ws_ref[2 * j], vp), _dot(ws_ref[2 * j + 1], vp)))
    return jnp.concatenate(parts, axis=1)


def _gmlp_fwd(zuv, ln_g, ln_b, wsm, bias, bg=None):
    t = zuv.shape[0]
    chunks = min(GMLP_CHUNKS, t // CHUNK)

    def body(z_ref, g_ref, b_ref, ws_ref, bias_ref, out_ref, outt_ref):
        for ch in range(chunks):
            rows = slice(ch * CHUNK, (ch + 1) * CHUNK)
            u = _gelu(z_ref[rows, 0:D_GMLP].astype(F32))
            v0 = _gelu(z_ref[rows, D_GMLP:2 * D_GMLP].astype(F32))
            v, _, _ = _ln_fwd(v0, g_ref[...], b_ref[...])
            s = _gmlp_spatial(ws_ref, v.astype(BF16)) + bias_ref[...]
            out = (u * s).astype(BF16)
            out_ref[rows, :] = out
            outt_ref[:, rows] = out.T

    step = chunks * CHUNK
    return _call(
        body, "gmlp_fwd", (t // step,),
        [_rows(step, 2 * D_GMLP), _fixed((1, D_GMLP)), _fixed((1, D_GMLP)),
         _fixed((GMLP_HEADS, CHUNK, CHUNK)), _fixed((CHUNK, D_GMLP))],
        (_rows(step, D_GMLP), _cols(D_GMLP, step)), (SDS((t, D_GMLP), BF16), SDS((D_GMLP, t), BF16)),
        (zuv, ln_g, ln_b, wsm, bias), sem=("parallel",), bg=bg)


def _gmlp_bwd(zuv, dgm, ln_g, ln_b, wsm, wsmt, bias, bg=None):
    t = zuv.shape[0]
    chunks = min(GMLP_CHUNKS, t // CHUNK)

    def body(z_ref, d_ref, g_ref, b_ref, ws_ref, wst_ref, bias_ref,
             dz_ref, dws_ref, dbias_ref, dg_ref, db_ref):
        @pl.when(pl.program_id(0) == 0)
        def _():
            for r in (dws_ref, dbias_ref, dg_ref, db_ref):
                r[...] = jnp.zeros_like(r)

        gam = g_ref[...]
        lane = lax.broadcasted_iota(jnp.int32, (CHUNK, 128), 1)
        tril = (lax.broadcasted_iota(jnp.int32, (CHUNK, CHUNK), 0)
                >= lax.broadcasted_iota(jnp.int32, (CHUNK, CHUNK), 1))
        zero_b = jnp.zeros((CHUNK, 128), BF16)
        for ch in range(chunks):
            rows = slice(ch * CHUNK, (ch + 1) * CHUNK)
            zu = z_ref[rows, 0:D_GMLP].astype(F32)
            zv = z_ref[rows, D_GMLP:2 * D_GMLP].astype(F32)
            u = _gelu(zu)
            v0 = _gelu(zv)
            v, vhat, rstd = _ln_fwd(v0, gam, b_ref[...])
            vb = v.astype(BF16)
            s = _gmlp_spatial(ws_ref, vb) + bias_ref[...]
            d = d_ref[rows, :]
            dz_ref[rows, 0:D_GMLP] = (d * s * _gelu_grad(zu)).astype(BF16)
            ds = d * u
            dbias_ref[...] += ds
            dsb = ds.astype(BF16)
            parts = []
            for j in range(GMLP_HEADS // 2):
                dsp = dsb[:, 128 * j:128 * (j + 1)]
                vp = vb[:, 128 * j:128 * (j + 1)]
                parts.append(jnp.where(lane < GMLP_HEAD_DIM, _dot(wst_ref[2 * j], dsp),
                                       _dot(wst_ref[2 * j + 1], dsp)))
                lo = jnp.where(lane < GMLP_HEAD_DIM, dsp, zero_b)
                hi = jnp.where(lane < GMLP_HEAD_DIM, zero_b, dsp)
                dws_ref[2 * j] += jnp.where(tril, _dot_nt(lo, vp), 0.0)
                dws_ref[2 * j + 1] += jnp.where(tril, _dot_nt(hi, vp), 0.0)
            dv = jnp.concatenate(parts, axis=1)
            dg_ref[...] += jnp.sum(dv * vhat, axis=0, keepdims=True)
            db_ref[...] += jnp.sum(dv, axis=0, keepdims=True)
            dz_ref[rows, D_GMLP:2 * D_GMLP] = (_ln_bwd(dv, vhat, rstd, gam) * _gelu_grad(zv)).astype(BF16)

    step = chunks * CHUNK
    return _call(
        body, "gmlp_bwd", (t // step,),
        [_rows(step, 2 * D_GMLP), _rows(step, D_GMLP), _fixed((1, D_GMLP)), _fixed((1, D_GMLP)),
         _fixed((GMLP_HEADS, CHUNK, CHUNK)), _fixed((GMLP_HEADS, CHUNK, CHUNK)), _fixed((CHUNK, D_GMLP))],
        (_rows(step, 2 * D_GMLP), _fixed((GMLP_HEADS, CHUNK, CHUNK)), _fixed((CHUNK, D_GMLP)),
         _fixed((1, D_GMLP)), _fixed((1, D_GMLP))),
        (SDS((t, 2 * D_GMLP), BF16), SDS((GMLP_HEADS, CHUNK, CHUNK), F32), SDS((CHUNK, D_GMLP), F32),
         SDS((1, D_GMLP), F32), SDS((1, D_GMLP), F32)),
        (zuv, dgm, ln_g, ln_b, wsm, wsmt, bias), sem=("arbitrary",), bg=bg)


def _mixout_fwd(x1, s5o, gm, gab, ua, ub, wmo, g, b, tm, bg=None):
    t = x1.shape[0]

    def body(x_ref, s_ref, m_ref, gab_ref, ua_ref, ub_ref, wmo_ref, g_ref, b_ref,
             xn_ref, xh_ref, rstd_ref):
        ya = _dot(s_ref[...], ua_ref[...])
        yb = _dot(m_ref[...], ub_ref[...])
        mix = (_sigmoid(gab_ref[:, 0:D_MODEL].astype(F32)) * ya
               + _sigmoid(gab_ref[:, D_MODEL:2 * D_MODEL].astype(F32)) * yb)
        r = ALPHA * x_ref[...] + _dot(mix.astype(BF16), wmo_ref[...])
        y, xh, rstd = _ln_fwd(r, g_ref[...], b_ref[...])
        xn_ref[...] = y
        xh_ref[...] = xh
        rstd_ref[...] = rstd

    return _call(
        body, "mixout_fwd", (t // tm,),
        [_rows(tm, D_MODEL), _rows(tm, D_SSM), _rows(tm, D_GMLP), _rows(tm, 2 * D_MODEL),
         _resident((D_SSM, D_MODEL)), _resident((D_GMLP, D_MODEL)), _resident((D_MODEL, D_MODEL)),
         _fixed((1, D_MODEL)), _fixed((1, D_MODEL))],
        (_rows(tm, D_MODEL), _rows(tm, D_MODEL), _rows(tm, 1)),
        (SDS((t, D_MODEL), F32), SDS((t, D_MODEL), F32), SDS((t, 1), F32)),
        (x1, s5o, gm, gab, ua, ub, wmo, g, b), sem=("parallel",), bg=bg)


def _mixout_bwd(dx2, xh, rstd, s5o, gm, gab, ua, ub, wmo, g, tm, bg=None):
    t = dx2.shape[0]

    def body(d_ref, xh_ref, rstd_ref, s_ref, m_ref, gab_ref, ua_ref, ub_ref, wmo_ref, g_ref,
             dx1_ref, dmx_ref, mb_ref, dya_ref, dyb_ref, ds5_ref, dgm_ref, dgab_ref, dg_ref, db_ref):
        @pl.when(pl.program_id(0) == 0)
        def _():
            dg_ref[...] = jnp.zeros_like(dg_ref)
            db_ref[...] = jnp.zeros_like(db_ref)

        dy = d_ref[...]
        xhv = xh_ref[...]
        dr = _ln_bwd(dy, xhv, rstd_ref[...], g_ref[...])
        dg_ref[...] += jnp.sum(dy * xhv, axis=0, keepdims=True)
        db_ref[...] += jnp.sum(dy, axis=0, keepdims=True)
        dx1_ref[...] = ALPHA * dr
        drb = dr.astype(BF16)
        dmx_ref[...] = drb
        dm = _dot_nt(drb, wmo_ref[...])
        ya = _dot(s_ref[...], ua_ref[...])
        yb = _dot(m_ref[...], ub_ref[...])
        sa = _sigmoid(gab_ref[:, 0:D_MODEL].astype(F32))
        sb = _sigmoid(gab_ref[:, D_MODEL:2 * D_MODEL].astype(F32))
        mb_ref[...] = (sa * ya + sb * yb).astype(BF16).T
        dya = (dm * sa).astype(BF16)
        dyb = (dm * sb).astype(BF16)
        dya_ref[...] = dya
        dyb_ref[...] = dyb
        dgab_ref[:, 0:D_MODEL] = (dm * ya * sa * (1.0 - sa)).astype(BF16)
        dgab_ref[:, D_MODEL:2 * D_MODEL] = (dm * yb * sb * (1.0 - sb)).astype(BF16)
        ds5_ref[...] = _dot_nt(dya, ua_ref[...])
        dgm_ref[...] = _dot_nt(dyb, ub_ref[...])

    return _call(
        body, "mixout_bwd", (t // tm,),
        [_rows(tm, D_MODEL), _rows(tm, D_MODEL), _rows(tm, 1), _rows(tm, D_SSM), _rows(tm, D_GMLP),
         _rows(tm, 2 * D_MODEL), _resident((D_SSM, D_MODEL)), _resident((D_GMLP, D_MODEL)),
         _resident((D_MODEL, D_MODEL)), _fixed((1, D_MODEL))],
        (_rows(tm, D_MODEL), _rows(tm, D_MODEL), _cols(D_MODEL, tm), _rows(tm, D_MODEL),
         _rows(tm, D_MODEL), _rows(tm, D_SSM), _rows(tm, D_GMLP), _rows(tm, 2 * D_MODEL),
         _fixed((1, D_MODEL)), _fixed((1, D_MODEL))),
        (SDS((t, D_MODEL), F32), SDS((t, D_MODEL), BF16), SDS((D_MODEL, t), BF16),
         SDS((t, D_MODEL), BF16), SDS((t, D_MODEL), BF16), SDS((t, D_SSM), F32),
         SDS((t, D_GMLP), F32), SDS((t, 2 * D_MODEL), BF16),
         SDS((1, D_MODEL), F32), SDS((1, D_MODEL), F32)),
        (dx2, xh, rstd, s5o, gm, gab, ua, ub, wmo, g), sem=("arbitrary",), bg=bg)


def _s5_discretise(lre, lim, log_dt, bre, bim):
    dt = jnp.exp(log_dt)[:, None]
    mag = jnp.exp(lre * dt)
    abr = mag * jnp.cos(lim * dt)
    abi = mag * jnp.sin(lim * dt)
    nr = abr - 1.0
    ni = abi
    den = lre * lre + lim * lim
    cr = ((nr * lre + ni * lim) / den)[..., None]
    ci = ((ni * lre - nr * lim) / den)[..., None]
    return abr, abi, cr * bre - ci * bim, cr * bim + ci * bre


def _block_diag_in(bb):
    v = bb.reshape(S5_BLOCKS, 8, SSM_STATE, SSM_GROUP_CH).transpose(0, 1, 3, 2)
    return jnp.einsum("bgip,gh->bgihp", v, jnp.eye(8, dtype=bb.dtype)).reshape(
        S5_BLOCKS, S5_BLOCK_IN, S5_BLOCK_ST)


def _block_diag_in_t(dm):
    v = dm.reshape(S5_BLOCKS, 8, SSM_GROUP_CH, 8, SSM_STATE)
    d = jnp.einsum("bgihp,gh->bgip", v, jnp.eye(8, dtype=dm.dtype))
    return d.transpose(0, 1, 3, 2).reshape(SSM_GROUPS, SSM_STATE, SSM_GROUP_CH)


def _block_diag_out(cc):
    v = cc.reshape(S5_BLOCKS, 8, SSM_GROUP_CH, SSM_STATE)
    return jnp.einsum("bgip,gh->bgphi", v, jnp.eye(8, dtype=cc.dtype)).reshape(
        S5_BLOCKS, S5_BLOCK_ST, S5_BLOCK_IN)


def _block_diag_out_t(dn):
    v = dn.reshape(S5_BLOCKS, 8, SSM_STATE, 8, SSM_GROUP_CH)
    d = jnp.einsum("bgphi,gh->bgip", v, jnp.eye(8, dtype=dn.dtype))
    return d.reshape(SSM_GROUPS, SSM_GROUP_CH, SSM_STATE)


def _s5_setup(lre, lim, log_dt, bre, bim, cre, cim, d_skip, glu_w, glu_b, tb):
    seg = tb // 8
    abr, abi, bbr, bbi = _s5_discretise(lre, lim, log_dt, bre, bim)
    pr, pi = abr, abi
    for _ in range(int(math.log2(seg))):
        pr, pi = pr * pr - pi * pi, 2.0 * pr * pi
    rows = jnp.arange(tb)
    src = (rows % 8) * seg + rows // 8
    perm = (src[:, None] == jnp.arange(tb)[None, :]).astype(BF16)
    mre = _block_diag_in(bbr)
    mim = _block_diag_in(bbi)
    nre = _block_diag_out(cre)
    nim = _block_diag_out(cim)
    return {
        "perm": perm, "permt": perm.T,
        "mre": mre.astype(BF16), "mim": mim.astype(BF16),
        "mtre": mre.transpose(0, 2, 1).astype(BF16), "mtim": mim.transpose(0, 2, 1).astype(BF16),
        "nre": nre.astype(BF16), "nim": nim.astype(BF16),
        "ntre": nre.transpose(0, 2, 1).astype(BF16), "ntim": nim.transpose(0, 2, 1).astype(BF16),
        "a": jnp.stack([abr.reshape(-1), abi.reshape(-1)]),
        "ap": jnp.stack([pr.reshape(-1), pi.reshape(-1)]),
        "dskip": d_skip.reshape(1, D_SSM), "glu_w": glu_w, "glu_wt": glu_w.T,
        "glu_b": glu_b.reshape(1, D_SSM),
    }


BIG = ("ffn1_w_in", "ffn1_w_out", "mix_w_in", "ssm_glu_w", "up_a", "up_b", "mix_w_out",
       "ffn2_w_in", "ffn2_w_out", "ple_w_proj", "ple_w_gate")
BIG_AXIS = {"ffn1_w_in": 1, "ffn1_w_out": 0, "mix_w_in": 1, "ssm_glu_w": 0, "up_a": 1, "up_b": 1,
            "mix_w_out": 0, "ffn2_w_in": 1, "ffn2_w_out": 0, "ple_w_proj": 1, "ple_w_gate": 0}
SHARD_MAJOR = 2
GATHER_AXIS = dict(BIG_AXIS, ffn1_w_in=SHARD_MAJOR, ffn2_w_in=SHARD_MAJOR)
GATHER_ORDER = (("ffn1_w_in",), ("ffn1_w_out",), ("mix_w_in",), ("ssm_glu_w", "up_a", "up_b", "mix_w_out"),
                ("ffn2_w_in",), ("ffn2_w_out", "ple_w_gate", "ple_w_proj"))
GATHER_FIRST_ID = 1
REDUCE_FIRST_ID = 7
SMALL = ("ln1_g", "ln1_b", "ssm_lambda_re", "ssm_lambda_im", "ssm_log_dt", "ssm_b_re", "ssm_b_im",
         "ssm_c_re", "ssm_c_im", "ssm_d", "ssm_glu_b", "gmlp_ln_g", "gmlp_ln_b", "gmlp_w_s",
         "gmlp_b_s", "ln2_g", "ln2_b", "ln3_g", "ln3_b")
SMALL_VIEW = {}


def _small_view(k, a):
    return a.reshape(SMALL_VIEW[k]) if k in SMALL_VIEW else a


def _place():
    return lax.axis_index("x"), lax.axis_index("y"), lax.axis_index("c")


def _other_chips(x, y):
    return [(1 - x, y), (x, 1 - y), (1 - x, 1 - y)]


def _window(ref, shard_shape, axis, chip, half):
    r, c = shard_shape
    hr = r // 2
    if axis == SHARD_MAJOR:
        return ref.at[chip] if half is None else ref.at[chip, pl.ds(half * hr, hr), :]
    if axis == 0:
        if half is None:
            return ref.at[pl.ds(chip * r, r), :]
        return ref.at[pl.ds(chip * r + half * hr, hr), :]
    if half is None:
        return ref.at[:, pl.ds(chip * c, c)]
    return ref.at[pl.ds(half * hr, hr), pl.ds(chip * c, c)]


def _gather_weights(shards, axes):
    n = len(shards)
    shapes = [s.shape for s in shards]
    full = [{0: (4 * r, c), 1: (r, 4 * c), SHARD_MAJOR: (4, r, c)}[ax] for (r, c), ax in zip(shapes, axes)]

    def remote(sems, i, k, src, dst, to):
        return pltpu.make_async_remote_copy(src_ref=src, dst_ref=dst, send_sem=sems[0].at[6 * i + k],
                                            recv_sem=sems[1].at[6 * i + k], device_id=to, device_id_type=MESH)

    def own_copies(ins, outs, sems):
        x, y, c = _place()
        me = 2 * x + y
        cps = []
        for i in range(n):
            hr = shapes[i][0] // 2
            mine = ins[i].at[pl.ds(c * hr, hr), :]
            for j, (cx, cy) in enumerate(_other_chips(x, y)):
                cps.append(remote(sems, i, j, mine, _window(outs[i], shapes[i], axes[i], me, c), (cx, cy, c)))
        local = [pltpu.make_async_copy(ins[i], _window(outs[i], shapes[i], axes[i], me, None), sems[2].at[i])
                 for i in range(n)]
        return cps, local

    def start(ins, outs, sems):
        cps, local = own_copies(ins, outs, sems)
        for cp in local + cps:
            cp.start()

    def finish(ins, outs, sems):
        x, y, c = _place()
        sibling = (x, y, 1 - c)
        passed = []
        for j, (cx, cy) in enumerate(_other_chips(x, y)):
            for i in range(n):
                w = _window(outs[i], shapes[i], axes[i], 2 * cx + cy, c)
                remote(sems, i, j, w, w, (cx, cy, c)).wait_recv()
                cp = remote(sems, i, 3 + j, w, w, sibling)
                cp.start()
                passed.append(cp)
        for j, (cx, cy) in enumerate(_other_chips(x, y)):
            for i in range(n):
                w = _window(outs[i], shapes[i], axes[i], 2 * cx + cy, 1 - c)
                remote(sems, i, 3 + j, w, w, sibling).wait_recv()
        cps, local = own_copies(ins, outs, sems)
        for cp in cps + passed:
            cp.wait_send()
        for cp in local:
            cp.wait()

    return _Exchange(shards, [SDS(f, BF16) for f in full],
                     [pltpu.SemaphoreType.DMA((6 * n,)), pltpu.SemaphoreType.DMA((6 * n,)),
                      pltpu.SemaphoreType.DMA((n,))], start, finish)


def _scatter_grads(parts, shapes, axes):
    n = len(parts)

    def copies(ins, outs, sems):
        x, y, c = _place()
        return [pltpu.make_async_remote_copy(
            src_ref=_window(ins[i], shapes[i], axes[i], 2 * cx + cy, None), dst_ref=outs[i].at[j],
            send_sem=sems[0].at[3 * i + j], recv_sem=sems[1].at[3 * i + j],
            device_id=(cx, cy, c), device_id_type=MESH)
            for i in range(n) for j, (cx, cy) in enumerate(_other_chips(x, y))]

    def start(ins, outs, sems):
        for cp in copies(ins, outs, sems):
            cp.start()

    def finish(ins, outs, sems):
        for cp in copies(ins, outs, sems):
            cp.wait()

    return _Exchange(parts, [SDS((3,) + tuple(s), BF16) for s in shapes],
                     [pltpu.SemaphoreType.DMA((3 * n,)), pltpu.SemaphoreType.DMA((3 * n,))], start, finish)


def _swap_halves(parts, shapes, axes):
    n = len(parts)

    def copies(ins, outs, sems):
        x, y, c = _place()
        cps = []
        for i in range(n):
            r, _ = shapes[i]
            hr = r // 2
            if axes[i] == 0:
                cps += [pltpu.make_async_remote_copy(
                    src_ref=ins[i].at[pl.ds(k * r + (1 - c) * hr, hr), :], dst_ref=outs[i].at[k],
                    send_sem=sems[0].at[i], recv_sem=sems[1].at[i], device_id=(x, y, 1 - c),
                    device_id_type=MESH) for k in range(4)]
            else:
                cps.append(pltpu.make_async_remote_copy(
                    src_ref=ins[i].at[pl.ds((1 - c) * hr, hr), :], dst_ref=outs[i],
                    send_sem=sems[0].at[i], recv_sem=sems[1].at[i], device_id=(x, y, 1 - c),
                    device_id_type=MESH))
        return cps

    def start(ins, outs, sems):
        for cp in copies(ins, outs, sems):
            cp.start()

    def finish(ins, outs, sems):
        x, y, c = _place()
        for i in range(n):
            pltpu.make_async_remote_copy(src_ref=outs[i], dst_ref=outs[i], send_sem=sems[0].at[i],
                                         recv_sem=sems[1].at[i], device_id=(x, y, 1 - c),
                                         device_id_type=MESH).wait()

    out = [SDS((4, r // 2, c), BF16) if ax == 0 else SDS((r // 2, 4 * c), BF16)
           for (r, c), ax in zip(shapes, axes)]
    return _Exchange(parts, out, [pltpu.SemaphoreType.DMA((n,)), pltpu.SemaphoreType.DMA((n,))], start, finish)


def _scatter_halves(pres, shapes):
    n = len(pres)

    def copies(ins, outs, sems):
        x, y, c = _place()
        return [pltpu.make_async_remote_copy(
            src_ref=ins[i].at[1 + j], dst_ref=outs[i].at[j], send_sem=sems[0].at[3 * i + j],
            recv_sem=sems[1].at[3 * i + j], device_id=(cx, cy, c), device_id_type=MESH)
            for i in range(n) for j, (cx, cy) in enumerate(_other_chips(x, y))]

    def start(ins, outs, sems):
        for cp in copies(ins, outs, sems):
            cp.start()

    def finish(ins, outs, sems):
        for cp in copies(ins, outs, sems):
            cp.wait()

    return _Exchange(pres, [SDS((3, r // 2, c), BF16) for r, c in shapes],
                     [pltpu.SemaphoreType.DMA((3 * n,)), pltpu.SemaphoreType.DMA((3 * n,))], start, finish)


def _swap_with_sibling(arrs):
    n = len(arrs)

    def copies(ins, outs, sems):
        x, y, c = _place()
        return [pltpu.make_async_remote_copy(src_ref=ins[i], dst_ref=outs[i], send_sem=sems[0].at[i],
                                             recv_sem=sems[1].at[i], device_id=(x, y, 1 - c),
                                             device_id_type=MESH) for i in range(n)]

    def start(ins, outs, sems):
        for cp in copies(ins, outs, sems):
            cp.start()

    def finish(ins, outs, sems):
        for cp in copies(ins, outs, sems):
            cp.wait()

    return _Exchange(arrs, [SDS(a.shape, a.dtype) for a in arrs],
                     [pltpu.SemaphoreType.DMA((n,)), pltpu.SemaphoreType.DMA((n,))], start, finish)


def _gather_small(arrs):
    n = len(arrs)

    def copy(sems, outs, i, k, block, to, src=None):
        px, py, pc = block
        dst = outs[i].at[4 * px + 2 * py + pc]
        return pltpu.make_async_remote_copy(
            src_ref=dst if src is None else src, dst_ref=dst, send_sem=sems[0].at[7 * i + k],
            recv_sem=sems[1].at[7 * i + k], device_id=to, device_id_type=MESH)

    def own_copies(ins, outs, sems):
        x, y, c = _place()
        cps = []
        for i in range(n):
            cps.append(copy(sems, outs, i, 0, (x, y, c), (x, y, 1 - c), src=ins[i]))
            for j, (cx, cy) in enumerate(_other_chips(x, y)):
                cps.append(copy(sems, outs, i, 1 + j, (x, y, c), (cx, cy, c), src=ins[i]))
        local = [pltpu.make_async_copy(ins[i], outs[i].at[4 * x + 2 * y + c], sems[2].at[i]) for i in range(n)]
        return cps, local

    def start(ins, outs, sems):
        cps, local = own_copies(ins, outs, sems)
        for cp in local + cps:
            cp.start()

    def finish(ins, outs, sems):
        x, y, c = _place()
        passed = []
        for j, (cx, cy) in enumerate(_other_chips(x, y)):
            for i in range(n):
                copy(sems, outs, i, 1 + j, (cx, cy, c), (x, y, c)).wait_recv()
                cp = copy(sems, outs, i, 4 + j, (cx, cy, c), (x, y, 1 - c))
                cp.start()
                passed.append(cp)
        for i in range(n):
            copy(sems, outs, i, 0, (x, y, 1 - c), (x, y, c)).wait_recv()
            for j, (cx, cy) in enumerate(_other_chips(x, y)):
                copy(sems, outs, i, 4 + j, (cx, cy, 1 - c), (x, y, c)).wait_recv()
        cps, local = own_copies(ins, outs, sems)
        for cp in cps + passed:
            cp.wait_send()
        for cp in local:
            cp.wait()

    return _Exchange(arrs, [SDS((N_DEV,) + a.shape, F32) for a in arrs],
                     [pltpu.SemaphoreType.DMA((7 * n,)), pltpu.SemaphoreType.DMA((7 * n,)),
                      pltpu.SemaphoreType.DMA((n,))], start, finish)


def _local_step(x, p, tgt, wb, ws, shards=None, opt=None):
    bsz, seq, _ = x.shape
    t = bsz * seq
    tm = min(256, t)
    tb = min(S5_TIME_BLOCK, seq)
    x0 = x.reshape(t, D_MODEL)
    p0 = p.reshape(t, PLE_DIM)
    tg = tgt.reshape(t, D_MODEL)
    row = lambda v: v.reshape(1, -1)
    dist = shards is not None
    wb = dict(wb)
    recv, sums, other, gathered = {}, {}, {}, {}
    gb = {}
    gs = {}
    shape_of, axis_of = {}, {}
    chip = None
    if dist:
        shape_of = {k: tuple(shards[k].shape) for k in BIG}
        axis_of = dict(BIG_AXIS)
        for q in range(LAST_PIECES):
            shape_of[LAST_PIECE % q] = (D_MODEL // LAST_PIECES, shape_of["ffn1_w_in"][1])
            axis_of[LAST_PIECE % q] = 1
        xi, yi, ci = _place()
        chip = (2 * xi + yi).astype(jnp.int32).reshape(1)
        ids = jnp.stack([2 * xi + yi] + [2 * cx + cy for cx, cy in _other_chips(xi, yi)] + [ci]).astype(jnp.int32)
    halfbuf, pre = {}, {}

    def gather(names):
        return _gather_weights([shards[k] for k in names], [GATHER_AXIS[k] for k in names]) if dist else None

    def exchange(scat=(), swap=(), halves=(), scat2=(), swap2=(), extra=None, after=None):
        if not dist:
            return None, []
        after = order[0] if after is None else after
        parts, tags = [], []
        if scat:
            parts.append(_scatter_grads([gb[k][1] for k in scat], [shape_of[k] for k in scat],
                                        [axis_of[k] for k in scat]))
            tags.append((recv, scat))
        if swap:
            for k in swap:
                sums[k] = order[0] = _sum_blocks(gb[k][0], recv[k], shape_of[k], axis_of[k], chip, "sum_" + k,
                                                 order[0])
            parts.append(_swap_with_sibling([sums[k] for k in swap]))
            tags.append((other, swap))
        if halves:
            parts.append(_swap_halves([gb[k][1] for k in halves], [shape_of[k] for k in halves],
                                      [axis_of[k] for k in halves]))
            tags.append((halfbuf, halves))
        if scat2:
            for k in scat2:
                pre[k] = _presum(gb[k][0], halfbuf[k], shape_of[k], axis_of[k], ids, "presum_" + k, order[0])
                order[0] = pre[k][0]
            parts.append(_scatter_halves([pre[k][1] for k in scat2], [shape_of[k] for k in scat2]))
            tags.append((recv, scat2))
        if swap2:
            for k in swap2:
                sums[k] = order[0] = _sum_half(pre[k][0], recv[k], "sum_" + k, order[0])
            parts.append(_swap_with_sibling([sums[k] for k in swap2]))
            tags.append((other, swap2))
        if extra is not None:
            parts.append(extra[0])
            tags.append((extra[1], extra[2]))
        return (_join(parts), tags) if parts else (None, [])

    def take(ex_tags, got):
        ex, tags = ex_tags
        if ex is not None:
            for (dst, names), (o0, o1) in zip(tags, ex.cuts):
                dst.update(zip(names, got[o0:o1]))

    order = [None]

    def ordered(builder, *args, **kw):
        res = builder(*args, bg=order[0] if dist else None, **kw)
        order[0] = res[0][0]
        return res

    launched = []

    def launch(ex_tags):
        if ex_tags[0] is not None:
            n = len(launched)
            launched.append(n)
            take(ex_tags, _run_exchange_on_sequencer(ex_tags[0], "reduce_%d" % n, REDUCE_FIRST_ID + n))

    small_shape = {k: _small_view(k, v).shape for k, v in ws.items()}
    small_shape["loss_rows"] = (1, D_MODEL)
    ws = {k: v if (v.ndim == 2 and k != "ssm_log_dt") else v[0] for k, v in ws.items()}
    tril = jnp.tril(jnp.ones((CHUNK, CHUNK), dtype=bool))
    wsm = jnp.where(tril[None], ws["gmlp_w_s"], 0.0)
    wsm_b = wsm.astype(BF16)
    wsmt_b = wsm.transpose(0, 2, 1).astype(BF16)
    bias = jnp.repeat(ws["gmlp_b_s"].T, GMLP_HEAD_DIM, axis=1)

    tf = min(512, t)
    if dist:
        for gi, names in enumerate(GATHER_ORDER):
            wb.update(zip(names, _run_exchange_on_sequencer(gather(names), "gather_%d" % gi, GATHER_FIRST_ID + gi)))
    (x0b, h1, a1), _ = _ffn_proj(x0, wb["ffn1_w_in"], tf, "ffn1_proj")
    (x1, xh1, rstd1), _ = _ffn_out(x0, a1, wb["ffn1_w_out"], row(ws["ln1_g"]), row(ws["ln1_b"]), tf, "ffn1_out")
    sp = _s5_setup(ws["ssm_lambda_re"], ws["ssm_lambda_im"], ws["ssm_log_dt"], ws["ssm_b_re"],
                   ws["ssm_b_im"], ws["ssm_c_re"], ws["ssm_c_im"], ws["ssm_d"], wb["ssm_glu_w"],
                   ws["ssm_glu_b"], tb)
    (x1b, za, zuv, gab), _ = _mixin_fwd(x1, wb["mix_w_in"], tf)
    (s5o, s5ot, y2p, carries), _ = _s5_fwd(za, sp, bsz, seq, tb)
    (gm, gmt), _ = _gmlp_fwd(zuv, row(ws["gmlp_ln_g"]), row(ws["gmlp_ln_b"]), wsm_b, bias)
    (x2, xh2, rstd2), _ = _mixout_fwd(x1, s5o, gm, gab, wb["up_a"], wb["up_b"], wb["mix_w_out"],
                                           row(ws["ln2_g"]), row(ws["ln2_b"]), tf)
    (x2b, h2, a2), _ = _ffn_proj(x2, wb["ffn2_w_in"], tf, "ffn2_proj")
    (xh3, rstd3, dx3, x3b, pb, dq, de, loss_rows), _ = _ffn_out_loss(
        x2, a2, wb["ffn2_w_out"], row(ws["ln3_g"]), row(ws["ln3_b"]), p0, tg, wb["ple_w_gate"], wb["ple_w_proj"], tf)
    order[0] = dx3
    gb["ple_w_gate"], _ = ordered(_tn_matmul, x3b, dq, "dw_ple_gate", 1024, 1024, a_t=True)
    gb["ple_w_proj"], _ = ordered(_tn_matmul, pb, de, "dw_ple_proj", 256, 1024, a_t=True)
    launch(exchange(scat=("ple_w_gate", "ple_w_proj")))
    (dx2, dh2, df2, gs["ln3_g"], gs["ln3_b"]), _ = ordered(
        _ffn_bwd, dx3, xh3, rstd3, h2, wb["ffn2_w_in"], wb["ffn2_w_out"], row(ws["ln3_g"]), tm, "ffn2_bwd")
    gb["ffn2_w_out"], _ = ordered(_tn_matmul, a2, df2, "dw_ffn2_out", 1408, 1024)
    launch(exchange(scat=("ffn2_w_out",)))
    gb["ffn2_w_in"], _ = ordered(_tn_matmul, x2b, dh2, "dw_ffn2_in", 1024, 1408, a_t=True)
    launch(exchange(scat=("ffn2_w_in",), swap=("ple_w_gate", "ple_w_proj")))
    (dx1a, dmx, mb, dya, dyb, ds5, dgm, dgab, gs["ln2_g"], gs["ln2_b"]), _ = ordered(
        _mixout_bwd, dx2, xh2, rstd2, s5o, gm, gab, wb["up_a"], wb["up_b"], wb["mix_w_out"], row(ws["ln2_g"]), tf)
    gb["mix_w_out"], _ = ordered(_tn_matmul, mb, dmx, "dw_mix_out", 1024, 1024, a_t=True)
    gb["up_a"], _ = ordered(_tn_matmul, s5ot, dya, "dw_up_a", 512, 1024, a_t=True)
    gb["up_b"], _ = ordered(_tn_matmul, gmt, dyb, "dw_up_b", 512, 1024, a_t=True)
    launch(exchange(scat=("mix_w_out", "up_a", "up_b"), swap=("ffn2_w_out",)))
    (dza, dmr, dmi, dnr, dni, da, ddsk, dgw, dgb), _ = ordered(_s5_bwd, za, y2p, ds5, carries, sp, bsz, seq, tb)
    gb["ssm_glu_w"] = (dgw, dgw.astype(BF16))
    launch(exchange(scat=("ssm_glu_w",), swap=("ffn2_w_in",)))
    (dzuv, dws, dbias, gs["gmlp_ln_g"], gs["gmlp_ln_b"]), _ = ordered(
        _gmlp_bwd, zuv, dgm, row(ws["gmlp_ln_g"]), row(ws["gmlp_ln_b"]), wsm_b, wsmt_b, bias)
    (dx1,), _ = ordered(_mixin_bwd, dx1a, dza, dzuv, dgab, wb["mix_w_in"], tf)
    g_mi, _ = ordered(_tn_matmul, x1b, dza, "dw_mix_in_a", 1024, 512, 0, 3584, a_t=True)
    g_mi, _ = ordered(_tn_matmul, x1b, dzuv, "dw_mix_in_uv", 1024, 512, 1, 3584, g_mi, a_t=True)
    gb["mix_w_in"], _ = ordered(_tn_matmul, x1b, dgab, "dw_mix_in_g", 1024, 512, 3, 3584, g_mi, a_t=True)
    launch(exchange(swap=("mix_w_out", "up_a", "up_b", "ssm_glu_w")))

    d_abr = da[0].sum(axis=0).reshape(SSM_GROUPS, SSM_STATE)
    d_abi = da[1].sum(axis=0).reshape(SSM_GROUPS, SSM_STATE)
    _, vjp = jax.vjp(_s5_discretise, ws["ssm_lambda_re"], ws["ssm_lambda_im"], ws["ssm_log_dt"],
                     ws["ssm_b_re"], ws["ssm_b_im"])
    (gs["ssm_lambda_re"], gs["ssm_lambda_im"], gs["ssm_log_dt"], gs["ssm_b_re"], gs["ssm_b_im"]) = vjp(
        (d_abr, d_abi, _block_diag_in_t(dmr), _block_diag_in_t(dmi)))
    gs["ssm_c_re"] = _block_diag_out_t(dnr)
    gs["ssm_c_im"] = _block_diag_out_t(dni)
    gs["ssm_d"] = ddsk
    gs["ssm_glu_b"] = dgb
    gs["gmlp_w_s"] = dws
    gs["gmlp_b_s"] = dbias.reshape(CHUNK, GMLP_HEADS, GMLP_HEAD_DIM).sum(axis=-1).T
    gs["loss_rows"] = loss_rows

    def small_gather(names):
        return (_gather_small([gs[k].reshape(small_shape[k]) for k in names]), gathered, names) if dist else None

    late = ("ln1_g", "ln1_b")
    launch(exchange(scat=("mix_w_in",), extra=small_gather(tuple(k for k in SMALL + ("loss_rows",) if k not in late))))
    (dx0, dh1, df1, gs["ln1_g"], gs["ln1_b"]), _ = ordered(
        _ffn_bwd, dx1, xh1, rstd1, h1, wb["ffn1_w_in"], wb["ffn1_w_out"], row(ws["ln1_g"]), tm, "ffn1_bwd")
    grad_x = dx0.reshape(bsz, seq, D_MODEL)
    if not dist:
        gb["ffn1_w_out"], _ = _tn_matmul(a1, df1, "dw_ffn1_out", 1408, 1024)
        gb["ffn1_w_in"], _ = _tn_matmul(x0b, dh1, "dw_ffn1_in", 1024, 1408, a_t=True)
        return (loss_rows, grad_x, gb, {k: gs[k].reshape(small_shape[k]) for k in SMALL}, sums, other, gathered,
                None, {})
    launch(exchange(extra=small_gather(late)))
    gb["ffn1_w_out"], _ = ordered(_tn_matmul, a1, df1, "dw_ffn1_out", 1408, 1024)
    last = ["ffn1_w_out"] + [LAST_PIECE % q for q in range(LAST_PIECES)]
    fillers = (("ffn2_w_in", "mix_w_in", "ple_w_gate"),
               ("ffn2_w_out", "mix_w_out", "up_a", "up_b", "ssm_glu_w", "ple_w_proj"))
    out = {}
    for i in range(1, len(last) + 3):
        stage = lambda d: tuple(last[i - d:i - d + 1]) if 0 <= i - d < len(last) else ()
        launch(exchange(halves=stage(1), scat2=stage(2), swap2=stage(3), swap=("mix_w_in",) if i == 2 else ()))
        if i < len(last):
            gb[last[i]], _ = ordered(_tn_matmul, x0b, dh1, "dw_" + last[i], D_MODEL // LAST_PIECES, 1408,
                                     a_cols=(i - 1, 1), a_t=True)
        elif i - len(last) < len(fillers):
            for k in fillers[i - len(last)]:
                w, m, v = opt[k]
                out[k] = _adam_big(w, sums[k], other[k], m, v, "adam_" + k, after=order[0])
                order[0] = out[k][1]
    return loss_rows, grad_x, gb, gs, sums, other, gathered, ids, out


def _adamw(w, g, m, v):
    m = ADAM_B1 * m + (1.0 - ADAM_B1) * g
    v = ADAM_B2 * v + (1.0 - ADAM_B2) * (g * g)
    m_hat = m / ADAM_C1
    v_hat = v / ADAM_C2
    delta = -ADAM_LR * (m_hat / (jnp.sqrt(v_hat) + ADAM_EPS) + ADAM_WD * w)
    return delta, m, v


def _pinned(after):
    return ([pl.BlockSpec(memory_space=pl.ANY)], [after]) if after is not None else ([], [])


def _sum_blocks(part, recv, shape, axis, chip, name, after=None):
    r, c = shape
    rb = r // ROW_STEPS

    def body(chip_ref, p_ref, r_ref, *rest):
        rest[-1][...] = (p_ref[...] + r_ref[0].astype(F32) + r_ref[1].astype(F32) + r_ref[2].astype(F32))

    if axis == 0:
        own = pl.BlockSpec((rb, c), lambda i, k: (k[0] * ROW_STEPS + i, 0))
    else:
        own = pl.BlockSpec((rb, c), lambda i, k: (i, k[0]))
    pin_specs, pin_args = _pinned(after)
    grid_spec = pltpu.PrefetchScalarGridSpec(
        num_scalar_prefetch=1, grid=(ROW_STEPS,),
        in_specs=[own, pl.BlockSpec((3, rb, c), lambda i, k: (0, i, 0))] + pin_specs,
        out_specs=pl.BlockSpec((rb, c), lambda i, k: (i, 0)))
    return pl.pallas_call(body, name=name, out_shape=SDS((r, c), F32), grid_spec=grid_spec,
                          compiler_params=_params(("parallel",)))(chip, part, recv, *pin_args)


def _presum(part, half, shape, axis, ids, name, after=None):
    r, c = shape
    rb = r // 2

    def body(ids_ref, p_ref, h_ref, *rest):
        of_ref, ob_ref = rest[-2:]
        s = p_ref[...] + h_ref[...].astype(F32)
        ob_ref[...] = s.astype(BF16)

        @pl.when(pl.program_id(1) == 0)
        def _():
            of_ref[...] = s

    if axis == 0:
        p_spec = pl.BlockSpec((rb, c), lambda i, t, ids: (ids[t] * 2 + ids[4] + i, 0))
        h_spec = pl.BlockSpec((None, rb, c), lambda i, t, ids: (ids[t], i, 0))
    else:
        p_spec = pl.BlockSpec((rb, c), lambda i, t, ids: (ids[4] + i, ids[t]))
        h_spec = pl.BlockSpec((rb, c), lambda i, t, ids: (i, ids[t]))
    pin_specs, pin_args = _pinned(after)
    grid_spec = pltpu.PrefetchScalarGridSpec(
        num_scalar_prefetch=1, grid=(1, 4), in_specs=[p_spec, h_spec] + pin_specs,
        out_specs=(pl.BlockSpec((rb, c), lambda i, t, ids: (i, 0)),
                   pl.BlockSpec((None, rb, c), lambda i, t, ids: (t, i, 0))))
    return pl.pallas_call(body, name=name, out_shape=(SDS((r // 2, c), F32), SDS((4, r // 2, c), BF16)),
                          grid_spec=grid_spec,
                          compiler_params=_params(("parallel", "arbitrary")))(ids, part, half, *pin_args)


def _sum_half(pre, recv, name, after=None):
    hr, c = pre.shape
    rb = hr

    def body(p_ref, r_ref, *rest):
        rest[-1][...] = (p_ref[...] + r_ref[0].astype(F32) + r_ref[1].astype(F32) + r_ref[2].astype(F32))

    spec = pl.BlockSpec((rb, c), lambda i: (i, 0))
    pin_specs, pin_args = _pinned(after)
    return pl.pallas_call(body, name=name, grid=(1,), out_shape=SDS((hr, c), F32),
                          in_specs=[spec, pl.BlockSpec((3, rb, c), lambda i: (0, i, 0))] + pin_specs,
                          out_specs=spec, compiler_params=_params(("parallel",)))(pre, recv, *pin_args)


def _adam_halves(w, mine, oth, m, v, ids, name, piece=0, prev=None):
    r, c = w.shape
    rb = mine.shape[0] // 2

    def body(ids_ref, w_ref, a_ref, b_ref, m_ref, v_ref, *rest):
        g_ref, d_ref, nm_ref, nv_ref = rest[-4:]
        g = jnp.where(pl.program_id(0) // 2 == ids_ref[4], a_ref[...], b_ref[...])
        g_ref[...] = g
        d_ref[...], nm_ref[...], nv_ref[...] = _adamw(w_ref[...], g, m_ref[...], v_ref[...])

    whole = pl.BlockSpec((rb, c), lambda i, ids: (i + 4 * piece, 0))
    part = pl.BlockSpec((rb, c), lambda i, ids: (i % 2, 0))
    in_specs = [whole, part, part, whole, whole]
    args = [w, mine, oth, m, v]
    aliases = {}
    if prev is not None:
        in_specs += [pl.BlockSpec(memory_space=pl.ANY)] * 4
        args += list(prev)
        aliases = {6: 0, 7: 1, 8: 2, 9: 3}
    grid_spec = pltpu.PrefetchScalarGridSpec(num_scalar_prefetch=1, grid=(4,), in_specs=in_specs,
                                             out_specs=(whole,) * 4)
    return pl.pallas_call(body, name=name, out_shape=tuple(SDS((r, c), F32) for _ in range(4)),
                          grid_spec=grid_spec, input_output_aliases=aliases,
                          compiler_params=_params(("parallel",)))(ids, *args)


def _adam_big(w, ga, gb, m, v, name, piece=0, prev=None, after=None):
    r, c = w.shape
    pr = ga.shape[0]
    steps = ROW_STEPS if pr == r else 2
    rb = pr // steps
    off = piece * steps

    def body(w_ref, ga_ref, gb_ref, m_ref, v_ref, *rest):
        g_ref, d_ref, nm_ref, nv_ref = rest[-4:]
        g = ga_ref[...] + gb_ref[...]
        g_ref[...] = g
        d_ref[...], nm_ref[...], nv_ref[...] = _adamw(w_ref[...], g, m_ref[...], v_ref[...])

    whole = pl.BlockSpec((rb, c), lambda i: (i + off, 0))
    part = pl.BlockSpec((rb, c), lambda i: (i, 0))
    in_specs = [whole, part, part, whole, whole]
    args = [w, ga, gb, m, v]
    aliases = {}
    if prev is not None:
        in_specs += [pl.BlockSpec(memory_space=pl.ANY)] * 4
        args += list(prev)
        aliases = {5: 0, 6: 1, 7: 2, 8: 3}
    if after is not None:
        in_specs.append(pl.BlockSpec(memory_space=pl.ANY))
        args.append(after)
    return pl.pallas_call(
        body, name=name, grid=(steps,), out_shape=tuple(SDS((r, c), F32) for _ in range(4)),
        in_specs=in_specs, out_specs=(whole,) * 4, input_output_aliases=aliases,
        compiler_params=_params(("parallel",)),
    )(*args)


def _adam_small(ws, gathered, ms, vs):
    n = len(ws)

    def body(*refs):
        w_refs, g_refs, m_refs, v_refs = refs[:n], refs[n:2 * n], refs[2 * n:3 * n], refs[3 * n:4 * n]
        outs = refs[4 * n:]
        for i in range(n):
            g = g_refs[i][0]
            for d in range(1, N_DEV):
                g = g + g_refs[i][d]
            delta, nm, nv = _adamw(w_refs[i][...], g, m_refs[i][...], v_refs[i][...])
            outs[i][...] = g
            outs[n + i][...] = delta
            outs[2 * n + i][...] = nm
            outs[3 * n + i][...] = nv

    vmem = pl.BlockSpec(memory_space=pltpu.VMEM)
    shapes = [w.shape for w in ws]
    return pl.pallas_call(
        body, name="adam_small", out_shape=tuple(SDS(s, F32) for s in shapes * 4),
        in_specs=[vmem] * (4 * n), out_specs=tuple([vmem] * (4 * n)),
        compiler_params=pltpu.CompilerParams(vmem_limit_bytes=VMEM_LIMIT_BYTES),
    )(*ws, *gathered, *ms, *vs)


def _sum_loss(gathered):
    def body(g_ref, o_ref):
        tot = g_ref[0]
        for d in range(1, N_DEV):
            tot = tot + g_ref[d]
        o_ref[...] = (0.5 / D_MODEL) * jnp.sum(tot, axis=1, keepdims=True)

    vmem = pl.BlockSpec(memory_space=pltpu.VMEM)
    return pl.pallas_call(body, name="sum_loss", out_shape=SDS((1, 1), F32), in_specs=[vmem],
                          out_specs=vmem)(gathered)


def kernel(x, p, ffn1_w_in, ffn1_w_out, ln1_g, ln1_b, mix_w_in, ssm_lambda_re, ssm_lambda_im, ssm_log_dt, ssm_b_re, ssm_b_im, ssm_c_re, ssm_c_im, ssm_d, ssm_glu_w, ssm_glu_b, gmlp_ln_g, gmlp_ln_b, gmlp_w_s, gmlp_b_s, up_a, up_b, mix_w_out, ln2_g, ln2_b, ffn2_w_in, ffn2_w_out, ln3_g, ln3_b, ple_w_proj, ple_w_gate, loss_target, m_ffn1_w_in, m_ffn1_w_out, m_ln1_g, m_ln1_b, m_mix_w_in, m_ssm_lambda_re, m_ssm_lambda_im, m_ssm_log_dt, m_ssm_b_re, m_ssm_b_im, m_ssm_c_re, m_ssm_c_im, m_ssm_d, m_ssm_glu_w, m_ssm_glu_b, m_gmlp_ln_g, m_gmlp_ln_b, m_gmlp_w_s, m_gmlp_b_s, m_up_a, m_up_b, m_mix_w_out, m_ln2_g, m_ln2_b, m_ffn2_w_in, m_ffn2_w_out, m_ln3_g, m_ln3_b, m_ple_w_proj, m_ple_w_gate, v_ffn1_w_in, v_ffn1_w_out, v_ln1_g, v_ln1_b, v_mix_w_in, v_ssm_lambda_re, v_ssm_lambda_im, v_ssm_log_dt, v_ssm_b_re, v_ssm_b_im, v_ssm_c_re, v_ssm_c_im, v_ssm_d, v_ssm_glu_w, v_ssm_glu_b, v_gmlp_ln_g, v_gmlp_ln_b, v_gmlp_w_s, v_gmlp_b_s, v_up_a, v_up_b, v_mix_w_out, v_ln2_g, v_ln2_b, v_ffn2_w_in, v_ffn2_w_out, v_ln3_g, v_ln3_b, v_ple_w_proj, v_ple_w_gate):
    given = dict(locals())
    order = ("ffn1_w_in", "ffn1_w_out", "ln1_g", "ln1_b", "mix_w_in", "ssm_lambda_re", "ssm_lambda_im",
             "ssm_log_dt", "ssm_b_re", "ssm_b_im", "ssm_c_re", "ssm_c_im", "ssm_d", "ssm_glu_w", "ssm_glu_b",
             "gmlp_ln_g", "gmlp_ln_b", "gmlp_w_s", "gmlp_b_s", "up_a", "up_b", "mix_w_out", "ln2_g", "ln2_b",
             "ffn2_w_in", "ffn2_w_out", "ln3_g", "ln3_b", "ple_w_proj", "ple_w_gate")
    assert set(order) == set(BIG + SMALL)

    shard = {k: given[k][0] for k in BIG}
    shard_b = {k: shard[k].astype(BF16) for k in BIG}
    opt = {k: (shard[k], given["m_" + k][0], given["v_" + k][0]) for k in BIG}
    loss_rows, grad_x, gb, gs, sums, other, gathered, ids, out = _local_step(
        x, given["p"][0], loss_target, {}, {k: given[k] for k in SMALL}, shard_b, opt)

    out = dict(out)
    for k in BIG:
        if k in out:
            continue
        moments = (given["m_" + k][0], given["v_" + k][0])
        if k == "ffn1_w_out":
            out[k] = _adam_halves(shard[k], sums[k], other[k], *moments, ids, "adam_" + k)
        elif k == "ffn1_w_in":
            for q in range(LAST_PIECES):
                kq = LAST_PIECE % q
                out[k] = _adam_halves(shard[k], sums[kq], other[kq], *moments, ids, "adam_" + kq, q, out.get(k))
        else:
            out[k] = _adam_big(shard[k], sums[k], other[k], *moments, "adam_" + k,
                               after=gb[LAST_PIECE % (LAST_PIECES - 1)][0])

    res = _adam_small([_small_view(k, given[k]) for k in SMALL], [gathered[k] for k in SMALL],
                      [_small_view(k, given["m_" + k]) for k in SMALL],
                      [_small_view(k, given["v_" + k]) for k in SMALL])
    ns = len(SMALL)
    for i, k in enumerate(SMALL):
        out[k] = tuple(res[j * ns + i].reshape(given[k].shape) for j in range(4))
    loss = _sum_loss(gathered["loss_rows"]).reshape(())

    lead = lambda k, j: out[k][j][None] if k in BIG else out[k][j]
    return (loss, grad_x, *[lead(k, 0) for k in order], *[lead(k, 1) for k in order],
            *[lead(k, 2) for k in order], *[lead(k, 3) for k in order])
```

```python
import math

import jax
import jax.numpy as jnp
from jax import lax
from jax.experimental import pallas as pl
from jax.experimental.pallas import tpu as pltpu
from jax.experimental.pallas import tpu_sc as plsc

F32 = jnp.float32
BF16 = jnp.bfloat16
MESH = pl.DeviceIdType.MESH
SDS = jax.ShapeDtypeStruct

D_MODEL = 1024
D_FF = 2816
D_SSM = 512
D_GMLP = 512
SSM_GROUPS = 32
SSM_GROUP_CH = 16
SSM_STATE = 64
SSM_LANES = SSM_GROUPS * SSM_STATE
GMLP_HEADS = 8
GMLP_HEAD_DIM = 64
CHUNK = 128
PLE_DIM = 256
LN_EPS = 1e-5
ALPHA = 2.0 ** 0.25

ADAM_LR = 0.001
ADAM_B1 = 0.9
ADAM_B2 = 0.999
ADAM_EPS = 1e-08
ADAM_WD = 0.01
ADAM_STEP = 10
ADAM_C1 = 1.0 - ADAM_B1 ** ADAM_STEP
ADAM_C2 = 1.0 - ADAM_B2 ** ADAM_STEP

N_DEV = 8
VMEM_LIMIT_BYTES = 56 * 1024 * 1024
FFN_COLS = 1408
S5_BLOCKS = 4
S5_BLOCK_IN = D_SSM // S5_BLOCKS
S5_BLOCK_ST = SSM_LANES // S5_BLOCKS
SCAN_LANES = 512
S5_TIME_BLOCK = 512
TN_K_BLOCK = 2048
TN_SMALL_BLOCK = 1024 * 1024
GMLP_CHUNKS = 8
ROW_STEPS = 4
LAST_PIECES = 2
LAST_PIECE = "ffn1_w_in_q%d"
_G0 = math.sqrt(2.0 / math.pi)
_G1 = 0.044715


def _dot(a, b):
    return jnp.dot(a, b, preferred_element_type=F32)


def _dot_nt(a, b):
    return lax.dot_general(a, b, (((1,), (1,)), ((), ())), preferred_element_type=F32)


def _dot_tn(a, b):
    return lax.dot_general(a, b, (((0,), (0,)), ((), ())), preferred_element_type=F32)


def _sigmoid(x):
    return 1.0 / (1.0 + jnp.exp(-x))


def _gelu(x):
    t = jnp.tanh(_G0 * (x + _G1 * x * x * x))
    return 0.5 * x * (1.0 + t)


def _gelu_grad(x):
    t = jnp.tanh(_G0 * (x + _G1 * x * x * x))
    return 0.5 * (1.0 + t) + 0.5 * x * (1.0 - t * t) * _G0 * (1.0 + 3.0 * _G1 * x * x)


def _ln_fwd(r, g, b):
    mu = jnp.mean(r, axis=-1, keepdims=True)
    d = r - mu
    var = jnp.mean(d * d, axis=-1, keepdims=True)
    rstd = lax.rsqrt(var + LN_EPS)
    xh = d * rstd
    return xh * g + b, xh, rstd


def _ln_bwd(dy, xh, rstd, g):
    dxh = dy * g
    m1 = jnp.mean(dxh, axis=-1, keepdims=True)
    m2 = jnp.mean(dxh * xh, axis=-1, keepdims=True)
    return rstd * (dxh - m1 - xh * m2)


def _resident(shape):
    nd = len(shape)
    return pl.BlockSpec(shape, lambda *_: (0,) * nd, pipeline_mode=pl.Buffered(1))


def _fixed(shape):
    nd = len(shape)
    return pl.BlockSpec(shape, lambda *_: (0,) * nd)


def _rows(tm, cols):
    return pl.BlockSpec((tm, cols), lambda i: (i, 0))


def _cols(rows, tm):
    return pl.BlockSpec((rows, tm), lambda i: (0, i))


def _params(sem):
    return pltpu.CompilerParams(dimension_semantics=sem, vmem_limit_bytes=VMEM_LIMIT_BYTES)


class _Exchange:
    def __init__(self, args, out_shape, sems, start, finish):
        self.args, self.out_shape, self.sems = list(args), list(out_shape), list(sems)
        self.start, self.finish = start, finish
        self.cuts = [(0, len(self.out_shape))]


def _call(body, name, grid, in_specs, out_specs, out_shape, args, scratch=(), sem=None, bg=None, aliases=None):
    aliases = {} if aliases is None else aliases
    in_specs, args = list(in_specs), list(args)
    fn = body
    if bg is not None:
        n_args = len(args)

        def fn(*refs):
            body(*refs[:n_args], *refs[n_args + 1:])

        in_specs.append(pl.BlockSpec(memory_space=pl.ANY))
        args.append(bg)
    res = pl.pallas_call(fn, name=name, grid=grid, out_shape=tuple(out_shape), in_specs=in_specs,
                         out_specs=tuple(out_specs), scratch_shapes=list(scratch),
                         input_output_aliases=aliases, compiler_params=_params(sem))(*args)
    return tuple(res), ()


def _run_exchange_on_sequencer(ex, name, collective_id):
    n_i, n_o = len(ex.args), len(ex.out_shape)

    def body(*refs):
        ins, outs, sems = refs[:n_i], refs[n_i:n_i + n_o], refs[n_i + n_o:]
        x, y, c = lax.axis_index("x"), lax.axis_index("y"), lax.axis_index("c")
        barrier = pltpu.get_barrier_semaphore()
        for peer in [(x, y, 1 - c), (1 - x, y, c), (x, 1 - y, c), (1 - x, 1 - y, c)]:
            pl.semaphore_signal(barrier, inc=1, device_id=peer, device_id_type=MESH)
        pl.semaphore_wait(barrier, 4)
        ex.start(ins, outs, sems)
        ex.finish(ins, outs, sems)

    return tuple(pl.kernel(body, out_type=tuple(ex.out_shape),
                           mesh=plsc.ScalarSubcoreMesh(axis_name="sequencer", num_cores=1),
                           scratch_types=list(ex.sems), name=name,
                           compiler_params=pltpu.CompilerParams(collective_id=collective_id))(*ex.args))


def _join(exchanges):
    cuts = []
    a = o = q = 0
    for e in exchanges:
        cuts.append((a, a + len(e.args), o, o + len(e.out_shape), q, q + len(e.sems)))
        a, o, q = cuts[-1][1], cuts[-1][3], cuts[-1][5]

    def start(ins, outs, sems):
        for e, (a0, a1, o0, o1, q0, q1) in zip(exchanges, cuts):
            e.start(ins[a0:a1], outs[o0:o1], sems[q0:q1])

    def finish(ins, outs, sems):
        for e, (a0, a1, o0, o1, q0, q1) in zip(exchanges, cuts):
            e.finish(ins[a0:a1], outs[o0:o1], sems[q0:q1])

    joined = _Exchange(sum((e.args for e in exchanges), []), sum((e.out_shape for e in exchanges), []),
                       sum((e.sems for e in exchanges), []), start, finish)
    joined.cuts = [(c[2], c[3]) for c in cuts]
    return joined


def _ffn_proj(x, w_in, tm, name, bg=None):
    t = x.shape[0]
    nch = D_FF // FFN_COLS

    def body(x_ref, win_ref, xbt_ref, h_ref, a_ref):
        xb = x_ref[...].astype(BF16)
        xbt_ref[...] = xb.T
        for k in range(nch):
            cg = slice(k * FFN_COLS, (k + 1) * FFN_COLS)
            cu = slice(D_FF + k * FFN_COLS, D_FF + (k + 1) * FFN_COLS)
            hg = _dot(xb, win_ref[k])
            hu = _dot(xb, win_ref[nch + k])
            h_ref[:, cg] = hg.astype(BF16)
            h_ref[:, cu] = hu.astype(BF16)
            a_ref[:, cg] = (hg * _sigmoid(hg) * hu).astype(BF16)

    return _call(
        body, name, (t // tm,),
        [_rows(tm, D_MODEL), _resident((2 * nch, D_MODEL, FFN_COLS))],
        (_cols(D_MODEL, tm), _rows(tm, 2 * D_FF), _rows(tm, D_FF)),
        (SDS((D_MODEL, t), BF16), SDS((t, 2 * D_FF), BF16), SDS((t, D_FF), BF16)),
        (x, w_in), sem=("parallel",), bg=bg)


def _ffn_out(x, a, w_out, g, b, tm, name, bg=None):
    t = x.shape[0]

    def body(x_ref, a_ref, wout_ref, g_ref, b_ref, xn_ref, xh_ref, rstd_ref):
        f = _dot(a_ref[...], wout_ref[...])
        y, xh, rstd = _ln_fwd(ALPHA * x_ref[...] + 0.5 * f, g_ref[...], b_ref[...])
        xn_ref[...] = y
        xh_ref[...] = xh
        rstd_ref[...] = rstd

    return _call(
        body, name, (t // tm,),
        [_rows(tm, D_MODEL), _rows(tm, D_FF), _resident((D_FF, D_MODEL)), _fixed((1, D_MODEL)), _fixed((1, D_MODEL))],
        (_rows(tm, D_MODEL), _rows(tm, D_MODEL), _rows(tm, 1)),
        (SDS((t, D_MODEL), F32), SDS((t, D_MODEL), F32), SDS((t, 1), F32)),
        (x, a, w_out, g, b), sem=("parallel",), bg=bg)


def _ffn_out_loss(x, a, w_out, g, b, p, tgt, wpg, wpp, tm):
    t = x.shape[0]

    def body(x_ref, a_ref, wout_ref, g_ref, b_ref, p_ref, t_ref, wpg_ref, wpp_ref,
             xh_ref, rstd_ref, dx_ref, xbt_ref, pbt_ref, dq_ref, de_ref, loss_ref):
        @pl.when(pl.program_id(0) == 0)
        def _():
            loss_ref[...] = jnp.zeros_like(loss_ref)

        f = _dot(a_ref[...], wout_ref[...])
        x3v, xh, rstd = _ln_fwd(ALPHA * x_ref[...] + 0.5 * f, g_ref[...], b_ref[...])
        xh_ref[...] = xh
        rstd_ref[...] = rstd
        xb = x3v.astype(BF16)
        pb = p_ref[...].astype(BF16)
        xbt_ref[...] = xb.T
        pbt_ref[...] = pb.T
        s = _sigmoid(_dot(xb, wpg_ref[...]))
        e = _dot(pb, wpp_ref[...])
        diff = x3v + s * e - t_ref[...]
        loss_ref[...] += jnp.sum(diff * diff, axis=0, keepdims=True)
        dout = diff * (1.0 / D_MODEL)
        de_ref[...] = (dout * s).astype(BF16)
        dq = (dout * e * s * (1.0 - s)).astype(BF16)
        dq_ref[...] = dq
        dx_ref[...] = dout + _dot_nt(dq, wpg_ref[...])

    return _call(
        body, "ffn2_out_loss", (t // tm,),
        [_rows(tm, D_MODEL), _rows(tm, D_FF), _resident((D_FF, D_MODEL)), _fixed((1, D_MODEL)), _fixed((1, D_MODEL)),
         _rows(tm, PLE_DIM), _rows(tm, D_MODEL), _resident((D_MODEL, D_MODEL)), _resident((PLE_DIM, D_MODEL))],
        (_rows(tm, D_MODEL), _rows(tm, 1), _rows(tm, D_MODEL), _cols(D_MODEL, tm), _cols(PLE_DIM, tm),
         _rows(tm, D_MODEL), _rows(tm, D_MODEL), _fixed((1, D_MODEL))),
        (SDS((t, D_MODEL), F32), SDS((t, 1), F32), SDS((t, D_MODEL), F32), SDS((D_MODEL, t), BF16),
         SDS((PLE_DIM, t), BF16), SDS((t, D_MODEL), BF16), SDS((t, D_MODEL), BF16), SDS((1, D_MODEL), F32)),
        (x, a, w_out, g, b, p, tgt, wpg, wpp), sem=("arbitrary",))


def _ffn_bwd(dxn, xh, rstd, h, w_in, w_out, g, tm, name, bg=None):
    t = dxn.shape[0]
    nch = D_FF // FFN_COLS

    def body(dxn_ref, xh_ref, rstd_ref, h_ref, win_ref, wout_ref, g_ref,
             dx_ref, dh_ref, df_ref, dg_ref, db_ref):
        @pl.when(pl.program_id(0) == 0)
        def _():
            dg_ref[...] = jnp.zeros_like(dg_ref)
            db_ref[...] = jnp.zeros_like(db_ref)

        dy = dxn_ref[...]
        xhv = xh_ref[...]
        dr = _ln_bwd(dy, xhv, rstd_ref[...], g_ref[...])
        dg_ref[...] += jnp.sum(dy * xhv, axis=0, keepdims=True)
        db_ref[...] += jnp.sum(dy, axis=0, keepdims=True)
        df = (0.5 * dr).astype(BF16)
        df_ref[...] = df
        dx = ALPHA * dr
        das = [_dot_nt(df, wout_ref[k * FFN_COLS:(k + 1) * FFN_COLS, :]) for k in range(nch)]
        for k in range(nch):
            cg = slice(k * FFN_COLS, (k + 1) * FFN_COLS)
            cu = slice(D_FF + k * FFN_COLS, D_FF + (k + 1) * FFN_COLS)
            hg = h_ref[:, cg].astype(F32)
            hu = h_ref[:, cu].astype(F32)
            sg = _sigmoid(hg)
            silu = hg * sg
            da = das[k]
            dhu = (da * silu).astype(BF16)
            dhg = (da * hu * (sg * (1.0 + hg * (1.0 - sg)))).astype(BF16)
            dh_ref[:, cg] = dhg
            dh_ref[:, cu] = dhu
            dx = dx + _dot_nt(dhg, win_ref[k]) + _dot_nt(dhu, win_ref[nch + k])
        dx_ref[...] = dx

    return _call(
        body, name, (t // tm,),
        [_rows(tm, D_MODEL), _rows(tm, D_MODEL), _rows(tm, 1), _rows(tm, 2 * D_FF),
         _resident((2 * nch, D_MODEL, FFN_COLS)), _resident((D_FF, D_MODEL)), _fixed((1, D_MODEL))],
        (_rows(tm, D_MODEL), _rows(tm, 2 * D_FF), _rows(tm, D_MODEL),
         _fixed((1, D_MODEL)), _fixed((1, D_MODEL))),
        (SDS((t, D_MODEL), F32), SDS((t, 2 * D_FF), BF16), SDS((t, D_MODEL), BF16),
         SDS((1, D_MODEL), F32), SDS((1, D_MODEL), F32)),
        (dxn, xh, rstd, h, w_in, w_out, g), sem=("arbitrary",), bg=bg)


def _tn_matmul(a, b, name, bm, bn, col_block=0, total_cols=None, prev=None, bg=None, a_cols=None, a_t=False):
    t, m = a.shape[::-1] if a_t else a.shape
    a_first = 0
    if a_cols is not None:
        a_first, m = a_cols[0], a_cols[1] * bm
    n = b.shape[1]
    total_cols = n if total_cols is None else total_cols
    whole = bm * bn <= TN_SMALL_BLOCK and (m // bm) * (n // bn) >= 2
    bk = min(2 * TN_K_BLOCK if whole else TN_K_BLOCK, t)
    nk = t // bk
    n_in = 2 if prev is None else 4

    def body(*refs):
        a_ref, b_ref = refs[0], refs[1]
        o_ref, ob_ref = refs[n_in], refs[n_in + 1]
        k = pl.program_id(2)

        @pl.when(k == 0)
        def _():
            o_ref[...] = jnp.zeros_like(o_ref)

        o_ref[...] += _dot(a_ref[...], b_ref[...]) if a_t else _dot_tn(a_ref[...], b_ref[...])

        @pl.when(k == nk - 1)
        def _():
            ob_ref[...] = o_ref[...].astype(BF16)

    a_spec = (pl.BlockSpec((bm, bk), lambda i, j, k: (i + a_first, k)) if a_t
              else pl.BlockSpec((bk, bm), lambda i, j, k: (k, i + a_first)))
    in_specs = [a_spec, pl.BlockSpec((bk, bn), lambda i, j, k: (k, j))]
    args = [a, b]
    aliases = {}
    if prev is not None:
        in_specs += [pl.BlockSpec(memory_space=pl.ANY), pl.BlockSpec(memory_space=pl.ANY)]
        args += list(prev)
        aliases = {2: 0, 3: 1}
        if any(bg is p for p in prev):
            bg = None
    out_spec = pl.BlockSpec((bm, bn), lambda i, j, k: (i, j + col_block))
    return _call(body, name, (m // bm, n // bn, nk), in_specs, (out_spec, out_spec),
                 (SDS((m, total_cols), F32), SDS((m, total_cols), BF16)), args,
                 sem=("parallel", "parallel", "arbitrary"), bg=bg, aliases=aliases)


def _mixin_fwd(x1, w, tm, bg=None):
    t = x1.shape[0]

    def body(x_ref, w_ref, xbt_ref, za_ref, zuv_ref, gab_ref):
        xb = x_ref[...].astype(BF16)
        xbt_ref[...] = xb.T
        za_ref[...] = _dot(xb, w_ref[:, 0:512]).astype(BF16)
        zuv_ref[...] = _dot(xb, w_ref[:, 512:1536]).astype(BF16)
        gab_ref[...] = _dot(xb, w_ref[:, 1536:3584]).astype(BF16)

    return _call(
        body, "mixin_fwd", (t // tm,),
        [_rows(tm, D_MODEL), _resident((D_MODEL, 3584))],
        (_cols(D_MODEL, tm), _rows(tm, 512), _rows(tm, 1024), _rows(tm, 2048)),
        (SDS((D_MODEL, t), BF16), SDS((t, 512), BF16), SDS((t, 1024), BF16), SDS((t, 2048), BF16)),
        (x1, w), sem=("parallel",), bg=bg)


def _mixin_bwd(dx1a, dza, dzuv, dgab, w, tm, bg=None):
    t = dx1a.shape[0]

    def body(d_ref, dza_ref, dzuv_ref, dgab_ref, w_ref, dx_ref):
        dx_ref[...] = (d_ref[...] + _dot_nt(dza_ref[...], w_ref[:, 0:512])
                       + _dot_nt(dzuv_ref[...], w_ref[:, 512:1536])
                       + _dot_nt(dgab_ref[...], w_ref[:, 1536:3584]))

    return _call(
        body, "mixin_bwd", (t // tm,),
        [_rows(tm, D_MODEL), _rows(tm, 512), _rows(tm, 1024), _rows(tm, 2048), _resident((D_MODEL, 3584))],
        (_rows(tm, D_MODEL),), (SDS((t, D_MODEL), F32),),
        (dx1a, dza, dzuv, dgab, w), sem=("parallel",), bg=bg)


def _unrolled(lo, hi, body, carry):
    for j in range(lo, hi):
        carry = body(j, carry)
    return carry


def _scan_fwd(hr_ref, hi_ref, a_ref, ap_ref, carry_ref, seg, cin_ref):
    for lc in range(SSM_LANES // SCAN_LANES):
        ls = slice(lc * SCAN_LANES, (lc + 1) * SCAN_LANES)
        a_r = jnp.broadcast_to(a_ref[0:1, ls], (8, SCAN_LANES))
        a_i = jnp.broadcast_to(a_ref[1:2, ls], (8, SCAN_LANES))

        def step(j, hc, ls=ls, a_r=a_r, a_i=a_i):
            h_r, h_i = hc
            rows = pl.ds(j * 8, 8)
            n_r = a_r * h_r - a_i * h_i + hr_ref[rows, ls]
            n_i = a_r * h_i + a_i * h_r + hi_ref[rows, ls]
            hr_ref[rows, ls] = n_r
            hi_ref[rows, ls] = n_i
            return n_r, n_i

        zero = jnp.zeros((8, SCAN_LANES), F32)
        f_r, f_i = _unrolled(0, seg, step, (zero, zero))
        c_r = carry_ref[0:1, ls]
        c_i = carry_ref[1:2, ls]
        p_r = ap_ref[0:1, ls]
        p_i = ap_ref[1:2, ls]
        rows_r, rows_i = [], []
        for s in range(8):
            rows_r.append(c_r)
            rows_i.append(c_i)
            c_r, c_i = (f_r[s:s + 1] + p_r * c_r - p_i * c_i,
                        f_i[s:s + 1] + p_r * c_i + p_i * c_r)
        carry_ref[0:1, ls] = c_r
        carry_ref[1:2, ls] = c_i
        cin_r = jnp.concatenate(rows_r, axis=0)
        cin_i = jnp.concatenate(rows_i, axis=0)
        if cin_ref is not None:
            cin_ref[0, :, ls] = cin_r
            cin_ref[1, :, ls] = cin_i

        def fix(j, cc, ls=ls, a_r=a_r, a_i=a_i):
            c_r, c_i = cc
            c_r, c_i = a_r * c_r - a_i * c_i, a_r * c_i + a_i * c_r
            rows = pl.ds(j * 8, 8)
            hr_ref[rows, ls] = hr_ref[rows, ls] + c_r
            hi_ref[rows, ls] = hi_ref[rows, ls] + c_i
            return c_r, c_i

        _unrolled(0, seg, fix, (cin_r, cin_i))


def _scan_bwd(gr_ref, gi_ref, hr_ref, hi_ref, cin_ref, a_ref, ap_ref, rcarry_ref, da_ref, seg):
    for lc in range(SSM_LANES // SCAN_LANES):
        ls = slice(lc * SCAN_LANES, (lc + 1) * SCAN_LANES)
        a_r = jnp.broadcast_to(a_ref[0:1, ls], (8, SCAN_LANES))
        a_i = jnp.broadcast_to(a_ref[1:2, ls], (8, SCAN_LANES))

        def step(t, gc, ls=ls, a_r=a_r, a_i=a_i):
            g_r, g_i = gc
            rows = pl.ds((seg - 1 - t) * 8, 8)
            n_r = gr_ref[rows, ls] + a_r * g_r + a_i * g_i
            n_i = gi_ref[rows, ls] + a_r * g_i - a_i * g_r
            gr_ref[rows, ls] = n_r
            gi_ref[rows, ls] = n_i
            return n_r, n_i

        zero = jnp.zeros((8, SCAN_LANES), F32)
        f_r, f_i = _unrolled(0, seg, step, (zero, zero))
        c_r = rcarry_ref[0:1, ls]
        c_i = rcarry_ref[1:2, ls]
        p_r = ap_ref[0:1, ls]
        p_i = ap_ref[1:2, ls]
        rows_r, rows_i = [None] * 8, [None] * 8
        for s in range(7, -1, -1):
            rows_r[s] = c_r
            rows_i[s] = c_i
            c_r, c_i = (f_r[s:s + 1] + p_r * c_r + p_i * c_i,
                        f_i[s:s + 1] + p_r * c_i - p_i * c_r)
        rcarry_ref[0:1, ls] = c_r
        rcarry_ref[1:2, ls] = c_i
        cin_r = jnp.concatenate(rows_r, axis=0)
        cin_i = jnp.concatenate(rows_i, axis=0)

        def fix_row(j_rows, hp_r, hp_i, cc, ls=ls, a_r=a_r, a_i=a_i):
            c_r, c_i, acc_r, acc_i = cc
            c_r, c_i = a_r * c_r + a_i * c_i, a_r * c_i - a_i * c_r
            g_r = gr_ref[j_rows, ls] + c_r
            g_i = gi_ref[j_rows, ls] + c_i
            gr_ref[j_rows, ls] = g_r
            gi_ref[j_rows, ls] = g_i
            acc_r = acc_r + g_r * hp_r + g_i * hp_i
            acc_i = acc_i + g_i * hp_r - g_r * hp_i
            return c_r, c_i, acc_r, acc_i

        def fix(t, cc, ls=ls, fix_row=fix_row):
            j = seg - 1 - t
            rows = pl.ds(j * 8, 8)
            prev = pl.ds((j - 1) * 8, 8)
            return fix_row(rows, hr_ref[prev, ls], hi_ref[prev, ls], cc)

        cc = _unrolled(0, seg - 1, fix, (cin_r, cin_i, zero, zero))
        _, _, acc_r, acc_i = fix_row(pl.ds(0, 8), cin_ref[0, :, ls], cin_ref[1, :, ls], cc)
        da_ref[0, :, ls] += acc_r
        da_ref[1, :, ls] += acc_i


def _s5_fwd(za, sp, bsz, seq, tb, bg=None):
    nb = seq // tb
    seg = tb // 8
    t = bsz * seq

    def body(za_ref, perm_ref, permt_ref, mre_ref, mim_ref, nre_ref, nim_ref, a_ref, ap_ref,
             dsk_ref, gw_ref, gb_ref, out_ref, outt_ref, y2_ref, car_ref, hr_ref, hi_ref, carry_ref):
        @pl.when(pl.program_id(1) == 0)
        def _():
            carry_ref[...] = jnp.zeros_like(carry_ref)

        car_ref[0] = carry_ref[...]
        up = _dot(perm_ref[...], za_ref[...])
        upb = up.astype(BF16)
        for bb in range(S5_BLOCKS):
            ub = upb[:, bb * S5_BLOCK_IN:(bb + 1) * S5_BLOCK_IN]
            st = slice(bb * S5_BLOCK_ST, (bb + 1) * S5_BLOCK_ST)
            hr_ref[:, st] = _dot(ub, mre_ref[bb])
            hi_ref[:, st] = _dot(ub, mim_ref[bb])
        _scan_fwd(hr_ref, hi_ref, a_ref, ap_ref, carry_ref, seg, None)
        ys = []
        for bb in range(S5_BLOCKS):
            st = slice(bb * S5_BLOCK_ST, (bb + 1) * S5_BLOCK_ST)
            ys.append(_dot(hr_ref[:, st].astype(BF16), nre_ref[bb])
                      - _dot(hi_ref[:, st].astype(BF16), nim_ref[bb]))
        y2 = jnp.concatenate(ys, axis=1) + dsk_ref[...] * up
        y2_ref[...] = y2
        y3 = _gelu(y2)
        gl = _dot(y3.astype(BF16), gw_ref[...]) + gb_ref[...]
        oa = y3 * _sigmoid(gl)
        out = _dot(permt_ref[...], oa.astype(BF16)).astype(BF16)
        out_ref[...] = out
        outt_ref[...] = out.T

    blk = pl.BlockSpec((tb, D_SSM), lambda b, j: (b * nb + j, 0))
    blk_t = pl.BlockSpec((D_SSM, tb), lambda b, j: (0, b * nb + j))
    m_shape = (S5_BLOCKS, S5_BLOCK_IN, S5_BLOCK_ST)
    n_shape = (S5_BLOCKS, S5_BLOCK_ST, S5_BLOCK_IN)
    return _call(
        body, "s5_fwd", (bsz, nb),
        [blk, _fixed((tb, tb)), _fixed((tb, tb)), _fixed(m_shape), _fixed(m_shape), _fixed(n_shape),
         _fixed(n_shape), _fixed((2, SSM_LANES)), _fixed((2, SSM_LANES)), _fixed((1, D_SSM)),
         _fixed((D_SSM, D_SSM)), _fixed((1, D_SSM))],
        (blk, blk_t, blk, pl.BlockSpec((1, 2, SSM_LANES), lambda b, j: (b * nb + j, 0, 0))),
        (SDS((t, D_SSM), BF16), SDS((D_SSM, t), BF16), SDS((t, D_SSM), F32), SDS((bsz * nb, 2, SSM_LANES), F32)),
        (za, sp["perm"], sp["permt"], sp["mre"], sp["mim"], sp["nre"], sp["nim"], sp["a"], sp["ap"],
         sp["dskip"], sp["glu_w"], sp["glu_b"]),
        scratch=[pltpu.VMEM((tb, SSM_LANES), F32), pltpu.VMEM((tb, SSM_LANES), F32),
                 pltpu.VMEM((2, SSM_LANES), F32)],
        sem=("arbitrary", "arbitrary"), bg=bg)


def _s5_bwd(za, y2p, doa, carries, sp, bsz, seq, tb, bg=None):
    nb = seq // tb
    seg = tb // 8
    t = bsz * seq

    def body(za_ref, y2_ref, doa_ref, car_ref, perm_ref, permt_ref, mre_ref, mim_ref, mtre_ref, mtim_ref,
             nre_ref, nim_ref, ntre_ref, ntim_ref, a_ref, ap_ref, dsk_ref, gw_ref, gwt_ref, gb_ref,
             dza_ref, dmr_ref, dmi_ref, dnr_ref, dni_ref, da_ref, ddsk_ref, dgw_ref, dgb_ref,
             hr_ref, hi_ref, gr_ref, gi_ref, cin_ref, carry_ref, rcarry_ref):
        first = jnp.logical_and(pl.program_id(0) == 0, pl.program_id(1) == 0)

        @pl.when(first)
        def _():
            for r in (dmr_ref, dmi_ref, dnr_ref, dni_ref, da_ref, ddsk_ref, dgw_ref, dgb_ref):
                r[...] = jnp.zeros_like(r)

        @pl.when(pl.program_id(1) == 0)
        def _():
            rcarry_ref[...] = jnp.zeros_like(rcarry_ref)

        carry_ref[...] = car_ref[0]
        perm = perm_ref[...]
        up = _dot(perm, za_ref[...])
        upb = up.astype(BF16)
        for bb in range(S5_BLOCKS):
            ub = upb[:, bb * S5_BLOCK_IN:(bb + 1) * S5_BLOCK_IN]
            st = slice(bb * S5_BLOCK_ST, (bb + 1) * S5_BLOCK_ST)
            hr_ref[:, st] = _dot(ub, mre_ref[bb])
            hi_ref[:, st] = _dot(ub, mim_ref[bb])
        _scan_fwd(hr_ref, hi_ref, a_ref, ap_ref, carry_ref, seg, cin_ref)

        y2 = y2_ref[...]
        y3 = _gelu(y2)
        y3b = y3.astype(BF16)
        sg = _sigmoid(_dot(y3b, gw_ref[...]) + gb_ref[...])
        d0 = doa_ref[...]
        d_hi = d0.astype(BF16)
        d1 = d0 - d_hi.astype(F32)
        d_mid = d1.astype(BF16)
        d_lo = (d1 - d_mid.astype(F32)).astype(BF16)
        doap = _dot(perm, d_hi) + _dot(perm, d_mid) + _dot(perm, d_lo)
        dgl = doap * y3 * sg * (1.0 - sg)
        dglb = dgl.astype(BF16)
        dy3 = doap * sg + _dot(dglb, gwt_ref[...])
        dgw_ref[...] += _dot_tn(y3b, dglb)
        dgb_ref[...] += jnp.sum(dgl, axis=0, keepdims=True)
        dy2 = dy3 * _gelu_grad(y2)
        ddsk_ref[...] += jnp.sum(dy2 * up, axis=0, keepdims=True)
        dyb = dy2.astype(BF16)
        for bb in range(S5_BLOCKS):
            dyc = dyb[:, bb * S5_BLOCK_IN:(bb + 1) * S5_BLOCK_IN]
            st = slice(bb * S5_BLOCK_ST, (bb + 1) * S5_BLOCK_ST)
            gr_ref[:, st] = _dot(dyc, ntre_ref[bb])
            gi_ref[:, st] = -_dot(dyc, ntim_ref[bb])
            dnr_ref[bb] += _dot_tn(hr_ref[:, st].astype(BF16), dyc)
            dni_ref[bb] += -_dot_tn(hi_ref[:, st].astype(BF16), dyc)
        _scan_bwd(gr_ref, gi_ref, hr_ref, hi_ref, cin_ref, a_ref, ap_ref, rcarry_ref, da_ref, seg)
        dus = []
        for bb in range(S5_BLOCKS):
            st = slice(bb * S5_BLOCK_ST, (bb + 1) * S5_BLOCK_ST)
            grb = gr_ref[:, st].astype(BF16)
            gib = gi_ref[:, st].astype(BF16)
            dus.append(_dot(grb, mtre_ref[bb]) + _dot(gib, mtim_ref[bb]))
            ub = upb[:, bb * S5_BLOCK_IN:(bb + 1) * S5_BLOCK_IN]
            dmr_ref[bb] += _dot_tn(ub, grb)
            dmi_ref[bb] += _dot_tn(ub, gib)
        du = jnp.concatenate(dus, axis=1) + dy2 * dsk_ref[...]
        dza_ref[...] = _dot(permt_ref[...], du.astype(BF16)).astype(BF16)

    def rev(b, j):
        return (b * nb + (nb - 1 - j), 0)

    blk = pl.BlockSpec((tb, D_SSM), rev)
    m_shape = (S5_BLOCKS, S5_BLOCK_IN, S5_BLOCK_ST)
    n_shape = (S5_BLOCKS, S5_BLOCK_ST, S5_BLOCK_IN)
    return _call(
        body, "s5_bwd", (bsz, nb),
        [blk, blk, blk, pl.BlockSpec((1, 2, SSM_LANES), lambda b, j: (b * nb + (nb - 1 - j), 0, 0)),
         _fixed((tb, tb)), _fixed((tb, tb)), _fixed(m_shape), _fixed(m_shape), _fixed(n_shape), _fixed(n_shape),
         _fixed(n_shape), _fixed(n_shape), _fixed(m_shape), _fixed(m_shape),
         _fixed((2, SSM_LANES)), _fixed((2, SSM_LANES)), _fixed((1, D_SSM)),
         _fixed((D_SSM, D_SSM)), _fixed((D_SSM, D_SSM)), _fixed((1, D_SSM))],
        (blk, _fixed(m_shape), _fixed(m_shape), _fixed(n_shape), _fixed(n_shape),
         _fixed((2, 8, SSM_LANES)), _fixed((1, D_SSM)), _fixed((D_SSM, D_SSM)), _fixed((1, D_SSM))),
        (SDS((t, D_SSM), BF16), SDS(m_shape, F32), SDS(m_shape, F32), SDS(n_shape, F32), SDS(n_shape, F32),
         SDS((2, 8, SSM_LANES), F32), SDS((1, D_SSM), F32), SDS((D_SSM, D_SSM), F32), SDS((1, D_SSM), F32)),
        (za, y2p, doa, carries, sp["perm"], sp["permt"], sp["mre"], sp["mim"], sp["mtre"], sp["mtim"],
         sp["nre"], sp["nim"], sp["ntre"], sp["ntim"], sp["a"], sp["ap"], sp["dskip"], sp["glu_w"],
         sp["glu_wt"], sp["glu_b"]),
        scratch=[pltpu.VMEM((tb, SSM_LANES), F32), pltpu.VMEM((tb, SSM_LANES), F32),
                 pltpu.VMEM((tb, SSM_LANES), F32), pltpu.VMEM((tb, SSM_LANES), F32),
                 pltpu.VMEM((2, 8, SSM_LANES), F32), pltpu.VMEM((2, SSM_LANES), F32),
                 pltpu.VMEM((2, SSM_LANES), F32)],
        sem=("arbitrary", "arbitrary"), bg=bg)


def _gmlp_spatial(ws_ref, vb):
    lane = lax.broadcasted_iota(jnp.int32, (CHUNK, 128), 1)
    parts = []
    for j in range(GMLP_HEADS // 2):
        vp = vb[:, 128 * j:128 * (j + 1)]
        parts.append(jnp.where(lane < GMLP_HEAD_DIM, _dot(ws_ref[2 * j], vp), _dot(ws_ref[2 * j + 1], vp)))
    return jnp.concatenate(parts, axis=1)


def _gmlp_fwd(zuv, ln_g, ln_b, wsm, bias, bg=None):
    t = zuv.shape[0]
    chunks = min(GMLP_CHUNKS, t // CHUNK)

    def body(z_ref, g_ref, b_ref, ws_ref, bias_ref, out_ref, outt_ref):
        for ch in range(chunks):
            rows = slice(ch * CHUNK, (ch + 1) * CHUNK)
            u = _gelu(z_ref[rows, 0:D_GMLP].astype(F32))
            v0 = _gelu(z_ref[rows, D_GMLP:2 * D_GMLP].astype(F32))
            v, _, _ = _ln_fwd(v0, g_ref[...], b_ref[...])
            s = _gmlp_spatial(ws_ref, v.astype(BF16)) + bias_ref[...]
            out = (u * s).astype(BF16)
            out_ref[rows, :] = out
            outt_ref[:, rows] = out.T

    step = chunks * CHUNK
    return _call(
        body, "gmlp_fwd", (t // step,),
        [_rows(step, 2 * D_GMLP), _fixed((1, D_GMLP)), _fixed((1, D_GMLP)),
         _fixed((GMLP_HEADS, CHUNK, CHUNK)), _fixed((CHUNK, D_GMLP))],
        (_rows(step, D_GMLP), _cols(D_GMLP, step)), (SDS((t, D_GMLP), BF16), SDS((D_GMLP, t), BF16)),
        (zuv, ln_g, ln_b, wsm, bias), sem=("parallel",), bg=bg)


def _gmlp_bwd(zuv, dgm, ln_g, ln_b, wsm, wsmt, bias, bg=None):
    t = zuv.shape[0]
    chunks = min(GMLP_CHUNKS, t // CHUNK)

    def body(z_ref, d_ref, g_ref, b_ref, ws_ref, wst_ref, bias_ref,
             dz_ref, dws_ref, dbias_ref, dg_ref, db_ref):
        @pl.when(pl.program_id(0) == 0)
        def _():
            for r in (dws_ref, dbias_ref, dg_ref, db_ref):
                r[...] = jnp.zeros_like(r)

        gam = g_ref[...]
        lane = lax.broadcasted_iota(jnp.int32, (CHUNK, 128), 1)
        tril = (lax.broadcasted_iota(jnp.int32, (CHUNK, CHUNK), 0)
                >= lax.broadcasted_iota(jnp.int32, (CHUNK, CHUNK), 1))
        zero_b = jnp.zeros((CHUNK, 128), BF16)
        for ch in range(chunks):
            rows = slice(ch * CHUNK, (ch + 1) * CHUNK)
            zu = z_ref[rows, 0:D_GMLP].astype(F32)
            zv = z_ref[rows, D_GMLP:2 * D_GMLP].astype(F32)
            u = _gelu(zu)
            v0 = _gelu(zv)
            v, vhat, rstd = _ln_fwd(v0, gam, b_ref[...])
            vb = v.astype(BF16)
            s = _gmlp_spatial(ws_ref, vb) + bias_ref[...]
            d = d_ref[rows, :]
            dz_ref[rows, 0:D_GMLP] = (d * s * _gelu_grad(zu)).astype(BF16)
            ds = d * u
            dbias_ref[...] += ds
            dsb = ds.astype(BF16)
            parts = []
            for j in range(GMLP_HEADS // 2):
                dsp = dsb[:, 128 * j:128 * (j + 1)]
                vp = vb[:, 128 * j:128 * (j + 1)]
                parts.append(jnp.where(lane < GMLP_HEAD_DIM, _dot(wst_ref[2 * j], dsp),
                                       _dot(wst_ref[2 * j + 1], dsp)))
                lo = jnp.where(lane < GMLP_HEAD_DIM, dsp, zero_b)
                hi = jnp.where(lane < GMLP_HEAD_DIM, zero_b, dsp)
                dws_ref[2 * j] += jnp.where(tril, _dot_nt(lo, vp), 0.0)
                dws_ref[2 * j + 1] += jnp.where(tril, _dot_nt(hi, vp), 0.0)
            dv = jnp.concatenate(parts, axis=1)
            dg_ref[...] += jnp.sum(dv * vhat, axis=0, keepdims=True)
            db_ref[...] += jnp.sum(dv, axis=0, keepdims=True)
            dz_ref[rows, D_GMLP:2 * D_GMLP] = (_ln_bwd(dv, vhat, rstd, gam) * _gelu_grad(zv)).astype(BF16)

    step = chunks * CHUNK
    return _call(
        body, "gmlp_bwd", (t // step,),
        [_rows(step, 2 * D_GMLP), _rows(step, D_GMLP), _fixed((1, D_GMLP)), _fixed((1, D_GMLP)),
         _fixed((GMLP_HEADS, CHUNK, CHUNK)), _fixed((GMLP_HEADS, CHUNK, CHUNK)), _fixed((CHUNK, D_GMLP))],
        (_rows(step, 2 * D_GMLP), _fixed((GMLP_HEADS, CHUNK, CHUNK)), _fixed((CHUNK, D_GMLP)),
         _fixed((1, D_GMLP)), _fixed((1, D_GMLP))),
        (SDS((t, 2 * D_GMLP), BF16), SDS((GMLP_HEADS, CHUNK, CHUNK), F32), SDS((CHUNK, D_GMLP), F32),
         SDS((1, D_GMLP), F32), SDS((1, D_GMLP), F32)),
        (zuv, dgm, ln_g, ln_b, wsm, wsmt, bias), sem=("arbitrary",), bg=bg)


def _mixout_fwd(x1, s5o, gm, gab, ua, ub, wmo, g, b, tm, bg=None):
    t = x1.shape[0]

    def body(x_ref, s_ref, m_ref, gab_ref, ua_ref, ub_ref, wmo_ref, g_ref, b_ref,
             xn_ref, xh_ref, rstd_ref):
        ya = _dot(s_ref[...], ua_ref[...])
        yb = _dot(m_ref[...], ub_ref[...])
        mix = (_sigmoid(gab_ref[:, 0:D_MODEL].astype(F32)) * ya
               + _sigmoid(gab_ref[:, D_MODEL:2 * D_MODEL].astype(F32)) * yb)
        r = ALPHA * x_ref[...] + _dot(mix.astype(BF16), wmo_ref[...])
        y, xh, rstd = _ln_fwd(r, g_ref[...], b_ref[...])
        xn_ref[...] = y
        xh_ref[...] = xh
        rstd_ref[...] = rstd

    return _call(
        body, "mixout_fwd", (t // tm,),
        [_rows(tm, D_MODEL), _rows(tm, D_SSM), _rows(tm, D_GMLP), _rows(tm, 2 * D_MODEL),
         _resident((D_SSM, D_MODEL)), _resident((D_GMLP, D_MODEL)), _resident((D_MODEL, D_MODEL)),
         _fixed((1, D_MODEL)), _fixed((1, D_MODEL))],
        (_rows(tm, D_MODEL), _rows(tm, D_MODEL), _rows(tm, 1)),
        (SDS((t, D_MODEL), F32), SDS((t, D_MODEL), F32), SDS((t, 1), F32)),
        (x1, s5o, gm, gab, ua, ub, wmo, g, b), sem=("parallel",), bg=bg)


def _mixout_bwd(dx2, xh, rstd, s5o, gm, gab, ua, ub, wmo, g, tm, bg=None):
    t = dx2.shape[0]

    def body(d_ref, xh_ref, rstd_ref, s_ref, m_ref, gab_ref, ua_ref, ub_ref, wmo_ref, g_ref,
             dx1_ref, dmx_ref, mb_ref, dya_ref, dyb_ref, ds5_ref, dgm_ref, dgab_ref, dg_ref, db_ref):
        @pl.when(pl.program_id(0) == 0)
        def _():
            dg_ref[...] = jnp.zeros_like(dg_ref)
            db_ref[...] = jnp.zeros_like(db_ref)

        dy = d_ref[...]
        xhv = xh_ref[...]
        dr = _ln_bwd(dy, xhv, rstd_ref[...], g_ref[...])
        dg_ref[...] += jnp.sum(dy * xhv, axis=0, keepdims=True)
        db_ref[...] += jnp.sum(dy, axis=0, keepdims=True)
        dx1_ref[...] = ALPHA * dr
        drb = dr.astype(BF16)
        dmx_ref[...] = drb
        dm = _dot_nt(drb, wmo_ref[...])
        ya = _dot(s_ref[...], ua_ref[...])
        yb = _dot(m_ref[...], ub_ref[...])
        sa = _sigmoid(gab_ref[:, 0:D_MODEL].astype(F32))
        sb = _sigmoid(gab_ref[:, D_MODEL:2 * D_MODEL].astype(F32))
        mb_ref[...] = (sa * ya + sb * yb).astype(BF16).T
        dya = (dm * sa).astype(BF16)
        dyb = (dm * sb).astype(BF16)
        dya_ref[...] = dya
        dyb_ref[...] = dyb
        dgab_ref[:, 0:D_MODEL] = (dm * ya * sa * (1.0 - sa)).astype(BF16)
        dgab_ref[:, D_MODEL:2 * D_MODEL] = (dm * yb * sb * (1.0 - sb)).astype(BF16)
        ds5_ref[...] = _dot_nt(dya, ua_ref[...])
        dgm_ref[...] = _dot_nt(dyb, ub_ref[...])

    return _call(
        body, "mixout_bwd", (t // tm,),
        [_rows(tm, D_MODEL), _rows(tm, D_MODEL), _rows(tm, 1), _rows(tm, D_SSM), _rows(tm, D_GMLP),
         _rows(tm, 2 * D_MODEL), _resident((D_SSM, D_MODEL)), _resident((D_GMLP, D_MODEL)),
         _resident((D_MODEL, D_MODEL)), _fixed((1, D_MODEL))],
        (_rows(tm, D_MODEL), _rows(tm, D_MODEL), _cols(D_MODEL, tm), _rows(tm, D_MODEL),
         _rows(tm, D_MODEL), _rows(tm, D_SSM), _rows(tm, D_GMLP), _rows(tm, 2 * D_MODEL),
         _fixed((1, D_MODEL)), _fixed((1, D_MODEL))),
        (SDS((t, D_MODEL), F32), SDS((t, D_MODEL), BF16), SDS((D_MODEL, t), BF16),
         SDS((t, D_MODEL), BF16), SDS((t, D_MODEL), BF16), SDS((t, D_SSM), F32),
         SDS((t, D_GMLP), F32), SDS((t, 2 * D_MODEL), BF16),
         SDS((1, D_MODEL), F32), SDS((1, D_MODEL), F32)),
        (dx2, xh, rstd, s5o, gm, gab, ua, ub, wmo, g), sem=("arbitrary",), bg=bg)


def _s5_discretise(lre, lim, log_dt, bre, bim):
    dt = jnp.exp(log_dt)[:, None]
    mag = jnp.exp(lre * dt)
    abr = mag * jnp.cos(lim * dt)
    abi = mag * jnp.sin(lim * dt)
    nr = abr - 1.0
    ni = abi
    den = lre * lre + lim * lim
    cr = ((nr * lre + ni * lim) / den)[..., None]
    ci = ((ni * lre - nr * lim) / den)[..., None]
    return abr, abi, cr * bre - ci * bim, cr * bim + ci * bre


def _block_diag_in(bb):
    v = bb.reshape(S5_BLOCKS, 8, SSM_STATE, SSM_GROUP_CH).transpose(0, 1, 3, 2)
    return jnp.einsum("bgip,gh->bgihp", v, jnp.eye(8, dtype=bb.dtype)).reshape(
        S5_BLOCKS, S5_BLOCK_IN, S5_BLOCK_ST)


def _block_diag_in_t(dm):
    v = dm.reshape(S5_BLOCKS, 8, SSM_GROUP_CH, 8, SSM_STATE)
    d = jnp.einsum("bgihp,gh->bgip", v, jnp.eye(8, dtype=dm.dtype))
    return d.transpose(0, 1, 3, 2).reshape(SSM_GROUPS, SSM_STATE, SSM_GROUP_CH)


def _block_diag_out(cc):
    v = cc.reshape(S5_BLOCKS, 8, SSM_GROUP_CH, SSM_STATE)
    return jnp.einsum("bgip,gh->bgphi", v, jnp.eye(8, dtype=cc.dtype)).reshape(
        S5_BLOCKS, S5_BLOCK_ST, S5_BLOCK_IN)


def _block_diag_out_t(dn):
    v = dn.reshape(S5_BLOCKS, 8, SSM_STATE, 8, SSM_GROUP_CH)
    d = jnp.einsum("bgphi,gh->bgip", v, jnp.eye(8, dtype=dn.dtype))
    return d.reshape(SSM_GROUPS, SSM_GROUP_CH, SSM_STATE)


def _s5_setup(lre, lim, log_dt, bre, bim, cre, cim, d_skip, glu_w, glu_b, tb):
    seg = tb // 8
    abr, abi, bbr, bbi = _s5_discretise(lre, lim, log_dt, bre, bim)
    pr, pi = abr, abi
    for _ in range(int(math.log2(seg))):
        pr, pi = pr * pr - pi * pi, 2.0 * pr * pi
    rows = jnp.arange(tb)
    src = (rows % 8) * seg + rows // 8
    perm = (src[:, None] == jnp.arange(tb)[None, :]).astype(BF16)
    mre = _block_diag_in(bbr)
    mim = _block_diag_in(bbi)
    nre = _block_diag_out(cre)
    nim = _block_diag_out(cim)
    return {
        "perm": perm, "permt": perm.T,
        "mre": mre.astype(BF16), "mim": mim.astype(BF16),
        "mtre": mre.transpose(0, 2, 1).astype(BF16), "mtim": mim.transpose(0, 2, 1).astype(BF16),
        "nre": nre.astype(BF16), "nim": nim.astype(BF16),
        "ntre": nre.transpose(0, 2, 1).astype(BF16), "ntim": nim.transpose(0, 2, 1).astype(BF16),
        "a": jnp.stack([abr.reshape(-1), abi.reshape(-1)]),
        "ap": jnp.stack([pr.reshape(-1), pi.reshape(-1)]),
        "dskip": d_skip.reshape(1, D_SSM), "glu_w": glu_w, "glu_wt": glu_w.T,
        "glu_b": glu_b.reshape(1, D_SSM),
    }


BIG = ("ffn1_w_in", "ffn1_w_out", "mix_w_in", "ssm_glu_w", "up_a", "up_b", "mix_w_out",
       "ffn2_w_in", "ffn2_w_out", "ple_w_proj", "ple_w_gate")
BIG_AXIS = {"ffn1_w_in": 1, "ffn1_w_out": 0, "mix_w_in": 1, "ssm_glu_w": 0, "up_a": 1, "up_b": 1,
            "mix_w_out": 0, "ffn2_w_in": 1, "ffn2_w_out": 0, "ple_w_proj": 1, "ple_w_gate": 0}
SHARD_MAJOR = 2
GATHER_AXIS = dict(BIG_AXIS, ffn1_w_in=SHARD_MAJOR, ffn2_w_in=SHARD_MAJOR)
GATHER_ORDER = (("ffn1_w_in",), ("ffn1_w_out",), ("mix_w_in",), ("ssm_glu_w", "up_a", "up_b", "mix_w_out"),
                ("ffn2_w_in",), ("ffn2_w_out", "ple_w_gate", "ple_w_proj"))
GATHER_FIRST_ID = 1
REDUCE_FIRST_ID = 7
SMALL = ("ln1_g", "ln1_b", "ssm_lambda_re", "ssm_lambda_im", "ssm_log_dt", "ssm_b_re", "ssm_b_im",
         "ssm_c_re", "ssm_c_im", "ssm_d", "ssm_glu_b", "gmlp_ln_g", "gmlp_ln_b", "gmlp_w_s",
         "gmlp_b_s", "ln2_g", "ln2_b", "ln3_g", "ln3_b")
SMALL_VIEW = {"ssm_b_re": (SSM_GROUPS, SSM_STATE * SSM_GROUP_CH), "ssm_b_im": (SSM_GROUPS, SSM_STATE * SSM_GROUP_CH)}


def _small_view(k, a):
    return a.reshape(SMALL_VIEW[k]) if k in SMALL_VIEW else a


def _place():
    return lax.axis_index("x"), lax.axis_index("y"), lax.axis_index("c")


def _other_chips(x, y):
    return [(1 - x, y), (x, 1 - y), (1 - x, 1 - y)]


def _window(ref, shard_shape, axis, chip, half):
    r, c = shard_shape
    hr = r // 2
    if axis == SHARD_MAJOR:
        return ref.at[chip] if half is None else ref.at[chip, pl.ds(half * hr, hr), :]
    if axis == 0:
        if half is None:
            return ref.at[pl.ds(chip * r, r), :]
        return ref.at[pl.ds(chip * r + half * hr, hr), :]
    if half is None:
        return ref.at[:, pl.ds(chip * c, c)]
    return ref.at[pl.ds(half * hr, hr), pl.ds(chip * c, c)]


def _gather_weights(shards, axes):
    n = len(shards)
    shapes = [s.shape for s in shards]
    full = [{0: (4 * r, c), 1: (r, 4 * c), SHARD_MAJOR: (4, r, c)}[ax] for (r, c), ax in zip(shapes, axes)]

    def remote(sems, i, k, src, dst, to):
        return pltpu.make_async_remote_copy(src_ref=src, dst_ref=dst, send_sem=sems[0].at[6 * i + k],
                                            recv_sem=sems[1].at[6 * i + k], device_id=to, device_id_type=MESH)

    def own_copies(ins, outs, sems):
        x, y, c = _place()
        me = 2 * x + y
        cps = []
        for i in range(n):
            hr = shapes[i][0] // 2
            mine = ins[i].at[pl.ds(c * hr, hr), :]
            for j, (cx, cy) in enumerate(_other_chips(x, y)):
                cps.append(remote(sems, i, j, mine, _window(outs[i], shapes[i], axes[i], me, c), (cx, cy, c)))
        local = [pltpu.make_async_copy(ins[i], _window(outs[i], shapes[i], axes[i], me, None), sems[2].at[i])
                 for i in range(n)]
        return cps, local

    def start(ins, outs, sems):
        cps, local = own_copies(ins, outs, sems)
        for cp in local + cps:
            cp.start()

    def finish(ins, outs, sems):
        x, y, c = _place()
        sibling = (x, y, 1 - c)
        passed = []
        for j, (cx, cy) in enumerate(_other_chips(x, y)):
            for i in range(n):
                w = _window(outs[i], shapes[i], axes[i], 2 * cx + cy, c)
                remote(sems, i, j, w, w, (cx, cy, c)).wait_recv()
                cp = remote(sems, i, 3 + j, w, w, sibling)
                cp.start()
                passed.append(cp)
        for j, (cx, cy) in enumerate(_other_chips(x, y)):
            for i in range(n):
                w = _window(outs[i], shapes[i], axes[i], 2 * cx + cy, 1 - c)
                remote(sems, i, 3 + j, w, w, sibling).wait_recv()
        cps, local = own_copies(ins, outs, sems)
        for cp in cps + passed:
            cp.wait_send()
        for cp in local:
            cp.wait()

    return _Exchange(shards, [SDS(f, BF16) for f in full],
                     [pltpu.SemaphoreType.DMA((6 * n,)), pltpu.SemaphoreType.DMA((6 * n,)),
                      pltpu.SemaphoreType.DMA((n,))], start, finish)


def _scatter_grads(parts, shapes, axes):
    n = len(parts)

    def copies(ins, outs, sems):
        x, y, c = _place()
        return [pltpu.make_async_remote_copy(
            src_ref=_window(ins[i], shapes[i], axes[i], 2 * cx + cy, None), dst_ref=outs[i].at[j],
            send_sem=sems[0].at[3 * i + j], recv_sem=sems[1].at[3 * i + j],
            device_id=(cx, cy, c), device_id_type=MESH)
            for i in range(n) for j, (cx, cy) in enumerate(_other_chips(x, y))]

    def start(ins, outs, sems):
        for cp in copies(ins, outs, sems):
            cp.start()

    def finish(ins, outs, sems):
        for cp in copies(ins, outs, sems):
            cp.wait()

    return _Exchange(parts, [SDS((3,) + tuple(s), BF16) for s in shapes],
                     [pltpu.SemaphoreType.DMA((3 * n,)), pltpu.SemaphoreType.DMA((3 * n,))], start, finish)


def _swap_halves(parts, shapes, axes):
    n = len(parts)

    def copies(ins, outs, sems):
        x, y, c = _place()
        cps = []
        for i in range(n):
            r, _ = shapes[i]
            hr = r // 2
            if axes[i] == 0:
                cps += [pltpu.make_async_remote_copy(
                    src_ref=ins[i].at[pl.ds(k * r + (1 - c) * hr, hr), :], dst_ref=outs[i].at[k],
                    send_sem=sems[0].at[i], recv_sem=sems[1].at[i], device_id=(x, y, 1 - c),
                    device_id_type=MESH) for k in range(4)]
            else:
                cps.append(pltpu.make_async_remote_copy(
                    src_ref=ins[i].at[pl.ds((1 - c) * hr, hr), :], dst_ref=outs[i],
                    send_sem=sems[0].at[i], recv_sem=sems[1].at[i], device_id=(x, y, 1 - c),
                    device_id_type=MESH))
        return cps

    def start(ins, outs, sems):
        for cp in copies(ins, outs, sems):
            cp.start()

    def finish(ins, outs, sems):
        x, y, c = _place()
        for i in range(n):
            pltpu.make_async_remote_copy(src_ref=outs[i], dst_ref=outs[i], send_sem=sems[0].at[i],
                                         recv_sem=sems[1].at[i], device_id=(x, y, 1 - c),
                                         device_id_type=MESH).wait()

    out = [SDS((4, r // 2, c), BF16) if ax == 0 else SDS((r // 2, 4 * c), BF16)
           for (r, c), ax in zip(shapes, axes)]
    return _Exchange(parts, out, [pltpu.SemaphoreType.DMA((n,)), pltpu.SemaphoreType.DMA((n,))], start, finish)


def _scatter_halves(pres, shapes):
    n = len(pres)

    def copies(ins, outs, sems):
        x, y, c = _place()
        return [pltpu.make_async_remote_copy(
            src_ref=ins[i].at[1 + j], dst_ref=outs[i].at[j], send_sem=sems[0].at[3 * i + j],
            recv_sem=sems[1].at[3 * i + j], device_id=(cx, cy, c), device_id_type=MESH)
            for i in range(n) for j, (cx, cy) in enumerate(_other_chips(x, y))]

    def start(ins, outs, sems):
        for cp in copies(ins, outs, sems):
            cp.start()

    def finish(ins, outs, sems):
        for cp in copies(ins, outs, sems):
            cp.wait()

    return _Exchange(pres, [SDS((3, r // 2, c), BF16) for r, c in shapes],
                     [pltpu.SemaphoreType.DMA((3 * n,)), pltpu.SemaphoreType.DMA((3 * n,))], start, finish)


def _swap_with_sibling(arrs):
    n = len(arrs)

    def copies(ins, outs, sems):
        x, y, c = _place()
        return [pltpu.make_async_remote_copy(src_ref=ins[i], dst_ref=outs[i], send_sem=sems[0].at[i],
                                             recv_sem=sems[1].at[i], device_id=(x, y, 1 - c),
                                             device_id_type=MESH) for i in range(n)]

    def start(ins, outs, sems):
        for cp in copies(ins, outs, sems):
            cp.start()

    def finish(ins, outs, sems):
        for cp in copies(ins, outs, sems):
            cp.wait()

    return _Exchange(arrs, [SDS(a.shape, a.dtype) for a in arrs],
                     [pltpu.SemaphoreType.DMA((n,)), pltpu.SemaphoreType.DMA((n,))], start, finish)


def _gather_small(arrs):
    n = len(arrs)

    def copy(sems, outs, i, k, block, to, src=None):
        px, py, pc = block
        dst = outs[i].at[4 * px + 2 * py + pc]
        return pltpu.make_async_remote_copy(
            src_ref=dst if src is None else src, dst_ref=dst, send_sem=sems[0].at[7 * i + k],
            recv_sem=sems[1].at[7 * i + k], device_id=to, device_id_type=MESH)

    def own_copies(ins, outs, sems):
        x, y, c = _place()
        cps = []
        for i in range(n):
            cps.append(copy(sems, outs, i, 0, (x, y, c), (x, y, 1 - c), src=ins[i]))
            for j, (cx, cy) in enumerate(_other_chips(x, y)):
                cps.append(copy(sems, outs, i, 1 + j, (x, y, c), (cx, cy, c), src=ins[i]))
        local = [pltpu.make_async_copy(ins[i], outs[i].at[4 * x + 2 * y + c], sems[2].at[i]) for i in range(n)]
        return cps, local

    def start(ins, outs, sems):
        cps, local = own_copies(ins, outs, sems)
        for cp in local + cps:
            cp.start()

    def finish(ins, outs, sems):
        x, y, c = _place()
        passed = []
        for j, (cx, cy) in enumerate(_other_chips(x, y)):
            for i in range(n):
                copy(sems, outs, i, 1 + j, (cx, cy, c), (x, y, c)).wait_recv()
                cp = copy(sems, outs, i, 4 + j, (cx, cy, c), (x, y, 1 - c))
                cp.start()
                passed.append(cp)
        for i in range(n):
            copy(sems, outs, i, 0, (x, y, 1 - c), (x, y, c)).wait_recv()
            for j, (cx, cy) in enumerate(_other_chips(x, y)):
                copy(sems, outs, i, 4 + j, (cx, cy, 1 - c), (x, y, c)).wait_recv()
        cps, local = own_copies(ins, outs, sems)
        for cp in cps + passed:
            cp.wait_send()
        for cp in local:
            cp.wait()

    return _Exchange(arrs, [SDS((N_DEV,) + a.shape, F32) for a in arrs],
                     [pltpu.SemaphoreType.DMA((7 * n,)), pltpu.SemaphoreType.DMA((7 * n,)),
                      pltpu.SemaphoreType.DMA((n,))], start, finish)


def _local_step(x, p, tgt, wb, ws, shards=None, opt=None):
    bsz, seq, _ = x.shape
    t = bsz * seq
    tm = min(256, t)
    tb = min(S5_TIME_BLOCK, seq)
    x0 = x.reshape(t, D_MODEL)
    p0 = p.reshape(t, PLE_DIM)
    tg = tgt.reshape(t, D_MODEL)
    row = lambda v: v.reshape(1, -1)
    dist = shards is not None
    wb = dict(wb)
    recv, sums, other, gathered = {}, {}, {}, {}
    gb = {}
    gs = {}
    shape_of, axis_of = {}, {}
    chip = None
    if dist:
        shape_of = {k: tuple(shards[k].shape) for k in BIG}
        axis_of = dict(BIG_AXIS)
        for q in range(LAST_PIECES):
            shape_of[LAST_PIECE % q] = (D_MODEL // LAST_PIECES, shape_of["ffn1_w_in"][1])
            axis_of[LAST_PIECE % q] = 1
        xi, yi, ci = _place()
        chip = (2 * xi + yi).astype(jnp.int32).reshape(1)
        ids = jnp.stack([2 * xi + yi] + [2 * cx + cy for cx, cy in _other_chips(xi, yi)] + [ci]).astype(jnp.int32)
    halfbuf, pre = {}, {}

    def gather(names):
        return _gather_weights([shards[k] for k in names], [GATHER_AXIS[k] for k in names]) if dist else None

    def exchange(scat=(), swap=(), halves=(), scat2=(), swap2=(), extra=None, after=None):
        if not dist:
            return None, []
        after = order[0] if after is None else after
        parts, tags = [], []
        if scat:
            parts.append(_scatter_grads([gb[k][1] for k in scat], [shape_of[k] for k in scat],
                                        [axis_of[k] for k in scat]))
            tags.append((recv, scat))
        if swap:
            for k in swap:
                sums[k] = order[0] = _sum_blocks(gb[k][0], recv[k], shape_of[k], axis_of[k], chip, "sum_" + k,
                                                 order[0])
            parts.append(_swap_with_sibling([sums[k] for k in swap]))
            tags.append((other, swap))
        if halves:
            parts.append(_swap_halves([gb[k][1] for k in halves], [shape_of[k] for k in halves],
                                      [axis_of[k] for k in halves]))
            tags.append((halfbuf, halves))
        if scat2:
            for k in scat2:
                pre[k] = _presum(gb[k][0], halfbuf[k], shape_of[k], axis_of[k], ids, "presum_" + k, order[0])
                order[0] = pre[k][0]
            parts.append(_scatter_halves([pre[k][1] for k in scat2], [shape_of[k] for k in scat2]))
            tags.append((recv, scat2))
        if swap2:
            for k in swap2:
                sums[k] = order[0] = _sum_half(pre[k][0], recv[k], "sum_" + k, order[0])
            parts.append(_swap_with_sibling([sums[k] for k in swap2]))
            tags.append((other, swap2))
        if extra is not None:
            parts.append(extra[0])
            tags.append((extra[1], extra[2]))
        return (_join(parts), tags) if parts else (None, [])

    def take(ex_tags, got):
        ex, tags = ex_tags
        if ex is not None:
            for (dst, names), (o0, o1) in zip(tags, ex.cuts):
                dst.update(zip(names, got[o0:o1]))

    order = [None]

    def ordered(builder, *args, **kw):
        res = builder(*args, bg=order[0] if dist else None, **kw)
        order[0] = res[0][0]
        return res

    launched = []

    def launch(ex_tags):
        if ex_tags[0] is not None:
            n = len(launched)
            launched.append(n)
            take(ex_tags, _run_exchange_on_sequencer(ex_tags[0], "reduce_%d" % n, REDUCE_FIRST_ID + n))

    small_shape = {k: _small_view(k, v).shape for k, v in ws.items()}
    small_shape["loss_rows"] = (1, D_MODEL)
    ws = {k: v if (v.ndim == 2 and k != "ssm_log_dt") else v[0] for k, v in ws.items()}
    tril = jnp.tril(jnp.ones((CHUNK, CHUNK), dtype=bool))
    wsm = jnp.where(tril[None], ws["gmlp_w_s"], 0.0)
    wsm_b = wsm.astype(BF16)
    wsmt_b = wsm.transpose(0, 2, 1).astype(BF16)
    bias = jnp.repeat(ws["gmlp_b_s"].T, GMLP_HEAD_DIM, axis=1)

    tf = min(512, t)
    if dist:
        for gi, names in enumerate(GATHER_ORDER):
            wb.update(zip(names, _run_exchange_on_sequencer(gather(names), "gather_%d" % gi, GATHER_FIRST_ID + gi)))
    (x0b, h1, a1), _ = _ffn_proj(x0, wb["ffn1_w_in"], tf, "ffn1_proj")
    (x1, xh1, rstd1), _ = _ffn_out(x0, a1, wb["ffn1_w_out"], row(ws["ln1_g"]), row(ws["ln1_b"]), tf, "ffn1_out")
    sp = _s5_setup(ws["ssm_lambda_re"], ws["ssm_lambda_im"], ws["ssm_log_dt"], ws["ssm_b_re"],
                   ws["ssm_b_im"], ws["ssm_c_re"], ws["ssm_c_im"], ws["ssm_d"], wb["ssm_glu_w"],
                   ws["ssm_glu_b"], tb)
    (x1b, za, zuv, gab), _ = _mixin_fwd(x1, wb["mix_w_in"], min(2 * tf, t))
    (s5o, s5ot, y2p, carries), _ = _s5_fwd(za, sp, bsz, seq, tb)
    (gm, gmt), _ = _gmlp_fwd(zuv, row(ws["gmlp_ln_g"]), row(ws["gmlp_ln_b"]), wsm_b, bias)
    (x2, xh2, rstd2), _ = _mixout_fwd(x1, s5o, gm, gab, wb["up_a"], wb["up_b"], wb["mix_w_out"],
                                           row(ws["ln2_g"]), row(ws["ln2_b"]), tf)
    (x2b, h2, a2), _ = _ffn_proj(x2, wb["ffn2_w_in"], tf, "ffn2_proj")
    (xh3, rstd3, dx3, x3b, pb, dq, de, loss_rows), _ = _ffn_out_loss(
        x2, a2, wb["ffn2_w_out"], row(ws["ln3_g"]), row(ws["ln3_b"]), p0, tg, wb["ple_w_gate"], wb["ple_w_proj"], tf)
    order[0] = dx3
    gb["ple_w_gate"], _ = ordered(_tn_matmul, x3b, dq, "dw_ple_gate", 1024, 1024, a_t=True)
    gb["ple_w_proj"], _ = ordered(_tn_matmul, pb, de, "dw_ple_proj", 256, 1024, a_t=True)
    launch(exchange(scat=("ple_w_gate", "ple_w_proj")))
    (dx2, dh2, df2, gs["ln3_g"], gs["ln3_b"]), _ = ordered(
        _ffn_bwd, dx3, xh3, rstd3, h2, wb["ffn2_w_in"], wb["ffn2_w_out"], row(ws["ln3_g"]), tm, "ffn2_bwd")
    gb["ffn2_w_out"], _ = ordered(_tn_matmul, a2, df2, "dw_ffn2_out", 1408, 1024)
    launch(exchange(scat=("ffn2_w_out",)))
    gb["ffn2_w_in"], _ = ordered(_tn_matmul, x2b, dh2, "dw_ffn2_in", 1024, 1408, a_t=True)
    launch(exchange(scat=("ffn2_w_in",), swap=("ple_w_gate", "ple_w_proj")))
    (dx1a, dmx, mb, dya, dyb, ds5, dgm, dgab, gs["ln2_g"], gs["ln2_b"]), _ = ordered(
        _mixout_bwd, dx2, xh2, rstd2, s5o, gm, gab, wb["up_a"], wb["up_b"], wb["mix_w_out"], row(ws["ln2_g"]), tf)
    gb["mix_w_out"], _ = ordered(_tn_matmul, mb, dmx, "dw_mix_out", 1024, 1024, a_t=True)
    gb["up_a"], _ = ordered(_tn_matmul, s5ot, dya, "dw_up_a", 512, 1024, a_t=True)
    gb["up_b"], _ = ordered(_tn_matmul, gmt, dyb, "dw_up_b", 512, 1024, a_t=True)
    launch(exchange(scat=("mix_w_out", "up_a", "up_b"), swap=("ffn2_w_out",)))
    (dza, dmr, dmi, dnr, dni, da, ddsk, dgw, dgb), _ = ordered(_s5_bwd, za, y2p, ds5, carries, sp, bsz, seq, tb)
    gb["ssm_glu_w"] = (dgw, dgw.astype(BF16))
    launch(exchange(scat=("ssm_glu_w",), swap=("ffn2_w_in",)))
    (dzuv, dws, dbias, gs["gmlp_ln_g"], gs["gmlp_ln_b"]), _ = ordered(
        _gmlp_bwd, zuv, dgm, row(ws["gmlp_ln_g"]), row(ws["gmlp_ln_b"]), wsm_b, wsmt_b, bias)
    (dx1,), _ = ordered(_mixin_bwd, dx1a, dza, dzuv, dgab, wb["mix_w_in"], min(2 * tf, t))
    g_mi, _ = ordered(_tn_matmul, x1b, dza, "dw_mix_in_a", 1024, 512, 0, 3584, a_t=True)
    g_mi, _ = ordered(_tn_matmul, x1b, dzuv, "dw_mix_in_uv", 1024, 512, 1, 3584, g_mi, a_t=True)
    gb["mix_w_in"], _ = ordered(_tn_matmul, x1b, dgab, "dw_mix_in_g", 1024, 512, 3, 3584, g_mi, a_t=True)
    launch(exchange(swap=("mix_w_out", "up_a", "up_b", "ssm_glu_w")))

    d_abr = da[0].sum(axis=0).reshape(SSM_GROUPS, SSM_STATE)
    d_abi = da[1].sum(axis=0).reshape(SSM_GROUPS, SSM_STATE)
    _, vjp = jax.vjp(_s5_discretise, ws["ssm_lambda_re"], ws["ssm_lambda_im"], ws["ssm_log_dt"],
                     ws["ssm_b_re"], ws["ssm_b_im"])
    (gs["ssm_lambda_re"], gs["ssm_lambda_im"], gs["ssm_log_dt"], gs["ssm_b_re"], gs["ssm_b_im"]) = vjp(
        (d_abr, d_abi, _block_diag_in_t(dmr), _block_diag_in_t(dmi)))
    gs["ssm_c_re"] = _block_diag_out_t(dnr)
    gs["ssm_c_im"] = _block_diag_out_t(dni)
    gs["ssm_d"] = ddsk
    gs["ssm_glu_b"] = dgb
    gs["gmlp_w_s"] = dws
    gs["gmlp_b_s"] = dbias.reshape(CHUNK, GMLP_HEADS, GMLP_HEAD_DIM).sum(axis=-1).T
    gs["loss_rows"] = loss_rows

    def small_gather(names):
        return (_gather_small([gs[k].reshape(small_shape[k]) for k in names]), gathered, names) if dist else None

    late = ("ln1_g", "ln1_b")
    launch(exchange(scat=("mix_w_in",), extra=small_gather(tuple(k for k in SMALL + ("loss_rows",) if k not in late))))
    (dx0, dh1, df1, gs["ln1_g"], gs["ln1_b"]), _ = ordered(
        _ffn_bwd, dx1, xh1, rstd1, h1, wb["ffn1_w_in"], wb["ffn1_w_out"], row(ws["ln1_g"]), tm, "ffn1_bwd")
    grad_x = dx0.reshape(bsz, seq, D_MODEL)
    if not dist:
        gb["ffn1_w_out"], _ = _tn_matmul(a1, df1, "dw_ffn1_out", 1408, 1024)
        gb["ffn1_w_in"], _ = _tn_matmul(x0b, dh1, "dw_ffn1_in", 1024, 1408, a_t=True)
        return (loss_rows, grad_x, gb, {k: gs[k].reshape(small_shape[k]) for k in SMALL}, sums, other, gathered,
                None, {})
    launch(exchange(extra=small_gather(late)))
    gb["ffn1_w_out"], _ = ordered(_tn_matmul, a1, df1, "dw_ffn1_out", 1408, 1024)
    last = ["ffn1_w_out"] + [LAST_PIECE % q for q in range(LAST_PIECES)]
    fillers = (("ffn2_w_in", "mix_w_in", "ple_w_gate"),
               ("ffn2_w_out", "mix_w_out", "up_a", "up_b", "ssm_glu_w", "ple_w_proj"))
    out = {}
    for i in range(1, len(last) + 3):
        stage = lambda d: tuple(last[i - d:i - d + 1]) if 0 <= i - d < len(last) else ()
        launch(exchange(halves=stage(1), scat2=stage(2), swap2=stage(3), swap=("mix_w_in",) if i == 2 else ()))
        if i < len(last):
            gb[last[i]], _ = ordered(_tn_matmul, x0b, dh1, "dw_" + last[i], D_MODEL // LAST_PIECES, 1408,
                                     a_cols=(i - 1, 1), a_t=True)
        elif i - len(last) < len(fillers):
            for k in fillers[i - len(last)]:
                w, m, v = opt[k]
                out[k] = _adam_big(w, sums[k], other[k], m, v, "adam_" + k, after=order[0])
                order[0] = out[k][1]
    return loss_rows, grad_x, gb, gs, sums, other, gathered, ids, out


def _adamw(w, g, m, v):
    m = ADAM_B1 * m + (1.0 - ADAM_B1) * g
    v = ADAM_B2 * v + (1.0 - ADAM_B2) * (g * g)
    m_hat = m / ADAM_C1
    v_hat = v / ADAM_C2
    delta = -ADAM_LR * (m_hat / (jnp.sqrt(v_hat) + ADAM_EPS) + ADAM_WD * w)
    return delta, m, v


def _pinned(after):
    return ([pl.BlockSpec(memory_space=pl.ANY)], [after]) if after is not None else ([], [])


def _sum_blocks(part, recv, shape, axis, chip, name, after=None):
    r, c = shape
    rb = r // ROW_STEPS

    def body(chip_ref, p_ref, r_ref, *rest):
        rest[-1][...] = (p_ref[...] + r_ref[0].astype(F32) + r_ref[1].astype(F32) + r_ref[2].astype(F32))

    if axis == 0:
        own = pl.BlockSpec((rb, c), lambda i, k: (k[0] * ROW_STEPS + i, 0))
    else:
        own = pl.BlockSpec((rb, c), lambda i, k: (i, k[0]))
    pin_specs, pin_args = _pinned(after)
    grid_spec = pltpu.PrefetchScalarGridSpec(
        num_scalar_prefetch=1, grid=(ROW_STEPS,),
        in_specs=[own, pl.BlockSpec((3, rb, c), lambda i, k: (0, i, 0))] + pin_specs,
        out_specs=pl.BlockSpec((rb, c), lambda i, k: (i, 0)))
    return pl.pallas_call(body, name=name, out_shape=SDS((r, c), F32), grid_spec=grid_spec,
                          compiler_params=_params(("parallel",)))(chip, part, recv, *pin_args)


def _presum(part, half, shape, axis, ids, name, after=None):
    r, c = shape
    rb = r // 2

    def body(ids_ref, p_ref, h_ref, *rest):
        of_ref, ob_ref = rest[-2:]
        s = p_ref[...] + h_ref[...].astype(F32)
        ob_ref[...] = s.astype(BF16)

        @pl.when(pl.program_id(1) == 0)
        def _():
            of_ref[...] = s

    if axis == 0:
        p_spec = pl.BlockSpec((rb, c), lambda i, t, ids: (ids[t] * 2 + ids[4] + i, 0))
        h_spec = pl.BlockSpec((None, rb, c), lambda i, t, ids: (ids[t], i, 0))
    else:
        p_spec = pl.BlockSpec((rb, c), lambda i, t, ids: (ids[4] + i, ids[t]))
        h_spec = pl.BlockSpec((rb, c), lambda i, t, ids: (i, ids[t]))
    pin_specs, pin_args = _pinned(after)
    grid_spec = pltpu.PrefetchScalarGridSpec(
        num_scalar_prefetch=1, grid=(1, 4), in_specs=[p_spec, h_spec] + pin_specs,
        out_specs=(pl.BlockSpec((rb, c), lambda i, t, ids: (i, 0)),
                   pl.BlockSpec((None, rb, c), lambda i, t, ids: (t, i, 0))))
    return pl.pallas_call(body, name=name, out_shape=(SDS((r // 2, c), F32), SDS((4, r // 2, c), BF16)),
                          grid_spec=grid_spec,
                          compiler_params=_params(("parallel", "arbitrary")))(ids, part, half, *pin_args)


def _sum_half(pre, recv, name, after=None):
    hr, c = pre.shape
    rb = hr

    def body(p_ref, r_ref, *rest):
        rest[-1][...] = (p_ref[...] + r_ref[0].astype(F32) + r_ref[1].astype(F32) + r_ref[2].astype(F32))

    spec = pl.BlockSpec((rb, c), lambda i: (i, 0))
    pin_specs, pin_args = _pinned(after)
    return pl.pallas_call(body, name=name, grid=(1,), out_shape=SDS((hr, c), F32),
                          in_specs=[spec, pl.BlockSpec((3, rb, c), lambda i: (0, i, 0))] + pin_specs,
                          out_specs=spec, compiler_params=_params(("parallel",)))(pre, recv, *pin_args)


def _adam_halves(w, mine, oth, m, v, ids, name, piece=0, prev=None):
    r, c = w.shape
    rb = mine.shape[0] // 2

    def body(ids_ref, w_ref, a_ref, b_ref, m_ref, v_ref, *rest):
        g_ref, d_ref, nm_ref, nv_ref = rest[-4:]
        g = jnp.where(pl.program_id(0) // 2 == ids_ref[4], a_ref[...], b_ref[...])
        g_ref[...] = g
        d_ref[...], nm_ref[...], nv_ref[...] = _adamw(w_ref[...], g, m_ref[...], v_ref[...])

    whole = pl.BlockSpec((rb, c), lambda i, ids: (i + 4 * piece, 0))
    part = pl.BlockSpec((rb, c), lambda i, ids: (i % 2, 0))
    in_specs = [whole, part, part, whole, whole]
    args = [w, mine, oth, m, v]
    aliases = {}
    if prev is not None:
        in_specs += [pl.BlockSpec(memory_space=pl.ANY)] * 4
        args += list(prev)
        aliases = {6: 0, 7: 1, 8: 2, 9: 3}
    grid_spec = pltpu.PrefetchScalarGridSpec(num_scalar_prefetch=1, grid=(4,), in_specs=in_specs,
                                             out_specs=(whole,) * 4)
    return pl.pallas_call(body, name=name, out_shape=tuple(SDS((r, c), F32) for _ in range(4)),
                          grid_spec=grid_spec, input_output_aliases=aliases,
                          compiler_params=_params(("parallel",)))(ids, *args)


def _adam_big(w, ga, gb, m, v, name, piece=0, prev=None, after=None):
    r, c = w.shape
    pr = ga.shape[0]
    steps = ROW_STEPS if pr == r else 2
    rb = pr // steps
    off = piece * steps

    def body(w_ref, ga_ref, gb_ref, m_ref, v_ref, *rest):
        g_ref, d_ref, nm_ref, nv_ref = rest[-4:]
        g = ga_ref[...] + gb_ref[...]
        g_ref[...] = g
        d_ref[...], nm_ref[...], nv_ref[...] = _adamw(w_ref[...], g, m_ref[...], v_ref[...])

    whole = pl.BlockSpec((rb, c), lambda i: (i + off, 0))
    part = pl.BlockSpec((rb, c), lambda i: (i, 0))
    in_specs = [whole, part, part, whole, whole]
    args = [w, ga, gb, m, v]
    aliases = {}
    if prev is not None:
        in_specs += [pl.BlockSpec(memory_space=pl.ANY)] * 4
        args += list(prev)
        aliases = {5: 0, 6: 1, 7: 2, 8: 3}
    if after is not None:
        in_specs.append(pl.BlockSpec(memory_space=pl.ANY))
        args.append(after)
    return pl.pallas_call(
        body, name=name, grid=(steps,), out_shape=tuple(SDS((r, c), F32) for _ in range(4)),
        in_specs=in_specs, out_specs=(whole,) * 4, input_output_aliases=aliases,
        compiler_params=_params(("parallel",)),
    )(*args)


def _adam_small(ws, gathered, ms, vs):
    n = len(ws)

    def body(*refs):
        w_refs, g_refs, m_refs, v_refs = refs[:n], refs[n:2 * n], refs[2 * n:3 * n], refs[3 * n:4 * n]
        outs = refs[4 * n:]
        for i in range(n):
            g = g_refs[i][0]
            for d in range(1, N_DEV):
                g = g + g_refs[i][d]
            delta, nm, nv = _adamw(w_refs[i][...], g, m_refs[i][...], v_refs[i][...])
            outs[i][...] = g
            outs[n + i][...] = delta
            outs[2 * n + i][...] = nm
            outs[3 * n + i][...] = nv

    vmem = pl.BlockSpec(memory_space=pltpu.VMEM)
    shapes = [w.shape for w in ws]
    return pl.pallas_call(
        body, name="adam_small", out_shape=tuple(SDS(s, F32) for s in shapes * 4),
        in_specs=[vmem] * (4 * n), out_specs=tuple([vmem] * (4 * n)),
        compiler_params=pltpu.CompilerParams(vmem_limit_bytes=VMEM_LIMIT_BYTES),
    )(*ws, *gathered, *ms, *vs)


def _sum_loss(gathered):
    def body(g_ref, o_ref):
        tot = g_ref[0]
        for d in range(1, N_DEV):
            tot = tot + g_ref[d]
        o_ref[...] = (0.5 / D_MODEL) * jnp.sum(tot, axis=1, keepdims=True)

    vmem = pl.BlockSpec(memory_space=pltpu.VMEM)
    return pl.pallas_call(body, name="sum_loss", out_shape=SDS((1, 1), F32), in_specs=[vmem],
                          out_specs=vmem)(gathered)


def kernel(x, p, ffn1_w_in, ffn1_w_out, ln1_g, ln1_b, mix_w_in, ssm_lambda_re, ssm_lambda_im, ssm_log_dt, ssm_b_re, ssm_b_im, ssm_c_re, ssm_c_im, ssm_d, ssm_glu_w, ssm_glu_b, gmlp_ln_g, gmlp_ln_b, gmlp_w_s, gmlp_b_s, up_a, up_b, mix_w_out, ln2_g, ln2_b, ffn2_w_in, ffn2_w_out, ln3_g, ln3_b, ple_w_proj, ple_w_gate, loss_target, m_ffn1_w_in, m_ffn1_w_out, m_ln1_g, m_ln1_b, m_mix_w_in, m_ssm_lambda_re, m_ssm_lambda_im, m_ssm_log_dt, m_ssm_b_re, m_ssm_b_im, m_ssm_c_re, m_ssm_c_im, m_ssm_d, m_ssm_glu_w, m_ssm_glu_b, m_gmlp_ln_g, m_gmlp_ln_b, m_gmlp_w_s, m_gmlp_b_s, m_up_a, m_up_b, m_mix_w_out, m_ln2_g, m_ln2_b, m_ffn2_w_in, m_ffn2_w_out, m_ln3_g, m_ln3_b, m_ple_w_proj, m_ple_w_gate, v_ffn1_w_in, v_ffn1_w_out, v_ln1_g, v_ln1_b, v_mix_w_in, v_ssm_lambda_re, v_ssm_lambda_im, v_ssm_log_dt, v_ssm_b_re, v_ssm_b_im, v_ssm_c_re, v_ssm_c_im, v_ssm_d, v_ssm_glu_w, v_ssm_glu_b, v_gmlp_ln_g, v_gmlp_ln_b, v_gmlp_w_s, v_gmlp_b_s, v_up_a, v_up_b, v_mix_w_out, v_ln2_g, v_ln2_b, v_ffn2_w_in, v_ffn2_w_out, v_ln3_g, v_ln3_b, v_ple_w_proj, v_ple_w_gate):
    given = dict(locals())
    order = ("ffn1_w_in", "ffn1_w_out", "ln1_g", "ln1_b", "mix_w_in", "ssm_lambda_re", "ssm_lambda_im",
             "ssm_log_dt", "ssm_b_re", "ssm_b_im", "ssm_c_re", "ssm_c_im", "ssm_d", "ssm_glu_w", "ssm_glu_b",
             "gmlp_ln_g", "gmlp_ln_b", "gmlp_w_s", "gmlp_b_s", "up_a", "up_b", "mix_w_out", "ln2_g", "ln2_b",
             "ffn2_w_in", "ffn2_w_out", "ln3_g", "ln3_b", "ple_w_proj", "ple_w_gate")
    assert set(order) == set(BIG + SMALL)

    shard = {k: given[k][0] for k in BIG}
    shard_b = {k: shard[k].astype(BF16) for k in BIG}
    opt = {k: (shard[k], given["m_" + k][0], given["v_" + k][0]) for k in BIG}
    loss_rows, grad_x, gb, gs, sums, other, gathered, ids, out = _local_step(
        x, given["p"][0], loss_target, {}, {k: given[k] for k in SMALL}, shard_b, opt)

    out = dict(out)
    for k in BIG:
        if k in out:
            continue
        moments = (given["m_" + k][0], given["v_" + k][0])
        if k == "ffn1_w_out":
            out[k] = _adam_halves(shard[k], sums[k], other[k], *moments, ids, "adam_" + k)
        elif k == "ffn1_w_in":
            for q in range(LAST_PIECES):
                kq = LAST_PIECE % q
                out[k] = _adam_halves(shard[k], sums[kq], other[kq], *moments, ids, "adam_" + kq, q, out.get(k))
        else:
            out[k] = _adam_big(shard[k], sums[k], other[k], *moments, "adam_" + k,
                               after=gb[LAST_PIECE % (LAST_PIECES - 1)][0])

    res = _adam_small([_small_view(k, given[k]) for k in SMALL], [gathered[k] for k in SMALL],
                      [_small_view(k, given["m_" + k]) for k in SMALL],
                      [_small_view(k, given["v_" + k]) for k in SMALL])
    ns = len(SMALL)
    for i, k in enumerate(SMALL):
        out[k] = tuple(res[j * ns + i].reshape(given[k].shape) for j in range(4))
    loss = _sum_loss(gathered["loss_rows"]).reshape(())

    lead = lambda k, j: out[k][j][None] if k in BIG else out[k][j]
    return (loss, grad_x, *[lead(k, 0) for k in order], *[lead(k, 1) for k in order],
            *[lead(k, 2) for k in order], *[lead(k, 3) for k in order])
```

```python
import math

import jax
import jax.numpy as jnp
from jax import lax
from jax.experimental import pallas as pl
from jax.experimental.pallas import tpu as pltpu
from jax.experimental.pallas import tpu_sc as plsc

F32 = jnp.float32
BF16 = jnp.bfloat16
MESH = pl.DeviceIdType.MESH
SDS = jax.ShapeDtypeStruct

D_MODEL = 1024
D_FF = 2816
D_SSM = 512
D_GMLP = 512
SSM_GROUPS = 32
SSM_GROUP_CH = 16
SSM_STATE = 64
SSM_LANES = SSM_GROUPS * SSM_STATE
GMLP_HEADS = 8
GMLP_HEAD_DIM = 64
CHUNK = 128
PLE_DIM = 256
LN_EPS = 1e-5
ALPHA = 2.0 ** 0.25

ADAM_LR = 0.001
ADAM_B1 = 0.9
ADAM_B2 = 0.999
ADAM_EPS = 1e-08
ADAM_WD = 0.01
ADAM_STEP = 10
ADAM_C1 = 1.0 - ADAM_B1 ** ADAM_STEP
ADAM_C2 = 1.0 - ADAM_B2 ** ADAM_STEP

N_DEV = 8
VMEM_LIMIT_BYTES = 56 * 1024 * 1024
FFN_COLS = 1408
S5_BLOCKS = 4
S5_BLOCK_IN = D_SSM // S5_BLOCKS
S5_BLOCK_ST = SSM_LANES // S5_BLOCKS
SCAN_LANES = 512
S5_TIME_BLOCK = 512
TN_K_BLOCK = 2048
TN_SMALL_BLOCK = 1024 * 1024
GMLP_CHUNKS = 8
ROW_STEPS = 4
LAST_PIECES = 1
LAST_PIECE = "ffn1_w_in_q%d"
_G0 = math.sqrt(2.0 / math.pi)
_G1 = 0.044715


def _dot(a, b):
    return jnp.dot(a, b, preferred_element_type=F32)


def _dot_nt(a, b):
    return lax.dot_general(a, b, (((1,), (1,)), ((), ())), preferred_element_type=F32)


def _dot_tn(a, b):
    return lax.dot_general(a, b, (((0,), (0,)), ((), ())), preferred_element_type=F32)


def _sigmoid(x):
    return 1.0 / (1.0 + jnp.exp(-x))


def _gelu(x):
    t = jnp.tanh(_G0 * (x + _G1 * x * x * x))
    return 0.5 * x * (1.0 + t)


def _gelu_grad(x):
    t = jnp.tanh(_G0 * (x + _G1 * x * x * x))
    return 0.5 * (1.0 + t) + 0.5 * x * (1.0 - t * t) * _G0 * (1.0 + 3.0 * _G1 * x * x)


def _ln_fwd(r, g, b):
    mu = jnp.mean(r, axis=-1, keepdims=True)
    d = r - mu
    var = jnp.mean(d * d, axis=-1, keepdims=True)
    rstd = lax.rsqrt(var + LN_EPS)
    xh = d * rstd
    return xh * g + b, xh, rstd


def _ln_bwd(dy, xh, rstd, g):
    dxh = dy * g
    m1 = jnp.mean(dxh, axis=-1, keepdims=True)
    m2 = jnp.mean(dxh * xh, axis=-1, keepdims=True)
    return rstd * (dxh - m1 - xh * m2)


def _resident(shape):
    nd = len(shape)
    return pl.BlockSpec(shape, lambda *_: (0,) * nd, pipeline_mode=pl.Buffered(1))


def _fixed(shape):
    nd = len(shape)
    return pl.BlockSpec(shape, lambda *_: (0,) * nd)


def _rows(tm, cols):
    return pl.BlockSpec((tm, cols), lambda i: (i, 0))


def _cols(rows, tm):
    return pl.BlockSpec((rows, tm), lambda i: (0, i))


def _params(sem):
    return pltpu.CompilerParams(dimension_semantics=sem, vmem_limit_bytes=VMEM_LIMIT_BYTES)


class _Exchange:
    def __init__(self, args, out_shape, sems, start, finish):
        self.args, self.out_shape, self.sems = list(args), list(out_shape), list(sems)
        self.start, self.finish = start, finish
        self.cuts = [(0, len(self.out_shape))]


def _call(body, name, grid, in_specs, out_specs, out_shape, args, scratch=(), sem=None, bg=None, aliases=None):
    aliases = {} if aliases is None else aliases
    in_specs, args = list(in_specs), list(args)
    fn = body
    if bg is not None:
        n_args = len(args)

        def fn(*refs):
            body(*refs[:n_args], *refs[n_args + 1:])

        in_specs.append(pl.BlockSpec(memory_space=pl.ANY))
        args.append(bg)
    res = pl.pallas_call(fn, name=name, grid=grid, out_shape=tuple(out_shape), in_specs=in_specs,
                         out_specs=tuple(out_specs), scratch_shapes=list(scratch),
                         input_output_aliases=aliases, compiler_params=_params(sem))(*args)
    return tuple(res), ()


def _run_exchange_on_sequencer(ex, name, collective_id):
    n_i, n_o = len(ex.args), len(ex.out_shape)

    def body(*refs):
        ins, outs, sems = refs[:n_i], refs[n_i:n_i + n_o], refs[n_i + n_o:]
        x, y, c = lax.axis_index("x"), lax.axis_index("y"), lax.axis_index("c")
        barrier = pltpu.get_barrier_semaphore()
        for peer in [(x, y, 1 - c), (1 - x, y, c), (x, 1 - y, c), (1 - x, 1 - y, c)]:
            pl.semaphore_signal(barrier, inc=1, device_id=peer, device_id_type=MESH)
        pl.semaphore_wait(barrier, 4)
        ex.start(ins, outs, sems)
        ex.finish(ins, outs, sems)

    return tuple(pl.kernel(body, out_type=tuple(ex.out_shape),
                           mesh=plsc.ScalarSubcoreMesh(axis_name="sequencer", num_cores=1),
                           scratch_types=list(ex.sems), name=name,
                           compiler_params=pltpu.CompilerParams(collective_id=collective_id))(*ex.args))


def _join(exchanges):
    cuts = []
    a = o = q = 0
    for e in exchanges:
        cuts.append((a, a + len(e.args), o, o + len(e.out_shape), q, q + len(e.sems)))
        a, o, q = cuts[-1][1], cuts[-1][3], cuts[-1][5]

    def start(ins, outs, sems):
        for e, (a0, a1, o0, o1, q0, q1) in zip(exchanges, cuts):
            e.start(ins[a0:a1], outs[o0:o1], sems[q0:q1])

    def finish(ins, outs, sems):
        for e, (a0, a1, o0, o1, q0, q1) in zip(exchanges, cuts):
            e.finish(ins[a0:a1], outs[o0:o1], sems[q0:q1])

    joined = _Exchange(sum((e.args for e in exchanges), []), sum((e.out_shape for e in exchanges), []),
                       sum((e.sems for e in exchanges), []), start, finish)
    joined.cuts = [(c[2], c[3]) for c in cuts]
    return joined


def _ffn_proj(x, w_in, tm, name, bg=None):
    t = x.shape[0]
    nch = D_FF // FFN_COLS

    def body(x_ref, win_ref, xbt_ref, h_ref, a_ref):
        xb = x_ref[...].astype(BF16)
        xbt_ref[...] = xb.T
        for k in range(nch):
            cg = slice(k * FFN_COLS, (k + 1) * FFN_COLS)
            cu = slice(D_FF + k * FFN_COLS, D_FF + (k + 1) * FFN_COLS)
            hg = _dot(xb, win_ref[k])
            hu = _dot(xb, win_ref[nch + k])
            h_ref[:, cg] = hg.astype(BF16)
            h_ref[:, cu] = hu.astype(BF16)
            a_ref[:, cg] = (hg * _sigmoid(hg) * hu).astype(BF16)

    return _call(
        body, name, (t // tm,),
        [_rows(tm, D_MODEL), _resident((2 * nch, D_MODEL, FFN_COLS))],
        (_cols(D_MODEL, tm), _rows(tm, 2 * D_FF), _rows(tm, D_FF)),
        (SDS((D_MODEL, t), BF16), SDS((t, 2 * D_FF), BF16), SDS((t, D_FF), BF16)),
        (x, w_in), sem=("parallel",), bg=bg)


def _ffn_out(x, a, w_out, g, b, tm, name, bg=None):
    t = x.shape[0]

    def body(x_ref, a_ref, wout_ref, g_ref, b_ref, xn_ref, xh_ref, rstd_ref):
        f = _dot(a_ref[...], wout_ref[...])
        y, xh, rstd = _ln_fwd(ALPHA * x_ref[...] + 0.5 * f, g_ref[...], b_ref[...])
        xn_ref[...] = y
        xh_ref[...] = xh
        rstd_ref[...] = rstd

    return _call(
        body, name, (t // tm,),
        [_rows(tm, D_MODEL), _rows(tm, D_FF), _resident((D_FF, D_MODEL)), _fixed((1, D_MODEL)), _fixed((1, D_MODEL))],
        (_rows(tm, D_MODEL), _rows(tm, D_MODEL), _rows(tm, 1)),
        (SDS((t, D_MODEL), F32), SDS((t, D_MODEL), F32), SDS((t, 1), F32)),
        (x, a, w_out, g, b), sem=("parallel",), bg=bg)


def _ffn_out_loss(x, a, w_out, g, b, p, tgt, wpg, wpp, tm):
    t = x.shape[0]

    def body(x_ref, a_ref, wout_ref, g_ref, b_ref, p_ref, t_ref, wpg_ref, wpp_ref,
             xh_ref, rstd_ref, dx_ref, xbt_ref, pbt_ref, dq_ref, de_ref, loss_ref):
        @pl.when(pl.program_id(0) == 0)
        def _():
            loss_ref[...] = jnp.zeros_like(loss_ref)

        f = _dot(a_ref[...], wout_ref[...])
        x3v, xh, rstd = _ln_fwd(ALPHA * x_ref[...] + 0.5 * f, g_ref[...], b_ref[...])
        xh_ref[...] = xh
        rstd_ref[...] = rstd
        xb = x3v.astype(BF16)
        pb = p_ref[...].astype(BF16)
        xbt_ref[...] = xb.T
        pbt_ref[...] = pb.T
        s = _sigmoid(_dot(xb, wpg_ref[...]))
        e = _dot(pb, wpp_ref[...])
        diff = x3v + s * e - t_ref[...]
        loss_ref[...] += jnp.sum(diff * diff, axis=0, keepdims=True)
        dout = diff * (1.0 / D_MODEL)
        de_ref[...] = (dout * s).astype(BF16)
        dq = (dout * e * s * (1.0 - s)).astype(BF16)
        dq_ref[...] = dq
        dx_ref[...] = dout + _dot_nt(dq, wpg_ref[...])

    return _call(
        body, "ffn2_out_loss", (t // tm,),
        [_rows(tm, D_MODEL), _rows(tm, D_FF), _resident((D_FF, D_MODEL)), _fixed((1, D_MODEL)), _fixed((1, D_MODEL)),
         _rows(tm, PLE_DIM), _rows(tm, D_MODEL), _resident((D_MODEL, D_MODEL)), _resident((PLE_DIM, D_MODEL))],
        (_rows(tm, D_MODEL), _rows(tm, 1), _rows(tm, D_MODEL), _cols(D_MODEL, tm), _cols(PLE_DIM, tm),
         _rows(tm, D_MODEL), _rows(tm, D_MODEL), _fixed((1, D_MODEL))),
        (SDS((t, D_MODEL), F32), SDS((t, 1), F32), SDS((t, D_MODEL), F32), SDS((D_MODEL, t), BF16),
         SDS((PLE_DIM, t), BF16), SDS((t, D_MODEL), BF16), SDS((t, D_MODEL), BF16), SDS((1, D_MODEL), F32)),
        (x, a, w_out, g, b, p, tgt, wpg, wpp), sem=("arbitrary",))


def _ffn_bwd(dxn, xh, rstd, h, w_in, w_out, g, tm, name, bg=None):
    t = dxn.shape[0]
    nch = D_FF // FFN_COLS

    def body(dxn_ref, xh_ref, rstd_ref, h_ref, win_ref, wout_ref, g_ref,
             dx_ref, dh_ref, df_ref, dg_ref, db_ref):
        @pl.when(pl.program_id(0) == 0)
        def _():
            dg_ref[...] = jnp.zeros_like(dg_ref)
            db_ref[...] = jnp.zeros_like(db_ref)

        dy = dxn_ref[...]
        xhv = xh_ref[...]
        dr = _ln_bwd(dy, xhv, rstd_ref[...], g_ref[...])
        dg_ref[...] += jnp.sum(dy * xhv, axis=0, keepdims=True)
        db_ref[...] += jnp.sum(dy, axis=0, keepdims=True)
        df = (0.5 * dr).astype(BF16)
        df_ref[...] = df
        dx = ALPHA * dr
        das = [_dot_nt(df, wout_ref[k * FFN_COLS:(k + 1) * FFN_COLS, :]) for k in range(nch)]
        for k in range(nch):
            cg = slice(k * FFN_COLS, (k + 1) * FFN_COLS)
            cu = slice(D_FF + k * FFN_COLS, D_FF + (k + 1) * FFN_COLS)
            hg = h_ref[:, cg].astype(F32)
            hu = h_ref[:, cu].astype(F32)
            sg = _sigmoid(hg)
            silu = hg * sg
            da = das[k]
            dhu = (da * silu).astype(BF16)
            dhg = (da * hu * (sg * (1.0 + hg * (1.0 - sg)))).astype(BF16)
            dh_ref[:, cg] = dhg
            dh_ref[:, cu] = dhu
            dx = dx + _dot_nt(dhg, win_ref[k]) + _dot_nt(dhu, win_ref[nch + k])
        dx_ref[...] = dx

    return _call(
        body, name, (t // tm,),
        [_rows(tm, D_MODEL), _rows(tm, D_MODEL), _rows(tm, 1), _rows(tm, 2 * D_FF),
         _resident((2 * nch, D_MODEL, FFN_COLS)), _resident((D_FF, D_MODEL)), _fixed((1, D_MODEL))],
        (_rows(tm, D_MODEL), _rows(tm, 2 * D_FF), _rows(tm, D_MODEL),
         _fixed((1, D_MODEL)), _fixed((1, D_MODEL))),
        (SDS((t, D_MODEL), F32), SDS((t, 2 * D_FF), BF16), SDS((t, D_MODEL), BF16),
         SDS((1, D_MODEL), F32), SDS((1, D_MODEL), F32)),
        (dxn, xh, rstd, h, w_in, w_out, g), sem=("arbitrary",), bg=bg)


def _tn_matmul(a, b, name, bm, bn, col_block=0, total_cols=None, prev=None, bg=None, a_cols=None, a_t=False):
    t, m = a.shape[::-1] if a_t else a.shape
    a_first = 0
    if a_cols is not None:
        a_first, m = a_cols[0], a_cols[1] * bm
    n = b.shape[1]
    total_cols = n if total_cols is None else total_cols
    whole = bm * bn <= TN_SMALL_BLOCK and (m // bm) * (n // bn) >= 2
    bk = min(2 * TN_K_BLOCK if whole else TN_K_BLOCK, t)
    nk = t // bk
    n_in = 2 if prev is None else 4

    def body(*refs):
        a_ref, b_ref = refs[0], refs[1]
        o_ref, ob_ref = refs[n_in], refs[n_in + 1]
        k = pl.program_id(2)

        @pl.when(k == 0)
        def _():
            o_ref[...] = jnp.zeros_like(o_ref)

        o_ref[...] += _dot(a_ref[...], b_ref[...]) if a_t else _dot_tn(a_ref[...], b_ref[...])

        @pl.when(k == nk - 1)
        def _():
            ob_ref[...] = o_ref[...].astype(BF16)

    a_spec = (pl.BlockSpec((bm, bk), lambda i, j, k: (i + a_first, k)) if a_t
              else pl.BlockSpec((bk, bm), lambda i, j, k: (k, i + a_first)))
    in_specs = [a_spec, pl.BlockSpec((bk, bn), lambda i, j, k: (k, j))]
    args = [a, b]
    aliases = {}
    if prev is not None:
        in_specs += [pl.BlockSpec(memory_space=pl.ANY), pl.BlockSpec(memory_space=pl.ANY)]
        args += list(prev)
        aliases = {2: 0, 3: 1}
        if any(bg is p for p in prev):
            bg = None
    out_spec = pl.BlockSpec((bm, bn), lambda i, j, k: (i, j + col_block))
    return _call(body, name, (m // bm, n // bn, nk), in_specs, (out_spec, out_spec),
                 (SDS((m, total_cols), F32), SDS((m, total_cols), BF16)), args,
                 sem=("parallel", "parallel", "arbitrary"), bg=bg, aliases=aliases)


def _mixin_fwd(x1, w, tm, bg=None):
    t = x1.shape[0]

    def body(x_ref, w_ref, xbt_ref, za_ref, zuv_ref, gab_ref):
        xb = x_ref[...].astype(BF16)
        xbt_ref[...] = xb.T
        za_ref[...] = _dot(xb, w_ref[:, 0:512]).astype(BF16)
        zuv_ref[...] = _dot(xb, w_ref[:, 512:1536]).astype(BF16)
        gab_ref[...] = _dot(xb, w_ref[:, 1536:3584]).astype(BF16)

    return _call(
        body, "mixin_fwd", (t // tm,),
        [_rows(tm, D_MODEL), _resident((D_MODEL, 3584))],
        (_cols(D_MODEL, tm), _rows(tm, 512), _rows(tm, 1024), _rows(tm, 2048)),
        (SDS((D_MODEL, t), BF16), SDS((t, 512), BF16), SDS((t, 1024), BF16), SDS((t, 2048), BF16)),
        (x1, w), sem=("parallel",), bg=bg)


def _mixin_bwd(dx1a, dza, dzuv, dgab, w, tm, bg=None):
    t = dx1a.shape[0]

    def body(d_ref, dza_ref, dzuv_ref, dgab_ref, w_ref, dx_ref):
        dx_ref[...] = (d_ref[...] + _dot_nt(dza_ref[...], w_ref[:, 0:512])
                       + _dot_nt(dzuv_ref[...], w_ref[:, 512:1536])
                       + _dot_nt(dgab_ref[...], w_ref[:, 1536:3584]))

    return _call(
        body, "mixin_bwd", (t // tm,),
        [_rows(tm, D_MODEL), _rows(tm, 512), _rows(tm, 1024), _rows(tm, 2048), _resident((D_MODEL, 3584))],
        (_rows(tm, D_MODEL),), (SDS((t, D_MODEL), F32),),
        (dx1a, dza, dzuv, dgab, w), sem=("parallel",), bg=bg)


def _unrolled(lo, hi, body, carry):
    for j in range(lo, hi):
        carry = body(j, carry)
    return carry


def _scan_fwd(hr_ref, hi_ref, a_ref, ap_ref, carry_ref, seg, cin_ref):
    for lc in range(SSM_LANES // SCAN_LANES):
        ls = slice(lc * SCAN_LANES, (lc + 1) * SCAN_LANES)
        a_r = jnp.broadcast_to(a_ref[0:1, ls], (8, SCAN_LANES))
        a_i = jnp.broadcast_to(a_ref[1:2, ls], (8, SCAN_LANES))

        def step(j, hc, ls=ls, a_r=a_r, a_i=a_i):
            h_r, h_i = hc
            rows = pl.ds(j * 8, 8)
            n_r = a_r * h_r - a_i * h_i + hr_ref[rows, ls]
            n_i = a_r * h_i + a_i * h_r + hi_ref[rows, ls]
            hr_ref[rows, ls] = n_r
            hi_ref[rows, ls] = n_i
            return n_r, n_i

        zero = jnp.zeros((8, SCAN_LANES), F32)
        f_r, f_i = _unrolled(0, seg, step, (zero, zero))
        c_r = carry_ref[0:1, ls]
        c_i = carry_ref[1:2, ls]
        p_r = ap_ref[0:1, ls]
        p_i = ap_ref[1:2, ls]
        rows_r, rows_i = [], []
        for s in range(8):
            rows_r.append(c_r)
            rows_i.append(c_i)
            c_r, c_i = (f_r[s:s + 1] + p_r * c_r - p_i * c_i,
                        f_i[s:s + 1] + p_r * c_i + p_i * c_r)
        carry_ref[0:1, ls] = c_r
        carry_ref[1:2, ls] = c_i
        cin_r = jnp.concatenate(rows_r, axis=0)
        cin_i = jnp.concatenate(rows_i, axis=0)
        if cin_ref is not None:
            cin_ref[0, :, ls] = cin_r
            cin_ref[1, :, ls] = cin_i

        def fix(j, cc, ls=ls, a_r=a_r, a_i=a_i):
            c_r, c_i = cc
            c_r, c_i = a_r * c_r - a_i * c_i, a_r * c_i + a_i * c_r
            rows = pl.ds(j * 8, 8)
            hr_ref[rows, ls] = hr_ref[rows, ls] + c_r
            hi_ref[rows, ls] = hi_ref[rows, ls] + c_i
            return c_r, c_i

        _unrolled(0, seg, fix, (cin_r, cin_i))


def _scan_bwd(gr_ref, gi_ref, hr_ref, hi_ref, cin_ref, a_ref, ap_ref, rcarry_ref, da_ref, seg):
    for lc in range(SSM_LANES // SCAN_LANES):
        ls = slice(lc * SCAN_LANES, (lc + 1) * SCAN_LANES)
        a_r = jnp.broadcast_to(a_ref[0:1, ls], (8, SCAN_LANES))
        a_i = jnp.broadcast_to(a_ref[1:2, ls], (8, SCAN_LANES))

        def step(t, gc, ls=ls, a_r=a_r, a_i=a_i):
            g_r, g_i = gc
            rows = pl.ds((seg - 1 - t) * 8, 8)
            n_r = gr_ref[rows, ls] + a_r * g_r + a_i * g_i
            n_i = gi_ref[rows, ls] + a_r * g_i - a_i * g_r
            gr_ref[rows, ls] = n_r
            gi_ref[rows, ls] = n_i
            return n_r, n_i

        zero = jnp.zeros((8, SCAN_LANES), F32)
        f_r, f_i = _unrolled(0, seg, step, (zero, zero))
        c_r = rcarry_ref[0:1, ls]
        c_i = rcarry_ref[1:2, ls]
        p_r = ap_ref[0:1, ls]
        p_i = ap_ref[1:2, ls]
        rows_r, rows_i = [None] * 8, [None] * 8
        for s in range(7, -1, -1):
            rows_r[s] = c_r
            rows_i[s] = c_i
            c_r, c_i = (f_r[s:s + 1] + p_r * c_r + p_i * c_i,
                        f_i[s:s + 1] + p_r * c_i - p_i * c_r)
        rcarry_ref[0:1, ls] = c_r
        rcarry_ref[1:2, ls] = c_i
        cin_r = jnp.concatenate(rows_r, axis=0)
        cin_i = jnp.concatenate(rows_i, axis=0)

        def fix_row(j_rows, hp_r, hp_i, cc, ls=ls, a_r=a_r, a_i=a_i):
            c_r, c_i, acc_r, acc_i = cc
            c_r, c_i = a_r * c_r + a_i * c_i, a_r * c_i - a_i * c_r
            g_r = gr_ref[j_rows, ls] + c_r
            g_i = gi_ref[j_rows, ls] + c_i
            gr_ref[j_rows, ls] = g_r
            gi_ref[j_rows, ls] = g_i
            acc_r = acc_r + g_r * hp_r + g_i * hp_i
            acc_i = acc_i + g_i * hp_r - g_r * hp_i
            return c_r, c_i, acc_r, acc_i

        def fix(t, cc, ls=ls, fix_row=fix_row):
            j = seg - 1 - t
            rows = pl.ds(j * 8, 8)
            prev = pl.ds((j - 1) * 8, 8)
            return fix_row(rows, hr_ref[prev, ls], hi_ref[prev, ls], cc)

        cc = _unrolled(0, seg - 1, fix, (cin_r, cin_i, zero, zero))
        _, _, acc_r, acc_i = fix_row(pl.ds(0, 8), cin_ref[0, :, ls], cin_ref[1, :, ls], cc)
        da_ref[0, :, ls] += acc_r
        da_ref[1, :, ls] += acc_i


def _s5_fwd(za, sp, bsz, seq, tb, bg=None):
    nb = seq // tb
    seg = tb // 8
    t = bsz * seq

    def body(za_ref, perm_ref, permt_ref, mre_ref, mim_ref, nre_ref, nim_ref, a_ref, ap_ref,
             dsk_ref, gw_ref, gb_ref, out_ref, outt_ref, y2_ref, car_ref, hr_ref, hi_ref, carry_ref):
        @pl.when(pl.program_id(1) == 0)
        def _():
            carry_ref[...] = jnp.zeros_like(carry_ref)

        car_ref[0] = carry_ref[...]
        up = _dot(perm_ref[...], za_ref[...])
        upb = up.astype(BF16)
        for bb in range(S5_BLOCKS):
            ub = upb[:, bb * S5_BLOCK_IN:(bb + 1) * S5_BLOCK_IN]
            st = slice(bb * S5_BLOCK_ST, (bb + 1) * S5_BLOCK_ST)
            hr_ref[:, st] = _dot(ub, mre_ref[bb])
            hi_ref[:, st] = _dot(ub, mim_ref[bb])
        _scan_fwd(hr_ref, hi_ref, a_ref, ap_ref, carry_ref, seg, None)
        ys = []
        for bb in range(S5_BLOCKS):
            st = slice(bb * S5_BLOCK_ST, (bb + 1) * S5_BLOCK_ST)
            ys.append(_dot(hr_ref[:, st].astype(BF16), nre_ref[bb])
                      - _dot(hi_ref[:, st].astype(BF16), nim_ref[bb]))
        y2 = jnp.concatenate(ys, axis=1) + dsk_ref[...] * up
        y2_ref[...] = y2
        y3 = _gelu(y2)
        gl = _dot(y3.astype(BF16), gw_ref[...]) + gb_ref[...]
        oa = y3 * _sigmoid(gl)
        out = _dot(permt_ref[...], oa.astype(BF16)).astype(BF16)
        out_ref[...] = out
        outt_ref[...] = out.T

    blk = pl.BlockSpec((tb, D_SSM), lambda b, j: (b * nb + j, 0))
    blk_t = pl.BlockSpec((D_SSM, tb), lambda b, j: (0, b * nb + j))
    m_shape = (S5_BLOCKS, S5_BLOCK_IN, S5_BLOCK_ST)
    n_shape = (S5_BLOCKS, S5_BLOCK_ST, S5_BLOCK_IN)
    return _call(
        body, "s5_fwd", (bsz, nb),
        [blk, _fixed((tb, tb)), _fixed((tb, tb)), _fixed(m_shape), _fixed(m_shape), _fixed(n_shape),
         _fixed(n_shape), _fixed((2, SSM_LANES)), _fixed((2, SSM_LANES)), _fixed((1, D_SSM)),
         _fixed((D_SSM, D_SSM)), _fixed((1, D_SSM))],
        (blk, blk_t, blk, pl.BlockSpec((1, 2, SSM_LANES), lambda b, j: (b * nb + j, 0, 0))),
        (SDS((t, D_SSM), BF16), SDS((D_SSM, t), BF16), SDS((t, D_SSM), F32), SDS((bsz * nb, 2, SSM_LANES), F32)),
        (za, sp["perm"], sp["permt"], sp["mre"], sp["mim"], sp["nre"], sp["nim"], sp["a"], sp["ap"],
         sp["dskip"], sp["glu_w"], sp["glu_b"]),
        scratch=[pltpu.VMEM((tb, SSM_LANES), F32), pltpu.VMEM((tb, SSM_LANES), F32),
                 pltpu.VMEM((2, SSM_LANES), F32)],
        sem=("arbitrary", "arbitrary"), bg=bg)


def _s5_bwd(za, y2p, doa, carries, sp, bsz, seq, tb, bg=None):
    nb = seq // tb
    seg = tb // 8
    t = bsz * seq

    def body(za_ref, y2_ref, doa_ref, car_ref, perm_ref, permt_ref, mre_ref, mim_ref, mtre_ref, mtim_ref,
             nre_ref, nim_ref, ntre_ref, ntim_ref, a_ref, ap_ref, dsk_ref, gw_ref, gwt_ref, gb_ref,
             dza_ref, dmr_ref, dmi_ref, dnr_ref, dni_ref, da_ref, ddsk_ref, dgw_ref, dgb_ref,
             hr_ref, hi_ref, gr_ref, gi_ref, cin_ref, carry_ref, rcarry_ref):
        first = jnp.logical_and(pl.program_id(0) == 0, pl.program_id(1) == 0)

        @pl.when(first)
        def _():
            for r in (dmr_ref, dmi_ref, dnr_ref, dni_ref, da_ref, ddsk_ref, dgw_ref, dgb_ref):
                r[...] = jnp.zeros_like(r)

        @pl.when(pl.program_id(1) == 0)
        def _():
            rcarry_ref[...] = jnp.zeros_like(rcarry_ref)

        carry_ref[...] = car_ref[0]
        perm = perm_ref[...]
        up = _dot(perm, za_ref[...])
        upb = up.astype(BF16)
        for bb in range(S5_BLOCKS):
            ub = upb[:, bb * S5_BLOCK_IN:(bb + 1) * S5_BLOCK_IN]
            st = slice(bb * S5_BLOCK_ST, (bb + 1) * S5_BLOCK_ST)
            hr_ref[:, st] = _dot(ub, mre_ref[bb])
            hi_ref[:, st] = _dot(ub, mim_ref[bb])
        _scan_fwd(hr_ref, hi_ref, a_ref, ap_ref, carry_ref, seg, cin_ref)

        y2 = y2_ref[...]
        y3 = _gelu(y2)
        y3b = y3.astype(BF16)
        sg = _sigmoid(_dot(y3b, gw_ref[...]) + gb_ref[...])
        d0 = doa_ref[...]
        d_hi = d0.astype(BF16)
        d1 = d0 - d_hi.astype(F32)
        d_mid = d1.astype(BF16)
        d_lo = (d1 - d_mid.astype(F32)).astype(BF16)
        doap = _dot(perm, d_hi) + _dot(perm, d_mid) + _dot(perm, d_lo)
        dgl = doap * y3 * sg * (1.0 - sg)
        dglb = dgl.astype(BF16)
        dy3 = doap * sg + _dot(dglb, gwt_ref[...])
        dgw_ref[...] += _dot_tn(y3b, dglb)
        dgb_ref[...] += jnp.sum(dgl, axis=0, keepdims=True)
        dy2 = dy3 * _gelu_grad(y2)
        ddsk_ref[...] += jnp.sum(dy2 * up, axis=0, keepdims=True)
        dyb = dy2.astype(BF16)
        for bb in range(S5_BLOCKS):
            dyc = dyb[:, bb * S5_BLOCK_IN:(bb + 1) * S5_BLOCK_IN]
            st = slice(bb * S5_BLOCK_ST, (bb + 1) * S5_BLOCK_ST)
            gr_ref[:, st] = _dot(dyc, ntre_ref[bb])
            gi_ref[:, st] = -_dot(dyc, ntim_ref[bb])
            dnr_ref[bb] += _dot_tn(hr_ref[:, st].astype(BF16), dyc)
            dni_ref[bb] += -_dot_tn(hi_ref[:, st].astype(BF16), dyc)
        _scan_bwd(gr_ref, gi_ref, hr_ref, hi_ref, cin_ref, a_ref, ap_ref, rcarry_ref, da_ref, seg)
        dus = []
        for bb in range(S5_BLOCKS):
            st = slice(bb * S5_BLOCK_ST, (bb + 1) * S5_BLOCK_ST)
            grb = gr_ref[:, st].astype(BF16)
            gib = gi_ref[:, st].astype(BF16)
            dus.append(_dot(grb, mtre_ref[bb]) + _dot(gib, mtim_ref[bb]))
            ub = upb[:, bb * S5_BLOCK_IN:(bb + 1) * S5_BLOCK_IN]
            dmr_ref[bb] += _dot_tn(ub, grb)
            dmi_ref[bb] += _dot_tn(ub, gib)
        du = jnp.concatenate(dus, axis=1) + dy2 * dsk_ref[...]
        dza_ref[...] = _dot(permt_ref[...], du.astype(BF16)).astype(BF16)

    def rev(b, j):
        return (b * nb + (nb - 1 - j), 0)

    blk = pl.BlockSpec((tb, D_SSM), rev)
    m_shape = (S5_BLOCKS, S5_BLOCK_IN, S5_BLOCK_ST)
    n_shape = (S5_BLOCKS, S5_BLOCK_ST, S5_BLOCK_IN)
    return _call(
        body, "s5_bwd", (bsz, nb),
        [blk, blk, blk, pl.BlockSpec((1, 2, SSM_LANES), lambda b, j: (b * nb + (nb - 1 - j), 0, 0)),
         _fixed((tb, tb)), _fixed((tb, tb)), _fixed(m_shape), _fixed(m_shape), _fixed(n_shape), _fixed(n_shape),
         _fixed(n_shape), _fixed(n_shape), _fixed(m_shape), _fixed(m_shape),
         _fixed((2, SSM_LANES)), _fixed((2, SSM_LANES)), _fixed((1, D_SSM)),
         _fixed((D_SSM, D_SSM)), _fixed((D_SSM, D_SSM)), _fixed((1, D_SSM))],
        (blk, _fixed(m_shape), _fixed(m_shape), _fixed(n_shape), _fixed(n_shape),
         _fixed((2, 8, SSM_LANES)), _fixed((1, D_SSM)), _fixed((D_SSM, D_SSM)), _fixed((1, D_SSM))),
        (SDS((t, D_SSM), BF16), SDS(m_shape, F32), SDS(m_shape, F32), SDS(n_shape, F32), SDS(n_shape, F32),
         SDS((2, 8, SSM_LANES), F32), SDS((1, D_SSM), F32), SDS((D_SSM, D_SSM), F32), SDS((1, D_SSM), F32)),
        (za, y2p, doa, carries, sp["perm"], sp["permt"], sp["mre"], sp["mim"], sp["mtre"], sp["mtim"],
         sp["nre"], sp["nim"], sp["ntre"], sp["ntim"], sp["a"], sp["ap"], sp["dskip"], sp["glu_w"],
         sp["glu_wt"], sp["glu_b"]),
        scratch=[pltpu.VMEM((tb, SSM_LANES), F32), pltpu.VMEM((tb, SSM_LANES), F32),
                 pltpu.VMEM((tb, SSM_LANES), F32), pltpu.VMEM((tb, SSM_LANES), F32),
                 pltpu.VMEM((2, 8, SSM_LANES), F32), pltpu.VMEM((2, SSM_LANES), F32),
                 pltpu.VMEM((2, SSM_LANES), F32)],
        sem=("arbitrary", "arbitrary"), bg=bg)


def _gmlp_spatial(ws_ref, vb):
    lane = lax.broadcasted_iota(jnp.int32, (CHUNK, 128), 1)
    parts = []
    for j in range(GMLP_HEADS // 2):
        vp = vb[:, 128 * j:128 * (j + 1)]
        parts.append(jnp.where(lane < GMLP_HEAD_DIM, _dot(ws_ref[2 * j], vp), _dot(ws_ref[2 * j + 1], vp)))
    return jnp.concatenate(parts, axis=1)


def _gmlp_fwd(zuv, ln_g, ln_b, wsm, bias, bg=None):
    t = zuv.shape[0]
    chunks = min(GMLP_CHUNKS, t // CHUNK)

    def body(z_ref, g_ref, b_ref, ws_ref, bias_ref, out_ref, outt_ref):
        for ch in range(chunks):
            rows = slice(ch * CHUNK, (ch + 1) * CHUNK)
            u = _gelu(z_ref[rows, 0:D_GMLP].astype(F32))
            v0 = _gelu(z_ref[rows, D_GMLP:2 * D_GMLP].astype(F32))
            v, _, _ = _ln_fwd(v0, g_ref[...], b_ref[...])
            s = _gmlp_spatial(ws_ref, v.astype(BF16)) + bias_ref[...]
            out = (u * s).astype(BF16)
            out_ref[rows, :] = out
            outt_ref[:, rows] = out.T

    step = chunks * CHUNK
    return _call(
        body, "gmlp_fwd", (t // step,),
        [_rows(step, 2 * D_GMLP), _fixed((1, D_GMLP)), _fixed((1, D_GMLP)),
         _fixed((GMLP_HEADS, CHUNK, CHUNK)), _fixed((CHUNK, D_GMLP))],
        (_rows(step, D_GMLP), _cols(D_GMLP, step)), (SDS((t, D_GMLP), BF16), SDS((D_GMLP, t), BF16)),
        (zuv, ln_g, ln_b, wsm, bias), sem=("parallel",), bg=bg)


def _gmlp_bwd(zuv, dgm, ln_g, ln_b, wsm, wsmt, bias, bg=None):
    t = zuv.shape[0]
    chunks = min(GMLP_CHUNKS, t // CHUNK)

    def body(z_ref, d_ref, g_ref, b_ref, ws_ref, wst_ref, bias_ref,
             dz_ref, dws_ref, dbias_ref, dg_ref, db_ref):
        @pl.when(pl.program_id(0) == 0)
        def _():
            for r in (dws_ref, dbias_ref, dg_ref, db_ref):
                r[...] = jnp.zeros_like(r)

        gam = g_ref[...]
        lane = lax.broadcasted_iota(jnp.int32, (CHUNK, 128), 1)
        tril = (lax.broadcasted_iota(jnp.int32, (CHUNK, CHUNK), 0)
                >= lax.broadcasted_iota(jnp.int32, (CHUNK, CHUNK), 1))
        zero_b = jnp.zeros((CHUNK, 128), BF16)
        for ch in range(chunks):
            rows = slice(ch * CHUNK, (ch + 1) * CHUNK)
            zu = z_ref[rows, 0:D_GMLP].astype(F32)
            zv = z_ref[rows, D_GMLP:2 * D_GMLP].astype(F32)
            u = _gelu(zu)
            v0 = _gelu(zv)
            v, vhat, rstd = _ln_fwd(v0, gam, b_ref[...])
            vb = v.astype(BF16)
            s = _gmlp_spatial(ws_ref, vb) + bias_ref[...]
            d = d_ref[rows, :]
            dz_ref[rows, 0:D_GMLP] = (d * s * _gelu_grad(zu)).astype(BF16)
            ds = d * u
            dbias_ref[...] += ds
            dsb = ds.astype(BF16)
            parts = []
            for j in range(GMLP_HEADS // 2):
                dsp = dsb[:, 128 * j:128 * (j + 1)]
                vp = vb[:, 128 * j:128 * (j + 1)]
                parts.append(jnp.where(lane < GMLP_HEAD_DIM, _dot(wst_ref[2 * j], dsp),
                                       _dot(wst_ref[2 * j + 1], dsp)))
                lo = jnp.where(lane < GMLP_HEAD_DIM, dsp, zero_b)
                hi = jnp.where(lane < GMLP_HEAD_DIM, zero_b, dsp)
                dws_ref[2 * j] += jnp.where(tril, _dot_nt(lo, vp), 0.0)
                dws_ref[2 * j + 1] += jnp.where(tril, _dot_nt(hi, vp), 0.0)
            dv = jnp.concatenate(parts, axis=1)
            dg_ref[...] += jnp.sum(dv * vhat, axis=0, keepdims=True)
            db_ref[...] += jnp.sum(dv, axis=0, keepdims=True)
            dz_ref[rows, D_GMLP:2 * D_GMLP] = (_ln_bwd(dv, vhat, rstd, gam) * _gelu_grad(zv)).astype(BF16)

    step = chunks * CHUNK
    return _call(
        body, "gmlp_bwd", (t // step,),
        [_rows(step, 2 * D_GMLP), _rows(step, D_GMLP), _fixed((1, D_GMLP)), _fixed((1, D_GMLP)),
         _fixed((GMLP_HEADS, CHUNK, CHUNK)), _fixed((GMLP_HEADS, CHUNK, CHUNK)), _fixed((CHUNK, D_GMLP))],
        (_rows(step, 2 * D_GMLP), _fixed((GMLP_HEADS, CHUNK, CHUNK)), _fixed((CHUNK, D_GMLP)),
         _fixed((1, D_GMLP)), _fixed((1, D_GMLP))),
        (SDS((t, 2 * D_GMLP), BF16), SDS((GMLP_HEADS, CHUNK, CHUNK), F32), SDS((CHUNK, D_GMLP), F32),
         SDS((1, D_GMLP), F32), SDS((1, D_GMLP), F32)),
        (zuv, dgm, ln_g, ln_b, wsm, wsmt, bias), sem=("arbitrary",), bg=bg)


def _mixout_fwd(x1, s5o, gm, gab, ua, ub, wmo, g, b, tm, bg=None):
    t = x1.shape[0]

    def body(x_ref, s_ref, m_ref, gab_ref, ua_ref, ub_ref, wmo_ref, g_ref, b_ref,
             xn_ref, xh_ref, rstd_ref):
        ya = _dot(s_ref[...], ua_ref[...])
        yb = _dot(m_ref[...], ub_ref[...])
        mix = (_sigmoid(gab_ref[:, 0:D_MODEL].astype(F32)) * ya
               + _sigmoid(gab_ref[:, D_MODEL:2 * D_MODEL].astype(F32)) * yb)
        r = ALPHA * x_ref[...] + _dot(mix.astype(BF16), wmo_ref[...])
        y, xh, rstd = _ln_fwd(r, g_ref[...], b_ref[...])
        xn_ref[...] = y
        xh_ref[...] = xh
        rstd_ref[...] = rstd

    return _call(
        body, "mixout_fwd", (t // tm,),
        [_rows(tm, D_MODEL), _rows(tm, D_SSM), _rows(tm, D_GMLP), _rows(tm, 2 * D_MODEL),
         _resident((D_SSM, D_MODEL)), _resident((D_GMLP, D_MODEL)), _resident((D_MODEL, D_MODEL)),
         _fixed((1, D_MODEL)), _fixed((1, D_MODEL))],
        (_rows(tm, D_MODEL), _rows(tm, D_MODEL), _rows(tm, 1)),
        (SDS((t, D_MODEL), F32), SDS((t, D_MODEL), F32), SDS((t, 1), F32)),
        (x1, s5o, gm, gab, ua, ub, wmo, g, b), sem=("parallel",), bg=bg)


def _mixout_bwd(dx2, xh, rstd, s5o, gm, gab, ua, ub, wmo, g, tm, bg=None):
    t = dx2.shape[0]

    def body(d_ref, xh_ref, rstd_ref, s_ref, m_ref, gab_ref, ua_ref, ub_ref, wmo_ref, g_ref,
             dx1_ref, dmx_ref, mb_ref, dya_ref, dyb_ref, ds5_ref, dgm_ref, dgab_ref, dg_ref, db_ref):
        @pl.when(pl.program_id(0) == 0)
        def _():
            dg_ref[...] = jnp.zeros_like(dg_ref)
            db_ref[...] = jnp.zeros_like(db_ref)

        dy = d_ref[...]
        xhv = xh_ref[...]
        dr = _ln_bwd(dy, xhv, rstd_ref[...], g_ref[...])
        dg_ref[...] += jnp.sum(dy * xhv, axis=0, keepdims=True)
        db_ref[...] += jnp.sum(dy, axis=0, keepdims=True)
        dx1_ref[...] = ALPHA * dr
        drb = dr.astype(BF16)
        dmx_ref[...] = drb
        dm = _dot_nt(drb, wmo_ref[...])
        ya = _dot(s_ref[...], ua_ref[...])
        yb = _dot(m_ref[...], ub_ref[...])
        sa = _sigmoid(gab_ref[:, 0:D_MODEL].astype(F32))
        sb = _sigmoid(gab_ref[:, D_MODEL:2 * D_MODEL].astype(F32))
        mb_ref[...] = (sa * ya + sb * yb).astype(BF16).T
        dya = (dm * sa).astype(BF16)
        dyb = (dm * sb).astype(BF16)
        dya_ref[...] = dya
        dyb_ref[...] = dyb
        dgab_ref[:, 0:D_MODEL] = (dm * ya * sa * (1.0 - sa)).astype(BF16)
        dgab_ref[:, D_MODEL:2 * D_MODEL] = (dm * yb * sb * (1.0 - sb)).astype(BF16)
        ds5_ref[...] = _dot_nt(dya, ua_ref[...])
        dgm_ref[...] = _dot_nt(dyb, ub_ref[...])

    return _call(
        body, "mixout_bwd", (t // tm,),
        [_rows(tm, D_MODEL), _rows(tm, D_MODEL), _rows(tm, 1), _rows(tm, D_SSM), _rows(tm, D_GMLP),
         _rows(tm, 2 * D_MODEL), _resident((D_SSM, D_MODEL)), _resident((D_GMLP, D_MODEL)),
         _resident((D_MODEL, D_MODEL)), _fixed((1, D_MODEL))],
        (_rows(tm, D_MODEL), _rows(tm, D_MODEL), _cols(D_MODEL, tm), _rows(tm, D_MODEL),
         _rows(tm, D_MODEL), _rows(tm, D_SSM), _rows(tm, D_GMLP), _rows(tm, 2 * D_MODEL),
         _fixed((1, D_MODEL)), _fixed((1, D_MODEL))),
        (SDS((t, D_MODEL), F32), SDS((t, D_MODEL), BF16), SDS((D_MODEL, t), BF16),
         SDS((t, D_MODEL), BF16), SDS((t, D_MODEL), BF16), SDS((t, D_SSM), F32),
         SDS((t, D_GMLP), F32), SDS((t, 2 * D_MODEL), BF16),
         SDS((1, D_MODEL), F32), SDS((1, D_MODEL), F32)),
        (dx2, xh, rstd, s5o, gm, gab, ua, ub, wmo, g), sem=("arbitrary",), bg=bg)


def _s5_discretise(lre, lim, log_dt, bre, bim):
    dt = jnp.exp(log_dt)[:, None]
    mag = jnp.exp(lre * dt)
    abr = mag * jnp.cos(lim * dt)
    abi = mag * jnp.sin(lim * dt)
    nr = abr - 1.0
    ni = abi
    den = lre * lre + lim * lim
    cr = ((nr * lre + ni * lim) / den)[..., None]
    ci = ((ni * lre - nr * lim) / den)[..., None]
    return abr, abi, cr * bre - ci * bim, cr * bim + ci * bre


def _block_diag_in(bb):
    v = bb.reshape(S5_BLOCKS, 8, SSM_STATE, SSM_GROUP_CH).transpose(0, 1, 3, 2)
    return jnp.einsum("bgip,gh->bgihp", v, jnp.eye(8, dtype=bb.dtype)).reshape(
        S5_BLOCKS, S5_BLOCK_IN, S5_BLOCK_ST)


def _block_diag_in_t(dm):
    v = dm.reshape(S5_BLOCKS, 8, SSM_GROUP_CH, 8, SSM_STATE)
    d = jnp.einsum("bgihp,gh->bgip", v, jnp.eye(8, dtype=dm.dtype))
    return d.transpose(0, 1, 3, 2).reshape(SSM_GROUPS, SSM_STATE, SSM_GROUP_CH)


def _block_diag_out(cc):
    v = cc.reshape(S5_BLOCKS, 8, SSM_GROUP_CH, SSM_STATE)
    return jnp.einsum("bgip,gh->bgphi", v, jnp.eye(8, dtype=cc.dtype)).reshape(
        S5_BLOCKS, S5_BLOCK_ST, S5_BLOCK_IN)


def _block_diag_out_t(dn):
    v = dn.reshape(S5_BLOCKS, 8, SSM_STATE, 8, SSM_GROUP_CH)
    d = jnp.einsum("bgphi,gh->bgip", v, jnp.eye(8, dtype=dn.dtype))
    return d.reshape(SSM_GROUPS, SSM_GROUP_CH, SSM_STATE)


def _s5_setup(lre, lim, log_dt, bre, bim, cre, cim, d_skip, glu_w, glu_b, tb):
    seg = tb // 8
    abr, abi, bbr, bbi = _s5_discretise(lre, lim, log_dt, bre, bim)
    pr, pi = abr, abi
    for _ in range(int(math.log2(seg))):
        pr, pi = pr * pr - pi * pi, 2.0 * pr * pi
    rows = jnp.arange(tb)
    src = (rows % 8) * seg + rows // 8
    perm = (src[:, None] == jnp.arange(tb)[None, :]).astype(BF16)
    mre = _block_diag_in(bbr)
    mim = _block_diag_in(bbi)
    nre = _block_diag_out(cre)
    nim = _block_diag_out(cim)
    return {
        "perm": perm, "permt": perm.T,
        "mre": mre.astype(BF16), "mim": mim.astype(BF16),
        "mtre": mre.transpose(0, 2, 1).astype(BF16), "mtim": mim.transpose(0, 2, 1).astype(BF16),
        "nre": nre.astype(BF16), "nim": nim.astype(BF16),
        "ntre": nre.transpose(0, 2, 1).astype(BF16), "ntim": nim.transpose(0, 2, 1).astype(BF16),
        "a": jnp.stack([abr.reshape(-1), abi.reshape(-1)]),
        "ap": jnp.stack([pr.reshape(-1), pi.reshape(-1)]),
        "dskip": d_skip.reshape(1, D_SSM), "glu_w": glu_w, "glu_wt": glu_w.T,
        "glu_b": glu_b.reshape(1, D_SSM),
    }


BIG = ("ffn1_w_in", "ffn1_w_out", "mix_w_in", "ssm_glu_w", "up_a", "up_b", "mix_w_out",
       "ffn2_w_in", "ffn2_w_out", "ple_w_proj", "ple_w_gate")
BIG_AXIS = {"ffn1_w_in": 1, "ffn1_w_out": 0, "mix_w_in": 1, "ssm_glu_w": 0, "up_a": 1, "up_b": 1,
            "mix_w_out": 0, "ffn2_w_in": 1, "ffn2_w_out": 0, "ple_w_proj": 1, "ple_w_gate": 0}
SHARD_MAJOR = 2
GATHER_AXIS = dict(BIG_AXIS, ffn1_w_in=SHARD_MAJOR, ffn2_w_in=SHARD_MAJOR)
GATHER_ORDER = (("ffn1_w_in",), ("ffn1_w_out",), ("mix_w_in",), ("ssm_glu_w", "up_a", "up_b", "mix_w_out"),
                ("ffn2_w_in",), ("ffn2_w_out", "ple_w_gate", "ple_w_proj"))
GATHER_FIRST_ID = 1
REDUCE_FIRST_ID = 7
SMALL = ("ln1_g", "ln1_b", "ssm_lambda_re", "ssm_lambda_im", "ssm_log_dt", "ssm_b_re", "ssm_b_im",
         "ssm_c_re", "ssm_c_im", "ssm_d", "ssm_glu_b", "gmlp_ln_g", "gmlp_ln_b", "gmlp_w_s",
         "gmlp_b_s", "ln2_g", "ln2_b", "ln3_g", "ln3_b")
SMALL_VIEW = {"ssm_b_re": (SSM_GROUPS, SSM_STATE * SSM_GROUP_CH), "ssm_b_im": (SSM_GROUPS, SSM_STATE * SSM_GROUP_CH)}


def _small_view(k, a):
    return a.reshape(SMALL_VIEW[k]) if k in SMALL_VIEW else a


def _place():
    return lax.axis_index("x"), lax.axis_index("y"), lax.axis_index("c")


def _other_chips(x, y):
    return [(1 - x, y), (x, 1 - y), (1 - x, 1 - y)]


def _window(ref, shard_shape, axis, chip, half):
    r, c = shard_shape
    hr = r // 2
    if axis == SHARD_MAJOR:
        return ref.at[chip] if half is None else ref.at[chip, pl.ds(half * hr, hr), :]
    if axis == 0:
        if half is None:
            return ref.at[pl.ds(chip * r, r), :]
        return ref.at[pl.ds(chip * r + half * hr, hr), :]
    if half is None:
        return ref.at[:, pl.ds(chip * c, c)]
    return ref.at[pl.ds(half * hr, hr), pl.ds(chip * c, c)]


def _gather_weights(shards, axes):
    n = len(shards)
    shapes = [s.shape for s in shards]
    full = [{0: (4 * r, c), 1: (r, 4 * c), SHARD_MAJOR: (4, r, c)}[ax] for (r, c), ax in zip(shapes, axes)]

    def remote(sems, i, k, src, dst, to):
        return pltpu.make_async_remote_copy(src_ref=src, dst_ref=dst, send_sem=sems[0].at[6 * i + k],
                                            recv_sem=sems[1].at[6 * i + k], device_id=to, device_id_type=MESH)

    def own_copies(ins, outs, sems):
        x, y, c = _place()
        me = 2 * x + y
        cps = []
        for i in range(n):
            hr = shapes[i][0] // 2
            mine = ins[i].at[pl.ds(c * hr, hr), :]
            for j, (cx, cy) in enumerate(_other_chips(x, y)):
                cps.append(remote(sems, i, j, mine, _window(outs[i], shapes[i], axes[i], me, c), (cx, cy, c)))
        local = [pltpu.make_async_copy(ins[i], _window(outs[i], shapes[i], axes[i], me, None), sems[2].at[i])
                 for i in range(n)]
        return cps, local

    def start(ins, outs, sems):
        cps, local = own_copies(ins, outs, sems)
        for cp in local + cps:
            cp.start()

    def finish(ins, outs, sems):
        x, y, c = _place()
        sibling = (x, y, 1 - c)
        passed = []
        for j, (cx, cy) in enumerate(_other_chips(x, y)):
            for i in range(n):
                w = _window(outs[i], shapes[i], axes[i], 2 * cx + cy, c)
                remote(sems, i, j, w, w, (cx, cy, c)).wait_recv()
                cp = remote(sems, i, 3 + j, w, w, sibling)
                cp.start()
                passed.append(cp)
        for j, (cx, cy) in enumerate(_other_chips(x, y)):
            for i in range(n):
                w = _window(outs[i], shapes[i], axes[i], 2 * cx + cy, 1 - c)
                remote(sems, i, 3 + j, w, w, sibling).wait_recv()
        cps, local = own_copies(ins, outs, sems)
        for cp in cps + passed:
            cp.wait_send()
        for cp in local:
            cp.wait()

    return _Exchange(shards, [SDS(f, BF16) for f in full],
                     [pltpu.SemaphoreType.DMA((6 * n,)), pltpu.SemaphoreType.DMA((6 * n,)),
                      pltpu.SemaphoreType.DMA((n,))], start, finish)


def _scatter_grads(parts, shapes, axes):
    n = len(parts)

    def copies(ins, outs, sems):
        x, y, c = _place()
        return [pltpu.make_async_remote_copy(
            src_ref=_window(ins[i], shapes[i], axes[i], 2 * cx + cy, None), dst_ref=outs[i].at[j],
            send_sem=sems[0].at[3 * i + j], recv_sem=sems[1].at[3 * i + j],
            device_id=(cx, cy, c), device_id_type=MESH)
            for i in range(n) for j, (cx, cy) in enumerate(_other_chips(x, y))]

    def start(ins, outs, sems):
        for cp in copies(ins, outs, sems):
            cp.start()

    def finish(ins, outs, sems):
        for cp in copies(ins, outs, sems):
            cp.wait()

    return _Exchange(parts, [SDS((3,) + tuple(s), BF16) for s in shapes],
                     [pltpu.SemaphoreType.DMA((3 * n,)), pltpu.SemaphoreType.DMA((3 * n,))], start, finish)


def _swap_halves(parts, shapes, axes):
    n = len(parts)

    def copies(ins, outs, sems):
        x, y, c = _place()
        cps = []
        for i in range(n):
            r, _ = shapes[i]
            hr = r // 2
            if axes[i] == 0:
                cps += [pltpu.make_async_remote_copy(
                    src_ref=ins[i].at[pl.ds(k * r + (1 - c) * hr, hr), :], dst_ref=outs[i].at[k],
                    send_sem=sems[0].at[i], recv_sem=sems[1].at[i], device_id=(x, y, 1 - c),
                    device_id_type=MESH) for k in range(4)]
            else:
                cps.append(pltpu.make_async_remote_copy(
                    src_ref=ins[i].at[pl.ds((1 - c) * hr, hr), :], dst_ref=outs[i],
                    send_sem=sems[0].at[i], recv_sem=sems[1].at[i], device_id=(x, y, 1 - c),
                    device_id_type=MESH))
        return cps

    def start(ins, outs, sems):
        for cp in copies(ins, outs, sems):
            cp.start()

    def finish(ins, outs, sems):
        x, y, c = _place()
        for i in range(n):
            pltpu.make_async_remote_copy(src_ref=outs[i], dst_ref=outs[i], send_sem=sems[0].at[i],
                                         recv_sem=sems[1].at[i], device_id=(x, y, 1 - c),
                                         device_id_type=MESH).wait()

    out = [SDS((4, r // 2, c), BF16) if ax == 0 else SDS((r // 2, 4 * c), BF16)
           for (r, c), ax in zip(shapes, axes)]
    return _Exchange(parts, out, [pltpu.SemaphoreType.DMA((n,)), pltpu.SemaphoreType.DMA((n,))], start, finish)


def _scatter_halves(pres, shapes):
    n = len(pres)

    def copies(ins, outs, sems):
        x, y, c = _place()
        return [pltpu.make_async_remote_copy(
            src_ref=ins[i].at[1 + j], dst_ref=outs[i].at[j], send_sem=sems[0].at[3 * i + j],
            recv_sem=sems[1].at[3 * i + j], device_id=(cx, cy, c), device_id_type=MESH)
            for i in range(n) for j, (cx, cy) in enumerate(_other_chips(x, y))]

    def start(ins, outs, sems):
        for cp in copies(ins, outs, sems):
            cp.start()

    def finish(ins, outs, sems):
        for cp in copies(ins, outs, sems):
            cp.wait()

    return _Exchange(pres, [SDS((3, r // 2, c), BF16) for r, c in shapes],
                     [pltpu.SemaphoreType.DMA((3 * n,)), pltpu.SemaphoreType.DMA((3 * n,))], start, finish)


def _swap_with_sibling(arrs):
    n = len(arrs)

    def copies(ins, outs, sems):
        x, y, c = _place()
        return [pltpu.make_async_remote_copy(src_ref=ins[i], dst_ref=outs[i], send_sem=sems[0].at[i],
                                             recv_sem=sems[1].at[i], device_id=(x, y, 1 - c),
                                             device_id_type=MESH) for i in range(n)]

    def start(ins, outs, sems):
        for cp in copies(ins, outs, sems):
            cp.start()

    def finish(ins, outs, sems):
        for cp in copies(ins, outs, sems):
            cp.wait()

    return _Exchange(arrs, [SDS(a.shape, a.dtype) for a in arrs],
                     [pltpu.SemaphoreType.DMA((n,)), pltpu.SemaphoreType.DMA((n,))], start, finish)


def _gather_small(arrs):
    n = len(arrs)

    def copy(sems, outs, i, k, block, to, src=None):
        px, py, pc = block
        dst = outs[i].at[4 * px + 2 * py + pc]
        return pltpu.make_async_remote_copy(
            src_ref=dst if src is None else src, dst_ref=dst, send_sem=sems[0].at[7 * i + k],
            recv_sem=sems[1].at[7 * i + k], device_id=to, device_id_type=MESH)

    def own_copies(ins, outs, sems):
        x, y, c = _place()
        cps = []
        for i in range(n):
            cps.append(copy(sems, outs, i, 0, (x, y, c), (x, y, 1 - c), src=ins[i]))
            for j, (cx, cy) in enumerate(_other_chips(x, y)):
                cps.append(copy(sems, outs, i, 1 + j, (x, y, c), (cx, cy, c), src=ins[i]))
        local = [pltpu.make_async_copy(ins[i], outs[i].at[4 * x + 2 * y + c], sems[2].at[i]) for i in range(n)]
        return cps, local

    def start(ins, outs, sems):
        cps, local = own_copies(ins, outs, sems)
        for cp in local + cps:
            cp.start()

    def finish(ins, outs, sems):
        x, y, c = _place()
        passed = []
        for j, (cx, cy) in enumerate(_other_chips(x, y)):
            for i in range(n):
                copy(sems, outs, i, 1 + j, (cx, cy, c), (x, y, c)).wait_recv()
                cp = copy(sems, outs, i, 4 + j, (cx, cy, c), (x, y, 1 - c))
                cp.start()
                passed.append(cp)
        for i in range(n):
            copy(sems, outs, i, 0, (x, y, 1 - c), (x, y, c)).wait_recv()
            for j, (cx, cy) in enumerate(_other_chips(x, y)):
                copy(sems, outs, i, 4 + j, (cx, cy, 1 - c), (x, y, c)).wait_recv()
        cps, local = own_copies(ins, outs, sems)
        for cp in cps + passed:
            cp.wait_send()
        for cp in local:
            cp.wait()

    return _Exchange(arrs, [SDS((N_DEV,) + a.shape, F32) for a in arrs],
                     [pltpu.SemaphoreType.DMA((7 * n,)), pltpu.SemaphoreType.DMA((7 * n,)),
                      pltpu.SemaphoreType.DMA((n,))], start, finish)


def _local_step(x, p, tgt, wb, ws, shards=None, opt=None):
    bsz, seq, _ = x.shape
    t = bsz * seq
    tm = min(256, t)
    tb = min(S5_TIME_BLOCK, seq)
    x0 = x.reshape(t, D_MODEL)
    p0 = p.reshape(t, PLE_DIM)
    tg = tgt.reshape(t, D_MODEL)
    row = lambda v: v.reshape(1, -1)
    dist = shards is not None
    wb = dict(wb)
    recv, sums, other, gathered = {}, {}, {}, {}
    gb = {}
    gs = {}
    shape_of, axis_of = {}, {}
    chip = None
    if dist:
        shape_of = {k: tuple(shards[k].shape) for k in BIG}
        axis_of = dict(BIG_AXIS)
        for q in range(LAST_PIECES):
            shape_of[LAST_PIECE % q] = (D_MODEL // LAST_PIECES, shape_of["ffn1_w_in"][1])
            axis_of[LAST_PIECE % q] = 1
        xi, yi, ci = _place()
        chip = (2 * xi + yi).astype(jnp.int32).reshape(1)
        ids = jnp.stack([2 * xi + yi] + [2 * cx + cy for cx, cy in _other_chips(xi, yi)] + [ci]).astype(jnp.int32)
    halfbuf, pre = {}, {}

    def gather(names):
        return _gather_weights([shards[k] for k in names], [GATHER_AXIS[k] for k in names]) if dist else None

    def exchange(scat=(), swap=(), halves=(), scat2=(), swap2=(), extra=None, after=None):
        if not dist:
            return None, []
        after = order[0] if after is None else after
        parts, tags = [], []
        if scat:
            parts.append(_scatter_grads([gb[k][1] for k in scat], [shape_of[k] for k in scat],
                                        [axis_of[k] for k in scat]))
            tags.append((recv, scat))
        if swap:
            for k in swap:
                sums[k] = order[0] = _sum_blocks(gb[k][0], recv[k], shape_of[k], axis_of[k], chip, "sum_" + k,
                                                 order[0])
            parts.append(_swap_with_sibling([sums[k] for k in swap]))
            tags.append((other, swap))
        if halves:
            parts.append(_swap_halves([gb[k][1] for k in halves], [shape_of[k] for k in halves],
                                      [axis_of[k] for k in halves]))
            tags.append((halfbuf, halves))
        if scat2:
            for k in scat2:
                pre[k] = _presum(gb[k][0], halfbuf[k], shape_of[k], axis_of[k], ids, "presum_" + k, order[0])
                order[0] = pre[k][0]
            parts.append(_scatter_halves([pre[k][1] for k in scat2], [shape_of[k] for k in scat2]))
            tags.append((recv, scat2))
        if swap2:
            for k in swap2:
                sums[k] = order[0] = _sum_half(pre[k][0], recv[k], "sum_" + k, order[0])
            parts.append(_swap_with_sibling([sums[k] for k in swap2]))
            tags.append((other, swap2))
        if extra is not None:
            parts.append(extra[0])
            tags.append((extra[1], extra[2]))
        return (_join(parts), tags) if parts else (None, [])

    def take(ex_tags, got):
        ex, tags = ex_tags
        if ex is not None:
            for (dst, names), (o0, o1) in zip(tags, ex.cuts):
                dst.update(zip(names, got[o0:o1]))

    order = [None]

    def ordered(builder, *args, **kw):
        res = builder(*args, bg=order[0] if dist else None, **kw)
        order[0] = res[0][0]
        return res

    launched = []

    def launch(ex_tags):
        if ex_tags[0] is not None:
            n = len(launched)
            launched.append(n)
            take(ex_tags, _run_exchange_on_sequencer(ex_tags[0], "reduce_%d" % n, REDUCE_FIRST_ID + n))

    small_shape = {k: _small_view(k, v).shape for k, v in ws.items()}
    small_shape["loss_rows"] = (1, D_MODEL)
    ws = {k: v if (v.ndim == 2 and k != "ssm_log_dt") else v[0] for k, v in ws.items()}
    tril = jnp.tril(jnp.ones((CHUNK, CHUNK), dtype=bool))
    wsm = jnp.where(tril[None], ws["gmlp_w_s"], 0.0)
    wsm_b = wsm.astype(BF16)
    wsmt_b = wsm.transpose(0, 2, 1).astype(BF16)
    bias = jnp.repeat(ws["gmlp_b_s"].T, GMLP_HEAD_DIM, axis=1)

    tf = min(512, t)
    if dist:
        for gi, names in enumerate(GATHER_ORDER):
            wb.update(zip(names, _run_exchange_on_sequencer(gather(names), "gather_%d" % gi, GATHER_FIRST_ID + gi)))
    (x0b, h1, a1), _ = _ffn_proj(x0, wb["ffn1_w_in"], tf, "ffn1_proj")
    (x1, xh1, rstd1), _ = _ffn_out(x0, a1, wb["ffn1_w_out"], row(ws["ln1_g"]), row(ws["ln1_b"]), tf, "ffn1_out")
    sp = _s5_setup(ws["ssm_lambda_re"], ws["ssm_lambda_im"], ws["ssm_log_dt"], ws["ssm_b_re"],
                   ws["ssm_b_im"], ws["ssm_c_re"], ws["ssm_c_im"], ws["ssm_d"], wb["ssm_glu_w"],
                   ws["ssm_glu_b"], tb)
    (x1b, za, zuv, gab), _ = _mixin_fwd(x1, wb["mix_w_in"], tf)
    (s5o, s5ot, y2p, carries), _ = _s5_fwd(za, sp, bsz, seq, tb)
    (gm, gmt), _ = _gmlp_fwd(zuv, row(ws["gmlp_ln_g"]), row(ws["gmlp_ln_b"]), wsm_b, bias)
    (x2, xh2, rstd2), _ = _mixout_fwd(x1, s5o, gm, gab, wb["up_a"], wb["up_b"], wb["mix_w_out"],
                                           row(ws["ln2_g"]), row(ws["ln2_b"]), tf)
    (x2b, h2, a2), _ = _ffn_proj(x2, wb["ffn2_w_in"], tf, "ffn2_proj")
    (xh3, rstd3, dx3, x3b, pb, dq, de, loss_rows), _ = _ffn_out_loss(
        x2, a2, wb["ffn2_w_out"], row(ws["ln3_g"]), row(ws["ln3_b"]), p0, tg, wb["ple_w_gate"], wb["ple_w_proj"], tf)
    order[0] = dx3
    gb["ple_w_gate"], _ = ordered(_tn_matmul, x3b, dq, "dw_ple_gate", 1024, 1024, a_t=True)
    gb["ple_w_proj"], _ = ordered(_tn_matmul, pb, de, "dw_ple_proj", 256, 1024, a_t=True)
    launch(exchange(scat=("ple_w_gate", "ple_w_proj")))
    (dx2, dh2, df2, gs["ln3_g"], gs["ln3_b"]), _ = ordered(
        _ffn_bwd, dx3, xh3, rstd3, h2, wb["ffn2_w_in"], wb["ffn2_w_out"], row(ws["ln3_g"]), tm, "ffn2_bwd")
    gb["ffn2_w_out"], _ = ordered(_tn_matmul, a2, df2, "dw_ffn2_out", 1408, 1024)
    launch(exchange(scat=("ffn2_w_out",)))
    gb["ffn2_w_in"], _ = ordered(_tn_matmul, x2b, dh2, "dw_ffn2_in", 1024, 1408, a_t=True)
    launch(exchange(scat=("ffn2_w_in",), swap=("ple_w_gate", "ple_w_proj")))
    (dx1a, dmx, mb, dya, dyb, ds5, dgm, dgab, gs["ln2_g"], gs["ln2_b"]), _ = ordered(
        _mixout_bwd, dx2, xh2, rstd2, s5o, gm, gab, wb["up_a"], wb["up_b"], wb["mix_w_out"], row(ws["ln2_g"]), tf)
    gb["mix_w_out"], _ = ordered(_tn_matmul, mb, dmx, "dw_mix_out", 1024, 1024, a_t=True)
    gb["up_a"], _ = ordered(_tn_matmul, s5ot, dya, "dw_up_a", 512, 1024, a_t=True)
    gb["up_b"], _ = ordered(_tn_matmul, gmt, dyb, "dw_up_b", 512, 1024, a_t=True)
    launch(exchange(scat=("mix_w_out", "up_a", "up_b"), swap=("ffn2_w_out",)))
    (dza, dmr, dmi, dnr, dni, da, ddsk, dgw, dgb), _ = ordered(_s5_bwd, za, y2p, ds5, carries, sp, bsz, seq, tb)
    gb["ssm_glu_w"] = (dgw, dgw.astype(BF16))
    launch(exchange(scat=("ssm_glu_w",), swap=("ffn2_w_in",)))
    (dzuv, dws, dbias, gs["gmlp_ln_g"], gs["gmlp_ln_b"]), _ = ordered(
        _gmlp_bwd, zuv, dgm, row(ws["gmlp_ln_g"]), row(ws["gmlp_ln_b"]), wsm_b, wsmt_b, bias)
    (dx1,), _ = ordered(_mixin_bwd, dx1a, dza, dzuv, dgab, wb["mix_w_in"], tf)
    g_mi, _ = ordered(_tn_matmul, x1b, dza, "dw_mix_in_a", 1024, 512, 0, 3584, a_t=True)
    g_mi, _ = ordered(_tn_matmul, x1b, dzuv, "dw_mix_in_uv", 1024, 512, 1, 3584, g_mi, a_t=True)
    gb["mix_w_in"], _ = ordered(_tn_matmul, x1b, dgab, "dw_mix_in_g", 1024, 512, 3, 3584, g_mi, a_t=True)
    launch(exchange(swap=("mix_w_out", "up_a", "up_b", "ssm_glu_w")))

    d_abr = da[0].sum(axis=0).reshape(SSM_GROUPS, SSM_STATE)
    d_abi = da[1].sum(axis=0).reshape(SSM_GROUPS, SSM_STATE)
    _, vjp = jax.vjp(_s5_discretise, ws["ssm_lambda_re"], ws["ssm_lambda_im"], ws["ssm_log_dt"],
                     ws["ssm_b_re"], ws["ssm_b_im"])
    (gs["ssm_lambda_re"], gs["ssm_lambda_im"], gs["ssm_log_dt"], gs["ssm_b_re"], gs["ssm_b_im"]) = vjp(
        (d_abr, d_abi, _block_diag_in_t(dmr), _block_diag_in_t(dmi)))
    gs["ssm_c_re"] = _block_diag_out_t(dnr)
    gs["ssm_c_im"] = _block_diag_out_t(dni)
    gs["ssm_d"] = ddsk
    gs["ssm_glu_b"] = dgb
    gs["gmlp_w_s"] = dws
    gs["gmlp_b_s"] = dbias.reshape(CHUNK, GMLP_HEADS, GMLP_HEAD_DIM).sum(axis=-1).T
    gs["loss_rows"] = loss_rows

    def small_gather(names):
        return (_gather_small([gs[k].reshape(small_shape[k]) for k in names]), gathered, names) if dist else None

    late = ("ln1_g", "ln1_b")
    launch(exchange(scat=("mix_w_in",), extra=small_gather(tuple(k for k in SMALL + ("loss_rows",) if k not in late))))
    (dx0, dh1, df1, gs["ln1_g"], gs["ln1_b"]), _ = ordered(
        _ffn_bwd, dx1, xh1, rstd1, h1, wb["ffn1_w_in"], wb["ffn1_w_out"], row(ws["ln1_g"]), tm, "ffn1_bwd")
    grad_x = dx0.reshape(bsz, seq, D_MODEL)
    if not dist:
        gb["ffn1_w_out"], _ = _tn_matmul(a1, df1, "dw_ffn1_out", 1408, 1024)
        gb["ffn1_w_in"], _ = _tn_matmul(x0b, dh1, "dw_ffn1_in", 1024, 1408, a_t=True)
        return (loss_rows, grad_x, gb, {k: gs[k].reshape(small_shape[k]) for k in SMALL}, sums, other, gathered,
                None, {})
    launch(exchange(extra=small_gather(late)))
    gb["ffn1_w_out"], _ = ordered(_tn_matmul, a1, df1, "dw_ffn1_out", 1408, 1024)
    last = ["ffn1_w_out"] + [LAST_PIECE % q for q in range(LAST_PIECES)]
    fillers = (("ple_w_gate", "ple_w_proj", "ssm_glu_w", "up_a", "up_b"),
               ("ffn2_w_in", "mix_w_in", "ffn2_w_out", "mix_w_out"))
    out = {}
    for i in range(1, len(last) + 3):
        stage = lambda d: tuple(last[i - d:i - d + 1]) if 0 <= i - d < len(last) else ()
        launch(exchange(halves=stage(1), scat2=stage(2), swap2=stage(3), swap=("mix_w_in",) if i == 2 else ()))
        if i < len(last):
            gb[last[i]], _ = ordered(_tn_matmul, x0b, dh1, "dw_" + last[i], D_MODEL // LAST_PIECES, 1408,
                                     a_cols=(i - 1, 1), a_t=True)
        elif i - len(last) < len(fillers):
            for k in fillers[i - len(last)]:
                w, m, v = opt[k]
                out[k] = _adam_big(w, sums[k], other[k], m, v, "adam_" + k, after=order[0])
                order[0] = out[k][1]
    return loss_rows, grad_x, gb, gs, sums, other, gathered, ids, out


def _adamw(w, g, m, v):
    m = ADAM_B1 * m + (1.0 - ADAM_B1) * g
    v = ADAM_B2 * v + (1.0 - ADAM_B2) * (g * g)
    m_hat = m / ADAM_C1
    v_hat = v / ADAM_C2
    delta = -ADAM_LR * (m_hat / (jnp.sqrt(v_hat) + ADAM_EPS) + ADAM_WD * w)
    return delta, m, v


def _pinned(after):
    return ([pl.BlockSpec(memory_space=pl.ANY)], [after]) if after is not None else ([], [])


def _sum_blocks(part, recv, shape, axis, chip, name, after=None):
    r, c = shape
    rb = r // ROW_STEPS

    def body(chip_ref, p_ref, r_ref, *rest):
        rest[-1][...] = (p_ref[...] + r_ref[0].astype(F32) + r_ref[1].astype(F32) + r_ref[2].astype(F32))

    if axis == 0:
        own = pl.BlockSpec((rb, c), lambda i, k: (k[0] * ROW_STEPS + i, 0))
    else:
        own = pl.BlockSpec((rb, c), lambda i, k: (i, k[0]))
    pin_specs, pin_args = _pinned(after)
    grid_spec = pltpu.PrefetchScalarGridSpec(
        num_scalar_prefetch=1, grid=(ROW_STEPS,),
        in_specs=[own, pl.BlockSpec((3, rb, c), lambda i, k: (0, i, 0))] + pin_specs,
        out_specs=pl.BlockSpec((rb, c), lambda i, k: (i, 0)))
    return pl.pallas_call(body, name=name, out_shape=SDS((r, c), F32), grid_spec=grid_spec,
                          compiler_params=_params(("parallel",)))(chip, part, recv, *pin_args)


def _presum(part, half, shape, axis, ids, name, after=None):
    r, c = shape
    rb = r // 2

    def body(ids_ref, p_ref, h_ref, *rest):
        of_ref, ob_ref = rest[-2:]
        s = p_ref[...] + h_ref[...].astype(F32)
        ob_ref[...] = s.astype(BF16)

        @pl.when(pl.program_id(1) == 0)
        def _():
            of_ref[...] = s

    if axis == 0:
        p_spec = pl.BlockSpec((rb, c), lambda i, t, ids: (ids[t] * 2 + ids[4] + i, 0))
        h_spec = pl.BlockSpec((None, rb, c), lambda i, t, ids: (ids[t], i, 0))
    else:
        p_spec = pl.BlockSpec((rb, c), lambda i, t, ids: (ids[4] + i, ids[t]))
        h_spec = pl.BlockSpec((rb, c), lambda i, t, ids: (i, ids[t]))
    pin_specs, pin_args = _pinned(after)
    grid_spec = pltpu.PrefetchScalarGridSpec(
        num_scalar_prefetch=1, grid=(1, 4), in_specs=[p_spec, h_spec] + pin_specs,
        out_specs=(pl.BlockSpec((rb, c), lambda i, t, ids: (i, 0)),
                   pl.BlockSpec((None, rb, c), lambda i, t, ids: (t, i, 0))))
    return pl.pallas_call(body, name=name, out_shape=(SDS((r // 2, c), F32), SDS((4, r // 2, c), BF16)),
                          grid_spec=grid_spec,
                          compiler_params=_params(("parallel", "arbitrary")))(ids, part, half, *pin_args)


def _sum_half(pre, recv, name, after=None):
    hr, c = pre.shape
    rb = hr

    def body(p_ref, r_ref, *rest):
        rest[-1][...] = (p_ref[...] + r_ref[0].astype(F32) + r_ref[1].astype(F32) + r_ref[2].astype(F32))

    spec = pl.BlockSpec((rb, c), lambda i: (i, 0))
    pin_specs, pin_args = _pinned(after)
    return pl.pallas_call(body, name=name, grid=(1,), out_shape=SDS((hr, c), F32),
                          in_specs=[spec, pl.BlockSpec((3, rb, c), lambda i: (0, i, 0))] + pin_specs,
                          out_specs=spec, compiler_params=_params(("parallel",)))(pre, recv, *pin_args)


def _adam_halves(w, mine, oth, m, v, ids, name, piece=0, prev=None):
    r, c = w.shape
    rb = mine.shape[0] // 2

    def body(ids_ref, w_ref, a_ref, b_ref, m_ref, v_ref, *rest):
        g_ref, d_ref, nm_ref, nv_ref = rest[-4:]
        g = jnp.where(pl.program_id(0) // 2 == ids_ref[4], a_ref[...], b_ref[...])
        g_ref[...] = g
        d_ref[...], nm_ref[...], nv_ref[...] = _adamw(w_ref[...], g, m_ref[...], v_ref[...])

    whole = pl.BlockSpec((rb, c), lambda i, ids: (i + 4 * piece, 0))
    part = pl.BlockSpec((rb, c), lambda i, ids: (i % 2, 0))
    in_specs = [whole, part, part, whole, whole]
    args = [w, mine, oth, m, v]
    aliases = {}
    if prev is not None:
        in_specs += [pl.BlockSpec(memory_space=pl.ANY)] * 4
        args += list(prev)
        aliases = {6: 0, 7: 1, 8: 2, 9: 3}
    grid_spec = pltpu.PrefetchScalarGridSpec(num_scalar_prefetch=1, grid=(4,), in_specs=in_specs,
                                             out_specs=(whole,) * 4)
    return pl.pallas_call(body, name=name, out_shape=tuple(SDS((r, c), F32) for _ in range(4)),
                          grid_spec=grid_spec, input_output_aliases=aliases,
                          compiler_params=_params(("parallel",)))(ids, *args)


def _adam_big(w, ga, gb, m, v, name, piece=0, prev=None, after=None):
    r, c = w.shape
    pr = ga.shape[0]
    steps = ROW_STEPS if pr == r else 2
    rb = pr // steps
    off = piece * steps

    def body(w_ref, ga_ref, gb_ref, m_ref, v_ref, *rest):
        g_ref, d_ref, nm_ref, nv_ref = rest[-4:]
        g = ga_ref[...] + gb_ref[...]
        g_ref[...] = g
        d_ref[...], nm_ref[...], nv_ref[...] = _adamw(w_ref[...], g, m_ref[...], v_ref[...])

    whole = pl.BlockSpec((rb, c), lambda i: (i + off, 0))
    part = pl.BlockSpec((rb, c), lambda i: (i, 0))
    in_specs = [whole, part, part, whole, whole]
    args = [w, ga, gb, m, v]
    aliases = {}
    if prev is not None:
        in_specs += [pl.BlockSpec(memory_space=pl.ANY)] * 4
        args += list(prev)
        aliases = {5: 0, 6: 1, 7: 2, 8: 3}
    if after is not None:
        in_specs.append(pl.BlockSpec(memory_space=pl.ANY))
        args.append(after)
    return pl.pallas_call(
        body, name=name, grid=(steps,), out_shape=tuple(SDS((r, c), F32) for _ in range(4)),
        in_specs=in_specs, out_specs=(whole,) * 4, input_output_aliases=aliases,
        compiler_params=_params(("parallel",)),
    )(*args)


def _adam_small(ws, gathered, ms, vs):
    n = len(ws)

    def body(*refs):
        w_refs, g_refs, m_refs, v_refs = refs[:n], refs[n:2 * n], refs[2 * n:3 * n], refs[3 * n:4 * n]
        outs = refs[4 * n:]
        for i in range(n):
            g = g_refs[i][0]
            for d in range(1, N_DEV):
                g = g + g_refs[i][d]
            delta, nm, nv = _adamw(w_refs[i][...], g, m_refs[i][...], v_refs[i][...])
            outs[i][...] = g
            outs[n + i][...] = delta
            outs[2 * n + i][...] = nm
            outs[3 * n + i][...] = nv

    vmem = pl.BlockSpec(memory_space=pltpu.VMEM)
    shapes = [w.shape for w in ws]
    return pl.pallas_call(
        body, name="adam_small", out_shape=tuple(SDS(s, F32) for s in shapes * 4),
        in_specs=[vmem] * (4 * n), out_specs=tuple([vmem] * (4 * n)),
        compiler_params=pltpu.CompilerParams(vmem_limit_bytes=VMEM_LIMIT_BYTES),
    )(*ws, *gathered, *ms, *vs)


def _sum_loss(gathered):
    def body(g_ref, o_ref):
        tot = g_ref[0]
        for d in range(1, N_DEV):
            tot = tot + g_ref[d]
        o_ref[...] = (0.5 / D_MODEL) * jnp.sum(tot, axis=1, keepdims=True)

    vmem = pl.BlockSpec(memory_space=pltpu.VMEM)
    return pl.pallas_call(body, name="sum_loss", out_shape=SDS((1, 1), F32), in_specs=[vmem],
                          out_specs=vmem)(gathered)


def kernel(x, p, ffn1_w_in, ffn1_w_out, ln1_g, ln1_b, mix_w_in, ssm_lambda_re, ssm_lambda_im, ssm_log_dt, ssm_b_re, ssm_b_im, ssm_c_re, ssm_c_im, ssm_d, ssm_glu_w, ssm_glu_b, gmlp_ln_g, gmlp_ln_b, gmlp_w_s, gmlp_b_s, up_a, up_b, mix_w_out, ln2_g, ln2_b, ffn2_w_in, ffn2_w_out, ln3_g, ln3_b, ple_w_proj, ple_w_gate, loss_target, m_ffn1_w_in, m_ffn1_w_out, m_ln1_g, m_ln1_b, m_mix_w_in, m_ssm_lambda_re, m_ssm_lambda_im, m_ssm_log_dt, m_ssm_b_re, m_ssm_b_im, m_ssm_c_re, m_ssm_c_im, m_ssm_d, m_ssm_glu_w, m_ssm_glu_b, m_gmlp_ln_g, m_gmlp_ln_b, m_gmlp_w_s, m_gmlp_b_s, m_up_a, m_up_b, m_mix_w_out, m_ln2_g, m_ln2_b, m_ffn2_w_in, m_ffn2_w_out, m_ln3_g, m_ln3_b, m_ple_w_proj, m_ple_w_gate, v_ffn1_w_in, v_ffn1_w_out, v_ln1_g, v_ln1_b, v_mix_w_in, v_ssm_lambda_re, v_ssm_lambda_im, v_ssm_log_dt, v_ssm_b_re, v_ssm_b_im, v_ssm_c_re, v_ssm_c_im, v_ssm_d, v_ssm_glu_w, v_ssm_glu_b, v_gmlp_ln_g, v_gmlp_ln_b, v_gmlp_w_s, v_gmlp_b_s, v_up_a, v_up_b, v_mix_w_out, v_ln2_g, v_ln2_b, v_ffn2_w_in, v_ffn2_w_out, v_ln3_g, v_ln3_b, v_ple_w_proj, v_ple_w_gate):
    given = dict(locals())
    order = ("ffn1_w_in", "ffn1_w_out", "ln1_g", "ln1_b", "mix_w_in", "ssm_lambda_re", "ssm_lambda_im",
             "ssm_log_dt", "ssm_b_re", "ssm_b_im", "ssm_c_re", "ssm_c_im", "ssm_d", "ssm_glu_w", "ssm_glu_b",
             "gmlp_ln_g", "gmlp_ln_b", "gmlp_w_s", "gmlp_b_s", "up_a", "up_b", "mix_w_out", "ln2_g", "ln2_b",
             "ffn2_w_in", "ffn2_w_out", "ln3_g", "ln3_b", "ple_w_proj", "ple_w_gate")
    assert set(order) == set(BIG + SMALL)

    shard = {k: given[k][0] for k in BIG}
    shard_b = {k: shard[k].astype(BF16) for k in BIG}
    opt = {k: (shard[k], given["m_" + k][0], given["v_" + k][0]) for k in BIG}
    loss_rows, grad_x, gb, gs, sums, other, gathered, ids, out = _local_step(
        x, given["p"][0], loss_target, {}, {k: given[k] for k in SMALL}, shard_b, opt)

    out = dict(out)
    for k in BIG:
        if k in out:
            continue
        moments = (given["m_" + k][0], given["v_" + k][0])
        if k == "ffn1_w_out":
            out[k] = _adam_halves(shard[k], sums[k], other[k], *moments, ids, "adam_" + k)
        elif k == "ffn1_w_in":
            for q in range(LAST_PIECES):
                kq = LAST_PIECE % q
                out[k] = _adam_halves(shard[k], sums[kq], other[kq], *moments, ids, "adam_" + kq, q, out.get(k))
        else:
            out[k] = _adam_big(shard[k], sums[k], other[k], *moments, "adam_" + k,
                               after=gb[LAST_PIECE % (LAST_PIECES - 1)][0])

    res = _adam_small([_small_view(k, given[k]) for k in SMALL], [gathered[k] for k in SMALL],
                      [_small_view(k, given["m_" + k]) for k in SMALL],
                      [_small_view(k, given["v_" + k]) for k in SMALL])
    ns = len(SMALL)
    for i, k in enumerate(SMALL):
        out[k] = tuple(res[j * ns + i].reshape(given[k].shape) for j in range(4))
    loss = _sum_loss(gathered["loss_rows"]).reshape(())

    lead = lambda k, j: out[k][j][None] if k in BIG else out[k][j]
    return (loss, grad_x, *[lead(k, 0) for k in order], *[lead(k, 1) for k in order],
            *[lead(k, 2) for k in order], *[lead(k, 3) for k in order])
```

```python
import math

import jax
import jax.numpy as jnp
from jax import lax
from jax.experimental import pallas as pl
from jax.experimental.pallas import tpu as pltpu
from jax.experimental.pallas import tpu_sc as plsc

F32 = jnp.float32
BF16 = jnp.bfloat16
MESH = pl.DeviceIdType.MESH
SDS = jax.ShapeDtypeStruct

D_MODEL = 1024
D_FF = 2816
D_SSM = 512
D_GMLP = 512
SSM_GROUPS = 32
SSM_GROUP_CH = 16
SSM_STATE = 64
SSM_LANES = SSM_GROUPS * SSM_STATE
GMLP_HEADS = 8
GMLP_HEAD_DIM = 64
CHUNK = 128
PLE_DIM = 256
LN_EPS = 1e-5
ALPHA = 2.0 ** 0.25

ADAM_LR = 0.001
ADAM_B1 = 0.9
ADAM_B2 = 0.999
ADAM_EPS = 1e-08
ADAM_WD = 0.01
ADAM_STEP = 10
ADAM_C1 = 1.0 - ADAM_B1 ** ADAM_STEP
ADAM_C2 = 1.0 - ADAM_B2 ** ADAM_STEP

N_DEV = 8
VMEM_LIMIT_BYTES = 56 * 1024 * 1024
FFN_COLS = 1408
S5_BLOCKS = 4
S5_BLOCK_IN = D_SSM // S5_BLOCKS
S5_BLOCK_ST = SSM_LANES // S5_BLOCKS
SCAN_LANES = 512
S5_TIME_BLOCK = 512
TN_K_BLOCK = 2048
TN_SMALL_BLOCK = 1024 * 1024
GMLP_CHUNKS = 8
ROW_STEPS = 4
LAST_PIECES = 2
LAST_PIECE = "ffn1_w_in_q%d"
_G0 = math.sqrt(2.0 / math.pi)
_G1 = 0.044715


def _dot(a, b):
    return jnp.dot(a, b, preferred_element_type=F32)


def _dot_nt(a, b):
    return lax.dot_general(a, b, (((1,), (1,)), ((), ())), preferred_element_type=F32)


def _dot_tn(a, b):
    return lax.dot_general(a, b, (((0,), (0,)), ((), ())), preferred_element_type=F32)


def _sigmoid(x):
    return 0.5 * jnp.tanh(0.5 * x) + 0.5


def _gelu(x):
    t = jnp.tanh(_G0 * (x + _G1 * x * x * x))
    return 0.5 * x * (1.0 + t)


def _gelu_grad(x):
    t = jnp.tanh(_G0 * (x + _G1 * x * x * x))
    return 0.5 * (1.0 + t) + 0.5 * x * (1.0 - t * t) * _G0 * (1.0 + 3.0 * _G1 * x * x)


def _ln_fwd(r, g, b):
    mu = jnp.mean(r, axis=-1, keepdims=True)
    d = r - mu
    var = jnp.mean(d * d, axis=-1, keepdims=True)
    rstd = lax.rsqrt(var + LN_EPS)
    xh = d * rstd
    return xh * g + b, xh, rstd


def _ln_bwd(dy, xh, rstd, g):
    dxh = dy * g
    m1 = jnp.mean(dxh, axis=-1, keepdims=True)
    m2 = jnp.mean(dxh * xh, axis=-1, keepdims=True)
    return rstd * (dxh - m1 - xh * m2)


def _resident(shape):
    nd = len(shape)
    return pl.BlockSpec(shape, lambda *_: (0,) * nd, pipeline_mode=pl.Buffered(1))


def _fixed(shape):
    nd = len(shape)
    return pl.BlockSpec(shape, lambda *_: (0,) * nd)


def _rows(tm, cols):
    return pl.BlockSpec((tm, cols), lambda i: (i, 0))


def _cols(rows, tm):
    return pl.BlockSpec((rows, tm), lambda i: (0, i))


def _params(sem):
    return pltpu.CompilerParams(dimension_semantics=sem, vmem_limit_bytes=VMEM_LIMIT_BYTES)


class _Exchange:
    def __init__(self, args, out_shape, sems, start, finish):
        self.args, self.out_shape, self.sems = list(args), list(out_shape), list(sems)
        self.start, self.finish = start, finish
        self.cuts = [(0, len(self.out_shape))]


def _call(body, name, grid, in_specs, out_specs, out_shape, args, scratch=(), sem=None, bg=None, aliases=None):
    aliases = {} if aliases is None else aliases
    in_specs, args = list(in_specs), list(args)
    fn = body
    if bg is not None:
        n_args = len(args)

        def fn(*refs):
            body(*refs[:n_args], *refs[n_args + 1:])

        in_specs.append(pl.BlockSpec(memory_space=pl.ANY))
        args.append(bg)
    res = pl.pallas_call(fn, name=name, grid=grid, out_shape=tuple(out_shape), in_specs=in_specs,
                         out_specs=tuple(out_specs), scratch_shapes=list(scratch),
                         input_output_aliases=aliases, compiler_params=_params(sem))(*args)
    return tuple(res), ()


def _run_exchange_on_sequencer(ex, name, collective_id):
    n_i, n_o = len(ex.args), len(ex.out_shape)

    def body(*refs):
        ins, outs, sems = refs[:n_i], refs[n_i:n_i + n_o], refs[n_i + n_o:]
        x, y, c = lax.axis_index("x"), lax.axis_index("y"), lax.axis_index("c")
        barrier = pltpu.get_barrier_semaphore()
        for peer in [(x, y, 1 - c), (1 - x, y, c), (x, 1 - y, c), (1 - x, 1 - y, c)]:
            pl.semaphore_signal(barrier, inc=1, device_id=peer, device_id_type=MESH)
        pl.semaphore_wait(barrier, 4)
        ex.start(ins, outs, sems)
        ex.finish(ins, outs, sems)

    return tuple(pl.kernel(body, out_type=tuple(ex.out_shape),
                           mesh=plsc.ScalarSubcoreMesh(axis_name="sequencer", num_cores=1),
                           scratch_types=list(ex.sems), name=name,
                           compiler_params=pltpu.CompilerParams(collective_id=collective_id))(*ex.args))


def _join(exchanges):
    cuts = []
    a = o = q = 0
    for e in exchanges:
        cuts.append((a, a + len(e.args), o, o + len(e.out_shape), q, q + len(e.sems)))
        a, o, q = cuts[-1][1], cuts[-1][3], cuts[-1][5]

    def start(ins, outs, sems):
        for e, (a0, a1, o0, o1, q0, q1) in zip(exchanges, cuts):
            e.start(ins[a0:a1], outs[o0:o1], sems[q0:q1])

    def finish(ins, outs, sems):
        for e, (a0, a1, o0, o1, q0, q1) in zip(exchanges, cuts):
            e.finish(ins[a0:a1], outs[o0:o1], sems[q0:q1])

    joined = _Exchange(sum((e.args for e in exchanges), []), sum((e.out_shape for e in exchanges), []),
                       sum((e.sems for e in exchanges), []), start, finish)
    joined.cuts = [(c[2], c[3]) for c in cuts]
    return joined


def _ffn_proj(x, w_in, tm, name, bg=None):
    t = x.shape[0]
    nch = D_FF // FFN_COLS

    def body(x_ref, win_ref, xbt_ref, h_ref, a_ref):
        xb = x_ref[...].astype(BF16)
        xbt_ref[...] = xb.T
        for k in range(nch):
            cg = slice(k * FFN_COLS, (k + 1) * FFN_COLS)
            cu = slice(D_FF + k * FFN_COLS, D_FF + (k + 1) * FFN_COLS)
            hg = _dot(xb, win_ref[k])
            hu = _dot(xb, win_ref[nch + k])
            h_ref[:, cg] = hg.astype(BF16)
            h_ref[:, cu] = hu.astype(BF16)
            a_ref[:, cg] = (hg * _sigmoid(hg) * hu).astype(BF16)

    return _call(
        body, name, (t // tm,),
        [_rows(tm, D_MODEL), _resident((2 * nch, D_MODEL, FFN_COLS))],
        (_cols(D_MODEL, tm), _rows(tm, 2 * D_FF), _rows(tm, D_FF)),
        (SDS((D_MODEL, t), BF16), SDS((t, 2 * D_FF), BF16), SDS((t, D_FF), BF16)),
        (x, w_in), sem=("parallel",), bg=bg)


def _ffn_out(x, a, w_out, g, b, tm, name, bg=None):
    t = x.shape[0]

    def body(x_ref, a_ref, wout_ref, g_ref, b_ref, xn_ref, xh_ref, rstd_ref):
        f = _dot(a_ref[...], wout_ref[...])
        y, xh, rstd = _ln_fwd(ALPHA * x_ref[...] + 0.5 * f, g_ref[...], b_ref[...])
        xn_ref[...] = y
        xh_ref[...] = xh
        rstd_ref[...] = rstd

    return _call(
        body, name, (t // tm,),
        [_rows(tm, D_MODEL), _rows(tm, D_FF), _resident((D_FF, D_MODEL)), _fixed((1, D_MODEL)), _fixed((1, D_MODEL))],
        (_rows(tm, D_MODEL), _rows(tm, D_MODEL), _rows(tm, 1)),
        (SDS((t, D_MODEL), F32), SDS((t, D_MODEL), F32), SDS((t, 1), F32)),
        (x, a, w_out, g, b), sem=("parallel",), bg=bg)


def _ffn_out_loss(x, a, w_out, g, b, p, tgt, wpg, wpp, tm):
    t = x.shape[0]

    def body(x_ref, a_ref, wout_ref, g_ref, b_ref, p_ref, t_ref, wpg_ref, wpp_ref,
             xh_ref, rstd_ref, dx_ref, xbt_ref, pbt_ref, dq_ref, de_ref, loss_ref):
        @pl.when(pl.program_id(0) == 0)
        def _():
            loss_ref[...] = jnp.zeros_like(loss_ref)

        f = _dot(a_ref[...], wout_ref[...])
        x3v, xh, rstd = _ln_fwd(ALPHA * x_ref[...] + 0.5 * f, g_ref[...], b_ref[...])
        xh_ref[...] = xh
        rstd_ref[...] = rstd
        xb = x3v.astype(BF16)
        pb = p_ref[...].astype(BF16)
        xbt_ref[...] = xb.T
        pbt_ref[...] = pb.T
        s = _sigmoid(_dot(xb, wpg_ref[...]))
        e = _dot(pb, wpp_ref[...])
        diff = x3v + s * e - t_ref[...]
        loss_ref[...] += jnp.sum(diff * diff, axis=0, keepdims=True)
        dout = diff * (1.0 / D_MODEL)
        de_ref[...] = (dout * s).astype(BF16)
        dq = (dout * e * s * (1.0 - s)).astype(BF16)
        dq_ref[...] = dq
        dx_ref[...] = dout + _dot_nt(dq, wpg_ref[...])

    return _call(
        body, "ffn2_out_loss", (t // tm,),
        [_rows(tm, D_MODEL), _rows(tm, D_FF), _resident((D_FF, D_MODEL)), _fixed((1, D_MODEL)), _fixed((1, D_MODEL)),
         _rows(tm, PLE_DIM), _rows(tm, D_MODEL), _resident((D_MODEL, D_MODEL)), _resident((PLE_DIM, D_MODEL))],
        (_rows(tm, D_MODEL), _rows(tm, 1), _rows(tm, D_MODEL), _cols(D_MODEL, tm), _cols(PLE_DIM, tm),
         _rows(tm, D_MODEL), _rows(tm, D_MODEL), _fixed((1, D_MODEL))),
        (SDS((t, D_MODEL), F32), SDS((t, 1), F32), SDS((t, D_MODEL), F32), SDS((D_MODEL, t), BF16),
         SDS((PLE_DIM, t), BF16), SDS((t, D_MODEL), BF16), SDS((t, D_MODEL), BF16), SDS((1, D_MODEL), F32)),
        (x, a, w_out, g, b, p, tgt, wpg, wpp), sem=("arbitrary",))


def _ffn_bwd(dxn, xh, rstd, h, w_in, w_out, g, tm, name, bg=None):
    t = dxn.shape[0]
    nch = D_FF // FFN_COLS

    def body(dxn_ref, xh_ref, rstd_ref, h_ref, win_ref, wout_ref, g_ref,
             dx_ref, dh_ref, df_ref, dg_ref, db_ref):
        @pl.when(pl.program_id(0) == 0)
        def _():
            dg_ref[...] = jnp.zeros_like(dg_ref)
            db_ref[...] = jnp.zeros_like(db_ref)

        dy = dxn_ref[...]
        xhv = xh_ref[...]
        dr = _ln_bwd(dy, xhv, rstd_ref[...], g_ref[...])
        dg_ref[...] += jnp.sum(dy * xhv, axis=0, keepdims=True)
        db_ref[...] += jnp.sum(dy, axis=0, keepdims=True)
        df = (0.5 * dr).astype(BF16)
        df_ref[...] = df
        dx = ALPHA * dr
        das = [_dot_nt(df, wout_ref[k * FFN_COLS:(k + 1) * FFN_COLS, :]) for k in range(nch)]
        for k in range(nch):
            cg = slice(k * FFN_COLS, (k + 1) * FFN_COLS)
            cu = slice(D_FF + k * FFN_COLS, D_FF + (k + 1) * FFN_COLS)
            hg = h_ref[:, cg].astype(F32)
            hu = h_ref[:, cu].astype(F32)
            sg = _sigmoid(hg)
            silu = hg * sg
            da = das[k]
            dhu = (da * silu).astype(BF16)
            dhg = (da * hu * (sg * (1.0 + hg * (1.0 - sg)))).astype(BF16)
            dh_ref[:, cg] = dhg
            dh_ref[:, cu] = dhu
            dx = dx + _dot_nt(dhg, win_ref[k]) + _dot_nt(dhu, win_ref[nch + k])
        dx_ref[...] = dx

    return _call(
        body, name, (t // tm,),
        [_rows(tm, D_MODEL), _rows(tm, D_MODEL), _rows(tm, 1), _rows(tm, 2 * D_FF),
         _resident((2 * nch, D_MODEL, FFN_COLS)), _resident((D_FF, D_MODEL)), _fixed((1, D_MODEL))],
        (_rows(tm, D_MODEL), _rows(tm, 2 * D_FF), _rows(tm, D_MODEL),
         _fixed((1, D_MODEL)), _fixed((1, D_MODEL))),
        (SDS((t, D_MODEL), F32), SDS((t, 2 * D_FF), BF16), SDS((t, D_MODEL), BF16),
         SDS((1, D_MODEL), F32), SDS((1, D_MODEL), F32)),
        (dxn, xh, rstd, h, w_in, w_out, g), sem=("arbitrary",), bg=bg)


def _tn_matmul(a, b, name, bm, bn, col_block=0, total_cols=None, prev=None, bg=None, a_cols=None, a_t=False):
    t, m = a.shape[::-1] if a_t else a.shape
    a_first = 0
    if a_cols is not None:
        a_first, m = a_cols[0], a_cols[1] * bm
    n = b.shape[1]
    total_cols = n if total_cols is None else total_cols
    whole = bm * bn <= TN_SMALL_BLOCK and (m // bm) * (n // bn) >= 2
    bk = min(2 * TN_K_BLOCK if whole else TN_K_BLOCK, t)
    nk = t // bk
    n_in = 2 if prev is None else 4

    def body(*refs):
        a_ref, b_ref = refs[0], refs[1]
        o_ref, ob_ref = refs[n_in], refs[n_in + 1]
        k = pl.program_id(2)

        @pl.when(k == 0)
        def _():
            o_ref[...] = jnp.zeros_like(o_ref)

        o_ref[...] += _dot(a_ref[...], b_ref[...]) if a_t else _dot_tn(a_ref[...], b_ref[...])

        @pl.when(k == nk - 1)
        def _():
            ob_ref[...] = o_ref[...].astype(BF16)

    a_spec = (pl.BlockSpec((bm, bk), lambda i, j, k: (i + a_first, k)) if a_t
              else pl.BlockSpec((bk, bm), lambda i, j, k: (k, i + a_first)))
    in_specs = [a_spec, pl.BlockSpec((bk, bn), lambda i, j, k: (k, j))]
    args = [a, b]
    aliases = {}
    if prev is not None:
        in_specs += [pl.BlockSpec(memory_space=pl.ANY), pl.BlockSpec(memory_space=pl.ANY)]
        args += list(prev)
        aliases = {2: 0, 3: 1}
        if any(bg is p for p in prev):
            bg = None
    out_spec = pl.BlockSpec((bm, bn), lambda i, j, k: (i, j + col_block))
    return _call(body, name, (m // bm, n // bn, nk), in_specs, (out_spec, out_spec),
                 (SDS((m, total_cols), F32), SDS((m, total_cols), BF16)), args,
                 sem=("parallel", "parallel", "arbitrary"), bg=bg, aliases=aliases)


def _mixin_fwd(x1, w, tm, bg=None):
    t = x1.shape[0]

    def body(x_ref, w_ref, xbt_ref, za_ref, zuv_ref, gab_ref):
        xb = x_ref[...].astype(BF16)
        xbt_ref[...] = xb.T
        za_ref[...] = _dot(xb, w_ref[:, 0:512]).astype(BF16)
        zuv_ref[...] = _dot(xb, w_ref[:, 512:1536]).astype(BF16)
        gab_ref[...] = _dot(xb, w_ref[:, 1536:3584]).astype(BF16)

    return _call(
        body, "mixin_fwd", (t // tm,),
        [_rows(tm, D_MODEL), _resident((D_MODEL, 3584))],
        (_cols(D_MODEL, tm), _rows(tm, 512), _rows(tm, 1024), _rows(tm, 2048)),
        (SDS((D_MODEL, t), BF16), SDS((t, 512), BF16), SDS((t, 1024), BF16), SDS((t, 2048), BF16)),
        (x1, w), sem=("parallel",), bg=bg)


def _mixin_bwd(dx1a, dza, dzuv, dgab, w, tm, bg=None):
    t = dx1a.shape[0]

    def body(d_ref, dza_ref, dzuv_ref, dgab_ref, w_ref, dx_ref):
        dx_ref[...] = (d_ref[...] + _dot_nt(dza_ref[...], w_ref[:, 0:512])
                       + _dot_nt(dzuv_ref[...], w_ref[:, 512:1536])
                       + _dot_nt(dgab_ref[...], w_ref[:, 1536:3584]))

    return _call(
        body, "mixin_bwd", (t // tm,),
        [_rows(tm, D_MODEL), _rows(tm, 512), _rows(tm, 1024), _rows(tm, 2048), _resident((D_MODEL, 3584))],
        (_rows(tm, D_MODEL),), (SDS((t, D_MODEL), F32),),
        (dx1a, dza, dzuv, dgab, w), sem=("parallel",), bg=bg)


def _unrolled(lo, hi, body, carry):
    for j in range(lo, hi):
        carry = body(j, carry)
    return carry


def _scan_fwd(hr_ref, hi_ref, a_ref, ap_ref, carry_ref, seg, cin_ref):
    for lc in range(SSM_LANES // SCAN_LANES):
        ls = slice(lc * SCAN_LANES, (lc + 1) * SCAN_LANES)
        a_r = jnp.broadcast_to(a_ref[0:1, ls], (8, SCAN_LANES))
        a_i = jnp.broadcast_to(a_ref[1:2, ls], (8, SCAN_LANES))

        def step(j, hc, ls=ls, a_r=a_r, a_i=a_i):
            h_r, h_i = hc
            rows = pl.ds(j * 8, 8)
            n_r = a_r * h_r - a_i * h_i + hr_ref[rows, ls]
            n_i = a_r * h_i + a_i * h_r + hi_ref[rows, ls]
            hr_ref[rows, ls] = n_r
            hi_ref[rows, ls] = n_i
            return n_r, n_i

        zero = jnp.zeros((8, SCAN_LANES), F32)
        f_r, f_i = _unrolled(0, seg, step, (zero, zero))
        c_r = carry_ref[0:1, ls]
        c_i = carry_ref[1:2, ls]
        p_r = ap_ref[0:1, ls]
        p_i = ap_ref[1:2, ls]
        rows_r, rows_i = [], []
        for s in range(8):
            rows_r.append(c_r)
            rows_i.append(c_i)
            c_r, c_i = (f_r[s:s + 1] + p_r * c_r - p_i * c_i,
                        f_i[s:s + 1] + p_r * c_i + p_i * c_r)
        carry_ref[0:1, ls] = c_r
        carry_ref[1:2, ls] = c_i
        cin_r = jnp.concatenate(rows_r, axis=0)
        cin_i = jnp.concatenate(rows_i, axis=0)
        if cin_ref is not None:
            cin_ref[0, :, ls] = cin_r
            cin_ref[1, :, ls] = cin_i

        def fix(j, cc, ls=ls, a_r=a_r, a_i=a_i):
            c_r, c_i = cc
            c_r, c_i = a_r * c_r - a_i * c_i, a_r * c_i + a_i * c_r
            rows = pl.ds(j * 8, 8)
            hr_ref[rows, ls] = hr_ref[rows, ls] + c_r
            hi_ref[rows, ls] = hi_ref[rows, ls] + c_i
            return c_r, c_i

        _unrolled(0, seg, fix, (cin_r, cin_i))


def _scan_bwd(gr_ref, gi_ref, hr_ref, hi_ref, cin_ref, a_ref, ap_ref, rcarry_ref, da_ref, seg):
    for lc in range(SSM_LANES // SCAN_LANES):
        ls = slice(lc * SCAN_LANES, (lc + 1) * SCAN_LANES)
        a_r = jnp.broadcast_to(a_ref[0:1, ls], (8, SCAN_LANES))
        a_i = jnp.broadcast_to(a_ref[1:2, ls], (8, SCAN_LANES))

        def step(t, gc, ls=ls, a_r=a_r, a_i=a_i):
            g_r, g_i = gc
            rows = pl.ds((seg - 1 - t) * 8, 8)
            n_r = gr_ref[rows, ls] + a_r * g_r + a_i * g_i
            n_i = gi_ref[rows, ls] + a_r * g_i - a_i * g_r
            gr_ref[rows, ls] = n_r
            gi_ref[rows, ls] = n_i
            return n_r, n_i

        zero = jnp.zeros((8, SCAN_LANES), F32)
        f_r, f_i = _unrolled(0, seg, step, (zero, zero))
        c_r = rcarry_ref[0:1, ls]
        c_i = rcarry_ref[1:2, ls]
        p_r = ap_ref[0:1, ls]
        p_i = ap_ref[1:2, ls]
        rows_r, rows_i = [None] * 8, [None] * 8
        for s in range(7, -1, -1):
            rows_r[s] = c_r
            rows_i[s] = c_i
            c_r, c_i = (f_r[s:s + 1] + p_r * c_r + p_i * c_i,
                        f_i[s:s + 1] + p_r * c_i - p_i * c_r)
        rcarry_ref[0:1, ls] = c_r
        rcarry_ref[1:2, ls] = c_i
        cin_r = jnp.concatenate(rows_r, axis=0)
        cin_i = jnp.concatenate(rows_i, axis=0)

        def fix_row(j_rows, hp_r, hp_i, cc, ls=ls, a_r=a_r, a_i=a_i):
            c_r, c_i, acc_r, acc_i = cc
            c_r, c_i = a_r * c_r + a_i * c_i, a_r * c_i - a_i * c_r
            g_r = gr_ref[j_rows, ls] + c_r
            g_i = gi_ref[j_rows, ls] + c_i
            gr_ref[j_rows, ls] = g_r
            gi_ref[j_rows, ls] = g_i
            acc_r = acc_r + g_r * hp_r + g_i * hp_i
            acc_i = acc_i + g_i * hp_r - g_r * hp_i
            return c_r, c_i, acc_r, acc_i

        def fix(t, cc, ls=ls, fix_row=fix_row):
            j = seg - 1 - t
            rows = pl.ds(j * 8, 8)
            prev = pl.ds((j - 1) * 8, 8)
            return fix_row(rows, hr_ref[prev, ls], hi_ref[prev, ls], cc)

        cc = _unrolled(0, seg - 1, fix, (cin_r, cin_i, zero, zero))
        _, _, acc_r, acc_i = fix_row(pl.ds(0, 8), cin_ref[0, :, ls], cin_ref[1, :, ls], cc)
        da_ref[0, :, ls] += acc_r
        da_ref[1, :, ls] += acc_i


def _s5_fwd(za, sp, bsz, seq, tb, bg=None):
    nb = seq // tb
    seg = tb // 8
    t = bsz * seq

    def body(za_ref, perm_ref, permt_ref, mre_ref, mim_ref, nre_ref, nim_ref, a_ref, ap_ref,
             dsk_ref, gw_ref, gb_ref, out_ref, outt_ref, y2_ref, car_ref, hr_ref, hi_ref, carry_ref):
        @pl.when(pl.program_id(1) == 0)
        def _():
            carry_ref[...] = jnp.zeros_like(carry_ref)

        car_ref[0] = carry_ref[...]
        up = _dot(perm_ref[...], za_ref[...])
        upb = up.astype(BF16)
        for bb in range(S5_BLOCKS):
            ub = upb[:, bb * S5_BLOCK_IN:(bb + 1) * S5_BLOCK_IN]
            st = slice(bb * S5_BLOCK_ST, (bb + 1) * S5_BLOCK_ST)
            hr_ref[:, st] = _dot(ub, mre_ref[bb])
            hi_ref[:, st] = _dot(ub, mim_ref[bb])
        _scan_fwd(hr_ref, hi_ref, a_ref, ap_ref, carry_ref, seg, None)
        ys = []
        for bb in range(S5_BLOCKS):
            st = slice(bb * S5_BLOCK_ST, (bb + 1) * S5_BLOCK_ST)
            ys.append(_dot(hr_ref[:, st].astype(BF16), nre_ref[bb])
                      - _dot(hi_ref[:, st].astype(BF16), nim_ref[bb]))
        y2 = jnp.concatenate(ys, axis=1) + dsk_ref[...] * up
        y2_ref[...] = y2
        y3 = _gelu(y2)
        gl = _dot(y3.astype(BF16), gw_ref[...]) + gb_ref[...]
        oa = y3 * _sigmoid(gl)
        out = _dot(permt_ref[...], oa.astype(BF16)).astype(BF16)
        out_ref[...] = out
        outt_ref[...] = out.T

    blk = pl.BlockSpec((tb, D_SSM), lambda b, j: (b * nb + j, 0))
    blk_t = pl.BlockSpec((D_SSM, tb), lambda b, j: (0, b * nb + j))
    m_shape = (S5_BLOCKS, S5_BLOCK_IN, S5_BLOCK_ST)
    n_shape = (S5_BLOCKS, S5_BLOCK_ST, S5_BLOCK_IN)
    return _call(
        body, "s5_fwd", (bsz, nb),
        [blk, _fixed((tb, tb)), _fixed((tb, tb)), _fixed(m_shape), _fixed(m_shape), _fixed(n_shape),
         _fixed(n_shape), _fixed((2, SSM_LANES)), _fixed((2, SSM_LANES)), _fixed((1, D_SSM)),
         _fixed((D_SSM, D_SSM)), _fixed((1, D_SSM))],
        (blk, blk_t, blk, pl.BlockSpec((1, 2, SSM_LANES), lambda b, j: (b * nb + j, 0, 0))),
        (SDS((t, D_SSM), BF16), SDS((D_SSM, t), BF16), SDS((t, D_SSM), F32), SDS((bsz * nb, 2, SSM_LANES), F32)),
        (za, sp["perm"], sp["permt"], sp["mre"], sp["mim"], sp["nre"], sp["nim"], sp["a"], sp["ap"],
         sp["dskip"], sp["glu_w"], sp["glu_b"]),
        scratch=[pltpu.VMEM((tb, SSM_LANES), F32), pltpu.VMEM((tb, SSM_LANES), F32),
                 pltpu.VMEM((2, SSM_LANES), F32)],
        sem=("arbitrary", "arbitrary"), bg=bg)


def _s5_bwd(za, y2p, doa, carries, sp, bsz, seq, tb, bg=None):
    nb = seq // tb
    seg = tb // 8
    t = bsz * seq

    def body(za_ref, y2_ref, doa_ref, car_ref, perm_ref, permt_ref, mre_ref, mim_ref, mtre_ref, mtim_ref,
             nre_ref, nim_ref, ntre_ref, ntim_ref, a_ref, ap_ref, dsk_ref, gw_ref, gwt_ref, gb_ref,
             dza_ref, dmr_ref, dmi_ref, dnr_ref, dni_ref, da_ref, ddsk_ref, dgw_ref, dgb_ref,
             hr_ref, hi_ref, gr_ref, gi_ref, cin_ref, carry_ref, rcarry_ref):
        first = jnp.logical_and(pl.program_id(0) == 0, pl.program_id(1) == 0)

        @pl.when(first)
        def _():
            for r in (dmr_ref, dmi_ref, dnr_ref, dni_ref, da_ref, ddsk_ref, dgw_ref, dgb_ref):
                r[...] = jnp.zeros_like(r)

        @pl.when(pl.program_id(1) == 0)
        def _():
            rcarry_ref[...] = jnp.zeros_like(rcarry_ref)

        carry_ref[...] = car_ref[0]
        perm = perm_ref[...]
        up = _dot(perm, za_ref[...])
        upb = up.astype(BF16)
        for bb in range(S5_BLOCKS):
            ub = upb[:, bb * S5_BLOCK_IN:(bb + 1) * S5_BLOCK_IN]
            st = slice(bb * S5_BLOCK_ST, (bb + 1) * S5_BLOCK_ST)
            hr_ref[:, st] = _dot(ub, mre_ref[bb])
            hi_ref[:, st] = _dot(ub, mim_ref[bb])
        _scan_fwd(hr_ref, hi_ref, a_ref, ap_ref, carry_ref, seg, cin_ref)

        y2 = y2_ref[...]
        y3 = _gelu(y2)
        y3b = y3.astype(BF16)
        sg = _sigmoid(_dot(y3b, gw_ref[...]) + gb_ref[...])
        d0 = doa_ref[...]
        d_hi = d0.astype(BF16)
        d1 = d0 - d_hi.astype(F32)
        d_mid = d1.astype(BF16)
        d_lo = (d1 - d_mid.astype(F32)).astype(BF16)
        doap = _dot(perm, d_hi) + _dot(perm, d_mid) + _dot(perm, d_lo)
        dgl = doap * y3 * sg * (1.0 - sg)
        dglb = dgl.astype(BF16)
        dy3 = doap * sg + _dot(dglb, gwt_ref[...])
        dgw_ref[...] += _dot_tn(y3b, dglb)
        dgb_ref[...] += jnp.sum(dgl, axis=0, keepdims=True)
        dy2 = dy3 * _gelu_grad(y2)
        ddsk_ref[...] += jnp.sum(dy2 * up, axis=0, keepdims=True)
        dyb = dy2.astype(BF16)
        for bb in range(S5_BLOCKS):
            dyc = dyb[:, bb * S5_BLOCK_IN:(bb + 1) * S5_BLOCK_IN]
            st = slice(bb * S5_BLOCK_ST, (bb + 1) * S5_BLOCK_ST)
            gr_ref[:, st] = _dot(dyc, ntre_ref[bb])
            gi_ref[:, st] = -_dot(dyc, ntim_ref[bb])
            dnr_ref[bb] += _dot_tn(hr_ref[:, st].astype(BF16), dyc)
            dni_ref[bb] += -_dot_tn(hi_ref[:, st].astype(BF16), dyc)
        _scan_bwd(gr_ref, gi_ref, hr_ref, hi_ref, cin_ref, a_ref, ap_ref, rcarry_ref, da_ref, seg)
        dus = []
        for bb in range(S5_BLOCKS):
            st = slice(bb * S5_BLOCK_ST, (bb + 1) * S5_BLOCK_ST)
            grb = gr_ref[:, st].astype(BF16)
            gib = gi_ref[:, st].astype(BF16)
            dus.append(_dot(grb, mtre_ref[bb]) + _dot(gib, mtim_ref[bb]))
            ub = upb[:, bb * S5_BLOCK_IN:(bb + 1) * S5_BLOCK_IN]
            dmr_ref[bb] += _dot_tn(ub, grb)
            dmi_ref[bb] += _dot_tn(ub, gib)
        du = jnp.concatenate(dus, axis=1) + dy2 * dsk_ref[...]
        dza_ref[...] = _dot(permt_ref[...], du.astype(BF16)).astype(BF16)

    def rev(b, j):
        return (b * nb + (nb - 1 - j), 0)

    blk = pl.BlockSpec((tb, D_SSM), rev)
    m_shape = (S5_BLOCKS, S5_BLOCK_IN, S5_BLOCK_ST)
    n_shape = (S5_BLOCKS, S5_BLOCK_ST, S5_BLOCK_IN)
    return _call(
        body, "s5_bwd", (bsz, nb),
        [blk, blk, blk, pl.BlockSpec((1, 2, SSM_LANES), lambda b, j: (b * nb + (nb - 1 - j), 0, 0)),
         _fixed((tb, tb)), _fixed((tb, tb)), _fixed(m_shape), _fixed(m_shape), _fixed(n_shape), _fixed(n_shape),
         _fixed(n_shape), _fixed(n_shape), _fixed(m_shape), _fixed(m_shape),
         _fixed((2, SSM_LANES)), _fixed((2, SSM_LANES)), _fixed((1, D_SSM)),
         _fixed((D_SSM, D_SSM)), _fixed((D_SSM, D_SSM)), _fixed((1, D_SSM))],
        (blk, _fixed(m_shape), _fixed(m_shape), _fixed(n_shape), _fixed(n_shape),
         _fixed((2, 8, SSM_LANES)), _fixed((1, D_SSM)), _fixed((D_SSM, D_SSM)), _fixed((1, D_SSM))),
        (SDS((t, D_SSM), BF16), SDS(m_shape, F32), SDS(m_shape, F32), SDS(n_shape, F32), SDS(n_shape, F32),
         SDS((2, 8, SSM_LANES), F32), SDS((1, D_SSM), F32), SDS((D_SSM, D_SSM), F32), SDS((1, D_SSM), F32)),
        (za, y2p, doa, carries, sp["perm"], sp["permt"], sp["mre"], sp["mim"], sp["mtre"], sp["mtim"],
         sp["nre"], sp["nim"], sp["ntre"], sp["ntim"], sp["a"], sp["ap"], sp["dskip"], sp["glu_w"],
         sp["glu_wt"], sp["glu_b"]),
        scratch=[pltpu.VMEM((tb, SSM_LANES), F32), pltpu.VMEM((tb, SSM_LANES), F32),
                 pltpu.VMEM((tb, SSM_LANES), F32), pltpu.VMEM((tb, SSM_LANES), F32),
                 pltpu.VMEM((2, 8, SSM_LANES), F32), pltpu.VMEM((2, SSM_LANES), F32),
                 pltpu.VMEM((2, SSM_LANES), F32)],
        sem=("arbitrary", "arbitrary"), bg=bg)


def _gmlp_spatial(ws_ref, vb):
    lane = lax.broadcasted_iota(jnp.int32, (CHUNK, 128), 1)
    parts = []
    for j in range(GMLP_HEADS // 2):
        vp = vb[:, 128 * j:128 * (j + 1)]
        parts.append(jnp.where(lane < GMLP_HEAD_DIM, _dot(ws_ref[2 * j], vp), _dot(ws_ref[2 * j + 1], vp)))
    return jnp.concatenate(parts, axis=1)


def _gmlp_fwd(zuv, ln_g, ln_b, wsm, bias, bg=None):
    t = zuv.shape[0]
    chunks = min(GMLP_CHUNKS, t // CHUNK)

    def body(z_ref, g_ref, b_ref, ws_ref, bias_ref, out_ref, outt_ref):
        for ch in range(chunks):
            rows = slice(ch * CHUNK, (ch + 1) * CHUNK)
            u = _gelu(z_ref[rows, 0:D_GMLP].astype(F32))
            v0 = _gelu(z_ref[rows, D_GMLP:2 * D_GMLP].astype(F32))
            v, _, _ = _ln_fwd(v0, g_ref[...], b_ref[...])
            s = _gmlp_spatial(ws_ref, v.astype(BF16)) + bias_ref[...]
            out = (u * s).astype(BF16)
            out_ref[rows, :] = out
            outt_ref[:, rows] = out.T

    step = chunks * CHUNK
    return _call(
        body, "gmlp_fwd", (t // step,),
        [_rows(step, 2 * D_GMLP), _fixed((1, D_GMLP)), _fixed((1, D_GMLP)),
         _fixed((GMLP_HEADS, CHUNK, CHUNK)), _fixed((CHUNK, D_GMLP))],
        (_rows(step, D_GMLP), _cols(D_GMLP, step)), (SDS((t, D_GMLP), BF16), SDS((D_GMLP, t), BF16)),
        (zuv, ln_g, ln_b, wsm, bias), sem=("parallel",), bg=bg)


def _gmlp_bwd(zuv, dgm, ln_g, ln_b, wsm, wsmt, bias, bg=None):
    t = zuv.shape[0]
    chunks = min(GMLP_CHUNKS, t // CHUNK)

    def body(z_ref, d_ref, g_ref, b_ref, ws_ref, wst_ref, bias_ref,
             dz_ref, dws_ref, dbias_ref, dg_ref, db_ref):
        @pl.when(pl.program_id(0) == 0)
        def _():
            for r in (dws_ref, dbias_ref, dg_ref, db_ref):
                r[...] = jnp.zeros_like(r)

        gam = g_ref[...]
        lane = lax.broadcasted_iota(jnp.int32, (CHUNK, 128), 1)
        tril = (lax.broadcasted_iota(jnp.int32, (CHUNK, CHUNK), 0)
                >= lax.broadcasted_iota(jnp.int32, (CHUNK, CHUNK), 1))
        zero_b = jnp.zeros((CHUNK, 128), BF16)
        for ch in range(chunks):
            rows = slice(ch * CHUNK, (ch + 1) * CHUNK)
            zu = z_ref[rows, 0:D_GMLP].astype(F32)
            zv = z_ref[rows, D_GMLP:2 * D_GMLP].astype(F32)
            u = _gelu(zu)
            v0 = _gelu(zv)
            v, vhat, rstd = _ln_fwd(v0, gam, b_ref[...])
            vb = v.astype(BF16)
            s = _gmlp_spatial(ws_ref, vb) + bias_ref[...]
            d = d_ref[rows, :]
            dz_ref[rows, 0:D_GMLP] = (d * s * _gelu_grad(zu)).astype(BF16)
            ds = d * u
            dbias_ref[...] += ds
            dsb = ds.astype(BF16)
            parts = []
            for j in range(GMLP_HEADS // 2):
                dsp = dsb[:, 128 * j:128 * (j + 1)]
                vp = vb[:, 128 * j:128 * (j + 1)]
                parts.append(jnp.where(lane < GMLP_HEAD_DIM, _dot(wst_ref[2 * j], dsp),
                                       _dot(wst_ref[2 * j + 1], dsp)))
                lo = jnp.where(lane < GMLP_HEAD_DIM, dsp, zero_b)
                hi = jnp.where(lane < GMLP_HEAD_DIM, zero_b, dsp)
                dws_ref[2 * j] += jnp.where(tril, _dot_nt(lo, vp), 0.0)
                dws_ref[2 * j + 1] += jnp.where(tril, _dot_nt(hi, vp), 0.0)
            dv = jnp.concatenate(parts, axis=1)
            dg_ref[...] += jnp.sum(dv * vhat, axis=0, keepdims=True)
            db_ref[...] += jnp.sum(dv, axis=0, keepdims=True)
            dz_ref[rows, D_GMLP:2 * D_GMLP] = (_ln_bwd(dv, vhat, rstd, gam) * _gelu_grad(zv)).astype(BF16)

    step = chunks * CHUNK
    return _call(
        body, "gmlp_bwd", (t // step,),
        [_rows(step, 2 * D_GMLP), _rows(step, D_GMLP), _fixed((1, D_GMLP)), _fixed((1, D_GMLP)),
         _fixed((GMLP_HEADS, CHUNK, CHUNK)), _fixed((GMLP_HEADS, CHUNK, CHUNK)), _fixed((CHUNK, D_GMLP))],
        (_rows(step, 2 * D_GMLP), _fixed((GMLP_HEADS, CHUNK, CHUNK)), _fixed((CHUNK, D_GMLP)),
         _fixed((1, D_GMLP)), _fixed((1, D_GMLP))),
        (SDS((t, 2 * D_GMLP), BF16), SDS((GMLP_HEADS, CHUNK, CHUNK), F32), SDS((CHUNK, D_GMLP), F32),
         SDS((1, D_GMLP), F32), SDS((1, D_GMLP), F32)),
        (zuv, dgm, ln_g, ln_b, wsm, wsmt, bias), sem=("arbitrary",), bg=bg)


def _mixout_fwd(x1, s5o, gm, gab, ua, ub, wmo, g, b, tm, bg=None):
    t = x1.shape[0]

    def body(x_ref, s_ref, m_ref, gab_ref, ua_ref, ub_ref, wmo_ref, g_ref, b_ref,
             xn_ref, xh_ref, rstd_ref):
        ya = _dot(s_ref[...], ua_ref[...])
        yb = _dot(m_ref[...], ub_ref[...])
        mix = (_sigmoid(gab_ref[:, 0:D_MODEL].astype(F32)) * ya
               + _sigmoid(gab_ref[:, D_MODEL:2 * D_MODEL].astype(F32)) * yb)
        r = ALPHA * x_ref[...] + _dot(mix.astype(BF16), wmo_ref[...])
        y, xh, rstd = _ln_fwd(r, g_ref[...], b_ref[...])
        xn_ref[...] = y
        xh_ref[...] = xh
        rstd_ref[...] = rstd

    return _call(
        body, "mixout_fwd", (t // tm,),
        [_rows(tm, D_MODEL), _rows(tm, D_SSM), _rows(tm, D_GMLP), _rows(tm, 2 * D_MODEL),
         _resident((D_SSM, D_MODEL)), _resident((D_GMLP, D_MODEL)), _resident((D_MODEL, D_MODEL)),
         _fixed((1, D_MODEL)), _fixed((1, D_MODEL))],
        (_rows(tm, D_MODEL), _rows(tm, D_MODEL), _rows(tm, 1)),
        (SDS((t, D_MODEL), F32), SDS((t, D_MODEL), F32), SDS((t, 1), F32)),
        (x1, s5o, gm, gab, ua, ub, wmo, g, b), sem=("parallel",), bg=bg)


def _mixout_bwd(dx2, xh, rstd, s5o, gm, gab, ua, ub, wmo, g, tm, bg=None):
    t = dx2.shape[0]

    def body(d_ref, xh_ref, rstd_ref, s_ref, m_ref, gab_ref, ua_ref, ub_ref, wmo_ref, g_ref,
             dx1_ref, dmx_ref, mb_ref, dya_ref, dyb_ref, ds5_ref, dgm_ref, dgab_ref, dg_ref, db_ref):
        @pl.when(pl.program_id(0) == 0)
        def _():
            dg_ref[...] = jnp.zeros_like(dg_ref)
            db_ref[...] = jnp.zeros_like(db_ref)

        dy = d_ref[...]
        xhv = xh_ref[...]
        dr = _ln_bwd(dy, xhv, rstd_ref[...], g_ref[...])
        dg_ref[...] += jnp.sum(dy * xhv, axis=0, keepdims=True)
        db_ref[...] += jnp.sum(dy, axis=0, keepdims=True)
        dx1_ref[...] = ALPHA * dr
        drb = dr.astype(BF16)
        dmx_ref[...] = drb
        dm = _dot_nt(drb, wmo_ref[...])
        ya = _dot(s_ref[...], ua_ref[...])
        yb = _dot(m_ref[...], ub_ref[...])
        sa = _sigmoid(gab_ref[:, 0:D_MODEL].astype(F32))
        sb = _sigmoid(gab_ref[:, D_MODEL:2 * D_MODEL].astype(F32))
        mb_ref[...] = (sa * ya + sb * yb).astype(BF16).T
        dya = (dm * sa).astype(BF16)
        dyb = (dm * sb).astype(BF16)
        dya_ref[...] = dya
        dyb_ref[...] = dyb
        dgab_ref[:, 0:D_MODEL] = (dm * ya * sa * (1.0 - sa)).astype(BF16)
        dgab_ref[:, D_MODEL:2 * D_MODEL] = (dm * yb * sb * (1.0 - sb)).astype(BF16)
        ds5_ref[...] = _dot_nt(dya, ua_ref[...])
        dgm_ref[...] = _dot_nt(dyb, ub_ref[...])

    return _call(
        body, "mixout_bwd", (t // tm,),
        [_rows(tm, D_MODEL), _rows(tm, D_MODEL), _rows(tm, 1), _rows(tm, D_SSM), _rows(tm, D_GMLP),
         _rows(tm, 2 * D_MODEL), _resident((D_SSM, D_MODEL)), _resident((D_GMLP, D_MODEL)),
         _resident((D_MODEL, D_MODEL)), _fixed((1, D_MODEL))],
        (_rows(tm, D_MODEL), _rows(tm, D_MODEL), _cols(D_MODEL, tm), _rows(tm, D_MODEL),
         _rows(tm, D_MODEL), _rows(tm, D_SSM), _rows(tm, D_GMLP), _rows(tm, 2 * D_MODEL),
         _fixed((1, D_MODEL)), _fixed((1, D_MODEL))),
        (SDS((t, D_MODEL), F32), SDS((t, D_MODEL), BF16), SDS((D_MODEL, t), BF16),
         SDS((t, D_MODEL), BF16), SDS((t, D_MODEL), BF16), SDS((t, D_SSM), F32),
         SDS((t, D_GMLP), F32), SDS((t, 2 * D_MODEL), BF16),
         SDS((1, D_MODEL), F32), SDS((1, D_MODEL), F32)),
        (dx2, xh, rstd, s5o, gm, gab, ua, ub, wmo, g), sem=("arbitrary",), bg=bg)


def _s5_discretise(lre, lim, log_dt, bre, bim):
    dt = jnp.exp(log_dt)[:, None]
    mag = jnp.exp(lre * dt)
    abr = mag * jnp.cos(lim * dt)
    abi = mag * jnp.sin(lim * dt)
    nr = abr - 1.0
    ni = abi
    den = lre * lre + lim * lim
    cr = ((nr * lre + ni * lim) / den)[..., None]
    ci = ((ni * lre - nr * lim) / den)[..., None]
    return abr, abi, cr * bre - ci * bim, cr * bim + ci * bre


def _block_diag_in(bb):
    v = bb.reshape(S5_BLOCKS, 8, SSM_STATE, SSM_GROUP_CH).transpose(0, 1, 3, 2)
    return jnp.einsum("bgip,gh->bgihp", v, jnp.eye(8, dtype=bb.dtype)).reshape(
        S5_BLOCKS, S5_BLOCK_IN, S5_BLOCK_ST)


def _block_diag_in_t(dm):
    v = dm.reshape(S5_BLOCKS, 8, SSM_GROUP_CH, 8, SSM_STATE)
    d = jnp.einsum("bgihp,gh->bgip", v, jnp.eye(8, dtype=dm.dtype))
    return d.transpose(0, 1, 3, 2).reshape(SSM_GROUPS, SSM_STATE, SSM_GROUP_CH)


def _block_diag_out(cc):
    v = cc.reshape(S5_BLOCKS, 8, SSM_GROUP_CH, SSM_STATE)
    return jnp.einsum("bgip,gh->bgphi", v, jnp.eye(8, dtype=cc.dtype)).reshape(
        S5_BLOCKS, S5_BLOCK_ST, S5_BLOCK_IN)


def _block_diag_out_t(dn):
    v = dn.reshape(S5_BLOCKS, 8, SSM_STATE, 8, SSM_GROUP_CH)
    d = jnp.einsum("bgphi,gh->bgip", v, jnp.eye(8, dtype=dn.dtype))
    return d.reshape(SSM_GROUPS, SSM_GROUP_CH, SSM_STATE)


def _s5_setup(lre, lim, log_dt, bre, bim, cre, cim, d_skip, glu_w, glu_b, tb):
    seg = tb // 8
    abr, abi, bbr, bbi = _s5_discretise(lre, lim, log_dt, bre, bim)
    pr, pi = abr, abi
    for _ in range(int(math.log2(seg))):
        pr, pi = pr * pr - pi * pi, 2.0 * pr * pi
    rows = jnp.arange(tb)
    src = (rows % 8) * seg + rows // 8
    perm = (src[:, None] == jnp.arange(tb)[None, :]).astype(BF16)
    mre = _block_diag_in(bbr)
    mim = _block_diag_in(bbi)
    nre = _block_diag_out(cre)
    nim = _block_diag_out(cim)
    return {
        "perm": perm, "permt": perm.T,
        "mre": mre.astype(BF16), "mim": mim.astype(BF16),
        "mtre": mre.transpose(0, 2, 1).astype(BF16), "mtim": mim.transpose(0, 2, 1).astype(BF16),
        "nre": nre.astype(BF16), "nim": nim.astype(BF16),
        "ntre": nre.transpose(0, 2, 1).astype(BF16), "ntim": nim.transpose(0, 2, 1).astype(BF16),
        "a": jnp.stack([abr.reshape(-1), abi.reshape(-1)]),
        "ap": jnp.stack([pr.reshape(-1), pi.reshape(-1)]),
        "dskip": d_skip.reshape(1, D_SSM), "glu_w": glu_w, "glu_wt": glu_w.T,
        "glu_b": glu_b.reshape(1, D_SSM),
    }


BIG = ("ffn1_w_in", "ffn1_w_out", "mix_w_in", "ssm_glu_w", "up_a", "up_b", "mix_w_out",
       "ffn2_w_in", "ffn2_w_out", "ple_w_proj", "ple_w_gate")
BIG_AXIS = {"ffn1_w_in": 1, "ffn1_w_out": 0, "mix_w_in": 1, "ssm_glu_w": 0, "up_a": 1, "up_b": 1,
            "mix_w_out": 0, "ffn2_w_in": 1, "ffn2_w_out": 0, "ple_w_proj": 1, "ple_w_gate": 0}
SHARD_MAJOR = 2
GATHER_AXIS = dict(BIG_AXIS, ffn1_w_in=SHARD_MAJOR, ffn2_w_in=SHARD_MAJOR)
GATHER_ORDER = (("ffn1_w_in",), ("ffn1_w_out",), ("mix_w_in",), ("ssm_glu_w", "up_a", "up_b", "mix_w_out"),
                ("ffn2_w_in",), ("ffn2_w_out", "ple_w_gate", "ple_w_proj"))
GATHER_FIRST_ID = 1
REDUCE_FIRST_ID = 7
SMALL = ("ln1_g", "ln1_b", "ssm_lambda_re", "ssm_lambda_im", "ssm_log_dt", "ssm_b_re", "ssm_b_im",
         "ssm_c_re", "ssm_c_im", "ssm_d", "ssm_glu_b", "gmlp_ln_g", "gmlp_ln_b", "gmlp_w_s",
         "gmlp_b_s", "ln2_g", "ln2_b", "ln3_g", "ln3_b")
SMALL_VIEW = {"ssm_b_re": (SSM_GROUPS, SSM_STATE * SSM_GROUP_CH), "ssm_b_im": (SSM_GROUPS, SSM_STATE * SSM_GROUP_CH)}


def _small_view(k, a):
    return a.reshape(SMALL_VIEW[k]) if k in SMALL_VIEW else a


def _place():
    return lax.axis_index("x"), lax.axis_index("y"), lax.axis_index("c")


def _other_chips(x, y):
    return [(1 - x, y), (x, 1 - y), (1 - x, 1 - y)]


def _window(ref, shard_shape, axis, chip, half):
    r, c = shard_shape
    hr = r // 2
    if axis == SHARD_MAJOR:
        return ref.at[chip] if half is None else ref.at[chip, pl.ds(half * hr, hr), :]
    if axis == 0:
        if half is None:
            return ref.at[pl.ds(chip * r, r), :]
        return ref.at[pl.ds(chip * r + half * hr, hr), :]
    if half is None:
        return ref.at[:, pl.ds(chip * c, c)]
    return ref.at[pl.ds(half * hr, hr), pl.ds(chip * c, c)]


def _gather_weights(shards, axes):
    n = len(shards)
    shapes = [s.shape for s in shards]
    full = [{0: (4 * r, c), 1: (r, 4 * c), SHARD_MAJOR: (4, r, c)}[ax] for (r, c), ax in zip(shapes, axes)]

    def remote(sems, i, k, src, dst, to):
        return pltpu.make_async_remote_copy(src_ref=src, dst_ref=dst, send_sem=sems[0].at[6 * i + k],
                                            recv_sem=sems[1].at[6 * i + k], device_id=to, device_id_type=MESH)

    def own_copies(ins, outs, sems):
        x, y, c = _place()
        me = 2 * x + y
        cps = []
        for i in range(n):
            hr = shapes[i][0] // 2
            mine = ins[i].at[pl.ds(c * hr, hr), :]
            for j, (cx, cy) in enumerate(_other_chips(x, y)):
                cps.append(remote(sems, i, j, mine, _window(outs[i], shapes[i], axes[i], me, c), (cx, cy, c)))
        local = [pltpu.make_async_copy(ins[i], _window(outs[i], shapes[i], axes[i], me, None), sems[2].at[i])
                 for i in range(n)]
        return cps, local

    def start(ins, outs, sems):
        cps, local = own_copies(ins, outs, sems)
        for cp in local + cps:
            cp.start()

    def finish(ins, outs, sems):
        x, y, c = _place()
        sibling = (x, y, 1 - c)
        passed = []
        for j, (cx, cy) in enumerate(_other_chips(x, y)):
            for i in range(n):
                w = _window(outs[i], shapes[i], axes[i], 2 * cx + cy, c)
                remote(sems, i, j, w, w, (cx, cy, c)).wait_recv()
                cp = remote(sems, i, 3 + j, w, w, sibling)
                cp.start()
                passed.append(cp)
        for j, (cx, cy) in enumerate(_other_chips(x, y)):
            for i in range(n):
                w = _window(outs[i], shapes[i], axes[i], 2 * cx + cy, 1 - c)
                remote(sems, i, 3 + j, w, w, sibling).wait_recv()
        cps, local = own_copies(ins, outs, sems)
        for cp in cps + passed:
            cp.wait_send()
        for cp in local:
            cp.wait()

    return _Exchange(shards, [SDS(f, BF16) for f in full],
                     [pltpu.SemaphoreType.DMA((6 * n,)), pltpu.SemaphoreType.DMA((6 * n,)),
                      pltpu.SemaphoreType.DMA((n,))], start, finish)


def _scatter_grads(parts, shapes, axes):
    n = len(parts)

    def copies(ins, outs, sems):
        x, y, c = _place()
        return [pltpu.make_async_remote_copy(
            src_ref=_window(ins[i], shapes[i], axes[i], 2 * cx + cy, None), dst_ref=outs[i].at[j],
            send_sem=sems[0].at[3 * i + j], recv_sem=sems[1].at[3 * i + j],
            device_id=(cx, cy, c), device_id_type=MESH)
            for i in range(n) for j, (cx, cy) in enumerate(_other_chips(x, y))]

    def start(ins, outs, sems):
        for cp in copies(ins, outs, sems):
            cp.start()

    def finish(ins, outs, sems):
        for cp in copies(ins, outs, sems):
            cp.wait()

    return _Exchange(parts, [SDS((3,) + tuple(s), BF16) for s in shapes],
                     [pltpu.SemaphoreType.DMA((3 * n,)), pltpu.SemaphoreType.DMA((3 * n,))], start, finish)


def _swap_halves(parts, shapes, axes):
    n = len(parts)

    def copies(ins, outs, sems):
        x, y, c = _place()
        cps = []
        for i in range(n):
            r, _ = shapes[i]
            hr = r // 2
            if axes[i] == 0:
                cps += [pltpu.make_async_remote_copy(
                    src_ref=ins[i].at[pl.ds(k * r + (1 - c) * hr, hr), :], dst_ref=outs[i].at[k],
                    send_sem=sems[0].at[i], recv_sem=sems[1].at[i], device_id=(x, y, 1 - c),
                    device_id_type=MESH) for k in range(4)]
            else:
                cps.append(pltpu.make_async_remote_copy(
                    src_ref=ins[i].at[pl.ds((1 - c) * hr, hr), :], dst_ref=outs[i],
                    send_sem=sems[0].at[i], recv_sem=sems[1].at[i], device_id=(x, y, 1 - c),
                    device_id_type=MESH))
        return cps

    def start(ins, outs, sems):
        for cp in copies(ins, outs, sems):
            cp.start()

    def finish(ins, outs, sems):
        x, y, c = _place()
        for i in range(n):
            pltpu.make_async_remote_copy(src_ref=outs[i], dst_ref=outs[i], send_sem=sems[0].at[i],
                                         recv_sem=sems[1].at[i], device_id=(x, y, 1 - c),
                                         device_id_type=MESH).wait()

    out = [SDS((4, r // 2, c), BF16) if ax == 0 else SDS((r // 2, 4 * c), BF16)
           for (r, c), ax in zip(shapes, axes)]
    return _Exchange(parts, out, [pltpu.SemaphoreType.DMA((n,)), pltpu.SemaphoreType.DMA((n,))], start, finish)


def _scatter_halves(pres, shapes):
    n = len(pres)

    def copies(ins, outs, sems):
        x, y, c = _place()
        return [pltpu.make_async_remote_copy(
            src_ref=ins[i].at[1 + j], dst_ref=outs[i].at[j], send_sem=sems[0].at[3 * i + j],
            recv_sem=sems[1].at[3 * i + j], device_id=(cx, cy, c), device_id_type=MESH)
            for i in range(n) for j, (cx, cy) in enumerate(_other_chips(x, y))]

    def start(ins, outs, sems):
        for cp in copies(ins, outs, sems):
            cp.start()

    def finish(ins, outs, sems):
        for cp in copies(ins, outs, sems):
            cp.wait()

    return _Exchange(pres, [SDS((3, r // 2, c), BF16) for r, c in shapes],
                     [pltpu.SemaphoreType.DMA((3 * n,)), pltpu.SemaphoreType.DMA((3 * n,))], start, finish)


def _swap_with_sibling(arrs):
    n = len(arrs)

    def copies(ins, outs, sems):
        x, y, c = _place()
        return [pltpu.make_async_remote_copy(src_ref=ins[i], dst_ref=outs[i], send_sem=sems[0].at[i],
                                             recv_sem=sems[1].at[i], device_id=(x, y, 1 - c),
                                             device_id_type=MESH) for i in range(n)]

    def start(ins, outs, sems):
        for cp in copies(ins, outs, sems):
            cp.start()

    def finish(ins, outs, sems):
        for cp in copies(ins, outs, sems):
            cp.wait()

    return _Exchange(arrs, [SDS(a.shape, a.dtype) for a in arrs],
                     [pltpu.SemaphoreType.DMA((n,)), pltpu.SemaphoreType.DMA((n,))], start, finish)


def _gather_small(arrs):
    n = len(arrs)

    def copy(sems, outs, i, k, block, to, src=None):
        px, py, pc = block
        dst = outs[i].at[4 * px + 2 * py + pc]
        return pltpu.make_async_remote_copy(
            src_ref=dst if src is None else src, dst_ref=dst, send_sem=sems[0].at[7 * i + k],
            recv_sem=sems[1].at[7 * i + k], device_id=to, device_id_type=MESH)

    def own_copies(ins, outs, sems):
        x, y, c = _place()
        cps = []
        for i in range(n):
            cps.append(copy(sems, outs, i, 0, (x, y, c), (x, y, 1 - c), src=ins[i]))
            for j, (cx, cy) in enumerate(_other_chips(x, y)):
                cps.append(copy(sems, outs, i, 1 + j, (x, y, c), (cx, cy, c), src=ins[i]))
        local = [pltpu.make_async_copy(ins[i], outs[i].at[4 * x + 2 * y + c], sems[2].at[i]) for i in range(n)]
        return cps, local

    def start(ins, outs, sems):
        cps, local = own_copies(ins, outs, sems)
        for cp in local + cps:
            cp.start()

    def finish(ins, outs, sems):
        x, y, c = _place()
        passed = []
        for j, (cx, cy) in enumerate(_other_chips(x, y)):
            for i in range(n):
                copy(sems, outs, i, 1 + j, (cx, cy, c), (x, y, c)).wait_recv()
                cp = copy(sems, outs, i, 4 + j, (cx, cy, c), (x, y, 1 - c))
                cp.start()
                passed.append(cp)
        for i in range(n):
            copy(sems, outs, i, 0, (x, y, 1 - c), (x, y, c)).wait_recv()
            for j, (cx, cy) in enumerate(_other_chips(x, y)):
                copy(sems, outs, i, 4 + j, (cx, cy, 1 - c), (x, y, c)).wait_recv()
        cps, local = own_copies(ins, outs, sems)
        for cp in cps + passed:
            cp.wait_send()
        for cp in local:
            cp.wait()

    return _Exchange(arrs, [SDS((N_DEV,) + a.shape, F32) for a in arrs],
                     [pltpu.SemaphoreType.DMA((7 * n,)), pltpu.SemaphoreType.DMA((7 * n,)),
                      pltpu.SemaphoreType.DMA((n,))], start, finish)


def _local_step(x, p, tgt, wb, ws, shards=None, opt=None):
    bsz, seq, _ = x.shape
    t = bsz * seq
    tm = min(256, t)
    tb = min(S5_TIME_BLOCK, seq)
    x0 = x.reshape(t, D_MODEL)
    p0 = p.reshape(t, PLE_DIM)
    tg = tgt.reshape(t, D_MODEL)
    row = lambda v: v.reshape(1, -1)
    dist = shards is not None
    wb = dict(wb)
    recv, sums, other, gathered = {}, {}, {}, {}
    gb = {}
    gs = {}
    shape_of, axis_of = {}, {}
    chip = None
    if dist:
        shape_of = {k: tuple(shards[k].shape) for k in BIG}
        axis_of = dict(BIG_AXIS)
        for q in range(LAST_PIECES):
            shape_of[LAST_PIECE % q] = (D_MODEL // LAST_PIECES, shape_of["ffn1_w_in"][1])
            axis_of[LAST_PIECE % q] = 1
        xi, yi, ci = _place()
        chip = (2 * xi + yi).astype(jnp.int32).reshape(1)
        ids = jnp.stack([2 * xi + yi] + [2 * cx + cy for cx, cy in _other_chips(xi, yi)] + [ci]).astype(jnp.int32)
    halfbuf, pre = {}, {}

    def gather(names):
        return _gather_weights([shards[k] for k in names], [GATHER_AXIS[k] for k in names]) if dist else None

    def exchange(scat=(), swap=(), halves=(), scat2=(), swap2=(), extra=None, after=None):
        if not dist:
            return None, []
        after = order[0] if after is None else after
        parts, tags = [], []
        if scat:
            parts.append(_scatter_grads([gb[k][1] for k in scat], [shape_of[k] for k in scat],
                                        [axis_of[k] for k in scat]))
            tags.append((recv, scat))
        if swap:
            for k in swap:
                sums[k] = order[0] = _sum_blocks(gb[k][0], recv[k], shape_of[k], axis_of[k], chip, "sum_" + k,
                                                 order[0])
            parts.append(_swap_with_sibling([sums[k] for k in swap]))
            tags.append((other, swap))
        if halves:
            parts.append(_swap_halves([gb[k][1] for k in halves], [shape_of[k] for k in halves],
                                      [axis_of[k] for k in halves]))
            tags.append((halfbuf, halves))
        if scat2:
            for k in scat2:
                pre[k] = _presum(gb[k][0], halfbuf[k], shape_of[k], axis_of[k], ids, "presum_" + k, order[0])
                order[0] = pre[k][0]
            parts.append(_scatter_halves([pre[k][1] for k in scat2], [shape_of[k] for k in scat2]))
            tags.append((recv, scat2))
        if swap2:
            for k in swap2:
                sums[k] = order[0] = _sum_half(pre[k][0], recv[k], "sum_" + k, order[0])
            parts.append(_swap_with_sibling([sums[k] for k in swap2]))
            tags.append((other, swap2))
        if extra is not None:
            parts.append(extra[0])
            tags.append((extra[1], extra[2]))
        return (_join(parts), tags) if parts else (None, [])

    def take(ex_tags, got):
        ex, tags = ex_tags
        if ex is not None:
            for (dst, names), (o0, o1) in zip(tags, ex.cuts):
                dst.update(zip(names, got[o0:o1]))

    order = [None]

    def ordered(builder, *args, **kw):
        res = builder(*args, bg=order[0] if dist else None, **kw)
        order[0] = res[0][0]
        return res

    launched = []

    def launch(ex_tags):
        if ex_tags[0] is not None:
            n = len(launched)
            launched.append(n)
            take(ex_tags, _run_exchange_on_sequencer(ex_tags[0], "reduce_%d" % n, REDUCE_FIRST_ID + n))

    small_shape = {k: _small_view(k, v).shape for k, v in ws.items()}
    small_shape["loss_rows"] = (1, D_MODEL)
    ws = {k: v if (v.ndim == 2 and k != "ssm_log_dt") else v[0] for k, v in ws.items()}
    tril = jnp.tril(jnp.ones((CHUNK, CHUNK), dtype=bool))
    wsm = jnp.where(tril[None], ws["gmlp_w_s"], 0.0)
    wsm_b = wsm.astype(BF16)
    wsmt_b = wsm.transpose(0, 2, 1).astype(BF16)
    bias = jnp.repeat(ws["gmlp_b_s"].T, GMLP_HEAD_DIM, axis=1)

    tf = min(512, t)
    if dist:
        for gi, names in enumerate(GATHER_ORDER):
            wb.update(zip(names, _run_exchange_on_sequencer(gather(names), "gather_%d" % gi, GATHER_FIRST_ID + gi)))
    (x0b, h1, a1), _ = _ffn_proj(x0, wb["ffn1_w_in"], tf, "ffn1_proj")
    (x1, xh1, rstd1), _ = _ffn_out(x0, a1, wb["ffn1_w_out"], row(ws["ln1_g"]), row(ws["ln1_b"]), tf, "ffn1_out")
    sp = _s5_setup(ws["ssm_lambda_re"], ws["ssm_lambda_im"], ws["ssm_log_dt"], ws["ssm_b_re"],
                   ws["ssm_b_im"], ws["ssm_c_re"], ws["ssm_c_im"], ws["ssm_d"], wb["ssm_glu_w"],
                   ws["ssm_glu_b"], tb)
    (x1b, za, zuv, gab), _ = _mixin_fwd(x1, wb["mix_w_in"], tf)
    (s5o, s5ot, y2p, carries), _ = _s5_fwd(za, sp, bsz, seq, tb)
    (gm, gmt), _ = _gmlp_fwd(zuv, row(ws["gmlp_ln_g"]), row(ws["gmlp_ln_b"]), wsm_b, bias)
    (x2, xh2, rstd2), _ = _mixout_fwd(x1, s5o, gm, gab, wb["up_a"], wb["up_b"], wb["mix_w_out"],
                                           row(ws["ln2_g"]), row(ws["ln2_b"]), tf)
    (x2b, h2, a2), _ = _ffn_proj(x2, wb["ffn2_w_in"], tf, "ffn2_proj")
    (xh3, rstd3, dx3, x3b, pb, dq, de, loss_rows), _ = _ffn_out_loss(
        x2, a2, wb["ffn2_w_out"], row(ws["ln3_g"]), row(ws["ln3_b"]), p0, tg, wb["ple_w_gate"], wb["ple_w_proj"], tf)
    order[0] = dx3
    gb["ple_w_gate"], _ = ordered(_tn_matmul, x3b, dq, "dw_ple_gate", 1024, 1024, a_t=True)
    gb["ple_w_proj"], _ = ordered(_tn_matmul, pb, de, "dw_ple_proj", 256, 1024, a_t=True)
    launch(exchange(scat=("ple_w_gate", "ple_w_proj")))
    (dx2, dh2, df2, gs["ln3_g"], gs["ln3_b"]), _ = ordered(
        _ffn_bwd, dx3, xh3, rstd3, h2, wb["ffn2_w_in"], wb["ffn2_w_out"], row(ws["ln3_g"]), tm, "ffn2_bwd")
    gb["ffn2_w_out"], _ = ordered(_tn_matmul, a2, df2, "dw_ffn2_out", 1408, 1024)
    launch(exchange(scat=("ffn2_w_out",)))
    gb["ffn2_w_in"], _ = ordered(_tn_matmul, x2b, dh2, "dw_ffn2_in", 1024, 1408, a_t=True)
    launch(exchange(scat=("ffn2_w_in",), swap=("ple_w_gate", "ple_w_proj")))
    (dx1a, dmx, mb, dya, dyb, ds5, dgm, dgab, gs["ln2_g"], gs["ln2_b"]), _ = ordered(
        _mixout_bwd, dx2, xh2, rstd2, s5o, gm, gab, wb["up_a"], wb["up_b"], wb["mix_w_out"], row(ws["ln2_g"]), tf)
    gb["mix_w_out"], _ = ordered(_tn_matmul, mb, dmx, "dw_mix_out", 1024, 1024, a_t=True)
    gb["up_a"], _ = ordered(_tn_matmul, s5ot, dya, "dw_up_a", 512, 1024, a_t=True)
    gb["up_b"], _ = ordered(_tn_matmul, gmt, dyb, "dw_up_b", 512, 1024, a_t=True)
    launch(exchange(scat=("mix_w_out", "up_a", "up_b"), swap=("ffn2_w_out",)))
    (dza, dmr, dmi, dnr, dni, da, ddsk, dgw, dgb), _ = ordered(_s5_bwd, za, y2p, ds5, carries, sp, bsz, seq, tb)
    gb["ssm_glu_w"] = (dgw, dgw.astype(BF16))
    launch(exchange(scat=("ssm_glu_w",), swap=("ffn2_w_in",)))
    (dzuv, dws, dbias, gs["gmlp_ln_g"], gs["gmlp_ln_b"]), _ = ordered(
        _gmlp_bwd, zuv, dgm, row(ws["gmlp_ln_g"]), row(ws["gmlp_ln_b"]), wsm_b, wsmt_b, bias)
    (dx1,), _ = ordered(_mixin_bwd, dx1a, dza, dzuv, dgab, wb["mix_w_in"], tf)
    g_mi, _ = ordered(_tn_matmul, x1b, dza, "dw_mix_in_a", 1024, 512, 0, 3584, a_t=True)
    g_mi, _ = ordered(_tn_matmul, x1b, dzuv, "dw_mix_in_uv", 1024, 512, 1, 3584, g_mi, a_t=True)
    gb["mix_w_in"], _ = ordered(_tn_matmul, x1b, dgab, "dw_mix_in_g", 1024, 512, 3, 3584, g_mi, a_t=True)
    launch(exchange(swap=("mix_w_out", "up_a", "up_b", "ssm_glu_w")))

    d_abr = da[0].sum(axis=0).reshape(SSM_GROUPS, SSM_STATE)
    d_abi = da[1].sum(axis=0).reshape(SSM_GROUPS, SSM_STATE)
    _, vjp = jax.vjp(_s5_discretise, ws["ssm_lambda_re"], ws["ssm_lambda_im"], ws["ssm_log_dt"],
                     ws["ssm_b_re"], ws["ssm_b_im"])
    (gs["ssm_lambda_re"], gs["ssm_lambda_im"], gs["ssm_log_dt"], gs["ssm_b_re"], gs["ssm_b_im"]) = vjp(
        (d_abr, d_abi, _block_diag_in_t(dmr), _block_diag_in_t(dmi)))
    gs["ssm_c_re"] = _block_diag_out_t(dnr)
    gs["ssm_c_im"] = _block_diag_out_t(dni)
    gs["ssm_d"] = ddsk
    gs["ssm_glu_b"] = dgb
    gs["gmlp_w_s"] = dws
    gs["gmlp_b_s"] = dbias.reshape(CHUNK, GMLP_HEADS, GMLP_HEAD_DIM).sum(axis=-1).T
    gs["loss_rows"] = loss_rows

    def small_gather(names):
        return (_gather_small([gs[k].reshape(small_shape[k]) for k in names]), gathered, names) if dist else None

    late = ("ln1_g", "ln1_b")
    launch(exchange(scat=("mix_w_in",), extra=small_gather(tuple(k for k in SMALL + ("loss_rows",) if k not in late))))
    (dx0, dh1, df1, gs["ln1_g"], gs["ln1_b"]), _ = ordered(
        _ffn_bwd, dx1, xh1, rstd1, h1, wb["ffn1_w_in"], wb["ffn1_w_out"], row(ws["ln1_g"]), tm, "ffn1_bwd")
    grad_x = dx0.reshape(bsz, seq, D_MODEL)
    if not dist:
        gb["ffn1_w_out"], _ = _tn_matmul(a1, df1, "dw_ffn1_out", 1408, 1024)
        gb["ffn1_w_in"], _ = _tn_matmul(x0b, dh1, "dw_ffn1_in", 1024, 1408, a_t=True)
        return (loss_rows, grad_x, gb, {k: gs[k].reshape(small_shape[k]) for k in SMALL}, sums, other, gathered,
                None, {})
    launch(exchange(extra=small_gather(late)))
    gb["ffn1_w_out"], _ = ordered(_tn_matmul, a1, df1, "dw_ffn1_out", 1408, 1024)
    last = ["ffn1_w_out"] + [LAST_PIECE % q for q in range(LAST_PIECES)]
    fillers = (("ffn2_w_in", "mix_w_in", "ple_w_gate"),
               ("ffn2_w_out", "mix_w_out", "up_a", "up_b", "ssm_glu_w", "ple_w_proj"))
    out = {}
    for i in range(1, len(last) + 3):
        stage = lambda d: tuple(last[i - d:i - d + 1]) if 0 <= i - d < len(last) else ()
        launch(exchange(halves=stage(1), scat2=stage(2), swap2=stage(3), swap=("mix_w_in",) if i == 2 else ()))
        if i < len(last):
            gb[last[i]], _ = ordered(_tn_matmul, x0b, dh1, "dw_" + last[i], D_MODEL // LAST_PIECES, 1408,
                                     a_cols=(i - 1, 1), a_t=True)
        elif i - len(last) < len(fillers):
            for k in fillers[i - len(last)]:
                w, m, v = opt[k]
                out[k] = _adam_big(w, sums[k], other[k], m, v, "adam_" + k, after=order[0])
                order[0] = out[k][1]
    return loss_rows, grad_x, gb, gs, sums, other, gathered, ids, out


def _adamw(w, g, m, v):
    m = ADAM_B1 * m + (1.0 - ADAM_B1) * g
    v = ADAM_B2 * v + (1.0 - ADAM_B2) * (g * g)
    m_hat = m / ADAM_C1
    v_hat = v / ADAM_C2
    delta = -ADAM_LR * (m_hat / (jnp.sqrt(v_hat) + ADAM_EPS) + ADAM_WD * w)
    return delta, m, v


def _pinned(after):
    return ([pl.BlockSpec(memory_space=pl.ANY)], [after]) if after is not None else ([], [])


def _sum_blocks(part, recv, shape, axis, chip, name, after=None):
    r, c = shape
    rb = r // ROW_STEPS

    def body(chip_ref, p_ref, r_ref, *rest):
        rest[-1][...] = (p_ref[...] + r_ref[0].astype(F32) + r_ref[1].astype(F32) + r_ref[2].astype(F32))

    if axis == 0:
        own = pl.BlockSpec((rb, c), lambda i, k: (k[0] * ROW_STEPS + i, 0))
    else:
        own = pl.BlockSpec((rb, c), lambda i, k: (i, k[0]))
    pin_specs, pin_args = _pinned(after)
    grid_spec = pltpu.PrefetchScalarGridSpec(
        num_scalar_prefetch=1, grid=(ROW_STEPS,),
        in_specs=[own, pl.BlockSpec((3, rb, c), lambda i, k: (0, i, 0))] + pin_specs,
        out_specs=pl.BlockSpec((rb, c), lambda i, k: (i, 0)))
    return pl.pallas_call(body, name=name, out_shape=SDS((r, c), F32), grid_spec=grid_spec,
                          compiler_params=_params(("parallel",)))(chip, part, recv, *pin_args)


def _presum(part, half, shape, axis, ids, name, after=None):
    r, c = shape
    rb = r // 2

    def body(ids_ref, p_ref, h_ref, *rest):
        of_ref, ob_ref = rest[-2:]
        s = p_ref[...] + h_ref[...].astype(F32)
        ob_ref[...] = s.astype(BF16)

        @pl.when(pl.program_id(1) == 0)
        def _():
            of_ref[...] = s

    if axis == 0:
        p_spec = pl.BlockSpec((rb, c), lambda i, t, ids: (ids[t] * 2 + ids[4] + i, 0))
        h_spec = pl.BlockSpec((None, rb, c), lambda i, t, ids: (ids[t], i, 0))
    else:
        p_spec = pl.BlockSpec((rb, c), lambda i, t, ids: (ids[4] + i, ids[t]))
        h_spec = pl.BlockSpec((rb, c), lambda i, t, ids: (i, ids[t]))
    pin_specs, pin_args = _pinned(after)
    grid_spec = pltpu.PrefetchScalarGridSpec(
        num_scalar_prefetch=1, grid=(1, 4), in_specs=[p_spec, h_spec] + pin_specs,
        out_specs=(pl.BlockSpec((rb, c), lambda i, t, ids: (i, 0)),
                   pl.BlockSpec((None, rb, c), lambda i, t, ids: (t, i, 0))))
    return pl.pallas_call(body, name=name, out_shape=(SDS((r // 2, c), F32), SDS((4, r // 2, c), BF16)),
                          grid_spec=grid_spec,
                          compiler_params=_params(("parallel", "arbitrary")))(ids, part, half, *pin_args)


def _sum_half(pre, recv, name, after=None):
    hr, c = pre.shape
    rb = hr

    def body(p_ref, r_ref, *rest):
        rest[-1][...] = (p_ref[...] + r_ref[0].astype(F32) + r_ref[1].astype(F32) + r_ref[2].astype(F32))

    spec = pl.BlockSpec((rb, c), lambda i: (i, 0))
    pin_specs, pin_args = _pinned(after)
    return pl.pallas_call(body, name=name, grid=(1,), out_shape=SDS((hr, c), F32),
                          in_specs=[spec, pl.BlockSpec((3, rb, c), lambda i: (0, i, 0))] + pin_specs,
                          out_specs=spec, compiler_params=_params(("parallel",)))(pre, recv, *pin_args)


def _adam_halves(w, mine, oth, m, v, ids, name, piece=0, prev=None):
    r, c = w.shape
    rb = mine.shape[0] // 2

    def body(ids_ref, w_ref, a_ref, b_ref, m_ref, v_ref, *rest):
        g_ref, d_ref, nm_ref, nv_ref = rest[-4:]
        g = jnp.where(pl.program_id(0) // 2 == ids_ref[4], a_ref[...], b_ref[...])
        g_ref[...] = g
        d_ref[...], nm_ref[...], nv_ref[...] = _adamw(w_ref[...], g, m_ref[...], v_ref[...])

    whole = pl.BlockSpec((rb, c), lambda i, ids: (i + 4 * piece, 0))
    part = pl.BlockSpec((rb, c), lambda i, ids: (i % 2, 0))
    in_specs = [whole, part, part, whole, whole]
    args = [w, mine, oth, m, v]
    aliases = {}
    if prev is not None:
        in_specs += [pl.BlockSpec(memory_space=pl.ANY)] * 4
        args += list(prev)
        aliases = {6: 0, 7: 1, 8: 2, 9: 3}
    grid_spec = pltpu.PrefetchScalarGridSpec(num_scalar_prefetch=1, grid=(4,), in_specs=in_specs,
                                             out_specs=(whole,) * 4)
    return pl.pallas_call(body, name=name, out_shape=tuple(SDS((r, c), F32) for _ in range(4)),
                          grid_spec=grid_spec, input_output_aliases=aliases,
                          compiler_params=_params(("parallel",)))(ids, *args)


def _adam_big(w, ga, gb, m, v, name, piece=0, prev=None, after=None):
    r, c = w.shape
    pr = ga.shape[0]
    steps = ROW_STEPS if pr == r else 2
    rb = pr // steps
    off = piece * steps

    def body(w_ref, ga_ref, gb_ref, m_ref, v_ref, *rest):
        g_ref, d_ref, nm_ref, nv_ref = rest[-4:]
        g = ga_ref[...] + gb_ref[...]
        g_ref[...] = g
        d_ref[...], nm_ref[...], nv_ref[...] = _adamw(w_ref[...], g, m_ref[...], v_ref[...])

    whole = pl.BlockSpec((rb, c), lambda i: (i + off, 0))
    part = pl.BlockSpec((rb, c), lambda i: (i, 0))
    in_specs = [whole, part, part, whole, whole]
    args = [w, ga, gb, m, v]
    aliases = {}
    if prev is not None:
        in_specs += [pl.BlockSpec(memory_space=pl.ANY)] * 4
        args += list(prev)
        aliases = {5: 0, 6: 1, 7: 2, 8: 3}
    if after is not None:
        in_specs.append(pl.BlockSpec(memory_space=pl.ANY))
        args.append(after)
    return pl.pallas_call(
        body, name=name, grid=(steps,), out_shape=tuple(SDS((r, c), F32) for _ in range(4)),
        in_specs=in_specs, out_specs=(whole,) * 4, input_output_aliases=aliases,
        compiler_params=_params(("parallel",)),
    )(*args)


def _adam_small(ws, gathered, ms, vs):
    n = len(ws)

    def body(*refs):
        w_refs, g_refs, m_refs, v_refs = refs[:n], refs[n:2 * n], refs[2 * n:3 * n], refs[3 * n:4 * n]
        outs = refs[4 * n:]
        for i in range(n):
            g = g_refs[i][0]
            for d in range(1, N_DEV):
                g = g + g_refs[i][d]
            delta, nm, nv = _adamw(w_refs[i][...], g, m_refs[i][...], v_refs[i][...])
            outs[i][...] = g
            outs[n + i][...] = delta
            outs[2 * n + i][...] = nm
            outs[3 * n + i][...] = nv

    vmem = pl.BlockSpec(memory_space=pltpu.VMEM)
    shapes = [w.shape for w in ws]
    return pl.pallas_call(
        body, name="adam_small", out_shape=tuple(SDS(s, F32) for s in shapes * 4),
        in_specs=[vmem] * (4 * n), out_specs=tuple([vmem] * (4 * n)),
        compiler_params=pltpu.CompilerParams(vmem_limit_bytes=VMEM_LIMIT_BYTES),
    )(*ws, *gathered, *ms, *vs)


def _sum_loss(gathered):
    def body(g_ref, o_ref):
        tot = g_ref[0]
        for d in range(1, N_DEV):
            tot = tot + g_ref[d]
        o_ref[...] = (0.5 / D_MODEL) * jnp.sum(tot, axis=1, keepdims=True)

    vmem = pl.BlockSpec(memory_space=pltpu.VMEM)
    return pl.pallas_call(body, name="sum_loss", out_shape=SDS((1, 1), F32), in_specs=[vmem],
                          out_specs=vmem)(gathered)


def kernel(x, p, ffn1_w_in, ffn1_w_out, ln1_g, ln1_b, mix_w_in, ssm_lambda_re, ssm_lambda_im, ssm_log_dt, ssm_b_re, ssm_b_im, ssm_c_re, ssm_c_im, ssm_d, ssm_glu_w, ssm_glu_b, gmlp_ln_g, gmlp_ln_b, gmlp_w_s, gmlp_b_s, up_a, up_b, mix_w_out, ln2_g, ln2_b, ffn2_w_in, ffn2_w_out, ln3_g, ln3_b, ple_w_proj, ple_w_gate, loss_target, m_ffn1_w_in, m_ffn1_w_out, m_ln1_g, m_ln1_b, m_mix_w_in, m_ssm_lambda_re, m_ssm_lambda_im, m_ssm_log_dt, m_ssm_b_re, m_ssm_b_im, m_ssm_c_re, m_ssm_c_im, m_ssm_d, m_ssm_glu_w, m_ssm_glu_b, m_gmlp_ln_g, m_gmlp_ln_b, m_gmlp_w_s, m_gmlp_b_s, m_up_a, m_up_b, m_mix_w_out, m_ln2_g, m_ln2_b, m_ffn2_w_in, m_ffn2_w_out, m_ln3_g, m_ln3_b, m_ple_w_proj, m_ple_w_gate, v_ffn1_w_in, v_ffn1_w_out, v_ln1_g, v_ln1_b, v_mix_w_in, v_ssm_lambda_re, v_ssm_lambda_im, v_ssm_log_dt, v_ssm_b_re, v_ssm_b_im, v_ssm_c_re, v_ssm_c_im, v_ssm_d, v_ssm_glu_w, v_ssm_glu_b, v_gmlp_ln_g, v_gmlp_ln_b, v_gmlp_w_s, v_gmlp_b_s, v_up_a, v_up_b, v_mix_w_out, v_ln2_g, v_ln2_b, v_ffn2_w_in, v_ffn2_w_out, v_ln3_g, v_ln3_b, v_ple_w_proj, v_ple_w_gate):
    given = dict(locals())
    order = ("ffn1_w_in", "ffn1_w_out", "ln1_g", "ln1_b", "mix_w_in", "ssm_lambda_re", "ssm_lambda_im",
             "ssm_log_dt", "ssm_b_re", "ssm_b_im", "ssm_c_re", "ssm_c_im", "ssm_d", "ssm_glu_w", "ssm_glu_b",
             "gmlp_ln_g", "gmlp_ln_b", "gmlp_w_s", "gmlp_b_s", "up_a", "up_b", "mix_w_out", "ln2_g", "ln2_b",
             "ffn2_w_in", "ffn2_w_out", "ln3_g", "ln3_b", "ple_w_proj", "ple_w_gate")
    assert set(order) == set(BIG + SMALL)

    shard = {k: given[k][0] for k in BIG}
    shard_b = {k: shard[k].astype(BF16) for k in BIG}
    opt = {k: (shard[k], given["m_" + k][0], given["v_" + k][0]) for k in BIG}
    loss_rows, grad_x, gb, gs, sums, other, gathered, ids, out = _local_step(
        x, given["p"][0], loss_target, {}, {k: given[k] for k in SMALL}, shard_b, opt)

    out = dict(out)
    for k in BIG:
        if k in out:
            continue
        moments = (given["m_" + k][0], given["v_" + k][0])
        if k == "ffn1_w_out":
            out[k] = _adam_halves(shard[k], sums[k], other[k], *moments, ids, "adam_" + k)
        elif k == "ffn1_w_in":
            for q in range(LAST_PIECES):
                kq = LAST_PIECE % q
                out[k] = _adam_halves(shard[k], sums[kq], other[kq], *moments, ids, "adam_" + kq, q, out.get(k))
        else:
            out[k] = _adam_big(shard[k], sums[k], other[k], *moments, "adam_" + k,
                               after=gb[LAST_PIECE % (LAST_PIECES - 1)][0])

    res = _adam_small([_small_view(k, given[k]) for k in SMALL], [gathered[k] for k in SMALL],
                      [_small_view(k, given["m_" + k]) for k in SMALL],
                      [_small_view(k, given["v_" + k]) for k in SMALL])
    ns = len(SMALL)
    for i, k in enumerate(SMALL):
        out[k] = tuple(res[j * ns + i].reshape(given[k].shape) for j in range(4))
    loss = _sum_loss(gathered["loss_rows"]).reshape(())

    lead = lambda k, j: out[k][j][None] if k in BIG else out[k][j]
    return (loss, grad_x, *[lead(k, 0) for k in order], *[lead(k, 1) for k in order],
            *[lead(k, 2) for k in order], *[lead(k, 3) for k in order])
```

```python
import math

import jax
import jax.numpy as jnp
from jax import lax
from jax.experimental import pallas as pl
from jax.experimental.pallas import tpu as pltpu
from jax.experimental.pallas import tpu_sc as plsc

F32 = jnp.float32
BF16 = jnp.bfloat16
MESH = pl.DeviceIdType.MESH
SDS = jax.ShapeDtypeStruct

D_MODEL = 1024
D_FF = 2816
D_SSM = 512
D_GMLP = 512
SSM_GROUPS = 32
SSM_GROUP_CH = 16
SSM_STATE = 64
SSM_LANES = SSM_GROUPS * SSM_STATE
GMLP_HEADS = 8
GMLP_HEAD_DIM = 64
CHUNK = 128
PLE_DIM = 256
LN_EPS = 1e-5
ALPHA = 2.0 ** 0.25

ADAM_LR = 0.001
ADAM_B1 = 0.9
ADAM_B2 = 0.999
ADAM_EPS = 1e-08
ADAM_WD = 0.01
ADAM_STEP = 10
ADAM_C1 = 1.0 - ADAM_B1 ** ADAM_STEP
ADAM_C2 = 1.0 - ADAM_B2 ** ADAM_STEP

N_DEV = 8
VMEM_LIMIT_BYTES = 56 * 1024 * 1024
FFN_COLS = 1408
S5_BLOCKS = 4
S5_BLOCK_IN = D_SSM // S5_BLOCKS
S5_BLOCK_ST = SSM_LANES // S5_BLOCKS
SCAN_LANES = 512
S5_TIME_BLOCK = 512
TN_K_BLOCK = 2048
TN_SMALL_BLOCK = 1024 * 1024
GMLP_CHUNKS = 8
ROW_STEPS = 4
LAST_PIECES = 2
LAST_PIECE = "ffn1_w_in_q%d"
_G0 = math.sqrt(2.0 / math.pi)
_G1 = 0.044715


def _dot(a, b):
    return jnp.dot(a, b, preferred_element_type=F32)


def _dot_nt(a, b):
    return lax.dot_general(a, b, (((1,), (1,)), ((), ())), preferred_element_type=F32)


def _dot_tn(a, b):
    return lax.dot_general(a, b, (((0,), (0,)), ((), ())), preferred_element_type=F32)


def _sigmoid(x):
    return 0.5 * jnp.tanh(0.5 * x) + 0.5


def _gelu(x):
    t = jnp.tanh(_G0 * (x + _G1 * x * x * x))
    return 0.5 * x * (1.0 + t)


def _gelu_grad(x):
    t = jnp.tanh(_G0 * (x + _G1 * x * x * x))
    return 0.5 * (1.0 + t) + 0.5 * x * (1.0 - t * t) * _G0 * (1.0 + 3.0 * _G1 * x * x)


def _ln_fwd(r, g, b):
    mu = jnp.mean(r, axis=-1, keepdims=True)
    d = r - mu
    var = jnp.mean(d * d, axis=-1, keepdims=True)
    rstd = lax.rsqrt(var + LN_EPS)
    xh = d * rstd
    return xh * g + b, xh, rstd


def _ln_bwd(dy, xh, rstd, g):
    dxh = dy * g
    m1 = jnp.mean(dxh, axis=-1, keepdims=True)
    m2 = jnp.mean(dxh * xh, axis=-1, keepdims=True)
    return rstd * (dxh - m1 - xh * m2)


def _resident(shape):
    nd = len(shape)
    return pl.BlockSpec(shape, lambda *_: (0,) * nd, pipeline_mode=pl.Buffered(1))


def _fixed(shape):
    nd = len(shape)
    return pl.BlockSpec(shape, lambda *_: (0,) * nd)


def _rows(tm, cols):
    return pl.BlockSpec((tm, cols), lambda i: (i, 0))


def _cols(rows, tm):
    return pl.BlockSpec((rows, tm), lambda i: (0, i))


def _params(sem):
    return pltpu.CompilerParams(dimension_semantics=sem, vmem_limit_bytes=VMEM_LIMIT_BYTES)


class _Exchange:
    def __init__(self, args, out_shape, sems, start, finish):
        self.args, self.out_shape, self.sems = list(args), list(out_shape), list(sems)
        self.start, self.finish = start, finish
        self.cuts = [(0, len(self.out_shape))]


def _call(body, name, grid, in_specs, out_specs, out_shape, args, scratch=(), sem=None, bg=None, aliases=None):
    aliases = {} if aliases is None else aliases
    in_specs, args = list(in_specs), list(args)
    fn = body
    if bg is not None:
        n_args = len(args)

        def fn(*refs):
            body(*refs[:n_args], *refs[n_args + 1:])

        in_specs.append(pl.BlockSpec(memory_space=pl.ANY))
        args.append(bg)
    res = pl.pallas_call(fn, name=name, grid=grid, out_shape=tuple(out_shape), in_specs=in_specs,
                         out_specs=tuple(out_specs), scratch_shapes=list(scratch),
                         input_output_aliases=aliases, compiler_params=_params(sem))(*args)
    return tuple(res), ()


def _run_exchange_on_sequencer(ex, name, collective_id):
    n_i, n_o = len(ex.args), len(ex.out_shape)

    def body(*refs):
        ins, outs, sems = refs[:n_i], refs[n_i:n_i + n_o], refs[n_i + n_o:]
        x, y, c = lax.axis_index("x"), lax.axis_index("y"), lax.axis_index("c")
        barrier = pltpu.get_barrier_semaphore()
        for peer in [(x, y, 1 - c), (1 - x, y, c), (x, 1 - y, c), (1 - x, 1 - y, c)]:
            pl.semaphore_signal(barrier, inc=1, device_id=peer, device_id_type=MESH)
        pl.semaphore_wait(barrier, 4)
        ex.start(ins, outs, sems)
        ex.finish(ins, outs, sems)

    return tuple(pl.kernel(body, out_type=tuple(ex.out_shape),
                           mesh=plsc.ScalarSubcoreMesh(axis_name="sequencer", num_cores=1),
                           scratch_types=list(ex.sems), name=name,
                           compiler_params=pltpu.CompilerParams(collective_id=collective_id))(*ex.args))


def _join(exchanges):
    cuts = []
    a = o = q = 0
    for e in exchanges:
        cuts.append((a, a + len(e.args), o, o + len(e.out_shape), q, q + len(e.sems)))
        a, o, q = cuts[-1][1], cuts[-1][3], cuts[-1][5]

    def start(ins, outs, sems):
        for e, (a0, a1, o0, o1, q0, q1) in zip(exchanges, cuts):
            e.start(ins[a0:a1], outs[o0:o1], sems[q0:q1])

    def finish(ins, outs, sems):
        for e, (a0, a1, o0, o1, q0, q1) in zip(exchanges, cuts):
            e.finish(ins[a0:a1], outs[o0:o1], sems[q0:q1])

    joined = _Exchange(sum((e.args for e in exchanges), []), sum((e.out_shape for e in exchanges), []),
                       sum((e.sems for e in exchanges), []), start, finish)
    joined.cuts = [(c[2], c[3]) for c in cuts]
    return joined


def _ffn_proj(x, w_in, tm, name, bg=None):
    t = x.shape[0]
    nch = D_FF // FFN_COLS

    def body(x_ref, win_ref, xbt_ref, h_ref, a_ref):
        xb = x_ref[...].astype(BF16)
        xbt_ref[...] = xb.T
        for k in range(nch):
            cg = slice(k * FFN_COLS, (k + 1) * FFN_COLS)
            cu = slice(D_FF + k * FFN_COLS, D_FF + (k + 1) * FFN_COLS)
            hg = _dot(xb, win_ref[k])
            hu = _dot(xb, win_ref[nch + k])
            h_ref[:, cg] = hg.astype(BF16)
            h_ref[:, cu] = hu.astype(BF16)
            a_ref[:, cg] = (hg * _sigmoid(hg) * hu).astype(BF16)

    return _call(
        body, name, (t // tm,),
        [_rows(tm, D_MODEL), _resident((2 * nch, D_MODEL, FFN_COLS))],
        (_cols(D_MODEL, tm), _rows(tm, 2 * D_FF), _rows(tm, D_FF)),
        (SDS((D_MODEL, t), BF16), SDS((t, 2 * D_FF), BF16), SDS((t, D_FF), BF16)),
        (x, w_in), sem=("parallel",), bg=bg)


def _ffn_out(x, a, w_out, g, b, tm, name, bg=None):
    t = x.shape[0]

    def body(x_ref, a_ref, wout_ref, g_ref, b_ref, xn_ref, xh_ref, rstd_ref):
        f = _dot(a_ref[...], wout_ref[...])
        y, xh, rstd = _ln_fwd(ALPHA * x_ref[...] + 0.5 * f, g_ref[...], b_ref[...])
        xn_ref[...] = y
        xh_ref[...] = xh
        rstd_ref[...] = rstd

    return _call(
        body, name, (t // tm,),
        [_rows(tm, D_MODEL), _rows(tm, D_FF), _resident((D_FF, D_MODEL)), _fixed((1, D_MODEL)), _fixed((1, D_MODEL))],
        (_rows(tm, D_MODEL), _rows(tm, D_MODEL), _rows(tm, 1)),
        (SDS((t, D_MODEL), F32), SDS((t, D_MODEL), F32), SDS((t, 1), F32)),
        (x, a, w_out, g, b), sem=("parallel",), bg=bg)


def _ffn_out_loss(x, a, w_out, g, b, p, tgt, wpg, wpp, tm):
    t = x.shape[0]

    def body(x_ref, a_ref, wout_ref, g_ref, b_ref, p_ref, t_ref, wpg_ref, wpp_ref,
             xh_ref, rstd_ref, dx_ref, xbt_ref, pbt_ref, dq_ref, de_ref, loss_ref):
        @pl.when(pl.program_id(0) == 0)
        def _():
            loss_ref[...] = jnp.zeros_like(loss_ref)

        f = _dot(a_ref[...], wout_ref[...])
        x3v, xh, rstd = _ln_fwd(ALPHA * x_ref[...] + 0.5 * f, g_ref[...], b_ref[...])
        xh_ref[...] = xh
        rstd_ref[...] = rstd
        xb = x3v.astype(BF16)
        pb = p_ref[...].astype(BF16)
        xbt_ref[...] = xb.T
        pbt_ref[...] = pb.T
        s = _sigmoid(_dot(xb, wpg_ref[...]))
        e = _dot(pb, wpp_ref[...])
        diff = x3v + s * e - t_ref[...]
        loss_ref[...] += jnp.sum(diff * diff, axis=0, keepdims=True)
        dout = diff * (1.0 / D_MODEL)
        de_ref[...] = (dout * s).astype(BF16)
        dq = (dout * e * s * (1.0 - s)).astype(BF16)
        dq_ref[...] = dq
        dx_ref[...] = dout + _dot_nt(dq, wpg_ref[...])

    return _call(
        body, "ffn2_out_loss", (t // tm,),
        [_rows(tm, D_MODEL), _rows(tm, D_FF), _resident((D_FF, D_MODEL)), _fixed((1, D_MODEL)), _fixed((1, D_MODEL)),
         _rows(tm, PLE_DIM), _rows(tm, D_MODEL), _resident((D_MODEL, D_MODEL)), _resident((PLE_DIM, D_MODEL))],
        (_rows(tm, D_MODEL), _rows(tm, 1), _rows(tm, D_MODEL), _cols(D_MODEL, tm), _cols(PLE_DIM, tm),
         _rows(tm, D_MODEL), _rows(tm, D_MODEL), _fixed((1, D_MODEL))),
        (SDS((t, D_MODEL), F32), SDS((t, 1), F32), SDS((t, D_MODEL), F32), SDS((D_MODEL, t), BF16),
         SDS((PLE_DIM, t), BF16), SDS((t, D_MODEL), BF16), SDS((t, D_MODEL), BF16), SDS((1, D_MODEL), F32)),
        (x, a, w_out, g, b, p, tgt, wpg, wpp), sem=("arbitrary",))


def _ffn_bwd(dxn, xh, rstd, h, w_in, w_out, g, tm, name, bg=None):
    t = dxn.shape[0]
    nch = D_FF // FFN_COLS

    def body(dxn_ref, xh_ref, rstd_ref, h_ref, win_ref, wout_ref, g_ref,
             dx_ref, dh_ref, df_ref, dg_ref, db_ref):
        @pl.when(pl.program_id(0) == 0)
        def _():
            dg_ref[...] = jnp.zeros_like(dg_ref)
            db_ref[...] = jnp.zeros_like(db_ref)

        dy = dxn_ref[...]
        xhv = xh_ref[...]
        dr = _ln_bwd(dy, xhv, rstd_ref[...], g_ref[...])
        dg_ref[...] += jnp.sum(dy * xhv, axis=0, keepdims=True)
        db_ref[...] += jnp.sum(dy, axis=0, keepdims=True)
        df = (0.5 * dr).astype(BF16)
        df_ref[...] = df
        dx = ALPHA * dr
        das = [_dot_nt(df, wout_ref[k * FFN_COLS:(k + 1) * FFN_COLS, :]) for k in range(nch)]
        for k in range(nch):
            cg = slice(k * FFN_COLS, (k + 1) * FFN_COLS)
            cu = slice(D_FF + k * FFN_COLS, D_FF + (k + 1) * FFN_COLS)
            hg = h_ref[:, cg].astype(F32)
            hu = h_ref[:, cu].astype(F32)
            sg = _sigmoid(hg)
            silu = hg * sg
            da = das[k]
            dhu = (da * silu).astype(BF16)
            dhg = (da * hu * (sg * (1.0 + hg * (1.0 - sg)))).astype(BF16)
            dh_ref[:, cg] = dhg
            dh_ref[:, cu] = dhu
            dx = dx + _dot_nt(dhg, win_ref[k]) + _dot_nt(dhu, win_ref[nch + k])
        dx_ref[...] = dx

    return _call(
        body, name, (t // tm,),
        [_rows(tm, D_MODEL), _rows(tm, D_MODEL), _rows(tm, 1), _rows(tm, 2 * D_FF),
         _resident((2 * nch, D_MODEL, FFN_COLS)), _resident((D_FF, D_MODEL)), _fixed((1, D_MODEL))],
        (_rows(tm, D_MODEL), _rows(tm, 2 * D_FF), _rows(tm, D_MODEL),
         _fixed((1, D_MODEL)), _fixed((1, D_MODEL))),
        (SDS((t, D_MODEL), F32), SDS((t, 2 * D_FF), BF16), SDS((t, D_MODEL), BF16),
         SDS((1, D_MODEL), F32), SDS((1, D_MODEL), F32)),
        (dxn, xh, rstd, h, w_in, w_out, g), sem=("arbitrary",), bg=bg)


def _tn_matmul(a, b, name, bm, bn, col_block=0, total_cols=None, prev=None, bg=None, a_cols=None, a_t=False):
    t, m = a.shape[::-1] if a_t else a.shape
    a_first = 0
    if a_cols is not None:
        a_first, m = a_cols[0], a_cols[1] * bm
    n = b.shape[1]
    total_cols = n if total_cols is None else total_cols
    whole = bm * bn <= TN_SMALL_BLOCK and (m // bm) * (n // bn) >= 2
    bk = min(2 * TN_K_BLOCK if whole else TN_K_BLOCK, t)
    nk = t // bk
    n_in = 2 if prev is None else 4

    def body(*refs):
        a_ref, b_ref = refs[0], refs[1]
        o_ref, ob_ref = refs[n_in], refs[n_in + 1]
        k = pl.program_id(2)

        @pl.when(k == 0)
        def _():
            o_ref[...] = jnp.zeros_like(o_ref)

        o_ref[...] += _dot(a_ref[...], b_ref[...]) if a_t else _dot_tn(a_ref[...], b_ref[...])

        @pl.when(k == nk - 1)
        def _():
            ob_ref[...] = o_ref[...].astype(BF16)

    a_spec = (pl.BlockSpec((bm, bk), lambda i, j, k: (i + a_first, k)) if a_t
              else pl.BlockSpec((bk, bm), lambda i, j, k: (k, i + a_first)))
    in_specs = [a_spec, pl.BlockSpec((bk, bn), lambda i, j, k: (k, j))]
    args = [a, b]
    aliases = {}
    if prev is not None:
        in_specs += [pl.BlockSpec(memory_space=pl.ANY), pl.BlockSpec(memory_space=pl.ANY)]
        args += list(prev)
        aliases = {2: 0, 3: 1}
        if any(bg is p for p in prev):
            bg = None
    out_spec = pl.BlockSpec((bm, bn), lambda i, j, k: (i, j + col_block))
    return _call(body, name, (m // bm, n // bn, nk), in_specs, (out_spec, out_spec),
                 (SDS((m, total_cols), F32), SDS((m, total_cols), BF16)), args,
                 sem=("parallel", "parallel", "arbitrary"), bg=bg, aliases=aliases)


def _mixin_fwd(x1, w, tm, bg=None):
    t = x1.shape[0]

    def body(x_ref, w_ref, xbt_ref, za_ref, zuv_ref, gab_ref):
        xb = x_ref[...].astype(BF16)
        xbt_ref[...] = xb.T
        za_ref[...] = _dot(xb, w_ref[:, 0:512]).astype(BF16)
        zuv_ref[...] = _dot(xb, w_ref[:, 512:1536]).astype(BF16)
        gab_ref[...] = _dot(xb, w_ref[:, 1536:3584]).astype(BF16)

    return _call(
        body, "mixin_fwd", (t // tm,),
        [_rows(tm, D_MODEL), _resident((D_MODEL, 3584))],
        (_cols(D_MODEL, tm), _rows(tm, 512), _rows(tm, 1024), _rows(tm, 2048)),
        (SDS((D_MODEL, t), BF16), SDS((t, 512), BF16), SDS((t, 1024), BF16), SDS((t, 2048), BF16)),
        (x1, w), sem=("parallel",), bg=bg)


def _mixin_bwd(dx1a, dza, dzuv, dgab, w, tm, bg=None):
    t = dx1a.shape[0]

    def body(d_ref, dza_ref, dzuv_ref, dgab_ref, w_ref, dx_ref):
        dx_ref[...] = (d_ref[...] + _dot_nt(dza_ref[...], w_ref[:, 0:512])
                       + _dot_nt(dzuv_ref[...], w_ref[:, 512:1536])
                       + _dot_nt(dgab_ref[...], w_ref[:, 1536:3584]))

    return _call(
        body, "mixin_bwd", (t // tm,),
        [_rows(tm, D_MODEL), _rows(tm, 512), _rows(tm, 1024), _rows(tm, 2048), _resident((D_MODEL, 3584))],
        (_rows(tm, D_MODEL),), (SDS((t, D_MODEL), F32),),
        (dx1a, dza, dzuv, dgab, w), sem=("parallel",), bg=bg)


def _unrolled(lo, hi, body, carry):
    for j in range(lo, hi):
        carry = body(j, carry)
    return carry


def _scan_fwd(hr_ref, hi_ref, a_ref, ap_ref, carry_ref, seg, cin_ref):
    for lc in range(SSM_LANES // SCAN_LANES):
        ls = slice(lc * SCAN_LANES, (lc + 1) * SCAN_LANES)
        a_r = jnp.broadcast_to(a_ref[0:1, ls], (8, SCAN_LANES))
        a_i = jnp.broadcast_to(a_ref[1:2, ls], (8, SCAN_LANES))

        def step(j, hc, ls=ls, a_r=a_r, a_i=a_i):
            h_r, h_i = hc
            rows = pl.ds(j * 8, 8)
            n_r = a_r * h_r - a_i * h_i + hr_ref[rows, ls]
            n_i = a_r * h_i + a_i * h_r + hi_ref[rows, ls]
            hr_ref[rows, ls] = n_r
            hi_ref[rows, ls] = n_i
            return n_r, n_i

        zero = jnp.zeros((8, SCAN_LANES), F32)
        f_r, f_i = _unrolled(0, seg, step, (zero, zero))
        c_r = carry_ref[0:1, ls]
        c_i = carry_ref[1:2, ls]
        p_r = ap_ref[0:1, ls]
        p_i = ap_ref[1:2, ls]
        rows_r, rows_i = [], []
        for s in range(8):
            rows_r.append(c_r)
            rows_i.append(c_i)
            c_r, c_i = (f_r[s:s + 1] + p_r * c_r - p_i * c_i,
                        f_i[s:s + 1] + p_r * c_i + p_i * c_r)
        carry_ref[0:1, ls] = c_r
        carry_ref[1:2, ls] = c_i
        cin_r = jnp.concatenate(rows_r, axis=0)
        cin_i = jnp.concatenate(rows_i, axis=0)
        if cin_ref is not None:
            cin_ref[0, :, ls] = cin_r
            cin_ref[1, :, ls] = cin_i

        def fix(j, cc, ls=ls, a_r=a_r, a_i=a_i):
            c_r, c_i = cc
            c_r, c_i = a_r * c_r - a_i * c_i, a_r * c_i + a_i * c_r
            rows = pl.ds(j * 8, 8)
            hr_ref[rows, ls] = hr_ref[rows, ls] + c_r
            hi_ref[rows, ls] = hi_ref[rows, ls] + c_i
            return c_r, c_i

        _unrolled(0, seg, fix, (cin_r, cin_i))


def _scan_bwd(gr_ref, gi_ref, hr_ref, hi_ref, cin_ref, a_ref, ap_ref, rcarry_ref, da_ref, seg):
    for lc in range(SSM_LANES // SCAN_LANES):
        ls = slice(lc * SCAN_LANES, (lc + 1) * SCAN_LANES)
        a_r = jnp.broadcast_to(a_ref[0:1, ls], (8, SCAN_LANES))
        a_i = jnp.broadcast_to(a_ref[1:2, ls], (8, SCAN_LANES))

        def step(t, gc, ls=ls, a_r=a_r, a_i=a_i):
            g_r, g_i = gc
            rows = pl.ds((seg - 1 - t) * 8, 8)
            n_r = gr_ref[rows, ls] + a_r * g_r + a_i * g_i
            n_i = gi_ref[rows, ls] + a_r * g_i - a_i * g_r
            gr_ref[rows, ls] = n_r
            gi_ref[rows, ls] = n_i
            return n_r, n_i

        zero = jnp.zeros((8, SCAN_LANES), F32)
        f_r, f_i = _unrolled(0, seg, step, (zero, zero))
        c_r = rcarry_ref[0:1, ls]
        c_i = rcarry_ref[1:2, ls]
        p_r = ap_ref[0:1, ls]
        p_i = ap_ref[1:2, ls]
        rows_r, rows_i = [None] * 8, [None] * 8
        for s in range(7, -1, -1):
            rows_r[s] = c_r
            rows_i[s] = c_i
            c_r, c_i = (f_r[s:s + 1] + p_r * c_r + p_i * c_i,
                        f_i[s:s + 1] + p_r * c_i - p_i * c_r)
        rcarry_ref[0:1, ls] = c_r
        rcarry_ref[1:2, ls] = c_i
        cin_r = jnp.concatenate(rows_r, axis=0)
        cin_i = jnp.concatenate(rows_i, axis=0)

        def fix_row(j_rows, hp_r, hp_i, cc, ls=ls, a_r=a_r, a_i=a_i):
            c_r, c_i, acc_r, acc_i = cc
            c_r, c_i = a_r * c_r + a_i * c_i, a_r * c_i - a_i * c_r
            g_r = gr_ref[j_rows, ls] + c_r
            g_i = gi_ref[j_rows, ls] + c_i
            gr_ref[j_rows, ls] = g_r
            gi_ref[j_rows, ls] = g_i
            acc_r = acc_r + g_r * hp_r + g_i * hp_i
            acc_i = acc_i + g_i * hp_r - g_r * hp_i
            return c_r, c_i, acc_r, acc_i

        def fix(t, cc, ls=ls, fix_row=fix_row):
            j = seg - 1 - t
            rows = pl.ds(j * 8, 8)
            prev = pl.ds((j - 1) * 8, 8)
            return fix_row(rows, hr_ref[prev, ls], hi_ref[prev, ls], cc)

        cc = _unrolled(0, seg - 1, fix, (cin_r, cin_i, zero, zero))
        _, _, acc_r, acc_i = fix_row(pl.ds(0, 8), cin_ref[0, :, ls], cin_ref[1, :, ls], cc)
        da_ref[0, :, ls] += acc_r
        da_ref[1, :, ls] += acc_i


def _s5_fwd(za, sp, bsz, seq, tb, bg=None):
    nb = seq // tb
    seg = tb // 8
    t = bsz * seq

    def body(za_ref, perm_ref, permt_ref, mre_ref, mim_ref, nre_ref, nim_ref, a_ref, ap_ref,
             dsk_ref, gw_ref, gb_ref, out_ref, outt_ref, y2_ref, car_ref, hr_ref, hi_ref, carry_ref):
        @pl.when(pl.program_id(1) == 0)
        def _():
            carry_ref[...] = jnp.zeros_like(carry_ref)

        car_ref[0] = carry_ref[...]
        up = _dot(perm_ref[...], za_ref[...])
        upb = up.astype(BF16)
        for bb in range(S5_BLOCKS):
            ub = upb[:, bb * S5_BLOCK_IN:(bb + 1) * S5_BLOCK_IN]
            st = slice(bb * S5_BLOCK_ST, (bb + 1) * S5_BLOCK_ST)
            hr_ref[:, st] = _dot(ub, mre_ref[bb])
            hi_ref[:, st] = _dot(ub, mim_ref[bb])
        _scan_fwd(hr_ref, hi_ref, a_ref, ap_ref, carry_ref, seg, None)
        ys = []
        for bb in range(S5_BLOCKS):
            st = slice(bb * S5_BLOCK_ST, (bb + 1) * S5_BLOCK_ST)
            ys.append(_dot(hr_ref[:, st].astype(BF16), nre_ref[bb])
                      - _dot(hi_ref[:, st].astype(BF16), nim_ref[bb]))
        y2 = jnp.concatenate(ys, axis=1) + dsk_ref[...] * up
        y2_ref[...] = y2
        y3 = _gelu(y2)
        gl = _dot(y3.astype(BF16), gw_ref[...]) + gb_ref[...]
        oa = y3 * _sigmoid(gl)
        out = _dot(permt_ref[...], oa.astype(BF16)).astype(BF16)
        out_ref[...] = out
        outt_ref[...] = out.T

    blk = pl.BlockSpec((tb, D_SSM), lambda b, j: (b * nb + j, 0))
    blk_t = pl.BlockSpec((D_SSM, tb), lambda b, j: (0, b * nb + j))
    m_shape = (S5_BLOCKS, S5_BLOCK_IN, S5_BLOCK_ST)
    n_shape = (S5_BLOCKS, S5_BLOCK_ST, S5_BLOCK_IN)
    return _call(
        body, "s5_fwd", (bsz, nb),
        [blk, _fixed((tb, tb)), _fixed((tb, tb)), _fixed(m_shape), _fixed(m_shape), _fixed(n_shape),
         _fixed(n_shape), _fixed((2, SSM_LANES)), _fixed((2, SSM_LANES)), _fixed((1, D_SSM)),
         _fixed((D_SSM, D_SSM)), _fixed((1, D_SSM))],
        (blk, blk_t, blk, pl.BlockSpec((1, 2, SSM_LANES), lambda b, j: (b * nb + j, 0, 0))),
        (SDS((t, D_SSM), BF16), SDS((D_SSM, t), BF16), SDS((t, D_SSM), F32), SDS((bsz * nb, 2, SSM_LANES), F32)),
        (za, sp["perm"], sp["permt"], sp["mre"], sp["mim"], sp["nre"], sp["nim"], sp["a"], sp["ap"],
         sp["dskip"], sp["glu_w"], sp["glu_b"]),
        scratch=[pltpu.VMEM((tb, SSM_LANES), F32), pltpu.VMEM((tb, SSM_LANES), F32),
                 pltpu.VMEM((2, SSM_LANES), F32)],
        sem=("arbitrary", "arbitrary"), bg=bg)


def _s5_bwd(za, y2p, doa, carries, sp, bsz, seq, tb, bg=None):
    nb = seq // tb
    seg = tb // 8
    t = bsz * seq

    def body(za_ref, y2_ref, doa_ref, car_ref, perm_ref, permt_ref, mre_ref, mim_ref, mtre_ref, mtim_ref,
             nre_ref, nim_ref, ntre_ref, ntim_ref, a_ref, ap_ref, dsk_ref, gw_ref, gwt_ref, gb_ref,
             dza_ref, dmr_ref, dmi_ref, dnr_ref, dni_ref, da_ref, ddsk_ref, dgw_ref, dgb_ref,
             hr_ref, hi_ref, gr_ref, gi_ref, cin_ref, carry_ref, rcarry_ref):
        first = jnp.logical_and(pl.program_id(0) == 0, pl.program_id(1) == 0)

        @pl.when(first)
        def _():
            for r in (dmr_ref, dmi_ref, dnr_ref, dni_ref, da_ref, ddsk_ref, dgw_ref, dgb_ref):
                r[...] = jnp.zeros_like(r)

        @pl.when(pl.program_id(1) == 0)
        def _():
            rcarry_ref[...] = jnp.zeros_like(rcarry_ref)

        carry_ref[...] = car_ref[0]
        perm = perm_ref[...]
        up = _dot(perm, za_ref[...])
        upb = up.astype(BF16)
        for bb in range(S5_BLOCKS):
            ub = upb[:, bb * S5_BLOCK_IN:(bb + 1) * S5_BLOCK_IN]
            st = slice(bb * S5_BLOCK_ST, (bb + 1) * S5_BLOCK_ST)
            hr_ref[:, st] = _dot(ub, mre_ref[bb])
            hi_ref[:, st] = _dot(ub, mim_ref[bb])
        _scan_fwd(hr_ref, hi_ref, a_ref, ap_ref, carry_ref, seg, cin_ref)

        y2 = y2_ref[...]
        y3 = _gelu(y2)
        y3b = y3.astype(BF16)
        sg = _sigmoid(_dot(y3b, gw_ref[...]) + gb_ref[...])
        d0 = doa_ref[...]
        d_hi = d0.astype(BF16)
        d1 = d0 - d_hi.astype(F32)
        d_mid = d1.astype(BF16)
        d_lo = (d1 - d_mid.astype(F32)).astype(BF16)
        doap = _dot(perm, d_hi) + _dot(perm, d_mid) + _dot(perm, d_lo)
        dgl = doap * y3 * sg * (1.0 - sg)
        dglb = dgl.astype(BF16)
        dy3 = doap * sg + _dot(dglb, gwt_ref[...])
        dgw_ref[...] += _dot_tn(y3b, dglb)
        dgb_ref[...] += jnp.sum(dgl, axis=0, keepdims=True)
        dy2 = dy3 * _gelu_grad(y2)
        ddsk_ref[...] += jnp.sum(dy2 * up, axis=0, keepdims=True)
        dyb = dy2.astype(BF16)
        for bb in range(S5_BLOCKS):
            dyc = dyb[:, bb * S5_BLOCK_IN:(bb + 1) * S5_BLOCK_IN]
            st = slice(bb * S5_BLOCK_ST, (bb + 1) * S5_BLOCK_ST)
            gr_ref[:, st] = _dot(dyc, ntre_ref[bb])
            gi_ref[:, st] = -_dot(dyc, ntim_ref[bb])
            dnr_ref[bb] += _dot_tn(hr_ref[:, st].astype(BF16), dyc)
            dni_ref[bb] += -_dot_tn(hi_ref[:, st].astype(BF16), dyc)
        _scan_bwd(gr_ref, gi_ref, hr_ref, hi_ref, cin_ref, a_ref, ap_ref, rcarry_ref, da_ref, seg)
        dus = []
        for bb in range(S5_BLOCKS):
            st = slice(bb * S5_BLOCK_ST, (bb + 1) * S5_BLOCK_ST)
            grb = gr_ref[:, st].astype(BF16)
            gib = gi_ref[:, st].astype(BF16)
            dus.append(_dot(grb, mtre_ref[bb]) + _dot(gib, mtim_ref[bb]))
            ub = upb[:, bb * S5_BLOCK_IN:(bb + 1) * S5_BLOCK_IN]
            dmr_ref[bb] += _dot_tn(ub, grb)
            dmi_ref[bb] += _dot_tn(ub, gib)
        du = jnp.concatenate(dus, axis=1) + dy2 * dsk_ref[...]
        dza_ref[...] = _dot(permt_ref[...], du.astype(BF16)).astype(BF16)

    def rev(b, j):
        return (b * nb + (nb - 1 - j), 0)

    blk = pl.BlockSpec((tb, D_SSM), rev)
    m_shape = (S5_BLOCKS, S5_BLOCK_IN, S5_BLOCK_ST)
    n_shape = (S5_BLOCKS, S5_BLOCK_ST, S5_BLOCK_IN)
    return _call(
        body, "s5_bwd", (bsz, nb),
        [blk, blk, blk, pl.BlockSpec((1, 2, SSM_LANES), lambda b, j: (b * nb + (nb - 1 - j), 0, 0)),
         _fixed((tb, tb)), _fixed((tb, tb)), _fixed(m_shape), _fixed(m_shape), _fixed(n_shape), _fixed(n_shape),
         _fixed(n_shape), _fixed(n_shape), _fixed(m_shape), _fixed(m_shape),
         _fixed((2, SSM_LANES)), _fixed((2, SSM_LANES)), _fixed((1, D_SSM)),
         _fixed((D_SSM, D_SSM)), _fixed((D_SSM, D_SSM)), _fixed((1, D_SSM))],
        (blk, _fixed(m_shape), _fixed(m_shape), _fixed(n_shape), _fixed(n_shape),
         _fixed((2, 8, SSM_LANES)), _fixed((1, D_SSM)), _fixed((D_SSM, D_SSM)), _fixed((1, D_SSM))),
        (SDS((t, D_SSM), BF16), SDS(m_shape, F32), SDS(m_shape, F32), SDS(n_shape, F32), SDS(n_shape, F32),
         SDS((2, 8, SSM_LANES), F32), SDS((1, D_SSM), F32), SDS((D_SSM, D_SSM), F32), SDS((1, D_SSM), F32)),
        (za, y2p, doa, carries, sp["perm"], sp["permt"], sp["mre"], sp["mim"], sp["mtre"], sp["mtim"],
         sp["nre"], sp["nim"], sp["ntre"], sp["ntim"], sp["a"], sp["ap"], sp["dskip"], sp["glu_w"],
         sp["glu_wt"], sp["glu_b"]),
        scratch=[pltpu.VMEM((tb, SSM_LANES), F32), pltpu.VMEM((tb, SSM_LANES), F32),
                 pltpu.VMEM((tb, SSM_LANES), F32), pltpu.VMEM((tb, SSM_LANES), F32),
                 pltpu.VMEM((2, 8, SSM_LANES), F32), pltpu.VMEM((2, SSM_LANES), F32),
                 pltpu.VMEM((2, SSM_LANES), F32)],
        sem=("arbitrary", "arbitrary"), bg=bg)


def _gmlp_spatial(ws_ref, vb):
    lane = lax.broadcasted_iota(jnp.int32, (CHUNK, 128), 1)
    parts = []
    for j in range(GMLP_HEADS // 2):
        vp = vb[:, 128 * j:128 * (j + 1)]
        parts.append(jnp.where(lane < GMLP_HEAD_DIM, _dot(ws_ref[2 * j], vp), _dot(ws_ref[2 * j + 1], vp)))
    return jnp.concatenate(parts, axis=1)


def _gmlp_fwd(zuv, ln_g, ln_b, wsm, bias, bg=None):
    t = zuv.shape[0]
    chunks = min(GMLP_CHUNKS, t // CHUNK)

    def body(z_ref, g_ref, b_ref, ws_ref, bias_ref, out_ref, outt_ref):
        for ch in range(chunks):
            rows = slice(ch * CHUNK, (ch + 1) * CHUNK)
            u = _gelu(z_ref[rows, 0:D_GMLP].astype(F32))
            v0 = _gelu(z_ref[rows, D_GMLP:2 * D_GMLP].astype(F32))
            v, _, _ = _ln_fwd(v0, g_ref[...], b_ref[...])
            s = _gmlp_spatial(ws_ref, v.astype(BF16)) + bias_ref[...]
            out = (u * s).astype(BF16)
            out_ref[rows, :] = out
            outt_ref[:, rows] = out.T

    step = chunks * CHUNK
    return _call(
        body, "gmlp_fwd", (t // step,),
        [_rows(step, 2 * D_GMLP), _fixed((1, D_GMLP)), _fixed((1, D_GMLP)),
         _fixed((GMLP_HEADS, CHUNK, CHUNK)), _fixed((CHUNK, D_GMLP))],
        (_rows(step, D_GMLP), _cols(D_GMLP, step)), (SDS((t, D_GMLP), BF16), SDS((D_GMLP, t), BF16)),
        (zuv, ln_g, ln_b, wsm, bias), sem=("parallel",), bg=bg)


def _gmlp_bwd(zuv, dgm, ln_g, ln_b, wsm, wsmt, bias, bg=None):
    t = zuv.shape[0]
    chunks = min(GMLP_CHUNKS, t // CHUNK)

    def body(z_ref, d_ref, g_ref, b_ref, ws_ref, wst_ref, bias_ref,
             dz_ref, dws_ref, dbias_ref, dg_ref, db_ref):
        @pl.when(pl.program_id(0) == 0)
        def _():
            for r in (dws_ref, dbias_ref, dg_ref, db_ref):
                r[...] = jnp.zeros_like(r)

        gam = g_ref[...]
        lane = lax.broadcasted_iota(jnp.int32, (CHUNK, 128), 1)
        tril = (lax.broadcasted_iota(jnp.int32, (CHUNK, CHUNK), 0)
                >= lax.broadcasted_iota(jnp.int32, (CHUNK, CHUNK), 1))
        zero_b = jnp.zeros((CHUNK, 128), BF16)
        for ch in range(chunks):
            rows = slice(ch * CHUNK, (ch + 1) * CHUNK)
            zu = z_ref[rows, 0:D_GMLP].astype(F32)
            zv = z_ref[rows, D_GMLP:2 * D_GMLP].astype(F32)
            u = _gelu(zu)
            v0 = _gelu(zv)
            v, vhat, rstd = _ln_fwd(v0, gam, b_ref[...])
            vb = v.astype(BF16)
            s = _gmlp_spatial(ws_ref, vb) + bias_ref[...]
            d = d_ref[rows, :]
            dz_ref[rows, 0:D_GMLP] = (d * s * _gelu_grad(zu)).astype(BF16)
            ds = d * u
            dbias_ref[...] += ds
            dsb = ds.astype(BF16)
            parts = []
            for j in range(GMLP_HEADS // 2):
                dsp = dsb[:, 128 * j:128 * (j + 1)]
                vp = vb[:, 128 * j:128 * (j + 1)]
                parts.append(jnp.where(lane < GMLP_HEAD_DIM, _dot(wst_ref[2 * j], dsp),
                                       _dot(wst_ref[2 * j + 1], dsp)))
                lo = jnp.where(lane < GMLP_HEAD_DIM, dsp, zero_b)
                hi = jnp.where(lane < GMLP_HEAD_DIM, zero_b, dsp)
                dws_ref[2 * j] += jnp.where(tril, _dot_nt(lo, vp), 0.0)
                dws_ref[2 * j + 1] += jnp.where(tril, _dot_nt(hi, vp), 0.0)
            dv = jnp.concatenate(parts, axis=1)
            dg_ref[...] += jnp.sum(dv * vhat, axis=0, keepdims=True)
            db_ref[...] += jnp.sum(dv, axis=0, keepdims=True)
            dz_ref[rows, D_GMLP:2 * D_GMLP] = (_ln_bwd(dv, vhat, rstd, gam) * _gelu_grad(zv)).astype(BF16)

    step = chunks * CHUNK
    return _call(
        body, "gmlp_bwd", (t // step,),
        [_rows(step, 2 * D_GMLP), _rows(step, D_GMLP), _fixed((1, D_GMLP)), _fixed((1, D_GMLP)),
         _fixed((GMLP_HEADS, CHUNK, CHUNK)), _fixed((GMLP_HEADS, CHUNK, CHUNK)), _fixed((CHUNK, D_GMLP))],
        (_rows(step, 2 * D_GMLP), _fixed((GMLP_HEADS, CHUNK, CHUNK)), _fixed((CHUNK, D_GMLP)),
         _fixed((1, D_GMLP)), _fixed((1, D_GMLP))),
        (SDS((t, 2 * D_GMLP), BF16), SDS((GMLP_HEADS, CHUNK, CHUNK), F32), SDS((CHUNK, D_GMLP), F32),
         SDS((1, D_GMLP), F32), SDS((1, D_GMLP), F32)),
        (zuv, dgm, ln_g, ln_b, wsm, wsmt, bias), sem=("arbitrary",), bg=bg)


def _mixout_fwd(x1, s5o, gm, gab, ua, ub, wmo, g, b, tm, bg=None):
    t = x1.shape[0]

    def body(x_ref, s_ref, m_ref, gab_ref, ua_ref, ub_ref, wmo_ref, g_ref, b_ref,
             xn_ref, xh_ref, rstd_ref):
        ya = _dot(s_ref[...], ua_ref[...])
        yb = _dot(m_ref[...], ub_ref[...])
        mix = (_sigmoid(gab_ref[:, 0:D_MODEL].astype(F32)) * ya
               + _sigmoid(gab_ref[:, D_MODEL:2 * D_MODEL].astype(F32)) * yb)
        r = ALPHA * x_ref[...] + _dot(mix.astype(BF16), wmo_ref[...])
        y, xh, rstd = _ln_fwd(r, g_ref[...], b_ref[...])
        xn_ref[...] = y
        xh_ref[...] = xh
        rstd_ref[...] = rstd

    return _call(
        body, "mixout_fwd", (t // tm,),
        [_rows(tm, D_MODEL), _rows(tm, D_SSM), _rows(tm, D_GMLP), _rows(tm, 2 * D_MODEL),
         _resident((D_SSM, D_MODEL)), _resident((D_GMLP, D_MODEL)), _resident((D_MODEL, D_MODEL)),
         _fixed((1, D_MODEL)), _fixed((1, D_MODEL))],
        (_rows(tm, D_MODEL), _rows(tm, D_MODEL), _rows(tm, 1)),
        (SDS((t, D_MODEL), F32), SDS((t, D_MODEL), F32), SDS((t, 1), F32)),
        (x1, s5o, gm, gab, ua, ub, wmo, g, b), sem=("parallel",), bg=bg)


def _mixout_bwd(dx2, xh, rstd, s5o, gm, gab, ua, ub, wmo, g, tm, bg=None):
    t = dx2.shape[0]

    def body(d_ref, xh_ref, rstd_ref, s_ref, m_ref, gab_ref, ua_ref, ub_ref, wmo_ref, g_ref,
             dx1_ref, dmx_ref, mb_ref, dya_ref, dyb_ref, ds5_ref, dgm_ref, dgab_ref, dg_ref, db_ref):
        @pl.when(pl.program_id(0) == 0)
        def _():
            dg_ref[...] = jnp.zeros_like(dg_ref)
            db_ref[...] = jnp.zeros_like(db_ref)

        dy = d_ref[...]
        xhv = xh_ref[...]
        dr = _ln_bwd(dy, xhv, rstd_ref[...], g_ref[...])
        dg_ref[...] += jnp.sum(dy * xhv, axis=0, keepdims=True)
        db_ref[...] += jnp.sum(dy, axis=0, keepdims=True)
        dx1_ref[...] = ALPHA * dr
        drb = dr.astype(BF16)
        dmx_ref[...] = drb
        dm = _dot_nt(drb, wmo_ref[...])
        ya = _dot(s_ref[...], ua_ref[...])
        yb = _dot(m_ref[...], ub_ref[...])
        sa = _sigmoid(gab_ref[:, 0:D_MODEL].astype(F32))
        sb = _sigmoid(gab_ref[:, D_MODEL:2 * D_MODEL].astype(F32))
        mb_ref[...] = (sa * ya + sb * yb).astype(BF16).T
        dya = (dm * sa).astype(BF16)
        dyb = (dm * sb).astype(BF16)
        dya_ref[...] = dya
        dyb_ref[...] = dyb
        dgab_ref[:, 0:D_MODEL] = (dm * ya * sa * (1.0 - sa)).astype(BF16)
        dgab_ref[:, D_MODEL:2 * D_MODEL] = (dm * yb * sb * (1.0 - sb)).astype(BF16)
        ds5_ref[...] = _dot_nt(dya, ua_ref[...])
        dgm_ref[...] = _dot_nt(dyb, ub_ref[...])

    return _call(
        body, "mixout_bwd", (t // tm,),
        [_rows(tm, D_MODEL), _rows(tm, D_MODEL), _rows(tm, 1), _rows(tm, D_SSM), _rows(tm, D_GMLP),
         _rows(tm, 2 * D_MODEL), _resident((D_SSM, D_MODEL)), _resident((D_GMLP, D_MODEL)),
         _resident((D_MODEL, D_MODEL)), _fixed((1, D_MODEL))],
        (_rows(tm, D_MODEL), _rows(tm, D_MODEL), _cols(D_MODEL, tm), _rows(tm, D_MODEL),
         _rows(tm, D_MODEL), _rows(tm, D_SSM), _rows(tm, D_GMLP), _rows(tm, 2 * D_MODEL),
         _fixed((1, D_MODEL)), _fixed((1, D_MODEL))),
        (SDS((t, D_MODEL), F32), SDS((t, D_MODEL), BF16), SDS((D_MODEL, t), BF16),
         SDS((t, D_MODEL), BF16), SDS((t, D_MODEL), BF16), SDS((t, D_SSM), F32),
         SDS((t, D_GMLP), F32), SDS((t, 2 * D_MODEL), BF16),
         SDS((1, D_MODEL), F32), SDS((1, D_MODEL), F32)),
        (dx2, xh, rstd, s5o, gm, gab, ua, ub, wmo, g), sem=("arbitrary",), bg=bg)


def _s5_discretise(lre, lim, log_dt, bre, bim):
    dt = jnp.exp(log_dt)[:, None]
    mag = jnp.exp(lre * dt)
    abr = mag * jnp.cos(lim * dt)
    abi = mag * jnp.sin(lim * dt)
    nr = abr - 1.0
    ni = abi
    den = lre * lre + lim * lim
    cr = ((nr * lre + ni * lim) / den)[..., None]
    ci = ((ni * lre - nr * lim) / den)[..., None]
    return abr, abi, cr * bre - ci * bim, cr * bim + ci * bre


def _block_diag_in(bb):
    v = bb.reshape(S5_BLOCKS, 8, SSM_STATE, SSM_GROUP_CH).transpose(0, 1, 3, 2)
    return jnp.einsum("bgip,gh->bgihp", v, jnp.eye(8, dtype=bb.dtype)).reshape(
        S5_BLOCKS, S5_BLOCK_IN, S5_BLOCK_ST)


def _block_diag_in_t(dm):
    v = dm.reshape(S5_BLOCKS, 8, SSM_GROUP_CH, 8, SSM_STATE)
    d = jnp.einsum("bgihp,gh->bgip", v, jnp.eye(8, dtype=dm.dtype))
    return d.transpose(0, 1, 3, 2).reshape(SSM_GROUPS, SSM_STATE, SSM_GROUP_CH)


def _block_diag_out(cc):
    v = cc.reshape(S5_BLOCKS, 8, SSM_GROUP_CH, SSM_STATE)
    return jnp.einsum("bgip,gh->bgphi", v, jnp.eye(8, dtype=cc.dtype)).reshape(
        S5_BLOCKS, S5_BLOCK_ST, S5_BLOCK_IN)


def _block_diag_out_t(dn):
    v = dn.reshape(S5_BLOCKS, 8, SSM_STATE, 8, SSM_GROUP_CH)
    d = jnp.einsum("bgphi,gh->bgip", v, jnp.eye(8, dtype=dn.dtype))
    return d.reshape(SSM_GROUPS, SSM_GROUP_CH, SSM_STATE)


def _s5_setup(lre, lim, log_dt, bre, bim, cre, cim, d_skip, glu_w, glu_b, tb):
    seg = tb // 8
    abr, abi, bbr, bbi = _s5_discretise(lre, lim, log_dt, bre, bim)
    pr, pi = abr, abi
    for _ in range(int(math.log2(seg))):
        pr, pi = pr * pr - pi * pi, 2.0 * pr * pi
    rows = jnp.arange(tb)
    src = (rows % 8) * seg + rows // 8
    perm = (src[:, None] == jnp.arange(tb)[None, :]).astype(BF16)
    mre = _block_diag_in(bbr)
    mim = _block_diag_in(bbi)
    nre = _block_diag_out(cre)
    nim = _block_diag_out(cim)
    return {
        "perm": perm, "permt": perm.T,
        "mre": mre.astype(BF16), "mim": mim.astype(BF16),
        "mtre": mre.transpose(0, 2, 1).astype(BF16), "mtim": mim.transpose(0, 2, 1).astype(BF16),
        "nre": nre.astype(BF16), "nim": nim.astype(BF16),
        "ntre": nre.transpose(0, 2, 1).astype(BF16), "ntim": nim.transpose(0, 2, 1).astype(BF16),
        "a": jnp.stack([abr.reshape(-1), abi.reshape(-1)]),
        "ap": jnp.stack([pr.reshape(-1), pi.reshape(-1)]),
        "dskip": d_skip.reshape(1, D_SSM), "glu_w": glu_w, "glu_wt": glu_w.T,
        "glu_b": glu_b.reshape(1, D_SSM),
    }


BIG = ("ffn1_w_in", "ffn1_w_out", "mix_w_in", "ssm_glu_w", "up_a", "up_b", "mix_w_out",
       "ffn2_w_in", "ffn2_w_out", "ple_w_proj", "ple_w_gate")
BIG_AXIS = {"ffn1_w_in": 1, "ffn1_w_out": 0, "mix_w_in": 1, "ssm_glu_w": 0, "up_a": 1, "up_b": 1,
            "mix_w_out": 0, "ffn2_w_in": 1, "ffn2_w_out": 0, "ple_w_proj": 1, "ple_w_gate": 0}
SHARD_MAJOR = 2
GATHER_AXIS = dict(BIG_AXIS, ffn1_w_in=SHARD_MAJOR, ffn2_w_in=SHARD_MAJOR)
GATHER_ORDER = (("ffn1_w_in",), ("ffn1_w_out",), ("mix_w_in",), ("ssm_glu_w", "up_a", "up_b", "mix_w_out"),
                ("ffn2_w_in",), ("ffn2_w_out", "ple_w_gate", "ple_w_proj"))
GATHER_FIRST_ID = 1
REDUCE_FIRST_ID = 7
SMALL = ("ln1_g", "ln1_b", "ssm_lambda_re", "ssm_lambda_im", "ssm_log_dt", "ssm_b_re", "ssm_b_im",
         "ssm_c_re", "ssm_c_im", "ssm_d", "ssm_glu_b", "gmlp_ln_g", "gmlp_ln_b", "gmlp_w_s",
         "gmlp_b_s", "ln2_g", "ln2_b", "ln3_g", "ln3_b")
SMALL_VIEW = {"ssm_b_re": (SSM_GROUPS, SSM_STATE * SSM_GROUP_CH), "ssm_b_im": (SSM_GROUPS, SSM_STATE * SSM_GROUP_CH)}


def _small_view(k, a):
    return a.reshape(SMALL_VIEW[k]) if k in SMALL_VIEW else a


def _place():
    return lax.axis_index("x"), lax.axis_index("y"), lax.axis_index("c")


def _other_chips(x, y):
    return [(1 - x, y), (x, 1 - y), (1 - x, 1 - y)]


def _window(ref, shard_shape, axis, chip, half):
    r, c = shard_shape
    hr = r // 2
    if axis == SHARD_MAJOR:
        return ref.at[chip] if half is None else ref.at[chip, pl.ds(half * hr, hr), :]
    if axis == 0:
        if half is None:
            return ref.at[pl.ds(chip * r, r), :]
        return ref.at[pl.ds(chip * r + half * hr, hr), :]
    if half is None:
        return ref.at[:, pl.ds(chip * c, c)]
    return ref.at[pl.ds(half * hr, hr), pl.ds(chip * c, c)]


def _gather_weights(shards, axes):
    n = len(shards)
    shapes = [s.shape for s in shards]
    full = [{0: (4 * r, c), 1: (r, 4 * c), SHARD_MAJOR: (4, r, c)}[ax] for (r, c), ax in zip(shapes, axes)]

    def remote(sems, i, k, src, dst, to):
        return pltpu.make_async_remote_copy(src_ref=src, dst_ref=dst, send_sem=sems[0].at[6 * i + k],
                                            recv_sem=sems[1].at[6 * i + k], device_id=to, device_id_type=MESH)

    def own_copies(ins, outs, sems):
        x, y, c = _place()
        me = 2 * x + y
        cps = []
        for i in range(n):
            hr = shapes[i][0] // 2
            mine = ins[i].at[pl.ds(c * hr, hr), :]
            for j, (cx, cy) in enumerate(_other_chips(x, y)):
                cps.append(remote(sems, i, j, mine, _window(outs[i], shapes[i], axes[i], me, c), (cx, cy, c)))
        local = [pltpu.make_async_copy(ins[i], _window(outs[i], shapes[i], axes[i], me, None), sems[2].at[i])
                 for i in range(n)]
        return cps, local

    def start(ins, outs, sems):
        cps, local = own_copies(ins, outs, sems)
        for cp in local + cps:
            cp.start()

    def finish(ins, outs, sems):
        x, y, c = _place()
        sibling = (x, y, 1 - c)
        passed = []
        for j, (cx, cy) in enumerate(_other_chips(x, y)):
            for i in range(n):
                w = _window(outs[i], shapes[i], axes[i], 2 * cx + cy, c)
                remote(sems, i, j, w, w, (cx, cy, c)).wait_recv()
                cp = remote(sems, i, 3 + j, w, w, sibling)
                cp.start()
                passed.append(cp)
        for j, (cx, cy) in enumerate(_other_chips(x, y)):
            for i in range(n):
                w = _window(outs[i], shapes[i], axes[i], 2 * cx + cy, 1 - c)
                remote(sems, i, 3 + j, w, w, sibling).wait_recv()
        cps, local = own_copies(ins, outs, sems)
        for cp in cps + passed:
            cp.wait_send()
        for cp in local:
            cp.wait()

    return _Exchange(shards, [SDS(f, BF16) for f in full],
                     [pltpu.SemaphoreType.DMA((6 * n,)), pltpu.SemaphoreType.DMA((6 * n,)),
                      pltpu.SemaphoreType.DMA((n,))], start, finish)


def _scatter_grads(parts, shapes, axes):
    n = len(parts)

    def copies(ins, outs, sems):
        x, y, c = _place()
        return [pltpu.make_async_remote_copy(
            src_ref=_window(ins[i], shapes[i], axes[i], 2 * cx + cy, None), dst_ref=outs[i].at[j],
            send_sem=sems[0].at[3 * i + j], recv_sem=sems[1].at[3 * i + j],
            device_id=(cx, cy, c), device_id_type=MESH)
            for i in range(n) for j, (cx, cy) in enumerate(_other_chips(x, y))]

    def start(ins, outs, sems):
        for cp in copies(ins, outs, sems):
            cp.start()

    def finish(ins, outs, sems):
        for cp in copies(ins, outs, sems):
            cp.wait()

    return _Exchange(parts, [SDS((3,) + tuple(s), BF16) for s in shapes],
                     [pltpu.SemaphoreType.DMA((3 * n,)), pltpu.SemaphoreType.DMA((3 * n,))], start, finish)


def _swap_halves(parts, shapes, axes):
    n = len(parts)

    def copies(ins, outs, sems):
        x, y, c = _place()
        cps = []
        for i in range(n):
            r, _ = shapes[i]
            hr = r // 2
            if axes[i] == 0:
                cps += [pltpu.make_async_remote_copy(
                    src_ref=ins[i].at[pl.ds(k * r + (1 - c) * hr, hr), :], dst_ref=outs[i].at[k],
                    send_sem=sems[0].at[i], recv_sem=sems[1].at[i], device_id=(x, y, 1 - c),
                    device_id_type=MESH) for k in range(4)]
            else:
                cps.append(pltpu.make_async_remote_copy(
                    src_ref=ins[i].at[pl.ds((1 - c) * hr, hr), :], dst_ref=outs[i],
                    send_sem=sems[0].at[i], recv_sem=sems[1].at[i], device_id=(x, y, 1 - c),
                    device_id_type=MESH))
        return cps

    def start(ins, outs, sems):
        for cp in copies(ins, outs, sems):
            cp.start()

    def finish(ins, outs, sems):
        x, y, c = _place()
        for i in range(n):
            pltpu.make_async_remote_copy(src_ref=outs[i], dst_ref=outs[i], send_sem=sems[0].at[i],
                                         recv_sem=sems[1].at[i], device_id=(x, y, 1 - c),
                                         device_id_type=MESH).wait()

    out = [SDS((4, r // 2, c), BF16) if ax == 0 else SDS((r // 2, 4 * c), BF16)
           for (r, c), ax in zip(shapes, axes)]
    return _Exchange(parts, out, [pltpu.SemaphoreType.DMA((n,)), pltpu.SemaphoreType.DMA((n,))], start, finish)


def _scatter_halves(pres, shapes):
    n = len(pres)

    def copies(ins, outs, sems):
        x, y, c = _place()
        return [pltpu.make_async_remote_copy(
            src_ref=ins[i].at[1 + j], dst_ref=outs[i].at[j], send_sem=sems[0].at[3 * i + j],
            recv_sem=sems[1].at[3 * i + j], device_id=(cx, cy, c), device_id_type=MESH)
            for i in range(n) for j, (cx, cy) in enumerate(_other_chips(x, y))]

    def start(ins, outs, sems):
        for cp in copies(ins, outs, sems):
            cp.start()

    def finish(ins, outs, sems):
        for cp in copies(ins, outs, sems):
            cp.wait()

    return _Exchange(pres, [SDS((3, r // 2, c), BF16) for r, c in shapes],
                     [pltpu.SemaphoreType.DMA((3 * n,)), pltpu.SemaphoreType.DMA((3 * n,))], start, finish)


def _swap_with_sibling(arrs):
    n = len(arrs)

    def copies(ins, outs, sems):
        x, y, c = _place()
        return [pltpu.make_async_remote_copy(src_ref=ins[i], dst_ref=outs[i], send_sem=sems[0].at[i],
                                             recv_sem=sems[1].at[i], device_id=(x, y, 1 - c),
                                             device_id_type=MESH) for i in range(n)]

    def start(ins, outs, sems):
        for cp in copies(ins, outs, sems):
            cp.start()

    def finish(ins, outs, sems):
        for cp in copies(ins, outs, sems):
            cp.wait()

    return _Exchange(arrs, [SDS(a.shape, a.dtype) for a in arrs],
                     [pltpu.SemaphoreType.DMA((n,)), pltpu.SemaphoreType.DMA((n,))], start, finish)


def _gather_small(arrs):
    n = len(arrs)

    def copy(sems, outs, i, k, block, to, src=None):
        px, py, pc = block
        dst = outs[i].at[4 * px + 2 * py + pc]
        return pltpu.make_async_remote_copy(
            src_ref=dst if src is None else src, dst_ref=dst, send_sem=sems[0].at[7 * i + k],
            recv_sem=sems[1].at[7 * i + k], device_id=to, device_id_type=MESH)

    def own_copies(ins, outs, sems):
        x, y, c = _place()
        cps = []
        for i in range(n):
            cps.append(copy(sems, outs, i, 0, (x, y, c), (x, y, 1 - c), src=ins[i]))
            for j, (cx, cy) in enumerate(_other_chips(x, y)):
                cps.append(copy(sems, outs, i, 1 + j, (x, y, c), (cx, cy, c), src=ins[i]))
        local = [pltpu.make_async_copy(ins[i], outs[i].at[4 * x + 2 * y + c], sems[2].at[i]) for i in range(n)]
        return cps, local

    def start(ins, outs, sems):
        cps, local = own_copies(ins, outs, sems)
        for cp in local + cps:
            cp.start()

    def finish(ins, outs, sems):
        x, y, c = _place()
        passed = []
        for j, (cx, cy) in enumerate(_other_chips(x, y)):
            for i in range(n):
                copy(sems, outs, i, 1 + j, (cx, cy, c), (x, y, c)).wait_recv()
                cp = copy(sems, outs, i, 4 + j, (cx, cy, c), (x, y, 1 - c))
                cp.start()
                passed.append(cp)
        for i in range(n):
            copy(sems, outs, i, 0, (x, y, 1 - c), (x, y, c)).wait_recv()
            for j, (cx, cy) in enumerate(_other_chips(x, y)):
                copy(sems, outs, i, 4 + j, (cx, cy, 1 - c), (x, y, c)).wait_recv()
        cps, local = own_copies(ins, outs, sems)
        for cp in cps + passed:
            cp.wait_send()
        for cp in local:
            cp.wait()

    return _Exchange(arrs, [SDS((N_DEV,) + a.shape, F32) for a in arrs],
                     [pltpu.SemaphoreType.DMA((7 * n,)), pltpu.SemaphoreType.DMA((7 * n,)),
                      pltpu.SemaphoreType.DMA((n,))], start, finish)


def _local_step(x, p, tgt, wb, ws, shards=None, opt=None):
    bsz, seq, _ = x.shape
    t = bsz * seq
    tm = min(256, t)
    tb = min(S5_TIME_BLOCK, seq)
    x0 = x.reshape(t, D_MODEL)
    p0 = p.reshape(t, PLE_DIM)
    tg = tgt.reshape(t, D_MODEL)
    row = lambda v: v.reshape(1, -1)
    dist = shards is not None
    wb = dict(wb)
    recv, sums, other, gathered = {}, {}, {}, {}
    gb = {}
    gs = {}
    shape_of, axis_of = {}, {}
    chip = None
    if dist:
        shape_of = {k: tuple(shards[k].shape) for k in BIG}
        axis_of = dict(BIG_AXIS)
        for q in range(LAST_PIECES):
            shape_of[LAST_PIECE % q] = (D_MODEL // LAST_PIECES, shape_of["ffn1_w_in"][1])
            axis_of[LAST_PIECE % q] = 1
        xi, yi, ci = _place()
        chip = (2 * xi + yi).astype(jnp.int32).reshape(1)
        ids = jnp.stack([2 * xi + yi] + [2 * cx + cy for cx, cy in _other_chips(xi, yi)] + [ci]).astype(jnp.int32)
    halfbuf, pre = {}, {}

    def gather(names):
        return _gather_weights([shards[k] for k in names], [GATHER_AXIS[k] for k in names]) if dist else None

    def exchange(scat=(), swap=(), halves=(), scat2=(), swap2=(), extra=None, after=None):
        if not dist:
            return None, []
        after = order[0] if after is None else after
        parts, tags = [], []
        if scat:
            parts.append(_scatter_grads([gb[k][1] for k in scat], [shape_of[k] for k in scat],
                                        [axis_of[k] for k in scat]))
            tags.append((recv, scat))
        if swap:
            for k in swap:
                sums[k] = order[0] = _sum_blocks(gb[k][0], recv[k], shape_of[k], axis_of[k], chip, "sum_" + k,
                                                 order[0])
            parts.append(_swap_with_sibling([sums[k] for k in swap]))
            tags.append((other, swap))
        if halves:
            parts.append(_swap_halves([gb[k][1] for k in halves], [shape_of[k] for k in halves],
                                      [axis_of[k] for k in halves]))
            tags.append((halfbuf, halves))
        if scat2:
            for k in scat2:
                pre[k] = _presum(gb[k][0], halfbuf[k], shape_of[k], axis_of[k], ids, "presum_" + k, order[0])
                order[0] = pre[k][0]
            parts.append(_scatter_halves([pre[k][1] for k in scat2], [shape_of[k] for k in scat2]))
            tags.append((recv, scat2))
        if swap2:
            for k in swap2:
                sums[k] = order[0] = _sum_half(pre[k][0], recv[k], "sum_" + k, order[0])
            parts.append(_swap_with_sibling([sums[k] for k in swap2]))
            tags.append((other, swap2))
        if extra is not None:
            parts.append(extra[0])
            tags.append((extra[1], extra[2]))
        return (_join(parts), tags) if parts else (None, [])

    def take(ex_tags, got):
        ex, tags = ex_tags
        if ex is not None:
            for (dst, names), (o0, o1) in zip(tags, ex.cuts):
                dst.update(zip(names, got[o0:o1]))

    order = [None]

    def ordered(builder, *args, **kw):
        res = builder(*args, bg=order[0] if dist else None, **kw)
        order[0] = res[0][0]
        return res

    launched = []

    def launch(ex_tags):
        if ex_tags[0] is not None:
            n = len(launched)
            launched.append(n)
            take(ex_tags, _run_exchange_on_sequencer(ex_tags[0], "reduce_%d" % n, REDUCE_FIRST_ID + n))

    small_shape = {k: _small_view(k, v).shape for k, v in ws.items()}
    small_shape["loss_rows"] = (1, D_MODEL)
    ws = {k: v if (v.ndim == 2 and k != "ssm_log_dt") else v[0] for k, v in ws.items()}
    tril = jnp.tril(jnp.ones((CHUNK, CHUNK), dtype=bool))
    wsm = jnp.where(tril[None], ws["gmlp_w_s"], 0.0)
    wsm_b = wsm.astype(BF16)
    wsmt_b = wsm.transpose(0, 2, 1).astype(BF16)
    bias = jnp.repeat(ws["gmlp_b_s"].T, GMLP_HEAD_DIM, axis=1)

    tf = min(512, t)
    if dist:
        for gi, names in enumerate(GATHER_ORDER):
            wb.update(zip(names, _run_exchange_on_sequencer(gather(names), "gather_%d" % gi, GATHER_FIRST_ID + gi)))
    (x0b, h1, a1), _ = _ffn_proj(x0, wb["ffn1_w_in"], tf, "ffn1_proj")
    (x1, xh1, rstd1), _ = _ffn_out(x0, a1, wb["ffn1_w_out"], row(ws["ln1_g"]), row(ws["ln1_b"]), tf, "ffn1_out")
    sp = _s5_setup(ws["ssm_lambda_re"], ws["ssm_lambda_im"], ws["ssm_log_dt"], ws["ssm_b_re"],
                   ws["ssm_b_im"], ws["ssm_c_re"], ws["ssm_c_im"], ws["ssm_d"], wb["ssm_glu_w"],
                   ws["ssm_glu_b"], tb)
    (x1b, za, zuv, gab), _ = _mixin_fwd(x1, wb["mix_w_in"], tf)
    (s5o, s5ot, y2p, carries), _ = _s5_fwd(za, sp, bsz, seq, tb)
    (gm, gmt), _ = _gmlp_fwd(zuv, row(ws["gmlp_ln_g"]), row(ws["gmlp_ln_b"]), wsm_b, bias)
    (x2, xh2, rstd2), _ = _mixout_fwd(x1, s5o, gm, gab, wb["up_a"], wb["up_b"], wb["mix_w_out"],
                                           row(ws["ln2_g"]), row(ws["ln2_b"]), tf)
    (x2b, h2, a2), _ = _ffn_proj(x2, wb["ffn2_w_in"], tf, "ffn2_proj")
    (xh3, rstd3, dx3, x3b, pb, dq, de, loss_rows), _ = _ffn_out_loss(
        x2, a2, wb["ffn2_w_out"], row(ws["ln3_g"]), row(ws["ln3_b"]), p0, tg, wb["ple_w_gate"], wb["ple_w_proj"], tf)
    order[0] = dx3
    gb["ple_w_gate"], _ = ordered(_tn_matmul, x3b, dq, "dw_ple_gate", 1024, 1024, a_t=True)
    gb["ple_w_proj"], _ = ordered(_tn_matmul, pb, de, "dw_ple_proj", 256, 1024, a_t=True)
    launch(exchange(scat=("ple_w_gate", "ple_w_proj")))
    (dx2, dh2, df2, gs["ln3_g"], gs["ln3_b"]), _ = ordered(
        _ffn_bwd, dx3, xh3, rstd3, h2, wb["ffn2_w_in"], wb["ffn2_w_out"], row(ws["ln3_g"]), tm // 2, "ffn2_bwd")
    gb["ffn2_w_out"], _ = ordered(_tn_matmul, a2, df2, "dw_ffn2_out", 1408, 1024)
    launch(exchange(scat=("ffn2_w_out",)))
    gb["ffn2_w_in"], _ = ordered(_tn_matmul, x2b, dh2, "dw_ffn2_in", 1024, 1408, a_t=True)
    launch(exchange(scat=("ffn2_w_in",), swap=("ple_w_gate", "ple_w_proj")))
    (dx1a, dmx, mb, dya, dyb, ds5, dgm, dgab, gs["ln2_g"], gs["ln2_b"]), _ = ordered(
        _mixout_bwd, dx2, xh2, rstd2, s5o, gm, gab, wb["up_a"], wb["up_b"], wb["mix_w_out"], row(ws["ln2_g"]), tf)
    gb["mix_w_out"], _ = ordered(_tn_matmul, mb, dmx, "dw_mix_out", 1024, 1024, a_t=True)
    gb["up_a"], _ = ordered(_tn_matmul, s5ot, dya, "dw_up_a", 512, 1024, a_t=True)
    gb["up_b"], _ = ordered(_tn_matmul, gmt, dyb, "dw_up_b", 512, 1024, a_t=True)
    launch(exchange(scat=("mix_w_out", "up_a", "up_b"), swap=("ffn2_w_out",)))
    (dza, dmr, dmi, dnr, dni, da, ddsk, dgw, dgb), _ = ordered(_s5_bwd, za, y2p, ds5, carries, sp, bsz, seq, tb)
    gb["ssm_glu_w"] = (dgw, dgw.astype(BF16))
    launch(exchange(scat=("ssm_glu_w",), swap=("ffn2_w_in",)))
    (dzuv, dws, dbias, gs["gmlp_ln_g"], gs["gmlp_ln_b"]), _ = ordered(
        _gmlp_bwd, zuv, dgm, row(ws["gmlp_ln_g"]), row(ws["gmlp_ln_b"]), wsm_b, wsmt_b, bias)
    (dx1,), _ = ordered(_mixin_bwd, dx1a, dza, dzuv, dgab, wb["mix_w_in"], tf)
    g_mi, _ = ordered(_tn_matmul, x1b, dza, "dw_mix_in_a", 1024, 512, 0, 3584, a_t=True)
    g_mi, _ = ordered(_tn_matmul, x1b, dzuv, "dw_mix_in_uv", 1024, 512, 1, 3584, g_mi, a_t=True)
    gb["mix_w_in"], _ = ordered(_tn_matmul, x1b, dgab, "dw_mix_in_g", 1024, 512, 3, 3584, g_mi, a_t=True)
    launch(exchange(swap=("mix_w_out", "up_a", "up_b", "ssm_glu_w")))

    d_abr = da[0].sum(axis=0).reshape(SSM_GROUPS, SSM_STATE)
    d_abi = da[1].sum(axis=0).reshape(SSM_GROUPS, SSM_STATE)
    _, vjp = jax.vjp(_s5_discretise, ws["ssm_lambda_re"], ws["ssm_lambda_im"], ws["ssm_log_dt"],
                     ws["ssm_b_re"], ws["ssm_b_im"])
    (gs["ssm_lambda_re"], gs["ssm_lambda_im"], gs["ssm_log_dt"], gs["ssm_b_re"], gs["ssm_b_im"]) = vjp(
        (d_abr, d_abi, _block_diag_in_t(dmr), _block_diag_in_t(dmi)))
    gs["ssm_c_re"] = _block_diag_out_t(dnr)
    gs["ssm_c_im"] = _block_diag_out_t(dni)
    gs["ssm_d"] = ddsk
    gs["ssm_glu_b"] = dgb
    gs["gmlp_w_s"] = dws
    gs["gmlp_b_s"] = dbias.reshape(CHUNK, GMLP_HEADS, GMLP_HEAD_DIM).sum(axis=-1).T
    gs["loss_rows"] = loss_rows

    def small_gather(names):
        return (_gather_small([gs[k].reshape(small_shape[k]) for k in names]), gathered, names) if dist else None

    late = ("ln1_g", "ln1_b")
    launch(exchange(scat=("mix_w_in",), extra=small_gather(tuple(k for k in SMALL + ("loss_rows",) if k not in late))))
    (dx0, dh1, df1, gs["ln1_g"], gs["ln1_b"]), _ = ordered(
        _ffn_bwd, dx1, xh1, rstd1, h1, wb["ffn1_w_in"], wb["ffn1_w_out"], row(ws["ln1_g"]), tm // 2, "ffn1_bwd")
    grad_x = dx0.reshape(bsz, seq, D_MODEL)
    if not dist:
        gb["ffn1_w_out"], _ = _tn_matmul(a1, df1, "dw_ffn1_out", 1408, 1024)
        gb["ffn1_w_in"], _ = _tn_matmul(x0b, dh1, "dw_ffn1_in", 1024, 1408, a_t=True)
        return (loss_rows, grad_x, gb, {k: gs[k].reshape(small_shape[k]) for k in SMALL}, sums, other, gathered,
                None, {})
    launch(exchange(extra=small_gather(late)))
    gb["ffn1_w_out"], _ = ordered(_tn_matmul, a1, df1, "dw_ffn1_out", 1408, 1024)
    last = ["ffn1_w_out"] + [LAST_PIECE % q for q in range(LAST_PIECES)]
    fillers = (("ffn2_w_in", "mix_w_in", "ple_w_gate"),
               ("ffn2_w_out", "mix_w_out", "up_a", "up_b", "ssm_glu_w", "ple_w_proj"))
    out = {}
    for i in range(1, len(last) + 3):
        stage = lambda d: tuple(last[i - d:i - d + 1]) if 0 <= i - d < len(last) else ()
        launch(exchange(halves=stage(1), scat2=stage(2), swap2=stage(3), swap=("mix_w_in",) if i == 2 else ()))
        if i < len(last):
            gb[last[i]], _ = ordered(_tn_matmul, x0b, dh1, "dw_" + last[i], D_MODEL // LAST_PIECES, 1408,
                                     a_cols=(i - 1, 1), a_t=True)
        elif i - len(last) < len(fillers):
            for k in fillers[i - len(last)]:
                w, m, v = opt[k]
                out[k] = _adam_big(w, sums[k], other[k], m, v, "adam_" + k, after=order[0])
                order[0] = out[k][1]
    return loss_rows, grad_x, gb, gs, sums, other, gathered, ids, out


def _adamw(w, g, m, v):
    m = ADAM_B1 * m + (1.0 - ADAM_B1) * g
    v = ADAM_B2 * v + (1.0 - ADAM_B2) * (g * g)
    m_hat = m / ADAM_C1
    v_hat = v / ADAM_C2
    delta = -ADAM_LR * (m_hat / (jnp.sqrt(v_hat) + ADAM_EPS) + ADAM_WD * w)
    return delta, m, v


def _pinned(after):
    return ([pl.BlockSpec(memory_space=pl.ANY)], [after]) if after is not None else ([], [])


def _sum_blocks(part, recv, shape, axis, chip, name, after=None):
    r, c = shape
    rb = r // ROW_STEPS

    def body(chip_ref, p_ref, r_ref, *rest):
        rest[-1][...] = (p_ref[...] + r_ref[0].astype(F32) + r_ref[1].astype(F32) + r_ref[2].astype(F32))

    if axis == 0:
        own = pl.BlockSpec((rb, c), lambda i, k: (k[0] * ROW_STEPS + i, 0))
    else:
        own = pl.BlockSpec((rb, c), lambda i, k: (i, k[0]))
    pin_specs, pin_args = _pinned(after)
    grid_spec = pltpu.PrefetchScalarGridSpec(
        num_scalar_prefetch=1, grid=(ROW_STEPS,),
        in_specs=[own, pl.BlockSpec((3, rb, c), lambda i, k: (0, i, 0))] + pin_specs,
        out_specs=pl.BlockSpec((rb, c), lambda i, k: (i, 0)))
    return pl.pallas_call(body, name=name, out_shape=SDS((r, c), F32), grid_spec=grid_spec,
                          compiler_params=_params(("parallel",)))(chip, part, recv, *pin_args)


def _presum(part, half, shape, axis, ids, name, after=None):
    r, c = shape
    rb = r // 2

    def body(ids_ref, p_ref, h_ref, *rest):
        of_ref, ob_ref = rest[-2:]
        s = p_ref[...] + h_ref[...].astype(F32)
        ob_ref[...] = s.astype(BF16)

        @pl.when(pl.program_id(1) == 0)
        def _():
            of_ref[...] = s

    if axis == 0:
        p_spec = pl.BlockSpec((rb, c), lambda i, t, ids: (ids[t] * 2 + ids[4] + i, 0))
        h_spec = pl.BlockSpec((None, rb, c), lambda i, t, ids: (ids[t], i, 0))
    else:
        p_spec = pl.BlockSpec((rb, c), lambda i, t, ids: (ids[4] + i, ids[t]))
        h_spec = pl.BlockSpec((rb, c), lambda i, t, ids: (i, ids[t]))
    pin_specs, pin_args = _pinned(after)
    grid_spec = pltpu.PrefetchScalarGridSpec(
        num_scalar_prefetch=1, grid=(1, 4), in_specs=[p_spec, h_spec] + pin_specs,
        out_specs=(pl.BlockSpec((rb, c), lambda i, t, ids: (i, 0)),
                   pl.BlockSpec((None, rb, c), lambda i, t, ids: (t, i, 0))))
    return pl.pallas_call(body, name=name, out_shape=(SDS((r // 2, c), F32), SDS((4, r // 2, c), BF16)),
                          grid_spec=grid_spec,
                          compiler_params=_params(("parallel", "arbitrary")))(ids, part, half, *pin_args)


def _sum_half(pre, recv, name, after=None):
    hr, c = pre.shape
    rb = hr

    def body(p_ref, r_ref, *rest):
        rest[-1][...] = (p_ref[...] + r_ref[0].astype(F32) + r_ref[1].astype(F32) + r_ref[2].astype(F32))

    spec = pl.BlockSpec((rb, c), lambda i: (i, 0))
    pin_specs, pin_args = _pinned(after)
    return pl.pallas_call(body, name=name, grid=(1,), out_shape=SDS((hr, c), F32),
                          in_specs=[spec, pl.BlockSpec((3, rb, c), lambda i: (0, i, 0))] + pin_specs,
                          out_specs=spec, compiler_params=_params(("parallel",)))(pre, recv, *pin_args)


def _adam_halves(w, mine, oth, m, v, ids, name, piece=0, prev=None):
    r, c = w.shape
    rb = mine.shape[0] // 2

    def body(ids_ref, w_ref, a_ref, b_ref, m_ref, v_ref, *rest):
        g_ref, d_ref, nm_ref, nv_ref = rest[-4:]
        g = jnp.where(pl.program_id(0) // 2 == ids_ref[4], a_ref[...], b_ref[...])
        g_ref[...] = g
        d_ref[...], nm_ref[...], nv_ref[...] = _adamw(w_ref[...], g, m_ref[...], v_ref[...])

    whole = pl.BlockSpec((rb, c), lambda i, ids: (i + 4 * piece, 0))
    part = pl.BlockSpec((rb, c), lambda i, ids: (i % 2, 0))
    in_specs = [whole, part, part, whole, whole]
    args = [w, mine, oth, m, v]
    aliases = {}
    if prev is not None:
        in_specs += [pl.BlockSpec(memory_space=pl.ANY)] * 4
        args += list(prev)
        aliases = {6: 0, 7: 1, 8: 2, 9: 3}
    grid_spec = pltpu.PrefetchScalarGridSpec(num_scalar_prefetch=1, grid=(4,), in_specs=in_specs,
                                             out_specs=(whole,) * 4)
    return pl.pallas_call(body, name=name, out_shape=tuple(SDS((r, c), F32) for _ in range(4)),
                          grid_spec=grid_spec, input_output_aliases=aliases,
                          compiler_params=_params(("parallel",)))(ids, *args)


def _adam_big(w, ga, gb, m, v, name, piece=0, prev=None, after=None):
    r, c = w.shape
    pr = ga.shape[0]
    steps = ROW_STEPS if pr == r else 2
    rb = pr // steps
    off = piece * steps

    def body(w_ref, ga_ref, gb_ref, m_ref, v_ref, *rest):
        g_ref, d_ref, nm_ref, nv_ref = rest[-4:]
        g = ga_ref[...] + gb_ref[...]
        g_ref[...] = g
        d_ref[...], nm_ref[...], nv_ref[...] = _adamw(w_ref[...], g, m_ref[...], v_ref[...])

    whole = pl.BlockSpec((rb, c), lambda i: (i + off, 0))
    part = pl.BlockSpec((rb, c), lambda i: (i, 0))
    in_specs = [whole, part, part, whole, whole]
    args = [w, ga, gb, m, v]
    aliases = {}
    if prev is not None:
        in_specs += [pl.BlockSpec(memory_space=pl.ANY)] * 4
        args += list(prev)
        aliases = {5: 0, 6: 1, 7: 2, 8: 3}
    if after is not None:
        in_specs.append(pl.BlockSpec(memory_space=pl.ANY))
        args.append(after)
    return pl.pallas_call(
        body, name=name, grid=(steps,), out_shape=tuple(SDS((r, c), F32) for _ in range(4)),
        in_specs=in_specs, out_specs=(whole,) * 4, input_output_aliases=aliases,
        compiler_params=_params(("parallel",)),
    )(*args)


def _adam_small(ws, gathered, ms, vs):
    n = len(ws)

    def body(*refs):
        w_refs, g_refs, m_refs, v_refs = refs[:n], refs[n:2 * n], refs[2 * n:3 * n], refs[3 * n:4 * n]
        outs = refs[4 * n:]
        for i in range(n):
            g = g_refs[i][0]
            for d in range(1, N_DEV):
                g = g + g_refs[i][d]
            delta, nm, nv = _adamw(w_refs[i][...], g, m_refs[i][...], v_refs[i][...])
            outs[i][...] = g
            outs[n + i][...] = delta
            outs[2 * n + i][...] = nm
            outs[3 * n + i][...] = nv

    vmem = pl.BlockSpec(memory_space=pltpu.VMEM)
    shapes = [w.shape for w in ws]
    return pl.pallas_call(
        body, name="adam_small", out_shape=tuple(SDS(s, F32) for s in shapes * 4),
        in_specs=[vmem] * (4 * n), out_specs=tuple([vmem] * (4 * n)),
        compiler_params=pltpu.CompilerParams(vmem_limit_bytes=VMEM_LIMIT_BYTES),
    )(*ws, *gathered, *ms, *vs)


def _sum_loss(gathered):
    def body(g_ref, o_ref):
        tot = g_ref[0]
        for d in range(1, N_DEV):
            tot = tot + g_ref[d]
        o_ref[...] = (0.5 / D_MODEL) * jnp.sum(tot, axis=1, keepdims=True)

    vmem = pl.BlockSpec(memory_space=pltpu.VMEM)
    return pl.pallas_call(body, name="sum_loss", out_shape=SDS((1, 1), F32), in_specs=[vmem],
                          out_specs=vmem)(gathered)


def kernel(x, p, ffn1_w_in, ffn1_w_out, ln1_g, ln1_b, mix_w_in, ssm_lambda_re, ssm_lambda_im, ssm_log_dt, ssm_b_re, ssm_b_im, ssm_c_re, ssm_c_im, ssm_d, ssm_glu_w, ssm_glu_b, gmlp_ln_g, gmlp_ln_b, gmlp_w_s, gmlp_b_s, up_a, up_b, mix_w_out, ln2_g, ln2_b, ffn2_w_in, ffn2_w_out, ln3_g, ln3_b, ple_w_proj, ple_w_gate, loss_target, m_ffn1_w_in, m_ffn1_w_out, m_ln1_g, m_ln1_b, m_mix_w_in, m_ssm_lambda_re, m_ssm_lambda_im, m_ssm_log_dt, m_ssm_b_re, m_ssm_b_im, m_ssm_c_re, m_ssm_c_im, m_ssm_d, m_ssm_glu_w, m_ssm_glu_b, m_gmlp_ln_g, m_gmlp_ln_b, m_gmlp_w_s, m_gmlp_b_s, m_up_a, m_up_b, m_mix_w_out, m_ln2_g, m_ln2_b, m_ffn2_w_in, m_ffn2_w_out, m_ln3_g, m_ln3_b, m_ple_w_proj, m_ple_w_gate, v_ffn1_w_in, v_ffn1_w_out, v_ln1_g, v_ln1_b, v_mix_w_in, v_ssm_lambda_re, v_ssm_lambda_im, v_ssm_log_dt, v_ssm_b_re, v_ssm_b_im, v_ssm_c_re, v_ssm_c_im, v_ssm_d, v_ssm_glu_w, v_ssm_glu_b, v_gmlp_ln_g, v_gmlp_ln_b, v_gmlp_w_s, v_gmlp_b_s, v_up_a, v_up_b, v_mix_w_out, v_ln2_g, v_ln2_b, v_ffn2_w_in, v_ffn2_w_out, v_ln3_g, v_ln3_b, v_ple_w_proj, v_ple_w_gate):
    given = dict(locals())
    order = ("ffn1_w_in", "ffn1_w_out", "ln1_g", "ln1_b", "mix_w_in", "ssm_lambda_re", "ssm_lambda_im",
             "ssm_log_dt", "ssm_b_re", "ssm_b_im", "ssm_c_re", "ssm_c_im", "ssm_d", "ssm_glu_w", "ssm_glu_b",
             "gmlp_ln_g", "gmlp_ln_b", "gmlp_w_s", "gmlp_b_s", "up_a", "up_b", "mix_w_out", "ln2_g", "ln2_b",
             "ffn2_w_in", "ffn2_w_out", "ln3_g", "ln3_b", "ple_w_proj", "ple_w_gate")
    assert set(order) == set(BIG + SMALL)

    shard = {k: given[k][0] for k in BIG}
    shard_b = {k: shard[k].astype(BF16) for k in BIG}
    opt = {k: (shard[k], given["m_" + k][0], given["v_" + k][0]) for k in BIG}
    loss_rows, grad_x, gb, gs, sums, other, gathered, ids, out = _local_step(
        x, given["p"][0], loss_target, {}, {k: given[k] for k in SMALL}, shard_b, opt)

    out = dict(out)
    for k in BIG:
        if k in out:
            continue
        moments = (given["m_" + k][0], given["v_" + k][0])
        if k == "ffn1_w_out":
            out[k] = _adam_halves(shard[k], sums[k], other[k], *moments, ids, "adam_" + k)
        elif k == "ffn1_w_in":
            for q in range(LAST_PIECES):
                kq = LAST_PIECE % q
                out[k] = _adam_halves(shard[k], sums[kq], other[kq], *moments, ids, "adam_" + kq, q, out.get(k))
        else:
            out[k] = _adam_big(shard[k], sums[k], other[k], *moments, "adam_" + k,
                               after=gb[LAST_PIECE % (LAST_PIECES - 1)][0])

    res = _adam_small([_small_view(k, given[k]) for k in SMALL], [gathered[k] for k in SMALL],
                      [_small_view(k, given["m_" + k]) for k in SMALL],
                      [_small_view(k, given["v_" + k]) for k in SMALL])
    ns = len(SMALL)
    for i, k in enumerate(SMALL):
        out[k] = tuple(res[j * ns + i].reshape(given[k].shape) for j in range(4))
    loss = _sum_loss(gathered["loss_rows"]).reshape(())

    lead = lambda k, j: out[k][j][None] if k in BIG else out[k][j]
    return (loss, grad_x, *[lead(k, 0) for k in order], *[lead(k, 1) for k in order],
            *[lead(k, 2) for k in order], *[lead(k, 3) for k in order])
```

```python
import math

import jax
import jax.numpy as jnp
from jax import lax
from jax.experimental import pallas as pl
from jax.experimental.pallas import tpu as pltpu
from jax.experimental.pallas import tpu_sc as plsc

F32 = jnp.float32
BF16 = jnp.bfloat16
MESH = pl.DeviceIdType.MESH
SDS = jax.ShapeDtypeStruct

D_MODEL = 1024
D_FF = 2816
D_SSM = 512
D_GMLP = 512
SSM_GROUPS = 32
SSM_GROUP_CH = 16
SSM_STATE = 64
SSM_LANES = SSM_GROUPS * SSM_STATE
GMLP_HEADS = 8
GMLP_HEAD_DIM = 64
CHUNK = 128
PLE_DIM = 256
LN_EPS = 1e-5
ALPHA = 2.0 ** 0.25

ADAM_LR = 0.001
ADAM_B1 = 0.9
ADAM_B2 = 0.999
ADAM_EPS = 1e-08
ADAM_WD = 0.01
ADAM_STEP = 10
ADAM_C1 = 1.0 - ADAM_B1 ** ADAM_STEP
ADAM_C2 = 1.0 - ADAM_B2 ** ADAM_STEP

N_DEV = 8
VMEM_LIMIT_BYTES = 56 * 1024 * 1024
FFN_COLS = 1408
S5_BLOCKS = 4
S5_BLOCK_IN = D_SSM // S5_BLOCKS
S5_BLOCK_ST = SSM_LANES // S5_BLOCKS
SCAN_LANES = 512
S5_TIME_BLOCK = 512
TN_K_BLOCK = 2048
TN_SMALL_BLOCK = 1024 * 1024
GMLP_CHUNKS = 8
ROW_STEPS = 4
LAST_PIECES = 2
LAST_PIECE = "ffn1_w_in_q%d"
_G0 = math.sqrt(2.0 / math.pi)
_G1 = 0.044715


def _dot(a, b):
    return jnp.dot(a, b, preferred_element_type=F32)


def _dot_nt(a, b):
    return lax.dot_general(a, b, (((1,), (1,)), ((), ())), preferred_element_type=F32)


def _dot_tn(a, b):
    return lax.dot_general(a, b, (((0,), (0,)), ((), ())), preferred_element_type=F32)


def _sigmoid(x):
    return 0.5 * jnp.tanh(0.5 * x) + 0.5


def _gelu(x):
    t = jnp.tanh(_G0 * (x + _G1 * x * x * x))
    return 0.5 * x * (1.0 + t)


def _gelu_grad(x):
    t = jnp.tanh(_G0 * (x + _G1 * x * x * x))
    return 0.5 * (1.0 + t) + 0.5 * x * (1.0 - t * t) * _G0 * (1.0 + 3.0 * _G1 * x * x)


def _ln_fwd(r, g, b):
    mu = jnp.mean(r, axis=-1, keepdims=True)
    d = r - mu
    var = jnp.mean(d * d, axis=-1, keepdims=True)
    rstd = lax.rsqrt(var + LN_EPS)
    xh = d * rstd
    return xh * g + b, xh, rstd


def _ln_bwd(dy, xh, rstd, g):
    dxh = dy * g
    m1 = jnp.mean(dxh, axis=-1, keepdims=True)
    m2 = jnp.mean(dxh * xh, axis=-1, keepdims=True)
    return rstd * (dxh - m1 - xh * m2)


def _resident(shape):
    nd = len(shape)
    return pl.BlockSpec(shape, lambda *_: (0,) * nd, pipeline_mode=pl.Buffered(1))


def _fixed(shape):
    nd = len(shape)
    return pl.BlockSpec(shape, lambda *_: (0,) * nd)


def _rows(tm, cols):
    return pl.BlockSpec((tm, cols), lambda i: (i, 0))


def _cols(rows, tm):
    return pl.BlockSpec((rows, tm), lambda i: (0, i))


def _params(sem):
    return pltpu.CompilerParams(dimension_semantics=sem, vmem_limit_bytes=VMEM_LIMIT_BYTES)


class _Exchange:
    def __init__(self, args, out_shape, sems, start, finish):
        self.args, self.out_shape, self.sems = list(args), list(out_shape), list(sems)
        self.start, self.finish = start, finish
        self.cuts = [(0, len(self.out_shape))]


def _call(body, name, grid, in_specs, out_specs, out_shape, args, scratch=(), sem=None, bg=None, aliases=None):
    aliases = {} if aliases is None else aliases
    in_specs, args = list(in_specs), list(args)
    fn = body
    if bg is not None:
        n_args = len(args)

        def fn(*refs):
            body(*refs[:n_args], *refs[n_args + 1:])

        in_specs.append(pl.BlockSpec(memory_space=pl.ANY))
        args.append(bg)
    res = pl.pallas_call(fn, name=name, grid=grid, out_shape=tuple(out_shape), in_specs=in_specs,
                         out_specs=tuple(out_specs), scratch_shapes=list(scratch),
                         input_output_aliases=aliases, compiler_params=_params(sem))(*args)
    return tuple(res), ()


def _run_exchange_on_sequencer(ex, name, collective_id):
    n_i, n_o = len(ex.args), len(ex.out_shape)

    def body(*refs):
        ins, outs, sems = refs[:n_i], refs[n_i:n_i + n_o], refs[n_i + n_o:]
        x, y, c = lax.axis_index("x"), lax.axis_index("y"), lax.axis_index("c")
        barrier = pltpu.get_barrier_semaphore()
        for peer in [(x, y, 1 - c), (1 - x, y, c), (x, 1 - y, c), (1 - x, 1 - y, c)]:
            pl.semaphore_signal(barrier, inc=1, device_id=peer, device_id_type=MESH)
        pl.semaphore_wait(barrier, 4)
        ex.start(ins, outs, sems)
        ex.finish(ins, outs, sems)

    return tuple(pl.kernel(body, out_type=tuple(ex.out_shape),
                           mesh=plsc.ScalarSubcoreMesh(axis_name="sequencer", num_cores=1),
                           scratch_types=list(ex.sems), name=name,
                           compiler_params=pltpu.CompilerParams(collective_id=collective_id))(*ex.args))


def _join(exchanges):
    cuts = []
    a = o = q = 0
    for e in exchanges:
        cuts.append((a, a + len(e.args), o, o + len(e.out_shape), q, q + len(e.sems)))
        a, o, q = cuts[-1][1], cuts[-1][3], cuts[-1][5]

    def start(ins, outs, sems):
        for e, (a0, a1, o0, o1, q0, q1) in zip(exchanges, cuts):
            e.start(ins[a0:a1], outs[o0:o1], sems[q0:q1])

    def finish(ins, outs, sems):
        for e, (a0, a1, o0, o1, q0, q1) in zip(exchanges, cuts):
            e.finish(ins[a0:a1], outs[o0:o1], sems[q0:q1])

    joined = _Exchange(sum((e.args for e in exchanges), []), sum((e.out_shape for e in exchanges), []),
                       sum((e.sems for e in exchanges), []), start, finish)
    joined.cuts = [(c[2], c[3]) for c in cuts]
    return joined


def _ffn_proj(x, w_in, tm, name, bg=None):
    t = x.shape[0]
    nch = D_FF // FFN_COLS

    def body(x_ref, win_ref, xbt_ref, h_ref, a_ref):
        xb = x_ref[...].astype(BF16)
        xbt_ref[...] = xb.T
        for k in range(nch):
            cg = slice(k * FFN_COLS, (k + 1) * FFN_COLS)
            cu = slice(D_FF + k * FFN_COLS, D_FF + (k + 1) * FFN_COLS)
            hg = _dot(xb, win_ref[k])
            hu = _dot(xb, win_ref[nch + k])
            h_ref[:, cg] = hg.astype(BF16)
            h_ref[:, cu] = hu.astype(BF16)
            a_ref[:, cg] = (hg * _sigmoid(hg) * hu).astype(BF16)

    return _call(
        body, name, (t // tm,),
        [_rows(tm, D_MODEL), _resident((2 * nch, D_MODEL, FFN_COLS))],
        (_cols(D_MODEL, tm), _rows(tm, 2 * D_FF), _rows(tm, D_FF)),
        (SDS((D_MODEL, t), BF16), SDS((t, 2 * D_FF), BF16), SDS((t, D_FF), BF16)),
        (x, w_in), sem=("parallel",), bg=bg)


def _ffn_out(x, a, w_out, g, b, tm, name, bg=None):
    t = x.shape[0]

    def body(x_ref, a_ref, wout_ref, g_ref, b_ref, xn_ref, xh_ref, rstd_ref):
        f = _dot(a_ref[...], wout_ref[...])
        y, xh, rstd = _ln_fwd(ALPHA * x_ref[...] + 0.5 * f, g_ref[...], b_ref[...])
        xn_ref[...] = y
        xh_ref[...] = xh
        rstd_ref[...] = rstd

    return _call(
        body, name, (t // tm,),
        [_rows(tm, D_MODEL), _rows(tm, D_FF), _resident((D_FF, D_MODEL)), _fixed((1, D_MODEL)), _fixed((1, D_MODEL))],
        (_rows(tm, D_MODEL), _rows(tm, D_MODEL), _rows(tm, 1)),
        (SDS((t, D_MODEL), F32), SDS((t, D_MODEL), F32), SDS((t, 1), F32)),
        (x, a, w_out, g, b), sem=("parallel",), bg=bg)


def _ffn_out_loss(x, a, w_out, g, b, p, tgt, wpg, wpp, tm):
    t = x.shape[0]

    def body(x_ref, a_ref, wout_ref, g_ref, b_ref, p_ref, t_ref, wpg_ref, wpp_ref,
             xh_ref, rstd_ref, dx_ref, xbt_ref, pbt_ref, dq_ref, de_ref, loss_ref):
        @pl.when(pl.program_id(0) == 0)
        def _():
            loss_ref[...] = jnp.zeros_like(loss_ref)

        f = _dot(a_ref[...], wout_ref[...])
        x3v, xh, rstd = _ln_fwd(ALPHA * x_ref[...] + 0.5 * f, g_ref[...], b_ref[...])
        xh_ref[...] = xh
        rstd_ref[...] = rstd
        xb = x3v.astype(BF16)
        pb = p_ref[...].astype(BF16)
        xbt_ref[...] = xb.T
        pbt_ref[...] = pb.T
        s = _sigmoid(_dot(xb, wpg_ref[...]))
        e = _dot(pb, wpp_ref[...])
        diff = x3v + s * e - t_ref[...]
        loss_ref[...] += jnp.sum(diff * diff, axis=0, keepdims=True)
        dout = diff * (1.0 / D_MODEL)
        de_ref[...] = (dout * s).astype(BF16)
        dq = (dout * e * s * (1.0 - s)).astype(BF16)
        dq_ref[...] = dq
        dx_ref[...] = dout + _dot_nt(dq, wpg_ref[...])

    return _call(
        body, "ffn2_out_loss", (t // tm,),
        [_rows(tm, D_MODEL), _rows(tm, D_FF), _resident((D_FF, D_MODEL)), _fixed((1, D_MODEL)), _fixed((1, D_MODEL)),
         _rows(tm, PLE_DIM), _rows(tm, D_MODEL), _resident((D_MODEL, D_MODEL)), _resident((PLE_DIM, D_MODEL))],
        (_rows(tm, D_MODEL), _rows(tm, 1), _rows(tm, D_MODEL), _cols(D_MODEL, tm), _cols(PLE_DIM, tm),
         _rows(tm, D_MODEL), _rows(tm, D_MODEL), _fixed((1, D_MODEL))),
        (SDS((t, D_MODEL), F32), SDS((t, 1), F32), SDS((t, D_MODEL), F32), SDS((D_MODEL, t), BF16),
         SDS((PLE_DIM, t), BF16), SDS((t, D_MODEL), BF16), SDS((t, D_MODEL), BF16), SDS((1, D_MODEL), F32)),
        (x, a, w_out, g, b, p, tgt, wpg, wpp), sem=("arbitrary",))


def _ffn_bwd(dxn, xh, rstd, h, w_in, w_out, g, tm, name, bg=None):
    t = dxn.shape[0]
    nch = D_FF // FFN_COLS

    def body(dxn_ref, xh_ref, rstd_ref, h_ref, win_ref, wout_ref, g_ref,
             dx_ref, dh_ref, df_ref, dg_ref, db_ref):
        @pl.when(pl.program_id(0) == 0)
        def _():
            dg_ref[...] = jnp.zeros_like(dg_ref)
            db_ref[...] = jnp.zeros_like(db_ref)

        dy = dxn_ref[...]
        xhv = xh_ref[...]
        dr = _ln_bwd(dy, xhv, rstd_ref[...], g_ref[...])
        dg_ref[...] += jnp.sum(dy * xhv, axis=0, keepdims=True)
        db_ref[...] += jnp.sum(dy, axis=0, keepdims=True)
        df = (0.5 * dr).astype(BF16)
        df_ref[...] = df
        dx = ALPHA * dr
        das = [_dot_nt(df, wout_ref[k * FFN_COLS:(k + 1) * FFN_COLS, :]) for k in range(nch)]
        for k in range(nch):
            cg = slice(k * FFN_COLS, (k + 1) * FFN_COLS)
            cu = slice(D_FF + k * FFN_COLS, D_FF + (k + 1) * FFN_COLS)
            hg = h_ref[:, cg].astype(F32)
            hu = h_ref[:, cu].astype(F32)
            sg = _sigmoid(hg)
            silu = hg * sg
            da = das[k]
            dhu = (da * silu).astype(BF16)
            dhg = (da * hu * (sg * (1.0 + hg * (1.0 - sg)))).astype(BF16)
            dh_ref[:, cg] = dhg
            dh_ref[:, cu] = dhu
            dx = dx + _dot_nt(dhg, win_ref[k]) + _dot_nt(dhu, win_ref[nch + k])
        dx_ref[...] = dx

    return _call(
        body, name, (t // tm,),
        [_rows(tm, D_MODEL), _rows(tm, D_MODEL), _rows(tm, 1), _rows(tm, 2 * D_FF),
         _resident((2 * nch, D_MODEL, FFN_COLS)), _resident((D_FF, D_MODEL)), _fixed((1, D_MODEL))],
        (_rows(tm, D_MODEL), _rows(tm, 2 * D_FF), _rows(tm, D_MODEL),
         _fixed((1, D_MODEL)), _fixed((1, D_MODEL))),
        (SDS((t, D_MODEL), F32), SDS((t, 2 * D_FF), BF16), SDS((t, D_MODEL), BF16),
         SDS((1, D_MODEL), F32), SDS((1, D_MODEL), F32)),
        (dxn, xh, rstd, h, w_in, w_out, g), sem=("arbitrary",), bg=bg)


def _tn_matmul(a, b, name, bm, bn, col_block=0, total_cols=None, prev=None, bg=None, a_cols=None, a_t=False):
    t, m = a.shape[::-1] if a_t else a.shape
    a_first = 0
    if a_cols is not None:
        a_first, m = a_cols[0], a_cols[1] * bm
    n = b.shape[1]
    total_cols = n if total_cols is None else total_cols
    whole = bm * bn <= TN_SMALL_BLOCK and (m // bm) * (n // bn) >= 2
    bk = min(2 * TN_K_BLOCK if whole else TN_K_BLOCK, t)
    nk = t // bk
    n_in = 2 if prev is None else 4

    def body(*refs):
        a_ref, b_ref = refs[0], refs[1]
        o_ref, ob_ref = refs[n_in], refs[n_in + 1]
        k = pl.program_id(2)

        @pl.when(k == 0)
        def _():
            o_ref[...] = jnp.zeros_like(o_ref)

        o_ref[...] += _dot(a_ref[...], b_ref[...]) if a_t else _dot_tn(a_ref[...], b_ref[...])

        @pl.when(k == nk - 1)
        def _():
            ob_ref[...] = o_ref[...].astype(BF16)

    a_spec = (pl.BlockSpec((bm, bk), lambda i, j, k: (i + a_first, k)) if a_t
              else pl.BlockSpec((bk, bm), lambda i, j, k: (k, i + a_first)))
    in_specs = [a_spec, pl.BlockSpec((bk, bn), lambda i, j, k: (k, j))]
    args = [a, b]
    aliases = {}
    if prev is not None:
        in_specs += [pl.BlockSpec(memory_space=pl.ANY), pl.BlockSpec(memory_space=pl.ANY)]
        args += list(prev)
        aliases = {2: 0, 3: 1}
        if any(bg is p for p in prev):
            bg = None
    out_spec = pl.BlockSpec((bm, bn), lambda i, j, k: (i, j + col_block))
    return _call(body, name, (m // bm, n // bn, nk), in_specs, (out_spec, out_spec),
                 (SDS((m, total_cols), F32), SDS((m, total_cols), BF16)), args,
                 sem=("parallel", "parallel", "arbitrary"), bg=bg, aliases=aliases)


def _mixin_fwd(x1, w, tm, bg=None):
    t = x1.shape[0]

    def body(x_ref, w_ref, xbt_ref, za_ref, zuv_ref, gab_ref):
        xb = x_ref[...].astype(BF16)
        xbt_ref[...] = xb.T
        za_ref[...] = _dot(xb, w_ref[:, 0:512]).astype(BF16)
        zuv_ref[...] = _dot(xb, w_ref[:, 512:1536]).astype(BF16)
        gab_ref[...] = _dot(xb, w_ref[:, 1536:3584]).astype(BF16)

    return _call(
        body, "mixin_fwd", (t // tm,),
        [_rows(tm, D_MODEL), _resident((D_MODEL, 3584))],
        (_cols(D_MODEL, tm), _rows(tm, 512), _rows(tm, 1024), _rows(tm, 2048)),
        (SDS((D_MODEL, t), BF16), SDS((t, 512), BF16), SDS((t, 1024), BF16), SDS((t, 2048), BF16)),
        (x1, w), sem=("parallel",), bg=bg)


def _mixin_bwd(dx1a, dza, dzuv, dgab, w, tm, bg=None):
    t = dx1a.shape[0]

    def body(d_ref, dza_ref, dzuv_ref, dgab_ref, w_ref, dx_ref):
        dx_ref[...] = (d_ref[...] + _dot_nt(dza_ref[...], w_ref[:, 0:512])
                       + _dot_nt(dzuv_ref[...], w_ref[:, 512:1536])
                       + _dot_nt(dgab_ref[...], w_ref[:, 1536:3584]))

    return _call(
        body, "mixin_bwd", (t // tm,),
        [_rows(tm, D_MODEL), _rows(tm, 512), _rows(tm, 1024), _rows(tm, 2048), _resident((D_MODEL, 3584))],
        (_rows(tm, D_MODEL),), (SDS((t, D_MODEL), F32),),
        (dx1a, dza, dzuv, dgab, w), sem=("parallel",), bg=bg)


def _unrolled(lo, hi, body, carry):
    for j in range(lo, hi):
        carry = body(j, carry)
    return carry


def _scan_fwd(hr_ref, hi_ref, a_ref, ap_ref, carry_ref, seg, cin_ref):
    for lc in range(SSM_LANES // SCAN_LANES):
        ls = slice(lc * SCAN_LANES, (lc + 1) * SCAN_LANES)
        a_r = jnp.broadcast_to(a_ref[0:1, ls], (8, SCAN_LANES))
        a_i = jnp.broadcast_to(a_ref[1:2, ls], (8, SCAN_LANES))

        def step(j, hc, ls=ls, a_r=a_r, a_i=a_i):
            h_r, h_i = hc
            rows = pl.ds(j * 8, 8)
            n_r = a_r * h_r - a_i * h_i + hr_ref[rows, ls]
            n_i = a_r * h_i + a_i * h_r + hi_ref[rows, ls]
            hr_ref[rows, ls] = n_r
            hi_ref[rows, ls] = n_i
            return n_r, n_i

        zero = jnp.zeros((8, SCAN_LANES), F32)
        f_r, f_i = _unrolled(0, seg, step, (zero, zero))
        c_r = carry_ref[0:1, ls]
        c_i = carry_ref[1:2, ls]
        p_r = ap_ref[0:1, ls]
        p_i = ap_ref[1:2, ls]
        rows_r, rows_i = [], []
        for s in range(8):
            rows_r.append(c_r)
            rows_i.append(c_i)
            c_r, c_i = (f_r[s:s + 1] + p_r * c_r - p_i * c_i,
                        f_i[s:s + 1] + p_r * c_i + p_i * c_r)
        carry_ref[0:1, ls] = c_r
        carry_ref[1:2, ls] = c_i
        cin_r = jnp.concatenate(rows_r, axis=0)
        cin_i = jnp.concatenate(rows_i, axis=0)
        if cin_ref is not None:
            cin_ref[0, :, ls] = cin_r
            cin_ref[1, :, ls] = cin_i

        def fix(j, cc, ls=ls, a_r=a_r, a_i=a_i):
            c_r, c_i = cc
            c_r, c_i = a_r * c_r - a_i * c_i, a_r * c_i + a_i * c_r
            rows = pl.ds(j * 8, 8)
            hr_ref[rows, ls] = hr_ref[rows, ls] + c_r
            hi_ref[rows, ls] = hi_ref[rows, ls] + c_i
            return c_r, c_i

        _unrolled(0, seg, fix, (cin_r, cin_i))


def _scan_bwd(gr_ref, gi_ref, hr_ref, hi_ref, cin_ref, a_ref, ap_ref, rcarry_ref, da_ref, seg):
    for lc in range(SSM_LANES // SCAN_LANES):
        ls = slice(lc * SCAN_LANES, (lc + 1) * SCAN_LANES)
        a_r = jnp.broadcast_to(a_ref[0:1, ls], (8, SCAN_LANES))
        a_i = jnp.broadcast_to(a_ref[1:2, ls], (8, SCAN_LANES))

        def step(t, gc, ls=ls, a_r=a_r, a_i=a_i):
            g_r, g_i = gc
            rows = pl.ds((seg - 1 - t) * 8, 8)
            n_r = gr_ref[rows, ls] + a_r * g_r + a_i * g_i
            n_i = gi_ref[rows, ls] + a_r * g_i - a_i * g_r
            gr_ref[rows, ls] = n_r
            gi_ref[rows, ls] = n_i
            return n_r, n_i

        zero = jnp.zeros((8, SCAN_LANES), F32)
        f_r, f_i = _unrolled(0, seg, step, (zero, zero))
        c_r = rcarry_ref[0:1, ls]
        c_i = rcarry_ref[1:2, ls]
        p_r = ap_ref[0:1, ls]
        p_i = ap_ref[1:2, ls]
        rows_r, rows_i = [None] * 8, [None] * 8
        for s in range(7, -1, -1):
            rows_r[s] = c_r
            rows_i[s] = c_i
            c_r, c_i = (f_r[s:s + 1] + p_r * c_r + p_i * c_i,
                        f_i[s:s + 1] + p_r * c_i - p_i * c_r)
        rcarry_ref[0:1, ls] = c_r
        rcarry_ref[1:2, ls] = c_i
        cin_r = jnp.concatenate(rows_r, axis=0)
        cin_i = jnp.concatenate(rows_i, axis=0)

        def fix_row(j_rows, hp_r, hp_i, cc, ls=ls, a_r=a_r, a_i=a_i):
            c_r, c_i, acc_r, acc_i = cc
            c_r, c_i = a_r * c_r + a_i * c_i, a_r * c_i - a_i * c_r
            g_r = gr_ref[j_rows, ls] + c_r
            g_i = gi_ref[j_rows, ls] + c_i
            gr_ref[j_rows, ls] = g_r
            gi_ref[j_rows, ls] = g_i
            acc_r = acc_r + g_r * hp_r + g_i * hp_i
            acc_i = acc_i + g_i * hp_r - g_r * hp_i
            return c_r, c_i, acc_r, acc_i

        def fix(t, cc, ls=ls, fix_row=fix_row):
            j = seg - 1 - t
            rows = pl.ds(j * 8, 8)
            prev = pl.ds((j - 1) * 8, 8)
            return fix_row(rows, hr_ref[prev, ls], hi_ref[prev, ls], cc)

        cc = _unrolled(0, seg - 1, fix, (cin_r, cin_i, zero, zero))
        _, _, acc_r, acc_i = fix_row(pl.ds(0, 8), cin_ref[0, :, ls], cin_ref[1, :, ls], cc)
        da_ref[0, :, ls] += acc_r
        da_ref[1, :, ls] += acc_i


def _s5_fwd(za, sp, bsz, seq, tb, bg=None):
    nb = seq // tb
    seg = tb // 8
    t = bsz * seq

    def body(za_ref, perm_ref, permt_ref, mre_ref, mim_ref, nre_ref, nim_ref, a_ref, ap_ref,
             dsk_ref, gw_ref, gb_ref, out_ref, outt_ref, y2_ref, car_ref, hr_ref, hi_ref, carry_ref):
        @pl.when(pl.program_id(1) == 0)
        def _():
            carry_ref[...] = jnp.zeros_like(carry_ref)

        car_ref[0] = carry_ref[...]
        up = _dot(perm_ref[...], za_ref[...])
        upb = up.astype(BF16)
        for bb in range(S5_BLOCKS):
            ub = upb[:, bb * S5_BLOCK_IN:(bb + 1) * S5_BLOCK_IN]
            st = slice(bb * S5_BLOCK_ST, (bb + 1) * S5_BLOCK_ST)
            hr_ref[:, st] = _dot(ub, mre_ref[bb])
            hi_ref[:, st] = _dot(ub, mim_ref[bb])
        _scan_fwd(hr_ref, hi_ref, a_ref, ap_ref, carry_ref, seg, None)
        ys = []
        for bb in range(S5_BLOCKS):
            st = slice(bb * S5_BLOCK_ST, (bb + 1) * S5_BLOCK_ST)
            ys.append(_dot(hr_ref[:, st].astype(BF16), nre_ref[bb])
                      - _dot(hi_ref[:, st].astype(BF16), nim_ref[bb]))
        y2 = jnp.concatenate(ys, axis=1) + dsk_ref[...] * up
        y2_ref[...] = y2
        y3 = _gelu(y2)
        gl = _dot(y3.astype(BF16), gw_ref[...]) + gb_ref[...]
        oa = y3 * _sigmoid(gl)
        out = _dot(permt_ref[...], oa.astype(BF16)).astype(BF16)
        out_ref[...] = out
        outt_ref[...] = out.T

    blk = pl.BlockSpec((tb, D_SSM), lambda b, j: (b * nb + j, 0))
    blk_t = pl.BlockSpec((D_SSM, tb), lambda b, j: (0, b * nb + j))
    m_shape = (S5_BLOCKS, S5_BLOCK_IN, S5_BLOCK_ST)
    n_shape = (S5_BLOCKS, S5_BLOCK_ST, S5_BLOCK_IN)
    return _call(
        body, "s5_fwd", (bsz, nb),
        [blk, _fixed((tb, tb)), _fixed((tb, tb)), _fixed(m_shape), _fixed(m_shape), _fixed(n_shape),
         _fixed(n_shape), _fixed((2, SSM_LANES)), _fixed((2, SSM_LANES)), _fixed((1, D_SSM)),
         _fixed((D_SSM, D_SSM)), _fixed((1, D_SSM))],
        (blk, blk_t, blk, pl.BlockSpec((1, 2, SSM_LANES), lambda b, j: (b * nb + j, 0, 0))),
        (SDS((t, D_SSM), BF16), SDS((D_SSM, t), BF16), SDS((t, D_SSM), F32), SDS((bsz * nb, 2, SSM_LANES), F32)),
        (za, sp["perm"], sp["permt"], sp["mre"], sp["mim"], sp["nre"], sp["nim"], sp["a"], sp["ap"],
         sp["dskip"], sp["glu_w"], sp["glu_b"]),
        scratch=[pltpu.VMEM((tb, SSM_LANES), F32), pltpu.VMEM((tb, SSM_LANES), F32),
                 pltpu.VMEM((2, SSM_LANES), F32)],
        sem=("arbitrary", "arbitrary"), bg=bg)


def _s5_bwd(za, y2p, doa, carries, sp, bsz, seq, tb, bg=None):
    nb = seq // tb
    seg = tb // 8
    t = bsz * seq

    def body(za_ref, y2_ref, doa_ref, car_ref, perm_ref, permt_ref, mre_ref, mim_ref, mtre_ref, mtim_ref,
             nre_ref, nim_ref, ntre_ref, ntim_ref, a_ref, ap_ref, dsk_ref, gw_ref, gwt_ref, gb_ref,
             dza_ref, dmr_ref, dmi_ref, dnr_ref, dni_ref, da_ref, ddsk_ref, dgw_ref, dgb_ref,
             hr_ref, hi_ref, gr_ref, gi_ref, cin_ref, carry_ref, rcarry_ref):
        first = jnp.logical_and(pl.program_id(0) == 0, pl.program_id(1) == 0)

        @pl.when(first)
        def _():
            for r in (dmr_ref, dmi_ref, dnr_ref, dni_ref, da_ref, ddsk_ref, dgw_ref, dgb_ref):
                r[...] = jnp.zeros_like(r)

        @pl.when(pl.program_id(1) == 0)
        def _():
            rcarry_ref[...] = jnp.zeros_like(rcarry_ref)

        carry_ref[...] = car_ref[0]
        perm = perm_ref[...]
        up = _dot(perm, za_ref[...])
        upb = up.astype(BF16)
        for bb in range(S5_BLOCKS):
            ub = upb[:, bb * S5_BLOCK_IN:(bb + 1) * S5_BLOCK_IN]
            st = slice(bb * S5_BLOCK_ST, (bb + 1) * S5_BLOCK_ST)
            hr_ref[:, st] = _dot(ub, mre_ref[bb])
            hi_ref[:, st] = _dot(ub, mim_ref[bb])
        _scan_fwd(hr_ref, hi_ref, a_ref, ap_ref, carry_ref, seg, cin_ref)

        y2 = y2_ref[...]
        y3 = _gelu(y2)
        y3b = y3.astype(BF16)
        sg = _sigmoid(_dot(y3b, gw_ref[...]) + gb_ref[...])
        d0 = doa_ref[...]
        d_hi = d0.astype(BF16)
        d1 = d0 - d_hi.astype(F32)
        d_mid = d1.astype(BF16)
        d_lo = (d1 - d_mid.astype(F32)).astype(BF16)
        doap = _dot(perm, d_hi) + _dot(perm, d_mid) + _dot(perm, d_lo)
        dgl = doap * y3 * sg * (1.0 - sg)
        dglb = dgl.astype(BF16)
        dy3 = doap * sg + _dot(dglb, gwt_ref[...])
        dgw_ref[...] += _dot_tn(y3b, dglb)
        dgb_ref[...] += jnp.sum(dgl, axis=0, keepdims=True)
        dy2 = dy3 * _gelu_grad(y2)
        ddsk_ref[...] += jnp.sum(dy2 * up, axis=0, keepdims=True)
        dyb = dy2.astype(BF16)
        for bb in range(S5_BLOCKS):
            dyc = dyb[:, bb * S5_BLOCK_IN:(bb + 1) * S5_BLOCK_IN]
            st = slice(bb * S5_BLOCK_ST, (bb + 1) * S5_BLOCK_ST)
            gr_ref[:, st] = _dot(dyc, ntre_ref[bb])
            gi_ref[:, st] = -_dot(dyc, ntim_ref[bb])
            dnr_ref[bb] += _dot_tn(hr_ref[:, st].astype(BF16), dyc)
            dni_ref[bb] += -_dot_tn(hi_ref[:, st].astype(BF16), dyc)
        _scan_bwd(gr_ref, gi_ref, hr_ref, hi_ref, cin_ref, a_ref, ap_ref, rcarry_ref, da_ref, seg)
        dus = []
        for bb in range(S5_BLOCKS):
            st = slice(bb * S5_BLOCK_ST, (bb + 1) * S5_BLOCK_ST)
            grb = gr_ref[:, st].astype(BF16)
            gib = gi_ref[:, st].astype(BF16)
            dus.append(_dot(grb, mtre_ref[bb]) + _dot(gib, mtim_ref[bb]))
            ub = upb[:, bb * S5_BLOCK_IN:(bb + 1) * S5_BLOCK_IN]
            dmr_ref[bb] += _dot_tn(ub, grb)
            dmi_ref[bb] += _dot_tn(ub, gib)
        du = jnp.concatenate(dus, axis=1) + dy2 * dsk_ref[...]
        dza_ref[...] = _dot(permt_ref[...], du.astype(BF16)).astype(BF16)

    def rev(b, j):
        return (b * nb + (nb - 1 - j), 0)

    blk = pl.BlockSpec((tb, D_SSM), rev)
    m_shape = (S5_BLOCKS, S5_BLOCK_IN, S5_BLOCK_ST)
    n_shape = (S5_BLOCKS, S5_BLOCK_ST, S5_BLOCK_IN)
    return _call(
        body, "s5_bwd", (bsz, nb),
        [blk, blk, blk, pl.BlockSpec((1, 2, SSM_LANES), lambda b, j: (b * nb + (nb - 1 - j), 0, 0)),
         _fixed((tb, tb)), _fixed((tb, tb)), _fixed(m_shape), _fixed(m_shape), _fixed(n_shape), _fixed(n_shape),
         _fixed(n_shape), _fixed(n_shape), _fixed(m_shape), _fixed(m_shape),
         _fixed((2, SSM_LANES)), _fixed((2, SSM_LANES)), _fixed((1, D_SSM)),
         _fixed((D_SSM, D_SSM)), _fixed((D_SSM, D_SSM)), _fixed((1, D_SSM))],
        (blk, _fixed(m_shape), _fixed(m_shape), _fixed(n_shape), _fixed(n_shape),
         _fixed((2, 8, SSM_LANES)), _fixed((1, D_SSM)), _fixed((D_SSM, D_SSM)), _fixed((1, D_SSM))),
        (SDS((t, D_SSM), BF16), SDS(m_shape, F32), SDS(m_shape, F32), SDS(n_shape, F32), SDS(n_shape, F32),
         SDS((2, 8, SSM_LANES), F32), SDS((1, D_SSM), F32), SDS((D_SSM, D_SSM), F32), SDS((1, D_SSM), F32)),
        (za, y2p, doa, carries, sp["perm"], sp["permt"], sp["mre"], sp["mim"], sp["mtre"], sp["mtim"],
         sp["nre"], sp["nim"], sp["ntre"], sp["ntim"], sp["a"], sp["ap"], sp["dskip"], sp["glu_w"],
         sp["glu_wt"], sp["glu_b"]),
        scratch=[pltpu.VMEM((tb, SSM_LANES), F32), pltpu.VMEM((tb, SSM_LANES), F32),
                 pltpu.VMEM((tb, SSM_LANES), F32), pltpu.VMEM((tb, SSM_LANES), F32),
                 pltpu.VMEM((2, 8, SSM_LANES), F32), pltpu.VMEM((2, SSM_LANES), F32),
                 pltpu.VMEM((2, SSM_LANES), F32)],
        sem=("arbitrary", "arbitrary"), bg=bg)


def _gmlp_spatial(ws_ref, vb):
    lane = lax.broadcasted_iota(jnp.int32, (CHUNK, 128), 1)
    parts = []
    for j in range(GMLP_HEADS // 2):
        vp = vb[:, 128 * j:128 * (j + 1)]
        parts.append(jnp.where(lane < GMLP_HEAD_DIM, _dot(ws_ref[2 * j], vp), _dot(ws_ref[2 * j + 1], vp)))
    return jnp.concatenate(parts, axis=1)


def _gmlp_fwd(zuv, ln_g, ln_b, wsm, bias, bg=None):
    t = zuv.shape[0]
    chunks = min(GMLP_CHUNKS, t // CHUNK)

    def body(z_ref, g_ref, b_ref, ws_ref, bias_ref, out_ref, outt_ref):
        for ch in range(chunks):
            rows = slice(ch * CHUNK, (ch + 1) * CHUNK)
            u = _gelu(z_ref[rows, 0:D_GMLP].astype(F32))
            v0 = _gelu(z_ref[rows, D_GMLP:2 * D_GMLP].astype(F32))
            v, _, _ = _ln_fwd(v0, g_ref[...], b_ref[...])
            s = _gmlp_spatial(ws_ref, v.astype(BF16)) + bias_ref[...]
            out = (u * s).astype(BF16)
            out_ref[rows, :] = out
            outt_ref[:, rows] = out.T

    step = chunks * CHUNK
    return _call(
        body, "gmlp_fwd", (t // step,),
        [_rows(step, 2 * D_GMLP), _fixed((1, D_GMLP)), _fixed((1, D_GMLP)),
         _fixed((GMLP_HEADS, CHUNK, CHUNK)), _fixed((CHUNK, D_GMLP))],
        (_rows(step, D_GMLP), _cols(D_GMLP, step)), (SDS((t, D_GMLP), BF16), SDS((D_GMLP, t), BF16)),
        (zuv, ln_g, ln_b, wsm, bias), sem=("parallel",), bg=bg)


def _gmlp_bwd(zuv, dgm, ln_g, ln_b, wsm, wsmt, bias, bg=None):
    t = zuv.shape[0]
    chunks = min(GMLP_CHUNKS, t // CHUNK)

    def body(z_ref, d_ref, g_ref, b_ref, ws_ref, wst_ref, bias_ref,
             dz_ref, dws_ref, dbias_ref, dg_ref, db_ref):
        @pl.when(pl.program_id(0) == 0)
        def _():
            for r in (dws_ref, dbias_ref, dg_ref, db_ref):
                r[...] = jnp.zeros_like(r)

        gam = g_ref[...]
        lane = lax.broadcasted_iota(jnp.int32, (CHUNK, 128), 1)
        tril = (lax.broadcasted_iota(jnp.int32, (CHUNK, CHUNK), 0)
                >= lax.broadcasted_iota(jnp.int32, (CHUNK, CHUNK), 1))
        zero_b = jnp.zeros((CHUNK, 128), BF16)
        for ch in range(chunks):
            rows = slice(ch * CHUNK, (ch + 1) * CHUNK)
            zu = z_ref[rows, 0:D_GMLP].astype(F32)
            zv = z_ref[rows, D_GMLP:2 * D_GMLP].astype(F32)
            u = _gelu(zu)
            v0 = _gelu(zv)
            v, vhat, rstd = _ln_fwd(v0, gam, b_ref[...])
            vb = v.astype(BF16)
            s = _gmlp_spatial(ws_ref, vb) + bias_ref[...]
            d = d_ref[rows, :]
            dz_ref[rows, 0:D_GMLP] = (d * s * _gelu_grad(zu)).astype(BF16)
            ds = d * u
            dbias_ref[...] += ds
            dsb = ds.astype(BF16)
            parts = []
            for j in range(GMLP_HEADS // 2):
                dsp = dsb[:, 128 * j:128 * (j + 1)]
                vp = vb[:, 128 * j:128 * (j + 1)]
                parts.append(jnp.where(lane < GMLP_HEAD_DIM, _dot(wst_ref[2 * j], dsp),
                                       _dot(wst_ref[2 * j + 1], dsp)))
                lo = jnp.where(lane < GMLP_HEAD_DIM, dsp, zero_b)
                hi = jnp.where(lane < GMLP_HEAD_DIM, zero_b, dsp)
                dws_ref[2 * j] += jnp.where(tril, _dot_nt(lo, vp), 0.0)
                dws_ref[2 * j + 1] += jnp.where(tril, _dot_nt(hi, vp), 0.0)
            dv = jnp.concatenate(parts, axis=1)
            dg_ref[...] += jnp.sum(dv * vhat, axis=0, keepdims=True)
            db_ref[...] += jnp.sum(dv, axis=0, keepdims=True)
            dz_ref[rows, D_GMLP:2 * D_GMLP] = (_ln_bwd(dv, vhat, rstd, gam) * _gelu_grad(zv)).astype(BF16)

    step = chunks * CHUNK
    return _call(
        body, "gmlp_bwd", (t // step,),
        [_rows(step, 2 * D_GMLP), _rows(step, D_GMLP), _fixed((1, D_GMLP)), _fixed((1, D_GMLP)),
         _fixed((GMLP_HEADS, CHUNK, CHUNK)), _fixed((GMLP_HEADS, CHUNK, CHUNK)), _fixed((CHUNK, D_GMLP))],
        (_rows(step, 2 * D_GMLP), _fixed((GMLP_HEADS, CHUNK, CHUNK)), _fixed((CHUNK, D_GMLP)),
         _fixed((1, D_GMLP)), _fixed((1, D_GMLP))),
        (SDS((t, 2 * D_GMLP), BF16), SDS((GMLP_HEADS, CHUNK, CHUNK), F32), SDS((CHUNK, D_GMLP), F32),
         SDS((1, D_GMLP), F32), SDS((1, D_GMLP), F32)),
        (zuv, dgm, ln_g, ln_b, wsm, wsmt, bias), sem=("arbitrary",), bg=bg)


def _mixout_fwd(x1, s5o, gm, gab, ua, ub, wmo, g, b, tm, bg=None):
    t = x1.shape[0]

    def body(x_ref, s_ref, m_ref, gab_ref, ua_ref, ub_ref, wmo_ref, g_ref, b_ref,
             xn_ref, xh_ref, rstd_ref):
        ya = _dot(s_ref[...], ua_ref[...])
        yb = _dot(m_ref[...], ub_ref[...])
        mix = (_sigmoid(gab_ref[:, 0:D_MODEL].astype(F32)) * ya
               + _sigmoid(gab_ref[:, D_MODEL:2 * D_MODEL].astype(F32)) * yb)
        r = ALPHA * x_ref[...] + _dot(mix.astype(BF16), wmo_ref[...])
        y, xh, rstd = _ln_fwd(r, g_ref[...], b_ref[...])
        xn_ref[...] = y
        xh_ref[...] = xh
        rstd_ref[...] = rstd

    return _call(
        body, "mixout_fwd", (t // tm,),
        [_rows(tm, D_MODEL), _rows(tm, D_SSM), _rows(tm, D_GMLP), _rows(tm, 2 * D_MODEL),
         _resident((D_SSM, D_MODEL)), _resident((D_GMLP, D_MODEL)), _resident((D_MODEL, D_MODEL)),
         _fixed((1, D_MODEL)), _fixed((1, D_MODEL))],
        (_rows(tm, D_MODEL), _rows(tm, D_MODEL), _rows(tm, 1)),
        (SDS((t, D_MODEL), F32), SDS((t, D_MODEL), F32), SDS((t, 1), F32)),
        (x1, s5o, gm, gab, ua, ub, wmo, g, b), sem=("parallel",), bg=bg)


def _mixout_bwd(dx2, xh, rstd, s5o, gm, gab, ua, ub, wmo, g, tm, bg=None):
    t = dx2.shape[0]

    def body(d_ref, xh_ref, rstd_ref, s_ref, m_ref, gab_ref, ua_ref, ub_ref, wmo_ref, g_ref,
             dx1_ref, dmx_ref, mb_ref, dya_ref, dyb_ref, ds5_ref, dgm_ref, dgab_ref, dg_ref, db_ref):
        @pl.when(pl.program_id(0) == 0)
        def _():
            dg_ref[...] = jnp.zeros_like(dg_ref)
            db_ref[...] = jnp.zeros_like(db_ref)

        dy = d_ref[...]
        xhv = xh_ref[...]
        dr = _ln_bwd(dy, xhv, rstd_ref[...], g_ref[...])
        dg_ref[...] += jnp.sum(dy * xhv, axis=0, keepdims=True)
        db_ref[...] += jnp.sum(dy, axis=0, keepdims=True)
        dx1_ref[...] = ALPHA * dr
        drb = dr.astype(BF16)
        dmx_ref[...] = drb
        dm = _dot_nt(drb, wmo_ref[...])
        ya = _dot(s_ref[...], ua_ref[...])
        yb = _dot(m_ref[...], ub_ref[...])
        sa = _sigmoid(gab_ref[:, 0:D_MODEL].astype(F32))
        sb = _sigmoid(gab_ref[:, D_MODEL:2 * D_MODEL].astype(F32))
        mb_ref[...] = (sa * ya + sb * yb).astype(BF16).T
        dya = (dm * sa).astype(BF16)
        dyb = (dm * sb).astype(BF16)
        dya_ref[...] = dya
        dyb_ref[...] = dyb
        dgab_ref[:, 0:D_MODEL] = (dm * ya * sa * (1.0 - sa)).astype(BF16)
        dgab_ref[:, D_MODEL:2 * D_MODEL] = (dm * yb * sb * (1.0 - sb)).astype(BF16)
        ds5_ref[...] = _dot_nt(dya, ua_ref[...])
        dgm_ref[...] = _dot_nt(dyb, ub_ref[...])

    return _call(
        body, "mixout_bwd", (t // tm,),
        [_rows(tm, D_MODEL), _rows(tm, D_MODEL), _rows(tm, 1), _rows(tm, D_SSM), _rows(tm, D_GMLP),
         _rows(tm, 2 * D_MODEL), _resident((D_SSM, D_MODEL)), _resident((D_GMLP, D_MODEL)),
         _resident((D_MODEL, D_MODEL)), _fixed((1, D_MODEL))],
        (_rows(tm, D_MODEL), _rows(tm, D_MODEL), _cols(D_MODEL, tm), _rows(tm, D_MODEL),
         _rows(tm, D_MODEL), _rows(tm, D_SSM), _rows(tm, D_GMLP), _rows(tm, 2 * D_MODEL),
         _fixed((1, D_MODEL)), _fixed((1, D_MODEL))),
        (SDS((t, D_MODEL), F32), SDS((t, D_MODEL), BF16), SDS((D_MODEL, t), BF16),
         SDS((t, D_MODEL), BF16), SDS((t, D_MODEL), BF16), SDS((t, D_SSM), F32),
         SDS((t, D_GMLP), F32), SDS((t, 2 * D_MODEL), BF16),
         SDS((1, D_MODEL), F32), SDS((1, D_MODEL), F32)),
        (dx2, xh, rstd, s5o, gm, gab, ua, ub, wmo, g), sem=("arbitrary",), bg=bg)


def _s5_discretise(lre, lim, log_dt, bre, bim):
    dt = jnp.exp(log_dt)[:, None]
    mag = jnp.exp(lre * dt)
    abr = mag * jnp.cos(lim * dt)
    abi = mag * jnp.sin(lim * dt)
    nr = abr - 1.0
    ni = abi
    den = lre * lre + lim * lim
    cr = ((nr * lre + ni * lim) / den)[..., None]
    ci = ((ni * lre - nr * lim) / den)[..., None]
    return abr, abi, cr * bre - ci * bim, cr * bim + ci * bre


def _block_diag_in(bb):
    v = bb.reshape(S5_BLOCKS, 8, SSM_STATE, SSM_GROUP_CH).transpose(0, 1, 3, 2)
    return jnp.einsum("bgip,gh->bgihp", v, jnp.eye(8, dtype=bb.dtype)).reshape(
        S5_BLOCKS, S5_BLOCK_IN, S5_BLOCK_ST)


def _block_diag_in_t(dm):
    v = dm.reshape(S5_BLOCKS, 8, SSM_GROUP_CH, 8, SSM_STATE)
    d = jnp.einsum("bgihp,gh->bgip", v, jnp.eye(8, dtype=dm.dtype))
    return d.transpose(0, 1, 3, 2).reshape(SSM_GROUPS, SSM_STATE, SSM_GROUP_CH)


def _block_diag_out(cc):
    v = cc.reshape(S5_BLOCKS, 8, SSM_GROUP_CH, SSM_STATE)
    return jnp.einsum("bgip,gh->bgphi", v, jnp.eye(8, dtype=cc.dtype)).reshape(
        S5_BLOCKS, S5_BLOCK_ST, S5_BLOCK_IN)


def _block_diag_out_t(dn):
    v = dn.reshape(S5_BLOCKS, 8, SSM_STATE, 8, SSM_GROUP_CH)
    d = jnp.einsum("bgphi,gh->bgip", v, jnp.eye(8, dtype=dn.dtype))
    return d.reshape(SSM_GROUPS, SSM_GROUP_CH, SSM_STATE)


def _s5_setup(lre, lim, log_dt, bre, bim, cre, cim, d_skip, glu_w, glu_b, tb):
    seg = tb // 8
    abr, abi, bbr, bbi = _s5_discretise(lre, lim, log_dt, bre, bim)
    pr, pi = abr, abi
    for _ in range(int(math.log2(seg))):
        pr, pi = pr * pr - pi * pi, 2.0 * pr * pi
    rows = jnp.arange(tb)
    src = (rows % 8) * seg + rows // 8
    perm = (src[:, None] == jnp.arange(tb)[None, :]).astype(BF16)
    mre = _block_diag_in(bbr)
    mim = _block_diag_in(bbi)
    nre = _block_diag_out(cre)
    nim = _block_diag_out(cim)
    return {
        "perm": perm, "permt": perm.T,
        "mre": mre.astype(BF16), "mim": mim.astype(BF16),
        "mtre": mre.transpose(0, 2, 1).astype(BF16), "mtim": mim.transpose(0, 2, 1).astype(BF16),
        "nre": nre.astype(BF16), "nim": nim.astype(BF16),
        "ntre": nre.transpose(0, 2, 1).astype(BF16), "ntim": nim.transpose(0, 2, 1).astype(BF16),
        "a": jnp.stack([abr.reshape(-1), abi.reshape(-1)]),
        "ap": jnp.stack([pr.reshape(-1), pi.reshape(-1)]),
        "dskip": d_skip.reshape(1, D_SSM), "glu_w": glu_w, "glu_wt": glu_w.T,
        "glu_b": glu_b.reshape(1, D_SSM),
    }


BIG = ("ffn1_w_in", "ffn1_w_out", "mix_w_in", "ssm_glu_w", "up_a", "up_b", "mix_w_out",
       "ffn2_w_in", "ffn2_w_out", "ple_w_proj", "ple_w_gate")
BIG_AXIS = {"ffn1_w_in": 1, "ffn1_w_out": 0, "mix_w_in": 1, "ssm_glu_w": 0, "up_a": 1, "up_b": 1,
            "mix_w_out": 0, "ffn2_w_in": 1, "ffn2_w_out": 0, "ple_w_proj": 1, "ple_w_gate": 0}
SHARD_MAJOR = 2
GATHER_AXIS = dict(BIG_AXIS, ffn1_w_in=SHARD_MAJOR, ffn2_w_in=SHARD_MAJOR)
GATHER_ORDER = (("ffn1_w_in",), ("ffn1_w_out",), ("mix_w_in",), ("ssm_glu_w", "up_a", "up_b", "mix_w_out"),
                ("ffn2_w_in",), ("ffn2_w_out", "ple_w_gate", "ple_w_proj"))
GATHER_FIRST_ID = 1
REDUCE_FIRST_ID = 7
SMALL = ("ln1_g", "ln1_b", "ssm_lambda_re", "ssm_lambda_im", "ssm_log_dt", "ssm_b_re", "ssm_b_im",
         "ssm_c_re", "ssm_c_im", "ssm_d", "ssm_glu_b", "gmlp_ln_g", "gmlp_ln_b", "gmlp_w_s",
         "gmlp_b_s", "ln2_g", "ln2_b", "ln3_g", "ln3_b")
SMALL_VIEW = {"ssm_b_re": (SSM_GROUPS, SSM_STATE * SSM_GROUP_CH), "ssm_b_im": (SSM_GROUPS, SSM_STATE * SSM_GROUP_CH)}


def _small_view(k, a):
    return a.reshape(SMALL_VIEW[k]) if k in SMALL_VIEW else a


def _place():
    return lax.axis_index("x"), lax.axis_index("y"), lax.axis_index("c")


def _other_chips(x, y):
    return [(1 - x, y), (x, 1 - y), (1 - x, 1 - y)]


def _window(ref, shard_shape, axis, chip, half):
    r, c = shard_shape
    hr = r // 2
    if axis == SHARD_MAJOR:
        return ref.at[chip] if half is None else ref.at[chip, pl.ds(half * hr, hr), :]
    if axis == 0:
        if half is None:
            return ref.at[pl.ds(chip * r, r), :]
        return ref.at[pl.ds(chip * r + half * hr, hr), :]
    if half is None:
        return ref.at[:, pl.ds(chip * c, c)]
    return ref.at[pl.ds(half * hr, hr), pl.ds(chip * c, c)]


def _gather_weights(shards, axes):
    n = len(shards)
    shapes = [s.shape for s in shards]
    full = [{0: (4 * r, c), 1: (r, 4 * c), SHARD_MAJOR: (4, r, c)}[ax] for (r, c), ax in zip(shapes, axes)]

    def remote(sems, i, k, src, dst, to):
        return pltpu.make_async_remote_copy(src_ref=src, dst_ref=dst, send_sem=sems[0].at[6 * i + k],
                                            recv_sem=sems[1].at[6 * i + k], device_id=to, device_id_type=MESH)

    def own_copies(ins, outs, sems):
        x, y, c = _place()
        me = 2 * x + y
        cps = []
        for i in range(n):
            hr = shapes[i][0] // 2
            mine = ins[i].at[pl.ds(c * hr, hr), :]
            for j, (cx, cy) in enumerate(_other_chips(x, y)):
                cps.append(remote(sems, i, j, mine, _window(outs[i], shapes[i], axes[i], me, c), (cx, cy, c)))
        local = [pltpu.make_async_copy(ins[i], _window(outs[i], shapes[i], axes[i], me, None), sems[2].at[i])
                 for i in range(n)]
        return cps, local

    def start(ins, outs, sems):
        cps, local = own_copies(ins, outs, sems)
        for cp in local + cps:
            cp.start()

    def finish(ins, outs, sems):
        x, y, c = _place()
        sibling = (x, y, 1 - c)
        passed = []
        for j, (cx, cy) in enumerate(_other_chips(x, y)):
            for i in range(n):
                w = _window(outs[i], shapes[i], axes[i], 2 * cx + cy, c)
                remote(sems, i, j, w, w, (cx, cy, c)).wait_recv()
                cp = remote(sems, i, 3 + j, w, w, sibling)
                cp.start()
                passed.append(cp)
        for j, (cx, cy) in enumerate(_other_chips(x, y)):
            for i in range(n):
                w = _window(outs[i], shapes[i], axes[i], 2 * cx + cy, 1 - c)
                remote(sems, i, 3 + j, w, w, sibling).wait_recv()
        cps, local = own_copies(ins, outs, sems)
        for cp in cps + passed:
            cp.wait_send()
        for cp in local:
            cp.wait()

    return _Exchange(shards, [SDS(f, BF16) for f in full],
                     [pltpu.SemaphoreType.DMA((6 * n,)), pltpu.SemaphoreType.DMA((6 * n,)),
                      pltpu.SemaphoreType.DMA((n,))], start, finish)


def _scatter_grads(parts, shapes, axes):
    n = len(parts)

    def copies(ins, outs, sems):
        x, y, c = _place()
        return [pltpu.make_async_remote_copy(
            src_ref=_window(ins[i], shapes[i], axes[i], 2 * cx + cy, None), dst_ref=outs[i].at[j],
            send_sem=sems[0].at[3 * i + j], recv_sem=sems[1].at[3 * i + j],
            device_id=(cx, cy, c), device_id_type=MESH)
            for i in range(n) for j, (cx, cy) in enumerate(_other_chips(x, y))]

    def start(ins, outs, sems):
        for cp in copies(ins, outs, sems):
            cp.start()

    def finish(ins, outs, sems):
        for cp in copies(ins, outs, sems):
            cp.wait()

    return _Exchange(parts, [SDS((3,) + tuple(s), BF16) for s in shapes],
                     [pltpu.SemaphoreType.DMA((3 * n,)), pltpu.SemaphoreType.DMA((3 * n,))], start, finish)


def _swap_halves(parts, shapes, axes):
    n = len(parts)

    def copies(ins, outs, sems):
        x, y, c = _place()
        cps = []
        for i in range(n):
            r, _ = shapes[i]
            hr = r // 2
            if axes[i] == 0:
                cps += [pltpu.make_async_remote_copy(
                    src_ref=ins[i].at[pl.ds(k * r + (1 - c) * hr, hr), :], dst_ref=outs[i].at[k],
                    send_sem=sems[0].at[i], recv_sem=sems[1].at[i], device_id=(x, y, 1 - c),
                    device_id_type=MESH) for k in range(4)]
            else:
                cps.append(pltpu.make_async_remote_copy(
                    src_ref=ins[i].at[pl.ds((1 - c) * hr, hr), :], dst_ref=outs[i],
                    send_sem=sems[0].at[i], recv_sem=sems[1].at[i], device_id=(x, y, 1 - c),
                    device_id_type=MESH))
        return cps

    def start(ins, outs, sems):
        for cp in copies(ins, outs, sems):
            cp.start()

    def finish(ins, outs, sems):
        x, y, c = _place()
        for i in range(n):
            pltpu.make_async_remote_copy(src_ref=outs[i], dst_ref=outs[i], send_sem=sems[0].at[i],
                                         recv_sem=sems[1].at[i], device_id=(x, y, 1 - c),
                                         device_id_type=MESH).wait()

    out = [SDS((4, r // 2, c), BF16) if ax == 0 else SDS((r // 2, 4 * c), BF16)
           for (r, c), ax in zip(shapes, axes)]
    return _Exchange(parts, out, [pltpu.SemaphoreType.DMA((n,)), pltpu.SemaphoreType.DMA((n,))], start, finish)


def _scatter_halves(pres, shapes):
    n = len(pres)

    def copies(ins, outs, sems):
        x, y, c = _place()
        return [pltpu.make_async_remote_copy(
            src_ref=ins[i].at[1 + j], dst_ref=outs[i].at[j], send_sem=sems[0].at[3 * i + j],
            recv_sem=sems[1].at[3 * i + j], device_id=(cx, cy, c), device_id_type=MESH)
            for i in range(n) for j, (cx, cy) in enumerate(_other_chips(x, y))]

    def start(ins, outs, sems):
        for cp in copies(ins, outs, sems):
            cp.start()

    def finish(ins, outs, sems):
        for cp in copies(ins, outs, sems):
            cp.wait()

    return _Exchange(pres, [SDS((3, r // 2, c), BF16) for r, c in shapes],
                     [pltpu.SemaphoreType.DMA((3 * n,)), pltpu.SemaphoreType.DMA((3 * n,))], start, finish)


def _swap_with_sibling(arrs):
    n = len(arrs)

    def copies(ins, outs, sems):
        x, y, c = _place()
        return [pltpu.make_async_remote_copy(src_ref=ins[i], dst_ref=outs[i], send_sem=sems[0].at[i],
                                             recv_sem=sems[1].at[i], device_id=(x, y, 1 - c),
                                             device_id_type=MESH) for i in range(n)]

    def start(ins, outs, sems):
        for cp in copies(ins, outs, sems):
            cp.start()

    def finish(ins, outs, sems):
        for cp in copies(ins, outs, sems):
            cp.wait()

    return _Exchange(arrs, [SDS(a.shape, a.dtype) for a in arrs],
                     [pltpu.SemaphoreType.DMA((n,)), pltpu.SemaphoreType.DMA((n,))], start, finish)


def _gather_small(arrs):
    n = len(arrs)

    def copy(sems, outs, i, k, block, to, src=None):
        px, py, pc = block
        dst = outs[i].at[4 * px + 2 * py + pc]
        return pltpu.make_async_remote_copy(
            src_ref=dst if src is None else src, dst_ref=dst, send_sem=sems[0].at[7 * i + k],
            recv_sem=sems[1].at[7 * i + k], device_id=to, device_id_type=MESH)

    def own_copies(ins, outs, sems):
        x, y, c = _place()
        cps = []
        for i in range(n):
            cps.append(copy(sems, outs, i, 0, (x, y, c), (x, y, 1 - c), src=ins[i]))
            for j, (cx, cy) in enumerate(_other_chips(x, y)):
                cps.append(copy(sems, outs, i, 1 + j, (x, y, c), (cx, cy, c), src=ins[i]))
        local = [pltpu.make_async_copy(ins[i], outs[i].at[4 * x + 2 * y + c], sems[2].at[i]) for i in range(n)]
        return cps, local

    def start(ins, outs, sems):
        cps, local = own_copies(ins, outs, sems)
        for cp in local + cps:
            cp.start()

    def finish(ins, outs, sems):
        x, y, c = _place()
        passed = []
        for j, (cx, cy) in enumerate(_other_chips(x, y)):
            for i in range(n):
                copy(sems, outs, i, 1 + j, (cx, cy, c), (x, y, c)).wait_recv()
                cp = copy(sems, outs, i, 4 + j, (cx, cy, c), (x, y, 1 - c))
                cp.start()
                passed.append(cp)
        for i in range(n):
            copy(sems, outs, i, 0, (x, y, 1 - c), (x, y, c)).wait_recv()
            for j, (cx, cy) in enumerate(_other_chips(x, y)):
                copy(sems, outs, i, 4 + j, (cx, cy, 1 - c), (x, y, c)).wait_recv()
        cps, local = own_copies(ins, outs, sems)
        for cp in cps + passed:
            cp.wait_send()
        for cp in local:
            cp.wait()

    return _Exchange(arrs, [SDS((N_DEV,) + a.shape, F32) for a in arrs],
                     [pltpu.SemaphoreType.DMA((7 * n,)), pltpu.SemaphoreType.DMA((7 * n,)),
                      pltpu.SemaphoreType.DMA((n,))], start, finish)


def _local_step(x, p, tgt, wb, ws, shards=None, opt=None):
    bsz, seq, _ = x.shape
    t = bsz * seq
    tm = min(256, t)
    tb = min(S5_TIME_BLOCK, seq)
    x0 = x.reshape(t, D_MODEL)
    p0 = p.reshape(t, PLE_DIM)
    tg = tgt.reshape(t, D_MODEL)
    row = lambda v: v.reshape(1, -1)
    dist = shards is not None
    wb = dict(wb)
    recv, sums, other, gathered = {}, {}, {}, {}
    gb = {}
    gs = {}
    shape_of, axis_of = {}, {}
    chip = None
    if dist:
        shape_of = {k: tuple(shards[k].shape) for k in BIG}
        axis_of = dict(BIG_AXIS)
        for q in range(LAST_PIECES):
            shape_of[LAST_PIECE % q] = (D_MODEL // LAST_PIECES, shape_of["ffn1_w_in"][1])
            axis_of[LAST_PIECE % q] = 1
        xi, yi, ci = _place()
        chip = (2 * xi + yi).astype(jnp.int32).reshape(1)
        ids = jnp.stack([2 * xi + yi] + [2 * cx + cy for cx, cy in _other_chips(xi, yi)] + [ci]).astype(jnp.int32)
    halfbuf, pre = {}, {}

    def gather(names):
        return _gather_weights([shards[k] for k in names], [GATHER_AXIS[k] for k in names]) if dist else None

    def exchange(scat=(), swap=(), halves=(), scat2=(), swap2=(), extra=None, after=None):
        if not dist:
            return None, []
        after = order[0] if after is None else after
        parts, tags = [], []
        if scat:
            parts.append(_scatter_grads([gb[k][1] for k in scat], [shape_of[k] for k in scat],
                                        [axis_of[k] for k in scat]))
            tags.append((recv, scat))
        if swap:
            for k in swap:
                sums[k] = order[0] = _sum_blocks(gb[k][0], recv[k], shape_of[k], axis_of[k], chip, "sum_" + k,
                                                 order[0])
            parts.append(_swap_with_sibling([sums[k] for k in swap]))
            tags.append((other, swap))
        if halves:
            parts.append(_swap_halves([gb[k][1] for k in halves], [shape_of[k] for k in halves],
                                      [axis_of[k] for k in halves]))
            tags.append((halfbuf, halves))
        if scat2:
            for k in scat2:
                pre[k] = _presum(gb[k][0], halfbuf[k], shape_of[k], axis_of[k], ids, "presum_" + k, order[0])
                order[0] = pre[k][0]
            parts.append(_scatter_halves([pre[k][1] for k in scat2], [shape_of[k] for k in scat2]))
            tags.append((recv, scat2))
        if swap2:
            for k in swap2:
                sums[k] = order[0] = _sum_half(pre[k][0], recv[k], "sum_" + k, order[0])
            parts.append(_swap_with_sibling([sums[k] for k in swap2]))
            tags.append((other, swap2))
        if extra is not None:
            parts.append(extra[0])
            tags.append((extra[1], extra[2]))
        return (_join(parts), tags) if parts else (None, [])

    def take(ex_tags, got):
        ex, tags = ex_tags
        if ex is not None:
            for (dst, names), (o0, o1) in zip(tags, ex.cuts):
                dst.update(zip(names, got[o0:o1]))

    order = [None]

    def ordered(builder, *args, **kw):
        res = builder(*args, bg=order[0] if dist else None, **kw)
        order[0] = res[0][0]
        return res

    launched = []

    def launch(ex_tags):
        if ex_tags[0] is not None:
            n = len(launched)
            launched.append(n)
            take(ex_tags, _run_exchange_on_sequencer(ex_tags[0], "reduce_%d" % n, REDUCE_FIRST_ID + n))

    small_shape = {k: _small_view(k, v).shape for k, v in ws.items()}
    small_shape["loss_rows"] = (1, D_MODEL)
    ws = {k: v if (v.ndim == 2 and k != "ssm_log_dt") else v[0] for k, v in ws.items()}
    tril = jnp.tril(jnp.ones((CHUNK, CHUNK), dtype=bool))
    wsm = jnp.where(tril[None], ws["gmlp_w_s"], 0.0)
    wsm_b = wsm.astype(BF16)
    wsmt_b = wsm.transpose(0, 2, 1).astype(BF16)
    bias = jnp.repeat(ws["gmlp_b_s"].T, GMLP_HEAD_DIM, axis=1)

    tf = min(512, t)
    if dist:
        for gi, names in enumerate(GATHER_ORDER):
            wb.update(zip(names, _run_exchange_on_sequencer(gather(names), "gather_%d" % gi, GATHER_FIRST_ID + gi)))
    (x0b, h1, a1), _ = _ffn_proj(x0, wb["ffn1_w_in"], tf, "ffn1_proj")
    (x1, xh1, rstd1), _ = _ffn_out(x0, a1, wb["ffn1_w_out"], row(ws["ln1_g"]), row(ws["ln1_b"]), min(2 * tf, t),
                                   "ffn1_out")
    sp = _s5_setup(ws["ssm_lambda_re"], ws["ssm_lambda_im"], ws["ssm_log_dt"], ws["ssm_b_re"],
                   ws["ssm_b_im"], ws["ssm_c_re"], ws["ssm_c_im"], ws["ssm_d"], wb["ssm_glu_w"],
                   ws["ssm_glu_b"], tb)
    (x1b, za, zuv, gab), _ = _mixin_fwd(x1, wb["mix_w_in"], tf)
    (s5o, s5ot, y2p, carries), _ = _s5_fwd(za, sp, bsz, seq, tb)
    (gm, gmt), _ = _gmlp_fwd(zuv, row(ws["gmlp_ln_g"]), row(ws["gmlp_ln_b"]), wsm_b, bias)
    (x2, xh2, rstd2), _ = _mixout_fwd(x1, s5o, gm, gab, wb["up_a"], wb["up_b"], wb["mix_w_out"],
                                           row(ws["ln2_g"]), row(ws["ln2_b"]), tf)
    (x2b, h2, a2), _ = _ffn_proj(x2, wb["ffn2_w_in"], tf, "ffn2_proj")
    (xh3, rstd3, dx3, x3b, pb, dq, de, loss_rows), _ = _ffn_out_loss(
        x2, a2, wb["ffn2_w_out"], row(ws["ln3_g"]), row(ws["ln3_b"]), p0, tg, wb["ple_w_gate"], wb["ple_w_proj"], tf)
    order[0] = dx3
    gb["ple_w_gate"], _ = ordered(_tn_matmul, x3b, dq, "dw_ple_gate", 1024, 1024, a_t=True)
    gb["ple_w_proj"], _ = ordered(_tn_matmul, pb, de, "dw_ple_proj", 256, 1024, a_t=True)
    launch(exchange(scat=("ple_w_gate", "ple_w_proj")))
    (dx2, dh2, df2, gs["ln3_g"], gs["ln3_b"]), _ = ordered(
        _ffn_bwd, dx3, xh3, rstd3, h2, wb["ffn2_w_in"], wb["ffn2_w_out"], row(ws["ln3_g"]), tm, "ffn2_bwd")
    gb["ffn2_w_out"], _ = ordered(_tn_matmul, a2, df2, "dw_ffn2_out", 1408, 1024)
    launch(exchange(scat=("ffn2_w_out",)))
    gb["ffn2_w_in"], _ = ordered(_tn_matmul, x2b, dh2, "dw_ffn2_in", 1024, 1408, a_t=True)
    launch(exchange(scat=("ffn2_w_in",), swap=("ple_w_gate", "ple_w_proj")))
    (dx1a, dmx, mb, dya, dyb, ds5, dgm, dgab, gs["ln2_g"], gs["ln2_b"]), _ = ordered(
        _mixout_bwd, dx2, xh2, rstd2, s5o, gm, gab, wb["up_a"], wb["up_b"], wb["mix_w_out"], row(ws["ln2_g"]), tf)
    gb["mix_w_out"], _ = ordered(_tn_matmul, mb, dmx, "dw_mix_out", 1024, 1024, a_t=True)
    gb["up_a"], _ = ordered(_tn_matmul, s5ot, dya, "dw_up_a", 512, 1024, a_t=True)
    gb["up_b"], _ = ordered(_tn_matmul, gmt, dyb, "dw_up_b", 512, 1024, a_t=True)
    launch(exchange(scat=("mix_w_out", "up_a", "up_b"), swap=("ffn2_w_out",)))
    (dza, dmr, dmi, dnr, dni, da, ddsk, dgw, dgb), _ = ordered(_s5_bwd, za, y2p, ds5, carries, sp, bsz, seq, tb)
    gb["ssm_glu_w"] = (dgw, dgw.astype(BF16))
    launch(exchange(scat=("ssm_glu_w",), swap=("ffn2_w_in",)))
    (dzuv, dws, dbias, gs["gmlp_ln_g"], gs["gmlp_ln_b"]), _ = ordered(
        _gmlp_bwd, zuv, dgm, row(ws["gmlp_ln_g"]), row(ws["gmlp_ln_b"]), wsm_b, wsmt_b, bias)
    (dx1,), _ = ordered(_mixin_bwd, dx1a, dza, dzuv, dgab, wb["mix_w_in"], tf)
    g_mi, _ = ordered(_tn_matmul, x1b, dza, "dw_mix_in_a", 1024, 512, 0, 3584, a_t=True)
    g_mi, _ = ordered(_tn_matmul, x1b, dzuv, "dw_mix_in_uv", 1024, 512, 1, 3584, g_mi, a_t=True)
    gb["mix_w_in"], _ = ordered(_tn_matmul, x1b, dgab, "dw_mix_in_g", 1024, 512, 3, 3584, g_mi, a_t=True)
    launch(exchange(swap=("mix_w_out", "up_a", "up_b", "ssm_glu_w")))

    d_abr = da[0].sum(axis=0).reshape(SSM_GROUPS, SSM_STATE)
    d_abi = da[1].sum(axis=0).reshape(SSM_GROUPS, SSM_STATE)
    _, vjp = jax.vjp(_s5_discretise, ws["ssm_lambda_re"], ws["ssm_lambda_im"], ws["ssm_log_dt"],
                     ws["ssm_b_re"], ws["ssm_b_im"])
    (gs["ssm_lambda_re"], gs["ssm_lambda_im"], gs["ssm_log_dt"], gs["ssm_b_re"], gs["ssm_b_im"]) = vjp(
        (d_abr, d_abi, _block_diag_in_t(dmr), _block_diag_in_t(dmi)))
    gs["ssm_c_re"] = _block_diag_out_t(dnr)
    gs["ssm_c_im"] = _block_diag_out_t(dni)
    gs["ssm_d"] = ddsk
    gs["ssm_glu_b"] = dgb
    gs["gmlp_w_s"] = dws
    gs["gmlp_b_s"] = dbias.reshape(CHUNK, GMLP_HEADS, GMLP_HEAD_DIM).sum(axis=-1).T
    gs["loss_rows"] = loss_rows

    def small_gather(names):
        return (_gather_small([gs[k].reshape(small_shape[k]) for k in names]), gathered, names) if dist else None

    late = ("ln1_g", "ln1_b")
    launch(exchange(scat=("mix_w_in",), extra=small_gather(tuple(k for k in SMALL + ("loss_rows",) if k not in late))))
    (dx0, dh1, df1, gs["ln1_g"], gs["ln1_b"]), _ = ordered(
        _ffn_bwd, dx1, xh1, rstd1, h1, wb["ffn1_w_in"], wb["ffn1_w_out"], row(ws["ln1_g"]), tm, "ffn1_bwd")
    grad_x = dx0.reshape(bsz, seq, D_MODEL)
    if not dist:
        gb["ffn1_w_out"], _ = _tn_matmul(a1, df1, "dw_ffn1_out", 1408, 1024)
        gb["ffn1_w_in"], _ = _tn_matmul(x0b, dh1, "dw_ffn1_in", 1024, 1408, a_t=True)
        return (loss_rows, grad_x, gb, {k: gs[k].reshape(small_shape[k]) for k in SMALL}, sums, other, gathered,
                None, {})
    launch(exchange(extra=small_gather(late)))
    gb["ffn1_w_out"], _ = ordered(_tn_matmul, a1, df1, "dw_ffn1_out", 1408, 1024)
    last = ["ffn1_w_out"] + [LAST_PIECE % q for q in range(LAST_PIECES)]
    fillers = (("ffn2_w_in", "mix_w_in", "ple_w_gate"),
               ("ffn2_w_out", "mix_w_out", "up_a", "up_b", "ssm_glu_w", "ple_w_proj"))
    out = {}
    for i in range(1, len(last) + 3):
        stage = lambda d: tuple(last[i - d:i - d + 1]) if 0 <= i - d < len(last) else ()
        launch(exchange(halves=stage(1), scat2=stage(2), swap2=stage(3), swap=("mix_w_in",) if i == 2 else ()))
        if i < len(last):
            gb[last[i]], _ = ordered(_tn_matmul, x0b, dh1, "dw_" + last[i], D_MODEL // LAST_PIECES, 1408,
                                     a_cols=(i - 1, 1), a_t=True)
        elif i - len(last) < len(fillers):
            for k in fillers[i - len(last)]:
                w, m, v = opt[k]
                out[k] = _adam_big(w, sums[k], other[k], m, v, "adam_" + k, after=order[0])
                order[0] = out[k][1]
    return loss_rows, grad_x, gb, gs, sums, other, gathered, ids, out


def _adamw(w, g, m, v):
    m = ADAM_B1 * m + (1.0 - ADAM_B1) * g
    v = ADAM_B2 * v + (1.0 - ADAM_B2) * (g * g)
    m_hat = m / ADAM_C1
    v_hat = v / ADAM_C2
    delta = -ADAM_LR * (m_hat / (jnp.sqrt(v_hat) + ADAM_EPS) + ADAM_WD * w)
    return delta, m, v


def _pinned(after):
    return ([pl.BlockSpec(memory_space=pl.ANY)], [after]) if after is not None else ([], [])


def _sum_blocks(part, recv, shape, axis, chip, name, after=None):
    r, c = shape
    rb = r // ROW_STEPS

    def body(chip_ref, p_ref, r_ref, *rest):
        rest[-1][...] = (p_ref[...] + r_ref[0].astype(F32) + r_ref[1].astype(F32) + r_ref[2].astype(F32))

    if axis == 0:
        own = pl.BlockSpec((rb, c), lambda i, k: (k[0] * ROW_STEPS + i, 0))
    else:
        own = pl.BlockSpec((rb, c), lambda i, k: (i, k[0]))
    pin_specs, pin_args = _pinned(after)
    grid_spec = pltpu.PrefetchScalarGridSpec(
        num_scalar_prefetch=1, grid=(ROW_STEPS,),
        in_specs=[own, pl.BlockSpec((3, rb, c), lambda i, k: (0, i, 0))] + pin_specs,
        out_specs=pl.BlockSpec((rb, c), lambda i, k: (i, 0)))
    return pl.pallas_call(body, name=name, out_shape=SDS((r, c), F32), grid_spec=grid_spec,
                          compiler_params=_params(("parallel",)))(chip, part, recv, *pin_args)


def _presum(part, half, shape, axis, ids, name, after=None):
    r, c = shape
    rb = r // 2

    def body(ids_ref, p_ref, h_ref, *rest):
        of_ref, ob_ref = rest[-2:]
        s = p_ref[...] + h_ref[...].astype(F32)
        ob_ref[...] = s.astype(BF16)

        @pl.when(pl.program_id(1) == 0)
        def _():
            of_ref[...] = s

    if axis == 0:
        p_spec = pl.BlockSpec((rb, c), lambda i, t, ids: (ids[t] * 2 + ids[4] + i, 0))
        h_spec = pl.BlockSpec((None, rb, c), lambda i, t, ids: (ids[t], i, 0))
    else:
        p_spec = pl.BlockSpec((rb, c), lambda i, t, ids: (ids[4] + i, ids[t]))
        h_spec = pl.BlockSpec((rb, c), lambda i, t, ids: (i, ids[t]))
    pin_specs, pin_args = _pinned(after)
    grid_spec = pltpu.PrefetchScalarGridSpec(
        num_scalar_prefetch=1, grid=(1, 4), in_specs=[p_spec, h_spec] + pin_specs,
        out_specs=(pl.BlockSpec((rb, c), lambda i, t, ids: (i, 0)),
                   pl.BlockSpec((None, rb, c), lambda i, t, ids: (t, i, 0))))
    return pl.pallas_call(body, name=name, out_shape=(SDS((r // 2, c), F32), SDS((4, r // 2, c), BF16)),
                          grid_spec=grid_spec,
                          compiler_params=_params(("parallel", "arbitrary")))(ids, part, half, *pin_args)


def _sum_half(pre, recv, name, after=None):
    hr, c = pre.shape
    rb = hr

    def body(p_ref, r_ref, *rest):
        rest[-1][...] = (p_ref[...] + r_ref[0].astype(F32) + r_ref[1].astype(F32) + r_ref[2].astype(F32))

    spec = pl.BlockSpec((rb, c), lambda i: (i, 0))
    pin_specs, pin_args = _pinned(after)
    return pl.pallas_call(body, name=name, grid=(1,), out_shape=SDS((hr, c), F32),
                          in_specs=[spec, pl.BlockSpec((3, rb, c), lambda i: (0, i, 0))] + pin_specs,
                          out_specs=spec, compiler_params=_params(("parallel",)))(pre, recv, *pin_args)


def _adam_halves(w, mine, oth, m, v, ids, name, piece=0, prev=None):
    r, c = w.shape
    rb = mine.shape[0] // 2

    def body(ids_ref, w_ref, a_ref, b_ref, m_ref, v_ref, *rest):
        g_ref, d_ref, nm_ref, nv_ref = rest[-4:]
        g = jnp.where(pl.program_id(0) // 2 == ids_ref[4], a_ref[...], b_ref[...])
        g_ref[...] = g
        d_ref[...], nm_ref[...], nv_ref[...] = _adamw(w_ref[...], g, m_ref[...], v_ref[...])

    whole = pl.BlockSpec((rb, c), lambda i, ids: (i + 4 * piece, 0))
    part = pl.BlockSpec((rb, c), lambda i, ids: (i % 2, 0))
    in_specs = [whole, part, part, whole, whole]
    args = [w, mine, oth, m, v]
    aliases = {}
    if prev is not None:
        in_specs += [pl.BlockSpec(memory_space=pl.ANY)] * 4
        args += list(prev)
        aliases = {6: 0, 7: 1, 8: 2, 9: 3}
    grid_spec = pltpu.PrefetchScalarGridSpec(num_scalar_prefetch=1, grid=(4,), in_specs=in_specs,
                                             out_specs=(whole,) * 4)
    return pl.pallas_call(body, name=name, out_shape=tuple(SDS((r, c), F32) for _ in range(4)),
                          grid_spec=grid_spec, input_output_aliases=aliases,
                          compiler_params=_params(("parallel",)))(ids, *args)


def _adam_big(w, ga, gb, m, v, name, piece=0, prev=None, after=None):
    r, c = w.shape
    pr = ga.shape[0]
    steps = ROW_STEPS if pr == r else 2
    rb = pr // steps
    off = piece * steps

    def body(w_ref, ga_ref, gb_ref, m_ref, v_ref, *rest):
        g_ref, d_ref, nm_ref, nv_ref = rest[-4:]
        g = ga_ref[...] + gb_ref[...]
        g_ref[...] = g
        d_ref[...], nm_ref[...], nv_ref[...] = _adamw(w_ref[...], g, m_ref[...], v_ref[...])

    whole = pl.BlockSpec((rb, c), lambda i: (i + off, 0))
    part = pl.BlockSpec((rb, c), lambda i: (i, 0))
    in_specs = [whole, part, part, whole, whole]
    args = [w, ga, gb, m, v]
    aliases = {}
    if prev is not None:
        in_specs += [pl.BlockSpec(memory_space=pl.ANY)] * 4
        args += list(prev)
        aliases = {5: 0, 6: 1, 7: 2, 8: 3}
    if after is not None:
        in_specs.append(pl.BlockSpec(memory_space=pl.ANY))
        args.append(after)
    return pl.pallas_call(
        body, name=name, grid=(steps,), out_shape=tuple(SDS((r, c), F32) for _ in range(4)),
        in_specs=in_specs, out_specs=(whole,) * 4, input_output_aliases=aliases,
        compiler_params=_params(("parallel",)),
    )(*args)


def _adam_small(ws, gathered, ms, vs):
    n = len(ws)

    def body(*refs):
        w_refs, g_refs, m_refs, v_refs = refs[:n], refs[n:2 * n], refs[2 * n:3 * n], refs[3 * n:4 * n]
        outs = refs[4 * n:]
        for i in range(n):
            g = g_refs[i][0]
            for d in range(1, N_DEV):
                g = g + g_refs[i][d]
            delta, nm, nv = _adamw(w_refs[i][...], g, m_refs[i][...], v_refs[i][...])
            outs[i][...] = g
            outs[n + i][...] = delta
            outs[2 * n + i][...] = nm
            outs[3 * n + i][...] = nv

    vmem = pl.BlockSpec(memory_space=pltpu.VMEM)
    shapes = [w.shape for w in ws]
    return pl.pallas_call(
        body, name="adam_small", out_shape=tuple(SDS(s, F32) for s in shapes * 4),
        in_specs=[vmem] * (4 * n), out_specs=tuple([vmem] * (4 * n)),
        compiler_params=pltpu.CompilerParams(vmem_limit_bytes=VMEM_LIMIT_BYTES),
    )(*ws, *gathered, *ms, *vs)


def _sum_loss(gathered):
    def body(g_ref, o_ref):
        tot = g_ref[0]
        for d in range(1, N_DEV):
            tot = tot + g_ref[d]
        o_ref[...] = (0.5 / D_MODEL) * jnp.sum(tot, axis=1, keepdims=True)

    vmem = pl.BlockSpec(memory_space=pltpu.VMEM)
    return pl.pallas_call(body, name="sum_loss", out_shape=SDS((1, 1), F32), in_specs=[vmem],
                          out_specs=vmem)(gathered)


def kernel(x, p, ffn1_w_in, ffn1_w_out, ln1_g, ln1_b, mix_w_in, ssm_lambda_re, ssm_lambda_im, ssm_log_dt, ssm_b_re, ssm_b_im, ssm_c_re, ssm_c_im, ssm_d, ssm_glu_w, ssm_glu_b, gmlp_ln_g, gmlp_ln_b, gmlp_w_s, gmlp_b_s, up_a, up_b, mix_w_out, ln2_g, ln2_b, ffn2_w_in, ffn2_w_out, ln3_g, ln3_b, ple_w_proj, ple_w_gate, loss_target, m_ffn1_w_in, m_ffn1_w_out, m_ln1_g, m_ln1_b, m_mix_w_in, m_ssm_lambda_re, m_ssm_lambda_im, m_ssm_log_dt, m_ssm_b_re, m_ssm_b_im, m_ssm_c_re, m_ssm_c_im, m_ssm_d, m_ssm_glu_w, m_ssm_glu_b, m_gmlp_ln_g, m_gmlp_ln_b, m_gmlp_w_s, m_gmlp_b_s, m_up_a, m_up_b, m_mix_w_out, m_ln2_g, m_ln2_b, m_ffn2_w_in, m_ffn2_w_out, m_ln3_g, m_ln3_b, m_ple_w_proj, m_ple_w_gate, v_ffn1_w_in, v_ffn1_w_out, v_ln1_g, v_ln1_b, v_mix_w_in, v_ssm_lambda_re, v_ssm_lambda_im, v_ssm_log_dt, v_ssm_b_re, v_ssm_b_im, v_ssm_c_re, v_ssm_c_im, v_ssm_d, v_ssm_glu_w, v_ssm_glu_b, v_gmlp_ln_g, v_gmlp_ln_b, v_gmlp_w_s, v_gmlp_b_s, v_up_a, v_up_b, v_mix_w_out, v_ln2_g, v_ln2_b, v_ffn2_w_in, v_ffn2_w_out, v_ln3_g, v_ln3_b, v_ple_w_proj, v_ple_w_gate):
    given = dict(locals())
    order = ("ffn1_w_in", "ffn1_w_out", "ln1_g", "ln1_b", "mix_w_in", "ssm_lambda_re", "ssm_lambda_im",
             "ssm_log_dt", "ssm_b_re", "ssm_b_im", "ssm_c_re", "ssm_c_im", "ssm_d", "ssm_glu_w", "ssm_glu_b",
             "gmlp_ln_g", "gmlp_ln_b", "gmlp_w_s", "gmlp_b_s", "up_a", "up_b", "mix_w_out", "ln2_g", "ln2_b",
             "ffn2_w_in", "ffn2_w_out", "ln3_g", "ln3_b", "ple_w_proj", "ple_w_gate")
    assert set(order) == set(BIG + SMALL)

    shard = {k: given[k][0] for k in BIG}
    shard_b = {k: shard[k].astype(BF16) for k in BIG}
    opt = {k: (shard[k], given["m_" + k][0], given["v_" + k][0]) for k in BIG}
    loss_rows, grad_x, gb, gs, sums, other, gathered, ids, out = _local_step(
        x, given["p"][0], loss_target, {}, {k: given[k] for k in SMALL}, shard_b, opt)

    out = dict(out)
    for k in BIG:
        if k in out:
            continue
        moments = (given["m_" + k][0], given["v_" + k][0])
        if k == "ffn1_w_out":
            out[k] = _adam_halves(shard[k], sums[k], other[k], *moments, ids, "adam_" + k)
        elif k == "ffn1_w_in":
            for q in range(LAST_PIECES):
                kq = LAST_PIECE % q
                out[k] = _adam_halves(shard[k], sums[kq], other[kq], *moments, ids, "adam_" + kq, q, out.get(k))
        else:
            out[k] = _adam_big(shard[k], sums[k], other[k], *moments, "adam_" + k,
                               after=gb[LAST_PIECE % (LAST_PIECES - 1)][0])

    res = _adam_small([_small_view(k, given[k]) for k in SMALL], [gathered[k] for k in SMALL],
                      [_small_view(k, given["m_" + k]) for k in SMALL],
                      [_small_view(k, given["v_" + k]) for k in SMALL])
    ns = len(SMALL)
    for i, k in enumerate(SMALL):
        out[k] = tuple(res[j * ns + i].reshape(given[k].shape) for j in range(4))
    loss = _sum_loss(gathered["loss_rows"]).reshape(())

    lead = lambda k, j: out[k][j][None] if k in BIG else out[k][j]
    return (loss, grad_x, *[lead(k, 0) for k in order], *[lead(k, 1) for k in order],
            *[lead(k, 2) for k in order], *[lead(k, 3) for k in order])
```
